```python
import jax
import jax.numpy as jnp
from jax import lax
import numpy as np

D_MODEL = 1024
BATCH = 8
SEQ = 8192
DEPTH = 2

EPS = 1e-6
LRU_WIDTH = D_MODEL // 2
LRU_BLOCKS = 8
LRU_BLOCK = LRU_WIDTH // LRU_BLOCKS
LRU_CONV = 4
LRU_C = 8.0
HG_HEADS = 4
HG_DK = 128
HG_DV = 128
HG_WIDTH = HG_HEADS * HG_DV
HG_CHUNK = 64
EVEN_SPLITS = (LRU_WIDTH, 2 * LRU_WIDTH, 2 * LRU_WIDTH + HG_HEADS * HG_DK, 2 * LRU_WIDTH + 2 * HG_HEADS * HG_DK, 2 * LRU_WIDTH + 2 * HG_HEADS * HG_DK + HG_WIDTH)
EVEN_IN = 2 * LRU_WIDTH + 2 * HG_HEADS * HG_DK + 2 * HG_WIDTH
EVEN_OUT = LRU_WIDTH + HG_WIDTH
SGU_WIDTH = D_MODEL
SGU_GROUPS = 8
SGU_GROUP = SGU_WIDTH // SGU_GROUPS
SGU_CHUNK = 128
D_FF = 2816
FFN_CONV = 3
N_EVEN = (DEPTH + 1) // 2
N_ODD = DEPTH // 2

kernel_name = "hybrid_rglru_hgrn2_gmlp_convffn"


def rms_norm(x, g):
    xf = x.astype(jnp.float32)
    xf = xf * lax.rsqrt(jnp.mean(xf * xf, axis=-1, keepdims=True) + EPS)
    return (xf * g.astype(jnp.float32)).astype(x.dtype)


def layer_norm(x, g, b):
    xf = x.astype(jnp.float32)
    xc = xf - jnp.mean(xf, axis=-1, keepdims=True)
    var = jnp.mean(xc * xc, axis=-1, keepdims=True)
    return (xc * lax.rsqrt(var + EPS) * g.astype(jnp.float32) + b.astype(jnp.float32)).astype(x.dtype)


def causal_dwconv(x, w, b):
    k_w = w.shape[0]
    t = x.shape[1]
    xp = jnp.pad(x, ((0, 0), (k_w - 1, 0), (0, 0)))
    out = b
    for k in range(k_w):
        out = out + xp[:, k:k + t] * w[k]
    return out


def rg_lru(x, w_a, b_a, w_x, b_x, lam):
    bsz, t, _ = x.shape
    xf = x.astype(jnp.float32)
    xb = xf.reshape(bsz, t, LRU_BLOCKS, LRU_BLOCK)
    gate_r = jax.nn.sigmoid(jnp.einsum("btni,nij->btnj", xb, w_a.astype(jnp.float32)).reshape(bsz, t, LRU_WIDTH) + b_a.astype(jnp.float32))
    gate_i = jax.nn.sigmoid(jnp.einsum("btni,nij->btnj", xb, w_x.astype(jnp.float32)).reshape(bsz, t, LRU_WIDTH) + b_x.astype(jnp.float32))
    log_a = -LRU_C * gate_r * jax.nn.softplus(-lam.astype(jnp.float32))
    a = jnp.exp(log_a)
    u = jnp.sqrt(-jnp.expm1(2.0 * log_a)) * (gate_i * xf)

    def combine(left, right):
        a_l, h_l = left
        a_r, h_r = right
        return a_l * a_r, a_r * h_l + h_r

    _, h = lax.associative_scan(combine, (a, u), axis=1)
    return h


def hgrn2(q, f_logit, v, g, lb, g_norm):
    bsz, t = q.shape[:2]
    n_c = t // HG_CHUNK
    lbh = lb.astype(jnp.float32).reshape(HG_HEADS, HG_DK)
    f = lbh + (1.0 - lbh) * jax.nn.sigmoid(f_logit.astype(jnp.float32))
    shp_k = (bsz, n_c, HG_CHUNK, HG_HEADS, HG_DK)
    shp_v = (bsz, n_c, HG_CHUNK, HG_HEADS, HG_DV)
    k = (1.0 - f).reshape(shp_k)
    qf = jax.nn.silu(q.astype(jnp.float32)).reshape(shp_k)
    vf = v.astype(jnp.float32).reshape(shp_v)
    b_cum = jnp.cumsum(jnp.log(f).reshape(shp_k), axis=2)
    b_mid = b_cum[:, :, HG_CHUNK // 2 - 1:HG_CHUNK // 2]
    b_last = b_cum[:, :, HG_CHUNK - 1:]
    scores = jnp.einsum("bnthd,bnshd->bnhts", qf * jnp.exp(b_cum - b_mid), k * jnp.exp(b_mid - b_cum))
    causal = jnp.tril(jnp.ones((HG_CHUNK, HG_CHUNK), dtype=bool))
    scores = jnp.where(causal, scores, 0.0)
    o_intra = jnp.einsum("bnhts,bnshv->bnthv", scores, vf)
    q_in = qf * jnp.exp(b_cum)
    kv = jnp.einsum("bnshd,bnshv->bnhdv", k * jnp.exp(b_last - b_cum), vf)
    decay = jnp.exp(b_last[:, :, 0])

    def step(state, xs):
        d_n, kv_n, q_n = xs
        o_n = jnp.einsum("bthd,bhdv->bthv", q_n, state)
        return d_n[..., None] * state + kv_n, o_n

    s0 = jnp.zeros((bsz, HG_HEADS, HG_DK, HG_DV), jnp.float32)
    _, o_inter = lax.scan(step, s0, (jnp.moveaxis(decay, 1, 0), jnp.moveaxis(kv, 1, 0), jnp.moveaxis(q_in, 1, 0)))
    o = (o_intra + jnp.moveaxis(o_inter, 0, 1)).reshape(bsz, t, HG_HEADS, HG_DV)
    o = rms_norm(o, g_norm) * jax.nn.silu(g.astype(jnp.float32))
    return o.reshape(bsz, t, HG_WIDTH)


def even_mixer(h, w_in, conv_w, conv_b, ga_w, ga_b, gx_w, gx_b, lam, lb, hg_norm, w_out):
    bsz, t, _ = h.shape
    z = h @ w_in
    y_gate, x_rec, q, f_logit, v, g = jnp.split(z, list(EVEN_SPLITS), axis=-1)
    x_rec = causal_dwconv(x_rec, conv_w, conv_b)
    out_a = jax.nn.gelu(y_gate.astype(jnp.float32)) * rg_lru(x_rec, ga_w, ga_b, gx_w, gx_b, lam)
    out_b = hgrn2(q.reshape(bsz, t, HG_HEADS, HG_DK), f_logit.reshape(bsz, t, HG_HEADS, HG_DK), v.reshape(bsz, t, HG_HEADS, HG_DV), g.reshape(bsz, t, HG_HEADS, HG_DV), lb, hg_norm)
    return jnp.concatenate([out_a, out_b], axis=-1).astype(h.dtype) @ w_out


def odd_mixer(h, w_in, b_in, ln_g, ln_b, w_s, b_s, w_out):
    bsz, t, _ = h.shape
    n_c = t // SGU_CHUNK
    z = jax.nn.gelu(h @ w_in + b_in)
    u, v = jnp.split(z, 2, axis=-1)
    v = layer_norm(v, ln_g, ln_b).reshape(bsz, n_c, SGU_CHUNK, SGU_GROUPS, SGU_GROUP)
    w_causal = jnp.where(jnp.tril(jnp.ones((SGU_CHUNK, SGU_CHUNK), dtype=bool)), w_s, 0.0)
    sv = jnp.einsum("gts,bnsgc->bntgc", w_causal, v) + b_s.T[:, :, None]
    return (u * sv.reshape(bsz, t, SGU_WIDTH)) @ w_out


def conv_ffn(h, w_up, conv_w, conv_b, w_down):
    gate, up = jnp.split(h @ w_up, 2, axis=-1)
    gate = causal_dwconv(gate, conv_w, conv_b)
    return (jax.nn.silu(gate) * up) @ w_down


def _fwd_setup_inputs(seed: int = 0) -> dict:
    key = jax.random.key(seed)
    ks = jax.random.split(key, 26)
    f32 = jnp.float32

    def nrm(k, shape, scale):
        return jax.random.normal(k, shape, f32) * scale

    def gain(k, shape):
        return 1.0 + 0.02 * jax.random.normal(k, shape, f32)

    a_c = jax.random.uniform(ks[11], (N_EVEN, LRU_WIDTH), f32, 0.9, 0.999)
    s = a_c ** (1.0 / LRU_C)
    lam = jnp.log(s) - jnp.log1p(-s)
    return {
        "x": nrm(ks[0], (BATCH, SEQ, D_MODEL), 1.0),
        "norm_mix": gain(ks[1], (DEPTH, D_MODEL)),
        "norm_ffn": gain(ks[2], (DEPTH, D_MODEL)),
        "norm_final": gain(ks[3], (D_MODEL,)),
        "ev_w_in": nrm(ks[4], (N_EVEN, D_MODEL, EVEN_IN), D_MODEL ** -0.5),
        "ev_conv_w": nrm(ks[5], (N_EVEN, LRU_CONV, LRU_WIDTH), LRU_CONV ** -0.5),
        "ev_conv_b": nrm(ks[6], (N_EVEN, LRU_WIDTH), 0.02),
        "ev_gate_a_w": nrm(ks[7], (N_EVEN, LRU_BLOCKS, LRU_BLOCK, LRU_BLOCK), LRU_BLOCK ** -0.5),
        "ev_gate_a_b": nrm(ks[8], (N_EVEN, LRU_WIDTH), 0.02),
        "ev_gate_x_w": nrm(ks[9], (N_EVEN, LRU_BLOCKS, LRU_BLOCK, LRU_BLOCK), LRU_BLOCK ** -0.5),
        "ev_gate_x_b": nrm(ks[10], (N_EVEN, LRU_WIDTH), 0.02),
        "ev_lru_lambda": lam,
        "hg_lb_logits": nrm(ks[12], (DEPTH + 1, HG_HEADS * HG_DK), 0.1),
        "ev_hg_norm": gain(ks[13], (N_EVEN, HG_DV)),
        "ev_w_out": nrm(ks[14], (N_EVEN, EVEN_OUT, D_MODEL), EVEN_OUT ** -0.5),
        "od_w_in": nrm(ks[15], (N_ODD, D_MODEL, 2 * SGU_WIDTH), D_MODEL ** -0.5),
        "od_b_in": nrm(ks[16], (N_ODD, 2 * SGU_WIDTH), 0.02),
        "od_ln_g": gain(ks[17], (N_ODD, SGU_WIDTH)),
        "od_ln_b": nrm(ks[18], (N_ODD, SGU_WIDTH), 0.02),
        "od_w_s": nrm(ks[19], (N_ODD, SGU_GROUPS, SGU_CHUNK, SGU_CHUNK), 0.5 * SGU_CHUNK ** -0.5),
        "od_b_s": gain(ks[20], (N_ODD, SGU_GROUPS, SGU_CHUNK)),
        "od_w_out": nrm(ks[21], (N_ODD, SGU_WIDTH, D_MODEL), SGU_WIDTH ** -0.5),
        "ffn_w_up": nrm(ks[22], (DEPTH, D_MODEL, 2 * D_FF), D_MODEL ** -0.5),
        "ffn_conv_w": nrm(ks[23], (DEPTH, FFN_CONV, D_FF), FFN_CONV ** -0.5),
        "ffn_conv_b": nrm(ks[24], (DEPTH, D_FF), 0.02),
        "ffn_w_down": nrm(ks[25], (DEPTH, D_FF, D_MODEL), D_FF ** -0.5),
    }


def _fwd_reference(x, norm_mix, norm_ffn, norm_final, ev_w_in, ev_conv_w, ev_conv_b, ev_gate_a_w, ev_gate_a_b, ev_gate_x_w, ev_gate_x_b, ev_lru_lambda, hg_lb_logits, ev_hg_norm, ev_w_out, od_w_in, od_b_in, od_ln_g, od_ln_b, od_w_s, od_b_s, od_w_out, ffn_w_up, ffn_conv_w, ffn_conv_b, ffn_w_down):
    lower_bounds = jnp.cumsum(jax.nn.softmax(hg_lb_logits.astype(jnp.float32), axis=0), axis=0)
    h = x
    for layer in range(DEPTH):
        hn = rms_norm(h, norm_mix[layer])
        if layer % 2 == 0:
            e = layer // 2
            mix = even_mixer(hn, ev_w_in[e], ev_conv_w[e], ev_conv_b[e], ev_gate_a_w[e], ev_gate_a_b[e], ev_gate_x_w[e], ev_gate_x_b[e], ev_lru_lambda[e], lower_bounds[layer], ev_hg_norm[e], ev_w_out[e])
        else:
            o = layer // 2
            mix = odd_mixer(hn, od_w_in[o], od_b_in[o], od_ln_g[o], od_ln_b[o], od_w_s[o], od_b_s[o], od_w_out[o])
        h = h + mix.astype(h.dtype)
        h = h + conv_ffn(rms_norm(h, norm_ffn[layer]), ffn_w_up[layer], ffn_conv_w[layer], ffn_conv_b[layer], ffn_w_down[layer]).astype(h.dtype)
    return rms_norm(h, norm_final)


import jax as _jax
import jax.numpy as _jnp

TWIN_FORMAT = 'train_step'
FWD_PARAMS = ['x', 'norm_mix', 'norm_ffn', 'norm_final', 'ev_w_in', 'ev_conv_w', 'ev_conv_b', 'ev_gate_a_w', 'ev_gate_a_b', 'ev_gate_x_w', 'ev_gate_x_b', 'ev_lru_lambda', 'hg_lb_logits', 'ev_hg_norm', 'ev_w_out', 'od_w_in', 'od_b_in', 'od_ln_g', 'od_ln_b', 'od_w_s', 'od_b_s', 'od_w_out', 'ffn_w_up', 'ffn_conv_w', 'ffn_conv_b', 'ffn_w_down']
TWIN_WEIGHTS = ['norm_mix', 'norm_ffn', 'norm_final', 'ev_w_in', 'ev_conv_w', 'ev_conv_b', 'ev_gate_a_w', 'ev_gate_a_b', 'ev_gate_x_w', 'ev_gate_x_b', 'ev_lru_lambda', 'hg_lb_logits', 'ev_hg_norm', 'ev_w_out', 'od_w_in', 'od_b_in', 'od_ln_g', 'od_ln_b', 'od_w_s', 'od_b_s', 'od_w_out', 'ffn_w_up', 'ffn_conv_w', 'ffn_conv_b', 'ffn_w_down']
TWIN_DIFF_INPUT = 'x'
TWIN_INPUTS = ['x', 'norm_mix', 'norm_ffn', 'norm_final', 'ev_w_in', 'ev_conv_w', 'ev_conv_b', 'ev_gate_a_w', 'ev_gate_a_b', 'ev_gate_x_w', 'ev_gate_x_b', 'ev_lru_lambda', 'hg_lb_logits', 'ev_hg_norm', 'ev_w_out', 'od_w_in', 'od_b_in', 'od_ln_g', 'od_ln_b', 'od_w_s', 'od_b_s', 'od_w_out', 'ffn_w_up', 'ffn_conv_w', 'ffn_conv_b', 'ffn_w_down', 'loss_target', 'm_norm_mix', 'm_norm_ffn', 'm_norm_final', 'm_ev_w_in', 'm_ev_conv_w', 'm_ev_conv_b', 'm_ev_gate_a_w', 'm_ev_gate_a_b', 'm_ev_gate_x_w', 'm_ev_gate_x_b', 'm_ev_lru_lambda', 'm_hg_lb_logits', 'm_ev_hg_norm', 'm_ev_w_out', 'm_od_w_in', 'm_od_b_in', 'm_od_ln_g', 'm_od_ln_b', 'm_od_w_s', 'm_od_b_s', 'm_od_w_out', 'm_ffn_w_up', 'm_ffn_conv_w', 'm_ffn_conv_b', 'm_ffn_w_down', 'v_norm_mix', 'v_norm_ffn', 'v_norm_final', 'v_ev_w_in', 'v_ev_conv_w', 'v_ev_conv_b', 'v_ev_gate_a_w', 'v_ev_gate_a_b', 'v_ev_gate_x_w', 'v_ev_gate_x_b', 'v_ev_lru_lambda', 'v_hg_lb_logits', 'v_ev_hg_norm', 'v_ev_w_out', 'v_od_w_in', 'v_od_b_in', 'v_od_ln_g', 'v_od_ln_b', 'v_od_w_s', 'v_od_b_s', 'v_od_w_out', 'v_ffn_w_up', 'v_ffn_conv_w', 'v_ffn_conv_b', 'v_ffn_w_down']
TWIN_OUTPUTS = ['loss', 'grad_x', 'grad_norm_mix', 'grad_norm_ffn', 'grad_norm_final', 'grad_ev_w_in', 'grad_ev_conv_w', 'grad_ev_conv_b', 'grad_ev_gate_a_w', 'grad_ev_gate_a_b', 'grad_ev_gate_x_w', 'grad_ev_gate_x_b', 'grad_ev_lru_lambda', 'grad_hg_lb_logits', 'grad_ev_hg_norm', 'grad_ev_w_out', 'grad_od_w_in', 'grad_od_b_in', 'grad_od_ln_g', 'grad_od_ln_b', 'grad_od_w_s', 'grad_od_b_s', 'grad_od_w_out', 'grad_ffn_w_up', 'grad_ffn_conv_w', 'grad_ffn_conv_b', 'grad_ffn_w_down', 'delta_norm_mix', 'delta_norm_ffn', 'delta_norm_final', 'delta_ev_w_in', 'delta_ev_conv_w', 'delta_ev_conv_b', 'delta_ev_gate_a_w', 'delta_ev_gate_a_b', 'delta_ev_gate_x_w', 'delta_ev_gate_x_b', 'delta_ev_lru_lambda', 'delta_hg_lb_logits', 'delta_ev_hg_norm', 'delta_ev_w_out', 'delta_od_w_in', 'delta_od_b_in', 'delta_od_ln_g', 'delta_od_ln_b', 'delta_od_w_s', 'delta_od_b_s', 'delta_od_w_out', 'delta_ffn_w_up', 'delta_ffn_conv_w', 'delta_ffn_conv_b', 'delta_ffn_w_down', 'new_m_norm_mix', 'new_m_norm_ffn', 'new_m_norm_final', 'new_m_ev_w_in', 'new_m_ev_conv_w', 'new_m_ev_conv_b', 'new_m_ev_gate_a_w', 'new_m_ev_gate_a_b', 'new_m_ev_gate_x_w', 'new_m_ev_gate_x_b', 'new_m_ev_lru_lambda', 'new_m_hg_lb_logits', 'new_m_ev_hg_norm', 'new_m_ev_w_out', 'new_m_od_w_in', 'new_m_od_b_in', 'new_m_od_ln_g', 'new_m_od_ln_b', 'new_m_od_w_s', 'new_m_od_b_s', 'new_m_od_w_out', 'new_m_ffn_w_up', 'new_m_ffn_conv_w', 'new_m_ffn_conv_b', 'new_m_ffn_w_down', 'new_v_norm_mix', 'new_v_norm_ffn', 'new_v_norm_final', 'new_v_ev_w_in', 'new_v_ev_conv_w', 'new_v_ev_conv_b', 'new_v_ev_gate_a_w', 'new_v_ev_gate_a_b', 'new_v_ev_gate_x_w', 'new_v_ev_gate_x_b', 'new_v_ev_lru_lambda', 'new_v_hg_lb_logits', 'new_v_ev_hg_norm', 'new_v_ev_w_out', 'new_v_od_w_in', 'new_v_od_b_in', 'new_v_od_ln_g', 'new_v_od_ln_b', 'new_v_od_w_s', 'new_v_od_b_s', 'new_v_od_w_out', 'new_v_ffn_w_up', 'new_v_ffn_conv_w', 'new_v_ffn_conv_b', 'new_v_ffn_w_down']
TWIN_LEAF_KINDS = {'loss': 'loss', 'grad_x': 'grad_x', 'grad_norm_mix': 'grad_w', 'grad_norm_ffn': 'grad_w', 'grad_norm_final': 'grad_w', 'grad_ev_w_in': 'grad_w', 'grad_ev_conv_w': 'grad_w', 'grad_ev_conv_b': 'grad_w', 'grad_ev_gate_a_w': 'grad_w', 'grad_ev_gate_a_b': 'grad_w', 'grad_ev_gate_x_w': 'grad_w', 'grad_ev_gate_x_b': 'grad_w', 'grad_ev_lru_lambda': 'grad_w', 'grad_hg_lb_logits': 'grad_w', 'grad_ev_hg_norm': 'grad_w', 'grad_ev_w_out': 'grad_w', 'grad_od_w_in': 'grad_w', 'grad_od_b_in': 'grad_w', 'grad_od_ln_g': 'grad_w', 'grad_od_ln_b': 'grad_w', 'grad_od_w_s': 'grad_w', 'grad_od_b_s': 'grad_w', 'grad_od_w_out': 'grad_w', 'grad_ffn_w_up': 'grad_w', 'grad_ffn_conv_w': 'grad_w', 'grad_ffn_conv_b': 'grad_w', 'grad_ffn_w_down': 'grad_w', 'delta_norm_mix': 'delta_w', 'delta_norm_ffn': 'delta_w', 'delta_norm_final': 'delta_w', 'delta_ev_w_in': 'delta_w', 'delta_ev_conv_w': 'delta_w', 'delta_ev_conv_b': 'delta_w', 'delta_ev_gate_a_w': 'delta_w', 'delta_ev_gate_a_b': 'delta_w', 'delta_ev_gate_x_w': 'delta_w', 'delta_ev_gate_x_b': 'delta_w', 'delta_ev_lru_lambda': 'delta_w', 'delta_hg_lb_logits': 'delta_w', 'delta_ev_hg_norm': 'delta_w', 'delta_ev_w_out': 'delta_w', 'delta_od_w_in': 'delta_w', 'delta_od_b_in': 'delta_w', 'delta_od_ln_g': 'delta_w', 'delta_od_ln_b': 'delta_w', 'delta_od_w_s': 'delta_w', 'delta_od_b_s': 'delta_w', 'delta_od_w_out': 'delta_w', 'delta_ffn_w_up': 'delta_w', 'delta_ffn_conv_w': 'delta_w', 'delta_ffn_conv_b': 'delta_w', 'delta_ffn_w_down': 'delta_w', 'new_m_norm_mix': 'new_m', 'new_m_norm_ffn': 'new_m', 'new_m_norm_final': 'new_m', 'new_m_ev_w_in': 'new_m', 'new_m_ev_conv_w': 'new_m', 'new_m_ev_conv_b': 'new_m', 'new_m_ev_gate_a_w': 'new_m', 'new_m_ev_gate_a_b': 'new_m', 'new_m_ev_gate_x_w': 'new_m', 'new_m_ev_gate_x_b': 'new_m', 'new_m_ev_lru_lambda': 'new_m', 'new_m_hg_lb_logits': 'new_m', 'new_m_ev_hg_norm': 'new_m', 'new_m_ev_w_out': 'new_m', 'new_m_od_w_in': 'new_m', 'new_m_od_b_in': 'new_m', 'new_m_od_ln_g': 'new_m', 'new_m_od_ln_b': 'new_m', 'new_m_od_w_s': 'new_m', 'new_m_od_b_s': 'new_m', 'new_m_od_w_out': 'new_m', 'new_m_ffn_w_up': 'new_m', 'new_m_ffn_conv_w': 'new_m', 'new_m_ffn_conv_b': 'new_m', 'new_m_ffn_w_down': 'new_m', 'new_v_norm_mix': 'new_v', 'new_v_norm_ffn': 'new_v', 'new_v_norm_final': 'new_v', 'new_v_ev_w_in': 'new_v', 'new_v_ev_conv_w': 'new_v', 'new_v_ev_conv_b': 'new_v', 'new_v_ev_gate_a_w': 'new_v', 'new_v_ev_gate_a_b': 'new_v', 'new_v_ev_gate_x_w': 'new_v', 'new_v_ev_gate_x_b': 'new_v', 'new_v_ev_lru_lambda': 'new_v', 'new_v_hg_lb_logits': 'new_v', 'new_v_ev_hg_norm': 'new_v', 'new_v_ev_w_out': 'new_v', 'new_v_od_w_in': 'new_v', 'new_v_od_b_in': 'new_v', 'new_v_od_ln_g': 'new_v', 'new_v_od_ln_b': 'new_v', 'new_v_od_w_s': 'new_v', 'new_v_od_b_s': 'new_v', 'new_v_od_w_out': 'new_v', 'new_v_ffn_w_up': 'new_v', 'new_v_ffn_conv_w': 'new_v', 'new_v_ffn_conv_b': 'new_v', 'new_v_ffn_w_down': 'new_v'}


def _forward(args):
    return _fwd_reference(*[args[k] for k in FWD_PARAMS])


def _output_shape():
    out = _jax.eval_shape(lambda: _forward(_fwd_setup_inputs(0)))
    return out.shape, out.dtype

N_MICROBATCH = 1
ADAM_LR = 0.001
ADAM_B1 = 0.9
ADAM_B2 = 0.999
ADAM_EPS = 1e-08
ADAM_WD = 0.01
ADAM_STEP = 10
PER_EXAMPLE_BATCH_AXIS = {'x': 0, 'loss_target': 0}
SHARED_INPUTS = []
_WEIGHT_DTYPES = {'norm_mix': _jnp.float32, 'norm_ffn': _jnp.float32, 'norm_final': _jnp.float32, 'ev_w_in': _jnp.float32, 'ev_conv_w': _jnp.float32, 'ev_conv_b': _jnp.float32, 'ev_gate_a_w': _jnp.float32, 'ev_gate_a_b': _jnp.float32, 'ev_gate_x_w': _jnp.float32, 'ev_gate_x_b': _jnp.float32, 'ev_lru_lambda': _jnp.float32, 'hg_lb_logits': _jnp.float32, 'ev_hg_norm': _jnp.float32, 'ev_w_out': _jnp.float32, 'od_w_in': _jnp.float32, 'od_b_in': _jnp.float32, 'od_ln_g': _jnp.float32, 'od_ln_b': _jnp.float32, 'od_w_s': _jnp.float32, 'od_b_s': _jnp.float32, 'od_w_out': _jnp.float32, 'ffn_w_up': _jnp.float32, 'ffn_conv_w': _jnp.float32, 'ffn_conv_b': _jnp.float32, 'ffn_w_down': _jnp.float32}
MOMENT_SCALE = {'norm_mix': 1.711600e-01, 'norm_ffn': 1.728655e-01, 'norm_final': 6.399418e+01, 'ev_w_in': 1.230720e-01, 'ev_conv_w': 1.542543e-01, 'ev_conv_b': 1.824391e+00, 'ev_gate_a_w': 5.865518e-02, 'ev_gate_a_b': 4.075470e-02, 'ev_gate_x_w': 1.040335e-01, 'ev_gate_x_b': 4.876702e-02, 'ev_lru_lambda': 6.522184e-02, 'hg_lb_logits': 7.809004e-03, 'ev_hg_norm': 3.476762e-01, 'ev_w_out': 1.538969e-01, 'od_w_in': 1.000271e-01, 'od_b_in': 1.114392e-01, 'od_ln_g': 4.466140e-02, 'od_ln_b': 4.349305e-02, 'od_w_s': 8.510747e-02, 'od_b_s': 1.254355e-01, 'od_w_out': 1.316642e-01, 'ffn_w_up': 7.125586e-02, 'ffn_conv_w': 7.197138e-02, 'ffn_conv_b': 7.219094e-02, 'ffn_w_down': 1.163974e-01}


def _to_microbatches(a, axis):
    t = _jnp.moveaxis(a, axis, 0)
    t = t.reshape((N_MICROBATCH, t.shape[0] // N_MICROBATCH) + t.shape[1:])
    return _jnp.moveaxis(t, 1, axis + 1)


def setup_inputs(seed: int = 0) -> dict:
    inp = _fwd_setup_inputs(seed)
    key = _jax.random.fold_in(_jax.random.key(seed), 7919)
    shape, _ = _output_shape()
    out = dict(inp)
    out["loss_target"] = _jax.random.normal(_jax.random.fold_in(key, 0), shape, _jnp.float32)
    for i, name in enumerate(TWIN_WEIGHTS):
        w = inp[name].astype(_jnp.float32)
        if MOMENT_SCALE is None:
            s = _jnp.sqrt(_jnp.mean(_jnp.square(w)) + 1e-30)
        else:
            s = MOMENT_SCALE[name]
        km, kv = _jax.random.split(_jax.random.fold_in(key, i + 1))
        out[name] = w
        out["m_" + name] = s * _jax.random.normal(km, w.shape, _jnp.float32)
        out["v_" + name] = (s * s) * _jax.random.uniform(kv, w.shape, _jnp.float32, 0.5, 1.5)
    if N_MICROBATCH > 1:
        for name, axis in PER_EXAMPLE_BATCH_AXIS.items():
            out[name] = _to_microbatches(out[name], axis)
    return {'x': out['x'], 'norm_mix': out['norm_mix'], 'norm_ffn': out['norm_ffn'], 'norm_final': out['norm_final'], 'ev_w_in': out['ev_w_in'], 'ev_conv_w': out['ev_conv_w'], 'ev_conv_b': out['ev_conv_b'], 'ev_gate_a_w': out['ev_gate_a_w'], 'ev_gate_a_b': out['ev_gate_a_b'], 'ev_gate_x_w': out['ev_gate_x_w'], 'ev_gate_x_b': out['ev_gate_x_b'], 'ev_lru_lambda': out['ev_lru_lambda'], 'hg_lb_logits': out['hg_lb_logits'], 'ev_hg_norm': out['ev_hg_norm'], 'ev_w_out': out['ev_w_out'], 'od_w_in': out['od_w_in'], 'od_b_in': out['od_b_in'], 'od_ln_g': out['od_ln_g'], 'od_ln_b': out['od_ln_b'], 'od_w_s': out['od_w_s'], 'od_b_s': out['od_b_s'], 'od_w_out': out['od_w_out'], 'ffn_w_up': out['ffn_w_up'], 'ffn_conv_w': out['ffn_conv_w'], 'ffn_conv_b': out['ffn_conv_b'], 'ffn_w_down': out['ffn_w_down'], 'loss_target': out['loss_target'], 'm_norm_mix': out['m_norm_mix'], 'm_norm_ffn': out['m_norm_ffn'], 'm_norm_final': out['m_norm_final'], 'm_ev_w_in': out['m_ev_w_in'], 'm_ev_conv_w': out['m_ev_conv_w'], 'm_ev_conv_b': out['m_ev_conv_b'], 'm_ev_gate_a_w': out['m_ev_gate_a_w'], 'm_ev_gate_a_b': out['m_ev_gate_a_b'], 'm_ev_gate_x_w': out['m_ev_gate_x_w'], 'm_ev_gate_x_b': out['m_ev_gate_x_b'], 'm_ev_lru_lambda': out['m_ev_lru_lambda'], 'm_hg_lb_logits': out['m_hg_lb_logits'], 'm_ev_hg_norm': out['m_ev_hg_norm'], 'm_ev_w_out': out['m_ev_w_out'], 'm_od_w_in': out['m_od_w_in'], 'm_od_b_in': out['m_od_b_in'], 'm_od_ln_g': out['m_od_ln_g'], 'm_od_ln_b': out['m_od_ln_b'], 'm_od_w_s': out['m_od_w_s'], 'm_od_b_s': out['m_od_b_s'], 'm_od_w_out': out['m_od_w_out'], 'm_ffn_w_up': out['m_ffn_w_up'], 'm_ffn_conv_w': out['m_ffn_conv_w'], 'm_ffn_conv_b': out['m_ffn_conv_b'], 'm_ffn_w_down': out['m_ffn_w_down'], 'v_norm_mix': out['v_norm_mix'], 'v_norm_ffn': out['v_norm_ffn'], 'v_norm_final': out['v_norm_final'], 'v_ev_w_in': out['v_ev_w_in'], 'v_ev_conv_w': out['v_ev_conv_w'], 'v_ev_conv_b': out['v_ev_conv_b'], 'v_ev_gate_a_w': out['v_ev_gate_a_w'], 'v_ev_gate_a_b': out['v_ev_gate_a_b'], 'v_ev_gate_x_w': out['v_ev_gate_x_w'], 'v_ev_gate_x_b': out['v_ev_gate_x_b'], 'v_ev_lru_lambda': out['v_ev_lru_lambda'], 'v_hg_lb_logits': out['v_hg_lb_logits'], 'v_ev_hg_norm': out['v_ev_hg_norm'], 'v_ev_w_out': out['v_ev_w_out'], 'v_od_w_in': out['v_od_w_in'], 'v_od_b_in': out['v_od_b_in'], 'v_od_ln_g': out['v_od_ln_g'], 'v_od_ln_b': out['v_od_ln_b'], 'v_od_w_s': out['v_od_w_s'], 'v_od_b_s': out['v_od_b_s'], 'v_od_w_out': out['v_od_w_out'], 'v_ffn_w_up': out['v_ffn_w_up'], 'v_ffn_conv_w': out['v_ffn_conv_w'], 'v_ffn_conv_b': out['v_ffn_conv_b'], 'v_ffn_w_down': out['v_ffn_w_down']}


def _loss(weights, diff, rest, loss_target):
    with _jax.named_scope("forward"):
        args = {**rest, TWIN_DIFF_INPUT: diff, **{k: w.astype(_WEIGHT_DTYPES[k]) for k, w in weights.items()}}
        y = _forward(args)
    with _jax.named_scope("loss_head"):
        err = _jnp.square(y.astype(_jnp.float32) - loss_target)
        return 0.5 * _jnp.sum(_jnp.mean(err, axis=-1)) if err.ndim else 0.5 * err


def _adamw(w, g, m, v):
    m = ADAM_B1 * m + (1.0 - ADAM_B1) * g
    v = ADAM_B2 * v + (1.0 - ADAM_B2) * _jnp.square(g)
    m_hat = m / (1.0 - ADAM_B1 ** ADAM_STEP)
    v_hat = v / (1.0 - ADAM_B2 ** ADAM_STEP)
    delta = -ADAM_LR * (m_hat / (_jnp.sqrt(v_hat) + ADAM_EPS) + ADAM_WD * w)
    return delta, m, v


def reference(x, norm_mix, norm_ffn, norm_final, ev_w_in, ev_conv_w, ev_conv_b, ev_gate_a_w, ev_gate_a_b, ev_gate_x_w, ev_gate_x_b, ev_lru_lambda, hg_lb_logits, ev_hg_norm, ev_w_out, od_w_in, od_b_in, od_ln_g, od_ln_b, od_w_s, od_b_s, od_w_out, ffn_w_up, ffn_conv_w, ffn_conv_b, ffn_w_down, loss_target, m_norm_mix, m_norm_ffn, m_norm_final, m_ev_w_in, m_ev_conv_w, m_ev_conv_b, m_ev_gate_a_w, m_ev_gate_a_b, m_ev_gate_x_w, m_ev_gate_x_b, m_ev_lru_lambda, m_hg_lb_logits, m_ev_hg_norm, m_ev_w_out, m_od_w_in, m_od_b_in, m_od_ln_g, m_od_ln_b, m_od_w_s, m_od_b_s, m_od_w_out, m_ffn_w_up, m_ffn_conv_w, m_ffn_conv_b, m_ffn_w_down, v_norm_mix, v_norm_ffn, v_norm_final, v_ev_w_in, v_ev_conv_w, v_ev_conv_b, v_ev_gate_a_w, v_ev_gate_a_b, v_ev_gate_x_w, v_ev_gate_x_b, v_ev_lru_lambda, v_hg_lb_logits, v_ev_hg_norm, v_ev_w_out, v_od_w_in, v_od_b_in, v_od_ln_g, v_od_ln_b, v_od_w_s, v_od_b_s, v_od_w_out, v_ffn_w_up, v_ffn_conv_w, v_ffn_conv_b, v_ffn_w_down):
    given = dict(x=x, norm_mix=norm_mix, norm_ffn=norm_ffn, norm_final=norm_final, ev_w_in=ev_w_in, ev_conv_w=ev_conv_w, ev_conv_b=ev_conv_b, ev_gate_a_w=ev_gate_a_w, ev_gate_a_b=ev_gate_a_b, ev_gate_x_w=ev_gate_x_w, ev_gate_x_b=ev_gate_x_b, ev_lru_lambda=ev_lru_lambda, hg_lb_logits=hg_lb_logits, ev_hg_norm=ev_hg_norm, ev_w_out=ev_w_out, od_w_in=od_w_in, od_b_in=od_b_in, od_ln_g=od_ln_g, od_ln_b=od_ln_b, od_w_s=od_w_s, od_b_s=od_b_s, od_w_out=od_w_out, ffn_w_up=ffn_w_up, ffn_conv_w=ffn_conv_w, ffn_conv_b=ffn_conv_b, ffn_w_down=ffn_w_down, loss_target=loss_target, m_norm_mix=m_norm_mix, m_norm_ffn=m_norm_ffn, m_norm_final=m_norm_final, m_ev_w_in=m_ev_w_in, m_ev_conv_w=m_ev_conv_w, m_ev_conv_b=m_ev_conv_b, m_ev_gate_a_w=m_ev_gate_a_w, m_ev_gate_a_b=m_ev_gate_a_b, m_ev_gate_x_w=m_ev_gate_x_w, m_ev_gate_x_b=m_ev_gate_x_b, m_ev_lru_lambda=m_ev_lru_lambda, m_hg_lb_logits=m_hg_lb_logits, m_ev_hg_norm=m_ev_hg_norm, m_ev_w_out=m_ev_w_out, m_od_w_in=m_od_w_in, m_od_b_in=m_od_b_in, m_od_ln_g=m_od_ln_g, m_od_ln_b=m_od_ln_b, m_od_w_s=m_od_w_s, m_od_b_s=m_od_b_s, m_od_w_out=m_od_w_out, m_ffn_w_up=m_ffn_w_up, m_ffn_conv_w=m_ffn_conv_w, m_ffn_conv_b=m_ffn_conv_b, m_ffn_w_down=m_ffn_w_down, v_norm_mix=v_norm_mix, v_norm_ffn=v_norm_ffn, v_norm_final=v_norm_final, v_ev_w_in=v_ev_w_in, v_ev_conv_w=v_ev_conv_w, v_ev_conv_b=v_ev_conv_b, v_ev_gate_a_w=v_ev_gate_a_w, v_ev_gate_a_b=v_ev_gate_a_b, v_ev_gate_x_w=v_ev_gate_x_w, v_ev_gate_x_b=v_ev_gate_x_b, v_ev_lru_lambda=v_ev_lru_lambda, v_hg_lb_logits=v_hg_lb_logits, v_ev_hg_norm=v_ev_hg_norm, v_ev_w_out=v_ev_w_out, v_od_w_in=v_od_w_in, v_od_b_in=v_od_b_in, v_od_ln_g=v_od_ln_g, v_od_ln_b=v_od_ln_b, v_od_w_s=v_od_w_s, v_od_b_s=v_od_b_s, v_od_w_out=v_od_w_out, v_ffn_w_up=v_ffn_w_up, v_ffn_conv_w=v_ffn_conv_w, v_ffn_conv_b=v_ffn_conv_b, v_ffn_w_down=v_ffn_w_down)
    weights = {n: given[n] for n in TWIN_WEIGHTS}
    shared = {n: given[n] for n in SHARED_INPUTS}
    per_example = {n: given[n] for n in ['x']}
    grad_fn = _jax.value_and_grad(_loss, argnums=(0, 1))

    def one_microbatch(ex, loss_target):
        ex = dict(ex)
        diff = ex.pop(TWIN_DIFF_INPUT)
        return grad_fn(weights, diff, {**shared, **ex}, loss_target)

    if N_MICROBATCH == 1:
        loss, (grad_w, grad_x) = one_microbatch(per_example, given["loss_target"])
    else:
        def body(carry, xs):
            loss_sum, grad_sum = carry
            l_k, (gw_k, gx_k) = one_microbatch(xs[0], xs[1])
            with _jax.named_scope("update"):
                return (loss_sum + l_k, _jax.tree.map(_jnp.add, grad_sum, gw_k)), gx_k

        init = (_jnp.zeros((), _jnp.float32), _jax.tree.map(_jnp.zeros_like, weights))
        (loss, grad_w), grad_x = _jax.lax.scan(body, init, (per_example, given["loss_target"]))
    with _jax.named_scope("update"):
        delta_w, new_m, new_v = {}, {}, {}
        for n in TWIN_WEIGHTS:
            delta_w[n], new_m[n], new_v[n] = _adamw(weights[n], grad_w[n], given["m_" + n], given["v_" + n])
    return (loss, grad_x, *[grad_w[n] for n in TWIN_WEIGHTS], *[delta_w[n] for n in TWIN_WEIGHTS],
            *[new_m[n] for n in TWIN_WEIGHTS], *[new_v[n] for n in TWIN_WEIGHTS])
```

```python
import functools

import jax
import jax.numpy as jnp
from jax import lax
from jax.experimental import pallas as pl
from jax.experimental.pallas import tpu as pltpu

F32 = jnp.float32
BF16 = jnp.bfloat16

EPS = 1e-6
D_MODEL = 1024
LRU_W = 512
LRU_BLOCKS = 8
LRU_C = 8.0
HG_HEADS = 4
HG_D = 128
HG_CHUNK = 64
SGU_G = 8
SGU_CHUNK = 128
D_FF = 2816
ADAM_LR, ADAM_B1, ADAM_B2, ADAM_EPS, ADAM_WD, ADAM_STEP = 0.001, 0.9, 0.999, 1e-08, 0.01, 10

V7X_VMEM_BYTES = 64 * 1024 * 1024
VMEM_LIMIT = V7X_VMEM_BYTES - 8 * 1024 * 1024
SUBLANES = 8
LANES = 128
BF16_ROWS = 16

GELU_C0 = 0.7978845608028654
GELU_C1 = 0.044715

NN = (((1,), (0,)), ((), ()))
NT = (((1,), (1,)), ((), ()))
TN = (((0,), (0,)), ((), ()))


def _dot(a, b, dims=NN):
    return lax.dot_general(a.astype(BF16), b.astype(BF16), dims, preferred_element_type=F32)


def _cp(n_grid):
    return pltpu.CompilerParams(dimension_semantics=("arbitrary",) * n_grid, vmem_limit_bytes=VMEM_LIMIT)


def _chunk(n, cap):
    best = LANES
    for c in range(LANES, cap + 1, LANES):
        if n % c == 0:
            best = c
    return best


def _resident(shape):
    nd = len(shape)
    return pl.BlockSpec(shape, lambda *_: (0,) * nd, pipeline_mode=pl.Buffered(1))


def _rsum8(x):
    r, c = x.shape
    return x.reshape(r // SUBLANES, SUBLANES, c).sum(axis=0)


def _sigmoid(x):
    return 1.0 / (1.0 + jnp.exp(-x))


def _gelu(x):
    return 0.5 * x * (1.0 + jnp.tanh(GELU_C0 * (x + GELU_C1 * x * x * x)))


def _gelu_grad(x):
    t = jnp.tanh(GELU_C0 * (x + GELU_C1 * x * x * x))
    return 0.5 * (1.0 + t) + 0.5 * x * (1.0 - t * t) * GELU_C0 * (1.0 + 3.0 * GELU_C1 * x * x)


def _silu_and_grad(x):
    s = _sigmoid(x)
    return x * s, s * (1.0 + x * (1.0 - s))


def _shift_rows(e, j):
    n = e.shape[0]
    return e if j % n == 0 else pltpu.roll(e, j % n, 0)


def _norm_mm(h, g, w, b, name, tt=512):
    T, D = h.shape
    N = w.shape[1]
    cn = _chunk(N, 512)

    def body(h_ref, g_ref, w_ref, b_ref, hn_ref, z_ref):
        x = h_ref[...]
        r = lax.rsqrt(jnp.mean(x * x, axis=-1, keepdims=True) + EPS)
        hn = (x * r * g_ref[...]).astype(BF16)
        hn_ref[...] = hn
        for j in range(0, N, cn):
            acc = jnp.dot(hn, w_ref[:, j:j + cn], preferred_element_type=F32) + b_ref[:, j:j + cn]
            z_ref[:, j:j + cn] = acc.astype(BF16)

    return pl.pallas_call(
        body, name=name, grid=(T // tt,),
        in_specs=[pl.BlockSpec((tt, D), lambda i: (i, 0)), _resident((1, D)), _resident((D, N)), _resident((1, N))],
        out_specs=[pl.BlockSpec((tt, D), lambda i: (i, 0)), pl.BlockSpec((tt, N), lambda i: (i, 0))],
        out_shape=[jax.ShapeDtypeStruct((T, D), BF16), jax.ShapeDtypeStruct((T, N), BF16)],
        compiler_params=_cp(1),
    )(h, g, w, b)


def _mm(a, w, res, out_dtype, name, tt=512):
    T, K = a.shape
    N = w.shape[1]
    cn = _chunk(N, 512)
    has_res = res is not None

    def body(*refs):
        a_ref, w_ref = refs[0], refs[1]
        res_ref = refs[2] if has_res else None
        o_ref = refs[-1]
        av = a_ref[...].astype(BF16)
        for j in range(0, N, cn):
            acc = jnp.dot(av, w_ref[:, j:j + cn], preferred_element_type=F32)
            if has_res:
                acc = acc + res_ref[:, j:j + cn]
            o_ref[:, j:j + cn] = acc.astype(out_dtype)

    in_specs = [pl.BlockSpec((tt, K), lambda i: (i, 0)), _resident((K, N))]
    args = [a, w]
    if has_res:
        in_specs.append(pl.BlockSpec((tt, N), lambda i: (i, 0)))
        args.append(res)
    return pl.pallas_call(
        body, name=name, grid=(T // tt,), in_specs=in_specs,
        out_specs=pl.BlockSpec((tt, N), lambda i: (i, 0)),
        out_shape=jax.ShapeDtypeStruct((T, N), out_dtype),
        compiler_params=_cp(1),
    )(*args)


def _mm_tn(a, b, name, tk, tt=512):
    T, K = a.shape
    N = b.shape[1]

    def body(a_ref, b_ref, o_ref):
        acc = lax.dot_general(a_ref[...].astype(BF16), b_ref[...].astype(BF16), TN, preferred_element_type=F32)

        @pl.when(pl.program_id(1) == 0)
        def _():
            o_ref[...] = acc

        @pl.when(pl.program_id(1) != 0)
        def _():
            o_ref[...] += acc

    return pl.pallas_call(
        body, name=name, grid=(K // tk, T // tt),
        in_specs=[pl.BlockSpec((tt, tk), lambda k, t: (t, k)), pl.BlockSpec((tt, N), lambda k, t: (t, 0))],
        out_specs=pl.BlockSpec((tk, N), lambda k, t: (k, 0)),
        out_shape=jax.ShapeDtypeStruct((K, N), F32),
        compiler_params=_cp(2),
    )(a, b)


def _mm_normbwd(dz, wt, x, g, dres, name, tt=512):
    T, N = dz.shape
    D = wt.shape[1]

    def body(dz_ref, wt_ref, x_ref, g_ref, dres_ref, dx_ref, dg_ref):
        @pl.when(pl.program_id(0) == 0)
        def _():
            dg_ref[...] = jnp.zeros_like(dg_ref)

        dy = jnp.dot(dz_ref[...], wt_ref[...], preferred_element_type=F32)
        x = x_ref[...]
        r = lax.rsqrt(jnp.mean(x * x, axis=-1, keepdims=True) + EPS)
        xn = x * r
        dg_ref[...] += _rsum8(dy * xn)
        dxn = dy * g_ref[...]
        dx_ref[...] = dres_ref[...] + r * (dxn - xn * jnp.mean(dxn * xn, axis=-1, keepdims=True))

    return pl.pallas_call(
        body, name=name, grid=(T // tt,),
        in_specs=[pl.BlockSpec((tt, N), lambda i: (i, 0)), _resident((N, D)), pl.BlockSpec((tt, D), lambda i: (i, 0)),
                  _resident((1, D)), pl.BlockSpec((tt, D), lambda i: (i, 0))],
        out_specs=[pl.BlockSpec((tt, D), lambda i: (i, 0)), pl.BlockSpec((SUBLANES, D), lambda i: (0, 0))],
        out_shape=[jax.ShapeDtypeStruct((T, D), F32), jax.ShapeDtypeStruct((SUBLANES, D), F32)],
        compiler_params=_cp(1),
    )(dz, wt, x, g, dres)


def _final_loss(h, g, tgt, name="final_loss", tt=512):
    T, D = h.shape

    def body(h_ref, g_ref, t_ref, dh_ref, dg_ref, sq_ref):
        @pl.when(pl.program_id(0) == 0)
        def _():
            dg_ref[...] = jnp.zeros_like(dg_ref)
            sq_ref[...] = jnp.zeros_like(sq_ref)

        x = h_ref[...]
        r = lax.rsqrt(jnp.mean(x * x, axis=-1, keepdims=True) + EPS)
        xn = x * r
        gv = g_ref[...]
        diff = xn * gv - t_ref[...]
        sq_ref[...] += _rsum8(diff * diff)
        dy = diff * (1.0 / D)
        dg_ref[...] += _rsum8(dy * xn)
        dxn = dy * gv
        dh_ref[...] = r * (dxn - xn * jnp.mean(dxn * xn, axis=-1, keepdims=True))

    return pl.pallas_call(
        body, name=name, grid=(T // tt,),
        in_specs=[pl.BlockSpec((tt, D), lambda i: (i, 0)), _resident((1, D)), pl.BlockSpec((tt, D), lambda i: (i, 0))],
        out_specs=[pl.BlockSpec((tt, D), lambda i: (i, 0)), pl.BlockSpec((SUBLANES, D), lambda i: (0, 0)),
                   pl.BlockSpec((SUBLANES, D), lambda i: (0, 0))],
        out_shape=[jax.ShapeDtypeStruct((T, D), F32), jax.ShapeDtypeStruct((SUBLANES, D), F32),
                   jax.ShapeDtypeStruct((SUBLANES, D), F32)],
        compiler_params=_cp(1),
    )(h, g, tgt)


def _ffn_act(gu, cw, cb, name, tt=512):
    T = gu.shape[0]
    F = gu.shape[1] // 2
    cc = _chunk(F, 256)
    hb = tt // BF16_ROWS

    def body(gu_ref, halo_ref, cw_ref, cb_ref, a_ref):
        first = pl.program_id(0) == 0
        for c0 in range(0, F, cc):
            cs = slice(c0, c0 + cc)
            x = gu_ref[:, cs].astype(F32)
            halo = jnp.where(first, 0.0, halo_ref[:, cs].astype(F32))
            e = jnp.concatenate([halo, x], axis=0)
            gc = (cb_ref[:, cs] + cw_ref[0:1, cs] * _shift_rows(e, 2)[BF16_ROWS:] + cw_ref[1:2, cs] * _shift_rows(e, 1)[BF16_ROWS:]
                  + cw_ref[2:3, cs] * x)
            up = gu_ref[:, F + c0:F + c0 + cc].astype(F32)
            a_ref[:, cs] = (gc * _sigmoid(gc) * up).astype(BF16)

    return pl.pallas_call(
        body, name=name, grid=(T // tt,),
        in_specs=[pl.BlockSpec((tt, 2 * F), lambda i: (i, 0)),
                  pl.BlockSpec((BF16_ROWS, F), lambda i: (jnp.maximum(i * hb - 1, 0), 0)),
                  _resident((SUBLANES, F)), _resident((1, F))],
        out_specs=pl.BlockSpec((tt, F), lambda i: (i, 0)),
        out_shape=jax.ShapeDtypeStruct((T, F), BF16),
        compiler_params=_cp(1),
    )(gu, gu, cw, cb)


def _ffn_act_bwd(gu, da, cw, cb, name, tt=512):
    T = gu.shape[0]
    F = gu.shape[1] // 2
    cc = _chunk(F, 256)
    hb = tt // BF16_ROWS
    last_hb = T // BF16_ROWS - 1
    nt = T // tt

    def body(gu_ref, gprev_ref, gunext_ref, da_ref, danext_ref, cw_ref, cb_ref, dgu_ref, dc_ref):
        i = pl.program_id(0)

        @pl.when(i == 0)
        def _():
            dc_ref[...] = jnp.zeros_like(dc_ref)

        n = tt + BF16_ROWS
        for c0 in range(0, F, cc):
            cs = slice(c0, c0 + cc)
            us = slice(F + c0, F + c0 + cc)
            g = gu_ref[:, cs].astype(F32)
            gp = jnp.where(i == 0, 0.0, gprev_ref[:, cs].astype(F32))
            ge = jnp.concatenate([gp, g, gunext_ref[:, cs].astype(F32)], axis=0)
            g1 = _shift_rows(ge, 1)[BF16_ROWS:]
            g2 = _shift_rows(ge, 2)[BF16_ROWS:]
            gc = cb_ref[:, cs] + cw_ref[0:1, cs] * g2 + cw_ref[1:2, cs] * g1 + cw_ref[2:3, cs] * ge[BF16_ROWS:]
            up = jnp.concatenate([gu_ref[:, us].astype(F32), gunext_ref[:, us].astype(F32)], axis=0)
            dan = jnp.where(i == nt - 1, 0.0, danext_ref[:, cs].astype(F32))
            dae = jnp.concatenate([da_ref[:, cs].astype(F32), dan], axis=0)
            s, ds = _silu_and_grad(gc)
            dgc = dae * up * ds
            dgu_ref[:, us] = (dae * s)[:tt].astype(BF16)
            dgate = cw_ref[2:3, cs] * dgc + cw_ref[1:2, cs] * _shift_rows(dgc, n - 1) + cw_ref[0:1, cs] * _shift_rows(dgc, n - 2)
            dgu_ref[:, cs] = dgate[:tt].astype(BF16)
            dm = dgc[:tt]
            dc_ref[0, :, cs] += _rsum8(dm * g2[:tt])
            dc_ref[1, :, cs] += _rsum8(dm * g1[:tt])
            dc_ref[2, :, cs] += _rsum8(dm * g)
            dc_ref[3, :, cs] += _rsum8(dm)

    return pl.pallas_call(
        body, name=name, grid=(nt,),
        in_specs=[pl.BlockSpec((tt, 2 * F), lambda i: (i, 0)),
                  pl.BlockSpec((BF16_ROWS, F), lambda i: (jnp.maximum(i * hb - 1, 0), 0)),
                  pl.BlockSpec((BF16_ROWS, 2 * F), lambda i: (jnp.minimum((i + 1) * hb, last_hb), 0)),
                  pl.BlockSpec((tt, F), lambda i: (i, 0)),
                  pl.BlockSpec((BF16_ROWS, F), lambda i: (jnp.minimum((i + 1) * hb, last_hb), 0)),
                  _resident((SUBLANES, F)), _resident((1, F))],
        out_specs=[pl.BlockSpec((tt, 2 * F), lambda i: (i, 0)), pl.BlockSpec((4, SUBLANES, F), lambda i: (0, 0, 0))],
        out_shape=[jax.ShapeDtypeStruct((T, 2 * F), BF16), jax.ShapeDtypeStruct((4, SUBLANES, F), F32)],
        compiler_params=_cp(1),
    )(gu, gu, gu, da, da, cw, cb)


def _softplus_neg(lam):
    x = -lam
    y = jnp.exp(-jnp.abs(x))
    l1p = jnp.where(y < 0.01, y * (1.0 - y * (0.5 - y * (1.0 / 3.0))), jnp.log(1.0 + y))
    return jnp.maximum(x, 0.0) + l1p


def _lru_gates(xc, wa_ref, ba_ref, wx_ref, bx_ref, sp):
    xcb = xc.astype(BF16)
    r = _sigmoid(jnp.dot(xcb, wa_ref[...], preferred_element_type=F32) + ba_ref[...])
    gi = _sigmoid(jnp.dot(xcb, wx_ref[...], preferred_element_type=F32) + bx_ref[...])
    log_a = -LRU_C * r * sp
    a = jnp.exp(log_a)
    x2 = 2.0 * log_a
    series = -x2 * (1.0 + x2 * 0.5 * (1.0 + x2 * (1.0 / 3.0) * (1.0 + x2 * 0.25 * (1.0 + x2 * 0.2))))
    om = jnp.where(x2 > -0.125, series, 1.0 - a * a)
    return r, gi, a, jnp.sqrt(om)


def _lru_conv(xr, halo, cw_ref, cb_ref):
    e = jnp.concatenate([halo, xr], axis=0)
    x1 = _shift_rows(e, 1)[BF16_ROWS:]
    x2 = _shift_rows(e, 2)[BF16_ROWS:]
    x3 = _shift_rows(e, 3)[BF16_ROWS:]
    xc = cb_ref[...] + cw_ref[0:1, :] * x3 + cw_ref[1:2, :] * x2 + cw_ref[2:3, :] * x1 + cw_ref[3:4, :] * xr
    return xc, x1, x2, x3


def _lru_fwd(z, cw, cb, wa, ba, wx, bx, lam, name="lru_fwd", tt=256):
    T = z.shape[0]
    W = LRU_W
    hb = tt // BF16_ROWS
    ng = tt // SUBLANES

    def body(z_ref, halo_ref, cw_ref, cb_ref, wa_ref, ba_ref, wx_ref, bx_ref, lam_ref, oa_ref, h_ref, a_s, u_s, hc):
        i = pl.program_id(0)

        @pl.when(i == 0)
        def _():
            hc[...] = jnp.zeros_like(hc)

        xr = z_ref[:, W:2 * W].astype(F32)
        halo = jnp.where(i == 0, 0.0, halo_ref[...].astype(F32))
        xc, _, _, _ = _lru_conv(xr, halo, cw_ref, cb_ref)
        sp = _softplus_neg(lam_ref[...])
        r, gi, a, mult = _lru_gates(xc, wa_ref, ba_ref, wx_ref, bx_ref, sp)
        a_s[...] = a
        u_s[...] = mult * gi * xc
        row = lax.broadcasted_iota(jnp.int32, (SUBLANES, W), 0)

        def step(j, hprev):
            r0 = pl.multiple_of(j * SUBLANES, SUBLANES)
            A = a_s[pl.ds(r0, SUBLANES), :]
            U = u_s[pl.ds(r0, SUBLANES), :]
            for k in (1, 2, 4):
                m = row >= k
                U = jnp.where(m, A * pltpu.roll(U, k, 0) + U, U)
                A = jnp.where(m, A * pltpu.roll(A, k, 0), A)
            H = U + A * hprev
            h_ref[pl.ds(r0, SUBLANES), :] = H
            return jnp.broadcast_to(H[SUBLANES - 1:SUBLANES, :], (SUBLANES, W))

        hc[...] = lax.fori_loop(0, ng, step, hc[...])
        oa_ref[...] = (_gelu(z_ref[:, 0:W].astype(F32)) * h_ref[...]).astype(BF16)

    return pl.pallas_call(
        body, name=name, grid=(T // tt,),
        in_specs=[pl.BlockSpec((tt, 2 * W), lambda i: (i, 0)),
                  pl.BlockSpec((BF16_ROWS, W), lambda i: (jnp.maximum(i * hb - 1, 0), 1)),
                  _resident((SUBLANES, W)), _resident((1, W)), _resident((W, W)), _resident((1, W)),
                  _resident((W, W)), _resident((1, W)), _resident((1, W))],
        out_specs=[pl.BlockSpec((tt, W), lambda i: (i, 0)), pl.BlockSpec((tt, W), lambda i: (i, 0))],
        out_shape=[jax.ShapeDtypeStruct((T, W), BF16), jax.ShapeDtypeStruct((T, W), F32)],
        scratch_shapes=[pltpu.VMEM((tt, W), F32), pltpu.VMEM((tt, W), F32), pltpu.VMEM((SUBLANES, W), F32)],
        compiler_params=_cp(1),
    )(z, z, cw, cb, wa, ba, wx, bx, lam)


def _lru_bwd(z, hseq, dmix, cw, cb, wa, wat, ba, wx, wxt, bx, lam, name="lru_bwd", tt=256):
    T = z.shape[0]
    W = LRU_W
    nt = T // tt
    hb = tt // BF16_ROWS
    sb = tt // SUBLANES
    ng = tt // SUBLANES

    def body(z_ref, halo_ref, h_ref, hprev_ref, dm_ref, cw_ref, cb_ref, wa_ref, wat_ref, ba_ref, wx_ref, wxt_ref, bx_ref,
             lam_ref, dz_ref, dc_ref, dwa_ref, dwx_ref, dv_ref, c_s, d_s, g_s, gc, an, dxn):
        i = pl.program_id(0)
        ti = nt - 1 - i

        @pl.when(i == 0)
        def _():
            dc_ref[...] = jnp.zeros_like(dc_ref)
            dwa_ref[...] = jnp.zeros_like(dwa_ref)
            dwx_ref[...] = jnp.zeros_like(dwx_ref)
            dv_ref[...] = jnp.zeros_like(dv_ref)
            gc[...] = jnp.zeros_like(gc)
            an[...] = jnp.zeros_like(an)
            dxn[...] = jnp.zeros_like(dxn)

        xr = z_ref[:, W:2 * W].astype(F32)
        yg = z_ref[:, 0:W].astype(F32)
        halo = jnp.where(ti == 0, 0.0, halo_ref[...].astype(F32))
        xc, x1, x2, x3 = _lru_conv(xr, halo, cw_ref, cb_ref)
        sp = _softplus_neg(lam_ref[...])
        r, gi, a, mult = _lru_gates(xc, wa_ref, ba_ref, wx_ref, bx_ref, sp)
        h = h_ref[...]
        hp = jnp.where(ti == 0, 0.0, hprev_ref[...])
        hm1 = _shift_rows(jnp.concatenate([hp, h], axis=0), 1)[SUBLANES:]
        dout = dm_ref[...].astype(F32)
        d_s[...] = dout * _gelu(yg)
        dz_ref[:, 0:W] = (dout * h * _gelu_grad(yg)).astype(BF16)
        c_s[...] = _shift_rows(jnp.concatenate([a, an[...]], axis=0), tt + SUBLANES - 1)[:tt]
        an[...] = a[0:SUBLANES, :]
        row = lax.broadcasted_iota(jnp.int32, (SUBLANES, W), 0)

        def step(j, gnext):
            r0 = pl.multiple_of((ng - 1 - j) * SUBLANES, SUBLANES)
            C = c_s[pl.ds(r0, SUBLANES), :]
            G = d_s[pl.ds(r0, SUBLANES), :]
            for k in (1, 2, 4):
                m = row < SUBLANES - k
                G = jnp.where(m, G + C * pltpu.roll(G, SUBLANES - k, 0), G)
                C = jnp.where(m, C * pltpu.roll(C, SUBLANES - k, 0), C)
            G = G + C * gnext
            g_s[pl.ds(r0, SUBLANES), :] = G
            return jnp.broadcast_to(G[0:1, :], (SUBLANES, W))

        gc[...] = lax.fori_loop(0, ng, step, gc[...])
        du = g_s[...]
        da = du * hm1
        dgi = du * mult * xc
        dxc = du * mult * gi
        dmult = du * gi * xc
        dlog_a = da * a - dmult * (a * a) / mult
        dr = dlog_a * (-LRU_C * sp)
        dv_ref[2] += _rsum8(dlog_a * (-LRU_C * r))
        dpr = (dr * r * (1.0 - r)).astype(BF16)
        dpi = (dgi * gi * (1.0 - gi)).astype(BF16)
        dv_ref[0] += _rsum8(dpr.astype(F32))
        dv_ref[1] += _rsum8(dpi.astype(F32))
        xcb = xc.astype(BF16)
        dwa_ref[...] += lax.dot_general(xcb, dpr, TN, preferred_element_type=F32)
        dwx_ref[...] += lax.dot_general(xcb, dpi, TN, preferred_element_type=F32)
        dxc = dxc + jnp.dot(dpr, wat_ref[...], preferred_element_type=F32) + jnp.dot(dpi, wxt_ref[...], preferred_element_type=F32)
        n = tt + BF16_ROWS
        de = jnp.concatenate([dxc, dxn[...]], axis=0)
        dxr = (cw_ref[3:4, :] * dxc + cw_ref[2:3, :] * _shift_rows(de, n - 1)[:tt] + cw_ref[1:2, :] * _shift_rows(de, n - 2)[:tt]
               + cw_ref[0:1, :] * _shift_rows(de, n - 3)[:tt])
        dxn[...] = dxc[0:BF16_ROWS, :]
        dz_ref[:, W:2 * W] = dxr.astype(BF16)
        dc_ref[0] += _rsum8(dxc * x3)
        dc_ref[1] += _rsum8(dxc * x2)
        dc_ref[2] += _rsum8(dxc * x1)
        dc_ref[3] += _rsum8(dxc * xr)
        dc_ref[4] += _rsum8(dxc)

    rev = lambda i: nt - 1 - i
    return pl.pallas_call(
        body, name=name, grid=(nt,),
        in_specs=[pl.BlockSpec((tt, 2 * W), lambda i: (rev(i), 0)),
                  pl.BlockSpec((BF16_ROWS, W), lambda i: (jnp.maximum(rev(i) * hb - 1, 0), 1)),
                  pl.BlockSpec((tt, W), lambda i: (rev(i), 0)),
                  pl.BlockSpec((SUBLANES, W), lambda i: (jnp.maximum(rev(i) * sb - 1, 0), 0)),
                  pl.BlockSpec((tt, W), lambda i: (rev(i), 0)),
                  _resident((SUBLANES, W)), _resident((1, W)), _resident((W, W)), _resident((W, W)), _resident((1, W)),
                  _resident((W, W)), _resident((W, W)), _resident((1, W)), _resident((1, W))],
        out_specs=[pl.BlockSpec((tt, 2 * W), lambda i: (rev(i), 0)),
                   pl.BlockSpec((5, SUBLANES, W), lambda i: (0, 0, 0)),
                   pl.BlockSpec((W, W), lambda i: (0, 0)), pl.BlockSpec((W, W), lambda i: (0, 0)),
                   pl.BlockSpec((3, SUBLANES, W), lambda i: (0, 0, 0))],
        out_shape=[jax.ShapeDtypeStruct((T, 2 * W), BF16), jax.ShapeDtypeStruct((5, SUBLANES, W), F32),
                   jax.ShapeDtypeStruct((W, W), F32), jax.ShapeDtypeStruct((W, W), F32),
                   jax.ShapeDtypeStruct((3, SUBLANES, W), F32)],
        scratch_shapes=[pltpu.VMEM((tt, W), F32), pltpu.VMEM((tt, W), F32), pltpu.VMEM((tt, W), F32),
                        pltpu.VMEM((SUBLANES, W), F32), pltpu.VMEM((SUBLANES, W), F32), pltpu.VMEM((BF16_ROWS, W), F32)],
        compiler_params=_cp(1),
    )(z, z, hseq, hseq, dmix, cw, cb, wa, wat, ba, wx, wxt, bx, lam)


def _split3(x):
    hi = x.astype(BF16)
    r1 = x - hi.astype(F32)
    mid = r1.astype(BF16)
    lo = (r1 - mid.astype(F32)).astype(BF16)
    return hi, mid, lo


def _tri_matmul(tri, x):
    hi, mid, lo = _split3(x)
    return (jnp.dot(tri, hi, preferred_element_type=F32) + jnp.dot(tri, mid, preferred_element_type=F32)
            + jnp.dot(tri, lo, preferred_element_type=F32))


def _hg_chunk(q, fl, lb):
    C = q.shape[0]
    ri = lax.broadcasted_iota(jnp.int32, (C, C), 0)
    ci = lax.broadcasted_iota(jnp.int32, (C, C), 1)
    causal = ri >= ci
    sig = _sigmoid(fl)
    f = lb + (1.0 - lb) * sig
    k = 1.0 - f
    sq = _sigmoid(q)
    qf = q * sq
    b = _tri_matmul(causal.astype(BF16), jnp.log(f))
    bm = b[C // 2 - 1:C // 2, :]
    bl = b[C - 1:C, :]
    qt = qf * jnp.exp(b - bm)
    kt = k * jnp.exp(bm - b)
    qin = qf * jnp.exp(b)
    kout = k * jnp.exp(bl - b)
    att = jnp.where(causal, _dot(qt, kt, NT), 0.0)
    return dict(sig=sig, f=f, k=k, sq=sq, qf=qf, b=b, bm=bm, bl=bl, qt=qt, kt=kt, qin=qin, kout=kout, att=att,
                causal=causal, anti=ri <= ci, decay=jnp.exp(bl))


def _hgrn_fwd(z, lb, gn, name="hgrn_fwd", tt=256):
    T = z.shape[0]
    C = HG_CHUNK
    nc = tt // C
    Dh = HG_D

    def body(q_ref, f_ref, v_ref, g_ref, lb_ref, gn_ref, o_ref, ss_ref, st):
        @pl.when(pl.program_id(1) == 0)
        def _():
            st[...] = jnp.zeros_like(st)

        S = st[...]
        for c in range(nc):
            rows = slice(c * C, (c + 1) * C)
            ck = _hg_chunk(q_ref[rows, :].astype(F32), f_ref[rows, :].astype(F32), lb_ref[...])
            v = v_ref[rows, :]
            g = g_ref[rows, :].astype(F32)
            ss_ref[0, c] = S
            o = _dot(ck["att"], v) + _dot(ck["qin"], S, NT)
            S = ck["decay"] * S + _dot(v, ck["kout"], TN)
            rn = lax.rsqrt(jnp.mean(o * o, axis=-1, keepdims=True) + EPS)
            o_ref[rows, :] = (o * rn * gn_ref[...] * (g * _sigmoid(g))).astype(BF16)
        st[...] = S

    col = lambda base: (lambda h, i: (i, base + h))
    return pl.pallas_call(
        body, name=name, grid=(HG_HEADS, T // tt),
        in_specs=[pl.BlockSpec((tt, Dh), col(8)), pl.BlockSpec((tt, Dh), col(12)), pl.BlockSpec((tt, Dh), col(16)),
                  pl.BlockSpec((tt, Dh), col(20)), pl.BlockSpec((1, Dh), lambda h, i: (0, h)),
                  pl.BlockSpec((1, Dh), lambda h, i: (0, 0))],
        out_specs=[pl.BlockSpec((tt, Dh), lambda h, i: (i, h)),
                   pl.BlockSpec((1, nc, Dh, Dh), lambda h, i: (h, i, 0, 0))],
        out_shape=[jax.ShapeDtypeStruct((T, HG_HEADS * Dh), BF16),
                   jax.ShapeDtypeStruct((HG_HEADS, T // C, Dh, Dh), F32)],
        scratch_shapes=[pltpu.VMEM((Dh, Dh), F32)],
        compiler_params=_cp(2),
    )(z, z, z, z, lb, gn)


def _hgrn_bwd(z, ss, dmix, lb, gn, name="hgrn_bwd", tt=256):
    T = z.shape[0]
    C = HG_CHUNK
    nc = tt // C
    nt = T // tt
    Dh = HG_D

    def body(q_ref, f_ref, v_ref, g_ref, ss_ref, dm_ref, lb_ref, gn_ref, dq_ref, df_ref, dv_ref, dg_ref, dlb_ref, dgn_ref, dst):
        @pl.when(pl.program_id(1) == 0)
        def _():
            dst[...] = jnp.zeros_like(dst)
            dlb_ref[...] = jnp.zeros_like(dlb_ref)
            dgn_ref[...] = jnp.zeros_like(dgn_ref)

        dS = dst[...]
        lbv = lb_ref[...]
        gnv = gn_ref[...]
        rowc = lax.broadcasted_iota(jnp.int32, (C, Dh), 0)
        for c in reversed(range(nc)):
            rows = slice(c * C, (c + 1) * C)
            q = q_ref[rows, :].astype(F32)
            ck = _hg_chunk(q, f_ref[rows, :].astype(F32), lbv)
            v = v_ref[rows, :]
            g = g_ref[rows, :].astype(F32)
            S = ss_ref[0, c]
            o = _dot(ck["att"], v) + _dot(ck["qin"], S, NT)
            rn = lax.rsqrt(jnp.mean(o * o, axis=-1, keepdims=True) + EPS)
            on = o * rn
            dout = dm_ref[rows, :].astype(F32)
            sg, dsg = _silu_and_grad(g)
            d_ong = dout * sg
            dgn_ref[...] += _rsum8(d_ong * on)
            dg_ref[rows, :] = (dout * on * gnv * dsg).astype(BF16)
            don = d_ong * gnv
            do = rn * (don - on * jnp.mean(don * on, axis=-1, keepdims=True))
            dv_ref[rows, :] = (_dot(ck["att"], do, TN) + _dot(ck["kout"], dS, NT)).astype(BF16)
            datt = jnp.where(ck["causal"], _dot(do, v, NT), 0.0)
            dqt = _dot(datt, ck["kt"])
            dkt = _dot(datt, ck["qt"], TN)
            dqin = _dot(do, S)
            dkout = _dot(v, dS)
            ddecay = jnp.sum(dS * S, axis=0, keepdims=True)
            dS = _dot(do, ck["qin"], TN) + ck["decay"] * dS
            b, bm, bl = ck["b"], ck["bm"], ck["bl"]
            dqf = dqt * jnp.exp(b - bm) + dqin * jnp.exp(b)
            dk = dkt * jnp.exp(bm - b) + dkout * jnp.exp(bl - b)
            kk = dkout * ck["kout"]
            db = dqt * ck["qt"] - dkt * ck["kt"] + dqin * ck["qin"] - kk
            dbl = jnp.sum(kk, axis=0, keepdims=True) + ddecay * ck["decay"]
            db = db + jnp.where(rowc == C - 1, dbl, 0.0)
            dlogf = _tri_matmul(ck["anti"].astype(BF16), db)
            dfv = dlogf / ck["f"] - dk
            sig = ck["sig"]
            df_ref[rows, :] = (dfv * (1.0 - lbv) * sig * (1.0 - sig)).astype(BF16)
            dlb_ref[...] += _rsum8(dfv * (1.0 - sig))
            sq = ck["sq"]
            dq_ref[rows, :] = (dqf * (sq * (1.0 + q * (1.0 - sq)))).astype(BF16)
        dst[...] = dS

    rev = lambda i: nt - 1 - i
    col = lambda base: (lambda h, i: (rev(i), base + h))
    out_tok = pl.BlockSpec((tt, Dh), lambda h, i: (rev(i), h))
    acc = pl.BlockSpec((SUBLANES, Dh), lambda h, i: (h, 0))
    tok_shape = jax.ShapeDtypeStruct((T, HG_HEADS * Dh), BF16)
    acc_shape = jax.ShapeDtypeStruct((HG_HEADS * SUBLANES, Dh), F32)
    return pl.pallas_call(
        body, name=name, grid=(HG_HEADS, nt),
        in_specs=[pl.BlockSpec((tt, Dh), col(8)), pl.BlockSpec((tt, Dh), col(12)), pl.BlockSpec((tt, Dh), col(16)),
                  pl.BlockSpec((tt, Dh), col(20)),
                  pl.BlockSpec((1, nc, Dh, Dh), lambda h, i: (h, rev(i), 0, 0)),
                  pl.BlockSpec((tt, Dh), lambda h, i: (rev(i), 4 + h)),
                  pl.BlockSpec((1, Dh), lambda h, i: (0, h)), pl.BlockSpec((1, Dh), lambda h, i: (0, 0))],
        out_specs=[out_tok, out_tok, out_tok, out_tok, acc, acc],
        out_shape=[tok_shape, tok_shape, tok_shape, tok_shape, acc_shape, acc_shape],
        scratch_shapes=[pltpu.VMEM((Dh, Dh), F32)],
        compiler_params=_cp(2),
    )(z, z, z, z, ss, dmix, lb, gn)


def _sgu_core(p, lg_ref, lb_ref, wsc_ref, bsb_ref):
    Wd = D_MODEL
    G = SGU_CHUNK
    zz = _gelu(p)
    u = zz[:, :Wd]
    v = zz[:, Wd:]
    vc = v - jnp.mean(v, axis=-1, keepdims=True)
    rstd = lax.rsqrt(jnp.mean(vc * vc, axis=-1, keepdims=True) + EPS)
    vhat = vc * rstd
    vn = vhat * lg_ref[...] + lb_ref[...]
    svs = []
    for gi in range(SGU_G):
        svs.append(jnp.dot(wsc_ref[gi], vn[:, gi * G:(gi + 1) * G].astype(BF16), preferred_element_type=F32) + bsb_ref[gi])
    return u, vhat, rstd, vn, jnp.concatenate(svs, axis=1)


def _sgu_fwd(p1, lg, lbias, wsc, bsb, name="sgu_fwd", tt=512):
    T = p1.shape[0]
    Wd = D_MODEL
    C = SGU_CHUNK

    def body(p_ref, lg_ref, lb_ref, wsc_ref, bsb_ref, s_ref):
        for c in range(tt // C):
            rows = slice(c * C, (c + 1) * C)
            u, _, _, _, sv = _sgu_core(p_ref[rows, :].astype(F32), lg_ref, lb_ref, wsc_ref, bsb_ref)
            s_ref[rows, :] = (u * sv).astype(BF16)

    return pl.pallas_call(
        body, name=name, grid=(T // tt,),
        in_specs=[pl.BlockSpec((tt, 2 * Wd), lambda i: (i, 0)), _resident((1, Wd)), _resident((1, Wd)),
                  _resident((SGU_G, C, C)), _resident((SGU_G, C, C))],
        out_specs=pl.BlockSpec((tt, Wd), lambda i: (i, 0)),
        out_shape=jax.ShapeDtypeStruct((T, Wd), BF16),
        compiler_params=_cp(1),
    )(p1, lg, lbias, wsc, bsb)


def _sgu_bwd(p1, ds, lg, lbias, wsc, wsct, bsb, name="sgu_bwd", tt=512):
    T = p1.shape[0]
    Wd = D_MODEL
    C = SGU_CHUNK

    def body(p_ref, ds_ref, lg_ref, lb_ref, wsc_ref, wsct_ref, bsb_ref, dp_ref, dws_ref, dbs_ref, dlg_ref, dlb_ref, dbin_ref):
        @pl.when(pl.program_id(0) == 0)
        def _():
            dws_ref[...] = jnp.zeros_like(dws_ref)
            dbs_ref[...] = jnp.zeros_like(dbs_ref)
            dlg_ref[...] = jnp.zeros_like(dlg_ref)
            dlb_ref[...] = jnp.zeros_like(dlb_ref)
            dbin_ref[...] = jnp.zeros_like(dbin_ref)

        for c in range(tt // C):
            rows = slice(c * C, (c + 1) * C)
            p = p_ref[rows, :].astype(F32)
            u, vhat, rstd, vn, sv = _sgu_core(p, lg_ref, lb_ref, wsc_ref, bsb_ref)
            dsc = ds_ref[rows, :].astype(F32)
            du = dsc * sv
            dsv = dsc * u
            dvns = []
            for gi in range(SGU_G):
                cs = slice(gi * C, (gi + 1) * C)
                dsv_g = dsv[:, cs]
                dvns.append(jnp.dot(wsct_ref[gi], dsv_g.astype(BF16), preferred_element_type=F32))
                dws_ref[gi] += _dot(dsv_g, vn[:, cs], NT)
                dbs_ref[gi] += dsv_g
            dvn = jnp.concatenate(dvns, axis=1)
            dlg_ref[...] += _rsum8(dvn * vhat)
            dlb_ref[...] += _rsum8(dvn)
            dvh = dvn * lg_ref[...]
            dv = rstd * (dvh - jnp.mean(dvh, axis=-1, keepdims=True) - vhat * jnp.mean(dvh * vhat, axis=-1, keepdims=True))
            dp = jnp.concatenate([du, dv], axis=1) * _gelu_grad(p)
            dbin_ref[...] += _rsum8(dp)
            dp_ref[rows, :] = dp.astype(BF16)

    full3 = pl.BlockSpec((SGU_G, C, C), lambda i: (0, 0, 0))
    return pl.pallas_call(
        body, name=name, grid=(T // tt,),
        in_specs=[pl.BlockSpec((tt, 2 * Wd), lambda i: (i, 0)), pl.BlockSpec((tt, Wd), lambda i: (i, 0)),
                  _resident((1, Wd)), _resident((1, Wd)), _resident((SGU_G, C, C)), _resident((SGU_G, C, C)),
                  _resident((SGU_G, C, C))],
        out_specs=[pl.BlockSpec((tt, 2 * Wd), lambda i: (i, 0)), full3, full3,
                   pl.BlockSpec((SUBLANES, Wd), lambda i: (0, 0)), pl.BlockSpec((SUBLANES, Wd), lambda i: (0, 0)),
                   pl.BlockSpec((SUBLANES, 2 * Wd), lambda i: (0, 0))],
        out_shape=[jax.ShapeDtypeStruct((T, 2 * Wd), BF16), jax.ShapeDtypeStruct((SGU_G, C, C), F32),
                   jax.ShapeDtypeStruct((SGU_G, C, C), F32), jax.ShapeDtypeStruct((SUBLANES, Wd), F32),
                   jax.ShapeDtypeStruct((SUBLANES, Wd), F32), jax.ShapeDtypeStruct((SUBLANES, 2 * Wd), F32)],
        compiler_params=_cp(1),
    )(p1, ds, lg, lbias, wsc, wsct, bsb)


def _pad_rows(w, rows=SUBLANES):
    return jnp.pad(w, ((0, rows - w.shape[0]), (0, 0)))


def _block_diag(w):
    n, b, _ = w.shape
    return (w[:, :, None, :] * jnp.eye(n, dtype=w.dtype)[:, None, :, None]).reshape(n * b, n * b)


def _diag_blocks(m, n):
    b = m.shape[0] // n
    m4 = m.reshape(n, b, n, b)
    return jnp.stack([m4[k, :, k, :] for k in range(n)], axis=0)


def _ffn_fwd(h, g, w_up, cw, cb, w_down, tag):
    hn, gu = _norm_mm(h, g, w_up, jnp.zeros((1, w_up.shape[1]), F32), name=f"ffn_up_{tag}")
    a = _ffn_act(gu, cw, cb, name=f"ffn_act_{tag}")
    out = _mm(a, w_down, h, F32, name=f"ffn_down_{tag}")
    return out, (hn, gu, a)


def _ffn_bwd(dh, h, g, saved, w_up_t, cw, cb, w_down_t, tag):
    hn, gu, a = saved
    da = _mm(dh, w_down_t, None, BF16, name=f"ffn_da_{tag}")
    dwd = _mm_tn(a, dh, name=f"ffn_dwd_{tag}", tk=D_FF // 2)
    dgu, dc = _ffn_act_bwd(gu, da, cw, cb, name=f"ffn_actb_{tag}")
    dhin, dg8 = _mm_normbwd(dgu, w_up_t, h, g, dh, name=f"ffn_dh_{tag}")
    dwu = _mm_tn(hn, dgu, name=f"ffn_dwu_{tag}", tk=256)
    dcs = dc.sum(axis=1)
    return dhin, dg8.sum(axis=0), dwu, dcs[0:3], dcs[3], dwd


def _local_step(x, tgt, p):
    row = lambda v: v.reshape(1, -1)
    grads = {}

    lower = jax.nn.softmax(p["hg_lb_logits"], axis=0)
    lb0 = row(lower[0])
    ev_cw = _pad_rows(p["ev_conv_w"][0])
    ev_cb = row(p["ev_conv_b"][0])
    wa = _block_diag(p["ev_gate_a_w"][0]).astype(BF16)
    wx = _block_diag(p["ev_gate_x_w"][0]).astype(BF16)
    ba, bx, lam = row(p["ev_gate_a_b"][0]), row(p["ev_gate_x_b"][0]), row(p["ev_lru_lambda"][0])
    gn = row(p["ev_hg_norm"][0])
    tril = jnp.tril(jnp.ones((SGU_CHUNK, SGU_CHUNK), F32))
    wsc = (p["od_w_s"][0] * tril).astype(BF16)
    bsb = jnp.broadcast_to(p["od_b_s"][0][:, :, None], (SGU_G, SGU_CHUNK, SGU_CHUNK)).astype(F32)
    ffn_cw = [_pad_rows(p["ffn_conv_w"][l]) for l in range(2)]
    ffn_cb = [row(p["ffn_conv_b"][l]) for l in range(2)]
    ev_w_in, ev_w_out = p["ev_w_in"][0], p["ev_w_out"][0]
    od_w_in, od_w_out = p["od_w_in"][0], p["od_w_out"][0]
    w_up = [p["ffn_w_up"][l] for l in range(2)]
    w_down = [p["ffn_w_down"][l] for l in range(2)]
    nm = [row(p["norm_mix"][l]) for l in range(2)]
    nf = [row(p["norm_ffn"][l]) for l in range(2)]

    h0 = x
    hn0, z0 = _norm_mm(h0, nm[0], ev_w_in, jnp.zeros((1, ev_w_in.shape[1]), F32), name="ev_in")
    out_a, hseq = _lru_fwd(z0, ev_cw, ev_cb, wa, ba, wx, bx, lam)
    out_b, ss = _hgrn_fwd(z0, lb0, gn)
    mix0 = jnp.concatenate([out_a, out_b], axis=1)
    h1 = _mm(mix0, ev_w_out, h0, F32, name="ev_out")
    h2, ffn0 = _ffn_fwd(h1, nf[0], w_up[0], ffn_cw[0], ffn_cb[0], w_down[0], "l0")
    hn1, p1 = _norm_mm(h2, nm[1], od_w_in, row(p["od_b_in"][0]), name="od_in")
    s1 = _sgu_fwd(p1, row(p["od_ln_g"][0]), row(p["od_ln_b"][0]), wsc, bsb)
    h3 = _mm(s1, od_w_out, h2, F32, name="od_out")
    h4, ffn1 = _ffn_fwd(h3, nf[1], w_up[1], ffn_cw[1], ffn_cb[1], w_down[1], "l1")
    dh4, dgf8, sq8 = _final_loss(h4, row(p["norm_final"]), tgt)
    grads["norm_final"] = dgf8.sum(axis=0)

    dh3, dnf1, dwu1, dcw1, dcb1, dwd1 = _ffn_bwd(dh4, h3, nf[1], ffn1, w_up[1].T, ffn_cw[1], ffn_cb[1], w_down[1].T, "l1")
    ds1 = _mm(dh3, od_w_out.T, None, BF16, name="od_ds")
    grads["od_w_out"] = _mm_tn(s1, dh3, name="od_dwo", tk=512)[None]
    wsct = jnp.swapaxes(wsc, 1, 2)
    dp1, dws, dbs, dlg8, dlb8, dbin8 = _sgu_bwd(p1, ds1, row(p["od_ln_g"][0]), row(p["od_ln_b"][0]), wsc, wsct, bsb)
    grads["od_w_s"] = (dws * tril)[None]
    grads["od_b_s"] = dbs.sum(axis=-1)[None]
    grads["od_ln_g"] = dlg8.sum(axis=0)[None]
    grads["od_ln_b"] = dlb8.sum(axis=0)[None]
    grads["od_b_in"] = dbin8.sum(axis=0)[None]
    dh2, dnm1 = _mm_normbwd(dp1, od_w_in.T, h2, nm[1], dh3, name="od_dh")
    grads["od_w_in"] = _mm_tn(hn1, dp1, name="od_dwi", tk=512)[None]

    dh1, dnf0, dwu0, dcw0, dcb0, dwd0 = _ffn_bwd(dh2, h1, nf[0], ffn0, w_up[0].T, ffn_cw[0], ffn_cb[0], w_down[0].T, "l0")
    dmix = _mm(dh1, ev_w_out.T, None, BF16, name="ev_dmix")
    grads["ev_w_out"] = _mm_tn(mix0, dh1, name="ev_dwo", tk=512)[None]
    dz01, dc5, dwa, dwx, dvec = _lru_bwd(z0, hseq, dmix, ev_cw, ev_cb, wa, wa.T, ba, wx, wx.T, bx, lam)
    dq, df, dv, dg, dlb32, dgn32 = _hgrn_bwd(z0, ss, dmix, lb0, gn)
    dz0 = jnp.concatenate([dz01, dq, df, dv, dg], axis=1)
    grad_x, dnm0 = _mm_normbwd(dz0, ev_w_in.T, h0, nm[0], dh1, name="ev_dh")
    grads["ev_w_in"] = _mm_tn(hn0, dz0, name="ev_dwi", tk=512)[None]

    dc5s = dc5.sum(axis=1)
    grads["ev_conv_w"] = dc5s[0:4][None]
    grads["ev_conv_b"] = dc5s[4][None]
    grads["ev_gate_a_w"] = _diag_blocks(dwa, LRU_BLOCKS)[None]
    grads["ev_gate_x_w"] = _diag_blocks(dwx, LRU_BLOCKS)[None]
    dvs = dvec.sum(axis=1)
    grads["ev_gate_a_b"] = dvs[0][None]
    grads["ev_gate_x_b"] = dvs[1][None]
    grads["ev_lru_lambda"] = (dvs[2] * (-jax.nn.sigmoid(-p["ev_lru_lambda"][0])))[None]
    dlb = dlb32.reshape(HG_HEADS, SUBLANES, HG_D).sum(axis=1).reshape(-1)
    grads["hg_lb_logits"] = dlb[None, :] * lower[0][None, :] * (jnp.eye(3, dtype=F32)[0][:, None] - lower)
    grads["ev_hg_norm"] = dgn32.reshape(HG_HEADS, SUBLANES, HG_D).sum(axis=(0, 1))[None]
    grads["norm_mix"] = jnp.stack([dnm0.sum(axis=0), dnm1.sum(axis=0)])
    grads["norm_ffn"] = jnp.stack([dnf0, dnf1])
    grads["ffn_w_up"] = jnp.stack([dwu0, dwu1])
    grads["ffn_conv_w"] = jnp.stack([dcw0, dcw1])
    grads["ffn_conv_b"] = jnp.stack([dcb0, dcb1])
    grads["ffn_w_down"] = jnp.stack([dwd0, dwd1])
    return sq8, grad_x, grads


MESH = pl.DeviceIdType.MESH
ANY = pl.BlockSpec(memory_space=pl.ANY)
N_CHIPS = 4
N_DEV = 8

SH_BIG = {"ev_w_in": 2, "ev_w_out": 1, "od_w_in": 2, "od_w_out": 1, "ffn_w_up": 2, "ffn_w_down": 1}
SH_SMALL = {"ev_conv_w": 2, "od_b_in": 1, "od_ln_g": 1, "od_ln_b": 1, "ffn_conv_w": 2}
REP = ["norm_mix", "norm_ffn", "norm_final", "ev_conv_b", "ev_gate_a_w", "ev_gate_a_b", "ev_gate_x_w", "ev_gate_x_b",
       "ev_lru_lambda", "hg_lb_logits", "ev_hg_norm", "od_w_s", "od_b_s", "ffn_conv_b"]
WEIGHTS = ["norm_mix", "norm_ffn", "norm_final", "ev_w_in", "ev_conv_w", "ev_conv_b", "ev_gate_a_w", "ev_gate_a_b", "ev_gate_x_w",
           "ev_gate_x_b", "ev_lru_lambda", "hg_lb_logits", "ev_hg_norm", "ev_w_out", "od_w_in", "od_b_in", "od_ln_g", "od_ln_b",
           "od_w_s", "od_b_s", "od_w_out", "ffn_w_up", "ffn_conv_w", "ffn_conv_b", "ffn_w_down"]


def _rows(n_elems, mult=SUBLANES):
    r = -(-n_elems // LANES)
    return -(-r // mult) * mult


def _pack(arrs, rows, dtype):
    flat = jnp.concatenate([a.reshape(-1).astype(dtype) for a in arrs])
    return jnp.pad(flat, (0, rows * LANES - flat.shape[0])).reshape(rows, LANES)


def _unpack(flat2d, shapes):
    flat = flat2d.reshape(-1)
    out, off = [], 0
    for s in shapes:
        n = 1
        for d in s:
            n *= d
        out.append(flat[off:off + n].reshape(s))
        off += n
    return out


def _mesh_pos():
    return lax.axis_index("x"), lax.axis_index("y"), lax.axis_index("c")


def _other_chips(x, y):
    return [(1 - x, y), (x, 1 - y), (1 - x, 1 - y)]


def _gather_chips(big, small):
    rb, rs = big.shape[0], small.shape[0]

    def body(b_ref, s_ref, ob_ref, os_ref, send_sems, recv_sems, loc_sems):
        x, y, c = _mesh_pos()
        k = 2 * x + y
        chips = _other_chips(x, y)

        def copy(src, dst, slot, j, to):
            return pltpu.make_async_remote_copy(src_ref=src, dst_ref=dst.at[slot], send_sem=send_sems.at[j], recv_sem=recv_sems.at[j],
                                                device_id=to, device_id_type=MESH)

        lb = pltpu.make_async_copy(b_ref, ob_ref.at[k], loc_sems.at[0])
        ls = pltpu.make_async_copy(s_ref, os_ref.at[k], loc_sems.at[1])
        lb.start()
        ls.start()
        sends = []
        for j, (px, py) in enumerate(chips):
            sends.append(copy(b_ref, ob_ref, k, 2 * j, (px, py, c)))
            sends.append(copy(s_ref, os_ref, k, 2 * j + 1, (px, py, c)))
        for cp in sends:
            cp.start()
        for j, (px, py) in enumerate(chips):
            copy(b_ref, ob_ref, 2 * px + py, 2 * j, (px, py, c)).wait_recv()
            copy(s_ref, os_ref, 2 * px + py, 2 * j + 1, (px, py, c)).wait_recv()
        for cp in sends:
            cp.wait_send()
        lb.wait()
        ls.wait()

    return pl.pallas_call(
        body, name="gather_chips", in_specs=[ANY, ANY], out_specs=[ANY, ANY],
        out_shape=[jax.ShapeDtypeStruct((N_CHIPS, rb, LANES), big.dtype), jax.ShapeDtypeStruct((N_CHIPS, rs, LANES), small.dtype)],
        scratch_shapes=[pltpu.SemaphoreType.DMA((6,)), pltpu.SemaphoreType.DMA((6,)), pltpu.SemaphoreType.DMA((2,))],
    )(big, small)


def _send_sibling_half(g):
    r = g.shape[2]

    def body(g_ref, o_ref, send_sems, recv_sems):
        x, y, c = _mesh_pos()
        cps = [pltpu.make_async_remote_copy(src_ref=g_ref.at[k, 1 - c], dst_ref=o_ref.at[k], send_sem=send_sems.at[k],
                                            recv_sem=recv_sems.at[k], device_id=(x, y, 1 - c), device_id_type=MESH)
               for k in range(N_CHIPS)]
        for cp in cps:
            cp.start()
        for cp in cps:
            cp.wait()

    return pl.pallas_call(
        body, name="send_sibling_half", in_specs=[ANY], out_specs=ANY,
        out_shape=jax.ShapeDtypeStruct((N_CHIPS, r, LANES), g.dtype),
        scratch_shapes=[pltpu.SemaphoreType.DMA((N_CHIPS,)), pltpu.SemaphoreType.DMA((N_CHIPS,))],
    )(g)


def _add_half(g, recv, c, tr):
    r = g.shape[2]

    def body(c_ref, g_ref, r_ref, o_ref):
        o_ref[...] = g_ref[0] + r_ref[...]

    return pl.pallas_call(
        body, name="add_half",
        grid_spec=pltpu.PrefetchScalarGridSpec(
            num_scalar_prefetch=1, grid=(N_CHIPS, r // tr),
            in_specs=[pl.BlockSpec((1, 1, tr, LANES), lambda k, i, c_ref: (k, c_ref[0], i, 0)),
                      pl.BlockSpec((1, tr, LANES), lambda k, i, c_ref: (k, i, 0))],
            out_specs=pl.BlockSpec((1, tr, LANES), lambda k, i, c_ref: (k, i, 0))),
        out_shape=jax.ShapeDtypeStruct((N_CHIPS, r, LANES), g.dtype),
        compiler_params=_cp(2),
    )(c, g, recv)


def _scatter_chips(h):
    r = h.shape[1]

    def body(h_ref, o_ref, send_sems, recv_sems, loc_sem):
        x, y, c = _mesh_pos()
        k = 2 * x + y
        chips = _other_chips(x, y)
        loc = pltpu.make_async_copy(h_ref.at[k], o_ref.at[k], loc_sem)
        loc.start()
        sends = [pltpu.make_async_remote_copy(src_ref=h_ref.at[2 * px + py], dst_ref=o_ref.at[k], send_sem=send_sems.at[j],
                                              recv_sem=recv_sems.at[j], device_id=(px, py, c), device_id_type=MESH)
                 for j, (px, py) in enumerate(chips)]
        for cp in sends:
            cp.start()
        for j, (px, py) in enumerate(chips):
            pltpu.make_async_remote_copy(src_ref=h_ref.at[k], dst_ref=o_ref.at[2 * px + py], send_sem=send_sems.at[j],
                                         recv_sem=recv_sems.at[j], device_id=(px, py, c), device_id_type=MESH).wait_recv()
        for cp in sends:
            cp.wait_send()
        loc.wait()

    return pl.pallas_call(
        body, name="scatter_chips", in_specs=[ANY], out_specs=ANY,
        out_shape=jax.ShapeDtypeStruct((N_CHIPS, r, LANES), h.dtype),
        scratch_shapes=[pltpu.SemaphoreType.DMA((3,)), pltpu.SemaphoreType.DMA((3,)), pltpu.SemaphoreType.DMA],
    )(h)


def _add_chips(p, tr):
    r = p.shape[1]

    def body(p_ref, o_ref):
        o_ref[...] = ((p_ref[0] + p_ref[1]) + p_ref[2]) + p_ref[3]

    return pl.pallas_call(
        body, name="add_chips", grid=(r // tr,),
        in_specs=[pl.BlockSpec((N_CHIPS, tr, LANES), lambda i: (0, i, 0))],
        out_specs=pl.BlockSpec((tr, LANES), lambda i: (i, 0)),
        out_shape=jax.ShapeDtypeStruct((r, LANES), p.dtype),
        compiler_params=_cp(1),
    )(p)


def _share_reduced(f, r_sh):
    r = f.shape[0]
    r_rep = r - r_sh

    def body(f_ref, osh_ref, orep_ref, send_sems, recv_sems, loc_sems):
        x, y, c = _mesh_pos()
        me = 4 * x + 2 * y + c
        sh = f_ref.at[pl.ds(0, r_sh)]
        rep = f_ref.at[pl.ds(r_sh, r_rep)]
        l0 = pltpu.make_async_copy(sh, osh_ref.at[c], loc_sems.at[0])
        l1 = pltpu.make_async_copy(rep, orep_ref.at[me], loc_sems.at[1])
        l0.start()
        l1.start()
        peers = []
        for j in range(1, N_DEV):
            peers.append(((1 - x) if j & 4 else x, (1 - y) if j & 2 else y, (1 - c) if j & 1 else c))
        sends = [pltpu.make_async_remote_copy(src_ref=sh, dst_ref=osh_ref.at[c], send_sem=send_sems.at[0], recv_sem=recv_sems.at[0],
                                              device_id=(x, y, 1 - c), device_id_type=MESH)]
        for j, (px, py, pc) in enumerate(peers):
            sends.append(pltpu.make_async_remote_copy(src_ref=rep, dst_ref=orep_ref.at[me], send_sem=send_sems.at[1 + j],
                                                      recv_sem=recv_sems.at[1 + j], device_id=(px, py, pc), device_id_type=MESH))
        for cp in sends:
            cp.start()
        pltpu.make_async_remote_copy(src_ref=sh, dst_ref=osh_ref.at[1 - c], send_sem=send_sems.at[0], recv_sem=recv_sems.at[0],
                                     device_id=(x, y, 1 - c), device_id_type=MESH).wait_recv()
        for j, (px, py, pc) in enumerate(peers):
            pltpu.make_async_remote_copy(src_ref=rep, dst_ref=orep_ref.at[4 * px + 2 * py + pc], send_sem=send_sems.at[1 + j],
                                         recv_sem=recv_sems.at[1 + j], device_id=(px, py, pc), device_id_type=MESH).wait_recv()
        for cp in sends:
            cp.wait_send()
        l0.wait()
        l1.wait()

    return pl.pallas_call(
        body, name="share_reduced", in_specs=[ANY], out_specs=[ANY, ANY],
        out_shape=[jax.ShapeDtypeStruct((2, r_sh, LANES), f.dtype), jax.ShapeDtypeStruct((N_DEV, r_rep, LANES), f.dtype)],
        scratch_shapes=[pltpu.SemaphoreType.DMA((N_DEV,)), pltpu.SemaphoreType.DMA((N_DEV,)), pltpu.SemaphoreType.DMA((2,))],
    )(f)


def _adamw(w, g, m, v, name):
    R, C = w.shape
    tr = R
    for cand in (512, 256, 128, 64, 32, 16, 8):
        if R % cand == 0 and cand * C * 4 <= 2 * 1024 * 1024:
            tr = cand
            break
    c1 = 1.0 / (1.0 - ADAM_B1 ** ADAM_STEP)
    c2 = 1.0 / (1.0 - ADAM_B2 ** ADAM_STEP)

    def body(w_ref, g_ref, m_ref, v_ref, d_ref, mo_ref, vo_ref):
        gv = g_ref[...]
        mn = ADAM_B1 * m_ref[...] + (1.0 - ADAM_B1) * gv
        vn = ADAM_B2 * v_ref[...] + (1.0 - ADAM_B2) * (gv * gv)
        mo_ref[...] = mn
        vo_ref[...] = vn
        d_ref[...] = -ADAM_LR * ((mn * c1) / (jnp.sqrt(vn * c2) + ADAM_EPS) + ADAM_WD * w_ref[...])

    spec = pl.BlockSpec((tr, C), lambda i: (i, 0))
    shp = jax.ShapeDtypeStruct((R, C), F32)
    return pl.pallas_call(body, name=name, grid=(R // tr,), in_specs=[spec] * 4, out_specs=[spec] * 3, out_shape=[shp] * 3,
                          compiler_params=_cp(1))(w, g, m, v)


def _shard_shape(full_shape, axis):
    s = list(full_shape)
    s[axis] //= N_CHIPS
    return tuple(s)


def _step(a):
    x, y, c = _mesh_pos()
    sh_all = {**SH_BIG, **SH_SMALL}

    rb = _rows(sum(a[n].size for n in SH_BIG), BF16_ROWS)
    rs = _rows(sum(a[n].size for n in SH_SMALL))
    gb, gs = _gather_chips(_pack([a[n] for n in SH_BIG], rb, BF16), _pack([a[n] for n in SH_SMALL], rs, F32))
    p = {n: a[n] for n in REP}
    for names, buf in ((SH_BIG, gb), (SH_SMALL, gs)):
        parts = [_unpack(buf[k], [a[n].shape for n in names]) for k in range(N_CHIPS)]
        for i, n in enumerate(names):
            p[n] = jnp.concatenate([parts[k][i] for k in range(N_CHIPS)], axis=names[n])

    sq8, grad_x, grads = _local_step(a["x"][0], a["loss_target"][0], p)
    loss = lax.psum(0.5 / D_MODEL * jnp.sum(sq8), ("x", "y", "c"))

    sh_names = list(SH_BIG) + list(SH_SMALL)
    n_sh = sum(a[n].size for n in sh_names)
    r_sh = _rows(n_sh, 2 * SUBLANES) // 2
    n_rep = sum(a[n].size for n in REP)
    r_rep = _rows(-(-n_rep // N_DEV))
    tr = 1280
    r_piece = -(-(r_sh + r_rep) // tr) * tr
    r_rep = r_piece - r_sh
    shard_rows = []
    for k in range(N_CHIPS):
        pieces = [lax.slice_in_dim(grads[n], k * a[n].shape[sh_all[n]], (k + 1) * a[n].shape[sh_all[n]], axis=sh_all[n]) for n in sh_names]
        shard_rows.append(_pack(pieces, 2 * r_sh, F32).reshape(2, r_sh, LANES))
    g_sh = jnp.stack(shard_rows)
    g_rep = _pack([grads[n] for n in REP], N_DEV * r_rep, F32).reshape(N_CHIPS, 2, r_rep, LANES)
    g_all = jnp.concatenate([g_sh, g_rep], axis=2)
    from_sibling = _send_sibling_half(g_all)
    chip_sum = _add_half(g_all, from_sibling, c.reshape(1).astype(jnp.int32), tr)
    from_chips = _scatter_chips(chip_sum)
    mine = _add_chips(from_chips, tr)
    red_sh, red_rep = _share_reduced(mine, r_sh)
    g_shard = dict(zip(sh_names, _unpack(red_sh, [a[n].shape for n in sh_names])))
    g_repl = dict(zip(REP, _unpack(red_rep, [a[n].shape for n in REP])))
    gfin = {**g_shard, **g_repl}

    out = {"loss": loss, "grad_x": grad_x[None]}
    small_names = list(SH_SMALL) + REP
    for n in SH_BIG:
        shp = a[n].shape
        two_d = lambda t: t.reshape(-1, shp[-1])
        d, mo, vo = _adamw(two_d(a[n]), two_d(gfin[n]), two_d(a["m_" + n]), two_d(a["v_" + n]), name=f"adamw_{n}")
        out["delta_" + n], out["new_m_" + n], out["new_v_" + n] = d.reshape(shp), mo.reshape(shp), vo.reshape(shp)
    r_small = _rows(sum(a[n].size for n in small_names))
    packs = [_pack([src(n) for n in small_names], r_small, F32)
             for src in (lambda n: a[n], lambda n: gfin[n], lambda n: a["m_" + n], lambda n: a["v_" + n])]
    d, mo, vo = _adamw(*packs, name="adamw_small")
    shapes = [a[n].shape for n in small_names]
    for n, dd, mm, vv in zip(small_names, _unpack(d, shapes), _unpack(mo, shapes), _unpack(vo, shapes)):
        out["delta_" + n], out["new_m_" + n], out["new_v_" + n] = dd, mm, vv
    for n in WEIGHTS:
        out["grad_" + n] = gfin[n]
    return out


def kernel(x, norm_mix, norm_ffn, norm_final, ev_w_in, ev_conv_w, ev_conv_b, ev_gate_a_w, ev_gate_a_b, ev_gate_x_w, ev_gate_x_b, ev_lru_lambda, hg_lb_logits, ev_hg_norm, ev_w_out, od_w_in, od_b_in, od_ln_g, od_ln_b, od_w_s, od_b_s, od_w_out, ffn_w_up, ffn_conv_w, ffn_conv_b, ffn_w_down, loss_target, m_norm_mix, m_norm_ffn, m_norm_final, m_ev_w_in, m_ev_conv_w, m_ev_conv_b, m_ev_gate_a_w, m_ev_gate_a_b, m_ev_gate_x_w, m_ev_gate_x_b, m_ev_lru_lambda, m_hg_lb_logits, m_ev_hg_norm, m_ev_w_out, m_od_w_in, m_od_b_in, m_od_ln_g, m_od_ln_b, m_od_w_s, m_od_b_s, m_od_w_out, m_ffn_w_up, m_ffn_conv_w, m_ffn_conv_b, m_ffn_w_down, v_norm_mix, v_norm_ffn, v_norm_final, v_ev_w_in, v_ev_conv_w, v_ev_conv_b, v_ev_gate_a_w, v_ev_gate_a_b, v_ev_gate_x_w, v_ev_gate_x_b, v_ev_lru_lambda, v_hg_lb_logits, v_ev_hg_norm, v_ev_w_out, v_od_w_in, v_od_b_in, v_od_ln_g, v_od_ln_b, v_od_w_s, v_od_b_s, v_od_w_out, v_ffn_w_up, v_ffn_conv_w, v_ffn_conv_b, v_ffn_w_down):
    vals = (x, norm_mix, norm_ffn, norm_final, ev_w_in, ev_conv_w, ev_conv_b, ev_gate_a_w, ev_gate_a_b, ev_gate_x_w, ev_gate_x_b, ev_lru_lambda, hg_lb_logits, ev_hg_norm, ev_w_out, od_w_in, od_b_in, od_ln_g, od_ln_b, od_w_s, od_b_s, od_w_out, ffn_w_up, ffn_conv_w, ffn_conv_b, ffn_w_down, loss_target, m_norm_mix, m_norm_ffn, m_norm_final, m_ev_w_in, m_ev_conv_w, m_ev_conv_b, m_ev_gate_a_w, m_ev_gate_a_b, m_ev_gate_x_w, m_ev_gate_x_b, m_ev_lru_lambda, m_hg_lb_logits, m_ev_hg_norm, m_ev_w_out, m_od_w_in, m_od_b_in, m_od_ln_g, m_od_ln_b, m_od_w_s, m_od_b_s, m_od_w_out, m_ffn_w_up, m_ffn_conv_w, m_ffn_conv_b, m_ffn_w_down, v_norm_mix, v_norm_ffn, v_norm_final, v_ev_w_in, v_ev_conv_w, v_ev_conv_b, v_ev_gate_a_w, v_ev_gate_a_b, v_ev_gate_x_w, v_ev_gate_x_b, v_ev_lru_lambda, v_hg_lb_logits, v_ev_hg_norm, v_ev_w_out, v_od_w_in, v_od_b_in, v_od_ln_g, v_od_ln_b, v_od_w_s, v_od_b_s, v_od_w_out, v_ffn_w_up, v_ffn_conv_w, v_ffn_conv_b, v_ffn_w_down)
    names = ["x"] + WEIGHTS + ["loss_target"] + ["m_" + n for n in WEIGHTS] + ["v_" + n for n in WEIGHTS]
    out = _step(dict(zip(names, vals)))
    return (out["loss"], out["grad_x"], *[out["grad_" + n] for n in WEIGHTS], *[out["delta_" + n] for n in WEIGHTS],
            *[out["new_m_" + n] for n in WEIGHTS], *[out["new_v_" + n] for n in WEIGHTS])
```

```python
import functools

import jax
import jax.numpy as jnp
from jax import lax
from jax.experimental import pallas as pl
from jax.experimental.pallas import tpu as pltpu

F32 = jnp.float32
BF16 = jnp.bfloat16

EPS = 1e-6
D_MODEL = 1024
LRU_W = 512
LRU_BLOCKS = 8
LRU_C = 8.0
HG_HEADS = 4
HG_D = 128
HG_CHUNK = 64
SGU_G = 8
SGU_CHUNK = 128
D_FF = 2816
ADAM_LR, ADAM_B1, ADAM_B2, ADAM_EPS, ADAM_WD, ADAM_STEP = 0.001, 0.9, 0.999, 1e-08, 0.01, 10

V7X_VMEM_BYTES = 64 * 1024 * 1024
VMEM_LIMIT = V7X_VMEM_BYTES - 8 * 1024 * 1024
SUBLANES = 8
LANES = 128
BF16_ROWS = 16

GELU_C0 = 0.7978845608028654
GELU_C1 = 0.044715

NN = (((1,), (0,)), ((), ()))
NT = (((1,), (1,)), ((), ()))
TN = (((0,), (0,)), ((), ()))


def _dot(a, b, dims=NN):
    return lax.dot_general(a.astype(BF16), b.astype(BF16), dims, preferred_element_type=F32)


def _cp(n_grid):
    return pltpu.CompilerParams(dimension_semantics=("arbitrary",) * n_grid, vmem_limit_bytes=VMEM_LIMIT)


def _chunk(n, cap):
    best = LANES
    for c in range(LANES, cap + 1, LANES):
        if n % c == 0:
            best = c
    return best


def _resident(shape):
    nd = len(shape)
    return pl.BlockSpec(shape, lambda *_: (0,) * nd, pipeline_mode=pl.Buffered(1))


def _rsum8(x):
    r, c = x.shape
    return x.reshape(r // SUBLANES, SUBLANES, c).sum(axis=0)


def _sigmoid(x):
    return 1.0 / (1.0 + jnp.exp(-x))


def _gelu(x):
    return 0.5 * x * (1.0 + jnp.tanh(GELU_C0 * (x + GELU_C1 * x * x * x)))


def _gelu_grad(x):
    t = jnp.tanh(GELU_C0 * (x + GELU_C1 * x * x * x))
    return 0.5 * (1.0 + t) + 0.5 * x * (1.0 - t * t) * GELU_C0 * (1.0 + 3.0 * GELU_C1 * x * x)


def _silu_and_grad(x):
    s = _sigmoid(x)
    return x * s, s * (1.0 + x * (1.0 - s))


def _shift_rows(e, j):
    n = e.shape[0]
    return e if j % n == 0 else pltpu.roll(e, j % n, 0)


def _norm_mm(h, g, w, b, name, tt=512):
    T, D = h.shape
    N = w.shape[1]
    cn = _chunk(N, 512)

    def body(h_ref, g_ref, w_ref, b_ref, hn_ref, z_ref):
        x = h_ref[...]
        r = lax.rsqrt(jnp.mean(x * x, axis=-1, keepdims=True) + EPS)
        hn = (x * r * g_ref[...]).astype(BF16)
        hn_ref[...] = hn
        for j in range(0, N, cn):
            acc = jnp.dot(hn, w_ref[:, j:j + cn], preferred_element_type=F32) + b_ref[:, j:j + cn]
            z_ref[:, j:j + cn] = acc.astype(BF16)

    return pl.pallas_call(
        body, name=name, grid=(T // tt,),
        in_specs=[pl.BlockSpec((tt, D), lambda i: (i, 0)), _resident((1, D)), _resident((D, N)), _resident((1, N))],
        out_specs=[pl.BlockSpec((tt, D), lambda i: (i, 0)), pl.BlockSpec((tt, N), lambda i: (i, 0))],
        out_shape=[jax.ShapeDtypeStruct((T, D), BF16), jax.ShapeDtypeStruct((T, N), BF16)],
        compiler_params=_cp(1),
    )(h, g, w, b)


def _mm(a, w, res, out_dtype, name, tt=512, transpose_w=False):
    T, K = a.shape
    N = w.shape[0] if transpose_w else w.shape[1]
    cn = _chunk(N, 512)
    has_res = res is not None

    def body(*refs):
        a_ref, w_ref = refs[0], refs[1]
        res_ref = refs[2] if has_res else None
        o_ref = refs[-1]
        av = a_ref[...].astype(BF16)
        for j in range(0, N, cn):
            if transpose_w:
                acc = lax.dot_general(av, w_ref[j:j + cn, :], NT, preferred_element_type=F32)
            else:
                acc = jnp.dot(av, w_ref[:, j:j + cn], preferred_element_type=F32)
            if has_res:
                acc = acc + res_ref[:, j:j + cn]
            o_ref[:, j:j + cn] = acc.astype(out_dtype)

    in_specs = [pl.BlockSpec((tt, K), lambda i: (i, 0)), _resident(w.shape)]
    args = [a, w]
    if has_res:
        in_specs.append(pl.BlockSpec((tt, N), lambda i: (i, 0)))
        args.append(res)
    return pl.pallas_call(
        body, name=name, grid=(T // tt,), in_specs=in_specs,
        out_specs=pl.BlockSpec((tt, N), lambda i: (i, 0)),
        out_shape=jax.ShapeDtypeStruct((T, N), out_dtype),
        compiler_params=_cp(1),
    )(*args)


def _mm_tn(a, b, name, tk, tt=512, col_shards=1):
    T, K = a.shape
    N = b.shape[1]
    ns = N // col_shards

    def body(a_ref, b_ref, o_ref):
        acc = lax.dot_general(a_ref[...].astype(BF16), b_ref[...].astype(BF16), TN, preferred_element_type=F32)
        first = pl.program_id(1) == 0
        if col_shards == 1:
            prev = jnp.where(first, 0.0, o_ref[...])
            o_ref[...] = prev + acc
        else:
            for s in range(col_shards):
                prev = jnp.where(first, 0.0, o_ref[s])
                o_ref[s] = prev + acc[:, s * ns:(s + 1) * ns]

    if col_shards == 1:
        out_spec = pl.BlockSpec((tk, N), lambda k, t: (k, 0))
        out_shape = jax.ShapeDtypeStruct((K, N), F32)
    else:
        out_spec = pl.BlockSpec((col_shards, tk, ns), lambda k, t: (0, k, 0))
        out_shape = jax.ShapeDtypeStruct((col_shards, K, ns), F32)
    return pl.pallas_call(
        body, name=name, grid=(K // tk, T // tt),
        in_specs=[pl.BlockSpec((tt, tk), lambda k, t: (t, k)), pl.BlockSpec((tt, N), lambda k, t: (t, 0))],
        out_specs=out_spec, out_shape=out_shape,
        compiler_params=_cp(2),
    )(a, b)


def _mm_normbwd(dz, w, x, g, dres, name, tt=512):
    T, N = dz.shape
    D = w.shape[0]

    def body(dz_ref, wt_ref, x_ref, g_ref, dres_ref, dx_ref, dg_ref):
        @pl.when(pl.program_id(0) == 0)
        def _():
            dg_ref[...] = jnp.zeros_like(dg_ref)

        dy = lax.dot_general(dz_ref[...], wt_ref[...], NT, preferred_element_type=F32)
        x = x_ref[...]
        r = lax.rsqrt(jnp.mean(x * x, axis=-1, keepdims=True) + EPS)
        xn = x * r
        dg_ref[...] += _rsum8(dy * xn)
        dxn = dy * g_ref[...]
        dx_ref[...] = dres_ref[...] + r * (dxn - xn * jnp.mean(dxn * xn, axis=-1, keepdims=True))

    return pl.pallas_call(
        body, name=name, grid=(T // tt,),
        in_specs=[pl.BlockSpec((tt, N), lambda i: (i, 0)), _resident((D, N)), pl.BlockSpec((tt, D), lambda i: (i, 0)),
                  _resident((1, D)), pl.BlockSpec((tt, D), lambda i: (i, 0))],
        out_specs=[pl.BlockSpec((tt, D), lambda i: (i, 0)), pl.BlockSpec((SUBLANES, D), lambda i: (0, 0))],
        out_shape=[jax.ShapeDtypeStruct((T, D), F32), jax.ShapeDtypeStruct((SUBLANES, D), F32)],
        compiler_params=_cp(1),
    )(dz, w, x, g, dres)


def _final_loss(h, g, tgt, name="final_loss", tt=512):
    T, D = h.shape

    def body(h_ref, g_ref, t_ref, dh_ref, dg_ref, sq_ref):
        @pl.when(pl.program_id(0) == 0)
        def _():
            dg_ref[...] = jnp.zeros_like(dg_ref)
            sq_ref[...] = jnp.zeros_like(sq_ref)

        x = h_ref[...]
        r = lax.rsqrt(jnp.mean(x * x, axis=-1, keepdims=True) + EPS)
        xn = x * r
        gv = g_ref[...]
        diff = xn * gv - t_ref[...]
        sq_ref[...] += _rsum8(diff * diff)
        dy = diff * (1.0 / D)
        dg_ref[...] += _rsum8(dy * xn)
        dxn = dy * gv
        dh_ref[...] = r * (dxn - xn * jnp.mean(dxn * xn, axis=-1, keepdims=True))

    return pl.pallas_call(
        body, name=name, grid=(T // tt,),
        in_specs=[pl.BlockSpec((tt, D), lambda i: (i, 0)), _resident((1, D)), pl.BlockSpec((tt, D), lambda i: (i, 0))],
        out_specs=[pl.BlockSpec((tt, D), lambda i: (i, 0)), pl.BlockSpec((SUBLANES, D), lambda i: (0, 0)),
                   pl.BlockSpec((SUBLANES, D), lambda i: (0, 0))],
        out_shape=[jax.ShapeDtypeStruct((T, D), F32), jax.ShapeDtypeStruct((SUBLANES, D), F32),
                   jax.ShapeDtypeStruct((SUBLANES, D), F32)],
        compiler_params=_cp(1),
    )(h, g, tgt)


def _ffn_act(gu, cw, cb, name, tt=512):
    T = gu.shape[0]
    F = gu.shape[1] // 2
    cc = _chunk(F, 256)
    hb = tt // BF16_ROWS

    def body(gu_ref, halo_ref, cw_ref, cb_ref, a_ref):
        first = pl.program_id(0) == 0
        for c0 in range(0, F, cc):
            cs = slice(c0, c0 + cc)
            x = gu_ref[:, cs].astype(F32)
            halo = jnp.where(first, 0.0, halo_ref[:, cs].astype(F32))
            e = jnp.concatenate([halo, x], axis=0)
            gc = (cb_ref[:, cs] + cw_ref[0:1, cs] * _shift_rows(e, 2)[BF16_ROWS:] + cw_ref[1:2, cs] * _shift_rows(e, 1)[BF16_ROWS:]
                  + cw_ref[2:3, cs] * x)
            up = gu_ref[:, F + c0:F + c0 + cc].astype(F32)
            a_ref[:, cs] = (gc * _sigmoid(gc) * up).astype(BF16)

    return pl.pallas_call(
        body, name=name, grid=(T // tt,),
        in_specs=[pl.BlockSpec((tt, 2 * F), lambda i: (i, 0)),
                  pl.BlockSpec((BF16_ROWS, F), lambda i: (jnp.maximum(i * hb - 1, 0), 0)),
                  _resident((SUBLANES, F)), _resident((1, F))],
        out_specs=pl.BlockSpec((tt, F), lambda i: (i, 0)),
        out_shape=jax.ShapeDtypeStruct((T, F), BF16),
        compiler_params=_cp(1),
    )(gu, gu, cw, cb)


def _ffn_act_bwd(gu, da, cw, cb, name, tt=512):
    T = gu.shape[0]
    F = gu.shape[1] // 2
    cc = _chunk(F, 256)
    hb = tt // BF16_ROWS
    last_hb = T // BF16_ROWS - 1
    nt = T // tt

    def body(gu_ref, gprev_ref, gunext_ref, da_ref, danext_ref, cw_ref, cb_ref, dgu_ref, dc_ref):
        i = pl.program_id(0)

        @pl.when(i == 0)
        def _():
            dc_ref[...] = jnp.zeros_like(dc_ref)

        n = tt + BF16_ROWS
        for c0 in range(0, F, cc):
            cs = slice(c0, c0 + cc)
            us = slice(F + c0, F + c0 + cc)
            g = gu_ref[:, cs].astype(F32)
            gp = jnp.where(i == 0, 0.0, gprev_ref[:, cs].astype(F32))
            ge = jnp.concatenate([gp, g, gunext_ref[:, cs].astype(F32)], axis=0)
            g1 = _shift_rows(ge, 1)[BF16_ROWS:]
            g2 = _shift_rows(ge, 2)[BF16_ROWS:]
            gc = cb_ref[:, cs] + cw_ref[0:1, cs] * g2 + cw_ref[1:2, cs] * g1 + cw_ref[2:3, cs] * ge[BF16_ROWS:]
            up = jnp.concatenate([gu_ref[:, us].astype(F32), gunext_ref[:, us].astype(F32)], axis=0)
            dan = jnp.where(i == nt - 1, 0.0, danext_ref[:, cs].astype(F32))
            dae = jnp.concatenate([da_ref[:, cs].astype(F32), dan], axis=0)
            s, ds = _silu_and_grad(gc)
            dgc = dae * up * ds
            dgu_ref[:, us] = (dae * s)[:tt].astype(BF16)
            dgate = cw_ref[2:3, cs] * dgc + cw_ref[1:2, cs] * _shift_rows(dgc, n - 1) + cw_ref[0:1, cs] * _shift_rows(dgc, n - 2)
            dgu_ref[:, cs] = dgate[:tt].astype(BF16)
            dm = dgc[:tt]
            dc_ref[0, :, cs] += _rsum8(dm * g2[:tt])
            dc_ref[1, :, cs] += _rsum8(dm * g1[:tt])
            dc_ref[2, :, cs] += _rsum8(dm * g)
            dc_ref[3, :, cs] += _rsum8(dm)

    return pl.pallas_call(
        body, name=name, grid=(nt,),
        in_specs=[pl.BlockSpec((tt, 2 * F), lambda i: (i, 0)),
                  pl.BlockSpec((BF16_ROWS, F), lambda i: (jnp.maximum(i * hb - 1, 0), 0)),
                  pl.BlockSpec((BF16_ROWS, 2 * F), lambda i: (jnp.minimum((i + 1) * hb, last_hb), 0)),
                  pl.BlockSpec((tt, F), lambda i: (i, 0)),
                  pl.BlockSpec((BF16_ROWS, F), lambda i: (jnp.minimum((i + 1) * hb, last_hb), 0)),
                  _resident((SUBLANES, F)), _resident((1, F))],
        out_specs=[pl.BlockSpec((tt, 2 * F), lambda i: (i, 0)), pl.BlockSpec((4, SUBLANES, F), lambda i: (0, 0, 0))],
        out_shape=[jax.ShapeDtypeStruct((T, 2 * F), BF16), jax.ShapeDtypeStruct((4, SUBLANES, F), F32)],
        compiler_params=_cp(1),
    )(gu, gu, gu, da, da, cw, cb)


def _softplus_neg(lam):
    x = -lam
    y = jnp.exp(-jnp.abs(x))
    l1p = jnp.where(y < 0.01, y * (1.0 - y * (0.5 - y * (1.0 / 3.0))), jnp.log(1.0 + y))
    return jnp.maximum(x, 0.0) + l1p


def _lru_gates(xc, wa_ref, ba_ref, wx_ref, bx_ref, sp):
    xcb = xc.astype(BF16)
    r = _sigmoid(jnp.dot(xcb, wa_ref[...], preferred_element_type=F32) + ba_ref[...])
    gi = _sigmoid(jnp.dot(xcb, wx_ref[...], preferred_element_type=F32) + bx_ref[...])
    log_a = -LRU_C * r * sp
    a = jnp.exp(log_a)
    x2 = 2.0 * log_a
    series = -x2 * (1.0 + x2 * 0.5 * (1.0 + x2 * (1.0 / 3.0) * (1.0 + x2 * 0.25 * (1.0 + x2 * 0.2))))
    om = jnp.where(x2 > -0.125, series, 1.0 - a * a)
    return r, gi, a, jnp.sqrt(om)


def _lru_conv(xr, halo, cw_ref, cb_ref):
    e = jnp.concatenate([halo, xr], axis=0)
    x1 = _shift_rows(e, 1)[BF16_ROWS:]
    x2 = _shift_rows(e, 2)[BF16_ROWS:]
    x3 = _shift_rows(e, 3)[BF16_ROWS:]
    xc = cb_ref[...] + cw_ref[0:1, :] * x3 + cw_ref[1:2, :] * x2 + cw_ref[2:3, :] * x1 + cw_ref[3:4, :] * xr
    return xc, x1, x2, x3


def _lru_fwd(z, cw, cb, wa, ba, wx, bx, lam, name="lru_fwd", tt=256):
    T = z.shape[0]
    W = LRU_W
    hb = tt // BF16_ROWS
    ng = tt // SUBLANES

    def body(z_ref, halo_ref, cw_ref, cb_ref, wa_ref, ba_ref, wx_ref, bx_ref, lam_ref, oa_ref, h_ref, a_s, u_s, hc):
        i = pl.program_id(0)

        @pl.when(i == 0)
        def _():
            hc[...] = jnp.zeros_like(hc)

        xr = z_ref[:, W:2 * W].astype(F32)
        halo = jnp.where(i == 0, 0.0, halo_ref[...].astype(F32))
        xc, _, _, _ = _lru_conv(xr, halo, cw_ref, cb_ref)
        sp = _softplus_neg(lam_ref[...])
        r, gi, a, mult = _lru_gates(xc, wa_ref, ba_ref, wx_ref, bx_ref, sp)
        a_s[...] = a
        u_s[...] = mult * gi * xc
        row = lax.broadcasted_iota(jnp.int32, (SUBLANES, W), 0)

        def step(j, hprev):
            r0 = pl.multiple_of(j * SUBLANES, SUBLANES)
            A = a_s[pl.ds(r0, SUBLANES), :]
            U = u_s[pl.ds(r0, SUBLANES), :]
            for k in (1, 2, 4):
                m = row >= k
                U = jnp.where(m, A * pltpu.roll(U, k, 0) + U, U)
                A = jnp.where(m, A * pltpu.roll(A, k, 0), A)
            H = U + A * hprev
            h_ref[pl.ds(r0, SUBLANES), :] = H
            return jnp.broadcast_to(H[SUBLANES - 1:SUBLANES, :], (SUBLANES, W))

        hc[...] = lax.fori_loop(0, ng, step, hc[...])
        oa_ref[...] = (_gelu(z_ref[:, 0:W].astype(F32)) * h_ref[...]).astype(BF16)

    return pl.pallas_call(
        body, name=name, grid=(T // tt,),
        in_specs=[pl.BlockSpec((tt, 2 * W), lambda i: (i, 0)),
                  pl.BlockSpec((BF16_ROWS, W), lambda i: (jnp.maximum(i * hb - 1, 0), 1)),
                  _resident((SUBLANES, W)), _resident((1, W)), _resident((W, W)), _resident((1, W)),
                  _resident((W, W)), _resident((1, W)), _resident((1, W))],
        out_specs=[pl.BlockSpec((tt, W), lambda i: (i, 0)), pl.BlockSpec((tt, W), lambda i: (i, 0))],
        out_shape=[jax.ShapeDtypeStruct((T, W), BF16), jax.ShapeDtypeStruct((T, W), F32)],
        scratch_shapes=[pltpu.VMEM((tt, W), F32), pltpu.VMEM((tt, W), F32), pltpu.VMEM((SUBLANES, W), F32)],
        compiler_params=_cp(1),
    )(z, z, cw, cb, wa, ba, wx, bx, lam)


def _lru_bwd(z, hseq, dmix, cw, cb, wa, wat, ba, wx, wxt, bx, lam, name="lru_bwd", tt=256):
    T = z.shape[0]
    W = LRU_W
    nt = T // tt
    hb = tt // BF16_ROWS
    sb = tt // SUBLANES
    ng = tt // SUBLANES

    def body(z_ref, halo_ref, h_ref, hprev_ref, dm_ref, cw_ref, cb_ref, wa_ref, wat_ref, ba_ref, wx_ref, wxt_ref, bx_ref,
             lam_ref, dz_ref, dc_ref, dwa_ref, dwx_ref, dv_ref, c_s, d_s, g_s, gc, an, dxn):
        i = pl.program_id(0)
        ti = nt - 1 - i

        @pl.when(i == 0)
        def _():
            dc_ref[...] = jnp.zeros_like(dc_ref)
            dwa_ref[...] = jnp.zeros_like(dwa_ref)
            dwx_ref[...] = jnp.zeros_like(dwx_ref)
            dv_ref[...] = jnp.zeros_like(dv_ref)
            gc[...] = jnp.zeros_like(gc)
            an[...] = jnp.zeros_like(an)
            dxn[...] = jnp.zeros_like(dxn)

        xr = z_ref[:, W:2 * W].astype(F32)
        yg = z_ref[:, 0:W].astype(F32)
        halo = jnp.where(ti == 0, 0.0, halo_ref[...].astype(F32))
        xc, x1, x2, x3 = _lru_conv(xr, halo, cw_ref, cb_ref)
        sp = _softplus_neg(lam_ref[...])
        r, gi, a, mult = _lru_gates(xc, wa_ref, ba_ref, wx_ref, bx_ref, sp)
        h = h_ref[...]
        hp = jnp.where(ti == 0, 0.0, hprev_ref[...])
        hm1 = _shift_rows(jnp.concatenate([hp, h], axis=0), 1)[SUBLANES:]
        dout = dm_ref[...].astype(F32)
        d_s[...] = dout * _gelu(yg)
        dz_ref[:, 0:W] = (dout * h * _gelu_grad(yg)).astype(BF16)
        c_s[...] = _shift_rows(jnp.concatenate([a, an[...]], axis=0), tt + SUBLANES - 1)[:tt]
        an[...] = a[0:SUBLANES, :]
        row = lax.broadcasted_iota(jnp.int32, (SUBLANES, W), 0)

        def step(j, gnext):
            r0 = pl.multiple_of((ng - 1 - j) * SUBLANES, SUBLANES)
            C = c_s[pl.ds(r0, SUBLANES), :]
            G = d_s[pl.ds(r0, SUBLANES), :]
            for k in (1, 2, 4):
                m = row < SUBLANES - k
                G = jnp.where(m, G + C * pltpu.roll(G, SUBLANES - k, 0), G)
                C = jnp.where(m, C * pltpu.roll(C, SUBLANES - k, 0), C)
            G = G + C * gnext
            g_s[pl.ds(r0, SUBLANES), :] = G
            return jnp.broadcast_to(G[0:1, :], (SUBLANES, W))

        gc[...] = lax.fori_loop(0, ng, step, gc[...])
        du = g_s[...]
        da = du * hm1
        dgi = du * mult * xc
        dxc = du * mult * gi
        dmult = du * gi * xc
        dlog_a = da * a - dmult * (a * a) / mult
        dr = dlog_a * (-LRU_C * sp)
        dv_ref[2] += _rsum8(dlog_a * (-LRU_C * r))
        dpr = (dr * r * (1.0 - r)).astype(BF16)
        dpi = (dgi * gi * (1.0 - gi)).astype(BF16)
        dv_ref[0] += _rsum8(dpr.astype(F32))
        dv_ref[1] += _rsum8(dpi.astype(F32))
        xcb = xc.astype(BF16)
        dwa_ref[...] += lax.dot_general(xcb, dpr, TN, preferred_element_type=F32)
        dwx_ref[...] += lax.dot_general(xcb, dpi, TN, preferred_element_type=F32)
        dxc = dxc + jnp.dot(dpr, wat_ref[...], preferred_element_type=F32) + jnp.dot(dpi, wxt_ref[...], preferred_element_type=F32)
        n = tt + BF16_ROWS
        de = jnp.concatenate([dxc, dxn[...]], axis=0)
        dxr = (cw_ref[3:4, :] * dxc + cw_ref[2:3, :] * _shift_rows(de, n - 1)[:tt] + cw_ref[1:2, :] * _shift_rows(de, n - 2)[:tt]
               + cw_ref[0:1, :] * _shift_rows(de, n - 3)[:tt])
        dxn[...] = dxc[0:BF16_ROWS, :]
        dz_ref[:, W:2 * W] = dxr.astype(BF16)
        dc_ref[0] += _rsum8(dxc * x3)
        dc_ref[1] += _rsum8(dxc * x2)
        dc_ref[2] += _rsum8(dxc * x1)
        dc_ref[3] += _rsum8(dxc * xr)
        dc_ref[4] += _rsum8(dxc)

    rev = lambda i: nt - 1 - i
    return pl.pallas_call(
        body, name=name, grid=(nt,),
        in_specs=[pl.BlockSpec((tt, 2 * W), lambda i: (rev(i), 0)),
                  pl.BlockSpec((BF16_ROWS, W), lambda i: (jnp.maximum(rev(i) * hb - 1, 0), 1)),
                  pl.BlockSpec((tt, W), lambda i: (rev(i), 0)),
                  pl.BlockSpec((SUBLANES, W), lambda i: (jnp.maximum(rev(i) * sb - 1, 0), 0)),
                  pl.BlockSpec((tt, W), lambda i: (rev(i), 0)),
                  _resident((SUBLANES, W)), _resident((1, W)), _resident((W, W)), _resident((W, W)), _resident((1, W)),
                  _resident((W, W)), _resident((W, W)), _resident((1, W)), _resident((1, W))],
        out_specs=[pl.BlockSpec((tt, 2 * W), lambda i: (rev(i), 0)),
                   pl.BlockSpec((5, SUBLANES, W), lambda i: (0, 0, 0)),
                   pl.BlockSpec((W, W), lambda i: (0, 0)), pl.BlockSpec((W, W), lambda i: (0, 0)),
                   pl.BlockSpec((3, SUBLANES, W), lambda i: (0, 0, 0))],
        out_shape=[jax.ShapeDtypeStruct((T, 2 * W), BF16), jax.ShapeDtypeStruct((5, SUBLANES, W), F32),
                   jax.ShapeDtypeStruct((W, W), F32), jax.ShapeDtypeStruct((W, W), F32),
                   jax.ShapeDtypeStruct((3, SUBLANES, W), F32)],
        scratch_shapes=[pltpu.VMEM((tt, W), F32), pltpu.VMEM((tt, W), F32), pltpu.VMEM((tt, W), F32),
                        pltpu.VMEM((SUBLANES, W), F32), pltpu.VMEM((SUBLANES, W), F32), pltpu.VMEM((BF16_ROWS, W), F32)],
        compiler_params=_cp(1),
    )(z, z, hseq, hseq, dmix, cw, cb, wa, wat, ba, wx, wxt, bx, lam)


def _split3(x):
    hi = x.astype(BF16)
    r1 = x - hi.astype(F32)
    mid = r1.astype(BF16)
    lo = (r1 - mid.astype(F32)).astype(BF16)
    return hi, mid, lo


def _tri_matmul(tri, x):
    hi, mid, lo = _split3(x)
    return (jnp.dot(tri, hi, preferred_element_type=F32) + jnp.dot(tri, mid, preferred_element_type=F32)
            + jnp.dot(tri, lo, preferred_element_type=F32))


def _hg_chunk(q, fl, lb):
    C = q.shape[0]
    ri = lax.broadcasted_iota(jnp.int32, (C, C), 0)
    ci = lax.broadcasted_iota(jnp.int32, (C, C), 1)
    causal = ri >= ci
    sig = _sigmoid(fl)
    f = lb + (1.0 - lb) * sig
    k = 1.0 - f
    sq = _sigmoid(q)
    qf = q * sq
    b = _tri_matmul(causal.astype(BF16), jnp.log(f))
    bm = b[C // 2 - 1:C // 2, :]
    bl = b[C - 1:C, :]
    qt = qf * jnp.exp(b - bm)
    kt = k * jnp.exp(bm - b)
    qin = qf * jnp.exp(b)
    kout = k * jnp.exp(bl - b)
    att = jnp.where(causal, _dot(qt, kt, NT), 0.0)
    return dict(sig=sig, f=f, k=k, sq=sq, qf=qf, b=b, bm=bm, bl=bl, qt=qt, kt=kt, qin=qin, kout=kout, att=att,
                causal=causal, anti=ri <= ci, decay=jnp.exp(bl))


def _hgrn_fwd(z, lb, gn, name="hgrn_fwd", tt=256):
    T = z.shape[0]
    C = HG_CHUNK
    nc = tt // C
    Dh = HG_D

    def body(q_ref, f_ref, v_ref, g_ref, lb_ref, gn_ref, o_ref, ss_ref, st):
        @pl.when(pl.program_id(1) == 0)
        def _():
            st[...] = jnp.zeros_like(st)

        S = st[...]
        for c in range(nc):
            rows = slice(c * C, (c + 1) * C)
            ck = _hg_chunk(q_ref[rows, :].astype(F32), f_ref[rows, :].astype(F32), lb_ref[...])
            v = v_ref[rows, :]
            g = g_ref[rows, :].astype(F32)
            ss_ref[0, c] = S
            o = _dot(ck["att"], v) + _dot(ck["qin"], S, NT)
            S = ck["decay"] * S + _dot(v, ck["kout"], TN)
            rn = lax.rsqrt(jnp.mean(o * o, axis=-1, keepdims=True) + EPS)
            o_ref[rows, :] = (o * rn * gn_ref[...] * (g * _sigmoid(g))).astype(BF16)
        st[...] = S

    col = lambda base: (lambda h, i: (i, base + h))
    return pl.pallas_call(
        body, name=name, grid=(HG_HEADS, T // tt),
        in_specs=[pl.BlockSpec((tt, Dh), col(8)), pl.BlockSpec((tt, Dh), col(12)), pl.BlockSpec((tt, Dh), col(16)),
                  pl.BlockSpec((tt, Dh), col(20)), pl.BlockSpec((1, Dh), lambda h, i: (0, h)),
                  pl.BlockSpec((1, Dh), lambda h, i: (0, 0))],
        out_specs=[pl.BlockSpec((tt, Dh), lambda h, i: (i, h)),
                   pl.BlockSpec((1, nc, Dh, Dh), lambda h, i: (h, i, 0, 0))],
        out_shape=[jax.ShapeDtypeStruct((T, HG_HEADS * Dh), BF16),
                   jax.ShapeDtypeStruct((HG_HEADS, T // C, Dh, Dh), F32)],
        scratch_shapes=[pltpu.VMEM((Dh, Dh), F32)],
        compiler_params=_cp(2),
    )(z, z, z, z, lb, gn)


def _hgrn_bwd(z, ss, dmix, lb, gn, name="hgrn_bwd", tt=256):
    T = z.shape[0]
    C = HG_CHUNK
    nc = tt // C
    nt = T // tt
    Dh = HG_D

    def body(q_ref, f_ref, v_ref, g_ref, ss_ref, dm_ref, lb_ref, gn_ref, dq_ref, df_ref, dv_ref, dg_ref, dlb_ref, dgn_ref, dst):
        @pl.when(pl.program_id(1) == 0)
        def _():
            dst[...] = jnp.zeros_like(dst)
            dlb_ref[...] = jnp.zeros_like(dlb_ref)
            dgn_ref[...] = jnp.zeros_like(dgn_ref)

        dS = dst[...]
        lbv = lb_ref[...]
        gnv = gn_ref[...]
        rowc = lax.broadcasted_iota(jnp.int32, (C, Dh), 0)
        for c in reversed(range(nc)):
            rows = slice(c * C, (c + 1) * C)
            q = q_ref[rows, :].astype(F32)
            ck = _hg_chunk(q, f_ref[rows, :].astype(F32), lbv)
            v = v_ref[rows, :]
            g = g_ref[rows, :].astype(F32)
            S = ss_ref[0, c]
            o = _dot(ck["att"], v) + _dot(ck["qin"], S, NT)
            rn = lax.rsqrt(jnp.mean(o * o, axis=-1, keepdims=True) + EPS)
            on = o * rn
            dout = dm_ref[rows, :].astype(F32)
            sg, dsg = _silu_and_grad(g)
            d_ong = dout * sg
            dgn_ref[...] += _rsum8(d_ong * on)
            dg_ref[rows, :] = (dout * on * gnv * dsg).astype(BF16)
            don = d_ong * gnv
            do = rn * (don - on * jnp.mean(don * on, axis=-1, keepdims=True))
            dv_ref[rows, :] = (_dot(ck["att"], do, TN) + _dot(ck["kout"], dS, NT)).astype(BF16)
            datt = jnp.where(ck["causal"], _dot(do, v, NT), 0.0)
            dqt = _dot(datt, ck["kt"])
            dkt = _dot(datt, ck["qt"], TN)
            dqin = _dot(do, S)
            dkout = _dot(v, dS)
            ddecay = jnp.sum(dS * S, axis=0, keepdims=True)
            dS = _dot(do, ck["qin"], TN) + ck["decay"] * dS
            b, bm, bl = ck["b"], ck["bm"], ck["bl"]
            dqf = dqt * jnp.exp(b - bm) + dqin * jnp.exp(b)
            dk = dkt * jnp.exp(bm - b) + dkout * jnp.exp(bl - b)
            kk = dkout * ck["kout"]
            db = dqt * ck["qt"] - dkt * ck["kt"] + dqin * ck["qin"] - kk
            dbl = jnp.sum(kk, axis=0, keepdims=True) + ddecay * ck["decay"]
            db = db + jnp.where(rowc == C - 1, dbl, 0.0)
            dlogf = _tri_matmul(ck["anti"].astype(BF16), db)
            dfv = dlogf / ck["f"] - dk
            sig = ck["sig"]
            df_ref[rows, :] = (dfv * (1.0 - lbv) * sig * (1.0 - sig)).astype(BF16)
            dlb_ref[...] += _rsum8(dfv * (1.0 - sig))
            sq = ck["sq"]
            dq_ref[rows, :] = (dqf * (sq * (1.0 + q * (1.0 - sq)))).astype(BF16)
        dst[...] = dS

    rev = lambda i: nt - 1 - i
    col = lambda base: (lambda h, i: (rev(i), base + h))
    out_tok = pl.BlockSpec((tt, Dh), lambda h, i: (rev(i), h))
    acc = pl.BlockSpec((SUBLANES, Dh), lambda h, i: (h, 0))
    tok_shape = jax.ShapeDtypeStruct((T, HG_HEADS * Dh), BF16)
    acc_shape = jax.ShapeDtypeStruct((HG_HEADS * SUBLANES, Dh), F32)
    return pl.pallas_call(
        body, name=name, grid=(HG_HEADS, nt),
        in_specs=[pl.BlockSpec((tt, Dh), col(8)), pl.BlockSpec((tt, Dh), col(12)), pl.BlockSpec((tt, Dh), col(16)),
                  pl.BlockSpec((tt, Dh), col(20)),
                  pl.BlockSpec((1, nc, Dh, Dh), lambda h, i: (h, rev(i), 0, 0)),
                  pl.BlockSpec((tt, Dh), lambda h, i: (rev(i), 4 + h)),
                  pl.BlockSpec((1, Dh), lambda h, i: (0, h)), pl.BlockSpec((1, Dh), lambda h, i: (0, 0))],
        out_specs=[out_tok, out_tok, out_tok, out_tok, acc, acc],
        out_shape=[tok_shape, tok_shape, tok_shape, tok_shape, acc_shape, acc_shape],
        scratch_shapes=[pltpu.VMEM((Dh, Dh), F32)],
        compiler_params=_cp(2),
    )(z, z, z, z, ss, dmix, lb, gn)


def _sgu_core(p, lg_ref, lb_ref, wsc_ref, bsb_ref):
    Wd = D_MODEL
    G = SGU_CHUNK
    zz = _gelu(p)
    u = zz[:, :Wd]
    v = zz[:, Wd:]
    vc = v - jnp.mean(v, axis=-1, keepdims=True)
    rstd = lax.rsqrt(jnp.mean(vc * vc, axis=-1, keepdims=True) + EPS)
    vhat = vc * rstd
    vn = vhat * lg_ref[...] + lb_ref[...]
    svs = []
    for gi in range(SGU_G):
        svs.append(jnp.dot(wsc_ref[gi], vn[:, gi * G:(gi + 1) * G].astype(BF16), preferred_element_type=F32) + bsb_ref[gi])
    return u, vhat, rstd, vn, jnp.concatenate(svs, axis=1)


def _sgu_fwd(p1, lg, lbias, wsc, bsb, name="sgu_fwd", tt=512):
    T = p1.shape[0]
    Wd = D_MODEL
    C = SGU_CHUNK

    def body(p_ref, lg_ref, lb_ref, wsc_ref, bsb_ref, s_ref):
        for c in range(tt // C):
            rows = slice(c * C, (c + 1) * C)
            u, _, _, _, sv = _sgu_core(p_ref[rows, :].astype(F32), lg_ref, lb_ref, wsc_ref, bsb_ref)
            s_ref[rows, :] = (u * sv).astype(BF16)

    return pl.pallas_call(
        body, name=name, grid=(T // tt,),
        in_specs=[pl.BlockSpec((tt, 2 * Wd), lambda i: (i, 0)), _resident((1, Wd)), _resident((1, Wd)),
                  _resident((SGU_G, C, C)), _resident((SGU_G, C, C))],
        out_specs=pl.BlockSpec((tt, Wd), lambda i: (i, 0)),
        out_shape=jax.ShapeDtypeStruct((T, Wd), BF16),
        compiler_params=_cp(1),
    )(p1, lg, lbias, wsc, bsb)


def _sgu_bwd(p1, ds, lg, lbias, wsc, wsct, bsb, name="sgu_bwd", tt=512):
    T = p1.shape[0]
    Wd = D_MODEL
    C = SGU_CHUNK

    def body(p_ref, ds_ref, lg_ref, lb_ref, wsc_ref, wsct_ref, bsb_ref, dp_ref, dws_ref, dbs_ref, dlg_ref, dlb_ref, dbin_ref):
        @pl.when(pl.program_id(0) == 0)
        def _():
            dws_ref[...] = jnp.zeros_like(dws_ref)
            dbs_ref[...] = jnp.zeros_like(dbs_ref)
            dlg_ref[...] = jnp.zeros_like(dlg_ref)
            dlb_ref[...] = jnp.zeros_like(dlb_ref)
            dbin_ref[...] = jnp.zeros_like(dbin_ref)

        for c in range(tt // C):
            rows = slice(c * C, (c + 1) * C)
            p = p_ref[rows, :].astype(F32)
            u, vhat, rstd, vn, sv = _sgu_core(p, lg_ref, lb_ref, wsc_ref, bsb_ref)
            dsc = ds_ref[rows, :].astype(F32)
            du = dsc * sv
            dsv = dsc * u
            dvns = []
            for gi in range(SGU_G):
                cs = slice(gi * C, (gi + 1) * C)
                dsv_g = dsv[:, cs]
                dvns.append(jnp.dot(wsct_ref[gi], dsv_g.astype(BF16), preferred_element_type=F32))
                dws_ref[gi] += _dot(dsv_g, vn[:, cs], NT)
                dbs_ref[gi] += dsv_g
            dvn = jnp.concatenate(dvns, axis=1)
            dlg_ref[...] += _rsum8(dvn * vhat)
            dlb_ref[...] += _rsum8(dvn)
            dvh = dvn * lg_ref[...]
            dv = rstd * (dvh - jnp.mean(dvh, axis=-1, keepdims=True) - vhat * jnp.mean(dvh * vhat, axis=-1, keepdims=True))
            dp = jnp.concatenate([du, dv], axis=1) * _gelu_grad(p)
            dbin_ref[...] += _rsum8(dp)
            dp_ref[rows, :] = dp.astype(BF16)

    full3 = pl.BlockSpec((SGU_G, C, C), lambda i: (0, 0, 0))
    return pl.pallas_call(
        body, name=name, grid=(T // tt,),
        in_specs=[pl.BlockSpec((tt, 2 * Wd), lambda i: (i, 0)), pl.BlockSpec((tt, Wd), lambda i: (i, 0)),
                  _resident((1, Wd)), _resident((1, Wd)), _resident((SGU_G, C, C)), _resident((SGU_G, C, C)),
                  _resident((SGU_G, C, C))],
        out_specs=[pl.BlockSpec((tt, 2 * Wd), lambda i: (i, 0)), full3, full3,
                   pl.BlockSpec((SUBLANES, Wd), lambda i: (0, 0)), pl.BlockSpec((SUBLANES, Wd), lambda i: (0, 0)),
                   pl.BlockSpec((SUBLANES, 2 * Wd), lambda i: (0, 0))],
        out_shape=[jax.ShapeDtypeStruct((T, 2 * Wd), BF16), jax.ShapeDtypeStruct((SGU_G, C, C), F32),
                   jax.ShapeDtypeStruct((SGU_G, C, C), F32), jax.ShapeDtypeStruct((SUBLANES, Wd), F32),
                   jax.ShapeDtypeStruct((SUBLANES, Wd), F32), jax.ShapeDtypeStruct((SUBLANES, 2 * Wd), F32)],
        compiler_params=_cp(1),
    )(p1, ds, lg, lbias, wsc, wsct, bsb)


def _pad_rows(w, rows=SUBLANES):
    return jnp.pad(w, ((0, rows - w.shape[0]), (0, 0)))


def _block_diag(w):
    n, b, _ = w.shape
    return (w[:, :, None, :] * jnp.eye(n, dtype=w.dtype)[:, None, :, None]).reshape(n * b, n * b)


def _diag_blocks(m, n):
    b = m.shape[0] // n
    m4 = m.reshape(n, b, n, b)
    return jnp.stack([m4[k, :, k, :] for k in range(n)], axis=0)


def _piece_major(dw):
    if dw.ndim == 2:
        K, N = dw.shape
        return dw.reshape(N_CHIPS, 2, K // (2 * N_CHIPS), N)
    _, K, ns = dw.shape
    return dw.reshape(N_CHIPS, 2, K // 2, ns)


def _ffn_fwd(h, g, w_up, cw, cb, w_down, tag):
    hn, gu = _norm_mm(h, g, w_up, jnp.zeros((1, w_up.shape[1]), F32), name=f"ffn_up_{tag}")
    a = _ffn_act(gu, cw, cb, name=f"ffn_act_{tag}")
    out = _mm(a, w_down, h, F32, name=f"ffn_down_{tag}")
    return out, (hn, gu, a)


def _ffn_bwd(dh, h, g, saved, w_up, cw, cb, w_down, tag):
    hn, gu, a = saved
    da = _mm(dh, w_down, None, BF16, name=f"ffn_da_{tag}", transpose_w=True)
    dwd = _mm_tn(a, dh, name=f"ffn_dwd_{tag}", tk=D_FF // 2)
    dgu, dc = _ffn_act_bwd(gu, da, cw, cb, name=f"ffn_actb_{tag}")
    dhin, dg8 = _mm_normbwd(dgu, w_up, h, g, dh, name=f"ffn_dh_{tag}")
    dwu = _mm_tn(hn, dgu, name=f"ffn_dwu_{tag}", tk=256, col_shards=N_CHIPS)
    dcs = dc.sum(axis=1)
    return dhin, dg8.sum(axis=0), dwu, dcs[0:3], dcs[3], dwd


def _local_step(x, tgt, p):
    row = lambda v: v.reshape(1, -1)
    grads = {}

    lower = jax.nn.softmax(p["hg_lb_logits"], axis=0)
    lb0 = row(lower[0])
    ev_cw = _pad_rows(p["ev_conv_w"][0])
    ev_cb = row(p["ev_conv_b"][0])
    wa = _block_diag(p["ev_gate_a_w"][0]).astype(BF16)
    wx = _block_diag(p["ev_gate_x_w"][0]).astype(BF16)
    ba, bx, lam = row(p["ev_gate_a_b"][0]), row(p["ev_gate_x_b"][0]), row(p["ev_lru_lambda"][0])
    gn = row(p["ev_hg_norm"][0])
    tril = jnp.tril(jnp.ones((SGU_CHUNK, SGU_CHUNK), F32))
    wsc = (p["od_w_s"][0] * tril).astype(BF16)
    bsb = jnp.broadcast_to(p["od_b_s"][0][:, :, None], (SGU_G, SGU_CHUNK, SGU_CHUNK)).astype(F32)
    ffn_cw = [_pad_rows(p["ffn_conv_w"][l]) for l in range(2)]
    ffn_cb = [row(p["ffn_conv_b"][l]) for l in range(2)]
    ev_w_in, ev_w_out = p["ev_w_in"][0], p["ev_w_out"][0]
    od_w_in, od_w_out = p["od_w_in"][0], p["od_w_out"][0]
    w_up = [p["ffn_w_up"][l] for l in range(2)]
    w_down = [p["ffn_w_down"][l] for l in range(2)]
    nm = [row(p["norm_mix"][l]) for l in range(2)]
    nf = [row(p["norm_ffn"][l]) for l in range(2)]

    h0 = x
    hn0, z0 = _norm_mm(h0, nm[0], ev_w_in, jnp.zeros((1, ev_w_in.shape[1]), F32), name="ev_in")
    out_a, hseq = _lru_fwd(z0, ev_cw, ev_cb, wa, ba, wx, bx, lam)
    out_b, ss = _hgrn_fwd(z0, lb0, gn)
    mix0 = jnp.concatenate([out_a, out_b], axis=1)
    h1 = _mm(mix0, ev_w_out, h0, F32, name="ev_out")
    h2, ffn0 = _ffn_fwd(h1, nf[0], w_up[0], ffn_cw[0], ffn_cb[0], w_down[0], "l0")
    hn1, p1 = _norm_mm(h2, nm[1], od_w_in, row(p["od_b_in"][0]), name="od_in")
    s1 = _sgu_fwd(p1, row(p["od_ln_g"][0]), row(p["od_ln_b"][0]), wsc, bsb)
    h3 = _mm(s1, od_w_out, h2, F32, name="od_out")
    h4, ffn1 = _ffn_fwd(h3, nf[1], w_up[1], ffn_cw[1], ffn_cb[1], w_down[1], "l1")
    dh4, dgf8, sq8 = _final_loss(h4, row(p["norm_final"]), tgt)
    grads["norm_final"] = dgf8.sum(axis=0)

    big = {}
    dh3, dnf1, dwu1, dcw1, dcb1, dwd1 = _ffn_bwd(dh4, h3, nf[1], ffn1, w_up[1], ffn_cw[1], ffn_cb[1], w_down[1], "l1")
    big["ffn_w_up", 1], big["ffn_w_down", 1] = _piece_major(dwu1), _piece_major(dwd1)
    ds1 = _mm(dh3, od_w_out, None, BF16, name="od_ds", transpose_w=True)
    big["od_w_out", 0] = _piece_major(_mm_tn(s1, dh3, name="od_dwo", tk=512))
    wsct = jnp.swapaxes(wsc, 1, 2)
    dp1, dws, dbs, dlg8, dlb8, dbin8 = _sgu_bwd(p1, ds1, row(p["od_ln_g"][0]), row(p["od_ln_b"][0]), wsc, wsct, bsb)
    grads["od_w_s"] = (dws * tril)[None]
    grads["od_b_s"] = dbs.sum(axis=-1)[None]
    grads["od_ln_g"] = dlg8.sum(axis=0)[None]
    grads["od_ln_b"] = dlb8.sum(axis=0)[None]
    grads["od_b_in"] = dbin8.sum(axis=0)[None]
    dh2, dnm1 = _mm_normbwd(dp1, od_w_in, h2, nm[1], dh3, name="od_dh")
    big["od_w_in", 0] = _piece_major(_mm_tn(hn1, dp1, name="od_dwi", tk=512, col_shards=N_CHIPS))

    dh1, dnf0, dwu0, dcw0, dcb0, dwd0 = _ffn_bwd(dh2, h1, nf[0], ffn0, w_up[0], ffn_cw[0], ffn_cb[0], w_down[0], "l0")
    big["ffn_w_up", 0], big["ffn_w_down", 0] = _piece_major(dwu0), _piece_major(dwd0)
    dmix = _mm(dh1, ev_w_out, None, BF16, name="ev_dmix", transpose_w=True)
    big["ev_w_out", 0] = _piece_major(_mm_tn(mix0, dh1, name="ev_dwo", tk=512))
    dz01, dc5, dwa, dwx, dvec = _lru_bwd(z0, hseq, dmix, ev_cw, ev_cb, wa, wa.T, ba, wx, wx.T, bx, lam)
    dq, df, dv, dg, dlb32, dgn32 = _hgrn_bwd(z0, ss, dmix, lb0, gn)
    dz0 = jnp.concatenate([dz01, dq, df, dv, dg], axis=1)
    grad_x, dnm0 = _mm_normbwd(dz0, ev_w_in, h0, nm[0], dh1, name="ev_dh")
    big["ev_w_in", 0] = _piece_major(_mm_tn(hn0, dz0, name="ev_dwi", tk=512, col_shards=N_CHIPS))

    dc5s = dc5.sum(axis=1)
    grads["ev_conv_w"] = dc5s[0:4][None]
    grads["ev_conv_b"] = dc5s[4][None]
    grads["ev_gate_a_w"] = _diag_blocks(dwa, LRU_BLOCKS)[None]
    grads["ev_gate_x_w"] = _diag_blocks(dwx, LRU_BLOCKS)[None]
    dvs = dvec.sum(axis=1)
    grads["ev_gate_a_b"] = dvs[0][None]
    grads["ev_gate_x_b"] = dvs[1][None]
    grads["ev_lru_lambda"] = (dvs[2] * (-jax.nn.sigmoid(-p["ev_lru_lambda"][0])))[None]
    dlb = dlb32.reshape(HG_HEADS, SUBLANES, HG_D).sum(axis=1).reshape(-1)
    grads["hg_lb_logits"] = dlb[None, :] * lower[0][None, :] * (jnp.eye(3, dtype=F32)[0][:, None] - lower)
    grads["ev_hg_norm"] = dgn32.reshape(HG_HEADS, SUBLANES, HG_D).sum(axis=(0, 1))[None]
    grads["norm_mix"] = jnp.stack([dnm0.sum(axis=0), dnm1.sum(axis=0)])
    grads["norm_ffn"] = jnp.stack([dnf0, dnf1])
    grads["ffn_conv_w"] = jnp.stack([dcw0, dcw1])
    grads["ffn_conv_b"] = jnp.stack([dcb0, dcb1])
    return sq8, grad_x, grads, big


MESH = pl.DeviceIdType.MESH
ANY = pl.BlockSpec(memory_space=pl.ANY)
N_CHIPS = 4
N_DEV = 8

SH_BIG = {"ev_w_in": 2, "ev_w_out": 1, "od_w_in": 2, "od_w_out": 1, "ffn_w_up": 2, "ffn_w_down": 1}
SH_SMALL = {"ev_conv_w": 2, "od_b_in": 1, "od_ln_g": 1, "od_ln_b": 1, "ffn_conv_w": 2}
REP = ["norm_mix", "norm_ffn", "norm_final", "ev_conv_b", "ev_gate_a_w", "ev_gate_a_b", "ev_gate_x_w", "ev_gate_x_b",
       "ev_lru_lambda", "hg_lb_logits", "ev_hg_norm", "od_w_s", "od_b_s", "ffn_conv_b"]
WEIGHTS = ["norm_mix", "norm_ffn", "norm_final", "ev_w_in", "ev_conv_w", "ev_conv_b", "ev_gate_a_w", "ev_gate_a_b", "ev_gate_x_w",
           "ev_gate_x_b", "ev_lru_lambda", "hg_lb_logits", "ev_hg_norm", "ev_w_out", "od_w_in", "od_b_in", "od_ln_g", "od_ln_b",
           "od_w_s", "od_b_s", "od_w_out", "ffn_w_up", "ffn_conv_w", "ffn_conv_b", "ffn_w_down"]


def _rows(n_elems, mult=SUBLANES):
    r = -(-n_elems // LANES)
    return -(-r // mult) * mult


def _pack(arrs, rows, dtype):
    flat = jnp.concatenate([a.reshape(-1).astype(dtype) for a in arrs])
    return jnp.pad(flat, (0, rows * LANES - flat.shape[0])).reshape(rows, LANES)


def _unpack(flat2d, shapes):
    flat = flat2d.reshape(-1)
    out, off = [], 0
    for s in shapes:
        n = 1
        for d in s:
            n *= d
        out.append(flat[off:off + n].reshape(s))
        off += n
    return out


def _mesh_pos():
    return lax.axis_index("x"), lax.axis_index("y"), lax.axis_index("c")


def _other_chips(x, y):
    return [(1 - x, y), (x, 1 - y), (1 - x, 1 - y)]


def _half_rows(n):
    return lambda r, c: r.at[0, pl.ds(c * (n // 2), n // 2), :]


GATHER_BIG = {
    "ev_w_in": ((1024, 3072), _half_rows(1024), lambda o, k, c: o.at[pl.ds(c * 512, 512), pl.ds(k * 768, 768)]),
    "ev_w_out": ((1024, 1024), _half_rows(256), lambda o, k, c: o.at[pl.ds(k * 256 + c * 128, 128), :]),
    "od_w_in": ((1024, 2048), _half_rows(1024), lambda o, k, c: o.at[pl.ds(c * 512, 512), pl.ds(k * 512, 512)]),
    "od_w_out": ((1024, 1024), _half_rows(256), lambda o, k, c: o.at[pl.ds(k * 256 + c * 128, 128), :]),
    "ffn_w_up": ((2, 1024, 2 * D_FF), lambda r, c: r.at[c], lambda o, k, c: o.at[c, :, pl.ds(k * (2 * D_FF // 4), 2 * D_FF // 4)]),
    "ffn_w_down": ((2, D_FF, 1024), lambda r, c: r.at[c], lambda o, k, c: o.at[c, pl.ds(k * (D_FF // 4), D_FF // 4), :]),
}


def _gather_weights(big, small):
    nb = len(big)
    descs = list(GATHER_BIG.values())
    rs = small.shape[0]

    def body(*refs):
        ins, s_ref = refs[:nb], refs[nb]
        outs, os_ref = refs[nb + 1:2 * nb + 1], refs[2 * nb + 1]
        ici_send, ici_recv, d2d_send, d2d_recv, loc_sems = refs[2 * nb + 2:]
        x, y, c = _mesh_pos()
        k = 2 * x + y
        chips = _other_chips(x, y)
        sib = (x, y, 1 - c)

        def remote(src, dst, ssem, rsem, to):
            return pltpu.make_async_remote_copy(src_ref=src, dst_ref=dst, send_sem=ssem, recv_sem=rsem, device_id=to,
                                                device_id_type=MESH)

        local = []
        for t, (_, src, dst) in enumerate(descs):
            for cc in (0, 1):
                local.append(pltpu.make_async_copy(src(ins[t], cc), dst(outs[t], k, cc), loc_sems.at[2 * t + cc]))
        local.append(pltpu.make_async_copy(s_ref, os_ref.at[k], loc_sems.at[2 * nb]))
        for cp in local:
            cp.start()
        sends = []
        for t, (_, src, dst) in enumerate(descs):
            for j, (px, py) in enumerate(chips):
                sends.append(remote(src(ins[t], c), dst(outs[t], k, c), ici_send.at[3 * t + j], ici_recv.at[3 * t + j], (px, py, c)))
        for j, (px, py) in enumerate(chips):
            sends.append(remote(s_ref, os_ref.at[k], ici_send.at[3 * nb + j], ici_recv.at[3 * nb + j], (px, py, c)))
        for cp in sends:
            cp.start()
        for t, (_, src, dst) in enumerate(descs):
            for j, (px, py) in enumerate(chips):
                got = dst(outs[t], 2 * px + py, c)
                remote(got, got, ici_send.at[3 * t + j], ici_recv.at[3 * t + j], (px, py, c)).wait_recv()
                fwd = remote(got, got, d2d_send.at[3 * t + j], d2d_recv.at[3 * t + j], sib)
                fwd.start()
                sends.append(fwd)
        for j, (px, py) in enumerate(chips):
            remote(s_ref, os_ref.at[2 * px + py], ici_send.at[3 * nb + j], ici_recv.at[3 * nb + j], (px, py, c)).wait_recv()
        for t, (_, src, dst) in enumerate(descs):
            for j, (px, py) in enumerate(chips):
                theirs = dst(outs[t], 2 * px + py, 1 - c)
                remote(theirs, theirs, d2d_send.at[3 * t + j], d2d_recv.at[3 * t + j], sib).wait_recv()
        for cp in sends:
            cp.wait_send()
        for cp in local:
            cp.wait()

    out_shape = [jax.ShapeDtypeStruct(d[0], BF16) for d in descs] + [jax.ShapeDtypeStruct((N_CHIPS, rs, LANES), small.dtype)]
    return pl.pallas_call(
        body, name="gather_weights", in_specs=[ANY] * (nb + 1), out_specs=[ANY] * (nb + 1), out_shape=out_shape,
        scratch_shapes=[pltpu.SemaphoreType.DMA((3 * nb + 3,)), pltpu.SemaphoreType.DMA((3 * nb + 3,)),
                        pltpu.SemaphoreType.DMA((3 * nb,)), pltpu.SemaphoreType.DMA((3 * nb,)),
                        pltpu.SemaphoreType.DMA((2 * nb + 1,))],
    )(*big, small)


def _remote(src, dst, ssem, rsem, to):
    return pltpu.make_async_remote_copy(src_ref=src, dst_ref=dst, send_sem=ssem, recv_sem=rsem, device_id=to, device_id_type=MESH)


def _rs_send_sibling(gs, tag):
    n = len(gs)
    counts = [N_CHIPS if g.ndim == 4 else 1 for g in gs]
    ns = sum(counts)

    def body(*refs):
        ins, outs = refs[:n], refs[n:2 * n]
        send_sems, recv_sems = refs[2 * n:]
        x, y, c = _mesh_pos()
        cps, s = [], 0
        for t in range(n):
            if counts[t] == 1:
                cps.append(_remote(ins[t].at[1 - c], outs[t], send_sems.at[s], recv_sems.at[s], (x, y, 1 - c)))
                s += 1
            else:
                for k in range(N_CHIPS):
                    cps.append(_remote(ins[t].at[k, 1 - c], outs[t].at[k], send_sems.at[s], recv_sems.at[s], (x, y, 1 - c)))
                    s += 1
        for cp in cps:
            cp.start()
        for cp in cps:
            cp.wait()

    out_shape = [jax.ShapeDtypeStruct(g.shape[:1] + g.shape[2:] if g.ndim == 4 else g.shape[1:], g.dtype) for g in gs]
    return pl.pallas_call(
        body, name=f"rs_send_sibling_{tag}", in_specs=[ANY] * n, out_specs=[ANY] * n, out_shape=out_shape,
        scratch_shapes=[pltpu.SemaphoreType.DMA((ns,)), pltpu.SemaphoreType.DMA((ns,))],
    )(*gs)


def _add_piece(g, recv, c, name):
    P, Q = g.shape[-2:]

    def body(c_ref, g_ref, r_ref, o_ref):
        o_ref[...] = g_ref[...].reshape(o_ref.shape) + r_ref[...]

    if g.ndim == 4:
        grid = (N_CHIPS,)
        in_specs = [pl.BlockSpec((1, 1, P, Q), lambda k, c_ref: (k, c_ref[0], 0, 0)), pl.BlockSpec((1, P, Q), lambda k, c_ref: (k, 0, 0))]
        out_spec = pl.BlockSpec((1, P, Q), lambda k, c_ref: (k, 0, 0))
    else:
        grid = (1,)
        in_specs = [pl.BlockSpec((1, P, Q), lambda k, c_ref: (c_ref[0], 0, 0)), pl.BlockSpec((P, Q), lambda k, c_ref: (0, 0))]
        out_spec = pl.BlockSpec((P, Q), lambda k, c_ref: (0, 0))
    return pl.pallas_call(
        body, name=name,
        grid_spec=pltpu.PrefetchScalarGridSpec(num_scalar_prefetch=1, grid=grid, in_specs=in_specs, out_specs=out_spec),
        out_shape=jax.ShapeDtypeStruct(recv.shape, g.dtype),
        compiler_params=_cp(1),
    )(c, g, recv)


def _rs_send_chips(hs, tag):
    n = len(hs)

    def body(*refs):
        ins, outs = refs[:n], refs[n:2 * n]
        send_sems, recv_sems, loc_sems = refs[2 * n:]
        x, y, c = _mesh_pos()
        k = 2 * x + y
        chips = _other_chips(x, y)
        piece = lambda t, kk: ins[t].at[kk] if hs[t].ndim == 3 else ins[t]
        local = [pltpu.make_async_copy(piece(t, k), outs[t].at[k], loc_sems.at[t]) for t in range(n)]
        for cp in local:
            cp.start()
        sends = [_remote(piece(t, 2 * px + py), outs[t].at[k], send_sems.at[3 * t + j], recv_sems.at[3 * t + j], (px, py, c))
                 for t in range(n) for j, (px, py) in enumerate(chips)]
        for cp in sends:
            cp.start()
        for t in range(n):
            for j, (px, py) in enumerate(chips):
                _remote(piece(t, k), outs[t].at[2 * px + py], send_sems.at[3 * t + j], recv_sems.at[3 * t + j], (px, py, c)).wait_recv()
        for cp in sends:
            cp.wait_send()
        for cp in local:
            cp.wait()

    out_shape = [jax.ShapeDtypeStruct((N_CHIPS,) + h.shape[-2:], h.dtype) for h in hs]
    return pl.pallas_call(
        body, name=f"rs_send_chips_{tag}", in_specs=[ANY] * n, out_specs=[ANY] * n, out_shape=out_shape,
        scratch_shapes=[pltpu.SemaphoreType.DMA((3 * n,)), pltpu.SemaphoreType.DMA((3 * n,)), pltpu.SemaphoreType.DMA((n,))],
    )(*hs)


def _add_chips(p, name):
    _, P, Q = p.shape
    tr = P
    while N_CHIPS * tr * Q * 4 > 6 * 1024 * 1024 and tr % 16 == 0:
        tr //= 2

    def body(p_ref, o_ref):
        o_ref[...] = ((p_ref[0] + p_ref[1]) + p_ref[2]) + p_ref[3]

    return pl.pallas_call(
        body, name=name, grid=(P // tr,),
        in_specs=[pl.BlockSpec((N_CHIPS, tr, Q), lambda i: (0, i, 0))],
        out_specs=pl.BlockSpec((tr, Q), lambda i: (i, 0)),
        out_shape=jax.ShapeDtypeStruct((P, Q), p.dtype),
        compiler_params=_cp(1),
    )(p)


def _rs_share(fs, tag):
    n = len(fs)

    def body(*refs):
        ins, outs = refs[:n], refs[n:2 * n]
        send_sems, recv_sems, loc_sems = refs[2 * n:]
        x, y, c = _mesh_pos()
        local = [pltpu.make_async_copy(ins[t], outs[t].at[c], loc_sems.at[t]) for t in range(n)]
        for cp in local:
            cp.start()
        sends = [_remote(ins[t], outs[t].at[c], send_sems.at[t], recv_sems.at[t], (x, y, 1 - c)) for t in range(n)]
        for cp in sends:
            cp.start()
        for t in range(n):
            _remote(ins[t], outs[t].at[1 - c], send_sems.at[t], recv_sems.at[t], (x, y, 1 - c)).wait_recv()
        for cp in sends:
            cp.wait_send()
        for cp in local:
            cp.wait()

    out_shape = [jax.ShapeDtypeStruct((2,) + f.shape, f.dtype) for f in fs]
    return pl.pallas_call(
        body, name=f"rs_share_{tag}", in_specs=[ANY] * n, out_specs=[ANY] * n, out_shape=out_shape,
        scratch_shapes=[pltpu.SemaphoreType.DMA((n,)), pltpu.SemaphoreType.DMA((n,)), pltpu.SemaphoreType.DMA((n,))],
    )(*fs)


def _reduce(gs, c, tag):
    from_sibling = _rs_send_sibling(gs, tag)
    chip_sums = [_add_piece(g, r, c, name=f"add_piece_{tag}_{t}") for t, (g, r) in enumerate(zip(gs, from_sibling))]
    from_chips = _rs_send_chips(chip_sums, tag)
    mine = [_add_chips(p, name=f"add_chips_{tag}_{t}") for t, p in enumerate(from_chips)]
    return _rs_share(mine, tag)


def _adamw(w, g, m, v, name):
    R, C = w.shape
    tr = R
    for cand in (512, 256, 128, 64, 32, 16, 8):
        if R % cand == 0 and cand * C * 4 <= 2 * 1024 * 1024:
            tr = cand
            break
    c1 = 1.0 / (1.0 - ADAM_B1 ** ADAM_STEP)
    c2 = 1.0 / (1.0 - ADAM_B2 ** ADAM_STEP)

    def body(w_ref, g_ref, m_ref, v_ref, d_ref, mo_ref, vo_ref):
        gv = g_ref[...]
        mn = ADAM_B1 * m_ref[...] + (1.0 - ADAM_B1) * gv
        vn = ADAM_B2 * v_ref[...] + (1.0 - ADAM_B2) * (gv * gv)
        mo_ref[...] = mn
        vo_ref[...] = vn
        d_ref[...] = -ADAM_LR * ((mn * c1) / (jnp.sqrt(vn * c2) + ADAM_EPS) + ADAM_WD * w_ref[...])

    spec = pl.BlockSpec((tr, C), lambda i: (i, 0))
    shp = jax.ShapeDtypeStruct((R, C), F32)
    return pl.pallas_call(body, name=name, grid=(R // tr,), in_specs=[spec] * 4, out_specs=[spec] * 3, out_shape=[shp] * 3,
                          compiler_params=_cp(1))(w, g, m, v)


BIG_GRADS = [("ffn_w_up", 1), ("ffn_w_down", 1), ("od_w_out", 0), ("od_w_in", 0), ("ffn_w_up", 0), ("ffn_w_down", 0),
             ("ev_w_out", 0), ("ev_w_in", 0)]


def _step(a):
    x, y, c = _mesh_pos()
    ci = c.reshape(1).astype(jnp.int32)

    rs = _rows(sum(a[n].size for n in SH_SMALL))
    *full, gs = _gather_weights([a[n].astype(BF16) for n in GATHER_BIG], _pack([a[n] for n in SH_SMALL], rs, F32))
    p = {n: a[n] for n in REP}
    for n, w in zip(GATHER_BIG, full):
        p[n] = w if w.ndim == 3 else w[None]
    parts = [_unpack(gs[k], [a[n].shape for n in SH_SMALL]) for k in range(N_CHIPS)]
    for i, n in enumerate(SH_SMALL):
        p[n] = jnp.concatenate([parts[k][i] for k in range(N_CHIPS)], axis=SH_SMALL[n])

    sq8, grad_x, grads, big = _local_step(a["x"][0], a["loss_target"][0], p)
    loss = lax.psum(0.5 / D_MODEL * jnp.sum(sq8), ("x", "y", "c"))

    r_s = _rows(sum(a[n].size for n in SH_SMALL), 2 * SUBLANES) // 2
    small_pieces = []
    for k in range(N_CHIPS):
        pieces = [lax.slice_in_dim(grads[n], k * a[n].shape[ax], (k + 1) * a[n].shape[ax], axis=ax) for n, ax in SH_SMALL.items()]
        small_pieces.append(_pack(pieces, 2 * r_s, F32).reshape(2, r_s, LANES))
    g_small = jnp.stack(small_pieces)
    r_r = _rows(sum(a[n].size for n in REP), 2 * SUBLANES) // 2
    g_rep = _pack([grads[n] for n in REP], 2 * r_r, F32).reshape(2, r_r, LANES)
    reduced = _reduce([big[key] for key in BIG_GRADS] + [g_small, g_rep], ci, "all")
    red = dict(zip(BIG_GRADS, reduced))
    gfin = {}
    for n in ("ev_w_in", "ev_w_out", "od_w_in", "od_w_out"):
        gfin[n] = red[n, 0].reshape(a[n].shape)
    for n in ("ffn_w_up", "ffn_w_down"):
        gfin[n] = jnp.stack([red[n, l].reshape(a[n].shape[1:]) for l in range(2)])
    gfin.update(zip(SH_SMALL, _unpack(reduced[-2], [a[n].shape for n in SH_SMALL])))
    gfin.update(zip(REP, _unpack(reduced[-1], [a[n].shape for n in REP])))

    out = {"loss": loss, "grad_x": grad_x[None]}
    small_names = list(SH_SMALL) + REP
    for n in SH_BIG:
        shp = a[n].shape
        two_d = lambda t: t.reshape(-1, shp[-1])
        d, mo, vo = _adamw(two_d(a[n]), two_d(gfin[n]), two_d(a["m_" + n]), two_d(a["v_" + n]), name=f"adamw_{n}")
        out["delta_" + n], out["new_m_" + n], out["new_v_" + n] = d.reshape(shp), mo.reshape(shp), vo.reshape(shp)
    r_small = _rows(sum(a[n].size for n in small_names), 512)
    packs = [_pack([src(n) for n in small_names], r_small, F32)
             for src in (lambda n: a[n], lambda n: gfin[n], lambda n: a["m_" + n], lambda n: a["v_" + n])]
    d, mo, vo = _adamw(*packs, name="adamw_small")
    shapes = [a[n].shape for n in small_names]
    for n, dd, mm, vv in zip(small_names, _unpack(d, shapes), _unpack(mo, shapes), _unpack(vo, shapes)):
        out["delta_" + n], out["new_m_" + n], out["new_v_" + n] = dd, mm, vv
    for n in WEIGHTS:
        out["grad_" + n] = gfin[n]
    return out


def kernel(x, norm_mix, norm_ffn, norm_final, ev_w_in, ev_conv_w, ev_conv_b, ev_gate_a_w, ev_gate_a_b, ev_gate_x_w, ev_gate_x_b, ev_lru_lambda, hg_lb_logits, ev_hg_norm, ev_w_out, od_w_in, od_b_in, od_ln_g, od_ln_b, od_w_s, od_b_s, od_w_out, ffn_w_up, ffn_conv_w, ffn_conv_b, ffn_w_down, loss_target, m_norm_mix, m_norm_ffn, m_norm_final, m_ev_w_in, m_ev_conv_w, m_ev_conv_b, m_ev_gate_a_w, m_ev_gate_a_b, m_ev_gate_x_w, m_ev_gate_x_b, m_ev_lru_lambda, m_hg_lb_logits, m_ev_hg_norm, m_ev_w_out, m_od_w_in, m_od_b_in, m_od_ln_g, m_od_ln_b, m_od_w_s, m_od_b_s, m_od_w_out, m_ffn_w_up, m_ffn_conv_w, m_ffn_conv_b, m_ffn_w_down, v_norm_mix, v_norm_ffn, v_norm_final, v_ev_w_in, v_ev_conv_w, v_ev_conv_b, v_ev_gate_a_w, v_ev_gate_a_b, v_ev_gate_x_w, v_ev_gate_x_b, v_ev_lru_lambda, v_hg_lb_logits, v_ev_hg_norm, v_ev_w_out, v_od_w_in, v_od_b_in, v_od_ln_g, v_od_ln_b, v_od_w_s, v_od_b_s, v_od_w_out, v_ffn_w_up, v_ffn_conv_w, v_ffn_conv_b, v_ffn_w_down):
    vals = (x, norm_mix, norm_ffn, norm_final, ev_w_in, ev_conv_w, ev_conv_b, ev_gate_a_w, ev_gate_a_b, ev_gate_x_w, ev_gate_x_b, ev_lru_lambda, hg_lb_logits, ev_hg_norm, ev_w_out, od_w_in, od_b_in, od_ln_g, od_ln_b, od_w_s, od_b_s, od_w_out, ffn_w_up, ffn_conv_w, ffn_conv_b, ffn_w_down, loss_target, m_norm_mix, m_norm_ffn, m_norm_final, m_ev_w_in, m_ev_conv_w, m_ev_conv_b, m_ev_gate_a_w, m_ev_gate_a_b, m_ev_gate_x_w, m_ev_gate_x_b, m_ev_lru_lambda, m_hg_lb_logits, m_ev_hg_norm, m_ev_w_out, m_od_w_in, m_od_b_in, m_od_ln_g, m_od_ln_b, m_od_w_s, m_od_b_s, m_od_w_out, m_ffn_w_up, m_ffn_conv_w, m_ffn_conv_b, m_ffn_w_down, v_norm_mix, v_norm_ffn, v_norm_final, v_ev_w_in, v_ev_conv_w, v_ev_conv_b, v_ev_gate_a_w, v_ev_gate_a_b, v_ev_gate_x_w, v_ev_gate_x_b, v_ev_lru_lambda, v_hg_lb_logits, v_ev_hg_norm, v_ev_w_out, v_od_w_in, v_od_b_in, v_od_ln_g, v_od_ln_b, v_od_w_s, v_od_b_s, v_od_w_out, v_ffn_w_up, v_ffn_conv_w, v_ffn_conv_b, v_ffn_w_down)
    names = ["x"] + WEIGHTS + ["loss_target"] + ["m_" + n for n in WEIGHTS] + ["v_" + n for n in WEIGHTS]
    out = _step(dict(zip(names, vals)))
    return (out["loss"], out["grad_x"], *[out["grad_" + n] for n in WEIGHTS], *[out["delta_" + n] for n in WEIGHTS],
            *[out["new_m_" + n] for n in WEIGHTS], *[out["new_v_" + n] for n in WEIGHTS])
```

```python
import functools

import jax
import jax.numpy as jnp
from jax import lax
from jax.experimental import pallas as pl
from jax.experimental.pallas import tpu as pltpu

F32 = jnp.float32
BF16 = jnp.bfloat16

EPS = 1e-6
D_MODEL = 1024
LRU_W = 512
LRU_BLOCKS = 8
LRU_C = 8.0
HG_HEADS = 4
HG_D = 128
HG_CHUNK = 64
SGU_G = 8
SGU_CHUNK = 128
D_FF = 2816
ADAM_LR, ADAM_B1, ADAM_B2, ADAM_EPS, ADAM_WD, ADAM_STEP = 0.001, 0.9, 0.999, 1e-08, 0.01, 10

V7X_VMEM_BYTES = 64 * 1024 * 1024
VMEM_LIMIT = V7X_VMEM_BYTES - 8 * 1024 * 1024
SUBLANES = 8
LANES = 128
BF16_ROWS = 16

GELU_C0 = 0.7978845608028654
GELU_C1 = 0.044715

NN = (((1,), (0,)), ((), ()))
NT = (((1,), (1,)), ((), ()))
TN = (((0,), (0,)), ((), ()))


def _dot(a, b, dims=NN):
    return lax.dot_general(a.astype(BF16), b.astype(BF16), dims, preferred_element_type=F32)


def _cp(n_grid):
    return pltpu.CompilerParams(dimension_semantics=("arbitrary",) * n_grid, vmem_limit_bytes=VMEM_LIMIT)


def _chunk(n, cap):
    best = LANES
    for c in range(LANES, cap + 1, LANES):
        if n % c == 0:
            best = c
    return best


def _resident(shape):
    nd = len(shape)
    return pl.BlockSpec(shape, lambda *_: (0,) * nd, pipeline_mode=pl.Buffered(1))


def _rsum8(x):
    r, c = x.shape
    return x.reshape(r // SUBLANES, SUBLANES, c).sum(axis=0)


def _sigmoid(x):
    return 1.0 / (1.0 + jnp.exp(-x))


def _gelu(x):
    return 0.5 * x * (1.0 + jnp.tanh(GELU_C0 * (x + GELU_C1 * x * x * x)))


def _gelu_grad(x):
    t = jnp.tanh(GELU_C0 * (x + GELU_C1 * x * x * x))
    return 0.5 * (1.0 + t) + 0.5 * x * (1.0 - t * t) * GELU_C0 * (1.0 + 3.0 * GELU_C1 * x * x)


def _silu_and_grad(x):
    s = _sigmoid(x)
    return x * s, s * (1.0 + x * (1.0 - s))


def _shift_rows(e, j):
    n = e.shape[0]
    return e if j % n == 0 else pltpu.roll(e, j % n, 0)


def _norm_mm(h, g, w, b, name, tt=512):
    T, D = h.shape
    N = w.shape[1]
    cn = _chunk(N, 512)

    def body(h_ref, g_ref, w_ref, b_ref, hn_ref, z_ref):
        x = h_ref[...]
        r = lax.rsqrt(jnp.mean(x * x, axis=-1, keepdims=True) + EPS)
        hn = (x * r * g_ref[...]).astype(BF16)
        hn_ref[...] = hn
        for j in range(0, N, cn):
            acc = jnp.dot(hn, w_ref[:, j:j + cn], preferred_element_type=F32) + b_ref[:, j:j + cn]
            z_ref[:, j:j + cn] = acc.astype(BF16)

    return pl.pallas_call(
        body, name=name, grid=(T // tt,),
        in_specs=[pl.BlockSpec((tt, D), lambda i: (i, 0)), _resident((1, D)), _resident((D, N)), _resident((1, N))],
        out_specs=[pl.BlockSpec((tt, D), lambda i: (i, 0)), pl.BlockSpec((tt, N), lambda i: (i, 0))],
        out_shape=[jax.ShapeDtypeStruct((T, D), BF16), jax.ShapeDtypeStruct((T, N), BF16)],
        compiler_params=_cp(1),
    )(h, g, w, b)


def _mm(a, w, res, out_dtype, name, tt=512, transpose_w=False):
    T, K = a.shape
    N = w.shape[0] if transpose_w else w.shape[1]
    cn = _chunk(N, 512)
    has_res = res is not None

    def body(*refs):
        a_ref, w_ref = refs[0], refs[1]
        res_ref = refs[2] if has_res else None
        o_ref = refs[-1]
        av = a_ref[...].astype(BF16)
        for j in range(0, N, cn):
            if transpose_w:
                acc = lax.dot_general(av, w_ref[j:j + cn, :], NT, preferred_element_type=F32)
            else:
                acc = jnp.dot(av, w_ref[:, j:j + cn], preferred_element_type=F32)
            if has_res:
                acc = acc + res_ref[:, j:j + cn]
            o_ref[:, j:j + cn] = acc.astype(out_dtype)

    in_specs = [pl.BlockSpec((tt, K), lambda i: (i, 0)), _resident(w.shape)]
    args = [a, w]
    if has_res:
        in_specs.append(pl.BlockSpec((tt, N), lambda i: (i, 0)))
        args.append(res)
    return pl.pallas_call(
        body, name=name, grid=(T // tt,), in_specs=in_specs,
        out_specs=pl.BlockSpec((tt, N), lambda i: (i, 0)),
        out_shape=jax.ShapeDtypeStruct((T, N), out_dtype),
        compiler_params=_cp(1),
    )(*args)


def _mm_tn(a, b, name, tk, tt=512, col_shards=1):
    T, K = a.shape
    N = b.shape[1]
    ns = N // col_shards

    def body(a_ref, b_ref, o_ref):
        acc = lax.dot_general(a_ref[...].astype(BF16), b_ref[...].astype(BF16), TN, preferred_element_type=F32)
        first = pl.program_id(1) == 0
        if col_shards == 1:
            prev = jnp.where(first, 0.0, o_ref[...])
            o_ref[...] = prev + acc
        else:
            for s in range(col_shards):
                prev = jnp.where(first, 0.0, o_ref[s])
                o_ref[s] = prev + acc[:, s * ns:(s + 1) * ns]

    if col_shards == 1:
        out_spec = pl.BlockSpec((tk, N), lambda k, t: (k, 0))
        out_shape = jax.ShapeDtypeStruct((K, N), F32)
    else:
        out_spec = pl.BlockSpec((col_shards, tk, ns), lambda k, t: (0, k, 0))
        out_shape = jax.ShapeDtypeStruct((col_shards, K, ns), F32)
    return pl.pallas_call(
        body, name=name, grid=(K // tk, T // tt),
        in_specs=[pl.BlockSpec((tt, tk), lambda k, t: (t, k)), pl.BlockSpec((tt, N), lambda k, t: (t, 0))],
        out_specs=out_spec, out_shape=out_shape,
        compiler_params=_cp(2),
    )(a, b)


def _mm_normbwd(dz, w, x, g, dres, name, tt=512):
    T, N = dz.shape
    D = w.shape[0]

    def body(dz_ref, wt_ref, x_ref, g_ref, dres_ref, dx_ref, dg_ref):
        @pl.when(pl.program_id(0) == 0)
        def _():
            dg_ref[...] = jnp.zeros_like(dg_ref)

        dy = lax.dot_general(dz_ref[...], wt_ref[...], NT, preferred_element_type=F32)
        x = x_ref[...]
        r = lax.rsqrt(jnp.mean(x * x, axis=-1, keepdims=True) + EPS)
        xn = x * r
        dg_ref[...] += _rsum8(dy * xn)
        dxn = dy * g_ref[...]
        dx_ref[...] = dres_ref[...] + r * (dxn - xn * jnp.mean(dxn * xn, axis=-1, keepdims=True))

    return pl.pallas_call(
        body, name=name, grid=(T // tt,),
        in_specs=[pl.BlockSpec((tt, N), lambda i: (i, 0)), _resident((D, N)), pl.BlockSpec((tt, D), lambda i: (i, 0)),
                  _resident((1, D)), pl.BlockSpec((tt, D), lambda i: (i, 0))],
        out_specs=[pl.BlockSpec((tt, D), lambda i: (i, 0)), pl.BlockSpec((SUBLANES, D), lambda i: (0, 0))],
        out_shape=[jax.ShapeDtypeStruct((T, D), F32), jax.ShapeDtypeStruct((SUBLANES, D), F32)],
        compiler_params=_cp(1),
    )(dz, w, x, g, dres)


def _final_loss(h, g, tgt, name="final_loss", tt=512):
    T, D = h.shape

    def body(h_ref, g_ref, t_ref, dh_ref, dg_ref, sq_ref):
        @pl.when(pl.program_id(0) == 0)
        def _():
            dg_ref[...] = jnp.zeros_like(dg_ref)
            sq_ref[...] = jnp.zeros_like(sq_ref)

        x = h_ref[...]
        r = lax.rsqrt(jnp.mean(x * x, axis=-1, keepdims=True) + EPS)
        xn = x * r
        gv = g_ref[...]
        diff = xn * gv - t_ref[...]
        sq_ref[...] += _rsum8(diff * diff)
        dy = diff * (1.0 / D)
        dg_ref[...] += _rsum8(dy * xn)
        dxn = dy * gv
        dh_ref[...] = r * (dxn - xn * jnp.mean(dxn * xn, axis=-1, keepdims=True))

    return pl.pallas_call(
        body, name=name, grid=(T // tt,),
        in_specs=[pl.BlockSpec((tt, D), lambda i: (i, 0)), _resident((1, D)), pl.BlockSpec((tt, D), lambda i: (i, 0))],
        out_specs=[pl.BlockSpec((tt, D), lambda i: (i, 0)), pl.BlockSpec((SUBLANES, D), lambda i: (0, 0)),
                   pl.BlockSpec((SUBLANES, D), lambda i: (0, 0))],
        out_shape=[jax.ShapeDtypeStruct((T, D), F32), jax.ShapeDtypeStruct((SUBLANES, D), F32),
                   jax.ShapeDtypeStruct((SUBLANES, D), F32)],
        compiler_params=_cp(1),
    )(h, g, tgt)


def _ffn_act(gu, cw, cb, name, tt=512):
    T = gu.shape[0]
    F = gu.shape[1] // 2
    cc = _chunk(F, 256)
    hb = tt // BF16_ROWS

    def body(gu_ref, halo_ref, cw_ref, cb_ref, a_ref):
        first = pl.program_id(0) == 0
        for c0 in range(0, F, cc):
            cs = slice(c0, c0 + cc)
            x = gu_ref[:, cs].astype(F32)
            halo = jnp.where(first, 0.0, halo_ref[:, cs].astype(F32))
            e = jnp.concatenate([halo, x], axis=0)
            gc = (cb_ref[:, cs] + cw_ref[0:1, cs] * _shift_rows(e, 2)[BF16_ROWS:] + cw_ref[1:2, cs] * _shift_rows(e, 1)[BF16_ROWS:]
                  + cw_ref[2:3, cs] * x)
            up = gu_ref[:, F + c0:F + c0 + cc].astype(F32)
            a_ref[:, cs] = (gc * _sigmoid(gc) * up).astype(BF16)

    return pl.pallas_call(
        body, name=name, grid=(T // tt,),
        in_specs=[pl.BlockSpec((tt, 2 * F), lambda i: (i, 0)),
                  pl.BlockSpec((BF16_ROWS, F), lambda i: (jnp.maximum(i * hb - 1, 0), 0)),
                  _resident((SUBLANES, F)), _resident((1, F))],
        out_specs=pl.BlockSpec((tt, F), lambda i: (i, 0)),
        out_shape=jax.ShapeDtypeStruct((T, F), BF16),
        compiler_params=_cp(1),
    )(gu, gu, cw, cb)


def _ffn_act_bwd(gu, da, cw, cb, name, tt=512):
    T = gu.shape[0]
    F = gu.shape[1] // 2
    cc = _chunk(F, 256)
    hb = tt // BF16_ROWS
    last_hb = T // BF16_ROWS - 1
    nt = T // tt

    def body(gu_ref, gprev_ref, gunext_ref, da_ref, danext_ref, cw_ref, cb_ref, dgu_ref, dc_ref):
        i = pl.program_id(0)

        @pl.when(i == 0)
        def _():
            dc_ref[...] = jnp.zeros_like(dc_ref)

        n = tt + BF16_ROWS
        for c0 in range(0, F, cc):
            cs = slice(c0, c0 + cc)
            us = slice(F + c0, F + c0 + cc)
            g = gu_ref[:, cs].astype(F32)
            gp = jnp.where(i == 0, 0.0, gprev_ref[:, cs].astype(F32))
            ge = jnp.concatenate([gp, g, gunext_ref[:, cs].astype(F32)], axis=0)
            g1 = _shift_rows(ge, 1)[BF16_ROWS:]
            g2 = _shift_rows(ge, 2)[BF16_ROWS:]
            gc = cb_ref[:, cs] + cw_ref[0:1, cs] * g2 + cw_ref[1:2, cs] * g1 + cw_ref[2:3, cs] * ge[BF16_ROWS:]
            up = jnp.concatenate([gu_ref[:, us].astype(F32), gunext_ref[:, us].astype(F32)], axis=0)
            dan = jnp.where(i == nt - 1, 0.0, danext_ref[:, cs].astype(F32))
            dae = jnp.concatenate([da_ref[:, cs].astype(F32), dan], axis=0)
            s, ds = _silu_and_grad(gc)
            dgc = dae * up * ds
            dgu_ref[:, us] = (dae * s)[:tt].astype(BF16)
            dgate = cw_ref[2:3, cs] * dgc + cw_ref[1:2, cs] * _shift_rows(dgc, n - 1) + cw_ref[0:1, cs] * _shift_rows(dgc, n - 2)
            dgu_ref[:, cs] = dgate[:tt].astype(BF16)
            dm = dgc[:tt]
            dc_ref[0, :, cs] += _rsum8(dm * g2[:tt])
            dc_ref[1, :, cs] += _rsum8(dm * g1[:tt])
            dc_ref[2, :, cs] += _rsum8(dm * g)
            dc_ref[3, :, cs] += _rsum8(dm)

    return pl.pallas_call(
        body, name=name, grid=(nt,),
        in_specs=[pl.BlockSpec((tt, 2 * F), lambda i: (i, 0)),
                  pl.BlockSpec((BF16_ROWS, F), lambda i: (jnp.maximum(i * hb - 1, 0), 0)),
                  pl.BlockSpec((BF16_ROWS, 2 * F), lambda i: (jnp.minimum((i + 1) * hb, last_hb), 0)),
                  pl.BlockSpec((tt, F), lambda i: (i, 0)),
                  pl.BlockSpec((BF16_ROWS, F), lambda i: (jnp.minimum((i + 1) * hb, last_hb), 0)),
                  _resident((SUBLANES, F)), _resident((1, F))],
        out_specs=[pl.BlockSpec((tt, 2 * F), lambda i: (i, 0)), pl.BlockSpec((4, SUBLANES, F), lambda i: (0, 0, 0))],
        out_shape=[jax.ShapeDtypeStruct((T, 2 * F), BF16), jax.ShapeDtypeStruct((4, SUBLANES, F), F32)],
        compiler_params=_cp(1),
    )(gu, gu, gu, da, da, cw, cb)


def _softplus_neg(lam):
    x = -lam
    y = jnp.exp(-jnp.abs(x))
    l1p = jnp.where(y < 0.01, y * (1.0 - y * (0.5 - y * (1.0 / 3.0))), jnp.log(1.0 + y))
    return jnp.maximum(x, 0.0) + l1p


def _lru_gates(xc, wa_ref, ba_ref, wx_ref, bx_ref, sp):
    xcb = xc.astype(BF16)
    r = _sigmoid(jnp.dot(xcb, wa_ref[...], preferred_element_type=F32) + ba_ref[...])
    gi = _sigmoid(jnp.dot(xcb, wx_ref[...], preferred_element_type=F32) + bx_ref[...])
    log_a = -LRU_C * r * sp
    a = jnp.exp(log_a)
    x2 = 2.0 * log_a
    series = -x2 * (1.0 + x2 * 0.5 * (1.0 + x2 * (1.0 / 3.0) * (1.0 + x2 * 0.25 * (1.0 + x2 * 0.2))))
    om = jnp.where(x2 > -0.125, series, 1.0 - a * a)
    return r, gi, a, jnp.sqrt(om)


def _lru_conv(xr, halo, cw_ref, cb_ref):
    e = jnp.concatenate([halo, xr], axis=0)
    x1 = _shift_rows(e, 1)[BF16_ROWS:]
    x2 = _shift_rows(e, 2)[BF16_ROWS:]
    x3 = _shift_rows(e, 3)[BF16_ROWS:]
    xc = cb_ref[...] + cw_ref[0:1, :] * x3 + cw_ref[1:2, :] * x2 + cw_ref[2:3, :] * x1 + cw_ref[3:4, :] * xr
    return xc, x1, x2, x3


def _lru_fwd(z, cw, cb, wa, ba, wx, bx, lam, name="lru_fwd", tt=256):
    T = z.shape[0]
    W = LRU_W
    hb = tt // BF16_ROWS
    ng = tt // SUBLANES

    def body(z_ref, halo_ref, cw_ref, cb_ref, wa_ref, ba_ref, wx_ref, bx_ref, lam_ref, oa_ref, h_ref, a_s, u_s, hc):
        i = pl.program_id(0)

        @pl.when(i == 0)
        def _():
            hc[...] = jnp.zeros_like(hc)

        xr = z_ref[:, W:2 * W].astype(F32)
        halo = jnp.where(i == 0, 0.0, halo_ref[...].astype(F32))
        xc, _, _, _ = _lru_conv(xr, halo, cw_ref, cb_ref)
        sp = _softplus_neg(lam_ref[...])
        r, gi, a, mult = _lru_gates(xc, wa_ref, ba_ref, wx_ref, bx_ref, sp)
        a_s[...] = a
        u_s[...] = mult * gi * xc
        row = lax.broadcasted_iota(jnp.int32, (SUBLANES, W), 0)

        def step(j, hprev):
            r0 = pl.multiple_of(j * SUBLANES, SUBLANES)
            A = a_s[pl.ds(r0, SUBLANES), :]
            U = u_s[pl.ds(r0, SUBLANES), :]
            for k in (1, 2, 4):
                m = row >= k
                U = jnp.where(m, A * pltpu.roll(U, k, 0) + U, U)
                A = jnp.where(m, A * pltpu.roll(A, k, 0), A)
            H = U + A * hprev
            h_ref[pl.ds(r0, SUBLANES), :] = H
            return jnp.broadcast_to(H[SUBLANES - 1:SUBLANES, :], (SUBLANES, W))

        hc[...] = lax.fori_loop(0, ng, step, hc[...])
        oa_ref[...] = (_gelu(z_ref[:, 0:W].astype(F32)) * h_ref[...]).astype(BF16)

    return pl.pallas_call(
        body, name=name, grid=(T // tt,),
        in_specs=[pl.BlockSpec((tt, 2 * W), lambda i: (i, 0)),
                  pl.BlockSpec((BF16_ROWS, W), lambda i: (jnp.maximum(i * hb - 1, 0), 1)),
                  _resident((SUBLANES, W)), _resident((1, W)), _resident((W, W)), _resident((1, W)),
                  _resident((W, W)), _resident((1, W)), _resident((1, W))],
        out_specs=[pl.BlockSpec((tt, W), lambda i: (i, 0)), pl.BlockSpec((tt, W), lambda i: (i, 0))],
        out_shape=[jax.ShapeDtypeStruct((T, W), BF16), jax.ShapeDtypeStruct((T, W), F32)],
        scratch_shapes=[pltpu.VMEM((tt, W), F32), pltpu.VMEM((tt, W), F32), pltpu.VMEM((SUBLANES, W), F32)],
        compiler_params=_cp(1),
    )(z, z, cw, cb, wa, ba, wx, bx, lam)


def _lru_bwd(z, hseq, dmix, cw, cb, wa, wat, ba, wx, wxt, bx, lam, name="lru_bwd", tt=256):
    T = z.shape[0]
    W = LRU_W
    nt = T // tt
    hb = tt // BF16_ROWS
    sb = tt // SUBLANES
    ng = tt // SUBLANES

    def body(z_ref, halo_ref, h_ref, hprev_ref, dm_ref, cw_ref, cb_ref, wa_ref, wat_ref, ba_ref, wx_ref, wxt_ref, bx_ref,
             lam_ref, dz_ref, dc_ref, dwa_ref, dwx_ref, dv_ref, c_s, d_s, g_s, gc, an, dxn):
        i = pl.program_id(0)
        ti = nt - 1 - i

        @pl.when(i == 0)
        def _():
            dc_ref[...] = jnp.zeros_like(dc_ref)
            dwa_ref[...] = jnp.zeros_like(dwa_ref)
            dwx_ref[...] = jnp.zeros_like(dwx_ref)
            dv_ref[...] = jnp.zeros_like(dv_ref)
            gc[...] = jnp.zeros_like(gc)
            an[...] = jnp.zeros_like(an)
            dxn[...] = jnp.zeros_like(dxn)

        xr = z_ref[:, W:2 * W].astype(F32)
        yg = z_ref[:, 0:W].astype(F32)
        halo = jnp.where(ti == 0, 0.0, halo_ref[...].astype(F32))
        xc, x1, x2, x3 = _lru_conv(xr, halo, cw_ref, cb_ref)
        sp = _softplus_neg(lam_ref[...])
        r, gi, a, mult = _lru_gates(xc, wa_ref, ba_ref, wx_ref, bx_ref, sp)
        h = h_ref[...]
        hp = jnp.where(ti == 0, 0.0, hprev_ref[...])
        hm1 = _shift_rows(jnp.concatenate([hp, h], axis=0), 1)[SUBLANES:]
        dout = dm_ref[...].astype(F32)
        d_s[...] = dout * _gelu(yg)
        dz_ref[:, 0:W] = (dout * h * _gelu_grad(yg)).astype(BF16)
        c_s[...] = _shift_rows(jnp.concatenate([a, an[...]], axis=0), tt + SUBLANES - 1)[:tt]
        an[...] = a[0:SUBLANES, :]
        row = lax.broadcasted_iota(jnp.int32, (SUBLANES, W), 0)

        def step(j, gnext):
            r0 = pl.multiple_of((ng - 1 - j) * SUBLANES, SUBLANES)
            C = c_s[pl.ds(r0, SUBLANES), :]
            G = d_s[pl.ds(r0, SUBLANES), :]
            for k in (1, 2, 4):
                m = row < SUBLANES - k
                G = jnp.where(m, G + C * pltpu.roll(G, SUBLANES - k, 0), G)
                C = jnp.where(m, C * pltpu.roll(C, SUBLANES - k, 0), C)
            G = G + C * gnext
            g_s[pl.ds(r0, SUBLANES), :] = G
            return jnp.broadcast_to(G[0:1, :], (SUBLANES, W))

        gc[...] = lax.fori_loop(0, ng, step, gc[...])
        du = g_s[...]
        da = du * hm1
        dgi = du * mult * xc
        dxc = du * mult * gi
        dmult = du * gi * xc
        dlog_a = da * a - dmult * (a * a) / mult
        dr = dlog_a * (-LRU_C * sp)
        dv_ref[2] += _rsum8(dlog_a * (-LRU_C * r))
        dpr = (dr * r * (1.0 - r)).astype(BF16)
        dpi = (dgi * gi * (1.0 - gi)).astype(BF16)
        dv_ref[0] += _rsum8(dpr.astype(F32))
        dv_ref[1] += _rsum8(dpi.astype(F32))
        xcb = xc.astype(BF16)
        dwa_ref[...] += lax.dot_general(xcb, dpr, TN, preferred_element_type=F32)
        dwx_ref[...] += lax.dot_general(xcb, dpi, TN, preferred_element_type=F32)
        dxc = dxc + jnp.dot(dpr, wat_ref[...], preferred_element_type=F32) + jnp.dot(dpi, wxt_ref[...], preferred_element_type=F32)
        n = tt + BF16_ROWS
        de = jnp.concatenate([dxc, dxn[...]], axis=0)
        dxr = (cw_ref[3:4, :] * dxc + cw_ref[2:3, :] * _shift_rows(de, n - 1)[:tt] + cw_ref[1:2, :] * _shift_rows(de, n - 2)[:tt]
               + cw_ref[0:1, :] * _shift_rows(de, n - 3)[:tt])
        dxn[...] = dxc[0:BF16_ROWS, :]
        dz_ref[:, W:2 * W] = dxr.astype(BF16)
        dc_ref[0] += _rsum8(dxc * x3)
        dc_ref[1] += _rsum8(dxc * x2)
        dc_ref[2] += _rsum8(dxc * x1)
        dc_ref[3] += _rsum8(dxc * xr)
        dc_ref[4] += _rsum8(dxc)

    rev = lambda i: nt - 1 - i
    return pl.pallas_call(
        body, name=name, grid=(nt,),
        in_specs=[pl.BlockSpec((tt, 2 * W), lambda i: (rev(i), 0)),
                  pl.BlockSpec((BF16_ROWS, W), lambda i: (jnp.maximum(rev(i) * hb - 1, 0), 1)),
                  pl.BlockSpec((tt, W), lambda i: (rev(i), 0)),
                  pl.BlockSpec((SUBLANES, W), lambda i: (jnp.maximum(rev(i) * sb - 1, 0), 0)),
                  pl.BlockSpec((tt, W), lambda i: (rev(i), 0)),
                  _resident((SUBLANES, W)), _resident((1, W)), _resident((W, W)), _resident((W, W)), _resident((1, W)),
                  _resident((W, W)), _resident((W, W)), _resident((1, W)), _resident((1, W))],
        out_specs=[pl.BlockSpec((tt, 2 * W), lambda i: (rev(i), 0)),
                   pl.BlockSpec((5, SUBLANES, W), lambda i: (0, 0, 0)),
                   pl.BlockSpec((W, W), lambda i: (0, 0)), pl.BlockSpec((W, W), lambda i: (0, 0)),
                   pl.BlockSpec((3, SUBLANES, W), lambda i: (0, 0, 0))],
        out_shape=[jax.ShapeDtypeStruct((T, 2 * W), BF16), jax.ShapeDtypeStruct((5, SUBLANES, W), F32),
                   jax.ShapeDtypeStruct((W, W), F32), jax.ShapeDtypeStruct((W, W), F32),
                   jax.ShapeDtypeStruct((3, SUBLANES, W), F32)],
        scratch_shapes=[pltpu.VMEM((tt, W), F32), pltpu.VMEM((tt, W), F32), pltpu.VMEM((tt, W), F32),
                        pltpu.VMEM((SUBLANES, W), F32), pltpu.VMEM((SUBLANES, W), F32), pltpu.VMEM((BF16_ROWS, W), F32)],
        compiler_params=_cp(1),
    )(z, z, hseq, hseq, dmix, cw, cb, wa, wat, ba, wx, wxt, bx, lam)


def _split3(x):
    hi = x.astype(BF16)
    r1 = x - hi.astype(F32)
    mid = r1.astype(BF16)
    lo = (r1 - mid.astype(F32)).astype(BF16)
    return hi, mid, lo


def _tri_matmul(tri, x):
    hi, mid, lo = _split3(x)
    return (jnp.dot(tri, hi, preferred_element_type=F32) + jnp.dot(tri, mid, preferred_element_type=F32)
            + jnp.dot(tri, lo, preferred_element_type=F32))


def _hg_chunk(q, fl, lb):
    C = q.shape[0]
    ri = lax.broadcasted_iota(jnp.int32, (C, C), 0)
    ci = lax.broadcasted_iota(jnp.int32, (C, C), 1)
    causal = ri >= ci
    sig = _sigmoid(fl)
    f = lb + (1.0 - lb) * sig
    k = 1.0 - f
    sq = _sigmoid(q)
    qf = q * sq
    b = _tri_matmul(causal.astype(BF16), jnp.log(f))
    bm = b[C // 2 - 1:C // 2, :]
    bl = b[C - 1:C, :]
    qt = qf * jnp.exp(b - bm)
    kt = k * jnp.exp(bm - b)
    qin = qf * jnp.exp(b)
    kout = k * jnp.exp(bl - b)
    att = jnp.where(causal, _dot(qt, kt, NT), 0.0)
    return dict(sig=sig, f=f, k=k, sq=sq, qf=qf, b=b, bm=bm, bl=bl, qt=qt, kt=kt, qin=qin, kout=kout, att=att,
                causal=causal, anti=ri <= ci, decay=jnp.exp(bl))


def _hgrn_fwd(z, lb, gn, name="hgrn_fwd", tt=256):
    T = z.shape[0]
    C = HG_CHUNK
    nc = tt // C
    Dh = HG_D

    def body(q_ref, f_ref, v_ref, g_ref, lb_ref, gn_ref, o_ref, ss_ref, st):
        @pl.when(pl.program_id(1) == 0)
        def _():
            st[...] = jnp.zeros_like(st)

        S = st[...]
        for c in range(nc):
            rows = slice(c * C, (c + 1) * C)
            ck = _hg_chunk(q_ref[rows, :].astype(F32), f_ref[rows, :].astype(F32), lb_ref[...])
            v = v_ref[rows, :]
            g = g_ref[rows, :].astype(F32)
            ss_ref[0, c] = S
            o = _dot(ck["att"], v) + _dot(ck["qin"], S, NT)
            S = ck["decay"] * S + _dot(v, ck["kout"], TN)
            rn = lax.rsqrt(jnp.mean(o * o, axis=-1, keepdims=True) + EPS)
            o_ref[rows, :] = (o * rn * gn_ref[...] * (g * _sigmoid(g))).astype(BF16)
        st[...] = S

    col = lambda base: (lambda h, i: (i, base + h))
    return pl.pallas_call(
        body, name=name, grid=(HG_HEADS, T // tt),
        in_specs=[pl.BlockSpec((tt, Dh), col(8)), pl.BlockSpec((tt, Dh), col(12)), pl.BlockSpec((tt, Dh), col(16)),
                  pl.BlockSpec((tt, Dh), col(20)), pl.BlockSpec((1, Dh), lambda h, i: (0, h)),
                  pl.BlockSpec((1, Dh), lambda h, i: (0, 0))],
        out_specs=[pl.BlockSpec((tt, Dh), lambda h, i: (i, h)),
                   pl.BlockSpec((1, nc, Dh, Dh), lambda h, i: (h, i, 0, 0))],
        out_shape=[jax.ShapeDtypeStruct((T, HG_HEADS * Dh), BF16),
                   jax.ShapeDtypeStruct((HG_HEADS, T // C, Dh, Dh), F32)],
        scratch_shapes=[pltpu.VMEM((Dh, Dh), F32)],
        compiler_params=_cp(2),
    )(z, z, z, z, lb, gn)


def _hgrn_bwd(z, ss, dmix, lb, gn, name="hgrn_bwd", tt=256):
    T = z.shape[0]
    C = HG_CHUNK
    nc = tt // C
    nt = T // tt
    Dh = HG_D

    def body(q_ref, f_ref, v_ref, g_ref, ss_ref, dm_ref, lb_ref, gn_ref, dq_ref, df_ref, dv_ref, dg_ref, dlb_ref, dgn_ref, dst):
        @pl.when(pl.program_id(1) == 0)
        def _():
            dst[...] = jnp.zeros_like(dst)
            dlb_ref[...] = jnp.zeros_like(dlb_ref)
            dgn_ref[...] = jnp.zeros_like(dgn_ref)

        dS = dst[...]
        lbv = lb_ref[...]
        gnv = gn_ref[...]
        rowc = lax.broadcasted_iota(jnp.int32, (C, Dh), 0)
        for c in reversed(range(nc)):
            rows = slice(c * C, (c + 1) * C)
            q = q_ref[rows, :].astype(F32)
            ck = _hg_chunk(q, f_ref[rows, :].astype(F32), lbv)
            v = v_ref[rows, :]
            g = g_ref[rows, :].astype(F32)
            S = ss_ref[0, c]
            o = _dot(ck["att"], v) + _dot(ck["qin"], S, NT)
            rn = lax.rsqrt(jnp.mean(o * o, axis=-1, keepdims=True) + EPS)
            on = o * rn
            dout = dm_ref[rows, :].astype(F32)
            sg, dsg = _silu_and_grad(g)
            d_ong = dout * sg
            dgn_ref[...] += _rsum8(d_ong * on)
            dg_ref[rows, :] = (dout * on * gnv * dsg).astype(BF16)
            don = d_ong * gnv
            do = rn * (don - on * jnp.mean(don * on, axis=-1, keepdims=True))
            dv_ref[rows, :] = (_dot(ck["att"], do, TN) + _dot(ck["kout"], dS, NT)).astype(BF16)
            datt = jnp.where(ck["causal"], _dot(do, v, NT), 0.0)
            dqt = _dot(datt, ck["kt"])
            dkt = _dot(datt, ck["qt"], TN)
            dqin = _dot(do, S)
            dkout = _dot(v, dS)
            ddecay = jnp.sum(dS * S, axis=0, keepdims=True)
            dS = _dot(do, ck["qin"], TN) + ck["decay"] * dS
            b, bm, bl = ck["b"], ck["bm"], ck["bl"]
            dqf = dqt * jnp.exp(b - bm) + dqin * jnp.exp(b)
            dk = dkt * jnp.exp(bm - b) + dkout * jnp.exp(bl - b)
            kk = dkout * ck["kout"]
            db = dqt * ck["qt"] - dkt * ck["kt"] + dqin * ck["qin"] - kk
            dbl = jnp.sum(kk, axis=0, keepdims=True) + ddecay * ck["decay"]
            db = db + jnp.where(rowc == C - 1, dbl, 0.0)
            dlogf = _tri_matmul(ck["anti"].astype(BF16), db)
            dfv = dlogf / ck["f"] - dk
            sig = ck["sig"]
            df_ref[rows, :] = (dfv * (1.0 - lbv) * sig * (1.0 - sig)).astype(BF16)
            dlb_ref[...] += _rsum8(dfv * (1.0 - sig))
            sq = ck["sq"]
            dq_ref[rows, :] = (dqf * (sq * (1.0 + q * (1.0 - sq)))).astype(BF16)
        dst[...] = dS

    rev = lambda i: nt - 1 - i
    col = lambda base: (lambda h, i: (rev(i), base + h))
    out_tok = pl.BlockSpec((tt, Dh), lambda h, i: (rev(i), h))
    acc = pl.BlockSpec((SUBLANES, Dh), lambda h, i: (h, 0))
    tok_shape = jax.ShapeDtypeStruct((T, HG_HEADS * Dh), BF16)
    acc_shape = jax.ShapeDtypeStruct((HG_HEADS * SUBLANES, Dh), F32)
    return pl.pallas_call(
        body, name=name, grid=(HG_HEADS, nt),
        in_specs=[pl.BlockSpec((tt, Dh), col(8)), pl.BlockSpec((tt, Dh), col(12)), pl.BlockSpec((tt, Dh), col(16)),
                  pl.BlockSpec((tt, Dh), col(20)),
                  pl.BlockSpec((1, nc, Dh, Dh), lambda h, i: (h, rev(i), 0, 0)),
                  pl.BlockSpec((tt, Dh), lambda h, i: (rev(i), 4 + h)),
                  pl.BlockSpec((1, Dh), lambda h, i: (0, h)), pl.BlockSpec((1, Dh), lambda h, i: (0, 0))],
        out_specs=[out_tok, out_tok, out_tok, out_tok, acc, acc],
        out_shape=[tok_shape, tok_shape, tok_shape, tok_shape, acc_shape, acc_shape],
        scratch_shapes=[pltpu.VMEM((Dh, Dh), F32)],
        compiler_params=_cp(2),
    )(z, z, z, z, ss, dmix, lb, gn)


def _sgu_core(p, lg_ref, lb_ref, wsc_ref, bsb_ref):
    Wd = D_MODEL
    G = SGU_CHUNK
    zz = _gelu(p)
    u = zz[:, :Wd]
    v = zz[:, Wd:]
    vc = v - jnp.mean(v, axis=-1, keepdims=True)
    rstd = lax.rsqrt(jnp.mean(vc * vc, axis=-1, keepdims=True) + EPS)
    vhat = vc * rstd
    vn = vhat * lg_ref[...] + lb_ref[...]
    svs = []
    for gi in range(SGU_G):
        svs.append(jnp.dot(wsc_ref[gi], vn[:, gi * G:(gi + 1) * G].astype(BF16), preferred_element_type=F32) + bsb_ref[gi])
    return u, vhat, rstd, vn, jnp.concatenate(svs, axis=1)


def _sgu_fwd(p1, lg, lbias, wsc, bsb, name="sgu_fwd", tt=512):
    T = p1.shape[0]
    Wd = D_MODEL
    C = SGU_CHUNK

    def body(p_ref, lg_ref, lb_ref, wsc_ref, bsb_ref, s_ref):
        for c in range(tt // C):
            rows = slice(c * C, (c + 1) * C)
            u, _, _, _, sv = _sgu_core(p_ref[rows, :].astype(F32), lg_ref, lb_ref, wsc_ref, bsb_ref)
            s_ref[rows, :] = (u * sv).astype(BF16)

    return pl.pallas_call(
        body, name=name, grid=(T // tt,),
        in_specs=[pl.BlockSpec((tt, 2 * Wd), lambda i: (i, 0)), _resident((1, Wd)), _resident((1, Wd)),
                  _resident((SGU_G, C, C)), _resident((SGU_G, C, C))],
        out_specs=pl.BlockSpec((tt, Wd), lambda i: (i, 0)),
        out_shape=jax.ShapeDtypeStruct((T, Wd), BF16),
        compiler_params=_cp(1),
    )(p1, lg, lbias, wsc, bsb)


def _sgu_bwd(p1, ds, lg, lbias, wsc, wsct, bsb, name="sgu_bwd", tt=512):
    T = p1.shape[0]
    Wd = D_MODEL
    C = SGU_CHUNK

    def body(p_ref, ds_ref, lg_ref, lb_ref, wsc_ref, wsct_ref, bsb_ref, dp_ref, dws_ref, dbs_ref, dlg_ref, dlb_ref, dbin_ref):
        @pl.when(pl.program_id(0) == 0)
        def _():
            dws_ref[...] = jnp.zeros_like(dws_ref)
            dbs_ref[...] = jnp.zeros_like(dbs_ref)
            dlg_ref[...] = jnp.zeros_like(dlg_ref)
            dlb_ref[...] = jnp.zeros_like(dlb_ref)
            dbin_ref[...] = jnp.zeros_like(dbin_ref)

        for c in range(tt // C):
            rows = slice(c * C, (c + 1) * C)
            p = p_ref[rows, :].astype(F32)
            u, vhat, rstd, vn, sv = _sgu_core(p, lg_ref, lb_ref, wsc_ref, bsb_ref)
            dsc = ds_ref[rows, :].astype(F32)
            du = dsc * sv
            dsv = dsc * u
            dvns = []
            for gi in range(SGU_G):
                cs = slice(gi * C, (gi + 1) * C)
                dsv_g = dsv[:, cs]
                dvns.append(jnp.dot(wsct_ref[gi], dsv_g.astype(BF16), preferred_element_type=F32))
                dws_ref[gi] += _dot(dsv_g, vn[:, cs], NT)
                dbs_ref[gi] += dsv_g
            dvn = jnp.concatenate(dvns, axis=1)
            dlg_ref[...] += _rsum8(dvn * vhat)
            dlb_ref[...] += _rsum8(dvn)
            dvh = dvn * lg_ref[...]
            dv = rstd * (dvh - jnp.mean(dvh, axis=-1, keepdims=True) - vhat * jnp.mean(dvh * vhat, axis=-1, keepdims=True))
            dp = jnp.concatenate([du, dv], axis=1) * _gelu_grad(p)
            dbin_ref[...] += _rsum8(dp)
            dp_ref[rows, :] = dp.astype(BF16)

    full3 = pl.BlockSpec((SGU_G, C, C), lambda i: (0, 0, 0))
    return pl.pallas_call(
        body, name=name, grid=(T // tt,),
        in_specs=[pl.BlockSpec((tt, 2 * Wd), lambda i: (i, 0)), pl.BlockSpec((tt, Wd), lambda i: (i, 0)),
                  _resident((1, Wd)), _resident((1, Wd)), _resident((SGU_G, C, C)), _resident((SGU_G, C, C)),
                  _resident((SGU_G, C, C))],
        out_specs=[pl.BlockSpec((tt, 2 * Wd), lambda i: (i, 0)), full3, full3,
                   pl.BlockSpec((SUBLANES, Wd), lambda i: (0, 0)), pl.BlockSpec((SUBLANES, Wd), lambda i: (0, 0)),
                   pl.BlockSpec((SUBLANES, 2 * Wd), lambda i: (0, 0))],
        out_shape=[jax.ShapeDtypeStruct((T, 2 * Wd), BF16), jax.ShapeDtypeStruct((SGU_G, C, C), F32),
                   jax.ShapeDtypeStruct((SGU_G, C, C), F32), jax.ShapeDtypeStruct((SUBLANES, Wd), F32),
                   jax.ShapeDtypeStruct((SUBLANES, Wd), F32), jax.ShapeDtypeStruct((SUBLANES, 2 * Wd), F32)],
        compiler_params=_cp(1),
    )(p1, ds, lg, lbias, wsc, wsct, bsb)


def _pad_rows(w, rows=SUBLANES):
    return jnp.pad(w, ((0, rows - w.shape[0]), (0, 0)))


def _block_diag(w):
    n, b, _ = w.shape
    return (w[:, :, None, :] * jnp.eye(n, dtype=w.dtype)[:, None, :, None]).reshape(n * b, n * b)


def _diag_blocks(m, n):
    b = m.shape[0] // n
    m4 = m.reshape(n, b, n, b)
    return jnp.stack([m4[k, :, k, :] for k in range(n)], axis=0)


def _piece_major(dw):
    if dw.ndim == 2:
        K, N = dw.shape
        return dw.reshape(N_CHIPS, 2, K // (2 * N_CHIPS), N)
    _, K, ns = dw.shape
    return dw.reshape(N_CHIPS, 2, K // 2, ns)


def _ffn_fwd(h, g, w_up, cw, cb, w_down, tag):
    hn, gu = _norm_mm(h, g, w_up, jnp.zeros((1, w_up.shape[1]), F32), name=f"ffn_up_{tag}")
    a = _ffn_act(gu, cw, cb, name=f"ffn_act_{tag}")
    out = _mm(a, w_down, h, F32, name=f"ffn_down_{tag}")
    return out, (hn, gu, a)


def _ffn_bwd(dh, h, g, saved, w_up, cw, cb, w_down, tag):
    hn, gu, a = saved
    da = _mm(dh, w_down, None, BF16, name=f"ffn_da_{tag}", transpose_w=True)
    dwd = _mm_tn(a, dh, name=f"ffn_dwd_{tag}", tk=D_FF // 2)
    dgu, dc = _ffn_act_bwd(gu, da, cw, cb, name=f"ffn_actb_{tag}")
    dhin, dg8 = _mm_normbwd(dgu, w_up, h, g, dh, name=f"ffn_dh_{tag}")
    dwu = _mm_tn(hn, dgu, name=f"ffn_dwu_{tag}", tk=256, col_shards=N_CHIPS)
    dcs = dc.sum(axis=1)
    return dhin, dg8.sum(axis=0), dwu, dcs[0:3], dcs[3], dwd


def _local_step(x, tgt, p):
    row = lambda v: v.reshape(1, -1)
    grads = {}

    lower = jax.nn.softmax(p["hg_lb_logits"], axis=0)
    lb0 = row(lower[0])
    ev_cw = _pad_rows(p["ev_conv_w"][0])
    ev_cb = row(p["ev_conv_b"][0])
    wa = _block_diag(p["ev_gate_a_w"][0]).astype(BF16)
    wx = _block_diag(p["ev_gate_x_w"][0]).astype(BF16)
    ba, bx, lam = row(p["ev_gate_a_b"][0]), row(p["ev_gate_x_b"][0]), row(p["ev_lru_lambda"][0])
    gn = row(p["ev_hg_norm"][0])
    tril = jnp.tril(jnp.ones((SGU_CHUNK, SGU_CHUNK), F32))
    wsc = (p["od_w_s"][0] * tril).astype(BF16)
    bsb = jnp.broadcast_to(p["od_b_s"][0][:, :, None], (SGU_G, SGU_CHUNK, SGU_CHUNK)).astype(F32)
    ffn_cw = [_pad_rows(p["ffn_conv_w"][l]) for l in range(2)]
    ffn_cb = [row(p["ffn_conv_b"][l]) for l in range(2)]
    ev_w_in, ev_w_out = p["ev_w_in"][0], p["ev_w_out"][0]
    od_w_in, od_w_out = p["od_w_in"][0], p["od_w_out"][0]
    w_up = [p["ffn_w_up"][l] for l in range(2)]
    w_down = [p["ffn_w_down"][l] for l in range(2)]
    nm = [row(p["norm_mix"][l]) for l in range(2)]
    nf = [row(p["norm_ffn"][l]) for l in range(2)]

    h0 = x
    hn0, z0 = _norm_mm(h0, nm[0], ev_w_in, jnp.zeros((1, ev_w_in.shape[1]), F32), name="ev_in")
    out_a, hseq = _lru_fwd(z0, ev_cw, ev_cb, wa, ba, wx, bx, lam)
    out_b, ss = _hgrn_fwd(z0, lb0, gn)
    mix0 = jnp.concatenate([out_a, out_b], axis=1)
    h1 = _mm(mix0, ev_w_out, h0, F32, name="ev_out")
    h2, ffn0 = _ffn_fwd(h1, nf[0], w_up[0], ffn_cw[0], ffn_cb[0], w_down[0], "l0")
    hn1, p1 = _norm_mm(h2, nm[1], od_w_in, row(p["od_b_in"][0]), name="od_in")
    s1 = _sgu_fwd(p1, row(p["od_ln_g"][0]), row(p["od_ln_b"][0]), wsc, bsb)
    h3 = _mm(s1, od_w_out, h2, F32, name="od_out")
    h4, ffn1 = _ffn_fwd(h3, nf[1], w_up[1], ffn_cw[1], ffn_cb[1], w_down[1], "l1")
    dh4, dgf8, sq8 = _final_loss(h4, row(p["norm_final"]), tgt)
    grads["norm_final"] = dgf8.sum(axis=0)

    big = {}
    dh3, dnf1, dwu1, dcw1, dcb1, dwd1 = _ffn_bwd(dh4, h3, nf[1], ffn1, w_up[1], ffn_cw[1], ffn_cb[1], w_down[1], "l1")
    big["ffn_w_up", 1], big["ffn_w_down", 1] = _piece_major(dwu1), _piece_major(dwd1)
    ds1 = _mm(dh3, od_w_out, None, BF16, name="od_ds", transpose_w=True)
    big["od_w_out", 0] = _piece_major(_mm_tn(s1, dh3, name="od_dwo", tk=512))
    wsct = jnp.swapaxes(wsc, 1, 2)
    dp1, dws, dbs, dlg8, dlb8, dbin8 = _sgu_bwd(p1, ds1, row(p["od_ln_g"][0]), row(p["od_ln_b"][0]), wsc, wsct, bsb)
    grads["od_w_s"] = (dws * tril)[None]
    grads["od_b_s"] = dbs.sum(axis=-1)[None]
    grads["od_ln_g"] = dlg8.sum(axis=0)[None]
    grads["od_ln_b"] = dlb8.sum(axis=0)[None]
    grads["od_b_in"] = dbin8.sum(axis=0)[None]
    dh2, dnm1 = _mm_normbwd(dp1, od_w_in, h2, nm[1], dh3, name="od_dh")
    big["od_w_in", 0] = _piece_major(_mm_tn(hn1, dp1, name="od_dwi", tk=512, col_shards=N_CHIPS))

    dh1, dnf0, dwu0, dcw0, dcb0, dwd0 = _ffn_bwd(dh2, h1, nf[0], ffn0, w_up[0], ffn_cw[0], ffn_cb[0], w_down[0], "l0")
    big["ffn_w_up", 0], big["ffn_w_down", 0] = _piece_major(dwu0), _piece_major(dwd0)
    dmix = _mm(dh1, ev_w_out, None, BF16, name="ev_dmix", transpose_w=True)
    big["ev_w_out", 0] = _piece_major(_mm_tn(mix0, dh1, name="ev_dwo", tk=512))
    dz01, dc5, dwa, dwx, dvec = _lru_bwd(z0, hseq, dmix, ev_cw, ev_cb, wa, wa.T, ba, wx, wx.T, bx, lam)
    dq, df, dv, dg, dlb32, dgn32 = _hgrn_bwd(z0, ss, dmix, lb0, gn)
    dz0 = jnp.concatenate([dz01, dq, df, dv, dg], axis=1)
    grad_x, dnm0 = _mm_normbwd(dz0, ev_w_in, h0, nm[0], dh1, name="ev_dh")
    big["ev_w_in", 0] = _piece_major(_mm_tn(hn0, dz0, name="ev_dwi", tk=512, col_shards=N_CHIPS))

    dc5s = dc5.sum(axis=1)
    grads["ev_conv_w"] = dc5s[0:4][None]
    grads["ev_conv_b"] = dc5s[4][None]
    grads["ev_gate_a_w"] = _diag_blocks(dwa, LRU_BLOCKS)[None]
    grads["ev_gate_x_w"] = _diag_blocks(dwx, LRU_BLOCKS)[None]
    dvs = dvec.sum(axis=1)
    grads["ev_gate_a_b"] = dvs[0][None]
    grads["ev_gate_x_b"] = dvs[1][None]
    grads["ev_lru_lambda"] = (dvs[2] * (-jax.nn.sigmoid(-p["ev_lru_lambda"][0])))[None]
    dlb = dlb32.reshape(HG_HEADS, SUBLANES, HG_D).sum(axis=1).reshape(-1)
    grads["hg_lb_logits"] = dlb[None, :] * lower[0][None, :] * (jnp.eye(3, dtype=F32)[0][:, None] - lower)
    grads["ev_hg_norm"] = dgn32.reshape(HG_HEADS, SUBLANES, HG_D).sum(axis=(0, 1))[None]
    grads["norm_mix"] = jnp.stack([dnm0.sum(axis=0), dnm1.sum(axis=0)])
    grads["norm_ffn"] = jnp.stack([dnf0, dnf1])
    grads["ffn_conv_w"] = jnp.stack([dcw0, dcw1])
    grads["ffn_conv_b"] = jnp.stack([dcb0, dcb1])
    return sq8, grad_x, grads, big


MESH = pl.DeviceIdType.MESH
ANY = pl.BlockSpec(memory_space=pl.ANY)
N_CHIPS = 4
N_DEV = 8

SH_BIG = {"ev_w_in": 2, "ev_w_out": 1, "od_w_in": 2, "od_w_out": 1, "ffn_w_up": 2, "ffn_w_down": 1}
SH_SMALL = {"ev_conv_w": 2, "od_b_in": 1, "od_ln_g": 1, "od_ln_b": 1, "ffn_conv_w": 2}
REP = ["norm_mix", "norm_ffn", "norm_final", "ev_conv_b", "ev_gate_a_w", "ev_gate_a_b", "ev_gate_x_w", "ev_gate_x_b",
       "ev_lru_lambda", "hg_lb_logits", "ev_hg_norm", "od_w_s", "od_b_s", "ffn_conv_b"]
WEIGHTS = ["norm_mix", "norm_ffn", "norm_final", "ev_w_in", "ev_conv_w", "ev_conv_b", "ev_gate_a_w", "ev_gate_a_b", "ev_gate_x_w",
           "ev_gate_x_b", "ev_lru_lambda", "hg_lb_logits", "ev_hg_norm", "ev_w_out", "od_w_in", "od_b_in", "od_ln_g", "od_ln_b",
           "od_w_s", "od_b_s", "od_w_out", "ffn_w_up", "ffn_conv_w", "ffn_conv_b", "ffn_w_down"]


def _rows(n_elems, mult=SUBLANES):
    r = -(-n_elems // LANES)
    return -(-r // mult) * mult


def _pack(arrs, rows, dtype):
    flat = jnp.concatenate([a.reshape(-1).astype(dtype) for a in arrs])
    return jnp.pad(flat, (0, rows * LANES - flat.shape[0])).reshape(rows, LANES)


def _unpack(flat2d, shapes):
    flat = flat2d.reshape(-1)
    out, off = [], 0
    for s in shapes:
        n = 1
        for d in s:
            n *= d
        out.append(flat[off:off + n].reshape(s))
        off += n
    return out


def _mesh_pos():
    return lax.axis_index("x"), lax.axis_index("y"), lax.axis_index("c")


def _other_chips(x, y):
    return [(1 - x, y), (x, 1 - y), (1 - x, 1 - y)]


def _half_rows(n):
    return lambda r, c: r.at[0, pl.ds(c * (n // 2), n // 2), :]


GATHER_BIG = {
    "ev_w_in": ((1024, 3072), _half_rows(1024), lambda o, k, c: o.at[pl.ds(c * 512, 512), pl.ds(k * 768, 768)]),
    "ev_w_out": ((1024, 1024), _half_rows(256), lambda o, k, c: o.at[pl.ds(k * 256 + c * 128, 128), :]),
    "od_w_in": ((1024, 2048), _half_rows(1024), lambda o, k, c: o.at[pl.ds(c * 512, 512), pl.ds(k * 512, 512)]),
    "od_w_out": ((1024, 1024), _half_rows(256), lambda o, k, c: o.at[pl.ds(k * 256 + c * 128, 128), :]),
    "ffn_w_up": ((2, 1024, 2 * D_FF), lambda r, c: r.at[c], lambda o, k, c: o.at[c, :, pl.ds(k * (2 * D_FF // 4), 2 * D_FF // 4)]),
    "ffn_w_down": ((2, D_FF, 1024), lambda r, c: r.at[c], lambda o, k, c: o.at[c, pl.ds(k * (D_FF // 4), D_FF // 4), :]),
}


def _gather_weights(big, small):
    nb = len(big)
    descs = list(GATHER_BIG.values())
    rs = small.shape[0]

    def body(*refs):
        ins, s_ref = refs[:nb], refs[nb]
        outs, os_ref = refs[nb + 1:2 * nb + 1], refs[2 * nb + 1]
        ici_send, ici_recv, d2d_send, d2d_recv, loc_sems = refs[2 * nb + 2:2 * nb + 7]
        vbufs = refs[2 * nb + 7:]
        x, y, c = _mesh_pos()
        k = 2 * x + y
        chips = _other_chips(x, y)
        sib = (x, y, 1 - c)

        def remote(src, dst, ssem, rsem, to):
            return pltpu.make_async_remote_copy(src_ref=src, dst_ref=dst, send_sem=ssem, recv_sem=rsem, device_id=to,
                                                device_id_type=MESH)

        stage = [pltpu.make_async_copy(ins[t], vbufs[t], loc_sems.at[2 * t]) for t in range(nb)]
        stage.append(pltpu.make_async_copy(s_ref, vbufs[nb], loc_sems.at[2 * nb]))
        for cp in stage:
            cp.start()
        sends = []
        for t, (_, src, dst) in enumerate(descs):
            for j, (px, py) in enumerate(chips):
                sends.append(remote(src(ins[t], c), dst(outs[t], k, c), ici_send.at[3 * t + j], ici_recv.at[3 * t + j], (px, py, c)))
        for j, (px, py) in enumerate(chips):
            sends.append(remote(s_ref, os_ref.at[k], ici_send.at[3 * nb + j], ici_recv.at[3 * nb + j], (px, py, c)))
        for cp in sends:
            cp.start()
        for cp in stage:
            cp.wait()
        local = []
        for t, (_, src, dst) in enumerate(descs):
            for cc in (0, 1):
                local.append(pltpu.make_async_copy(src(vbufs[t], cc), dst(outs[t], k, cc), loc_sems.at[2 * t + cc]))
        local.append(pltpu.make_async_copy(vbufs[nb], os_ref.at[k], loc_sems.at[2 * nb]))
        for cp in local:
            cp.start()
        for t, (_, src, dst) in enumerate(descs):
            for j, (px, py) in enumerate(chips):
                got = dst(outs[t], 2 * px + py, c)
                remote(got, got, ici_send.at[3 * t + j], ici_recv.at[3 * t + j], (px, py, c)).wait_recv()
                fwd = remote(got, got, d2d_send.at[3 * t + j], d2d_recv.at[3 * t + j], sib)
                fwd.start()
                sends.append(fwd)
        for j, (px, py) in enumerate(chips):
            remote(s_ref, os_ref.at[2 * px + py], ici_send.at[3 * nb + j], ici_recv.at[3 * nb + j], (px, py, c)).wait_recv()
        for t, (_, src, dst) in enumerate(descs):
            for j, (px, py) in enumerate(chips):
                theirs = dst(outs[t], 2 * px + py, 1 - c)
                remote(theirs, theirs, d2d_send.at[3 * t + j], d2d_recv.at[3 * t + j], sib).wait_recv()
        for cp in sends:
            cp.wait_send()
        for cp in local:
            cp.wait()

    out_shape = [jax.ShapeDtypeStruct(d[0], BF16) for d in descs] + [jax.ShapeDtypeStruct((N_CHIPS, rs, LANES), small.dtype)]
    return pl.pallas_call(
        body, name="gather_weights", in_specs=[ANY] * (nb + 1), out_specs=[ANY] * (nb + 1), out_shape=out_shape,
        scratch_shapes=[pltpu.SemaphoreType.DMA((3 * nb + 3,)), pltpu.SemaphoreType.DMA((3 * nb + 3,)),
                        pltpu.SemaphoreType.DMA((3 * nb,)), pltpu.SemaphoreType.DMA((3 * nb,)),
                        pltpu.SemaphoreType.DMA((2 * nb + 1,))]
        + [pltpu.VMEM(b.shape, b.dtype) for b in big] + [pltpu.VMEM(small.shape, small.dtype)],
        compiler_params=pltpu.CompilerParams(vmem_limit_bytes=VMEM_LIMIT),
    )(*big, small)


def _remote(src, dst, ssem, rsem, to):
    return pltpu.make_async_remote_copy(src_ref=src, dst_ref=dst, send_sem=ssem, recv_sem=rsem, device_id=to, device_id_type=MESH)


def _rs_send_sibling(gs, tag):
    n = len(gs)
    counts = [N_CHIPS if g.ndim == 4 else 1 for g in gs]
    ns = sum(counts)

    def body(*refs):
        ins, outs = refs[:n], refs[n:2 * n]
        send_sems, recv_sems = refs[2 * n:]
        x, y, c = _mesh_pos()
        cps, s = [], 0
        for t in range(n):
            if counts[t] == 1:
                cps.append(_remote(ins[t].at[1 - c], outs[t], send_sems.at[s], recv_sems.at[s], (x, y, 1 - c)))
                s += 1
            else:
                for k in range(N_CHIPS):
                    cps.append(_remote(ins[t].at[k, 1 - c], outs[t].at[k], send_sems.at[s], recv_sems.at[s], (x, y, 1 - c)))
                    s += 1
        for cp in cps:
            cp.start()
        for cp in cps:
            cp.wait()

    out_shape = [jax.ShapeDtypeStruct(g.shape[:1] + g.shape[2:] if g.ndim == 4 else g.shape[1:], g.dtype) for g in gs]
    return pl.pallas_call(
        body, name=f"rs_send_sibling_{tag}", in_specs=[ANY] * n, out_specs=[ANY] * n, out_shape=out_shape,
        scratch_shapes=[pltpu.SemaphoreType.DMA((ns,)), pltpu.SemaphoreType.DMA((ns,))],
    )(*gs)


def _add_piece(g, recv, c, name):
    P, Q = g.shape[-2:]

    def body(c_ref, g_ref, r_ref, o_ref):
        o_ref[...] = g_ref[...].reshape(o_ref.shape) + r_ref[...]

    if g.ndim == 4:
        grid = (N_CHIPS,)
        in_specs = [pl.BlockSpec((1, 1, P, Q), lambda k, c_ref: (k, c_ref[0], 0, 0)), pl.BlockSpec((1, P, Q), lambda k, c_ref: (k, 0, 0))]
        out_spec = pl.BlockSpec((1, P, Q), lambda k, c_ref: (k, 0, 0))
    else:
        grid = (1,)
        in_specs = [pl.BlockSpec((1, P, Q), lambda k, c_ref: (c_ref[0], 0, 0)), pl.BlockSpec((P, Q), lambda k, c_ref: (0, 0))]
        out_spec = pl.BlockSpec((P, Q), lambda k, c_ref: (0, 0))
    return pl.pallas_call(
        body, name=name,
        grid_spec=pltpu.PrefetchScalarGridSpec(num_scalar_prefetch=1, grid=grid, in_specs=in_specs, out_specs=out_spec),
        out_shape=jax.ShapeDtypeStruct(recv.shape, g.dtype),
        compiler_params=_cp(1),
    )(c, g, recv)


def _rs_send_chips(hs, tag):
    n = len(hs)

    def body(*refs):
        ins, outs = refs[:n], refs[n:2 * n]
        send_sems, recv_sems = refs[2 * n:]
        x, y, c = _mesh_pos()
        k = 2 * x + y
        chips = _other_chips(x, y)
        piece = lambda t, kk: ins[t].at[kk] if hs[t].ndim == 3 else ins[t]
        sends = [_remote(piece(t, 2 * px + py), outs[t].at[k], send_sems.at[3 * t + j], recv_sems.at[3 * t + j], (px, py, c))
                 for t in range(n) for j, (px, py) in enumerate(chips)]
        for cp in sends:
            cp.start()
        for t in range(n):
            for j, (px, py) in enumerate(chips):
                _remote(piece(t, k), outs[t].at[2 * px + py], send_sems.at[3 * t + j], recv_sems.at[3 * t + j], (px, py, c)).wait_recv()
        for cp in sends:
            cp.wait_send()

    out_shape = [jax.ShapeDtypeStruct((N_CHIPS,) + h.shape[-2:], h.dtype) for h in hs]
    return pl.pallas_call(
        body, name=f"rs_send_chips_{tag}", in_specs=[ANY] * n, out_specs=[ANY] * n, out_shape=out_shape,
        scratch_shapes=[pltpu.SemaphoreType.DMA((3 * n,)), pltpu.SemaphoreType.DMA((3 * n,))],
    )(*hs)


def _add_chips(p, own, kc, name):
    _, P, Q = p.shape
    tr = P
    while N_CHIPS * tr * Q * 4 > 6 * 1024 * 1024 and tr % 16 == 0:
        tr //= 2
    sharded = own.ndim == 3

    def body(kc_ref, p_ref, own_ref, o_ref):
        k = kc_ref[0]
        mine = own_ref[...].reshape(tr, Q)
        v = [jnp.where(k == j, mine, p_ref[j]) for j in range(N_CHIPS)]
        o_ref[0] = ((v[0] + v[1]) + v[2]) + v[3]

    own_spec = (pl.BlockSpec((1, tr, Q), lambda i, kc_ref: (kc_ref[0], i, 0)) if sharded
                else pl.BlockSpec((tr, Q), lambda i, kc_ref: (i, 0)))
    return pl.pallas_call(
        body, name=name,
        grid_spec=pltpu.PrefetchScalarGridSpec(
            num_scalar_prefetch=1, grid=(P // tr,),
            in_specs=[pl.BlockSpec((N_CHIPS, tr, Q), lambda i, kc_ref: (0, i, 0)), own_spec],
            out_specs=pl.BlockSpec((1, tr, Q), lambda i, kc_ref: (kc_ref[1], i, 0))),
        out_shape=jax.ShapeDtypeStruct((2, P, Q), p.dtype),
        compiler_params=_cp(1),
    )(kc, p, own)


def _rs_share(fs, tag):
    n = len(fs)

    def body(*refs):
        outs = refs[n:2 * n]
        send_sems, recv_sems = refs[2 * n:]
        x, y, c = _mesh_pos()
        sends = [_remote(outs[t].at[c], outs[t].at[c], send_sems.at[t], recv_sems.at[t], (x, y, 1 - c)) for t in range(n)]
        for cp in sends:
            cp.start()
        for t in range(n):
            _remote(outs[t].at[c], outs[t].at[1 - c], send_sems.at[t], recv_sems.at[t], (x, y, 1 - c)).wait_recv()
        for cp in sends:
            cp.wait_send()

    return pl.pallas_call(
        body, name=f"rs_share_{tag}", in_specs=[ANY] * n, out_specs=[ANY] * n,
        out_shape=[jax.ShapeDtypeStruct(f.shape, f.dtype) for f in fs], input_output_aliases={t: t for t in range(n)},
        scratch_shapes=[pltpu.SemaphoreType.DMA((n,)), pltpu.SemaphoreType.DMA((n,))],
    )(*fs)


def _reduce(gs, kc, tag):
    from_sibling = _rs_send_sibling(gs, tag)
    chip_sums = [_add_piece(g, r, kc[1:], name=f"add_piece_{tag}_{t}") for t, (g, r) in enumerate(zip(gs, from_sibling))]
    from_chips = _rs_send_chips(chip_sums, tag)
    mine = [_add_chips(p, h, kc, name=f"add_chips_{tag}_{t}") for t, (p, h) in enumerate(zip(from_chips, chip_sums))]
    return _rs_share(mine, tag)


def _adamw(w, g, m, v, name):
    R, C = w.shape
    tr = R
    for cand in (512, 256, 128, 64, 32, 16, 8):
        if R % cand == 0 and cand * C * 4 <= 2 * 1024 * 1024:
            tr = cand
            break
    c1 = 1.0 / (1.0 - ADAM_B1 ** ADAM_STEP)
    c2 = 1.0 / (1.0 - ADAM_B2 ** ADAM_STEP)

    def body(w_ref, g_ref, m_ref, v_ref, d_ref, mo_ref, vo_ref):
        gv = g_ref[...]
        mn = ADAM_B1 * m_ref[...] + (1.0 - ADAM_B1) * gv
        vn = ADAM_B2 * v_ref[...] + (1.0 - ADAM_B2) * (gv * gv)
        mo_ref[...] = mn
        vo_ref[...] = vn
        d_ref[...] = -ADAM_LR * ((mn * c1) / (jnp.sqrt(vn * c2) + ADAM_EPS) + ADAM_WD * w_ref[...])

    spec = pl.BlockSpec((tr, C), lambda i: (i, 0))
    shp = jax.ShapeDtypeStruct((R, C), F32)
    return pl.pallas_call(body, name=name, grid=(R // tr,), in_specs=[spec] * 4, out_specs=[spec] * 3, out_shape=[shp] * 3,
                          compiler_params=_cp(1))(w, g, m, v)


BIG_GRADS = [("ffn_w_up", 1), ("ffn_w_down", 1), ("od_w_out", 0), ("od_w_in", 0), ("ffn_w_up", 0), ("ffn_w_down", 0),
             ("ev_w_out", 0), ("ev_w_in", 0)]


def _step(a):
    x, y, c = _mesh_pos()
    kc = jnp.stack([2 * x + y, c]).astype(jnp.int32)

    rs = _rows(sum(a[n].size for n in SH_SMALL))
    *full, gs = _gather_weights([a[n].astype(BF16) for n in GATHER_BIG], _pack([a[n] for n in SH_SMALL], rs, F32))
    p = {n: a[n] for n in REP}
    for n, w in zip(GATHER_BIG, full):
        p[n] = w if w.ndim == 3 else w[None]
    parts = [_unpack(gs[k], [a[n].shape for n in SH_SMALL]) for k in range(N_CHIPS)]
    for i, n in enumerate(SH_SMALL):
        p[n] = jnp.concatenate([parts[k][i] for k in range(N_CHIPS)], axis=SH_SMALL[n])

    sq8, grad_x, grads, big = _local_step(a["x"][0], a["loss_target"][0], p)
    loss = lax.psum(0.5 / D_MODEL * jnp.sum(sq8), ("x", "y", "c"))

    r_s = _rows(sum(a[n].size for n in SH_SMALL), 2 * SUBLANES) // 2
    small_pieces = []
    for k in range(N_CHIPS):
        pieces = [lax.slice_in_dim(grads[n], k * a[n].shape[ax], (k + 1) * a[n].shape[ax], axis=ax) for n, ax in SH_SMALL.items()]
        small_pieces.append(_pack(pieces, 2 * r_s, F32).reshape(2, r_s, LANES))
    g_small = jnp.stack(small_pieces)
    r_r = _rows(sum(a[n].size for n in REP), 2 * SUBLANES) // 2
    g_rep = _pack([grads[n] for n in REP], 2 * r_r, F32).reshape(2, r_r, LANES)
    reduced = _reduce([big[key] for key in BIG_GRADS] + [g_small, g_rep], kc, "all")
    red = dict(zip(BIG_GRADS, reduced))
    gfin = {}
    for n in ("ev_w_in", "ev_w_out", "od_w_in", "od_w_out"):
        gfin[n] = red[n, 0].reshape(a[n].shape)
    for n in ("ffn_w_up", "ffn_w_down"):
        gfin[n] = jnp.stack([red[n, l].reshape(a[n].shape[1:]) for l in range(2)])
    gfin.update(zip(SH_SMALL, _unpack(reduced[-2], [a[n].shape for n in SH_SMALL])))
    gfin.update(zip(REP, _unpack(reduced[-1], [a[n].shape for n in REP])))

    out = {"loss": loss, "grad_x": grad_x[None]}
    small_names = list(SH_SMALL) + REP
    for n in SH_BIG:
        shp = a[n].shape
        two_d = lambda t: t.reshape(-1, shp[-1])
        d, mo, vo = _adamw(two_d(a[n]), two_d(gfin[n]), two_d(a["m_" + n]), two_d(a["v_" + n]), name=f"adamw_{n}")
        out["delta_" + n], out["new_m_" + n], out["new_v_" + n] = d.reshape(shp), mo.reshape(shp), vo.reshape(shp)
    r_small = _rows(sum(a[n].size for n in small_names), 512)
    packs = [_pack([src(n) for n in small_names], r_small, F32)
             for src in (lambda n: a[n], lambda n: gfin[n], lambda n: a["m_" + n], lambda n: a["v_" + n])]
    d, mo, vo = _adamw(*packs, name="adamw_small")
    shapes = [a[n].shape for n in small_names]
    for n, dd, mm, vv in zip(small_names, _unpack(d, shapes), _unpack(mo, shapes), _unpack(vo, shapes)):
        out["delta_" + n], out["new_m_" + n], out["new_v_" + n] = dd, mm, vv
    for n in WEIGHTS:
        out["grad_" + n] = gfin[n]
    return out


def kernel(x, norm_mix, norm_ffn, norm_final, ev_w_in, ev_conv_w, ev_conv_b, ev_gate_a_w, ev_gate_a_b, ev_gate_x_w, ev_gate_x_b, ev_lru_lambda, hg_lb_logits, ev_hg_norm, ev_w_out, od_w_in, od_b_in, od_ln_g, od_ln_b, od_w_s, od_b_s, od_w_out, ffn_w_up, ffn_conv_w, ffn_conv_b, ffn_w_down, loss_target, m_norm_mix, m_norm_ffn, m_norm_final, m_ev_w_in, m_ev_conv_w, m_ev_conv_b, m_ev_gate_a_w, m_ev_gate_a_b, m_ev_gate_x_w, m_ev_gate_x_b, m_ev_lru_lambda, m_hg_lb_logits, m_ev_hg_norm, m_ev_w_out, m_od_w_in, m_od_b_in, m_od_ln_g, m_od_ln_b, m_od_w_s, m_od_b_s, m_od_w_out, m_ffn_w_up, m_ffn_conv_w, m_ffn_conv_b, m_ffn_w_down, v_norm_mix, v_norm_ffn, v_norm_final, v_ev_w_in, v_ev_conv_w, v_ev_conv_b, v_ev_gate_a_w, v_ev_gate_a_b, v_ev_gate_x_w, v_ev_gate_x_b, v_ev_lru_lambda, v_hg_lb_logits, v_ev_hg_norm, v_ev_w_out, v_od_w_in, v_od_b_in, v_od_ln_g, v_od_ln_b, v_od_w_s, v_od_b_s, v_od_w_out, v_ffn_w_up, v_ffn_conv_w, v_ffn_conv_b, v_ffn_w_down):
    vals = (x, norm_mix, norm_ffn, norm_final, ev_w_in, ev_conv_w, ev_conv_b, ev_gate_a_w, ev_gate_a_b, ev_gate_x_w, ev_gate_x_b, ev_lru_lambda, hg_lb_logits, ev_hg_norm, ev_w_out, od_w_in, od_b_in, od_ln_g, od_ln_b, od_w_s, od_b_s, od_w_out, ffn_w_up, ffn_conv_w, ffn_conv_b, ffn_w_down, loss_target, m_norm_mix, m_norm_ffn, m_norm_final, m_ev_w_in, m_ev_conv_w, m_ev_conv_b, m_ev_gate_a_w, m_ev_gate_a_b, m_ev_gate_x_w, m_ev_gate_x_b, m_ev_lru_lambda, m_hg_lb_logits, m_ev_hg_norm, m_ev_w_out, m_od_w_in, m_od_b_in, m_od_ln_g, m_od_ln_b, m_od_w_s, m_od_b_s, m_od_w_out, m_ffn_w_up, m_ffn_conv_w, m_ffn_conv_b, m_ffn_w_down, v_norm_mix, v_norm_ffn, v_norm_final, v_ev_w_in, v_ev_conv_w, v_ev_conv_b, v_ev_gate_a_w, v_ev_gate_a_b, v_ev_gate_x_w, v_ev_gate_x_b, v_ev_lru_lambda, v_hg_lb_logits, v_ev_hg_norm, v_ev_w_out, v_od_w_in, v_od_b_in, v_od_ln_g, v_od_ln_b, v_od_w_s, v_od_b_s, v_od_w_out, v_ffn_w_up, v_ffn_conv_w, v_ffn_conv_b, v_ffn_w_down)
    names = ["x"] + WEIGHTS + ["loss_target"] + ["m_" + n for n in WEIGHTS] + ["v_" + n for n in WEIGHTS]
    out = _step(dict(zip(names, vals)))
    return (out["loss"], out["grad_x"], *[out["grad_" + n] for n in WEIGHTS], *[out["delta_" + n] for n in WEIGHTS],
            *[out["new_m_" + n] for n in WEIGHTS], *[out["new_v_" + n] for n in WEIGHTS])
```

```python
import functools

import jax
import jax.numpy as jnp
from jax import lax
from jax.experimental import pallas as pl
from jax.experimental.pallas import tpu as pltpu

F32 = jnp.float32
BF16 = jnp.bfloat16

EPS = 1e-6
D_MODEL = 1024
LRU_W = 512
LRU_BLOCKS = 8
LRU_C = 8.0
HG_HEADS = 4
HG_D = 128
HG_CHUNK = 64
SGU_G = 8
SGU_CHUNK = 128
D_FF = 2816
ADAM_LR, ADAM_B1, ADAM_B2, ADAM_EPS, ADAM_WD, ADAM_STEP = 0.001, 0.9, 0.999, 1e-08, 0.01, 10

V7X_VMEM_BYTES = 64 * 1024 * 1024
VMEM_LIMIT = V7X_VMEM_BYTES - 8 * 1024 * 1024
SUBLANES = 8
LANES = 128
BF16_ROWS = 16

GELU_C0 = 0.7978845608028654
GELU_C1 = 0.044715

NN = (((1,), (0,)), ((), ()))
NT = (((1,), (1,)), ((), ()))
TN = (((0,), (0,)), ((), ()))


def _dot(a, b, dims=NN):
    return lax.dot_general(a.astype(BF16), b.astype(BF16), dims, preferred_element_type=F32)


def _cp(n_grid):
    return pltpu.CompilerParams(dimension_semantics=("arbitrary",) * n_grid, vmem_limit_bytes=VMEM_LIMIT)


def _chunk(n, cap):
    best = LANES
    for c in range(LANES, cap + 1, LANES):
        if n % c == 0:
            best = c
    return best


def _resident(shape):
    nd = len(shape)
    return pl.BlockSpec(shape, lambda *_: (0,) * nd, pipeline_mode=pl.Buffered(1))


def _rsum8(x):
    r, c = x.shape
    return x.reshape(r // SUBLANES, SUBLANES, c).sum(axis=0)


def _sigmoid(x):
    return 1.0 / (1.0 + jnp.exp(-x))


def _gelu(x):
    return 0.5 * x * (1.0 + jnp.tanh(GELU_C0 * (x + GELU_C1 * x * x * x)))


def _gelu_grad(x):
    t = jnp.tanh(GELU_C0 * (x + GELU_C1 * x * x * x))
    return 0.5 * (1.0 + t) + 0.5 * x * (1.0 - t * t) * GELU_C0 * (1.0 + 3.0 * GELU_C1 * x * x)


def _silu_and_grad(x):
    s = _sigmoid(x)
    return x * s, s * (1.0 + x * (1.0 - s))


def _shift_rows(e, j):
    n = e.shape[0]
    return e if j % n == 0 else pltpu.roll(e, j % n, 0)


def _norm_mm(h, g, w, b, name, tt=512):
    T, D = h.shape
    N = w.shape[1]
    cn = _chunk(N, 512)

    def body(h_ref, g_ref, w_ref, b_ref, hn_ref, z_ref):
        x = h_ref[...]
        r = lax.rsqrt(jnp.mean(x * x, axis=-1, keepdims=True) + EPS)
        hn = (x * r * g_ref[...]).astype(BF16)
        hn_ref[...] = hn
        for j in range(0, N, cn):
            acc = jnp.dot(hn, w_ref[:, j:j + cn], preferred_element_type=F32) + b_ref[:, j:j + cn]
            z_ref[:, j:j + cn] = acc.astype(BF16)

    return pl.pallas_call(
        body, name=name, grid=(T // tt,),
        in_specs=[pl.BlockSpec((tt, D), lambda i: (i, 0)), _resident((1, D)), _resident((D, N)), _resident((1, N))],
        out_specs=[pl.BlockSpec((tt, D), lambda i: (i, 0)), pl.BlockSpec((tt, N), lambda i: (i, 0))],
        out_shape=[jax.ShapeDtypeStruct((T, D), BF16), jax.ShapeDtypeStruct((T, N), BF16)],
        compiler_params=_cp(1),
    )(h, g, w, b)


def _mm(a, w, res, out_dtype, name, tt=512, transpose_w=False):
    T, K = a.shape
    N = w.shape[0] if transpose_w else w.shape[1]
    cn = _chunk(N, 512)
    has_res = res is not None

    def body(*refs):
        a_ref, w_ref = refs[0], refs[1]
        res_ref = refs[2] if has_res else None
        o_ref = refs[-1]
        av = a_ref[...].astype(BF16)
        for j in range(0, N, cn):
            if transpose_w:
                acc = lax.dot_general(av, w_ref[j:j + cn, :], NT, preferred_element_type=F32)
            else:
                acc = jnp.dot(av, w_ref[:, j:j + cn], preferred_element_type=F32)
            if has_res:
                acc = acc + res_ref[:, j:j + cn]
            o_ref[:, j:j + cn] = acc.astype(out_dtype)

    in_specs = [pl.BlockSpec((tt, K), lambda i: (i, 0)), _resident(w.shape)]
    args = [a, w]
    if has_res:
        in_specs.append(pl.BlockSpec((tt, N), lambda i: (i, 0)))
        args.append(res)
    return pl.pallas_call(
        body, name=name, grid=(T // tt,), in_specs=in_specs,
        out_specs=pl.BlockSpec((tt, N), lambda i: (i, 0)),
        out_shape=jax.ShapeDtypeStruct((T, N), out_dtype),
        compiler_params=_cp(1),
    )(*args)


def _mm_tn(a, b, name, tk, tt=512, col_shards=1):
    T, K = a.shape
    N = b.shape[1]
    ns = N // col_shards

    def body(a_ref, b_ref, o_ref):
        acc = lax.dot_general(a_ref[...].astype(BF16), b_ref[...].astype(BF16), TN, preferred_element_type=F32)
        first = pl.program_id(1) == 0
        if col_shards == 1:
            prev = jnp.where(first, 0.0, o_ref[...])
            o_ref[...] = prev + acc
        else:
            for s in range(col_shards):
                prev = jnp.where(first, 0.0, o_ref[s])
                o_ref[s] = prev + acc[:, s * ns:(s + 1) * ns]

    if col_shards == 1:
        out_spec = pl.BlockSpec((tk, N), lambda k, t: (k, 0))
        out_shape = jax.ShapeDtypeStruct((K, N), F32)
    else:
        out_spec = pl.BlockSpec((col_shards, tk, ns), lambda k, t: (0, k, 0))
        out_shape = jax.ShapeDtypeStruct((col_shards, K, ns), F32)
    return pl.pallas_call(
        body, name=name, grid=(K // tk, T // tt),
        in_specs=[pl.BlockSpec((tt, tk), lambda k, t: (t, k)), pl.BlockSpec((tt, N), lambda k, t: (t, 0))],
        out_specs=out_spec, out_shape=out_shape,
        compiler_params=_cp(2),
    )(a, b)


def _mm_normbwd(dz, w, x, g, dres, name, tt=512):
    T, N = dz.shape
    D = w.shape[0]

    def body(dz_ref, wt_ref, x_ref, g_ref, dres_ref, dx_ref, dg_ref):
        @pl.when(pl.program_id(0) == 0)
        def _():
            dg_ref[...] = jnp.zeros_like(dg_ref)

        dy = lax.dot_general(dz_ref[...], wt_ref[...], NT, preferred_element_type=F32)
        x = x_ref[...]
        r = lax.rsqrt(jnp.mean(x * x, axis=-1, keepdims=True) + EPS)
        xn = x * r
        dg_ref[...] += _rsum8(dy * xn)
        dxn = dy * g_ref[...]
        dx_ref[...] = dres_ref[...] + r * (dxn - xn * jnp.mean(dxn * xn, axis=-1, keepdims=True))

    return pl.pallas_call(
        body, name=name, grid=(T // tt,),
        in_specs=[pl.BlockSpec((tt, N), lambda i: (i, 0)), _resident((D, N)), pl.BlockSpec((tt, D), lambda i: (i, 0)),
                  _resident((1, D)), pl.BlockSpec((tt, D), lambda i: (i, 0))],
        out_specs=[pl.BlockSpec((tt, D), lambda i: (i, 0)), pl.BlockSpec((SUBLANES, D), lambda i: (0, 0))],
        out_shape=[jax.ShapeDtypeStruct((T, D), F32), jax.ShapeDtypeStruct((SUBLANES, D), F32)],
        compiler_params=_cp(1),
    )(dz, w, x, g, dres)


def _final_loss(h, g, tgt, name="final_loss", tt=512):
    T, D = h.shape

    def body(h_ref, g_ref, t_ref, dh_ref, dg_ref, sq_ref):
        @pl.when(pl.program_id(0) == 0)
        def _():
            dg_ref[...] = jnp.zeros_like(dg_ref)
            sq_ref[...] = jnp.zeros_like(sq_ref)

        x = h_ref[...]
        r = lax.rsqrt(jnp.mean(x * x, axis=-1, keepdims=True) + EPS)
        xn = x * r
        gv = g_ref[...]
        diff = xn * gv - t_ref[...]
        sq_ref[...] += _rsum8(diff * diff)
        dy = diff * (1.0 / D)
        dg_ref[...] += _rsum8(dy * xn)
        dxn = dy * gv
        dh_ref[...] = r * (dxn - xn * jnp.mean(dxn * xn, axis=-1, keepdims=True))

    return pl.pallas_call(
        body, name=name, grid=(T // tt,),
        in_specs=[pl.BlockSpec((tt, D), lambda i: (i, 0)), _resident((1, D)), pl.BlockSpec((tt, D), lambda i: (i, 0))],
        out_specs=[pl.BlockSpec((tt, D), lambda i: (i, 0)), pl.BlockSpec((SUBLANES, D), lambda i: (0, 0)),
                   pl.BlockSpec((SUBLANES, D), lambda i: (0, 0))],
        out_shape=[jax.ShapeDtypeStruct((T, D), F32), jax.ShapeDtypeStruct((SUBLANES, D), F32),
                   jax.ShapeDtypeStruct((SUBLANES, D), F32)],
        compiler_params=_cp(1),
    )(h, g, tgt)


def _ffn_act(gu, cw, cb, name, tt=512):
    T = gu.shape[0]
    F = gu.shape[1] // 2
    cc = _chunk(F, 256)
    hb = tt // BF16_ROWS

    def body(gu_ref, halo_ref, cw_ref, cb_ref, a_ref):
        first = pl.program_id(0) == 0
        for c0 in range(0, F, cc):
            cs = slice(c0, c0 + cc)
            x = gu_ref[:, cs].astype(F32)
            halo = jnp.where(first, 0.0, halo_ref[:, cs].astype(F32))
            e = jnp.concatenate([halo, x], axis=0)
            gc = (cb_ref[:, cs] + cw_ref[0:1, cs] * _shift_rows(e, 2)[BF16_ROWS:] + cw_ref[1:2, cs] * _shift_rows(e, 1)[BF16_ROWS:]
                  + cw_ref[2:3, cs] * x)
            up = gu_ref[:, F + c0:F + c0 + cc].astype(F32)
            a_ref[:, cs] = (gc * _sigmoid(gc) * up).astype(BF16)

    return pl.pallas_call(
        body, name=name, grid=(T // tt,),
        in_specs=[pl.BlockSpec((tt, 2 * F), lambda i: (i, 0)),
                  pl.BlockSpec((BF16_ROWS, F), lambda i: (jnp.maximum(i * hb - 1, 0), 0)),
                  _resident((SUBLANES, F)), _resident((1, F))],
        out_specs=pl.BlockSpec((tt, F), lambda i: (i, 0)),
        out_shape=jax.ShapeDtypeStruct((T, F), BF16),
        compiler_params=_cp(1),
    )(gu, gu, cw, cb)


def _ffn_act_bwd(gu, da, cw, cb, name, tt=512):
    T = gu.shape[0]
    F = gu.shape[1] // 2
    cc = _chunk(F, 256)
    hb = tt // BF16_ROWS
    last_hb = T // BF16_ROWS - 1
    nt = T // tt

    def body(gu_ref, gprev_ref, gunext_ref, da_ref, danext_ref, cw_ref, cb_ref, dgu_ref, dc_ref):
        i = pl.program_id(0)

        @pl.when(i == 0)
        def _():
            dc_ref[...] = jnp.zeros_like(dc_ref)

        n = tt + BF16_ROWS
        for c0 in range(0, F, cc):
            cs = slice(c0, c0 + cc)
            us = slice(F + c0, F + c0 + cc)
            g = gu_ref[:, cs].astype(F32)
            gp = jnp.where(i == 0, 0.0, gprev_ref[:, cs].astype(F32))
            ge = jnp.concatenate([gp, g, gunext_ref[:, cs].astype(F32)], axis=0)
            g1 = _shift_rows(ge, 1)[BF16_ROWS:]
            g2 = _shift_rows(ge, 2)[BF16_ROWS:]
            gc = cb_ref[:, cs] + cw_ref[0:1, cs] * g2 + cw_ref[1:2, cs] * g1 + cw_ref[2:3, cs] * ge[BF16_ROWS:]
            up = jnp.concatenate([gu_ref[:, us].astype(F32), gunext_ref[:, us].astype(F32)], axis=0)
            dan = jnp.where(i == nt - 1, 0.0, danext_ref[:, cs].astype(F32))
            dae = jnp.concatenate([da_ref[:, cs].astype(F32), dan], axis=0)
            s, ds = _silu_and_grad(gc)
            dgc = dae * up * ds
            dgu_ref[:, us] = (dae * s)[:tt].astype(BF16)
            dgate = cw_ref[2:3, cs] * dgc + cw_ref[1:2, cs] * _shift_rows(dgc, n - 1) + cw_ref[0:1, cs] * _shift_rows(dgc, n - 2)
            dgu_ref[:, cs] = dgate[:tt].astype(BF16)
            dm = dgc[:tt]
            dc_ref[0, :, cs] += _rsum8(dm * g2[:tt])
            dc_ref[1, :, cs] += _rsum8(dm * g1[:tt])
            dc_ref[2, :, cs] += _rsum8(dm * g)
            dc_ref[3, :, cs] += _rsum8(dm)

    return pl.pallas_call(
        body, name=name, grid=(nt,),
        in_specs=[pl.BlockSpec((tt, 2 * F), lambda i: (i, 0)),
                  pl.BlockSpec((BF16_ROWS, F), lambda i: (jnp.maximum(i * hb - 1, 0), 0)),
                  pl.BlockSpec((BF16_ROWS, 2 * F), lambda i: (jnp.minimum((i + 1) * hb, last_hb), 0)),
                  pl.BlockSpec((tt, F), lambda i: (i, 0)),
                  pl.BlockSpec((BF16_ROWS, F), lambda i: (jnp.minimum((i + 1) * hb, last_hb), 0)),
                  _resident((SUBLANES, F)), _resident((1, F))],
        out_specs=[pl.BlockSpec((tt, 2 * F), lambda i: (i, 0)), pl.BlockSpec((4, SUBLANES, F), lambda i: (0, 0, 0))],
        out_shape=[jax.ShapeDtypeStruct((T, 2 * F), BF16), jax.ShapeDtypeStruct((4, SUBLANES, F), F32)],
        compiler_params=_cp(1),
    )(gu, gu, gu, da, da, cw, cb)


def _softplus_neg(lam):
    x = -lam
    y = jnp.exp(-jnp.abs(x))
    l1p = jnp.where(y < 0.01, y * (1.0 - y * (0.5 - y * (1.0 / 3.0))), jnp.log(1.0 + y))
    return jnp.maximum(x, 0.0) + l1p


def _lru_gates(xc, wa_ref, ba_ref, wx_ref, bx_ref, sp):
    xcb = xc.astype(BF16)
    r = _sigmoid(jnp.dot(xcb, wa_ref[...], preferred_element_type=F32) + ba_ref[...])
    gi = _sigmoid(jnp.dot(xcb, wx_ref[...], preferred_element_type=F32) + bx_ref[...])
    log_a = -LRU_C * r * sp
    a = jnp.exp(log_a)
    x2 = 2.0 * log_a
    series = -x2 * (1.0 + x2 * 0.5 * (1.0 + x2 * (1.0 / 3.0) * (1.0 + x2 * 0.25 * (1.0 + x2 * 0.2))))
    om = jnp.where(x2 > -0.125, series, 1.0 - a * a)
    return r, gi, a, jnp.sqrt(om)


def _lru_conv(xr, halo, cw_ref, cb_ref):
    e = jnp.concatenate([halo, xr], axis=0)
    x1 = _shift_rows(e, 1)[BF16_ROWS:]
    x2 = _shift_rows(e, 2)[BF16_ROWS:]
    x3 = _shift_rows(e, 3)[BF16_ROWS:]
    xc = cb_ref[...] + cw_ref[0:1, :] * x3 + cw_ref[1:2, :] * x2 + cw_ref[2:3, :] * x1 + cw_ref[3:4, :] * xr
    return xc, x1, x2, x3


def _lru_fwd(z, cw, cb, wa, ba, wx, bx, lam, name="lru_fwd", tt=256):
    T = z.shape[0]
    W = LRU_W
    hb = tt // BF16_ROWS
    ng = tt // SUBLANES

    def body(z_ref, halo_ref, cw_ref, cb_ref, wa_ref, ba_ref, wx_ref, bx_ref, lam_ref, oa_ref, h_ref, a_s, u_s, hc):
        i = pl.program_id(0)

        @pl.when(i == 0)
        def _():
            hc[...] = jnp.zeros_like(hc)

        xr = z_ref[:, W:2 * W].astype(F32)
        halo = jnp.where(i == 0, 0.0, halo_ref[...].astype(F32))
        xc, _, _, _ = _lru_conv(xr, halo, cw_ref, cb_ref)
        sp = _softplus_neg(lam_ref[...])
        r, gi, a, mult = _lru_gates(xc, wa_ref, ba_ref, wx_ref, bx_ref, sp)
        a_s[...] = a
        u_s[...] = mult * gi * xc
        row = lax.broadcasted_iota(jnp.int32, (SUBLANES, W), 0)

        def step(j, hprev):
            r0 = pl.multiple_of(j * SUBLANES, SUBLANES)
            A = a_s[pl.ds(r0, SUBLANES), :]
            U = u_s[pl.ds(r0, SUBLANES), :]
            for k in (1, 2, 4):
                m = row >= k
                U = jnp.where(m, A * pltpu.roll(U, k, 0) + U, U)
                A = jnp.where(m, A * pltpu.roll(A, k, 0), A)
            H = U + A * hprev
            h_ref[pl.ds(r0, SUBLANES), :] = H
            return jnp.broadcast_to(H[SUBLANES - 1:SUBLANES, :], (SUBLANES, W))

        hc[...] = lax.fori_loop(0, ng, step, hc[...])
        oa_ref[...] = (_gelu(z_ref[:, 0:W].astype(F32)) * h_ref[...]).astype(BF16)

    return pl.pallas_call(
        body, name=name, grid=(T // tt,),
        in_specs=[pl.BlockSpec((tt, 2 * W), lambda i: (i, 0)),
                  pl.BlockSpec((BF16_ROWS, W), lambda i: (jnp.maximum(i * hb - 1, 0), 1)),
                  _resident((SUBLANES, W)), _resident((1, W)), _resident((W, W)), _resident((1, W)),
                  _resident((W, W)), _resident((1, W)), _resident((1, W))],
        out_specs=[pl.BlockSpec((tt, W), lambda i: (i, 0)), pl.BlockSpec((tt, W), lambda i: (i, 0))],
        out_shape=[jax.ShapeDtypeStruct((T, W), BF16), jax.ShapeDtypeStruct((T, W), F32)],
        scratch_shapes=[pltpu.VMEM((tt, W), F32), pltpu.VMEM((tt, W), F32), pltpu.VMEM((SUBLANES, W), F32)],
        compiler_params=_cp(1),
    )(z, z, cw, cb, wa, ba, wx, bx, lam)


def _lru_bwd(z, hseq, dmix, cw, cb, wa, wat, ba, wx, wxt, bx, lam, name="lru_bwd", tt=256):
    T = z.shape[0]
    W = LRU_W
    nt = T // tt
    hb = tt // BF16_ROWS
    sb = tt // SUBLANES
    ng = tt // SUBLANES

    def body(z_ref, halo_ref, h_ref, hprev_ref, dm_ref, cw_ref, cb_ref, wa_ref, wat_ref, ba_ref, wx_ref, wxt_ref, bx_ref,
             lam_ref, dz_ref, dc_ref, dwa_ref, dwx_ref, dv_ref, c_s, d_s, g_s, gc, an, dxn):
        i = pl.program_id(0)
        ti = nt - 1 - i

        @pl.when(i == 0)
        def _():
            dc_ref[...] = jnp.zeros_like(dc_ref)
            dwa_ref[...] = jnp.zeros_like(dwa_ref)
            dwx_ref[...] = jnp.zeros_like(dwx_ref)
            dv_ref[...] = jnp.zeros_like(dv_ref)
            gc[...] = jnp.zeros_like(gc)
            an[...] = jnp.zeros_like(an)
            dxn[...] = jnp.zeros_like(dxn)

        xr = z_ref[:, W:2 * W].astype(F32)
        yg = z_ref[:, 0:W].astype(F32)
        halo = jnp.where(ti == 0, 0.0, halo_ref[...].astype(F32))
        xc, x1, x2, x3 = _lru_conv(xr, halo, cw_ref, cb_ref)
        sp = _softplus_neg(lam_ref[...])
        r, gi, a, mult = _lru_gates(xc, wa_ref, ba_ref, wx_ref, bx_ref, sp)
        h = h_ref[...]
        hp = jnp.where(ti == 0, 0.0, hprev_ref[...])
        hm1 = _shift_rows(jnp.concatenate([hp, h], axis=0), 1)[SUBLANES:]
        dout = dm_ref[...].astype(F32)
        d_s[...] = dout * _gelu(yg)
        dz_ref[:, 0:W] = (dout * h * _gelu_grad(yg)).astype(BF16)
        c_s[...] = _shift_rows(jnp.concatenate([a, an[...]], axis=0), tt + SUBLANES - 1)[:tt]
        an[...] = a[0:SUBLANES, :]
        row = lax.broadcasted_iota(jnp.int32, (SUBLANES, W), 0)

        def step(j, gnext):
            r0 = pl.multiple_of((ng - 1 - j) * SUBLANES, SUBLANES)
            C = c_s[pl.ds(r0, SUBLANES), :]
            G = d_s[pl.ds(r0, SUBLANES), :]
            for k in (1, 2, 4):
                m = row < SUBLANES - k
                G = jnp.where(m, G + C * pltpu.roll(G, SUBLANES - k, 0), G)
                C = jnp.where(m, C * pltpu.roll(C, SUBLANES - k, 0), C)
            G = G + C * gnext
            g_s[pl.ds(r0, SUBLANES), :] = G
            return jnp.broadcast_to(G[0:1, :], (SUBLANES, W))

        gc[...] = lax.fori_loop(0, ng, step, gc[...])
        du = g_s[...]
        da = du * hm1
        dgi = du * mult * xc
        dxc = du * mult * gi
        dmult = du * gi * xc
        dlog_a = da * a - dmult * (a * a) / mult
        dr = dlog_a * (-LRU_C * sp)
        dv_ref[2] += _rsum8(dlog_a * (-LRU_C * r))
        dpr = (dr * r * (1.0 - r)).astype(BF16)
        dpi = (dgi * gi * (1.0 - gi)).astype(BF16)
        dv_ref[0] += _rsum8(dpr.astype(F32))
        dv_ref[1] += _rsum8(dpi.astype(F32))
        xcb = xc.astype(BF16)
        dwa_ref[...] += lax.dot_general(xcb, dpr, TN, preferred_element_type=F32)
        dwx_ref[...] += lax.dot_general(xcb, dpi, TN, preferred_element_type=F32)
        dxc = dxc + jnp.dot(dpr, wat_ref[...], preferred_element_type=F32) + jnp.dot(dpi, wxt_ref[...], preferred_element_type=F32)
        n = tt + BF16_ROWS
        de = jnp.concatenate([dxc, dxn[...]], axis=0)
        dxr = (cw_ref[3:4, :] * dxc + cw_ref[2:3, :] * _shift_rows(de, n - 1)[:tt] + cw_ref[1:2, :] * _shift_rows(de, n - 2)[:tt]
               + cw_ref[0:1, :] * _shift_rows(de, n - 3)[:tt])
        dxn[...] = dxc[0:BF16_ROWS, :]
        dz_ref[:, W:2 * W] = dxr.astype(BF16)
        dc_ref[0] += _rsum8(dxc * x3)
        dc_ref[1] += _rsum8(dxc * x2)
        dc_ref[2] += _rsum8(dxc * x1)
        dc_ref[3] += _rsum8(dxc * xr)
        dc_ref[4] += _rsum8(dxc)

    rev = lambda i: nt - 1 - i
    return pl.pallas_call(
        body, name=name, grid=(nt,),
        in_specs=[pl.BlockSpec((tt, 2 * W), lambda i: (rev(i), 0)),
                  pl.BlockSpec((BF16_ROWS, W), lambda i: (jnp.maximum(rev(i) * hb - 1, 0), 1)),
                  pl.BlockSpec((tt, W), lambda i: (rev(i), 0)),
                  pl.BlockSpec((SUBLANES, W), lambda i: (jnp.maximum(rev(i) * sb - 1, 0), 0)),
                  pl.BlockSpec((tt, W), lambda i: (rev(i), 0)),
                  _resident((SUBLANES, W)), _resident((1, W)), _resident((W, W)), _resident((W, W)), _resident((1, W)),
                  _resident((W, W)), _resident((W, W)), _resident((1, W)), _resident((1, W))],
        out_specs=[pl.BlockSpec((tt, 2 * W), lambda i: (rev(i), 0)),
                   pl.BlockSpec((5, SUBLANES, W), lambda i: (0, 0, 0)),
                   pl.BlockSpec((W, W), lambda i: (0, 0)), pl.BlockSpec((W, W), lambda i: (0, 0)),
                   pl.BlockSpec((3, SUBLANES, W), lambda i: (0, 0, 0))],
        out_shape=[jax.ShapeDtypeStruct((T, 2 * W), BF16), jax.ShapeDtypeStruct((5, SUBLANES, W), F32),
                   jax.ShapeDtypeStruct((W, W), F32), jax.ShapeDtypeStruct((W, W), F32),
                   jax.ShapeDtypeStruct((3, SUBLANES, W), F32)],
        scratch_shapes=[pltpu.VMEM((tt, W), F32), pltpu.VMEM((tt, W), F32), pltpu.VMEM((tt, W), F32),
                        pltpu.VMEM((SUBLANES, W), F32), pltpu.VMEM((SUBLANES, W), F32), pltpu.VMEM((BF16_ROWS, W), F32)],
        compiler_params=_cp(1),
    )(z, z, hseq, hseq, dmix, cw, cb, wa, wat, ba, wx, wxt, bx, lam)


def _split3(x):
    hi = x.astype(BF16)
    r1 = x - hi.astype(F32)
    mid = r1.astype(BF16)
    lo = (r1 - mid.astype(F32)).astype(BF16)
    return hi, mid, lo


def _tri_matmul(tri, x):
    hi, mid, lo = _split3(x)
    return (jnp.dot(tri, hi, preferred_element_type=F32) + jnp.dot(tri, mid, preferred_element_type=F32)
            + jnp.dot(tri, lo, preferred_element_type=F32))


def _hg_chunk(q, fl, lb):
    C = q.shape[0]
    ri = lax.broadcasted_iota(jnp.int32, (C, C), 0)
    ci = lax.broadcasted_iota(jnp.int32, (C, C), 1)
    causal = ri >= ci
    sig = _sigmoid(fl)
    f = lb + (1.0 - lb) * sig
    k = 1.0 - f
    sq = _sigmoid(q)
    qf = q * sq
    b = _tri_matmul(causal.astype(BF16), jnp.log(f))
    bm = b[C // 2 - 1:C // 2, :]
    bl = b[C - 1:C, :]
    qt = qf * jnp.exp(b - bm)
    kt = k * jnp.exp(bm - b)
    qin = qf * jnp.exp(b)
    kout = k * jnp.exp(bl - b)
    att = jnp.where(causal, _dot(qt, kt, NT), 0.0)
    return dict(sig=sig, f=f, k=k, sq=sq, qf=qf, b=b, bm=bm, bl=bl, qt=qt, kt=kt, qin=qin, kout=kout, att=att,
                causal=causal, anti=ri <= ci, decay=jnp.exp(bl))


def _hgrn_fwd(z, lb, gn, name="hgrn_fwd", tt=256):
    T = z.shape[0]
    C = HG_CHUNK
    nc = tt // C
    Dh = HG_D

    def body(q_ref, f_ref, v_ref, g_ref, lb_ref, gn_ref, o_ref, ss_ref, st):
        @pl.when(pl.program_id(1) == 0)
        def _():
            st[...] = jnp.zeros_like(st)

        S = st[...]
        for c in range(nc):
            rows = slice(c * C, (c + 1) * C)
            ck = _hg_chunk(q_ref[rows, :].astype(F32), f_ref[rows, :].astype(F32), lb_ref[...])
            v = v_ref[rows, :]
            g = g_ref[rows, :].astype(F32)
            ss_ref[0, c] = S
            o = _dot(ck["att"], v) + _dot(ck["qin"], S, NT)
            S = ck["decay"] * S + _dot(v, ck["kout"], TN)
            rn = lax.rsqrt(jnp.mean(o * o, axis=-1, keepdims=True) + EPS)
            o_ref[rows, :] = (o * rn * gn_ref[...] * (g * _sigmoid(g))).astype(BF16)
        st[...] = S

    col = lambda base: (lambda h, i: (i, base + h))
    return pl.pallas_call(
        body, name=name, grid=(HG_HEADS, T // tt),
        in_specs=[pl.BlockSpec((tt, Dh), col(8)), pl.BlockSpec((tt, Dh), col(12)), pl.BlockSpec((tt, Dh), col(16)),
                  pl.BlockSpec((tt, Dh), col(20)), pl.BlockSpec((1, Dh), lambda h, i: (0, h)),
                  pl.BlockSpec((1, Dh), lambda h, i: (0, 0))],
        out_specs=[pl.BlockSpec((tt, Dh), lambda h, i: (i, h)),
                   pl.BlockSpec((1, nc, Dh, Dh), lambda h, i: (h, i, 0, 0))],
        out_shape=[jax.ShapeDtypeStruct((T, HG_HEADS * Dh), BF16),
                   jax.ShapeDtypeStruct((HG_HEADS, T // C, Dh, Dh), F32)],
        scratch_shapes=[pltpu.VMEM((Dh, Dh), F32)],
        compiler_params=_cp(2),
    )(z, z, z, z, lb, gn)


def _hgrn_bwd(z, ss, dmix, lb, gn, name="hgrn_bwd", tt=256):
    T = z.shape[0]
    C = HG_CHUNK
    nc = tt // C
    nt = T // tt
    Dh = HG_D

    def body(q_ref, f_ref, v_ref, g_ref, ss_ref, dm_ref, lb_ref, gn_ref, dq_ref, df_ref, dv_ref, dg_ref, dlb_ref, dgn_ref, dst):
        @pl.when(pl.program_id(1) == 0)
        def _():
            dst[...] = jnp.zeros_like(dst)
            dlb_ref[...] = jnp.zeros_like(dlb_ref)
            dgn_ref[...] = jnp.zeros_like(dgn_ref)

        dS = dst[...]
        lbv = lb_ref[...]
        gnv = gn_ref[...]
        rowc = lax.broadcasted_iota(jnp.int32, (C, Dh), 0)
        for c in reversed(range(nc)):
            rows = slice(c * C, (c + 1) * C)
            q = q_ref[rows, :].astype(F32)
            ck = _hg_chunk(q, f_ref[rows, :].astype(F32), lbv)
            v = v_ref[rows, :]
            g = g_ref[rows, :].astype(F32)
            S = ss_ref[0, c]
            o = _dot(ck["att"], v) + _dot(ck["qin"], S, NT)
            rn = lax.rsqrt(jnp.mean(o * o, axis=-1, keepdims=True) + EPS)
            on = o * rn
            dout = dm_ref[rows, :].astype(F32)
            sg, dsg = _silu_and_grad(g)
            d_ong = dout * sg
            dgn_ref[...] += _rsum8(d_ong * on)
            dg_ref[rows, :] = (dout * on * gnv * dsg).astype(BF16)
            don = d_ong * gnv
            do = rn * (don - on * jnp.mean(don * on, axis=-1, keepdims=True))
            dv_ref[rows, :] = (_dot(ck["att"], do, TN) + _dot(ck["kout"], dS, NT)).astype(BF16)
            datt = jnp.where(ck["causal"], _dot(do, v, NT), 0.0)
            dqt = _dot(datt, ck["kt"])
            dkt = _dot(datt, ck["qt"], TN)
            dqin = _dot(do, S)
            dkout = _dot(v, dS)
            ddecay = jnp.sum(dS * S, axis=0, keepdims=True)
            dS = _dot(do, ck["qin"], TN) + ck["decay"] * dS
            b, bm, bl = ck["b"], ck["bm"], ck["bl"]
            dqf = dqt * jnp.exp(b - bm) + dqin * jnp.exp(b)
            dk = dkt * jnp.exp(bm - b) + dkout * jnp.exp(bl - b)
            kk = dkout * ck["kout"]
            db = dqt * ck["qt"] - dkt * ck["kt"] + dqin * ck["qin"] - kk
            dbl = jnp.sum(kk, axis=0, keepdims=True) + ddecay * ck["decay"]
            db = db + jnp.where(rowc == C - 1, dbl, 0.0)
            dlogf = _tri_matmul(ck["anti"].astype(BF16), db)
            dfv = dlogf / ck["f"] - dk
            sig = ck["sig"]
            df_ref[rows, :] = (dfv * (1.0 - lbv) * sig * (1.0 - sig)).astype(BF16)
            dlb_ref[...] += _rsum8(dfv * (1.0 - sig))
            sq = ck["sq"]
            dq_ref[rows, :] = (dqf * (sq * (1.0 + q * (1.0 - sq)))).astype(BF16)
        dst[...] = dS

    rev = lambda i: nt - 1 - i
    col = lambda base: (lambda h, i: (rev(i), base + h))
    out_tok = pl.BlockSpec((tt, Dh), lambda h, i: (rev(i), h))
    acc = pl.BlockSpec((SUBLANES, Dh), lambda h, i: (h, 0))
    tok_shape = jax.ShapeDtypeStruct((T, HG_HEADS * Dh), BF16)
    acc_shape = jax.ShapeDtypeStruct((HG_HEADS * SUBLANES, Dh), F32)
    return pl.pallas_call(
        body, name=name, grid=(HG_HEADS, nt),
        in_specs=[pl.BlockSpec((tt, Dh), col(8)), pl.BlockSpec((tt, Dh), col(12)), pl.BlockSpec((tt, Dh), col(16)),
                  pl.BlockSpec((tt, Dh), col(20)),
                  pl.BlockSpec((1, nc, Dh, Dh), lambda h, i: (h, rev(i), 0, 0)),
                  pl.BlockSpec((tt, Dh), lambda h, i: (rev(i), 4 + h)),
                  pl.BlockSpec((1, Dh), lambda h, i: (0, h)), pl.BlockSpec((1, Dh), lambda h, i: (0, 0))],
        out_specs=[out_tok, out_tok, out_tok, out_tok, acc, acc],
        out_shape=[tok_shape, tok_shape, tok_shape, tok_shape, acc_shape, acc_shape],
        scratch_shapes=[pltpu.VMEM((Dh, Dh), F32)],
        compiler_params=_cp(2),
    )(z, z, z, z, ss, dmix, lb, gn)


def _sgu_core(p, lg_ref, lb_ref, wsc_ref, bsb_ref):
    Wd = D_MODEL
    G = SGU_CHUNK
    zz = _gelu(p)
    u = zz[:, :Wd]
    v = zz[:, Wd:]
    vc = v - jnp.mean(v, axis=-1, keepdims=True)
    rstd = lax.rsqrt(jnp.mean(vc * vc, axis=-1, keepdims=True) + EPS)
    vhat = vc * rstd
    vn = vhat * lg_ref[...] + lb_ref[...]
    svs = []
    for gi in range(SGU_G):
        svs.append(jnp.dot(wsc_ref[gi], vn[:, gi * G:(gi + 1) * G].astype(BF16), preferred_element_type=F32) + bsb_ref[gi])
    return u, vhat, rstd, vn, jnp.concatenate(svs, axis=1)


def _sgu_fwd(p1, lg, lbias, wsc, bsb, name="sgu_fwd", tt=512):
    T = p1.shape[0]
    Wd = D_MODEL
    C = SGU_CHUNK

    def body(p_ref, lg_ref, lb_ref, wsc_ref, bsb_ref, s_ref):
        for c in range(tt // C):
            rows = slice(c * C, (c + 1) * C)
            u, _, _, _, sv = _sgu_core(p_ref[rows, :].astype(F32), lg_ref, lb_ref, wsc_ref, bsb_ref)
            s_ref[rows, :] = (u * sv).astype(BF16)

    return pl.pallas_call(
        body, name=name, grid=(T // tt,),
        in_specs=[pl.BlockSpec((tt, 2 * Wd), lambda i: (i, 0)), _resident((1, Wd)), _resident((1, Wd)),
                  _resident((SGU_G, C, C)), _resident((SGU_G, C, C))],
        out_specs=pl.BlockSpec((tt, Wd), lambda i: (i, 0)),
        out_shape=jax.ShapeDtypeStruct((T, Wd), BF16),
        compiler_params=_cp(1),
    )(p1, lg, lbias, wsc, bsb)


def _sgu_bwd(p1, ds, lg, lbias, wsc, wsct, bsb, name="sgu_bwd", tt=512):
    T = p1.shape[0]
    Wd = D_MODEL
    C = SGU_CHUNK

    def body(p_ref, ds_ref, lg_ref, lb_ref, wsc_ref, wsct_ref, bsb_ref, dp_ref, dws_ref, dbs_ref, dlg_ref, dlb_ref, dbin_ref):
        @pl.when(pl.program_id(0) == 0)
        def _():
            dws_ref[...] = jnp.zeros_like(dws_ref)
            dbs_ref[...] = jnp.zeros_like(dbs_ref)
            dlg_ref[...] = jnp.zeros_like(dlg_ref)
            dlb_ref[...] = jnp.zeros_like(dlb_ref)
            dbin_ref[...] = jnp.zeros_like(dbin_ref)

        for c in range(tt // C):
            rows = slice(c * C, (c + 1) * C)
            p = p_ref[rows, :].astype(F32)
            u, vhat, rstd, vn, sv = _sgu_core(p, lg_ref, lb_ref, wsc_ref, bsb_ref)
            dsc = ds_ref[rows, :].astype(F32)
            du = dsc * sv
            dsv = dsc * u
            dvns = []
            for gi in range(SGU_G):
                cs = slice(gi * C, (gi + 1) * C)
                dsv_g = dsv[:, cs]
                dvns.append(jnp.dot(wsct_ref[gi], dsv_g.astype(BF16), preferred_element_type=F32))
                dws_ref[gi] += _dot(dsv_g, vn[:, cs], NT)
                dbs_ref[gi] += dsv_g
            dvn = jnp.concatenate(dvns, axis=1)
            dlg_ref[...] += _rsum8(dvn * vhat)
            dlb_ref[...] += _rsum8(dvn)
            dvh = dvn * lg_ref[...]
            dv = rstd * (dvh - jnp.mean(dvh, axis=-1, keepdims=True) - vhat * jnp.mean(dvh * vhat, axis=-1, keepdims=True))
            dp = jnp.concatenate([du, dv], axis=1) * _gelu_grad(p)
            dbin_ref[...] += _rsum8(dp)
            dp_ref[rows, :] = dp.astype(BF16)

    full3 = pl.BlockSpec((SGU_G, C, C), lambda i: (0, 0, 0))
    return pl.pallas_call(
        body, name=name, grid=(T // tt,),
        in_specs=[pl.BlockSpec((tt, 2 * Wd), lambda i: (i, 0)), pl.BlockSpec((tt, Wd), lambda i: (i, 0)),
                  _resident((1, Wd)), _resident((1, Wd)), _resident((SGU_G, C, C)), _resident((SGU_G, C, C)),
                  _resident((SGU_G, C, C))],
        out_specs=[pl.BlockSpec((tt, 2 * Wd), lambda i: (i, 0)), full3, full3,
                   pl.BlockSpec((SUBLANES, Wd), lambda i: (0, 0)), pl.BlockSpec((SUBLANES, Wd), lambda i: (0, 0)),
                   pl.BlockSpec((SUBLANES, 2 * Wd), lambda i: (0, 0))],
        out_shape=[jax.ShapeDtypeStruct((T, 2 * Wd), BF16), jax.ShapeDtypeStruct((SGU_G, C, C), F32),
                   jax.ShapeDtypeStruct((SGU_G, C, C), F32), jax.ShapeDtypeStruct((SUBLANES, Wd), F32),
                   jax.ShapeDtypeStruct((SUBLANES, Wd), F32), jax.ShapeDtypeStruct((SUBLANES, 2 * Wd), F32)],
        compiler_params=_cp(1),
    )(p1, ds, lg, lbias, wsc, wsct, bsb)


def _pad_rows(w, rows=SUBLANES):
    return jnp.pad(w, ((0, rows - w.shape[0]), (0, 0)))


def _block_diag(w):
    n, b, _ = w.shape
    return (w[:, :, None, :] * jnp.eye(n, dtype=w.dtype)[:, None, :, None]).reshape(n * b, n * b)


def _diag_blocks(m, n):
    b = m.shape[0] // n
    m4 = m.reshape(n, b, n, b)
    return jnp.stack([m4[k, :, k, :] for k in range(n)], axis=0)


def _piece_major(dw):
    if dw.ndim == 2:
        K, N = dw.shape
        return dw.reshape(N_CHIPS, 2, K // (2 * N_CHIPS), N)
    _, K, ns = dw.shape
    return dw.reshape(N_CHIPS, 2, K // 2, ns)


def _ffn_fwd(h, g, w_up, cw, cb, w_down, tag):
    hn, gu = _norm_mm(h, g, w_up, jnp.zeros((1, w_up.shape[1]), F32), name=f"ffn_up_{tag}")
    a = _ffn_act(gu, cw, cb, name=f"ffn_act_{tag}")
    out = _mm(a, w_down, h, F32, name=f"ffn_down_{tag}")
    return out, (hn, gu, a)


def _ffn_bwd(dh, h, g, saved, w_up, cw, cb, w_down, tag):
    hn, gu, a = saved
    da = _mm(dh, w_down, None, BF16, name=f"ffn_da_{tag}", transpose_w=True)
    dwd = _mm_tn(a, dh, name=f"ffn_dwd_{tag}", tk=D_FF // 2)
    dgu, dc = _ffn_act_bwd(gu, da, cw, cb, name=f"ffn_actb_{tag}")
    dhin, dg8 = _mm_normbwd(dgu, w_up, h, g, dh, name=f"ffn_dh_{tag}")
    dwu = _mm_tn(hn, dgu, name=f"ffn_dwu_{tag}", tk=256, col_shards=N_CHIPS)
    dcs = dc.sum(axis=1)
    return dhin, dg8.sum(axis=0), dwu, dcs[0:3], dcs[3], dwd


REDUCE_GROUPS = {"g1": [("ffn_w_up", 1), ("ffn_w_down", 1), ("od_w_out", 0), ("od_w_in", 0)],
                 "g2": [("ffn_w_up", 0), ("ffn_w_down", 0)],
                 "g3": [("ev_w_out", 0), ("ev_w_in", 0)]}


def _local_step(x, tgt, p, start_reduce=None):
    row = lambda v: v.reshape(1, -1)
    grads = {}

    lower = jax.nn.softmax(p["hg_lb_logits"], axis=0)
    lb0 = row(lower[0])
    ev_cw = _pad_rows(p["ev_conv_w"][0])
    ev_cb = row(p["ev_conv_b"][0])
    wa = _block_diag(p["ev_gate_a_w"][0]).astype(BF16)
    wx = _block_diag(p["ev_gate_x_w"][0]).astype(BF16)
    ba, bx, lam = row(p["ev_gate_a_b"][0]), row(p["ev_gate_x_b"][0]), row(p["ev_lru_lambda"][0])
    gn = row(p["ev_hg_norm"][0])
    tril = jnp.tril(jnp.ones((SGU_CHUNK, SGU_CHUNK), F32))
    wsc = (p["od_w_s"][0] * tril).astype(BF16)
    bsb = jnp.broadcast_to(p["od_b_s"][0][:, :, None], (SGU_G, SGU_CHUNK, SGU_CHUNK)).astype(F32)
    ffn_cw = [_pad_rows(p["ffn_conv_w"][l]) for l in range(2)]
    ffn_cb = [row(p["ffn_conv_b"][l]) for l in range(2)]
    ev_w_in, ev_w_out = p["ev_w_in"][0], p["ev_w_out"][0]
    od_w_in, od_w_out = p["od_w_in"][0], p["od_w_out"][0]
    w_up = [p["ffn_w_up"][l] for l in range(2)]
    w_down = [p["ffn_w_down"][l] for l in range(2)]
    nm = [row(p["norm_mix"][l]) for l in range(2)]
    nf = [row(p["norm_ffn"][l]) for l in range(2)]

    h0 = x
    hn0, z0 = _norm_mm(h0, nm[0], ev_w_in, jnp.zeros((1, ev_w_in.shape[1]), F32), name="ev_in")
    out_a, hseq = _lru_fwd(z0, ev_cw, ev_cb, wa, ba, wx, bx, lam)
    out_b, ss = _hgrn_fwd(z0, lb0, gn)
    mix0 = jnp.concatenate([out_a, out_b], axis=1)
    h1 = _mm(mix0, ev_w_out, h0, F32, name="ev_out")
    h2, ffn0 = _ffn_fwd(h1, nf[0], w_up[0], ffn_cw[0], ffn_cb[0], w_down[0], "l0")
    hn1, p1 = _norm_mm(h2, nm[1], od_w_in, row(p["od_b_in"][0]), name="od_in")
    s1 = _sgu_fwd(p1, row(p["od_ln_g"][0]), row(p["od_ln_b"][0]), wsc, bsb)
    h3 = _mm(s1, od_w_out, h2, F32, name="od_out")
    h4, ffn1 = _ffn_fwd(h3, nf[1], w_up[1], ffn_cw[1], ffn_cb[1], w_down[1], "l1")
    dh4, dgf8, sq8 = _final_loss(h4, row(p["norm_final"]), tgt)
    grads["norm_final"] = dgf8.sum(axis=0)

    big = {}
    dh3, dnf1, dwu1, dcw1, dcb1, dwd1 = _ffn_bwd(dh4, h3, nf[1], ffn1, w_up[1], ffn_cw[1], ffn_cb[1], w_down[1], "l1")
    big["ffn_w_up", 1], big["ffn_w_down", 1] = _piece_major(dwu1), _piece_major(dwd1)
    ds1 = _mm(dh3, od_w_out, None, BF16, name="od_ds", transpose_w=True)
    big["od_w_out", 0] = _piece_major(_mm_tn(s1, dh3, name="od_dwo", tk=512))
    wsct = jnp.swapaxes(wsc, 1, 2)
    dp1, dws, dbs, dlg8, dlb8, dbin8 = _sgu_bwd(p1, ds1, row(p["od_ln_g"][0]), row(p["od_ln_b"][0]), wsc, wsct, bsb)
    grads["od_w_s"] = (dws * tril)[None]
    grads["od_b_s"] = dbs.sum(axis=-1)[None]
    grads["od_ln_g"] = dlg8.sum(axis=0)[None]
    grads["od_ln_b"] = dlb8.sum(axis=0)[None]
    grads["od_b_in"] = dbin8.sum(axis=0)[None]
    dh2, dnm1 = _mm_normbwd(dp1, od_w_in, h2, nm[1], dh3, name="od_dh")
    big["od_w_in", 0] = _piece_major(_mm_tn(hn1, dp1, name="od_dwi", tk=512, col_shards=N_CHIPS))
    if start_reduce is not None:
        token = start_reduce("g1", [big[key] for key in REDUCE_GROUPS["g1"]])
        ffn_cb[0] = ffn_cb[0] + token[0:1, 0:1]

    dh1, dnf0, dwu0, dcw0, dcb0, dwd0 = _ffn_bwd(dh2, h1, nf[0], ffn0, w_up[0], ffn_cw[0], ffn_cb[0], w_down[0], "l0")
    big["ffn_w_up", 0], big["ffn_w_down", 0] = _piece_major(dwu0), _piece_major(dwd0)
    if start_reduce is not None:
        token = start_reduce("g2", [big[key] for key in REDUCE_GROUPS["g2"]])
        ev_cb = ev_cb + token[0:1, 0:1]
    dmix = _mm(dh1, ev_w_out, None, BF16, name="ev_dmix", transpose_w=True)
    big["ev_w_out", 0] = _piece_major(_mm_tn(mix0, dh1, name="ev_dwo", tk=512))
    dz01, dc5, dwa, dwx, dvec = _lru_bwd(z0, hseq, dmix, ev_cw, ev_cb, wa, wa.T, ba, wx, wx.T, bx, lam)
    dq, df, dv, dg, dlb32, dgn32 = _hgrn_bwd(z0, ss, dmix, lb0, gn)
    dz0 = jnp.concatenate([dz01, dq, df, dv, dg], axis=1)
    grad_x, dnm0 = _mm_normbwd(dz0, ev_w_in, h0, nm[0], dh1, name="ev_dh")
    big["ev_w_in", 0] = _piece_major(_mm_tn(hn0, dz0, name="ev_dwi", tk=512, col_shards=N_CHIPS))

    dc5s = dc5.sum(axis=1)
    grads["ev_conv_w"] = dc5s[0:4][None]
    grads["ev_conv_b"] = dc5s[4][None]
    grads["ev_gate_a_w"] = _diag_blocks(dwa, LRU_BLOCKS)[None]
    grads["ev_gate_x_w"] = _diag_blocks(dwx, LRU_BLOCKS)[None]
    dvs = dvec.sum(axis=1)
    grads["ev_gate_a_b"] = dvs[0][None]
    grads["ev_gate_x_b"] = dvs[1][None]
    grads["ev_lru_lambda"] = (dvs[2] * (-jax.nn.sigmoid(-p["ev_lru_lambda"][0])))[None]
    dlb = dlb32.reshape(HG_HEADS, SUBLANES, HG_D).sum(axis=1).reshape(-1)
    grads["hg_lb_logits"] = dlb[None, :] * lower[0][None, :] * (jnp.eye(3, dtype=F32)[0][:, None] - lower)
    grads["ev_hg_norm"] = dgn32.reshape(HG_HEADS, SUBLANES, HG_D).sum(axis=(0, 1))[None]
    grads["norm_mix"] = jnp.stack([dnm0.sum(axis=0), dnm1.sum(axis=0)])
    grads["norm_ffn"] = jnp.stack([dnf0, dnf1])
    grads["ffn_conv_w"] = jnp.stack([dcw0, dcw1])
    grads["ffn_conv_b"] = jnp.stack([dcb0, dcb1])
    return sq8, grad_x, grads, big


MESH = pl.DeviceIdType.MESH
ANY = pl.BlockSpec(memory_space=pl.ANY)
N_CHIPS = 4
N_DEV = 8

SH_BIG = {"ev_w_in": 2, "ev_w_out": 1, "od_w_in": 2, "od_w_out": 1, "ffn_w_up": 2, "ffn_w_down": 1}
SH_SMALL = {"ev_conv_w": 2, "od_b_in": 1, "od_ln_g": 1, "od_ln_b": 1, "ffn_conv_w": 2}
REP = ["norm_mix", "norm_ffn", "norm_final", "ev_conv_b", "ev_gate_a_w", "ev_gate_a_b", "ev_gate_x_w", "ev_gate_x_b",
       "ev_lru_lambda", "hg_lb_logits", "ev_hg_norm", "od_w_s", "od_b_s", "ffn_conv_b"]
WEIGHTS = ["norm_mix", "norm_ffn", "norm_final", "ev_w_in", "ev_conv_w", "ev_conv_b", "ev_gate_a_w", "ev_gate_a_b", "ev_gate_x_w",
           "ev_gate_x_b", "ev_lru_lambda", "hg_lb_logits", "ev_hg_norm", "ev_w_out", "od_w_in", "od_b_in", "od_ln_g", "od_ln_b",
           "od_w_s", "od_b_s", "od_w_out", "ffn_w_up", "ffn_conv_w", "ffn_conv_b", "ffn_w_down"]


def _rows(n_elems, mult=SUBLANES):
    r = -(-n_elems // LANES)
    return -(-r // mult) * mult


def _pack(arrs, rows, dtype):
    flat = jnp.concatenate([a.reshape(-1).astype(dtype) for a in arrs])
    return jnp.pad(flat, (0, rows * LANES - flat.shape[0])).reshape(rows, LANES)


def _unpack(flat2d, shapes):
    flat = flat2d.reshape(-1)
    out, off = [], 0
    for s in shapes:
        n = 1
        for d in s:
            n *= d
        out.append(flat[off:off + n].reshape(s))
        off += n
    return out


def _mesh_pos():
    return lax.axis_index("x"), lax.axis_index("y"), lax.axis_index("c")


def _other_chips(x, y):
    return [(1 - x, y), (x, 1 - y), (1 - x, 1 - y)]


def _half_rows(n):
    return lambda r, c: r.at[0, pl.ds(c * (n // 2), n // 2), :]


GATHER_BIG = {
    "ev_w_in": ((1024, 3072), _half_rows(1024), lambda o, k, c: o.at[pl.ds(c * 512, 512), pl.ds(k * 768, 768)]),
    "ev_w_out": ((1024, 1024), _half_rows(256), lambda o, k, c: o.at[pl.ds(k * 256 + c * 128, 128), :]),
    "od_w_in": ((1024, 2048), _half_rows(1024), lambda o, k, c: o.at[pl.ds(c * 512, 512), pl.ds(k * 512, 512)]),
    "od_w_out": ((1024, 1024), _half_rows(256), lambda o, k, c: o.at[pl.ds(k * 256 + c * 128, 128), :]),
    "ffn_w_up": ((2, 1024, 2 * D_FF), lambda r, c: r.at[c], lambda o, k, c: o.at[c, :, pl.ds(k * (2 * D_FF // 4), 2 * D_FF // 4)]),
    "ffn_w_down": ((2, D_FF, 1024), lambda r, c: r.at[c], lambda o, k, c: o.at[c, pl.ds(k * (D_FF // 4), D_FF // 4), :]),
}


def _gather_weights(big, small):
    nb = len(big)
    descs = list(GATHER_BIG.values())
    rs = small.shape[0]

    def body(*refs):
        ins, s_ref = refs[:nb], refs[nb]
        outs, os_ref = refs[nb + 1:2 * nb + 1], refs[2 * nb + 1]
        ici_send, ici_recv, d2d_send, d2d_recv, loc_sems = refs[2 * nb + 2:2 * nb + 7]
        vbufs = refs[2 * nb + 7:]
        x, y, c = _mesh_pos()
        k = 2 * x + y
        chips = _other_chips(x, y)
        sib = (x, y, 1 - c)

        def remote(src, dst, ssem, rsem, to):
            return pltpu.make_async_remote_copy(src_ref=src, dst_ref=dst, send_sem=ssem, recv_sem=rsem, device_id=to,
                                                device_id_type=MESH)

        stage = [pltpu.make_async_copy(ins[t], vbufs[t], loc_sems.at[2 * t]) for t in range(nb)]
        stage.append(pltpu.make_async_copy(s_ref, vbufs[nb], loc_sems.at[2 * nb]))
        for cp in stage:
            cp.start()
        sends = []
        for t, (_, src, dst) in enumerate(descs):
            for j, (px, py) in enumerate(chips):
                sends.append(remote(src(ins[t], c), dst(outs[t], k, c), ici_send.at[3 * t + j], ici_recv.at[3 * t + j], (px, py, c)))
        for j, (px, py) in enumerate(chips):
            sends.append(remote(s_ref, os_ref.at[k], ici_send.at[3 * nb + j], ici_recv.at[3 * nb + j], (px, py, c)))
        for cp in sends:
            cp.start()
        for cp in stage:
            cp.wait()
        local = []
        for t, (_, src, dst) in enumerate(descs):
            for cc in (0, 1):
                local.append(pltpu.make_async_copy(src(vbufs[t], cc), dst(outs[t], k, cc), loc_sems.at[2 * t + cc]))
        local.append(pltpu.make_async_copy(vbufs[nb], os_ref.at[k], loc_sems.at[2 * nb]))
        for cp in local:
            cp.start()
        for t, (_, src, dst) in enumerate(descs):
            for j, (px, py) in enumerate(chips):
                got = dst(outs[t], 2 * px + py, c)
                remote(got, got, ici_send.at[3 * t + j], ici_recv.at[3 * t + j], (px, py, c)).wait_recv()
                fwd = remote(got, got, d2d_send.at[3 * t + j], d2d_recv.at[3 * t + j], sib)
                fwd.start()
                sends.append(fwd)
        for j, (px, py) in enumerate(chips):
            remote(s_ref, os_ref.at[2 * px + py], ici_send.at[3 * nb + j], ici_recv.at[3 * nb + j], (px, py, c)).wait_recv()
        for t, (_, src, dst) in enumerate(descs):
            for j, (px, py) in enumerate(chips):
                theirs = dst(outs[t], 2 * px + py, 1 - c)
                remote(theirs, theirs, d2d_send.at[3 * t + j], d2d_recv.at[3 * t + j], sib).wait_recv()
        for cp in sends:
            cp.wait_send()
        for cp in local:
            cp.wait()

    out_shape = [jax.ShapeDtypeStruct(d[0], BF16) for d in descs] + [jax.ShapeDtypeStruct((N_CHIPS, rs, LANES), small.dtype)]
    return pl.pallas_call(
        body, name="gather_weights", in_specs=[ANY] * (nb + 1), out_specs=[ANY] * (nb + 1), out_shape=out_shape,
        scratch_shapes=[pltpu.SemaphoreType.DMA((3 * nb + 3,)), pltpu.SemaphoreType.DMA((3 * nb + 3,)),
                        pltpu.SemaphoreType.DMA((3 * nb,)), pltpu.SemaphoreType.DMA((3 * nb,)),
                        pltpu.SemaphoreType.DMA((2 * nb + 1,))]
        + [pltpu.VMEM(b.shape, b.dtype) for b in big] + [pltpu.VMEM(small.shape, small.dtype)],
        compiler_params=pltpu.CompilerParams(vmem_limit_bytes=VMEM_LIMIT),
    )(*big, small)


def _remote(src, dst, ssem, rsem, to):
    return pltpu.make_async_remote_copy(src_ref=src, dst_ref=dst, send_sem=ssem, recv_sem=rsem, device_id=to, device_id_type=MESH)


def _rs_send_sibling(gs, tag):
    n = len(gs)
    counts = [N_CHIPS if g.ndim == 4 else 1 for g in gs]
    ns = sum(counts)

    def body(*refs):
        ins, outs = refs[:n], refs[n:2 * n]
        send_sems, recv_sems = refs[2 * n:]
        x, y, c = _mesh_pos()
        cps, s = [], 0
        for t in range(n):
            if counts[t] == 1:
                cps.append(_remote(ins[t].at[1 - c], outs[t], send_sems.at[s], recv_sems.at[s], (x, y, 1 - c)))
                s += 1
            else:
                for k in range(N_CHIPS):
                    cps.append(_remote(ins[t].at[k, 1 - c], outs[t].at[k], send_sems.at[s], recv_sems.at[s], (x, y, 1 - c)))
                    s += 1
        for cp in cps:
            cp.start()
        for cp in cps:
            cp.wait()

    out_shape = [jax.ShapeDtypeStruct(g.shape[:1] + g.shape[2:] if g.ndim == 4 else g.shape[1:], g.dtype) for g in gs]
    return pl.pallas_call(
        body, name=f"rs_send_sibling_{tag}", in_specs=[ANY] * n, out_specs=[ANY] * n, out_shape=out_shape,
        scratch_shapes=[pltpu.SemaphoreType.DMA((ns,)), pltpu.SemaphoreType.DMA((ns,))],
    )(*gs)


def _add_piece(g, recv, c, name):
    P, Q = g.shape[-2:]

    def body(c_ref, g_ref, r_ref, o_ref):
        o_ref[...] = g_ref[...].reshape(o_ref.shape) + r_ref[...]

    if g.ndim == 4:
        grid = (N_CHIPS,)
        in_specs = [pl.BlockSpec((1, 1, P, Q), lambda k, c_ref: (k, c_ref[0], 0, 0)), pl.BlockSpec((1, P, Q), lambda k, c_ref: (k, 0, 0))]
        out_spec = pl.BlockSpec((1, P, Q), lambda k, c_ref: (k, 0, 0))
    else:
        grid = (1,)
        in_specs = [pl.BlockSpec((1, P, Q), lambda k, c_ref: (c_ref[0], 0, 0)), pl.BlockSpec((P, Q), lambda k, c_ref: (0, 0))]
        out_spec = pl.BlockSpec((P, Q), lambda k, c_ref: (0, 0))
    return pl.pallas_call(
        body, name=name,
        grid_spec=pltpu.PrefetchScalarGridSpec(num_scalar_prefetch=1, grid=grid, in_specs=in_specs, out_specs=out_spec),
        out_shape=jax.ShapeDtypeStruct(recv.shape, g.dtype),
        compiler_params=_cp(1),
    )(c, g, recv)


def _rs_send_chips(hs, tag):
    n = len(hs)

    def body(*refs):
        ins, outs = refs[:n], refs[n:2 * n]
        send_sems, recv_sems = refs[2 * n:]
        x, y, c = _mesh_pos()
        k = 2 * x + y
        chips = _other_chips(x, y)
        piece = lambda t, kk: ins[t].at[kk] if hs[t].ndim == 3 else ins[t]
        sends = [_remote(piece(t, 2 * px + py), outs[t].at[k], send_sems.at[3 * t + j], recv_sems.at[3 * t + j], (px, py, c))
                 for t in range(n) for j, (px, py) in enumerate(chips)]
        for cp in sends:
            cp.start()
        for t in range(n):
            for j, (px, py) in enumerate(chips):
                _remote(piece(t, k), outs[t].at[2 * px + py], send_sems.at[3 * t + j], recv_sems.at[3 * t + j], (px, py, c)).wait_recv()
        for cp in sends:
            cp.wait_send()

    out_shape = [jax.ShapeDtypeStruct((N_CHIPS,) + h.shape[-2:], h.dtype) for h in hs]
    return pl.pallas_call(
        body, name=f"rs_send_chips_{tag}", in_specs=[ANY] * n, out_specs=[ANY] * n, out_shape=out_shape,
        scratch_shapes=[pltpu.SemaphoreType.DMA((3 * n,)), pltpu.SemaphoreType.DMA((3 * n,))],
    )(*hs)


HBM = pl.BlockSpec(memory_space=pltpu.HBM)
SEM = pl.BlockSpec(memory_space=pltpu.SEMAPHORE)
DATAFLOW = pltpu.SideEffectType.DATAFLOW_SIDE_EFFECTING


def _chips_start(hs, tag):
    n = len(hs)
    lands = [pltpu.with_memory_space_constraint(lax.empty((N_CHIPS,) + h.shape[-2:], h.dtype), pltpu.HBM) for h in hs]

    def body(*refs):
        ins, lnd = refs[:n], refs[n:2 * n]
        send_sems, recv_sems, token = refs[2 * n], refs[2 * n + 1], refs[-1]
        x, y, c = _mesh_pos()
        k = 2 * x + y
        piece = lambda t, kk: ins[t].at[kk] if hs[t].ndim == 3 else ins[t]
        for t in range(n):
            for j, (px, py) in enumerate(_other_chips(x, y)):
                _remote(piece(t, 2 * px + py), lnd[t].at[k], send_sems.at[3 * t + j], recv_sems.at[3 * t + j], (px, py, c)).start()
        token[...] = jnp.zeros_like(token)

    out = pl.pallas_call(
        body, name=f"chips_start_{tag}",
        out_shape=(pltpu.SemaphoreType.DMA((3 * n,)), pltpu.SemaphoreType.DMA((3 * n,)),
                   *[pltpu.HBM(h.shape, h.dtype) for h in hs], *[pltpu.HBM(l.shape, l.dtype) for l in lands],
                   jax.ShapeDtypeStruct((SUBLANES, LANES), F32)),
        in_specs=[HBM] * (2 * n), out_specs=(SEM, SEM, *[HBM] * (2 * n), pl.BlockSpec(memory_space=pltpu.VMEM)),
        input_output_aliases={i: 2 + i for i in range(2 * n)},
        compiler_params=pltpu.CompilerParams(has_side_effects=DATAFLOW),
    )(*[pltpu.with_memory_space_constraint(h, pltpu.HBM) for h in hs], *lands)
    return out[0], out[1], list(out[2:2 + n]), list(out[2 + n:2 + 2 * n]), out[-1]


def _chips_wait(send_sems, recv_sems, hs, lands, after, tag):
    n = len(hs)

    def body(*refs):
        ins, lnd = refs[:n], refs[n:2 * n]
        ssem, rsem = refs[2 * n], refs[2 * n + 1]
        x, y, c = _mesh_pos()
        k = 2 * x + y
        piece = lambda t, kk: ins[t].at[kk] if hs[t].ndim == 3 else ins[t]
        for t in range(n):
            for j, (px, py) in enumerate(_other_chips(x, y)):
                cp = _remote(piece(t, k), lnd[t].at[2 * px + py], ssem.at[3 * t + j], rsem.at[3 * t + j], (px, py, c))
                cp.wait_send()
                cp.wait_recv()

    out = pl.pallas_call(
        body, name=f"chips_wait_{tag}",
        out_shape=(*[pltpu.HBM(h.shape, h.dtype) for h in hs], *[pltpu.HBM(l.shape, l.dtype) for l in lands]),
        in_specs=[HBM] * (2 * n) + [SEM, SEM, ANY], out_specs=tuple([HBM] * (2 * n)),
        input_output_aliases={i: i for i in range(2 * n)},
        compiler_params=pltpu.CompilerParams(has_side_effects=DATAFLOW),
    )(*hs, *lands, send_sems, recv_sems, after)
    return list(out[:n]), list(out[n:])


def _add_chips(p, own, kc, name):
    _, P, Q = p.shape
    tr = P
    while N_CHIPS * tr * Q * 4 > 6 * 1024 * 1024 and tr % 16 == 0:
        tr //= 2
    sharded = own.ndim == 3

    def body(kc_ref, p_ref, own_ref, o_ref):
        k = kc_ref[0]
        mine = own_ref[...].reshape(tr, Q)
        v = [jnp.where(k == j, mine, p_ref[j]) for j in range(N_CHIPS)]
        o_ref[0] = ((v[0] + v[1]) + v[2]) + v[3]

    own_spec = (pl.BlockSpec((1, tr, Q), lambda i, kc_ref: (kc_ref[0], i, 0)) if sharded
                else pl.BlockSpec((tr, Q), lambda i, kc_ref: (i, 0)))
    return pl.pallas_call(
        body, name=name,
        grid_spec=pltpu.PrefetchScalarGridSpec(
            num_scalar_prefetch=1, grid=(P // tr,),
            in_specs=[pl.BlockSpec((N_CHIPS, tr, Q), lambda i, kc_ref: (0, i, 0)), own_spec],
            out_specs=pl.BlockSpec((1, tr, Q), lambda i, kc_ref: (kc_ref[1], i, 0))),
        out_shape=jax.ShapeDtypeStruct((2, P, Q), p.dtype),
        compiler_params=_cp(1),
    )(kc, p, own)


def _rs_share(fs, tag):
    n = len(fs)

    def body(*refs):
        outs = refs[n:2 * n]
        send_sems, recv_sems = refs[2 * n:]
        x, y, c = _mesh_pos()
        sends = [_remote(outs[t].at[c], outs[t].at[c], send_sems.at[t], recv_sems.at[t], (x, y, 1 - c)) for t in range(n)]
        for cp in sends:
            cp.start()
        for t in range(n):
            _remote(outs[t].at[c], outs[t].at[1 - c], send_sems.at[t], recv_sems.at[t], (x, y, 1 - c)).wait_recv()
        for cp in sends:
            cp.wait_send()

    return pl.pallas_call(
        body, name=f"rs_share_{tag}", in_specs=[ANY] * n, out_specs=[ANY] * n,
        out_shape=[jax.ShapeDtypeStruct(f.shape, f.dtype) for f in fs], input_output_aliases={t: t for t in range(n)},
        scratch_shapes=[pltpu.SemaphoreType.DMA((n,)), pltpu.SemaphoreType.DMA((n,))],
    )(*fs)


def _reduce_start(gs, kc, tag):
    from_sibling = _rs_send_sibling(gs, tag)
    chip_sums = [_add_piece(g, r, kc[1:], name=f"add_piece_{tag}_{t}") for t, (g, r) in enumerate(zip(gs, from_sibling))]
    send_sems, recv_sems, chip_sums, lands, token = _chips_start(chip_sums, tag)
    return (send_sems, recv_sems, chip_sums, lands, tag), token


def _reduce_finish(states, kc, after):
    mine = []
    for send_sems, recv_sems, chip_sums, lands, tag in states:
        chip_sums, from_chips = _chips_wait(send_sems, recv_sems, chip_sums, lands, after, tag)
        mine += [_add_chips(p, h, kc, name=f"add_chips_{tag}_{t}") for t, (p, h) in enumerate(zip(from_chips, chip_sums))]
    return _rs_share(mine, "all")


def _adamw(w, g, m, v, name):
    R, C = w.shape
    tr = R
    for cand in (512, 256, 128, 64, 32, 16, 8):
        if R % cand == 0 and cand * C * 4 <= 2 * 1024 * 1024:
            tr = cand
            break
    c1 = 1.0 / (1.0 - ADAM_B1 ** ADAM_STEP)
    c2 = 1.0 / (1.0 - ADAM_B2 ** ADAM_STEP)

    def body(w_ref, g_ref, m_ref, v_ref, d_ref, mo_ref, vo_ref):
        gv = g_ref[...]
        mn = ADAM_B1 * m_ref[...] + (1.0 - ADAM_B1) * gv
        vn = ADAM_B2 * v_ref[...] + (1.0 - ADAM_B2) * (gv * gv)
        mo_ref[...] = mn
        vo_ref[...] = vn
        d_ref[...] = -ADAM_LR * ((mn * c1) / (jnp.sqrt(vn * c2) + ADAM_EPS) + ADAM_WD * w_ref[...])

    spec = pl.BlockSpec((tr, C), lambda i: (i, 0))
    shp = jax.ShapeDtypeStruct((R, C), F32)
    return pl.pallas_call(body, name=name, grid=(R // tr,), in_specs=[spec] * 4, out_specs=[spec] * 3, out_shape=[shp] * 3,
                          compiler_params=_cp(1))(w, g, m, v)


BIG_GRADS = [("ffn_w_up", 1), ("ffn_w_down", 1), ("od_w_out", 0), ("od_w_in", 0), ("ffn_w_up", 0), ("ffn_w_down", 0),
             ("ev_w_out", 0), ("ev_w_in", 0)]


def _step(a):
    x, y, c = _mesh_pos()
    kc = jnp.stack([2 * x + y, c]).astype(jnp.int32)

    rs = _rows(sum(a[n].size for n in SH_SMALL))
    *full, gs = _gather_weights([a[n].astype(BF16) for n in GATHER_BIG], _pack([a[n] for n in SH_SMALL], rs, F32))
    p = {n: a[n] for n in REP}
    for n, w in zip(GATHER_BIG, full):
        p[n] = w if w.ndim == 3 else w[None]
    parts = [_unpack(gs[k], [a[n].shape for n in SH_SMALL]) for k in range(N_CHIPS)]
    for i, n in enumerate(SH_SMALL):
        p[n] = jnp.concatenate([parts[k][i] for k in range(N_CHIPS)], axis=SH_SMALL[n])

    states = []

    def start_reduce(tag, gs):
        state, token = _reduce_start(gs, kc, tag)
        states.append(state)
        return token

    sq8, grad_x, grads, big = _local_step(a["x"][0], a["loss_target"][0], p, start_reduce)
    loss = lax.psum(0.5 / D_MODEL * jnp.sum(sq8), ("x", "y", "c"))

    r_s = _rows(sum(a[n].size for n in SH_SMALL), 2 * SUBLANES) // 2
    small_pieces = []
    for k in range(N_CHIPS):
        pieces = [lax.slice_in_dim(grads[n], k * a[n].shape[ax], (k + 1) * a[n].shape[ax], axis=ax) for n, ax in SH_SMALL.items()]
        small_pieces.append(_pack(pieces, 2 * r_s, F32).reshape(2, r_s, LANES))
    g_small = jnp.stack(small_pieces)
    r_r = _rows(sum(a[n].size for n in REP), 2 * SUBLANES) // 2
    g_rep = _pack([grads[n] for n in REP], 2 * r_r, F32).reshape(2, r_r, LANES)
    start_reduce("g3", [big[key] for key in REDUCE_GROUPS["g3"]] + [g_small, g_rep])
    reduced = _reduce_finish(states, kc, grad_x)
    red = dict(zip([key for tag in ("g1", "g2", "g3") for key in REDUCE_GROUPS[tag]], reduced))
    gfin = {}
    for n in ("ev_w_in", "ev_w_out", "od_w_in", "od_w_out"):
        gfin[n] = red[n, 0].reshape(a[n].shape)
    for n in ("ffn_w_up", "ffn_w_down"):
        gfin[n] = jnp.stack([red[n, l].reshape(a[n].shape[1:]) for l in range(2)])
    gfin.update(zip(SH_SMALL, _unpack(reduced[-2], [a[n].shape for n in SH_SMALL])))
    gfin.update(zip(REP, _unpack(reduced[-1], [a[n].shape for n in REP])))

    out = {"loss": loss, "grad_x": grad_x[None]}
    small_names = list(SH_SMALL) + REP
    for n in SH_BIG:
        shp = a[n].shape
        two_d = lambda t: t.reshape(-1, shp[-1])
        d, mo, vo = _adamw(two_d(a[n]), two_d(gfin[n]), two_d(a["m_" + n]), two_d(a["v_" + n]), name=f"adamw_{n}")
        out["delta_" + n], out["new_m_" + n], out["new_v_" + n] = d.reshape(shp), mo.reshape(shp), vo.reshape(shp)
    r_small = _rows(sum(a[n].size for n in small_names), 512)
    packs = [_pack([src(n) for n in small_names], r_small, F32)
             for src in (lambda n: a[n], lambda n: gfin[n], lambda n: a["m_" + n], lambda n: a["v_" + n])]
    d, mo, vo = _adamw(*packs, name="adamw_small")
    shapes = [a[n].shape for n in small_names]
    for n, dd, mm, vv in zip(small_names, _unpack(d, shapes), _unpack(mo, shapes), _unpack(vo, shapes)):
        out["delta_" + n], out["new_m_" + n], out["new_v_" + n] = dd, mm, vv
    for n in WEIGHTS:
        out["grad_" + n] = gfin[n]
    return out


def kernel(x, norm_mix, norm_ffn, norm_final, ev_w_in, ev_conv_w, ev_conv_b, ev_gate_a_w, ev_gate_a_b, ev_gate_x_w, ev_gate_x_b, ev_lru_lambda, hg_lb_logits, ev_hg_norm, ev_w_out, od_w_in, od_b_in, od_ln_g, od_ln_b, od_w_s, od_b_s, od_w_out, ffn_w_up, ffn_conv_w, ffn_conv_b, ffn_w_down, loss_target, m_norm_mix, m_norm_ffn, m_norm_final, m_ev_w_in, m_ev_conv_w, m_ev_conv_b, m_ev_gate_a_w, m_ev_gate_a_b, m_ev_gate_x_w, m_ev_gate_x_b, m_ev_lru_lambda, m_hg_lb_logits, m_ev_hg_norm, m_ev_w_out, m_od_w_in, m_od_b_in, m_od_ln_g, m_od_ln_b, m_od_w_s, m_od_b_s, m_od_w_out, m_ffn_w_up, m_ffn_conv_w, m_ffn_conv_b, m_ffn_w_down, v_norm_mix, v_norm_ffn, v_norm_final, v_ev_w_in, v_ev_conv_w, v_ev_conv_b, v_ev_gate_a_w, v_ev_gate_a_b, v_ev_gate_x_w, v_ev_gate_x_b, v_ev_lru_lambda, v_hg_lb_logits, v_ev_hg_norm, v_ev_w_out, v_od_w_in, v_od_b_in, v_od_ln_g, v_od_ln_b, v_od_w_s, v_od_b_s, v_od_w_out, v_ffn_w_up, v_ffn_conv_w, v_ffn_conv_b, v_ffn_w_down):
    vals = (x, norm_mix, norm_ffn, norm_final, ev_w_in, ev_conv_w, ev_conv_b, ev_gate_a_w, ev_gate_a_b, ev_gate_x_w, ev_gate_x_b, ev_lru_lambda, hg_lb_logits, ev_hg_norm, ev_w_out, od_w_in, od_b_in, od_ln_g, od_ln_b, od_w_s, od_b_s, od_w_out, ffn_w_up, ffn_conv_w, ffn_conv_b, ffn_w_down, loss_target, m_norm_mix, m_norm_ffn, m_norm_final, m_ev_w_in, m_ev_conv_w, m_ev_conv_b, m_ev_gate_a_w, m_ev_gate_a_b, m_ev_gate_x_w, m_ev_gate_x_b, m_ev_lru_lambda, m_hg_lb_logits, m_ev_hg_norm, m_ev_w_out, m_od_w_in, m_od_b_in, m_od_ln_g, m_od_ln_b, m_od_w_s, m_od_b_s, m_od_w_out, m_ffn_w_up, m_ffn_conv_w, m_ffn_conv_b, m_ffn_w_down, v_norm_mix, v_norm_ffn, v_norm_final, v_ev_w_in, v_ev_conv_w, v_ev_conv_b, v_ev_gate_a_w, v_ev_gate_a_b, v_ev_gate_x_w, v_ev_gate_x_b, v_ev_lru_lambda, v_hg_lb_logits, v_ev_hg_norm, v_ev_w_out, v_od_w_in, v_od_b_in, v_od_ln_g, v_od_ln_b, v_od_w_s, v_od_b_s, v_od_w_out, v_ffn_w_up, v_ffn_conv_w, v_ffn_conv_b, v_ffn_w_down)
    names = ["x"] + WEIGHTS + ["loss_target"] + ["m_" + n for n in WEIGHTS] + ["v_" + n for n in WEIGHTS]
    out = _step(dict(zip(names, vals)))
    return (out["loss"], out["grad_x"], *[out["grad_" + n] for n in WEIGHTS], *[out["delta_" + n] for n in WEIGHTS],
            *[out["new_m_" + n] for n in WEIGHTS], *[out["new_v_" + n] for n in WEIGHTS])
```

```python
import functools

import jax
import jax.numpy as jnp
from jax import lax
from jax.experimental import pallas as pl
from jax.experimental.pallas import tpu as pltpu

F32 = jnp.float32
BF16 = jnp.bfloat16

EPS = 1e-6
D_MODEL = 1024
LRU_W = 512
LRU_BLOCKS = 8
LRU_C = 8.0
HG_HEADS = 4
HG_D = 128
HG_CHUNK = 64
SGU_G = 8
SGU_CHUNK = 128
D_FF = 2816
ADAM_LR, ADAM_B1, ADAM_B2, ADAM_EPS, ADAM_WD, ADAM_STEP = 0.001, 0.9, 0.999, 1e-08, 0.01, 10

V7X_VMEM_BYTES = 64 * 1024 * 1024
VMEM_LIMIT = V7X_VMEM_BYTES - 8 * 1024 * 1024
SUBLANES = 8
LANES = 128
BF16_ROWS = 16

GELU_C0 = 0.7978845608028654
GELU_C1 = 0.044715

NN = (((1,), (0,)), ((), ()))
NT = (((1,), (1,)), ((), ()))
TN = (((0,), (0,)), ((), ()))


def _dot(a, b, dims=NN):
    return lax.dot_general(a.astype(BF16), b.astype(BF16), dims, preferred_element_type=F32)


def _cp(n_grid):
    return pltpu.CompilerParams(dimension_semantics=("arbitrary",) * n_grid, vmem_limit_bytes=VMEM_LIMIT)


def _chunk(n, cap):
    best = LANES
    for c in range(LANES, cap + 1, LANES):
        if n % c == 0:
            best = c
    return best


def _resident(shape):
    nd = len(shape)
    return pl.BlockSpec(shape, lambda *_: (0,) * nd, pipeline_mode=pl.Buffered(1))


def _rsum8(x):
    r, c = x.shape
    return x.reshape(r // SUBLANES, SUBLANES, c).sum(axis=0)


def _sigmoid(x):
    return 1.0 / (1.0 + jnp.exp(-x))


def _gelu(x):
    return 0.5 * x * (1.0 + jnp.tanh(GELU_C0 * (x + GELU_C1 * x * x * x)))


def _gelu_grad(x):
    t = jnp.tanh(GELU_C0 * (x + GELU_C1 * x * x * x))
    return 0.5 * (1.0 + t) + 0.5 * x * (1.0 - t * t) * GELU_C0 * (1.0 + 3.0 * GELU_C1 * x * x)


def _silu_and_grad(x):
    s = _sigmoid(x)
    return x * s, s * (1.0 + x * (1.0 - s))


def _shift_rows(e, j):
    n = e.shape[0]
    return e if j % n == 0 else pltpu.roll(e, j % n, 0)


def _norm_mm(h, g, w, b, name, tt=512):
    T, D = h.shape
    N = w.shape[1]
    cn = _chunk(N, 512)

    def body(h_ref, g_ref, w_ref, b_ref, hn_ref, z_ref):
        x = h_ref[...]
        r = lax.rsqrt(jnp.mean(x * x, axis=-1, keepdims=True) + EPS)
        hn = (x * r * g_ref[...]).astype(BF16)
        hn_ref[...] = hn
        for j in range(0, N, cn):
            acc = jnp.dot(hn, w_ref[:, j:j + cn], preferred_element_type=F32) + b_ref[:, j:j + cn]
            z_ref[:, j:j + cn] = acc.astype(BF16)

    return pl.pallas_call(
        body, name=name, grid=(T // tt,),
        in_specs=[pl.BlockSpec((tt, D), lambda i: (i, 0)), _resident((1, D)), _resident((D, N)), _resident((1, N))],
        out_specs=[pl.BlockSpec((tt, D), lambda i: (i, 0)), pl.BlockSpec((tt, N), lambda i: (i, 0))],
        out_shape=[jax.ShapeDtypeStruct((T, D), BF16), jax.ShapeDtypeStruct((T, N), BF16)],
        compiler_params=_cp(1),
    )(h, g, w, b)


def _mm(a, w, res, out_dtype, name, tt=512, transpose_w=False):
    T, K = a.shape
    N = w.shape[0] if transpose_w else w.shape[1]
    cn = _chunk(N, 512)
    has_res = res is not None

    def body(*refs):
        a_ref, w_ref = refs[0], refs[1]
        res_ref = refs[2] if has_res else None
        o_ref = refs[-1]
        av = a_ref[...].astype(BF16)
        for j in range(0, N, cn):
            if transpose_w:
                acc = lax.dot_general(av, w_ref[j:j + cn, :], NT, preferred_element_type=F32)
            else:
                acc = jnp.dot(av, w_ref[:, j:j + cn], preferred_element_type=F32)
            if has_res:
                acc = acc + res_ref[:, j:j + cn]
            o_ref[:, j:j + cn] = acc.astype(out_dtype)

    in_specs = [pl.BlockSpec((tt, K), lambda i: (i, 0)), _resident(w.shape)]
    args = [a, w]
    if has_res:
        in_specs.append(pl.BlockSpec((tt, N), lambda i: (i, 0)))
        args.append(res)
    return pl.pallas_call(
        body, name=name, grid=(T // tt,), in_specs=in_specs,
        out_specs=pl.BlockSpec((tt, N), lambda i: (i, 0)),
        out_shape=jax.ShapeDtypeStruct((T, N), out_dtype),
        compiler_params=_cp(1),
    )(*args)


def _mm_tn(a, b, name, tk, tt=512, col_shards=1):
    T, K = a.shape
    N = b.shape[1]
    ns = N // col_shards

    def body(a_ref, b_ref, o_ref):
        acc = lax.dot_general(a_ref[...].astype(BF16), b_ref[...].astype(BF16), TN, preferred_element_type=F32)
        first = pl.program_id(1) == 0
        if col_shards == 1:
            prev = jnp.where(first, 0.0, o_ref[...])
            o_ref[...] = prev + acc
        else:
            for s in range(col_shards):
                prev = jnp.where(first, 0.0, o_ref[s])
                o_ref[s] = prev + acc[:, s * ns:(s + 1) * ns]

    if col_shards == 1:
        out_spec = pl.BlockSpec((tk, N), lambda k, t: (k, 0))
        out_shape = jax.ShapeDtypeStruct((K, N), F32)
    else:
        out_spec = pl.BlockSpec((col_shards, tk, ns), lambda k, t: (0, k, 0))
        out_shape = jax.ShapeDtypeStruct((col_shards, K, ns), F32)
    return pl.pallas_call(
        body, name=name, grid=(K // tk, T // tt),
        in_specs=[pl.BlockSpec((tt, tk), lambda k, t: (t, k)), pl.BlockSpec((tt, N), lambda k, t: (t, 0))],
        out_specs=out_spec, out_shape=out_shape,
        compiler_params=_cp(2),
    )(a, b)


def _mm_normbwd(dz, w, x, g, dres, name, tt=512):
    T, N = dz.shape
    D = w.shape[0]

    def body(dz_ref, wt_ref, x_ref, g_ref, dres_ref, dx_ref, dg_ref):
        @pl.when(pl.program_id(0) == 0)
        def _():
            dg_ref[...] = jnp.zeros_like(dg_ref)

        dy = lax.dot_general(dz_ref[...], wt_ref[...], NT, preferred_element_type=F32)
        x = x_ref[...]
        r = lax.rsqrt(jnp.mean(x * x, axis=-1, keepdims=True) + EPS)
        xn = x * r
        dg_ref[...] += _rsum8(dy * xn)
        dxn = dy * g_ref[...]
        dx_ref[...] = dres_ref[...] + r * (dxn - xn * jnp.mean(dxn * xn, axis=-1, keepdims=True))

    return pl.pallas_call(
        body, name=name, grid=(T // tt,),
        in_specs=[pl.BlockSpec((tt, N), lambda i: (i, 0)), _resident((D, N)), pl.BlockSpec((tt, D), lambda i: (i, 0)),
                  _resident((1, D)), pl.BlockSpec((tt, D), lambda i: (i, 0))],
        out_specs=[pl.BlockSpec((tt, D), lambda i: (i, 0)), pl.BlockSpec((SUBLANES, D), lambda i: (0, 0))],
        out_shape=[jax.ShapeDtypeStruct((T, D), F32), jax.ShapeDtypeStruct((SUBLANES, D), F32)],
        compiler_params=_cp(1),
    )(dz, w, x, g, dres)


def _final_loss(h, g, tgt, name="final_loss", tt=512):
    T, D = h.shape

    def body(h_ref, g_ref, t_ref, dh_ref, dg_ref, sq_ref):
        @pl.when(pl.program_id(0) == 0)
        def _():
            dg_ref[...] = jnp.zeros_like(dg_ref)
            sq_ref[...] = jnp.zeros_like(sq_ref)

        x = h_ref[...]
        r = lax.rsqrt(jnp.mean(x * x, axis=-1, keepdims=True) + EPS)
        xn = x * r
        gv = g_ref[...]
        diff = xn * gv - t_ref[...]
        sq_ref[...] += _rsum8(diff * diff)
        dy = diff * (1.0 / D)
        dg_ref[...] += _rsum8(dy * xn)
        dxn = dy * gv
        dh_ref[...] = r * (dxn - xn * jnp.mean(dxn * xn, axis=-1, keepdims=True))

    return pl.pallas_call(
        body, name=name, grid=(T // tt,),
        in_specs=[pl.BlockSpec((tt, D), lambda i: (i, 0)), _resident((1, D)), pl.BlockSpec((tt, D), lambda i: (i, 0))],
        out_specs=[pl.BlockSpec((tt, D), lambda i: (i, 0)), pl.BlockSpec((SUBLANES, D), lambda i: (0, 0)),
                   pl.BlockSpec((SUBLANES, D), lambda i: (0, 0))],
        out_shape=[jax.ShapeDtypeStruct((T, D), F32), jax.ShapeDtypeStruct((SUBLANES, D), F32),
                   jax.ShapeDtypeStruct((SUBLANES, D), F32)],
        compiler_params=_cp(1),
    )(h, g, tgt)


def _ffn_act(gu, cw, cb, name, tt=512):
    T = gu.shape[0]
    F = gu.shape[1] // 2
    cc = _chunk(F, 256)
    hb = tt // BF16_ROWS

    def body(gu_ref, halo_ref, cw_ref, cb_ref, a_ref):
        first = pl.program_id(0) == 0
        for c0 in range(0, F, cc):
            cs = slice(c0, c0 + cc)
            x = gu_ref[:, cs].astype(F32)
            halo = jnp.where(first, 0.0, halo_ref[:, cs].astype(F32))
            e = jnp.concatenate([halo, x], axis=0)
            gc = (cb_ref[:, cs] + cw_ref[0:1, cs] * _shift_rows(e, 2)[BF16_ROWS:] + cw_ref[1:2, cs] * _shift_rows(e, 1)[BF16_ROWS:]
                  + cw_ref[2:3, cs] * x)
            up = gu_ref[:, F + c0:F + c0 + cc].astype(F32)
            a_ref[:, cs] = (gc * _sigmoid(gc) * up).astype(BF16)

    return pl.pallas_call(
        body, name=name, grid=(T // tt,),
        in_specs=[pl.BlockSpec((tt, 2 * F), lambda i: (i, 0)),
                  pl.BlockSpec((BF16_ROWS, F), lambda i: (jnp.maximum(i * hb - 1, 0), 0)),
                  _resident((SUBLANES, F)), _resident((1, F))],
        out_specs=pl.BlockSpec((tt, F), lambda i: (i, 0)),
        out_shape=jax.ShapeDtypeStruct((T, F), BF16),
        compiler_params=_cp(1),
    )(gu, gu, cw, cb)


def _ffn_act_bwd(gu, da, cw, cb, name, tt=512):
    T = gu.shape[0]
    F = gu.shape[1] // 2
    cc = _chunk(F, 256)
    hb = tt // BF16_ROWS
    last_hb = T // BF16_ROWS - 1
    nt = T // tt

    def body(gu_ref, gprev_ref, gunext_ref, da_ref, danext_ref, cw_ref, cb_ref, dgu_ref, dc_ref):
        i = pl.program_id(0)

        @pl.when(i == 0)
        def _():
            dc_ref[...] = jnp.zeros_like(dc_ref)

        n = tt + BF16_ROWS
        for c0 in range(0, F, cc):
            cs = slice(c0, c0 + cc)
            us = slice(F + c0, F + c0 + cc)
            g = gu_ref[:, cs].astype(F32)
            gp = jnp.where(i == 0, 0.0, gprev_ref[:, cs].astype(F32))
            ge = jnp.concatenate([gp, g, gunext_ref[:, cs].astype(F32)], axis=0)
            g1 = _shift_rows(ge, 1)[BF16_ROWS:]
            g2 = _shift_rows(ge, 2)[BF16_ROWS:]
            gc = cb_ref[:, cs] + cw_ref[0:1, cs] * g2 + cw_ref[1:2, cs] * g1 + cw_ref[2:3, cs] * ge[BF16_ROWS:]
            up = jnp.concatenate([gu_ref[:, us].astype(F32), gunext_ref[:, us].astype(F32)], axis=0)
            dan = jnp.where(i == nt - 1, 0.0, danext_ref[:, cs].astype(F32))
            dae = jnp.concatenate([da_ref[:, cs].astype(F32), dan], axis=0)
            s, ds = _silu_and_grad(gc)
            dgc = dae * up * ds
            dgu_ref[:, us] = (dae * s)[:tt].astype(BF16)
            dgate = cw_ref[2:3, cs] * dgc + cw_ref[1:2, cs] * _shift_rows(dgc, n - 1) + cw_ref[0:1, cs] * _shift_rows(dgc, n - 2)
            dgu_ref[:, cs] = dgate[:tt].astype(BF16)
            dm = dgc[:tt]
            dc_ref[0, :, cs] += _rsum8(dm * g2[:tt])
            dc_ref[1, :, cs] += _rsum8(dm * g1[:tt])
            dc_ref[2, :, cs] += _rsum8(dm * g)
            dc_ref[3, :, cs] += _rsum8(dm)

    return pl.pallas_call(
        body, name=name, grid=(nt,),
        in_specs=[pl.BlockSpec((tt, 2 * F), lambda i: (i, 0)),
                  pl.BlockSpec((BF16_ROWS, F), lambda i: (jnp.maximum(i * hb - 1, 0), 0)),
                  pl.BlockSpec((BF16_ROWS, 2 * F), lambda i: (jnp.minimum((i + 1) * hb, last_hb), 0)),
                  pl.BlockSpec((tt, F), lambda i: (i, 0)),
                  pl.BlockSpec((BF16_ROWS, F), lambda i: (jnp.minimum((i + 1) * hb, last_hb), 0)),
                  _resident((SUBLANES, F)), _resident((1, F))],
        out_specs=[pl.BlockSpec((tt, 2 * F), lambda i: (i, 0)), pl.BlockSpec((4, SUBLANES, F), lambda i: (0, 0, 0))],
        out_shape=[jax.ShapeDtypeStruct((T, 2 * F), BF16), jax.ShapeDtypeStruct((4, SUBLANES, F), F32)],
        compiler_params=_cp(1),
    )(gu, gu, gu, da, da, cw, cb)


def _softplus_neg(lam):
    x = -lam
    y = jnp.exp(-jnp.abs(x))
    l1p = jnp.where(y < 0.01, y * (1.0 - y * (0.5 - y * (1.0 / 3.0))), jnp.log(1.0 + y))
    return jnp.maximum(x, 0.0) + l1p


def _lru_gates(xc, wa_ref, ba_ref, wx_ref, bx_ref, sp):
    xcb = xc.astype(BF16)
    r = _sigmoid(jnp.dot(xcb, wa_ref[...], preferred_element_type=F32) + ba_ref[...])
    gi = _sigmoid(jnp.dot(xcb, wx_ref[...], preferred_element_type=F32) + bx_ref[...])
    log_a = -LRU_C * r * sp
    a = jnp.exp(log_a)
    x2 = 2.0 * log_a
    series = -x2 * (1.0 + x2 * 0.5 * (1.0 + x2 * (1.0 / 3.0) * (1.0 + x2 * 0.25 * (1.0 + x2 * 0.2))))
    om = jnp.where(x2 > -0.125, series, 1.0 - a * a)
    return r, gi, a, jnp.sqrt(om)


def _lru_conv(xr, halo, cw_ref, cb_ref):
    e = jnp.concatenate([halo, xr], axis=0)
    x1 = _shift_rows(e, 1)[BF16_ROWS:]
    x2 = _shift_rows(e, 2)[BF16_ROWS:]
    x3 = _shift_rows(e, 3)[BF16_ROWS:]
    xc = cb_ref[...] + cw_ref[0:1, :] * x3 + cw_ref[1:2, :] * x2 + cw_ref[2:3, :] * x1 + cw_ref[3:4, :] * xr
    return xc, x1, x2, x3


def _lru_fwd(z, cw, cb, wa, ba, wx, bx, lam, name="lru_fwd", tt=256):
    T = z.shape[0]
    W = LRU_W
    hb = tt // BF16_ROWS
    ng = tt // SUBLANES

    def body(z_ref, halo_ref, cw_ref, cb_ref, wa_ref, ba_ref, wx_ref, bx_ref, lam_ref, oa_ref, h_ref, a_s, u_s, hc):
        i = pl.program_id(0)

        @pl.when(i == 0)
        def _():
            hc[...] = jnp.zeros_like(hc)

        xr = z_ref[:, W:2 * W].astype(F32)
        halo = jnp.where(i == 0, 0.0, halo_ref[...].astype(F32))
        xc, _, _, _ = _lru_conv(xr, halo, cw_ref, cb_ref)
        sp = _softplus_neg(lam_ref[...])
        r, gi, a, mult = _lru_gates(xc, wa_ref, ba_ref, wx_ref, bx_ref, sp)
        a_s[...] = a
        u_s[...] = mult * gi * xc
        row = lax.broadcasted_iota(jnp.int32, (SUBLANES, W), 0)

        def step(j, hprev):
            r0 = pl.multiple_of(j * SUBLANES, SUBLANES)
            A = a_s[pl.ds(r0, SUBLANES), :]
            U = u_s[pl.ds(r0, SUBLANES), :]
            for k in (1, 2, 4):
                m = row >= k
                U = jnp.where(m, A * pltpu.roll(U, k, 0) + U, U)
                A = jnp.where(m, A * pltpu.roll(A, k, 0), A)
            H = U + A * hprev
            h_ref[pl.ds(r0, SUBLANES), :] = H
            return jnp.broadcast_to(H[SUBLANES - 1:SUBLANES, :], (SUBLANES, W))

        hc[...] = lax.fori_loop(0, ng, step, hc[...])
        oa_ref[...] = (_gelu(z_ref[:, 0:W].astype(F32)) * h_ref[...]).astype(BF16)

    return pl.pallas_call(
        body, name=name, grid=(T // tt,),
        in_specs=[pl.BlockSpec((tt, 2 * W), lambda i: (i, 0)),
                  pl.BlockSpec((BF16_ROWS, W), lambda i: (jnp.maximum(i * hb - 1, 0), 1)),
                  _resident((SUBLANES, W)), _resident((1, W)), _resident((W, W)), _resident((1, W)),
                  _resident((W, W)), _resident((1, W)), _resident((1, W))],
        out_specs=[pl.BlockSpec((tt, W), lambda i: (i, 0)), pl.BlockSpec((tt, W), lambda i: (i, 0))],
        out_shape=[jax.ShapeDtypeStruct((T, W), BF16), jax.ShapeDtypeStruct((T, W), F32)],
        scratch_shapes=[pltpu.VMEM((tt, W), F32), pltpu.VMEM((tt, W), F32), pltpu.VMEM((SUBLANES, W), F32)],
        compiler_params=_cp(1),
    )(z, z, cw, cb, wa, ba, wx, bx, lam)


def _lru_bwd(z, hseq, dmix, cw, cb, wa, wat, ba, wx, wxt, bx, lam, name="lru_bwd", tt=256):
    T = z.shape[0]
    W = LRU_W
    nt = T // tt
    hb = tt // BF16_ROWS
    sb = tt // SUBLANES
    ng = tt // SUBLANES

    def body(z_ref, halo_ref, h_ref, hprev_ref, dm_ref, cw_ref, cb_ref, wa_ref, wat_ref, ba_ref, wx_ref, wxt_ref, bx_ref,
             lam_ref, dz_ref, dc_ref, dwa_ref, dwx_ref, dv_ref, c_s, d_s, g_s, gc, an, dxn):
        i = pl.program_id(0)
        ti = nt - 1 - i

        @pl.when(i == 0)
        def _():
            dc_ref[...] = jnp.zeros_like(dc_ref)
            dwa_ref[...] = jnp.zeros_like(dwa_ref)
            dwx_ref[...] = jnp.zeros_like(dwx_ref)
            dv_ref[...] = jnp.zeros_like(dv_ref)
            gc[...] = jnp.zeros_like(gc)
            an[...] = jnp.zeros_like(an)
            dxn[...] = jnp.zeros_like(dxn)

        xr = z_ref[:, W:2 * W].astype(F32)
        yg = z_ref[:, 0:W].astype(F32)
        halo = jnp.where(ti == 0, 0.0, halo_ref[...].astype(F32))
        xc, x1, x2, x3 = _lru_conv(xr, halo, cw_ref, cb_ref)
        sp = _softplus_neg(lam_ref[...])
        r, gi, a, mult = _lru_gates(xc, wa_ref, ba_ref, wx_ref, bx_ref, sp)
        h = h_ref[...]
        hp = jnp.where(ti == 0, 0.0, hprev_ref[...])
        hm1 = _shift_rows(jnp.concatenate([hp, h], axis=0), 1)[SUBLANES:]
        dout = dm_ref[...].astype(F32)
        d_s[...] = dout * _gelu(yg)
        dz_ref[:, 0:W] = (dout * h * _gelu_grad(yg)).astype(BF16)
        c_s[...] = _shift_rows(jnp.concatenate([a, an[...]], axis=0), tt + SUBLANES - 1)[:tt]
        an[...] = a[0:SUBLANES, :]
        row = lax.broadcasted_iota(jnp.int32, (SUBLANES, W), 0)

        def step(j, gnext):
            r0 = pl.multiple_of((ng - 1 - j) * SUBLANES, SUBLANES)
            C = c_s[pl.ds(r0, SUBLANES), :]
            G = d_s[pl.ds(r0, SUBLANES), :]
            for k in (1, 2, 4):
                m = row < SUBLANES - k
                G = jnp.where(m, G + C * pltpu.roll(G, SUBLANES - k, 0), G)
                C = jnp.where(m, C * pltpu.roll(C, SUBLANES - k, 0), C)
            G = G + C * gnext
            g_s[pl.ds(r0, SUBLANES), :] = G
            return jnp.broadcast_to(G[0:1, :], (SUBLANES, W))

        gc[...] = lax.fori_loop(0, ng, step, gc[...])
        du = g_s[...]
        da = du * hm1
        dgi = du * mult * xc
        dxc = du * mult * gi
        dmult = du * gi * xc
        dlog_a = da * a - dmult * (a * a) / mult
        dr = dlog_a * (-LRU_C * sp)
        dv_ref[2] += _rsum8(dlog_a * (-LRU_C * r))
        dpr = (dr * r * (1.0 - r)).astype(BF16)
        dpi = (dgi * gi * (1.0 - gi)).astype(BF16)
        dv_ref[0] += _rsum8(dpr.astype(F32))
        dv_ref[1] += _rsum8(dpi.astype(F32))
        xcb = xc.astype(BF16)
        dwa_ref[...] += lax.dot_general(xcb, dpr, TN, preferred_element_type=F32)
        dwx_ref[...] += lax.dot_general(xcb, dpi, TN, preferred_element_type=F32)
        dxc = dxc + jnp.dot(dpr, wat_ref[...], preferred_element_type=F32) + jnp.dot(dpi, wxt_ref[...], preferred_element_type=F32)
        n = tt + BF16_ROWS
        de = jnp.concatenate([dxc, dxn[...]], axis=0)
        dxr = (cw_ref[3:4, :] * dxc + cw_ref[2:3, :] * _shift_rows(de, n - 1)[:tt] + cw_ref[1:2, :] * _shift_rows(de, n - 2)[:tt]
               + cw_ref[0:1, :] * _shift_rows(de, n - 3)[:tt])
        dxn[...] = dxc[0:BF16_ROWS, :]
        dz_ref[:, W:2 * W] = dxr.astype(BF16)
        dc_ref[0] += _rsum8(dxc * x3)
        dc_ref[1] += _rsum8(dxc * x2)
        dc_ref[2] += _rsum8(dxc * x1)
        dc_ref[3] += _rsum8(dxc * xr)
        dc_ref[4] += _rsum8(dxc)

    rev = lambda i: nt - 1 - i
    return pl.pallas_call(
        body, name=name, grid=(nt,),
        in_specs=[pl.BlockSpec((tt, 2 * W), lambda i: (rev(i), 0)),
                  pl.BlockSpec((BF16_ROWS, W), lambda i: (jnp.maximum(rev(i) * hb - 1, 0), 1)),
                  pl.BlockSpec((tt, W), lambda i: (rev(i), 0)),
                  pl.BlockSpec((SUBLANES, W), lambda i: (jnp.maximum(rev(i) * sb - 1, 0), 0)),
                  pl.BlockSpec((tt, W), lambda i: (rev(i), 0)),
                  _resident((SUBLANES, W)), _resident((1, W)), _resident((W, W)), _resident((W, W)), _resident((1, W)),
                  _resident((W, W)), _resident((W, W)), _resident((1, W)), _resident((1, W))],
        out_specs=[pl.BlockSpec((tt, 2 * W), lambda i: (rev(i), 0)),
                   pl.BlockSpec((5, SUBLANES, W), lambda i: (0, 0, 0)),
                   pl.BlockSpec((W, W), lambda i: (0, 0)), pl.BlockSpec((W, W), lambda i: (0, 0)),
                   pl.BlockSpec((3, SUBLANES, W), lambda i: (0, 0, 0))],
        out_shape=[jax.ShapeDtypeStruct((T, 2 * W), BF16), jax.ShapeDtypeStruct((5, SUBLANES, W), F32),
                   jax.ShapeDtypeStruct((W, W), F32), jax.ShapeDtypeStruct((W, W), F32),
                   jax.ShapeDtypeStruct((3, SUBLANES, W), F32)],
        scratch_shapes=[pltpu.VMEM((tt, W), F32), pltpu.VMEM((tt, W), F32), pltpu.VMEM((tt, W), F32),
                        pltpu.VMEM((SUBLANES, W), F32), pltpu.VMEM((SUBLANES, W), F32), pltpu.VMEM((BF16_ROWS, W), F32)],
        compiler_params=_cp(1),
    )(z, z, hseq, hseq, dmix, cw, cb, wa, wat, ba, wx, wxt, bx, lam)


def _split3(x):
    hi = x.astype(BF16)
    r1 = x - hi.astype(F32)
    mid = r1.astype(BF16)
    lo = (r1 - mid.astype(F32)).astype(BF16)
    return hi, mid, lo


def _tri_matmul(tri, x):
    hi, mid, lo = _split3(x)
    return (jnp.dot(tri, hi, preferred_element_type=F32) + jnp.dot(tri, mid, preferred_element_type=F32)
            + jnp.dot(tri, lo, preferred_element_type=F32))


def _hg_chunk(q, fl, lb):
    C = q.shape[0]
    ri = lax.broadcasted_iota(jnp.int32, (C, C), 0)
    ci = lax.broadcasted_iota(jnp.int32, (C, C), 1)
    causal = ri >= ci
    sig = _sigmoid(fl)
    f = lb + (1.0 - lb) * sig
    k = 1.0 - f
    sq = _sigmoid(q)
    qf = q * sq
    b = _tri_matmul(causal.astype(BF16), jnp.log(f))
    bm = b[C // 2 - 1:C // 2, :]
    bl = b[C - 1:C, :]
    qt = qf * jnp.exp(b - bm)
    kt = k * jnp.exp(bm - b)
    qin = qf * jnp.exp(b)
    kout = k * jnp.exp(bl - b)
    att = jnp.where(causal, _dot(qt, kt, NT), 0.0)
    return dict(sig=sig, f=f, k=k, sq=sq, qf=qf, b=b, bm=bm, bl=bl, qt=qt, kt=kt, qin=qin, kout=kout, att=att,
                causal=causal, anti=ri <= ci, decay=jnp.exp(bl))


def _hgrn_fwd(z, lb, gn, name="hgrn_fwd", tt=256):
    T = z.shape[0]
    C = HG_CHUNK
    nc = tt // C
    Dh = HG_D

    def body(q_ref, f_ref, v_ref, g_ref, lb_ref, gn_ref, o_ref, ss_ref, st):
        @pl.when(pl.program_id(1) == 0)
        def _():
            st[...] = jnp.zeros_like(st)

        S = st[...]
        for c in range(nc):
            rows = slice(c * C, (c + 1) * C)
            ck = _hg_chunk(q_ref[rows, :].astype(F32), f_ref[rows, :].astype(F32), lb_ref[...])
            v = v_ref[rows, :]
            g = g_ref[rows, :].astype(F32)
            ss_ref[0, c] = S
            o = _dot(ck["att"], v) + _dot(ck["qin"], S, NT)
            S = ck["decay"] * S + _dot(v, ck["kout"], TN)
            rn = lax.rsqrt(jnp.mean(o * o, axis=-1, keepdims=True) + EPS)
            o_ref[rows, :] = (o * rn * gn_ref[...] * (g * _sigmoid(g))).astype(BF16)
        st[...] = S

    col = lambda base: (lambda h, i: (i, base + h))
    return pl.pallas_call(
        body, name=name, grid=(HG_HEADS, T // tt),
        in_specs=[pl.BlockSpec((tt, Dh), col(8)), pl.BlockSpec((tt, Dh), col(12)), pl.BlockSpec((tt, Dh), col(16)),
                  pl.BlockSpec((tt, Dh), col(20)), pl.BlockSpec((1, Dh), lambda h, i: (0, h)),
                  pl.BlockSpec((1, Dh), lambda h, i: (0, 0))],
        out_specs=[pl.BlockSpec((tt, Dh), lambda h, i: (i, h)),
                   pl.BlockSpec((1, nc, Dh, Dh), lambda h, i: (h, i, 0, 0))],
        out_shape=[jax.ShapeDtypeStruct((T, HG_HEADS * Dh), BF16),
                   jax.ShapeDtypeStruct((HG_HEADS, T // C, Dh, Dh), F32)],
        scratch_shapes=[pltpu.VMEM((Dh, Dh), F32)],
        compiler_params=_cp(2),
    )(z, z, z, z, lb, gn)


def _hgrn_bwd(z, ss, dmix, lb, gn, name="hgrn_bwd", tt=256):
    T = z.shape[0]
    C = HG_CHUNK
    nc = tt // C
    nt = T // tt
    Dh = HG_D

    def body(q_ref, f_ref, v_ref, g_ref, ss_ref, dm_ref, lb_ref, gn_ref, dq_ref, df_ref, dv_ref, dg_ref, dlb_ref, dgn_ref, dst):
        @pl.when(pl.program_id(1) == 0)
        def _():
            dst[...] = jnp.zeros_like(dst)
            dlb_ref[...] = jnp.zeros_like(dlb_ref)
            dgn_ref[...] = jnp.zeros_like(dgn_ref)

        dS = dst[...]
        lbv = lb_ref[...]
        gnv = gn_ref[...]
        rowc = lax.broadcasted_iota(jnp.int32, (C, Dh), 0)
        for c in reversed(range(nc)):
            rows = slice(c * C, (c + 1) * C)
            q = q_ref[rows, :].astype(F32)
            ck = _hg_chunk(q, f_ref[rows, :].astype(F32), lbv)
            v = v_ref[rows, :]
            g = g_ref[rows, :].astype(F32)
            S = ss_ref[0, c]
            o = _dot(ck["att"], v) + _dot(ck["qin"], S, NT)
            rn = lax.rsqrt(jnp.mean(o * o, axis=-1, keepdims=True) + EPS)
            on = o * rn
            dout = dm_ref[rows, :].astype(F32)
            sg, dsg = _silu_and_grad(g)
            d_ong = dout * sg
            dgn_ref[...] += _rsum8(d_ong * on)
            dg_ref[rows, :] = (dout * on * gnv * dsg).astype(BF16)
            don = d_ong * gnv
            do = rn * (don - on * jnp.mean(don * on, axis=-1, keepdims=True))
            dv_ref[rows, :] = (_dot(ck["att"], do, TN) + _dot(ck["kout"], dS, NT)).astype(BF16)
            datt = jnp.where(ck["causal"], _dot(do, v, NT), 0.0)
            dqt = _dot(datt, ck["kt"])
            dkt = _dot(datt, ck["qt"], TN)
            dqin = _dot(do, S)
            dkout = _dot(v, dS)
            ddecay = jnp.sum(dS * S, axis=0, keepdims=True)
            dS = _dot(do, ck["qin"], TN) + ck["decay"] * dS
            b, bm, bl = ck["b"], ck["bm"], ck["bl"]
            dqf = dqt * jnp.exp(b - bm) + dqin * jnp.exp(b)
            dk = dkt * jnp.exp(bm - b) + dkout * jnp.exp(bl - b)
            kk = dkout * ck["kout"]
            db = dqt * ck["qt"] - dkt * ck["kt"] + dqin * ck["qin"] - kk
            dbl = jnp.sum(kk, axis=0, keepdims=True) + ddecay * ck["decay"]
            db = db + jnp.where(rowc == C - 1, dbl, 0.0)
            dlogf = _tri_matmul(ck["anti"].astype(BF16), db)
            dfv = dlogf / ck["f"] - dk
            sig = ck["sig"]
            df_ref[rows, :] = (dfv * (1.0 - lbv) * sig * (1.0 - sig)).astype(BF16)
            dlb_ref[...] += _rsum8(dfv * (1.0 - sig))
            sq = ck["sq"]
            dq_ref[rows, :] = (dqf * (sq * (1.0 + q * (1.0 - sq)))).astype(BF16)
        dst[...] = dS

    rev = lambda i: nt - 1 - i
    col = lambda base: (lambda h, i: (rev(i), base + h))
    out_tok = pl.BlockSpec((tt, Dh), lambda h, i: (rev(i), h))
    acc = pl.BlockSpec((SUBLANES, Dh), lambda h, i: (h, 0))
    tok_shape = jax.ShapeDtypeStruct((T, HG_HEADS * Dh), BF16)
    acc_shape = jax.ShapeDtypeStruct((HG_HEADS * SUBLANES, Dh), F32)
    return pl.pallas_call(
        body, name=name, grid=(HG_HEADS, nt),
        in_specs=[pl.BlockSpec((tt, Dh), col(8)), pl.BlockSpec((tt, Dh), col(12)), pl.BlockSpec((tt, Dh), col(16)),
                  pl.BlockSpec((tt, Dh), col(20)),
                  pl.BlockSpec((1, nc, Dh, Dh), lambda h, i: (h, rev(i), 0, 0)),
                  pl.BlockSpec((tt, Dh), lambda h, i: (rev(i), 4 + h)),
                  pl.BlockSpec((1, Dh), lambda h, i: (0, h)), pl.BlockSpec((1, Dh), lambda h, i: (0, 0))],
        out_specs=[out_tok, out_tok, out_tok, out_tok, acc, acc],
        out_shape=[tok_shape, tok_shape, tok_shape, tok_shape, acc_shape, acc_shape],
        scratch_shapes=[pltpu.VMEM((Dh, Dh), F32)],
        compiler_params=_cp(2),
    )(z, z, z, z, ss, dmix, lb, gn)


def _sgu_core(p, lg_ref, lb_ref, wsc_ref, bsb_ref):
    Wd = D_MODEL
    G = SGU_CHUNK
    zz = _gelu(p)
    u = zz[:, :Wd]
    v = zz[:, Wd:]
    vc = v - jnp.mean(v, axis=-1, keepdims=True)
    rstd = lax.rsqrt(jnp.mean(vc * vc, axis=-1, keepdims=True) + EPS)
    vhat = vc * rstd
    vn = vhat * lg_ref[...] + lb_ref[...]
    svs = []
    for gi in range(SGU_G):
        svs.append(jnp.dot(wsc_ref[gi], vn[:, gi * G:(gi + 1) * G].astype(BF16), preferred_element_type=F32) + bsb_ref[gi])
    return u, vhat, rstd, vn, jnp.concatenate(svs, axis=1)


def _sgu_fwd(p1, lg, lbias, wsc, bsb, name="sgu_fwd", tt=512):
    T = p1.shape[0]
    Wd = D_MODEL
    C = SGU_CHUNK

    def body(p_ref, lg_ref, lb_ref, wsc_ref, bsb_ref, s_ref):
        for c in range(tt // C):
            rows = slice(c * C, (c + 1) * C)
            u, _, _, _, sv = _sgu_core(p_ref[rows, :].astype(F32), lg_ref, lb_ref, wsc_ref, bsb_ref)
            s_ref[rows, :] = (u * sv).astype(BF16)

    return pl.pallas_call(
        body, name=name, grid=(T // tt,),
        in_specs=[pl.BlockSpec((tt, 2 * Wd), lambda i: (i, 0)), _resident((1, Wd)), _resident((1, Wd)),
                  _resident((SGU_G, C, C)), _resident((SGU_G, C, C))],
        out_specs=pl.BlockSpec((tt, Wd), lambda i: (i, 0)),
        out_shape=jax.ShapeDtypeStruct((T, Wd), BF16),
        compiler_params=_cp(1),
    )(p1, lg, lbias, wsc, bsb)


def _sgu_bwd(p1, ds, lg, lbias, wsc, wsct, bsb, name="sgu_bwd", tt=512):
    T = p1.shape[0]
    Wd = D_MODEL
    C = SGU_CHUNK

    def body(p_ref, ds_ref, lg_ref, lb_ref, wsc_ref, wsct_ref, bsb_ref, dp_ref, dws_ref, dbs_ref, dlg_ref, dlb_ref, dbin_ref):
        @pl.when(pl.program_id(0) == 0)
        def _():
            dws_ref[...] = jnp.zeros_like(dws_ref)
            dbs_ref[...] = jnp.zeros_like(dbs_ref)
            dlg_ref[...] = jnp.zeros_like(dlg_ref)
            dlb_ref[...] = jnp.zeros_like(dlb_ref)
            dbin_ref[...] = jnp.zeros_like(dbin_ref)

        for c in range(tt // C):
            rows = slice(c * C, (c + 1) * C)
            p = p_ref[rows, :].astype(F32)
            u, vhat, rstd, vn, sv = _sgu_core(p, lg_ref, lb_ref, wsc_ref, bsb_ref)
            dsc = ds_ref[rows, :].astype(F32)
            du = dsc * sv
            dsv = dsc * u
            dvns = []
            for gi in range(SGU_G):
                cs = slice(gi * C, (gi + 1) * C)
                dsv_g = dsv[:, cs]
                dvns.append(jnp.dot(wsct_ref[gi], dsv_g.astype(BF16), preferred_element_type=F32))
                dws_ref[gi] += _dot(dsv_g, vn[:, cs], NT)
                dbs_ref[gi] += dsv_g
            dvn = jnp.concatenate(dvns, axis=1)
            dlg_ref[...] += _rsum8(dvn * vhat)
            dlb_ref[...] += _rsum8(dvn)
            dvh = dvn * lg_ref[...]
            dv = rstd * (dvh - jnp.mean(dvh, axis=-1, keepdims=True) - vhat * jnp.mean(dvh * vhat, axis=-1, keepdims=True))
            dp = jnp.concatenate([du, dv], axis=1) * _gelu_grad(p)
            dbin_ref[...] += _rsum8(dp)
            dp_ref[rows, :] = dp.astype(BF16)

    full3 = pl.BlockSpec((SGU_G, C, C), lambda i: (0, 0, 0))
    return pl.pallas_call(
        body, name=name, grid=(T // tt,),
        in_specs=[pl.BlockSpec((tt, 2 * Wd), lambda i: (i, 0)), pl.BlockSpec((tt, Wd), lambda i: (i, 0)),
                  _resident((1, Wd)), _resident((1, Wd)), _resident((SGU_G, C, C)), _resident((SGU_G, C, C)),
                  _resident((SGU_G, C, C))],
        out_specs=[pl.BlockSpec((tt, 2 * Wd), lambda i: (i, 0)), full3, full3,
                   pl.BlockSpec((SUBLANES, Wd), lambda i: (0, 0)), pl.BlockSpec((SUBLANES, Wd), lambda i: (0, 0)),
                   pl.BlockSpec((SUBLANES, 2 * Wd), lambda i: (0, 0))],
        out_shape=[jax.ShapeDtypeStruct((T, 2 * Wd), BF16), jax.ShapeDtypeStruct((SGU_G, C, C), F32),
                   jax.ShapeDtypeStruct((SGU_G, C, C), F32), jax.ShapeDtypeStruct((SUBLANES, Wd), F32),
                   jax.ShapeDtypeStruct((SUBLANES, Wd), F32), jax.ShapeDtypeStruct((SUBLANES, 2 * Wd), F32)],
        compiler_params=_cp(1),
    )(p1, ds, lg, lbias, wsc, wsct, bsb)


def _pad_rows(w, rows=SUBLANES):
    return jnp.pad(w, ((0, rows - w.shape[0]), (0, 0)))


def _block_diag(w):
    n, b, _ = w.shape
    return (w[:, :, None, :] * jnp.eye(n, dtype=w.dtype)[:, None, :, None]).reshape(n * b, n * b)


def _diag_blocks(m, n):
    b = m.shape[0] // n
    m4 = m.reshape(n, b, n, b)
    return jnp.stack([m4[k, :, k, :] for k in range(n)], axis=0)


def _piece_major(dw):
    if dw.ndim == 2:
        K, N = dw.shape
        return dw.reshape(N_CHIPS, 2, K // (2 * N_CHIPS), N)
    _, K, ns = dw.shape
    return dw.reshape(N_CHIPS, 2, K // 2, ns)


def _ffn_fwd(h, g, w_up, cw, cb, w_down, tag):
    hn, gu = _norm_mm(h, g, w_up, jnp.zeros((1, w_up.shape[1]), F32), name=f"ffn_up_{tag}")
    a = _ffn_act(gu, cw, cb, name=f"ffn_act_{tag}")
    out = _mm(a, w_down, h, F32, name=f"ffn_down_{tag}")
    return out, (hn, gu, a)


def _ffn_bwd(dh, h, g, saved, w_up, cw, cb, w_down, tag):
    hn, gu, a = saved
    da = _mm(dh, w_down, None, BF16, name=f"ffn_da_{tag}", transpose_w=True)
    dwd = _mm_tn(a, dh, name=f"ffn_dwd_{tag}", tk=D_FF // 2)
    dgu, dc = _ffn_act_bwd(gu, da, cw, cb, name=f"ffn_actb_{tag}")
    dhin, dg8 = _mm_normbwd(dgu, w_up, h, g, dh, name=f"ffn_dh_{tag}")
    dwu = _mm_tn(hn, dgu, name=f"ffn_dwu_{tag}", tk=256, col_shards=N_CHIPS)
    dcs = dc.sum(axis=1)
    return dhin, dg8.sum(axis=0), dwu, dcs[0:3], dcs[3], dwd


REDUCE_GROUPS = {"g1": [("ffn_w_up", 1), ("ffn_w_down", 1), ("od_w_out", 0), ("od_w_in", 0)],
                 "g2": [("ffn_w_up", 0), ("ffn_w_down", 0)],
                 "g3": [("ev_w_out", 0), ("ev_w_in", 0)]}


def _local_step(x, tgt, p, start_reduce=None):
    row = lambda v: v.reshape(1, -1)
    grads = {}

    lower = jax.nn.softmax(p["hg_lb_logits"], axis=0)
    lb0 = row(lower[0])
    ev_cw = _pad_rows(p["ev_conv_w"][0])
    ev_cb = row(p["ev_conv_b"][0])
    wa = _block_diag(p["ev_gate_a_w"][0]).astype(BF16)
    wx = _block_diag(p["ev_gate_x_w"][0]).astype(BF16)
    ba, bx, lam = row(p["ev_gate_a_b"][0]), row(p["ev_gate_x_b"][0]), row(p["ev_lru_lambda"][0])
    gn = row(p["ev_hg_norm"][0])
    tril = jnp.tril(jnp.ones((SGU_CHUNK, SGU_CHUNK), F32))
    wsc = (p["od_w_s"][0] * tril).astype(BF16)
    bsb = jnp.broadcast_to(p["od_b_s"][0][:, :, None], (SGU_G, SGU_CHUNK, SGU_CHUNK)).astype(F32)
    ffn_cw = [_pad_rows(p["ffn_conv_w"][l]) for l in range(2)]
    ffn_cb = [row(p["ffn_conv_b"][l]) for l in range(2)]
    ev_w_in, ev_w_out = p["ev_w_in"][0], p["ev_w_out"][0]
    nm = [row(p["norm_mix"][l]) for l in range(2)]
    nf = [row(p["norm_ffn"][l]) for l in range(2)]

    h0 = x
    hn0, z0 = _norm_mm(h0, nm[0], ev_w_in, jnp.zeros((1, ev_w_in.shape[1]), F32), name="ev_in")
    out_a, hseq = _lru_fwd(z0, ev_cw, ev_cb, wa, ba, wx, bx, lam)
    out_b, ss = _hgrn_fwd(z0, lb0, gn)
    mix0 = jnp.concatenate([out_a, out_b], axis=1)
    h1 = _mm(mix0, ev_w_out, h0, F32, name="ev_out")
    late = p["late"](h1) if "late" in p else p
    od_w_in, od_w_out = late["od_w_in"][0], late["od_w_out"][0]
    w_up = [late["ffn_w_up"][l] for l in range(2)]
    w_down = [late["ffn_w_down"][l] for l in range(2)]
    h2, ffn0 = _ffn_fwd(h1, nf[0], w_up[0], ffn_cw[0], ffn_cb[0], w_down[0], "l0")
    hn1, p1 = _norm_mm(h2, nm[1], od_w_in, row(p["od_b_in"][0]), name="od_in")
    s1 = _sgu_fwd(p1, row(p["od_ln_g"][0]), row(p["od_ln_b"][0]), wsc, bsb)
    h3 = _mm(s1, od_w_out, h2, F32, name="od_out")
    h4, ffn1 = _ffn_fwd(h3, nf[1], w_up[1], ffn_cw[1], ffn_cb[1], w_down[1], "l1")
    dh4, dgf8, sq8 = _final_loss(h4, row(p["norm_final"]), tgt)
    grads["norm_final"] = dgf8.sum(axis=0)

    big = {}
    dh3, dnf1, dwu1, dcw1, dcb1, dwd1 = _ffn_bwd(dh4, h3, nf[1], ffn1, w_up[1], ffn_cw[1], ffn_cb[1], w_down[1], "l1")
    big["ffn_w_up", 1], big["ffn_w_down", 1] = _piece_major(dwu1), _piece_major(dwd1)
    ds1 = _mm(dh3, od_w_out, None, BF16, name="od_ds", transpose_w=True)
    big["od_w_out", 0] = _piece_major(_mm_tn(s1, dh3, name="od_dwo", tk=512))
    wsct = jnp.swapaxes(wsc, 1, 2)
    dp1, dws, dbs, dlg8, dlb8, dbin8 = _sgu_bwd(p1, ds1, row(p["od_ln_g"][0]), row(p["od_ln_b"][0]), wsc, wsct, bsb)
    grads["od_w_s"] = (dws * tril)[None]
    grads["od_b_s"] = dbs.sum(axis=-1)[None]
    grads["od_ln_g"] = dlg8.sum(axis=0)[None]
    grads["od_ln_b"] = dlb8.sum(axis=0)[None]
    grads["od_b_in"] = dbin8.sum(axis=0)[None]
    dh2, dnm1 = _mm_normbwd(dp1, od_w_in, h2, nm[1], dh3, name="od_dh")
    big["od_w_in", 0] = _piece_major(_mm_tn(hn1, dp1, name="od_dwi", tk=512, col_shards=N_CHIPS))
    if start_reduce is not None:
        token = start_reduce("g1", [big[key] for key in REDUCE_GROUPS["g1"]])
        ffn_cb[0] = ffn_cb[0] + token[0:1, 0:1]

    dh1, dnf0, dwu0, dcw0, dcb0, dwd0 = _ffn_bwd(dh2, h1, nf[0], ffn0, w_up[0], ffn_cw[0], ffn_cb[0], w_down[0], "l0")
    big["ffn_w_up", 0], big["ffn_w_down", 0] = _piece_major(dwu0), _piece_major(dwd0)
    if start_reduce is not None:
        token = start_reduce("g2", [big[key] for key in REDUCE_GROUPS["g2"]])
        ev_cb = ev_cb + token[0:1, 0:1]
    dmix = _mm(dh1, ev_w_out, None, BF16, name="ev_dmix", transpose_w=True)
    big["ev_w_out", 0] = _piece_major(_mm_tn(mix0, dh1, name="ev_dwo", tk=512))
    dz01, dc5, dwa, dwx, dvec = _lru_bwd(z0, hseq, dmix, ev_cw, ev_cb, wa, wa.T, ba, wx, wx.T, bx, lam)
    dq, df, dv, dg, dlb32, dgn32 = _hgrn_bwd(z0, ss, dmix, lb0, gn)
    dz0 = jnp.concatenate([dz01, dq, df, dv, dg], axis=1)
    grad_x, dnm0 = _mm_normbwd(dz0, ev_w_in, h0, nm[0], dh1, name="ev_dh")
    big["ev_w_in", 0] = _piece_major(_mm_tn(hn0, dz0, name="ev_dwi", tk=512, col_shards=N_CHIPS))

    dc5s = dc5.sum(axis=1)
    grads["ev_conv_w"] = dc5s[0:4][None]
    grads["ev_conv_b"] = dc5s[4][None]
    grads["ev_gate_a_w"] = _diag_blocks(dwa, LRU_BLOCKS)[None]
    grads["ev_gate_x_w"] = _diag_blocks(dwx, LRU_BLOCKS)[None]
    dvs = dvec.sum(axis=1)
    grads["ev_gate_a_b"] = dvs[0][None]
    grads["ev_gate_x_b"] = dvs[1][None]
    grads["ev_lru_lambda"] = (dvs[2] * (-jax.nn.sigmoid(-p["ev_lru_lambda"][0])))[None]
    dlb = dlb32.reshape(HG_HEADS, SUBLANES, HG_D).sum(axis=1).reshape(-1)
    grads["hg_lb_logits"] = dlb[None, :] * lower[0][None, :] * (jnp.eye(3, dtype=F32)[0][:, None] - lower)
    grads["ev_hg_norm"] = dgn32.reshape(HG_HEADS, SUBLANES, HG_D).sum(axis=(0, 1))[None]
    grads["norm_mix"] = jnp.stack([dnm0.sum(axis=0), dnm1.sum(axis=0)])
    grads["norm_ffn"] = jnp.stack([dnf0, dnf1])
    grads["ffn_conv_w"] = jnp.stack([dcw0, dcw1])
    grads["ffn_conv_b"] = jnp.stack([dcb0, dcb1])
    return sq8, grad_x, grads, big


MESH = pl.DeviceIdType.MESH
ANY = pl.BlockSpec(memory_space=pl.ANY)
N_CHIPS = 4
N_DEV = 8

SH_BIG = {"ev_w_in": 2, "ev_w_out": 1, "od_w_in": 2, "od_w_out": 1, "ffn_w_up": 2, "ffn_w_down": 1}
SH_SMALL = {"ev_conv_w": 2, "od_b_in": 1, "od_ln_g": 1, "od_ln_b": 1, "ffn_conv_w": 2}
REP = ["norm_mix", "norm_ffn", "norm_final", "ev_conv_b", "ev_gate_a_w", "ev_gate_a_b", "ev_gate_x_w", "ev_gate_x_b",
       "ev_lru_lambda", "hg_lb_logits", "ev_hg_norm", "od_w_s", "od_b_s", "ffn_conv_b"]
WEIGHTS = ["norm_mix", "norm_ffn", "norm_final", "ev_w_in", "ev_conv_w", "ev_conv_b", "ev_gate_a_w", "ev_gate_a_b", "ev_gate_x_w",
           "ev_gate_x_b", "ev_lru_lambda", "hg_lb_logits", "ev_hg_norm", "ev_w_out", "od_w_in", "od_b_in", "od_ln_g", "od_ln_b",
           "od_w_s", "od_b_s", "od_w_out", "ffn_w_up", "ffn_conv_w", "ffn_conv_b", "ffn_w_down"]


def _rows(n_elems, mult=SUBLANES):
    r = -(-n_elems // LANES)
    return -(-r // mult) * mult


def _pack(arrs, rows, dtype):
    flat = jnp.concatenate([a.reshape(-1).astype(dtype) for a in arrs])
    return jnp.pad(flat, (0, rows * LANES - flat.shape[0])).reshape(rows, LANES)


def _unpack(flat2d, shapes):
    flat = flat2d.reshape(-1)
    out, off = [], 0
    for s in shapes:
        n = 1
        for d in s:
            n *= d
        out.append(flat[off:off + n].reshape(s))
        off += n
    return out


def _mesh_pos():
    return lax.axis_index("x"), lax.axis_index("y"), lax.axis_index("c")


def _other_chips(x, y):
    return [(1 - x, y), (x, 1 - y), (1 - x, 1 - y)]


def _half_rows(n):
    return lambda r, c: r.at[0, pl.ds(c * (n // 2), n // 2), :]


GATHER_BIG = {
    "ev_w_in": ((1024, 3072), _half_rows(1024), lambda o, k, c: o.at[pl.ds(c * 512, 512), pl.ds(k * 768, 768)]),
    "ev_w_out": ((1024, 1024), _half_rows(256), lambda o, k, c: o.at[pl.ds(k * 256 + c * 128, 128), :]),
    "od_w_in": ((1024, 2048), _half_rows(1024), lambda o, k, c: o.at[pl.ds(c * 512, 512), pl.ds(k * 512, 512)]),
    "od_w_out": ((1024, 1024), _half_rows(256), lambda o, k, c: o.at[pl.ds(k * 256 + c * 128, 128), :]),
    "ffn_w_up": ((2, 1024, 2 * D_FF), lambda r, c: r.at[c], lambda o, k, c: o.at[c, :, pl.ds(k * (2 * D_FF // 4), 2 * D_FF // 4)]),
    "ffn_w_down": ((2, D_FF, 1024), lambda r, c: r.at[c], lambda o, k, c: o.at[c, pl.ds(k * (D_FF // 4), D_FF // 4), :]),
}


def _gather_weights(names, big, small):
    nb = len(big)
    descs = [GATHER_BIG[n] for n in names]
    rs = small.shape[0]

    def body(*refs):
        ins, s_ref = refs[:nb], refs[nb]
        outs, os_ref = refs[nb + 1:2 * nb + 1], refs[2 * nb + 1]
        ici_send, ici_recv, d2d_send, d2d_recv, loc_sems = refs[2 * nb + 2:2 * nb + 7]
        vbufs = refs[2 * nb + 7:]
        x, y, c = _mesh_pos()
        k = 2 * x + y
        chips = _other_chips(x, y)
        sib = (x, y, 1 - c)

        def remote(src, dst, ssem, rsem, to):
            return pltpu.make_async_remote_copy(src_ref=src, dst_ref=dst, send_sem=ssem, recv_sem=rsem, device_id=to,
                                                device_id_type=MESH)

        stage = [pltpu.make_async_copy(ins[t], vbufs[t], loc_sems.at[2 * t]) for t in range(nb)]
        stage.append(pltpu.make_async_copy(s_ref, vbufs[nb], loc_sems.at[2 * nb]))
        for cp in stage:
            cp.start()
        sends = []
        for t, (_, src, dst) in enumerate(descs):
            for j, (px, py) in enumerate(chips):
                sends.append(remote(src(ins[t], c), dst(outs[t], k, c), ici_send.at[3 * t + j], ici_recv.at[3 * t + j], (px, py, c)))
        for j, (px, py) in enumerate(chips):
            sends.append(remote(s_ref, os_ref.at[k], ici_send.at[3 * nb + j], ici_recv.at[3 * nb + j], (px, py, c)))
        for cp in sends:
            cp.start()
        for cp in stage:
            cp.wait()
        local = []
        for t, (_, src, dst) in enumerate(descs):
            for cc in (0, 1):
                local.append(pltpu.make_async_copy(src(vbufs[t], cc), dst(outs[t], k, cc), loc_sems.at[2 * t + cc]))
        local.append(pltpu.make_async_copy(vbufs[nb], os_ref.at[k], loc_sems.at[2 * nb]))
        for cp in local:
            cp.start()
        for t, (_, src, dst) in enumerate(descs):
            for j, (px, py) in enumerate(chips):
                got = dst(outs[t], 2 * px + py, c)
                remote(got, got, ici_send.at[3 * t + j], ici_recv.at[3 * t + j], (px, py, c)).wait_recv()
                fwd = remote(got, got, d2d_send.at[3 * t + j], d2d_recv.at[3 * t + j], sib)
                fwd.start()
                sends.append(fwd)
        for j, (px, py) in enumerate(chips):
            remote(s_ref, os_ref.at[2 * px + py], ici_send.at[3 * nb + j], ici_recv.at[3 * nb + j], (px, py, c)).wait_recv()
        for t, (_, src, dst) in enumerate(descs):
            for j, (px, py) in enumerate(chips):
                theirs = dst(outs[t], 2 * px + py, 1 - c)
                remote(theirs, theirs, d2d_send.at[3 * t + j], d2d_recv.at[3 * t + j], sib).wait_recv()
        for cp in sends:
            cp.wait_send()
        for cp in local:
            cp.wait()

    out_shape = [jax.ShapeDtypeStruct(d[0], BF16) for d in descs] + [jax.ShapeDtypeStruct((N_CHIPS, rs, LANES), small.dtype)]
    return pl.pallas_call(
        body, name="gather_weights", in_specs=[ANY] * (nb + 1), out_specs=[ANY] * (nb + 1), out_shape=out_shape,
        scratch_shapes=[pltpu.SemaphoreType.DMA((3 * nb + 3,)), pltpu.SemaphoreType.DMA((3 * nb + 3,)),
                        pltpu.SemaphoreType.DMA((3 * nb,)), pltpu.SemaphoreType.DMA((3 * nb,)),
                        pltpu.SemaphoreType.DMA((2 * nb + 1,))]
        + [pltpu.VMEM(b.shape, b.dtype) for b in big] + [pltpu.VMEM(small.shape, small.dtype)],
        compiler_params=pltpu.CompilerParams(vmem_limit_bytes=VMEM_LIMIT),
    )(*big, small)


def _place_own(names, big):
    nb = len(big)
    descs = [GATHER_BIG[n] for n in names]

    def body(*refs):
        ins, outs = refs[:nb], refs[nb:2 * nb]
        sems, vbufs = refs[2 * nb], refs[2 * nb + 1:]
        x, y, c = _mesh_pos()
        k = 2 * x + y
        stage = [pltpu.make_async_copy(ins[t], vbufs[t], sems.at[2 * t]) for t in range(nb)]
        for cp in stage:
            cp.start()
        for cp in stage:
            cp.wait()
        local = [pltpu.make_async_copy(src(vbufs[t], cc), dst(outs[t], k, cc), sems.at[2 * t + cc])
                 for t, (_, src, dst) in enumerate(descs) for cc in (0, 1)]
        for cp in local:
            cp.start()
        for cp in local:
            cp.wait()

    return pl.pallas_call(
        body, name="place_own", in_specs=[ANY] * nb, out_specs=[ANY] * nb,
        out_shape=[jax.ShapeDtypeStruct(d[0], BF16) for d in descs],
        scratch_shapes=[pltpu.SemaphoreType.DMA((2 * nb,))] + [pltpu.VMEM(b.shape, b.dtype) for b in big],
        compiler_params=pltpu.CompilerParams(vmem_limit_bytes=VMEM_LIMIT),
    )(*big)


def _gather_start(names, big, bufs):
    nb = len(big)
    descs = [GATHER_BIG[n] for n in names]

    def body(*refs):
        ins, lnd = refs[:nb], refs[nb:2 * nb]
        send_sems, recv_sems, token = refs[2 * nb], refs[2 * nb + 1], refs[-1]
        x, y, c = _mesh_pos()
        k = 2 * x + y
        for t, (_, src, dst) in enumerate(descs):
            for j, (px, py) in enumerate(_other_chips(x, y)):
                _remote(src(ins[t], c), dst(lnd[t], k, c), send_sems.at[3 * t + j], recv_sems.at[3 * t + j], (px, py, c)).start()
        token[...] = jnp.zeros_like(token)

    out = pl.pallas_call(
        body, name="gather_start",
        out_shape=(pltpu.SemaphoreType.DMA((3 * nb,)), pltpu.SemaphoreType.DMA((3 * nb,)),
                   *[pltpu.HBM(b.shape, b.dtype) for b in big], *[pltpu.HBM(b.shape, b.dtype) for b in bufs],
                   jax.ShapeDtypeStruct((SUBLANES, LANES), F32)),
        in_specs=[HBM] * (2 * nb), out_specs=(SEM, SEM, *[HBM] * (2 * nb), pl.BlockSpec(memory_space=pltpu.VMEM)),
        input_output_aliases={i: 2 + i for i in range(2 * nb)},
        compiler_params=pltpu.CompilerParams(has_side_effects=DATAFLOW),
    )(*[pltpu.with_memory_space_constraint(b, pltpu.HBM) for b in big], *[pltpu.with_memory_space_constraint(b, pltpu.HBM) for b in bufs])
    return out[0], out[1], list(out[2:2 + nb]), list(out[2 + nb:2 + 2 * nb]), out[-1]


def _gather_wait(names, send_sems, recv_sems, big, bufs, after):
    nb = len(big)
    descs = [GATHER_BIG[n] for n in names]

    def body(*refs):
        ins, lnd = refs[:nb], refs[nb:2 * nb]
        ssem, rsem = refs[2 * nb], refs[2 * nb + 1]
        x, y, c = _mesh_pos()
        for t, (_, src, dst) in enumerate(descs):
            for j, (px, py) in enumerate(_other_chips(x, y)):
                cp = _remote(src(ins[t], c), dst(lnd[t], 2 * px + py, c), ssem.at[3 * t + j], rsem.at[3 * t + j], (px, py, c))
                cp.wait_send()
                cp.wait_recv()

    out = pl.pallas_call(
        body, name="gather_wait",
        out_shape=(*[pltpu.HBM(b.shape, b.dtype) for b in big], *[pltpu.HBM(b.shape, b.dtype) for b in bufs]),
        in_specs=[HBM] * (2 * nb) + [SEM, SEM, ANY], out_specs=tuple([HBM] * (2 * nb)),
        input_output_aliases={i: i for i in range(2 * nb)},
        compiler_params=pltpu.CompilerParams(has_side_effects=DATAFLOW),
    )(*big, *bufs, send_sems, recv_sems, after)
    return list(out[nb:])


def _gather_forward(names, bufs):
    nb = len(bufs)
    descs = [GATHER_BIG[n] for n in names]

    def body(*refs):
        outs = refs[nb:2 * nb]
        send_sems, recv_sems = refs[2 * nb:]
        x, y, c = _mesh_pos()
        sib = (x, y, 1 - c)
        sends = []
        for t, (_, src, dst) in enumerate(descs):
            for j, (px, py) in enumerate(_other_chips(x, y)):
                got = dst(outs[t], 2 * px + py, c)
                sends.append(_remote(got, got, send_sems.at[3 * t + j], recv_sems.at[3 * t + j], sib))
        for cp in sends:
            cp.start()
        for t, (_, src, dst) in enumerate(descs):
            for j, (px, py) in enumerate(_other_chips(x, y)):
                theirs = dst(outs[t], 2 * px + py, 1 - c)
                _remote(theirs, theirs, send_sems.at[3 * t + j], recv_sems.at[3 * t + j], sib).wait_recv()
        for cp in sends:
            cp.wait_send()

    return pl.pallas_call(
        body, name="gather_forward", in_specs=[ANY] * nb, out_specs=[ANY] * nb,
        out_shape=[jax.ShapeDtypeStruct(b.shape, b.dtype) for b in bufs], input_output_aliases={t: t for t in range(nb)},
        scratch_shapes=[pltpu.SemaphoreType.DMA((3 * nb,)), pltpu.SemaphoreType.DMA((3 * nb,))],
    )(*bufs)


def _remote(src, dst, ssem, rsem, to):
    return pltpu.make_async_remote_copy(src_ref=src, dst_ref=dst, send_sem=ssem, recv_sem=rsem, device_id=to, device_id_type=MESH)


def _rs_send_sibling(gs, tag):
    n = len(gs)
    counts = [N_CHIPS if g.ndim == 4 else 1 for g in gs]
    ns = sum(counts)

    def body(*refs):
        ins, outs = refs[:n], refs[n:2 * n]
        send_sems, recv_sems = refs[2 * n:]
        x, y, c = _mesh_pos()
        cps, s = [], 0
        for t in range(n):
            if counts[t] == 1:
                cps.append(_remote(ins[t].at[1 - c], outs[t], send_sems.at[s], recv_sems.at[s], (x, y, 1 - c)))
                s += 1
            else:
                for k in range(N_CHIPS):
                    cps.append(_remote(ins[t].at[k, 1 - c], outs[t].at[k], send_sems.at[s], recv_sems.at[s], (x, y, 1 - c)))
                    s += 1
        for cp in cps:
            cp.start()
        for cp in cps:
            cp.wait()

    out_shape = [jax.ShapeDtypeStruct(g.shape[:1] + g.shape[2:] if g.ndim == 4 else g.shape[1:], g.dtype) for g in gs]
    return pl.pallas_call(
        body, name=f"rs_send_sibling_{tag}", in_specs=[ANY] * n, out_specs=[ANY] * n, out_shape=out_shape,
        scratch_shapes=[pltpu.SemaphoreType.DMA((ns,)), pltpu.SemaphoreType.DMA((ns,))],
    )(*gs)


def _add_piece(g, recv, c, name):
    P, Q = g.shape[-2:]

    def body(c_ref, g_ref, r_ref, o_ref):
        o_ref[...] = g_ref[...].reshape(o_ref.shape) + r_ref[...]

    if g.ndim == 4:
        grid = (N_CHIPS,)
        in_specs = [pl.BlockSpec((1, 1, P, Q), lambda k, c_ref: (k, c_ref[0], 0, 0)), pl.BlockSpec((1, P, Q), lambda k, c_ref: (k, 0, 0))]
        out_spec = pl.BlockSpec((1, P, Q), lambda k, c_ref: (k, 0, 0))
    else:
        grid = (1,)
        in_specs = [pl.BlockSpec((1, P, Q), lambda k, c_ref: (c_ref[0], 0, 0)), pl.BlockSpec((P, Q), lambda k, c_ref: (0, 0))]
        out_spec = pl.BlockSpec((P, Q), lambda k, c_ref: (0, 0))
    return pl.pallas_call(
        body, name=name,
        grid_spec=pltpu.PrefetchScalarGridSpec(num_scalar_prefetch=1, grid=grid, in_specs=in_specs, out_specs=out_spec),
        out_shape=jax.ShapeDtypeStruct(recv.shape, g.dtype),
        compiler_params=_cp(1),
    )(c, g, recv)


HBM = pl.BlockSpec(memory_space=pltpu.HBM)
SEM = pl.BlockSpec(memory_space=pltpu.SEMAPHORE)
DATAFLOW = pltpu.SideEffectType.DATAFLOW_SIDE_EFFECTING


def _chips_start(hs, tag):
    n = len(hs)
    lands = [pltpu.with_memory_space_constraint(lax.empty((N_CHIPS,) + h.shape[-2:], h.dtype), pltpu.HBM) for h in hs]

    def body(*refs):
        ins, lnd = refs[:n], refs[n:2 * n]
        send_sems, recv_sems, token = refs[2 * n], refs[2 * n + 1], refs[-1]
        x, y, c = _mesh_pos()
        k = 2 * x + y
        piece = lambda t, kk: ins[t].at[kk] if hs[t].ndim == 3 else ins[t]
        for t in range(n):
            for j, (px, py) in enumerate(_other_chips(x, y)):
                _remote(piece(t, 2 * px + py), lnd[t].at[k], send_sems.at[3 * t + j], recv_sems.at[3 * t + j], (px, py, c)).start()
        token[...] = jnp.zeros_like(token)

    out = pl.pallas_call(
        body, name=f"chips_start_{tag}",
        out_shape=(pltpu.SemaphoreType.DMA((3 * n,)), pltpu.SemaphoreType.DMA((3 * n,)),
                   *[pltpu.HBM(h.shape, h.dtype) for h in hs], *[pltpu.HBM(l.shape, l.dtype) for l in lands],
                   jax.ShapeDtypeStruct((SUBLANES, LANES), F32)),
        in_specs=[HBM] * (2 * n), out_specs=(SEM, SEM, *[HBM] * (2 * n), pl.BlockSpec(memory_space=pltpu.VMEM)),
        input_output_aliases={i: 2 + i for i in range(2 * n)},
        compiler_params=pltpu.CompilerParams(has_side_effects=DATAFLOW),
    )(*[pltpu.with_memory_space_constraint(h, pltpu.HBM) for h in hs], *lands)
    return out[0], out[1], list(out[2:2 + n]), list(out[2 + n:2 + 2 * n]), out[-1]


def _chips_wait(send_sems, recv_sems, hs, lands, after, tag):
    n = len(hs)

    def body(*refs):
        ins, lnd = refs[:n], refs[n:2 * n]
        ssem, rsem = refs[2 * n], refs[2 * n + 1]
        x, y, c = _mesh_pos()
        k = 2 * x + y
        piece = lambda t, kk: ins[t].at[kk] if hs[t].ndim == 3 else ins[t]
        for t in range(n):
            for j, (px, py) in enumerate(_other_chips(x, y)):
                cp = _remote(piece(t, k), lnd[t].at[2 * px + py], ssem.at[3 * t + j], rsem.at[3 * t + j], (px, py, c))
                cp.wait_send()
                cp.wait_recv()

    out = pl.pallas_call(
        body, name=f"chips_wait_{tag}",
        out_shape=(*[pltpu.HBM(h.shape, h.dtype) for h in hs], *[pltpu.HBM(l.shape, l.dtype) for l in lands]),
        in_specs=[HBM] * (2 * n) + [SEM, SEM, ANY], out_specs=tuple([HBM] * (2 * n)),
        input_output_aliases={i: i for i in range(2 * n)},
        compiler_params=pltpu.CompilerParams(has_side_effects=DATAFLOW),
    )(*hs, *lands, send_sems, recv_sems, after)
    return list(out[:n]), list(out[n:])


def _add_chips(p, own, kc, name):
    _, P, Q = p.shape
    tr = P
    while N_CHIPS * tr * Q * 4 > 6 * 1024 * 1024 and tr % 16 == 0:
        tr //= 2
    sharded = own.ndim == 3

    def body(kc_ref, p_ref, own_ref, o_ref):
        k = kc_ref[0]
        mine = own_ref[...].reshape(tr, Q)
        v = [jnp.where(k == j, mine, p_ref[j]) for j in range(N_CHIPS)]
        o_ref[0] = ((v[0] + v[1]) + v[2]) + v[3]

    own_spec = (pl.BlockSpec((1, tr, Q), lambda i, kc_ref: (kc_ref[0], i, 0)) if sharded
                else pl.BlockSpec((tr, Q), lambda i, kc_ref: (i, 0)))
    return pl.pallas_call(
        body, name=name,
        grid_spec=pltpu.PrefetchScalarGridSpec(
            num_scalar_prefetch=1, grid=(P // tr,),
            in_specs=[pl.BlockSpec((N_CHIPS, tr, Q), lambda i, kc_ref: (0, i, 0)), own_spec],
            out_specs=pl.BlockSpec((1, tr, Q), lambda i, kc_ref: (kc_ref[1], i, 0))),
        out_shape=jax.ShapeDtypeStruct((2, P, Q), p.dtype),
        compiler_params=_cp(1),
    )(kc, p, own)


def _rs_share(fs, tag):
    n = len(fs)

    def body(*refs):
        outs = refs[n:2 * n]
        send_sems, recv_sems = refs[2 * n:]
        x, y, c = _mesh_pos()
        sends = [_remote(outs[t].at[c], outs[t].at[c], send_sems.at[t], recv_sems.at[t], (x, y, 1 - c)) for t in range(n)]
        for cp in sends:
            cp.start()
        for t in range(n):
            _remote(outs[t].at[c], outs[t].at[1 - c], send_sems.at[t], recv_sems.at[t], (x, y, 1 - c)).wait_recv()
        for cp in sends:
            cp.wait_send()

    return pl.pallas_call(
        body, name=f"rs_share_{tag}", in_specs=[ANY] * n, out_specs=[ANY] * n,
        out_shape=[jax.ShapeDtypeStruct(f.shape, f.dtype) for f in fs], input_output_aliases={t: t for t in range(n)},
        scratch_shapes=[pltpu.SemaphoreType.DMA((n,)), pltpu.SemaphoreType.DMA((n,))],
    )(*fs)


def _reduce_start(gs, kc, tag):
    from_sibling = _rs_send_sibling(gs, tag)
    chip_sums = [_add_piece(g, r, kc[1:], name=f"add_piece_{tag}_{t}") for t, (g, r) in enumerate(zip(gs, from_sibling))]
    send_sems, recv_sems, chip_sums, lands, token = _chips_start(chip_sums, tag)
    return (send_sems, recv_sems, chip_sums, lands, tag), token


def _reduce_finish(states, kc, after):
    mine = []
    for send_sems, recv_sems, chip_sums, lands, tag in states:
        chip_sums, from_chips = _chips_wait(send_sems, recv_sems, chip_sums, lands, after, tag)
        mine += [_add_chips(p, h, kc, name=f"add_chips_{tag}_{t}") for t, (p, h) in enumerate(zip(from_chips, chip_sums))]
    return _rs_share(mine, "all")


def _adamw(w, g, m, v, name):
    R, C = w.shape
    tr = R
    for cand in (512, 256, 128, 64, 32, 16, 8):
        if R % cand == 0 and cand * C * 4 <= 2 * 1024 * 1024:
            tr = cand
            break
    c1 = 1.0 / (1.0 - ADAM_B1 ** ADAM_STEP)
    c2 = 1.0 / (1.0 - ADAM_B2 ** ADAM_STEP)

    def body(w_ref, g_ref, m_ref, v_ref, d_ref, mo_ref, vo_ref):
        gv = g_ref[...]
        mn = ADAM_B1 * m_ref[...] + (1.0 - ADAM_B1) * gv
        vn = ADAM_B2 * v_ref[...] + (1.0 - ADAM_B2) * (gv * gv)
        mo_ref[...] = mn
        vo_ref[...] = vn
        d_ref[...] = -ADAM_LR * ((mn * c1) / (jnp.sqrt(vn * c2) + ADAM_EPS) + ADAM_WD * w_ref[...])

    spec = pl.BlockSpec((tr, C), lambda i: (i, 0))
    shp = jax.ShapeDtypeStruct((R, C), F32)
    return pl.pallas_call(body, name=name, grid=(R // tr,), in_specs=[spec] * 4, out_specs=[spec] * 3, out_shape=[shp] * 3,
                          compiler_params=_cp(1))(w, g, m, v)


def _step(a):
    x, y, c = _mesh_pos()
    kc = jnp.stack([2 * x + y, c]).astype(jnp.int32)

    rs = _rows(sum(a[n].size for n in SH_SMALL))
    first, later = ["ev_w_in", "ev_w_out"], ["od_w_in", "od_w_out", "ffn_w_up", "ffn_w_down"]
    lead = lambda w: w if w.ndim == 3 else w[None]
    *full, gs = _gather_weights(first, [a[n].astype(BF16) for n in first], _pack([a[n] for n in SH_SMALL], rs, F32))
    p = {n: a[n] for n in REP}
    p.update({n: lead(w) for n, w in zip(first, full)})
    parts = [_unpack(gs[k], [a[n].shape for n in SH_SMALL]) for k in range(N_CHIPS)]
    for i, n in enumerate(SH_SMALL):
        p[n] = jnp.concatenate([parts[k][i] for k in range(N_CHIPS)], axis=SH_SMALL[n])
    shards = [a[n].astype(BF16) for n in later]
    g_send, g_recv, shards, bufs, token = _gather_start(later, shards, _place_own(later, shards))
    p["norm_mix"] = p["norm_mix"] + token[0:1, 0:1]

    def late(after):
        got = _gather_forward(later, _gather_wait(later, g_send, g_recv, shards, bufs, after))
        return {n: lead(w) for n, w in zip(later, got)}

    p["late"] = late

    states = []

    def start_reduce(tag, gs):
        state, token = _reduce_start(gs, kc, tag)
        states.append(state)
        return token

    sq8, grad_x, grads, big = _local_step(a["x"][0], a["loss_target"][0], p, start_reduce)
    loss = lax.psum(0.5 / D_MODEL * jnp.sum(sq8), ("x", "y", "c"))

    r_s = _rows(sum(a[n].size for n in SH_SMALL), 2 * SUBLANES) // 2
    small_pieces = []
    for k in range(N_CHIPS):
        pieces = [lax.slice_in_dim(grads[n], k * a[n].shape[ax], (k + 1) * a[n].shape[ax], axis=ax) for n, ax in SH_SMALL.items()]
        small_pieces.append(_pack(pieces, 2 * r_s, F32).reshape(2, r_s, LANES))
    g_small = jnp.stack(small_pieces)
    r_r = _rows(sum(a[n].size for n in REP), 2 * SUBLANES) // 2
    g_rep = _pack([grads[n] for n in REP], 2 * r_r, F32).reshape(2, r_r, LANES)
    start_reduce("g3", [big[key] for key in REDUCE_GROUPS["g3"]] + [g_small, g_rep])
    reduced = _reduce_finish(states, kc, grad_x)
    red = dict(zip([key for tag in ("g1", "g2", "g3") for key in REDUCE_GROUPS[tag]], reduced))
    gfin = {}
    for n in ("ev_w_in", "ev_w_out", "od_w_in", "od_w_out"):
        gfin[n] = red[n, 0].reshape(a[n].shape)
    for n in ("ffn_w_up", "ffn_w_down"):
        gfin[n] = jnp.stack([red[n, l].reshape(a[n].shape[1:]) for l in range(2)])
    gfin.update(zip(SH_SMALL, _unpack(reduced[-2], [a[n].shape for n in SH_SMALL])))
    gfin.update(zip(REP, _unpack(reduced[-1], [a[n].shape for n in REP])))

    out = {"loss": loss, "grad_x": grad_x[None]}
    small_names = list(SH_SMALL) + REP
    for n in SH_BIG:
        shp = a[n].shape
        two_d = lambda t: t.reshape(-1, shp[-1])
        d, mo, vo = _adamw(two_d(a[n]), two_d(gfin[n]), two_d(a["m_" + n]), two_d(a["v_" + n]), name=f"adamw_{n}")
        out["delta_" + n], out["new_m_" + n], out["new_v_" + n] = d.reshape(shp), mo.reshape(shp), vo.reshape(shp)
    r_small = _rows(sum(a[n].size for n in small_names), 512)
    packs = [_pack([src(n) for n in small_names], r_small, F32)
             for src in (lambda n: a[n], lambda n: gfin[n], lambda n: a["m_" + n], lambda n: a["v_" + n])]
    d, mo, vo = _adamw(*packs, name="adamw_small")
    shapes = [a[n].shape for n in small_names]
    for n, dd, mm, vv in zip(small_names, _unpack(d, shapes), _unpack(mo, shapes), _unpack(vo, shapes)):
        out["delta_" + n], out["new_m_" + n], out["new_v_" + n] = dd, mm, vv
    for n in WEIGHTS:
        out["grad_" + n] = gfin[n]
    return out


def kernel(x, norm_mix, norm_ffn, norm_final, ev_w_in, ev_conv_w, ev_conv_b, ev_gate_a_w, ev_gate_a_b, ev_gate_x_w, ev_gate_x_b, ev_lru_lambda, hg_lb_logits, ev_hg_norm, ev_w_out, od_w_in, od_b_in, od_ln_g, od_ln_b, od_w_s, od_b_s, od_w_out, ffn_w_up, ffn_conv_w, ffn_conv_b, ffn_w_down, loss_target, m_norm_mix, m_norm_ffn, m_norm_final, m_ev_w_in, m_ev_conv_w, m_ev_conv_b, m_ev_gate_a_w, m_ev_gate_a_b, m_ev_gate_x_w, m_ev_gate_x_b, m_ev_lru_lambda, m_hg_lb_logits, m_ev_hg_norm, m_ev_w_out, m_od_w_in, m_od_b_in, m_od_ln_g, m_od_ln_b, m_od_w_s, m_od_b_s, m_od_w_out, m_ffn_w_up, m_ffn_conv_w, m_ffn_conv_b, m_ffn_w_down, v_norm_mix, v_norm_ffn, v_norm_final, v_ev_w_in, v_ev_conv_w, v_ev_conv_b, v_ev_gate_a_w, v_ev_gate_a_b, v_ev_gate_x_w, v_ev_gate_x_b, v_ev_lru_lambda, v_hg_lb_logits, v_ev_hg_norm, v_ev_w_out, v_od_w_in, v_od_b_in, v_od_ln_g, v_od_ln_b, v_od_w_s, v_od_b_s, v_od_w_out, v_ffn_w_up, v_ffn_conv_w, v_ffn_conv_b, v_ffn_w_down):
    vals = (x, norm_mix, norm_ffn, norm_final, ev_w_in, ev_conv_w, ev_conv_b, ev_gate_a_w, ev_gate_a_b, ev_gate_x_w, ev_gate_x_b, ev_lru_lambda, hg_lb_logits, ev_hg_norm, ev_w_out, od_w_in, od_b_in, od_ln_g, od_ln_b, od_w_s, od_b_s, od_w_out, ffn_w_up, ffn_conv_w, ffn_conv_b, ffn_w_down, loss_target, m_norm_mix, m_norm_ffn, m_norm_final, m_ev_w_in, m_ev_conv_w, m_ev_conv_b, m_ev_gate_a_w, m_ev_gate_a_b, m_ev_gate_x_w, m_ev_gate_x_b, m_ev_lru_lambda, m_hg_lb_logits, m_ev_hg_norm, m_ev_w_out, m_od_w_in, m_od_b_in, m_od_ln_g, m_od_ln_b, m_od_w_s, m_od_b_s, m_od_w_out, m_ffn_w_up, m_ffn_conv_w, m_ffn_conv_b, m_ffn_w_down, v_norm_mix, v_norm_ffn, v_norm_final, v_ev_w_in, v_ev_conv_w, v_ev_conv_b, v_ev_gate_a_w, v_ev_gate_a_b, v_ev_gate_x_w, v_ev_gate_x_b, v_ev_lru_lambda, v_hg_lb_logits, v_ev_hg_norm, v_ev_w_out, v_od_w_in, v_od_b_in, v_od_ln_g, v_od_ln_b, v_od_w_s, v_od_b_s, v_od_w_out, v_ffn_w_up, v_ffn_conv_w, v_ffn_conv_b, v_ffn_w_down)
    names = ["x"] + WEIGHTS + ["loss_target"] + ["m_" + n for n in WEIGHTS] + ["v_" + n for n in WEIGHTS]
    out = _step(dict(zip(names, vals)))
    return (out["loss"], out["grad_x"], *[out["grad_" + n] for n in WEIGHTS], *[out["delta_" + n] for n in WEIGHTS],
            *[out["new_m_" + n] for n in WEIGHTS], *[out["new_v_" + n] for n in WEIGHTS])
```

```python
import functools

import jax
import jax.numpy as jnp
from jax import lax
from jax.experimental import pallas as pl
from jax.experimental.pallas import tpu as pltpu

F32 = jnp.float32
BF16 = jnp.bfloat16

EPS = 1e-6
D_MODEL = 1024
LRU_W = 512
LRU_BLOCKS = 8
LRU_C = 8.0
HG_HEADS = 4
HG_D = 128
HG_CHUNK = 64
SGU_G = 8
SGU_CHUNK = 128
D_FF = 2816
ADAM_LR, ADAM_B1, ADAM_B2, ADAM_EPS, ADAM_WD, ADAM_STEP = 0.001, 0.9, 0.999, 1e-08, 0.01, 10

V7X_VMEM_BYTES = 64 * 1024 * 1024
VMEM_LIMIT = V7X_VMEM_BYTES - 8 * 1024 * 1024
SUBLANES = 8
LANES = 128
BF16_ROWS = 16

GELU_C0 = 0.7978845608028654
GELU_C1 = 0.044715

NN = (((1,), (0,)), ((), ()))
NT = (((1,), (1,)), ((), ()))
TN = (((0,), (0,)), ((), ()))


def _dot(a, b, dims=NN):
    return lax.dot_general(a.astype(BF16), b.astype(BF16), dims, preferred_element_type=F32)


def _cp(n_grid):
    return pltpu.CompilerParams(dimension_semantics=("arbitrary",) * n_grid, vmem_limit_bytes=VMEM_LIMIT)


def _chunk(n, cap):
    best = LANES
    for c in range(LANES, cap + 1, LANES):
        if n % c == 0:
            best = c
    return best


def _resident(shape):
    nd = len(shape)
    return pl.BlockSpec(shape, lambda *_: (0,) * nd, pipeline_mode=pl.Buffered(1))


def _rsum8(x):
    r, c = x.shape
    return x.reshape(r // SUBLANES, SUBLANES, c).sum(axis=0)


def _sigmoid(x):
    return 1.0 / (1.0 + jnp.exp(-x))


def _gelu(x):
    return 0.5 * x * (1.0 + jnp.tanh(GELU_C0 * (x + GELU_C1 * x * x * x)))


def _gelu_grad(x):
    t = jnp.tanh(GELU_C0 * (x + GELU_C1 * x * x * x))
    return 0.5 * (1.0 + t) + 0.5 * x * (1.0 - t * t) * GELU_C0 * (1.0 + 3.0 * GELU_C1 * x * x)


def _silu_and_grad(x):
    s = _sigmoid(x)
    return x * s, s * (1.0 + x * (1.0 - s))


def _shift_rows(e, j):
    n = e.shape[0]
    return e if j % n == 0 else pltpu.roll(e, j % n, 0)


def _norm_mm(h, g, w, b, name, tt=512):
    T, D = h.shape
    N = w.shape[1]
    cn = _chunk(N, 512)

    def body(h_ref, g_ref, w_ref, b_ref, hn_ref, z_ref):
        x = h_ref[...]
        r = lax.rsqrt(jnp.mean(x * x, axis=-1, keepdims=True) + EPS)
        hn = (x * r * g_ref[...]).astype(BF16)
        hn_ref[...] = hn
        for j in range(0, N, cn):
            acc = jnp.dot(hn, w_ref[:, j:j + cn], preferred_element_type=F32) + b_ref[:, j:j + cn]
            z_ref[:, j:j + cn] = acc.astype(BF16)

    return pl.pallas_call(
        body, name=name, grid=(T // tt,),
        in_specs=[pl.BlockSpec((tt, D), lambda i: (i, 0)), _resident((1, D)), _resident((D, N)), _resident((1, N))],
        out_specs=[pl.BlockSpec((tt, D), lambda i: (i, 0)), pl.BlockSpec((tt, N), lambda i: (i, 0))],
        out_shape=[jax.ShapeDtypeStruct((T, D), BF16), jax.ShapeDtypeStruct((T, N), BF16)],
        compiler_params=_cp(1),
    )(h, g, w, b)


def _mm(a, w, res, out_dtype, name, tt=512, transpose_w=False):
    T, K = a.shape
    N = w.shape[0] if transpose_w else w.shape[1]
    cn = _chunk(N, 512)
    has_res = res is not None

    def body(*refs):
        a_ref, w_ref = refs[0], refs[1]
        res_ref = refs[2] if has_res else None
        o_ref = refs[-1]
        av = a_ref[...].astype(BF16)
        for j in range(0, N, cn):
            if transpose_w:
                acc = lax.dot_general(av, w_ref[j:j + cn, :], NT, preferred_element_type=F32)
            else:
                acc = jnp.dot(av, w_ref[:, j:j + cn], preferred_element_type=F32)
            if has_res:
                acc = acc + res_ref[:, j:j + cn]
            o_ref[:, j:j + cn] = acc.astype(out_dtype)

    in_specs = [pl.BlockSpec((tt, K), lambda i: (i, 0)), _resident(w.shape)]
    args = [a, w]
    if has_res:
        in_specs.append(pl.BlockSpec((tt, N), lambda i: (i, 0)))
        args.append(res)
    return pl.pallas_call(
        body, name=name, grid=(T // tt,), in_specs=in_specs,
        out_specs=pl.BlockSpec((tt, N), lambda i: (i, 0)),
        out_shape=jax.ShapeDtypeStruct((T, N), out_dtype),
        compiler_params=_cp(1),
    )(*args)


def _mm_tn(a, b, name, tk, tt=512, col_shards=1):
    T, K = a.shape
    N = b.shape[1]
    ns = N // col_shards

    def body(a_ref, b_ref, o_ref):
        acc = lax.dot_general(a_ref[...].astype(BF16), b_ref[...].astype(BF16), TN, preferred_element_type=F32)
        first = pl.program_id(1) == 0
        if col_shards == 1:
            prev = jnp.where(first, 0.0, o_ref[...])
            o_ref[...] = prev + acc
        else:
            for s in range(col_shards):
                prev = jnp.where(first, 0.0, o_ref[s])
                o_ref[s] = prev + acc[:, s * ns:(s + 1) * ns]

    if col_shards == 1:
        out_spec = pl.BlockSpec((tk, N), lambda k, t: (k, 0))
        out_shape = jax.ShapeDtypeStruct((K, N), F32)
    else:
        out_spec = pl.BlockSpec((col_shards, tk, ns), lambda k, t: (0, k, 0))
        out_shape = jax.ShapeDtypeStruct((col_shards, K, ns), F32)
    return pl.pallas_call(
        body, name=name, grid=(K // tk, T // tt),
        in_specs=[pl.BlockSpec((tt, tk), lambda k, t: (t, k)), pl.BlockSpec((tt, N), lambda k, t: (t, 0))],
        out_specs=out_spec, out_shape=out_shape,
        compiler_params=_cp(2),
    )(a, b)


def _mm_normbwd(dz, w, x, g, dres, name, tt=512):
    T, N = dz.shape
    D = w.shape[0]

    def body(dz_ref, wt_ref, x_ref, g_ref, dres_ref, dx_ref, dg_ref):
        @pl.when(pl.program_id(0) == 0)
        def _():
            dg_ref[...] = jnp.zeros_like(dg_ref)

        dy = lax.dot_general(dz_ref[...], wt_ref[...], NT, preferred_element_type=F32)
        x = x_ref[...]
        r = lax.rsqrt(jnp.mean(x * x, axis=-1, keepdims=True) + EPS)
        xn = x * r
        dg_ref[...] += _rsum8(dy * xn)
        dxn = dy * g_ref[...]
        dx_ref[...] = dres_ref[...] + r * (dxn - xn * jnp.mean(dxn * xn, axis=-1, keepdims=True))

    return pl.pallas_call(
        body, name=name, grid=(T // tt,),
        in_specs=[pl.BlockSpec((tt, N), lambda i: (i, 0)), _resident((D, N)), pl.BlockSpec((tt, D), lambda i: (i, 0)),
                  _resident((1, D)), pl.BlockSpec((tt, D), lambda i: (i, 0))],
        out_specs=[pl.BlockSpec((tt, D), lambda i: (i, 0)), pl.BlockSpec((SUBLANES, D), lambda i: (0, 0))],
        out_shape=[jax.ShapeDtypeStruct((T, D), F32), jax.ShapeDtypeStruct((SUBLANES, D), F32)],
        compiler_params=_cp(1),
    )(dz, w, x, g, dres)


def _final_loss(h, g, tgt, name="final_loss", tt=512):
    T, D = h.shape

    def body(h_ref, g_ref, t_ref, dh_ref, dg_ref, sq_ref):
        @pl.when(pl.program_id(0) == 0)
        def _():
            dg_ref[...] = jnp.zeros_like(dg_ref)
            sq_ref[...] = jnp.zeros_like(sq_ref)

        x = h_ref[...]
        r = lax.rsqrt(jnp.mean(x * x, axis=-1, keepdims=True) + EPS)
        xn = x * r
        gv = g_ref[...]
        diff = xn * gv - t_ref[...]
        sq_ref[...] += _rsum8(diff * diff)
        dy = diff * (1.0 / D)
        dg_ref[...] += _rsum8(dy * xn)
        dxn = dy * gv
        dh_ref[...] = r * (dxn - xn * jnp.mean(dxn * xn, axis=-1, keepdims=True))

    return pl.pallas_call(
        body, name=name, grid=(T // tt,),
        in_specs=[pl.BlockSpec((tt, D), lambda i: (i, 0)), _resident((1, D)), pl.BlockSpec((tt, D), lambda i: (i, 0))],
        out_specs=[pl.BlockSpec((tt, D), lambda i: (i, 0)), pl.BlockSpec((SUBLANES, D), lambda i: (0, 0)),
                   pl.BlockSpec((SUBLANES, D), lambda i: (0, 0))],
        out_shape=[jax.ShapeDtypeStruct((T, D), F32), jax.ShapeDtypeStruct((SUBLANES, D), F32),
                   jax.ShapeDtypeStruct((SUBLANES, D), F32)],
        compiler_params=_cp(1),
    )(h, g, tgt)


def _ffn_act(gu, cw, cb, name, tt=512):
    T = gu.shape[0]
    F = gu.shape[1] // 2
    cc = _chunk(F, 256)
    hb = tt // BF16_ROWS

    def body(gu_ref, halo_ref, cw_ref, cb_ref, a_ref):
        first = pl.program_id(0) == 0
        for c0 in range(0, F, cc):
            cs = slice(c0, c0 + cc)
            x = gu_ref[:, cs].astype(F32)
            halo = jnp.where(first, 0.0, halo_ref[:, cs].astype(F32))
            e = jnp.concatenate([halo, x], axis=0)
            gc = (cb_ref[:, cs] + cw_ref[0:1, cs] * _shift_rows(e, 2)[BF16_ROWS:] + cw_ref[1:2, cs] * _shift_rows(e, 1)[BF16_ROWS:]
                  + cw_ref[2:3, cs] * x)
            up = gu_ref[:, F + c0:F + c0 + cc].astype(F32)
            a_ref[:, cs] = (gc * _sigmoid(gc) * up).astype(BF16)

    return pl.pallas_call(
        body, name=name, grid=(T // tt,),
        in_specs=[pl.BlockSpec((tt, 2 * F), lambda i: (i, 0)),
                  pl.BlockSpec((BF16_ROWS, F), lambda i: (jnp.maximum(i * hb - 1, 0), 0)),
                  _resident((SUBLANES, F)), _resident((1, F))],
        out_specs=pl.BlockSpec((tt, F), lambda i: (i, 0)),
        out_shape=jax.ShapeDtypeStruct((T, F), BF16),
        compiler_params=_cp(1),
    )(gu, gu, cw, cb)


def _ffn_act_bwd(gu, da, cw, cb, name, tt=512):
    T = gu.shape[0]
    F = gu.shape[1] // 2
    cc = _chunk(F, 256)
    hb = tt // BF16_ROWS
    last_hb = T // BF16_ROWS - 1
    nt = T // tt

    def body(gu_ref, gprev_ref, gunext_ref, da_ref, danext_ref, cw_ref, cb_ref, dgu_ref, dc_ref):
        i = pl.program_id(0)

        @pl.when(i == 0)
        def _():
            dc_ref[...] = jnp.zeros_like(dc_ref)

        n = tt + BF16_ROWS
        for c0 in range(0, F, cc):
            cs = slice(c0, c0 + cc)
            us = slice(F + c0, F + c0 + cc)
            g = gu_ref[:, cs].astype(F32)
            gp = jnp.where(i == 0, 0.0, gprev_ref[:, cs].astype(F32))
            ge = jnp.concatenate([gp, g, gunext_ref[:, cs].astype(F32)], axis=0)
            g1 = _shift_rows(ge, 1)[BF16_ROWS:]
            g2 = _shift_rows(ge, 2)[BF16_ROWS:]
            gc = cb_ref[:, cs] + cw_ref[0:1, cs] * g2 + cw_ref[1:2, cs] * g1 + cw_ref[2:3, cs] * ge[BF16_ROWS:]
            up = jnp.concatenate([gu_ref[:, us].astype(F32), gunext_ref[:, us].astype(F32)], axis=0)
            dan = jnp.where(i == nt - 1, 0.0, danext_ref[:, cs].astype(F32))
            dae = jnp.concatenate([da_ref[:, cs].astype(F32), dan], axis=0)
            s, ds = _silu_and_grad(gc)
            dgc = dae * up * ds
            dgu_ref[:, us] = (dae * s)[:tt].astype(BF16)
            dgate = cw_ref[2:3, cs] * dgc + cw_ref[1:2, cs] * _shift_rows(dgc, n - 1) + cw_ref[0:1, cs] * _shift_rows(dgc, n - 2)
            dgu_ref[:, cs] = dgate[:tt].astype(BF16)
            dm = dgc[:tt]
            dc_ref[0, :, cs] += _rsum8(dm * g2[:tt])
            dc_ref[1, :, cs] += _rsum8(dm * g1[:tt])
            dc_ref[2, :, cs] += _rsum8(dm * g)
            dc_ref[3, :, cs] += _rsum8(dm)

    return pl.pallas_call(
        body, name=name, grid=(nt,),
        in_specs=[pl.BlockSpec((tt, 2 * F), lambda i: (i, 0)),
                  pl.BlockSpec((BF16_ROWS, F), lambda i: (jnp.maximum(i * hb - 1, 0), 0)),
                  pl.BlockSpec((BF16_ROWS, 2 * F), lambda i: (jnp.minimum((i + 1) * hb, last_hb), 0)),
                  pl.BlockSpec((tt, F), lambda i: (i, 0)),
                  pl.BlockSpec((BF16_ROWS, F), lambda i: (jnp.minimum((i + 1) * hb, last_hb), 0)),
                  _resident((SUBLANES, F)), _resident((1, F))],
        out_specs=[pl.BlockSpec((tt, 2 * F), lambda i: (i, 0)), pl.BlockSpec((4, SUBLANES, F), lambda i: (0, 0, 0))],
        out_shape=[jax.ShapeDtypeStruct((T, 2 * F), BF16), jax.ShapeDtypeStruct((4, SUBLANES, F), F32)],
        compiler_params=_cp(1),
    )(gu, gu, gu, da, da, cw, cb)


def _softplus_neg(lam):
    x = -lam
    y = jnp.exp(-jnp.abs(x))
    l1p = jnp.where(y < 0.01, y * (1.0 - y * (0.5 - y * (1.0 / 3.0))), jnp.log(1.0 + y))
    return jnp.maximum(x, 0.0) + l1p


def _lru_gates(xc, wa_ref, ba_ref, wx_ref, bx_ref, sp):
    xcb = xc.astype(BF16)
    r = _sigmoid(jnp.dot(xcb, wa_ref[...], preferred_element_type=F32) + ba_ref[...])
    gi = _sigmoid(jnp.dot(xcb, wx_ref[...], preferred_element_type=F32) + bx_ref[...])
    log_a = -LRU_C * r * sp
    a = jnp.exp(log_a)
    x2 = 2.0 * log_a
    series = -x2 * (1.0 + x2 * 0.5 * (1.0 + x2 * (1.0 / 3.0) * (1.0 + x2 * 0.25 * (1.0 + x2 * 0.2))))
    om = jnp.where(x2 > -0.125, series, 1.0 - a * a)
    return r, gi, a, jnp.sqrt(om)


def _lru_conv(xr, halo, cw_ref, cb_ref):
    e = jnp.concatenate([halo, xr], axis=0)
    x1 = _shift_rows(e, 1)[BF16_ROWS:]
    x2 = _shift_rows(e, 2)[BF16_ROWS:]
    x3 = _shift_rows(e, 3)[BF16_ROWS:]
    xc = cb_ref[...] + cw_ref[0:1, :] * x3 + cw_ref[1:2, :] * x2 + cw_ref[2:3, :] * x1 + cw_ref[3:4, :] * xr
    return xc, x1, x2, x3


def _lru_fwd(z, cw, cb, wa, ba, wx, bx, lam, name="lru_fwd", tt=256):
    T = z.shape[0]
    W = LRU_W
    hb = tt // BF16_ROWS
    ng = tt // SUBLANES

    def body(z_ref, halo_ref, cw_ref, cb_ref, wa_ref, ba_ref, wx_ref, bx_ref, lam_ref, oa_ref, h_ref, a_s, u_s, hc):
        i = pl.program_id(0)

        @pl.when(i == 0)
        def _():
            hc[...] = jnp.zeros_like(hc)

        xr = z_ref[:, W:2 * W].astype(F32)
        halo = jnp.where(i == 0, 0.0, halo_ref[...].astype(F32))
        xc, _, _, _ = _lru_conv(xr, halo, cw_ref, cb_ref)
        sp = _softplus_neg(lam_ref[...])
        r, gi, a, mult = _lru_gates(xc, wa_ref, ba_ref, wx_ref, bx_ref, sp)
        a_s[...] = a
        u_s[...] = mult * gi * xc
        row = lax.broadcasted_iota(jnp.int32, (SUBLANES, W), 0)

        def step(j, hprev):
            r0 = pl.multiple_of(j * SUBLANES, SUBLANES)
            A = a_s[pl.ds(r0, SUBLANES), :]
            U = u_s[pl.ds(r0, SUBLANES), :]
            for k in (1, 2, 4):
                m = row >= k
                U = jnp.where(m, A * pltpu.roll(U, k, 0) + U, U)
                A = jnp.where(m, A * pltpu.roll(A, k, 0), A)
            H = U + A * hprev
            h_ref[pl.ds(r0, SUBLANES), :] = H
            return jnp.broadcast_to(H[SUBLANES - 1:SUBLANES, :], (SUBLANES, W))

        hc[...] = lax.fori_loop(0, ng, step, hc[...])
        oa_ref[...] = (_gelu(z_ref[:, 0:W].astype(F32)) * h_ref[...]).astype(BF16)

    return pl.pallas_call(
        body, name=name, grid=(T // tt,),
        in_specs=[pl.BlockSpec((tt, 2 * W), lambda i: (i, 0)),
                  pl.BlockSpec((BF16_ROWS, W), lambda i: (jnp.maximum(i * hb - 1, 0), 1)),
                  _resident((SUBLANES, W)), _resident((1, W)), _resident((W, W)), _resident((1, W)),
                  _resident((W, W)), _resident((1, W)), _resident((1, W))],
        out_specs=[pl.BlockSpec((tt, W), lambda i: (i, 0)), pl.BlockSpec((tt, W), lambda i: (i, 0))],
        out_shape=[jax.ShapeDtypeStruct((T, W), BF16), jax.ShapeDtypeStruct((T, W), F32)],
        scratch_shapes=[pltpu.VMEM((tt, W), F32), pltpu.VMEM((tt, W), F32), pltpu.VMEM((SUBLANES, W), F32)],
        compiler_params=_cp(1),
    )(z, z, cw, cb, wa, ba, wx, bx, lam)


def _lru_bwd(z, hseq, dmix, cw, cb, wa, wat, ba, wx, wxt, bx, lam, name="lru_bwd", tt=256):
    T = z.shape[0]
    W = LRU_W
    nt = T // tt
    hb = tt // BF16_ROWS
    sb = tt // SUBLANES
    ng = tt // SUBLANES

    def body(z_ref, halo_ref, h_ref, hprev_ref, dm_ref, cw_ref, cb_ref, wa_ref, wat_ref, ba_ref, wx_ref, wxt_ref, bx_ref,
             lam_ref, dz_ref, dc_ref, dwa_ref, dwx_ref, dv_ref, c_s, d_s, g_s, gc, an, dxn):
        i = pl.program_id(0)
        ti = nt - 1 - i

        @pl.when(i == 0)
        def _():
            dc_ref[...] = jnp.zeros_like(dc_ref)
            dwa_ref[...] = jnp.zeros_like(dwa_ref)
            dwx_ref[...] = jnp.zeros_like(dwx_ref)
            dv_ref[...] = jnp.zeros_like(dv_ref)
            gc[...] = jnp.zeros_like(gc)
            an[...] = jnp.zeros_like(an)
            dxn[...] = jnp.zeros_like(dxn)

        xr = z_ref[:, W:2 * W].astype(F32)
        yg = z_ref[:, 0:W].astype(F32)
        halo = jnp.where(ti == 0, 0.0, halo_ref[...].astype(F32))
        xc, x1, x2, x3 = _lru_conv(xr, halo, cw_ref, cb_ref)
        sp = _softplus_neg(lam_ref[...])
        r, gi, a, mult = _lru_gates(xc, wa_ref, ba_ref, wx_ref, bx_ref, sp)
        h = h_ref[...]
        hp = jnp.where(ti == 0, 0.0, hprev_ref[...])
        hm1 = _shift_rows(jnp.concatenate([hp, h], axis=0), 1)[SUBLANES:]
        dout = dm_ref[...].astype(F32)
        d_s[...] = dout * _gelu(yg)
        dz_ref[:, 0:W] = (dout * h * _gelu_grad(yg)).astype(BF16)
        c_s[...] = _shift_rows(jnp.concatenate([a, an[...]], axis=0), tt + SUBLANES - 1)[:tt]
        an[...] = a[0:SUBLANES, :]
        row = lax.broadcasted_iota(jnp.int32, (SUBLANES, W), 0)

        def step(j, gnext):
            r0 = pl.multiple_of((ng - 1 - j) * SUBLANES, SUBLANES)
            C = c_s[pl.ds(r0, SUBLANES), :]
            G = d_s[pl.ds(r0, SUBLANES), :]
            for k in (1, 2, 4):
                m = row < SUBLANES - k
                G = jnp.where(m, G + C * pltpu.roll(G, SUBLANES - k, 0), G)
                C = jnp.where(m, C * pltpu.roll(C, SUBLANES - k, 0), C)
            G = G + C * gnext
            g_s[pl.ds(r0, SUBLANES), :] = G
            return jnp.broadcast_to(G[0:1, :], (SUBLANES, W))

        gc[...] = lax.fori_loop(0, ng, step, gc[...])
        du = g_s[...]
        da = du * hm1
        dgi = du * mult * xc
        dxc = du * mult * gi
        dmult = du * gi * xc
        dlog_a = da * a - dmult * (a * a) / mult
        dr = dlog_a * (-LRU_C * sp)
        dv_ref[2] += _rsum8(dlog_a * (-LRU_C * r))
        dpr = (dr * r * (1.0 - r)).astype(BF16)
        dpi = (dgi * gi * (1.0 - gi)).astype(BF16)
        dv_ref[0] += _rsum8(dpr.astype(F32))
        dv_ref[1] += _rsum8(dpi.astype(F32))
        xcb = xc.astype(BF16)
        dwa_ref[...] += lax.dot_general(xcb, dpr, TN, preferred_element_type=F32)
        dwx_ref[...] += lax.dot_general(xcb, dpi, TN, preferred_element_type=F32)
        dxc = dxc + jnp.dot(dpr, wat_ref[...], preferred_element_type=F32) + jnp.dot(dpi, wxt_ref[...], preferred_element_type=F32)
        n = tt + BF16_ROWS
        de = jnp.concatenate([dxc, dxn[...]], axis=0)
        dxr = (cw_ref[3:4, :] * dxc + cw_ref[2:3, :] * _shift_rows(de, n - 1)[:tt] + cw_ref[1:2, :] * _shift_rows(de, n - 2)[:tt]
               + cw_ref[0:1, :] * _shift_rows(de, n - 3)[:tt])
        dxn[...] = dxc[0:BF16_ROWS, :]
        dz_ref[:, W:2 * W] = dxr.astype(BF16)
        dc_ref[0] += _rsum8(dxc * x3)
        dc_ref[1] += _rsum8(dxc * x2)
        dc_ref[2] += _rsum8(dxc * x1)
        dc_ref[3] += _rsum8(dxc * xr)
        dc_ref[4] += _rsum8(dxc)

    rev = lambda i: nt - 1 - i
    return pl.pallas_call(
        body, name=name, grid=(nt,),
        in_specs=[pl.BlockSpec((tt, 2 * W), lambda i: (rev(i), 0)),
                  pl.BlockSpec((BF16_ROWS, W), lambda i: (jnp.maximum(rev(i) * hb - 1, 0), 1)),
                  pl.BlockSpec((tt, W), lambda i: (rev(i), 0)),
                  pl.BlockSpec((SUBLANES, W), lambda i: (jnp.maximum(rev(i) * sb - 1, 0), 0)),
                  pl.BlockSpec((tt, W), lambda i: (rev(i), 0)),
                  _resident((SUBLANES, W)), _resident((1, W)), _resident((W, W)), _resident((W, W)), _resident((1, W)),
                  _resident((W, W)), _resident((W, W)), _resident((1, W)), _resident((1, W))],
        out_specs=[pl.BlockSpec((tt, 2 * W), lambda i: (rev(i), 0)),
                   pl.BlockSpec((5, SUBLANES, W), lambda i: (0, 0, 0)),
                   pl.BlockSpec((W, W), lambda i: (0, 0)), pl.BlockSpec((W, W), lambda i: (0, 0)),
                   pl.BlockSpec((3, SUBLANES, W), lambda i: (0, 0, 0))],
        out_shape=[jax.ShapeDtypeStruct((T, 2 * W), BF16), jax.ShapeDtypeStruct((5, SUBLANES, W), F32),
                   jax.ShapeDtypeStruct((W, W), F32), jax.ShapeDtypeStruct((W, W), F32),
                   jax.ShapeDtypeStruct((3, SUBLANES, W), F32)],
        scratch_shapes=[pltpu.VMEM((tt, W), F32), pltpu.VMEM((tt, W), F32), pltpu.VMEM((tt, W), F32),
                        pltpu.VMEM((SUBLANES, W), F32), pltpu.VMEM((SUBLANES, W), F32), pltpu.VMEM((BF16_ROWS, W), F32)],
        compiler_params=_cp(1),
    )(z, z, hseq, hseq, dmix, cw, cb, wa, wat, ba, wx, wxt, bx, lam)


def _split3(x):
    hi = x.astype(BF16)
    r1 = x - hi.astype(F32)
    mid = r1.astype(BF16)
    lo = (r1 - mid.astype(F32)).astype(BF16)
    return hi, mid, lo


def _tri_matmul(tri, x):
    hi, mid, lo = _split3(x)
    return (jnp.dot(tri, hi, preferred_element_type=F32) + jnp.dot(tri, mid, preferred_element_type=F32)
            + jnp.dot(tri, lo, preferred_element_type=F32))


def _hg_chunk(q, fl, lb):
    C = q.shape[0]
    ri = lax.broadcasted_iota(jnp.int32, (C, C), 0)
    ci = lax.broadcasted_iota(jnp.int32, (C, C), 1)
    causal = ri >= ci
    sig = _sigmoid(fl)
    f = lb + (1.0 - lb) * sig
    k = 1.0 - f
    sq = _sigmoid(q)
    qf = q * sq
    b = _tri_matmul(causal.astype(BF16), jnp.log(f))
    bm = b[C // 2 - 1:C // 2, :]
    bl = b[C - 1:C, :]
    qt = qf * jnp.exp(b - bm)
    kt = k * jnp.exp(bm - b)
    qin = qf * jnp.exp(b)
    kout = k * jnp.exp(bl - b)
    att = jnp.where(causal, _dot(qt, kt, NT), 0.0)
    return dict(sig=sig, f=f, k=k, sq=sq, qf=qf, b=b, bm=bm, bl=bl, qt=qt, kt=kt, qin=qin, kout=kout, att=att,
                causal=causal, anti=ri <= ci, decay=jnp.exp(bl))


def _hgrn_fwd(z, lb, gn, name="hgrn_fwd", tt=256):
    T = z.shape[0]
    C = HG_CHUNK
    nc = tt // C
    Dh = HG_D

    def body(q_ref, f_ref, v_ref, g_ref, lb_ref, gn_ref, o_ref, ss_ref, st):
        @pl.when(pl.program_id(1) == 0)
        def _():
            st[...] = jnp.zeros_like(st)

        S = st[...]
        for c in range(nc):
            rows = slice(c * C, (c + 1) * C)
            ck = _hg_chunk(q_ref[rows, :].astype(F32), f_ref[rows, :].astype(F32), lb_ref[...])
            v = v_ref[rows, :]
            g = g_ref[rows, :].astype(F32)
            ss_ref[0, c] = S
            o = _dot(ck["att"], v) + _dot(ck["qin"], S, NT)
            S = ck["decay"] * S + _dot(v, ck["kout"], TN)
            rn = lax.rsqrt(jnp.mean(o * o, axis=-1, keepdims=True) + EPS)
            o_ref[rows, :] = (o * rn * gn_ref[...] * (g * _sigmoid(g))).astype(BF16)
        st[...] = S

    col = lambda base: (lambda h, i: (i, base + h))
    return pl.pallas_call(
        body, name=name, grid=(HG_HEADS, T // tt),
        in_specs=[pl.BlockSpec((tt, Dh), col(8)), pl.BlockSpec((tt, Dh), col(12)), pl.BlockSpec((tt, Dh), col(16)),
                  pl.BlockSpec((tt, Dh), col(20)), pl.BlockSpec((1, Dh), lambda h, i: (0, h)),
                  pl.BlockSpec((1, Dh), lambda h, i: (0, 0))],
        out_specs=[pl.BlockSpec((tt, Dh), lambda h, i: (i, h)),
                   pl.BlockSpec((1, nc, Dh, Dh), lambda h, i: (h, i, 0, 0))],
        out_shape=[jax.ShapeDtypeStruct((T, HG_HEADS * Dh), BF16),
                   jax.ShapeDtypeStruct((HG_HEADS, T // C, Dh, Dh), F32)],
        scratch_shapes=[pltpu.VMEM((Dh, Dh), F32)],
        compiler_params=_cp(2),
    )(z, z, z, z, lb, gn)


def _hgrn_bwd(z, ss, dmix, lb, gn, name="hgrn_bwd", tt=256):
    T = z.shape[0]
    C = HG_CHUNK
    nc = tt // C
    nt = T // tt
    Dh = HG_D

    def body(q_ref, f_ref, v_ref, g_ref, ss_ref, dm_ref, lb_ref, gn_ref, dq_ref, df_ref, dv_ref, dg_ref, dlb_ref, dgn_ref, dst):
        @pl.when(pl.program_id(1) == 0)
        def _():
            dst[...] = jnp.zeros_like(dst)
            dlb_ref[...] = jnp.zeros_like(dlb_ref)
            dgn_ref[...] = jnp.zeros_like(dgn_ref)

        dS = dst[...]
        lbv = lb_ref[...]
        gnv = gn_ref[...]
        rowc = lax.broadcasted_iota(jnp.int32, (C, Dh), 0)
        for c in reversed(range(nc)):
            rows = slice(c * C, (c + 1) * C)
            q = q_ref[rows, :].astype(F32)
            ck = _hg_chunk(q, f_ref[rows, :].astype(F32), lbv)
            v = v_ref[rows, :]
            g = g_ref[rows, :].astype(F32)
            S = ss_ref[0, c]
            o = _dot(ck["att"], v) + _dot(ck["qin"], S, NT)
            rn = lax.rsqrt(jnp.mean(o * o, axis=-1, keepdims=True) + EPS)
            on = o * rn
            dout = dm_ref[rows, :].astype(F32)
            sg, dsg = _silu_and_grad(g)
            d_ong = dout * sg
            dgn_ref[...] += _rsum8(d_ong * on)
            dg_ref[rows, :] = (dout * on * gnv * dsg).astype(BF16)
            don = d_ong * gnv
            do = rn * (don - on * jnp.mean(don * on, axis=-1, keepdims=True))
            dv_ref[rows, :] = (_dot(ck["att"], do, TN) + _dot(ck["kout"], dS, NT)).astype(BF16)
            datt = jnp.where(ck["causal"], _dot(do, v, NT), 0.0)
            dqt = _dot(datt, ck["kt"])
            dkt = _dot(datt, ck["qt"], TN)
            dqin = _dot(do, S)
            dkout = _dot(v, dS)
            ddecay = jnp.sum(dS * S, axis=0, keepdims=True)
            dS = _dot(do, ck["qin"], TN) + ck["decay"] * dS
            b, bm, bl = ck["b"], ck["bm"], ck["bl"]
            dqf = dqt * jnp.exp(b - bm) + dqin * jnp.exp(b)
            dk = dkt * jnp.exp(bm - b) + dkout * jnp.exp(bl - b)
            kk = dkout * ck["kout"]
            db = dqt * ck["qt"] - dkt * ck["kt"] + dqin * ck["qin"] - kk
            dbl = jnp.sum(kk, axis=0, keepdims=True) + ddecay * ck["decay"]
            db = db + jnp.where(rowc == C - 1, dbl, 0.0)
            dlogf = _tri_matmul(ck["anti"].astype(BF16), db)
            dfv = dlogf / ck["f"] - dk
            sig = ck["sig"]
            df_ref[rows, :] = (dfv * (1.0 - lbv) * sig * (1.0 - sig)).astype(BF16)
            dlb_ref[...] += _rsum8(dfv * (1.0 - sig))
            sq = ck["sq"]
            dq_ref[rows, :] = (dqf * (sq * (1.0 + q * (1.0 - sq)))).astype(BF16)
        dst[...] = dS

    rev = lambda i: nt - 1 - i
    col = lambda base: (lambda h, i: (rev(i), base + h))
    out_tok = pl.BlockSpec((tt, Dh), lambda h, i: (rev(i), h))
    acc = pl.BlockSpec((SUBLANES, Dh), lambda h, i: (h, 0))
    tok_shape = jax.ShapeDtypeStruct((T, HG_HEADS * Dh), BF16)
    acc_shape = jax.ShapeDtypeStruct((HG_HEADS * SUBLANES, Dh), F32)
    return pl.pallas_call(
        body, name=name, grid=(HG_HEADS, nt),
        in_specs=[pl.BlockSpec((tt, Dh), col(8)), pl.BlockSpec((tt, Dh), col(12)), pl.BlockSpec((tt, Dh), col(16)),
                  pl.BlockSpec((tt, Dh), col(20)),
                  pl.BlockSpec((1, nc, Dh, Dh), lambda h, i: (h, rev(i), 0, 0)),
                  pl.BlockSpec((tt, Dh), lambda h, i: (rev(i), 4 + h)),
                  pl.BlockSpec((1, Dh), lambda h, i: (0, h)), pl.BlockSpec((1, Dh), lambda h, i: (0, 0))],
        out_specs=[out_tok, out_tok, out_tok, out_tok, acc, acc],
        out_shape=[tok_shape, tok_shape, tok_shape, tok_shape, acc_shape, acc_shape],
        scratch_shapes=[pltpu.VMEM((Dh, Dh), F32)],
        compiler_params=_cp(2),
    )(z, z, z, z, ss, dmix, lb, gn)


def _sgu_core(p, lg_ref, lb_ref, wsc_ref, bsb_ref):
    Wd = D_MODEL
    G = SGU_CHUNK
    zz = _gelu(p)
    u = zz[:, :Wd]
    v = zz[:, Wd:]
    vc = v - jnp.mean(v, axis=-1, keepdims=True)
    rstd = lax.rsqrt(jnp.mean(vc * vc, axis=-1, keepdims=True) + EPS)
    vhat = vc * rstd
    vn = vhat * lg_ref[...] + lb_ref[...]
    svs = []
    for gi in range(SGU_G):
        svs.append(jnp.dot(wsc_ref[gi], vn[:, gi * G:(gi + 1) * G].astype(BF16), preferred_element_type=F32) + bsb_ref[gi])
    return u, vhat, rstd, vn, jnp.concatenate(svs, axis=1)


def _sgu_fwd(p1, lg, lbias, wsc, bsb, name="sgu_fwd", tt=512):
    T = p1.shape[0]
    Wd = D_MODEL
    C = SGU_CHUNK

    def body(p_ref, lg_ref, lb_ref, wsc_ref, bsb_ref, s_ref):
        for c in range(tt // C):
            rows = slice(c * C, (c + 1) * C)
            u, _, _, _, sv = _sgu_core(p_ref[rows, :].astype(F32), lg_ref, lb_ref, wsc_ref, bsb_ref)
            s_ref[rows, :] = (u * sv).astype(BF16)

    return pl.pallas_call(
        body, name=name, grid=(T // tt,),
        in_specs=[pl.BlockSpec((tt, 2 * Wd), lambda i: (i, 0)), _resident((1, Wd)), _resident((1, Wd)),
                  _resident((SGU_G, C, C)), _resident((SGU_G, C, C))],
        out_specs=pl.BlockSpec((tt, Wd), lambda i: (i, 0)),
        out_shape=jax.ShapeDtypeStruct((T, Wd), BF16),
        compiler_params=_cp(1),
    )(p1, lg, lbias, wsc, bsb)


def _sgu_bwd(p1, ds, lg, lbias, wsc, wsct, bsb, name="sgu_bwd", tt=512):
    T = p1.shape[0]
    Wd = D_MODEL
    C = SGU_CHUNK

    def body(p_ref, ds_ref, lg_ref, lb_ref, wsc_ref, wsct_ref, bsb_ref, dp_ref, dws_ref, dbs_ref, dlg_ref, dlb_ref, dbin_ref):
        @pl.when(pl.program_id(0) == 0)
        def _():
            dws_ref[...] = jnp.zeros_like(dws_ref)
            dbs_ref[...] = jnp.zeros_like(dbs_ref)
            dlg_ref[...] = jnp.zeros_like(dlg_ref)
            dlb_ref[...] = jnp.zeros_like(dlb_ref)
            dbin_ref[...] = jnp.zeros_like(dbin_ref)

        for c in range(tt // C):
            rows = slice(c * C, (c + 1) * C)
            p = p_ref[rows, :].astype(F32)
            u, vhat, rstd, vn, sv = _sgu_core(p, lg_ref, lb_ref, wsc_ref, bsb_ref)
            dsc = ds_ref[rows, :].astype(F32)
            du = dsc * sv
            dsv = dsc * u
            dvns = []
            for gi in range(SGU_G):
                cs = slice(gi * C, (gi + 1) * C)
                dsv_g = dsv[:, cs]
                dvns.append(jnp.dot(wsct_ref[gi], dsv_g.astype(BF16), preferred_element_type=F32))
                dws_ref[gi] += _dot(dsv_g, vn[:, cs], NT)
                dbs_ref[gi] += dsv_g
            dvn = jnp.concatenate(dvns, axis=1)
            dlg_ref[...] += _rsum8(dvn * vhat)
            dlb_ref[...] += _rsum8(dvn)
            dvh = dvn * lg_ref[...]
            dv = rstd * (dvh - jnp.mean(dvh, axis=-1, keepdims=True) - vhat * jnp.mean(dvh * vhat, axis=-1, keepdims=True))
            dp = jnp.concatenate([du, dv], axis=1) * _gelu_grad(p)
            dbin_ref[...] += _rsum8(dp)
            dp_ref[rows, :] = dp.astype(BF16)

    full3 = pl.BlockSpec((SGU_G, C, C), lambda i: (0, 0, 0))
    return pl.pallas_call(
        body, name=name, grid=(T // tt,),
        in_specs=[pl.BlockSpec((tt, 2 * Wd), lambda i: (i, 0)), pl.BlockSpec((tt, Wd), lambda i: (i, 0)),
                  _resident((1, Wd)), _resident((1, Wd)), _resident((SGU_G, C, C)), _resident((SGU_G, C, C)),
                  _resident((SGU_G, C, C))],
        out_specs=[pl.BlockSpec((tt, 2 * Wd), lambda i: (i, 0)), full3, full3,
                   pl.BlockSpec((SUBLANES, Wd), lambda i: (0, 0)), pl.BlockSpec((SUBLANES, Wd), lambda i: (0, 0)),
                   pl.BlockSpec((SUBLANES, 2 * Wd), lambda i: (0, 0))],
        out_shape=[jax.ShapeDtypeStruct((T, 2 * Wd), BF16), jax.ShapeDtypeStruct((SGU_G, C, C), F32),
                   jax.ShapeDtypeStruct((SGU_G, C, C), F32), jax.ShapeDtypeStruct((SUBLANES, Wd), F32),
                   jax.ShapeDtypeStruct((SUBLANES, Wd), F32), jax.ShapeDtypeStruct((SUBLANES, 2 * Wd), F32)],
        compiler_params=_cp(1),
    )(p1, ds, lg, lbias, wsc, wsct, bsb)


def _pad_rows(w, rows=SUBLANES):
    return jnp.pad(w, ((0, rows - w.shape[0]), (0, 0)))


def _block_diag(w):
    n, b, _ = w.shape
    return (w[:, :, None, :] * jnp.eye(n, dtype=w.dtype)[:, None, :, None]).reshape(n * b, n * b)


def _diag_blocks(m, n):
    b = m.shape[0] // n
    m4 = m.reshape(n, b, n, b)
    return jnp.stack([m4[k, :, k, :] for k in range(n)], axis=0)


def _piece_major(dw):
    if dw.ndim == 2:
        K, N = dw.shape
        return dw.reshape(N_CHIPS, 2, K // (2 * N_CHIPS), N)
    _, K, ns = dw.shape
    return dw.reshape(N_CHIPS, 2, K // 2, ns)


def _ffn_fwd(h, g, w_up, cw, cb, w_down, tag):
    hn, gu = _norm_mm(h, g, w_up, jnp.zeros((1, w_up.shape[1]), F32), name=f"ffn_up_{tag}")
    a = _ffn_act(gu, cw, cb, name=f"ffn_act_{tag}")
    out = _mm(a, w_down, h, F32, name=f"ffn_down_{tag}")
    return out, (hn, gu, a)


def _ffn_bwd(dh, h, g, saved, w_up, cw, cb, w_down, tag):
    hn, gu, a = saved
    da = _mm(dh, w_down, None, BF16, name=f"ffn_da_{tag}", transpose_w=True)
    dwd = _mm_tn(a, dh, name=f"ffn_dwd_{tag}", tk=D_FF // 2)
    dgu, dc = _ffn_act_bwd(gu, da, cw, cb, name=f"ffn_actb_{tag}")
    dhin, dg8 = _mm_normbwd(dgu, w_up, h, g, dh, name=f"ffn_dh_{tag}")
    dwu = _mm_tn(hn, dgu, name=f"ffn_dwu_{tag}", tk=256, col_shards=N_CHIPS)
    dcs = dc.sum(axis=1)
    return dhin, dg8.sum(axis=0), dwu, dcs[0:3], dcs[3], dwd


REDUCE_GROUPS = {"g1": [("ffn_w_up", 1), ("ffn_w_down", 1), ("od_w_out", 0), ("od_w_in", 0)],
                 "g2": [("ffn_w_up", 0), ("ffn_w_down", 0)],
                 "g3": [("ev_w_out", 0), ("ev_w_in", 0)]}


def _local_step(x, tgt, p, start_reduce=None):
    row = lambda v: v.reshape(1, -1)
    grads = {}

    lower = jax.nn.softmax(p["hg_lb_logits"], axis=0)
    lb0 = row(lower[0])
    ev_cw = _pad_rows(p["ev_conv_w"][0])
    ev_cb = row(p["ev_conv_b"][0])
    wa = _block_diag(p["ev_gate_a_w"][0]).astype(BF16)
    wx = _block_diag(p["ev_gate_x_w"][0]).astype(BF16)
    ba, bx, lam = row(p["ev_gate_a_b"][0]), row(p["ev_gate_x_b"][0]), row(p["ev_lru_lambda"][0])
    gn = row(p["ev_hg_norm"][0])
    tril = jnp.tril(jnp.ones((SGU_CHUNK, SGU_CHUNK), F32))
    wsc = (p["od_w_s"][0] * tril).astype(BF16)
    bsb = jnp.broadcast_to(p["od_b_s"][0][:, :, None], (SGU_G, SGU_CHUNK, SGU_CHUNK)).astype(F32)
    ffn_cw = [_pad_rows(p["ffn_conv_w"][l]) for l in range(2)]
    ffn_cb = [row(p["ffn_conv_b"][l]) for l in range(2)]
    ev_w_in, ev_w_out = p["ev_w_in"][0], p["ev_w_out"][0]
    nm = [row(p["norm_mix"][l]) for l in range(2)]
    nf = [row(p["norm_ffn"][l]) for l in range(2)]

    h0 = x
    hn0, z0 = _norm_mm(h0, nm[0], ev_w_in, jnp.zeros((1, ev_w_in.shape[1]), F32), name="ev_in")
    out_a, hseq = _lru_fwd(z0, ev_cw, ev_cb, wa, ba, wx, bx, lam)
    out_b, ss = _hgrn_fwd(z0, lb0, gn)
    mix0 = jnp.concatenate([out_a, out_b], axis=1)
    h1 = _mm(mix0, ev_w_out, h0, F32, name="ev_out")
    late = p["late"](h1) if "late" in p else p
    od_w_in, od_w_out = late["od_w_in"][0], late["od_w_out"][0]
    w_up = [late["ffn_w_up"][l] for l in range(2)]
    w_down = [late["ffn_w_down"][l] for l in range(2)]
    h2, ffn0 = _ffn_fwd(h1, nf[0], w_up[0], ffn_cw[0], ffn_cb[0], w_down[0], "l0")
    hn1, p1 = _norm_mm(h2, nm[1], od_w_in, row(p["od_b_in"][0]), name="od_in")
    s1 = _sgu_fwd(p1, row(p["od_ln_g"][0]), row(p["od_ln_b"][0]), wsc, bsb)
    h3 = _mm(s1, od_w_out, h2, F32, name="od_out")
    h4, ffn1 = _ffn_fwd(h3, nf[1], w_up[1], ffn_cw[1], ffn_cb[1], w_down[1], "l1")
    dh4, dgf8, sq8 = _final_loss(h4, row(p["norm_final"]), tgt)
    grads["norm_final"] = dgf8.sum(axis=0)

    big = {}
    dh3, dnf1, dwu1, dcw1, dcb1, dwd1 = _ffn_bwd(dh4, h3, nf[1], ffn1, w_up[1], ffn_cw[1], ffn_cb[1], w_down[1], "l1")
    big["ffn_w_up", 1], big["ffn_w_down", 1] = _piece_major(dwu1), _piece_major(dwd1)
    ds1 = _mm(dh3, od_w_out, None, BF16, name="od_ds", transpose_w=True)
    big["od_w_out", 0] = _piece_major(_mm_tn(s1, dh3, name="od_dwo", tk=512))
    wsct = jnp.swapaxes(wsc, 1, 2)
    dp1, dws, dbs, dlg8, dlb8, dbin8 = _sgu_bwd(p1, ds1, row(p["od_ln_g"][0]), row(p["od_ln_b"][0]), wsc, wsct, bsb)
    grads["od_w_s"] = (dws * tril)[None]
    grads["od_b_s"] = dbs.sum(axis=-1)[None]
    grads["od_ln_g"] = dlg8.sum(axis=0)[None]
    grads["od_ln_b"] = dlb8.sum(axis=0)[None]
    grads["od_b_in"] = dbin8.sum(axis=0)[None]
    dh2, dnm1 = _mm_normbwd(dp1, od_w_in, h2, nm[1], dh3, name="od_dh")
    big["od_w_in", 0] = _piece_major(_mm_tn(hn1, dp1, name="od_dwi", tk=512, col_shards=N_CHIPS))
    if start_reduce is not None:
        token = start_reduce("g1", [big[key] for key in REDUCE_GROUPS["g1"]])
        ffn_cb[0] = ffn_cb[0] + token[0:1, 0:1]

    dh1, dnf0, dwu0, dcw0, dcb0, dwd0 = _ffn_bwd(dh2, h1, nf[0], ffn0, w_up[0], ffn_cw[0], ffn_cb[0], w_down[0], "l0")
    big["ffn_w_up", 0], big["ffn_w_down", 0] = _piece_major(dwu0), _piece_major(dwd0)
    if start_reduce is not None:
        token = start_reduce("g2", [big[key] for key in REDUCE_GROUPS["g2"]])
        ev_cb = ev_cb + token[0:1, 0:1]
    dmix = _mm(dh1, ev_w_out, None, BF16, name="ev_dmix", transpose_w=True)
    big["ev_w_out", 0] = _piece_major(_mm_tn(mix0, dh1, name="ev_dwo", tk=512))
    dz01, dc5, dwa, dwx, dvec = _lru_bwd(z0, hseq, dmix, ev_cw, ev_cb, wa, wa.T, ba, wx, wx.T, bx, lam)
    dq, df, dv, dg, dlb32, dgn32 = _hgrn_bwd(z0, ss, dmix, lb0, gn)
    dz0 = jnp.concatenate([dz01, dq, df, dv, dg], axis=1)
    big["ev_w_in", 0] = _piece_major(_mm_tn(hn0, dz0, name="ev_dwi", tk=512, col_shards=N_CHIPS))
    if start_reduce is not None:
        token = start_reduce("g3", [big[key] for key in REDUCE_GROUPS["g3"]])
        nm[0] = nm[0] + token[0:1, 0:1]
    grad_x, dnm0 = _mm_normbwd(dz0, ev_w_in, h0, nm[0], dh1, name="ev_dh")

    dc5s = dc5.sum(axis=1)
    grads["ev_conv_w"] = dc5s[0:4][None]
    grads["ev_conv_b"] = dc5s[4][None]
    grads["ev_gate_a_w"] = _diag_blocks(dwa, LRU_BLOCKS)[None]
    grads["ev_gate_x_w"] = _diag_blocks(dwx, LRU_BLOCKS)[None]
    dvs = dvec.sum(axis=1)
    grads["ev_gate_a_b"] = dvs[0][None]
    grads["ev_gate_x_b"] = dvs[1][None]
    grads["ev_lru_lambda"] = (dvs[2] * (-jax.nn.sigmoid(-p["ev_lru_lambda"][0])))[None]
    dlb = dlb32.reshape(HG_HEADS, SUBLANES, HG_D).sum(axis=1).reshape(-1)
    grads["hg_lb_logits"] = dlb[None, :] * lower[0][None, :] * (jnp.eye(3, dtype=F32)[0][:, None] - lower)
    grads["ev_hg_norm"] = dgn32.reshape(HG_HEADS, SUBLANES, HG_D).sum(axis=(0, 1))[None]
    grads["norm_mix"] = jnp.stack([dnm0.sum(axis=0), dnm1.sum(axis=0)])
    grads["norm_ffn"] = jnp.stack([dnf0, dnf1])
    grads["ffn_conv_w"] = jnp.stack([dcw0, dcw1])
    grads["ffn_conv_b"] = jnp.stack([dcb0, dcb1])
    return sq8, grad_x, grads, big


MESH = pl.DeviceIdType.MESH
ANY = pl.BlockSpec(memory_space=pl.ANY)
N_CHIPS = 4
N_DEV = 8

SH_BIG = {"ev_w_in": 2, "ev_w_out": 1, "od_w_in": 2, "od_w_out": 1, "ffn_w_up": 2, "ffn_w_down": 1}
SH_SMALL = {"ev_conv_w": 2, "od_b_in": 1, "od_ln_g": 1, "od_ln_b": 1, "ffn_conv_w": 2}
REP = ["norm_mix", "norm_ffn", "norm_final", "ev_conv_b", "ev_gate_a_w", "ev_gate_a_b", "ev_gate_x_w", "ev_gate_x_b",
       "ev_lru_lambda", "hg_lb_logits", "ev_hg_norm", "od_w_s", "od_b_s", "ffn_conv_b"]
WEIGHTS = ["norm_mix", "norm_ffn", "norm_final", "ev_w_in", "ev_conv_w", "ev_conv_b", "ev_gate_a_w", "ev_gate_a_b", "ev_gate_x_w",
           "ev_gate_x_b", "ev_lru_lambda", "hg_lb_logits", "ev_hg_norm", "ev_w_out", "od_w_in", "od_b_in", "od_ln_g", "od_ln_b",
           "od_w_s", "od_b_s", "od_w_out", "ffn_w_up", "ffn_conv_w", "ffn_conv_b", "ffn_w_down"]


def _rows(n_elems, mult=SUBLANES):
    r = -(-n_elems // LANES)
    return -(-r // mult) * mult


def _pack(arrs, rows, dtype):
    flat = jnp.concatenate([a.reshape(-1).astype(dtype) for a in arrs])
    return jnp.pad(flat, (0, rows * LANES - flat.shape[0])).reshape(rows, LANES)


def _unpack(flat2d, shapes):
    flat = flat2d.reshape(-1)
    out, off = [], 0
    for s in shapes:
        n = 1
        for d in s:
            n *= d
        out.append(flat[off:off + n].reshape(s))
        off += n
    return out


def _mesh_pos():
    return lax.axis_index("x"), lax.axis_index("y"), lax.axis_index("c")


def _other_chips(x, y):
    return [(1 - x, y), (x, 1 - y), (1 - x, 1 - y)]


def _half_rows(n):
    return lambda r, c: r.at[0, pl.ds(c * (n // 2), n // 2), :]


GATHER_BIG = {
    "ev_w_in": ((1024, 3072), _half_rows(1024), lambda o, k, c: o.at[pl.ds(c * 512, 512), pl.ds(k * 768, 768)]),
    "ev_w_out": ((1024, 1024), _half_rows(256), lambda o, k, c: o.at[pl.ds(k * 256 + c * 128, 128), :]),
    "od_w_in": ((1024, 2048), _half_rows(1024), lambda o, k, c: o.at[pl.ds(c * 512, 512), pl.ds(k * 512, 512)]),
    "od_w_out": ((1024, 1024), _half_rows(256), lambda o, k, c: o.at[pl.ds(k * 256 + c * 128, 128), :]),
    "ffn_w_up": ((2, 1024, 2 * D_FF), lambda r, c: r.at[c], lambda o, k, c: o.at[c, :, pl.ds(k * (2 * D_FF // 4), 2 * D_FF // 4)]),
    "ffn_w_down": ((2, D_FF, 1024), lambda r, c: r.at[c], lambda o, k, c: o.at[c, pl.ds(k * (D_FF // 4), D_FF // 4), :]),
}


def _gather_weights(names, big, small):
    nb = len(big)
    descs = [GATHER_BIG[n] for n in names]
    rs = small.shape[0]

    def body(*refs):
        ins, s_ref = refs[:nb], refs[nb]
        outs, os_ref = refs[nb + 1:2 * nb + 1], refs[2 * nb + 1]
        ici_send, ici_recv, d2d_send, d2d_recv, loc_sems = refs[2 * nb + 2:2 * nb + 7]
        vbufs = refs[2 * nb + 7:]
        x, y, c = _mesh_pos()
        k = 2 * x + y
        chips = _other_chips(x, y)
        sib = (x, y, 1 - c)

        def remote(src, dst, ssem, rsem, to):
            return pltpu.make_async_remote_copy(src_ref=src, dst_ref=dst, send_sem=ssem, recv_sem=rsem, device_id=to,
                                                device_id_type=MESH)

        stage = [pltpu.make_async_copy(ins[t], vbufs[t], loc_sems.at[2 * t]) for t in range(nb)]
        stage.append(pltpu.make_async_copy(s_ref, vbufs[nb], loc_sems.at[2 * nb]))
        for cp in stage:
            cp.start()
        sends = []
        for t, (_, src, dst) in enumerate(descs):
            for j, (px, py) in enumerate(chips):
                sends.append(remote(src(ins[t], c), dst(outs[t], k, c), ici_send.at[3 * t + j], ici_recv.at[3 * t + j], (px, py, c)))
        for j, (px, py) in enumerate(chips):
            sends.append(remote(s_ref, os_ref.at[k], ici_send.at[3 * nb + j], ici_recv.at[3 * nb + j], (px, py, c)))
        for cp in sends:
            cp.start()
        for cp in stage:
            cp.wait()
        local = []
        for t, (_, src, dst) in enumerate(descs):
            for cc in (0, 1):
                local.append(pltpu.make_async_copy(src(vbufs[t], cc), dst(outs[t], k, cc), loc_sems.at[2 * t + cc]))
        local.append(pltpu.make_async_copy(vbufs[nb], os_ref.at[k], loc_sems.at[2 * nb]))
        for cp in local:
            cp.start()
        for t, (_, src, dst) in enumerate(descs):
            for j, (px, py) in enumerate(chips):
                got = dst(outs[t], 2 * px + py, c)
                remote(got, got, ici_send.at[3 * t + j], ici_recv.at[3 * t + j], (px, py, c)).wait_recv()
                fwd = remote(got, got, d2d_send.at[3 * t + j], d2d_recv.at[3 * t + j], sib)
                fwd.start()
                sends.append(fwd)
        for j, (px, py) in enumerate(chips):
            remote(s_ref, os_ref.at[2 * px + py], ici_send.at[3 * nb + j], ici_recv.at[3 * nb + j], (px, py, c)).wait_recv()
        for t, (_, src, dst) in enumerate(descs):
            for j, (px, py) in enumerate(chips):
                theirs = dst(outs[t], 2 * px + py, 1 - c)
                remote(theirs, theirs, d2d_send.at[3 * t + j], d2d_recv.at[3 * t + j], sib).wait_recv()
        for cp in sends:
            cp.wait_send()
        for cp in local:
            cp.wait()

    out_shape = [jax.ShapeDtypeStruct(d[0], BF16) for d in descs] + [jax.ShapeDtypeStruct((N_CHIPS, rs, LANES), small.dtype)]
    return pl.pallas_call(
        body, name="gather_weights", in_specs=[ANY] * (nb + 1), out_specs=[ANY] * (nb + 1), out_shape=out_shape,
        scratch_shapes=[pltpu.SemaphoreType.DMA((3 * nb + 3,)), pltpu.SemaphoreType.DMA((3 * nb + 3,)),
                        pltpu.SemaphoreType.DMA((3 * nb,)), pltpu.SemaphoreType.DMA((3 * nb,)),
                        pltpu.SemaphoreType.DMA((2 * nb + 1,))]
        + [pltpu.VMEM(b.shape, b.dtype) for b in big] + [pltpu.VMEM(small.shape, small.dtype)],
        compiler_params=pltpu.CompilerParams(vmem_limit_bytes=VMEM_LIMIT),
    )(*big, small)


def _place_own(names, big):
    nb = len(big)
    descs = [GATHER_BIG[n] for n in names]

    def body(*refs):
        ins, outs = refs[:nb], refs[nb:2 * nb]
        sems, vbufs = refs[2 * nb], refs[2 * nb + 1:]
        x, y, c = _mesh_pos()
        k = 2 * x + y
        stage = [pltpu.make_async_copy(ins[t], vbufs[t], sems.at[2 * t]) for t in range(nb)]
        for cp in stage:
            cp.start()
        for cp in stage:
            cp.wait()
        local = [pltpu.make_async_copy(src(vbufs[t], cc), dst(outs[t], k, cc), sems.at[2 * t + cc])
                 for t, (_, src, dst) in enumerate(descs) for cc in (0, 1)]
        for cp in local:
            cp.start()
        for cp in local:
            cp.wait()

    return pl.pallas_call(
        body, name="place_own", in_specs=[ANY] * nb, out_specs=[ANY] * nb,
        out_shape=[jax.ShapeDtypeStruct(d[0], BF16) for d in descs],
        scratch_shapes=[pltpu.SemaphoreType.DMA((2 * nb,))] + [pltpu.VMEM(b.shape, b.dtype) for b in big],
        compiler_params=pltpu.CompilerParams(vmem_limit_bytes=VMEM_LIMIT),
    )(*big)


def _gather_start(names, big, bufs):
    nb = len(big)
    descs = [GATHER_BIG[n] for n in names]

    def body(*refs):
        ins, lnd = refs[:nb], refs[nb:2 * nb]
        send_sems, recv_sems, token = refs[2 * nb], refs[2 * nb + 1], refs[-1]
        x, y, c = _mesh_pos()
        k = 2 * x + y
        for t, (_, src, dst) in enumerate(descs):
            for j, (px, py) in enumerate(_other_chips(x, y)):
                _remote(src(ins[t], c), dst(lnd[t], k, c), send_sems.at[3 * t + j], recv_sems.at[3 * t + j], (px, py, c)).start()
        token[...] = jnp.zeros_like(token)

    out = pl.pallas_call(
        body, name="gather_start",
        out_shape=(pltpu.SemaphoreType.DMA((3 * nb,)), pltpu.SemaphoreType.DMA((3 * nb,)),
                   *[pltpu.HBM(b.shape, b.dtype) for b in big], *[pltpu.HBM(b.shape, b.dtype) for b in bufs],
                   jax.ShapeDtypeStruct((SUBLANES, LANES), F32)),
        in_specs=[HBM] * (2 * nb), out_specs=(SEM, SEM, *[HBM] * (2 * nb), pl.BlockSpec(memory_space=pltpu.VMEM)),
        input_output_aliases={i: 2 + i for i in range(2 * nb)},
        compiler_params=pltpu.CompilerParams(has_side_effects=DATAFLOW),
    )(*[pltpu.with_memory_space_constraint(b, pltpu.HBM) for b in big], *[pltpu.with_memory_space_constraint(b, pltpu.HBM) for b in bufs])
    return out[0], out[1], list(out[2:2 + nb]), list(out[2 + nb:2 + 2 * nb]), out[-1]


def _gather_wait(names, send_sems, recv_sems, big, bufs, after):
    nb = len(big)
    descs = [GATHER_BIG[n] for n in names]

    def body(*refs):
        ins, lnd = refs[:nb], refs[nb:2 * nb]
        ssem, rsem = refs[2 * nb], refs[2 * nb + 1]
        x, y, c = _mesh_pos()
        for t, (_, src, dst) in enumerate(descs):
            for j, (px, py) in enumerate(_other_chips(x, y)):
                cp = _remote(src(ins[t], c), dst(lnd[t], 2 * px + py, c), ssem.at[3 * t + j], rsem.at[3 * t + j], (px, py, c))
                cp.wait_send()
                cp.wait_recv()

    out = pl.pallas_call(
        body, name="gather_wait",
        out_shape=(*[pltpu.HBM(b.shape, b.dtype) for b in big], *[pltpu.HBM(b.shape, b.dtype) for b in bufs]),
        in_specs=[HBM] * (2 * nb) + [SEM, SEM, ANY], out_specs=tuple([HBM] * (2 * nb)),
        input_output_aliases={i: i for i in range(2 * nb)},
        compiler_params=pltpu.CompilerParams(has_side_effects=DATAFLOW),
    )(*big, *bufs, send_sems, recv_sems, after)
    return list(out[nb:])


def _gather_forward(names, bufs):
    nb = len(bufs)
    descs = [GATHER_BIG[n] for n in names]

    def body(*refs):
        outs = refs[nb:2 * nb]
        send_sems, recv_sems = refs[2 * nb:]
        x, y, c = _mesh_pos()
        sib = (x, y, 1 - c)
        sends = []
        for t, (_, src, dst) in enumerate(descs):
            for j, (px, py) in enumerate(_other_chips(x, y)):
                got = dst(outs[t], 2 * px + py, c)
                sends.append(_remote(got, got, send_sems.at[3 * t + j], recv_sems.at[3 * t + j], sib))
        for cp in sends:
            cp.start()
        for t, (_, src, dst) in enumerate(descs):
            for j, (px, py) in enumerate(_other_chips(x, y)):
                theirs = dst(outs[t], 2 * px + py, 1 - c)
                _remote(theirs, theirs, send_sems.at[3 * t + j], recv_sems.at[3 * t + j], sib).wait_recv()
        for cp in sends:
            cp.wait_send()

    return pl.pallas_call(
        body, name="gather_forward", in_specs=[ANY] * nb, out_specs=[ANY] * nb,
        out_shape=[jax.ShapeDtypeStruct(b.shape, b.dtype) for b in bufs], input_output_aliases={t: t for t in range(nb)},
        scratch_shapes=[pltpu.SemaphoreType.DMA((3 * nb,)), pltpu.SemaphoreType.DMA((3 * nb,))],
    )(*bufs)


def _remote(src, dst, ssem, rsem, to):
    return pltpu.make_async_remote_copy(src_ref=src, dst_ref=dst, send_sem=ssem, recv_sem=rsem, device_id=to, device_id_type=MESH)


def _rs_send_sibling(gs, tag):
    n = len(gs)
    counts = [N_CHIPS if g.ndim == 4 else 1 for g in gs]
    ns = sum(counts)

    def body(*refs):
        ins, outs = refs[:n], refs[n:2 * n]
        send_sems, recv_sems = refs[2 * n:]
        x, y, c = _mesh_pos()
        cps, s = [], 0
        for t in range(n):
            if counts[t] == 1:
                cps.append(_remote(ins[t].at[1 - c], outs[t], send_sems.at[s], recv_sems.at[s], (x, y, 1 - c)))
                s += 1
            else:
                for k in range(N_CHIPS):
                    cps.append(_remote(ins[t].at[k, 1 - c], outs[t].at[k], send_sems.at[s], recv_sems.at[s], (x, y, 1 - c)))
                    s += 1
        for cp in cps:
            cp.start()
        for cp in cps:
            cp.wait()

    out_shape = [jax.ShapeDtypeStruct(g.shape[:1] + g.shape[2:] if g.ndim == 4 else g.shape[1:], g.dtype) for g in gs]
    return pl.pallas_call(
        body, name=f"rs_send_sibling_{tag}", in_specs=[ANY] * n, out_specs=[ANY] * n, out_shape=out_shape,
        scratch_shapes=[pltpu.SemaphoreType.DMA((ns,)), pltpu.SemaphoreType.DMA((ns,))],
    )(*gs)


def _add_piece(g, recv, c, name):
    P, Q = g.shape[-2:]

    def body(c_ref, g_ref, r_ref, o_ref):
        o_ref[...] = g_ref[...].reshape(o_ref.shape) + r_ref[...]

    if g.ndim == 4:
        grid = (N_CHIPS,)
        in_specs = [pl.BlockSpec((1, 1, P, Q), lambda k, c_ref: (k, c_ref[0], 0, 0)), pl.BlockSpec((1, P, Q), lambda k, c_ref: (k, 0, 0))]
        out_spec = pl.BlockSpec((1, P, Q), lambda k, c_ref: (k, 0, 0))
    else:
        grid = (1,)
        in_specs = [pl.BlockSpec((1, P, Q), lambda k, c_ref: (c_ref[0], 0, 0)), pl.BlockSpec((P, Q), lambda k, c_ref: (0, 0))]
        out_spec = pl.BlockSpec((P, Q), lambda k, c_ref: (0, 0))
    return pl.pallas_call(
        body, name=name,
        grid_spec=pltpu.PrefetchScalarGridSpec(num_scalar_prefetch=1, grid=grid, in_specs=in_specs, out_specs=out_spec),
        out_shape=jax.ShapeDtypeStruct(recv.shape, g.dtype),
        compiler_params=_cp(1),
    )(c, g, recv)


HBM = pl.BlockSpec(memory_space=pltpu.HBM)
SEM = pl.BlockSpec(memory_space=pltpu.SEMAPHORE)
DATAFLOW = pltpu.SideEffectType.DATAFLOW_SIDE_EFFECTING


def _chips_start(hs, tag):
    n = len(hs)
    lands = [pltpu.with_memory_space_constraint(lax.empty((N_CHIPS,) + h.shape[-2:], h.dtype), pltpu.HBM) for h in hs]

    def body(*refs):
        ins, lnd = refs[:n], refs[n:2 * n]
        send_sems, recv_sems, token = refs[2 * n], refs[2 * n + 1], refs[-1]
        x, y, c = _mesh_pos()
        k = 2 * x + y
        piece = lambda t, kk: ins[t].at[kk] if hs[t].ndim == 3 else ins[t]
        for t in range(n):
            for j, (px, py) in enumerate(_other_chips(x, y)):
                _remote(piece(t, 2 * px + py), lnd[t].at[k], send_sems.at[3 * t + j], recv_sems.at[3 * t + j], (px, py, c)).start()
        token[...] = jnp.zeros_like(token)

    out = pl.pallas_call(
        body, name=f"chips_start_{tag}",
        out_shape=(pltpu.SemaphoreType.DMA((3 * n,)), pltpu.SemaphoreType.DMA((3 * n,)),
                   *[pltpu.HBM(h.shape, h.dtype) for h in hs], *[pltpu.HBM(l.shape, l.dtype) for l in lands],
                   jax.ShapeDtypeStruct((SUBLANES, LANES), F32)),
        in_specs=[HBM] * (2 * n), out_specs=(SEM, SEM, *[HBM] * (2 * n), pl.BlockSpec(memory_space=pltpu.VMEM)),
        input_output_aliases={i: 2 + i for i in range(2 * n)},
        compiler_params=pltpu.CompilerParams(has_side_effects=DATAFLOW),
    )(*[pltpu.with_memory_space_constraint(h, pltpu.HBM) for h in hs], *lands)
    return out[0], out[1], list(out[2:2 + n]), list(out[2 + n:2 + 2 * n]), out[-1]


def _chips_wait(send_sems, recv_sems, hs, lands, after, tag):
    n = len(hs)

    def body(*refs):
        ins, lnd = refs[:n], refs[n:2 * n]
        ssem, rsem = refs[2 * n], refs[2 * n + 1]
        x, y, c = _mesh_pos()
        k = 2 * x + y
        piece = lambda t, kk: ins[t].at[kk] if hs[t].ndim == 3 else ins[t]
        for t in range(n):
            for j, (px, py) in enumerate(_other_chips(x, y)):
                cp = _remote(piece(t, k), lnd[t].at[2 * px + py], ssem.at[3 * t + j], rsem.at[3 * t + j], (px, py, c))
                cp.wait_send()
                cp.wait_recv()

    out = pl.pallas_call(
        body, name=f"chips_wait_{tag}",
        out_shape=(*[pltpu.HBM(h.shape, h.dtype) for h in hs], *[pltpu.HBM(l.shape, l.dtype) for l in lands]),
        in_specs=[HBM] * (2 * n) + [SEM, SEM, ANY], out_specs=tuple([HBM] * (2 * n)),
        input_output_aliases={i: i for i in range(2 * n)},
        compiler_params=pltpu.CompilerParams(has_side_effects=DATAFLOW),
    )(*hs, *lands, send_sems, recv_sems, after)
    return list(out[:n]), list(out[n:])


def _add_chips(p, own, kc, name):
    _, P, Q = p.shape
    tr = P
    while N_CHIPS * tr * Q * 4 > 6 * 1024 * 1024 and tr % 16 == 0:
        tr //= 2
    sharded = own.ndim == 3

    def body(kc_ref, p_ref, own_ref, o_ref):
        k = kc_ref[0]
        mine = own_ref[...].reshape(tr, Q)
        v = [jnp.where(k == j, mine, p_ref[j]) for j in range(N_CHIPS)]
        o_ref[0] = ((v[0] + v[1]) + v[2]) + v[3]

    own_spec = (pl.BlockSpec((1, tr, Q), lambda i, kc_ref: (kc_ref[0], i, 0)) if sharded
                else pl.BlockSpec((tr, Q), lambda i, kc_ref: (i, 0)))
    return pl.pallas_call(
        body, name=name,
        grid_spec=pltpu.PrefetchScalarGridSpec(
            num_scalar_prefetch=1, grid=(P // tr,),
            in_specs=[pl.BlockSpec((N_CHIPS, tr, Q), lambda i, kc_ref: (0, i, 0)), own_spec],
            out_specs=pl.BlockSpec((1, tr, Q), lambda i, kc_ref: (kc_ref[1], i, 0))),
        out_shape=jax.ShapeDtypeStruct((2, P, Q), p.dtype),
        compiler_params=_cp(1),
    )(kc, p, own)


def _rs_share(fs, tag):
    n = len(fs)

    def body(*refs):
        outs = refs[n:2 * n]
        send_sems, recv_sems = refs[2 * n:]
        x, y, c = _mesh_pos()
        sends = [_remote(outs[t].at[c], outs[t].at[c], send_sems.at[t], recv_sems.at[t], (x, y, 1 - c)) for t in range(n)]
        for cp in sends:
            cp.start()
        for t in range(n):
            _remote(outs[t].at[c], outs[t].at[1 - c], send_sems.at[t], recv_sems.at[t], (x, y, 1 - c)).wait_recv()
        for cp in sends:
            cp.wait_send()

    return pl.pallas_call(
        body, name=f"rs_share_{tag}", in_specs=[ANY] * n, out_specs=[ANY] * n,
        out_shape=[jax.ShapeDtypeStruct(f.shape, f.dtype) for f in fs], input_output_aliases={t: t for t in range(n)},
        scratch_shapes=[pltpu.SemaphoreType.DMA((n,)), pltpu.SemaphoreType.DMA((n,))],
    )(*fs)


def _reduce_start(gs, kc, tag):
    from_sibling = _rs_send_sibling(gs, tag)
    chip_sums = [_add_piece(g, r, kc[1:], name=f"add_piece_{tag}_{t}") for t, (g, r) in enumerate(zip(gs, from_sibling))]
    send_sems, recv_sems, chip_sums, lands, token = _chips_start(chip_sums, tag)
    return (send_sems, recv_sems, chip_sums, lands, tag), token


def _reduce_finish(states, kc, after):
    mine = []
    for send_sems, recv_sems, chip_sums, lands, tag in states:
        chip_sums, from_chips = _chips_wait(send_sems, recv_sems, chip_sums, lands, after, tag)
        mine += [_add_chips(p, h, kc, name=f"add_chips_{tag}_{t}") for t, (p, h) in enumerate(zip(from_chips, chip_sums))]
    return _rs_share(mine, "all")


def _adamw(w, g, m, v, name):
    R, C = w.shape
    tr = R
    for cand in (512, 256, 128, 64, 32, 16, 8):
        if R % cand == 0 and cand * C * 4 <= 2 * 1024 * 1024:
            tr = cand
            break
    c1 = 1.0 / (1.0 - ADAM_B1 ** ADAM_STEP)
    c2 = 1.0 / (1.0 - ADAM_B2 ** ADAM_STEP)

    def body(w_ref, g_ref, m_ref, v_ref, d_ref, mo_ref, vo_ref):
        gv = g_ref[...]
        mn = ADAM_B1 * m_ref[...] + (1.0 - ADAM_B1) * gv
        vn = ADAM_B2 * v_ref[...] + (1.0 - ADAM_B2) * (gv * gv)
        mo_ref[...] = mn
        vo_ref[...] = vn
        d_ref[...] = -ADAM_LR * ((mn * c1) / (jnp.sqrt(vn * c2) + ADAM_EPS) + ADAM_WD * w_ref[...])

    spec = pl.BlockSpec((tr, C), lambda i: (i, 0))
    shp = jax.ShapeDtypeStruct((R, C), F32)
    return pl.pallas_call(body, name=name, grid=(R // tr,), in_specs=[spec] * 4, out_specs=[spec] * 3, out_shape=[shp] * 3,
                          compiler_params=_cp(1))(w, g, m, v)


def _step(a):
    x, y, c = _mesh_pos()
    kc = jnp.stack([2 * x + y, c]).astype(jnp.int32)

    rs = _rows(sum(a[n].size for n in SH_SMALL))
    first, later = ["ev_w_in", "ev_w_out"], ["od_w_in", "od_w_out", "ffn_w_up", "ffn_w_down"]
    lead = lambda w: w if w.ndim == 3 else w[None]
    *full, gs = _gather_weights(first, [a[n].astype(BF16) for n in first], _pack([a[n] for n in SH_SMALL], rs, F32))
    p = {n: a[n] for n in REP}
    p.update({n: lead(w) for n, w in zip(first, full)})
    parts = [_unpack(gs[k], [a[n].shape for n in SH_SMALL]) for k in range(N_CHIPS)]
    for i, n in enumerate(SH_SMALL):
        p[n] = jnp.concatenate([parts[k][i] for k in range(N_CHIPS)], axis=SH_SMALL[n])
    shards = lax.optimization_barrier(([a[n].astype(BF16) for n in later], full))[0]
    g_send, g_recv, shards, bufs, token = _gather_start(later, shards, _place_own(later, shards))
    p["norm_mix"] = p["norm_mix"] + token[0:1, 0:1]

    def late(after):
        got = _gather_forward(later, _gather_wait(later, g_send, g_recv, shards, bufs, after))
        return {n: lead(w) for n, w in zip(later, got)}

    p["late"] = late

    states = []

    def start_reduce(tag, gs):
        state, token = _reduce_start(gs, kc, tag)
        states.append(state)
        return token

    sq8, grad_x, grads, big = _local_step(a["x"][0], a["loss_target"][0], p, start_reduce)
    loss = lax.psum(0.5 / D_MODEL * jnp.sum(sq8), ("x", "y", "c"))

    r_s = _rows(sum(a[n].size for n in SH_SMALL), 2 * SUBLANES) // 2
    small_pieces = []
    for k in range(N_CHIPS):
        pieces = [lax.slice_in_dim(grads[n], k * a[n].shape[ax], (k + 1) * a[n].shape[ax], axis=ax) for n, ax in SH_SMALL.items()]
        small_pieces.append(_pack(pieces, 2 * r_s, F32).reshape(2, r_s, LANES))
    g_small = jnp.stack(small_pieces)
    r_r = _rows(sum(a[n].size for n in REP), 2 * SUBLANES) // 2
    g_rep = _pack([grads[n] for n in REP], 2 * r_r, F32).reshape(2, r_r, LANES)
    start_reduce("g4", [g_small, g_rep])
    reduced = _reduce_finish(states, kc, grad_x)
    red = dict(zip([key for tag in ("g1", "g2", "g3") for key in REDUCE_GROUPS[tag]], reduced))
    gfin = {}
    for n in ("ev_w_in", "ev_w_out", "od_w_in", "od_w_out"):
        gfin[n] = red[n, 0].reshape(a[n].shape)
    for n in ("ffn_w_up", "ffn_w_down"):
        gfin[n] = jnp.stack([red[n, l].reshape(a[n].shape[1:]) for l in range(2)])
    gfin.update(zip(SH_SMALL, _unpack(reduced[-2], [a[n].shape for n in SH_SMALL])))
    gfin.update(zip(REP, _unpack(reduced[-1], [a[n].shape for n in REP])))

    out = {"loss": loss, "grad_x": grad_x[None]}
    small_names = list(SH_SMALL) + REP
    for n in SH_BIG:
        shp = a[n].shape
        two_d = lambda t: t.reshape(-1, shp[-1])
        d, mo, vo = _adamw(two_d(a[n]), two_d(gfin[n]), two_d(a["m_" + n]), two_d(a["v_" + n]), name=f"adamw_{n}")
        out["delta_" + n], out["new_m_" + n], out["new_v_" + n] = d.reshape(shp), mo.reshape(shp), vo.reshape(shp)
    r_small = _rows(sum(a[n].size for n in small_names), 512)
    packs = [_pack([src(n) for n in small_names], r_small, F32)
             for src in (lambda n: a[n], lambda n: gfin[n], lambda n: a["m_" + n], lambda n: a["v_" + n])]
    d, mo, vo = _adamw(*packs, name="adamw_small")
    shapes = [a[n].shape for n in small_names]
    for n, dd, mm, vv in zip(small_names, _unpack(d, shapes), _unpack(mo, shapes), _unpack(vo, shapes)):
        out["delta_" + n], out["new_m_" + n], out["new_v_" + n] = dd, mm, vv
    for n in WEIGHTS:
        out["grad_" + n] = gfin[n]
    return out


def kernel(x, norm_mix, norm_ffn, norm_final, ev_w_in, ev_conv_w, ev_conv_b, ev_gate_a_w, ev_gate_a_b, ev_gate_x_w, ev_gate_x_b, ev_lru_lambda, hg_lb_logits, ev_hg_norm, ev_w_out, od_w_in, od_b_in, od_ln_g, od_ln_b, od_w_s, od_b_s, od_w_out, ffn_w_up, ffn_conv_w, ffn_conv_b, ffn_w_down, loss_target, m_norm_mix, m_norm_ffn, m_norm_final, m_ev_w_in, m_ev_conv_w, m_ev_conv_b, m_ev_gate_a_w, m_ev_gate_a_b, m_ev_gate_x_w, m_ev_gate_x_b, m_ev_lru_lambda, m_hg_lb_logits, m_ev_hg_norm, m_ev_w_out, m_od_w_in, m_od_b_in, m_od_ln_g, m_od_ln_b, m_od_w_s, m_od_b_s, m_od_w_out, m_ffn_w_up, m_ffn_conv_w, m_ffn_conv_b, m_ffn_w_down, v_norm_mix, v_norm_ffn, v_norm_final, v_ev_w_in, v_ev_conv_w, v_ev_conv_b, v_ev_gate_a_w, v_ev_gate_a_b, v_ev_gate_x_w, v_ev_gate_x_b, v_ev_lru_lambda, v_hg_lb_logits, v_ev_hg_norm, v_ev_w_out, v_od_w_in, v_od_b_in, v_od_ln_g, v_od_ln_b, v_od_w_s, v_od_b_s, v_od_w_out, v_ffn_w_up, v_ffn_conv_w, v_ffn_conv_b, v_ffn_w_down):
    vals = (x, norm_mix, norm_ffn, norm_final, ev_w_in, ev_conv_w, ev_conv_b, ev_gate_a_w, ev_gate_a_b, ev_gate_x_w, ev_gate_x_b, ev_lru_lambda, hg_lb_logits, ev_hg_norm, ev_w_out, od_w_in, od_b_in, od_ln_g, od_ln_b, od_w_s, od_b_s, od_w_out, ffn_w_up, ffn_conv_w, ffn_conv_b, ffn_w_down, loss_target, m_norm_mix, m_norm_ffn, m_norm_final, m_ev_w_in, m_ev_conv_w, m_ev_conv_b, m_ev_gate_a_w, m_ev_gate_a_b, m_ev_gate_x_w, m_ev_gate_x_b, m_ev_lru_lambda, m_hg_lb_logits, m_ev_hg_norm, m_ev_w_out, m_od_w_in, m_od_b_in, m_od_ln_g, m_od_ln_b, m_od_w_s, m_od_b_s, m_od_w_out, m_ffn_w_up, m_ffn_conv_w, m_ffn_conv_b, m_ffn_w_down, v_norm_mix, v_norm_ffn, v_norm_final, v_ev_w_in, v_ev_conv_w, v_ev_conv_b, v_ev_gate_a_w, v_ev_gate_a_b, v_ev_gate_x_w, v_ev_gate_x_b, v_ev_lru_lambda, v_hg_lb_logits, v_ev_hg_norm, v_ev_w_out, v_od_w_in, v_od_b_in, v_od_ln_g, v_od_ln_b, v_od_w_s, v_od_b_s, v_od_w_out, v_ffn_w_up, v_ffn_conv_w, v_ffn_conv_b, v_ffn_w_down)
    names = ["x"] + WEIGHTS + ["loss_target"] + ["m_" + n for n in WEIGHTS] + ["v_" + n for n in WEIGHTS]
    out = _step(dict(zip(names, vals)))
    return (out["loss"], out["grad_x"], *[out["grad_" + n] for n in WEIGHTS], *[out["delta_" + n] for n in WEIGHTS],
            *[out["new_m_" + n] for n in WEIGHTS], *[out["new_v_" + n] for n in WEIGHTS])
```

```python
import functools

import jax
import jax.numpy as jnp
from jax import lax
from jax.experimental import pallas as pl
from jax.experimental.pallas import tpu as pltpu

F32 = jnp.float32
BF16 = jnp.bfloat16

EPS = 1e-6
D_MODEL = 1024
LRU_W = 512
LRU_BLOCKS = 8
LRU_C = 8.0
HG_HEADS = 4
HG_D = 128
HG_CHUNK = 64
SGU_G = 8
SGU_CHUNK = 128
D_FF = 2816
ADAM_LR, ADAM_B1, ADAM_B2, ADAM_EPS, ADAM_WD, ADAM_STEP = 0.001, 0.9, 0.999, 1e-08, 0.01, 10

V7X_VMEM_BYTES = 64 * 1024 * 1024
VMEM_LIMIT = V7X_VMEM_BYTES - 8 * 1024 * 1024
SUBLANES = 8
LANES = 128
BF16_ROWS = 16

GELU_C0 = 0.7978845608028654
GELU_C1 = 0.044715

NN = (((1,), (0,)), ((), ()))
NT = (((1,), (1,)), ((), ()))
TN = (((0,), (0,)), ((), ()))


def _dot(a, b, dims=NN):
    return lax.dot_general(a.astype(BF16), b.astype(BF16), dims, preferred_element_type=F32)


def _cp(n_grid):
    return pltpu.CompilerParams(dimension_semantics=("arbitrary",) * n_grid, vmem_limit_bytes=VMEM_LIMIT)


def _chunk(n, cap):
    best = LANES
    for c in range(LANES, cap + 1, LANES):
        if n % c == 0:
            best = c
    return best


def _resident(shape):
    nd = len(shape)
    return pl.BlockSpec(shape, lambda *_: (0,) * nd, pipeline_mode=pl.Buffered(1))


def _rsum8(x):
    r, c = x.shape
    return x.reshape(r // SUBLANES, SUBLANES, c).sum(axis=0)


def _sigmoid(x):
    return 0.5 * jnp.tanh(0.5 * x) + 0.5


def _gelu(x):
    return 0.5 * x * (1.0 + jnp.tanh(GELU_C0 * (x + GELU_C1 * x * x * x)))


def _gelu_grad(x):
    t = jnp.tanh(GELU_C0 * (x + GELU_C1 * x * x * x))
    return 0.5 * (1.0 + t) + 0.5 * x * (1.0 - t * t) * GELU_C0 * (1.0 + 3.0 * GELU_C1 * x * x)


def _silu_and_grad(x):
    s = _sigmoid(x)
    return x * s, s * (1.0 + x * (1.0 - s))


def _shift_rows(e, j):
    n = e.shape[0]
    return e if j % n == 0 else pltpu.roll(e, j % n, 0)


def _norm_mm(h, g, w, b, name, tt=1024):
    T, D = h.shape
    N = w.shape[1]
    cn = _chunk(N, 512)

    def body(h_ref, g_ref, w_ref, b_ref, hn_ref, z_ref):
        x = h_ref[...]
        r = lax.rsqrt(jnp.mean(x * x, axis=-1, keepdims=True) + EPS)
        hn = (x * r * g_ref[...]).astype(BF16)
        hn_ref[...] = hn
        for j in range(0, N, cn):
            acc = jnp.dot(hn, w_ref[:, j:j + cn], preferred_element_type=F32) + b_ref[:, j:j + cn]
            z_ref[:, j:j + cn] = acc.astype(BF16)

    return pl.pallas_call(
        body, name=name, grid=(T // tt,),
        in_specs=[pl.BlockSpec((tt, D), lambda i: (i, 0)), _resident((1, D)), _resident((D, N)), _resident((1, N))],
        out_specs=[pl.BlockSpec((tt, D), lambda i: (i, 0)), pl.BlockSpec((tt, N), lambda i: (i, 0))],
        out_shape=[jax.ShapeDtypeStruct((T, D), BF16), jax.ShapeDtypeStruct((T, N), BF16)],
        compiler_params=_cp(1),
    )(h, g, w, b)


def _mm(a, w, res, out_dtype, name, tt=1024, transpose_w=False):
    T, K = a.shape
    N = w.shape[0] if transpose_w else w.shape[1]
    cn = _chunk(N, 512)
    has_res = res is not None

    def body(*refs):
        a_ref, w_ref = refs[0], refs[1]
        res_ref = refs[2] if has_res else None
        o_ref = refs[-1]
        av = a_ref[...].astype(BF16)
        for j in range(0, N, cn):
            if transpose_w:
                acc = lax.dot_general(av, w_ref[j:j + cn, :], NT, preferred_element_type=F32)
            else:
                acc = jnp.dot(av, w_ref[:, j:j + cn], preferred_element_type=F32)
            if has_res:
                acc = acc + res_ref[:, j:j + cn]
            o_ref[:, j:j + cn] = acc.astype(out_dtype)

    in_specs = [pl.BlockSpec((tt, K), lambda i: (i, 0)), _resident(w.shape)]
    args = [a, w]
    if has_res:
        in_specs.append(pl.BlockSpec((tt, N), lambda i: (i, 0)))
        args.append(res)
    return pl.pallas_call(
        body, name=name, grid=(T // tt,), in_specs=in_specs,
        out_specs=pl.BlockSpec((tt, N), lambda i: (i, 0)),
        out_shape=jax.ShapeDtypeStruct((T, N), out_dtype),
        compiler_params=_cp(1),
    )(*args)


def _mm_tn(a, b, name, tk, tt=512, col_shards=1):
    T, K = a.shape
    N = b.shape[1]
    ns = N // col_shards

    def body(a_ref, b_ref, o_ref):
        acc = lax.dot_general(a_ref[...].astype(BF16), b_ref[...].astype(BF16), TN, preferred_element_type=F32)
        first = pl.program_id(1) == 0
        if col_shards == 1:
            prev = jnp.where(first, 0.0, o_ref[...])
            o_ref[...] = prev + acc
        else:
            for s in range(col_shards):
                prev = jnp.where(first, 0.0, o_ref[s])
                o_ref[s] = prev + acc[:, s * ns:(s + 1) * ns]

    if col_shards == 1:
        out_spec = pl.BlockSpec((tk, N), lambda k, t: (k, 0))
        out_shape = jax.ShapeDtypeStruct((K, N), F32)
    else:
        out_spec = pl.BlockSpec((col_shards, tk, ns), lambda k, t: (0, k, 0))
        out_shape = jax.ShapeDtypeStruct((col_shards, K, ns), F32)
    return pl.pallas_call(
        body, name=name, grid=(K // tk, T // tt),
        in_specs=[pl.BlockSpec((tt, tk), lambda k, t: (t, k)), pl.BlockSpec((tt, N), lambda k, t: (t, 0))],
        out_specs=out_spec, out_shape=out_shape,
        compiler_params=_cp(2),
    )(a, b)


def _mm_normbwd(dz, w, x, g, dres, name):
    T, N = dz.shape
    D = w.shape[0]
    tt = 1024 if N <= 3072 else 512

    def body(dz_ref, wt_ref, x_ref, g_ref, dres_ref, dx_ref, dg_ref):
        @pl.when(pl.program_id(0) == 0)
        def _():
            dg_ref[...] = jnp.zeros_like(dg_ref)

        dy = lax.dot_general(dz_ref[...], wt_ref[...], NT, preferred_element_type=F32)
        x = x_ref[...]
        r = lax.rsqrt(jnp.mean(x * x, axis=-1, keepdims=True) + EPS)
        xn = x * r
        dg_ref[...] += _rsum8(dy * xn)
        dxn = dy * g_ref[...]
        dx_ref[...] = dres_ref[...] + r * (dxn - xn * jnp.mean(dxn * xn, axis=-1, keepdims=True))

    return pl.pallas_call(
        body, name=name, grid=(T // tt,),
        in_specs=[pl.BlockSpec((tt, N), lambda i: (i, 0)), _resident((D, N)), pl.BlockSpec((tt, D), lambda i: (i, 0)),
                  _resident((1, D)), pl.BlockSpec((tt, D), lambda i: (i, 0))],
        out_specs=[pl.BlockSpec((tt, D), lambda i: (i, 0)), pl.BlockSpec((SUBLANES, D), lambda i: (0, 0))],
        out_shape=[jax.ShapeDtypeStruct((T, D), F32), jax.ShapeDtypeStruct((SUBLANES, D), F32)],
        compiler_params=_cp(1),
    )(dz, w, x, g, dres)


def _final_loss(h, g, tgt, name="final_loss", tt=512):
    T, D = h.shape

    def body(h_ref, g_ref, t_ref, dh_ref, dg_ref, sq_ref):
        @pl.when(pl.program_id(0) == 0)
        def _():
            dg_ref[...] = jnp.zeros_like(dg_ref)
            sq_ref[...] = jnp.zeros_like(sq_ref)

        x = h_ref[...]
        r = lax.rsqrt(jnp.mean(x * x, axis=-1, keepdims=True) + EPS)
        xn = x * r
        gv = g_ref[...]
        diff = xn * gv - t_ref[...]
        sq_ref[...] += _rsum8(diff * diff)
        dy = diff * (1.0 / D)
        dg_ref[...] += _rsum8(dy * xn)
        dxn = dy * gv
        dh_ref[...] = r * (dxn - xn * jnp.mean(dxn * xn, axis=-1, keepdims=True))

    return pl.pallas_call(
        body, name=name, grid=(T // tt,),
        in_specs=[pl.BlockSpec((tt, D), lambda i: (i, 0)), _resident((1, D)), pl.BlockSpec((tt, D), lambda i: (i, 0))],
        out_specs=[pl.BlockSpec((tt, D), lambda i: (i, 0)), pl.BlockSpec((SUBLANES, D), lambda i: (0, 0)),
                   pl.BlockSpec((SUBLANES, D), lambda i: (0, 0))],
        out_shape=[jax.ShapeDtypeStruct((T, D), F32), jax.ShapeDtypeStruct((SUBLANES, D), F32),
                   jax.ShapeDtypeStruct((SUBLANES, D), F32)],
        compiler_params=_cp(1),
    )(h, g, tgt)


def _ffn_act(gu, cw, cb, name, tt=512):
    T = gu.shape[0]
    F = gu.shape[1] // 2
    cc = _chunk(F, 256)
    hb = tt // BF16_ROWS

    def body(gu_ref, halo_ref, cw_ref, cb_ref, a_ref):
        first = pl.program_id(0) == 0
        for c0 in range(0, F, cc):
            cs = slice(c0, c0 + cc)
            x = gu_ref[:, cs].astype(F32)
            halo = jnp.where(first, 0.0, halo_ref[:, cs].astype(F32))
            e = jnp.concatenate([halo, x], axis=0)
            gc = (cb_ref[:, cs] + cw_ref[0:1, cs] * _shift_rows(e, 2)[BF16_ROWS:] + cw_ref[1:2, cs] * _shift_rows(e, 1)[BF16_ROWS:]
                  + cw_ref[2:3, cs] * x)
            up = gu_ref[:, F + c0:F + c0 + cc].astype(F32)
            a_ref[:, cs] = (gc * _sigmoid(gc) * up).astype(BF16)

    return pl.pallas_call(
        body, name=name, grid=(T // tt,),
        in_specs=[pl.BlockSpec((tt, 2 * F), lambda i: (i, 0)),
                  pl.BlockSpec((BF16_ROWS, F), lambda i: (jnp.maximum(i * hb - 1, 0), 0)),
                  _resident((SUBLANES, F)), _resident((1, F))],
        out_specs=pl.BlockSpec((tt, F), lambda i: (i, 0)),
        out_shape=jax.ShapeDtypeStruct((T, F), BF16),
        compiler_params=_cp(1),
    )(gu, gu, cw, cb)


def _ffn_act_bwd(gu, da, cw, cb, name, tt=512):
    T = gu.shape[0]
    F = gu.shape[1] // 2
    cc = _chunk(F, 256)
    hb = tt // BF16_ROWS
    last_hb = T // BF16_ROWS - 1
    nt = T // tt

    def body(gu_ref, gprev_ref, gunext_ref, da_ref, danext_ref, cw_ref, cb_ref, dgu_ref, dc_ref):
        i = pl.program_id(0)

        @pl.when(i == 0)
        def _():
            dc_ref[...] = jnp.zeros_like(dc_ref)

        n = tt + BF16_ROWS
        for c0 in range(0, F, cc):
            cs = slice(c0, c0 + cc)
            us = slice(F + c0, F + c0 + cc)
            g = gu_ref[:, cs].astype(F32)
            gp = jnp.where(i == 0, 0.0, gprev_ref[:, cs].astype(F32))
            ge = jnp.concatenate([gp, g, gunext_ref[:, cs].astype(F32)], axis=0)
            g1 = _shift_rows(ge, 1)[BF16_ROWS:]
            g2 = _shift_rows(ge, 2)[BF16_ROWS:]
            gc = cb_ref[:, cs] + cw_ref[0:1, cs] * g2 + cw_ref[1:2, cs] * g1 + cw_ref[2:3, cs] * ge[BF16_ROWS:]
            up = jnp.concatenate([gu_ref[:, us].astype(F32), gunext_ref[:, us].astype(F32)], axis=0)
            dan = jnp.where(i == nt - 1, 0.0, danext_ref[:, cs].astype(F32))
            dae = jnp.concatenate([da_ref[:, cs].astype(F32), dan], axis=0)
            s, ds = _silu_and_grad(gc)
            dgc = dae * up * ds
            dgu_ref[:, us] = (dae * s)[:tt].astype(BF16)
            dgate = cw_ref[2:3, cs] * dgc + cw_ref[1:2, cs] * _shift_rows(dgc, n - 1) + cw_ref[0:1, cs] * _shift_rows(dgc, n - 2)
            dgu_ref[:, cs] = dgate[:tt].astype(BF16)
            dm = dgc[:tt]
            dc_ref[0, :, cs] += _rsum8(dm * g2[:tt])
            dc_ref[1, :, cs] += _rsum8(dm * g1[:tt])
            dc_ref[2, :, cs] += _rsum8(dm * g)
            dc_ref[3, :, cs] += _rsum8(dm)

    return pl.pallas_call(
        body, name=name, grid=(nt,),
        in_specs=[pl.BlockSpec((tt, 2 * F), lambda i: (i, 0)),
                  pl.BlockSpec((BF16_ROWS, F), lambda i: (jnp.maximum(i * hb - 1, 0), 0)),
                  pl.BlockSpec((BF16_ROWS, 2 * F), lambda i: (jnp.minimum((i + 1) * hb, last_hb), 0)),
                  pl.BlockSpec((tt, F), lambda i: (i, 0)),
                  pl.BlockSpec((BF16_ROWS, F), lambda i: (jnp.minimum((i + 1) * hb, last_hb), 0)),
                  _resident((SUBLANES, F)), _resident((1, F))],
        out_specs=[pl.BlockSpec((tt, 2 * F), lambda i: (i, 0)), pl.BlockSpec((4, SUBLANES, F), lambda i: (0, 0, 0))],
        out_shape=[jax.ShapeDtypeStruct((T, 2 * F), BF16), jax.ShapeDtypeStruct((4, SUBLANES, F), F32)],
        compiler_params=_cp(1),
    )(gu, gu, gu, da, da, cw, cb)


def _softplus_neg(lam):
    x = -lam
    y = jnp.exp(-jnp.abs(x))
    l1p = jnp.where(y < 0.01, y * (1.0 - y * (0.5 - y * (1.0 / 3.0))), jnp.log(1.0 + y))
    return jnp.maximum(x, 0.0) + l1p


def _lru_gates(xc, wa_ref, ba_ref, wx_ref, bx_ref, sp):
    xcb = xc.astype(BF16)
    r = _sigmoid(jnp.dot(xcb, wa_ref[...], preferred_element_type=F32) + ba_ref[...])
    gi = _sigmoid(jnp.dot(xcb, wx_ref[...], preferred_element_type=F32) + bx_ref[...])
    log_a = -LRU_C * r * sp
    a = jnp.exp(log_a)
    x2 = 2.0 * log_a
    series = -x2 * (1.0 + x2 * 0.5 * (1.0 + x2 * (1.0 / 3.0) * (1.0 + x2 * 0.25 * (1.0 + x2 * 0.2))))
    om = jnp.where(x2 > -0.125, series, 1.0 - a * a)
    return r, gi, a, jnp.sqrt(om)


def _lru_conv(xr, halo, cw_ref, cb_ref):
    e = jnp.concatenate([halo, xr], axis=0)
    x1 = _shift_rows(e, 1)[BF16_ROWS:]
    x2 = _shift_rows(e, 2)[BF16_ROWS:]
    x3 = _shift_rows(e, 3)[BF16_ROWS:]
    xc = cb_ref[...] + cw_ref[0:1, :] * x3 + cw_ref[1:2, :] * x2 + cw_ref[2:3, :] * x1 + cw_ref[3:4, :] * xr
    return xc, x1, x2, x3


def _lru_fwd(z, cw, cb, wa, ba, wx, bx, lam, name="lru_fwd", tt=256):
    T = z.shape[0]
    W = LRU_W
    hb = tt // BF16_ROWS
    ng = tt // SUBLANES

    def body(z_ref, halo_ref, cw_ref, cb_ref, wa_ref, ba_ref, wx_ref, bx_ref, lam_ref, oa_ref, h_ref, a_s, u_s, hc):
        i = pl.program_id(0)

        @pl.when(i == 0)
        def _():
            hc[...] = jnp.zeros_like(hc)

        xr = z_ref[:, W:2 * W].astype(F32)
        halo = jnp.where(i == 0, 0.0, halo_ref[...].astype(F32))
        xc, _, _, _ = _lru_conv(xr, halo, cw_ref, cb_ref)
        sp = _softplus_neg(lam_ref[...])
        r, gi, a, mult = _lru_gates(xc, wa_ref, ba_ref, wx_ref, bx_ref, sp)
        a_s[...] = a
        u_s[...] = mult * gi * xc
        row = lax.broadcasted_iota(jnp.int32, (SUBLANES, W), 0)

        def step(j, hprev):
            r0 = pl.multiple_of(j * SUBLANES, SUBLANES)
            A = a_s[pl.ds(r0, SUBLANES), :]
            U = u_s[pl.ds(r0, SUBLANES), :]
            for k in (1, 2, 4):
                m = row >= k
                U = jnp.where(m, A * pltpu.roll(U, k, 0) + U, U)
                A = jnp.where(m, A * pltpu.roll(A, k, 0), A)
            H = U + A * hprev
            h_ref[pl.ds(r0, SUBLANES), :] = H
            return jnp.broadcast_to(H[SUBLANES - 1:SUBLANES, :], (SUBLANES, W))

        hc[...] = lax.fori_loop(0, ng, step, hc[...])
        oa_ref[...] = (_gelu(z_ref[:, 0:W].astype(F32)) * h_ref[...]).astype(BF16)

    return pl.pallas_call(
        body, name=name, grid=(T // tt,),
        in_specs=[pl.BlockSpec((tt, 2 * W), lambda i: (i, 0)),
                  pl.BlockSpec((BF16_ROWS, W), lambda i: (jnp.maximum(i * hb - 1, 0), 1)),
                  _resident((SUBLANES, W)), _resident((1, W)), _resident((W, W)), _resident((1, W)),
                  _resident((W, W)), _resident((1, W)), _resident((1, W))],
        out_specs=[pl.BlockSpec((tt, W), lambda i: (i, 0)), pl.BlockSpec((tt, W), lambda i: (i, 0))],
        out_shape=[jax.ShapeDtypeStruct((T, W), BF16), jax.ShapeDtypeStruct((T, W), F32)],
        scratch_shapes=[pltpu.VMEM((tt, W), F32), pltpu.VMEM((tt, W), F32), pltpu.VMEM((SUBLANES, W), F32)],
        compiler_params=_cp(1),
    )(z, z, cw, cb, wa, ba, wx, bx, lam)


def _lru_bwd(z, hseq, dmix, cw, cb, wa, wat, ba, wx, wxt, bx, lam, name="lru_bwd", tt=256):
    T = z.shape[0]
    W = LRU_W
    nt = T // tt
    hb = tt // BF16_ROWS
    sb = tt // SUBLANES
    ng = tt // SUBLANES

    def body(z_ref, halo_ref, h_ref, hprev_ref, dm_ref, cw_ref, cb_ref, wa_ref, wat_ref, ba_ref, wx_ref, wxt_ref, bx_ref,
             lam_ref, dz_ref, dc_ref, dwa_ref, dwx_ref, dv_ref, c_s, d_s, g_s, gc, an, dxn):
        i = pl.program_id(0)
        ti = nt - 1 - i

        @pl.when(i == 0)
        def _():
            dc_ref[...] = jnp.zeros_like(dc_ref)
            dwa_ref[...] = jnp.zeros_like(dwa_ref)
            dwx_ref[...] = jnp.zeros_like(dwx_ref)
            dv_ref[...] = jnp.zeros_like(dv_ref)
            gc[...] = jnp.zeros_like(gc)
            an[...] = jnp.zeros_like(an)
            dxn[...] = jnp.zeros_like(dxn)

        xr = z_ref[:, W:2 * W].astype(F32)
        yg = z_ref[:, 0:W].astype(F32)
        halo = jnp.where(ti == 0, 0.0, halo_ref[...].astype(F32))
        xc, x1, x2, x3 = _lru_conv(xr, halo, cw_ref, cb_ref)
        sp = _softplus_neg(lam_ref[...])
        r, gi, a, mult = _lru_gates(xc, wa_ref, ba_ref, wx_ref, bx_ref, sp)
        h = h_ref[...]
        hp = jnp.where(ti == 0, 0.0, hprev_ref[...])
        hm1 = _shift_rows(jnp.concatenate([hp, h], axis=0), 1)[SUBLANES:]
        dout = dm_ref[...].astype(F32)
        d_s[...] = dout * _gelu(yg)
        dz_ref[:, 0:W] = (dout * h * _gelu_grad(yg)).astype(BF16)
        c_s[...] = _shift_rows(jnp.concatenate([a, an[...]], axis=0), tt + SUBLANES - 1)[:tt]
        an[...] = a[0:SUBLANES, :]
        row = lax.broadcasted_iota(jnp.int32, (SUBLANES, W), 0)

        def step(j, gnext):
            r0 = pl.multiple_of((ng - 1 - j) * SUBLANES, SUBLANES)
            C = c_s[pl.ds(r0, SUBLANES), :]
            G = d_s[pl.ds(r0, SUBLANES), :]
            for k in (1, 2, 4):
                m = row < SUBLANES - k
                G = jnp.where(m, G + C * pltpu.roll(G, SUBLANES - k, 0), G)
                C = jnp.where(m, C * pltpu.roll(C, SUBLANES - k, 0), C)
            G = G + C * gnext
            g_s[pl.ds(r0, SUBLANES), :] = G
            return jnp.broadcast_to(G[0:1, :], (SUBLANES, W))

        gc[...] = lax.fori_loop(0, ng, step, gc[...])
        du = g_s[...]
        da = du * hm1
        dgi = du * mult * xc
        dxc = du * mult * gi
        dmult = du * gi * xc
        dlog_a = da * a - dmult * (a * a) / mult
        dr = dlog_a * (-LRU_C * sp)
        dv_ref[2] += _rsum8(dlog_a * (-LRU_C * r))
        dpr = (dr * r * (1.0 - r)).astype(BF16)
        dpi = (dgi * gi * (1.0 - gi)).astype(BF16)
        dv_ref[0] += _rsum8(dpr.astype(F32))
        dv_ref[1] += _rsum8(dpi.astype(F32))
        xcb = xc.astype(BF16)
        dwa_ref[...] += lax.dot_general(xcb, dpr, TN, preferred_element_type=F32)
        dwx_ref[...] += lax.dot_general(xcb, dpi, TN, preferred_element_type=F32)
        dxc = dxc + jnp.dot(dpr, wat_ref[...], preferred_element_type=F32) + jnp.dot(dpi, wxt_ref[...], preferred_element_type=F32)
        n = tt + BF16_ROWS
        de = jnp.concatenate([dxc, dxn[...]], axis=0)
        dxr = (cw_ref[3:4, :] * dxc + cw_ref[2:3, :] * _shift_rows(de, n - 1)[:tt] + cw_ref[1:2, :] * _shift_rows(de, n - 2)[:tt]
               + cw_ref[0:1, :] * _shift_rows(de, n - 3)[:tt])
        dxn[...] = dxc[0:BF16_ROWS, :]
        dz_ref[:, W:2 * W] = dxr.astype(BF16)
        dc_ref[0] += _rsum8(dxc * x3)
        dc_ref[1] += _rsum8(dxc * x2)
        dc_ref[2] += _rsum8(dxc * x1)
        dc_ref[3] += _rsum8(dxc * xr)
        dc_ref[4] += _rsum8(dxc)

    rev = lambda i: nt - 1 - i
    return pl.pallas_call(
        body, name=name, grid=(nt,),
        in_specs=[pl.BlockSpec((tt, 2 * W), lambda i: (rev(i), 0)),
                  pl.BlockSpec((BF16_ROWS, W), lambda i: (jnp.maximum(rev(i) * hb - 1, 0), 1)),
                  pl.BlockSpec((tt, W), lambda i: (rev(i), 0)),
                  pl.BlockSpec((SUBLANES, W), lambda i: (jnp.maximum(rev(i) * sb - 1, 0), 0)),
                  pl.BlockSpec((tt, W), lambda i: (rev(i), 0)),
                  _resident((SUBLANES, W)), _resident((1, W)), _resident((W, W)), _resident((W, W)), _resident((1, W)),
                  _resident((W, W)), _resident((W, W)), _resident((1, W)), _resident((1, W))],
        out_specs=[pl.BlockSpec((tt, 2 * W), lambda i: (rev(i), 0)),
                   pl.BlockSpec((5, SUBLANES, W), lambda i: (0, 0, 0)),
                   pl.BlockSpec((W, W), lambda i: (0, 0)), pl.BlockSpec((W, W), lambda i: (0, 0)),
                   pl.BlockSpec((3, SUBLANES, W), lambda i: (0, 0, 0))],
        out_shape=[jax.ShapeDtypeStruct((T, 2 * W), BF16), jax.ShapeDtypeStruct((5, SUBLANES, W), F32),
                   jax.ShapeDtypeStruct((W, W), F32), jax.ShapeDtypeStruct((W, W), F32),
                   jax.ShapeDtypeStruct((3, SUBLANES, W), F32)],
        scratch_shapes=[pltpu.VMEM((tt, W), F32), pltpu.VMEM((tt, W), F32), pltpu.VMEM((tt, W), F32),
                        pltpu.VMEM((SUBLANES, W), F32), pltpu.VMEM((SUBLANES, W), F32), pltpu.VMEM((BF16_ROWS, W), F32)],
        compiler_params=_cp(1),
    )(z, z, hseq, hseq, dmix, cw, cb, wa, wat, ba, wx, wxt, bx, lam)


def _split3(x):
    hi = x.astype(BF16)
    r1 = x - hi.astype(F32)
    mid = r1.astype(BF16)
    lo = (r1 - mid.astype(F32)).astype(BF16)
    return hi, mid, lo


def _tri_matmul(tri, x):
    hi, mid, lo = _split3(x)
    return (jnp.dot(tri, hi, preferred_element_type=F32) + jnp.dot(tri, mid, preferred_element_type=F32)
            + jnp.dot(tri, lo, preferred_element_type=F32))


def _hg_chunk(q, fl, lb):
    C = q.shape[0]
    ri = lax.broadcasted_iota(jnp.int32, (C, C), 0)
    ci = lax.broadcasted_iota(jnp.int32, (C, C), 1)
    causal = ri >= ci
    sig = _sigmoid(fl)
    f = lb + (1.0 - lb) * sig
    k = 1.0 - f
    sq = _sigmoid(q)
    qf = q * sq
    b = _tri_matmul(causal.astype(BF16), jnp.log(f))
    bm = b[C // 2 - 1:C // 2, :]
    bl = b[C - 1:C, :]
    qt = qf * jnp.exp(b - bm)
    kt = k * jnp.exp(bm - b)
    qin = qf * jnp.exp(b)
    kout = k * jnp.exp(bl - b)
    qtb, ktb = qt.astype(BF16), kt.astype(BF16)
    att = [jnp.where(causal, _dot(qtb[:, _head(h)], ktb[:, _head(h)], NT), 0.0).astype(BF16) for h in range(HG_HEADS)]
    return dict(sig=sig, f=f, k=k, sq=sq, qf=qf, b=b, bm=bm, bl=bl, qt=qt, kt=kt, qin=qin, kout=kout, att=att,
                causal=causal, anti=ri <= ci, decay=jnp.exp(bl))


def _head(h):
    return slice(h * HG_D, (h + 1) * HG_D)


def _hgrn_fwd(z, lb, gn, name="hgrn_fwd", tt=256):
    T = z.shape[0]
    C = HG_CHUNK
    nc = tt // C
    Dh = HG_D
    Wd = HG_HEADS * Dh

    def body(q_ref, f_ref, v_ref, g_ref, lb_ref, gn_ref, o_ref, ss_ref, st):
        @pl.when(pl.program_id(0) == 0)
        def _():
            st[...] = jnp.zeros_like(st)

        S = [st[h] for h in range(HG_HEADS)]
        for c in range(nc):
            rows = slice(c * C, (c + 1) * C)
            ck = _hg_chunk(q_ref[rows, :].astype(F32), f_ref[rows, :].astype(F32), lb_ref[...])
            v = v_ref[rows, :]
            g = g_ref[rows, :].astype(F32)
            H = range(HG_HEADS)
            qinb, koutb = ck["qin"].astype(BF16), ck["kout"].astype(BF16)
            for h in H:
                ss_ref[h, c] = S[h]
            o = [_dot(ck["att"][h], v[:, _head(h)]) + _dot(qinb[:, _head(h)], S[h], NT) for h in H]
            S = [ck["decay"][:, _head(h)] * S[h] + _dot(v[:, _head(h)], koutb[:, _head(h)], TN) for h in H]
            outs = [o[h] * lax.rsqrt(jnp.mean(o[h] * o[h], axis=-1, keepdims=True) + EPS) * gn_ref[...] for h in H]
            o_ref[rows, :] = (jnp.concatenate(outs, axis=1) * (g * _sigmoid(g))).astype(BF16)
        for h in range(HG_HEADS):
            st[h] = S[h]

    col = lambda base: (lambda i: (i, base))
    return pl.pallas_call(
        body, name=name, grid=(T // tt,),
        in_specs=[pl.BlockSpec((tt, Wd), col(2)), pl.BlockSpec((tt, Wd), col(3)), pl.BlockSpec((tt, Wd), col(4)),
                  pl.BlockSpec((tt, Wd), col(5)), _resident((1, Wd)), _resident((1, Dh))],
        out_specs=[pl.BlockSpec((tt, Wd), lambda i: (i, 0)),
                   pl.BlockSpec((HG_HEADS, nc, Dh, Dh), lambda i: (0, i, 0, 0))],
        out_shape=[jax.ShapeDtypeStruct((T, Wd), BF16),
                   jax.ShapeDtypeStruct((HG_HEADS, T // C, Dh, Dh), F32)],
        scratch_shapes=[pltpu.VMEM((HG_HEADS, Dh, Dh), F32)],
        compiler_params=_cp(1),
    )(z, z, z, z, lb, gn)


def _hgrn_bwd(z, ss, dmix, lb, gn, name="hgrn_bwd", tt=256):
    T = z.shape[0]
    C = HG_CHUNK
    nc = tt // C
    nt = T // tt
    Dh = HG_D
    Wd = HG_HEADS * Dh

    def body(q_ref, f_ref, v_ref, g_ref, ss_ref, dm_ref, lb_ref, gn_ref, dz_ref, dlb_ref, dgn_ref, dst):
        @pl.when(pl.program_id(0) == 0)
        def _():
            dst[...] = jnp.zeros_like(dst)
            dlb_ref[...] = jnp.zeros_like(dlb_ref)
            dgn_ref[...] = jnp.zeros_like(dgn_ref)

        dS = [dst[h] for h in range(HG_HEADS)]
        lbv = lb_ref[...]
        gnv = gn_ref[...]
        rowc = lax.broadcasted_iota(jnp.int32, (C, Wd), 0)
        cat = lambda xs: jnp.concatenate(xs, axis=1)
        for c in reversed(range(nc)):
            rows = slice(c * C, (c + 1) * C)
            q = q_ref[rows, :].astype(F32)
            ck = _hg_chunk(q, f_ref[rows, :].astype(F32), lbv)
            v = v_ref[rows, :]
            g = g_ref[rows, :].astype(F32)
            dout = dm_ref[rows, :].astype(F32)
            sg, dsg = _silu_and_grad(g)
            d_ong = dout * sg
            H = range(HG_HEADS)
            qinb, koutb, qtb, ktb = (ck[n].astype(BF16) for n in ("qin", "kout", "qt", "kt"))
            S = [ss_ref[h, c] for h in H]
            Sb = [s.astype(BF16) for s in S]
            dSb = [d.astype(BF16) for d in dS]
            o = [_dot(ck["att"][h], v[:, _head(h)]) + _dot(qinb[:, _head(h)], Sb[h], NT) for h in H]
            rn = [lax.rsqrt(jnp.mean(o[h] * o[h], axis=-1, keepdims=True) + EPS) for h in H]
            on = [o[h] * rn[h] for h in H]
            don = [d_ong[:, _head(h)] * gnv for h in H]
            do = [(rn[h] * (don[h] - on[h] * jnp.mean(don[h] * on[h], axis=-1, keepdims=True))).astype(BF16) for h in H]
            datt = [jnp.where(ck["causal"], _dot(do[h], v[:, _head(h)], NT), 0.0).astype(BF16) for h in H]
            dvs = [_dot(ck["att"][h], do[h], TN) + _dot(koutb[:, _head(h)], dSb[h], NT) for h in H]
            dqins = [_dot(do[h], Sb[h]) for h in H]
            dkouts = [_dot(v[:, _head(h)], dSb[h]) for h in H]
            dqts = [_dot(datt[h], ktb[:, _head(h)]) for h in H]
            dkts = [_dot(datt[h], qtb[:, _head(h)], TN) for h in H]
            ddecays = [jnp.sum(dS[h] * S[h], axis=0, keepdims=True) for h in H]
            dS = [_dot(do[h], qinb[:, _head(h)], TN) + ck["decay"][:, _head(h)] * dS[h] for h in H]
            ons = [on[h] * gnv for h in H]
            dgn = _rsum8(d_ong[:, _head(0)] * on[0])
            for h in range(1, HG_HEADS):
                dgn = dgn + _rsum8(d_ong[:, _head(h)] * on[h])
            dgn_ref[...] += dgn
            dqt, dkt, dqin, dkout, ddecay = cat(dqts), cat(dkts), cat(dqins), cat(dkouts), cat(ddecays)
            b, bm, bl = ck["b"], ck["bm"], ck["bl"]
            dqf = dqt * jnp.exp(b - bm) + dqin * jnp.exp(b)
            dk = dkt * jnp.exp(bm - b) + dkout * jnp.exp(bl - b)
            kk = dkout * ck["kout"]
            db = dqt * ck["qt"] - dkt * ck["kt"] + dqin * ck["qin"] - kk
            dbl = jnp.sum(kk, axis=0, keepdims=True) + ddecay * ck["decay"]
            db = db + jnp.where(rowc == C - 1, dbl, 0.0)
            dlogf = _tri_matmul(ck["anti"].astype(BF16), db)
            dfv = dlogf / ck["f"] - dk
            sig, sq = ck["sig"], ck["sq"]
            dlb_ref[...] += _rsum8(dfv * (1.0 - sig))
            dz_ref[rows, 0:Wd] = (dqf * (sq * (1.0 + q * (1.0 - sq)))).astype(BF16)
            dz_ref[rows, Wd:2 * Wd] = (dfv * (1.0 - lbv) * sig * (1.0 - sig)).astype(BF16)
            dz_ref[rows, 2 * Wd:3 * Wd] = cat(dvs).astype(BF16)
            dz_ref[rows, 3 * Wd:4 * Wd] = (dout * cat(ons) * dsg).astype(BF16)
        for h in range(HG_HEADS):
            dst[h] = dS[h]

    rev = lambda i: nt - 1 - i
    col = lambda base: (lambda i: (rev(i), base))
    return pl.pallas_call(
        body, name=name, grid=(nt,),
        in_specs=[pl.BlockSpec((tt, Wd), col(2)), pl.BlockSpec((tt, Wd), col(3)), pl.BlockSpec((tt, Wd), col(4)),
                  pl.BlockSpec((tt, Wd), col(5)),
                  pl.BlockSpec((HG_HEADS, nc, Dh, Dh), lambda i: (0, rev(i), 0, 0)),
                  pl.BlockSpec((tt, Wd), col(1)), _resident((1, Wd)), _resident((1, Dh))],
        out_specs=[pl.BlockSpec((tt, 4 * Wd), lambda i: (rev(i), 0)), pl.BlockSpec((SUBLANES, Wd), lambda i: (0, 0)),
                   pl.BlockSpec((SUBLANES, Dh), lambda i: (0, 0))],
        out_shape=[jax.ShapeDtypeStruct((T, 4 * Wd), BF16), jax.ShapeDtypeStruct((SUBLANES, Wd), F32),
                   jax.ShapeDtypeStruct((SUBLANES, Dh), F32)],
        scratch_shapes=[pltpu.VMEM((HG_HEADS, Dh, Dh), F32)],
        compiler_params=_cp(1),
    )(z, z, z, z, ss, dmix, lb, gn)


def _sgu_core(p, lg_ref, lb_ref, wsc_ref, bsb_ref):
    Wd = D_MODEL
    G = SGU_CHUNK
    zz = _gelu(p)
    u = zz[:, :Wd]
    v = zz[:, Wd:]
    vc = v - jnp.mean(v, axis=-1, keepdims=True)
    rstd = lax.rsqrt(jnp.mean(vc * vc, axis=-1, keepdims=True) + EPS)
    vhat = vc * rstd
    vn = vhat * lg_ref[...] + lb_ref[...]
    svs = []
    for gi in range(SGU_G):
        svs.append(jnp.dot(wsc_ref[gi], vn[:, gi * G:(gi + 1) * G].astype(BF16), preferred_element_type=F32) + bsb_ref[gi])
    return u, vhat, rstd, vn, jnp.concatenate(svs, axis=1)


def _sgu_fwd(p1, lg, lbias, wsc, bsb, name="sgu_fwd", tt=512):
    T = p1.shape[0]
    Wd = D_MODEL
    C = SGU_CHUNK

    def body(p_ref, lg_ref, lb_ref, wsc_ref, bsb_ref, s_ref):
        for c in range(tt // C):
            rows = slice(c * C, (c + 1) * C)
            u, _, _, _, sv = _sgu_core(p_ref[rows, :].astype(F32), lg_ref, lb_ref, wsc_ref, bsb_ref)
            s_ref[rows, :] = (u * sv).astype(BF16)

    return pl.pallas_call(
        body, name=name, grid=(T // tt,),
        in_specs=[pl.BlockSpec((tt, 2 * Wd), lambda i: (i, 0)), _resident((1, Wd)), _resident((1, Wd)),
                  _resident((SGU_G, C, C)), _resident((SGU_G, C, C))],
        out_specs=pl.BlockSpec((tt, Wd), lambda i: (i, 0)),
        out_shape=jax.ShapeDtypeStruct((T, Wd), BF16),
        compiler_params=_cp(1),
    )(p1, lg, lbias, wsc, bsb)


def _sgu_bwd(p1, ds, lg, lbias, wsc, wsct, bsb, name="sgu_bwd", tt=512):
    T = p1.shape[0]
    Wd = D_MODEL
    C = SGU_CHUNK

    def body(p_ref, ds_ref, lg_ref, lb_ref, wsc_ref, wsct_ref, bsb_ref, dp_ref, dws_ref, dbs_ref, dlg_ref, dlb_ref, dbin_ref):
        @pl.when(pl.program_id(0) == 0)
        def _():
            dws_ref[...] = jnp.zeros_like(dws_ref)
            dbs_ref[...] = jnp.zeros_like(dbs_ref)
            dlg_ref[...] = jnp.zeros_like(dlg_ref)
            dlb_ref[...] = jnp.zeros_like(dlb_ref)
            dbin_ref[...] = jnp.zeros_like(dbin_ref)

        for c in range(tt // C):
            rows = slice(c * C, (c + 1) * C)
            p = p_ref[rows, :].astype(F32)
            u, vhat, rstd, vn, sv = _sgu_core(p, lg_ref, lb_ref, wsc_ref, bsb_ref)
            dsc = ds_ref[rows, :].astype(F32)
            du = dsc * sv
            dsv = dsc * u
            dvns = []
            for gi in range(SGU_G):
                cs = slice(gi * C, (gi + 1) * C)
                dsv_g = dsv[:, cs]
                dvns.append(jnp.dot(wsct_ref[gi], dsv_g.astype(BF16), preferred_element_type=F32))
                dws_ref[gi] += _dot(dsv_g, vn[:, cs], NT)
                dbs_ref[gi] += dsv_g
            dvn = jnp.concatenate(dvns, axis=1)
            dlg_ref[...] += _rsum8(dvn * vhat)
            dlb_ref[...] += _rsum8(dvn)
            dvh = dvn * lg_ref[...]
            dv = rstd * (dvh - jnp.mean(dvh, axis=-1, keepdims=True) - vhat * jnp.mean(dvh * vhat, axis=-1, keepdims=True))
            dp = jnp.concatenate([du, dv], axis=1) * _gelu_grad(p)
            dbin_ref[...] += _rsum8(dp)
            dp_ref[rows, :] = dp.astype(BF16)

    full3 = pl.BlockSpec((SGU_G, C, C), lambda i: (0, 0, 0))
    return pl.pallas_call(
        body, name=name, grid=(T // tt,),
        in_specs=[pl.BlockSpec((tt, 2 * Wd), lambda i: (i, 0)), pl.BlockSpec((tt, Wd), lambda i: (i, 0)),
                  _resident((1, Wd)), _resident((1, Wd)), _resident((SGU_G, C, C)), _resident((SGU_G, C, C)),
                  _resident((SGU_G, C, C))],
        out_specs=[pl.BlockSpec((tt, 2 * Wd), lambda i: (i, 0)), full3, full3,
                   pl.BlockSpec((SUBLANES, Wd), lambda i: (0, 0)), pl.BlockSpec((SUBLANES, Wd), lambda i: (0, 0)),
                   pl.BlockSpec((SUBLANES, 2 * Wd), lambda i: (0, 0))],
        out_shape=[jax.ShapeDtypeStruct((T, 2 * Wd), BF16), jax.ShapeDtypeStruct((SGU_G, C, C), F32),
                   jax.ShapeDtypeStruct((SGU_G, C, C), F32), jax.ShapeDtypeStruct((SUBLANES, Wd), F32),
                   jax.ShapeDtypeStruct((SUBLANES, Wd), F32), jax.ShapeDtypeStruct((SUBLANES, 2 * Wd), F32)],
        compiler_params=_cp(1),
    )(p1, ds, lg, lbias, wsc, wsct, bsb)


def _pad_rows(w, rows=SUBLANES):
    return jnp.pad(w, ((0, rows - w.shape[0]), (0, 0)))


def _block_diag(w):
    n, b, _ = w.shape
    return (w[:, :, None, :] * jnp.eye(n, dtype=w.dtype)[:, None, :, None]).reshape(n * b, n * b)


def _diag_blocks(m, n):
    b = m.shape[0] // n
    m4 = m.reshape(n, b, n, b)
    return jnp.stack([m4[k, :, k, :] for k in range(n)], axis=0)


def _piece_major(dw):
    if dw.ndim == 2:
        K, N = dw.shape
        return dw.reshape(N_CHIPS, 2, K // (2 * N_CHIPS), N)
    _, K, ns = dw.shape
    return dw.reshape(N_CHIPS, 2, K // 2, ns)


def _ffn_fwd(h, g, w_up, cw, cb, w_down, tag):
    hn, gu = _norm_mm(h, g, w_up, jnp.zeros((1, w_up.shape[1]), F32), name=f"ffn_up_{tag}")
    a = _ffn_act(gu, cw, cb, name=f"ffn_act_{tag}")
    out = _mm(a, w_down, h, F32, name=f"ffn_down_{tag}")
    return out, (hn, gu, a)


def _ffn_bwd(dh, h, g, saved, w_up, cw, cb, w_down, tag):
    hn, gu, a = saved
    da = _mm(dh, w_down, None, BF16, name=f"ffn_da_{tag}", transpose_w=True)
    dwd = _mm_tn(a, dh, name=f"ffn_dwd_{tag}", tk=D_FF // 2)
    dgu, dc = _ffn_act_bwd(gu, da, cw, cb, name=f"ffn_actb_{tag}")
    dhin, dg8 = _mm_normbwd(dgu, w_up, h, g, dh, name=f"ffn_dh_{tag}")
    dwu = _mm_tn(hn, dgu, name=f"ffn_dwu_{tag}", tk=256, col_shards=N_CHIPS)
    dcs = dc.sum(axis=1)
    return dhin, dg8.sum(axis=0), dwu, dcs[0:3], dcs[3], dwd


REDUCE_GROUPS = {"g1": [("ffn_w_up", 1), ("ffn_w_down", 1), ("od_w_out", 0), ("od_w_in", 0)],
                 "g2": [("ffn_w_up", 0), ("ffn_w_down", 0)],
                 "g3": [("ev_w_out", 0), ("ev_w_in", 0)]}


def _local_step(x, tgt, p, start_reduce=None):
    row = lambda v: v.reshape(1, -1)
    grads = {}

    lower = jax.nn.softmax(p["hg_lb_logits"], axis=0)
    lb0 = row(lower[0])
    ev_cw = _pad_rows(p["ev_conv_w"][0])
    ev_cb = row(p["ev_conv_b"][0])
    wa = _block_diag(p["ev_gate_a_w"][0]).astype(BF16)
    wx = _block_diag(p["ev_gate_x_w"][0]).astype(BF16)
    ba, bx, lam = row(p["ev_gate_a_b"][0]), row(p["ev_gate_x_b"][0]), row(p["ev_lru_lambda"][0])
    gn = row(p["ev_hg_norm"][0])
    tril = jnp.tril(jnp.ones((SGU_CHUNK, SGU_CHUNK), F32))
    wsc = (p["od_w_s"][0] * tril).astype(BF16)
    bsb = jnp.broadcast_to(p["od_b_s"][0][:, :, None], (SGU_G, SGU_CHUNK, SGU_CHUNK)).astype(F32)
    ffn_cw = [_pad_rows(p["ffn_conv_w"][l]) for l in range(2)]
    ffn_cb = [row(p["ffn_conv_b"][l]) for l in range(2)]
    ev_w_in, ev_w_out = p["ev_w_in"][0], p["ev_w_out"][0]
    nm = [row(p["norm_mix"][l]) for l in range(2)]
    nf = [row(p["norm_ffn"][l]) for l in range(2)]

    h0 = x
    hn0, z0 = _norm_mm(h0, nm[0], ev_w_in, jnp.zeros((1, ev_w_in.shape[1]), F32), name="ev_in")
    out_a, hseq = _lru_fwd(z0, ev_cw, ev_cb, wa, ba, wx, bx, lam)
    out_b, ss = _hgrn_fwd(z0, lb0, gn)
    mix0 = jnp.concatenate([out_a, out_b], axis=1)
    h1 = _mm(mix0, ev_w_out, h0, F32, name="ev_out")
    late = p["late"](h1) if "late" in p else p
    od_w_in, od_w_out = late["od_w_in"][0], late["od_w_out"][0]
    w_up = [late["ffn_w_up"][l] for l in range(2)]
    w_down = [late["ffn_w_down"][l] for l in range(2)]
    h2, ffn0 = _ffn_fwd(h1, nf[0], w_up[0], ffn_cw[0], ffn_cb[0], w_down[0], "l0")
    hn1, p1 = _norm_mm(h2, nm[1], od_w_in, row(p["od_b_in"][0]), name="od_in")
    s1 = _sgu_fwd(p1, row(p["od_ln_g"][0]), row(p["od_ln_b"][0]), wsc, bsb)
    h3 = _mm(s1, od_w_out, h2, F32, name="od_out")
    h4, ffn1 = _ffn_fwd(h3, nf[1], w_up[1], ffn_cw[1], ffn_cb[1], w_down[1], "l1")
    dh4, dgf8, sq8 = _final_loss(h4, row(p["norm_final"]), tgt)
    grads["norm_final"] = dgf8.sum(axis=0)

    big = {}
    dh3, dnf1, dwu1, dcw1, dcb1, dwd1 = _ffn_bwd(dh4, h3, nf[1], ffn1, w_up[1], ffn_cw[1], ffn_cb[1], w_down[1], "l1")
    big["ffn_w_up", 1], big["ffn_w_down", 1] = _piece_major(dwu1), _piece_major(dwd1)
    ds1 = _mm(dh3, od_w_out, None, BF16, name="od_ds", transpose_w=True)
    big["od_w_out", 0] = _piece_major(_mm_tn(s1, dh3, name="od_dwo", tk=512))
    wsct = jnp.swapaxes(wsc, 1, 2)
    dp1, dws, dbs, dlg8, dlb8, dbin8 = _sgu_bwd(p1, ds1, row(p["od_ln_g"][0]), row(p["od_ln_b"][0]), wsc, wsct, bsb)
    grads["od_w_s"] = (dws * tril)[None]
    grads["od_b_s"] = dbs.sum(axis=-1)[None]
    grads["od_ln_g"] = dlg8.sum(axis=0)[None]
    grads["od_ln_b"] = dlb8.sum(axis=0)[None]
    grads["od_b_in"] = dbin8.sum(axis=0)[None]
    dh2, dnm1 = _mm_normbwd(dp1, od_w_in, h2, nm[1], dh3, name="od_dh")
    big["od_w_in", 0] = _piece_major(_mm_tn(hn1, dp1, name="od_dwi", tk=512, col_shards=N_CHIPS))
    if start_reduce is not None:
        token = start_reduce("g1", [big[key] for key in REDUCE_GROUPS["g1"]])
        ffn_cb[0] = ffn_cb[0] + token[0:1, 0:1]

    dh1, dnf0, dwu0, dcw0, dcb0, dwd0 = _ffn_bwd(dh2, h1, nf[0], ffn0, w_up[0], ffn_cw[0], ffn_cb[0], w_down[0], "l0")
    big["ffn_w_up", 0], big["ffn_w_down", 0] = _piece_major(dwu0), _piece_major(dwd0)
    if start_reduce is not None:
        token = start_reduce("g2", [big[key] for key in REDUCE_GROUPS["g2"]])
        ev_cb = ev_cb + token[0:1, 0:1]
    dmix = _mm(dh1, ev_w_out, None, BF16, name="ev_dmix", transpose_w=True)
    big["ev_w_out", 0] = _piece_major(_mm_tn(mix0, dh1, name="ev_dwo", tk=512))
    dz01, dc5, dwa, dwx, dvec = _lru_bwd(z0, hseq, dmix, ev_cw, ev_cb, wa, wa.T, ba, wx, wx.T, bx, lam)
    dz2, dlb8, dgn8 = _hgrn_bwd(z0, ss, dmix, lb0, gn)
    dz0 = jnp.concatenate([dz01, dz2], axis=1)
    big["ev_w_in", 0] = _piece_major(_mm_tn(hn0, dz0, name="ev_dwi", tk=512, col_shards=N_CHIPS))
    if start_reduce is not None:
        token = start_reduce("g3", [big[key] for key in REDUCE_GROUPS["g3"]])
        nm[0] = nm[0] + token[0:1, 0:1]
    grad_x, dnm0 = _mm_normbwd(dz0, ev_w_in, h0, nm[0], dh1, name="ev_dh")

    dc5s = dc5.sum(axis=1)
    grads["ev_conv_w"] = dc5s[0:4][None]
    grads["ev_conv_b"] = dc5s[4][None]
    grads["ev_gate_a_w"] = _diag_blocks(dwa, LRU_BLOCKS)[None]
    grads["ev_gate_x_w"] = _diag_blocks(dwx, LRU_BLOCKS)[None]
    dvs = dvec.sum(axis=1)
    grads["ev_gate_a_b"] = dvs[0][None]
    grads["ev_gate_x_b"] = dvs[1][None]
    grads["ev_lru_lambda"] = (dvs[2] * (-jax.nn.sigmoid(-p["ev_lru_lambda"][0])))[None]
    dlb = dlb8.sum(axis=0)
    grads["hg_lb_logits"] = dlb[None, :] * lower[0][None, :] * (jnp.eye(3, dtype=F32)[0][:, None] - lower)
    grads["ev_hg_norm"] = dgn8.sum(axis=0)[None]
    grads["norm_mix"] = jnp.stack([dnm0.sum(axis=0), dnm1.sum(axis=0)])
    grads["norm_ffn"] = jnp.stack([dnf0, dnf1])
    grads["ffn_conv_w"] = jnp.stack([dcw0, dcw1])
    grads["ffn_conv_b"] = jnp.stack([dcb0, dcb1])
    return sq8, grad_x, grads, big


MESH = pl.DeviceIdType.MESH
ANY = pl.BlockSpec(memory_space=pl.ANY)
N_CHIPS = 4
N_DEV = 8

SH_BIG = {"ev_w_in": 2, "ev_w_out": 1, "od_w_in": 2, "od_w_out": 1, "ffn_w_up": 2, "ffn_w_down": 1}
SH_SMALL = {"ev_conv_w": 2, "od_b_in": 1, "od_ln_g": 1, "od_ln_b": 1, "ffn_conv_w": 2}
REP = ["norm_mix", "norm_ffn", "norm_final", "ev_conv_b", "ev_gate_a_w", "ev_gate_a_b", "ev_gate_x_w", "ev_gate_x_b",
       "ev_lru_lambda", "hg_lb_logits", "ev_hg_norm", "od_w_s", "od_b_s", "ffn_conv_b"]
WEIGHTS = ["norm_mix", "norm_ffn", "norm_final", "ev_w_in", "ev_conv_w", "ev_conv_b", "ev_gate_a_w", "ev_gate_a_b", "ev_gate_x_w",
           "ev_gate_x_b", "ev_lru_lambda", "hg_lb_logits", "ev_hg_norm", "ev_w_out", "od_w_in", "od_b_in", "od_ln_g", "od_ln_b",
           "od_w_s", "od_b_s", "od_w_out", "ffn_w_up", "ffn_conv_w", "ffn_conv_b", "ffn_w_down"]


def _rows(n_elems, mult=SUBLANES):
    r = -(-n_elems // LANES)
    return -(-r // mult) * mult


def _pack(arrs, rows, dtype):
    flat = jnp.concatenate([a.reshape(-1).astype(dtype) for a in arrs])
    return jnp.pad(flat, (0, rows * LANES - flat.shape[0])).reshape(rows, LANES)


def _unpack(flat2d, shapes):
    flat = flat2d.reshape(-1)
    out, off = [], 0
    for s in shapes:
        n = 1
        for d in s:
            n *= d
        out.append(flat[off:off + n].reshape(s))
        off += n
    return out


def _mesh_pos():
    return lax.axis_index("x"), lax.axis_index("y"), lax.axis_index("c")


def _other_chips(x, y):
    return [(1 - x, y), (x, 1 - y), (1 - x, 1 - y)]


def _half_rows(n):
    return lambda r, c: r.at[0, pl.ds(c * (n // 2), n // 2), :]


GATHER_BIG = {
    "ev_w_in": ((1024, 3072), _half_rows(1024), lambda o, k, c: o.at[pl.ds(c * 512, 512), pl.ds(k * 768, 768)]),
    "ev_w_out": ((1024, 1024), _half_rows(256), lambda o, k, c: o.at[pl.ds(k * 256 + c * 128, 128), :]),
    "od_w_in": ((1024, 2048), _half_rows(1024), lambda o, k, c: o.at[pl.ds(c * 512, 512), pl.ds(k * 512, 512)]),
    "od_w_out": ((1024, 1024), _half_rows(256), lambda o, k, c: o.at[pl.ds(k * 256 + c * 128, 128), :]),
    "ffn_w_up": ((2, 1024, 2 * D_FF), lambda r, c: r.at[c], lambda o, k, c: o.at[c, :, pl.ds(k * (2 * D_FF // 4), 2 * D_FF // 4)]),
    "ffn_w_down": ((2, D_FF, 1024), lambda r, c: r.at[c], lambda o, k, c: o.at[c, pl.ds(k * (D_FF // 4), D_FF // 4), :]),
}


def _gather_weights(names, big, small):
    nb = len(big)
    descs = [GATHER_BIG[n] for n in names]
    rs = small.shape[0]

    def body(*refs):
        ins, s_ref = refs[:nb], refs[nb]
        outs, os_ref = refs[nb + 1:2 * nb + 1], refs[2 * nb + 1]
        ici_send, ici_recv, d2d_send, d2d_recv, loc_sems = refs[2 * nb + 2:2 * nb + 7]
        vbufs = refs[2 * nb + 7:]
        x, y, c = _mesh_pos()
        k = 2 * x + y
        chips = _other_chips(x, y)
        sib = (x, y, 1 - c)

        def remote(src, dst, ssem, rsem, to):
            return pltpu.make_async_remote_copy(src_ref=src, dst_ref=dst, send_sem=ssem, recv_sem=rsem, device_id=to,
                                                device_id_type=MESH)

        stage = [pltpu.make_async_copy(ins[t], vbufs[t], loc_sems.at[2 * t]) for t in range(nb)]
        stage.append(pltpu.make_async_copy(s_ref, vbufs[nb], loc_sems.at[2 * nb]))
        for cp in stage:
            cp.start()
        sends = []
        for t, (_, src, dst) in enumerate(descs):
            for j, (px, py) in enumerate(chips):
                sends.append(remote(src(ins[t], c), dst(outs[t], k, c), ici_send.at[3 * t + j], ici_recv.at[3 * t + j], (px, py, c)))
        for j, (px, py) in enumerate(chips):
            sends.append(remote(s_ref, os_ref.at[k], ici_send.at[3 * nb + j], ici_recv.at[3 * nb + j], (px, py, c)))
        for cp in sends:
            cp.start()
        for cp in stage:
            cp.wait()
        local = []
        for t, (_, src, dst) in enumerate(descs):
            for cc in (0, 1):
                local.append(pltpu.make_async_copy(src(vbufs[t], cc), dst(outs[t], k, cc), loc_sems.at[2 * t + cc]))
        local.append(pltpu.make_async_copy(vbufs[nb], os_ref.at[k], loc_sems.at[2 * nb]))
        for cp in local:
            cp.start()
        for t, (_, src, dst) in enumerate(descs):
            for j, (px, py) in enumerate(chips):
                got = dst(outs[t], 2 * px + py, c)
                remote(got, got, ici_send.at[3 * t + j], ici_recv.at[3 * t + j], (px, py, c)).wait_recv()
                fwd = remote(got, got, d2d_send.at[3 * t + j], d2d_recv.at[3 * t + j], sib)
                fwd.start()
                sends.append(fwd)
        for j, (px, py) in enumerate(chips):
            remote(s_ref, os_ref.at[2 * px + py], ici_send.at[3 * nb + j], ici_recv.at[3 * nb + j], (px, py, c)).wait_recv()
        for t, (_, src, dst) in enumerate(descs):
            for j, (px, py) in enumerate(chips):
                theirs = dst(outs[t], 2 * px + py, 1 - c)
                remote(theirs, theirs, d2d_send.at[3 * t + j], d2d_recv.at[3 * t + j], sib).wait_recv()
        for cp in sends:
            cp.wait_send()
        for cp in local:
            cp.wait()

    out_shape = [jax.ShapeDtypeStruct(d[0], BF16) for d in descs] + [jax.ShapeDtypeStruct((N_CHIPS, rs, LANES), small.dtype)]
    return pl.pallas_call(
        body, name="gather_weights", in_specs=[ANY] * (nb + 1), out_specs=[ANY] * (nb + 1), out_shape=out_shape,
        scratch_shapes=[pltpu.SemaphoreType.DMA((3 * nb + 3,)), pltpu.SemaphoreType.DMA((3 * nb + 3,)),
                        pltpu.SemaphoreType.DMA((3 * nb,)), pltpu.SemaphoreType.DMA((3 * nb,)),
                        pltpu.SemaphoreType.DMA((2 * nb + 1,))]
        + [pltpu.VMEM(b.shape, b.dtype) for b in big] + [pltpu.VMEM(small.shape, small.dtype)],
        compiler_params=pltpu.CompilerParams(vmem_limit_bytes=VMEM_LIMIT),
    )(*big, small)


def _place_own(names, big):
    nb = len(big)
    descs = [GATHER_BIG[n] for n in names]

    def body(*refs):
        ins, outs = refs[:nb], refs[nb:2 * nb]
        sems, vbufs = refs[2 * nb], refs[2 * nb + 1:]
        x, y, c = _mesh_pos()
        k = 2 * x + y
        stage = [pltpu.make_async_copy(ins[t], vbufs[t], sems.at[2 * t]) for t in range(nb)]
        for cp in stage:
            cp.start()
        for cp in stage:
            cp.wait()
        local = [pltpu.make_async_copy(src(vbufs[t], cc), dst(outs[t], k, cc), sems.at[2 * t + cc])
                 for t, (_, src, dst) in enumerate(descs) for cc in (0, 1)]
        for cp in local:
            cp.start()
        for cp in local:
            cp.wait()

    return pl.pallas_call(
        body, name="place_own", in_specs=[ANY] * nb, out_specs=[ANY] * nb,
        out_shape=[jax.ShapeDtypeStruct(d[0], BF16) for d in descs],
        scratch_shapes=[pltpu.SemaphoreType.DMA((2 * nb,))] + [pltpu.VMEM(b.shape, b.dtype) for b in big],
        compiler_params=pltpu.CompilerParams(vmem_limit_bytes=VMEM_LIMIT),
    )(*big)


def _gather_start(names, big, bufs):
    nb = len(big)
    descs = [GATHER_BIG[n] for n in names]

    def body(*refs):
        ins, lnd = refs[:nb], refs[nb:2 * nb]
        send_sems, recv_sems, token = refs[2 * nb], refs[2 * nb + 1], refs[-1]
        x, y, c = _mesh_pos()
        k = 2 * x + y
        for t, (_, src, dst) in enumerate(descs):
            for j, (px, py) in enumerate(_other_chips(x, y)):
                _remote(src(ins[t], c), dst(lnd[t], k, c), send_sems.at[3 * t + j], recv_sems.at[3 * t + j], (px, py, c)).start()
        token[...] = jnp.zeros_like(token)

    out = pl.pallas_call(
        body, name="gather_start",
        out_shape=(pltpu.SemaphoreType.DMA((3 * nb,)), pltpu.SemaphoreType.DMA((3 * nb,)),
                   *[pltpu.HBM(b.shape, b.dtype) for b in big], *[pltpu.HBM(b.shape, b.dtype) for b in bufs],
                   jax.ShapeDtypeStruct((SUBLANES, LANES), F32)),
        in_specs=[HBM] * (2 * nb), out_specs=(SEM, SEM, *[HBM] * (2 * nb), pl.BlockSpec(memory_space=pltpu.VMEM)),
        input_output_aliases={i: 2 + i for i in range(2 * nb)},
        compiler_params=pltpu.CompilerParams(has_side_effects=DATAFLOW),
    )(*[pltpu.with_memory_space_constraint(b, pltpu.HBM) for b in big], *[pltpu.with_memory_space_constraint(b, pltpu.HBM) for b in bufs])
    return out[0], out[1], list(out[2:2 + nb]), list(out[2 + nb:2 + 2 * nb]), out[-1]


def _gather_wait(names, send_sems, recv_sems, big, bufs, after):
    nb = len(big)
    descs = [GATHER_BIG[n] for n in names]

    def body(*refs):
        ins, lnd = refs[:nb], refs[nb:2 * nb]
        ssem, rsem = refs[2 * nb], refs[2 * nb + 1]
        x, y, c = _mesh_pos()
        for t, (_, src, dst) in enumerate(descs):
            for j, (px, py) in enumerate(_other_chips(x, y)):
                cp = _remote(src(ins[t], c), dst(lnd[t], 2 * px + py, c), ssem.at[3 * t + j], rsem.at[3 * t + j], (px, py, c))
                cp.wait_send()
                cp.wait_recv()

    out = pl.pallas_call(
        body, name="gather_wait",
        out_shape=(*[pltpu.HBM(b.shape, b.dtype) for b in big], *[pltpu.HBM(b.shape, b.dtype) for b in bufs]),
        in_specs=[HBM] * (2 * nb) + [SEM, SEM, ANY], out_specs=tuple([HBM] * (2 * nb)),
        input_output_aliases={i: i for i in range(2 * nb)},
        compiler_params=pltpu.CompilerParams(has_side_effects=DATAFLOW),
    )(*big, *bufs, send_sems, recv_sems, after)
    return list(out[nb:])


def _gather_forward(names, bufs):
    nb = len(bufs)
    descs = [GATHER_BIG[n] for n in names]

    def body(*refs):
        outs = refs[nb:2 * nb]
        send_sems, recv_sems = refs[2 * nb:]
        x, y, c = _mesh_pos()
        sib = (x, y, 1 - c)
        sends = []
        for t, (_, src, dst) in enumerate(descs):
            for j, (px, py) in enumerate(_other_chips(x, y)):
                got = dst(outs[t], 2 * px + py, c)
                sends.append(_remote(got, got, send_sems.at[3 * t + j], recv_sems.at[3 * t + j], sib))
        for cp in sends:
            cp.start()
        for t, (_, src, dst) in enumerate(descs):
            for j, (px, py) in enumerate(_other_chips(x, y)):
                theirs = dst(outs[t], 2 * px + py, 1 - c)
                _remote(theirs, theirs, send_sems.at[3 * t + j], recv_sems.at[3 * t + j], sib).wait_recv()
        for cp in sends:
            cp.wait_send()

    return pl.pallas_call(
        body, name="gather_forward", in_specs=[ANY] * nb, out_specs=[ANY] * nb,
        out_shape=[jax.ShapeDtypeStruct(b.shape, b.dtype) for b in bufs], input_output_aliases={t: t for t in range(nb)},
        scratch_shapes=[pltpu.SemaphoreType.DMA((3 * nb,)), pltpu.SemaphoreType.DMA((3 * nb,))],
    )(*bufs)


def _remote(src, dst, ssem, rsem, to):
    return pltpu.make_async_remote_copy(src_ref=src, dst_ref=dst, send_sem=ssem, recv_sem=rsem, device_id=to, device_id_type=MESH)


def _rs_send_sibling(gs, tag):
    n = len(gs)
    counts = [N_CHIPS if g.ndim == 4 else 1 for g in gs]
    ns = sum(counts)

    def body(*refs):
        ins, outs = refs[:n], refs[n:2 * n]
        send_sems, recv_sems = refs[2 * n:]
        x, y, c = _mesh_pos()
        cps, s = [], 0
        for t in range(n):
            if counts[t] == 1:
                cps.append(_remote(ins[t].at[1 - c], outs[t], send_sems.at[s], recv_sems.at[s], (x, y, 1 - c)))
                s += 1
            else:
                for k in range(N_CHIPS):
                    cps.append(_remote(ins[t].at[k, 1 - c], outs[t].at[k], send_sems.at[s], recv_sems.at[s], (x, y, 1 - c)))
                    s += 1
        for cp in cps:
            cp.start()
        for cp in cps:
            cp.wait()

    out_shape = [jax.ShapeDtypeStruct(g.shape[:1] + g.shape[2:] if g.ndim == 4 else g.shape[1:], g.dtype) for g in gs]
    return pl.pallas_call(
        body, name=f"rs_send_sibling_{tag}", in_specs=[ANY] * n, out_specs=[ANY] * n, out_shape=out_shape,
        scratch_shapes=[pltpu.SemaphoreType.DMA((ns,)), pltpu.SemaphoreType.DMA((ns,))],
    )(*gs)


def _add_piece(g, recv, c, name):
    P, Q = g.shape[-2:]

    def body(c_ref, g_ref, r_ref, o_ref):
        o_ref[...] = g_ref[...].reshape(o_ref.shape) + r_ref[...]

    if g.ndim == 4:
        grid = (N_CHIPS,)
        in_specs = [pl.BlockSpec((1, 1, P, Q), lambda k, c_ref: (k, c_ref[0], 0, 0)), pl.BlockSpec((1, P, Q), lambda k, c_ref: (k, 0, 0))]
        out_spec = pl.BlockSpec((1, P, Q), lambda k, c_ref: (k, 0, 0))
    else:
        grid = (1,)
        in_specs = [pl.BlockSpec((1, P, Q), lambda k, c_ref: (c_ref[0], 0, 0)), pl.BlockSpec((P, Q), lambda k, c_ref: (0, 0))]
        out_spec = pl.BlockSpec((P, Q), lambda k, c_ref: (0, 0))
    return pl.pallas_call(
        body, name=name,
        grid_spec=pltpu.PrefetchScalarGridSpec(num_scalar_prefetch=1, grid=grid, in_specs=in_specs, out_specs=out_spec),
        out_shape=jax.ShapeDtypeStruct(recv.shape, g.dtype),
        compiler_params=_cp(1),
    )(c, g, recv)


HBM = pl.BlockSpec(memory_space=pltpu.HBM)
SEM = pl.BlockSpec(memory_space=pltpu.SEMAPHORE)
DATAFLOW = pltpu.SideEffectType.DATAFLOW_SIDE_EFFECTING


def _chips_start(hs, tag):
    n = len(hs)
    lands = [pltpu.with_memory_space_constraint(lax.empty((N_CHIPS,) + h.shape[-2:], h.dtype), pltpu.HBM) for h in hs]

    def body(*refs):
        ins, lnd = refs[:n], refs[n:2 * n]
        send_sems, recv_sems, token = refs[2 * n], refs[2 * n + 1], refs[-1]
        x, y, c = _mesh_pos()
        k = 2 * x + y
        piece = lambda t, kk: ins[t].at[kk] if hs[t].ndim == 3 else ins[t]
        for t in range(n):
            for j, (px, py) in enumerate(_other_chips(x, y)):
                _remote(piece(t, 2 * px + py), lnd[t].at[k], send_sems.at[3 * t + j], recv_sems.at[3 * t + j], (px, py, c)).start()
        token[...] = jnp.zeros_like(token)

    out = pl.pallas_call(
        body, name=f"chips_start_{tag}",
        out_shape=(pltpu.SemaphoreType.DMA((3 * n,)), pltpu.SemaphoreType.DMA((3 * n,)),
                   *[pltpu.HBM(h.shape, h.dtype) for h in hs], *[pltpu.HBM(l.shape, l.dtype) for l in lands],
                   jax.ShapeDtypeStruct((SUBLANES, LANES), F32)),
        in_specs=[HBM] * (2 * n), out_specs=(SEM, SEM, *[HBM] * (2 * n), pl.BlockSpec(memory_space=pltpu.VMEM)),
        input_output_aliases={i: 2 + i for i in range(2 * n)},
        compiler_params=pltpu.CompilerParams(has_side_effects=DATAFLOW),
    )(*[pltpu.with_memory_space_constraint(h, pltpu.HBM) for h in hs], *lands)
    return out[0], out[1], list(out[2:2 + n]), list(out[2 + n:2 + 2 * n]), out[-1]


def _chips_wait(send_sems, recv_sems, hs, lands, after, tag):
    n = len(hs)

    def body(*refs):
        ins, lnd = refs[:n], refs[n:2 * n]
        ssem, rsem = refs[2 * n], refs[2 * n + 1]
        x, y, c = _mesh_pos()
        k = 2 * x + y
        piece = lambda t, kk: ins[t].at[kk] if hs[t].ndim == 3 else ins[t]
        for t in range(n):
            for j, (px, py) in enumerate(_other_chips(x, y)):
                cp = _remote(piece(t, k), lnd[t].at[2 * px + py], ssem.at[3 * t + j], rsem.at[3 * t + j], (px, py, c))
                cp.wait_send()
                cp.wait_recv()

    out = pl.pallas_call(
        body, name=f"chips_wait_{tag}",
        out_shape=(*[pltpu.HBM(h.shape, h.dtype) for h in hs], *[pltpu.HBM(l.shape, l.dtype) for l in lands]),
        in_specs=[HBM] * (2 * n) + [SEM, SEM, ANY], out_specs=tuple([HBM] * (2 * n)),
        input_output_aliases={i: i for i in range(2 * n)},
        compiler_params=pltpu.CompilerParams(has_side_effects=DATAFLOW),
    )(*hs, *lands, send_sems, recv_sems, after)
    return list(out[:n]), list(out[n:])


def _add_chips(p, own, kc, name):
    _, P, Q = p.shape
    tr = P
    while N_CHIPS * tr * Q * 4 > 6 * 1024 * 1024 and tr % 16 == 0:
        tr //= 2
    sharded = own.ndim == 3

    def body(kc_ref, p_ref, own_ref, o_ref):
        k = kc_ref[0]
        mine = own_ref[...].reshape(tr, Q)
        v = [jnp.where(k == j, mine, p_ref[j]) for j in range(N_CHIPS)]
        o_ref[0] = ((v[0] + v[1]) + v[2]) + v[3]

    own_spec = (pl.BlockSpec((1, tr, Q), lambda i, kc_ref: (kc_ref[0], i, 0)) if sharded
                else pl.BlockSpec((tr, Q), lambda i, kc_ref: (i, 0)))
    return pl.pallas_call(
        body, name=name,
        grid_spec=pltpu.PrefetchScalarGridSpec(
            num_scalar_prefetch=1, grid=(P // tr,),
            in_specs=[pl.BlockSpec((N_CHIPS, tr, Q), lambda i, kc_ref: (0, i, 0)), own_spec],
            out_specs=pl.BlockSpec((1, tr, Q), lambda i, kc_ref: (kc_ref[1], i, 0))),
        out_shape=jax.ShapeDtypeStruct((2, P, Q), p.dtype),
        compiler_params=_cp(1),
    )(kc, p, own)


def _rs_share(fs, tag):
    n = len(fs)

    def body(*refs):
        outs = refs[n:2 * n]
        send_sems, recv_sems = refs[2 * n:]
        x, y, c = _mesh_pos()
        sends = [_remote(outs[t].at[c], outs[t].at[c], send_sems.at[t], recv_sems.at[t], (x, y, 1 - c)) for t in range(n)]
        for cp in sends:
            cp.start()
        for t in range(n):
            _remote(outs[t].at[c], outs[t].at[1 - c], send_sems.at[t], recv_sems.at[t], (x, y, 1 - c)).wait_recv()
        for cp in sends:
            cp.wait_send()

    return pl.pallas_call(
        body, name=f"rs_share_{tag}", in_specs=[ANY] * n, out_specs=[ANY] * n,
        out_shape=[jax.ShapeDtypeStruct(f.shape, f.dtype) for f in fs], input_output_aliases={t: t for t in range(n)},
        scratch_shapes=[pltpu.SemaphoreType.DMA((n,)), pltpu.SemaphoreType.DMA((n,))],
    )(*fs)


def _reduce_start(gs, kc, tag):
    from_sibling = _rs_send_sibling(gs, tag)
    chip_sums = [_add_piece(g, r, kc[1:], name=f"add_piece_{tag}_{t}") for t, (g, r) in enumerate(zip(gs, from_sibling))]
    send_sems, recv_sems, chip_sums, lands, token = _chips_start(chip_sums, tag)
    return (send_sems, recv_sems, chip_sums, lands, tag), token


def _reduce_finish(states, kc, after):
    mine = []
    for send_sems, recv_sems, chip_sums, lands, tag in states:
        chip_sums, from_chips = _chips_wait(send_sems, recv_sems, chip_sums, lands, after, tag)
        mine += [_add_chips(p, h, kc, name=f"add_chips_{tag}_{t}") for t, (p, h) in enumerate(zip(from_chips, chip_sums))]
    return _rs_share(mine, "all")


def _adamw(w, g, m, v, name):
    R, C = w.shape
    tr = R
    for cand in (512, 256, 128, 64, 32, 16, 8):
        if R % cand == 0 and cand * C * 4 <= 2 * 1024 * 1024:
            tr = cand
            break
    c1 = 1.0 / (1.0 - ADAM_B1 ** ADAM_STEP)
    c2 = 1.0 / (1.0 - ADAM_B2 ** ADAM_STEP)

    def body(w_ref, g_ref, m_ref, v_ref, d_ref, mo_ref, vo_ref):
        gv = g_ref[...]
        mn = ADAM_B1 * m_ref[...] + (1.0 - ADAM_B1) * gv
        vn = ADAM_B2 * v_ref[...] + (1.0 - ADAM_B2) * (gv * gv)
        mo_ref[...] = mn
        vo_ref[...] = vn
        d_ref[...] = -ADAM_LR * ((mn * c1) / (jnp.sqrt(vn * c2) + ADAM_EPS) + ADAM_WD * w_ref[...])

    spec = pl.BlockSpec((tr, C), lambda i: (i, 0))
    shp = jax.ShapeDtypeStruct((R, C), F32)
    return pl.pallas_call(body, name=name, grid=(R // tr,), in_specs=[spec] * 4, out_specs=[spec] * 3, out_shape=[shp] * 3,
                          compiler_params=_cp(1))(w, g, m, v)


def _step(a):
    x, y, c = _mesh_pos()
    kc = jnp.stack([2 * x + y, c]).astype(jnp.int32)

    rs = _rows(sum(a[n].size for n in SH_SMALL))
    first, later = ["ev_w_in", "ev_w_out"], ["od_w_in", "od_w_out", "ffn_w_up", "ffn_w_down"]
    lead = lambda w: w if w.ndim == 3 else w[None]
    *full, gs = _gather_weights(first, [a[n].astype(BF16) for n in first], _pack([a[n] for n in SH_SMALL], rs, F32))
    p = {n: a[n] for n in REP}
    p.update({n: lead(w) for n, w in zip(first, full)})
    parts = [_unpack(gs[k], [a[n].shape for n in SH_SMALL]) for k in range(N_CHIPS)]
    for i, n in enumerate(SH_SMALL):
        p[n] = jnp.concatenate([parts[k][i] for k in range(N_CHIPS)], axis=SH_SMALL[n])
    shards = lax.optimization_barrier(([a[n].astype(BF16) for n in later], full))[0]
    g_send, g_recv, shards, bufs, token = _gather_start(later, shards, _place_own(later, shards))
    p["norm_mix"] = p["norm_mix"] + token[0:1, 0:1]

    def late(after):
        got = _gather_forward(later, _gather_wait(later, g_send, g_recv, shards, bufs, after))
        return {n: lead(w) for n, w in zip(later, got)}

    p["late"] = late

    states = []

    def start_reduce(tag, gs):
        state, token = _reduce_start(gs, kc, tag)
        states.append(state)
        return token

    sq8, grad_x, grads, big = _local_step(a["x"][0], a["loss_target"][0], p, start_reduce)
    loss = lax.psum(0.5 / D_MODEL * jnp.sum(sq8), ("x", "y", "c"))

    r_s = _rows(sum(a[n].size for n in SH_SMALL), 2 * SUBLANES) // 2
    small_pieces = []
    for k in range(N_CHIPS):
        pieces = [lax.slice_in_dim(grads[n], k * a[n].shape[ax], (k + 1) * a[n].shape[ax], axis=ax) for n, ax in SH_SMALL.items()]
        small_pieces.append(_pack(pieces, 2 * r_s, F32).reshape(2, r_s, LANES))
    g_small = jnp.stack(small_pieces)
    r_r = _rows(sum(a[n].size for n in REP), 2 * SUBLANES) // 2
    g_rep = _pack([grads[n] for n in REP], 2 * r_r, F32).reshape(2, r_r, LANES)
    start_reduce("g4", [g_small, g_rep])
    reduced = _reduce_finish(states, kc, grad_x)
    red = dict(zip([key for tag in ("g1", "g2", "g3") for key in REDUCE_GROUPS[tag]], reduced))
    gfin = {}
    for n in ("ev_w_in", "ev_w_out", "od_w_in", "od_w_out"):
        gfin[n] = red[n, 0].reshape(a[n].shape)
    for n in ("ffn_w_up", "ffn_w_down"):
        gfin[n] = jnp.stack([red[n, l].reshape(a[n].shape[1:]) for l in range(2)])
    gfin.update(zip(SH_SMALL, _unpack(reduced[-2], [a[n].shape for n in SH_SMALL])))
    gfin.update(zip(REP, _unpack(reduced[-1], [a[n].shape for n in REP])))

    out = {"loss": loss, "grad_x": grad_x[None]}
    small_names = list(SH_SMALL) + REP
    for n in SH_BIG:
        shp = a[n].shape
        two_d = lambda t: t.reshape(-1, shp[-1])
        d, mo, vo = _adamw(two_d(a[n]), two_d(gfin[n]), two_d(a["m_" + n]), two_d(a["v_" + n]), name=f"adamw_{n}")
        out["delta_" + n], out["new_m_" + n], out["new_v_" + n] = d.reshape(shp), mo.reshape(shp), vo.reshape(shp)
    r_small = _rows(sum(a[n].size for n in small_names), 512)
    packs = [_pack([src(n) for n in small_names], r_small, F32)
             for src in (lambda n: a[n], lambda n: gfin[n], lambda n: a["m_" + n], lambda n: a["v_" + n])]
    d, mo, vo = _adamw(*packs, name="adamw_small")
    shapes = [a[n].shape for n in small_names]
    for n, dd, mm, vv in zip(small_names, _unpack(d, shapes), _unpack(mo, shapes), _unpack(vo, shapes)):
        out["delta_" + n], out["new_m_" + n], out["new_v_" + n] = dd, mm, vv
    for n in WEIGHTS:
        out["grad_" + n] = gfin[n]
    return out


def kernel(x, norm_mix, norm_ffn, norm_final, ev_w_in, ev_conv_w, ev_conv_b, ev_gate_a_w, ev_gate_a_b, ev_gate_x_w, ev_gate_x_b, ev_lru_lambda, hg_lb_logits, ev_hg_norm, ev_w_out, od_w_in, od_b_in, od_ln_g, od_ln_b, od_w_s, od_b_s, od_w_out, ffn_w_up, ffn_conv_w, ffn_conv_b, ffn_w_down, loss_target, m_norm_mix, m_norm_ffn, m_norm_final, m_ev_w_in, m_ev_conv_w, m_ev_conv_b, m_ev_gate_a_w, m_ev_gate_a_b, m_ev_gate_x_w, m_ev_gate_x_b, m_ev_lru_lambda, m_hg_lb_logits, m_ev_hg_norm, m_ev_w_out, m_od_w_in, m_od_b_in, m_od_ln_g, m_od_ln_b, m_od_w_s, m_od_b_s, m_od_w_out, m_ffn_w_up, m_ffn_conv_w, m_ffn_conv_b, m_ffn_w_down, v_norm_mix, v_norm_ffn, v_norm_final, v_ev_w_in, v_ev_conv_w, v_ev_conv_b, v_ev_gate_a_w, v_ev_gate_a_b, v_ev_gate_x_w, v_ev_gate_x_b, v_ev_lru_lambda, v_hg_lb_logits, v_ev_hg_norm, v_ev_w_out, v_od_w_in, v_od_b_in, v_od_ln_g, v_od_ln_b, v_od_w_s, v_od_b_s, v_od_w_out, v_ffn_w_up, v_ffn_conv_w, v_ffn_conv_b, v_ffn_w_down):
    vals = (x, norm_mix, norm_ffn, norm_final, ev_w_in, ev_conv_w, ev_conv_b, ev_gate_a_w, ev_gate_a_b, ev_gate_x_w, ev_gate_x_b, ev_lru_lambda, hg_lb_logits, ev_hg_norm, ev_w_out, od_w_in, od_b_in, od_ln_g, od_ln_b, od_w_s, od_b_s, od_w_out, ffn_w_up, ffn_conv_w, ffn_conv_b, ffn_w_down, loss_target, m_norm_mix, m_norm_ffn, m_norm_final, m_ev_w_in, m_ev_conv_w, m_ev_conv_b, m_ev_gate_a_w, m_ev_gate_a_b, m_ev_gate_x_w, m_ev_gate_x_b, m_ev_lru_lambda, m_hg_lb_logits, m_ev_hg_norm, m_ev_w_out, m_od_w_in, m_od_b_in, m_od_ln_g, m_od_ln_b, m_od_w_s, m_od_b_s, m_od_w_out, m_ffn_w_up, m_ffn_conv_w, m_ffn_conv_b, m_ffn_w_down, v_norm_mix, v_norm_ffn, v_norm_final, v_ev_w_in, v_ev_conv_w, v_ev_conv_b, v_ev_gate_a_w, v_ev_gate_a_b, v_ev_gate_x_w, v_ev_gate_x_b, v_ev_lru_lambda, v_hg_lb_logits, v_ev_hg_norm, v_ev_w_out, v_od_w_in, v_od_b_in, v_od_ln_g, v_od_ln_b, v_od_w_s, v_od_b_s, v_od_w_out, v_ffn_w_up, v_ffn_conv_w, v_ffn_conv_b, v_ffn_w_down)
    names = ["x"] + WEIGHTS + ["loss_target"] + ["m_" + n for n in WEIGHTS] + ["v_" + n for n in WEIGHTS]
    out = _step(dict(zip(names, vals)))
    return (out["loss"], out["grad_x"], *[out["grad_" + n] for n in WEIGHTS], *[out["delta_" + n] for n in WEIGHTS],
            *[out["new_m_" + n] for n in WEIGHTS], *[out["new_v_" + n] for n in WEIGHTS])
```

```python
import functools

import jax
import jax.numpy as jnp
from jax import lax
from jax.experimental import pallas as pl
from jax.experimental.pallas import tpu as pltpu

F32 = jnp.float32
BF16 = jnp.bfloat16

EPS = 1e-6
D_MODEL = 1024
LRU_W = 512
LRU_BLOCKS = 8
LRU_C = 8.0
HG_HEADS = 4
HG_D = 128
HG_CHUNK = 64
SGU_G = 8
SGU_CHUNK = 128
D_FF = 2816
ADAM_LR, ADAM_B1, ADAM_B2, ADAM_EPS, ADAM_WD, ADAM_STEP = 0.001, 0.9, 0.999, 1e-08, 0.01, 10

V7X_VMEM_BYTES = 64 * 1024 * 1024
VMEM_LIMIT = V7X_VMEM_BYTES - 8 * 1024 * 1024
SUBLANES = 8
LANES = 128
BF16_ROWS = 16

GELU_C0 = 0.7978845608028654
GELU_C1 = 0.044715

NN = (((1,), (0,)), ((), ()))
NT = (((1,), (1,)), ((), ()))
TN = (((0,), (0,)), ((), ()))


def _dot(a, b, dims=NN):
    return lax.dot_general(a.astype(BF16), b.astype(BF16), dims, preferred_element_type=F32)


def _cp(n_grid):
    return pltpu.CompilerParams(dimension_semantics=("arbitrary",) * n_grid, vmem_limit_bytes=VMEM_LIMIT)


def _chunk(n, cap):
    best = LANES
    for c in range(LANES, cap + 1, LANES):
        if n % c == 0:
            best = c
    return best


def _resident(shape):
    nd = len(shape)
    return pl.BlockSpec(shape, lambda *_: (0,) * nd, pipeline_mode=pl.Buffered(1))


def _rsum8(x):
    r, c = x.shape
    return x.reshape(r // SUBLANES, SUBLANES, c).sum(axis=0)


def _sigmoid(x):
    return 0.5 * jnp.tanh(0.5 * x) + 0.5


def _gelu(x):
    return 0.5 * x * (1.0 + jnp.tanh(GELU_C0 * (x + GELU_C1 * x * x * x)))


def _gelu_grad(x):
    t = jnp.tanh(GELU_C0 * (x + GELU_C1 * x * x * x))
    return 0.5 * (1.0 + t) + 0.5 * x * (1.0 - t * t) * GELU_C0 * (1.0 + 3.0 * GELU_C1 * x * x)


def _silu_and_grad(x):
    s = _sigmoid(x)
    return x * s, s * (1.0 + x * (1.0 - s))


def _shift_rows(e, j):
    n = e.shape[0]
    return e if j % n == 0 else pltpu.roll(e, j % n, 0)


def _norm_mm(h, g, w, b, name, tt=1024):
    T, D = h.shape
    N = w.shape[1]
    cn = _chunk(N, 512)

    def body(h_ref, g_ref, w_ref, b_ref, hn_ref, z_ref):
        x = h_ref[...]
        r = lax.rsqrt(jnp.mean(x * x, axis=-1, keepdims=True) + EPS)
        hn = (x * r * g_ref[...]).astype(BF16)
        hn_ref[...] = hn
        for j in range(0, N, cn):
            acc = jnp.dot(hn, w_ref[:, j:j + cn], preferred_element_type=F32) + b_ref[:, j:j + cn]
            z_ref[:, j:j + cn] = acc.astype(BF16)

    return pl.pallas_call(
        body, name=name, grid=(T // tt,),
        in_specs=[pl.BlockSpec((tt, D), lambda i: (i, 0)), _resident((1, D)), _resident((D, N)), _resident((1, N))],
        out_specs=[pl.BlockSpec((tt, D), lambda i: (i, 0)), pl.BlockSpec((tt, N), lambda i: (i, 0))],
        out_shape=[jax.ShapeDtypeStruct((T, D), BF16), jax.ShapeDtypeStruct((T, N), BF16)],
        compiler_params=_cp(1),
    )(h, g, w, b)


def _mm(a, w, res, out_dtype, name, tt=1024, transpose_w=False):
    T, K = a.shape
    N = w.shape[0] if transpose_w else w.shape[1]
    cn = _chunk(N, 512)
    has_res = res is not None

    def body(*refs):
        a_ref, w_ref = refs[0], refs[1]
        res_ref = refs[2] if has_res else None
        o_ref = refs[-1]
        av = a_ref[...].astype(BF16)
        for j in range(0, N, cn):
            if transpose_w:
                acc = lax.dot_general(av, w_ref[j:j + cn, :], NT, preferred_element_type=F32)
            else:
                acc = jnp.dot(av, w_ref[:, j:j + cn], preferred_element_type=F32)
            if has_res:
                acc = acc + res_ref[:, j:j + cn]
            o_ref[:, j:j + cn] = acc.astype(out_dtype)

    in_specs = [pl.BlockSpec((tt, K), lambda i: (i, 0)), _resident(w.shape)]
    args = [a, w]
    if has_res:
        in_specs.append(pl.BlockSpec((tt, N), lambda i: (i, 0)))
        args.append(res)
    return pl.pallas_call(
        body, name=name, grid=(T // tt,), in_specs=in_specs,
        out_specs=pl.BlockSpec((tt, N), lambda i: (i, 0)),
        out_shape=jax.ShapeDtypeStruct((T, N), out_dtype),
        compiler_params=_cp(1),
    )(*args)


def _mm_tn(a, b, name, col_shards=1, tt=2048):
    T, K = a.shape
    N = b.shape[1]
    ns = N // col_shards
    tt = min(tt, T)
    while 2 * (tt * K * a.dtype.itemsize + tt * ns * b.dtype.itemsize + K * ns * 4) > VMEM_LIMIT - 12 * 1024 * 1024:
        tt //= 2

    def body(a_ref, b_ref, o_ref):
        acc = lax.dot_general(a_ref[...].astype(BF16), b_ref[...].astype(BF16), TN, preferred_element_type=F32)
        prev = jnp.where(pl.program_id(1) == 0, 0.0, o_ref[0])
        o_ref[0] = prev + acc

    out = pl.pallas_call(
        body, name=name, grid=(col_shards, T // tt),
        in_specs=[pl.BlockSpec((tt, K), lambda n, t: (t, 0)), pl.BlockSpec((tt, ns), lambda n, t: (t, n))],
        out_specs=pl.BlockSpec((1, K, ns), lambda n, t: (n, 0, 0)),
        out_shape=jax.ShapeDtypeStruct((col_shards, K, ns), F32),
        compiler_params=_cp(2),
    )(a, b)
    return out if col_shards > 1 else out[0]


def _mm_normbwd(dz, w, x, g, dres, name):
    T, N = dz.shape
    D = w.shape[0]
    tt = 1024 if N <= 3072 else 512

    def body(dz_ref, wt_ref, x_ref, g_ref, dres_ref, dx_ref, dg_ref):
        @pl.when(pl.program_id(0) == 0)
        def _():
            dg_ref[...] = jnp.zeros_like(dg_ref)

        dy = lax.dot_general(dz_ref[...], wt_ref[...], NT, preferred_element_type=F32)
        x = x_ref[...]
        r = lax.rsqrt(jnp.mean(x * x, axis=-1, keepdims=True) + EPS)
        xn = x * r
        dg_ref[...] += _rsum8(dy * xn)
        dxn = dy * g_ref[...]
        dx_ref[...] = dres_ref[...] + r * (dxn - xn * jnp.mean(dxn * xn, axis=-1, keepdims=True))

    return pl.pallas_call(
        body, name=name, grid=(T // tt,),
        in_specs=[pl.BlockSpec((tt, N), lambda i: (i, 0)), _resident((D, N)), pl.BlockSpec((tt, D), lambda i: (i, 0)),
                  _resident((1, D)), pl.BlockSpec((tt, D), lambda i: (i, 0))],
        out_specs=[pl.BlockSpec((tt, D), lambda i: (i, 0)), pl.BlockSpec((SUBLANES, D), lambda i: (0, 0))],
        out_shape=[jax.ShapeDtypeStruct((T, D), F32), jax.ShapeDtypeStruct((SUBLANES, D), F32)],
        compiler_params=_cp(1),
    )(dz, w, x, g, dres)


def _final_loss(h, g, tgt, name="final_loss", tt=512):
    T, D = h.shape

    def body(h_ref, g_ref, t_ref, dh_ref, dg_ref, sq_ref):
        @pl.when(pl.program_id(0) == 0)
        def _():
            dg_ref[...] = jnp.zeros_like(dg_ref)
            sq_ref[...] = jnp.zeros_like(sq_ref)

        x = h_ref[...]
        r = lax.rsqrt(jnp.mean(x * x, axis=-1, keepdims=True) + EPS)
        xn = x * r
        gv = g_ref[...]
        diff = xn * gv - t_ref[...]
        sq_ref[...] += _rsum8(diff * diff)
        dy = diff * (1.0 / D)
        dg_ref[...] += _rsum8(dy * xn)
        dxn = dy * gv
        dh_ref[...] = r * (dxn - xn * jnp.mean(dxn * xn, axis=-1, keepdims=True))

    return pl.pallas_call(
        body, name=name, grid=(T // tt,),
        in_specs=[pl.BlockSpec((tt, D), lambda i: (i, 0)), _resident((1, D)), pl.BlockSpec((tt, D), lambda i: (i, 0))],
        out_specs=[pl.BlockSpec((tt, D), lambda i: (i, 0)), pl.BlockSpec((SUBLANES, D), lambda i: (0, 0)),
                   pl.BlockSpec((SUBLANES, D), lambda i: (0, 0))],
        out_shape=[jax.ShapeDtypeStruct((T, D), F32), jax.ShapeDtypeStruct((SUBLANES, D), F32),
                   jax.ShapeDtypeStruct((SUBLANES, D), F32)],
        compiler_params=_cp(1),
    )(h, g, tgt)


def _col_groups(F, cc, per_group=4):
    step = cc * per_group
    return [(g0, min(g0 + step, F)) for g0 in range(0, F, step)]


def _ffn_act(gu, cw, cb, w_down, res, name, tt=512):
    T = gu.shape[0]
    F = gu.shape[1] // 2
    D = w_down.shape[1]
    cc = _chunk(F, 256)
    hb = tt // BF16_ROWS

    def body(gu_ref, halo_ref, cw_ref, cb_ref, wd_ref, res_ref, a_ref, o_ref):
        first = pl.program_id(0) == 0
        acc = res_ref[...]
        for g0, g1 in _col_groups(F, cc):
            for c0 in range(g0, g1, cc):
                cs = slice(c0, c0 + cc)
                x = gu_ref[:, cs].astype(F32)
                halo = jnp.where(first, 0.0, halo_ref[:, cs].astype(F32))
                e = jnp.concatenate([halo, x], axis=0)
                gc = (cb_ref[:, cs] + cw_ref[0:1, cs] * _shift_rows(e, 2)[BF16_ROWS:] + cw_ref[1:2, cs] * _shift_rows(e, 1)[BF16_ROWS:]
                      + cw_ref[2:3, cs] * x)
                up = gu_ref[:, F + c0:F + c0 + cc].astype(F32)
                a_ref[:, cs] = (gc * _sigmoid(gc) * up).astype(BF16)
            acc = acc + jnp.dot(a_ref[:, g0:g1], wd_ref[g0:g1, :], preferred_element_type=F32)
        o_ref[...] = acc

    return pl.pallas_call(
        body, name=name, grid=(T // tt,),
        in_specs=[pl.BlockSpec((tt, 2 * F), lambda i: (i, 0)),
                  pl.BlockSpec((BF16_ROWS, F), lambda i: (jnp.maximum(i * hb - 1, 0), 0)),
                  _resident((SUBLANES, F)), _resident((1, F)), _resident((F, D)), pl.BlockSpec((tt, D), lambda i: (i, 0))],
        out_specs=[pl.BlockSpec((tt, F), lambda i: (i, 0)), pl.BlockSpec((tt, D), lambda i: (i, 0))],
        out_shape=[jax.ShapeDtypeStruct((T, F), BF16), jax.ShapeDtypeStruct((T, D), F32)],
        compiler_params=_cp(1),
    )(gu, gu, cw, cb, w_down, res)


def _ffn_act_bwd(gu, da, cw, cb, w_up, x, g, dres, name, tt=256):
    T = gu.shape[0]
    F = gu.shape[1] // 2
    D = w_up.shape[0]
    cc = _chunk(F, 256)
    hb = tt // BF16_ROWS
    last_hb = T // BF16_ROWS - 1
    nt = T // tt

    def body(gu_ref, gprev_ref, gunext_ref, da_ref, danext_ref, cw_ref, cb_ref, wu_ref, x_ref, g_ref, dres_ref,
             dgu_ref, dc_ref, dx_ref, dg_ref):
        i = pl.program_id(0)

        @pl.when(i == 0)
        def _():
            dc_ref[...] = jnp.zeros_like(dc_ref)
            dg_ref[...] = jnp.zeros_like(dg_ref)

        n = tt + BF16_ROWS
        dy = jnp.zeros((tt, D), F32)
        for lo, hi in _col_groups(F, cc):
            for c0 in range(lo, hi, cc):
                cs = slice(c0, c0 + cc)
                us = slice(F + c0, F + c0 + cc)
                gt = gu_ref[:, cs].astype(F32)
                gp = jnp.where(i == 0, 0.0, gprev_ref[:, cs].astype(F32))
                ge = jnp.concatenate([gp, gt, gunext_ref[:, cs].astype(F32)], axis=0)
                g1 = _shift_rows(ge, 1)[BF16_ROWS:]
                g2 = _shift_rows(ge, 2)[BF16_ROWS:]
                gc = cb_ref[:, cs] + cw_ref[0:1, cs] * g2 + cw_ref[1:2, cs] * g1 + cw_ref[2:3, cs] * ge[BF16_ROWS:]
                up = jnp.concatenate([gu_ref[:, us].astype(F32), gunext_ref[:, us].astype(F32)], axis=0)
                dan = jnp.where(i == nt - 1, 0.0, danext_ref[:, cs].astype(F32))
                dae = jnp.concatenate([da_ref[:, cs].astype(F32), dan], axis=0)
                s, ds = _silu_and_grad(gc)
                dgc = dae * up * ds
                dgu_ref[:, us] = (dae * s)[:tt].astype(BF16)
                dgate = cw_ref[2:3, cs] * dgc + cw_ref[1:2, cs] * _shift_rows(dgc, n - 1) + cw_ref[0:1, cs] * _shift_rows(dgc, n - 2)
                dgu_ref[:, cs] = dgate[:tt].astype(BF16)
                dm = dgc[:tt]
                dc_ref[0, :, cs] += _rsum8(dm * g2[:tt])
                dc_ref[1, :, cs] += _rsum8(dm * g1[:tt])
                dc_ref[2, :, cs] += _rsum8(dm * gt)
                dc_ref[3, :, cs] += _rsum8(dm)
            dy = (dy + lax.dot_general(dgu_ref[:, lo:hi], wu_ref[:, lo:hi], NT, preferred_element_type=F32)
                  + lax.dot_general(dgu_ref[:, F + lo:F + hi], wu_ref[:, F + lo:F + hi], NT, preferred_element_type=F32))
        xv = x_ref[...]
        r = lax.rsqrt(jnp.mean(xv * xv, axis=-1, keepdims=True) + EPS)
        xn = xv * r
        dg_ref[...] += _rsum8(dy * xn)
        dxn = dy * g_ref[...]
        dx_ref[...] = dres_ref[...] + r * (dxn - xn * jnp.mean(dxn * xn, axis=-1, keepdims=True))

    tok = lambda w: pl.BlockSpec((tt, w), lambda i: (i, 0))
    return pl.pallas_call(
        body, name=name, grid=(nt,),
        in_specs=[tok(2 * F),
                  pl.BlockSpec((BF16_ROWS, F), lambda i: (jnp.maximum(i * hb - 1, 0), 0)),
                  pl.BlockSpec((BF16_ROWS, 2 * F), lambda i: (jnp.minimum((i + 1) * hb, last_hb), 0)),
                  tok(F),
                  pl.BlockSpec((BF16_ROWS, F), lambda i: (jnp.minimum((i + 1) * hb, last_hb), 0)),
                  _resident((SUBLANES, F)), _resident((1, F)), _resident((D, 2 * F)), tok(D), _resident((1, D)), tok(D)],
        out_specs=[tok(2 * F), pl.BlockSpec((4, SUBLANES, F), lambda i: (0, 0, 0)), tok(D),
                   pl.BlockSpec((SUBLANES, D), lambda i: (0, 0))],
        out_shape=[jax.ShapeDtypeStruct((T, 2 * F), BF16), jax.ShapeDtypeStruct((4, SUBLANES, F), F32),
                   jax.ShapeDtypeStruct((T, D), F32), jax.ShapeDtypeStruct((SUBLANES, D), F32)],
        compiler_params=_cp(1),
    )(gu, gu, gu, da, da, cw, cb, w_up, x, g, dres)


def _softplus_neg(lam):
    x = -lam
    y = jnp.exp(-jnp.abs(x))
    l1p = jnp.where(y < 0.01, y * (1.0 - y * (0.5 - y * (1.0 / 3.0))), jnp.log(1.0 + y))
    return jnp.maximum(x, 0.0) + l1p


def _lru_gates(xc, wa_ref, ba_ref, wx_ref, bx_ref, sp):
    xcb = xc.astype(BF16)
    r = _sigmoid(jnp.dot(xcb, wa_ref[...], preferred_element_type=F32) + ba_ref[...])
    gi = _sigmoid(jnp.dot(xcb, wx_ref[...], preferred_element_type=F32) + bx_ref[...])
    log_a = -LRU_C * r * sp
    a = jnp.exp(log_a)
    x2 = 2.0 * log_a
    series = -x2 * (1.0 + x2 * 0.5 * (1.0 + x2 * (1.0 / 3.0) * (1.0 + x2 * 0.25 * (1.0 + x2 * 0.2))))
    om = jnp.where(x2 > -0.125, series, 1.0 - a * a)
    return r, gi, a, jnp.sqrt(om)


def _lru_conv(xr, halo, cw_ref, cb_ref):
    e = jnp.concatenate([halo, xr], axis=0)
    x1 = _shift_rows(e, 1)[BF16_ROWS:]
    x2 = _shift_rows(e, 2)[BF16_ROWS:]
    x3 = _shift_rows(e, 3)[BF16_ROWS:]
    xc = cb_ref[...] + cw_ref[0:1, :] * x3 + cw_ref[1:2, :] * x2 + cw_ref[2:3, :] * x1 + cw_ref[3:4, :] * xr
    return xc, x1, x2, x3


def _lru_fwd(z, cw, cb, wa, ba, wx, bx, lam, name="lru_fwd", tt=256):
    T = z.shape[0]
    W = LRU_W
    hb = tt // BF16_ROWS
    ng = tt // SUBLANES

    def body(z_ref, halo_ref, cw_ref, cb_ref, wa_ref, ba_ref, wx_ref, bx_ref, lam_ref, oa_ref, h_ref, a_s, u_s, hc):
        i = pl.program_id(0)

        @pl.when(i == 0)
        def _():
            hc[...] = jnp.zeros_like(hc)

        xr = z_ref[:, W:2 * W].astype(F32)
        halo = jnp.where(i == 0, 0.0, halo_ref[...].astype(F32))
        xc, _, _, _ = _lru_conv(xr, halo, cw_ref, cb_ref)
        sp = _softplus_neg(lam_ref[...])
        r, gi, a, mult = _lru_gates(xc, wa_ref, ba_ref, wx_ref, bx_ref, sp)
        a_s[...] = a
        u_s[...] = mult * gi * xc
        row = lax.broadcasted_iota(jnp.int32, (SUBLANES, W), 0)

        def step(j, hprev):
            r0 = pl.multiple_of(j * SUBLANES, SUBLANES)
            A = a_s[pl.ds(r0, SUBLANES), :]
            U = u_s[pl.ds(r0, SUBLANES), :]
            for k in (1, 2, 4):
                m = row >= k
                U = jnp.where(m, A * pltpu.roll(U, k, 0) + U, U)
                A = jnp.where(m, A * pltpu.roll(A, k, 0), A)
            H = U + A * hprev
            h_ref[pl.ds(r0, SUBLANES), :] = H
            return jnp.broadcast_to(H[SUBLANES - 1:SUBLANES, :], (SUBLANES, W))

        hc[...] = lax.fori_loop(0, ng, step, hc[...])
        oa_ref[...] = (_gelu(z_ref[:, 0:W].astype(F32)) * h_ref[...]).astype(BF16)

    return pl.pallas_call(
        body, name=name, grid=(T // tt,),
        in_specs=[pl.BlockSpec((tt, 2 * W), lambda i: (i, 0)),
                  pl.BlockSpec((BF16_ROWS, W), lambda i: (jnp.maximum(i * hb - 1, 0), 1)),
                  _resident((SUBLANES, W)), _resident((1, W)), _resident((W, W)), _resident((1, W)),
                  _resident((W, W)), _resident((1, W)), _resident((1, W))],
        out_specs=[pl.BlockSpec((tt, W), lambda i: (i, 0)), pl.BlockSpec((tt, W), lambda i: (i, 0))],
        out_shape=[jax.ShapeDtypeStruct((T, W), BF16), jax.ShapeDtypeStruct((T, W), F32)],
        scratch_shapes=[pltpu.VMEM((tt, W), F32), pltpu.VMEM((tt, W), F32), pltpu.VMEM((SUBLANES, W), F32)],
        compiler_params=_cp(1),
    )(z, z, cw, cb, wa, ba, wx, bx, lam)


def _lru_bwd(z, hseq, dmix, cw, cb, wa, wat, ba, wx, wxt, bx, lam, name="lru_bwd", tt=256):
    T = z.shape[0]
    W = LRU_W
    nt = T // tt
    hb = tt // BF16_ROWS
    sb = tt // SUBLANES
    ng = tt // SUBLANES

    def body(z_ref, halo_ref, h_ref, hprev_ref, dm_ref, cw_ref, cb_ref, wa_ref, wat_ref, ba_ref, wx_ref, wxt_ref, bx_ref,
             lam_ref, dz_ref, dc_ref, dwa_ref, dwx_ref, dv_ref, c_s, d_s, g_s, gc, an, dxn):
        i = pl.program_id(0)
        ti = nt - 1 - i

        @pl.when(i == 0)
        def _():
            dc_ref[...] = jnp.zeros_like(dc_ref)
            dwa_ref[...] = jnp.zeros_like(dwa_ref)
            dwx_ref[...] = jnp.zeros_like(dwx_ref)
            dv_ref[...] = jnp.zeros_like(dv_ref)
            gc[...] = jnp.zeros_like(gc)
            an[...] = jnp.zeros_like(an)
            dxn[...] = jnp.zeros_like(dxn)

        xr = z_ref[:, W:2 * W].astype(F32)
        yg = z_ref[:, 0:W].astype(F32)
        halo = jnp.where(ti == 0, 0.0, halo_ref[...].astype(F32))
        xc, x1, x2, x3 = _lru_conv(xr, halo, cw_ref, cb_ref)
        sp = _softplus_neg(lam_ref[...])
        r, gi, a, mult = _lru_gates(xc, wa_ref, ba_ref, wx_ref, bx_ref, sp)
        h = h_ref[...]
        hp = jnp.where(ti == 0, 0.0, hprev_ref[...])
        hm1 = _shift_rows(jnp.concatenate([hp, h], axis=0), 1)[SUBLANES:]
        dout = dm_ref[...].astype(F32)
        d_s[...] = dout * _gelu(yg)
        dz_ref[:, 0:W] = (dout * h * _gelu_grad(yg)).astype(BF16)
        c_s[...] = _shift_rows(jnp.concatenate([a, an[...]], axis=0), tt + SUBLANES - 1)[:tt]
        an[...] = a[0:SUBLANES, :]
        row = lax.broadcasted_iota(jnp.int32, (SUBLANES, W), 0)

        def step(j, gnext):
            r0 = pl.multiple_of((ng - 1 - j) * SUBLANES, SUBLANES)
            C = c_s[pl.ds(r0, SUBLANES), :]
            G = d_s[pl.ds(r0, SUBLANES), :]
            for k in (1, 2, 4):
                m = row < SUBLANES - k
                G = jnp.where(m, G + C * pltpu.roll(G, SUBLANES - k, 0), G)
                C = jnp.where(m, C * pltpu.roll(C, SUBLANES - k, 0), C)
            G = G + C * gnext
            g_s[pl.ds(r0, SUBLANES), :] = G
            return jnp.broadcast_to(G[0:1, :], (SUBLANES, W))

        gc[...] = lax.fori_loop(0, ng, step, gc[...])
        du = g_s[...]
        da = du * hm1
        dgi = du * mult * xc
        dxc = du * mult * gi
        dmult = du * gi * xc
        dlog_a = da * a - dmult * (a * a) / mult
        dr = dlog_a * (-LRU_C * sp)
        dv_ref[2] += _rsum8(dlog_a * (-LRU_C * r))
        dpr = (dr * r * (1.0 - r)).astype(BF16)
        dpi = (dgi * gi * (1.0 - gi)).astype(BF16)
        dv_ref[0] += _rsum8(dpr.astype(F32))
        dv_ref[1] += _rsum8(dpi.astype(F32))
        xcb = xc.astype(BF16)
        dwa_ref[...] += lax.dot_general(xcb, dpr, TN, preferred_element_type=F32)
        dwx_ref[...] += lax.dot_general(xcb, dpi, TN, preferred_element_type=F32)
        dxc = dxc + jnp.dot(dpr, wat_ref[...], preferred_element_type=F32) + jnp.dot(dpi, wxt_ref[...], preferred_element_type=F32)
        n = tt + BF16_ROWS
        de = jnp.concatenate([dxc, dxn[...]], axis=0)
        dxr = (cw_ref[3:4, :] * dxc + cw_ref[2:3, :] * _shift_rows(de, n - 1)[:tt] + cw_ref[1:2, :] * _shift_rows(de, n - 2)[:tt]
               + cw_ref[0:1, :] * _shift_rows(de, n - 3)[:tt])
        dxn[...] = dxc[0:BF16_ROWS, :]
        dz_ref[:, W:2 * W] = dxr.astype(BF16)
        dc_ref[0] += _rsum8(dxc * x3)
        dc_ref[1] += _rsum8(dxc * x2)
        dc_ref[2] += _rsum8(dxc * x1)
        dc_ref[3] += _rsum8(dxc * xr)
        dc_ref[4] += _rsum8(dxc)

    rev = lambda i: nt - 1 - i
    return pl.pallas_call(
        body, name=name, grid=(nt,),
        in_specs=[pl.BlockSpec((tt, 2 * W), lambda i: (rev(i), 0)),
                  pl.BlockSpec((BF16_ROWS, W), lambda i: (jnp.maximum(rev(i) * hb - 1, 0), 1)),
                  pl.BlockSpec((tt, W), lambda i: (rev(i), 0)),
                  pl.BlockSpec((SUBLANES, W), lambda i: (jnp.maximum(rev(i) * sb - 1, 0), 0)),
                  pl.BlockSpec((tt, W), lambda i: (rev(i), 0)),
                  _resident((SUBLANES, W)), _resident((1, W)), _resident((W, W)), _resident((W, W)), _resident((1, W)),
                  _resident((W, W)), _resident((W, W)), _resident((1, W)), _resident((1, W))],
        out_specs=[pl.BlockSpec((tt, 2 * W), lambda i: (rev(i), 0)),
                   pl.BlockSpec((5, SUBLANES, W), lambda i: (0, 0, 0)),
                   pl.BlockSpec((W, W), lambda i: (0, 0)), pl.BlockSpec((W, W), lambda i: (0, 0)),
                   pl.BlockSpec((3, SUBLANES, W), lambda i: (0, 0, 0))],
        out_shape=[jax.ShapeDtypeStruct((T, 2 * W), BF16), jax.ShapeDtypeStruct((5, SUBLANES, W), F32),
                   jax.ShapeDtypeStruct((W, W), F32), jax.ShapeDtypeStruct((W, W), F32),
                   jax.ShapeDtypeStruct((3, SUBLANES, W), F32)],
        scratch_shapes=[pltpu.VMEM((tt, W), F32), pltpu.VMEM((tt, W), F32), pltpu.VMEM((tt, W), F32),
                        pltpu.VMEM((SUBLANES, W), F32), pltpu.VMEM((SUBLANES, W), F32), pltpu.VMEM((BF16_ROWS, W), F32)],
        compiler_params=_cp(1),
    )(z, z, hseq, hseq, dmix, cw, cb, wa, wat, ba, wx, wxt, bx, lam)


def _split3(x):
    hi = x.astype(BF16)
    r1 = x - hi.astype(F32)
    mid = r1.astype(BF16)
    lo = (r1 - mid.astype(F32)).astype(BF16)
    return hi, mid, lo


def _tri_matmul(tri, x):
    hi, mid, lo = _split3(x)
    return (jnp.dot(tri, hi, preferred_element_type=F32) + jnp.dot(tri, mid, preferred_element_type=F32)
            + jnp.dot(tri, lo, preferred_element_type=F32))


def _hg_chunk(q, fl, lb):
    C = q.shape[0]
    ri = lax.broadcasted_iota(jnp.int32, (C, C), 0)
    ci = lax.broadcasted_iota(jnp.int32, (C, C), 1)
    causal = ri >= ci
    sig = _sigmoid(fl)
    f = lb + (1.0 - lb) * sig
    k = 1.0 - f
    sq = _sigmoid(q)
    qf = q * sq
    b = _tri_matmul(causal.astype(BF16), jnp.log(f))
    bm = b[C // 2 - 1:C // 2, :]
    bl = b[C - 1:C, :]
    qt = qf * jnp.exp(b - bm)
    kt = k * jnp.exp(bm - b)
    qin = qf * jnp.exp(b)
    kout = k * jnp.exp(bl - b)
    qtb, ktb = qt.astype(BF16), kt.astype(BF16)
    att = [jnp.where(causal, _dot(qtb[:, _head(h)], ktb[:, _head(h)], NT), 0.0).astype(BF16) for h in range(HG_HEADS)]
    return dict(sig=sig, f=f, k=k, sq=sq, qf=qf, b=b, bm=bm, bl=bl, qt=qt, kt=kt, qin=qin, kout=kout, att=att,
                causal=causal, anti=ri <= ci, decay=jnp.exp(bl))


def _head(h):
    return slice(h * HG_D, (h + 1) * HG_D)


def _hgrn_fwd(z, lb, gn, name="hgrn_fwd", tt=256):
    T = z.shape[0]
    C = HG_CHUNK
    nc = tt // C
    Dh = HG_D
    Wd = HG_HEADS * Dh

    def body(q_ref, f_ref, v_ref, g_ref, lb_ref, gn_ref, o_ref, ss_ref, st):
        @pl.when(pl.program_id(0) == 0)
        def _():
            st[...] = jnp.zeros_like(st)

        S = [st[h] for h in range(HG_HEADS)]
        for c in range(nc):
            rows = slice(c * C, (c + 1) * C)
            ck = _hg_chunk(q_ref[rows, :].astype(F32), f_ref[rows, :].astype(F32), lb_ref[...])
            v = v_ref[rows, :]
            g = g_ref[rows, :].astype(F32)
            H = range(HG_HEADS)
            qinb, koutb = ck["qin"].astype(BF16), ck["kout"].astype(BF16)
            for h in H:
                ss_ref[h, c] = S[h]
            o = [_dot(ck["att"][h], v[:, _head(h)]) + _dot(qinb[:, _head(h)], S[h], NT) for h in H]
            S = [ck["decay"][:, _head(h)] * S[h] + _dot(v[:, _head(h)], koutb[:, _head(h)], TN) for h in H]
            outs = [o[h] * lax.rsqrt(jnp.mean(o[h] * o[h], axis=-1, keepdims=True) + EPS) * gn_ref[...] for h in H]
            o_ref[rows, :] = (jnp.concatenate(outs, axis=1) * (g * _sigmoid(g))).astype(BF16)
        for h in range(HG_HEADS):
            st[h] = S[h]

    col = lambda base: (lambda i: (i, base))
    return pl.pallas_call(
        body, name=name, grid=(T // tt,),
        in_specs=[pl.BlockSpec((tt, Wd), col(2)), pl.BlockSpec((tt, Wd), col(3)), pl.BlockSpec((tt, Wd), col(4)),
                  pl.BlockSpec((tt, Wd), col(5)), _resident((1, Wd)), _resident((1, Dh))],
        out_specs=[pl.BlockSpec((tt, Wd), lambda i: (i, 0)),
                   pl.BlockSpec((HG_HEADS, nc, Dh, Dh), lambda i: (0, i, 0, 0))],
        out_shape=[jax.ShapeDtypeStruct((T, Wd), BF16),
                   jax.ShapeDtypeStruct((HG_HEADS, T // C, Dh, Dh), F32)],
        scratch_shapes=[pltpu.VMEM((HG_HEADS, Dh, Dh), F32)],
        compiler_params=_cp(1),
    )(z, z, z, z, lb, gn)


def _hgrn_bwd(z, ss, dmix, lb, gn, name="hgrn_bwd", tt=256):
    T = z.shape[0]
    C = HG_CHUNK
    nc = tt // C
    nt = T // tt
    Dh = HG_D
    Wd = HG_HEADS * Dh

    def body(q_ref, f_ref, v_ref, g_ref, ss_ref, dm_ref, lb_ref, gn_ref, dz_ref, dlb_ref, dgn_ref, dst):
        @pl.when(pl.program_id(0) == 0)
        def _():
            dst[...] = jnp.zeros_like(dst)
            dlb_ref[...] = jnp.zeros_like(dlb_ref)
            dgn_ref[...] = jnp.zeros_like(dgn_ref)

        dS = [dst[h] for h in range(HG_HEADS)]
        lbv = lb_ref[...]
        gnv = gn_ref[...]
        rowc = lax.broadcasted_iota(jnp.int32, (C, Wd), 0)
        cat = lambda xs: jnp.concatenate(xs, axis=1)
        for c in reversed(range(nc)):
            rows = slice(c * C, (c + 1) * C)
            q = q_ref[rows, :].astype(F32)
            ck = _hg_chunk(q, f_ref[rows, :].astype(F32), lbv)
            v = v_ref[rows, :]
            g = g_ref[rows, :].astype(F32)
            dout = dm_ref[rows, :].astype(F32)
            sg, dsg = _silu_and_grad(g)
            d_ong = dout * sg
            H = range(HG_HEADS)
            qinb, koutb, qtb, ktb = (ck[n].astype(BF16) for n in ("qin", "kout", "qt", "kt"))
            S = [ss_ref[h, c] for h in H]
            Sb = [s.astype(BF16) for s in S]
            dSb = [d.astype(BF16) for d in dS]
            o = [_dot(ck["att"][h], v[:, _head(h)]) + _dot(qinb[:, _head(h)], Sb[h], NT) for h in H]
            rn = [lax.rsqrt(jnp.mean(o[h] * o[h], axis=-1, keepdims=True) + EPS) for h in H]
            on = [o[h] * rn[h] for h in H]
            don = [d_ong[:, _head(h)] * gnv for h in H]
            do = [(rn[h] * (don[h] - on[h] * jnp.mean(don[h] * on[h], axis=-1, keepdims=True))).astype(BF16) for h in H]
            datt = [jnp.where(ck["causal"], _dot(do[h], v[:, _head(h)], NT), 0.0).astype(BF16) for h in H]
            dvs = [_dot(ck["att"][h], do[h], TN) + _dot(koutb[:, _head(h)], dSb[h], NT) for h in H]
            dqins = [_dot(do[h], Sb[h]) for h in H]
            dkouts = [_dot(v[:, _head(h)], dSb[h]) for h in H]
            dqts = [_dot(datt[h], ktb[:, _head(h)]) for h in H]
            dkts = [_dot(datt[h], qtb[:, _head(h)], TN) for h in H]
            ddecays = [jnp.sum(dS[h] * S[h], axis=0, keepdims=True) for h in H]
            dS = [_dot(do[h], qinb[:, _head(h)], TN) + ck["decay"][:, _head(h)] * dS[h] for h in H]
            ons = [on[h] * gnv for h in H]
            dgn = _rsum8(d_ong[:, _head(0)] * on[0])
            for h in range(1, HG_HEADS):
                dgn = dgn + _rsum8(d_ong[:, _head(h)] * on[h])
            dgn_ref[...] += dgn
            dqt, dkt, dqin, dkout, ddecay = cat(dqts), cat(dkts), cat(dqins), cat(dkouts), cat(ddecays)
            b, bm, bl = ck["b"], ck["bm"], ck["bl"]
            dqf = dqt * jnp.exp(b - bm) + dqin * jnp.exp(b)
            dk = dkt * jnp.exp(bm - b) + dkout * jnp.exp(bl - b)
            kk = dkout * ck["kout"]
            db = dqt * ck["qt"] - dkt * ck["kt"] + dqin * ck["qin"] - kk
            dbl = jnp.sum(kk, axis=0, keepdims=True) + ddecay * ck["decay"]
            db = db + jnp.where(rowc == C - 1, dbl, 0.0)
            dlogf = _tri_matmul(ck["anti"].astype(BF16), db)
            dfv = dlogf / ck["f"] - dk
            sig, sq = ck["sig"], ck["sq"]
            dlb_ref[...] += _rsum8(dfv * (1.0 - sig))
            dz_ref[rows, 0:Wd] = (dqf * (sq * (1.0 + q * (1.0 - sq)))).astype(BF16)
            dz_ref[rows, Wd:2 * Wd] = (dfv * (1.0 - lbv) * sig * (1.0 - sig)).astype(BF16)
            dz_ref[rows, 2 * Wd:3 * Wd] = cat(dvs).astype(BF16)
            dz_ref[rows, 3 * Wd:4 * Wd] = (dout * cat(ons) * dsg).astype(BF16)
        for h in range(HG_HEADS):
            dst[h] = dS[h]

    rev = lambda i: nt - 1 - i
    col = lambda base: (lambda i: (rev(i), base))
    return pl.pallas_call(
        body, name=name, grid=(nt,),
        in_specs=[pl.BlockSpec((tt, Wd), col(2)), pl.BlockSpec((tt, Wd), col(3)), pl.BlockSpec((tt, Wd), col(4)),
                  pl.BlockSpec((tt, Wd), col(5)),
                  pl.BlockSpec((HG_HEADS, nc, Dh, Dh), lambda i: (0, rev(i), 0, 0)),
                  pl.BlockSpec((tt, Wd), col(1)), _resident((1, Wd)), _resident((1, Dh))],
        out_specs=[pl.BlockSpec((tt, 4 * Wd), lambda i: (rev(i), 0)), pl.BlockSpec((SUBLANES, Wd), lambda i: (0, 0)),
                   pl.BlockSpec((SUBLANES, Dh), lambda i: (0, 0))],
        out_shape=[jax.ShapeDtypeStruct((T, 4 * Wd), BF16), jax.ShapeDtypeStruct((SUBLANES, Wd), F32),
                   jax.ShapeDtypeStruct((SUBLANES, Dh), F32)],
        scratch_shapes=[pltpu.VMEM((HG_HEADS, Dh, Dh), F32)],
        compiler_params=_cp(1),
    )(z, z, z, z, ss, dmix, lb, gn)


def _sgu_core(p, lg_ref, lb_ref, wsc_ref, bsb_ref):
    Wd = D_MODEL
    G = SGU_CHUNK
    zz = _gelu(p)
    u = zz[:, :Wd]
    v = zz[:, Wd:]
    vc = v - jnp.mean(v, axis=-1, keepdims=True)
    rstd = lax.rsqrt(jnp.mean(vc * vc, axis=-1, keepdims=True) + EPS)
    vhat = vc * rstd
    vn = vhat * lg_ref[...] + lb_ref[...]
    svs = []
    for gi in range(SGU_G):
        svs.append(jnp.dot(wsc_ref[gi], vn[:, gi * G:(gi + 1) * G].astype(BF16), preferred_element_type=F32) + bsb_ref[gi])
    return u, vhat, rstd, vn, jnp.concatenate(svs, axis=1)


def _sgu_fwd(p1, lg, lbias, wsc, bsb, name="sgu_fwd", tt=512):
    T = p1.shape[0]
    Wd = D_MODEL
    C = SGU_CHUNK

    def body(p_ref, lg_ref, lb_ref, wsc_ref, bsb_ref, s_ref):
        for c in range(tt // C):
            rows = slice(c * C, (c + 1) * C)
            u, _, _, _, sv = _sgu_core(p_ref[rows, :].astype(F32), lg_ref, lb_ref, wsc_ref, bsb_ref)
            s_ref[rows, :] = (u * sv).astype(BF16)

    return pl.pallas_call(
        body, name=name, grid=(T // tt,),
        in_specs=[pl.BlockSpec((tt, 2 * Wd), lambda i: (i, 0)), _resident((1, Wd)), _resident((1, Wd)),
                  _resident((SGU_G, C, C)), _resident((SGU_G, C, C))],
        out_specs=pl.BlockSpec((tt, Wd), lambda i: (i, 0)),
        out_shape=jax.ShapeDtypeStruct((T, Wd), BF16),
        compiler_params=_cp(1),
    )(p1, lg, lbias, wsc, bsb)


def _sgu_bwd(p1, ds, lg, lbias, wsc, wsct, bsb, name="sgu_bwd", tt=512):
    T = p1.shape[0]
    Wd = D_MODEL
    C = SGU_CHUNK

    def body(p_ref, ds_ref, lg_ref, lb_ref, wsc_ref, wsct_ref, bsb_ref, dp_ref, dws_ref, dbs_ref, dlg_ref, dlb_ref, dbin_ref):
        @pl.when(pl.program_id(0) == 0)
        def _():
            dws_ref[...] = jnp.zeros_like(dws_ref)
            dbs_ref[...] = jnp.zeros_like(dbs_ref)
            dlg_ref[...] = jnp.zeros_like(dlg_ref)
            dlb_ref[...] = jnp.zeros_like(dlb_ref)
            dbin_ref[...] = jnp.zeros_like(dbin_ref)

        for c in range(tt // C):
            rows = slice(c * C, (c + 1) * C)
            p = p_ref[rows, :].astype(F32)
            u, vhat, rstd, vn, sv = _sgu_core(p, lg_ref, lb_ref, wsc_ref, bsb_ref)
            dsc = ds_ref[rows, :].astype(F32)
            du = dsc * sv
            dsv = dsc * u
            dvns = []
            for gi in range(SGU_G):
                cs = slice(gi * C, (gi + 1) * C)
                dsv_g = dsv[:, cs]
                dvns.append(jnp.dot(wsct_ref[gi], dsv_g.astype(BF16), preferred_element_type=F32))
                dws_ref[gi] += _dot(dsv_g, vn[:, cs], NT)
                dbs_ref[gi] += dsv_g
            dvn = jnp.concatenate(dvns, axis=1)
            dlg_ref[...] += _rsum8(dvn * vhat)
            dlb_ref[...] += _rsum8(dvn)
            dvh = dvn * lg_ref[...]
            dv = rstd * (dvh - jnp.mean(dvh, axis=-1, keepdims=True) - vhat * jnp.mean(dvh * vhat, axis=-1, keepdims=True))
            dp = jnp.concatenate([du, dv], axis=1) * _gelu_grad(p)
            dbin_ref[...] += _rsum8(dp)
            dp_ref[rows, :] = dp.astype(BF16)

    full3 = pl.BlockSpec((SGU_G, C, C), lambda i: (0, 0, 0))
    return pl.pallas_call(
        body, name=name, grid=(T // tt,),
        in_specs=[pl.BlockSpec((tt, 2 * Wd), lambda i: (i, 0)), pl.BlockSpec((tt, Wd), lambda i: (i, 0)),
                  _resident((1, Wd)), _resident((1, Wd)), _resident((SGU_G, C, C)), _resident((SGU_G, C, C)),
                  _resident((SGU_G, C, C))],
        out_specs=[pl.BlockSpec((tt, 2 * Wd), lambda i: (i, 0)), full3, full3,
                   pl.BlockSpec((SUBLANES, Wd), lambda i: (0, 0)), pl.BlockSpec((SUBLANES, Wd), lambda i: (0, 0)),
                   pl.BlockSpec((SUBLANES, 2 * Wd), lambda i: (0, 0))],
        out_shape=[jax.ShapeDtypeStruct((T, 2 * Wd), BF16), jax.ShapeDtypeStruct((SGU_G, C, C), F32),
                   jax.ShapeDtypeStruct((SGU_G, C, C), F32), jax.ShapeDtypeStruct((SUBLANES, Wd), F32),
                   jax.ShapeDtypeStruct((SUBLANES, Wd), F32), jax.ShapeDtypeStruct((SUBLANES, 2 * Wd), F32)],
        compiler_params=_cp(1),
    )(p1, ds, lg, lbias, wsc, wsct, bsb)


def _pad_rows(w, rows=SUBLANES):
    return jnp.pad(w, ((0, rows - w.shape[0]), (0, 0)))


def _block_diag(w):
    n, b, _ = w.shape
    return (w[:, :, None, :] * jnp.eye(n, dtype=w.dtype)[:, None, :, None]).reshape(n * b, n * b)


def _diag_blocks(m, n):
    b = m.shape[0] // n
    m4 = m.reshape(n, b, n, b)
    return jnp.stack([m4[k, :, k, :] for k in range(n)], axis=0)


def _piece_major(dw):
    if dw.ndim == 2:
        K, N = dw.shape
        return dw.reshape(N_CHIPS, 2, K // (2 * N_CHIPS), N)
    _, K, ns = dw.shape
    return dw.reshape(N_CHIPS, 2, K // 2, ns)


def _ffn_fwd(h, g, w_up, cw, cb, w_down, tag):
    hn, gu = _norm_mm(h, g, w_up, jnp.zeros((1, w_up.shape[1]), F32), name=f"ffn_up_{tag}")
    a, out = _ffn_act(gu, cw, cb, w_down, h, name=f"ffn_act_down_{tag}")
    return out, (hn, gu, a)


def _ffn_bwd(dh, h, g, saved, w_up, cw, cb, w_down, tag):
    hn, gu, a = saved
    da = _mm(dh, w_down, None, BF16, name=f"ffn_da_{tag}", transpose_w=True)
    dwd = _mm_tn(a, dh, name=f"ffn_dwd_{tag}")
    dgu, dc, dhin, dg8 = _ffn_act_bwd(gu, da, cw, cb, w_up, h, g, dh, name=f"ffn_actb_dh_{tag}")
    dwu = _mm_tn(hn, dgu, name=f"ffn_dwu_{tag}", col_shards=N_CHIPS)
    dcs = dc.sum(axis=1)
    return dhin, dg8.sum(axis=0), dwu, dcs[0:3], dcs[3], dwd


REDUCE_GROUPS = {"g1": [("ffn_w_up", 1), ("ffn_w_down", 1), ("od_w_out", 0), ("od_w_in", 0)],
                 "g2": [("ffn_w_up", 0), ("ffn_w_down", 0)],
                 "g3": [("ev_w_out", 0), ("ev_w_in", 0)]}


def _local_step(x, tgt, p, start_reduce=None):
    row = lambda v: v.reshape(1, -1)
    grads = {}

    lower = jax.nn.softmax(p["hg_lb_logits"], axis=0)
    lb0 = row(lower[0])
    ev_cw = _pad_rows(p["ev_conv_w"][0])
    ev_cb = row(p["ev_conv_b"][0])
    wa = _block_diag(p["ev_gate_a_w"][0]).astype(BF16)
    wx = _block_diag(p["ev_gate_x_w"][0]).astype(BF16)
    ba, bx, lam = row(p["ev_gate_a_b"][0]), row(p["ev_gate_x_b"][0]), row(p["ev_lru_lambda"][0])
    gn = row(p["ev_hg_norm"][0])
    tril = jnp.tril(jnp.ones((SGU_CHUNK, SGU_CHUNK), F32))
    wsc = (p["od_w_s"][0] * tril).astype(BF16)
    bsb = jnp.broadcast_to(p["od_b_s"][0][:, :, None], (SGU_G, SGU_CHUNK, SGU_CHUNK)).astype(F32)
    ffn_cw = [_pad_rows(p["ffn_conv_w"][l]) for l in range(2)]
    ffn_cb = [row(p["ffn_conv_b"][l]) for l in range(2)]
    ev_w_in, ev_w_out = p["ev_w_in"][0], p["ev_w_out"][0]
    nm = [row(p["norm_mix"][l]) for l in range(2)]
    nf = [row(p["norm_ffn"][l]) for l in range(2)]

    h0 = x
    hn0, z0 = _norm_mm(h0, nm[0], ev_w_in, jnp.zeros((1, ev_w_in.shape[1]), F32), name="ev_in")
    out_a, hseq = _lru_fwd(z0, ev_cw, ev_cb, wa, ba, wx, bx, lam)
    out_b, ss = _hgrn_fwd(z0, lb0, gn)
    mix0 = jnp.concatenate([out_a, out_b], axis=1)
    h1 = _mm(mix0, ev_w_out, h0, F32, name="ev_out")
    late = p["late"](h1) if "late" in p else p
    od_w_in, od_w_out = late["od_w_in"][0], late["od_w_out"][0]
    w_up = [late["ffn_w_up"][l] for l in range(2)]
    w_down = [late["ffn_w_down"][l] for l in range(2)]
    h2, ffn0 = _ffn_fwd(h1, nf[0], w_up[0], ffn_cw[0], ffn_cb[0], w_down[0], "l0")
    hn1, p1 = _norm_mm(h2, nm[1], od_w_in, row(p["od_b_in"][0]), name="od_in")
    s1 = _sgu_fwd(p1, row(p["od_ln_g"][0]), row(p["od_ln_b"][0]), wsc, bsb)
    h3 = _mm(s1, od_w_out, h2, F32, name="od_out")
    h4, ffn1 = _ffn_fwd(h3, nf[1], w_up[1], ffn_cw[1], ffn_cb[1], w_down[1], "l1")
    dh4, dgf8, sq8 = _final_loss(h4, row(p["norm_final"]), tgt)
    grads["norm_final"] = dgf8.sum(axis=0)

    big = {}
    dh3, dnf1, dwu1, dcw1, dcb1, dwd1 = _ffn_bwd(dh4, h3, nf[1], ffn1, w_up[1], ffn_cw[1], ffn_cb[1], w_down[1], "l1")
    big["ffn_w_up", 1], big["ffn_w_down", 1] = _piece_major(dwu1), _piece_major(dwd1)
    ds1 = _mm(dh3, od_w_out, None, BF16, name="od_ds", transpose_w=True)
    big["od_w_out", 0] = _piece_major(_mm_tn(s1, dh3, name="od_dwo"))
    wsct = jnp.swapaxes(wsc, 1, 2)
    dp1, dws, dbs, dlg8, dlb8, dbin8 = _sgu_bwd(p1, ds1, row(p["od_ln_g"][0]), row(p["od_ln_b"][0]), wsc, wsct, bsb)
    grads["od_w_s"] = (dws * tril)[None]
    grads["od_b_s"] = dbs.sum(axis=-1)[None]
    grads["od_ln_g"] = dlg8.sum(axis=0)[None]
    grads["od_ln_b"] = dlb8.sum(axis=0)[None]
    grads["od_b_in"] = dbin8.sum(axis=0)[None]
    dh2, dnm1 = _mm_normbwd(dp1, od_w_in, h2, nm[1], dh3, name="od_dh")
    big["od_w_in", 0] = _piece_major(_mm_tn(hn1, dp1, name="od_dwi", col_shards=N_CHIPS))
    if start_reduce is not None:
        token = start_reduce("g1", [big[key] for key in REDUCE_GROUPS["g1"]])
        ffn_cb[0] = ffn_cb[0] + token[0:1, 0:1]

    dh1, dnf0, dwu0, dcw0, dcb0, dwd0 = _ffn_bwd(dh2, h1, nf[0], ffn0, w_up[0], ffn_cw[0], ffn_cb[0], w_down[0], "l0")
    big["ffn_w_up", 0], big["ffn_w_down", 0] = _piece_major(dwu0), _piece_major(dwd0)
    if start_reduce is not None:
        token = start_reduce("g2", [big[key] for key in REDUCE_GROUPS["g2"]])
        ev_cb = ev_cb + token[0:1, 0:1]
    dmix = _mm(dh1, ev_w_out, None, BF16, name="ev_dmix", transpose_w=True)
    big["ev_w_out", 0] = _piece_major(_mm_tn(mix0, dh1, name="ev_dwo"))
    dz01, dc5, dwa, dwx, dvec = _lru_bwd(z0, hseq, dmix, ev_cw, ev_cb, wa, wa.T, ba, wx, wx.T, bx, lam)
    dz2, dlb8, dgn8 = _hgrn_bwd(z0, ss, dmix, lb0, gn)
    dz0 = jnp.concatenate([dz01, dz2], axis=1)
    big["ev_w_in", 0] = _piece_major(_mm_tn(hn0, dz0, name="ev_dwi", col_shards=N_CHIPS))
    if start_reduce is not None:
        token = start_reduce("g3", [big[key] for key in REDUCE_GROUPS["g3"]])
        nm[0] = nm[0] + token[0:1, 0:1]
    grad_x, dnm0 = _mm_normbwd(dz0, ev_w_in, h0, nm[0], dh1, name="ev_dh")

    dc5s = dc5.sum(axis=1)
    grads["ev_conv_w"] = dc5s[0:4][None]
    grads["ev_conv_b"] = dc5s[4][None]
    grads["ev_gate_a_w"] = _diag_blocks(dwa, LRU_BLOCKS)[None]
    grads["ev_gate_x_w"] = _diag_blocks(dwx, LRU_BLOCKS)[None]
    dvs = dvec.sum(axis=1)
    grads["ev_gate_a_b"] = dvs[0][None]
    grads["ev_gate_x_b"] = dvs[1][None]
    grads["ev_lru_lambda"] = (dvs[2] * (-jax.nn.sigmoid(-p["ev_lru_lambda"][0])))[None]
    dlb = dlb8.sum(axis=0)
    grads["hg_lb_logits"] = dlb[None, :] * lower[0][None, :] * (jnp.eye(3, dtype=F32)[0][:, None] - lower)
    grads["ev_hg_norm"] = dgn8.sum(axis=0)[None]
    grads["norm_mix"] = jnp.stack([dnm0.sum(axis=0), dnm1.sum(axis=0)])
    grads["norm_ffn"] = jnp.stack([dnf0, dnf1])
    grads["ffn_conv_w"] = jnp.stack([dcw0, dcw1])
    grads["ffn_conv_b"] = jnp.stack([dcb0, dcb1])
    return sq8, grad_x, grads, big


MESH = pl.DeviceIdType.MESH
ANY = pl.BlockSpec(memory_space=pl.ANY)
N_CHIPS = 4
N_DEV = 8

SH_BIG = {"ev_w_in": 2, "ev_w_out": 1, "od_w_in": 2, "od_w_out": 1, "ffn_w_up": 2, "ffn_w_down": 1}
SH_SMALL = {"ev_conv_w": 2, "od_b_in": 1, "od_ln_g": 1, "od_ln_b": 1, "ffn_conv_w": 2}
REP = ["norm_mix", "norm_ffn", "norm_final", "ev_conv_b", "ev_gate_a_w", "ev_gate_a_b", "ev_gate_x_w", "ev_gate_x_b",
       "ev_lru_lambda", "hg_lb_logits", "ev_hg_norm", "od_w_s", "od_b_s", "ffn_conv_b"]
WEIGHTS = ["norm_mix", "norm_ffn", "norm_final", "ev_w_in", "ev_conv_w", "ev_conv_b", "ev_gate_a_w", "ev_gate_a_b", "ev_gate_x_w",
           "ev_gate_x_b", "ev_lru_lambda", "hg_lb_logits", "ev_hg_norm", "ev_w_out", "od_w_in", "od_b_in", "od_ln_g", "od_ln_b",
           "od_w_s", "od_b_s", "od_w_out", "ffn_w_up", "ffn_conv_w", "ffn_conv_b", "ffn_w_down"]


def _rows(n_elems, mult=SUBLANES):
    r = -(-n_elems // LANES)
    return -(-r // mult) * mult


def _pack(arrs, rows, dtype):
    flat = jnp.concatenate([a.reshape(-1).astype(dtype) for a in arrs])
    return jnp.pad(flat, (0, rows * LANES - flat.shape[0])).reshape(rows, LANES)


def _unpack(flat2d, shapes):
    flat = flat2d.reshape(-1)
    out, off = [], 0
    for s in shapes:
        n = 1
        for d in s:
            n *= d
        out.append(flat[off:off + n].reshape(s))
        off += n
    return out


def _mesh_pos():
    return lax.axis_index("x"), lax.axis_index("y"), lax.axis_index("c")


def _other_chips(x, y):
    return [(1 - x, y), (x, 1 - y), (1 - x, 1 - y)]


def _half_rows(n):
    return lambda r, c: r.at[0, pl.ds(c * (n // 2), n // 2), :]


GATHER_BIG = {
    "ev_w_in": ((1024, 3072), _half_rows(1024), lambda o, k, c: o.at[pl.ds(c * 512, 512), pl.ds(k * 768, 768)]),
    "ev_w_out": ((1024, 1024), _half_rows(256), lambda o, k, c: o.at[pl.ds(k * 256 + c * 128, 128), :]),
    "od_w_in": ((1024, 2048), _half_rows(1024), lambda o, k, c: o.at[pl.ds(c * 512, 512), pl.ds(k * 512, 512)]),
    "od_w_out": ((1024, 1024), _half_rows(256), lambda o, k, c: o.at[pl.ds(k * 256 + c * 128, 128), :]),
    "ffn_w_up": ((2, 1024, 2 * D_FF), lambda r, c: r.at[c], lambda o, k, c: o.at[c, :, pl.ds(k * (2 * D_FF // 4), 2 * D_FF // 4)]),
    "ffn_w_down": ((2, D_FF, 1024), lambda r, c: r.at[c], lambda o, k, c: o.at[c, pl.ds(k * (D_FF // 4), D_FF // 4), :]),
}


def _gather_weights(names, big, small):
    nb = len(big)
    descs = [GATHER_BIG[n] for n in names]
    rs = small.shape[0]

    def body(*refs):
        ins, s_ref = refs[:nb], refs[nb]
        outs, os_ref = refs[nb + 1:2 * nb + 1], refs[2 * nb + 1]
        ici_send, ici_recv, d2d_send, d2d_recv, loc_sems = refs[2 * nb + 2:2 * nb + 7]
        vbufs = refs[2 * nb + 7:]
        x, y, c = _mesh_pos()
        k = 2 * x + y
        chips = _other_chips(x, y)
        sib = (x, y, 1 - c)

        def remote(src, dst, ssem, rsem, to):
            return pltpu.make_async_remote_copy(src_ref=src, dst_ref=dst, send_sem=ssem, recv_sem=rsem, device_id=to,
                                                device_id_type=MESH)

        stage = [pltpu.make_async_copy(ins[t], vbufs[t], loc_sems.at[2 * t]) for t in range(nb)]
        stage.append(pltpu.make_async_copy(s_ref, vbufs[nb], loc_sems.at[2 * nb]))
        for cp in stage:
            cp.start()
        sends = []
        for t, (_, src, dst) in enumerate(descs):
            for j, (px, py) in enumerate(chips):
                sends.append(remote(src(ins[t], c), dst(outs[t], k, c), ici_send.at[3 * t + j], ici_recv.at[3 * t + j], (px, py, c)))
        for j, (px, py) in enumerate(chips):
            sends.append(remote(s_ref, os_ref.at[k], ici_send.at[3 * nb + j], ici_recv.at[3 * nb + j], (px, py, c)))
        for cp in sends:
            cp.start()
        for cp in stage:
            cp.wait()
        local = []
        for t, (_, src, dst) in enumerate(descs):
            for cc in (0, 1):
                local.append(pltpu.make_async_copy(src(vbufs[t], cc), dst(outs[t], k, cc), loc_sems.at[2 * t + cc]))
        local.append(pltpu.make_async_copy(vbufs[nb], os_ref.at[k], loc_sems.at[2 * nb]))
        for cp in local:
            cp.start()
        for t, (_, src, dst) in enumerate(descs):
            for j, (px, py) in enumerate(chips):
                got = dst(outs[t], 2 * px + py, c)
                remote(got, got, ici_send.at[3 * t + j], ici_recv.at[3 * t + j], (px, py, c)).wait_recv()
                fwd = remote(got, got, d2d_send.at[3 * t + j], d2d_recv.at[3 * t + j], sib)
                fwd.start()
                sends.append(fwd)
        for j, (px, py) in enumerate(chips):
            remote(s_ref, os_ref.at[2 * px + py], ici_send.at[3 * nb + j], ici_recv.at[3 * nb + j], (px, py, c)).wait_recv()
        for t, (_, src, dst) in enumerate(descs):
            for j, (px, py) in enumerate(chips):
                theirs = dst(outs[t], 2 * px + py, 1 - c)
                remote(theirs, theirs, d2d_send.at[3 * t + j], d2d_recv.at[3 * t + j], sib).wait_recv()
        for cp in sends:
            cp.wait_send()
        for cp in local:
            cp.wait()

    out_shape = [jax.ShapeDtypeStruct(d[0], BF16) for d in descs] + [jax.ShapeDtypeStruct((N_CHIPS, rs, LANES), small.dtype)]
    return pl.pallas_call(
        body, name="gather_weights", in_specs=[ANY] * (nb + 1), out_specs=[ANY] * (nb + 1), out_shape=out_shape,
        scratch_shapes=[pltpu.SemaphoreType.DMA((3 * nb + 3,)), pltpu.SemaphoreType.DMA((3 * nb + 3,)),
                        pltpu.SemaphoreType.DMA((3 * nb,)), pltpu.SemaphoreType.DMA((3 * nb,)),
                        pltpu.SemaphoreType.DMA((2 * nb + 1,))]
        + [pltpu.VMEM(b.shape, b.dtype) for b in big] + [pltpu.VMEM(small.shape, small.dtype)],
        compiler_params=pltpu.CompilerParams(vmem_limit_bytes=VMEM_LIMIT),
    )(*big, small)


def _place_own(names, big):
    nb = len(big)
    descs = [GATHER_BIG[n] for n in names]

    def body(*refs):
        ins, outs = refs[:nb], refs[nb:2 * nb]
        sems, vbufs = refs[2 * nb], refs[2 * nb + 1:]
        x, y, c = _mesh_pos()
        k = 2 * x + y
        stage = [pltpu.make_async_copy(ins[t], vbufs[t], sems.at[2 * t]) for t in range(nb)]
        for cp in stage:
            cp.start()
        for cp in stage:
            cp.wait()
        local = [pltpu.make_async_copy(src(vbufs[t], cc), dst(outs[t], k, cc), sems.at[2 * t + cc])
                 for t, (_, src, dst) in enumerate(descs) for cc in (0, 1)]
        for cp in local:
            cp.start()
        for cp in local:
            cp.wait()

    return pl.pallas_call(
        body, name="place_own", in_specs=[ANY] * nb, out_specs=[ANY] * nb,
        out_shape=[jax.ShapeDtypeStruct(d[0], BF16) for d in descs],
        scratch_shapes=[pltpu.SemaphoreType.DMA((2 * nb,))] + [pltpu.VMEM(b.shape, b.dtype) for b in big],
        compiler_params=pltpu.CompilerParams(vmem_limit_bytes=VMEM_LIMIT),
    )(*big)


def _gather_start(names, big, bufs):
    nb = len(big)
    descs = [GATHER_BIG[n] for n in names]

    def body(*refs):
        ins, lnd = refs[:nb], refs[nb:2 * nb]
        send_sems, recv_sems, token = refs[2 * nb], refs[2 * nb + 1], refs[-1]
        x, y, c = _mesh_pos()
        k = 2 * x + y
        for t, (_, src, dst) in enumerate(descs):
            for j, (px, py) in enumerate(_other_chips(x, y)):
                _remote(src(ins[t], c), dst(lnd[t], k, c), send_sems.at[3 * t + j], recv_sems.at[3 * t + j], (px, py, c)).start()
        token[...] = jnp.zeros_like(token)

    out = pl.pallas_call(
        body, name="gather_start",
        out_shape=(pltpu.SemaphoreType.DMA((3 * nb,)), pltpu.SemaphoreType.DMA((3 * nb,)),
                   *[pltpu.HBM(b.shape, b.dtype) for b in big], *[pltpu.HBM(b.shape, b.dtype) for b in bufs],
                   jax.ShapeDtypeStruct((SUBLANES, LANES), F32)),
        in_specs=[HBM] * (2 * nb), out_specs=(SEM, SEM, *[HBM] * (2 * nb), pl.BlockSpec(memory_space=pltpu.VMEM)),
        input_output_aliases={i: 2 + i for i in range(2 * nb)},
        compiler_params=pltpu.CompilerParams(has_side_effects=DATAFLOW),
    )(*[pltpu.with_memory_space_constraint(b, pltpu.HBM) for b in big], *[pltpu.with_memory_space_constraint(b, pltpu.HBM) for b in bufs])
    return out[0], out[1], list(out[2:2 + nb]), list(out[2 + nb:2 + 2 * nb]), out[-1]


def _gather_wait(names, send_sems, recv_sems, big, bufs, after):
    nb = len(big)
    descs = [GATHER_BIG[n] for n in names]

    def body(*refs):
        ins, lnd = refs[:nb], refs[nb:2 * nb]
        ssem, rsem = refs[2 * nb], refs[2 * nb + 1]
        x, y, c = _mesh_pos()
        for t, (_, src, dst) in enumerate(descs):
            for j, (px, py) in enumerate(_other_chips(x, y)):
                cp = _remote(src(ins[t], c), dst(lnd[t], 2 * px + py, c), ssem.at[3 * t + j], rsem.at[3 * t + j], (px, py, c))
                cp.wait_send()
                cp.wait_recv()

    out = pl.pallas_call(
        body, name="gather_wait",
        out_shape=(*[pltpu.HBM(b.shape, b.dtype) for b in big], *[pltpu.HBM(b.shape, b.dtype) for b in bufs]),
        in_specs=[HBM] * (2 * nb) + [SEM, SEM, ANY], out_specs=tuple([HBM] * (2 * nb)),
        input_output_aliases={i: i for i in range(2 * nb)},
        compiler_params=pltpu.CompilerParams(has_side_effects=DATAFLOW),
    )(*big, *bufs, send_sems, recv_sems, after)
    return list(out[nb:])


def _gather_forward(names, bufs):
    nb = len(bufs)
    descs = [GATHER_BIG[n] for n in names]

    def body(*refs):
        outs = refs[nb:2 * nb]
        send_sems, recv_sems = refs[2 * nb:]
        x, y, c = _mesh_pos()
        sib = (x, y, 1 - c)
        sends = []
        for t, (_, src, dst) in enumerate(descs):
            for j, (px, py) in enumerate(_other_chips(x, y)):
                got = dst(outs[t], 2 * px + py, c)
                sends.append(_remote(got, got, send_sems.at[3 * t + j], recv_sems.at[3 * t + j], sib))
        for cp in sends:
            cp.start()
        for t, (_, src, dst) in enumerate(descs):
            for j, (px, py) in enumerate(_other_chips(x, y)):
                theirs = dst(outs[t], 2 * px + py, 1 - c)
                _remote(theirs, theirs, send_sems.at[3 * t + j], recv_sems.at[3 * t + j], sib).wait_recv()
        for cp in sends:
            cp.wait_send()

    return pl.pallas_call(
        body, name="gather_forward", in_specs=[ANY] * nb, out_specs=[ANY] * nb,
        out_shape=[jax.ShapeDtypeStruct(b.shape, b.dtype) for b in bufs], input_output_aliases={t: t for t in range(nb)},
        scratch_shapes=[pltpu.SemaphoreType.DMA((3 * nb,)), pltpu.SemaphoreType.DMA((3 * nb,))],
    )(*bufs)


def _remote(src, dst, ssem, rsem, to):
    return pltpu.make_async_remote_copy(src_ref=src, dst_ref=dst, send_sem=ssem, recv_sem=rsem, device_id=to, device_id_type=MESH)


def _rs_send_sibling(gs, tag):
    n = len(gs)
    counts = [N_CHIPS if g.ndim == 4 else 1 for g in gs]
    ns = sum(counts)

    def body(*refs):
        ins, outs = refs[:n], refs[n:2 * n]
        send_sems, recv_sems = refs[2 * n:]
        x, y, c = _mesh_pos()
        cps, s = [], 0
        for t in range(n):
            if counts[t] == 1:
                cps.append(_remote(ins[t].at[1 - c], outs[t], send_sems.at[s], recv_sems.at[s], (x, y, 1 - c)))
                s += 1
            else:
                for k in range(N_CHIPS):
                    cps.append(_remote(ins[t].at[k, 1 - c], outs[t].at[k], send_sems.at[s], recv_sems.at[s], (x, y, 1 - c)))
                    s += 1
        for cp in cps:
            cp.start()
        for cp in cps:
            cp.wait()

    out_shape = [jax.ShapeDtypeStruct(g.shape[:1] + g.shape[2:] if g.ndim == 4 else g.shape[1:], g.dtype) for g in gs]
    return pl.pallas_call(
        body, name=f"rs_send_sibling_{tag}", in_specs=[ANY] * n, out_specs=[ANY] * n, out_shape=out_shape,
        scratch_shapes=[pltpu.SemaphoreType.DMA((ns,)), pltpu.SemaphoreType.DMA((ns,))],
    )(*gs)


def _add_piece(g, recv, c, name):
    P, Q = g.shape[-2:]

    def body(c_ref, g_ref, r_ref, o_ref):
        o_ref[...] = g_ref[...].reshape(o_ref.shape) + r_ref[...]

    if g.ndim == 4:
        grid = (N_CHIPS,)
        in_specs = [pl.BlockSpec((1, 1, P, Q), lambda k, c_ref: (k, c_ref[0], 0, 0)), pl.BlockSpec((1, P, Q), lambda k, c_ref: (k, 0, 0))]
        out_spec = pl.BlockSpec((1, P, Q), lambda k, c_ref: (k, 0, 0))
    else:
        grid = (1,)
        in_specs = [pl.BlockSpec((1, P, Q), lambda k, c_ref: (c_ref[0], 0, 0)), pl.BlockSpec((P, Q), lambda k, c_ref: (0, 0))]
        out_spec = pl.BlockSpec((P, Q), lambda k, c_ref: (0, 0))
    return pl.pallas_call(
        body, name=name,
        grid_spec=pltpu.PrefetchScalarGridSpec(num_scalar_prefetch=1, grid=grid, in_specs=in_specs, out_specs=out_spec),
        out_shape=jax.ShapeDtypeStruct(recv.shape, g.dtype),
        compiler_params=_cp(1),
    )(c, g, recv)


HBM = pl.BlockSpec(memory_space=pltpu.HBM)
SEM = pl.BlockSpec(memory_space=pltpu.SEMAPHORE)
DATAFLOW = pltpu.SideEffectType.DATAFLOW_SIDE_EFFECTING


def _chips_start(hs, tag):
    n = len(hs)
    lands = [pltpu.with_memory_space_constraint(lax.empty((N_CHIPS,) + h.shape[-2:], h.dtype), pltpu.HBM) for h in hs]

    def body(*refs):
        ins, lnd = refs[:n], refs[n:2 * n]
        send_sems, recv_sems, token = refs[2 * n], refs[2 * n + 1], refs[-1]
        x, y, c = _mesh_pos()
        k = 2 * x + y
        piece = lambda t, kk: ins[t].at[kk] if hs[t].ndim == 3 else ins[t]
        for t in range(n):
            for j, (px, py) in enumerate(_other_chips(x, y)):
                _remote(piece(t, 2 * px + py), lnd[t].at[k], send_sems.at[3 * t + j], recv_sems.at[3 * t + j], (px, py, c)).start()
        token[...] = jnp.zeros_like(token)

    out = pl.pallas_call(
        body, name=f"chips_start_{tag}",
        out_shape=(pltpu.SemaphoreType.DMA((3 * n,)), pltpu.SemaphoreType.DMA((3 * n,)),
                   *[pltpu.HBM(h.shape, h.dtype) for h in hs], *[pltpu.HBM(l.shape, l.dtype) for l in lands],
                   jax.ShapeDtypeStruct((SUBLANES, LANES), F32)),
        in_specs=[HBM] * (2 * n), out_specs=(SEM, SEM, *[HBM] * (2 * n), pl.BlockSpec(memory_space=pltpu.VMEM)),
        input_output_aliases={i: 2 + i for i in range(2 * n)},
        compiler_params=pltpu.CompilerParams(has_side_effects=DATAFLOW),
    )(*[pltpu.with_memory_space_constraint(h, pltpu.HBM) for h in hs], *lands)
    return out[0], out[1], list(out[2:2 + n]), list(out[2 + n:2 + 2 * n]), out[-1]


def _chips_wait(send_sems, recv_sems, hs, lands, after, tag):
    n = len(hs)

    def body(*refs):
        ins, lnd = refs[:n], refs[n:2 * n]
        ssem, rsem = refs[2 * n], refs[2 * n + 1]
        x, y, c = _mesh_pos()
        k = 2 * x + y
        piece = lambda t, kk: ins[t].at[kk] if hs[t].ndim == 3 else ins[t]
        for t in range(n):
            for j, (px, py) in enumerate(_other_chips(x, y)):
                cp = _remote(piece(t, k), lnd[t].at[2 * px + py], ssem.at[3 * t + j], rsem.at[3 * t + j], (px, py, c))
                cp.wait_send()
                cp.wait_recv()

    out = pl.pallas_call(
        body, name=f"chips_wait_{tag}",
        out_shape=(*[pltpu.HBM(h.shape, h.dtype) for h in hs], *[pltpu.HBM(l.shape, l.dtype) for l in lands]),
        in_specs=[HBM] * (2 * n) + [SEM, SEM, ANY], out_specs=tuple([HBM] * (2 * n)),
        input_output_aliases={i: i for i in range(2 * n)},
        compiler_params=pltpu.CompilerParams(has_side_effects=DATAFLOW),
    )(*hs, *lands, send_sems, recv_sems, after)
    return list(out[:n]), list(out[n:])


def _add_chips(p, own, kc, name):
    _, P, Q = p.shape
    tr = P
    while N_CHIPS * tr * Q * 4 > 6 * 1024 * 1024 and tr % 16 == 0:
        tr //= 2
    sharded = own.ndim == 3

    def body(kc_ref, p_ref, own_ref, o_ref):
        k = kc_ref[0]
        mine = own_ref[...].reshape(tr, Q)
        v = [jnp.where(k == j, mine, p_ref[j]) for j in range(N_CHIPS)]
        o_ref[0] = ((v[0] + v[1]) + v[2]) + v[3]

    own_spec = (pl.BlockSpec((1, tr, Q), lambda i, kc_ref: (kc_ref[0], i, 0)) if sharded
                else pl.BlockSpec((tr, Q), lambda i, kc_ref: (i, 0)))
    return pl.pallas_call(
        body, name=name,
        grid_spec=pltpu.PrefetchScalarGridSpec(
            num_scalar_prefetch=1, grid=(P // tr,),
            in_specs=[pl.BlockSpec((N_CHIPS, tr, Q), lambda i, kc_ref: (0, i, 0)), own_spec],
            out_specs=pl.BlockSpec((1, tr, Q), lambda i, kc_ref: (kc_ref[1], i, 0))),
        out_shape=jax.ShapeDtypeStruct((2, P, Q), p.dtype),
        compiler_params=_cp(1),
    )(kc, p, own)


def _rs_share(fs, tag):
    n = len(fs)

    def body(*refs):
        outs = refs[n:2 * n]
        send_sems, recv_sems = refs[2 * n:]
        x, y, c = _mesh_pos()
        sends = [_remote(outs[t].at[c], outs[t].at[c], send_sems.at[t], recv_sems.at[t], (x, y, 1 - c)) for t in range(n)]
        for cp in sends:
            cp.start()
        for t in range(n):
            _remote(outs[t].at[c], outs[t].at[1 - c], send_sems.at[t], recv_sems.at[t], (x, y, 1 - c)).wait_recv()
        for cp in sends:
            cp.wait_send()

    return pl.pallas_call(
        body, name=f"rs_share_{tag}", in_specs=[ANY] * n, out_specs=[ANY] * n,
        out_shape=[jax.ShapeDtypeStruct(f.shape, f.dtype) for f in fs], input_output_aliases={t: t for t in range(n)},
        scratch_shapes=[pltpu.SemaphoreType.DMA((n,)), pltpu.SemaphoreType.DMA((n,))],
    )(*fs)


def _reduce_start(gs, kc, tag):
    from_sibling = _rs_send_sibling(gs, tag)
    chip_sums = [_add_piece(g, r, kc[1:], name=f"add_piece_{tag}_{t}") for t, (g, r) in enumerate(zip(gs, from_sibling))]
    send_sems, recv_sems, chip_sums, lands, token = _chips_start(chip_sums, tag)
    return (send_sems, recv_sems, chip_sums, lands, tag), token


def _reduce_finish(states, kc, after):
    mine = []
    for send_sems, recv_sems, chip_sums, lands, tag in states:
        chip_sums, from_chips = _chips_wait(send_sems, recv_sems, chip_sums, lands, after, tag)
        mine += [_add_chips(p, h, kc, name=f"add_chips_{tag}_{t}") for t, (p, h) in enumerate(zip(from_chips, chip_sums))]
    return _rs_share(mine, "all")


def _adamw(w, g, m, v, name):
    R, C = w.shape
    tr = R
    for cand in (512, 256, 128, 64, 32, 16, 8):
        if R % cand == 0 and cand * C * 4 <= 2 * 1024 * 1024:
            tr = cand
            break
    c1 = 1.0 / (1.0 - ADAM_B1 ** ADAM_STEP)
    c2 = 1.0 / (1.0 - ADAM_B2 ** ADAM_STEP)

    def body(w_ref, g_ref, m_ref, v_ref, d_ref, mo_ref, vo_ref):
        gv = g_ref[...]
        mn = ADAM_B1 * m_ref[...] + (1.0 - ADAM_B1) * gv
        vn = ADAM_B2 * v_ref[...] + (1.0 - ADAM_B2) * (gv * gv)
        mo_ref[...] = mn
        vo_ref[...] = vn
        d_ref[...] = -ADAM_LR * ((mn * c1) / (jnp.sqrt(vn * c2) + ADAM_EPS) + ADAM_WD * w_ref[...])

    spec = pl.BlockSpec((tr, C), lambda i: (i, 0))
    shp = jax.ShapeDtypeStruct((R, C), F32)
    return pl.pallas_call(body, name=name, grid=(R // tr,), in_specs=[spec] * 4, out_specs=[spec] * 3, out_shape=[shp] * 3,
                          compiler_params=_cp(1))(w, g, m, v)


def _step(a):
    x, y, c = _mesh_pos()
    kc = jnp.stack([2 * x + y, c]).astype(jnp.int32)

    rs = _rows(sum(a[n].size for n in SH_SMALL))
    first, later = ["ev_w_in", "ev_w_out"], ["od_w_in", "od_w_out", "ffn_w_up", "ffn_w_down"]
    lead = lambda w: w if w.ndim == 3 else w[None]
    *full, gs = _gather_weights(first, [a[n].astype(BF16) for n in first], _pack([a[n] for n in SH_SMALL], rs, F32))
    p = {n: a[n] for n in REP}
    p.update({n: lead(w) for n, w in zip(first, full)})
    parts = [_unpack(gs[k], [a[n].shape for n in SH_SMALL]) for k in range(N_CHIPS)]
    for i, n in enumerate(SH_SMALL):
        p[n] = jnp.concatenate([parts[k][i] for k in range(N_CHIPS)], axis=SH_SMALL[n])
    shards = lax.optimization_barrier(([a[n].astype(BF16) for n in later], full))[0]
    g_send, g_recv, shards, bufs, token = _gather_start(later, shards, _place_own(later, shards))
    p["norm_mix"] = p["norm_mix"] + token[0:1, 0:1]

    def late(after):
        got = _gather_forward(later, _gather_wait(later, g_send, g_recv, shards, bufs, after))
        return {n: lead(w) for n, w in zip(later, got)}

    p["late"] = late

    states = []

    def start_reduce(tag, gs):
        state, token = _reduce_start(gs, kc, tag)
        states.append(state)
        return token

    sq8, grad_x, grads, big = _local_step(a["x"][0], a["loss_target"][0], p, start_reduce)
    loss = lax.psum(0.5 / D_MODEL * jnp.sum(sq8), ("x", "y", "c"))

    r_s = _rows(sum(a[n].size for n in SH_SMALL), 2 * SUBLANES) // 2
    small_pieces = []
    for k in range(N_CHIPS):
        pieces = [lax.slice_in_dim(grads[n], k * a[n].shape[ax], (k + 1) * a[n].shape[ax], axis=ax) for n, ax in SH_SMALL.items()]
        small_pieces.append(_pack(pieces, 2 * r_s, F32).reshape(2, r_s, LANES))
    g_small = jnp.stack(small_pieces)
    r_r = _rows(sum(a[n].size for n in REP), 2 * SUBLANES) // 2
    g_rep = _pack([grads[n] for n in REP], 2 * r_r, F32).reshape(2, r_r, LANES)
    start_reduce("g4", [g_small, g_rep])
    reduced = _reduce_finish(states, kc, grad_x)
    red = dict(zip([key for tag in ("g1", "g2", "g3") for key in REDUCE_GROUPS[tag]], reduced))
    gfin = {}
    for n in ("ev_w_in", "ev_w_out", "od_w_in", "od_w_out"):
        gfin[n] = red[n, 0].reshape(a[n].shape)
    for n in ("ffn_w_up", "ffn_w_down"):
        gfin[n] = jnp.stack([red[n, l].reshape(a[n].shape[1:]) for l in range(2)])
    gfin.update(zip(SH_SMALL, _unpack(reduced[-2], [a[n].shape for n in SH_SMALL])))
    gfin.update(zip(REP, _unpack(reduced[-1], [a[n].shape for n in REP])))

    out = {"loss": loss, "grad_x": grad_x[None]}
    small_names = list(SH_SMALL) + REP
    for n in SH_BIG:
        shp = a[n].shape
        two_d = lambda t: t.reshape(-1, shp[-1])
        d, mo, vo = _adamw(two_d(a[n]), two_d(gfin[n]), two_d(a["m_" + n]), two_d(a["v_" + n]), name=f"adamw_{n}")
        out["delta_" + n], out["new_m_" + n], out["new_v_" + n] = d.reshape(shp), mo.reshape(shp), vo.reshape(shp)
    r_small = _rows(sum(a[n].size for n in small_names), 512)
    packs = [_pack([src(n) for n in small_names], r_small, F32)
             for src in (lambda n: a[n], lambda n: gfin[n], lambda n: a["m_" + n], lambda n: a["v_" + n])]
    d, mo, vo = _adamw(*packs, name="adamw_small")
    shapes = [a[n].shape for n in small_names]
    for n, dd, mm, vv in zip(small_names, _unpack(d, shapes), _unpack(mo, shapes), _unpack(vo, shapes)):
        out["delta_" + n], out["new_m_" + n], out["new_v_" + n] = dd, mm, vv
    for n in WEIGHTS:
        out["grad_" + n] = gfin[n]
    return out


def kernel(x, norm_mix, norm_ffn, norm_final, ev_w_in, ev_conv_w, ev_conv_b, ev_gate_a_w, ev_gate_a_b, ev_gate_x_w, ev_gate_x_b, ev_lru_lambda, hg_lb_logits, ev_hg_norm, ev_w_out, od_w_in, od_b_in, od_ln_g, od_ln_b, od_w_s, od_b_s, od_w_out, ffn_w_up, ffn_conv_w, ffn_conv_b, ffn_w_down, loss_target, m_norm_mix, m_norm_ffn, m_norm_final, m_ev_w_in, m_ev_conv_w, m_ev_conv_b, m_ev_gate_a_w, m_ev_gate_a_b, m_ev_gate_x_w, m_ev_gate_x_b, m_ev_lru_lambda, m_hg_lb_logits, m_ev_hg_norm, m_ev_w_out, m_od_w_in, m_od_b_in, m_od_ln_g, m_od_ln_b, m_od_w_s, m_od_b_s, m_od_w_out, m_ffn_w_up, m_ffn_conv_w, m_ffn_conv_b, m_ffn_w_down, v_norm_mix, v_norm_ffn, v_norm_final, v_ev_w_in, v_ev_conv_w, v_ev_conv_b, v_ev_gate_a_w, v_ev_gate_a_b, v_ev_gate_x_w, v_ev_gate_x_b, v_ev_lru_lambda, v_hg_lb_logits, v_ev_hg_norm, v_ev_w_out, v_od_w_in, v_od_b_in, v_od_ln_g, v_od_ln_b, v_od_w_s, v_od_b_s, v_od_w_out, v_ffn_w_up, v_ffn_conv_w, v_ffn_conv_b, v_ffn_w_down):
    vals = (x, norm_mix, norm_ffn, norm_final, ev_w_in, ev_conv_w, ev_conv_b, ev_gate_a_w, ev_gate_a_b, ev_gate_x_w, ev_gate_x_b, ev_lru_lambda, hg_lb_logits, ev_hg_norm, ev_w_out, od_w_in, od_b_in, od_ln_g, od_ln_b, od_w_s, od_b_s, od_w_out, ffn_w_up, ffn_conv_w, ffn_conv_b, ffn_w_down, loss_target, m_norm_mix, m_norm_ffn, m_norm_final, m_ev_w_in, m_ev_conv_w, m_ev_conv_b, m_ev_gate_a_w, m_ev_gate_a_b, m_ev_gate_x_w, m_ev_gate_x_b, m_ev_lru_lambda, m_hg_lb_logits, m_ev_hg_norm, m_ev_w_out, m_od_w_in, m_od_b_in, m_od_ln_g, m_od_ln_b, m_od_w_s, m_od_b_s, m_od_w_out, m_ffn_w_up, m_ffn_conv_w, m_ffn_conv_b, m_ffn_w_down, v_norm_mix, v_norm_ffn, v_norm_final, v_ev_w_in, v_ev_conv_w, v_ev_conv_b, v_ev_gate_a_w, v_ev_gate_a_b, v_ev_gate_x_w, v_ev_gate_x_b, v_ev_lru_lambda, v_hg_lb_logits, v_ev_hg_norm, v_ev_w_out, v_od_w_in, v_od_b_in, v_od_ln_g, v_od_ln_b, v_od_w_s, v_od_b_s, v_od_w_out, v_ffn_w_up, v_ffn_conv_w, v_ffn_conv_b, v_ffn_w_down)
    names = ["x"] + WEIGHTS + ["loss_target"] + ["m_" + n for n in WEIGHTS] + ["v_" + n for n in WEIGHTS]
    out = _step(dict(zip(names, vals)))
    return (out["loss"], out["grad_x"], *[out["grad_" + n] for n in WEIGHTS], *[out["delta_" + n] for n in WEIGHTS],
            *[out["new_m_" + n] for n in WEIGHTS], *[out["new_v_" + n] for n in WEIGHTS])
```

```python
import functools

import jax
import jax.numpy as jnp
from jax import lax
from jax.experimental import pallas as pl
from jax.experimental.pallas import tpu as pltpu

F32 = jnp.float32
BF16 = jnp.bfloat16

EPS = 1e-6
D_MODEL = 1024
LRU_W = 512
LRU_BLOCKS = 8
LRU_C = 8.0
HG_HEADS = 4
HG_D = 128
HG_CHUNK = 64
SGU_G = 8
SGU_CHUNK = 128
D_FF = 2816
ADAM_LR, ADAM_B1, ADAM_B2, ADAM_EPS, ADAM_WD, ADAM_STEP = 0.001, 0.9, 0.999, 1e-08, 0.01, 10

V7X_VMEM_BYTES = 64 * 1024 * 1024
VMEM_LIMIT = V7X_VMEM_BYTES - 8 * 1024 * 1024
SUBLANES = 8
LANES = 128
BF16_ROWS = 16

GELU_C0 = 0.7978845608028654
GELU_C1 = 0.044715

NN = (((1,), (0,)), ((), ()))
NT = (((1,), (1,)), ((), ()))
TN = (((0,), (0,)), ((), ()))


def _dot(a, b, dims=NN):
    return lax.dot_general(a.astype(BF16), b.astype(BF16), dims, preferred_element_type=F32)


def _cp(n_grid):
    return pltpu.CompilerParams(dimension_semantics=("arbitrary",) * n_grid, vmem_limit_bytes=VMEM_LIMIT)


def _chunk(n, cap):
    best = LANES
    for c in range(LANES, cap + 1, LANES):
        if n % c == 0:
            best = c
    return best


def _resident(shape):
    nd = len(shape)
    return pl.BlockSpec(shape, lambda *_: (0,) * nd, pipeline_mode=pl.Buffered(1))


def _rsum8(x):
    r, c = x.shape
    return x.reshape(r // SUBLANES, SUBLANES, c).sum(axis=0)


def _sigmoid(x):
    return 0.5 * jnp.tanh(0.5 * x) + 0.5


def _gelu(x):
    return 0.5 * x * (1.0 + jnp.tanh(GELU_C0 * (x + GELU_C1 * x * x * x)))


def _gelu_grad(x):
    t = jnp.tanh(GELU_C0 * (x + GELU_C1 * x * x * x))
    return 0.5 * (1.0 + t) + 0.5 * x * (1.0 - t * t) * GELU_C0 * (1.0 + 3.0 * GELU_C1 * x * x)


def _silu_and_grad(x):
    s = _sigmoid(x)
    return x * s, s * (1.0 + x * (1.0 - s))


def _shift_rows(e, j):
    n = e.shape[0]
    return e if j % n == 0 else pltpu.roll(e, j % n, 0)


def _norm_mm(h, g, w, b, name, tt=1024):
    T, D = h.shape
    N = w.shape[1]
    cn = _chunk(N, 512)

    def body(h_ref, g_ref, w_ref, b_ref, hn_ref, z_ref):
        x = h_ref[...]
        r = lax.rsqrt(jnp.mean(x * x, axis=-1, keepdims=True) + EPS)
        hn = (x * r * g_ref[...]).astype(BF16)
        hn_ref[...] = hn
        for j in range(0, N, cn):
            acc = jnp.dot(hn, w_ref[:, j:j + cn], preferred_element_type=F32) + b_ref[:, j:j + cn]
            z_ref[:, j:j + cn] = acc.astype(BF16)

    return pl.pallas_call(
        body, name=name, grid=(T // tt,),
        in_specs=[pl.BlockSpec((tt, D), lambda i: (i, 0)), _resident((1, D)), _resident((D, N)), _resident((1, N))],
        out_specs=[pl.BlockSpec((tt, D), lambda i: (i, 0)), pl.BlockSpec((tt, N), lambda i: (i, 0))],
        out_shape=[jax.ShapeDtypeStruct((T, D), BF16), jax.ShapeDtypeStruct((T, N), BF16)],
        compiler_params=_cp(1),
    )(h, g, w, b)


def _mm(a, w, res, out_dtype, name, tt=1024, transpose_w=False):
    T, K = a.shape
    N = w.shape[0] if transpose_w else w.shape[1]
    cn = _chunk(N, 512)
    has_res = res is not None

    def body(*refs):
        a_ref, w_ref = refs[0], refs[1]
        res_ref = refs[2] if has_res else None
        o_ref = refs[-1]
        av = a_ref[...].astype(BF16)
        for j in range(0, N, cn):
            if transpose_w:
                acc = lax.dot_general(av, w_ref[j:j + cn, :], NT, preferred_element_type=F32)
            else:
                acc = jnp.dot(av, w_ref[:, j:j + cn], preferred_element_type=F32)
            if has_res:
                acc = acc + res_ref[:, j:j + cn]
            o_ref[:, j:j + cn] = acc.astype(out_dtype)

    in_specs = [pl.BlockSpec((tt, K), lambda i: (i, 0)), _resident(w.shape)]
    args = [a, w]
    if has_res:
        in_specs.append(pl.BlockSpec((tt, N), lambda i: (i, 0)))
        args.append(res)
    return pl.pallas_call(
        body, name=name, grid=(T // tt,), in_specs=in_specs,
        out_specs=pl.BlockSpec((tt, N), lambda i: (i, 0)),
        out_shape=jax.ShapeDtypeStruct((T, N), out_dtype),
        compiler_params=_cp(1),
    )(*args)


def _mm_tn(a, b, name, col_shards=1, tt=2048):
    T, K = a.shape
    N = b.shape[1]
    ns = N // col_shards
    tt = min(tt, T)
    while 2 * (tt * K * a.dtype.itemsize + tt * ns * b.dtype.itemsize + K * ns * 4) > VMEM_LIMIT - 12 * 1024 * 1024:
        tt //= 2

    def body(a_ref, b_ref, o_ref):
        acc = lax.dot_general(a_ref[...].astype(BF16), b_ref[...].astype(BF16), TN, preferred_element_type=F32)
        prev = jnp.where(pl.program_id(1) == 0, 0.0, o_ref[0])
        o_ref[0] = prev + acc

    out = pl.pallas_call(
        body, name=name, grid=(col_shards, T // tt),
        in_specs=[pl.BlockSpec((tt, K), lambda n, t: (t, 0)), pl.BlockSpec((tt, ns), lambda n, t: (t, n))],
        out_specs=pl.BlockSpec((1, K, ns), lambda n, t: (n, 0, 0)),
        out_shape=jax.ShapeDtypeStruct((col_shards, K, ns), F32),
        compiler_params=_cp(2),
    )(a, b)
    return out if col_shards > 1 else out[0]


def _mm_normbwd(dz, w, x, g, dres, name):
    T, N = dz.shape
    D = w.shape[0]
    tt = 1024 if N <= 3072 else 512

    def body(dz_ref, wt_ref, x_ref, g_ref, dres_ref, dx_ref, dg_ref):
        @pl.when(pl.program_id(0) == 0)
        def _():
            dg_ref[...] = jnp.zeros_like(dg_ref)

        dy = lax.dot_general(dz_ref[...], wt_ref[...], NT, preferred_element_type=F32)
        x = x_ref[...]
        r = lax.rsqrt(jnp.mean(x * x, axis=-1, keepdims=True) + EPS)
        xn = x * r
        dg_ref[...] += _rsum8(dy * xn)
        dxn = dy * g_ref[...]
        dx_ref[...] = dres_ref[...] + r * (dxn - xn * jnp.mean(dxn * xn, axis=-1, keepdims=True))

    return pl.pallas_call(
        body, name=name, grid=(T // tt,),
        in_specs=[pl.BlockSpec((tt, N), lambda i: (i, 0)), _resident((D, N)), pl.BlockSpec((tt, D), lambda i: (i, 0)),
                  _resident((1, D)), pl.BlockSpec((tt, D), lambda i: (i, 0))],
        out_specs=[pl.BlockSpec((tt, D), lambda i: (i, 0)), pl.BlockSpec((SUBLANES, D), lambda i: (0, 0))],
        out_shape=[jax.ShapeDtypeStruct((T, D), F32), jax.ShapeDtypeStruct((SUBLANES, D), F32)],
        compiler_params=_cp(1),
    )(dz, w, x, g, dres)


def _final_loss(h, g, tgt, name="final_loss", tt=512):
    T, D = h.shape

    def body(h_ref, g_ref, t_ref, dh_ref, dg_ref, sq_ref):
        @pl.when(pl.program_id(0) == 0)
        def _():
            dg_ref[...] = jnp.zeros_like(dg_ref)
            sq_ref[...] = jnp.zeros_like(sq_ref)

        x = h_ref[...]
        r = lax.rsqrt(jnp.mean(x * x, axis=-1, keepdims=True) + EPS)
        xn = x * r
        gv = g_ref[...]
        diff = xn * gv - t_ref[...]
        sq_ref[...] += _rsum8(diff * diff)
        dy = diff * (1.0 / D)
        dg_ref[...] += _rsum8(dy * xn)
        dxn = dy * gv
        dh_ref[...] = r * (dxn - xn * jnp.mean(dxn * xn, axis=-1, keepdims=True))

    return pl.pallas_call(
        body, name=name, grid=(T // tt,),
        in_specs=[pl.BlockSpec((tt, D), lambda i: (i, 0)), _resident((1, D)), pl.BlockSpec((tt, D), lambda i: (i, 0))],
        out_specs=[pl.BlockSpec((tt, D), lambda i: (i, 0)), pl.BlockSpec((SUBLANES, D), lambda i: (0, 0)),
                   pl.BlockSpec((SUBLANES, D), lambda i: (0, 0))],
        out_shape=[jax.ShapeDtypeStruct((T, D), F32), jax.ShapeDtypeStruct((SUBLANES, D), F32),
                   jax.ShapeDtypeStruct((SUBLANES, D), F32)],
        compiler_params=_cp(1),
    )(h, g, tgt)


def _col_groups(F, cc, per_group=4):
    step = cc * per_group
    return [(g0, min(g0 + step, F)) for g0 in range(0, F, step)]


def _ffn_act(gu, cw, cb, w_down, res, name, tt=512):
    T = gu.shape[0]
    F = gu.shape[1] // 2
    D = w_down.shape[1]
    cc = _chunk(F, 256)
    hb = tt // BF16_ROWS

    def body(gu_ref, halo_ref, cw_ref, cb_ref, wd_ref, res_ref, a_ref, gc_ref, o_ref):
        first = pl.program_id(0) == 0
        acc = res_ref[...]
        for g0, g1 in _col_groups(F, cc):
            for c0 in range(g0, g1, cc):
                cs = slice(c0, c0 + cc)
                x = gu_ref[:, cs].astype(F32)
                halo = jnp.where(first, 0.0, halo_ref[:, cs].astype(F32))
                e = jnp.concatenate([halo, x], axis=0)
                gc = (cb_ref[:, cs] + cw_ref[0:1, cs] * _shift_rows(e, 2)[BF16_ROWS:] + cw_ref[1:2, cs] * _shift_rows(e, 1)[BF16_ROWS:]
                      + cw_ref[2:3, cs] * x)
                up = gu_ref[:, F + c0:F + c0 + cc].astype(F32)
                gc_ref[:, cs] = gc.astype(BF16)
                a_ref[:, cs] = (gc * _sigmoid(gc) * up).astype(BF16)
            acc = acc + jnp.dot(a_ref[:, g0:g1], wd_ref[g0:g1, :], preferred_element_type=F32)
        o_ref[...] = acc

    return pl.pallas_call(
        body, name=name, grid=(T // tt,),
        in_specs=[pl.BlockSpec((tt, 2 * F), lambda i: (i, 0)),
                  pl.BlockSpec((BF16_ROWS, F), lambda i: (jnp.maximum(i * hb - 1, 0), 0)),
                  _resident((SUBLANES, F)), _resident((1, F)), _resident((F, D)), pl.BlockSpec((tt, D), lambda i: (i, 0))],
        out_specs=[pl.BlockSpec((tt, F), lambda i: (i, 0)), pl.BlockSpec((tt, F), lambda i: (i, 0)),
                   pl.BlockSpec((tt, D), lambda i: (i, 0))],
        out_shape=[jax.ShapeDtypeStruct((T, F), BF16), jax.ShapeDtypeStruct((T, F), BF16), jax.ShapeDtypeStruct((T, D), F32)],
        compiler_params=_cp(1),
    )(gu, gu, cw, cb, w_down, res)


def _ffn_act_bwd(gu, gc, da, cw, w_up, x, g, dres, name, tt=256):
    T = gu.shape[0]
    F = gu.shape[1] // 2
    D = w_up.shape[0]
    cc = _chunk(F, 256)
    hb = tt // BF16_ROWS
    last_hb = T // BF16_ROWS - 1
    nt = T // tt

    def body(gu_ref, upnext_ref, gc_ref, gcnext_ref, da_ref, danext_ref, cw_ref, wu_ref, x_ref, g_ref, dres_ref,
             dgu_ref, dc_ref, dx_ref, dg_ref):
        i = pl.program_id(0)

        @pl.when(i == 0)
        def _():
            dc_ref[...] = jnp.zeros_like(dc_ref)
            dg_ref[...] = jnp.zeros_like(dg_ref)

        n = tt + BF16_ROWS
        ext = lambda main, nxt: jnp.concatenate([main.astype(F32), nxt.astype(F32)], axis=0)
        dy = jnp.zeros((tt, D), F32)
        for lo, hi in _col_groups(F, cc):
            for c0 in range(lo, hi, cc):
                cs = slice(c0, c0 + cc)
                us = slice(F + c0, F + c0 + cc)
                gc = ext(gc_ref[:, cs], gcnext_ref[:, cs])
                up = ext(gu_ref[:, us], upnext_ref[:, cs])
                dae = ext(da_ref[:, cs], jnp.where(i == nt - 1, 0.0, danext_ref[:, cs].astype(F32)))
                s, ds = _silu_and_grad(gc)
                dgc = dae * up * ds
                dgu_ref[:, us] = (dae * s)[:tt].astype(BF16)
                dgc1 = _shift_rows(dgc, n - 1)[:tt]
                dgc2 = _shift_rows(dgc, n - 2)[:tt]
                dm = dgc[:tt]
                dgu_ref[:, cs] = (cw_ref[2:3, cs] * dm + cw_ref[1:2, cs] * dgc1 + cw_ref[0:1, cs] * dgc2).astype(BF16)
                gt = gu_ref[:, cs].astype(F32)
                dc_ref[0, :, cs] += _rsum8(dgc2 * gt)
                dc_ref[1, :, cs] += _rsum8(dgc1 * gt)
                dc_ref[2, :, cs] += _rsum8(dm * gt)
                dc_ref[3, :, cs] += _rsum8(dm)
            dy = (dy + lax.dot_general(dgu_ref[:, lo:hi], wu_ref[:, lo:hi], NT, preferred_element_type=F32)
                  + lax.dot_general(dgu_ref[:, F + lo:F + hi], wu_ref[:, F + lo:F + hi], NT, preferred_element_type=F32))
        xv = x_ref[...]
        r = lax.rsqrt(jnp.mean(xv * xv, axis=-1, keepdims=True) + EPS)
        xn = xv * r
        dg_ref[...] += _rsum8(dy * xn)
        dxn = dy * g_ref[...]
        dx_ref[...] = dres_ref[...] + r * (dxn - xn * jnp.mean(dxn * xn, axis=-1, keepdims=True))

    tok = lambda w: pl.BlockSpec((tt, w), lambda i: (i, 0))
    nxt = lambda col: pl.BlockSpec((BF16_ROWS, F), lambda i: (jnp.minimum((i + 1) * hb, last_hb), col))
    return pl.pallas_call(
        body, name=name, grid=(nt,),
        in_specs=[tok(2 * F), nxt(1), tok(F), nxt(0), tok(F), nxt(0),
                  _resident((SUBLANES, F)), _resident((D, 2 * F)), tok(D), _resident((1, D)), tok(D)],
        out_specs=[tok(2 * F), pl.BlockSpec((4, SUBLANES, F), lambda i: (0, 0, 0)), tok(D),
                   pl.BlockSpec((SUBLANES, D), lambda i: (0, 0))],
        out_shape=[jax.ShapeDtypeStruct((T, 2 * F), BF16), jax.ShapeDtypeStruct((4, SUBLANES, F), F32),
                   jax.ShapeDtypeStruct((T, D), F32), jax.ShapeDtypeStruct((SUBLANES, D), F32)],
        compiler_params=_cp(1),
    )(gu, gu, gc, gc, da, da, cw, w_up, x, g, dres)


def _softplus_neg(lam):
    x = -lam
    y = jnp.exp(-jnp.abs(x))
    l1p = jnp.where(y < 0.01, y * (1.0 - y * (0.5 - y * (1.0 / 3.0))), jnp.log(1.0 + y))
    return jnp.maximum(x, 0.0) + l1p


def _lru_gates(xc, wa_ref, ba_ref, wx_ref, bx_ref, sp):
    xcb = xc.astype(BF16)
    r = _sigmoid(jnp.dot(xcb, wa_ref[...], preferred_element_type=F32) + ba_ref[...])
    gi = _sigmoid(jnp.dot(xcb, wx_ref[...], preferred_element_type=F32) + bx_ref[...])
    log_a = -LRU_C * r * sp
    a = jnp.exp(log_a)
    x2 = 2.0 * log_a
    series = -x2 * (1.0 + x2 * 0.5 * (1.0 + x2 * (1.0 / 3.0) * (1.0 + x2 * 0.25 * (1.0 + x2 * 0.2))))
    om = jnp.where(x2 > -0.125, series, 1.0 - a * a)
    return r, gi, a, jnp.sqrt(om)


def _lru_conv(xr, halo, cw_ref, cb_ref):
    e = jnp.concatenate([halo, xr], axis=0)
    x1 = _shift_rows(e, 1)[BF16_ROWS:]
    x2 = _shift_rows(e, 2)[BF16_ROWS:]
    x3 = _shift_rows(e, 3)[BF16_ROWS:]
    xc = cb_ref[...] + cw_ref[0:1, :] * x3 + cw_ref[1:2, :] * x2 + cw_ref[2:3, :] * x1 + cw_ref[3:4, :] * xr
    return xc, x1, x2, x3


def _lru_fwd(z, cw, cb, wa, ba, wx, bx, lam, name="lru_fwd", tt=256):
    T = z.shape[0]
    W = LRU_W
    hb = tt // BF16_ROWS
    ng = tt // SUBLANES

    def body(z_ref, halo_ref, cw_ref, cb_ref, wa_ref, ba_ref, wx_ref, bx_ref, lam_ref, oa_ref, h_ref, a_s, u_s, hc):
        i = pl.program_id(0)

        @pl.when(i == 0)
        def _():
            hc[...] = jnp.zeros_like(hc)

        xr = z_ref[:, W:2 * W].astype(F32)
        halo = jnp.where(i == 0, 0.0, halo_ref[...].astype(F32))
        xc, _, _, _ = _lru_conv(xr, halo, cw_ref, cb_ref)
        sp = _softplus_neg(lam_ref[...])
        r, gi, a, mult = _lru_gates(xc, wa_ref, ba_ref, wx_ref, bx_ref, sp)
        a_s[...] = a
        u_s[...] = mult * gi * xc
        row = lax.broadcasted_iota(jnp.int32, (SUBLANES, W), 0)

        def step(j, hprev):
            r0 = pl.multiple_of(j * SUBLANES, SUBLANES)
            A = a_s[pl.ds(r0, SUBLANES), :]
            U = u_s[pl.ds(r0, SUBLANES), :]
            for k in (1, 2, 4):
                m = row >= k
                U = jnp.where(m, A * pltpu.roll(U, k, 0) + U, U)
                A = jnp.where(m, A * pltpu.roll(A, k, 0), A)
            H = U + A * hprev
            h_ref[pl.ds(r0, SUBLANES), :] = H
            return jnp.broadcast_to(H[SUBLANES - 1:SUBLANES, :], (SUBLANES, W))

        hc[...] = lax.fori_loop(0, ng, step, hc[...])
        oa_ref[...] = (_gelu(z_ref[:, 0:W].astype(F32)) * h_ref[...]).astype(BF16)

    return pl.pallas_call(
        body, name=name, grid=(T // tt,),
        in_specs=[pl.BlockSpec((tt, 2 * W), lambda i: (i, 0)),
                  pl.BlockSpec((BF16_ROWS, W), lambda i: (jnp.maximum(i * hb - 1, 0), 1)),
                  _resident((SUBLANES, W)), _resident((1, W)), _resident((W, W)), _resident((1, W)),
                  _resident((W, W)), _resident((1, W)), _resident((1, W))],
        out_specs=[pl.BlockSpec((tt, W), lambda i: (i, 0)), pl.BlockSpec((tt, W), lambda i: (i, 0))],
        out_shape=[jax.ShapeDtypeStruct((T, W), BF16), jax.ShapeDtypeStruct((T, W), F32)],
        scratch_shapes=[pltpu.VMEM((tt, W), F32), pltpu.VMEM((tt, W), F32), pltpu.VMEM((SUBLANES, W), F32)],
        compiler_params=_cp(1),
    )(z, z, cw, cb, wa, ba, wx, bx, lam)


def _lru_bwd(z, hseq, dmix, cw, cb, wa, wat, ba, wx, wxt, bx, lam, name="lru_bwd", tt=256):
    T = z.shape[0]
    W = LRU_W
    nt = T // tt
    hb = tt // BF16_ROWS
    sb = tt // SUBLANES
    ng = tt // SUBLANES

    def body(z_ref, halo_ref, h_ref, hprev_ref, dm_ref, cw_ref, cb_ref, wa_ref, wat_ref, ba_ref, wx_ref, wxt_ref, bx_ref,
             lam_ref, dz_ref, dc_ref, dwa_ref, dwx_ref, dv_ref, c_s, d_s, g_s, gc, an, dxn):
        i = pl.program_id(0)
        ti = nt - 1 - i

        @pl.when(i == 0)
        def _():
            dc_ref[...] = jnp.zeros_like(dc_ref)
            dwa_ref[...] = jnp.zeros_like(dwa_ref)
            dwx_ref[...] = jnp.zeros_like(dwx_ref)
            dv_ref[...] = jnp.zeros_like(dv_ref)
            gc[...] = jnp.zeros_like(gc)
            an[...] = jnp.zeros_like(an)
            dxn[...] = jnp.zeros_like(dxn)

        xr = z_ref[:, W:2 * W].astype(F32)
        yg = z_ref[:, 0:W].astype(F32)
        halo = jnp.where(ti == 0, 0.0, halo_ref[...].astype(F32))
        xc, x1, x2, x3 = _lru_conv(xr, halo, cw_ref, cb_ref)
        sp = _softplus_neg(lam_ref[...])
        r, gi, a, mult = _lru_gates(xc, wa_ref, ba_ref, wx_ref, bx_ref, sp)
        h = h_ref[...]
        hp = jnp.where(ti == 0, 0.0, hprev_ref[...])
        hm1 = _shift_rows(jnp.concatenate([hp, h], axis=0), 1)[SUBLANES:]
        dout = dm_ref[...].astype(F32)
        d_s[...] = dout * _gelu(yg)
        dz_ref[:, 0:W] = (dout * h * _gelu_grad(yg)).astype(BF16)
        c_s[...] = _shift_rows(jnp.concatenate([a, an[...]], axis=0), tt + SUBLANES - 1)[:tt]
        an[...] = a[0:SUBLANES, :]
        row = lax.broadcasted_iota(jnp.int32, (SUBLANES, W), 0)

        def step(j, gnext):
            r0 = pl.multiple_of((ng - 1 - j) * SUBLANES, SUBLANES)
            C = c_s[pl.ds(r0, SUBLANES), :]
            G = d_s[pl.ds(r0, SUBLANES), :]
            for k in (1, 2, 4):
                m = row < SUBLANES - k
                G = jnp.where(m, G + C * pltpu.roll(G, SUBLANES - k, 0), G)
                C = jnp.where(m, C * pltpu.roll(C, SUBLANES - k, 0), C)
            G = G + C * gnext
            g_s[pl.ds(r0, SUBLANES), :] = G
            return jnp.broadcast_to(G[0:1, :], (SUBLANES, W))

        gc[...] = lax.fori_loop(0, ng, step, gc[...])
        du = g_s[...]
        da = du * hm1
        dgi = du * mult * xc
        dxc = du * mult * gi
        dmult = du * gi * xc
        dlog_a = da * a - dmult * (a * a) / mult
        dr = dlog_a * (-LRU_C * sp)
        dv_ref[2] += _rsum8(dlog_a * (-LRU_C * r))
        dpr = (dr * r * (1.0 - r)).astype(BF16)
        dpi = (dgi * gi * (1.0 - gi)).astype(BF16)
        dv_ref[0] += _rsum8(dpr.astype(F32))
        dv_ref[1] += _rsum8(dpi.astype(F32))
        xcb = xc.astype(BF16)
        dwa_ref[...] += lax.dot_general(xcb, dpr, TN, preferred_element_type=F32)
        dwx_ref[...] += lax.dot_general(xcb, dpi, TN, preferred_element_type=F32)
        dxc = dxc + jnp.dot(dpr, wat_ref[...], preferred_element_type=F32) + jnp.dot(dpi, wxt_ref[...], preferred_element_type=F32)
        n = tt + BF16_ROWS
        de = jnp.concatenate([dxc, dxn[...]], axis=0)
        dxr = (cw_ref[3:4, :] * dxc + cw_ref[2:3, :] * _shift_rows(de, n - 1)[:tt] + cw_ref[1:2, :] * _shift_rows(de, n - 2)[:tt]
               + cw_ref[0:1, :] * _shift_rows(de, n - 3)[:tt])
        dxn[...] = dxc[0:BF16_ROWS, :]
        dz_ref[:, W:2 * W] = dxr.astype(BF16)
        dc_ref[0] += _rsum8(dxc * x3)
        dc_ref[1] += _rsum8(dxc * x2)
        dc_ref[2] += _rsum8(dxc * x1)
        dc_ref[3] += _rsum8(dxc * xr)
        dc_ref[4] += _rsum8(dxc)

    rev = lambda i: nt - 1 - i
    return pl.pallas_call(
        body, name=name, grid=(nt,),
        in_specs=[pl.BlockSpec((tt, 2 * W), lambda i: (rev(i), 0)),
                  pl.BlockSpec((BF16_ROWS, W), lambda i: (jnp.maximum(rev(i) * hb - 1, 0), 1)),
                  pl.BlockSpec((tt, W), lambda i: (rev(i), 0)),
                  pl.BlockSpec((SUBLANES, W), lambda i: (jnp.maximum(rev(i) * sb - 1, 0), 0)),
                  pl.BlockSpec((tt, W), lambda i: (rev(i), 0)),
                  _resident((SUBLANES, W)), _resident((1, W)), _resident((W, W)), _resident((W, W)), _resident((1, W)),
                  _resident((W, W)), _resident((W, W)), _resident((1, W)), _resident((1, W))],
        out_specs=[pl.BlockSpec((tt, 2 * W), lambda i: (rev(i), 0)),
                   pl.BlockSpec((5, SUBLANES, W), lambda i: (0, 0, 0)),
                   pl.BlockSpec((W, W), lambda i: (0, 0)), pl.BlockSpec((W, W), lambda i: (0, 0)),
                   pl.BlockSpec((3, SUBLANES, W), lambda i: (0, 0, 0))],
        out_shape=[jax.ShapeDtypeStruct((T, 2 * W), BF16), jax.ShapeDtypeStruct((5, SUBLANES, W), F32),
                   jax.ShapeDtypeStruct((W, W), F32), jax.ShapeDtypeStruct((W, W), F32),
                   jax.ShapeDtypeStruct((3, SUBLANES, W), F32)],
        scratch_shapes=[pltpu.VMEM((tt, W), F32), pltpu.VMEM((tt, W), F32), pltpu.VMEM((tt, W), F32),
                        pltpu.VMEM((SUBLANES, W), F32), pltpu.VMEM((SUBLANES, W), F32), pltpu.VMEM((BF16_ROWS, W), F32)],
        compiler_params=_cp(1),
    )(z, z, hseq, hseq, dmix, cw, cb, wa, wat, ba, wx, wxt, bx, lam)


def _split3(x):
    hi = x.astype(BF16)
    r1 = x - hi.astype(F32)
    mid = r1.astype(BF16)
    lo = (r1 - mid.astype(F32)).astype(BF16)
    return hi, mid, lo


def _tri_matmul(tri, x):
    hi, mid, lo = _split3(x)
    return (jnp.dot(tri, hi, preferred_element_type=F32) + jnp.dot(tri, mid, preferred_element_type=F32)
            + jnp.dot(tri, lo, preferred_element_type=F32))


def _hg_chunk(q, fl, lb):
    C = q.shape[0]
    ri = lax.broadcasted_iota(jnp.int32, (C, C), 0)
    ci = lax.broadcasted_iota(jnp.int32, (C, C), 1)
    causal = ri >= ci
    sig = _sigmoid(fl)
    f = lb + (1.0 - lb) * sig
    k = 1.0 - f
    sq = _sigmoid(q)
    qf = q * sq
    b = _tri_matmul(causal.astype(BF16), jnp.log(f))
    bm = b[C // 2 - 1:C // 2, :]
    bl = b[C - 1:C, :]
    qt = qf * jnp.exp(b - bm)
    kt = k * jnp.exp(bm - b)
    qin = qf * jnp.exp(b)
    kout = k * jnp.exp(bl - b)
    qtb, ktb = qt.astype(BF16), kt.astype(BF16)
    att = [jnp.where(causal, _dot(qtb[:, _head(h)], ktb[:, _head(h)], NT), 0.0).astype(BF16) for h in range(HG_HEADS)]
    return dict(sig=sig, f=f, k=k, sq=sq, qf=qf, b=b, bm=bm, bl=bl, qt=qt, kt=kt, qin=qin, kout=kout, att=att,
                causal=causal, anti=ri <= ci, decay=jnp.exp(bl))


def _head(h):
    return slice(h * HG_D, (h + 1) * HG_D)


def _hgrn_fwd(z, lb, gn, name="hgrn_fwd", tt=256):
    T = z.shape[0]
    C = HG_CHUNK
    nc = tt // C
    Dh = HG_D
    Wd = HG_HEADS * Dh

    def body(q_ref, f_ref, v_ref, g_ref, lb_ref, gn_ref, o_ref, ss_ref, st):
        @pl.when(pl.program_id(0) == 0)
        def _():
            st[...] = jnp.zeros_like(st)

        S = [st[h] for h in range(HG_HEADS)]
        for c in range(nc):
            rows = slice(c * C, (c + 1) * C)
            ck = _hg_chunk(q_ref[rows, :].astype(F32), f_ref[rows, :].astype(F32), lb_ref[...])
            v = v_ref[rows, :]
            g = g_ref[rows, :].astype(F32)
            H = range(HG_HEADS)
            qinb, koutb = ck["qin"].astype(BF16), ck["kout"].astype(BF16)
            for h in H:
                ss_ref[h, c] = S[h]
            o = [_dot(ck["att"][h], v[:, _head(h)]) + _dot(qinb[:, _head(h)], S[h], NT) for h in H]
            S = [ck["decay"][:, _head(h)] * S[h] + _dot(v[:, _head(h)], koutb[:, _head(h)], TN) for h in H]
            outs = [o[h] * lax.rsqrt(jnp.mean(o[h] * o[h], axis=-1, keepdims=True) + EPS) * gn_ref[...] for h in H]
            o_ref[rows, :] = (jnp.concatenate(outs, axis=1) * (g * _sigmoid(g))).astype(BF16)
        for h in range(HG_HEADS):
            st[h] = S[h]

    col = lambda base: (lambda i: (i, base))
    return pl.pallas_call(
        body, name=name, grid=(T // tt,),
        in_specs=[pl.BlockSpec((tt, Wd), col(2)), pl.BlockSpec((tt, Wd), col(3)), pl.BlockSpec((tt, Wd), col(4)),
                  pl.BlockSpec((tt, Wd), col(5)), _resident((1, Wd)), _resident((1, Dh))],
        out_specs=[pl.BlockSpec((tt, Wd), lambda i: (i, 0)),
                   pl.BlockSpec((HG_HEADS, nc, Dh, Dh), lambda i: (0, i, 0, 0))],
        out_shape=[jax.ShapeDtypeStruct((T, Wd), BF16),
                   jax.ShapeDtypeStruct((HG_HEADS, T // C, Dh, Dh), F32)],
        scratch_shapes=[pltpu.VMEM((HG_HEADS, Dh, Dh), F32)],
        compiler_params=_cp(1),
    )(z, z, z, z, lb, gn)


def _hgrn_bwd(z, ss, dmix, lb, gn, name="hgrn_bwd", tt=256):
    T = z.shape[0]
    C = HG_CHUNK
    nc = tt // C
    nt = T // tt
    Dh = HG_D
    Wd = HG_HEADS * Dh

    def body(q_ref, f_ref, v_ref, g_ref, ss_ref, dm_ref, lb_ref, gn_ref, dz_ref, dlb_ref, dgn_ref, dst):
        @pl.when(pl.program_id(0) == 0)
        def _():
            dst[...] = jnp.zeros_like(dst)
            dlb_ref[...] = jnp.zeros_like(dlb_ref)
            dgn_ref[...] = jnp.zeros_like(dgn_ref)

        dS = [dst[h] for h in range(HG_HEADS)]
        lbv = lb_ref[...]
        gnv = gn_ref[...]
        rowc = lax.broadcasted_iota(jnp.int32, (C, Wd), 0)
        cat = lambda xs: jnp.concatenate(xs, axis=1)
        for c in reversed(range(nc)):
            rows = slice(c * C, (c + 1) * C)
            q = q_ref[rows, :].astype(F32)
            ck = _hg_chunk(q, f_ref[rows, :].astype(F32), lbv)
            v = v_ref[rows, :]
            g = g_ref[rows, :].astype(F32)
            dout = dm_ref[rows, :].astype(F32)
            sg, dsg = _silu_and_grad(g)
            d_ong = dout * sg
            H = range(HG_HEADS)
            qinb, koutb, qtb, ktb = (ck[n].astype(BF16) for n in ("qin", "kout", "qt", "kt"))
            S = [ss_ref[h, c] for h in H]
            Sb = [s.astype(BF16) for s in S]
            dSb = [d.astype(BF16) for d in dS]
            o = [_dot(ck["att"][h], v[:, _head(h)]) + _dot(qinb[:, _head(h)], Sb[h], NT) for h in H]
            rn = [lax.rsqrt(jnp.mean(o[h] * o[h], axis=-1, keepdims=True) + EPS) for h in H]
            on = [o[h] * rn[h] for h in H]
            don = [d_ong[:, _head(h)] * gnv for h in H]
            do = [(rn[h] * (don[h] - on[h] * jnp.mean(don[h] * on[h], axis=-1, keepdims=True))).astype(BF16) for h in H]
            datt = [jnp.where(ck["causal"], _dot(do[h], v[:, _head(h)], NT), 0.0).astype(BF16) for h in H]
            dvs = [_dot(ck["att"][h], do[h], TN) + _dot(koutb[:, _head(h)], dSb[h], NT) for h in H]
            dqins = [_dot(do[h], Sb[h]) for h in H]
            dkouts = [_dot(v[:, _head(h)], dSb[h]) for h in H]
            dqts = [_dot(datt[h], ktb[:, _head(h)]) for h in H]
            dkts = [_dot(datt[h], qtb[:, _head(h)], TN) for h in H]
            ddecays = [jnp.sum(dS[h] * S[h], axis=0, keepdims=True) for h in H]
            dS = [_dot(do[h], qinb[:, _head(h)], TN) + ck["decay"][:, _head(h)] * dS[h] for h in H]
            ons = [on[h] * gnv for h in H]
            dgn = _rsum8(d_ong[:, _head(0)] * on[0])
            for h in range(1, HG_HEADS):
                dgn = dgn + _rsum8(d_ong[:, _head(h)] * on[h])
            dgn_ref[...] += dgn
            dqt, dkt, dqin, dkout, ddecay = cat(dqts), cat(dkts), cat(dqins), cat(dkouts), cat(ddecays)
            b, bm, bl = ck["b"], ck["bm"], ck["bl"]
            dqf = dqt * jnp.exp(b - bm) + dqin * jnp.exp(b)
            dk = dkt * jnp.exp(bm - b) + dkout * jnp.exp(bl - b)
            kk = dkout * ck["kout"]
            db = dqt * ck["qt"] - dkt * ck["kt"] + dqin * ck["qin"] - kk
            dbl = jnp.sum(kk, axis=0, keepdims=True) + ddecay * ck["decay"]
            db = db + jnp.where(rowc == C - 1, dbl, 0.0)
            dlogf = _tri_matmul(ck["anti"].astype(BF16), db)
            dfv = dlogf / ck["f"] - dk
            sig, sq = ck["sig"], ck["sq"]
            dlb_ref[...] += _rsum8(dfv * (1.0 - sig))
            dz_ref[rows, 0:Wd] = (dqf * (sq * (1.0 + q * (1.0 - sq)))).astype(BF16)
            dz_ref[rows, Wd:2 * Wd] = (dfv * (1.0 - lbv) * sig * (1.0 - sig)).astype(BF16)
            dz_ref[rows, 2 * Wd:3 * Wd] = cat(dvs).astype(BF16)
            dz_ref[rows, 3 * Wd:4 * Wd] = (dout * cat(ons) * dsg).astype(BF16)
        for h in range(HG_HEADS):
            dst[h] = dS[h]

    rev = lambda i: nt - 1 - i
    col = lambda base: (lambda i: (rev(i), base))
    return pl.pallas_call(
        body, name=name, grid=(nt,),
        in_specs=[pl.BlockSpec((tt, Wd), col(2)), pl.BlockSpec((tt, Wd), col(3)), pl.BlockSpec((tt, Wd), col(4)),
                  pl.BlockSpec((tt, Wd), col(5)),
                  pl.BlockSpec((HG_HEADS, nc, Dh, Dh), lambda i: (0, rev(i), 0, 0)),
                  pl.BlockSpec((tt, Wd), col(1)), _resident((1, Wd)), _resident((1, Dh))],
        out_specs=[pl.BlockSpec((tt, 4 * Wd), lambda i: (rev(i), 0)), pl.BlockSpec((SUBLANES, Wd), lambda i: (0, 0)),
                   pl.BlockSpec((SUBLANES, Dh), lambda i: (0, 0))],
        out_shape=[jax.ShapeDtypeStruct((T, 4 * Wd), BF16), jax.ShapeDtypeStruct((SUBLANES, Wd), F32),
                   jax.ShapeDtypeStruct((SUBLANES, Dh), F32)],
        scratch_shapes=[pltpu.VMEM((HG_HEADS, Dh, Dh), F32)],
        compiler_params=_cp(1),
    )(z, z, z, z, ss, dmix, lb, gn)


def _sgu_core(p, lg_ref, lb_ref, wsc_ref, bsb_ref):
    Wd = D_MODEL
    G = SGU_CHUNK
    zz = _gelu(p)
    u = zz[:, :Wd]
    v = zz[:, Wd:]
    vc = v - jnp.mean(v, axis=-1, keepdims=True)
    rstd = lax.rsqrt(jnp.mean(vc * vc, axis=-1, keepdims=True) + EPS)
    vhat = vc * rstd
    vn = vhat * lg_ref[...] + lb_ref[...]
    svs = []
    for gi in range(SGU_G):
        svs.append(jnp.dot(wsc_ref[gi], vn[:, gi * G:(gi + 1) * G].astype(BF16), preferred_element_type=F32) + bsb_ref[gi])
    return u, vhat, rstd, vn, jnp.concatenate(svs, axis=1)


def _sgu_fwd(p1, lg, lbias, wsc, bsb, name="sgu_fwd", tt=512):
    T = p1.shape[0]
    Wd = D_MODEL
    C = SGU_CHUNK

    def body(p_ref, lg_ref, lb_ref, wsc_ref, bsb_ref, s_ref):
        for c in range(tt // C):
            rows = slice(c * C, (c + 1) * C)
            u, _, _, _, sv = _sgu_core(p_ref[rows, :].astype(F32), lg_ref, lb_ref, wsc_ref, bsb_ref)
            s_ref[rows, :] = (u * sv).astype(BF16)

    return pl.pallas_call(
        body, name=name, grid=(T // tt,),
        in_specs=[pl.BlockSpec((tt, 2 * Wd), lambda i: (i, 0)), _resident((1, Wd)), _resident((1, Wd)),
                  _resident((SGU_G, C, C)), _resident((SGU_G, C, C))],
        out_specs=pl.BlockSpec((tt, Wd), lambda i: (i, 0)),
        out_shape=jax.ShapeDtypeStruct((T, Wd), BF16),
        compiler_params=_cp(1),
    )(p1, lg, lbias, wsc, bsb)


def _sgu_bwd(p1, ds, lg, lbias, wsc, wsct, bsb, name="sgu_bwd", tt=512):
    T = p1.shape[0]
    Wd = D_MODEL
    C = SGU_CHUNK

    def body(p_ref, ds_ref, lg_ref, lb_ref, wsc_ref, wsct_ref, bsb_ref, dp_ref, dws_ref, dbs_ref, dlg_ref, dlb_ref, dbin_ref):
        @pl.when(pl.program_id(0) == 0)
        def _():
            dws_ref[...] = jnp.zeros_like(dws_ref)
            dbs_ref[...] = jnp.zeros_like(dbs_ref)
            dlg_ref[...] = jnp.zeros_like(dlg_ref)
            dlb_ref[...] = jnp.zeros_like(dlb_ref)
            dbin_ref[...] = jnp.zeros_like(dbin_ref)

        for c in range(tt // C):
            rows = slice(c * C, (c + 1) * C)
            p = p_ref[rows, :].astype(F32)
            u, vhat, rstd, vn, sv = _sgu_core(p, lg_ref, lb_ref, wsc_ref, bsb_ref)
            dsc = ds_ref[rows, :].astype(F32)
            du = dsc * sv
            dsv = dsc * u
            dvns = []
            for gi in range(SGU_G):
                cs = slice(gi * C, (gi + 1) * C)
                dsv_g = dsv[:, cs]
                dvns.append(jnp.dot(wsct_ref[gi], dsv_g.astype(BF16), preferred_element_type=F32))
                dws_ref[gi] += _dot(dsv_g, vn[:, cs], NT)
                dbs_ref[gi] += dsv_g
            dvn = jnp.concatenate(dvns, axis=1)
            dlg_ref[...] += _rsum8(dvn * vhat)
            dlb_ref[...] += _rsum8(dvn)
            dvh = dvn * lg_ref[...]
            dv = rstd * (dvh - jnp.mean(dvh, axis=-1, keepdims=True) - vhat * jnp.mean(dvh * vhat, axis=-1, keepdims=True))
            dp = jnp.concatenate([du, dv], axis=1) * _gelu_grad(p)
            dbin_ref[...] += _rsum8(dp)
            dp_ref[rows, :] = dp.astype(BF16)

    full3 = pl.BlockSpec((SGU_G, C, C), lambda i: (0, 0, 0))
    return pl.pallas_call(
        body, name=name, grid=(T // tt,),
        in_specs=[pl.BlockSpec((tt, 2 * Wd), lambda i: (i, 0)), pl.BlockSpec((tt, Wd), lambda i: (i, 0)),
                  _resident((1, Wd)), _resident((1, Wd)), _resident((SGU_G, C, C)), _resident((SGU_G, C, C)),
                  _resident((SGU_G, C, C))],
        out_specs=[pl.BlockSpec((tt, 2 * Wd), lambda i: (i, 0)), full3, full3,
                   pl.BlockSpec((SUBLANES, Wd), lambda i: (0, 0)), pl.BlockSpec((SUBLANES, Wd), lambda i: (0, 0)),
                   pl.BlockSpec((SUBLANES, 2 * Wd), lambda i: (0, 0))],
        out_shape=[jax.ShapeDtypeStruct((T, 2 * Wd), BF16), jax.ShapeDtypeStruct((SGU_G, C, C), F32),
                   jax.ShapeDtypeStruct((SGU_G, C, C), F32), jax.ShapeDtypeStruct((SUBLANES, Wd), F32),
                   jax.ShapeDtypeStruct((SUBLANES, Wd), F32), jax.ShapeDtypeStruct((SUBLANES, 2 * Wd), F32)],
        compiler_params=_cp(1),
    )(p1, ds, lg, lbias, wsc, wsct, bsb)


def _pad_rows(w, rows=SUBLANES):
    return jnp.pad(w, ((0, rows - w.shape[0]), (0, 0)))


def _block_diag(w):
    n, b, _ = w.shape
    return (w[:, :, None, :] * jnp.eye(n, dtype=w.dtype)[:, None, :, None]).reshape(n * b, n * b)


def _diag_blocks(m, n):
    b = m.shape[0] // n
    m4 = m.reshape(n, b, n, b)
    return jnp.stack([m4[k, :, k, :] for k in range(n)], axis=0)


def _piece_major(dw):
    if dw.ndim == 2:
        K, N = dw.shape
        return dw.reshape(N_CHIPS, 2, K // (2 * N_CHIPS), N)
    _, K, ns = dw.shape
    return dw.reshape(N_CHIPS, 2, K // 2, ns)


def _ffn_fwd(h, g, w_up, cw, cb, w_down, tag):
    hn, gu = _norm_mm(h, g, w_up, jnp.zeros((1, w_up.shape[1]), F32), name=f"ffn_up_{tag}")
    a, gc, out = _ffn_act(gu, cw, cb, w_down, h, name=f"ffn_act_down_{tag}")
    return out, (hn, gu, gc, a)


def _ffn_bwd(dh, h, g, saved, w_up, cw, w_down, tag):
    hn, gu, gc, a = saved
    da = _mm(dh, w_down, None, BF16, name=f"ffn_da_{tag}", transpose_w=True)
    dwd = _mm_tn(a, dh, name=f"ffn_dwd_{tag}")
    dgu, dc, dhin, dg8 = _ffn_act_bwd(gu, gc, da, cw, w_up, h, g, dh, name=f"ffn_actb_dh_{tag}")
    dwu = _mm_tn(hn, dgu, name=f"ffn_dwu_{tag}", col_shards=N_CHIPS)
    dcs = dc.sum(axis=1)
    return dhin, dg8.sum(axis=0), dwu, dcs[0:3], dcs[3], dwd


REDUCE_GROUPS = {"g1": [("ffn_w_up", 1), ("ffn_w_down", 1), ("od_w_out", 0), ("od_w_in", 0)],
                 "g2": [("ffn_w_up", 0), ("ffn_w_down", 0)],
                 "g3": [("ev_w_out", 0), ("ev_w_in", 0)]}


def _local_step(x, tgt, p, start_reduce=None):
    row = lambda v: v.reshape(1, -1)
    grads = {}

    lower = jax.nn.softmax(p["hg_lb_logits"], axis=0)
    lb0 = row(lower[0])
    ev_cw = _pad_rows(p["ev_conv_w"][0])
    ev_cb = row(p["ev_conv_b"][0])
    wa = _block_diag(p["ev_gate_a_w"][0]).astype(BF16)
    wx = _block_diag(p["ev_gate_x_w"][0]).astype(BF16)
    ba, bx, lam = row(p["ev_gate_a_b"][0]), row(p["ev_gate_x_b"][0]), row(p["ev_lru_lambda"][0])
    gn = row(p["ev_hg_norm"][0])
    tril = jnp.tril(jnp.ones((SGU_CHUNK, SGU_CHUNK), F32))
    wsc = (p["od_w_s"][0] * tril).astype(BF16)
    bsb = jnp.broadcast_to(p["od_b_s"][0][:, :, None], (SGU_G, SGU_CHUNK, SGU_CHUNK)).astype(F32)
    ffn_cw = [_pad_rows(p["ffn_conv_w"][l]) for l in range(2)]
    ffn_cb = [row(p["ffn_conv_b"][l]) for l in range(2)]
    ev_w_in, ev_w_out = p["ev_w_in"][0], p["ev_w_out"][0]
    nm = [row(p["norm_mix"][l]) for l in range(2)]
    nf = [row(p["norm_ffn"][l]) for l in range(2)]

    h0 = x
    hn0, z0 = _norm_mm(h0, nm[0], ev_w_in, jnp.zeros((1, ev_w_in.shape[1]), F32), name="ev_in")
    out_a, hseq = _lru_fwd(z0, ev_cw, ev_cb, wa, ba, wx, bx, lam)
    out_b, ss = _hgrn_fwd(z0, lb0, gn)
    mix0 = jnp.concatenate([out_a, out_b], axis=1)
    h1 = _mm(mix0, ev_w_out, h0, F32, name="ev_out")
    late = p["late"](h1) if "late" in p else p
    od_w_in, od_w_out = late["od_w_in"][0], late["od_w_out"][0]
    w_up = [late["ffn_w_up"][l] for l in range(2)]
    w_down = [late["ffn_w_down"][l] for l in range(2)]
    h2, ffn0 = _ffn_fwd(h1, nf[0], w_up[0], ffn_cw[0], ffn_cb[0], w_down[0], "l0")
    hn1, p1 = _norm_mm(h2, nm[1], od_w_in, row(p["od_b_in"][0]), name="od_in")
    s1 = _sgu_fwd(p1, row(p["od_ln_g"][0]), row(p["od_ln_b"][0]), wsc, bsb)
    h3 = _mm(s1, od_w_out, h2, F32, name="od_out")
    h4, ffn1 = _ffn_fwd(h3, nf[1], w_up[1], ffn_cw[1], ffn_cb[1], w_down[1], "l1")
    dh4, dgf8, sq8 = _final_loss(h4, row(p["norm_final"]), tgt)
    grads["norm_final"] = dgf8.sum(axis=0)

    big = {}
    dh3, dnf1, dwu1, dcw1, dcb1, dwd1 = _ffn_bwd(dh4, h3, nf[1], ffn1, w_up[1], ffn_cw[1], w_down[1], "l1")
    big["ffn_w_up", 1], big["ffn_w_down", 1] = _piece_major(dwu1), _piece_major(dwd1)
    ds1 = _mm(dh3, od_w_out, None, BF16, name="od_ds", transpose_w=True)
    big["od_w_out", 0] = _piece_major(_mm_tn(s1, dh3, name="od_dwo"))
    wsct = jnp.swapaxes(wsc, 1, 2)
    dp1, dws, dbs, dlg8, dlb8, dbin8 = _sgu_bwd(p1, ds1, row(p["od_ln_g"][0]), row(p["od_ln_b"][0]), wsc, wsct, bsb)
    grads["od_w_s"] = (dws * tril)[None]
    grads["od_b_s"] = dbs.sum(axis=-1)[None]
    grads["od_ln_g"] = dlg8.sum(axis=0)[None]
    grads["od_ln_b"] = dlb8.sum(axis=0)[None]
    grads["od_b_in"] = dbin8.sum(axis=0)[None]
    dh2, dnm1 = _mm_normbwd(dp1, od_w_in, h2, nm[1], dh3, name="od_dh")
    big["od_w_in", 0] = _piece_major(_mm_tn(hn1, dp1, name="od_dwi", col_shards=N_CHIPS))
    if start_reduce is not None:
        token = start_reduce("g1", [big[key] for key in REDUCE_GROUPS["g1"]])
        ffn_cw[0] = ffn_cw[0] + token[0:1, 0:1]

    dh1, dnf0, dwu0, dcw0, dcb0, dwd0 = _ffn_bwd(dh2, h1, nf[0], ffn0, w_up[0], ffn_cw[0], w_down[0], "l0")
    big["ffn_w_up", 0], big["ffn_w_down", 0] = _piece_major(dwu0), _piece_major(dwd0)
    if start_reduce is not None:
        token = start_reduce("g2", [big[key] for key in REDUCE_GROUPS["g2"]])
        ev_cb = ev_cb + token[0:1, 0:1]
    dmix = _mm(dh1, ev_w_out, None, BF16, name="ev_dmix", transpose_w=True)
    big["ev_w_out", 0] = _piece_major(_mm_tn(mix0, dh1, name="ev_dwo"))
    dz01, dc5, dwa, dwx, dvec = _lru_bwd(z0, hseq, dmix, ev_cw, ev_cb, wa, wa.T, ba, wx, wx.T, bx, lam)
    dz2, dlb8, dgn8 = _hgrn_bwd(z0, ss, dmix, lb0, gn)
    dz0 = jnp.concatenate([dz01, dz2], axis=1)
    big["ev_w_in", 0] = _piece_major(_mm_tn(hn0, dz0, name="ev_dwi", col_shards=N_CHIPS))
    if start_reduce is not None:
        token = start_reduce("g3", [big[key] for key in REDUCE_GROUPS["g3"]])
        nm[0] = nm[0] + token[0:1, 0:1]
    grad_x, dnm0 = _mm_normbwd(dz0, ev_w_in, h0, nm[0], dh1, name="ev_dh")

    dc5s = dc5.sum(axis=1)
    grads["ev_conv_w"] = dc5s[0:4][None]
    grads["ev_conv_b"] = dc5s[4][None]
    grads["ev_gate_a_w"] = _diag_blocks(dwa, LRU_BLOCKS)[None]
    grads["ev_gate_x_w"] = _diag_blocks(dwx, LRU_BLOCKS)[None]
    dvs = dvec.sum(axis=1)
    grads["ev_gate_a_b"] = dvs[0][None]
    grads["ev_gate_x_b"] = dvs[1][None]
    grads["ev_lru_lambda"] = (dvs[2] * (-jax.nn.sigmoid(-p["ev_lru_lambda"][0])))[None]
    dlb = dlb8.sum(axis=0)
    grads["hg_lb_logits"] = dlb[None, :] * lower[0][None, :] * (jnp.eye(3, dtype=F32)[0][:, None] - lower)
    grads["ev_hg_norm"] = dgn8.sum(axis=0)[None]
    grads["norm_mix"] = jnp.stack([dnm0.sum(axis=0), dnm1.sum(axis=0)])
    grads["norm_ffn"] = jnp.stack([dnf0, dnf1])
    grads["ffn_conv_w"] = jnp.stack([dcw0, dcw1])
    grads["ffn_conv_b"] = jnp.stack([dcb0, dcb1])
    return sq8, grad_x, grads, big


MESH = pl.DeviceIdType.MESH
ANY = pl.BlockSpec(memory_space=pl.ANY)
N_CHIPS = 4
N_DEV = 8

SH_BIG = {"ev_w_in": 2, "ev_w_out": 1, "od_w_in": 2, "od_w_out": 1, "ffn_w_up": 2, "ffn_w_down": 1}
SH_SMALL = {"ev_conv_w": 2, "od_b_in": 1, "od_ln_g": 1, "od_ln_b": 1, "ffn_conv_w": 2}
REP = ["norm_mix", "norm_ffn", "norm_final", "ev_conv_b", "ev_gate_a_w", "ev_gate_a_b", "ev_gate_x_w", "ev_gate_x_b",
       "ev_lru_lambda", "hg_lb_logits", "ev_hg_norm", "od_w_s", "od_b_s", "ffn_conv_b"]
WEIGHTS = ["norm_mix", "norm_ffn", "norm_final", "ev_w_in", "ev_conv_w", "ev_conv_b", "ev_gate_a_w", "ev_gate_a_b", "ev_gate_x_w",
           "ev_gate_x_b", "ev_lru_lambda", "hg_lb_logits", "ev_hg_norm", "ev_w_out", "od_w_in", "od_b_in", "od_ln_g", "od_ln_b",
           "od_w_s", "od_b_s", "od_w_out", "ffn_w_up", "ffn_conv_w", "ffn_conv_b", "ffn_w_down"]


def _rows(n_elems, mult=SUBLANES):
    r = -(-n_elems // LANES)
    return -(-r // mult) * mult


def _pack(arrs, rows, dtype):
    flat = jnp.concatenate([a.reshape(-1).astype(dtype) for a in arrs])
    return jnp.pad(flat, (0, rows * LANES - flat.shape[0])).reshape(rows, LANES)


def _unpack(flat2d, shapes):
    flat = flat2d.reshape(-1)
    out, off = [], 0
    for s in shapes:
        n = 1
        for d in s:
            n *= d
        out.append(flat[off:off + n].reshape(s))
        off += n
    return out


def _mesh_pos():
    return lax.axis_index("x"), lax.axis_index("y"), lax.axis_index("c")


def _other_chips(x, y):
    return [(1 - x, y), (x, 1 - y), (1 - x, 1 - y)]


def _half_rows(n):
    return lambda r, c: r.at[0, pl.ds(c * (n // 2), n // 2), :]


GATHER_BIG = {
    "ev_w_in": ((1024, 3072), _half_rows(1024), lambda o, k, c: o.at[pl.ds(c * 512, 512), pl.ds(k * 768, 768)]),
    "ev_w_out": ((1024, 1024), _half_rows(256), lambda o, k, c: o.at[pl.ds(k * 256 + c * 128, 128), :]),
    "od_w_in": ((1024, 2048), _half_rows(1024), lambda o, k, c: o.at[pl.ds(c * 512, 512), pl.ds(k * 512, 512)]),
    "od_w_out": ((1024, 1024), _half_rows(256), lambda o, k, c: o.at[pl.ds(k * 256 + c * 128, 128), :]),
    "ffn_w_up": ((2, 1024, 2 * D_FF), lambda r, c: r.at[c], lambda o, k, c: o.at[c, :, pl.ds(k * (2 * D_FF // 4), 2 * D_FF // 4)]),
    "ffn_w_down": ((2, D_FF, 1024), lambda r, c: r.at[c], lambda o, k, c: o.at[c, pl.ds(k * (D_FF // 4), D_FF // 4), :]),
}


def _gather_weights(names, big, small):
    nb = len(big)
    descs = [GATHER_BIG[n] for n in names]
    rs = small.shape[0]

    def body(*refs):
        ins, s_ref = refs[:nb], refs[nb]
        outs, os_ref = refs[nb + 1:2 * nb + 1], refs[2 * nb + 1]
        ici_send, ici_recv, d2d_send, d2d_recv, loc_sems = refs[2 * nb + 2:2 * nb + 7]
        vbufs = refs[2 * nb + 7:]
        x, y, c = _mesh_pos()
        k = 2 * x + y
        chips = _other_chips(x, y)
        sib = (x, y, 1 - c)

        def remote(src, dst, ssem, rsem, to):
            return pltpu.make_async_remote_copy(src_ref=src, dst_ref=dst, send_sem=ssem, recv_sem=rsem, device_id=to,
                                                device_id_type=MESH)

        stage = [pltpu.make_async_copy(ins[t], vbufs[t], loc_sems.at[2 * t]) for t in range(nb)]
        stage.append(pltpu.make_async_copy(s_ref, vbufs[nb], loc_sems.at[2 * nb]))
        for cp in stage:
            cp.start()
        sends = []
        for t, (_, src, dst) in enumerate(descs):
            for j, (px, py) in enumerate(chips):
                sends.append(remote(src(ins[t], c), dst(outs[t], k, c), ici_send.at[3 * t + j], ici_recv.at[3 * t + j], (px, py, c)))
        for j, (px, py) in enumerate(chips):
            sends.append(remote(s_ref, os_ref.at[k], ici_send.at[3 * nb + j], ici_recv.at[3 * nb + j], (px, py, c)))
        for cp in sends:
            cp.start()
        for cp in stage:
            cp.wait()
        local = []
        for t, (_, src, dst) in enumerate(descs):
            for cc in (0, 1):
                local.append(pltpu.make_async_copy(src(vbufs[t], cc), dst(outs[t], k, cc), loc_sems.at[2 * t + cc]))
        local.append(pltpu.make_async_copy(vbufs[nb], os_ref.at[k], loc_sems.at[2 * nb]))
        for cp in local:
            cp.start()
        for t, (_, src, dst) in enumerate(descs):
            for j, (px, py) in enumerate(chips):
                got = dst(outs[t], 2 * px + py, c)
                remote(got, got, ici_send.at[3 * t + j], ici_recv.at[3 * t + j], (px, py, c)).wait_recv()
                fwd = remote(got, got, d2d_send.at[3 * t + j], d2d_recv.at[3 * t + j], sib)
                fwd.start()
                sends.append(fwd)
        for j, (px, py) in enumerate(chips):
            remote(s_ref, os_ref.at[2 * px + py], ici_send.at[3 * nb + j], ici_recv.at[3 * nb + j], (px, py, c)).wait_recv()
        for t, (_, src, dst) in enumerate(descs):
            for j, (px, py) in enumerate(chips):
                theirs = dst(outs[t], 2 * px + py, 1 - c)
                remote(theirs, theirs, d2d_send.at[3 * t + j], d2d_recv.at[3 * t + j], sib).wait_recv()
        for cp in sends:
            cp.wait_send()
        for cp in local:
            cp.wait()

    out_shape = [jax.ShapeDtypeStruct(d[0], BF16) for d in descs] + [jax.ShapeDtypeStruct((N_CHIPS, rs, LANES), small.dtype)]
    return pl.pallas_call(
        body, name="gather_weights", in_specs=[ANY] * (nb + 1), out_specs=[ANY] * (nb + 1), out_shape=out_shape,
        scratch_shapes=[pltpu.SemaphoreType.DMA((3 * nb + 3,)), pltpu.SemaphoreType.DMA((3 * nb + 3,)),
                        pltpu.SemaphoreType.DMA((3 * nb,)), pltpu.SemaphoreType.DMA((3 * nb,)),
                        pltpu.SemaphoreType.DMA((2 * nb + 1,))]
        + [pltpu.VMEM(b.shape, b.dtype) for b in big] + [pltpu.VMEM(small.shape, small.dtype)],
        compiler_params=pltpu.CompilerParams(vmem_limit_bytes=VMEM_LIMIT),
    )(*big, small)


def _place_own(names, big):
    nb = len(big)
    descs = [GATHER_BIG[n] for n in names]

    def body(*refs):
        ins, outs = refs[:nb], refs[nb:2 * nb]
        sems, vbufs = refs[2 * nb], refs[2 * nb + 1:]
        x, y, c = _mesh_pos()
        k = 2 * x + y
        stage = [pltpu.make_async_copy(ins[t], vbufs[t], sems.at[2 * t]) for t in range(nb)]
        for cp in stage:
            cp.start()
        for cp in stage:
            cp.wait()
        local = [pltpu.make_async_copy(src(vbufs[t], cc), dst(outs[t], k, cc), sems.at[2 * t + cc])
                 for t, (_, src, dst) in enumerate(descs) for cc in (0, 1)]
        for cp in local:
            cp.start()
        for cp in local:
            cp.wait()

    return pl.pallas_call(
        body, name="place_own", in_specs=[ANY] * nb, out_specs=[ANY] * nb,
        out_shape=[jax.ShapeDtypeStruct(d[0], BF16) for d in descs],
        scratch_shapes=[pltpu.SemaphoreType.DMA((2 * nb,))] + [pltpu.VMEM(b.shape, b.dtype) for b in big],
        compiler_params=pltpu.CompilerParams(vmem_limit_bytes=VMEM_LIMIT),
    )(*big)


def _gather_start(names, big, bufs):
    nb = len(big)
    descs = [GATHER_BIG[n] for n in names]

    def body(*refs):
        ins, lnd = refs[:nb], refs[nb:2 * nb]
        send_sems, recv_sems, token = refs[2 * nb], refs[2 * nb + 1], refs[-1]
        x, y, c = _mesh_pos()
        k = 2 * x + y
        for t, (_, src, dst) in enumerate(descs):
            for j, (px, py) in enumerate(_other_chips(x, y)):
                _remote(src(ins[t], c), dst(lnd[t], k, c), send_sems.at[3 * t + j], recv_sems.at[3 * t + j], (px, py, c)).start()
        token[...] = jnp.zeros_like(token)

    out = pl.pallas_call(
        body, name="gather_start",
        out_shape=(pltpu.SemaphoreType.DMA((3 * nb,)), pltpu.SemaphoreType.DMA((3 * nb,)),
                   *[pltpu.HBM(b.shape, b.dtype) for b in big], *[pltpu.HBM(b.shape, b.dtype) for b in bufs],
                   jax.ShapeDtypeStruct((SUBLANES, LANES), F32)),
        in_specs=[HBM] * (2 * nb), out_specs=(SEM, SEM, *[HBM] * (2 * nb), pl.BlockSpec(memory_space=pltpu.VMEM)),
        input_output_aliases={i: 2 + i for i in range(2 * nb)},
        compiler_params=pltpu.CompilerParams(has_side_effects=DATAFLOW),
    )(*[pltpu.with_memory_space_constraint(b, pltpu.HBM) for b in big], *[pltpu.with_memory_space_constraint(b, pltpu.HBM) for b in bufs])
    return out[0], out[1], list(out[2:2 + nb]), list(out[2 + nb:2 + 2 * nb]), out[-1]


def _gather_wait(names, send_sems, recv_sems, big, bufs, after):
    nb = len(big)
    descs = [GATHER_BIG[n] for n in names]

    def body(*refs):
        ins, lnd = refs[:nb], refs[nb:2 * nb]
        ssem, rsem = refs[2 * nb], refs[2 * nb + 1]
        x, y, c = _mesh_pos()
        for t, (_, src, dst) in enumerate(descs):
            for j, (px, py) in enumerate(_other_chips(x, y)):
                cp = _remote(src(ins[t], c), dst(lnd[t], 2 * px + py, c), ssem.at[3 * t + j], rsem.at[3 * t + j], (px, py, c))
                cp.wait_send()
                cp.wait_recv()

    out = pl.pallas_call(
        body, name="gather_wait",
        out_shape=(*[pltpu.HBM(b.shape, b.dtype) for b in big], *[pltpu.HBM(b.shape, b.dtype) for b in bufs]),
        in_specs=[HBM] * (2 * nb) + [SEM, SEM, ANY], out_specs=tuple([HBM] * (2 * nb)),
        input_output_aliases={i: i for i in range(2 * nb)},
        compiler_params=pltpu.CompilerParams(has_side_effects=DATAFLOW),
    )(*big, *bufs, send_sems, recv_sems, after)
    return list(out[nb:])


def _gather_forward(names, bufs):
    nb = len(bufs)
    descs = [GATHER_BIG[n] for n in names]

    def body(*refs):
        outs = refs[nb:2 * nb]
        send_sems, recv_sems = refs[2 * nb:]
        x, y, c = _mesh_pos()
        sib = (x, y, 1 - c)
        sends = []
        for t, (_, src, dst) in enumerate(descs):
            for j, (px, py) in enumerate(_other_chips(x, y)):
                got = dst(outs[t], 2 * px + py, c)
                sends.append(_remote(got, got, send_sems.at[3 * t + j], recv_sems.at[3 * t + j], sib))
        for cp in sends:
            cp.start()
        for t, (_, src, dst) in enumerate(descs):
            for j, (px, py) in enumerate(_other_chips(x, y)):
                theirs = dst(outs[t], 2 * px + py, 1 - c)
                _remote(theirs, theirs, send_sems.at[3 * t + j], recv_sems.at[3 * t + j], sib).wait_recv()
        for cp in sends:
            cp.wait_send()

    return pl.pallas_call(
        body, name="gather_forward", in_specs=[ANY] * nb, out_specs=[ANY] * nb,
        out_shape=[jax.ShapeDtypeStruct(b.shape, b.dtype) for b in bufs], input_output_aliases={t: t for t in range(nb)},
        scratch_shapes=[pltpu.SemaphoreType.DMA((3 * nb,)), pltpu.SemaphoreType.DMA((3 * nb,))],
    )(*bufs)


def _remote(src, dst, ssem, rsem, to):
    return pltpu.make_async_remote_copy(src_ref=src, dst_ref=dst, send_sem=ssem, recv_sem=rsem, device_id=to, device_id_type=MESH)


def _rs_send_sibling(gs, tag):
    n = len(gs)
    counts = [N_CHIPS if g.ndim == 4 else 1 for g in gs]
    ns = sum(counts)

    def body(*refs):
        ins, outs = refs[:n], refs[n:2 * n]
        send_sems, recv_sems = refs[2 * n:]
        x, y, c = _mesh_pos()
        cps, s = [], 0
        for t in range(n):
            if counts[t] == 1:
                cps.append(_remote(ins[t].at[1 - c], outs[t], send_sems.at[s], recv_sems.at[s], (x, y, 1 - c)))
                s += 1
            else:
                for k in range(N_CHIPS):
                    cps.append(_remote(ins[t].at[k, 1 - c], outs[t].at[k], send_sems.at[s], recv_sems.at[s], (x, y, 1 - c)))
                    s += 1
        for cp in cps:
            cp.start()
        for cp in cps:
            cp.wait()

    out_shape = [jax.ShapeDtypeStruct(g.shape[:1] + g.shape[2:] if g.ndim == 4 else g.shape[1:], g.dtype) for g in gs]
    return pl.pallas_call(
        body, name=f"rs_send_sibling_{tag}", in_specs=[ANY] * n, out_specs=[ANY] * n, out_shape=out_shape,
        scratch_shapes=[pltpu.SemaphoreType.DMA((ns,)), pltpu.SemaphoreType.DMA((ns,))],
    )(*gs)


def _add_piece(g, recv, c, name):
    P, Q = g.shape[-2:]

    def body(c_ref, g_ref, r_ref, o_ref):
        o_ref[...] = g_ref[...].reshape(o_ref.shape) + r_ref[...]

    if g.ndim == 4:
        grid = (N_CHIPS,)
        in_specs = [pl.BlockSpec((1, 1, P, Q), lambda k, c_ref: (k, c_ref[0], 0, 0)), pl.BlockSpec((1, P, Q), lambda k, c_ref: (k, 0, 0))]
        out_spec = pl.BlockSpec((1, P, Q), lambda k, c_ref: (k, 0, 0))
    else:
        grid = (1,)
        in_specs = [pl.BlockSpec((1, P, Q), lambda k, c_ref: (c_ref[0], 0, 0)), pl.BlockSpec((P, Q), lambda k, c_ref: (0, 0))]
        out_spec = pl.BlockSpec((P, Q), lambda k, c_ref: (0, 0))
    return pl.pallas_call(
        body, name=name,
        grid_spec=pltpu.PrefetchScalarGridSpec(num_scalar_prefetch=1, grid=grid, in_specs=in_specs, out_specs=out_spec),
        out_shape=jax.ShapeDtypeStruct(recv.shape, g.dtype),
        compiler_params=_cp(1),
    )(c, g, recv)


HBM = pl.BlockSpec(memory_space=pltpu.HBM)
SEM = pl.BlockSpec(memory_space=pltpu.SEMAPHORE)
DATAFLOW = pltpu.SideEffectType.DATAFLOW_SIDE_EFFECTING


def _chips_start(hs, tag):
    n = len(hs)
    lands = [pltpu.with_memory_space_constraint(lax.empty((N_CHIPS,) + h.shape[-2:], h.dtype), pltpu.HBM) for h in hs]

    def body(*refs):
        ins, lnd = refs[:n], refs[n:2 * n]
        send_sems, recv_sems, token = refs[2 * n], refs[2 * n + 1], refs[-1]
        x, y, c = _mesh_pos()
        k = 2 * x + y
        piece = lambda t, kk: ins[t].at[kk] if hs[t].ndim == 3 else ins[t]
        for t in range(n):
            for j, (px, py) in enumerate(_other_chips(x, y)):
                _remote(piece(t, 2 * px + py), lnd[t].at[k], send_sems.at[3 * t + j], recv_sems.at[3 * t + j], (px, py, c)).start()
        token[...] = jnp.zeros_like(token)

    out = pl.pallas_call(
        body, name=f"chips_start_{tag}",
        out_shape=(pltpu.SemaphoreType.DMA((3 * n,)), pltpu.SemaphoreType.DMA((3 * n,)),
                   *[pltpu.HBM(h.shape, h.dtype) for h in hs], *[pltpu.HBM(l.shape, l.dtype) for l in lands],
                   jax.ShapeDtypeStruct((SUBLANES, LANES), F32)),
        in_specs=[HBM] * (2 * n), out_specs=(SEM, SEM, *[HBM] * (2 * n), pl.BlockSpec(memory_space=pltpu.VMEM)),
        input_output_aliases={i: 2 + i for i in range(2 * n)},
        compiler_params=pltpu.CompilerParams(has_side_effects=DATAFLOW),
    )(*[pltpu.with_memory_space_constraint(h, pltpu.HBM) for h in hs], *lands)
    return out[0], out[1], list(out[2:2 + n]), list(out[2 + n:2 + 2 * n]), out[-1]


def _chips_wait(send_sems, recv_sems, hs, lands, after, tag):
    n = len(hs)

    def body(*refs):
        ins, lnd = refs[:n], refs[n:2 * n]
        ssem, rsem = refs[2 * n], refs[2 * n + 1]
        x, y, c = _mesh_pos()
        k = 2 * x + y
        piece = lambda t, kk: ins[t].at[kk] if hs[t].ndim == 3 else ins[t]
        for t in range(n):
            for j, (px, py) in enumerate(_other_chips(x, y)):
                cp = _remote(piece(t, k), lnd[t].at[2 * px + py], ssem.at[3 * t + j], rsem.at[3 * t + j], (px, py, c))
                cp.wait_send()
                cp.wait_recv()

    out = pl.pallas_call(
        body, name=f"chips_wait_{tag}",
        out_shape=(*[pltpu.HBM(h.shape, h.dtype) for h in hs], *[pltpu.HBM(l.shape, l.dtype) for l in lands]),
        in_specs=[HBM] * (2 * n) + [SEM, SEM, ANY], out_specs=tuple([HBM] * (2 * n)),
        input_output_aliases={i: i for i in range(2 * n)},
        compiler_params=pltpu.CompilerParams(has_side_effects=DATAFLOW),
    )(*hs, *lands, send_sems, recv_sems, after)
    return list(out[:n]), list(out[n:])


def _add_chips(p, own, kc, name):
    _, P, Q = p.shape
    tr = P
    while N_CHIPS * tr * Q * 4 > 6 * 1024 * 1024 and tr % 16 == 0:
        tr //= 2
    sharded = own.ndim == 3

    def body(kc_ref, p_ref, own_ref, o_ref):
        k = kc_ref[0]
        mine = own_ref[...].reshape(tr, Q)
        v = [jnp.where(k == j, mine, p_ref[j]) for j in range(N_CHIPS)]
        o_ref[0] = ((v[0] + v[1]) + v[2]) + v[3]

    own_spec = (pl.BlockSpec((1, tr, Q), lambda i, kc_ref: (kc_ref[0], i, 0)) if sharded
                else pl.BlockSpec((tr, Q), lambda i, kc_ref: (i, 0)))
    return pl.pallas_call(
        body, name=name,
        grid_spec=pltpu.PrefetchScalarGridSpec(
            num_scalar_prefetch=1, grid=(P // tr,),
            in_specs=[pl.BlockSpec((N_CHIPS, tr, Q), lambda i, kc_ref: (0, i, 0)), own_spec],
            out_specs=pl.BlockSpec((1, tr, Q), lambda i, kc_ref: (kc_ref[1], i, 0))),
        out_shape=jax.ShapeDtypeStruct((2, P, Q), p.dtype),
        compiler_params=_cp(1),
    )(kc, p, own)


def _rs_share(fs, tag):
    n = len(fs)

    def body(*refs):
        outs = refs[n:2 * n]
        send_sems, recv_sems = refs[2 * n:]
        x, y, c = _mesh_pos()
        sends = [_remote(outs[t].at[c], outs[t].at[c], send_sems.at[t], recv_sems.at[t], (x, y, 1 - c)) for t in range(n)]
        for cp in sends:
            cp.start()
        for t in range(n):
            _remote(outs[t].at[c], outs[t].at[1 - c], send_sems.at[t], recv_sems.at[t], (x, y, 1 - c)).wait_recv()
        for cp in sends:
            cp.wait_send()

    return pl.pallas_call(
        body, name=f"rs_share_{tag}", in_specs=[ANY] * n, out_specs=[ANY] * n,
        out_shape=[jax.ShapeDtypeStruct(f.shape, f.dtype) for f in fs], input_output_aliases={t: t for t in range(n)},
        scratch_shapes=[pltpu.SemaphoreType.DMA((n,)), pltpu.SemaphoreType.DMA((n,))],
    )(*fs)


def _reduce_start(gs, kc, tag):
    from_sibling = _rs_send_sibling(gs, tag)
    chip_sums = [_add_piece(g, r, kc[1:], name=f"add_piece_{tag}_{t}") for t, (g, r) in enumerate(zip(gs, from_sibling))]
    send_sems, recv_sems, chip_sums, lands, token = _chips_start(chip_sums, tag)
    return (send_sems, recv_sems, chip_sums, lands, tag), token


def _reduce_finish(states, kc, after):
    mine = []
    for send_sems, recv_sems, chip_sums, lands, tag in states:
        chip_sums, from_chips = _chips_wait(send_sems, recv_sems, chip_sums, lands, after, tag)
        mine += [_add_chips(p, h, kc, name=f"add_chips_{tag}_{t}") for t, (p, h) in enumerate(zip(from_chips, chip_sums))]
    return _rs_share(mine, "all")


def _adamw(w, g, m, v, name):
    R, C = w.shape
    tr = R
    for cand in (512, 256, 128, 64, 32, 16, 8):
        if R % cand == 0 and cand * C * 4 <= 2 * 1024 * 1024:
            tr = cand
            break
    c1 = 1.0 / (1.0 - ADAM_B1 ** ADAM_STEP)
    c2 = 1.0 / (1.0 - ADAM_B2 ** ADAM_STEP)

    def body(w_ref, g_ref, m_ref, v_ref, d_ref, mo_ref, vo_ref):
        gv = g_ref[...]
        mn = ADAM_B1 * m_ref[...] + (1.0 - ADAM_B1) * gv
        vn = ADAM_B2 * v_ref[...] + (1.0 - ADAM_B2) * (gv * gv)
        mo_ref[...] = mn
        vo_ref[...] = vn
        d_ref[...] = -ADAM_LR * ((mn * c1) / (jnp.sqrt(vn * c2) + ADAM_EPS) + ADAM_WD * w_ref[...])

    spec = pl.BlockSpec((tr, C), lambda i: (i, 0))
    shp = jax.ShapeDtypeStruct((R, C), F32)
    return pl.pallas_call(body, name=name, grid=(R // tr,), in_specs=[spec] * 4, out_specs=[spec] * 3, out_shape=[shp] * 3,
                          compiler_params=_cp(1))(w, g, m, v)


def _adamw_many(ws, gs, ms, vs):
    n = len(ws)
    c1 = 1.0 / (1.0 - ADAM_B1 ** ADAM_STEP)
    c2 = 1.0 / (1.0 - ADAM_B2 ** ADAM_STEP)

    def body(*refs):
        w_refs, g_refs, m_refs, v_refs = (refs[k * n:(k + 1) * n] for k in range(4))
        outs = refs[4 * n:]
        for i in range(n):
            gv = g_refs[i][...]
            mn = ADAM_B1 * m_refs[i][...] + (1.0 - ADAM_B1) * gv
            vn = ADAM_B2 * v_refs[i][...] + (1.0 - ADAM_B2) * (gv * gv)
            outs[3 * i][...] = -ADAM_LR * ((mn * c1) / (jnp.sqrt(vn * c2) + ADAM_EPS) + ADAM_WD * w_refs[i][...])
            outs[3 * i + 1][...] = mn
            outs[3 * i + 2][...] = vn

    out_shape = [jax.ShapeDtypeStruct(w.shape, F32) for w in ws for _ in range(3)]
    return pl.pallas_call(body, name="adamw_small", out_shape=out_shape)(*ws, *gs, *ms, *vs)


def _step(a):
    x, y, c = _mesh_pos()
    kc = jnp.stack([2 * x + y, c]).astype(jnp.int32)

    rs = _rows(sum(a[n].size for n in SH_SMALL))
    first, later = ["ev_w_in", "ev_w_out"], ["od_w_in", "od_w_out", "ffn_w_up", "ffn_w_down"]
    lead = lambda w: w if w.ndim == 3 else w[None]
    *full, gs = _gather_weights(first, [a[n].astype(BF16) for n in first], _pack([a[n] for n in SH_SMALL], rs, F32))
    p = {n: a[n] for n in REP}
    p.update({n: lead(w) for n, w in zip(first, full)})
    parts = [_unpack(gs[k], [a[n].shape for n in SH_SMALL]) for k in range(N_CHIPS)]
    for i, n in enumerate(SH_SMALL):
        p[n] = jnp.concatenate([parts[k][i] for k in range(N_CHIPS)], axis=SH_SMALL[n])
    shards = lax.optimization_barrier(([a[n].astype(BF16) for n in later], full))[0]
    g_send, g_recv, shards, bufs, token = _gather_start(later, shards, _place_own(later, shards))
    p["norm_mix"] = p["norm_mix"] + token[0:1, 0:1]

    def late(after):
        got = _gather_forward(later, _gather_wait(later, g_send, g_recv, shards, bufs, after))
        return {n: lead(w) for n, w in zip(later, got)}

    p["late"] = late

    states = []

    def start_reduce(tag, gs):
        state, token = _reduce_start(gs, kc, tag)
        states.append(state)
        return token

    sq8, grad_x, grads, big = _local_step(a["x"][0], a["loss_target"][0], p, start_reduce)
    loss = lax.psum(0.5 / D_MODEL * jnp.sum(sq8), ("x", "y", "c"))

    r_s = _rows(sum(a[n].size for n in SH_SMALL), 2 * SUBLANES) // 2
    small_pieces = []
    for k in range(N_CHIPS):
        pieces = [lax.slice_in_dim(grads[n], k * a[n].shape[ax], (k + 1) * a[n].shape[ax], axis=ax) for n, ax in SH_SMALL.items()]
        small_pieces.append(_pack(pieces, 2 * r_s, F32).reshape(2, r_s, LANES))
    g_small = jnp.stack(small_pieces)
    r_r = _rows(sum(a[n].size for n in REP), 2 * SUBLANES) // 2
    g_rep = _pack([grads[n] for n in REP], 2 * r_r, F32).reshape(2, r_r, LANES)
    start_reduce("g4", [g_small, g_rep])
    reduced = _reduce_finish(states, kc, grad_x)
    red = dict(zip([key for tag in ("g1", "g2", "g3") for key in REDUCE_GROUPS[tag]], reduced))
    gfin = {}
    for n in ("ev_w_in", "ev_w_out", "od_w_in", "od_w_out"):
        gfin[n] = red[n, 0].reshape(a[n].shape)
    for n in ("ffn_w_up", "ffn_w_down"):
        gfin[n] = jnp.stack([red[n, l].reshape(a[n].shape[1:]) for l in range(2)])
    gfin.update(zip(SH_SMALL, _unpack(reduced[-2], [a[n].shape for n in SH_SMALL])))
    gfin.update(zip(REP, _unpack(reduced[-1], [a[n].shape for n in REP])))

    out = {"loss": loss, "grad_x": grad_x[None]}
    small_names = list(SH_SMALL) + REP
    for n in SH_BIG:
        shp = a[n].shape
        two_d = lambda t: t.reshape(-1, shp[-1])
        d, mo, vo = _adamw(two_d(a[n]), two_d(gfin[n]), two_d(a["m_" + n]), two_d(a["v_" + n]), name=f"adamw_{n}")
        out["delta_" + n], out["new_m_" + n], out["new_v_" + n] = d.reshape(shp), mo.reshape(shp), vo.reshape(shp)
    two_d = lambda t: t.reshape(-1, t.shape[-1])
    res = _adamw_many(*[[two_d(src(n)) for n in small_names]
                        for src in (lambda n: a[n], lambda n: gfin[n], lambda n: a["m_" + n], lambda n: a["v_" + n])])
    for i, n in enumerate(small_names):
        out["delta_" + n], out["new_m_" + n], out["new_v_" + n] = (r.reshape(a[n].shape) for r in res[3 * i:3 * i + 3])
    for n in WEIGHTS:
        out["grad_" + n] = gfin[n]
    return out


def kernel(x, norm_mix, norm_ffn, norm_final, ev_w_in, ev_conv_w, ev_conv_b, ev_gate_a_w, ev_gate_a_b, ev_gate_x_w, ev_gate_x_b, ev_lru_lambda, hg_lb_logits, ev_hg_norm, ev_w_out, od_w_in, od_b_in, od_ln_g, od_ln_b, od_w_s, od_b_s, od_w_out, ffn_w_up, ffn_conv_w, ffn_conv_b, ffn_w_down, loss_target, m_norm_mix, m_norm_ffn, m_norm_final, m_ev_w_in, m_ev_conv_w, m_ev_conv_b, m_ev_gate_a_w, m_ev_gate_a_b, m_ev_gate_x_w, m_ev_gate_x_b, m_ev_lru_lambda, m_hg_lb_logits, m_ev_hg_norm, m_ev_w_out, m_od_w_in, m_od_b_in, m_od_ln_g, m_od_ln_b, m_od_w_s, m_od_b_s, m_od_w_out, m_ffn_w_up, m_ffn_conv_w, m_ffn_conv_b, m_ffn_w_down, v_norm_mix, v_norm_ffn, v_norm_final, v_ev_w_in, v_ev_conv_w, v_ev_conv_b, v_ev_gate_a_w, v_ev_gate_a_b, v_ev_gate_x_w, v_ev_gate_x_b, v_ev_lru_lambda, v_hg_lb_logits, v_ev_hg_norm, v_ev_w_out, v_od_w_in, v_od_b_in, v_od_ln_g, v_od_ln_b, v_od_w_s, v_od_b_s, v_od_w_out, v_ffn_w_up, v_ffn_conv_w, v_ffn_conv_b, v_ffn_w_down):
    vals = (x, norm_mix, norm_ffn, norm_final, ev_w_in, ev_conv_w, ev_conv_b, ev_gate_a_w, ev_gate_a_b, ev_gate_x_w, ev_gate_x_b, ev_lru_lambda, hg_lb_logits, ev_hg_norm, ev_w_out, od_w_in, od_b_in, od_ln_g, od_ln_b, od_w_s, od_b_s, od_w_out, ffn_w_up, ffn_conv_w, ffn_conv_b, ffn_w_down, loss_target, m_norm_mix, m_norm_ffn, m_norm_final, m_ev_w_in, m_ev_conv_w, m_ev_conv_b, m_ev_gate_a_w, m_ev_gate_a_b, m_ev_gate_x_w, m_ev_gate_x_b, m_ev_lru_lambda, m_hg_lb_logits, m_ev_hg_norm, m_ev_w_out, m_od_w_in, m_od_b_in, m_od_ln_g, m_od_ln_b, m_od_w_s, m_od_b_s, m_od_w_out, m_ffn_w_up, m_ffn_conv_w, m_ffn_conv_b, m_ffn_w_down, v_norm_mix, v_norm_ffn, v_norm_final, v_ev_w_in, v_ev_conv_w, v_ev_conv_b, v_ev_gate_a_w, v_ev_gate_a_b, v_ev_gate_x_w, v_ev_gate_x_b, v_ev_lru_lambda, v_hg_lb_logits, v_ev_hg_norm, v_ev_w_out, v_od_w_in, v_od_b_in, v_od_ln_g, v_od_ln_b, v_od_w_s, v_od_b_s, v_od_w_out, v_ffn_w_up, v_ffn_conv_w, v_ffn_conv_b, v_ffn_w_down)
    names = ["x"] + WEIGHTS + ["loss_target"] + ["m_" + n for n in WEIGHTS] + ["v_" + n for n in WEIGHTS]
    out = _step(dict(zip(names, vals)))
    return (out["loss"], out["grad_x"], *[out["grad_" + n] for n in WEIGHTS], *[out["delta_" + n] for n in WEIGHTS],
            *[out["new_m_" + n] for n in WEIGHTS], *[out["new_v_" + n] for n in WEIGHTS])
```

```python
import functools

import jax
import jax.numpy as jnp
from jax import lax
from jax.experimental import pallas as pl
from jax.experimental.pallas import tpu as pltpu

F32 = jnp.float32
BF16 = jnp.bfloat16

EPS = 1e-6
D_MODEL = 1024
LRU_W = 512
LRU_BLOCKS = 8
LRU_C = 8.0
HG_HEADS = 4
HG_D = 128
HG_CHUNK = 64
SGU_G = 8
SGU_CHUNK = 128
D_FF = 2816
ADAM_LR, ADAM_B1, ADAM_B2, ADAM_EPS, ADAM_WD, ADAM_STEP = 0.001, 0.9, 0.999, 1e-08, 0.01, 10

V7X_VMEM_BYTES = 64 * 1024 * 1024
VMEM_LIMIT = V7X_VMEM_BYTES - 8 * 1024 * 1024
SUBLANES = 8
LANES = 128
BF16_ROWS = 16

GELU_C0 = 0.7978845608028654
GELU_C1 = 0.044715

NN = (((1,), (0,)), ((), ()))
NT = (((1,), (1,)), ((), ()))
TN = (((0,), (0,)), ((), ()))


def _dot(a, b, dims=NN):
    return lax.dot_general(a.astype(BF16), b.astype(BF16), dims, preferred_element_type=F32)


def _cp(n_grid):
    return pltpu.CompilerParams(dimension_semantics=("arbitrary",) * n_grid, vmem_limit_bytes=VMEM_LIMIT)


def _chunk(n, cap):
    best = LANES
    for c in range(LANES, cap + 1, LANES):
        if n % c == 0:
            best = c
    return best


def _resident(shape):
    nd = len(shape)
    return pl.BlockSpec(shape, lambda *_: (0,) * nd, pipeline_mode=pl.Buffered(1))


def _rsum8(x):
    r, c = x.shape
    return x.reshape(r // SUBLANES, SUBLANES, c).sum(axis=0)


def _sigmoid(x):
    return 0.5 * jnp.tanh(0.5 * x) + 0.5


def _gelu(x):
    return 0.5 * x * (1.0 + jnp.tanh(GELU_C0 * (x + GELU_C1 * x * x * x)))


def _gelu_grad(x):
    t = jnp.tanh(GELU_C0 * (x + GELU_C1 * x * x * x))
    return 0.5 * (1.0 + t) + 0.5 * x * (1.0 - t * t) * GELU_C0 * (1.0 + 3.0 * GELU_C1 * x * x)


def _silu_and_grad(x):
    s = _sigmoid(x)
    return x * s, s * (1.0 + x * (1.0 - s))


def _shift_rows(e, j):
    n = e.shape[0]
    return e if j % n == 0 else pltpu.roll(e, j % n, 0)


def _weight(w):
    if isinstance(w, tuple):
        stack, layer = w
        K, N = stack.shape[1:]
        return stack, pl.BlockSpec((None, K, N), lambda *_: (layer, 0, 0), pipeline_mode=pl.Buffered(1)), (K, N)
    return w, _resident(w.shape), w.shape


def _norm_mm(h, g, w, b, name, tt=1024):
    T, D = h.shape
    w, w_spec, (_, N) = _weight(w)
    cn = _chunk(N, 512)

    def body(h_ref, g_ref, w_ref, b_ref, hn_ref, z_ref):
        x = h_ref[...]
        r = lax.rsqrt(jnp.mean(x * x, axis=-1, keepdims=True) + EPS)
        hn = (x * r * g_ref[...]).astype(BF16)
        hn_ref[...] = hn
        for j in range(0, N, cn):
            acc = jnp.dot(hn, w_ref[:, j:j + cn], preferred_element_type=F32) + b_ref[:, j:j + cn]
            z_ref[:, j:j + cn] = acc.astype(BF16)

    return pl.pallas_call(
        body, name=name, grid=(T // tt,),
        in_specs=[pl.BlockSpec((tt, D), lambda i: (i, 0)), _resident((1, D)), w_spec, _resident((1, N))],
        out_specs=[pl.BlockSpec((tt, D), lambda i: (i, 0)), pl.BlockSpec((tt, N), lambda i: (i, 0))],
        out_shape=[jax.ShapeDtypeStruct((T, D), BF16), jax.ShapeDtypeStruct((T, N), BF16)],
        compiler_params=_cp(1),
    )(h, g, w, b)


def _mm(a, w, res, out_dtype, name, tt=1024, transpose_w=False):
    T, K = a.shape
    w, w_spec, w_shape = _weight(w)
    N = w_shape[0] if transpose_w else w_shape[1]
    cn = _chunk(N, 512)
    has_res = res is not None

    def body(*refs):
        a_ref, w_ref = refs[0], refs[1]
        res_ref = refs[2] if has_res else None
        o_ref = refs[-1]
        av = a_ref[...].astype(BF16)
        for j in range(0, N, cn):
            if transpose_w:
                acc = lax.dot_general(av, w_ref[j:j + cn, :], NT, preferred_element_type=F32)
            else:
                acc = jnp.dot(av, w_ref[:, j:j + cn], preferred_element_type=F32)
            if has_res:
                acc = acc + res_ref[:, j:j + cn]
            o_ref[:, j:j + cn] = acc.astype(out_dtype)

    in_specs = [pl.BlockSpec((tt, K), lambda i: (i, 0)), w_spec]
    args = [a, w]
    if has_res:
        in_specs.append(pl.BlockSpec((tt, N), lambda i: (i, 0)))
        args.append(res)
    return pl.pallas_call(
        body, name=name, grid=(T // tt,), in_specs=in_specs,
        out_specs=pl.BlockSpec((tt, N), lambda i: (i, 0)),
        out_shape=jax.ShapeDtypeStruct((T, N), out_dtype),
        compiler_params=_cp(1),
    )(*args)


def _mm_tn(a, b, name, col_shards=1, tt=2048):
    T, K = a.shape
    N = b.shape[1]
    ns = N // col_shards
    tt = min(tt, T)
    while 2 * (tt * K * a.dtype.itemsize + tt * ns * b.dtype.itemsize + K * ns * 4) > VMEM_LIMIT - 12 * 1024 * 1024:
        tt //= 2

    def body(a_ref, b_ref, o_ref):
        acc = lax.dot_general(a_ref[...].astype(BF16), b_ref[...].astype(BF16), TN, preferred_element_type=F32)
        prev = jnp.where(pl.program_id(1) == 0, 0.0, o_ref[0])
        o_ref[0] = prev + acc

    out = pl.pallas_call(
        body, name=name, grid=(col_shards, T // tt),
        in_specs=[pl.BlockSpec((tt, K), lambda n, t: (t, 0)), pl.BlockSpec((tt, ns), lambda n, t: (t, n))],
        out_specs=pl.BlockSpec((1, K, ns), lambda n, t: (n, 0, 0)),
        out_shape=jax.ShapeDtypeStruct((col_shards, K, ns), F32),
        compiler_params=_cp(2),
    )(a, b)
    return out if col_shards > 1 else out[0]


def _mm_normbwd(dz, w, x, g, dres, name):
    T, N = dz.shape
    D = w.shape[0]
    tt = 1024 if N <= 3072 else 512

    def body(dz_ref, wt_ref, x_ref, g_ref, dres_ref, dx_ref, dg_ref):
        @pl.when(pl.program_id(0) == 0)
        def _():
            dg_ref[...] = jnp.zeros_like(dg_ref)

        dy = lax.dot_general(dz_ref[...], wt_ref[...], NT, preferred_element_type=F32)
        x = x_ref[...]
        r = lax.rsqrt(jnp.mean(x * x, axis=-1, keepdims=True) + EPS)
        xn = x * r
        dg_ref[...] += _rsum8(dy * xn)
        dxn = dy * g_ref[...]
        dx_ref[...] = dres_ref[...] + r * (dxn - xn * jnp.mean(dxn * xn, axis=-1, keepdims=True))

    return pl.pallas_call(
        body, name=name, grid=(T // tt,),
        in_specs=[pl.BlockSpec((tt, N), lambda i: (i, 0)), _resident((D, N)), pl.BlockSpec((tt, D), lambda i: (i, 0)),
                  _resident((1, D)), pl.BlockSpec((tt, D), lambda i: (i, 0))],
        out_specs=[pl.BlockSpec((tt, D), lambda i: (i, 0)), pl.BlockSpec((SUBLANES, D), lambda i: (0, 0))],
        out_shape=[jax.ShapeDtypeStruct((T, D), F32), jax.ShapeDtypeStruct((SUBLANES, D), F32)],
        compiler_params=_cp(1),
    )(dz, w, x, g, dres)


def _final_loss(h, g, tgt, name="final_loss", tt=512):
    T, D = h.shape

    def body(h_ref, g_ref, t_ref, dh_ref, dg_ref, sq_ref):
        @pl.when(pl.program_id(0) == 0)
        def _():
            dg_ref[...] = jnp.zeros_like(dg_ref)
            sq_ref[...] = jnp.zeros_like(sq_ref)

        x = h_ref[...]
        r = lax.rsqrt(jnp.mean(x * x, axis=-1, keepdims=True) + EPS)
        xn = x * r
        gv = g_ref[...]
        diff = xn * gv - t_ref[...]
        sq_ref[...] += _rsum8(diff * diff)
        dy = diff * (1.0 / D)
        dg_ref[...] += _rsum8(dy * xn)
        dxn = dy * gv
        dh_ref[...] = r * (dxn - xn * jnp.mean(dxn * xn, axis=-1, keepdims=True))

    return pl.pallas_call(
        body, name=name, grid=(T // tt,),
        in_specs=[pl.BlockSpec((tt, D), lambda i: (i, 0)), _resident((1, D)), pl.BlockSpec((tt, D), lambda i: (i, 0))],
        out_specs=[pl.BlockSpec((tt, D), lambda i: (i, 0)), pl.BlockSpec((SUBLANES, D), lambda i: (0, 0)),
                   pl.BlockSpec((SUBLANES, D), lambda i: (0, 0))],
        out_shape=[jax.ShapeDtypeStruct((T, D), F32), jax.ShapeDtypeStruct((SUBLANES, D), F32),
                   jax.ShapeDtypeStruct((SUBLANES, D), F32)],
        compiler_params=_cp(1),
    )(h, g, tgt)


def _col_groups(F, cc, per_group=4):
    step = cc * per_group
    return [(g0, min(g0 + step, F)) for g0 in range(0, F, step)]


def _ffn_act(gu, cw, cb, w_down, res, name, tt=512):
    T = gu.shape[0]
    F = gu.shape[1] // 2
    w_down, wd_spec, (_, D) = _weight(w_down)
    cc = _chunk(F, 256)
    hb = tt // BF16_ROWS

    def body(gu_ref, halo_ref, cw_ref, cb_ref, wd_ref, res_ref, a_ref, gc_ref, o_ref):
        first = pl.program_id(0) == 0
        acc = res_ref[...]
        for g0, g1 in _col_groups(F, cc):
            for c0 in range(g0, g1, cc):
                cs = slice(c0, c0 + cc)
                x = gu_ref[:, cs].astype(F32)
                halo = jnp.where(first, 0.0, halo_ref[:, cs].astype(F32))
                e = jnp.concatenate([halo, x], axis=0)
                gc = (cb_ref[:, cs] + cw_ref[0:1, cs] * _shift_rows(e, 2)[BF16_ROWS:] + cw_ref[1:2, cs] * _shift_rows(e, 1)[BF16_ROWS:]
                      + cw_ref[2:3, cs] * x)
                up = gu_ref[:, F + c0:F + c0 + cc].astype(F32)
                gc_ref[:, cs] = gc.astype(BF16)
                a_ref[:, cs] = (gc * _sigmoid(gc) * up).astype(BF16)
            acc = acc + jnp.dot(a_ref[:, g0:g1], wd_ref[g0:g1, :], preferred_element_type=F32)
        o_ref[...] = acc

    return pl.pallas_call(
        body, name=name, grid=(T // tt,),
        in_specs=[pl.BlockSpec((tt, 2 * F), lambda i: (i, 0)),
                  pl.BlockSpec((BF16_ROWS, F), lambda i: (jnp.maximum(i * hb - 1, 0), 0)),
                  _resident((SUBLANES, F)), _resident((1, F)), wd_spec, pl.BlockSpec((tt, D), lambda i: (i, 0))],
        out_specs=[pl.BlockSpec((tt, F), lambda i: (i, 0)), pl.BlockSpec((tt, F), lambda i: (i, 0)),
                   pl.BlockSpec((tt, D), lambda i: (i, 0))],
        out_shape=[jax.ShapeDtypeStruct((T, F), BF16), jax.ShapeDtypeStruct((T, F), BF16), jax.ShapeDtypeStruct((T, D), F32)],
        compiler_params=_cp(1),
    )(gu, gu, cw, cb, w_down, res)


def _ffn_act_bwd(gu, gc, da, cw, w_up, x, g, dres, name, tt=256):
    T = gu.shape[0]
    F = gu.shape[1] // 2
    w_up, wu_spec, (D, _) = _weight(w_up)
    cc = _chunk(F, 256)
    hb = tt // BF16_ROWS
    last_hb = T // BF16_ROWS - 1
    nt = T // tt

    def body(gu_ref, upnext_ref, gc_ref, gcnext_ref, da_ref, danext_ref, cw_ref, wu_ref, x_ref, g_ref, dres_ref,
             dgu_ref, dc_ref, dx_ref, dg_ref):
        i = pl.program_id(0)

        @pl.when(i == 0)
        def _():
            dc_ref[...] = jnp.zeros_like(dc_ref)
            dg_ref[...] = jnp.zeros_like(dg_ref)

        n = tt + BF16_ROWS
        ext = lambda main, nxt: jnp.concatenate([main.astype(F32), nxt.astype(F32)], axis=0)
        dy = jnp.zeros((tt, D), F32)
        for lo, hi in _col_groups(F, cc):
            for c0 in range(lo, hi, cc):
                cs = slice(c0, c0 + cc)
                us = slice(F + c0, F + c0 + cc)
                gc = ext(gc_ref[:, cs], gcnext_ref[:, cs])
                up = ext(gu_ref[:, us], upnext_ref[:, cs])
                dae = ext(da_ref[:, cs], jnp.where(i == nt - 1, 0.0, danext_ref[:, cs].astype(F32)))
                s, ds = _silu_and_grad(gc)
                dgc = dae * up * ds
                dgu_ref[:, us] = (dae * s)[:tt].astype(BF16)
                dgc1 = _shift_rows(dgc, n - 1)[:tt]
                dgc2 = _shift_rows(dgc, n - 2)[:tt]
                dm = dgc[:tt]
                dgu_ref[:, cs] = (cw_ref[2:3, cs] * dm + cw_ref[1:2, cs] * dgc1 + cw_ref[0:1, cs] * dgc2).astype(BF16)
                gt = gu_ref[:, cs].astype(F32)
                dc_ref[0, :, cs] += _rsum8(dgc2 * gt)
                dc_ref[1, :, cs] += _rsum8(dgc1 * gt)
                dc_ref[2, :, cs] += _rsum8(dm * gt)
                dc_ref[3, :, cs] += _rsum8(dm)
            dy = (dy + lax.dot_general(dgu_ref[:, lo:hi], wu_ref[:, lo:hi], NT, preferred_element_type=F32)
                  + lax.dot_general(dgu_ref[:, F + lo:F + hi], wu_ref[:, F + lo:F + hi], NT, preferred_element_type=F32))
        xv = x_ref[...]
        r = lax.rsqrt(jnp.mean(xv * xv, axis=-1, keepdims=True) + EPS)
        xn = xv * r
        dg_ref[...] += _rsum8(dy * xn)
        dxn = dy * g_ref[...]
        dx_ref[...] = dres_ref[...] + r * (dxn - xn * jnp.mean(dxn * xn, axis=-1, keepdims=True))

    tok = lambda w: pl.BlockSpec((tt, w), lambda i: (i, 0))
    nxt = lambda col: pl.BlockSpec((BF16_ROWS, F), lambda i: (jnp.minimum((i + 1) * hb, last_hb), col))
    return pl.pallas_call(
        body, name=name, grid=(nt,),
        in_specs=[tok(2 * F), nxt(1), tok(F), nxt(0), tok(F), nxt(0),
                  _resident((SUBLANES, F)), wu_spec, tok(D), _resident((1, D)), tok(D)],
        out_specs=[tok(2 * F), pl.BlockSpec((4, SUBLANES, F), lambda i: (0, 0, 0)), tok(D),
                   pl.BlockSpec((SUBLANES, D), lambda i: (0, 0))],
        out_shape=[jax.ShapeDtypeStruct((T, 2 * F), BF16), jax.ShapeDtypeStruct((4, SUBLANES, F), F32),
                   jax.ShapeDtypeStruct((T, D), F32), jax.ShapeDtypeStruct((SUBLANES, D), F32)],
        compiler_params=_cp(1),
    )(gu, gu, gc, gc, da, da, cw, w_up, x, g, dres)


def _softplus_neg(lam):
    x = -lam
    y = jnp.exp(-jnp.abs(x))
    l1p = jnp.where(y < 0.01, y * (1.0 - y * (0.5 - y * (1.0 / 3.0))), jnp.log(1.0 + y))
    return jnp.maximum(x, 0.0) + l1p


def _lru_gates(xc, wa_ref, ba_ref, wx_ref, bx_ref, sp):
    xcb = xc.astype(BF16)
    r = _sigmoid(jnp.dot(xcb, wa_ref[...], preferred_element_type=F32) + ba_ref[...])
    gi = _sigmoid(jnp.dot(xcb, wx_ref[...], preferred_element_type=F32) + bx_ref[...])
    log_a = -LRU_C * r * sp
    a = jnp.exp(log_a)
    x2 = 2.0 * log_a
    series = -x2 * (1.0 + x2 * 0.5 * (1.0 + x2 * (1.0 / 3.0) * (1.0 + x2 * 0.25 * (1.0 + x2 * 0.2))))
    om = jnp.where(x2 > -0.125, series, 1.0 - a * a)
    return r, gi, a, jnp.sqrt(om)


def _lru_conv(xr, halo, cw_ref, cb_ref):
    e = jnp.concatenate([halo, xr], axis=0)
    x1 = _shift_rows(e, 1)[BF16_ROWS:]
    x2 = _shift_rows(e, 2)[BF16_ROWS:]
    x3 = _shift_rows(e, 3)[BF16_ROWS:]
    xc = cb_ref[...] + cw_ref[0:1, :] * x3 + cw_ref[1:2, :] * x2 + cw_ref[2:3, :] * x1 + cw_ref[3:4, :] * xr
    return xc, x1, x2, x3


def _lru_fwd(z, cw, cb, wa, ba, wx, bx, lam, name="lru_fwd", tt=256):
    T = z.shape[0]
    W = LRU_W
    hb = tt // BF16_ROWS
    ng = tt // SUBLANES

    def body(z_ref, halo_ref, cw_ref, cb_ref, wa_ref, ba_ref, wx_ref, bx_ref, lam_ref, oa_ref, h_ref, a_s, u_s, hc):
        i = pl.program_id(0)

        @pl.when(i == 0)
        def _():
            hc[...] = jnp.zeros_like(hc)

        xr = z_ref[:, W:2 * W].astype(F32)
        halo = jnp.where(i == 0, 0.0, halo_ref[...].astype(F32))
        xc, _, _, _ = _lru_conv(xr, halo, cw_ref, cb_ref)
        sp = _softplus_neg(lam_ref[...])
        r, gi, a, mult = _lru_gates(xc, wa_ref, ba_ref, wx_ref, bx_ref, sp)
        a_s[...] = a
        u_s[...] = mult * gi * xc
        row = lax.broadcasted_iota(jnp.int32, (SUBLANES, W), 0)

        def step(j, hprev):
            r0 = pl.multiple_of(j * SUBLANES, SUBLANES)
            A = a_s[pl.ds(r0, SUBLANES), :]
            U = u_s[pl.ds(r0, SUBLANES), :]
            for k in (1, 2, 4):
                m = row >= k
                U = jnp.where(m, A * pltpu.roll(U, k, 0) + U, U)
                A = jnp.where(m, A * pltpu.roll(A, k, 0), A)
            H = U + A * hprev
            h_ref[pl.ds(r0, SUBLANES), :] = H
            return jnp.broadcast_to(H[SUBLANES - 1:SUBLANES, :], (SUBLANES, W))

        hc[...] = lax.fori_loop(0, ng, step, hc[...])
        oa_ref[...] = (_gelu(z_ref[:, 0:W].astype(F32)) * h_ref[...]).astype(BF16)

    return pl.pallas_call(
        body, name=name, grid=(T // tt,),
        in_specs=[pl.BlockSpec((tt, 2 * W), lambda i: (i, 0)),
                  pl.BlockSpec((BF16_ROWS, W), lambda i: (jnp.maximum(i * hb - 1, 0), 1)),
                  _resident((SUBLANES, W)), _resident((1, W)), _resident((W, W)), _resident((1, W)),
                  _resident((W, W)), _resident((1, W)), _resident((1, W))],
        out_specs=[pl.BlockSpec((tt, W), lambda i: (i, 0)), pl.BlockSpec((tt, W), lambda i: (i, 0))],
        out_shape=[jax.ShapeDtypeStruct((T, W), BF16), jax.ShapeDtypeStruct((T, W), F32)],
        scratch_shapes=[pltpu.VMEM((tt, W), F32), pltpu.VMEM((tt, W), F32), pltpu.VMEM((SUBLANES, W), F32)],
        compiler_params=_cp(1),
    )(z, z, cw, cb, wa, ba, wx, bx, lam)


def _lru_bwd(z, hseq, dmix, cw, cb, wa, wat, ba, wx, wxt, bx, lam, name="lru_bwd", tt=256):
    T = z.shape[0]
    W = LRU_W
    nt = T // tt
    hb = tt // BF16_ROWS
    sb = tt // SUBLANES
    ng = tt // SUBLANES

    def body(z_ref, halo_ref, h_ref, hprev_ref, dm_ref, cw_ref, cb_ref, wa_ref, wat_ref, ba_ref, wx_ref, wxt_ref, bx_ref,
             lam_ref, dz_ref, dc_ref, dwa_ref, dwx_ref, dv_ref, c_s, d_s, g_s, gc, an, dxn):
        i = pl.program_id(0)
        ti = nt - 1 - i

        @pl.when(i == 0)
        def _():
            dc_ref[...] = jnp.zeros_like(dc_ref)
            dwa_ref[...] = jnp.zeros_like(dwa_ref)
            dwx_ref[...] = jnp.zeros_like(dwx_ref)
            dv_ref[...] = jnp.zeros_like(dv_ref)
            gc[...] = jnp.zeros_like(gc)
            an[...] = jnp.zeros_like(an)
            dxn[...] = jnp.zeros_like(dxn)

        xr = z_ref[:, W:2 * W].astype(F32)
        yg = z_ref[:, 0:W].astype(F32)
        halo = jnp.where(ti == 0, 0.0, halo_ref[...].astype(F32))
        xc, x1, x2, x3 = _lru_conv(xr, halo, cw_ref, cb_ref)
        sp = _softplus_neg(lam_ref[...])
        r, gi, a, mult = _lru_gates(xc, wa_ref, ba_ref, wx_ref, bx_ref, sp)
        h = h_ref[...]
        hp = jnp.where(ti == 0, 0.0, hprev_ref[...])
        hm1 = _shift_rows(jnp.concatenate([hp, h], axis=0), 1)[SUBLANES:]
        dout = dm_ref[...].astype(F32)
        d_s[...] = dout * _gelu(yg)
        dz_ref[:, 0:W] = (dout * h * _gelu_grad(yg)).astype(BF16)
        c_s[...] = _shift_rows(jnp.concatenate([a, an[...]], axis=0), tt + SUBLANES - 1)[:tt]
        an[...] = a[0:SUBLANES, :]
        row = lax.broadcasted_iota(jnp.int32, (SUBLANES, W), 0)

        def step(j, gnext):
            r0 = pl.multiple_of((ng - 1 - j) * SUBLANES, SUBLANES)
            C = c_s[pl.ds(r0, SUBLANES), :]
            G = d_s[pl.ds(r0, SUBLANES), :]
            for k in (1, 2, 4):
                m = row < SUBLANES - k
                G = jnp.where(m, G + C * pltpu.roll(G, SUBLANES - k, 0), G)
                C = jnp.where(m, C * pltpu.roll(C, SUBLANES - k, 0), C)
            G = G + C * gnext
            g_s[pl.ds(r0, SUBLANES), :] = G
            return jnp.broadcast_to(G[0:1, :], (SUBLANES, W))

        gc[...] = lax.fori_loop(0, ng, step, gc[...])
        du = g_s[...]
        da = du * hm1
        dgi = du * mult * xc
        dxc = du * mult * gi
        dmult = du * gi * xc
        dlog_a = da * a - dmult * (a * a) / mult
        dr = dlog_a * (-LRU_C * sp)
        dv_ref[2] += _rsum8(dlog_a * (-LRU_C * r))
        dpr = (dr * r * (1.0 - r)).astype(BF16)
        dpi = (dgi * gi * (1.0 - gi)).astype(BF16)
        dv_ref[0] += _rsum8(dpr.astype(F32))
        dv_ref[1] += _rsum8(dpi.astype(F32))
        xcb = xc.astype(BF16)
        dwa_ref[...] += lax.dot_general(xcb, dpr, TN, preferred_element_type=F32)
        dwx_ref[...] += lax.dot_general(xcb, dpi, TN, preferred_element_type=F32)
        dxc = dxc + jnp.dot(dpr, wat_ref[...], preferred_element_type=F32) + jnp.dot(dpi, wxt_ref[...], preferred_element_type=F32)
        n = tt + BF16_ROWS
        de = jnp.concatenate([dxc, dxn[...]], axis=0)
        dxr = (cw_ref[3:4, :] * dxc + cw_ref[2:3, :] * _shift_rows(de, n - 1)[:tt] + cw_ref[1:2, :] * _shift_rows(de, n - 2)[:tt]
               + cw_ref[0:1, :] * _shift_rows(de, n - 3)[:tt])
        dxn[...] = dxc[0:BF16_ROWS, :]
        dz_ref[:, W:2 * W] = dxr.astype(BF16)
        dc_ref[0] += _rsum8(dxc * x3)
        dc_ref[1] += _rsum8(dxc * x2)
        dc_ref[2] += _rsum8(dxc * x1)
        dc_ref[3] += _rsum8(dxc * xr)
        dc_ref[4] += _rsum8(dxc)

    rev = lambda i: nt - 1 - i
    return pl.pallas_call(
        body, name=name, grid=(nt,),
        in_specs=[pl.BlockSpec((tt, 2 * W), lambda i: (rev(i), 0)),
                  pl.BlockSpec((BF16_ROWS, W), lambda i: (jnp.maximum(rev(i) * hb - 1, 0), 1)),
                  pl.BlockSpec((tt, W), lambda i: (rev(i), 0)),
                  pl.BlockSpec((SUBLANES, W), lambda i: (jnp.maximum(rev(i) * sb - 1, 0), 0)),
                  pl.BlockSpec((tt, W), lambda i: (rev(i), 0)),
                  _resident((SUBLANES, W)), _resident((1, W)), _resident((W, W)), _resident((W, W)), _resident((1, W)),
                  _resident((W, W)), _resident((W, W)), _resident((1, W)), _resident((1, W))],
        out_specs=[pl.BlockSpec((tt, 2 * W), lambda i: (rev(i), 0)),
                   pl.BlockSpec((5, SUBLANES, W), lambda i: (0, 0, 0)),
                   pl.BlockSpec((W, W), lambda i: (0, 0)), pl.BlockSpec((W, W), lambda i: (0, 0)),
                   pl.BlockSpec((3, SUBLANES, W), lambda i: (0, 0, 0))],
        out_shape=[jax.ShapeDtypeStruct((T, z.shape[1]), BF16), jax.ShapeDtypeStruct((5, SUBLANES, W), F32),
                   jax.ShapeDtypeStruct((W, W), F32), jax.ShapeDtypeStruct((W, W), F32),
                   jax.ShapeDtypeStruct((3, SUBLANES, W), F32)],
        scratch_shapes=[pltpu.VMEM((tt, W), F32), pltpu.VMEM((tt, W), F32), pltpu.VMEM((tt, W), F32),
                        pltpu.VMEM((SUBLANES, W), F32), pltpu.VMEM((SUBLANES, W), F32), pltpu.VMEM((BF16_ROWS, W), F32)],
        compiler_params=_cp(1),
    )(z, z, hseq, hseq, dmix, cw, cb, wa, wat, ba, wx, wxt, bx, lam)


def _split3(x):
    hi = x.astype(BF16)
    r1 = x - hi.astype(F32)
    mid = r1.astype(BF16)
    lo = (r1 - mid.astype(F32)).astype(BF16)
    return hi, mid, lo


def _tri_matmul(tri, x):
    hi, mid, lo = _split3(x)
    return (jnp.dot(tri, hi, preferred_element_type=F32) + jnp.dot(tri, mid, preferred_element_type=F32)
            + jnp.dot(tri, lo, preferred_element_type=F32))


def _hg_chunk(q, fl, lb):
    C = q.shape[0]
    ri = lax.broadcasted_iota(jnp.int32, (C, C), 0)
    ci = lax.broadcasted_iota(jnp.int32, (C, C), 1)
    causal = ri >= ci
    sig = _sigmoid(fl)
    f = lb + (1.0 - lb) * sig
    k = 1.0 - f
    sq = _sigmoid(q)
    qf = q * sq
    b = _tri_matmul(causal.astype(BF16), jnp.log(f))
    bm = b[C // 2 - 1:C // 2, :]
    bl = b[C - 1:C, :]
    qt = qf * jnp.exp(b - bm)
    kt = k * jnp.exp(bm - b)
    qin = qf * jnp.exp(b)
    kout = k * jnp.exp(bl - b)
    qtb, ktb = qt.astype(BF16), kt.astype(BF16)
    att = [jnp.where(causal, _dot(qtb[:, _head(h)], ktb[:, _head(h)], NT), 0.0).astype(BF16) for h in range(HG_HEADS)]
    return dict(sig=sig, f=f, k=k, sq=sq, qf=qf, b=b, bm=bm, bl=bl, qt=qt, kt=kt, qin=qin, kout=kout, att=att,
                causal=causal, anti=ri <= ci, decay=jnp.exp(bl))


def _head(h):
    return slice(h * HG_D, (h + 1) * HG_D)


def _hgrn_fwd(z, lb, gn, name="hgrn_fwd", tt=256):
    T = z.shape[0]
    C = HG_CHUNK
    nc = tt // C
    Dh = HG_D
    Wd = HG_HEADS * Dh

    def body(q_ref, f_ref, v_ref, g_ref, lb_ref, gn_ref, o_ref, ss_ref, st):
        @pl.when(pl.program_id(0) == 0)
        def _():
            st[...] = jnp.zeros_like(st)

        S = [st[h] for h in range(HG_HEADS)]
        for c in range(nc):
            rows = slice(c * C, (c + 1) * C)
            ck = _hg_chunk(q_ref[rows, :].astype(F32), f_ref[rows, :].astype(F32), lb_ref[...])
            v = v_ref[rows, :]
            g = g_ref[rows, :].astype(F32)
            H = range(HG_HEADS)
            qinb, koutb = ck["qin"].astype(BF16), ck["kout"].astype(BF16)
            for h in H:
                ss_ref[h, c] = S[h]
            o = [_dot(ck["att"][h], v[:, _head(h)]) + _dot(qinb[:, _head(h)], S[h], NT) for h in H]
            S = [ck["decay"][:, _head(h)] * S[h] + _dot(v[:, _head(h)], koutb[:, _head(h)], TN) for h in H]
            outs = [o[h] * lax.rsqrt(jnp.mean(o[h] * o[h], axis=-1, keepdims=True) + EPS) * gn_ref[...] for h in H]
            o_ref[rows, :] = (jnp.concatenate(outs, axis=1) * (g * _sigmoid(g))).astype(BF16)
        for h in range(HG_HEADS):
            st[h] = S[h]

    col = lambda base: (lambda i: (i, base))
    return pl.pallas_call(
        body, name=name, grid=(T // tt,),
        in_specs=[pl.BlockSpec((tt, Wd), col(2)), pl.BlockSpec((tt, Wd), col(3)), pl.BlockSpec((tt, Wd), col(4)),
                  pl.BlockSpec((tt, Wd), col(5)), _resident((1, Wd)), _resident((1, Dh))],
        out_specs=[pl.BlockSpec((tt, Wd), lambda i: (i, 0)),
                   pl.BlockSpec((HG_HEADS, nc, Dh, Dh), lambda i: (0, i, 0, 0))],
        out_shape=[jax.ShapeDtypeStruct((T, Wd), BF16),
                   jax.ShapeDtypeStruct((HG_HEADS, T // C, Dh, Dh), F32)],
        scratch_shapes=[pltpu.VMEM((HG_HEADS, Dh, Dh), F32)],
        compiler_params=_cp(1),
    )(z, z, z, z, lb, gn)


def _hgrn_bwd(z, ss, dmix, lb, gn, dz, name="hgrn_bwd", tt=256):
    T = z.shape[0]
    C = HG_CHUNK
    nc = tt // C
    nt = T // tt
    Dh = HG_D
    Wd = HG_HEADS * Dh

    def body(q_ref, f_ref, v_ref, g_ref, ss_ref, dm_ref, lb_ref, gn_ref, dz01_ref, dz_ref, dlb_ref, dgn_ref, dst):
        @pl.when(pl.program_id(0) == 0)
        def _():
            dst[...] = jnp.zeros_like(dst)
            dlb_ref[...] = jnp.zeros_like(dlb_ref)
            dgn_ref[...] = jnp.zeros_like(dgn_ref)

        dS = [dst[h] for h in range(HG_HEADS)]
        lbv = lb_ref[...]
        gnv = gn_ref[...]
        rowc = lax.broadcasted_iota(jnp.int32, (C, Wd), 0)
        cat = lambda xs: jnp.concatenate(xs, axis=1)
        for c in reversed(range(nc)):
            rows = slice(c * C, (c + 1) * C)
            q = q_ref[rows, :].astype(F32)
            ck = _hg_chunk(q, f_ref[rows, :].astype(F32), lbv)
            v = v_ref[rows, :]
            g = g_ref[rows, :].astype(F32)
            dout = dm_ref[rows, :].astype(F32)
            sg, dsg = _silu_and_grad(g)
            d_ong = dout * sg
            H = range(HG_HEADS)
            qinb, koutb, qtb, ktb = (ck[n].astype(BF16) for n in ("qin", "kout", "qt", "kt"))
            S = [ss_ref[h, c] for h in H]
            Sb = [s.astype(BF16) for s in S]
            dSb = [d.astype(BF16) for d in dS]
            o = [_dot(ck["att"][h], v[:, _head(h)]) + _dot(qinb[:, _head(h)], Sb[h], NT) for h in H]
            rn = [lax.rsqrt(jnp.mean(o[h] * o[h], axis=-1, keepdims=True) + EPS) for h in H]
            on = [o[h] * rn[h] for h in H]
            don = [d_ong[:, _head(h)] * gnv for h in H]
            do = [(rn[h] * (don[h] - on[h] * jnp.mean(don[h] * on[h], axis=-1, keepdims=True))).astype(BF16) for h in H]
            datt = [jnp.where(ck["causal"], _dot(do[h], v[:, _head(h)], NT), 0.0).astype(BF16) for h in H]
            dvs = [_dot(ck["att"][h], do[h], TN) + _dot(koutb[:, _head(h)], dSb[h], NT) for h in H]
            dqins = [_dot(do[h], Sb[h]) for h in H]
            dkouts = [_dot(v[:, _head(h)], dSb[h]) for h in H]
            dqts = [_dot(datt[h], ktb[:, _head(h)]) for h in H]
            dkts = [_dot(datt[h], qtb[:, _head(h)], TN) for h in H]
            ddecays = [jnp.sum(dS[h] * S[h], axis=0, keepdims=True) for h in H]
            dS = [_dot(do[h], qinb[:, _head(h)], TN) + ck["decay"][:, _head(h)] * dS[h] for h in H]
            ons = [on[h] * gnv for h in H]
            dgn = _rsum8(d_ong[:, _head(0)] * on[0])
            for h in range(1, HG_HEADS):
                dgn = dgn + _rsum8(d_ong[:, _head(h)] * on[h])
            dgn_ref[...] += dgn
            dqt, dkt, dqin, dkout, ddecay = cat(dqts), cat(dkts), cat(dqins), cat(dkouts), cat(ddecays)
            b, bm, bl = ck["b"], ck["bm"], ck["bl"]
            dqf = dqt * jnp.exp(b - bm) + dqin * jnp.exp(b)
            dk = dkt * jnp.exp(bm - b) + dkout * jnp.exp(bl - b)
            kk = dkout * ck["kout"]
            db = dqt * ck["qt"] - dkt * ck["kt"] + dqin * ck["qin"] - kk
            dbl = jnp.sum(kk, axis=0, keepdims=True) + ddecay * ck["decay"]
            db = db + jnp.where(rowc == C - 1, dbl, 0.0)
            dlogf = _tri_matmul(ck["anti"].astype(BF16), db)
            dfv = dlogf / ck["f"] - dk
            sig, sq = ck["sig"], ck["sq"]
            dlb_ref[...] += _rsum8(dfv * (1.0 - sig))
            dz_ref[rows, 2 * Wd:3 * Wd] = (dqf * (sq * (1.0 + q * (1.0 - sq)))).astype(BF16)
            dz_ref[rows, 3 * Wd:4 * Wd] = (dfv * (1.0 - lbv) * sig * (1.0 - sig)).astype(BF16)
            dz_ref[rows, 4 * Wd:5 * Wd] = cat(dvs).astype(BF16)
            dz_ref[rows, 5 * Wd:6 * Wd] = (dout * cat(ons) * dsg).astype(BF16)
        dz_ref[:, 0:2 * Wd] = dz01_ref[...]
        for h in range(HG_HEADS):
            dst[h] = dS[h]

    rev = lambda i: nt - 1 - i
    col = lambda base: (lambda i: (rev(i), base))
    return pl.pallas_call(
        body, name=name, grid=(nt,),
        in_specs=[pl.BlockSpec((tt, Wd), col(2)), pl.BlockSpec((tt, Wd), col(3)), pl.BlockSpec((tt, Wd), col(4)),
                  pl.BlockSpec((tt, Wd), col(5)),
                  pl.BlockSpec((HG_HEADS, nc, Dh, Dh), lambda i: (0, rev(i), 0, 0)),
                  pl.BlockSpec((tt, Wd), col(1)), _resident((1, Wd)), _resident((1, Dh)),
                  pl.BlockSpec((tt, 2 * Wd), col(0))],
        out_specs=[pl.BlockSpec((tt, 6 * Wd), lambda i: (rev(i), 0)), pl.BlockSpec((SUBLANES, Wd), lambda i: (0, 0)),
                   pl.BlockSpec((SUBLANES, Dh), lambda i: (0, 0))],
        out_shape=[jax.ShapeDtypeStruct((T, 6 * Wd), BF16), jax.ShapeDtypeStruct((SUBLANES, Wd), F32),
                   jax.ShapeDtypeStruct((SUBLANES, Dh), F32)],
        input_output_aliases={8: 0},
        scratch_shapes=[pltpu.VMEM((HG_HEADS, Dh, Dh), F32)],
        compiler_params=_cp(1),
    )(z, z, z, z, ss, dmix, lb, gn, dz)


def _sgu_core(p, lg_ref, lb_ref, wsc_ref, bsb_ref):
    Wd = D_MODEL
    G = SGU_CHUNK
    zz = _gelu(p)
    u = zz[:, :Wd]
    v = zz[:, Wd:]
    vc = v - jnp.mean(v, axis=-1, keepdims=True)
    rstd = lax.rsqrt(jnp.mean(vc * vc, axis=-1, keepdims=True) + EPS)
    vhat = vc * rstd
    vn = vhat * lg_ref[...] + lb_ref[...]
    svs = []
    for gi in range(SGU_G):
        svs.append(jnp.dot(wsc_ref[gi], vn[:, gi * G:(gi + 1) * G].astype(BF16), preferred_element_type=F32) + bsb_ref[gi])
    return u, vhat, rstd, vn, jnp.concatenate(svs, axis=1)


def _sgu_fwd(p1, lg, lbias, wsc, bsb, name="sgu_fwd", tt=512):
    T = p1.shape[0]
    Wd = D_MODEL
    C = SGU_CHUNK

    def body(p_ref, lg_ref, lb_ref, wsc_ref, bsb_ref, s_ref):
        for c in range(tt // C):
            rows = slice(c * C, (c + 1) * C)
            u, _, _, _, sv = _sgu_core(p_ref[rows, :].astype(F32), lg_ref, lb_ref, wsc_ref, bsb_ref)
            s_ref[rows, :] = (u * sv).astype(BF16)

    return pl.pallas_call(
        body, name=name, grid=(T // tt,),
        in_specs=[pl.BlockSpec((tt, 2 * Wd), lambda i: (i, 0)), _resident((1, Wd)), _resident((1, Wd)),
                  _resident((SGU_G, C, C)), _resident((SGU_G, C, C))],
        out_specs=pl.BlockSpec((tt, Wd), lambda i: (i, 0)),
        out_shape=jax.ShapeDtypeStruct((T, Wd), BF16),
        compiler_params=_cp(1),
    )(p1, lg, lbias, wsc, bsb)


def _sgu_bwd(p1, ds, lg, lbias, wsc, wsct, bsb, name="sgu_bwd", tt=512):
    T = p1.shape[0]
    Wd = D_MODEL
    C = SGU_CHUNK

    def body(p_ref, ds_ref, lg_ref, lb_ref, wsc_ref, wsct_ref, bsb_ref, dp_ref, dws_ref, dbs_ref, dlg_ref, dlb_ref, dbin_ref):
        @pl.when(pl.program_id(0) == 0)
        def _():
            dws_ref[...] = jnp.zeros_like(dws_ref)
            dbs_ref[...] = jnp.zeros_like(dbs_ref)
            dlg_ref[...] = jnp.zeros_like(dlg_ref)
            dlb_ref[...] = jnp.zeros_like(dlb_ref)
            dbin_ref[...] = jnp.zeros_like(dbin_ref)

        for c in range(tt // C):
            rows = slice(c * C, (c + 1) * C)
            p = p_ref[rows, :].astype(F32)
            u, vhat, rstd, vn, sv = _sgu_core(p, lg_ref, lb_ref, wsc_ref, bsb_ref)
            dsc = ds_ref[rows, :].astype(F32)
            du = dsc * sv
            dsv = dsc * u
            dvns = []
            for gi in range(SGU_G):
                cs = slice(gi * C, (gi + 1) * C)
                dsv_g = dsv[:, cs]
                dvns.append(jnp.dot(wsct_ref[gi], dsv_g.astype(BF16), preferred_element_type=F32))
                dws_ref[gi] += _dot(dsv_g, vn[:, cs], NT)
                dbs_ref[gi] += dsv_g
            dvn = jnp.concatenate(dvns, axis=1)
            dlg_ref[...] += _rsum8(dvn * vhat)
            dlb_ref[...] += _rsum8(dvn)
            dvh = dvn * lg_ref[...]
            dv = rstd * (dvh - jnp.mean(dvh, axis=-1, keepdims=True) - vhat * jnp.mean(dvh * vhat, axis=-1, keepdims=True))
            dp = jnp.concatenate([du, dv], axis=1) * _gelu_grad(p)
            dbin_ref[...] += _rsum8(dp)
            dp_ref[rows, :] = dp.astype(BF16)

    full3 = pl.BlockSpec((SGU_G, C, C), lambda i: (0, 0, 0))
    return pl.pallas_call(
        body, name=name, grid=(T // tt,),
        in_specs=[pl.BlockSpec((tt, 2 * Wd), lambda i: (i, 0)), pl.BlockSpec((tt, Wd), lambda i: (i, 0)),
                  _resident((1, Wd)), _resident((1, Wd)), _resident((SGU_G, C, C)), _resident((SGU_G, C, C)),
                  _resident((SGU_G, C, C))],
        out_specs=[pl.BlockSpec((tt, 2 * Wd), lambda i: (i, 0)), full3, full3,
                   pl.BlockSpec((SUBLANES, Wd), lambda i: (0, 0)), pl.BlockSpec((SUBLANES, Wd), lambda i: (0, 0)),
                   pl.BlockSpec((SUBLANES, 2 * Wd), lambda i: (0, 0))],
        out_shape=[jax.ShapeDtypeStruct((T, 2 * Wd), BF16), jax.ShapeDtypeStruct((SGU_G, C, C), F32),
                   jax.ShapeDtypeStruct((SGU_G, C, C), F32), jax.ShapeDtypeStruct((SUBLANES, Wd), F32),
                   jax.ShapeDtypeStruct((SUBLANES, Wd), F32), jax.ShapeDtypeStruct((SUBLANES, 2 * Wd), F32)],
        compiler_params=_cp(1),
    )(p1, ds, lg, lbias, wsc, wsct, bsb)


def _pad_rows(w, rows=SUBLANES):
    return jnp.pad(w, ((0, rows - w.shape[0]), (0, 0)))


def _block_diag(w):
    n, b, _ = w.shape
    return (w[:, :, None, :] * jnp.eye(n, dtype=w.dtype)[:, None, :, None]).reshape(n * b, n * b)


def _diag_blocks(m, n):
    b = m.shape[0] // n
    m4 = m.reshape(n, b, n, b)
    return jnp.stack([m4[k, :, k, :] for k in range(n)], axis=0)


def _piece_major(dw):
    if dw.ndim == 2:
        K, N = dw.shape
        return dw.reshape(N_CHIPS, 2, K // (2 * N_CHIPS), N)
    _, K, ns = dw.shape
    return dw.reshape(N_CHIPS, 2, K // 2, ns)


def _ffn_fwd(h, g, w_up, cw, cb, w_down, tag):
    hn, gu = _norm_mm(h, g, w_up, jnp.zeros((1, 2 * D_FF), F32), name=f"ffn_up_{tag}")
    a, gc, out = _ffn_act(gu, cw, cb, w_down, h, name=f"ffn_act_down_{tag}")
    return out, (hn, gu, gc, a)


def _ffn_bwd(dh, h, g, saved, w_up, cw, w_down, tag):
    hn, gu, gc, a = saved
    da = _mm(dh, w_down, None, BF16, name=f"ffn_da_{tag}", transpose_w=True)
    dwd = _mm_tn(a, dh, name=f"ffn_dwd_{tag}")
    dgu, dc, dhin, dg8 = _ffn_act_bwd(gu, gc, da, cw, w_up, h, g, dh, name=f"ffn_actb_dh_{tag}")
    dwu = _mm_tn(hn, dgu, name=f"ffn_dwu_{tag}", col_shards=N_CHIPS)
    dcs = dc.sum(axis=1)
    return dhin, dg8.sum(axis=0), dwu, dcs[0:3], dcs[3], dwd


REDUCE_GROUPS = {"g1": [("ffn_w_up", 1), ("ffn_w_down", 1), ("od_w_out", 0), ("od_w_in", 0)],
                 "g2": [("ffn_w_up", 0), ("ffn_w_down", 0)],
                 "g3": [("ev_w_out", 0), ("ev_w_in", 0)]}


def _local_step(x, tgt, p, start_reduce=None, continue_reduce=None):
    row = lambda v: v.reshape(1, -1)
    grads = {}

    lower = jax.nn.softmax(p["hg_lb_logits"], axis=0)
    lb0 = row(lower[0])
    ev_cw = _pad_rows(p["ev_conv_w"][0])
    ev_cb = row(p["ev_conv_b"][0])
    wa = _block_diag(p["ev_gate_a_w"][0]).astype(BF16)
    wx = _block_diag(p["ev_gate_x_w"][0]).astype(BF16)
    ba, bx, lam = row(p["ev_gate_a_b"][0]), row(p["ev_gate_x_b"][0]), row(p["ev_lru_lambda"][0])
    gn = row(p["ev_hg_norm"][0])
    tril = jnp.tril(jnp.ones((SGU_CHUNK, SGU_CHUNK), F32))
    wsc = (p["od_w_s"][0] * tril).astype(BF16)
    bsb = jnp.broadcast_to(p["od_b_s"][0][:, :, None], (SGU_G, SGU_CHUNK, SGU_CHUNK)).astype(F32)
    ffn_cw = [_pad_rows(p["ffn_conv_w"][l]) for l in range(2)]
    ffn_cb = [row(p["ffn_conv_b"][l]) for l in range(2)]
    ev_w_in, ev_w_out = p["ev_w_in"][0], p["ev_w_out"][0]
    nm = [row(p["norm_mix"][l]) for l in range(2)]
    nf = [row(p["norm_ffn"][l]) for l in range(2)]

    h0 = x
    hn0, z0 = _norm_mm(h0, nm[0], ev_w_in, jnp.zeros((1, ev_w_in.shape[1]), F32), name="ev_in")
    out_a, hseq = _lru_fwd(z0, ev_cw, ev_cb, wa, ba, wx, bx, lam)
    out_b, ss = _hgrn_fwd(z0, lb0, gn)
    mix0 = jnp.concatenate([out_a, out_b], axis=1)
    h1 = _mm(mix0, ev_w_out, h0, F32, name="ev_out")
    late = p["late"](h1) if "late" in p else p
    od_w_in, od_w_out = late["od_w_in"][0], late["od_w_out"][0]
    w_up = [(late["ffn_w_up"], l) for l in range(2)]
    w_down = [(late["ffn_w_down"], l) for l in range(2)]
    h2, ffn0 = _ffn_fwd(h1, nf[0], w_up[0], ffn_cw[0], ffn_cb[0], w_down[0], "l0")
    hn1, p1 = _norm_mm(h2, nm[1], od_w_in, row(p["od_b_in"][0]), name="od_in")
    s1 = _sgu_fwd(p1, row(p["od_ln_g"][0]), row(p["od_ln_b"][0]), wsc, bsb)
    h3 = _mm(s1, od_w_out, h2, F32, name="od_out")
    h4, ffn1 = _ffn_fwd(h3, nf[1], w_up[1], ffn_cw[1], ffn_cb[1], w_down[1], "l1")
    dh4, dgf8, sq8 = _final_loss(h4, row(p["norm_final"]), tgt)
    grads["norm_final"] = dgf8.sum(axis=0)

    big = {}
    dh3, dnf1, dwu1, dcw1, dcb1, dwd1 = _ffn_bwd(dh4, h3, nf[1], ffn1, w_up[1], ffn_cw[1], w_down[1], "l1")
    big["ffn_w_up", 1], big["ffn_w_down", 1] = _piece_major(dwu1), _piece_major(dwd1)
    ds1 = _mm(dh3, od_w_out, None, BF16, name="od_ds", transpose_w=True)
    big["od_w_out", 0] = _piece_major(_mm_tn(s1, dh3, name="od_dwo"))
    wsct = jnp.swapaxes(wsc, 1, 2)
    dp1, dws, dbs, dlg8, dlb8, dbin8 = _sgu_bwd(p1, ds1, row(p["od_ln_g"][0]), row(p["od_ln_b"][0]), wsc, wsct, bsb)
    grads["od_w_s"] = (dws * tril)[None]
    grads["od_b_s"] = dbs.sum(axis=-1)[None]
    grads["od_ln_g"] = dlg8.sum(axis=0)[None]
    grads["od_ln_b"] = dlb8.sum(axis=0)[None]
    grads["od_b_in"] = dbin8.sum(axis=0)[None]
    dh2, dnm1 = _mm_normbwd(dp1, od_w_in, h2, nm[1], dh3, name="od_dh")
    big["od_w_in", 0] = _piece_major(_mm_tn(hn1, dp1, name="od_dwi", col_shards=N_CHIPS))
    if start_reduce is not None:
        token = start_reduce("g1", [big[key] for key in REDUCE_GROUPS["g1"]], split=True)
        ffn_cw[0] = ffn_cw[0] + token[0:1, 0:1]

    dh1, dnf0, dwu0, dcw0, dcb0, dwd0 = _ffn_bwd(dh2, h1, nf[0], ffn0, w_up[0], ffn_cw[0], w_down[0], "l0")
    big["ffn_w_up", 0], big["ffn_w_down", 0] = _piece_major(dwu0), _piece_major(dwd0)
    if start_reduce is not None:
        token = continue_reduce("g1", dh1) + start_reduce("g2", [big[key] for key in REDUCE_GROUPS["g2"]], split=True)
        ev_cb = ev_cb + token[0:1, 0:1]
    dmix = _mm(dh1, ev_w_out, None, BF16, name="ev_dmix", transpose_w=True)
    big["ev_w_out", 0] = _piece_major(_mm_tn(mix0, dh1, name="ev_dwo"))
    dz01, dc5, dwa, dwx, dvec = _lru_bwd(z0, hseq, dmix, ev_cw, ev_cb, wa, wa.T, ba, wx, wx.T, bx, lam)
    if start_reduce is not None:
        lb0 = lb0 + continue_reduce("g2", dc5)[0:1, 0:1]
    dz0, dlb8, dgn8 = _hgrn_bwd(z0, ss, dmix, lb0, gn, dz01)
    big["ev_w_in", 0] = _piece_major(_mm_tn(hn0, dz0, name="ev_dwi", col_shards=N_CHIPS))
    if start_reduce is not None:
        token = start_reduce("g3", [big[key] for key in REDUCE_GROUPS["g3"]])
        nm[0] = nm[0] + token[0:1, 0:1]
    grad_x, dnm0 = _mm_normbwd(dz0, ev_w_in, h0, nm[0], dh1, name="ev_dh")

    dc5s = dc5.sum(axis=1)
    grads["ev_conv_w"] = dc5s[0:4][None]
    grads["ev_conv_b"] = dc5s[4][None]
    grads["ev_gate_a_w"] = _diag_blocks(dwa, LRU_BLOCKS)[None]
    grads["ev_gate_x_w"] = _diag_blocks(dwx, LRU_BLOCKS)[None]
    dvs = dvec.sum(axis=1)
    grads["ev_gate_a_b"] = dvs[0][None]
    grads["ev_gate_x_b"] = dvs[1][None]
    grads["ev_lru_lambda"] = (dvs[2] * (-jax.nn.sigmoid(-p["ev_lru_lambda"][0])))[None]
    dlb = dlb8.sum(axis=0)
    grads["hg_lb_logits"] = dlb[None, :] * lower[0][None, :] * (jnp.eye(3, dtype=F32)[0][:, None] - lower)
    grads["ev_hg_norm"] = dgn8.sum(axis=0)[None]
    grads["norm_mix"] = jnp.stack([dnm0.sum(axis=0), dnm1.sum(axis=0)])
    grads["norm_ffn"] = jnp.stack([dnf0, dnf1])
    grads["ffn_conv_w"] = jnp.stack([dcw0, dcw1])
    grads["ffn_conv_b"] = jnp.stack([dcb0, dcb1])
    return sq8, grad_x, grads, big


MESH = pl.DeviceIdType.MESH
ANY = pl.BlockSpec(memory_space=pl.ANY)
N_CHIPS = 4
N_DEV = 8

SH_BIG = {"ev_w_in": 2, "ev_w_out": 1, "od_w_in": 2, "od_w_out": 1, "ffn_w_up": 2, "ffn_w_down": 1}
SH_SMALL = {"ev_conv_w": 2, "od_b_in": 1, "od_ln_g": 1, "od_ln_b": 1, "ffn_conv_w": 2}
REP = ["norm_mix", "norm_ffn", "norm_final", "ev_conv_b", "ev_gate_a_w", "ev_gate_a_b", "ev_gate_x_w", "ev_gate_x_b",
       "ev_lru_lambda", "hg_lb_logits", "ev_hg_norm", "od_w_s", "od_b_s", "ffn_conv_b"]
WEIGHTS = ["norm_mix", "norm_ffn", "norm_final", "ev_w_in", "ev_conv_w", "ev_conv_b", "ev_gate_a_w", "ev_gate_a_b", "ev_gate_x_w",
           "ev_gate_x_b", "ev_lru_lambda", "hg_lb_logits", "ev_hg_norm", "ev_w_out", "od_w_in", "od_b_in", "od_ln_g", "od_ln_b",
           "od_w_s", "od_b_s", "od_w_out", "ffn_w_up", "ffn_conv_w", "ffn_conv_b", "ffn_w_down"]


def _rows(n_elems, mult=SUBLANES):
    r = -(-n_elems // LANES)
    return -(-r // mult) * mult


def _pack(arrs, rows, dtype):
    flat = jnp.concatenate([a.reshape(-1).astype(dtype) for a in arrs])
    return jnp.pad(flat, (0, rows * LANES - flat.shape[0])).reshape(rows, LANES)


def _unpack(flat2d, shapes):
    flat = flat2d.reshape(-1)
    out, off = [], 0
    for s in shapes:
        n = 1
        for d in s:
            n *= d
        out.append(flat[off:off + n].reshape(s))
        off += n
    return out


def _mesh_pos():
    return lax.axis_index("x"), lax.axis_index("y"), lax.axis_index("c")


def _other_chips(x, y):
    return [(1 - x, y), (x, 1 - y), (1 - x, 1 - y)]


def _half_rows(n):
    return lambda r, c: r.at[0, pl.ds(c * (n // 2), n // 2), :]


GATHER_BIG = {
    "ev_w_in": ((1024, 3072), _half_rows(1024), lambda o, k, c: o.at[pl.ds(c * 512, 512), pl.ds(k * 768, 768)]),
    "ev_w_out": ((1024, 1024), _half_rows(256), lambda o, k, c: o.at[pl.ds(k * 256 + c * 128, 128), :]),
    "od_w_in": ((1024, 2048), _half_rows(1024), lambda o, k, c: o.at[pl.ds(c * 512, 512), pl.ds(k * 512, 512)]),
    "od_w_out": ((1024, 1024), _half_rows(256), lambda o, k, c: o.at[pl.ds(k * 256 + c * 128, 128), :]),
    "ffn_w_up": ((2, 1024, 2 * D_FF), lambda r, c: r.at[c], lambda o, k, c: o.at[c, :, pl.ds(k * (2 * D_FF // 4), 2 * D_FF // 4)]),
    "ffn_w_down": ((2, D_FF, 1024), lambda r, c: r.at[c], lambda o, k, c: o.at[c, pl.ds(k * (D_FF // 4), D_FF // 4), :]),
}


def _gather_weights(names, big, small):
    nb = len(big)
    descs = [GATHER_BIG[n] for n in names]
    rs = small.shape[0]

    def body(*refs):
        ins, s_ref = refs[:nb], refs[nb]
        outs, os_ref = refs[nb + 1:2 * nb + 1], refs[2 * nb + 1]
        ici_send, ici_recv, d2d_send, d2d_recv, loc_sems = refs[2 * nb + 2:2 * nb + 7]
        vbufs = refs[2 * nb + 7:]
        x, y, c = _mesh_pos()
        k = 2 * x + y
        chips = _other_chips(x, y)
        sib = (x, y, 1 - c)

        def remote(src, dst, ssem, rsem, to):
            return pltpu.make_async_remote_copy(src_ref=src, dst_ref=dst, send_sem=ssem, recv_sem=rsem, device_id=to,
                                                device_id_type=MESH)

        stage = [pltpu.make_async_copy(ins[t], vbufs[t], loc_sems.at[2 * t]) for t in range(nb)]
        stage.append(pltpu.make_async_copy(s_ref, vbufs[nb], loc_sems.at[2 * nb]))
        for cp in stage:
            cp.start()
        sends = []
        for t, (_, src, dst) in enumerate(descs):
            for j, (px, py) in enumerate(chips):
                sends.append(remote(src(ins[t], c), dst(outs[t], k, c), ici_send.at[3 * t + j], ici_recv.at[3 * t + j], (px, py, c)))
        for j, (px, py) in enumerate(chips):
            sends.append(remote(s_ref, os_ref.at[k], ici_send.at[3 * nb + j], ici_recv.at[3 * nb + j], (px, py, c)))
        for cp in sends:
            cp.start()
        for cp in stage:
            cp.wait()
        local = []
        for t, (_, src, dst) in enumerate(descs):
            for cc in (0, 1):
                local.append(pltpu.make_async_copy(src(vbufs[t], cc), dst(outs[t], k, cc), loc_sems.at[2 * t + cc]))
        local.append(pltpu.make_async_copy(vbufs[nb], os_ref.at[k], loc_sems.at[2 * nb]))
        for cp in local:
            cp.start()
        for t, (_, src, dst) in enumerate(descs):
            for j, (px, py) in enumerate(chips):
                got = dst(outs[t], 2 * px + py, c)
                remote(got, got, ici_send.at[3 * t + j], ici_recv.at[3 * t + j], (px, py, c)).wait_recv()
                fwd = remote(got, got, d2d_send.at[3 * t + j], d2d_recv.at[3 * t + j], sib)
                fwd.start()
                sends.append(fwd)
        for j, (px, py) in enumerate(chips):
            remote(s_ref, os_ref.at[2 * px + py], ici_send.at[3 * nb + j], ici_recv.at[3 * nb + j], (px, py, c)).wait_recv()
        for t, (_, src, dst) in enumerate(descs):
            for j, (px, py) in enumerate(chips):
                theirs = dst(outs[t], 2 * px + py, 1 - c)
                remote(theirs, theirs, d2d_send.at[3 * t + j], d2d_recv.at[3 * t + j], sib).wait_recv()
        for cp in sends:
            cp.wait_send()
        for cp in local:
            cp.wait()

    out_shape = [jax.ShapeDtypeStruct(d[0], BF16) for d in descs] + [jax.ShapeDtypeStruct((N_CHIPS, rs, LANES), small.dtype)]
    return pl.pallas_call(
        body, name="gather_weights", in_specs=[ANY] * (nb + 1), out_specs=[ANY] * (nb + 1), out_shape=out_shape,
        scratch_shapes=[pltpu.SemaphoreType.DMA((3 * nb + 3,)), pltpu.SemaphoreType.DMA((3 * nb + 3,)),
                        pltpu.SemaphoreType.DMA((3 * nb,)), pltpu.SemaphoreType.DMA((3 * nb,)),
                        pltpu.SemaphoreType.DMA((2 * nb + 1,))]
        + [pltpu.VMEM(b.shape, b.dtype) for b in big] + [pltpu.VMEM(small.shape, small.dtype)],
        compiler_params=pltpu.CompilerParams(vmem_limit_bytes=VMEM_LIMIT),
    )(*big, small)


def _place_own(names, big):
    nb = len(big)
    descs = [GATHER_BIG[n] for n in names]

    def body(*refs):
        ins, outs = refs[:nb], refs[nb:2 * nb]
        sems, vbufs = refs[2 * nb], refs[2 * nb + 1:]
        x, y, c = _mesh_pos()
        k = 2 * x + y
        stage = [pltpu.make_async_copy(ins[t], vbufs[t], sems.at[2 * t]) for t in range(nb)]
        for cp in stage:
            cp.start()
        for cp in stage:
            cp.wait()
        local = [pltpu.make_async_copy(src(vbufs[t], cc), dst(outs[t], k, cc), sems.at[2 * t + cc])
                 for t, (_, src, dst) in enumerate(descs) for cc in (0, 1)]
        for cp in local:
            cp.start()
        for cp in local:
            cp.wait()

    return pl.pallas_call(
        body, name="place_own", in_specs=[ANY] * nb, out_specs=[ANY] * nb,
        out_shape=[jax.ShapeDtypeStruct(d[0], BF16) for d in descs],
        scratch_shapes=[pltpu.SemaphoreType.DMA((2 * nb,))] + [pltpu.VMEM(b.shape, b.dtype) for b in big],
        compiler_params=pltpu.CompilerParams(vmem_limit_bytes=VMEM_LIMIT),
    )(*big)


def _gather_start(names, big, bufs):
    nb = len(big)
    descs = [GATHER_BIG[n] for n in names]

    def body(*refs):
        ins, lnd = refs[:nb], refs[nb:2 * nb]
        send_sems, recv_sems, token = refs[2 * nb], refs[2 * nb + 1], refs[-1]
        x, y, c = _mesh_pos()
        k = 2 * x + y
        for t, (_, src, dst) in enumerate(descs):
            for j, (px, py) in enumerate(_other_chips(x, y)):
                _remote(src(ins[t], c), dst(lnd[t], k, c), send_sems.at[3 * t + j], recv_sems.at[3 * t + j], (px, py, c)).start()
        token[...] = jnp.zeros_like(token)

    out = pl.pallas_call(
        body, name="gather_start",
        out_shape=(pltpu.SemaphoreType.DMA((3 * nb,)), pltpu.SemaphoreType.DMA((3 * nb,)),
                   *[pltpu.HBM(b.shape, b.dtype) for b in big], *[pltpu.HBM(b.shape, b.dtype) for b in bufs],
                   jax.ShapeDtypeStruct((SUBLANES, LANES), F32)),
        in_specs=[HBM] * (2 * nb), out_specs=(SEM, SEM, *[HBM] * (2 * nb), pl.BlockSpec(memory_space=pltpu.VMEM)),
        input_output_aliases={i: 2 + i for i in range(2 * nb)},
        compiler_params=pltpu.CompilerParams(has_side_effects=DATAFLOW),
    )(*[pltpu.with_memory_space_constraint(b, pltpu.HBM) for b in big], *[pltpu.with_memory_space_constraint(b, pltpu.HBM) for b in bufs])
    return out[0], out[1], list(out[2:2 + nb]), list(out[2 + nb:2 + 2 * nb]), out[-1]


def _gather_wait(names, send_sems, recv_sems, big, bufs, after):
    nb = len(big)
    descs = [GATHER_BIG[n] for n in names]

    def body(*refs):
        ins, lnd = refs[:nb], refs[nb:2 * nb]
        ssem, rsem = refs[2 * nb], refs[2 * nb + 1]
        x, y, c = _mesh_pos()
        for t, (_, src, dst) in enumerate(descs):
            for j, (px, py) in enumerate(_other_chips(x, y)):
                cp = _remote(src(ins[t], c), dst(lnd[t], 2 * px + py, c), ssem.at[3 * t + j], rsem.at[3 * t + j], (px, py, c))
                cp.wait_send()
                cp.wait_recv()

    out = pl.pallas_call(
        body, name="gather_wait",
        out_shape=(*[pltpu.HBM(b.shape, b.dtype) for b in big], *[pltpu.HBM(b.shape, b.dtype) for b in bufs]),
        in_specs=[HBM] * (2 * nb) + [SEM, SEM, ANY], out_specs=tuple([HBM] * (2 * nb)),
        input_output_aliases={i: i for i in range(2 * nb)},
        compiler_params=pltpu.CompilerParams(has_side_effects=DATAFLOW),
    )(*big, *bufs, send_sems, recv_sems, after)
    return list(out[nb:])


def _gather_forward(names, bufs):
    nb = len(bufs)
    descs = [GATHER_BIG[n] for n in names]

    def body(*refs):
        outs = refs[nb:2 * nb]
        send_sems, recv_sems = refs[2 * nb:]
        x, y, c = _mesh_pos()
        sib = (x, y, 1 - c)
        sends = []
        for t, (_, src, dst) in enumerate(descs):
            for j, (px, py) in enumerate(_other_chips(x, y)):
                got = dst(outs[t], 2 * px + py, c)
                sends.append(_remote(got, got, send_sems.at[3 * t + j], recv_sems.at[3 * t + j], sib))
        for cp in sends:
            cp.start()
        for t, (_, src, dst) in enumerate(descs):
            for j, (px, py) in enumerate(_other_chips(x, y)):
                theirs = dst(outs[t], 2 * px + py, 1 - c)
                _remote(theirs, theirs, send_sems.at[3 * t + j], recv_sems.at[3 * t + j], sib).wait_recv()
        for cp in sends:
            cp.wait_send()

    return pl.pallas_call(
        body, name="gather_forward", in_specs=[ANY] * nb, out_specs=[ANY] * nb,
        out_shape=[jax.ShapeDtypeStruct(b.shape, b.dtype) for b in bufs], input_output_aliases={t: t for t in range(nb)},
        scratch_shapes=[pltpu.SemaphoreType.DMA((3 * nb,)), pltpu.SemaphoreType.DMA((3 * nb,))],
    )(*bufs)


def _remote(src, dst, ssem, rsem, to):
    return pltpu.make_async_remote_copy(src_ref=src, dst_ref=dst, send_sem=ssem, recv_sem=rsem, device_id=to, device_id_type=MESH)


def _rs_send_sibling(gs, tag):
    n = len(gs)
    counts = [N_CHIPS if g.ndim == 4 else 1 for g in gs]
    ns = sum(counts)

    def body(*refs):
        ins, outs = refs[:n], refs[n:2 * n]
        send_sems, recv_sems = refs[2 * n:]
        x, y, c = _mesh_pos()
        cps, s = [], 0
        for t in range(n):
            if counts[t] == 1:
                cps.append(_remote(ins[t].at[1 - c], outs[t], send_sems.at[s], recv_sems.at[s], (x, y, 1 - c)))
                s += 1
            else:
                for k in range(N_CHIPS):
                    cps.append(_remote(ins[t].at[k, 1 - c], outs[t].at[k], send_sems.at[s], recv_sems.at[s], (x, y, 1 - c)))
                    s += 1
        for cp in cps:
            cp.start()
        for cp in cps:
            cp.wait()

    out_shape = [jax.ShapeDtypeStruct(g.shape[:1] + g.shape[2:] if g.ndim == 4 else g.shape[1:], g.dtype) for g in gs]
    return pl.pallas_call(
        body, name=f"rs_send_sibling_{tag}", in_specs=[ANY] * n, out_specs=[ANY] * n, out_shape=out_shape,
        scratch_shapes=[pltpu.SemaphoreType.DMA((ns,)), pltpu.SemaphoreType.DMA((ns,))],
    )(*gs)


def _add_piece(g, recv, c, name):
    P, Q = g.shape[-2:]

    def body(c_ref, g_ref, r_ref, o_ref):
        o_ref[...] = g_ref[...].reshape(o_ref.shape) + r_ref[...]

    if g.ndim == 4:
        grid = (N_CHIPS,)
        in_specs = [pl.BlockSpec((1, 1, P, Q), lambda k, c_ref: (k, c_ref[0], 0, 0)), pl.BlockSpec((1, P, Q), lambda k, c_ref: (k, 0, 0))]
        out_spec = pl.BlockSpec((1, P, Q), lambda k, c_ref: (k, 0, 0))
    else:
        grid = (1,)
        in_specs = [pl.BlockSpec((1, P, Q), lambda k, c_ref: (c_ref[0], 0, 0)), pl.BlockSpec((P, Q), lambda k, c_ref: (0, 0))]
        out_spec = pl.BlockSpec((P, Q), lambda k, c_ref: (0, 0))
    return pl.pallas_call(
        body, name=name,
        grid_spec=pltpu.PrefetchScalarGridSpec(num_scalar_prefetch=1, grid=grid, in_specs=in_specs, out_specs=out_spec),
        out_shape=jax.ShapeDtypeStruct(recv.shape, g.dtype),
        compiler_params=_cp(1),
    )(c, g, recv)


HBM = pl.BlockSpec(memory_space=pltpu.HBM)
SEM = pl.BlockSpec(memory_space=pltpu.SEMAPHORE)
DATAFLOW = pltpu.SideEffectType.DATAFLOW_SIDE_EFFECTING


def _sibling_copies(gs, srcs, dsts, send_sems, recv_sems):
    x, y, c = _mesh_pos()
    cps, s = [], 0
    for t, g in enumerate(gs):
        if g.ndim == 4:
            for k in range(N_CHIPS):
                cps.append(_remote(srcs[t].at[k, 1 - c], dsts[t].at[k], send_sems.at[s], recv_sems.at[s], (x, y, 1 - c)))
                s += 1
        else:
            cps.append(_remote(srcs[t].at[1 - c], dsts[t], send_sems.at[s], recv_sems.at[s], (x, y, 1 - c)))
            s += 1
    return cps


def _sibling_start(gs, tag):
    n = len(gs)
    ns = sum(N_CHIPS if g.ndim == 4 else 1 for g in gs)
    lands = [pltpu.with_memory_space_constraint(lax.empty(g.shape[:1] + g.shape[2:] if g.ndim == 4 else g.shape[1:], g.dtype), pltpu.HBM)
             for g in gs]

    def body(*refs):
        for cp in _sibling_copies(gs, refs[:n], refs[n:2 * n], refs[2 * n], refs[2 * n + 1]):
            cp.start()
        refs[-1][...] = jnp.zeros_like(refs[-1])

    out = pl.pallas_call(
        body, name=f"sibling_start_{tag}",
        out_shape=(pltpu.SemaphoreType.DMA((ns,)), pltpu.SemaphoreType.DMA((ns,)),
                   *[pltpu.HBM(g.shape, g.dtype) for g in gs], *[pltpu.HBM(l.shape, l.dtype) for l in lands],
                   jax.ShapeDtypeStruct((SUBLANES, LANES), F32)),
        in_specs=[HBM] * (2 * n), out_specs=(SEM, SEM, *[HBM] * (2 * n), pl.BlockSpec(memory_space=pltpu.VMEM)),
        input_output_aliases={i: 2 + i for i in range(2 * n)},
        compiler_params=pltpu.CompilerParams(has_side_effects=DATAFLOW),
    )(*[pltpu.with_memory_space_constraint(g, pltpu.HBM) for g in gs], *lands)
    return (out[0], out[1], list(out[2:2 + n]), list(out[2 + n:2 + 2 * n])), out[-1]


def _sibling_wait(send_sems, recv_sems, gs, lands, after, tag):
    n = len(gs)

    def body(*refs):
        for cp in _sibling_copies(gs, refs[:n], refs[n:2 * n], refs[2 * n], refs[2 * n + 1]):
            cp.wait_send()
            cp.wait_recv()

    out = pl.pallas_call(
        body, name=f"sibling_wait_{tag}",
        out_shape=(*[pltpu.HBM(g.shape, g.dtype) for g in gs], *[pltpu.HBM(l.shape, l.dtype) for l in lands]),
        in_specs=[HBM] * (2 * n) + [SEM, SEM, ANY], out_specs=tuple([HBM] * (2 * n)),
        input_output_aliases={i: i for i in range(2 * n)},
        compiler_params=pltpu.CompilerParams(has_side_effects=DATAFLOW),
    )(*gs, *lands, send_sems, recv_sems, after)
    return list(out[:n]), list(out[n:])


def _chips_start(hs, tag):
    n = len(hs)
    lands = [pltpu.with_memory_space_constraint(lax.empty((N_CHIPS,) + h.shape[-2:], h.dtype), pltpu.HBM) for h in hs]

    def body(*refs):
        ins, lnd = refs[:n], refs[n:2 * n]
        send_sems, recv_sems, token = refs[2 * n], refs[2 * n + 1], refs[-1]
        x, y, c = _mesh_pos()
        k = 2 * x + y
        piece = lambda t, kk: ins[t].at[kk] if hs[t].ndim == 3 else ins[t]
        for t in range(n):
            for j, (px, py) in enumerate(_other_chips(x, y)):
                _remote(piece(t, 2 * px + py), lnd[t].at[k], send_sems.at[3 * t + j], recv_sems.at[3 * t + j], (px, py, c)).start()
        token[...] = jnp.zeros_like(token)

    out = pl.pallas_call(
        body, name=f"chips_start_{tag}",
        out_shape=(pltpu.SemaphoreType.DMA((3 * n,)), pltpu.SemaphoreType.DMA((3 * n,)),
                   *[pltpu.HBM(h.shape, h.dtype) for h in hs], *[pltpu.HBM(l.shape, l.dtype) for l in lands],
                   jax.ShapeDtypeStruct((SUBLANES, LANES), F32)),
        in_specs=[HBM] * (2 * n), out_specs=(SEM, SEM, *[HBM] * (2 * n), pl.BlockSpec(memory_space=pltpu.VMEM)),
        input_output_aliases={i: 2 + i for i in range(2 * n)},
        compiler_params=pltpu.CompilerParams(has_side_effects=DATAFLOW),
    )(*[pltpu.with_memory_space_constraint(h, pltpu.HBM) for h in hs], *lands)
    return out[0], out[1], list(out[2:2 + n]), list(out[2 + n:2 + 2 * n]), out[-1]


def _chips_wait(send_sems, recv_sems, hs, lands, after, tag):
    n = len(hs)

    def body(*refs):
        ins, lnd = refs[:n], refs[n:2 * n]
        ssem, rsem = refs[2 * n], refs[2 * n + 1]
        x, y, c = _mesh_pos()
        k = 2 * x + y
        piece = lambda t, kk: ins[t].at[kk] if hs[t].ndim == 3 else ins[t]
        for t in range(n):
            for j, (px, py) in enumerate(_other_chips(x, y)):
                cp = _remote(piece(t, k), lnd[t].at[2 * px + py], ssem.at[3 * t + j], rsem.at[3 * t + j], (px, py, c))
                cp.wait_send()
                cp.wait_recv()

    out = pl.pallas_call(
        body, name=f"chips_wait_{tag}",
        out_shape=(*[pltpu.HBM(h.shape, h.dtype) for h in hs], *[pltpu.HBM(l.shape, l.dtype) for l in lands]),
        in_specs=[HBM] * (2 * n) + [SEM, SEM, ANY], out_specs=tuple([HBM] * (2 * n)),
        input_output_aliases={i: i for i in range(2 * n)},
        compiler_params=pltpu.CompilerParams(has_side_effects=DATAFLOW),
    )(*hs, *lands, send_sems, recv_sems, after)
    return list(out[:n]), list(out[n:])


def _add_chips(p, own, kc, name):
    _, P, Q = p.shape
    tr = P
    while N_CHIPS * tr * Q * 4 > 6 * 1024 * 1024 and tr % 16 == 0:
        tr //= 2
    sharded = own.ndim == 3

    def body(kc_ref, p_ref, own_ref, o_ref):
        k = kc_ref[0]
        mine = own_ref[...].reshape(tr, Q)
        v = [jnp.where(k == j, mine, p_ref[j]) for j in range(N_CHIPS)]
        o_ref[0] = ((v[0] + v[1]) + v[2]) + v[3]

    own_spec = (pl.BlockSpec((1, tr, Q), lambda i, kc_ref: (kc_ref[0], i, 0)) if sharded
                else pl.BlockSpec((tr, Q), lambda i, kc_ref: (i, 0)))
    return pl.pallas_call(
        body, name=name,
        grid_spec=pltpu.PrefetchScalarGridSpec(
            num_scalar_prefetch=1, grid=(P // tr,),
            in_specs=[pl.BlockSpec((N_CHIPS, tr, Q), lambda i, kc_ref: (0, i, 0)), own_spec],
            out_specs=pl.BlockSpec((1, tr, Q), lambda i, kc_ref: (kc_ref[1], i, 0))),
        out_shape=jax.ShapeDtypeStruct((2, P, Q), p.dtype),
        compiler_params=_cp(1),
    )(kc, p, own)


def _rs_share(fs, tag):
    n = len(fs)

    def body(*refs):
        outs = refs[n:2 * n]
        send_sems, recv_sems = refs[2 * n:]
        x, y, c = _mesh_pos()
        sends = [_remote(outs[t].at[c], outs[t].at[c], send_sems.at[t], recv_sems.at[t], (x, y, 1 - c)) for t in range(n)]
        for cp in sends:
            cp.start()
        for t in range(n):
            _remote(outs[t].at[c], outs[t].at[1 - c], send_sems.at[t], recv_sems.at[t], (x, y, 1 - c)).wait_recv()
        for cp in sends:
            cp.wait_send()

    return pl.pallas_call(
        body, name=f"rs_share_{tag}", in_specs=[ANY] * n, out_specs=[ANY] * n,
        out_shape=[jax.ShapeDtypeStruct(f.shape, f.dtype) for f in fs], input_output_aliases={t: t for t in range(n)},
        scratch_shapes=[pltpu.SemaphoreType.DMA((n,)), pltpu.SemaphoreType.DMA((n,))],
    )(*fs)


def _reduce_start(gs, kc, tag):
    return _reduce_continue(gs, _rs_send_sibling(gs, tag), kc, tag)


def _reduce_continue(gs, from_sibling, kc, tag):
    chip_sums = [_add_piece(g, r, kc[1:], name=f"add_piece_{tag}_{t}") for t, (g, r) in enumerate(zip(gs, from_sibling))]
    send_sems, recv_sems, chip_sums, lands, token = _chips_start(chip_sums, tag)
    return (send_sems, recv_sems, chip_sums, lands, tag), token


def _reduce_finish(states, kc, after):
    mine = []
    for send_sems, recv_sems, chip_sums, lands, tag in states:
        chip_sums, from_chips = _chips_wait(send_sems, recv_sems, chip_sums, lands, after, tag)
        mine += [_add_chips(p, h, kc, name=f"add_chips_{tag}_{t}") for t, (p, h) in enumerate(zip(from_chips, chip_sums))]
    return _rs_share(mine, "all")


def _adamw(w, g, m, v, name):
    R, C = w.shape
    tr = R
    for cand in (512, 256, 128, 64, 32, 16, 8):
        if R % cand == 0 and cand * C * 4 <= 2 * 1024 * 1024:
            tr = cand
            break
    c1 = 1.0 / (1.0 - ADAM_B1 ** ADAM_STEP)
    c2 = 1.0 / (1.0 - ADAM_B2 ** ADAM_STEP)

    def body(w_ref, g_ref, m_ref, v_ref, d_ref, mo_ref, vo_ref):
        gv = g_ref[...]
        mn = ADAM_B1 * m_ref[...] + (1.0 - ADAM_B1) * gv
        vn = ADAM_B2 * v_ref[...] + (1.0 - ADAM_B2) * (gv * gv)
        mo_ref[...] = mn
        vo_ref[...] = vn
        d_ref[...] = -ADAM_LR * ((mn * c1) / (jnp.sqrt(vn * c2) + ADAM_EPS) + ADAM_WD * w_ref[...])

    spec = pl.BlockSpec((tr, C), lambda i: (i, 0))
    shp = jax.ShapeDtypeStruct((R, C), F32)
    return pl.pallas_call(body, name=name, grid=(R // tr,), in_specs=[spec] * 4, out_specs=[spec] * 3, out_shape=[shp] * 3,
                          compiler_params=_cp(1))(w, g, m, v)


def _adamw_many(ws, gs, ms, vs):
    n = len(ws)
    c1 = 1.0 / (1.0 - ADAM_B1 ** ADAM_STEP)
    c2 = 1.0 / (1.0 - ADAM_B2 ** ADAM_STEP)

    def body(*refs):
        w_refs, g_refs, m_refs, v_refs = (refs[k * n:(k + 1) * n] for k in range(4))
        outs = refs[4 * n:]
        for i in range(n):
            gv = g_refs[i][...]
            mn = ADAM_B1 * m_refs[i][...] + (1.0 - ADAM_B1) * gv
            vn = ADAM_B2 * v_refs[i][...] + (1.0 - ADAM_B2) * (gv * gv)
            outs[3 * i][...] = -ADAM_LR * ((mn * c1) / (jnp.sqrt(vn * c2) + ADAM_EPS) + ADAM_WD * w_refs[i][...])
            outs[3 * i + 1][...] = mn
            outs[3 * i + 2][...] = vn

    out_shape = [jax.ShapeDtypeStruct(w.shape, F32) for w in ws for _ in range(3)]
    return pl.pallas_call(body, name="adamw_small", out_shape=out_shape)(*ws, *gs, *ms, *vs)


def _step(a):
    x, y, c = _mesh_pos()
    kc = jnp.stack([2 * x + y, c]).astype(jnp.int32)

    rs = _rows(sum(a[n].size for n in SH_SMALL))
    first, later = ["ev_w_in", "ev_w_out"], ["od_w_in", "od_w_out", "ffn_w_up", "ffn_w_down"]
    lead = lambda w: w if w.ndim == 3 else w[None]
    *full, gs = _gather_weights(first, [a[n].astype(BF16) for n in first], _pack([a[n] for n in SH_SMALL], rs, F32))
    p = {n: a[n] for n in REP}
    p.update({n: lead(w) for n, w in zip(first, full)})
    parts = [_unpack(gs[k], [a[n].shape for n in SH_SMALL]) for k in range(N_CHIPS)]
    for i, n in enumerate(SH_SMALL):
        p[n] = jnp.concatenate([parts[k][i] for k in range(N_CHIPS)], axis=SH_SMALL[n])
    shards = lax.optimization_barrier(([a[n].astype(BF16) for n in later], full))[0]
    g_send, g_recv, shards, bufs, token = _gather_start(later, shards, _place_own(later, shards))
    p["norm_mix"] = p["norm_mix"] + token[0:1, 0:1]

    def late(after):
        got = _gather_forward(later, _gather_wait(later, g_send, g_recv, shards, bufs, after))
        return {n: lead(w) for n, w in zip(later, got)}

    p["late"] = late

    states, pending = {}, {}

    def start_reduce(tag, gs, split=False):
        if split:
            pending[tag], token = _sibling_start(gs, tag)
        else:
            states[tag], token = _reduce_start(gs, kc, tag)
        return token

    def continue_reduce(tag, after):
        send_sems, recv_sems, gs, lands = pending.pop(tag)
        gs, from_sibling = _sibling_wait(send_sems, recv_sems, gs, lands, after, tag)
        states[tag], token = _reduce_continue(gs, from_sibling, kc, tag)
        return token

    sq8, grad_x, grads, big = _local_step(a["x"][0], a["loss_target"][0], p, start_reduce, continue_reduce)
    loss = lax.psum(0.5 / D_MODEL * jnp.sum(sq8), ("x", "y", "c"))

    r_s = _rows(sum(a[n].size for n in SH_SMALL), 2 * SUBLANES) // 2
    small_pieces = []
    for k in range(N_CHIPS):
        pieces = [lax.slice_in_dim(grads[n], k * a[n].shape[ax], (k + 1) * a[n].shape[ax], axis=ax) for n, ax in SH_SMALL.items()]
        small_pieces.append(_pack(pieces, 2 * r_s, F32).reshape(2, r_s, LANES))
    g_small = jnp.stack(small_pieces)
    r_r = _rows(sum(a[n].size for n in REP), 2 * SUBLANES) // 2
    g_rep = _pack([grads[n] for n in REP], 2 * r_r, F32).reshape(2, r_r, LANES)
    start_reduce("g4", [g_small, g_rep])
    reduced = _reduce_finish([states[tag] for tag in ("g1", "g2", "g3", "g4")], kc, grad_x)
    red = dict(zip([key for tag in ("g1", "g2", "g3") for key in REDUCE_GROUPS[tag]], reduced))
    gfin = {}
    for n in ("ev_w_in", "ev_w_out", "od_w_in", "od_w_out"):
        gfin[n] = red[n, 0].reshape(a[n].shape)
    for n in ("ffn_w_up", "ffn_w_down"):
        gfin[n] = jnp.stack([red[n, l].reshape(a[n].shape[1:]) for l in range(2)])
    gfin.update(zip(SH_SMALL, _unpack(reduced[-2], [a[n].shape for n in SH_SMALL])))
    gfin.update(zip(REP, _unpack(reduced[-1], [a[n].shape for n in REP])))

    out = {"loss": loss, "grad_x": grad_x[None]}
    small_names = list(SH_SMALL) + REP
    for n in SH_BIG:
        shp = a[n].shape
        two_d = lambda t: t.reshape(-1, shp[-1])
        d, mo, vo = _adamw(two_d(a[n]), two_d(gfin[n]), two_d(a["m_" + n]), two_d(a["v_" + n]), name=f"adamw_{n}")
        out["delta_" + n], out["new_m_" + n], out["new_v_" + n] = d.reshape(shp), mo.reshape(shp), vo.reshape(shp)
    two_d = lambda t: t.reshape(-1, t.shape[-1])
    res = _adamw_many(*[[two_d(src(n)) for n in small_names]
                        for src in (lambda n: a[n], lambda n: gfin[n], lambda n: a["m_" + n], lambda n: a["v_" + n])])
    for i, n in enumerate(small_names):
        out["delta_" + n], out["new_m_" + n], out["new_v_" + n] = (r.reshape(a[n].shape) for r in res[3 * i:3 * i + 3])
    for n in WEIGHTS:
        out["grad_" + n] = gfin[n]
    return out


def kernel(x, norm_mix, norm_ffn, norm_final, ev_w_in, ev_conv_w, ev_conv_b, ev_gate_a_w, ev_gate_a_b, ev_gate_x_w, ev_gate_x_b, ev_lru_lambda, hg_lb_logits, ev_hg_norm, ev_w_out, od_w_in, od_b_in, od_ln_g, od_ln_b, od_w_s, od_b_s, od_w_out, ffn_w_up, ffn_conv_w, ffn_conv_b, ffn_w_down, loss_target, m_norm_mix, m_norm_ffn, m_norm_final, m_ev_w_in, m_ev_conv_w, m_ev_conv_b, m_ev_gate_a_w, m_ev_gate_a_b, m_ev_gate_x_w, m_ev_gate_x_b, m_ev_lru_lambda, m_hg_lb_logits, m_ev_hg_norm, m_ev_w_out, m_od_w_in, m_od_b_in, m_od_ln_g, m_od_ln_b, m_od_w_s, m_od_b_s, m_od_w_out, m_ffn_w_up, m_ffn_conv_w, m_ffn_conv_b, m_ffn_w_down, v_norm_mix, v_norm_ffn, v_norm_final, v_ev_w_in, v_ev_conv_w, v_ev_conv_b, v_ev_gate_a_w, v_ev_gate_a_b, v_ev_gate_x_w, v_ev_gate_x_b, v_ev_lru_lambda, v_hg_lb_logits, v_ev_hg_norm, v_ev_w_out, v_od_w_in, v_od_b_in, v_od_ln_g, v_od_ln_b, v_od_w_s, v_od_b_s, v_od_w_out, v_ffn_w_up, v_ffn_conv_w, v_ffn_conv_b, v_ffn_w_down):
    vals = (x, norm_mix, norm_ffn, norm_final, ev_w_in, ev_conv_w, ev_conv_b, ev_gate_a_w, ev_gate_a_b, ev_gate_x_w, ev_gate_x_b, ev_lru_lambda, hg_lb_logits, ev_hg_norm, ev_w_out, od_w_in, od_b_in, od_ln_g, od_ln_b, od_w_s, od_b_s, od_w_out, ffn_w_up, ffn_conv_w, ffn_conv_b, ffn_w_down, loss_target, m_norm_mix, m_norm_ffn, m_norm_final, m_ev_w_in, m_ev_conv_w, m_ev_conv_b, m_ev_gate_a_w, m_ev_gate_a_b, m_ev_gate_x_w, m_ev_gate_x_b, m_ev_lru_lambda, m_hg_lb_logits, m_ev_hg_norm, m_ev_w_out, m_od_w_in, m_od_b_in, m_od_ln_g, m_od_ln_b, m_od_w_s, m_od_b_s, m_od_w_out, m_ffn_w_up, m_ffn_conv_w, m_ffn_conv_b, m_ffn_w_down, v_norm_mix, v_norm_ffn, v_norm_final, v_ev_w_in, v_ev_conv_w, v_ev_conv_b, v_ev_gate_a_w, v_ev_gate_a_b, v_ev_gate_x_w, v_ev_gate_x_b, v_ev_lru_lambda, v_hg_lb_logits, v_ev_hg_norm, v_ev_w_out, v_od_w_in, v_od_b_in, v_od_ln_g, v_od_ln_b, v_od_w_s, v_od_b_s, v_od_w_out, v_ffn_w_up, v_ffn_conv_w, v_ffn_conv_b, v_ffn_w_down)
    names = ["x"] + WEIGHTS + ["loss_target"] + ["m_" + n for n in WEIGHTS] + ["v_" + n for n in WEIGHTS]
    out = _step(dict(zip(names, vals)))
    return (out["loss"], out["grad_x"], *[out["grad_" + n] for n in WEIGHTS], *[out["delta_" + n] for n in WEIGHTS],
            *[out["new_m_" + n] for n in WEIGHTS], *[out["new_v_" + n] for n in WEIGHTS])
```

```python
import functools

import jax
import jax.numpy as jnp
from jax import lax
from jax.experimental import pallas as pl
from jax.experimental.pallas import tpu as pltpu

F32 = jnp.float32
BF16 = jnp.bfloat16

EPS = 1e-6
D_MODEL = 1024
LRU_W = 512
LRU_BLOCKS = 8
LRU_C = 8.0
HG_HEADS = 4
HG_D = 128
HG_CHUNK = 64
SGU_G = 8
SGU_CHUNK = 128
D_FF = 2816
ADAM_LR, ADAM_B1, ADAM_B2, ADAM_EPS, ADAM_WD, ADAM_STEP = 0.001, 0.9, 0.999, 1e-08, 0.01, 10

V7X_VMEM_BYTES = 64 * 1024 * 1024
VMEM_LIMIT = V7X_VMEM_BYTES - 8 * 1024 * 1024
SUBLANES = 8
LANES = 128
BF16_ROWS = 16

GELU_C0 = 0.7978845608028654
GELU_C1 = 0.044715

NN = (((1,), (0,)), ((), ()))
NT = (((1,), (1,)), ((), ()))
TN = (((0,), (0,)), ((), ()))


def _dot(a, b, dims=NN):
    return lax.dot_general(a.astype(BF16), b.astype(BF16), dims, preferred_element_type=F32)


def _cp(n_grid):
    return pltpu.CompilerParams(dimension_semantics=("arbitrary",) * n_grid, vmem_limit_bytes=VMEM_LIMIT)


def _chunk(n, cap):
    best = LANES
    for c in range(LANES, cap + 1, LANES):
        if n % c == 0:
            best = c
    return best


def _resident(shape):
    nd = len(shape)
    return pl.BlockSpec(shape, lambda *_: (0,) * nd, pipeline_mode=pl.Buffered(1))


def _rsum8(x):
    r, c = x.shape
    return x.reshape(r // SUBLANES, SUBLANES, c).sum(axis=0)


def _sigmoid(x):
    return 0.5 * jnp.tanh(0.5 * x) + 0.5


def _gelu(x):
    return 0.5 * x * (1.0 + jnp.tanh(GELU_C0 * (x + GELU_C1 * x * x * x)))


def _gelu_grad(x):
    t = jnp.tanh(GELU_C0 * (x + GELU_C1 * x * x * x))
    return 0.5 * (1.0 + t) + 0.5 * x * (1.0 - t * t) * GELU_C0 * (1.0 + 3.0 * GELU_C1 * x * x)


def _silu_and_grad(x):
    s = _sigmoid(x)
    return x * s, s * (1.0 + x * (1.0 - s))


def _shift_rows(e, j):
    n = e.shape[0]
    return e if j % n == 0 else pltpu.roll(e, j % n, 0)


def _weight(w):
    if isinstance(w, tuple):
        stack, layer = w
        K, N = stack.shape[1:]
        return stack, pl.BlockSpec((None, K, N), lambda *_: (layer, 0, 0), pipeline_mode=pl.Buffered(1)), (K, N)
    return w, _resident(w.shape), w.shape


def _norm_mm(h, g, w, b, name, tt=1024):
    T, D = h.shape
    w, w_spec, (_, N) = _weight(w)
    cn = _chunk(N, 512)

    def body(h_ref, g_ref, w_ref, b_ref, hn_ref, z_ref):
        x = h_ref[...]
        r = lax.rsqrt(jnp.mean(x * x, axis=-1, keepdims=True) + EPS)
        hn = (x * r * g_ref[...]).astype(BF16)
        hn_ref[...] = hn
        for j in range(0, N, cn):
            acc = jnp.dot(hn, w_ref[:, j:j + cn], preferred_element_type=F32) + b_ref[:, j:j + cn]
            z_ref[:, j:j + cn] = acc.astype(BF16)

    return pl.pallas_call(
        body, name=name, grid=(T // tt,),
        in_specs=[pl.BlockSpec((tt, D), lambda i: (i, 0)), _resident((1, D)), w_spec, _resident((1, N))],
        out_specs=[pl.BlockSpec((tt, D), lambda i: (i, 0)), pl.BlockSpec((tt, N), lambda i: (i, 0))],
        out_shape=[jax.ShapeDtypeStruct((T, D), BF16), jax.ShapeDtypeStruct((T, N), BF16)],
        compiler_params=_cp(1),
    )(h, g, w, b)


def _mm(a, w, res, out_dtype, name, tt=1024, transpose_w=False):
    T, K = a.shape
    w, w_spec, w_shape = _weight(w)
    N = w_shape[0] if transpose_w else w_shape[1]
    cn = _chunk(N, 512)
    has_res = res is not None

    def body(*refs):
        a_ref, w_ref = refs[0], refs[1]
        res_ref = refs[2] if has_res else None
        o_ref = refs[-1]
        av = a_ref[...].astype(BF16)
        for j in range(0, N, cn):
            if transpose_w:
                acc = lax.dot_general(av, w_ref[j:j + cn, :], NT, preferred_element_type=F32)
            else:
                acc = jnp.dot(av, w_ref[:, j:j + cn], preferred_element_type=F32)
            if has_res:
                acc = acc + res_ref[:, j:j + cn]
            o_ref[:, j:j + cn] = acc.astype(out_dtype)

    in_specs = [pl.BlockSpec((tt, K), lambda i: (i, 0)), w_spec]
    args = [a, w]
    if has_res:
        in_specs.append(pl.BlockSpec((tt, N), lambda i: (i, 0)))
        args.append(res)
    return pl.pallas_call(
        body, name=name, grid=(T // tt,), in_specs=in_specs,
        out_specs=pl.BlockSpec((tt, N), lambda i: (i, 0)),
        out_shape=jax.ShapeDtypeStruct((T, N), out_dtype),
        compiler_params=_cp(1),
    )(*args)


def _mm_tn(a, b, name, col_shards=1, tt=2048):
    T, K = a.shape
    N = b.shape[1]
    ns = N // col_shards
    tt = min(tt, T)
    while 2 * (tt * K * a.dtype.itemsize + tt * ns * b.dtype.itemsize + K * ns * 4) > VMEM_LIMIT - 12 * 1024 * 1024:
        tt //= 2

    def body(a_ref, b_ref, o_ref):
        acc = lax.dot_general(a_ref[...].astype(BF16), b_ref[...].astype(BF16), TN, preferred_element_type=F32)
        prev = jnp.where(pl.program_id(1) == 0, 0.0, o_ref[0])
        o_ref[0] = prev + acc

    out = pl.pallas_call(
        body, name=name, grid=(col_shards, T // tt),
        in_specs=[pl.BlockSpec((tt, K), lambda n, t: (t, 0)), pl.BlockSpec((tt, ns), lambda n, t: (t, n))],
        out_specs=pl.BlockSpec((1, K, ns), lambda n, t: (n, 0, 0)),
        out_shape=jax.ShapeDtypeStruct((col_shards, K, ns), F32),
        compiler_params=_cp(2),
    )(a, b)
    return out if col_shards > 1 else out[0]


def _mm_normbwd(dz, w, x, g, dres, name):
    T, N = dz.shape
    D = w.shape[0]
    tt = 1024 if N <= 3072 else 512

    def body(dz_ref, wt_ref, x_ref, g_ref, dres_ref, dx_ref, dg_ref):
        @pl.when(pl.program_id(0) == 0)
        def _():
            dg_ref[...] = jnp.zeros_like(dg_ref)

        dy = lax.dot_general(dz_ref[...], wt_ref[...], NT, preferred_element_type=F32)
        x = x_ref[...]
        r = lax.rsqrt(jnp.mean(x * x, axis=-1, keepdims=True) + EPS)
        xn = x * r
        dg_ref[...] += _rsum8(dy * xn)
        dxn = dy * g_ref[...]
        dx_ref[...] = dres_ref[...] + r * (dxn - xn * jnp.mean(dxn * xn, axis=-1, keepdims=True))

    return pl.pallas_call(
        body, name=name, grid=(T // tt,),
        in_specs=[pl.BlockSpec((tt, N), lambda i: (i, 0)), _resident((D, N)), pl.BlockSpec((tt, D), lambda i: (i, 0)),
                  _resident((1, D)), pl.BlockSpec((tt, D), lambda i: (i, 0))],
        out_specs=[pl.BlockSpec((tt, D), lambda i: (i, 0)), pl.BlockSpec((SUBLANES, D), lambda i: (0, 0))],
        out_shape=[jax.ShapeDtypeStruct((T, D), F32), jax.ShapeDtypeStruct((SUBLANES, D), F32)],
        compiler_params=_cp(1),
    )(dz, w, x, g, dres)


def _final_loss(h, g, tgt, name="final_loss", tt=512):
    T, D = h.shape

    def body(h_ref, g_ref, t_ref, dh_ref, dg_ref, sq_ref):
        @pl.when(pl.program_id(0) == 0)
        def _():
            dg_ref[...] = jnp.zeros_like(dg_ref)
            sq_ref[...] = jnp.zeros_like(sq_ref)

        x = h_ref[...]
        r = lax.rsqrt(jnp.mean(x * x, axis=-1, keepdims=True) + EPS)
        xn = x * r
        gv = g_ref[...]
        diff = xn * gv - t_ref[...]
        sq_ref[...] += _rsum8(diff * diff)
        dy = diff * (1.0 / D)
        dg_ref[...] += _rsum8(dy * xn)
        dxn = dy * gv
        dh_ref[...] = r * (dxn - xn * jnp.mean(dxn * xn, axis=-1, keepdims=True))

    return pl.pallas_call(
        body, name=name, grid=(T // tt,),
        in_specs=[pl.BlockSpec((tt, D), lambda i: (i, 0)), _resident((1, D)), pl.BlockSpec((tt, D), lambda i: (i, 0))],
        out_specs=[pl.BlockSpec((tt, D), lambda i: (i, 0)), pl.BlockSpec((SUBLANES, D), lambda i: (0, 0)),
                   pl.BlockSpec((SUBLANES, D), lambda i: (0, 0))],
        out_shape=[jax.ShapeDtypeStruct((T, D), F32), jax.ShapeDtypeStruct((SUBLANES, D), F32),
                   jax.ShapeDtypeStruct((SUBLANES, D), F32)],
        compiler_params=_cp(1),
    )(h, g, tgt)


def _col_groups(F, cc, per_group=4):
    step = cc * per_group
    return [(g0, min(g0 + step, F)) for g0 in range(0, F, step)]


def _ffn_act(gu, cw, cb, w_down, res, name, tt=512):
    T = gu.shape[0]
    F = gu.shape[1] // 2
    w_down, wd_spec, (_, D) = _weight(w_down)
    cc = _chunk(F, 256)
    hb = tt // BF16_ROWS

    def body(gu_ref, halo_ref, cw_ref, cb_ref, wd_ref, res_ref, a_ref, gc_ref, o_ref):
        first = pl.program_id(0) == 0
        acc = res_ref[...]
        for g0, g1 in _col_groups(F, cc):
            for c0 in range(g0, g1, cc):
                cs = slice(c0, c0 + cc)
                x = gu_ref[:, cs].astype(F32)
                halo = jnp.where(first, 0.0, halo_ref[:, cs].astype(F32))
                e = jnp.concatenate([halo, x], axis=0)
                gc = (cb_ref[:, cs] + cw_ref[0:1, cs] * _shift_rows(e, 2)[BF16_ROWS:] + cw_ref[1:2, cs] * _shift_rows(e, 1)[BF16_ROWS:]
                      + cw_ref[2:3, cs] * x)
                up = gu_ref[:, F + c0:F + c0 + cc].astype(F32)
                gc_ref[:, cs] = gc.astype(BF16)
                a_ref[:, cs] = (gc * _sigmoid(gc) * up).astype(BF16)
            acc = acc + jnp.dot(a_ref[:, g0:g1], wd_ref[g0:g1, :], preferred_element_type=F32)
        o_ref[...] = acc

    return pl.pallas_call(
        body, name=name, grid=(T // tt,),
        in_specs=[pl.BlockSpec((tt, 2 * F), lambda i: (i, 0)),
                  pl.BlockSpec((BF16_ROWS, F), lambda i: (jnp.maximum(i * hb - 1, 0), 0)),
                  _resident((SUBLANES, F)), _resident((1, F)), wd_spec, pl.BlockSpec((tt, D), lambda i: (i, 0))],
        out_specs=[pl.BlockSpec((tt, F), lambda i: (i, 0)), pl.BlockSpec((tt, F), lambda i: (i, 0)),
                   pl.BlockSpec((tt, D), lambda i: (i, 0))],
        out_shape=[jax.ShapeDtypeStruct((T, F), BF16), jax.ShapeDtypeStruct((T, F), BF16), jax.ShapeDtypeStruct((T, D), F32)],
        compiler_params=_cp(1),
    )(gu, gu, cw, cb, w_down, res)


def _ffn_act_bwd(gu, gc, da, cw, w_up, x, g, dres, name, tt=256):
    T = gu.shape[0]
    F = gu.shape[1] // 2
    w_up, wu_spec, (D, _) = _weight(w_up)
    cc = _chunk(F, 256)
    hb = tt // BF16_ROWS
    last_hb = T // BF16_ROWS - 1
    nt = T // tt

    def body(gu_ref, upnext_ref, gc_ref, gcnext_ref, da_ref, danext_ref, cw_ref, wu_ref, x_ref, g_ref, dres_ref,
             dgu_ref, dc_ref, dx_ref, dg_ref):
        i = pl.program_id(0)

        @pl.when(i == 0)
        def _():
            dc_ref[...] = jnp.zeros_like(dc_ref)
            dg_ref[...] = jnp.zeros_like(dg_ref)

        n = tt + BF16_ROWS
        ext = lambda main, nxt: jnp.concatenate([main.astype(F32), nxt.astype(F32)], axis=0)
        dy = jnp.zeros((tt, D), F32)
        for lo, hi in _col_groups(F, cc):
            for c0 in range(lo, hi, cc):
                cs = slice(c0, c0 + cc)
                us = slice(F + c0, F + c0 + cc)
                gc = ext(gc_ref[:, cs], gcnext_ref[:, cs])
                up = ext(gu_ref[:, us], upnext_ref[:, cs])
                dae = ext(da_ref[:, cs], jnp.where(i == nt - 1, 0.0, danext_ref[:, cs].astype(F32)))
                s, ds = _silu_and_grad(gc)
                dgc = dae * up * ds
                dgu_ref[:, us] = (dae * s)[:tt].astype(BF16)
                dgc1 = _shift_rows(dgc, n - 1)[:tt]
                dgc2 = _shift_rows(dgc, n - 2)[:tt]
                dm = dgc[:tt]
                dgu_ref[:, cs] = (cw_ref[2:3, cs] * dm + cw_ref[1:2, cs] * dgc1 + cw_ref[0:1, cs] * dgc2).astype(BF16)
                gt = gu_ref[:, cs].astype(F32)
                dc_ref[0, :, cs] += _rsum8(dgc2 * gt)
                dc_ref[1, :, cs] += _rsum8(dgc1 * gt)
                dc_ref[2, :, cs] += _rsum8(dm * gt)
                dc_ref[3, :, cs] += _rsum8(dm)
            dy = (dy + lax.dot_general(dgu_ref[:, lo:hi], wu_ref[:, lo:hi], NT, preferred_element_type=F32)
                  + lax.dot_general(dgu_ref[:, F + lo:F + hi], wu_ref[:, F + lo:F + hi], NT, preferred_element_type=F32))
        xv = x_ref[...]
        r = lax.rsqrt(jnp.mean(xv * xv, axis=-1, keepdims=True) + EPS)
        xn = xv * r
        dg_ref[...] += _rsum8(dy * xn)
        dxn = dy * g_ref[...]
        dx_ref[...] = dres_ref[...] + r * (dxn - xn * jnp.mean(dxn * xn, axis=-1, keepdims=True))

    tok = lambda w: pl.BlockSpec((tt, w), lambda i: (i, 0))
    nxt = lambda col: pl.BlockSpec((BF16_ROWS, F), lambda i: (jnp.minimum((i + 1) * hb, last_hb), col))
    return pl.pallas_call(
        body, name=name, grid=(nt,),
        in_specs=[tok(2 * F), nxt(1), tok(F), nxt(0), tok(F), nxt(0),
                  _resident((SUBLANES, F)), wu_spec, tok(D), _resident((1, D)), tok(D)],
        out_specs=[tok(2 * F), pl.BlockSpec((4, SUBLANES, F), lambda i: (0, 0, 0)), tok(D),
                   pl.BlockSpec((SUBLANES, D), lambda i: (0, 0))],
        out_shape=[jax.ShapeDtypeStruct((T, 2 * F), BF16), jax.ShapeDtypeStruct((4, SUBLANES, F), F32),
                   jax.ShapeDtypeStruct((T, D), F32), jax.ShapeDtypeStruct((SUBLANES, D), F32)],
        compiler_params=_cp(1),
    )(gu, gu, gc, gc, da, da, cw, w_up, x, g, dres)


def _softplus_neg(lam):
    x = -lam
    y = jnp.exp(-jnp.abs(x))
    l1p = jnp.where(y < 0.01, y * (1.0 - y * (0.5 - y * (1.0 / 3.0))), jnp.log(1.0 + y))
    return jnp.maximum(x, 0.0) + l1p


def _lru_gates(xc, wa_ref, ba_ref, wx_ref, bx_ref, sp):
    xcb = xc.astype(BF16)
    r = _sigmoid(jnp.dot(xcb, wa_ref[...], preferred_element_type=F32) + ba_ref[...])
    gi = _sigmoid(jnp.dot(xcb, wx_ref[...], preferred_element_type=F32) + bx_ref[...])
    log_a = -LRU_C * r * sp
    a = jnp.exp(log_a)
    x2 = 2.0 * log_a
    series = -x2 * (1.0 + x2 * 0.5 * (1.0 + x2 * (1.0 / 3.0)))
    om = jnp.where(x2 > -0.02, series, 1.0 - a * a)
    return r, gi, a, jnp.sqrt(om)


def _lru_conv(xr, halo, cw_ref, cb_ref):
    e = jnp.concatenate([halo, xr], axis=0)
    x1 = _shift_rows(e, 1)[BF16_ROWS:]
    x2 = _shift_rows(e, 2)[BF16_ROWS:]
    x3 = _shift_rows(e, 3)[BF16_ROWS:]
    xc = cb_ref[...] + cw_ref[0:1, :] * x3 + cw_ref[1:2, :] * x2 + cw_ref[2:3, :] * x1 + cw_ref[3:4, :] * xr
    return xc, x1, x2, x3


def _lru_fwd(z, cw, cb, wa, ba, wx, bx, lam, name="lru_fwd", tt=256):
    T = z.shape[0]
    W = LRU_W
    hb = tt // BF16_ROWS
    ng = tt // SUBLANES

    def body(z_ref, halo_ref, cw_ref, cb_ref, wa_ref, ba_ref, wx_ref, bx_ref, lam_ref, oa_ref, h_ref, a_s, sv_ref, u_s, hc):
        i = pl.program_id(0)

        @pl.when(i == 0)
        def _():
            hc[...] = jnp.zeros_like(hc)

        xr = z_ref[:, W:2 * W].astype(F32)
        halo = jnp.where(i == 0, 0.0, halo_ref[...].astype(F32))
        xc, _, _, _ = _lru_conv(xr, halo, cw_ref, cb_ref)
        sp = _softplus_neg(lam_ref[...])
        r, gi, a, mult = _lru_gates(xc, wa_ref, ba_ref, wx_ref, bx_ref, sp)
        a_s[...] = a
        u_s[...] = mult * gi * xc
        for k, saved in enumerate((mult, r, gi, xc)):
            sv_ref[:, k * W:(k + 1) * W] = saved.astype(BF16)
        row = lax.broadcasted_iota(jnp.int32, (SUBLANES, W), 0)

        def step(j, hprev):
            r0 = pl.multiple_of(j * SUBLANES, SUBLANES)
            A = a_s[pl.ds(r0, SUBLANES), :]
            U = u_s[pl.ds(r0, SUBLANES), :]
            for k in (1, 2, 4):
                m = row >= k
                U = jnp.where(m, A * pltpu.roll(U, k, 0) + U, U)
                A = jnp.where(m, A * pltpu.roll(A, k, 0), A)
            H = U + A * hprev
            h_ref[pl.ds(r0, SUBLANES), :] = H
            return jnp.broadcast_to(H[SUBLANES - 1:SUBLANES, :], (SUBLANES, W))

        hc[...] = lax.fori_loop(0, ng, step, hc[...])
        oa_ref[...] = (_gelu(z_ref[:, 0:W].astype(F32)) * h_ref[...]).astype(BF16)

    return pl.pallas_call(
        body, name=name, grid=(T // tt,),
        in_specs=[pl.BlockSpec((tt, 2 * W), lambda i: (i, 0)),
                  pl.BlockSpec((BF16_ROWS, W), lambda i: (jnp.maximum(i * hb - 1, 0), 1)),
                  _resident((SUBLANES, W)), _resident((1, W)), _resident((W, W)), _resident((1, W)),
                  _resident((W, W)), _resident((1, W)), _resident((1, W))],
        out_specs=[pl.BlockSpec((tt, W), lambda i: (i, 0)), pl.BlockSpec((tt, W), lambda i: (i, 0)),
                   pl.BlockSpec((tt, W), lambda i: (i, 0)), pl.BlockSpec((tt, 4 * W), lambda i: (i, 0))],
        out_shape=[jax.ShapeDtypeStruct((T, W), BF16), jax.ShapeDtypeStruct((T, W), F32),
                   jax.ShapeDtypeStruct((T, W), F32), jax.ShapeDtypeStruct((T, 4 * W), BF16)],
        scratch_shapes=[pltpu.VMEM((tt, W), F32), pltpu.VMEM((SUBLANES, W), F32)],
        compiler_params=_cp(1),
    )(z, z, cw, cb, wa, ba, wx, bx, lam)


def _lru_bwd(z, hseq, a_all, saved, dmix, cw, wat, wxt, lam, name="lru_bwd", tt=256):
    T = z.shape[0]
    W = LRU_W
    nt = T // tt
    sb = tt // SUBLANES
    ng = tt // SUBLANES

    def body(z_ref, h_ref, hprev_ref, a_ref, sv_ref, dm_ref, cw_ref, wat_ref, wxt_ref, lam_ref,
             dz_ref, dc_ref, dwa_ref, dwx_ref, dv_ref, c_s, d_s, g_s, gc, an, dxn):
        i = pl.program_id(0)
        ti = nt - 1 - i

        @pl.when(i == 0)
        def _():
            dc_ref[...] = jnp.zeros_like(dc_ref)
            dwa_ref[...] = jnp.zeros_like(dwa_ref)
            dwx_ref[...] = jnp.zeros_like(dwx_ref)
            dv_ref[...] = jnp.zeros_like(dv_ref)
            gc[...] = jnp.zeros_like(gc)
            an[...] = jnp.zeros_like(an)
            dxn[...] = jnp.zeros_like(dxn)

        xr = z_ref[:, W:2 * W].astype(F32)
        yg = z_ref[:, 0:W].astype(F32)
        sp = _softplus_neg(lam_ref[...])
        a = a_ref[...]
        mult, r, gi, xc = (sv_ref[:, k * W:(k + 1) * W].astype(F32) for k in range(4))
        h = h_ref[...]
        hp = jnp.where(ti == 0, 0.0, hprev_ref[...])
        hm1 = _shift_rows(jnp.concatenate([hp, h], axis=0), 1)[SUBLANES:]
        dout = dm_ref[...].astype(F32)
        d_s[...] = dout * _gelu(yg)
        dz_ref[:, 0:W] = (dout * h * _gelu_grad(yg)).astype(BF16)
        c_s[...] = _shift_rows(jnp.concatenate([a, an[...]], axis=0), tt + SUBLANES - 1)[:tt]
        an[...] = a[0:SUBLANES, :]
        row = lax.broadcasted_iota(jnp.int32, (SUBLANES, W), 0)

        def step(j, gnext):
            r0 = pl.multiple_of((ng - 1 - j) * SUBLANES, SUBLANES)
            C = c_s[pl.ds(r0, SUBLANES), :]
            G = d_s[pl.ds(r0, SUBLANES), :]
            for k in (1, 2, 4):
                m = row < SUBLANES - k
                G = jnp.where(m, G + C * pltpu.roll(G, SUBLANES - k, 0), G)
                C = jnp.where(m, C * pltpu.roll(C, SUBLANES - k, 0), C)
            G = G + C * gnext
            g_s[pl.ds(r0, SUBLANES), :] = G
            return jnp.broadcast_to(G[0:1, :], (SUBLANES, W))

        gc[...] = lax.fori_loop(0, ng, step, gc[...])
        du = g_s[...]
        da = du * hm1
        dgi = du * mult * xc
        dxc = du * mult * gi
        dmult = du * gi * xc
        dlog_a = da * a - dmult * (a * a) / mult
        dr = dlog_a * (-LRU_C * sp)
        dv_ref[2] += _rsum8(dlog_a * (-LRU_C * r))
        dpr = (dr * r * (1.0 - r)).astype(BF16)
        dpi = (dgi * gi * (1.0 - gi)).astype(BF16)
        dv_ref[0] += _rsum8(dpr.astype(F32))
        dv_ref[1] += _rsum8(dpi.astype(F32))
        xcb = sv_ref[:, 3 * W:4 * W]
        dwa_ref[...] += lax.dot_general(xcb, dpr, TN, preferred_element_type=F32)
        dwx_ref[...] += lax.dot_general(xcb, dpi, TN, preferred_element_type=F32)
        dxc = dxc + jnp.dot(dpr, wat_ref[...], preferred_element_type=F32) + jnp.dot(dpi, wxt_ref[...], preferred_element_type=F32)
        n = tt + BF16_ROWS
        de = jnp.concatenate([dxc, dxn[...]], axis=0)
        d1, d2, d3 = (_shift_rows(de, n - j)[:tt] for j in (1, 2, 3))
        dxn[...] = dxc[0:BF16_ROWS, :]
        dz_ref[:, W:2 * W] = (cw_ref[3:4, :] * dxc + cw_ref[2:3, :] * d1 + cw_ref[1:2, :] * d2 + cw_ref[0:1, :] * d3).astype(BF16)
        dc_ref[0] += _rsum8(d3 * xr)
        dc_ref[1] += _rsum8(d2 * xr)
        dc_ref[2] += _rsum8(d1 * xr)
        dc_ref[3] += _rsum8(dxc * xr)
        dc_ref[4] += _rsum8(dxc)

    rev = lambda i: nt - 1 - i
    tok = lambda w: pl.BlockSpec((tt, w), lambda i: (rev(i), 0))
    return pl.pallas_call(
        body, name=name, grid=(nt,),
        in_specs=[tok(2 * W), tok(W),
                  pl.BlockSpec((SUBLANES, W), lambda i: (jnp.maximum(rev(i) * sb - 1, 0), 0)),
                  tok(W), tok(4 * W), tok(W),
                  _resident((SUBLANES, W)), _resident((W, W)), _resident((W, W)), _resident((1, W))],
        out_specs=[pl.BlockSpec((tt, 2 * W), lambda i: (rev(i), 0)),
                   pl.BlockSpec((5, SUBLANES, W), lambda i: (0, 0, 0)),
                   pl.BlockSpec((W, W), lambda i: (0, 0)), pl.BlockSpec((W, W), lambda i: (0, 0)),
                   pl.BlockSpec((3, SUBLANES, W), lambda i: (0, 0, 0))],
        out_shape=[jax.ShapeDtypeStruct((T, z.shape[1]), BF16), jax.ShapeDtypeStruct((5, SUBLANES, W), F32),
                   jax.ShapeDtypeStruct((W, W), F32), jax.ShapeDtypeStruct((W, W), F32),
                   jax.ShapeDtypeStruct((3, SUBLANES, W), F32)],
        scratch_shapes=[pltpu.VMEM((tt, W), F32), pltpu.VMEM((tt, W), F32), pltpu.VMEM((tt, W), F32),
                        pltpu.VMEM((SUBLANES, W), F32), pltpu.VMEM((SUBLANES, W), F32), pltpu.VMEM((BF16_ROWS, W), F32)],
        compiler_params=_cp(1),
    )(z, hseq, hseq, a_all, saved, dmix, cw, wat, wxt, lam)


def _split3(x):
    hi = x.astype(BF16)
    r1 = x - hi.astype(F32)
    mid = r1.astype(BF16)
    lo = (r1 - mid.astype(F32)).astype(BF16)
    return hi, mid, lo


def _tri_matmul(tri, x):
    hi, mid, lo = _split3(x)
    return (jnp.dot(tri, hi, preferred_element_type=F32) + jnp.dot(tri, mid, preferred_element_type=F32)
            + jnp.dot(tri, lo, preferred_element_type=F32))


def _hg_chunk(q, fl, lb):
    C = q.shape[0]
    ri = lax.broadcasted_iota(jnp.int32, (C, C), 0)
    ci = lax.broadcasted_iota(jnp.int32, (C, C), 1)
    causal = ri >= ci
    sig = _sigmoid(fl)
    f = lb + (1.0 - lb) * sig
    k = 1.0 - f
    sq = _sigmoid(q)
    qf = q * sq
    b = _tri_matmul(causal.astype(BF16), jnp.log(f))
    bm = b[C // 2 - 1:C // 2, :]
    bl = b[C - 1:C, :]
    e_qt, e_kt, e_in, e_out = jnp.exp(b - bm), jnp.exp(bm - b), jnp.exp(b), jnp.exp(bl - b)
    qt = qf * e_qt
    kt = k * e_kt
    qin = qf * e_in
    kout = k * e_out
    qtb, ktb = qt.astype(BF16), kt.astype(BF16)
    att = [jnp.where(causal, _dot(qtb[:, _head(h)], ktb[:, _head(h)], NT), 0.0).astype(BF16) for h in range(HG_HEADS)]
    return dict(sig=sig, f=f, k=k, sq=sq, qf=qf, b=b, bm=bm, bl=bl, qt=qt, kt=kt, qin=qin, kout=kout, att=att,
                causal=causal, anti=ri <= ci, decay=jnp.exp(bl), e_qt=e_qt, e_kt=e_kt, e_in=e_in, e_out=e_out)


def _head(h):
    return slice(h * HG_D, (h + 1) * HG_D)


def _hgrn_fwd(z, lb, gn, name="hgrn_fwd", tt=256):
    T = z.shape[0]
    C = HG_CHUNK
    nc = tt // C
    Dh = HG_D
    Wd = HG_HEADS * Dh

    def body(q_ref, f_ref, v_ref, g_ref, lb_ref, gn_ref, o_ref, ss_ref, st):
        @pl.when(pl.program_id(0) == 0)
        def _():
            st[...] = jnp.zeros_like(st)

        S = [st[h] for h in range(HG_HEADS)]
        for c in range(nc):
            rows = slice(c * C, (c + 1) * C)
            ck = _hg_chunk(q_ref[rows, :].astype(F32), f_ref[rows, :].astype(F32), lb_ref[...])
            v = v_ref[rows, :]
            g = g_ref[rows, :].astype(F32)
            H = range(HG_HEADS)
            qinb, koutb = ck["qin"].astype(BF16), ck["kout"].astype(BF16)
            for h in H:
                ss_ref[h, c] = S[h]
            o = [_dot(ck["att"][h], v[:, _head(h)]) + _dot(qinb[:, _head(h)], S[h], NT) for h in H]
            S = [ck["decay"][:, _head(h)] * S[h] + _dot(v[:, _head(h)], koutb[:, _head(h)], TN) for h in H]
            outs = [o[h] * lax.rsqrt(jnp.mean(o[h] * o[h], axis=-1, keepdims=True) + EPS) * gn_ref[...] for h in H]
            o_ref[rows, :] = (jnp.concatenate(outs, axis=1) * (g * _sigmoid(g))).astype(BF16)
        for h in range(HG_HEADS):
            st[h] = S[h]

    col = lambda base: (lambda i: (i, base))
    return pl.pallas_call(
        body, name=name, grid=(T // tt,),
        in_specs=[pl.BlockSpec((tt, Wd), col(2)), pl.BlockSpec((tt, Wd), col(3)), pl.BlockSpec((tt, Wd), col(4)),
                  pl.BlockSpec((tt, Wd), col(5)), _resident((1, Wd)), _resident((1, Dh))],
        out_specs=[pl.BlockSpec((tt, Wd), lambda i: (i, 0)),
                   pl.BlockSpec((HG_HEADS, nc, Dh, Dh), lambda i: (0, i, 0, 0))],
        out_shape=[jax.ShapeDtypeStruct((T, Wd), BF16),
                   jax.ShapeDtypeStruct((HG_HEADS, T // C, Dh, Dh), F32)],
        scratch_shapes=[pltpu.VMEM((HG_HEADS, Dh, Dh), F32)],
        compiler_params=_cp(1),
    )(z, z, z, z, lb, gn)


def _hgrn_bwd(z, ss, dmix, lb, gn, dz, name="hgrn_bwd", tt=256):
    T = z.shape[0]
    C = HG_CHUNK
    nc = tt // C
    nt = T // tt
    Dh = HG_D
    Wd = HG_HEADS * Dh

    def body(q_ref, f_ref, v_ref, g_ref, ss_ref, dm_ref, lb_ref, gn_ref, dz01_ref, dz_ref, dlb_ref, dgn_ref, dst):
        @pl.when(pl.program_id(0) == 0)
        def _():
            dst[...] = jnp.zeros_like(dst)
            dlb_ref[...] = jnp.zeros_like(dlb_ref)
            dgn_ref[...] = jnp.zeros_like(dgn_ref)

        dS = [dst[h] for h in range(HG_HEADS)]
        lbv = lb_ref[...]
        gnv = gn_ref[...]
        rowc = lax.broadcasted_iota(jnp.int32, (C, Wd), 0)
        cat = lambda xs: jnp.concatenate(xs, axis=1)
        for c in reversed(range(nc)):
            rows = slice(c * C, (c + 1) * C)
            q = q_ref[rows, :].astype(F32)
            ck = _hg_chunk(q, f_ref[rows, :].astype(F32), lbv)
            v = v_ref[rows, :]
            g = g_ref[rows, :].astype(F32)
            dout = dm_ref[rows, :].astype(F32)
            sg, dsg = _silu_and_grad(g)
            d_ong = dout * sg
            H = range(HG_HEADS)
            qinb, koutb, qtb, ktb = (ck[n].astype(BF16) for n in ("qin", "kout", "qt", "kt"))
            S = [ss_ref[h, c] for h in H]
            Sb = [s.astype(BF16) for s in S]
            dSb = [d.astype(BF16) for d in dS]
            o = [_dot(ck["att"][h], v[:, _head(h)]) + _dot(qinb[:, _head(h)], Sb[h], NT) for h in H]
            rn = [lax.rsqrt(jnp.mean(o[h] * o[h], axis=-1, keepdims=True) + EPS) for h in H]
            on = [o[h] * rn[h] for h in H]
            don = [d_ong[:, _head(h)] * gnv for h in H]
            do = [(rn[h] * (don[h] - on[h] * jnp.mean(don[h] * on[h], axis=-1, keepdims=True))).astype(BF16) for h in H]
            datt = [jnp.where(ck["causal"], _dot(do[h], v[:, _head(h)], NT), 0.0).astype(BF16) for h in H]
            dvs = [_dot(ck["att"][h], do[h], TN) + _dot(koutb[:, _head(h)], dSb[h], NT) for h in H]
            dqins = [_dot(do[h], Sb[h]) for h in H]
            dkouts = [_dot(v[:, _head(h)], dSb[h]) for h in H]
            dqts = [_dot(datt[h], ktb[:, _head(h)]) for h in H]
            dkts = [_dot(datt[h], qtb[:, _head(h)], TN) for h in H]
            ddecays = [jnp.sum(dS[h] * S[h], axis=0, keepdims=True) for h in H]
            dS = [_dot(do[h], qinb[:, _head(h)], TN) + ck["decay"][:, _head(h)] * dS[h] for h in H]
            ons = [on[h] * gnv for h in H]
            dgn = _rsum8(d_ong[:, _head(0)] * on[0])
            for h in range(1, HG_HEADS):
                dgn = dgn + _rsum8(d_ong[:, _head(h)] * on[h])
            dgn_ref[...] += dgn
            dqt, dkt, dqin, dkout, ddecay = cat(dqts), cat(dkts), cat(dqins), cat(dkouts), cat(ddecays)
            dqf = dqt * ck["e_qt"] + dqin * ck["e_in"]
            dk = dkt * ck["e_kt"] + dkout * ck["e_out"]
            kk = dkout * ck["kout"]
            db = dqt * ck["qt"] - dkt * ck["kt"] + dqin * ck["qin"] - kk
            dbl = jnp.sum(kk, axis=0, keepdims=True) + ddecay * ck["decay"]
            db = db + jnp.where(rowc == C - 1, dbl, 0.0)
            dlogf = _tri_matmul(ck["anti"].astype(BF16), db)
            dfv = dlogf / ck["f"] - dk
            sig, sq = ck["sig"], ck["sq"]
            dlb_ref[...] += _rsum8(dfv * (1.0 - sig))
            dz_ref[rows, 2 * Wd:3 * Wd] = (dqf * (sq * (1.0 + q * (1.0 - sq)))).astype(BF16)
            dz_ref[rows, 3 * Wd:4 * Wd] = (dfv * (1.0 - lbv) * sig * (1.0 - sig)).astype(BF16)
            dz_ref[rows, 4 * Wd:5 * Wd] = cat(dvs).astype(BF16)
            dz_ref[rows, 5 * Wd:6 * Wd] = (dout * cat(ons) * dsg).astype(BF16)
        dz_ref[:, 0:2 * Wd] = dz01_ref[...]
        for h in range(HG_HEADS):
            dst[h] = dS[h]

    rev = lambda i: nt - 1 - i
    col = lambda base: (lambda i: (rev(i), base))
    return pl.pallas_call(
        body, name=name, grid=(nt,),
        in_specs=[pl.BlockSpec((tt, Wd), col(2)), pl.BlockSpec((tt, Wd), col(3)), pl.BlockSpec((tt, Wd), col(4)),
                  pl.BlockSpec((tt, Wd), col(5)),
                  pl.BlockSpec((HG_HEADS, nc, Dh, Dh), lambda i: (0, rev(i), 0, 0)),
                  pl.BlockSpec((tt, Wd), col(1)), _resident((1, Wd)), _resident((1, Dh)),
                  pl.BlockSpec((tt, 2 * Wd), col(0))],
        out_specs=[pl.BlockSpec((tt, 6 * Wd), lambda i: (rev(i), 0)), pl.BlockSpec((SUBLANES, Wd), lambda i: (0, 0)),
                   pl.BlockSpec((SUBLANES, Dh), lambda i: (0, 0))],
        out_shape=[jax.ShapeDtypeStruct((T, 6 * Wd), BF16), jax.ShapeDtypeStruct((SUBLANES, Wd), F32),
                   jax.ShapeDtypeStruct((SUBLANES, Dh), F32)],
        input_output_aliases={8: 0},
        scratch_shapes=[pltpu.VMEM((HG_HEADS, Dh, Dh), F32)],
        compiler_params=_cp(1),
    )(z, z, z, z, ss, dmix, lb, gn, dz)


def _sgu_core(p, lg_ref, lb_ref, wsc_ref, bsb_ref):
    Wd = D_MODEL
    G = SGU_CHUNK
    zz = _gelu(p)
    u = zz[:, :Wd]
    v = zz[:, Wd:]
    vc = v - jnp.mean(v, axis=-1, keepdims=True)
    rstd = lax.rsqrt(jnp.mean(vc * vc, axis=-1, keepdims=True) + EPS)
    vhat = vc * rstd
    vn = vhat * lg_ref[...] + lb_ref[...]
    svs = []
    for gi in range(SGU_G):
        svs.append(jnp.dot(wsc_ref[gi], vn[:, gi * G:(gi + 1) * G].astype(BF16), preferred_element_type=F32) + bsb_ref[gi])
    return u, vhat, rstd, vn, jnp.concatenate(svs, axis=1)


def _sgu_fwd(p1, lg, lbias, wsc, bsb, name="sgu_fwd", tt=512):
    T = p1.shape[0]
    Wd = D_MODEL
    C = SGU_CHUNK

    def body(p_ref, lg_ref, lb_ref, wsc_ref, bsb_ref, s_ref):
        for c in range(tt // C):
            rows = slice(c * C, (c + 1) * C)
            u, _, _, _, sv = _sgu_core(p_ref[rows, :].astype(F32), lg_ref, lb_ref, wsc_ref, bsb_ref)
            s_ref[rows, :] = (u * sv).astype(BF16)

    return pl.pallas_call(
        body, name=name, grid=(T // tt,),
        in_specs=[pl.BlockSpec((tt, 2 * Wd), lambda i: (i, 0)), _resident((1, Wd)), _resident((1, Wd)),
                  _resident((SGU_G, C, C)), _resident((SGU_G, C, C))],
        out_specs=pl.BlockSpec((tt, Wd), lambda i: (i, 0)),
        out_shape=jax.ShapeDtypeStruct((T, Wd), BF16),
        compiler_params=_cp(1),
    )(p1, lg, lbias, wsc, bsb)


def _sgu_bwd(p1, ds, lg, lbias, wsc, wsct, bsb, name="sgu_bwd", tt=512):
    T = p1.shape[0]
    Wd = D_MODEL
    C = SGU_CHUNK

    def body(p_ref, ds_ref, lg_ref, lb_ref, wsc_ref, wsct_ref, bsb_ref, dp_ref, dws_ref, dbs_ref, dlg_ref, dlb_ref, dbin_ref):
        @pl.when(pl.program_id(0) == 0)
        def _():
            dws_ref[...] = jnp.zeros_like(dws_ref)
            dbs_ref[...] = jnp.zeros_like(dbs_ref)
            dlg_ref[...] = jnp.zeros_like(dlg_ref)
            dlb_ref[...] = jnp.zeros_like(dlb_ref)
            dbin_ref[...] = jnp.zeros_like(dbin_ref)

        for c in range(tt // C):
            rows = slice(c * C, (c + 1) * C)
            p = p_ref[rows, :].astype(F32)
            u, vhat, rstd, vn, sv = _sgu_core(p, lg_ref, lb_ref, wsc_ref, bsb_ref)
            dsc = ds_ref[rows, :].astype(F32)
            du = dsc * sv
            dsv = dsc * u
            dvns = []
            for gi in range(SGU_G):
                cs = slice(gi * C, (gi + 1) * C)
                dsv_g = dsv[:, cs]
                dvns.append(jnp.dot(wsct_ref[gi], dsv_g.astype(BF16), preferred_element_type=F32))
                dws_ref[gi] += _dot(dsv_g, vn[:, cs], NT)
                dbs_ref[gi] += dsv_g
            dvn = jnp.concatenate(dvns, axis=1)
            dlg_ref[...] += _rsum8(dvn * vhat)
            dlb_ref[...] += _rsum8(dvn)
            dvh = dvn * lg_ref[...]
            dv = rstd * (dvh - jnp.mean(dvh, axis=-1, keepdims=True) - vhat * jnp.mean(dvh * vhat, axis=-1, keepdims=True))
            dp = jnp.concatenate([du, dv], axis=1) * _gelu_grad(p)
            dbin_ref[...] += _rsum8(dp)
            dp_ref[rows, :] = dp.astype(BF16)

    full3 = pl.BlockSpec((SGU_G, C, C), lambda i: (0, 0, 0))
    return pl.pallas_call(
        body, name=name, grid=(T // tt,),
        in_specs=[pl.BlockSpec((tt, 2 * Wd), lambda i: (i, 0)), pl.BlockSpec((tt, Wd), lambda i: (i, 0)),
                  _resident((1, Wd)), _resident((1, Wd)), _resident((SGU_G, C, C)), _resident((SGU_G, C, C)),
                  _resident((SGU_G, C, C))],
        out_specs=[pl.BlockSpec((tt, 2 * Wd), lambda i: (i, 0)), full3, full3,
                   pl.BlockSpec((SUBLANES, Wd), lambda i: (0, 0)), pl.BlockSpec((SUBLANES, Wd), lambda i: (0, 0)),
                   pl.BlockSpec((SUBLANES, 2 * Wd), lambda i: (0, 0))],
        out_shape=[jax.ShapeDtypeStruct((T, 2 * Wd), BF16), jax.ShapeDtypeStruct((SGU_G, C, C), F32),
                   jax.ShapeDtypeStruct((SGU_G, C, C), F32), jax.ShapeDtypeStruct((SUBLANES, Wd), F32),
                   jax.ShapeDtypeStruct((SUBLANES, Wd), F32), jax.ShapeDtypeStruct((SUBLANES, 2 * Wd), F32)],
        compiler_params=_cp(1),
    )(p1, ds, lg, lbias, wsc, wsct, bsb)


def _pad_rows(w, rows=SUBLANES):
    return jnp.pad(w, ((0, rows - w.shape[0]), (0, 0)))


def _block_diag(w):
    n, b, _ = w.shape
    return (w[:, :, None, :] * jnp.eye(n, dtype=w.dtype)[:, None, :, None]).reshape(n * b, n * b)


def _diag_blocks(m, n):
    b = m.shape[0] // n
    m4 = m.reshape(n, b, n, b)
    return jnp.stack([m4[k, :, k, :] for k in range(n)], axis=0)


def _piece_major(dw):
    if dw.ndim == 2:
        K, N = dw.shape
        return dw.reshape(N_CHIPS, 2, K // (2 * N_CHIPS), N)
    _, K, ns = dw.shape
    return dw.reshape(N_CHIPS, 2, K // 2, ns)


def _ffn_fwd(h, g, w_up, cw, cb, w_down, tag):
    hn, gu = _norm_mm(h, g, w_up, jnp.zeros((1, 2 * D_FF), F32), name=f"ffn_up_{tag}")
    a, gc, out = _ffn_act(gu, cw, cb, w_down, h, name=f"ffn_act_down_{tag}")
    return out, (hn, gu, gc, a)


def _ffn_bwd(dh, h, g, saved, w_up, cw, w_down, tag):
    hn, gu, gc, a = saved
    da = _mm(dh, w_down, None, BF16, name=f"ffn_da_{tag}", transpose_w=True)
    dwd = _mm_tn(a, dh, name=f"ffn_dwd_{tag}")
    dgu, dc, dhin, dg8 = _ffn_act_bwd(gu, gc, da, cw, w_up, h, g, dh, name=f"ffn_actb_dh_{tag}")
    dwu = _mm_tn(hn, dgu, name=f"ffn_dwu_{tag}", col_shards=N_CHIPS)
    dcs = dc.sum(axis=1)
    return dhin, dg8.sum(axis=0), dwu, dcs[0:3], dcs[3], dwd


REDUCE_GROUPS = {"g1": [("ffn_w_up", 1), ("ffn_w_down", 1), ("od_w_out", 0), ("od_w_in", 0)],
                 "g2": [("ffn_w_up", 0), ("ffn_w_down", 0)],
                 "g3": [("ev_w_out", 0), ("ev_w_in", 0)]}


def _local_step(x, tgt, p, start_reduce=None, continue_reduce=None):
    row = lambda v: v.reshape(1, -1)
    grads = {}

    lower = jax.nn.softmax(p["hg_lb_logits"], axis=0)
    lb0 = row(lower[0])
    ev_cw = _pad_rows(p["ev_conv_w"][0])
    ev_cb = row(p["ev_conv_b"][0])
    wa = _block_diag(p["ev_gate_a_w"][0]).astype(BF16)
    wx = _block_diag(p["ev_gate_x_w"][0]).astype(BF16)
    ba, bx, lam = row(p["ev_gate_a_b"][0]), row(p["ev_gate_x_b"][0]), row(p["ev_lru_lambda"][0])
    gn = row(p["ev_hg_norm"][0])
    tril = jnp.tril(jnp.ones((SGU_CHUNK, SGU_CHUNK), F32))
    wsc = (p["od_w_s"][0] * tril).astype(BF16)
    bsb = jnp.broadcast_to(p["od_b_s"][0][:, :, None], (SGU_G, SGU_CHUNK, SGU_CHUNK)).astype(F32)
    ffn_cw = [_pad_rows(p["ffn_conv_w"][l]) for l in range(2)]
    ffn_cb = [row(p["ffn_conv_b"][l]) for l in range(2)]
    ev_w_in, ev_w_out = p["ev_w_in"][0], p["ev_w_out"][0]
    nm = [row(p["norm_mix"][l]) for l in range(2)]
    nf = [row(p["norm_ffn"][l]) for l in range(2)]

    h0 = x
    hn0, z0 = _norm_mm(h0, nm[0], ev_w_in, jnp.zeros((1, ev_w_in.shape[1]), F32), name="ev_in")
    out_a, hseq, a_all, lru_saved = _lru_fwd(z0, ev_cw, ev_cb, wa, ba, wx, bx, lam)
    out_b, ss = _hgrn_fwd(z0, lb0, gn)
    mix0 = jnp.concatenate([out_a, out_b], axis=1)
    h1 = _mm(mix0, ev_w_out, h0, F32, name="ev_out")
    late = p["late"](h1) if "late" in p else p
    od_w_in, od_w_out = late["od_w_in"][0], late["od_w_out"][0]
    w_up = [(late["ffn_w_up"], l) for l in range(2)]
    w_down = [(late["ffn_w_down"], l) for l in range(2)]
    h2, ffn0 = _ffn_fwd(h1, nf[0], w_up[0], ffn_cw[0], ffn_cb[0], w_down[0], "l0")
    hn1, p1 = _norm_mm(h2, nm[1], od_w_in, row(p["od_b_in"][0]), name="od_in")
    s1 = _sgu_fwd(p1, row(p["od_ln_g"][0]), row(p["od_ln_b"][0]), wsc, bsb)
    h3 = _mm(s1, od_w_out, h2, F32, name="od_out")
    h4, ffn1 = _ffn_fwd(h3, nf[1], w_up[1], ffn_cw[1], ffn_cb[1], w_down[1], "l1")
    dh4, dgf8, sq8 = _final_loss(h4, row(p["norm_final"]), tgt)
    grads["norm_final"] = dgf8.sum(axis=0)

    big = {}
    dh3, dnf1, dwu1, dcw1, dcb1, dwd1 = _ffn_bwd(dh4, h3, nf[1], ffn1, w_up[1], ffn_cw[1], w_down[1], "l1")
    big["ffn_w_up", 1], big["ffn_w_down", 1] = _piece_major(dwu1), _piece_major(dwd1)
    ds1 = _mm(dh3, od_w_out, None, BF16, name="od_ds", transpose_w=True)
    big["od_w_out", 0] = _piece_major(_mm_tn(s1, dh3, name="od_dwo"))
    wsct = jnp.swapaxes(wsc, 1, 2)
    dp1, dws, dbs, dlg8, dlb8, dbin8 = _sgu_bwd(p1, ds1, row(p["od_ln_g"][0]), row(p["od_ln_b"][0]), wsc, wsct, bsb)
    grads["od_w_s"] = (dws * tril)[None]
    grads["od_b_s"] = dbs.sum(axis=-1)[None]
    grads["od_ln_g"] = dlg8.sum(axis=0)[None]
    grads["od_ln_b"] = dlb8.sum(axis=0)[None]
    grads["od_b_in"] = dbin8.sum(axis=0)[None]
    dh2, dnm1 = _mm_normbwd(dp1, od_w_in, h2, nm[1], dh3, name="od_dh")
    big["od_w_in", 0] = _piece_major(_mm_tn(hn1, dp1, name="od_dwi", col_shards=N_CHIPS))
    if start_reduce is not None:
        token = start_reduce("g1", [big[key] for key in REDUCE_GROUPS["g1"]], split=True)
        ffn_cw[0] = ffn_cw[0] + token[0:1, 0:1]

    dh1, dnf0, dwu0, dcw0, dcb0, dwd0 = _ffn_bwd(dh2, h1, nf[0], ffn0, w_up[0], ffn_cw[0], w_down[0], "l0")
    big["ffn_w_up", 0], big["ffn_w_down", 0] = _piece_major(dwu0), _piece_major(dwd0)
    if start_reduce is not None:
        token = continue_reduce("g1", dh1) + start_reduce("g2", [big[key] for key in REDUCE_GROUPS["g2"]], split=True)
        lam = lam + token[0:1, 0:1]
    dmix = _mm(dh1, ev_w_out, None, BF16, name="ev_dmix", transpose_w=True)
    big["ev_w_out", 0] = _piece_major(_mm_tn(mix0, dh1, name="ev_dwo"))
    dz01, dc5, dwa, dwx, dvec = _lru_bwd(z0, hseq, a_all, lru_saved, dmix, ev_cw, wa.T, wx.T, lam)
    if start_reduce is not None:
        lb0 = lb0 + continue_reduce("g2", dc5)[0:1, 0:1]
    dz0, dlb8, dgn8 = _hgrn_bwd(z0, ss, dmix, lb0, gn, dz01)
    big["ev_w_in", 0] = _piece_major(_mm_tn(hn0, dz0, name="ev_dwi", col_shards=N_CHIPS))
    if start_reduce is not None:
        token = start_reduce("g3", [big[key] for key in REDUCE_GROUPS["g3"]])
        nm[0] = nm[0] + token[0:1, 0:1]
    grad_x, dnm0 = _mm_normbwd(dz0, ev_w_in, h0, nm[0], dh1, name="ev_dh")

    dc5s = dc5.sum(axis=1)
    grads["ev_conv_w"] = dc5s[0:4][None]
    grads["ev_conv_b"] = dc5s[4][None]
    grads["ev_gate_a_w"] = _diag_blocks(dwa, LRU_BLOCKS)[None]
    grads["ev_gate_x_w"] = _diag_blocks(dwx, LRU_BLOCKS)[None]
    dvs = dvec.sum(axis=1)
    grads["ev_gate_a_b"] = dvs[0][None]
    grads["ev_gate_x_b"] = dvs[1][None]
    grads["ev_lru_lambda"] = (dvs[2] * (-jax.nn.sigmoid(-p["ev_lru_lambda"][0])))[None]
    dlb = dlb8.sum(axis=0)
    grads["hg_lb_logits"] = dlb[None, :] * lower[0][None, :] * (jnp.eye(3, dtype=F32)[0][:, None] - lower)
    grads["ev_hg_norm"] = dgn8.sum(axis=0)[None]
    grads["norm_mix"] = jnp.stack([dnm0.sum(axis=0), dnm1.sum(axis=0)])
    grads["norm_ffn"] = jnp.stack([dnf0, dnf1])
    grads["ffn_conv_w"] = jnp.stack([dcw0, dcw1])
    grads["ffn_conv_b"] = jnp.stack([dcb0, dcb1])
    return sq8, grad_x, grads, big


MESH = pl.DeviceIdType.MESH
ANY = pl.BlockSpec(memory_space=pl.ANY)
N_CHIPS = 4
N_DEV = 8

SH_BIG = {"ev_w_in": 2, "ev_w_out": 1, "od_w_in": 2, "od_w_out": 1, "ffn_w_up": 2, "ffn_w_down": 1}
SH_SMALL = {"ev_conv_w": 2, "od_b_in": 1, "od_ln_g": 1, "od_ln_b": 1, "ffn_conv_w": 2}
REP = ["norm_mix", "norm_ffn", "norm_final", "ev_conv_b", "ev_gate_a_w", "ev_gate_a_b", "ev_gate_x_w", "ev_gate_x_b",
       "ev_lru_lambda", "hg_lb_logits", "ev_hg_norm", "od_w_s", "od_b_s", "ffn_conv_b"]
WEIGHTS = ["norm_mix", "norm_ffn", "norm_final", "ev_w_in", "ev_conv_w", "ev_conv_b", "ev_gate_a_w", "ev_gate_a_b", "ev_gate_x_w",
           "ev_gate_x_b", "ev_lru_lambda", "hg_lb_logits", "ev_hg_norm", "ev_w_out", "od_w_in", "od_b_in", "od_ln_g", "od_ln_b",
           "od_w_s", "od_b_s", "od_w_out", "ffn_w_up", "ffn_conv_w", "ffn_conv_b", "ffn_w_down"]


def _rows(n_elems, mult=SUBLANES):
    r = -(-n_elems // LANES)
    return -(-r // mult) * mult


def _pack(arrs, rows, dtype):
    flat = jnp.concatenate([a.reshape(-1).astype(dtype) for a in arrs])
    return jnp.pad(flat, (0, rows * LANES - flat.shape[0])).reshape(rows, LANES)


def _unpack(flat2d, shapes):
    flat = flat2d.reshape(-1)
    out, off = [], 0
    for s in shapes:
        n = 1
        for d in s:
            n *= d
        out.append(flat[off:off + n].reshape(s))
        off += n
    return out


def _mesh_pos():
    return lax.axis_index("x"), lax.axis_index("y"), lax.axis_index("c")


def _other_chips(x, y):
    return [(1 - x, y), (x, 1 - y), (1 - x, 1 - y)]


def _half_rows(n):
    return lambda r, c: r.at[0, pl.ds(c * (n // 2), n // 2), :]


GATHER_BIG = {
    "ev_w_in": ((1024, 3072), _half_rows(1024), lambda o, k, c: o.at[pl.ds(c * 512, 512), pl.ds(k * 768, 768)]),
    "ev_w_out": ((1024, 1024), _half_rows(256), lambda o, k, c: o.at[pl.ds(k * 256 + c * 128, 128), :]),
    "od_w_in": ((1024, 2048), _half_rows(1024), lambda o, k, c: o.at[pl.ds(c * 512, 512), pl.ds(k * 512, 512)]),
    "od_w_out": ((1024, 1024), _half_rows(256), lambda o, k, c: o.at[pl.ds(k * 256 + c * 128, 128), :]),
    "ffn_w_up": ((2, 1024, 2 * D_FF), lambda r, c: r.at[c], lambda o, k, c: o.at[c, :, pl.ds(k * (2 * D_FF // 4), 2 * D_FF // 4)]),
    "ffn_w_down": ((2, D_FF, 1024), lambda r, c: r.at[c], lambda o, k, c: o.at[c, pl.ds(k * (D_FF // 4), D_FF // 4), :]),
}


def _gather_weights(names, big, small):
    nb = len(big)
    descs = [GATHER_BIG[n] for n in names]
    rs = small.shape[0]

    def body(*refs):
        ins, s_ref = refs[:nb], refs[nb]
        outs, os_ref = refs[nb + 1:2 * nb + 1], refs[2 * nb + 1]
        ici_send, ici_recv, d2d_send, d2d_recv, loc_sems = refs[2 * nb + 2:2 * nb + 7]
        vbufs = refs[2 * nb + 7:]
        x, y, c = _mesh_pos()
        k = 2 * x + y
        chips = _other_chips(x, y)
        sib = (x, y, 1 - c)

        def remote(src, dst, ssem, rsem, to):
            return pltpu.make_async_remote_copy(src_ref=src, dst_ref=dst, send_sem=ssem, recv_sem=rsem, device_id=to,
                                                device_id_type=MESH)

        stage = [pltpu.make_async_copy(ins[t], vbufs[t], loc_sems.at[2 * t]) for t in range(nb)]
        stage.append(pltpu.make_async_copy(s_ref, vbufs[nb], loc_sems.at[2 * nb]))
        for cp in stage:
            cp.start()
        sends = []
        for t, (_, src, dst) in enumerate(descs):
            for j, (px, py) in enumerate(chips):
                sends.append(remote(src(ins[t], c), dst(outs[t], k, c), ici_send.at[3 * t + j], ici_recv.at[3 * t + j], (px, py, c)))
        for j, (px, py) in enumerate(chips):
            sends.append(remote(s_ref, os_ref.at[k], ici_send.at[3 * nb + j], ici_recv.at[3 * nb + j], (px, py, c)))
        for cp in sends:
            cp.start()
        for cp in stage:
            cp.wait()
        local = []
        for t, (_, src, dst) in enumerate(descs):
            for cc in (0, 1):
                local.append(pltpu.make_async_copy(src(vbufs[t], cc), dst(outs[t], k, cc), loc_sems.at[2 * t + cc]))
        local.append(pltpu.make_async_copy(vbufs[nb], os_ref.at[k], loc_sems.at[2 * nb]))
        for cp in local:
            cp.start()
        for t, (_, src, dst) in enumerate(descs):
            for j, (px, py) in enumerate(chips):
                got = dst(outs[t], 2 * px + py, c)
                remote(got, got, ici_send.at[3 * t + j], ici_recv.at[3 * t + j], (px, py, c)).wait_recv()
                fwd = remote(got, got, d2d_send.at[3 * t + j], d2d_recv.at[3 * t + j], sib)
                fwd.start()
                sends.append(fwd)
        for j, (px, py) in enumerate(chips):
            remote(s_ref, os_ref.at[2 * px + py], ici_send.at[3 * nb + j], ici_recv.at[3 * nb + j], (px, py, c)).wait_recv()
        for t, (_, src, dst) in enumerate(descs):
            for j, (px, py) in enumerate(chips):
                theirs = dst(outs[t], 2 * px + py, 1 - c)
                remote(theirs, theirs, d2d_send.at[3 * t + j], d2d_recv.at[3 * t + j], sib).wait_recv()
        for cp in sends:
            cp.wait_send()
        for cp in local:
            cp.wait()

    out_shape = [jax.ShapeDtypeStruct(d[0], BF16) for d in descs] + [jax.ShapeDtypeStruct((N_CHIPS, rs, LANES), small.dtype)]
    return pl.pallas_call(
        body, name="gather_weights", in_specs=[ANY] * (nb + 1), out_specs=[ANY] * (nb + 1), out_shape=out_shape,
        scratch_shapes=[pltpu.SemaphoreType.DMA((3 * nb + 3,)), pltpu.SemaphoreType.DMA((3 * nb + 3,)),
                        pltpu.SemaphoreType.DMA((3 * nb,)), pltpu.SemaphoreType.DMA((3 * nb,)),
                        pltpu.SemaphoreType.DMA((2 * nb + 1,))]
        + [pltpu.VMEM(b.shape, b.dtype) for b in big] + [pltpu.VMEM(small.shape, small.dtype)],
        compiler_params=pltpu.CompilerParams(vmem_limit_bytes=VMEM_LIMIT),
    )(*big, small)


def _place_own(names, big):
    nb = len(big)
    descs = [GATHER_BIG[n] for n in names]

    def body(*refs):
        ins, outs = refs[:nb], refs[nb:2 * nb]
        sems, vbufs = refs[2 * nb], refs[2 * nb + 1:]
        x, y, c = _mesh_pos()
        k = 2 * x + y
        stage = [pltpu.make_async_copy(ins[t], vbufs[t], sems.at[2 * t]) for t in range(nb)]
        for cp in stage:
            cp.start()
        for cp in stage:
            cp.wait()
        local = [pltpu.make_async_copy(src(vbufs[t], cc), dst(outs[t], k, cc), sems.at[2 * t + cc])
                 for t, (_, src, dst) in enumerate(descs) for cc in (0, 1)]
        for cp in local:
            cp.start()
        for cp in local:
            cp.wait()

    return pl.pallas_call(
        body, name="place_own", in_specs=[ANY] * nb, out_specs=[ANY] * nb,
        out_shape=[jax.ShapeDtypeStruct(d[0], BF16) for d in descs],
        scratch_shapes=[pltpu.SemaphoreType.DMA((2 * nb,))] + [pltpu.VMEM(b.shape, b.dtype) for b in big],
        compiler_params=pltpu.CompilerParams(vmem_limit_bytes=VMEM_LIMIT),
    )(*big)


def _gather_start(names, big, bufs):
    nb = len(big)
    descs = [GATHER_BIG[n] for n in names]

    def body(*refs):
        ins, lnd = refs[:nb], refs[nb:2 * nb]
        send_sems, recv_sems, token = refs[2 * nb], refs[2 * nb + 1], refs[-1]
        x, y, c = _mesh_pos()
        k = 2 * x + y
        for t, (_, src, dst) in enumerate(descs):
            for j, (px, py) in enumerate(_other_chips(x, y)):
                _remote(src(ins[t], c), dst(lnd[t], k, c), send_sems.at[3 * t + j], recv_sems.at[3 * t + j], (px, py, c)).start()
        token[...] = jnp.zeros_like(token)

    out = pl.pallas_call(
        body, name="gather_start",
        out_shape=(pltpu.SemaphoreType.DMA((3 * nb,)), pltpu.SemaphoreType.DMA((3 * nb,)),
                   *[pltpu.HBM(b.shape, b.dtype) for b in big], *[pltpu.HBM(b.shape, b.dtype) for b in bufs],
                   jax.ShapeDtypeStruct((SUBLANES, LANES), F32)),
        in_specs=[HBM] * (2 * nb), out_specs=(SEM, SEM, *[HBM] * (2 * nb), pl.BlockSpec(memory_space=pltpu.VMEM)),
        input_output_aliases={i: 2 + i for i in range(2 * nb)},
        compiler_params=pltpu.CompilerParams(has_side_effects=DATAFLOW),
    )(*[pltpu.with_memory_space_constraint(b, pltpu.HBM) for b in big], *[pltpu.with_memory_space_constraint(b, pltpu.HBM) for b in bufs])
    return out[0], out[1], list(out[2:2 + nb]), list(out[2 + nb:2 + 2 * nb]), out[-1]


def _gather_wait(names, send_sems, recv_sems, big, bufs, after):
    nb = len(big)
    descs = [GATHER_BIG[n] for n in names]

    def body(*refs):
        ins, lnd = refs[:nb], refs[nb:2 * nb]
        ssem, rsem = refs[2 * nb], refs[2 * nb + 1]
        x, y, c = _mesh_pos()
        for t, (_, src, dst) in enumerate(descs):
            for j, (px, py) in enumerate(_other_chips(x, y)):
                cp = _remote(src(ins[t], c), dst(lnd[t], 2 * px + py, c), ssem.at[3 * t + j], rsem.at[3 * t + j], (px, py, c))
                cp.wait_send()
                cp.wait_recv()

    out = pl.pallas_call(
        body, name="gather_wait",
        out_shape=(*[pltpu.HBM(b.shape, b.dtype) for b in big], *[pltpu.HBM(b.shape, b.dtype) for b in bufs]),
        in_specs=[HBM] * (2 * nb) + [SEM, SEM, ANY], out_specs=tuple([HBM] * (2 * nb)),
        input_output_aliases={i: i for i in range(2 * nb)},
        compiler_params=pltpu.CompilerParams(has_side_effects=DATAFLOW),
    )(*big, *bufs, send_sems, recv_sems, after)
    return list(out[nb:])


def _gather_forward(names, bufs):
    nb = len(bufs)
    descs = [GATHER_BIG[n] for n in names]

    def body(*refs):
        outs = refs[nb:2 * nb]
        send_sems, recv_sems = refs[2 * nb:]
        x, y, c = _mesh_pos()
        sib = (x, y, 1 - c)
        sends = []
        for t, (_, src, dst) in enumerate(descs):
            for j, (px, py) in enumerate(_other_chips(x, y)):
                got = dst(outs[t], 2 * px + py, c)
                sends.append(_remote(got, got, send_sems.at[3 * t + j], recv_sems.at[3 * t + j], sib))
        for cp in sends:
            cp.start()
        for t, (_, src, dst) in enumerate(descs):
            for j, (px, py) in enumerate(_other_chips(x, y)):
                theirs = dst(outs[t], 2 * px + py, 1 - c)
                _remote(theirs, theirs, send_sems.at[3 * t + j], recv_sems.at[3 * t + j], sib).wait_recv()
        for cp in sends:
            cp.wait_send()

    return pl.pallas_call(
        body, name="gather_forward", in_specs=[ANY] * nb, out_specs=[ANY] * nb,
        out_shape=[jax.ShapeDtypeStruct(b.shape, b.dtype) for b in bufs], input_output_aliases={t: t for t in range(nb)},
        scratch_shapes=[pltpu.SemaphoreType.DMA((3 * nb,)), pltpu.SemaphoreType.DMA((3 * nb,))],
    )(*bufs)


def _remote(src, dst, ssem, rsem, to):
    return pltpu.make_async_remote_copy(src_ref=src, dst_ref=dst, send_sem=ssem, recv_sem=rsem, device_id=to, device_id_type=MESH)


def _rs_send_sibling(gs, tag):
    n = len(gs)
    counts = [N_CHIPS if g.ndim == 4 else 1 for g in gs]
    ns = sum(counts)

    def body(*refs):
        ins, outs = refs[:n], refs[n:2 * n]
        send_sems, recv_sems = refs[2 * n:]
        x, y, c = _mesh_pos()
        cps, s = [], 0
        for t in range(n):
            if counts[t] == 1:
                cps.append(_remote(ins[t].at[1 - c], outs[t], send_sems.at[s], recv_sems.at[s], (x, y, 1 - c)))
                s += 1
            else:
                for k in range(N_CHIPS):
                    cps.append(_remote(ins[t].at[k, 1 - c], outs[t].at[k], send_sems.at[s], recv_sems.at[s], (x, y, 1 - c)))
                    s += 1
        for cp in cps:
            cp.start()
        for cp in cps:
            cp.wait()

    out_shape = [jax.ShapeDtypeStruct(g.shape[:1] + g.shape[2:] if g.ndim == 4 else g.shape[1:], g.dtype) for g in gs]
    return pl.pallas_call(
        body, name=f"rs_send_sibling_{tag}", in_specs=[ANY] * n, out_specs=[ANY] * n, out_shape=out_shape,
        scratch_shapes=[pltpu.SemaphoreType.DMA((ns,)), pltpu.SemaphoreType.DMA((ns,))],
    )(*gs)


def _add_piece(g, recv, c, name):
    P, Q = g.shape[-2:]

    def body(c_ref, g_ref, r_ref, o_ref):
        o_ref[...] = g_ref[...].reshape(o_ref.shape) + r_ref[...]

    if g.ndim == 4:
        grid = (N_CHIPS,)
        in_specs = [pl.BlockSpec((1, 1, P, Q), lambda k, c_ref: (k, c_ref[0], 0, 0)), pl.BlockSpec((1, P, Q), lambda k, c_ref: (k, 0, 0))]
        out_spec = pl.BlockSpec((1, P, Q), lambda k, c_ref: (k, 0, 0))
    else:
        grid = (1,)
        in_specs = [pl.BlockSpec((1, P, Q), lambda k, c_ref: (c_ref[0], 0, 0)), pl.BlockSpec((P, Q), lambda k, c_ref: (0, 0))]
        out_spec = pl.BlockSpec((P, Q), lambda k, c_ref: (0, 0))
    return pl.pallas_call(
        body, name=name,
        grid_spec=pltpu.PrefetchScalarGridSpec(num_scalar_prefetch=1, grid=grid, in_specs=in_specs, out_specs=out_spec),
        out_shape=jax.ShapeDtypeStruct(recv.shape, g.dtype),
        compiler_params=_cp(1),
    )(c, g, recv)


HBM = pl.BlockSpec(memory_space=pltpu.HBM)
SEM = pl.BlockSpec(memory_space=pltpu.SEMAPHORE)
DATAFLOW = pltpu.SideEffectType.DATAFLOW_SIDE_EFFECTING


def _sibling_copies(gs, srcs, dsts, send_sems, recv_sems):
    x, y, c = _mesh_pos()
    cps, s = [], 0
    for t, g in enumerate(gs):
        if g.ndim == 4:
            for k in range(N_CHIPS):
                cps.append(_remote(srcs[t].at[k, 1 - c], dsts[t].at[k], send_sems.at[s], recv_sems.at[s], (x, y, 1 - c)))
                s += 1
        else:
            cps.append(_remote(srcs[t].at[1 - c], dsts[t], send_sems.at[s], recv_sems.at[s], (x, y, 1 - c)))
            s += 1
    return cps


def _sibling_start(gs, tag):
    n = len(gs)
    ns = sum(N_CHIPS if g.ndim == 4 else 1 for g in gs)
    lands = [pltpu.with_memory_space_constraint(lax.empty(g.shape[:1] + g.shape[2:] if g.ndim == 4 else g.shape[1:], g.dtype), pltpu.HBM)
             for g in gs]

    def body(*refs):
        for cp in _sibling_copies(gs, refs[:n], refs[n:2 * n], refs[2 * n], refs[2 * n + 1]):
            cp.start()
        refs[-1][...] = jnp.zeros_like(refs[-1])

    out = pl.pallas_call(
        body, name=f"sibling_start_{tag}",
        out_shape=(pltpu.SemaphoreType.DMA((ns,)), pltpu.SemaphoreType.DMA((ns,)),
                   *[pltpu.HBM(g.shape, g.dtype) for g in gs], *[pltpu.HBM(l.shape, l.dtype) for l in lands],
                   jax.ShapeDtypeStruct((SUBLANES, LANES), F32)),
        in_specs=[HBM] * (2 * n), out_specs=(SEM, SEM, *[HBM] * (2 * n), pl.BlockSpec(memory_space=pltpu.VMEM)),
        input_output_aliases={i: 2 + i for i in range(2 * n)},
        compiler_params=pltpu.CompilerParams(has_side_effects=DATAFLOW),
    )(*[pltpu.with_memory_space_constraint(g, pltpu.HBM) for g in gs], *lands)
    return (out[0], out[1], list(out[2:2 + n]), list(out[2 + n:2 + 2 * n])), out[-1]


def _sibling_wait(send_sems, recv_sems, gs, lands, after, tag):
    n = len(gs)

    def body(*refs):
        for cp in _sibling_copies(gs, refs[:n], refs[n:2 * n], refs[2 * n], refs[2 * n + 1]):
            cp.wait_send()
            cp.wait_recv()

    out = pl.pallas_call(
        body, name=f"sibling_wait_{tag}",
        out_shape=(*[pltpu.HBM(g.shape, g.dtype) for g in gs], *[pltpu.HBM(l.shape, l.dtype) for l in lands]),
        in_specs=[HBM] * (2 * n) + [SEM, SEM, ANY], out_specs=tuple([HBM] * (2 * n)),
        input_output_aliases={i: i for i in range(2 * n)},
        compiler_params=pltpu.CompilerParams(has_side_effects=DATAFLOW),
    )(*gs, *lands, send_sems, recv_sems, after)
    return list(out[:n]), list(out[n:])


def _chips_start(hs, tag):
    n = len(hs)
    lands = [pltpu.with_memory_space_constraint(lax.empty((N_CHIPS,) + h.shape[-2:], h.dtype), pltpu.HBM) for h in hs]

    def body(*refs):
        ins, lnd = refs[:n], refs[n:2 * n]
        send_sems, recv_sems, token = refs[2 * n], refs[2 * n + 1], refs[-1]
        x, y, c = _mesh_pos()
        k = 2 * x + y
        piece = lambda t, kk: ins[t].at[kk] if hs[t].ndim == 3 else ins[t]
        for t in range(n):
            for j, (px, py) in enumerate(_other_chips(x, y)):
                _remote(piece(t, 2 * px + py), lnd[t].at[k], send_sems.at[3 * t + j], recv_sems.at[3 * t + j], (px, py, c)).start()
        token[...] = jnp.zeros_like(token)

    out = pl.pallas_call(
        body, name=f"chips_start_{tag}",
        out_shape=(pltpu.SemaphoreType.DMA((3 * n,)), pltpu.SemaphoreType.DMA((3 * n,)),
                   *[pltpu.HBM(h.shape, h.dtype) for h in hs], *[pltpu.HBM(l.shape, l.dtype) for l in lands],
                   jax.ShapeDtypeStruct((SUBLANES, LANES), F32)),
        in_specs=[HBM] * (2 * n), out_specs=(SEM, SEM, *[HBM] * (2 * n), pl.BlockSpec(memory_space=pltpu.VMEM)),
        input_output_aliases={i: 2 + i for i in range(2 * n)},
        compiler_params=pltpu.CompilerParams(has_side_effects=DATAFLOW),
    )(*[pltpu.with_memory_space_constraint(h, pltpu.HBM) for h in hs], *lands)
    return out[0], out[1], list(out[2:2 + n]), list(out[2 + n:2 + 2 * n]), out[-1]


def _chips_wait(send_sems, recv_sems, hs, lands, after, tag):
    n = len(hs)

    def body(*refs):
        ins, lnd = refs[:n], refs[n:2 * n]
        ssem, rsem = refs[2 * n], refs[2 * n + 1]
        x, y, c = _mesh_pos()
        k = 2 * x + y
        piece = lambda t, kk: ins[t].at[kk] if hs[t].ndim == 3 else ins[t]
        for t in range(n):
            for j, (px, py) in enumerate(_other_chips(x, y)):
                cp = _remote(piece(t, k), lnd[t].at[2 * px + py], ssem.at[3 * t + j], rsem.at[3 * t + j], (px, py, c))
                cp.wait_send()
                cp.wait_recv()

    out = pl.pallas_call(
        body, name=f"chips_wait_{tag}",
        out_shape=(*[pltpu.HBM(h.shape, h.dtype) for h in hs], *[pltpu.HBM(l.shape, l.dtype) for l in lands]),
        in_specs=[HBM] * (2 * n) + [SEM, SEM, ANY], out_specs=tuple([HBM] * (2 * n)),
        input_output_aliases={i: i for i in range(2 * n)},
        compiler_params=pltpu.CompilerParams(has_side_effects=DATAFLOW),
    )(*hs, *lands, send_sems, recv_sems, after)
    return list(out[:n]), list(out[n:])


def _add_chips(p, own, kc, name):
    _, P, Q = p.shape
    tr = P
    while N_CHIPS * tr * Q * 4 > 6 * 1024 * 1024 and tr % 16 == 0:
        tr //= 2
    sharded = own.ndim == 3

    def body(kc_ref, p_ref, own_ref, o_ref):
        k = kc_ref[0]
        mine = own_ref[...].reshape(tr, Q)
        v = [jnp.where(k == j, mine, p_ref[j]) for j in range(N_CHIPS)]
        o_ref[0] = ((v[0] + v[1]) + v[2]) + v[3]

    own_spec = (pl.BlockSpec((1, tr, Q), lambda i, kc_ref: (kc_ref[0], i, 0)) if sharded
                else pl.BlockSpec((tr, Q), lambda i, kc_ref: (i, 0)))
    return pl.pallas_call(
        body, name=name,
        grid_spec=pltpu.PrefetchScalarGridSpec(
            num_scalar_prefetch=1, grid=(P // tr,),
            in_specs=[pl.BlockSpec((N_CHIPS, tr, Q), lambda i, kc_ref: (0, i, 0)), own_spec],
            out_specs=pl.BlockSpec((1, tr, Q), lambda i, kc_ref: (kc_ref[1], i, 0))),
        out_shape=jax.ShapeDtypeStruct((2, P, Q), p.dtype),
        compiler_params=_cp(1),
    )(kc, p, own)


def _rs_share(fs, tag):
    n = len(fs)

    def body(*refs):
        outs = refs[n:2 * n]
        send_sems, recv_sems = refs[2 * n:]
        x, y, c = _mesh_pos()
        sends = [_remote(outs[t].at[c], outs[t].at[c], send_sems.at[t], recv_sems.at[t], (x, y, 1 - c)) for t in range(n)]
        for cp in sends:
            cp.start()
        for t in range(n):
            _remote(outs[t].at[c], outs[t].at[1 - c], send_sems.at[t], recv_sems.at[t], (x, y, 1 - c)).wait_recv()
        for cp in sends:
            cp.wait_send()

    return pl.pallas_call(
        body, name=f"rs_share_{tag}", in_specs=[ANY] * n, out_specs=[ANY] * n,
        out_shape=[jax.ShapeDtypeStruct(f.shape, f.dtype) for f in fs], input_output_aliases={t: t for t in range(n)},
        scratch_shapes=[pltpu.SemaphoreType.DMA((n,)), pltpu.SemaphoreType.DMA((n,))],
    )(*fs)


def _reduce_start(gs, kc, tag):
    return _reduce_continue(gs, _rs_send_sibling(gs, tag), kc, tag)


def _reduce_continue(gs, from_sibling, kc, tag):
    chip_sums = [_add_piece(g, r, kc[1:], name=f"add_piece_{tag}_{t}") for t, (g, r) in enumerate(zip(gs, from_sibling))]
    send_sems, recv_sems, chip_sums, lands, token = _chips_start(chip_sums, tag)
    return (send_sems, recv_sems, chip_sums, lands, tag), token


def _reduce_finish(states, kc, after):
    mine = []
    for send_sems, recv_sems, chip_sums, lands, tag in states:
        chip_sums, from_chips = _chips_wait(send_sems, recv_sems, chip_sums, lands, after, tag)
        mine += [_add_chips(p, h, kc, name=f"add_chips_{tag}_{t}") for t, (p, h) in enumerate(zip(from_chips, chip_sums))]
    return _rs_share(mine, "all")


def _adamw(w, g, m, v, name):
    R, C = w.shape
    tr = R
    for cand in (512, 256, 128, 64, 32, 16, 8):
        if R % cand == 0 and cand * C * 4 <= 2 * 1024 * 1024:
            tr = cand
            break
    c1 = 1.0 / (1.0 - ADAM_B1 ** ADAM_STEP)
    c2 = 1.0 / (1.0 - ADAM_B2 ** ADAM_STEP)

    def body(w_ref, g_ref, m_ref, v_ref, d_ref, mo_ref, vo_ref):
        gv = g_ref[...]
        mn = ADAM_B1 * m_ref[...] + (1.0 - ADAM_B1) * gv
        vn = ADAM_B2 * v_ref[...] + (1.0 - ADAM_B2) * (gv * gv)
        mo_ref[...] = mn
        vo_ref[...] = vn
        d_ref[...] = -ADAM_LR * ((mn * c1) / (jnp.sqrt(vn * c2) + ADAM_EPS) + ADAM_WD * w_ref[...])

    spec = pl.BlockSpec((tr, C), lambda i: (i, 0))
    shp = jax.ShapeDtypeStruct((R, C), F32)
    return pl.pallas_call(body, name=name, grid=(R // tr,), in_specs=[spec] * 4, out_specs=[spec] * 3, out_shape=[shp] * 3,
                          compiler_params=_cp(1))(w, g, m, v)


def _adamw_many(ws, gs, ms, vs):
    n = len(ws)
    c1 = 1.0 / (1.0 - ADAM_B1 ** ADAM_STEP)
    c2 = 1.0 / (1.0 - ADAM_B2 ** ADAM_STEP)

    def body(*refs):
        w_refs, g_refs, m_refs, v_refs = (refs[k * n:(k + 1) * n] for k in range(4))
        outs = refs[4 * n:]
        for i in range(n):
            gv = g_refs[i][...]
            mn = ADAM_B1 * m_refs[i][...] + (1.0 - ADAM_B1) * gv
            vn = ADAM_B2 * v_refs[i][...] + (1.0 - ADAM_B2) * (gv * gv)
            outs[3 * i][...] = -ADAM_LR * ((mn * c1) / (jnp.sqrt(vn * c2) + ADAM_EPS) + ADAM_WD * w_refs[i][...])
            outs[3 * i + 1][...] = mn
            outs[3 * i + 2][...] = vn

    out_shape = [jax.ShapeDtypeStruct(w.shape, F32) for w in ws for _ in range(3)]
    return pl.pallas_call(body, name="adamw_small", out_shape=out_shape)(*ws, *gs, *ms, *vs)


def _step(a):
    x, y, c = _mesh_pos()
    kc = jnp.stack([2 * x + y, c]).astype(jnp.int32)

    rs = _rows(sum(a[n].size for n in SH_SMALL))
    first, later = ["ev_w_in", "ev_w_out"], ["od_w_in", "od_w_out", "ffn_w_up", "ffn_w_down"]
    lead = lambda w: w if w.ndim == 3 else w[None]
    *full, gs = _gather_weights(first, [a[n].astype(BF16) for n in first], _pack([a[n] for n in SH_SMALL], rs, F32))
    p = {n: a[n] for n in REP}
    p.update({n: lead(w) for n, w in zip(first, full)})
    parts = [_unpack(gs[k], [a[n].shape for n in SH_SMALL]) for k in range(N_CHIPS)]
    for i, n in enumerate(SH_SMALL):
        p[n] = jnp.concatenate([parts[k][i] for k in range(N_CHIPS)], axis=SH_SMALL[n])
    shards = lax.optimization_barrier(([a[n].astype(BF16) for n in later], full))[0]
    g_send, g_recv, shards, bufs, token = _gather_start(later, shards, _place_own(later, shards))
    p["norm_mix"] = p["norm_mix"] + token[0:1, 0:1]

    def late(after):
        got = _gather_forward(later, _gather_wait(later, g_send, g_recv, shards, bufs, after))
        return {n: lead(w) for n, w in zip(later, got)}

    p["late"] = late

    states, pending = {}, {}

    def start_reduce(tag, gs, split=False):
        if split:
            pending[tag], token = _sibling_start(gs, tag)
        else:
            states[tag], token = _reduce_start(gs, kc, tag)
        return token

    def continue_reduce(tag, after):
        send_sems, recv_sems, gs, lands = pending.pop(tag)
        gs, from_sibling = _sibling_wait(send_sems, recv_sems, gs, lands, after, tag)
        states[tag], token = _reduce_continue(gs, from_sibling, kc, tag)
        return token

    sq8, grad_x, grads, big = _local_step(a["x"][0], a["loss_target"][0], p, start_reduce, continue_reduce)
    loss = lax.psum(0.5 / D_MODEL * jnp.sum(sq8), ("x", "y", "c"))

    r_s = _rows(sum(a[n].size for n in SH_SMALL), 2 * SUBLANES) // 2
    small_pieces = []
    for k in range(N_CHIPS):
        pieces = [lax.slice_in_dim(grads[n], k * a[n].shape[ax], (k + 1) * a[n].shape[ax], axis=ax) for n, ax in SH_SMALL.items()]
        small_pieces.append(_pack(pieces, 2 * r_s, F32).reshape(2, r_s, LANES))
    g_small = jnp.stack(small_pieces)
    r_r = _rows(sum(a[n].size for n in REP), 2 * SUBLANES) // 2
    g_rep = _pack([grads[n] for n in REP], 2 * r_r, F32).reshape(2, r_r, LANES)
    start_reduce("g4", [g_small, g_rep])
    reduced = _reduce_finish([states[tag] for tag in ("g1", "g2", "g3", "g4")], kc, grad_x)
    red = dict(zip([key for tag in ("g1", "g2", "g3") for key in REDUCE_GROUPS[tag]], reduced))
    gfin = {}
    for n in ("ev_w_in", "ev_w_out", "od_w_in", "od_w_out"):
        gfin[n] = red[n, 0].reshape(a[n].shape)
    for n in ("ffn_w_up", "ffn_w_down"):
        gfin[n] = jnp.stack([red[n, l].reshape(a[n].shape[1:]) for l in range(2)])
    gfin.update(zip(SH_SMALL, _unpack(reduced[-2], [a[n].shape for n in SH_SMALL])))
    gfin.update(zip(REP, _unpack(reduced[-1], [a[n].shape for n in REP])))

    out = {"loss": loss, "grad_x": grad_x[None]}
    small_names = list(SH_SMALL) + REP
    for n in SH_BIG:
        shp = a[n].shape
        two_d = lambda t: t.reshape(-1, shp[-1])
        d, mo, vo = _adamw(two_d(a[n]), two_d(gfin[n]), two_d(a["m_" + n]), two_d(a["v_" + n]), name=f"adamw_{n}")
        out["delta_" + n], out["new_m_" + n], out["new_v_" + n] = d.reshape(shp), mo.reshape(shp), vo.reshape(shp)
    two_d = lambda t: t.reshape(-1, t.shape[-1])
    res = _adamw_many(*[[two_d(src(n)) for n in small_names]
                        for src in (lambda n: a[n], lambda n: gfin[n], lambda n: a["m_" + n], lambda n: a["v_" + n])])
    for i, n in enumerate(small_names):
        out["delta_" + n], out["new_m_" + n], out["new_v_" + n] = (r.reshape(a[n].shape) for r in res[3 * i:3 * i + 3])
    for n in WEIGHTS:
        out["grad_" + n] = gfin[n]
    return out


def kernel(x, norm_mix, norm_ffn, norm_final, ev_w_in, ev_conv_w, ev_conv_b, ev_gate_a_w, ev_gate_a_b, ev_gate_x_w, ev_gate_x_b, ev_lru_lambda, hg_lb_logits, ev_hg_norm, ev_w_out, od_w_in, od_b_in, od_ln_g, od_ln_b, od_w_s, od_b_s, od_w_out, ffn_w_up, ffn_conv_w, ffn_conv_b, ffn_w_down, loss_target, m_norm_mix, m_norm_ffn, m_norm_final, m_ev_w_in, m_ev_conv_w, m_ev_conv_b, m_ev_gate_a_w, m_ev_gate_a_b, m_ev_gate_x_w, m_ev_gate_x_b, m_ev_lru_lambda, m_hg_lb_logits, m_ev_hg_norm, m_ev_w_out, m_od_w_in, m_od_b_in, m_od_ln_g, m_od_ln_b, m_od_w_s, m_od_b_s, m_od_w_out, m_ffn_w_up, m_ffn_conv_w, m_ffn_conv_b, m_ffn_w_down, v_norm_mix, v_norm_ffn, v_norm_final, v_ev_w_in, v_ev_conv_w, v_ev_conv_b, v_ev_gate_a_w, v_ev_gate_a_b, v_ev_gate_x_w, v_ev_gate_x_b, v_ev_lru_lambda, v_hg_lb_logits, v_ev_hg_norm, v_ev_w_out, v_od_w_in, v_od_b_in, v_od_ln_g, v_od_ln_b, v_od_w_s, v_od_b_s, v_od_w_out, v_ffn_w_up, v_ffn_conv_w, v_ffn_conv_b, v_ffn_w_down):
    vals = (x, norm_mix, norm_ffn, norm_final, ev_w_in, ev_conv_w, ev_conv_b, ev_gate_a_w, ev_gate_a_b, ev_gate_x_w, ev_gate_x_b, ev_lru_lambda, hg_lb_logits, ev_hg_norm, ev_w_out, od_w_in, od_b_in, od_ln_g, od_ln_b, od_w_s, od_b_s, od_w_out, ffn_w_up, ffn_conv_w, ffn_conv_b, ffn_w_down, loss_target, m_norm_mix, m_norm_ffn, m_norm_final, m_ev_w_in, m_ev_conv_w, m_ev_conv_b, m_ev_gate_a_w, m_ev_gate_a_b, m_ev_gate_x_w, m_ev_gate_x_b, m_ev_lru_lambda, m_hg_lb_logits, m_ev_hg_norm, m_ev_w_out, m_od_w_in, m_od_b_in, m_od_ln_g, m_od_ln_b, m_od_w_s, m_od_b_s, m_od_w_out, m_ffn_w_up, m_ffn_conv_w, m_ffn_conv_b, m_ffn_w_down, v_norm_mix, v_norm_ffn, v_norm_final, v_ev_w_in, v_ev_conv_w, v_ev_conv_b, v_ev_gate_a_w, v_ev_gate_a_b, v_ev_gate_x_w, v_ev_gate_x_b, v_ev_lru_lambda, v_hg_lb_logits, v_ev_hg_norm, v_ev_w_out, v_od_w_in, v_od_b_in, v_od_ln_g, v_od_ln_b, v_od_w_s, v_od_b_s, v_od_w_out, v_ffn_w_up, v_ffn_conv_w, v_ffn_conv_b, v_ffn_w_down)
    names = ["x"] + WEIGHTS + ["loss_target"] + ["m_" + n for n in WEIGHTS] + ["v_" + n for n in WEIGHTS]
    out = _step(dict(zip(names, vals)))
    return (out["loss"], out["grad_x"], *[out["grad_" + n] for n in WEIGHTS], *[out["delta_" + n] for n in WEIGHTS],
            *[out["new_m_" + n] for n in WEIGHTS], *[out["new_v_" + n] for n in WEIGHTS])
```

```python
import functools

import jax
import jax.numpy as jnp
from jax import lax
from jax.experimental import pallas as pl
from jax.experimental.pallas import tpu as pltpu

F32 = jnp.float32
BF16 = jnp.bfloat16

EPS = 1e-6
D_MODEL = 1024
LRU_W = 512
LRU_BLOCKS = 8
LRU_C = 8.0
HG_HEADS = 4
HG_D = 128
HG_CHUNK = 64
SGU_G = 8
SGU_CHUNK = 128
D_FF = 2816
ADAM_LR, ADAM_B1, ADAM_B2, ADAM_EPS, ADAM_WD, ADAM_STEP = 0.001, 0.9, 0.999, 1e-08, 0.01, 10

V7X_VMEM_BYTES = 64 * 1024 * 1024
VMEM_LIMIT = V7X_VMEM_BYTES - 8 * 1024 * 1024
SUBLANES = 8
LANES = 128
BF16_ROWS = 16

GELU_C0 = 0.7978845608028654
GELU_C1 = 0.044715

NN = (((1,), (0,)), ((), ()))
NT = (((1,), (1,)), ((), ()))
TN = (((0,), (0,)), ((), ()))


def _dot(a, b, dims=NN):
    return lax.dot_general(a.astype(BF16), b.astype(BF16), dims, preferred_element_type=F32)


def _cp(n_grid):
    return pltpu.CompilerParams(dimension_semantics=("arbitrary",) * n_grid, vmem_limit_bytes=VMEM_LIMIT)


def _chunk(n, cap):
    best = LANES
    for c in range(LANES, cap + 1, LANES):
        if n % c == 0:
            best = c
    return best


def _resident(shape):
    nd = len(shape)
    return pl.BlockSpec(shape, lambda *_: (0,) * nd, pipeline_mode=pl.Buffered(1))


def _rsum8(x):
    r, c = x.shape
    return x.reshape(r // SUBLANES, SUBLANES, c).sum(axis=0)


def _sigmoid(x):
    return 0.5 * jnp.tanh(0.5 * x) + 0.5


def _gelu(x):
    return 0.5 * x * (1.0 + jnp.tanh(GELU_C0 * (x + GELU_C1 * x * x * x)))


def _gelu_grad(x):
    t = jnp.tanh(GELU_C0 * (x + GELU_C1 * x * x * x))
    return 0.5 * (1.0 + t) + 0.5 * x * (1.0 - t * t) * GELU_C0 * (1.0 + 3.0 * GELU_C1 * x * x)


def _silu_and_grad(x):
    s = _sigmoid(x)
    return x * s, s * (1.0 + x * (1.0 - s))


def _shift_rows(e, j):
    n = e.shape[0]
    return e if j % n == 0 else pltpu.roll(e, j % n, 0)


def _weight(w):
    if isinstance(w, tuple):
        stack, layer = w
        K, N = stack.shape[1:]
        return stack, pl.BlockSpec((None, K, N), lambda *_: (layer, 0, 0), pipeline_mode=pl.Buffered(1)), (K, N)
    return w, _resident(w.shape), w.shape


def _norm_mm(h, g, w, b, name, tt=1024):
    T, D = h.shape
    w, w_spec, (_, N) = _weight(w)
    cn = _chunk(N, 512)

    def body(h_ref, g_ref, w_ref, b_ref, hn_ref, z_ref):
        x = h_ref[...]
        r = lax.rsqrt(jnp.mean(x * x, axis=-1, keepdims=True) + EPS)
        hn = (x * r * g_ref[...]).astype(BF16)
        hn_ref[...] = hn
        for j in range(0, N, cn):
            acc = jnp.dot(hn, w_ref[:, j:j + cn], preferred_element_type=F32) + b_ref[:, j:j + cn]
            z_ref[:, j:j + cn] = acc.astype(BF16)

    return pl.pallas_call(
        body, name=name, grid=(T // tt,),
        in_specs=[pl.BlockSpec((tt, D), lambda i: (i, 0)), _resident((1, D)), w_spec, _resident((1, N))],
        out_specs=[pl.BlockSpec((tt, D), lambda i: (i, 0)), pl.BlockSpec((tt, N), lambda i: (i, 0))],
        out_shape=[jax.ShapeDtypeStruct((T, D), BF16), jax.ShapeDtypeStruct((T, N), BF16)],
        compiler_params=_cp(1),
    )(h, g, w, b)


def _mm(a, w, res, out_dtype, name, tt=1024, transpose_w=False):
    T, K = a.shape
    w, w_spec, w_shape = _weight(w)
    N = w_shape[0] if transpose_w else w_shape[1]
    cn = _chunk(N, 512)
    has_res = res is not None

    def body(*refs):
        a_ref, w_ref = refs[0], refs[1]
        res_ref = refs[2] if has_res else None
        o_ref = refs[-1]
        av = a_ref[...].astype(BF16)
        for j in range(0, N, cn):
            if transpose_w:
                acc = lax.dot_general(av, w_ref[j:j + cn, :], NT, preferred_element_type=F32)
            else:
                acc = jnp.dot(av, w_ref[:, j:j + cn], preferred_element_type=F32)
            if has_res:
                acc = acc + res_ref[:, j:j + cn]
            o_ref[:, j:j + cn] = acc.astype(out_dtype)

    in_specs = [pl.BlockSpec((tt, K), lambda i: (i, 0)), w_spec]
    args = [a, w]
    if has_res:
        in_specs.append(pl.BlockSpec((tt, N), lambda i: (i, 0)))
        args.append(res)
    return pl.pallas_call(
        body, name=name, grid=(T // tt,), in_specs=in_specs,
        out_specs=pl.BlockSpec((tt, N), lambda i: (i, 0)),
        out_shape=jax.ShapeDtypeStruct((T, N), out_dtype),
        compiler_params=_cp(1),
    )(*args)


def _mm_tn(a, b, name, col_shards=1, tt=2048):
    T, K = a.shape
    N = b.shape[1]
    ns = N // col_shards
    tt = min(tt, T)
    while 2 * (tt * K * a.dtype.itemsize + tt * ns * b.dtype.itemsize + K * ns * 4) > VMEM_LIMIT - 12 * 1024 * 1024:
        tt //= 2

    def body(a_ref, b_ref, o_ref):
        acc = lax.dot_general(a_ref[...].astype(BF16), b_ref[...].astype(BF16), TN, preferred_element_type=F32)
        prev = jnp.where(pl.program_id(1) == 0, 0.0, o_ref[0])
        o_ref[0] = prev + acc

    out = pl.pallas_call(
        body, name=name, grid=(col_shards, T // tt),
        in_specs=[pl.BlockSpec((tt, K), lambda n, t: (t, 0)), pl.BlockSpec((tt, ns), lambda n, t: (t, n))],
        out_specs=pl.BlockSpec((1, K, ns), lambda n, t: (n, 0, 0)),
        out_shape=jax.ShapeDtypeStruct((col_shards, K, ns), F32),
        compiler_params=_cp(2),
    )(a, b)
    return out if col_shards > 1 else out[0]


def _mm_normbwd(dz, w, x, g, dres, name):
    T, N = dz.shape
    D = w.shape[0]
    tt = 1024 if N <= 3072 else 512

    def body(dz_ref, wt_ref, x_ref, g_ref, dres_ref, dx_ref, dg_ref):
        @pl.when(pl.program_id(0) == 0)
        def _():
            dg_ref[...] = jnp.zeros_like(dg_ref)

        dy = lax.dot_general(dz_ref[...], wt_ref[...], NT, preferred_element_type=F32)
        x = x_ref[...]
        r = lax.rsqrt(jnp.mean(x * x, axis=-1, keepdims=True) + EPS)
        xn = x * r
        dg_ref[...] += _rsum8(dy * xn)
        dxn = dy * g_ref[...]
        dx_ref[...] = dres_ref[...] + r * (dxn - xn * jnp.mean(dxn * xn, axis=-1, keepdims=True))

    return pl.pallas_call(
        body, name=name, grid=(T // tt,),
        in_specs=[pl.BlockSpec((tt, N), lambda i: (i, 0)), _resident((D, N)), pl.BlockSpec((tt, D), lambda i: (i, 0)),
                  _resident((1, D)), pl.BlockSpec((tt, D), lambda i: (i, 0))],
        out_specs=[pl.BlockSpec((tt, D), lambda i: (i, 0)), pl.BlockSpec((SUBLANES, D), lambda i: (0, 0))],
        out_shape=[jax.ShapeDtypeStruct((T, D), F32), jax.ShapeDtypeStruct((SUBLANES, D), F32)],
        compiler_params=_cp(1),
    )(dz, w, x, g, dres)


def _final_loss(h, g, tgt, name="final_loss", tt=512):
    T, D = h.shape

    def body(h_ref, g_ref, t_ref, dh_ref, dg_ref, sq_ref):
        @pl.when(pl.program_id(0) == 0)
        def _():
            dg_ref[...] = jnp.zeros_like(dg_ref)
            sq_ref[...] = jnp.zeros_like(sq_ref)

        x = h_ref[...]
        r = lax.rsqrt(jnp.mean(x * x, axis=-1, keepdims=True) + EPS)
        xn = x * r
        gv = g_ref[...]
        diff = xn * gv - t_ref[...]
        sq_ref[...] += _rsum8(diff * diff)
        dy = diff * (1.0 / D)
        dg_ref[...] += _rsum8(dy * xn)
        dxn = dy * gv
        dh_ref[...] = r * (dxn - xn * jnp.mean(dxn * xn, axis=-1, keepdims=True))

    return pl.pallas_call(
        body, name=name, grid=(T // tt,),
        in_specs=[pl.BlockSpec((tt, D), lambda i: (i, 0)), _resident((1, D)), pl.BlockSpec((tt, D), lambda i: (i, 0))],
        out_specs=[pl.BlockSpec((tt, D), lambda i: (i, 0)), pl.BlockSpec((SUBLANES, D), lambda i: (0, 0)),
                   pl.BlockSpec((SUBLANES, D), lambda i: (0, 0))],
        out_shape=[jax.ShapeDtypeStruct((T, D), F32), jax.ShapeDtypeStruct((SUBLANES, D), F32),
                   jax.ShapeDtypeStruct((SUBLANES, D), F32)],
        compiler_params=_cp(1),
    )(h, g, tgt)


def _col_groups(F, cc, per_group=4):
    step = cc * per_group
    return [(g0, min(g0 + step, F)) for g0 in range(0, F, step)]


def _ffn_act(gu, cw, cb, w_down, res, name, tt=512):
    T = gu.shape[0]
    F = gu.shape[1] // 2
    w_down, wd_spec, (_, D) = _weight(w_down)
    cc = _chunk(F, 256)
    hb = tt // BF16_ROWS

    def body(gu_ref, halo_ref, cw_ref, cb_ref, wd_ref, res_ref, a_ref, gc_ref, o_ref):
        first = pl.program_id(0) == 0
        acc = res_ref[...]
        for g0, g1 in _col_groups(F, cc):
            for c0 in range(g0, g1, cc):
                cs = slice(c0, c0 + cc)
                x = gu_ref[:, cs].astype(F32)
                halo = jnp.where(first, 0.0, halo_ref[:, cs].astype(F32))
                e = jnp.concatenate([halo, x], axis=0)
                gc = (cb_ref[:, cs] + cw_ref[0:1, cs] * _shift_rows(e, 2)[BF16_ROWS:] + cw_ref[1:2, cs] * _shift_rows(e, 1)[BF16_ROWS:]
                      + cw_ref[2:3, cs] * x)
                up = gu_ref[:, F + c0:F + c0 + cc].astype(F32)
                gc_ref[:, cs] = gc.astype(BF16)
                a_ref[:, cs] = (gc * _sigmoid(gc) * up).astype(BF16)
            acc = acc + jnp.dot(a_ref[:, g0:g1], wd_ref[g0:g1, :], preferred_element_type=F32)
        o_ref[...] = acc

    return pl.pallas_call(
        body, name=name, grid=(T // tt,),
        in_specs=[pl.BlockSpec((tt, 2 * F), lambda i: (i, 0)),
                  pl.BlockSpec((BF16_ROWS, F), lambda i: (jnp.maximum(i * hb - 1, 0), 0)),
                  _resident((SUBLANES, F)), _resident((1, F)), wd_spec, pl.BlockSpec((tt, D), lambda i: (i, 0))],
        out_specs=[pl.BlockSpec((tt, F), lambda i: (i, 0)), pl.BlockSpec((tt, F), lambda i: (i, 0)),
                   pl.BlockSpec((tt, D), lambda i: (i, 0))],
        out_shape=[jax.ShapeDtypeStruct((T, F), BF16), jax.ShapeDtypeStruct((T, F), BF16), jax.ShapeDtypeStruct((T, D), F32)],
        compiler_params=_cp(1),
    )(gu, gu, cw, cb, w_down, res)


def _ffn_act_bwd(gu, gc, cw, w_up, w_down, x, g, dh, name, tt=256):
    T = gu.shape[0]
    F = gu.shape[1] // 2
    w_up, wu_spec, (D, _) = _weight(w_up)
    w_down, wd_spec, _ = _weight(w_down)
    cc = _chunk(F, 256)
    hb = tt // BF16_ROWS
    last_hb = T // BF16_ROWS - 1
    nt = T // tt

    def body(gu_ref, upnext_ref, gc_ref, gcnext_ref, cw_ref, wu_ref, wd_ref, x_ref, g_ref, dh_ref, dhnext_ref,
             dgu_ref, dc_ref, dx_ref, dg_ref):
        i = pl.program_id(0)

        @pl.when(i == 0)
        def _():
            dc_ref[...] = jnp.zeros_like(dc_ref)
            dg_ref[...] = jnp.zeros_like(dg_ref)

        n = tt + BF16_ROWS
        ext = lambda main, nxt: jnp.concatenate([main.astype(F32), nxt.astype(F32)], axis=0)
        dhe = ext(dh_ref[...], jnp.where(i == nt - 1, 0.0, dhnext_ref[...])).astype(BF16)
        dy = jnp.zeros((tt, D), F32)
        groups = _col_groups(F, cc)
        da_of = lambda grp: lax.dot_general(dhe, wd_ref[grp[0]:grp[1], :], NT, preferred_element_type=F32)
        da_next = da_of(groups[0])
        for gi, (lo, hi) in enumerate(groups):
            da, da_next = da_next, (da_of(groups[gi + 1]) if gi + 1 < len(groups) else None)
            for c0 in range(lo, hi, cc):
                cs = slice(c0, c0 + cc)
                us = slice(F + c0, F + c0 + cc)
                gc = ext(gc_ref[:, cs], gcnext_ref[:, cs])
                up = ext(gu_ref[:, us], upnext_ref[:, cs])
                dae = da[:, c0 - lo:c0 - lo + cc]
                s, ds = _silu_and_grad(gc)
                dgc = dae * up * ds
                dgu_ref[:, us] = (dae * s)[:tt].astype(BF16)
                dgc1 = _shift_rows(dgc, n - 1)[:tt]
                dgc2 = _shift_rows(dgc, n - 2)[:tt]
                dm = dgc[:tt]
                dgu_ref[:, cs] = (cw_ref[2:3, cs] * dm + cw_ref[1:2, cs] * dgc1 + cw_ref[0:1, cs] * dgc2).astype(BF16)
                gt = gu_ref[:, cs].astype(F32)
                dc_ref[0, :, cs] += _rsum8(dgc2 * gt)
                dc_ref[1, :, cs] += _rsum8(dgc1 * gt)
                dc_ref[2, :, cs] += _rsum8(dm * gt)
                dc_ref[3, :, cs] += _rsum8(dm)
            dy = (dy + lax.dot_general(dgu_ref[:, lo:hi], wu_ref[:, lo:hi], NT, preferred_element_type=F32)
                  + lax.dot_general(dgu_ref[:, F + lo:F + hi], wu_ref[:, F + lo:F + hi], NT, preferred_element_type=F32))
        xv = x_ref[...]
        r = lax.rsqrt(jnp.mean(xv * xv, axis=-1, keepdims=True) + EPS)
        xn = xv * r
        dg_ref[...] += _rsum8(dy * xn)
        dxn = dy * g_ref[...]
        dx_ref[...] = dh_ref[...] + r * (dxn - xn * jnp.mean(dxn * xn, axis=-1, keepdims=True))

    tok = lambda w: pl.BlockSpec((tt, w), lambda i: (i, 0))
    nxt = lambda w, col: pl.BlockSpec((BF16_ROWS, w), lambda i: (jnp.minimum((i + 1) * hb, last_hb), col))
    return pl.pallas_call(
        body, name=name, grid=(nt,),
        in_specs=[tok(2 * F), nxt(F, 1), tok(F), nxt(F, 0),
                  _resident((SUBLANES, F)), wu_spec, wd_spec, tok(D), _resident((1, D)), tok(D), nxt(D, 0)],
        out_specs=[tok(2 * F), pl.BlockSpec((4, SUBLANES, F), lambda i: (0, 0, 0)), tok(D),
                   pl.BlockSpec((SUBLANES, D), lambda i: (0, 0))],
        out_shape=[jax.ShapeDtypeStruct((T, 2 * F), BF16), jax.ShapeDtypeStruct((4, SUBLANES, F), F32),
                   jax.ShapeDtypeStruct((T, D), F32), jax.ShapeDtypeStruct((SUBLANES, D), F32)],
        compiler_params=_cp(1),
    )(gu, gu, gc, gc, cw, w_up, w_down, x, g, dh, dh)


def _softplus_neg(lam):
    x = -lam
    y = jnp.exp(-jnp.abs(x))
    l1p = jnp.where(y < 0.01, y * (1.0 - y * (0.5 - y * (1.0 / 3.0))), jnp.log(1.0 + y))
    return jnp.maximum(x, 0.0) + l1p


def _lru_gates(xc, wa_ref, ba_ref, wx_ref, bx_ref, sp):
    xcb = xc.astype(BF16)
    r = _sigmoid(jnp.dot(xcb, wa_ref[...], preferred_element_type=F32) + ba_ref[...])
    gi = _sigmoid(jnp.dot(xcb, wx_ref[...], preferred_element_type=F32) + bx_ref[...])
    log_a = -LRU_C * r * sp
    a = jnp.exp(log_a)
    x2 = 2.0 * log_a
    series = -x2 * (1.0 + x2 * 0.5 * (1.0 + x2 * (1.0 / 3.0)))
    om = jnp.where(x2 > -0.02, series, 1.0 - a * a)
    return r, gi, a, jnp.sqrt(om)


def _lru_conv(xr, halo, cw_ref, cb_ref):
    e = jnp.concatenate([halo, xr], axis=0)
    x1 = _shift_rows(e, 1)[BF16_ROWS:]
    x2 = _shift_rows(e, 2)[BF16_ROWS:]
    x3 = _shift_rows(e, 3)[BF16_ROWS:]
    xc = cb_ref[...] + cw_ref[0:1, :] * x3 + cw_ref[1:2, :] * x2 + cw_ref[2:3, :] * x1 + cw_ref[3:4, :] * xr
    return xc, x1, x2, x3


def _lru_fwd(z, cw, cb, wa, ba, wx, bx, lam, name="lru_fwd", tt=256):
    T = z.shape[0]
    W = LRU_W
    hb = tt // BF16_ROWS
    ng = tt // SUBLANES

    def body(z_ref, halo_ref, cw_ref, cb_ref, wa_ref, ba_ref, wx_ref, bx_ref, lam_ref, oa_ref, h_ref, a_s, sv_ref, u_s, hc):
        i = pl.program_id(0)

        @pl.when(i == 0)
        def _():
            hc[...] = jnp.zeros_like(hc)

        xr = z_ref[:, W:2 * W].astype(F32)
        halo = jnp.where(i == 0, 0.0, halo_ref[...].astype(F32))
        xc, _, _, _ = _lru_conv(xr, halo, cw_ref, cb_ref)
        sp = _softplus_neg(lam_ref[...])
        r, gi, a, mult = _lru_gates(xc, wa_ref, ba_ref, wx_ref, bx_ref, sp)
        a_s[...] = a
        u_s[...] = mult * gi * xc
        for k, saved in enumerate((mult, r, gi, xc)):
            sv_ref[:, k * W:(k + 1) * W] = saved.astype(BF16)
        row = lax.broadcasted_iota(jnp.int32, (SUBLANES, W), 0)

        def step(j, hprev):
            r0 = pl.multiple_of(j * SUBLANES, SUBLANES)
            A = a_s[pl.ds(r0, SUBLANES), :]
            U = u_s[pl.ds(r0, SUBLANES), :]
            for k in (1, 2, 4):
                m = row >= k
                U = jnp.where(m, A * pltpu.roll(U, k, 0) + U, U)
                A = jnp.where(m, A * pltpu.roll(A, k, 0), A)
            H = U + A * hprev
            h_ref[pl.ds(r0, SUBLANES), :] = H
            return jnp.broadcast_to(H[SUBLANES - 1:SUBLANES, :], (SUBLANES, W))

        hc[...] = lax.fori_loop(0, ng, step, hc[...])
        oa_ref[...] = (_gelu(z_ref[:, 0:W].astype(F32)) * h_ref[...]).astype(BF16)

    return pl.pallas_call(
        body, name=name, grid=(T // tt,),
        in_specs=[pl.BlockSpec((tt, 2 * W), lambda i: (i, 0)),
                  pl.BlockSpec((BF16_ROWS, W), lambda i: (jnp.maximum(i * hb - 1, 0), 1)),
                  _resident((SUBLANES, W)), _resident((1, W)), _resident((W, W)), _resident((1, W)),
                  _resident((W, W)), _resident((1, W)), _resident((1, W))],
        out_specs=[pl.BlockSpec((tt, W), lambda i: (i, 0)), pl.BlockSpec((tt, W), lambda i: (i, 0)),
                   pl.BlockSpec((tt, W), lambda i: (i, 0)), pl.BlockSpec((tt, 4 * W), lambda i: (i, 0))],
        out_shape=[jax.ShapeDtypeStruct((T, W), BF16), jax.ShapeDtypeStruct((T, W), F32),
                   jax.ShapeDtypeStruct((T, W), F32), jax.ShapeDtypeStruct((T, 4 * W), BF16)],
        scratch_shapes=[pltpu.VMEM((tt, W), F32), pltpu.VMEM((SUBLANES, W), F32)],
        compiler_params=_cp(1),
    )(z, z, cw, cb, wa, ba, wx, bx, lam)


def _lru_bwd(z, hseq, a_all, saved, dmix, cw, wat, wxt, lam, name="lru_bwd", tt=256):
    T = z.shape[0]
    W = LRU_W
    nt = T // tt
    sb = tt // SUBLANES
    ng = tt // SUBLANES

    def body(z_ref, h_ref, hprev_ref, a_ref, sv_ref, dm_ref, cw_ref, wat_ref, wxt_ref, lam_ref,
             dz_ref, dc_ref, dwa_ref, dwx_ref, dv_ref, c_s, d_s, g_s, gc, an, dxn):
        i = pl.program_id(0)
        ti = nt - 1 - i

        @pl.when(i == 0)
        def _():
            dc_ref[...] = jnp.zeros_like(dc_ref)
            dwa_ref[...] = jnp.zeros_like(dwa_ref)
            dwx_ref[...] = jnp.zeros_like(dwx_ref)
            dv_ref[...] = jnp.zeros_like(dv_ref)
            gc[...] = jnp.zeros_like(gc)
            an[...] = jnp.zeros_like(an)
            dxn[...] = jnp.zeros_like(dxn)

        xr = z_ref[:, W:2 * W].astype(F32)
        yg = z_ref[:, 0:W].astype(F32)
        sp = _softplus_neg(lam_ref[...])
        a = a_ref[...]
        mult, r, gi, xc = (sv_ref[:, k * W:(k + 1) * W].astype(F32) for k in range(4))
        h = h_ref[...]
        hp = jnp.where(ti == 0, 0.0, hprev_ref[...])
        hm1 = _shift_rows(jnp.concatenate([hp, h], axis=0), 1)[SUBLANES:]
        dout = dm_ref[...].astype(F32)
        d_s[...] = dout * _gelu(yg)
        dz_ref[:, 0:W] = (dout * h * _gelu_grad(yg)).astype(BF16)
        c_s[...] = _shift_rows(jnp.concatenate([a, an[...]], axis=0), tt + SUBLANES - 1)[:tt]
        an[...] = a[0:SUBLANES, :]
        row = lax.broadcasted_iota(jnp.int32, (SUBLANES, W), 0)

        def step(j, gnext):
            r0 = pl.multiple_of((ng - 1 - j) * SUBLANES, SUBLANES)
            C = c_s[pl.ds(r0, SUBLANES), :]
            G = d_s[pl.ds(r0, SUBLANES), :]
            for k in (1, 2, 4):
                m = row < SUBLANES - k
                G = jnp.where(m, G + C * pltpu.roll(G, SUBLANES - k, 0), G)
                C = jnp.where(m, C * pltpu.roll(C, SUBLANES - k, 0), C)
            G = G + C * gnext
            g_s[pl.ds(r0, SUBLANES), :] = G
            return jnp.broadcast_to(G[0:1, :], (SUBLANES, W))

        gc[...] = lax.fori_loop(0, ng, step, gc[...])
        du = g_s[...]
        da = du * hm1
        dgi = du * mult * xc
        dxc = du * mult * gi
        dmult = du * gi * xc
        dlog_a = da * a - dmult * (a * a) / mult
        dr = dlog_a * (-LRU_C * sp)
        dv_ref[2] += _rsum8(dlog_a * (-LRU_C * r))
        dpr = (dr * r * (1.0 - r)).astype(BF16)
        dpi = (dgi * gi * (1.0 - gi)).astype(BF16)
        dv_ref[0] += _rsum8(dpr.astype(F32))
        dv_ref[1] += _rsum8(dpi.astype(F32))
        xcb = sv_ref[:, 3 * W:4 * W]
        dwa_ref[...] += lax.dot_general(xcb, dpr, TN, preferred_element_type=F32)
        dwx_ref[...] += lax.dot_general(xcb, dpi, TN, preferred_element_type=F32)
        dxc = dxc + jnp.dot(dpr, wat_ref[...], preferred_element_type=F32) + jnp.dot(dpi, wxt_ref[...], preferred_element_type=F32)
        n = tt + BF16_ROWS
        de = jnp.concatenate([dxc, dxn[...]], axis=0)
        d1, d2, d3 = (_shift_rows(de, n - j)[:tt] for j in (1, 2, 3))
        dxn[...] = dxc[0:BF16_ROWS, :]
        dz_ref[:, W:2 * W] = (cw_ref[3:4, :] * dxc + cw_ref[2:3, :] * d1 + cw_ref[1:2, :] * d2 + cw_ref[0:1, :] * d3).astype(BF16)
        dc_ref[0] += _rsum8(d3 * xr)
        dc_ref[1] += _rsum8(d2 * xr)
        dc_ref[2] += _rsum8(d1 * xr)
        dc_ref[3] += _rsum8(dxc * xr)
        dc_ref[4] += _rsum8(dxc)

    rev = lambda i: nt - 1 - i
    tok = lambda w: pl.BlockSpec((tt, w), lambda i: (rev(i), 0))
    return pl.pallas_call(
        body, name=name, grid=(nt,),
        in_specs=[tok(2 * W), tok(W),
                  pl.BlockSpec((SUBLANES, W), lambda i: (jnp.maximum(rev(i) * sb - 1, 0), 0)),
                  tok(W), tok(4 * W), tok(W),
                  _resident((SUBLANES, W)), _resident((W, W)), _resident((W, W)), _resident((1, W))],
        out_specs=[pl.BlockSpec((tt, 2 * W), lambda i: (rev(i), 0)),
                   pl.BlockSpec((5, SUBLANES, W), lambda i: (0, 0, 0)),
                   pl.BlockSpec((W, W), lambda i: (0, 0)), pl.BlockSpec((W, W), lambda i: (0, 0)),
                   pl.BlockSpec((3, SUBLANES, W), lambda i: (0, 0, 0))],
        out_shape=[jax.ShapeDtypeStruct((T, z.shape[1]), BF16), jax.ShapeDtypeStruct((5, SUBLANES, W), F32),
                   jax.ShapeDtypeStruct((W, W), F32), jax.ShapeDtypeStruct((W, W), F32),
                   jax.ShapeDtypeStruct((3, SUBLANES, W), F32)],
        scratch_shapes=[pltpu.VMEM((tt, W), F32), pltpu.VMEM((tt, W), F32), pltpu.VMEM((tt, W), F32),
                        pltpu.VMEM((SUBLANES, W), F32), pltpu.VMEM((SUBLANES, W), F32), pltpu.VMEM((BF16_ROWS, W), F32)],
        compiler_params=_cp(1),
    )(z, hseq, hseq, a_all, saved, dmix, cw, wat, wxt, lam)


def _split3(x):
    hi = x.astype(BF16)
    r1 = x - hi.astype(F32)
    mid = r1.astype(BF16)
    lo = (r1 - mid.astype(F32)).astype(BF16)
    return hi, mid, lo


def _tri_matmul(tri, x):
    hi, mid, lo = _split3(x)
    return (jnp.dot(tri, hi, preferred_element_type=F32) + jnp.dot(tri, mid, preferred_element_type=F32)
            + jnp.dot(tri, lo, preferred_element_type=F32))


def _hg_chunk(q, fl, lb):
    C = q.shape[0]
    ri = lax.broadcasted_iota(jnp.int32, (C, C), 0)
    ci = lax.broadcasted_iota(jnp.int32, (C, C), 1)
    causal = ri >= ci
    sig = _sigmoid(fl)
    f = lb + (1.0 - lb) * sig
    k = 1.0 - f
    sq = _sigmoid(q)
    qf = q * sq
    b = _tri_matmul(causal.astype(BF16), jnp.log(f))
    bm = b[C // 2 - 1:C // 2, :]
    bl = b[C - 1:C, :]
    e_qt, e_kt, e_in, e_out = jnp.exp(b - bm), jnp.exp(bm - b), jnp.exp(b), jnp.exp(bl - b)
    qt = qf * e_qt
    kt = k * e_kt
    qin = qf * e_in
    kout = k * e_out
    qtb, ktb = qt.astype(BF16), kt.astype(BF16)
    att = [jnp.where(causal, _dot(qtb[:, _head(h)], ktb[:, _head(h)], NT), 0.0).astype(BF16) for h in range(HG_HEADS)]
    return dict(sig=sig, f=f, k=k, sq=sq, qf=qf, b=b, bm=bm, bl=bl, qt=qt, kt=kt, qin=qin, kout=kout, att=att,
                causal=causal, anti=ri <= ci, decay=jnp.exp(bl), e_qt=e_qt, e_kt=e_kt, e_in=e_in, e_out=e_out)


def _head(h):
    return slice(h * HG_D, (h + 1) * HG_D)


def _hgrn_fwd(z, lb, gn, name="hgrn_fwd", tt=256):
    T = z.shape[0]
    C = HG_CHUNK
    nc = tt // C
    Dh = HG_D
    Wd = HG_HEADS * Dh

    def body(q_ref, f_ref, v_ref, g_ref, lb_ref, gn_ref, o_ref, ss_ref, st):
        @pl.when(pl.program_id(0) == 0)
        def _():
            st[...] = jnp.zeros_like(st)

        S = [st[h] for h in range(HG_HEADS)]
        for c in range(nc):
            rows = slice(c * C, (c + 1) * C)
            ck = _hg_chunk(q_ref[rows, :].astype(F32), f_ref[rows, :].astype(F32), lb_ref[...])
            v = v_ref[rows, :]
            g = g_ref[rows, :].astype(F32)
            H = range(HG_HEADS)
            qinb, koutb = ck["qin"].astype(BF16), ck["kout"].astype(BF16)
            for h in H:
                ss_ref[h, c] = S[h]
            o = [_dot(ck["att"][h], v[:, _head(h)]) + _dot(qinb[:, _head(h)], S[h], NT) for h in H]
            S = [ck["decay"][:, _head(h)] * S[h] + _dot(v[:, _head(h)], koutb[:, _head(h)], TN) for h in H]
            outs = [o[h] * lax.rsqrt(jnp.mean(o[h] * o[h], axis=-1, keepdims=True) + EPS) * gn_ref[...] for h in H]
            o_ref[rows, :] = (jnp.concatenate(outs, axis=1) * (g * _sigmoid(g))).astype(BF16)
        for h in range(HG_HEADS):
            st[h] = S[h]

    col = lambda base: (lambda i: (i, base))
    return pl.pallas_call(
        body, name=name, grid=(T // tt,),
        in_specs=[pl.BlockSpec((tt, Wd), col(2)), pl.BlockSpec((tt, Wd), col(3)), pl.BlockSpec((tt, Wd), col(4)),
                  pl.BlockSpec((tt, Wd), col(5)), _resident((1, Wd)), _resident((1, Dh))],
        out_specs=[pl.BlockSpec((tt, Wd), lambda i: (i, 0)),
                   pl.BlockSpec((HG_HEADS, nc, Dh, Dh), lambda i: (0, i, 0, 0))],
        out_shape=[jax.ShapeDtypeStruct((T, Wd), BF16),
                   jax.ShapeDtypeStruct((HG_HEADS, T // C, Dh, Dh), F32)],
        scratch_shapes=[pltpu.VMEM((HG_HEADS, Dh, Dh), F32)],
        compiler_params=_cp(1),
    )(z, z, z, z, lb, gn)


def _hgrn_bwd(z, ss, dmix, lb, gn, dz, name="hgrn_bwd", tt=256):
    T = z.shape[0]
    C = HG_CHUNK
    nc = tt // C
    nt = T // tt
    Dh = HG_D
    Wd = HG_HEADS * Dh

    def body(q_ref, f_ref, v_ref, g_ref, ss_ref, dm_ref, lb_ref, gn_ref, dz01_ref, dz_ref, dlb_ref, dgn_ref, dst):
        @pl.when(pl.program_id(0) == 0)
        def _():
            dst[...] = jnp.zeros_like(dst)
            dlb_ref[...] = jnp.zeros_like(dlb_ref)
            dgn_ref[...] = jnp.zeros_like(dgn_ref)

        dS = [dst[h] for h in range(HG_HEADS)]
        lbv = lb_ref[...]
        gnv = gn_ref[...]
        rowc = lax.broadcasted_iota(jnp.int32, (C, Wd), 0)
        cat = lambda xs: jnp.concatenate(xs, axis=1)
        for c in reversed(range(nc)):
            rows = slice(c * C, (c + 1) * C)
            q = q_ref[rows, :].astype(F32)
            ck = _hg_chunk(q, f_ref[rows, :].astype(F32), lbv)
            v = v_ref[rows, :]
            g = g_ref[rows, :].astype(F32)
            dout = dm_ref[rows, :].astype(F32)
            sg, dsg = _silu_and_grad(g)
            d_ong = dout * sg
            H = range(HG_HEADS)
            qinb, koutb, qtb, ktb = (ck[n].astype(BF16) for n in ("qin", "kout", "qt", "kt"))
            S = [ss_ref[h, c] for h in H]
            Sb = [s.astype(BF16) for s in S]
            dSb = [d.astype(BF16) for d in dS]
            o = [_dot(ck["att"][h], v[:, _head(h)]) + _dot(qinb[:, _head(h)], Sb[h], NT) for h in H]
            rn = [lax.rsqrt(jnp.mean(o[h] * o[h], axis=-1, keepdims=True) + EPS) for h in H]
            on = [o[h] * rn[h] for h in H]
            don = [d_ong[:, _head(h)] * gnv for h in H]
            do = [(rn[h] * (don[h] - on[h] * jnp.mean(don[h] * on[h], axis=-1, keepdims=True))).astype(BF16) for h in H]
            datt = [jnp.where(ck["causal"], _dot(do[h], v[:, _head(h)], NT), 0.0).astype(BF16) for h in H]
            dvs = [_dot(ck["att"][h], do[h], TN) + _dot(koutb[:, _head(h)], dSb[h], NT) for h in H]
            dqins = [_dot(do[h], Sb[h]) for h in H]
            dkouts = [_dot(v[:, _head(h)], dSb[h]) for h in H]
            dqts = [_dot(datt[h], ktb[:, _head(h)]) for h in H]
            dkts = [_dot(datt[h], qtb[:, _head(h)], TN) for h in H]
            ddecays = [jnp.sum(dS[h] * S[h], axis=0, keepdims=True) for h in H]
            dS = [_dot(do[h], qinb[:, _head(h)], TN) + ck["decay"][:, _head(h)] * dS[h] for h in H]
            ons = [on[h] * gnv for h in H]
            dgn = _rsum8(d_ong[:, _head(0)] * on[0])
            for h in range(1, HG_HEADS):
                dgn = dgn + _rsum8(d_ong[:, _head(h)] * on[h])
            dgn_ref[...] += dgn
            dqt, dkt, dqin, dkout, ddecay = cat(dqts), cat(dkts), cat(dqins), cat(dkouts), cat(ddecays)
            dqf = dqt * ck["e_qt"] + dqin * ck["e_in"]
            dk = dkt * ck["e_kt"] + dkout * ck["e_out"]
            kk = dkout * ck["kout"]
            db = dqt * ck["qt"] - dkt * ck["kt"] + dqin * ck["qin"] - kk
            dbl = jnp.sum(kk, axis=0, keepdims=True) + ddecay * ck["decay"]
            db = db + jnp.where(rowc == C - 1, dbl, 0.0)
            dlogf = _tri_matmul(ck["anti"].astype(BF16), db)
            dfv = dlogf / ck["f"] - dk
            sig, sq = ck["sig"], ck["sq"]
            dlb_ref[...] += _rsum8(dfv * (1.0 - sig))
            dz_ref[rows, 2 * Wd:3 * Wd] = (dqf * (sq * (1.0 + q * (1.0 - sq)))).astype(BF16)
            dz_ref[rows, 3 * Wd:4 * Wd] = (dfv * (1.0 - lbv) * sig * (1.0 - sig)).astype(BF16)
            dz_ref[rows, 4 * Wd:5 * Wd] = cat(dvs).astype(BF16)
            dz_ref[rows, 5 * Wd:6 * Wd] = (dout * cat(ons) * dsg).astype(BF16)
        dz_ref[:, 0:2 * Wd] = dz01_ref[...]
        for h in range(HG_HEADS):
            dst[h] = dS[h]

    rev = lambda i: nt - 1 - i
    col = lambda base: (lambda i: (rev(i), base))
    return pl.pallas_call(
        body, name=name, grid=(nt,),
        in_specs=[pl.BlockSpec((tt, Wd), col(2)), pl.BlockSpec((tt, Wd), col(3)), pl.BlockSpec((tt, Wd), col(4)),
                  pl.BlockSpec((tt, Wd), col(5)),
                  pl.BlockSpec((HG_HEADS, nc, Dh, Dh), lambda i: (0, rev(i), 0, 0)),
                  pl.BlockSpec((tt, Wd), col(1)), _resident((1, Wd)), _resident((1, Dh)),
                  pl.BlockSpec((tt, 2 * Wd), col(0))],
        out_specs=[pl.BlockSpec((tt, 6 * Wd), lambda i: (rev(i), 0)), pl.BlockSpec((SUBLANES, Wd), lambda i: (0, 0)),
                   pl.BlockSpec((SUBLANES, Dh), lambda i: (0, 0))],
        out_shape=[jax.ShapeDtypeStruct((T, 6 * Wd), BF16), jax.ShapeDtypeStruct((SUBLANES, Wd), F32),
                   jax.ShapeDtypeStruct((SUBLANES, Dh), F32)],
        input_output_aliases={8: 0},
        scratch_shapes=[pltpu.VMEM((HG_HEADS, Dh, Dh), F32)],
        compiler_params=_cp(1),
    )(z, z, z, z, ss, dmix, lb, gn, dz)


def _sgu_core(p, lg_ref, lb_ref, wsc_ref, bsb_ref):
    Wd = D_MODEL
    G = SGU_CHUNK
    zz = _gelu(p)
    u = zz[:, :Wd]
    v = zz[:, Wd:]
    vc = v - jnp.mean(v, axis=-1, keepdims=True)
    rstd = lax.rsqrt(jnp.mean(vc * vc, axis=-1, keepdims=True) + EPS)
    vhat = vc * rstd
    vn = vhat * lg_ref[...] + lb_ref[...]
    svs = []
    for gi in range(SGU_G):
        svs.append(jnp.dot(wsc_ref[gi], vn[:, gi * G:(gi + 1) * G].astype(BF16), preferred_element_type=F32) + bsb_ref[gi])
    return u, vhat, rstd, vn, jnp.concatenate(svs, axis=1)


def _sgu_fwd(p1, lg, lbias, wsc, bsb, name="sgu_fwd", tt=512):
    T = p1.shape[0]
    Wd = D_MODEL
    C = SGU_CHUNK

    def body(p_ref, lg_ref, lb_ref, wsc_ref, bsb_ref, s_ref):
        for c in range(tt // C):
            rows = slice(c * C, (c + 1) * C)
            u, _, _, _, sv = _sgu_core(p_ref[rows, :].astype(F32), lg_ref, lb_ref, wsc_ref, bsb_ref)
            s_ref[rows, :] = (u * sv).astype(BF16)

    return pl.pallas_call(
        body, name=name, grid=(T // tt,),
        in_specs=[pl.BlockSpec((tt, 2 * Wd), lambda i: (i, 0)), _resident((1, Wd)), _resident((1, Wd)),
                  _resident((SGU_G, C, C)), _resident((SGU_G, C, C))],
        out_specs=pl.BlockSpec((tt, Wd), lambda i: (i, 0)),
        out_shape=jax.ShapeDtypeStruct((T, Wd), BF16),
        compiler_params=_cp(1),
    )(p1, lg, lbias, wsc, bsb)


def _sgu_bwd(p1, ds, lg, lbias, wsc, wsct, bsb, name="sgu_bwd", tt=512):
    T = p1.shape[0]
    Wd = D_MODEL
    C = SGU_CHUNK

    def body(p_ref, ds_ref, lg_ref, lb_ref, wsc_ref, wsct_ref, bsb_ref, dp_ref, dws_ref, dbs_ref, dlg_ref, dlb_ref, dbin_ref):
        @pl.when(pl.program_id(0) == 0)
        def _():
            dws_ref[...] = jnp.zeros_like(dws_ref)
            dbs_ref[...] = jnp.zeros_like(dbs_ref)
            dlg_ref[...] = jnp.zeros_like(dlg_ref)
            dlb_ref[...] = jnp.zeros_like(dlb_ref)
            dbin_ref[...] = jnp.zeros_like(dbin_ref)

        for c in range(tt // C):
            rows = slice(c * C, (c + 1) * C)
            p = p_ref[rows, :].astype(F32)
            u, vhat, rstd, vn, sv = _sgu_core(p, lg_ref, lb_ref, wsc_ref, bsb_ref)
            dsc = ds_ref[rows, :].astype(F32)
            du = dsc * sv
            dsv = dsc * u
            dvns = []
            for gi in range(SGU_G):
                cs = slice(gi * C, (gi + 1) * C)
                dsv_g = dsv[:, cs]
                dvns.append(jnp.dot(wsct_ref[gi], dsv_g.astype(BF16), preferred_element_type=F32))
                dws_ref[gi] += _dot(dsv_g, vn[:, cs], NT)
                dbs_ref[gi] += dsv_g
            dvn = jnp.concatenate(dvns, axis=1)
            dlg_ref[...] += _rsum8(dvn * vhat)
            dlb_ref[...] += _rsum8(dvn)
            dvh = dvn * lg_ref[...]
            dv = rstd * (dvh - jnp.mean(dvh, axis=-1, keepdims=True) - vhat * jnp.mean(dvh * vhat, axis=-1, keepdims=True))
            dp = jnp.concatenate([du, dv], axis=1) * _gelu_grad(p)
            dbin_ref[...] += _rsum8(dp)
            dp_ref[rows, :] = dp.astype(BF16)

    full3 = pl.BlockSpec((SGU_G, C, C), lambda i: (0, 0, 0))
    return pl.pallas_call(
        body, name=name, grid=(T // tt,),
        in_specs=[pl.BlockSpec((tt, 2 * Wd), lambda i: (i, 0)), pl.BlockSpec((tt, Wd), lambda i: (i, 0)),
                  _resident((1, Wd)), _resident((1, Wd)), _resident((SGU_G, C, C)), _resident((SGU_G, C, C)),
                  _resident((SGU_G, C, C))],
        out_specs=[pl.BlockSpec((tt, 2 * Wd), lambda i: (i, 0)), full3, full3,
                   pl.BlockSpec((SUBLANES, Wd), lambda i: (0, 0)), pl.BlockSpec((SUBLANES, Wd), lambda i: (0, 0)),
                   pl.BlockSpec((SUBLANES, 2 * Wd), lambda i: (0, 0))],
        out_shape=[jax.ShapeDtypeStruct((T, 2 * Wd), BF16), jax.ShapeDtypeStruct((SGU_G, C, C), F32),
                   jax.ShapeDtypeStruct((SGU_G, C, C), F32), jax.ShapeDtypeStruct((SUBLANES, Wd), F32),
                   jax.ShapeDtypeStruct((SUBLANES, Wd), F32), jax.ShapeDtypeStruct((SUBLANES, 2 * Wd), F32)],
        compiler_params=_cp(1),
    )(p1, ds, lg, lbias, wsc, wsct, bsb)


def _pad_rows(w, rows=SUBLANES):
    return jnp.pad(w, ((0, rows - w.shape[0]), (0, 0)))


def _block_diag(w):
    n, b, _ = w.shape
    return (w[:, :, None, :] * jnp.eye(n, dtype=w.dtype)[:, None, :, None]).reshape(n * b, n * b)


def _diag_blocks(m, n):
    b = m.shape[0] // n
    m4 = m.reshape(n, b, n, b)
    return jnp.stack([m4[k, :, k, :] for k in range(n)], axis=0)


def _piece_major(dw):
    if dw.ndim == 2:
        K, N = dw.shape
        return dw.reshape(N_CHIPS, 2, K // (2 * N_CHIPS), N)
    _, K, ns = dw.shape
    return dw.reshape(N_CHIPS, 2, K // 2, ns)


def _ffn_fwd(h, g, w_up, cw, cb, w_down, tag):
    hn, gu = _norm_mm(h, g, w_up, jnp.zeros((1, 2 * D_FF), F32), name=f"ffn_up_{tag}")
    a, gc, out = _ffn_act(gu, cw, cb, w_down, h, name=f"ffn_act_down_{tag}")
    return out, (hn, gu, gc, a)


def _ffn_bwd(dh, h, g, saved, w_up, cw, w_down, tag):
    hn, gu, gc, a = saved
    dwd = _mm_tn(a, dh, name=f"ffn_dwd_{tag}")
    dgu, dc, dhin, dg8 = _ffn_act_bwd(gu, gc, cw, w_up, w_down, h, g, dh, name=f"ffn_bwd_{tag}")
    dwu = _mm_tn(hn, dgu, name=f"ffn_dwu_{tag}", col_shards=N_CHIPS)
    dcs = dc.sum(axis=1)
    return dhin, dg8.sum(axis=0), dwu, dcs[0:3], dcs[3], dwd


REDUCE_GROUPS = {"g1": [("ffn_w_up", 1), ("ffn_w_down", 1), ("od_w_out", 0), ("od_w_in", 0)],
                 "g2": [("ffn_w_up", 0), ("ffn_w_down", 0)],
                 "g3": [("ev_w_out", 0), ("ev_w_in", 0)]}


def _local_step(x, tgt, p, start_reduce=None, continue_reduce=None):
    row = lambda v: v.reshape(1, -1)
    grads = {}

    lower = jax.nn.softmax(p["hg_lb_logits"], axis=0)
    lb0 = row(lower[0])
    ev_cw = _pad_rows(p["ev_conv_w"][0])
    ev_cb = row(p["ev_conv_b"][0])
    wa = _block_diag(p["ev_gate_a_w"][0]).astype(BF16)
    wx = _block_diag(p["ev_gate_x_w"][0]).astype(BF16)
    ba, bx, lam = row(p["ev_gate_a_b"][0]), row(p["ev_gate_x_b"][0]), row(p["ev_lru_lambda"][0])
    gn = row(p["ev_hg_norm"][0])
    tril = jnp.tril(jnp.ones((SGU_CHUNK, SGU_CHUNK), F32))
    wsc = (p["od_w_s"][0] * tril).astype(BF16)
    bsb = jnp.broadcast_to(p["od_b_s"][0][:, :, None], (SGU_G, SGU_CHUNK, SGU_CHUNK)).astype(F32)
    ffn_cw = [_pad_rows(p["ffn_conv_w"][l]) for l in range(2)]
    ffn_cb = [row(p["ffn_conv_b"][l]) for l in range(2)]
    ev_w_in, ev_w_out = p["ev_w_in"][0], p["ev_w_out"][0]
    nm = [row(p["norm_mix"][l]) for l in range(2)]
    nf = [row(p["norm_ffn"][l]) for l in range(2)]

    h0 = x
    hn0, z0 = _norm_mm(h0, nm[0], ev_w_in, jnp.zeros((1, ev_w_in.shape[1]), F32), name="ev_in")
    out_a, hseq, a_all, lru_saved = _lru_fwd(z0, ev_cw, ev_cb, wa, ba, wx, bx, lam)
    out_b, ss = _hgrn_fwd(z0, lb0, gn)
    mix0 = jnp.concatenate([out_a, out_b], axis=1)
    h1 = _mm(mix0, ev_w_out, h0, F32, name="ev_out")
    late = p["late"](h1) if "late" in p else p
    od_w_in, od_w_out = late["od_w_in"][0], late["od_w_out"][0]
    w_up = [(late["ffn_w_up"], l) for l in range(2)]
    w_down = [(late["ffn_w_down"], l) for l in range(2)]
    h2, ffn0 = _ffn_fwd(h1, nf[0], w_up[0], ffn_cw[0], ffn_cb[0], w_down[0], "l0")
    hn1, p1 = _norm_mm(h2, nm[1], od_w_in, row(p["od_b_in"][0]), name="od_in")
    s1 = _sgu_fwd(p1, row(p["od_ln_g"][0]), row(p["od_ln_b"][0]), wsc, bsb)
    h3 = _mm(s1, od_w_out, h2, F32, name="od_out")
    h4, ffn1 = _ffn_fwd(h3, nf[1], w_up[1], ffn_cw[1], ffn_cb[1], w_down[1], "l1")
    dh4, dgf8, sq8 = _final_loss(h4, row(p["norm_final"]), tgt)
    grads["norm_final"] = dgf8.sum(axis=0)

    big = {}
    dh3, dnf1, dwu1, dcw1, dcb1, dwd1 = _ffn_bwd(dh4, h3, nf[1], ffn1, w_up[1], ffn_cw[1], w_down[1], "l1")
    big["ffn_w_up", 1], big["ffn_w_down", 1] = _piece_major(dwu1), _piece_major(dwd1)
    ds1 = _mm(dh3, od_w_out, None, BF16, name="od_ds", transpose_w=True)
    big["od_w_out", 0] = _piece_major(_mm_tn(s1, dh3, name="od_dwo"))
    wsct = jnp.swapaxes(wsc, 1, 2)
    dp1, dws, dbs, dlg8, dlb8, dbin8 = _sgu_bwd(p1, ds1, row(p["od_ln_g"][0]), row(p["od_ln_b"][0]), wsc, wsct, bsb)
    grads["od_w_s"] = (dws * tril)[None]
    grads["od_b_s"] = dbs.sum(axis=-1)[None]
    grads["od_ln_g"] = dlg8.sum(axis=0)[None]
    grads["od_ln_b"] = dlb8.sum(axis=0)[None]
    grads["od_b_in"] = dbin8.sum(axis=0)[None]
    dh2, dnm1 = _mm_normbwd(dp1, od_w_in, h2, nm[1], dh3, name="od_dh")
    big["od_w_in", 0] = _piece_major(_mm_tn(hn1, dp1, name="od_dwi", col_shards=N_CHIPS))
    if start_reduce is not None:
        token = start_reduce("g1", [big[key] for key in REDUCE_GROUPS["g1"]], split=True)
        ffn_cw[0] = ffn_cw[0] + token[0:1, 0:1]

    dh1, dnf0, dwu0, dcw0, dcb0, dwd0 = _ffn_bwd(dh2, h1, nf[0], ffn0, w_up[0], ffn_cw[0], w_down[0], "l0")
    big["ffn_w_up", 0], big["ffn_w_down", 0] = _piece_major(dwu0), _piece_major(dwd0)
    if start_reduce is not None:
        token = continue_reduce("g1", dh1) + start_reduce("g2", [big[key] for key in REDUCE_GROUPS["g2"]], split=True)
        lam = lam + token[0:1, 0:1]
    dmix = _mm(dh1, ev_w_out, None, BF16, name="ev_dmix", transpose_w=True)
    big["ev_w_out", 0] = _piece_major(_mm_tn(mix0, dh1, name="ev_dwo"))
    dz01, dc5, dwa, dwx, dvec = _lru_bwd(z0, hseq, a_all, lru_saved, dmix, ev_cw, wa.T, wx.T, lam)
    if start_reduce is not None:
        lb0 = lb0 + continue_reduce("g2", dc5)[0:1, 0:1]
    dz0, dlb8, dgn8 = _hgrn_bwd(z0, ss, dmix, lb0, gn, dz01)
    big["ev_w_in", 0] = _piece_major(_mm_tn(hn0, dz0, name="ev_dwi", col_shards=N_CHIPS))
    if start_reduce is not None:
        token = start_reduce("g3", [big[key] for key in REDUCE_GROUPS["g3"]])
        nm[0] = nm[0] + token[0:1, 0:1]
    grad_x, dnm0 = _mm_normbwd(dz0, ev_w_in, h0, nm[0], dh1, name="ev_dh")

    dc5s = dc5.sum(axis=1)
    grads["ev_conv_w"] = dc5s[0:4][None]
    grads["ev_conv_b"] = dc5s[4][None]
    grads["ev_gate_a_w"] = _diag_blocks(dwa, LRU_BLOCKS)[None]
    grads["ev_gate_x_w"] = _diag_blocks(dwx, LRU_BLOCKS)[None]
    dvs = dvec.sum(axis=1)
    grads["ev_gate_a_b"] = dvs[0][None]
    grads["ev_gate_x_b"] = dvs[1][None]
    grads["ev_lru_lambda"] = (dvs[2] * (-jax.nn.sigmoid(-p["ev_lru_lambda"][0])))[None]
    dlb = dlb8.sum(axis=0)
    grads["hg_lb_logits"] = dlb[None, :] * lower[0][None, :] * (jnp.eye(3, dtype=F32)[0][:, None] - lower)
    grads["ev_hg_norm"] = dgn8.sum(axis=0)[None]
    grads["norm_mix"] = jnp.stack([dnm0.sum(axis=0), dnm1.sum(axis=0)])
    grads["norm_ffn"] = jnp.stack([dnf0, dnf1])
    grads["ffn_conv_w"] = jnp.stack([dcw0, dcw1])
    grads["ffn_conv_b"] = jnp.stack([dcb0, dcb1])
    return sq8, grad_x, grads, big


MESH = pl.DeviceIdType.MESH
ANY = pl.BlockSpec(memory_space=pl.ANY)
N_CHIPS = 4
N_DEV = 8

SH_BIG = {"ev_w_in": 2, "ev_w_out": 1, "od_w_in": 2, "od_w_out": 1, "ffn_w_up": 2, "ffn_w_down": 1}
SH_SMALL = {"ev_conv_w": 2, "od_b_in": 1, "od_ln_g": 1, "od_ln_b": 1, "ffn_conv_w": 2}
REP = ["norm_mix", "norm_ffn", "norm_final", "ev_conv_b", "ev_gate_a_w", "ev_gate_a_b", "ev_gate_x_w", "ev_gate_x_b",
       "ev_lru_lambda", "hg_lb_logits", "ev_hg_norm", "od_w_s", "od_b_s", "ffn_conv_b"]
WEIGHTS = ["norm_mix", "norm_ffn", "norm_final", "ev_w_in", "ev_conv_w", "ev_conv_b", "ev_gate_a_w", "ev_gate_a_b", "ev_gate_x_w",
           "ev_gate_x_b", "ev_lru_lambda", "hg_lb_logits", "ev_hg_norm", "ev_w_out", "od_w_in", "od_b_in", "od_ln_g", "od_ln_b",
           "od_w_s", "od_b_s", "od_w_out", "ffn_w_up", "ffn_conv_w", "ffn_conv_b", "ffn_w_down"]


def _rows(n_elems, mult=SUBLANES):
    r = -(-n_elems // LANES)
    return -(-r // mult) * mult


def _pack(arrs, rows, dtype):
    flat = jnp.concatenate([a.reshape(-1).astype(dtype) for a in arrs])
    return jnp.pad(flat, (0, rows * LANES - flat.shape[0])).reshape(rows, LANES)


def _unpack(flat2d, shapes):
    flat = flat2d.reshape(-1)
    out, off = [], 0
    for s in shapes:
        n = 1
        for d in s:
            n *= d
        out.append(flat[off:off + n].reshape(s))
        off += n
    return out


def _mesh_pos():
    return lax.axis_index("x"), lax.axis_index("y"), lax.axis_index("c")


def _other_chips(x, y):
    return [(1 - x, y), (x, 1 - y), (1 - x, 1 - y)]


def _half_rows(n):
    return lambda r, c: r.at[0, pl.ds(c * (n // 2), n // 2), :]


GATHER_BIG = {
    "ev_w_in": ((1024, 3072), _half_rows(1024), lambda o, k, c: o.at[pl.ds(c * 512, 512), pl.ds(k * 768, 768)]),
    "ev_w_out": ((1024, 1024), _half_rows(256), lambda o, k, c: o.at[pl.ds(k * 256 + c * 128, 128), :]),
    "od_w_in": ((1024, 2048), _half_rows(1024), lambda o, k, c: o.at[pl.ds(c * 512, 512), pl.ds(k * 512, 512)]),
    "od_w_out": ((1024, 1024), _half_rows(256), lambda o, k, c: o.at[pl.ds(k * 256 + c * 128, 128), :]),
    "ffn_w_up": ((2, 1024, 2 * D_FF), lambda r, c: r.at[c], lambda o, k, c: o.at[c, :, pl.ds(k * (2 * D_FF // 4), 2 * D_FF // 4)]),
    "ffn_w_down": ((2, D_FF, 1024), lambda r, c: r.at[c], lambda o, k, c: o.at[c, pl.ds(k * (D_FF // 4), D_FF // 4), :]),
}


def _gather_weights(names, big, small):
    nb = len(big)
    descs = [GATHER_BIG[n] for n in names]
    rs = small.shape[0]

    def body(*refs):
        ins, s_ref = refs[:nb], refs[nb]
        outs, os_ref = refs[nb + 1:2 * nb + 1], refs[2 * nb + 1]
        ici_send, ici_recv, d2d_send, d2d_recv, loc_sems = refs[2 * nb + 2:2 * nb + 7]
        vbufs = refs[2 * nb + 7:]
        x, y, c = _mesh_pos()
        k = 2 * x + y
        chips = _other_chips(x, y)
        sib = (x, y, 1 - c)

        def remote(src, dst, ssem, rsem, to):
            return pltpu.make_async_remote_copy(src_ref=src, dst_ref=dst, send_sem=ssem, recv_sem=rsem, device_id=to,
                                                device_id_type=MESH)

        stage = [pltpu.make_async_copy(ins[t], vbufs[t], loc_sems.at[2 * t]) for t in range(nb)]
        stage.append(pltpu.make_async_copy(s_ref, vbufs[nb], loc_sems.at[2 * nb]))
        for cp in stage:
            cp.start()
        sends = []
        for t, (_, src, dst) in enumerate(descs):
            for j, (px, py) in enumerate(chips):
                sends.append(remote(src(ins[t], c), dst(outs[t], k, c), ici_send.at[3 * t + j], ici_recv.at[3 * t + j], (px, py, c)))
        for j, (px, py) in enumerate(chips):
            sends.append(remote(s_ref, os_ref.at[k], ici_send.at[3 * nb + j], ici_recv.at[3 * nb + j], (px, py, c)))
        for cp in sends:
            cp.start()
        for cp in stage:
            cp.wait()
        local = []
        for t, (_, src, dst) in enumerate(descs):
            for cc in (0, 1):
                local.append(pltpu.make_async_copy(src(vbufs[t], cc), dst(outs[t], k, cc), loc_sems.at[2 * t + cc]))
        local.append(pltpu.make_async_copy(vbufs[nb], os_ref.at[k], loc_sems.at[2 * nb]))
        for cp in local:
            cp.start()
        for t, (_, src, dst) in enumerate(descs):
            for j, (px, py) in enumerate(chips):
                got = dst(outs[t], 2 * px + py, c)
                remote(got, got, ici_send.at[3 * t + j], ici_recv.at[3 * t + j], (px, py, c)).wait_recv()
                fwd = remote(got, got, d2d_send.at[3 * t + j], d2d_recv.at[3 * t + j], sib)
                fwd.start()
                sends.append(fwd)
        for j, (px, py) in enumerate(chips):
            remote(s_ref, os_ref.at[2 * px + py], ici_send.at[3 * nb + j], ici_recv.at[3 * nb + j], (px, py, c)).wait_recv()
        for t, (_, src, dst) in enumerate(descs):
            for j, (px, py) in enumerate(chips):
                theirs = dst(outs[t], 2 * px + py, 1 - c)
                remote(theirs, theirs, d2d_send.at[3 * t + j], d2d_recv.at[3 * t + j], sib).wait_recv()
        for cp in sends:
            cp.wait_send()
        for cp in local:
            cp.wait()

    out_shape = [jax.ShapeDtypeStruct(d[0], BF16) for d in descs] + [jax.ShapeDtypeStruct((N_CHIPS, rs, LANES), small.dtype)]
    return pl.pallas_call(
        body, name="gather_weights", in_specs=[ANY] * (nb + 1), out_specs=[ANY] * (nb + 1), out_shape=out_shape,
        scratch_shapes=[pltpu.SemaphoreType.DMA((3 * nb + 3,)), pltpu.SemaphoreType.DMA((3 * nb + 3,)),
                        pltpu.SemaphoreType.DMA((3 * nb,)), pltpu.SemaphoreType.DMA((3 * nb,)),
                        pltpu.SemaphoreType.DMA((2 * nb + 1,))]
        + [pltpu.VMEM(b.shape, b.dtype) for b in big] + [pltpu.VMEM(small.shape, small.dtype)],
        compiler_params=pltpu.CompilerParams(vmem_limit_bytes=VMEM_LIMIT),
    )(*big, small)


def _place_own(names, big):
    nb = len(big)
    descs = [GATHER_BIG[n] for n in names]

    def body(*refs):
        ins, outs = refs[:nb], refs[nb:2 * nb]
        sems, vbufs = refs[2 * nb], refs[2 * nb + 1:]
        x, y, c = _mesh_pos()
        k = 2 * x + y
        stage = [pltpu.make_async_copy(ins[t], vbufs[t], sems.at[2 * t]) for t in range(nb)]
        for cp in stage:
            cp.start()
        for cp in stage:
            cp.wait()
        local = [pltpu.make_async_copy(src(vbufs[t], cc), dst(outs[t], k, cc), sems.at[2 * t + cc])
                 for t, (_, src, dst) in enumerate(descs) for cc in (0, 1)]
        for cp in local:
            cp.start()
        for cp in local:
            cp.wait()

    return pl.pallas_call(
        body, name="place_own", in_specs=[ANY] * nb, out_specs=[ANY] * nb,
        out_shape=[jax.ShapeDtypeStruct(d[0], BF16) for d in descs],
        scratch_shapes=[pltpu.SemaphoreType.DMA((2 * nb,))] + [pltpu.VMEM(b.shape, b.dtype) for b in big],
        compiler_params=pltpu.CompilerParams(vmem_limit_bytes=VMEM_LIMIT),
    )(*big)


def _gather_start(names, big, bufs):
    nb = len(big)
    descs = [GATHER_BIG[n] for n in names]

    def body(*refs):
        ins, lnd = refs[:nb], refs[nb:2 * nb]
        send_sems, recv_sems, token = refs[2 * nb], refs[2 * nb + 1], refs[-1]
        x, y, c = _mesh_pos()
        k = 2 * x + y
        for t, (_, src, dst) in enumerate(descs):
            for j, (px, py) in enumerate(_other_chips(x, y)):
                _remote(src(ins[t], c), dst(lnd[t], k, c), send_sems.at[3 * t + j], recv_sems.at[3 * t + j], (px, py, c)).start()
        token[...] = jnp.zeros_like(token)

    out = pl.pallas_call(
        body, name="gather_start",
        out_shape=(pltpu.SemaphoreType.DMA((3 * nb,)), pltpu.SemaphoreType.DMA((3 * nb,)),
                   *[pltpu.HBM(b.shape, b.dtype) for b in big], *[pltpu.HBM(b.shape, b.dtype) for b in bufs],
                   jax.ShapeDtypeStruct((SUBLANES, LANES), F32)),
        in_specs=[HBM] * (2 * nb), out_specs=(SEM, SEM, *[HBM] * (2 * nb), pl.BlockSpec(memory_space=pltpu.VMEM)),
        input_output_aliases={i: 2 + i for i in range(2 * nb)},
        compiler_params=pltpu.CompilerParams(has_side_effects=DATAFLOW),
    )(*[pltpu.with_memory_space_constraint(b, pltpu.HBM) for b in big], *[pltpu.with_memory_space_constraint(b, pltpu.HBM) for b in bufs])
    return out[0], out[1], list(out[2:2 + nb]), list(out[2 + nb:2 + 2 * nb]), out[-1]


def _gather_wait(names, send_sems, recv_sems, big, bufs, after):
    nb = len(big)
    descs = [GATHER_BIG[n] for n in names]

    def body(*refs):
        ins, lnd = refs[:nb], refs[nb:2 * nb]
        ssem, rsem = refs[2 * nb], refs[2 * nb + 1]
        x, y, c = _mesh_pos()
        for t, (_, src, dst) in enumerate(descs):
            for j, (px, py) in enumerate(_other_chips(x, y)):
                cp = _remote(src(ins[t], c), dst(lnd[t], 2 * px + py, c), ssem.at[3 * t + j], rsem.at[3 * t + j], (px, py, c))
                cp.wait_send()
                cp.wait_recv()

    out = pl.pallas_call(
        body, name="gather_wait",
        out_shape=(*[pltpu.HBM(b.shape, b.dtype) for b in big], *[pltpu.HBM(b.shape, b.dtype) for b in bufs]),
        in_specs=[HBM] * (2 * nb) + [SEM, SEM, ANY], out_specs=tuple([HBM] * (2 * nb)),
        input_output_aliases={i: i for i in range(2 * nb)},
        compiler_params=pltpu.CompilerParams(has_side_effects=DATAFLOW),
    )(*big, *bufs, send_sems, recv_sems, after)
    return list(out[nb:])


def _gather_forward(names, bufs):
    nb = len(bufs)
    descs = [GATHER_BIG[n] for n in names]

    def body(*refs):
        outs = refs[nb:2 * nb]
        send_sems, recv_sems = refs[2 * nb:]
        x, y, c = _mesh_pos()
        sib = (x, y, 1 - c)
        sends = []
        for t, (_, src, dst) in enumerate(descs):
            for j, (px, py) in enumerate(_other_chips(x, y)):
                got = dst(outs[t], 2 * px + py, c)
                sends.append(_remote(got, got, send_sems.at[3 * t + j], recv_sems.at[3 * t + j], sib))
        for cp in sends:
            cp.start()
        for t, (_, src, dst) in enumerate(descs):
            for j, (px, py) in enumerate(_other_chips(x, y)):
                theirs = dst(outs[t], 2 * px + py, 1 - c)
                _remote(theirs, theirs, send_sems.at[3 * t + j], recv_sems.at[3 * t + j], sib).wait_recv()
        for cp in sends:
            cp.wait_send()

    return pl.pallas_call(
        body, name="gather_forward", in_specs=[ANY] * nb, out_specs=[ANY] * nb,
        out_shape=[jax.ShapeDtypeStruct(b.shape, b.dtype) for b in bufs], input_output_aliases={t: t for t in range(nb)},
        scratch_shapes=[pltpu.SemaphoreType.DMA((3 * nb,)), pltpu.SemaphoreType.DMA((3 * nb,))],
    )(*bufs)


def _remote(src, dst, ssem, rsem, to):
    return pltpu.make_async_remote_copy(src_ref=src, dst_ref=dst, send_sem=ssem, recv_sem=rsem, device_id=to, device_id_type=MESH)


def _rs_send_sibling(gs, tag):
    n = len(gs)
    counts = [N_CHIPS if g.ndim == 4 else 1 for g in gs]
    ns = sum(counts)

    def body(*refs):
        ins, outs = refs[:n], refs[n:2 * n]
        send_sems, recv_sems = refs[2 * n:]
        x, y, c = _mesh_pos()
        cps, s = [], 0
        for t in range(n):
            if counts[t] == 1:
                cps.append(_remote(ins[t].at[1 - c], outs[t], send_sems.at[s], recv_sems.at[s], (x, y, 1 - c)))
                s += 1
            else:
                for k in range(N_CHIPS):
                    cps.append(_remote(ins[t].at[k, 1 - c], outs[t].at[k], send_sems.at[s], recv_sems.at[s], (x, y, 1 - c)))
                    s += 1
        for cp in cps:
            cp.start()
        for cp in cps:
            cp.wait()

    out_shape = [jax.ShapeDtypeStruct(g.shape[:1] + g.shape[2:] if g.ndim == 4 else g.shape[1:], g.dtype) for g in gs]
    return pl.pallas_call(
        body, name=f"rs_send_sibling_{tag}", in_specs=[ANY] * n, out_specs=[ANY] * n, out_shape=out_shape,
        scratch_shapes=[pltpu.SemaphoreType.DMA((ns,)), pltpu.SemaphoreType.DMA((ns,))],
    )(*gs)


def _add_piece(g, recv, c, name):
    P, Q = g.shape[-2:]

    def body(c_ref, g_ref, r_ref, o_ref):
        o_ref[...] = g_ref[...].reshape(o_ref.shape) + r_ref[...]

    if g.ndim == 4:
        grid = (N_CHIPS,)
        in_specs = [pl.BlockSpec((1, 1, P, Q), lambda k, c_ref: (k, c_ref[0], 0, 0)), pl.BlockSpec((1, P, Q), lambda k, c_ref: (k, 0, 0))]
        out_spec = pl.BlockSpec((1, P, Q), lambda k, c_ref: (k, 0, 0))
    else:
        grid = (1,)
        in_specs = [pl.BlockSpec((1, P, Q), lambda k, c_ref: (c_ref[0], 0, 0)), pl.BlockSpec((P, Q), lambda k, c_ref: (0, 0))]
        out_spec = pl.BlockSpec((P, Q), lambda k, c_ref: (0, 0))
    return pl.pallas_call(
        body, name=name,
        grid_spec=pltpu.PrefetchScalarGridSpec(num_scalar_prefetch=1, grid=grid, in_specs=in_specs, out_specs=out_spec),
        out_shape=jax.ShapeDtypeStruct(recv.shape, g.dtype),
        compiler_params=_cp(1),
    )(c, g, recv)


HBM = pl.BlockSpec(memory_space=pltpu.HBM)
SEM = pl.BlockSpec(memory_space=pltpu.SEMAPHORE)
DATAFLOW = pltpu.SideEffectType.DATAFLOW_SIDE_EFFECTING


def _sibling_copies(gs, srcs, dsts, send_sems, recv_sems):
    x, y, c = _mesh_pos()
    cps, s = [], 0
    for t, g in enumerate(gs):
        if g.ndim == 4:
            for k in range(N_CHIPS):
                cps.append(_remote(srcs[t].at[k, 1 - c], dsts[t].at[k], send_sems.at[s], recv_sems.at[s], (x, y, 1 - c)))
                s += 1
        else:
            cps.append(_remote(srcs[t].at[1 - c], dsts[t], send_sems.at[s], recv_sems.at[s], (x, y, 1 - c)))
            s += 1
    return cps


def _sibling_start(gs, tag):
    n = len(gs)
    ns = sum(N_CHIPS if g.ndim == 4 else 1 for g in gs)
    lands = [pltpu.with_memory_space_constraint(lax.empty(g.shape[:1] + g.shape[2:] if g.ndim == 4 else g.shape[1:], g.dtype), pltpu.HBM)
             for g in gs]

    def body(*refs):
        for cp in _sibling_copies(gs, refs[:n], refs[n:2 * n], refs[2 * n], refs[2 * n + 1]):
            cp.start()
        refs[-1][...] = jnp.zeros_like(refs[-1])

    out = pl.pallas_call(
        body, name=f"sibling_start_{tag}",
        out_shape=(pltpu.SemaphoreType.DMA((ns,)), pltpu.SemaphoreType.DMA((ns,)),
                   *[pltpu.HBM(g.shape, g.dtype) for g in gs], *[pltpu.HBM(l.shape, l.dtype) for l in lands],
                   jax.ShapeDtypeStruct((SUBLANES, LANES), F32)),
        in_specs=[HBM] * (2 * n), out_specs=(SEM, SEM, *[HBM] * (2 * n), pl.BlockSpec(memory_space=pltpu.VMEM)),
        input_output_aliases={i: 2 + i for i in range(2 * n)},
        compiler_params=pltpu.CompilerParams(has_side_effects=DATAFLOW),
    )(*[pltpu.with_memory_space_constraint(g, pltpu.HBM) for g in gs], *lands)
    return (out[0], out[1], list(out[2:2 + n]), list(out[2 + n:2 + 2 * n])), out[-1]


def _sibling_wait(send_sems, recv_sems, gs, lands, after, tag):
    n = len(gs)

    def body(*refs):
        for cp in _sibling_copies(gs, refs[:n], refs[n:2 * n], refs[2 * n], refs[2 * n + 1]):
            cp.wait_send()
            cp.wait_recv()

    out = pl.pallas_call(
        body, name=f"sibling_wait_{tag}",
        out_shape=(*[pltpu.HBM(g.shape, g.dtype) for g in gs], *[pltpu.HBM(l.shape, l.dtype) for l in lands]),
        in_specs=[HBM] * (2 * n) + [SEM, SEM, ANY], out_specs=tuple([HBM] * (2 * n)),
        input_output_aliases={i: i for i in range(2 * n)},
        compiler_params=pltpu.CompilerParams(has_side_effects=DATAFLOW),
    )(*gs, *lands, send_sems, recv_sems, after)
    return list(out[:n]), list(out[n:])


def _chips_start(hs, tag):
    n = len(hs)
    lands = [pltpu.with_memory_space_constraint(lax.empty((N_CHIPS,) + h.shape[-2:], h.dtype), pltpu.HBM) for h in hs]

    def body(*refs):
        ins, lnd = refs[:n], refs[n:2 * n]
        send_sems, recv_sems, token = refs[2 * n], refs[2 * n + 1], refs[-1]
        x, y, c = _mesh_pos()
        k = 2 * x + y
        piece = lambda t, kk: ins[t].at[kk] if hs[t].ndim == 3 else ins[t]
        for t in range(n):
            for j, (px, py) in enumerate(_other_chips(x, y)):
                _remote(piece(t, 2 * px + py), lnd[t].at[k], send_sems.at[3 * t + j], recv_sems.at[3 * t + j], (px, py, c)).start()
        token[...] = jnp.zeros_like(token)

    out = pl.pallas_call(
        body, name=f"chips_start_{tag}",
        out_shape=(pltpu.SemaphoreType.DMA((3 * n,)), pltpu.SemaphoreType.DMA((3 * n,)),
                   *[pltpu.HBM(h.shape, h.dtype) for h in hs], *[pltpu.HBM(l.shape, l.dtype) for l in lands],
                   jax.ShapeDtypeStruct((SUBLANES, LANES), F32)),
        in_specs=[HBM] * (2 * n), out_specs=(SEM, SEM, *[HBM] * (2 * n), pl.BlockSpec(memory_space=pltpu.VMEM)),
        input_output_aliases={i: 2 + i for i in range(2 * n)},
        compiler_params=pltpu.CompilerParams(has_side_effects=DATAFLOW),
    )(*[pltpu.with_memory_space_constraint(h, pltpu.HBM) for h in hs], *lands)
    return out[0], out[1], list(out[2:2 + n]), list(out[2 + n:2 + 2 * n]), out[-1]


def _chips_wait(send_sems, recv_sems, hs, lands, after, tag):
    n = len(hs)

    def body(*refs):
        ins, lnd = refs[:n], refs[n:2 * n]
        ssem, rsem = refs[2 * n], refs[2 * n + 1]
        x, y, c = _mesh_pos()
        k = 2 * x + y
        piece = lambda t, kk: ins[t].at[kk] if hs[t].ndim == 3 else ins[t]
        for t in range(n):
            for j, (px, py) in enumerate(_other_chips(x, y)):
                cp = _remote(piece(t, k), lnd[t].at[2 * px + py], ssem.at[3 * t + j], rsem.at[3 * t + j], (px, py, c))
                cp.wait_send()
                cp.wait_recv()

    out = pl.pallas_call(
        body, name=f"chips_wait_{tag}",
        out_shape=(*[pltpu.HBM(h.shape, h.dtype) for h in hs], *[pltpu.HBM(l.shape, l.dtype) for l in lands]),
        in_specs=[HBM] * (2 * n) + [SEM, SEM, ANY], out_specs=tuple([HBM] * (2 * n)),
        input_output_aliases={i: i for i in range(2 * n)},
        compiler_params=pltpu.CompilerParams(has_side_effects=DATAFLOW),
    )(*hs, *lands, send_sems, recv_sems, after)
    return list(out[:n]), list(out[n:])


def _add_chips(p, own, kc, name):
    _, P, Q = p.shape
    tr = P
    while N_CHIPS * tr * Q * 4 > 6 * 1024 * 1024 and tr % 16 == 0:
        tr //= 2
    sharded = own.ndim == 3

    def body(kc_ref, p_ref, own_ref, o_ref):
        k = kc_ref[0]
        mine = own_ref[...].reshape(tr, Q)
        v = [jnp.where(k == j, mine, p_ref[j]) for j in range(N_CHIPS)]
        o_ref[0] = ((v[0] + v[1]) + v[2]) + v[3]

    own_spec = (pl.BlockSpec((1, tr, Q), lambda i, kc_ref: (kc_ref[0], i, 0)) if sharded
                else pl.BlockSpec((tr, Q), lambda i, kc_ref: (i, 0)))
    return pl.pallas_call(
        body, name=name,
        grid_spec=pltpu.PrefetchScalarGridSpec(
            num_scalar_prefetch=1, grid=(P // tr,),
            in_specs=[pl.BlockSpec((N_CHIPS, tr, Q), lambda i, kc_ref: (0, i, 0)), own_spec],
            out_specs=pl.BlockSpec((1, tr, Q), lambda i, kc_ref: (kc_ref[1], i, 0))),
        out_shape=jax.ShapeDtypeStruct((2, P, Q), p.dtype),
        compiler_params=_cp(1),
    )(kc, p, own)


def _rs_share(fs, tag):
    n = len(fs)

    def body(*refs):
        outs = refs[n:2 * n]
        send_sems, recv_sems = refs[2 * n:]
        x, y, c = _mesh_pos()
        sends = [_remote(outs[t].at[c], outs[t].at[c], send_sems.at[t], recv_sems.at[t], (x, y, 1 - c)) for t in range(n)]
        for cp in sends:
            cp.start()
        for t in range(n):
            _remote(outs[t].at[c], outs[t].at[1 - c], send_sems.at[t], recv_sems.at[t], (x, y, 1 - c)).wait_recv()
        for cp in sends:
            cp.wait_send()

    return pl.pallas_call(
        body, name=f"rs_share_{tag}", in_specs=[ANY] * n, out_specs=[ANY] * n,
        out_shape=[jax.ShapeDtypeStruct(f.shape, f.dtype) for f in fs], input_output_aliases={t: t for t in range(n)},
        scratch_shapes=[pltpu.SemaphoreType.DMA((n,)), pltpu.SemaphoreType.DMA((n,))],
    )(*fs)


def _reduce_start(gs, kc, tag):
    return _reduce_continue(gs, _rs_send_sibling(gs, tag), kc, tag)


def _reduce_continue(gs, from_sibling, kc, tag):
    chip_sums = [_add_piece(g, r, kc[1:], name=f"add_piece_{tag}_{t}") for t, (g, r) in enumerate(zip(gs, from_sibling))]
    send_sems, recv_sems, chip_sums, lands, token = _chips_start(chip_sums, tag)
    return (send_sems, recv_sems, chip_sums, lands, tag), token


def _reduce_finish(states, kc, after):
    mine = []
    for send_sems, recv_sems, chip_sums, lands, tag in states:
        chip_sums, from_chips = _chips_wait(send_sems, recv_sems, chip_sums, lands, after, tag)
        mine += [_add_chips(p, h, kc, name=f"add_chips_{tag}_{t}") for t, (p, h) in enumerate(zip(from_chips, chip_sums))]
    return _rs_share(mine, "all")


def _adamw(w, g, m, v, name):
    R, C = w.shape
    tr = R
    for cand in (512, 256, 128, 64, 32, 16, 8):
        if R % cand == 0 and cand * C * 4 <= 2 * 1024 * 1024:
            tr = cand
            break
    c1 = 1.0 / (1.0 - ADAM_B1 ** ADAM_STEP)
    c2 = 1.0 / (1.0 - ADAM_B2 ** ADAM_STEP)

    def body(w_ref, g_ref, m_ref, v_ref, d_ref, mo_ref, vo_ref):
        gv = g_ref[...]
        mn = ADAM_B1 * m_ref[...] + (1.0 - ADAM_B1) * gv
        vn = ADAM_B2 * v_ref[...] + (1.0 - ADAM_B2) * (gv * gv)
        mo_ref[...] = mn
        vo_ref[...] = vn
        d_ref[...] = -ADAM_LR * ((mn * c1) / (jnp.sqrt(vn * c2) + ADAM_EPS) + ADAM_WD * w_ref[...])

    spec = pl.BlockSpec((tr, C), lambda i: (i, 0))
    shp = jax.ShapeDtypeStruct((R, C), F32)
    return pl.pallas_call(body, name=name, grid=(R // tr,), in_specs=[spec] * 4, out_specs=[spec] * 3, out_shape=[shp] * 3,
                          compiler_params=_cp(1))(w, g, m, v)


def _adamw_many(ws, gs, ms, vs):
    n = len(ws)
    c1 = 1.0 / (1.0 - ADAM_B1 ** ADAM_STEP)
    c2 = 1.0 / (1.0 - ADAM_B2 ** ADAM_STEP)

    def body(*refs):
        w_refs, g_refs, m_refs, v_refs = (refs[k * n:(k + 1) * n] for k in range(4))
        outs = refs[4 * n:]
        for i in range(n):
            gv = g_refs[i][...]
            mn = ADAM_B1 * m_refs[i][...] + (1.0 - ADAM_B1) * gv
            vn = ADAM_B2 * v_refs[i][...] + (1.0 - ADAM_B2) * (gv * gv)
            outs[3 * i][...] = -ADAM_LR * ((mn * c1) / (jnp.sqrt(vn * c2) + ADAM_EPS) + ADAM_WD * w_refs[i][...])
            outs[3 * i + 1][...] = mn
            outs[3 * i + 2][...] = vn

    out_shape = [jax.ShapeDtypeStruct(w.shape, F32) for w in ws for _ in range(3)]
    return pl.pallas_call(body, name="adamw_small", out_shape=out_shape)(*ws, *gs, *ms, *vs)


def _step(a):
    x, y, c = _mesh_pos()
    kc = jnp.stack([2 * x + y, c]).astype(jnp.int32)

    rs = _rows(sum(a[n].size for n in SH_SMALL))
    first, later = ["ev_w_in", "ev_w_out"], ["od_w_in", "od_w_out", "ffn_w_up", "ffn_w_down"]
    lead = lambda w: w if w.ndim == 3 else w[None]
    *full, gs = _gather_weights(first, [a[n].astype(BF16) for n in first], _pack([a[n] for n in SH_SMALL], rs, F32))
    p = {n: a[n] for n in REP}
    p.update({n: lead(w) for n, w in zip(first, full)})
    parts = [_unpack(gs[k], [a[n].shape for n in SH_SMALL]) for k in range(N_CHIPS)]
    for i, n in enumerate(SH_SMALL):
        p[n] = jnp.concatenate([parts[k][i] for k in range(N_CHIPS)], axis=SH_SMALL[n])
    shards = lax.optimization_barrier(([a[n].astype(BF16) for n in later], full))[0]
    g_send, g_recv, shards, bufs, token = _gather_start(later, shards, _place_own(later, shards))
    p["norm_mix"] = p["norm_mix"] + token[0:1, 0:1]

    def late(after):
        got = _gather_forward(later, _gather_wait(later, g_send, g_recv, shards, bufs, after))
        return {n: lead(w) for n, w in zip(later, got)}

    p["late"] = late

    states, pending = {}, {}

    def start_reduce(tag, gs, split=False):
        if split:
            pending[tag], token = _sibling_start(gs, tag)
        else:
            states[tag], token = _reduce_start(gs, kc, tag)
        return token

    def continue_reduce(tag, after):
        send_sems, recv_sems, gs, lands = pending.pop(tag)
        gs, from_sibling = _sibling_wait(send_sems, recv_sems, gs, lands, after, tag)
        states[tag], token = _reduce_continue(gs, from_sibling, kc, tag)
        return token

    sq8, grad_x, grads, big = _local_step(a["x"][0], a["loss_target"][0], p, start_reduce, continue_reduce)
    loss = lax.psum(0.5 / D_MODEL * jnp.sum(sq8), ("x", "y", "c"))

    r_s = _rows(sum(a[n].size for n in SH_SMALL), 2 * SUBLANES) // 2
    small_pieces = []
    for k in range(N_CHIPS):
        pieces = [lax.slice_in_dim(grads[n], k * a[n].shape[ax], (k + 1) * a[n].shape[ax], axis=ax) for n, ax in SH_SMALL.items()]
        small_pieces.append(_pack(pieces, 2 * r_s, F32).reshape(2, r_s, LANES))
    g_small = jnp.stack(small_pieces)
    r_r = _rows(sum(a[n].size for n in REP), 2 * SUBLANES) // 2
    g_rep = _pack([grads[n] for n in REP], 2 * r_r, F32).reshape(2, r_r, LANES)
    start_reduce("g4", [g_small, g_rep])
    reduced = _reduce_finish([states[tag] for tag in ("g1", "g2", "g3", "g4")], kc, grad_x)
    red = dict(zip([key for tag in ("g1", "g2", "g3") for key in REDUCE_GROUPS[tag]], reduced))
    gfin = {}
    for n in ("ev_w_in", "ev_w_out", "od_w_in", "od_w_out"):
        gfin[n] = red[n, 0].reshape(a[n].shape)
    for n in ("ffn_w_up", "ffn_w_down"):
        gfin[n] = jnp.stack([red[n, l].reshape(a[n].shape[1:]) for l in range(2)])
    gfin.update(zip(SH_SMALL, _unpack(reduced[-2], [a[n].shape for n in SH_SMALL])))
    gfin.update(zip(REP, _unpack(reduced[-1], [a[n].shape for n in REP])))

    out = {"loss": loss, "grad_x": grad_x[None]}
    small_names = list(SH_SMALL) + REP
    for n in SH_BIG:
        shp = a[n].shape
        two_d = lambda t: t.reshape(-1, shp[-1])
        d, mo, vo = _adamw(two_d(a[n]), two_d(gfin[n]), two_d(a["m_" + n]), two_d(a["v_" + n]), name=f"adamw_{n}")
        out["delta_" + n], out["new_m_" + n], out["new_v_" + n] = d.reshape(shp), mo.reshape(shp), vo.reshape(shp)
    two_d = lambda t: t.reshape(-1, t.shape[-1])
    res = _adamw_many(*[[two_d(src(n)) for n in small_names]
                        for src in (lambda n: a[n], lambda n: gfin[n], lambda n: a["m_" + n], lambda n: a["v_" + n])])
    for i, n in enumerate(small_names):
        out["delta_" + n], out["new_m_" + n], out["new_v_" + n] = (r.reshape(a[n].shape) for r in res[3 * i:3 * i + 3])
    for n in WEIGHTS:
        out["grad_" + n] = gfin[n]
    return out


def kernel(x, norm_mix, norm_ffn, norm_final, ev_w_in, ev_conv_w, ev_conv_b, ev_gate_a_w, ev_gate_a_b, ev_gate_x_w, ev_gate_x_b, ev_lru_lambda, hg_lb_logits, ev_hg_norm, ev_w_out, od_w_in, od_b_in, od_ln_g, od_ln_b, od_w_s, od_b_s, od_w_out, ffn_w_up, ffn_conv_w, ffn_conv_b, ffn_w_down, loss_target, m_norm_mix, m_norm_ffn, m_norm_final, m_ev_w_in, m_ev_conv_w, m_ev_conv_b, m_ev_gate_a_w, m_ev_gate_a_b, m_ev_gate_x_w, m_ev_gate_x_b, m_ev_lru_lambda, m_hg_lb_logits, m_ev_hg_norm, m_ev_w_out, m_od_w_in, m_od_b_in, m_od_ln_g, m_od_ln_b, m_od_w_s, m_od_b_s, m_od_w_out, m_ffn_w_up, m_ffn_conv_w, m_ffn_conv_b, m_ffn_w_down, v_norm_mix, v_norm_ffn, v_norm_final, v_ev_w_in, v_ev_conv_w, v_ev_conv_b, v_ev_gate_a_w, v_ev_gate_a_b, v_ev_gate_x_w, v_ev_gate_x_b, v_ev_lru_lambda, v_hg_lb_logits, v_ev_hg_norm, v_ev_w_out, v_od_w_in, v_od_b_in, v_od_ln_g, v_od_ln_b, v_od_w_s, v_od_b_s, v_od_w_out, v_ffn_w_up, v_ffn_conv_w, v_ffn_conv_b, v_ffn_w_down):
    vals = (x, norm_mix, norm_ffn, norm_final, ev_w_in, ev_conv_w, ev_conv_b, ev_gate_a_w, ev_gate_a_b, ev_gate_x_w, ev_gate_x_b, ev_lru_lambda, hg_lb_logits, ev_hg_norm, ev_w_out, od_w_in, od_b_in, od_ln_g, od_ln_b, od_w_s, od_b_s, od_w_out, ffn_w_up, ffn_conv_w, ffn_conv_b, ffn_w_down, loss_target, m_norm_mix, m_norm_ffn, m_norm_final, m_ev_w_in, m_ev_conv_w, m_ev_conv_b, m_ev_gate_a_w, m_ev_gate_a_b, m_ev_gate_x_w, m_ev_gate_x_b, m_ev_lru_lambda, m_hg_lb_logits, m_ev_hg_norm, m_ev_w_out, m_od_w_in, m_od_b_in, m_od_ln_g, m_od_ln_b, m_od_w_s, m_od_b_s, m_od_w_out, m_ffn_w_up, m_ffn_conv_w, m_ffn_conv_b, m_ffn_w_down, v_norm_mix, v_norm_ffn, v_norm_final, v_ev_w_in, v_ev_conv_w, v_ev_conv_b, v_ev_gate_a_w, v_ev_gate_a_b, v_ev_gate_x_w, v_ev_gate_x_b, v_ev_lru_lambda, v_hg_lb_logits, v_ev_hg_norm, v_ev_w_out, v_od_w_in, v_od_b_in, v_od_ln_g, v_od_ln_b, v_od_w_s, v_od_b_s, v_od_w_out, v_ffn_w_up, v_ffn_conv_w, v_ffn_conv_b, v_ffn_w_down)
    names = ["x"] + WEIGHTS + ["loss_target"] + ["m_" + n for n in WEIGHTS] + ["v_" + n for n in WEIGHTS]
    out = _step(dict(zip(names, vals)))
    return (out["loss"], out["grad_x"], *[out["grad_" + n] for n in WEIGHTS], *[out["delta_" + n] for n in WEIGHTS],
            *[out["new_m_" + n] for n in WEIGHTS], *[out["new_v_" + n] for n in WEIGHTS])
```

```python
import functools

import jax
import jax.numpy as jnp
from jax import lax
from jax.experimental import pallas as pl
from jax.experimental.pallas import tpu as pltpu

F32 = jnp.float32
BF16 = jnp.bfloat16

EPS = 1e-6
D_MODEL = 1024
LRU_W = 512
LRU_BLOCKS = 8
LRU_C = 8.0
HG_HEADS = 4
HG_D = 128
HG_CHUNK = 64
SGU_G = 8
SGU_CHUNK = 128
D_FF = 2816
ADAM_LR, ADAM_B1, ADAM_B2, ADAM_EPS, ADAM_WD, ADAM_STEP = 0.001, 0.9, 0.999, 1e-08, 0.01, 10

V7X_VMEM_BYTES = 64 * 1024 * 1024
VMEM_LIMIT = V7X_VMEM_BYTES - 8 * 1024 * 1024
SUBLANES = 8
LANES = 128
BF16_ROWS = 16

GELU_C0 = 0.7978845608028654
GELU_C1 = 0.044715

NN = (((1,), (0,)), ((), ()))
NT = (((1,), (1,)), ((), ()))
TN = (((0,), (0,)), ((), ()))


def _dot(a, b, dims=NN):
    return lax.dot_general(a.astype(BF16), b.astype(BF16), dims, preferred_element_type=F32)


def _cp(n_grid):
    return pltpu.CompilerParams(dimension_semantics=("arbitrary",) * n_grid, vmem_limit_bytes=VMEM_LIMIT)


def _chunk(n, cap):
    best = LANES
    for c in range(LANES, cap + 1, LANES):
        if n % c == 0:
            best = c
    return best


def _resident(shape):
    nd = len(shape)
    return pl.BlockSpec(shape, lambda *_: (0,) * nd, pipeline_mode=pl.Buffered(1))


def _rsum8(x):
    r, c = x.shape
    return x.reshape(r // SUBLANES, SUBLANES, c).sum(axis=0)


def _sigmoid(x):
    return 0.5 * jnp.tanh(0.5 * x) + 0.5


def _gelu(x):
    return 0.5 * x * (1.0 + jnp.tanh(GELU_C0 * (x + GELU_C1 * x * x * x)))


def _gelu_grad(x):
    t = jnp.tanh(GELU_C0 * (x + GELU_C1 * x * x * x))
    return 0.5 * (1.0 + t) + 0.5 * x * (1.0 - t * t) * GELU_C0 * (1.0 + 3.0 * GELU_C1 * x * x)


def _silu_and_grad(x):
    s = _sigmoid(x)
    return x * s, s * (1.0 + x * (1.0 - s))


def _shift_rows(e, j):
    n = e.shape[0]
    return e if j % n == 0 else pltpu.roll(e, j % n, 0)


def _weight(w):
    if isinstance(w, tuple):
        stack, layer = w
        K, N = stack.shape[1:]
        return stack, pl.BlockSpec((None, K, N), lambda *_: (layer, 0, 0), pipeline_mode=pl.Buffered(1)), (K, N)
    return w, _resident(w.shape), w.shape


def _norm_mm(h, g, w, b, name, tt=1024):
    T, D = h.shape
    w, w_spec, (_, N) = _weight(w)
    cn = _chunk(N, 512)

    def body(h_ref, g_ref, w_ref, b_ref, hn_ref, z_ref):
        x = h_ref[...]
        r = lax.rsqrt(jnp.mean(x * x, axis=-1, keepdims=True) + EPS)
        hn = (x * r * g_ref[...]).astype(BF16)
        hn_ref[...] = hn
        for j in range(0, N, cn):
            acc = jnp.dot(hn, w_ref[:, j:j + cn], preferred_element_type=F32) + b_ref[:, j:j + cn]
            z_ref[:, j:j + cn] = acc.astype(BF16)

    return pl.pallas_call(
        body, name=name, grid=(T // tt,),
        in_specs=[pl.BlockSpec((tt, D), lambda i: (i, 0)), _resident((1, D)), w_spec, _resident((1, N))],
        out_specs=[pl.BlockSpec((tt, D), lambda i: (i, 0)), pl.BlockSpec((tt, N), lambda i: (i, 0))],
        out_shape=[jax.ShapeDtypeStruct((T, D), BF16), jax.ShapeDtypeStruct((T, N), BF16)],
        compiler_params=_cp(1),
    )(h, g, w, b)


def _mm(a, w, res, out_dtype, name, tt=1024, transpose_w=False):
    T, K = a.shape
    w, w_spec, w_shape = _weight(w)
    N = w_shape[0] if transpose_w else w_shape[1]
    cn = _chunk(N, 512)
    has_res = res is not None

    def body(*refs):
        a_ref, w_ref = refs[0], refs[1]
        res_ref = refs[2] if has_res else None
        o_ref = refs[-1]
        av = a_ref[...].astype(BF16)
        for j in range(0, N, cn):
            if transpose_w:
                acc = lax.dot_general(av, w_ref[j:j + cn, :], NT, preferred_element_type=F32)
            else:
                acc = jnp.dot(av, w_ref[:, j:j + cn], preferred_element_type=F32)
            if has_res:
                acc = acc + res_ref[:, j:j + cn]
            o_ref[:, j:j + cn] = acc.astype(out_dtype)

    in_specs = [pl.BlockSpec((tt, K), lambda i: (i, 0)), w_spec]
    args = [a, w]
    if has_res:
        in_specs.append(pl.BlockSpec((tt, N), lambda i: (i, 0)))
        args.append(res)
    return pl.pallas_call(
        body, name=name, grid=(T // tt,), in_specs=in_specs,
        out_specs=pl.BlockSpec((tt, N), lambda i: (i, 0)),
        out_shape=jax.ShapeDtypeStruct((T, N), out_dtype),
        compiler_params=_cp(1),
    )(*args)


def _mm_tn(a, b, name, col_shards=1, tt=2048):
    T, K = a.shape
    N = b.shape[1]
    ns = N // col_shards
    tt = min(tt, T)
    while 2 * (tt * K * a.dtype.itemsize + tt * ns * b.dtype.itemsize + K * ns * 4) > VMEM_LIMIT - 12 * 1024 * 1024:
        tt //= 2

    def body(a_ref, b_ref, o_ref):
        acc = lax.dot_general(a_ref[...].astype(BF16), b_ref[...].astype(BF16), TN, preferred_element_type=F32)
        prev = jnp.where(pl.program_id(1) == 0, 0.0, o_ref[0])
        o_ref[0] = prev + acc

    out = pl.pallas_call(
        body, name=name, grid=(col_shards, T // tt),
        in_specs=[pl.BlockSpec((tt, K), lambda n, t: (t, 0)), pl.BlockSpec((tt, ns), lambda n, t: (t, n))],
        out_specs=pl.BlockSpec((1, K, ns), lambda n, t: (n, 0, 0)),
        out_shape=jax.ShapeDtypeStruct((col_shards, K, ns), F32),
        compiler_params=_cp(2),
    )(a, b)
    return out if col_shards > 1 else out[0]


def _mm_normbwd(dz, w, x, g, dres, name):
    T, N = dz.shape
    D = w.shape[0]
    tt = 1024 if N <= 3072 else 512

    def body(dz_ref, wt_ref, x_ref, g_ref, dres_ref, dx_ref, dg_ref):
        @pl.when(pl.program_id(0) == 0)
        def _():
            dg_ref[...] = jnp.zeros_like(dg_ref)

        dy = lax.dot_general(dz_ref[...], wt_ref[...], NT, preferred_element_type=F32)
        x = x_ref[...]
        r = lax.rsqrt(jnp.mean(x * x, axis=-1, keepdims=True) + EPS)
        xn = x * r
        dg_ref[...] += _rsum8(dy * xn)
        dxn = dy * g_ref[...]
        dx_ref[...] = dres_ref[...] + r * (dxn - xn * jnp.mean(dxn * xn, axis=-1, keepdims=True))

    return pl.pallas_call(
        body, name=name, grid=(T // tt,),
        in_specs=[pl.BlockSpec((tt, N), lambda i: (i, 0)), _resident((D, N)), pl.BlockSpec((tt, D), lambda i: (i, 0)),
                  _resident((1, D)), pl.BlockSpec((tt, D), lambda i: (i, 0))],
        out_specs=[pl.BlockSpec((tt, D), lambda i: (i, 0)), pl.BlockSpec((SUBLANES, D), lambda i: (0, 0))],
        out_shape=[jax.ShapeDtypeStruct((T, D), F32), jax.ShapeDtypeStruct((SUBLANES, D), F32)],
        compiler_params=_cp(1),
    )(dz, w, x, g, dres)


def _col_groups(F, cc, per_group=4):
    step = cc * per_group
    return [(g0, min(g0 + step, F)) for g0 in range(0, F, step)]


def _loss_head(x, gv, tgt):
    r = lax.rsqrt(jnp.mean(x * x, axis=-1, keepdims=True) + EPS)
    xn = x * r
    diff = xn * gv - tgt
    dy = diff * (1.0 / x.shape[-1])
    dxn = dy * gv
    return r * (dxn - xn * jnp.mean(dxn * xn, axis=-1, keepdims=True)), _rsum8(diff * diff), _rsum8(dy * xn)


def _ffn_act(gu, cw, cb, w_down, res, name, loss=None, tt=512):
    T = gu.shape[0]
    F = gu.shape[1] // 2
    w_down, wd_spec, (_, D) = _weight(w_down)
    cc = _chunk(F, 256)
    hb = tt // BF16_ROWS

    def body(gu_ref, halo_ref, cw_ref, cb_ref, wd_ref, res_ref, *rest):
        if loss is None:
            a_ref, gc_ref, o_ref = rest
        else:
            gf_ref, t_ref, a_ref, gc_ref, o_ref, sq_ref, dgf_ref = rest
        first = pl.program_id(0) == 0
        acc = res_ref[...]
        for g0, g1 in _col_groups(F, cc):
            for c0 in range(g0, g1, cc):
                cs = slice(c0, c0 + cc)
                x = gu_ref[:, cs].astype(F32)
                halo = jnp.where(first, 0.0, halo_ref[:, cs].astype(F32))
                e = jnp.concatenate([halo, x], axis=0)
                gc = (cb_ref[:, cs] + cw_ref[0:1, cs] * _shift_rows(e, 2)[BF16_ROWS:] + cw_ref[1:2, cs] * _shift_rows(e, 1)[BF16_ROWS:]
                      + cw_ref[2:3, cs] * x)
                up = gu_ref[:, F + c0:F + c0 + cc].astype(F32)
                gc_ref[:, cs] = gc.astype(BF16)
                a_ref[:, cs] = (gc * _sigmoid(gc) * up).astype(BF16)
            acc = acc + jnp.dot(a_ref[:, g0:g1], wd_ref[g0:g1, :], preferred_element_type=F32)
        if loss is None:
            o_ref[...] = acc
        else:
            @pl.when(first)
            def _():
                sq_ref[...] = jnp.zeros_like(sq_ref)
                dgf_ref[...] = jnp.zeros_like(dgf_ref)

            dx, sq, dgf = _loss_head(acc, gf_ref[...], t_ref[...])
            o_ref[...] = dx
            sq_ref[...] += sq
            dgf_ref[...] += dgf

    tok = lambda w: pl.BlockSpec((tt, w), lambda i: (i, 0))
    acc8 = pl.BlockSpec((SUBLANES, D), lambda i: (0, 0))
    in_specs = [tok(2 * F), pl.BlockSpec((BF16_ROWS, F), lambda i: (jnp.maximum(i * hb - 1, 0), 0)),
                _resident((SUBLANES, F)), _resident((1, F)), wd_spec, tok(D)]
    out_specs = [tok(F), tok(F), tok(D)]
    out_shape = [jax.ShapeDtypeStruct((T, F), BF16), jax.ShapeDtypeStruct((T, F), BF16), jax.ShapeDtypeStruct((T, D), F32)]
    args = [gu, gu, cw, cb, w_down, res]
    if loss is not None:
        in_specs += [_resident((1, D)), tok(D)]
        out_specs += [acc8, acc8]
        out_shape += [jax.ShapeDtypeStruct((SUBLANES, D), F32)] * 2
        args += list(loss)
    return pl.pallas_call(body, name=name, grid=(T // tt,), in_specs=in_specs, out_specs=out_specs, out_shape=out_shape,
                          compiler_params=_cp(1))(*args)


def _ffn_act_bwd(gu, gc, cw, w_up, w_down, x, g, dh, name, tt=256):
    T = gu.shape[0]
    F = gu.shape[1] // 2
    w_up, wu_spec, (D, _) = _weight(w_up)
    w_down, wd_spec, _ = _weight(w_down)
    cc = _chunk(F, 256)
    hb = tt // BF16_ROWS
    last_hb = T // BF16_ROWS - 1
    nt = T // tt

    def body(gu_ref, upnext_ref, gc_ref, gcnext_ref, cw_ref, wu_ref, wd_ref, x_ref, g_ref, dh_ref, dhnext_ref,
             dgu_ref, dc_ref, dx_ref, dg_ref):
        i = pl.program_id(0)

        @pl.when(i == 0)
        def _():
            dc_ref[...] = jnp.zeros_like(dc_ref)
            dg_ref[...] = jnp.zeros_like(dg_ref)

        n = tt + BF16_ROWS
        ext = lambda main, nxt: jnp.concatenate([main.astype(F32), nxt.astype(F32)], axis=0)
        dhe = ext(dh_ref[...], jnp.where(i == nt - 1, 0.0, dhnext_ref[...])).astype(BF16)
        dy = jnp.zeros((tt, D), F32)
        groups = _col_groups(F, cc)
        da_of = lambda grp: lax.dot_general(dhe, wd_ref[grp[0]:grp[1], :], NT, preferred_element_type=F32)
        da_next = da_of(groups[0])
        for gi, (lo, hi) in enumerate(groups):
            da, da_next = da_next, (da_of(groups[gi + 1]) if gi + 1 < len(groups) else None)
            for c0 in range(lo, hi, cc):
                cs = slice(c0, c0 + cc)
                us = slice(F + c0, F + c0 + cc)
                gc = ext(gc_ref[:, cs], gcnext_ref[:, cs])
                up = ext(gu_ref[:, us], upnext_ref[:, cs])
                dae = da[:, c0 - lo:c0 - lo + cc]
                s, ds = _silu_and_grad(gc)
                dgc = dae * up * ds
                dgu_ref[:, us] = (dae * s)[:tt].astype(BF16)
                dgc1 = _shift_rows(dgc, n - 1)[:tt]
                dgc2 = _shift_rows(dgc, n - 2)[:tt]
                dm = dgc[:tt]
                dgu_ref[:, cs] = (cw_ref[2:3, cs] * dm + cw_ref[1:2, cs] * dgc1 + cw_ref[0:1, cs] * dgc2).astype(BF16)
                gt = gu_ref[:, cs].astype(F32)
                dc_ref[0, :, cs] += _rsum8(dgc2 * gt)
                dc_ref[1, :, cs] += _rsum8(dgc1 * gt)
                dc_ref[2, :, cs] += _rsum8(dm * gt)
                dc_ref[3, :, cs] += _rsum8(dm)
            dy = (dy + lax.dot_general(dgu_ref[:, lo:hi], wu_ref[:, lo:hi], NT, preferred_element_type=F32)
                  + lax.dot_general(dgu_ref[:, F + lo:F + hi], wu_ref[:, F + lo:F + hi], NT, preferred_element_type=F32))
        xv = x_ref[...]
        r = lax.rsqrt(jnp.mean(xv * xv, axis=-1, keepdims=True) + EPS)
        xn = xv * r
        dg_ref[...] += _rsum8(dy * xn)
        dxn = dy * g_ref[...]
        dx_ref[...] = dh_ref[...] + r * (dxn - xn * jnp.mean(dxn * xn, axis=-1, keepdims=True))

    tok = lambda w: pl.BlockSpec((tt, w), lambda i: (i, 0))
    nxt = lambda w, col: pl.BlockSpec((BF16_ROWS, w), lambda i: (jnp.minimum((i + 1) * hb, last_hb), col))
    return pl.pallas_call(
        body, name=name, grid=(nt,),
        in_specs=[tok(2 * F), nxt(F, 1), tok(F), nxt(F, 0),
                  _resident((SUBLANES, F)), wu_spec, wd_spec, tok(D), _resident((1, D)), tok(D), nxt(D, 0)],
        out_specs=[tok(2 * F), pl.BlockSpec((4, SUBLANES, F), lambda i: (0, 0, 0)), tok(D),
                   pl.BlockSpec((SUBLANES, D), lambda i: (0, 0))],
        out_shape=[jax.ShapeDtypeStruct((T, 2 * F), BF16), jax.ShapeDtypeStruct((4, SUBLANES, F), F32),
                   jax.ShapeDtypeStruct((T, D), F32), jax.ShapeDtypeStruct((SUBLANES, D), F32)],
        compiler_params=_cp(1),
    )(gu, gu, gc, gc, cw, w_up, w_down, x, g, dh, dh)


def _softplus_neg(lam):
    x = -lam
    y = jnp.exp(-jnp.abs(x))
    l1p = jnp.where(y < 0.01, y * (1.0 - y * (0.5 - y * (1.0 / 3.0))), jnp.log(1.0 + y))
    return jnp.maximum(x, 0.0) + l1p


def _lru_gates(xc, wa_ref, ba_ref, wx_ref, bx_ref, sp):
    xcb = xc.astype(BF16)
    r = _sigmoid(jnp.dot(xcb, wa_ref[...], preferred_element_type=F32) + ba_ref[...])
    gi = _sigmoid(jnp.dot(xcb, wx_ref[...], preferred_element_type=F32) + bx_ref[...])
    log_a = -LRU_C * r * sp
    a = jnp.exp(log_a)
    x2 = 2.0 * log_a
    series = -x2 * (1.0 + x2 * 0.5 * (1.0 + x2 * (1.0 / 3.0)))
    om = jnp.where(x2 > -0.02, series, 1.0 - a * a)
    return r, gi, a, jnp.sqrt(om)


def _lru_conv(xr, halo, cw_ref, cb_ref):
    e = jnp.concatenate([halo, xr], axis=0)
    x1 = _shift_rows(e, 1)[BF16_ROWS:]
    x2 = _shift_rows(e, 2)[BF16_ROWS:]
    x3 = _shift_rows(e, 3)[BF16_ROWS:]
    xc = cb_ref[...] + cw_ref[0:1, :] * x3 + cw_ref[1:2, :] * x2 + cw_ref[2:3, :] * x1 + cw_ref[3:4, :] * xr
    return xc, x1, x2, x3


def _lru_fwd(z, cw, cb, wa, ba, wx, bx, lam, name="lru_fwd", tt=256):
    T = z.shape[0]
    W = LRU_W
    hb = tt // BF16_ROWS
    ng = tt // SUBLANES

    def body(z_ref, halo_ref, cw_ref, cb_ref, wa_ref, ba_ref, wx_ref, bx_ref, lam_ref, oa_ref, h_ref, a_s, sv_ref, u_s, hc):
        i = pl.program_id(0)

        @pl.when(i == 0)
        def _():
            hc[...] = jnp.zeros_like(hc)

        xr = z_ref[:, W:2 * W].astype(F32)
        halo = jnp.where(i == 0, 0.0, halo_ref[...].astype(F32))
        xc, _, _, _ = _lru_conv(xr, halo, cw_ref, cb_ref)
        sp = _softplus_neg(lam_ref[...])
        r, gi, a, mult = _lru_gates(xc, wa_ref, ba_ref, wx_ref, bx_ref, sp)
        a_s[...] = a
        u_s[...] = mult * gi * xc
        for k, saved in enumerate((mult, r, gi, xc)):
            sv_ref[:, k * W:(k + 1) * W] = saved.astype(BF16)
        row = lax.broadcasted_iota(jnp.int32, (SUBLANES, W), 0)

        def step(j, hprev):
            r0 = pl.multiple_of(j * SUBLANES, SUBLANES)
            A = a_s[pl.ds(r0, SUBLANES), :]
            U = u_s[pl.ds(r0, SUBLANES), :]
            for k in (1, 2, 4):
                m = row >= k
                U = jnp.where(m, A * pltpu.roll(U, k, 0) + U, U)
                A = jnp.where(m, A * pltpu.roll(A, k, 0), A)
            H = U + A * hprev
            h_ref[pl.ds(r0, SUBLANES), :] = H
            return jnp.broadcast_to(H[SUBLANES - 1:SUBLANES, :], (SUBLANES, W))

        hc[...] = lax.fori_loop(0, ng, step, hc[...])
        oa_ref[...] = (_gelu(z_ref[:, 0:W].astype(F32)) * h_ref[...]).astype(BF16)

    return pl.pallas_call(
        body, name=name, grid=(T // tt,),
        in_specs=[pl.BlockSpec((tt, 2 * W), lambda i: (i, 0)),
                  pl.BlockSpec((BF16_ROWS, W), lambda i: (jnp.maximum(i * hb - 1, 0), 1)),
                  _resident((SUBLANES, W)), _resident((1, W)), _resident((W, W)), _resident((1, W)),
                  _resident((W, W)), _resident((1, W)), _resident((1, W))],
        out_specs=[pl.BlockSpec((tt, W), lambda i: (i, 0)), pl.BlockSpec((tt, W), lambda i: (i, 0)),
                   pl.BlockSpec((tt, W), lambda i: (i, 0)), pl.BlockSpec((tt, 4 * W), lambda i: (i, 0))],
        out_shape=[jax.ShapeDtypeStruct((T, W), BF16), jax.ShapeDtypeStruct((T, W), F32),
                   jax.ShapeDtypeStruct((T, W), F32), jax.ShapeDtypeStruct((T, 4 * W), BF16)],
        scratch_shapes=[pltpu.VMEM((tt, W), F32), pltpu.VMEM((SUBLANES, W), F32)],
        compiler_params=_cp(1),
    )(z, z, cw, cb, wa, ba, wx, bx, lam)


def _lru_bwd(z, hseq, a_all, saved, dmix, cw, wat, wxt, lam, name="lru_bwd", tt=256):
    T = z.shape[0]
    W = LRU_W
    nt = T // tt
    sb = tt // SUBLANES
    ng = tt // SUBLANES

    def body(z_ref, h_ref, hprev_ref, a_ref, sv_ref, dm_ref, cw_ref, wat_ref, wxt_ref, lam_ref,
             dz_ref, dc_ref, dwa_ref, dwx_ref, dv_ref, c_s, d_s, g_s, gc, an, dxn):
        i = pl.program_id(0)
        ti = nt - 1 - i

        @pl.when(i == 0)
        def _():
            dc_ref[...] = jnp.zeros_like(dc_ref)
            dwa_ref[...] = jnp.zeros_like(dwa_ref)
            dwx_ref[...] = jnp.zeros_like(dwx_ref)
            dv_ref[...] = jnp.zeros_like(dv_ref)
            gc[...] = jnp.zeros_like(gc)
            an[...] = jnp.zeros_like(an)
            dxn[...] = jnp.zeros_like(dxn)

        xr = z_ref[:, W:2 * W].astype(F32)
        yg = z_ref[:, 0:W].astype(F32)
        sp = _softplus_neg(lam_ref[...])
        a = a_ref[...]
        mult, r, gi, xc = (sv_ref[:, k * W:(k + 1) * W].astype(F32) for k in range(4))
        h = h_ref[...]
        hp = jnp.where(ti == 0, 0.0, hprev_ref[...])
        hm1 = _shift_rows(jnp.concatenate([hp, h], axis=0), 1)[SUBLANES:]
        dout = dm_ref[...].astype(F32)
        d_s[...] = dout * _gelu(yg)
        dz_ref[:, 0:W] = (dout * h * _gelu_grad(yg)).astype(BF16)
        c_s[...] = _shift_rows(jnp.concatenate([a, an[...]], axis=0), tt + SUBLANES - 1)[:tt]
        an[...] = a[0:SUBLANES, :]
        row = lax.broadcasted_iota(jnp.int32, (SUBLANES, W), 0)

        def step(j, gnext):
            r0 = pl.multiple_of((ng - 1 - j) * SUBLANES, SUBLANES)
            C = c_s[pl.ds(r0, SUBLANES), :]
            G = d_s[pl.ds(r0, SUBLANES), :]
            for k in (1, 2, 4):
                m = row < SUBLANES - k
                G = jnp.where(m, G + C * pltpu.roll(G, SUBLANES - k, 0), G)
                C = jnp.where(m, C * pltpu.roll(C, SUBLANES - k, 0), C)
            G = G + C * gnext
            g_s[pl.ds(r0, SUBLANES), :] = G
            return jnp.broadcast_to(G[0:1, :], (SUBLANES, W))

        gc[...] = lax.fori_loop(0, ng, step, gc[...])
        du = g_s[...]
        da = du * hm1
        dgi = du * mult * xc
        dxc = du * mult * gi
        dmult = du * gi * xc
        dlog_a = da * a - dmult * (a * a) / mult
        dr = dlog_a * (-LRU_C * sp)
        dv_ref[2] += _rsum8(dlog_a * (-LRU_C * r))
        dpr = (dr * r * (1.0 - r)).astype(BF16)
        dpi = (dgi * gi * (1.0 - gi)).astype(BF16)
        dv_ref[0] += _rsum8(dpr.astype(F32))
        dv_ref[1] += _rsum8(dpi.astype(F32))
        xcb = sv_ref[:, 3 * W:4 * W]
        dwa_ref[...] += lax.dot_general(xcb, dpr, TN, preferred_element_type=F32)
        dwx_ref[...] += lax.dot_general(xcb, dpi, TN, preferred_element_type=F32)
        dxc = dxc + jnp.dot(dpr, wat_ref[...], preferred_element_type=F32) + jnp.dot(dpi, wxt_ref[...], preferred_element_type=F32)
        n = tt + BF16_ROWS
        de = jnp.concatenate([dxc, dxn[...]], axis=0)
        d1, d2, d3 = (_shift_rows(de, n - j)[:tt] for j in (1, 2, 3))
        dxn[...] = dxc[0:BF16_ROWS, :]
        dz_ref[:, W:2 * W] = (cw_ref[3:4, :] * dxc + cw_ref[2:3, :] * d1 + cw_ref[1:2, :] * d2 + cw_ref[0:1, :] * d3).astype(BF16)
        dc_ref[0] += _rsum8(d3 * xr)
        dc_ref[1] += _rsum8(d2 * xr)
        dc_ref[2] += _rsum8(d1 * xr)
        dc_ref[3] += _rsum8(dxc * xr)
        dc_ref[4] += _rsum8(dxc)

    rev = lambda i: nt - 1 - i
    tok = lambda w: pl.BlockSpec((tt, w), lambda i: (rev(i), 0))
    return pl.pallas_call(
        body, name=name, grid=(nt,),
        in_specs=[tok(2 * W), tok(W),
                  pl.BlockSpec((SUBLANES, W), lambda i: (jnp.maximum(rev(i) * sb - 1, 0), 0)),
                  tok(W), tok(4 * W), tok(W),
                  _resident((SUBLANES, W)), _resident((W, W)), _resident((W, W)), _resident((1, W))],
        out_specs=[pl.BlockSpec((tt, 2 * W), lambda i: (rev(i), 0)),
                   pl.BlockSpec((5, SUBLANES, W), lambda i: (0, 0, 0)),
                   pl.BlockSpec((W, W), lambda i: (0, 0)), pl.BlockSpec((W, W), lambda i: (0, 0)),
                   pl.BlockSpec((3, SUBLANES, W), lambda i: (0, 0, 0))],
        out_shape=[jax.ShapeDtypeStruct((T, z.shape[1]), BF16), jax.ShapeDtypeStruct((5, SUBLANES, W), F32),
                   jax.ShapeDtypeStruct((W, W), F32), jax.ShapeDtypeStruct((W, W), F32),
                   jax.ShapeDtypeStruct((3, SUBLANES, W), F32)],
        scratch_shapes=[pltpu.VMEM((tt, W), F32), pltpu.VMEM((tt, W), F32), pltpu.VMEM((tt, W), F32),
                        pltpu.VMEM((SUBLANES, W), F32), pltpu.VMEM((SUBLANES, W), F32), pltpu.VMEM((BF16_ROWS, W), F32)],
        compiler_params=_cp(1),
    )(z, hseq, hseq, a_all, saved, dmix, cw, wat, wxt, lam)


def _split3(x):
    hi = x.astype(BF16)
    r1 = x - hi.astype(F32)
    mid = r1.astype(BF16)
    lo = (r1 - mid.astype(F32)).astype(BF16)
    return hi, mid, lo


def _tri_matmul(tri, x):
    hi, mid, lo = _split3(x)
    return (jnp.dot(tri, hi, preferred_element_type=F32) + jnp.dot(tri, mid, preferred_element_type=F32)
            + jnp.dot(tri, lo, preferred_element_type=F32))


def _hg_chunk(q, fl, lb):
    C = q.shape[0]
    ri = lax.broadcasted_iota(jnp.int32, (C, C), 0)
    ci = lax.broadcasted_iota(jnp.int32, (C, C), 1)
    causal = ri >= ci
    sig = _sigmoid(fl)
    f = lb + (1.0 - lb) * sig
    k = 1.0 - f
    sq = _sigmoid(q)
    qf = q * sq
    b = _tri_matmul(causal.astype(BF16), jnp.log(f))
    bm = b[C // 2 - 1:C // 2, :]
    bl = b[C - 1:C, :]
    e_qt, e_kt, e_in, e_out = jnp.exp(b - bm), jnp.exp(bm - b), jnp.exp(b), jnp.exp(bl - b)
    qt = qf * e_qt
    kt = k * e_kt
    qin = qf * e_in
    kout = k * e_out
    qtb, ktb = qt.astype(BF16), kt.astype(BF16)
    att = [jnp.where(causal, _dot(qtb[:, _head(h)], ktb[:, _head(h)], NT), 0.0).astype(BF16) for h in range(HG_HEADS)]
    return dict(sig=sig, f=f, k=k, sq=sq, qf=qf, b=b, bm=bm, bl=bl, qt=qt, kt=kt, qin=qin, kout=kout, att=att,
                causal=causal, anti=ri <= ci, decay=jnp.exp(bl), e_qt=e_qt, e_kt=e_kt, e_in=e_in, e_out=e_out)


def _head(h):
    return slice(h * HG_D, (h + 1) * HG_D)


def _hgrn_fwd(z, lb, gn, name="hgrn_fwd", tt=512):
    T = z.shape[0]
    C = HG_CHUNK
    nc = tt // C
    Dh = HG_D
    Wd = HG_HEADS * Dh

    def body(q_ref, f_ref, v_ref, g_ref, lb_ref, gn_ref, o_ref, ss_ref, st):
        @pl.when(pl.program_id(0) == 0)
        def _():
            st[...] = jnp.zeros_like(st)

        S = [st[h] for h in range(HG_HEADS)]
        for c in range(nc):
            rows = slice(c * C, (c + 1) * C)
            ck = _hg_chunk(q_ref[rows, :].astype(F32), f_ref[rows, :].astype(F32), lb_ref[...])
            v = v_ref[rows, :]
            g = g_ref[rows, :].astype(F32)
            H = range(HG_HEADS)
            qinb, koutb = ck["qin"].astype(BF16), ck["kout"].astype(BF16)
            for h in H:
                ss_ref[h, c] = S[h]
            o = [_dot(ck["att"][h], v[:, _head(h)]) + _dot(qinb[:, _head(h)], S[h], NT) for h in H]
            S = [ck["decay"][:, _head(h)] * S[h] + _dot(v[:, _head(h)], koutb[:, _head(h)], TN) for h in H]
            outs = [o[h] * lax.rsqrt(jnp.mean(o[h] * o[h], axis=-1, keepdims=True) + EPS) * gn_ref[...] for h in H]
            o_ref[rows, :] = (jnp.concatenate(outs, axis=1) * (g * _sigmoid(g))).astype(BF16)
        for h in range(HG_HEADS):
            st[h] = S[h]

    col = lambda base: (lambda i: (i, base))
    return pl.pallas_call(
        body, name=name, grid=(T // tt,),
        in_specs=[pl.BlockSpec((tt, Wd), col(2)), pl.BlockSpec((tt, Wd), col(3)), pl.BlockSpec((tt, Wd), col(4)),
                  pl.BlockSpec((tt, Wd), col(5)), _resident((1, Wd)), _resident((1, Dh))],
        out_specs=[pl.BlockSpec((tt, Wd), lambda i: (i, 0)),
                   pl.BlockSpec((HG_HEADS, nc, Dh, Dh), lambda i: (0, i, 0, 0))],
        out_shape=[jax.ShapeDtypeStruct((T, Wd), BF16),
                   jax.ShapeDtypeStruct((HG_HEADS, T // C, Dh, Dh), F32)],
        scratch_shapes=[pltpu.VMEM((HG_HEADS, Dh, Dh), F32)],
        compiler_params=_cp(1),
    )(z, z, z, z, lb, gn)


def _hgrn_bwd(z, ss, dmix, lb, gn, dz, name="hgrn_bwd", tt=512):
    T = z.shape[0]
    C = HG_CHUNK
    nc = tt // C
    nt = T // tt
    Dh = HG_D
    Wd = HG_HEADS * Dh

    def body(q_ref, f_ref, v_ref, g_ref, ss_ref, dm_ref, lb_ref, gn_ref, dz01_ref, dz_ref, dlb_ref, dgn_ref, dst):
        @pl.when(pl.program_id(0) == 0)
        def _():
            dst[...] = jnp.zeros_like(dst)
            dlb_ref[...] = jnp.zeros_like(dlb_ref)
            dgn_ref[...] = jnp.zeros_like(dgn_ref)

        dS = [dst[h] for h in range(HG_HEADS)]
        lbv = lb_ref[...]
        gnv = gn_ref[...]
        rowc = lax.broadcasted_iota(jnp.int32, (C, Wd), 0)
        cat = lambda xs: jnp.concatenate(xs, axis=1)
        for c in reversed(range(nc)):
            rows = slice(c * C, (c + 1) * C)
            q = q_ref[rows, :].astype(F32)
            ck = _hg_chunk(q, f_ref[rows, :].astype(F32), lbv)
            v = v_ref[rows, :]
            g = g_ref[rows, :].astype(F32)
            dout = dm_ref[rows, :].astype(F32)
            sg, dsg = _silu_and_grad(g)
            d_ong = dout * sg
            H = range(HG_HEADS)
            qinb, koutb, qtb, ktb = (ck[n].astype(BF16) for n in ("qin", "kout", "qt", "kt"))
            S = [ss_ref[h, c] for h in H]
            Sb = [s.astype(BF16) for s in S]
            dSb = [d.astype(BF16) for d in dS]
            o = [_dot(ck["att"][h], v[:, _head(h)]) + _dot(qinb[:, _head(h)], Sb[h], NT) for h in H]
            rn = [lax.rsqrt(jnp.mean(o[h] * o[h], axis=-1, keepdims=True) + EPS) for h in H]
            on = [o[h] * rn[h] for h in H]
            don = [d_ong[:, _head(h)] * gnv for h in H]
            do = [(rn[h] * (don[h] - on[h] * jnp.mean(don[h] * on[h], axis=-1, keepdims=True))).astype(BF16) for h in H]
            datt = [jnp.where(ck["causal"], _dot(do[h], v[:, _head(h)], NT), 0.0).astype(BF16) for h in H]
            dvs = [_dot(ck["att"][h], do[h], TN) + _dot(koutb[:, _head(h)], dSb[h], NT) for h in H]
            dqins = [_dot(do[h], Sb[h]) for h in H]
            dkouts = [_dot(v[:, _head(h)], dSb[h]) for h in H]
            dqts = [_dot(datt[h], ktb[:, _head(h)]) for h in H]
            dkts = [_dot(datt[h], qtb[:, _head(h)], TN) for h in H]
            ddecays = [jnp.sum(dS[h] * S[h], axis=0, keepdims=True) for h in H]
            dS = [_dot(do[h], qinb[:, _head(h)], TN) + ck["decay"][:, _head(h)] * dS[h] for h in H]
            ons = [on[h] * gnv for h in H]
            dgn = _rsum8(d_ong[:, _head(0)] * on[0])
            for h in range(1, HG_HEADS):
                dgn = dgn + _rsum8(d_ong[:, _head(h)] * on[h])
            dgn_ref[...] += dgn
            dqt, dkt, dqin, dkout, ddecay = cat(dqts), cat(dkts), cat(dqins), cat(dkouts), cat(ddecays)
            dqf = dqt * ck["e_qt"] + dqin * ck["e_in"]
            dk = dkt * ck["e_kt"] + dkout * ck["e_out"]
            kk = dkout * ck["kout"]
            db = dqt * ck["qt"] - dkt * ck["kt"] + dqin * ck["qin"] - kk
            dbl = jnp.sum(kk, axis=0, keepdims=True) + ddecay * ck["decay"]
            db = db + jnp.where(rowc == C - 1, dbl, 0.0)
            dlogf = _tri_matmul(ck["anti"].astype(BF16), db)
            dfv = dlogf / ck["f"] - dk
            sig, sq = ck["sig"], ck["sq"]
            dlb_ref[...] += _rsum8(dfv * (1.0 - sig))
            dz_ref[rows, 2 * Wd:3 * Wd] = (dqf * (sq * (1.0 + q * (1.0 - sq)))).astype(BF16)
            dz_ref[rows, 3 * Wd:4 * Wd] = (dfv * (1.0 - lbv) * sig * (1.0 - sig)).astype(BF16)
            dz_ref[rows, 4 * Wd:5 * Wd] = cat(dvs).astype(BF16)
            dz_ref[rows, 5 * Wd:6 * Wd] = (dout * cat(ons) * dsg).astype(BF16)
        dz_ref[:, 0:2 * Wd] = dz01_ref[...]
        for h in range(HG_HEADS):
            dst[h] = dS[h]

    rev = lambda i: nt - 1 - i
    col = lambda base: (lambda i: (rev(i), base))
    return pl.pallas_call(
        body, name=name, grid=(nt,),
        in_specs=[pl.BlockSpec((tt, Wd), col(2)), pl.BlockSpec((tt, Wd), col(3)), pl.BlockSpec((tt, Wd), col(4)),
                  pl.BlockSpec((tt, Wd), col(5)),
                  pl.BlockSpec((HG_HEADS, nc, Dh, Dh), lambda i: (0, rev(i), 0, 0)),
                  pl.BlockSpec((tt, Wd), col(1)), _resident((1, Wd)), _resident((1, Dh)),
                  pl.BlockSpec((tt, 2 * Wd), col(0))],
        out_specs=[pl.BlockSpec((tt, 6 * Wd), lambda i: (rev(i), 0)), pl.BlockSpec((SUBLANES, Wd), lambda i: (0, 0)),
                   pl.BlockSpec((SUBLANES, Dh), lambda i: (0, 0))],
        out_shape=[jax.ShapeDtypeStruct((T, 6 * Wd), BF16), jax.ShapeDtypeStruct((SUBLANES, Wd), F32),
                   jax.ShapeDtypeStruct((SUBLANES, Dh), F32)],
        input_output_aliases={8: 0},
        scratch_shapes=[pltpu.VMEM((HG_HEADS, Dh, Dh), F32)],
        compiler_params=_cp(1),
    )(z, z, z, z, ss, dmix, lb, gn, dz)


def _sgu_core(p, lg_ref, lb_ref, wsc_ref, bsb_ref):
    Wd = D_MODEL
    G = SGU_CHUNK
    zz = _gelu(p)
    u = zz[:, :Wd]
    v = zz[:, Wd:]
    vc = v - jnp.mean(v, axis=-1, keepdims=True)
    rstd = lax.rsqrt(jnp.mean(vc * vc, axis=-1, keepdims=True) + EPS)
    vhat = vc * rstd
    vn = vhat * lg_ref[...] + lb_ref[...]
    svs = []
    for gi in range(SGU_G):
        svs.append(jnp.dot(wsc_ref[gi], vn[:, gi * G:(gi + 1) * G].astype(BF16), preferred_element_type=F32) + bsb_ref[gi])
    return u, vhat, rstd, vn, jnp.concatenate(svs, axis=1)


def _sgu_fwd(p1, lg, lbias, wsc, bsb, name="sgu_fwd", tt=512):
    T = p1.shape[0]
    Wd = D_MODEL
    C = SGU_CHUNK

    def body(p_ref, lg_ref, lb_ref, wsc_ref, bsb_ref, s_ref):
        for c in range(tt // C):
            rows = slice(c * C, (c + 1) * C)
            u, _, _, _, sv = _sgu_core(p_ref[rows, :].astype(F32), lg_ref, lb_ref, wsc_ref, bsb_ref)
            s_ref[rows, :] = (u * sv).astype(BF16)

    return pl.pallas_call(
        body, name=name, grid=(T // tt,),
        in_specs=[pl.BlockSpec((tt, 2 * Wd), lambda i: (i, 0)), _resident((1, Wd)), _resident((1, Wd)),
                  _resident((SGU_G, C, C)), _resident((SGU_G, C, C))],
        out_specs=pl.BlockSpec((tt, Wd), lambda i: (i, 0)),
        out_shape=jax.ShapeDtypeStruct((T, Wd), BF16),
        compiler_params=_cp(1),
    )(p1, lg, lbias, wsc, bsb)


def _sgu_bwd(p1, ds, lg, lbias, wsc, wsct, bsb, name="sgu_bwd", tt=512):
    T = p1.shape[0]
    Wd = D_MODEL
    C = SGU_CHUNK

    def body(p_ref, ds_ref, lg_ref, lb_ref, wsc_ref, wsct_ref, bsb_ref, dp_ref, dws_ref, dbs_ref, dlg_ref, dlb_ref, dbin_ref):
        @pl.when(pl.program_id(0) == 0)
        def _():
            dws_ref[...] = jnp.zeros_like(dws_ref)
            dbs_ref[...] = jnp.zeros_like(dbs_ref)
            dlg_ref[...] = jnp.zeros_like(dlg_ref)
            dlb_ref[...] = jnp.zeros_like(dlb_ref)
            dbin_ref[...] = jnp.zeros_like(dbin_ref)

        for c in range(tt // C):
            rows = slice(c * C, (c + 1) * C)
            p = p_ref[rows, :].astype(F32)
            u, vhat, rstd, vn, sv = _sgu_core(p, lg_ref, lb_ref, wsc_ref, bsb_ref)
            dsc = ds_ref[rows, :].astype(F32)
            du = dsc * sv
            dsv = dsc * u
            dvns = []
            for gi in range(SGU_G):
                cs = slice(gi * C, (gi + 1) * C)
                dsv_g = dsv[:, cs]
                dvns.append(jnp.dot(wsct_ref[gi], dsv_g.astype(BF16), preferred_element_type=F32))
                dws_ref[gi] += _dot(dsv_g, vn[:, cs], NT)
                dbs_ref[gi] += dsv_g
            dvn = jnp.concatenate(dvns, axis=1)
            dlg_ref[...] += _rsum8(dvn * vhat)
            dlb_ref[...] += _rsum8(dvn)
            dvh = dvn * lg_ref[...]
            dv = rstd * (dvh - jnp.mean(dvh, axis=-1, keepdims=True) - vhat * jnp.mean(dvh * vhat, axis=-1, keepdims=True))
            dp = jnp.concatenate([du, dv], axis=1) * _gelu_grad(p)
            dbin_ref[...] += _rsum8(dp)
            dp_ref[rows, :] = dp.astype(BF16)

    full3 = pl.BlockSpec((SGU_G, C, C), lambda i: (0, 0, 0))
    return pl.pallas_call(
        body, name=name, grid=(T // tt,),
        in_specs=[pl.BlockSpec((tt, 2 * Wd), lambda i: (i, 0)), pl.BlockSpec((tt, Wd), lambda i: (i, 0)),
                  _resident((1, Wd)), _resident((1, Wd)), _resident((SGU_G, C, C)), _resident((SGU_G, C, C)),
                  _resident((SGU_G, C, C))],
        out_specs=[pl.BlockSpec((tt, 2 * Wd), lambda i: (i, 0)), full3, full3,
                   pl.BlockSpec((SUBLANES, Wd), lambda i: (0, 0)), pl.BlockSpec((SUBLANES, Wd), lambda i: (0, 0)),
                   pl.BlockSpec((SUBLANES, 2 * Wd), lambda i: (0, 0))],
        out_shape=[jax.ShapeDtypeStruct((T, 2 * Wd), BF16), jax.ShapeDtypeStruct((SGU_G, C, C), F32),
                   jax.ShapeDtypeStruct((SGU_G, C, C), F32), jax.ShapeDtypeStruct((SUBLANES, Wd), F32),
                   jax.ShapeDtypeStruct((SUBLANES, Wd), F32), jax.ShapeDtypeStruct((SUBLANES, 2 * Wd), F32)],
        compiler_params=_cp(1),
    )(p1, ds, lg, lbias, wsc, wsct, bsb)


def _pad_rows(w, rows=SUBLANES):
    return jnp.pad(w, ((0, rows - w.shape[0]), (0, 0)))


def _block_diag(w):
    n, b, _ = w.shape
    return (w[:, :, None, :] * jnp.eye(n, dtype=w.dtype)[:, None, :, None]).reshape(n * b, n * b)


def _diag_blocks(m, n):
    b = m.shape[0] // n
    m4 = m.reshape(n, b, n, b)
    return jnp.stack([m4[k, :, k, :] for k in range(n)], axis=0)


def _piece_major(dw):
    if dw.ndim == 2:
        K, N = dw.shape
        return dw.reshape(N_CHIPS, 2, K // (2 * N_CHIPS), N)
    _, K, ns = dw.shape
    return dw.reshape(N_CHIPS, 2, K // 2, ns)


def _ffn_fwd(h, g, w_up, cw, cb, w_down, tag, loss=None):
    hn, gu = _norm_mm(h, g, w_up, jnp.zeros((1, 2 * D_FF), F32), name=f"ffn_up_{tag}")
    a, gc, *out = _ffn_act(gu, cw, cb, w_down, h, name=f"ffn_act_down_{tag}", loss=loss)
    return (out[0] if loss is None else out), (hn, gu, gc, a)


def _ffn_bwd(dh, h, g, saved, w_up, cw, w_down, tag):
    hn, gu, gc, a = saved
    dwd = _mm_tn(a, dh, name=f"ffn_dwd_{tag}")
    dgu, dc, dhin, dg8 = _ffn_act_bwd(gu, gc, cw, w_up, w_down, h, g, dh, name=f"ffn_bwd_{tag}")
    dwu = _mm_tn(hn, dgu, name=f"ffn_dwu_{tag}", col_shards=N_CHIPS)
    dcs = dc.sum(axis=1)
    return dhin, dg8.sum(axis=0), dwu, dcs[0:3], dcs[3], dwd


REDUCE_GROUPS = {"g1": [("ffn_w_up", 1), ("ffn_w_down", 1), ("od_w_out", 0), ("od_w_in", 0)],
                 "g2": [("ffn_w_up", 0), ("ffn_w_down", 0)],
                 "g3": [("ev_w_out", 0), ("ev_w_in", 0)]}


def _local_step(x, tgt, p, start_reduce=None, continue_reduce=None):
    row = lambda v: v.reshape(1, -1)
    grads = {}

    lower = jax.nn.softmax(p["hg_lb_logits"], axis=0)
    lb0 = row(lower[0])
    ev_cw = _pad_rows(p["ev_conv_w"][0])
    ev_cb = row(p["ev_conv_b"][0])
    wa = _block_diag(p["ev_gate_a_w"][0]).astype(BF16)
    wx = _block_diag(p["ev_gate_x_w"][0]).astype(BF16)
    ba, bx, lam = row(p["ev_gate_a_b"][0]), row(p["ev_gate_x_b"][0]), row(p["ev_lru_lambda"][0])
    gn = row(p["ev_hg_norm"][0])
    tril = jnp.tril(jnp.ones((SGU_CHUNK, SGU_CHUNK), F32))
    wsc = (p["od_w_s"][0] * tril).astype(BF16)
    bsb = jnp.broadcast_to(p["od_b_s"][0][:, :, None], (SGU_G, SGU_CHUNK, SGU_CHUNK)).astype(F32)
    ffn_cw = [_pad_rows(p["ffn_conv_w"][l]) for l in range(2)]
    ffn_cb = [row(p["ffn_conv_b"][l]) for l in range(2)]
    ev_w_in, ev_w_out = p["ev_w_in"][0], p["ev_w_out"][0]
    nm = [row(p["norm_mix"][l]) for l in range(2)]
    nf = [row(p["norm_ffn"][l]) for l in range(2)]

    h0 = x
    hn0, z0 = _norm_mm(h0, nm[0], ev_w_in, jnp.zeros((1, ev_w_in.shape[1]), F32), name="ev_in")
    out_a, hseq, a_all, lru_saved = _lru_fwd(z0, ev_cw, ev_cb, wa, ba, wx, bx, lam)
    out_b, ss = _hgrn_fwd(z0, lb0, gn)
    mix0 = jnp.concatenate([out_a, out_b], axis=1)
    h1 = _mm(mix0, ev_w_out, h0, F32, name="ev_out")
    late = p["late"](h1) if "late" in p else p
    od_w_in, od_w_out = late["od_w_in"][0], late["od_w_out"][0]
    w_up = [(late["ffn_w_up"], l) for l in range(2)]
    w_down = [(late["ffn_w_down"], l) for l in range(2)]
    h2, ffn0 = _ffn_fwd(h1, nf[0], w_up[0], ffn_cw[0], ffn_cb[0], w_down[0], "l0")
    hn1, p1 = _norm_mm(h2, nm[1], od_w_in, row(p["od_b_in"][0]), name="od_in")
    s1 = _sgu_fwd(p1, row(p["od_ln_g"][0]), row(p["od_ln_b"][0]), wsc, bsb)
    h3 = _mm(s1, od_w_out, h2, F32, name="od_out")
    (dh4, sq8, dgf8), ffn1 = _ffn_fwd(h3, nf[1], w_up[1], ffn_cw[1], ffn_cb[1], w_down[1], "l1", loss=(row(p["norm_final"]), tgt))
    grads["norm_final"] = dgf8.sum(axis=0)

    big = {}
    dh3, dnf1, dwu1, dcw1, dcb1, dwd1 = _ffn_bwd(dh4, h3, nf[1], ffn1, w_up[1], ffn_cw[1], w_down[1], "l1")
    big["ffn_w_up", 1], big["ffn_w_down", 1] = _piece_major(dwu1), _piece_major(dwd1)
    ds1 = _mm(dh3, od_w_out, None, BF16, name="od_ds", transpose_w=True)
    big["od_w_out", 0] = _piece_major(_mm_tn(s1, dh3, name="od_dwo"))
    wsct = jnp.swapaxes(wsc, 1, 2)
    dp1, dws, dbs, dlg8, dlb8, dbin8 = _sgu_bwd(p1, ds1, row(p["od_ln_g"][0]), row(p["od_ln_b"][0]), wsc, wsct, bsb)
    grads["od_w_s"] = (dws * tril)[None]
    grads["od_b_s"] = dbs.sum(axis=-1)[None]
    grads["od_ln_g"] = dlg8.sum(axis=0)[None]
    grads["od_ln_b"] = dlb8.sum(axis=0)[None]
    grads["od_b_in"] = dbin8.sum(axis=0)[None]
    dh2, dnm1 = _mm_normbwd(dp1, od_w_in, h2, nm[1], dh3, name="od_dh")
    big["od_w_in", 0] = _piece_major(_mm_tn(hn1, dp1, name="od_dwi", col_shards=N_CHIPS))
    if start_reduce is not None:
        token = start_reduce("g1", [big[key] for key in REDUCE_GROUPS["g1"]], split=True)
        ffn_cw[0] = ffn_cw[0] + token[0:1, 0:1]

    dh1, dnf0, dwu0, dcw0, dcb0, dwd0 = _ffn_bwd(dh2, h1, nf[0], ffn0, w_up[0], ffn_cw[0], w_down[0], "l0")
    big["ffn_w_up", 0], big["ffn_w_down", 0] = _piece_major(dwu0), _piece_major(dwd0)
    if start_reduce is not None:
        token = continue_reduce("g1", dh1) + start_reduce("g2", [big[key] for key in REDUCE_GROUPS["g2"]], split=True)
        lam = lam + token[0:1, 0:1]
    dmix = _mm(dh1, ev_w_out, None, BF16, name="ev_dmix", transpose_w=True)
    big["ev_w_out", 0] = _piece_major(_mm_tn(mix0, dh1, name="ev_dwo"))
    dz01, dc5, dwa, dwx, dvec = _lru_bwd(z0, hseq, a_all, lru_saved, dmix, ev_cw, wa.T, wx.T, lam)
    if start_reduce is not None:
        lb0 = lb0 + continue_reduce("g2", dc5)[0:1, 0:1]
    dz0, dlb8, dgn8 = _hgrn_bwd(z0, ss, dmix, lb0, gn, dz01)
    big["ev_w_in", 0] = _piece_major(_mm_tn(hn0, dz0, name="ev_dwi", col_shards=N_CHIPS))
    if start_reduce is not None:
        token = start_reduce("g3", [big[key] for key in REDUCE_GROUPS["g3"]])
        nm[0] = nm[0] + token[0:1, 0:1]
    grad_x, dnm0 = _mm_normbwd(dz0, ev_w_in, h0, nm[0], dh1, name="ev_dh")

    dc5s = dc5.sum(axis=1)
    grads["ev_conv_w"] = dc5s[0:4][None]
    grads["ev_conv_b"] = dc5s[4][None]
    grads["ev_gate_a_w"] = _diag_blocks(dwa, LRU_BLOCKS)[None]
    grads["ev_gate_x_w"] = _diag_blocks(dwx, LRU_BLOCKS)[None]
    dvs = dvec.sum(axis=1)
    grads["ev_gate_a_b"] = dvs[0][None]
    grads["ev_gate_x_b"] = dvs[1][None]
    grads["ev_lru_lambda"] = (dvs[2] * (-jax.nn.sigmoid(-p["ev_lru_lambda"][0])))[None]
    dlb = dlb8.sum(axis=0)
    grads["hg_lb_logits"] = dlb[None, :] * lower[0][None, :] * (jnp.eye(3, dtype=F32)[0][:, None] - lower)
    grads["ev_hg_norm"] = dgn8.sum(axis=0)[None]
    grads["norm_mix"] = jnp.stack([dnm0.sum(axis=0), dnm1.sum(axis=0)])
    grads["norm_ffn"] = jnp.stack([dnf0, dnf1])
    grads["ffn_conv_w"] = jnp.stack([dcw0, dcw1])
    grads["ffn_conv_b"] = jnp.stack([dcb0, dcb1])
    return sq8, grad_x, grads, big


MESH = pl.DeviceIdType.MESH
ANY = pl.BlockSpec(memory_space=pl.ANY)
N_CHIPS = 4
N_DEV = 8

SH_BIG = {"ev_w_in": 2, "ev_w_out": 1, "od_w_in": 2, "od_w_out": 1, "ffn_w_up": 2, "ffn_w_down": 1}
SH_SMALL = {"ev_conv_w": 2, "od_b_in": 1, "od_ln_g": 1, "od_ln_b": 1, "ffn_conv_w": 2}
REP = ["norm_mix", "norm_ffn", "norm_final", "ev_conv_b", "ev_gate_a_w", "ev_gate_a_b", "ev_gate_x_w", "ev_gate_x_b",
       "ev_lru_lambda", "hg_lb_logits", "ev_hg_norm", "od_w_s", "od_b_s", "ffn_conv_b"]
WEIGHTS = ["norm_mix", "norm_ffn", "norm_final", "ev_w_in", "ev_conv_w", "ev_conv_b", "ev_gate_a_w", "ev_gate_a_b", "ev_gate_x_w",
           "ev_gate_x_b", "ev_lru_lambda", "hg_lb_logits", "ev_hg_norm", "ev_w_out", "od_w_in", "od_b_in", "od_ln_g", "od_ln_b",
           "od_w_s", "od_b_s", "od_w_out", "ffn_w_up", "ffn_conv_w", "ffn_conv_b", "ffn_w_down"]


def _rows(n_elems, mult=SUBLANES):
    r = -(-n_elems // LANES)
    return -(-r // mult) * mult


def _pack(arrs, rows, dtype):
    flat = jnp.concatenate([a.reshape(-1).astype(dtype) for a in arrs])
    return jnp.pad(flat, (0, rows * LANES - flat.shape[0])).reshape(rows, LANES)


def _unpack(flat2d, shapes):
    flat = flat2d.reshape(-1)
    out, off = [], 0
    for s in shapes:
        n = 1
        for d in s:
            n *= d
        out.append(flat[off:off + n].reshape(s))
        off += n
    return out


def _mesh_pos():
    return lax.axis_index("x"), lax.axis_index("y"), lax.axis_index("c")


def _other_chips(x, y):
    return [(1 - x, y), (x, 1 - y), (1 - x, 1 - y)]


def _half_rows(n):
    return lambda r, c: r.at[0, pl.ds(c * (n // 2), n // 2), :]


GATHER_BIG = {
    "ev_w_in": ((1024, 3072), _half_rows(1024), lambda o, k, c: o.at[pl.ds(c * 512, 512), pl.ds(k * 768, 768)]),
    "ev_w_out": ((1024, 1024), _half_rows(256), lambda o, k, c: o.at[pl.ds(k * 256 + c * 128, 128), :]),
    "od_w_in": ((1024, 2048), _half_rows(1024), lambda o, k, c: o.at[pl.ds(c * 512, 512), pl.ds(k * 512, 512)]),
    "od_w_out": ((1024, 1024), _half_rows(256), lambda o, k, c: o.at[pl.ds(k * 256 + c * 128, 128), :]),
    "ffn_w_up": ((2, 1024, 2 * D_FF), lambda r, c: r.at[c], lambda o, k, c: o.at[c, :, pl.ds(k * (2 * D_FF // 4), 2 * D_FF // 4)]),
    "ffn_w_down": ((2, D_FF, 1024), lambda r, c: r.at[c], lambda o, k, c: o.at[c, pl.ds(k * (D_FF // 4), D_FF // 4), :]),
}


def _gather_weights(names, big, small):
    nb = len(big)
    descs = [GATHER_BIG[n] for n in names]
    rs = small.shape[0]

    def body(*refs):
        ins, s_ref = refs[:nb], refs[nb]
        outs, os_ref = refs[nb + 1:2 * nb + 1], refs[2 * nb + 1]
        ici_send, ici_recv, d2d_send, d2d_recv, loc_sems = refs[2 * nb + 2:2 * nb + 7]
        vbufs = refs[2 * nb + 7:]
        x, y, c = _mesh_pos()
        k = 2 * x + y
        chips = _other_chips(x, y)
        sib = (x, y, 1 - c)

        def remote(src, dst, ssem, rsem, to):
            return pltpu.make_async_remote_copy(src_ref=src, dst_ref=dst, send_sem=ssem, recv_sem=rsem, device_id=to,
                                                device_id_type=MESH)

        stage = [pltpu.make_async_copy(ins[t], vbufs[t], loc_sems.at[2 * t]) for t in range(nb)]
        stage.append(pltpu.make_async_copy(s_ref, vbufs[nb], loc_sems.at[2 * nb]))
        for cp in stage:
            cp.start()
        sends = []
        for t, (_, src, dst) in enumerate(descs):
            for j, (px, py) in enumerate(chips):
                sends.append(remote(src(ins[t], c), dst(outs[t], k, c), ici_send.at[3 * t + j], ici_recv.at[3 * t + j], (px, py, c)))
        for j, (px, py) in enumerate(chips):
            sends.append(remote(s_ref, os_ref.at[k], ici_send.at[3 * nb + j], ici_recv.at[3 * nb + j], (px, py, c)))
        for cp in sends:
            cp.start()
        for cp in stage:
            cp.wait()
        local = []
        for t, (_, src, dst) in enumerate(descs):
            for cc in (0, 1):
                local.append(pltpu.make_async_copy(src(vbufs[t], cc), dst(outs[t], k, cc), loc_sems.at[2 * t + cc]))
        local.append(pltpu.make_async_copy(vbufs[nb], os_ref.at[k], loc_sems.at[2 * nb]))
        for cp in local:
            cp.start()
        for t, (_, src, dst) in enumerate(descs):
            for j, (px, py) in enumerate(chips):
                got = dst(outs[t], 2 * px + py, c)
                remote(got, got, ici_send.at[3 * t + j], ici_recv.at[3 * t + j], (px, py, c)).wait_recv()
                fwd = remote(got, got, d2d_send.at[3 * t + j], d2d_recv.at[3 * t + j], sib)
                fwd.start()
                sends.append(fwd)
        for j, (px, py) in enumerate(chips):
            remote(s_ref, os_ref.at[2 * px + py], ici_send.at[3 * nb + j], ici_recv.at[3 * nb + j], (px, py, c)).wait_recv()
        for t, (_, src, dst) in enumerate(descs):
            for j, (px, py) in enumerate(chips):
                theirs = dst(outs[t], 2 * px + py, 1 - c)
                remote(theirs, theirs, d2d_send.at[3 * t + j], d2d_recv.at[3 * t + j], sib).wait_recv()
        for cp in sends:
            cp.wait_send()
        for cp in local:
            cp.wait()

    out_shape = [jax.ShapeDtypeStruct(d[0], BF16) for d in descs] + [jax.ShapeDtypeStruct((N_CHIPS, rs, LANES), small.dtype)]
    return pl.pallas_call(
        body, name="gather_weights", in_specs=[ANY] * (nb + 1), out_specs=[ANY] * (nb + 1), out_shape=out_shape,
        scratch_shapes=[pltpu.SemaphoreType.DMA((3 * nb + 3,)), pltpu.SemaphoreType.DMA((3 * nb + 3,)),
                        pltpu.SemaphoreType.DMA((3 * nb,)), pltpu.SemaphoreType.DMA((3 * nb,)),
                        pltpu.SemaphoreType.DMA((2 * nb + 1,))]
        + [pltpu.VMEM(b.shape, b.dtype) for b in big] + [pltpu.VMEM(small.shape, small.dtype)],
        compiler_params=pltpu.CompilerParams(vmem_limit_bytes=VMEM_LIMIT),
    )(*big, small)


def _place_own(names, big):
    nb = len(big)
    descs = [GATHER_BIG[n] for n in names]

    def body(*refs):
        ins, outs = refs[:nb], refs[nb:2 * nb]
        sems, vbufs = refs[2 * nb], refs[2 * nb + 1:]
        x, y, c = _mesh_pos()
        k = 2 * x + y
        stage = [pltpu.make_async_copy(ins[t], vbufs[t], sems.at[2 * t]) for t in range(nb)]
        for cp in stage:
            cp.start()
        for cp in stage:
            cp.wait()
        local = [pltpu.make_async_copy(src(vbufs[t], cc), dst(outs[t], k, cc), sems.at[2 * t + cc])
                 for t, (_, src, dst) in enumerate(descs) for cc in (0, 1)]
        for cp in local:
            cp.start()
        for cp in local:
            cp.wait()

    return pl.pallas_call(
        body, name="place_own", in_specs=[ANY] * nb, out_specs=[ANY] * nb,
        out_shape=[jax.ShapeDtypeStruct(d[0], BF16) for d in descs],
        scratch_shapes=[pltpu.SemaphoreType.DMA((2 * nb,))] + [pltpu.VMEM(b.shape, b.dtype) for b in big],
        compiler_params=pltpu.CompilerParams(vmem_limit_bytes=VMEM_LIMIT),
    )(*big)


def _gather_start(names, big, bufs):
    nb = len(big)
    descs = [GATHER_BIG[n] for n in names]

    def body(*refs):
        ins, lnd = refs[:nb], refs[nb:2 * nb]
        send_sems, recv_sems, token = refs[2 * nb], refs[2 * nb + 1], refs[-1]
        x, y, c = _mesh_pos()
        k = 2 * x + y
        for t, (_, src, dst) in enumerate(descs):
            for j, (px, py) in enumerate(_other_chips(x, y)):
                _remote(src(ins[t], c), dst(lnd[t], k, c), send_sems.at[3 * t + j], recv_sems.at[3 * t + j], (px, py, c)).start()
        token[...] = jnp.zeros_like(token)

    out = pl.pallas_call(
        body, name="gather_start",
        out_shape=(pltpu.SemaphoreType.DMA((3 * nb,)), pltpu.SemaphoreType.DMA((3 * nb,)),
                   *[pltpu.HBM(b.shape, b.dtype) for b in big], *[pltpu.HBM(b.shape, b.dtype) for b in bufs],
                   jax.ShapeDtypeStruct((SUBLANES, LANES), F32)),
        in_specs=[HBM] * (2 * nb), out_specs=(SEM, SEM, *[HBM] * (2 * nb), pl.BlockSpec(memory_space=pltpu.VMEM)),
        input_output_aliases={i: 2 + i for i in range(2 * nb)},
        compiler_params=pltpu.CompilerParams(has_side_effects=DATAFLOW),
    )(*[pltpu.with_memory_space_constraint(b, pltpu.HBM) for b in big], *[pltpu.with_memory_space_constraint(b, pltpu.HBM) for b in bufs])
    return out[0], out[1], list(out[2:2 + nb]), list(out[2 + nb:2 + 2 * nb]), out[-1]


def _gather_wait(names, send_sems, recv_sems, big, bufs, after):
    nb = len(big)
    descs = [GATHER_BIG[n] for n in names]

    def body(*refs):
        ins, lnd = refs[:nb], refs[nb:2 * nb]
        ssem, rsem = refs[2 * nb], refs[2 * nb + 1]
        x, y, c = _mesh_pos()
        for t, (_, src, dst) in enumerate(descs):
            for j, (px, py) in enumerate(_other_chips(x, y)):
                cp = _remote(src(ins[t], c), dst(lnd[t], 2 * px + py, c), ssem.at[3 * t + j], rsem.at[3 * t + j], (px, py, c))
                cp.wait_send()
                cp.wait_recv()

    out = pl.pallas_call(
        body, name="gather_wait",
        out_shape=(*[pltpu.HBM(b.shape, b.dtype) for b in big], *[pltpu.HBM(b.shape, b.dtype) for b in bufs]),
        in_specs=[HBM] * (2 * nb) + [SEM, SEM, ANY], out_specs=tuple([HBM] * (2 * nb)),
        input_output_aliases={i: i for i in range(2 * nb)},
        compiler_params=pltpu.CompilerParams(has_side_effects=DATAFLOW),
    )(*big, *bufs, send_sems, recv_sems, after)
    return list(out[nb:])


def _gather_forward(names, bufs):
    nb = len(bufs)
    descs = [GATHER_BIG[n] for n in names]

    def body(*refs):
        outs = refs[nb:2 * nb]
        send_sems, recv_sems = refs[2 * nb:]
        x, y, c = _mesh_pos()
        sib = (x, y, 1 - c)
        sends = []
        for t, (_, src, dst) in enumerate(descs):
            for j, (px, py) in enumerate(_other_chips(x, y)):
                got = dst(outs[t], 2 * px + py, c)
                sends.append(_remote(got, got, send_sems.at[3 * t + j], recv_sems.at[3 * t + j], sib))
        for cp in sends:
            cp.start()
        for t, (_, src, dst) in enumerate(descs):
            for j, (px, py) in enumerate(_other_chips(x, y)):
                theirs = dst(outs[t], 2 * px + py, 1 - c)
                _remote(theirs, theirs, send_sems.at[3 * t + j], recv_sems.at[3 * t + j], sib).wait_recv()
        for cp in sends:
            cp.wait_send()

    return pl.pallas_call(
        body, name="gather_forward", in_specs=[ANY] * nb, out_specs=[ANY] * nb,
        out_shape=[jax.ShapeDtypeStruct(b.shape, b.dtype) for b in bufs], input_output_aliases={t: t for t in range(nb)},
        scratch_shapes=[pltpu.SemaphoreType.DMA((3 * nb,)), pltpu.SemaphoreType.DMA((3 * nb,))],
    )(*bufs)


def _remote(src, dst, ssem, rsem, to):
    return pltpu.make_async_remote_copy(src_ref=src, dst_ref=dst, send_sem=ssem, recv_sem=rsem, device_id=to, device_id_type=MESH)


def _rs_send_sibling(gs, tag):
    n = len(gs)
    counts = [N_CHIPS if g.ndim == 4 else 1 for g in gs]
    ns = sum(counts)

    def body(*refs):
        ins, outs = refs[:n], refs[n:2 * n]
        send_sems, recv_sems = refs[2 * n:]
        x, y, c = _mesh_pos()
        cps, s = [], 0
        for t in range(n):
            if counts[t] == 1:
                cps.append(_remote(ins[t].at[1 - c], outs[t], send_sems.at[s], recv_sems.at[s], (x, y, 1 - c)))
                s += 1
            else:
                for k in range(N_CHIPS):
                    cps.append(_remote(ins[t].at[k, 1 - c], outs[t].at[k], send_sems.at[s], recv_sems.at[s], (x, y, 1 - c)))
                    s += 1
        for cp in cps:
            cp.start()
        for cp in cps:
            cp.wait()

    out_shape = [jax.ShapeDtypeStruct(g.shape[:1] + g.shape[2:] if g.ndim == 4 else g.shape[1:], g.dtype) for g in gs]
    return pl.pallas_call(
        body, name=f"rs_send_sibling_{tag}", in_specs=[ANY] * n, out_specs=[ANY] * n, out_shape=out_shape,
        scratch_shapes=[pltpu.SemaphoreType.DMA((ns,)), pltpu.SemaphoreType.DMA((ns,))],
    )(*gs)


def _add_piece(g, recv, c, name):
    P, Q = g.shape[-2:]

    def body(c_ref, g_ref, r_ref, o_ref):
        o_ref[...] = g_ref[...].reshape(o_ref.shape) + r_ref[...]

    if g.ndim == 4:
        grid = (N_CHIPS,)
        in_specs = [pl.BlockSpec((1, 1, P, Q), lambda k, c_ref: (k, c_ref[0], 0, 0)), pl.BlockSpec((1, P, Q), lambda k, c_ref: (k, 0, 0))]
        out_spec = pl.BlockSpec((1, P, Q), lambda k, c_ref: (k, 0, 0))
    else:
        grid = (1,)
        in_specs = [pl.BlockSpec((1, P, Q), lambda k, c_ref: (c_ref[0], 0, 0)), pl.BlockSpec((P, Q), lambda k, c_ref: (0, 0))]
        out_spec = pl.BlockSpec((P, Q), lambda k, c_ref: (0, 0))
    return pl.pallas_call(
        body, name=name,
        grid_spec=pltpu.PrefetchScalarGridSpec(num_scalar_prefetch=1, grid=grid, in_specs=in_specs, out_specs=out_spec),
        out_shape=jax.ShapeDtypeStruct(recv.shape, g.dtype),
        compiler_params=_cp(1),
    )(c, g, recv)


HBM = pl.BlockSpec(memory_space=pltpu.HBM)
SEM = pl.BlockSpec(memory_space=pltpu.SEMAPHORE)
DATAFLOW = pltpu.SideEffectType.DATAFLOW_SIDE_EFFECTING


def _sibling_copies(gs, srcs, dsts, send_sems, recv_sems):
    x, y, c = _mesh_pos()
    cps, s = [], 0
    for t, g in enumerate(gs):
        if g.ndim == 4:
            for k in range(N_CHIPS):
                cps.append(_remote(srcs[t].at[k, 1 - c], dsts[t].at[k], send_sems.at[s], recv_sems.at[s], (x, y, 1 - c)))
                s += 1
        else:
            cps.append(_remote(srcs[t].at[1 - c], dsts[t], send_sems.at[s], recv_sems.at[s], (x, y, 1 - c)))
            s += 1
    return cps


def _sibling_start(gs, tag):
    n = len(gs)
    ns = sum(N_CHIPS if g.ndim == 4 else 1 for g in gs)
    lands = [pltpu.with_memory_space_constraint(lax.empty(g.shape[:1] + g.shape[2:] if g.ndim == 4 else g.shape[1:], g.dtype), pltpu.HBM)
             for g in gs]

    def body(*refs):
        for cp in _sibling_copies(gs, refs[:n], refs[n:2 * n], refs[2 * n], refs[2 * n + 1]):
            cp.start()
        refs[-1][...] = jnp.zeros_like(refs[-1])

    out = pl.pallas_call(
        body, name=f"sibling_start_{tag}",
        out_shape=(pltpu.SemaphoreType.DMA((ns,)), pltpu.SemaphoreType.DMA((ns,)),
                   *[pltpu.HBM(g.shape, g.dtype) for g in gs], *[pltpu.HBM(l.shape, l.dtype) for l in lands],
                   jax.ShapeDtypeStruct((SUBLANES, LANES), F32)),
        in_specs=[HBM] * (2 * n), out_specs=(SEM, SEM, *[HBM] * (2 * n), pl.BlockSpec(memory_space=pltpu.VMEM)),
        input_output_aliases={i: 2 + i for i in range(2 * n)},
        compiler_params=pltpu.CompilerParams(has_side_effects=DATAFLOW),
    )(*[pltpu.with_memory_space_constraint(g, pltpu.HBM) for g in gs], *lands)
    return (out[0], out[1], list(out[2:2 + n]), list(out[2 + n:2 + 2 * n])), out[-1]


def _sibling_wait(send_sems, recv_sems, gs, lands, after, tag):
    n = len(gs)

    def body(*refs):
        for cp in _sibling_copies(gs, refs[:n], refs[n:2 * n], refs[2 * n], refs[2 * n + 1]):
            cp.wait_send()
            cp.wait_recv()

    out = pl.pallas_call(
        body, name=f"sibling_wait_{tag}",
        out_shape=(*[pltpu.HBM(g.shape, g.dtype) for g in gs], *[pltpu.HBM(l.shape, l.dtype) for l in lands]),
        in_specs=[HBM] * (2 * n) + [SEM, SEM, ANY], out_specs=tuple([HBM] * (2 * n)),
        input_output_aliases={i: i for i in range(2 * n)},
        compiler_params=pltpu.CompilerParams(has_side_effects=DATAFLOW),
    )(*gs, *lands, send_sems, recv_sems, after)
    return list(out[:n]), list(out[n:])


def _chips_start(hs, tag):
    n = len(hs)
    lands = [pltpu.with_memory_space_constraint(lax.empty((N_CHIPS,) + h.shape[-2:], h.dtype), pltpu.HBM) for h in hs]

    def body(*refs):
        ins, lnd = refs[:n], refs[n:2 * n]
        send_sems, recv_sems, token = refs[2 * n], refs[2 * n + 1], refs[-1]
        x, y, c = _mesh_pos()
        k = 2 * x + y
        piece = lambda t, kk: ins[t].at[kk] if hs[t].ndim == 3 else ins[t]
        for t in range(n):
            for j, (px, py) in enumerate(_other_chips(x, y)):
                _remote(piece(t, 2 * px + py), lnd[t].at[k], send_sems.at[3 * t + j], recv_sems.at[3 * t + j], (px, py, c)).start()
        token[...] = jnp.zeros_like(token)

    out = pl.pallas_call(
        body, name=f"chips_start_{tag}",
        out_shape=(pltpu.SemaphoreType.DMA((3 * n,)), pltpu.SemaphoreType.DMA((3 * n,)),
                   *[pltpu.HBM(h.shape, h.dtype) for h in hs], *[pltpu.HBM(l.shape, l.dtype) for l in lands],
                   jax.ShapeDtypeStruct((SUBLANES, LANES), F32)),
        in_specs=[HBM] * (2 * n), out_specs=(SEM, SEM, *[HBM] * (2 * n), pl.BlockSpec(memory_space=pltpu.VMEM)),
        input_output_aliases={i: 2 + i for i in range(2 * n)},
        compiler_params=pltpu.CompilerParams(has_side_effects=DATAFLOW),
    )(*[pltpu.with_memory_space_constraint(h, pltpu.HBM) for h in hs], *lands)
    return out[0], out[1], list(out[2:2 + n]), list(out[2 + n:2 + 2 * n]), out[-1]


def _chips_wait(send_sems, recv_sems, hs, lands, after, tag):
    n = len(hs)

    def body(*refs):
        ins, lnd = refs[:n], refs[n:2 * n]
        ssem, rsem = refs[2 * n], refs[2 * n + 1]
        x, y, c = _mesh_pos()
        k = 2 * x + y
        piece = lambda t, kk: ins[t].at[kk] if hs[t].ndim == 3 else ins[t]
        for t in range(n):
            for j, (px, py) in enumerate(_other_chips(x, y)):
                cp = _remote(piece(t, k), lnd[t].at[2 * px + py], ssem.at[3 * t + j], rsem.at[3 * t + j], (px, py, c))
                cp.wait_send()
                cp.wait_recv()

    out = pl.pallas_call(
        body, name=f"chips_wait_{tag}",
        out_shape=(*[pltpu.HBM(h.shape, h.dtype) for h in hs], *[pltpu.HBM(l.shape, l.dtype) for l in lands]),
        in_specs=[HBM] * (2 * n) + [SEM, SEM, ANY], out_specs=tuple([HBM] * (2 * n)),
        input_output_aliases={i: i for i in range(2 * n)},
        compiler_params=pltpu.CompilerParams(has_side_effects=DATAFLOW),
    )(*hs, *lands, send_sems, recv_sems, after)
    return list(out[:n]), list(out[n:])


def _add_chips(p, own, kc, name):
    _, P, Q = p.shape
    tr = P
    while N_CHIPS * tr * Q * 4 > 6 * 1024 * 1024 and tr % 16 == 0:
        tr //= 2
    sharded = own.ndim == 3

    def body(kc_ref, p_ref, own_ref, o_ref):
        k = kc_ref[0]
        mine = own_ref[...].reshape(tr, Q)
        v = [jnp.where(k == j, mine, p_ref[j]) for j in range(N_CHIPS)]
        o_ref[0] = ((v[0] + v[1]) + v[2]) + v[3]

    own_spec = (pl.BlockSpec((1, tr, Q), lambda i, kc_ref: (kc_ref[0], i, 0)) if sharded
                else pl.BlockSpec((tr, Q), lambda i, kc_ref: (i, 0)))
    return pl.pallas_call(
        body, name=name,
        grid_spec=pltpu.PrefetchScalarGridSpec(
            num_scalar_prefetch=1, grid=(P // tr,),
            in_specs=[pl.BlockSpec((N_CHIPS, tr, Q), lambda i, kc_ref: (0, i, 0)), own_spec],
            out_specs=pl.BlockSpec((1, tr, Q), lambda i, kc_ref: (kc_ref[1], i, 0))),
        out_shape=jax.ShapeDtypeStruct((2, P, Q), p.dtype),
        compiler_params=_cp(1),
    )(kc, p, own)


def _rs_share(fs, tag):
    n = len(fs)

    def body(*refs):
        outs = refs[n:2 * n]
        send_sems, recv_sems = refs[2 * n:]
        x, y, c = _mesh_pos()
        sends = [_remote(outs[t].at[c], outs[t].at[c], send_sems.at[t], recv_sems.at[t], (x, y, 1 - c)) for t in range(n)]
        for cp in sends:
            cp.start()
        for t in range(n):
            _remote(outs[t].at[c], outs[t].at[1 - c], send_sems.at[t], recv_sems.at[t], (x, y, 1 - c)).wait_recv()
        for cp in sends:
            cp.wait_send()

    return pl.pallas_call(
        body, name=f"rs_share_{tag}", in_specs=[ANY] * n, out_specs=[ANY] * n,
        out_shape=[jax.ShapeDtypeStruct(f.shape, f.dtype) for f in fs], input_output_aliases={t: t for t in range(n)},
        scratch_shapes=[pltpu.SemaphoreType.DMA((n,)), pltpu.SemaphoreType.DMA((n,))],
    )(*fs)


def _reduce_start(gs, kc, tag):
    return _reduce_continue(gs, _rs_send_sibling(gs, tag), kc, tag)


def _reduce_continue(gs, from_sibling, kc, tag):
    chip_sums = [_add_piece(g, r, kc[1:], name=f"add_piece_{tag}_{t}") for t, (g, r) in enumerate(zip(gs, from_sibling))]
    send_sems, recv_sems, chip_sums, lands, token = _chips_start(chip_sums, tag)
    return (send_sems, recv_sems, chip_sums, lands, tag), token


def _reduce_finish(states, kc, after):
    mine = []
    for send_sems, recv_sems, chip_sums, lands, tag in states:
        chip_sums, from_chips = _chips_wait(send_sems, recv_sems, chip_sums, lands, after, tag)
        mine += [_add_chips(p, h, kc, name=f"add_chips_{tag}_{t}") for t, (p, h) in enumerate(zip(from_chips, chip_sums))]
    return _rs_share(mine, "all")


def _adamw(w, g, m, v, name):
    R, C = w.shape
    tr = R
    for cand in (512, 256, 128, 64, 32, 16, 8):
        if R % cand == 0 and cand * C * 4 <= 2 * 1024 * 1024:
            tr = cand
            break
    c1 = 1.0 / (1.0 - ADAM_B1 ** ADAM_STEP)
    c2 = 1.0 / (1.0 - ADAM_B2 ** ADAM_STEP)

    def body(w_ref, g_ref, m_ref, v_ref, d_ref, mo_ref, vo_ref):
        gv = g_ref[...]
        mn = ADAM_B1 * m_ref[...] + (1.0 - ADAM_B1) * gv
        vn = ADAM_B2 * v_ref[...] + (1.0 - ADAM_B2) * (gv * gv)
        mo_ref[...] = mn
        vo_ref[...] = vn
        d_ref[...] = -ADAM_LR * ((mn * c1) / (jnp.sqrt(vn * c2) + ADAM_EPS) + ADAM_WD * w_ref[...])

    spec = pl.BlockSpec((tr, C), lambda i: (i, 0))
    shp = jax.ShapeDtypeStruct((R, C), F32)
    return pl.pallas_call(body, name=name, grid=(R // tr,), in_specs=[spec] * 4, out_specs=[spec] * 3, out_shape=[shp] * 3,
                          compiler_params=_cp(1))(w, g, m, v)


def _adamw_many(ws, gs, ms, vs):
    n = len(ws)
    c1 = 1.0 / (1.0 - ADAM_B1 ** ADAM_STEP)
    c2 = 1.0 / (1.0 - ADAM_B2 ** ADAM_STEP)

    def body(*refs):
        w_refs, g_refs, m_refs, v_refs = (refs[k * n:(k + 1) * n] for k in range(4))
        outs = refs[4 * n:]
        for i in range(n):
            gv = g_refs[i][...]
            mn = ADAM_B1 * m_refs[i][...] + (1.0 - ADAM_B1) * gv
            vn = ADAM_B2 * v_refs[i][...] + (1.0 - ADAM_B2) * (gv * gv)
            outs[3 * i][...] = -ADAM_LR * ((mn * c1) / (jnp.sqrt(vn * c2) + ADAM_EPS) + ADAM_WD * w_refs[i][...])
            outs[3 * i + 1][...] = mn
            outs[3 * i + 2][...] = vn

    out_shape = [jax.ShapeDtypeStruct(w.shape, F32) for w in ws for _ in range(3)]
    return pl.pallas_call(body, name="adamw_small", out_shape=out_shape)(*ws, *gs, *ms, *vs)


def _step(a):
    x, y, c = _mesh_pos()
    kc = jnp.stack([2 * x + y, c]).astype(jnp.int32)

    rs = _rows(sum(a[n].size for n in SH_SMALL))
    first, later = ["ev_w_in", "ev_w_out"], ["od_w_in", "od_w_out", "ffn_w_up", "ffn_w_down"]
    lead = lambda w: w if w.ndim == 3 else w[None]
    *full, gs = _gather_weights(first, [a[n].astype(BF16) for n in first], _pack([a[n] for n in SH_SMALL], rs, F32))
    p = {n: a[n] for n in REP}
    p.update({n: lead(w) for n, w in zip(first, full)})
    parts = [_unpack(gs[k], [a[n].shape for n in SH_SMALL]) for k in range(N_CHIPS)]
    for i, n in enumerate(SH_SMALL):
        p[n] = jnp.concatenate([parts[k][i] for k in range(N_CHIPS)], axis=SH_SMALL[n])
    shards = lax.optimization_barrier(([a[n].astype(BF16) for n in later], full))[0]
    g_send, g_recv, shards, bufs, token = _gather_start(later, shards, _place_own(later, shards))
    p["norm_mix"] = p["norm_mix"] + token[0:1, 0:1]

    def late(after):
        got = _gather_forward(later, _gather_wait(later, g_send, g_recv, shards, bufs, after))
        return {n: lead(w) for n, w in zip(later, got)}

    p["late"] = late

    states, pending = {}, {}

    def start_reduce(tag, gs, split=False):
        if split:
            pending[tag], token = _sibling_start(gs, tag)
        else:
            states[tag], token = _reduce_start(gs, kc, tag)
        return token

    def continue_reduce(tag, after):
        send_sems, recv_sems, gs, lands = pending.pop(tag)
        gs, from_sibling = _sibling_wait(send_sems, recv_sems, gs, lands, after, tag)
        states[tag], token = _reduce_continue(gs, from_sibling, kc, tag)
        return token

    sq8, grad_x, grads, big = _local_step(a["x"][0], a["loss_target"][0], p, start_reduce, continue_reduce)
    loss = lax.psum(0.5 / D_MODEL * jnp.sum(sq8), ("x", "y", "c"))

    r_s = _rows(sum(a[n].size for n in SH_SMALL), 2 * SUBLANES) // 2
    small_pieces = []
    for k in range(N_CHIPS):
        pieces = [lax.slice_in_dim(grads[n], k * a[n].shape[ax], (k + 1) * a[n].shape[ax], axis=ax) for n, ax in SH_SMALL.items()]
        small_pieces.append(_pack(pieces, 2 * r_s, F32).reshape(2, r_s, LANES))
    g_small = jnp.stack(small_pieces)
    r_r = _rows(sum(a[n].size for n in REP), 2 * SUBLANES) // 2
    g_rep = _pack([grads[n] for n in REP], 2 * r_r, F32).reshape(2, r_r, LANES)
    start_reduce("g4", [g_small, g_rep])
    reduced = _reduce_finish([states[tag] for tag in ("g1", "g2", "g3", "g4")], kc, grad_x)
    red = dict(zip([key for tag in ("g1", "g2", "g3") for key in REDUCE_GROUPS[tag]], reduced))
    gfin = {}
    for n in ("ev_w_in", "ev_w_out", "od_w_in", "od_w_out"):
        gfin[n] = red[n, 0].reshape(a[n].shape)
    for n in ("ffn_w_up", "ffn_w_down"):
        gfin[n] = jnp.stack([red[n, l].reshape(a[n].shape[1:]) for l in range(2)])
    gfin.update(zip(SH_SMALL, _unpack(reduced[-2], [a[n].shape for n in SH_SMALL])))
    gfin.update(zip(REP, _unpack(reduced[-1], [a[n].shape for n in REP])))

    out = {"loss": loss, "grad_x": grad_x[None]}
    small_names = list(SH_SMALL) + REP
    for n in SH_BIG:
        shp = a[n].shape
        two_d = lambda t: t.reshape(-1, shp[-1])
        d, mo, vo = _adamw(two_d(a[n]), two_d(gfin[n]), two_d(a["m_" + n]), two_d(a["v_" + n]), name=f"adamw_{n}")
        out["delta_" + n], out["new_m_" + n], out["new_v_" + n] = d.reshape(shp), mo.reshape(shp), vo.reshape(shp)
    two_d = lambda t: t.reshape(-1, t.shape[-1])
    res = _adamw_many(*[[two_d(src(n)) for n in small_names]
                        for src in (lambda n: a[n], lambda n: gfin[n], lambda n: a["m_" + n], lambda n: a["v_" + n])])
    for i, n in enumerate(small_names):
        out["delta_" + n], out["new_m_" + n], out["new_v_" + n] = (r.reshape(a[n].shape) for r in res[3 * i:3 * i + 3])
    for n in WEIGHTS:
        out["grad_" + n] = gfin[n]
    return out


def kernel(x, norm_mix, norm_ffn, norm_final, ev_w_in, ev_conv_w, ev_conv_b, ev_gate_a_w, ev_gate_a_b, ev_gate_x_w, ev_gate_x_b, ev_lru_lambda, hg_lb_logits, ev_hg_norm, ev_w_out, od_w_in, od_b_in, od_ln_g, od_ln_b, od_w_s, od_b_s, od_w_out, ffn_w_up, ffn_conv_w, ffn_conv_b, ffn_w_down, loss_target, m_norm_mix, m_norm_ffn, m_norm_final, m_ev_w_in, m_ev_conv_w, m_ev_conv_b, m_ev_gate_a_w, m_ev_gate_a_b, m_ev_gate_x_w, m_ev_gate_x_b, m_ev_lru_lambda, m_hg_lb_logits, m_ev_hg_norm, m_ev_w_out, m_od_w_in, m_od_b_in, m_od_ln_g, m_od_ln_b, m_od_w_s, m_od_b_s, m_od_w_out, m_ffn_w_up, m_ffn_conv_w, m_ffn_conv_b, m_ffn_w_down, v_norm_mix, v_norm_ffn, v_norm_final, v_ev_w_in, v_ev_conv_w, v_ev_conv_b, v_ev_gate_a_w, v_ev_gate_a_b, v_ev_gate_x_w, v_ev_gate_x_b, v_ev_lru_lambda, v_hg_lb_logits, v_ev_hg_norm, v_ev_w_out, v_od_w_in, v_od_b_in, v_od_ln_g, v_od_ln_b, v_od_w_s, v_od_b_s, v_od_w_out, v_ffn_w_up, v_ffn_conv_w, v_ffn_conv_b, v_ffn_w_down):
    vals = (x, norm_mix, norm_ffn, norm_final, ev_w_in, ev_conv_w, ev_conv_b, ev_gate_a_w, ev_gate_a_b, ev_gate_x_w, ev_gate_x_b, ev_lru_lambda, hg_lb_logits, ev_hg_norm, ev_w_out, od_w_in, od_b_in, od_ln_g, od_ln_b, od_w_s, od_b_s, od_w_out, ffn_w_up, ffn_conv_w, ffn_conv_b, ffn_w_down, loss_target, m_norm_mix, m_norm_ffn, m_norm_final, m_ev_w_in, m_ev_conv_w, m_ev_conv_b, m_ev_gate_a_w, m_ev_gate_a_b, m_ev_gate_x_w, m_ev_gate_x_b, m_ev_lru_lambda, m_hg_lb_logits, m_ev_hg_norm, m_ev_w_out, m_od_w_in, m_od_b_in, m_od_ln_g, m_od_ln_b, m_od_w_s, m_od_b_s, m_od_w_out, m_ffn_w_up, m_ffn_conv_w, m_ffn_conv_b, m_ffn_w_down, v_norm_mix, v_norm_ffn, v_norm_final, v_ev_w_in, v_ev_conv_w, v_ev_conv_b, v_ev_gate_a_w, v_ev_gate_a_b, v_ev_gate_x_w, v_ev_gate_x_b, v_ev_lru_lambda, v_hg_lb_logits, v_ev_hg_norm, v_ev_w_out, v_od_w_in, v_od_b_in, v_od_ln_g, v_od_ln_b, v_od_w_s, v_od_b_s, v_od_w_out, v_ffn_w_up, v_ffn_conv_w, v_ffn_conv_b, v_ffn_w_down)
    names = ["x"] + WEIGHTS + ["loss_target"] + ["m_" + n for n in WEIGHTS] + ["v_" + n for n in WEIGHTS]
    out = _step(dict(zip(names, vals)))
    return (out["loss"], out["grad_x"], *[out["grad_" + n] for n in WEIGHTS], *[out["delta_" + n] for n in WEIGHTS],
            *[out["new_m_" + n] for n in WEIGHTS], *[out["new_v_" + n] for n in WEIGHTS])
```

```python
import functools

import jax
import jax.numpy as jnp
from jax import lax
from jax.experimental import pallas as pl
from jax.experimental.pallas import tpu as pltpu

F32 = jnp.float32
BF16 = jnp.bfloat16

EPS = 1e-6
D_MODEL = 1024
LRU_W = 512
LRU_BLOCKS = 8
LRU_C = 8.0
HG_HEADS = 4
HG_D = 128
HG_CHUNK = 64
SGU_G = 8
SGU_CHUNK = 128
D_FF = 2816
ADAM_LR, ADAM_B1, ADAM_B2, ADAM_EPS, ADAM_WD, ADAM_STEP = 0.001, 0.9, 0.999, 1e-08, 0.01, 10

V7X_VMEM_BYTES = 64 * 1024 * 1024
VMEM_LIMIT = V7X_VMEM_BYTES - 8 * 1024 * 1024
SUBLANES = 8
LANES = 128
BF16_ROWS = 16

GELU_C0 = 0.7978845608028654
GELU_C1 = 0.044715

NN = (((1,), (0,)), ((), ()))
NT = (((1,), (1,)), ((), ()))
TN = (((0,), (0,)), ((), ()))


def _dot(a, b, dims=NN):
    return lax.dot_general(a.astype(BF16), b.astype(BF16), dims, preferred_element_type=F32)


def _cp(n_grid):
    return pltpu.CompilerParams(dimension_semantics=("arbitrary",) * n_grid, vmem_limit_bytes=VMEM_LIMIT)


def _chunk(n, cap):
    best = LANES
    for c in range(LANES, cap + 1, LANES):
        if n % c == 0:
            best = c
    return best


def _resident(shape):
    nd = len(shape)
    return pl.BlockSpec(shape, lambda *_: (0,) * nd, pipeline_mode=pl.Buffered(1))


def _rsum8(x):
    r, c = x.shape
    return x.reshape(r // SUBLANES, SUBLANES, c).sum(axis=0)


def _sigmoid(x):
    return 0.5 * jnp.tanh(0.5 * x) + 0.5


def _gelu(x):
    return 0.5 * x * (1.0 + jnp.tanh(GELU_C0 * (x + GELU_C1 * x * x * x)))


def _gelu_grad(x):
    t = jnp.tanh(GELU_C0 * (x + GELU_C1 * x * x * x))
    return 0.5 * (1.0 + t) + 0.5 * x * (1.0 - t * t) * GELU_C0 * (1.0 + 3.0 * GELU_C1 * x * x)


def _silu_and_grad(x):
    s = _sigmoid(x)
    return x * s, s * (1.0 + x * (1.0 - s))


def _shift_rows(e, j):
    n = e.shape[0]
    return e if j % n == 0 else pltpu.roll(e, j % n, 0)


def _weight(w):
    if isinstance(w, tuple):
        stack, layer = w
        K, N = stack.shape[1:]
        return stack, pl.BlockSpec((None, K, N), lambda *_: (layer, 0, 0), pipeline_mode=pl.Buffered(1)), (K, N)
    return w, _resident(w.shape), w.shape


def _norm_mm(h, g, w, b, name, tt=1024):
    T, D = h.shape
    w, w_spec, (_, N) = _weight(w)
    cn = _chunk(N, 512)

    def body(h_ref, g_ref, w_ref, b_ref, hn_ref, z_ref):
        x = h_ref[...]
        r = lax.rsqrt(jnp.mean(x * x, axis=-1, keepdims=True) + EPS)
        hn = (x * r * g_ref[...]).astype(BF16)
        hn_ref[...] = hn
        for j in range(0, N, cn):
            acc = jnp.dot(hn, w_ref[:, j:j + cn], preferred_element_type=F32) + b_ref[:, j:j + cn]
            z_ref[:, j:j + cn] = acc.astype(BF16)

    return pl.pallas_call(
        body, name=name, grid=(T // tt,),
        in_specs=[pl.BlockSpec((tt, D), lambda i: (i, 0)), _resident((1, D)), w_spec, _resident((1, N))],
        out_specs=[pl.BlockSpec((tt, D), lambda i: (i, 0)), pl.BlockSpec((tt, N), lambda i: (i, 0))],
        out_shape=[jax.ShapeDtypeStruct((T, D), BF16), jax.ShapeDtypeStruct((T, N), BF16)],
        compiler_params=_cp(1),
    )(h, g, w, b)


def _mm_res(a, w, res, name, tt=1024):
    T, K = a.shape
    N = w.shape[1]
    cn = _chunk(N, 512)

    def body(a_ref, w_ref, res_ref, o_ref):
        av = a_ref[...]
        for j in range(0, N, cn):
            o_ref[:, j:j + cn] = jnp.dot(av, w_ref[:, j:j + cn], preferred_element_type=F32) + res_ref[:, j:j + cn]

    return pl.pallas_call(
        body, name=name, grid=(T // tt,),
        in_specs=[pl.BlockSpec((tt, K), lambda i: (i, 0)), _resident(w.shape), pl.BlockSpec((tt, N), lambda i: (i, 0))],
        out_specs=pl.BlockSpec((tt, N), lambda i: (i, 0)),
        out_shape=jax.ShapeDtypeStruct((T, N), F32),
        compiler_params=_cp(1),
    )(a, w, res)


def _mm_tn(a, b, name, col_shards=1, tt=2048):
    T, K = a.shape
    N = b.shape[1]
    ns = N // col_shards
    tt = min(tt, T)
    while 2 * (tt * K * a.dtype.itemsize + tt * ns * b.dtype.itemsize + K * ns * 4) > VMEM_LIMIT - 12 * 1024 * 1024:
        tt //= 2

    def body(a_ref, b_ref, o_ref):
        acc = lax.dot_general(a_ref[...].astype(BF16), b_ref[...].astype(BF16), TN, preferred_element_type=F32)
        prev = jnp.where(pl.program_id(1) == 0, 0.0, o_ref[0])
        o_ref[0] = prev + acc

    out = pl.pallas_call(
        body, name=name, grid=(col_shards, T // tt),
        in_specs=[pl.BlockSpec((tt, K), lambda n, t: (t, 0)), pl.BlockSpec((tt, ns), lambda n, t: (t, n))],
        out_specs=pl.BlockSpec((1, K, ns), lambda n, t: (n, 0, 0)),
        out_shape=jax.ShapeDtypeStruct((col_shards, K, ns), F32),
        compiler_params=_cp(2),
    )(a, b)
    return out if col_shards > 1 else out[0]


def _mm_normbwd(dz, w, x, g, dres, name):
    T, N = dz.shape
    D = w.shape[0]
    tt = 1024 if N <= 3072 else 512

    def body(dz_ref, wt_ref, x_ref, g_ref, dres_ref, dx_ref, dg_ref):
        @pl.when(pl.program_id(0) == 0)
        def _():
            dg_ref[...] = jnp.zeros_like(dg_ref)

        dy = lax.dot_general(dz_ref[...], wt_ref[...], NT, preferred_element_type=F32)
        x = x_ref[...]
        r = lax.rsqrt(jnp.mean(x * x, axis=-1, keepdims=True) + EPS)
        xn = x * r
        dg_ref[...] += _rsum8(dy * xn)
        dxn = dy * g_ref[...]
        dx_ref[...] = dres_ref[...] + r * (dxn - xn * jnp.mean(dxn * xn, axis=-1, keepdims=True))

    return pl.pallas_call(
        body, name=name, grid=(T // tt,),
        in_specs=[pl.BlockSpec((tt, N), lambda i: (i, 0)), _resident((D, N)), pl.BlockSpec((tt, D), lambda i: (i, 0)),
                  _resident((1, D)), pl.BlockSpec((tt, D), lambda i: (i, 0))],
        out_specs=[pl.BlockSpec((tt, D), lambda i: (i, 0)), pl.BlockSpec((SUBLANES, D), lambda i: (0, 0))],
        out_shape=[jax.ShapeDtypeStruct((T, D), F32), jax.ShapeDtypeStruct((SUBLANES, D), F32)],
        compiler_params=_cp(1),
    )(dz, w, x, g, dres)


def _col_groups(F, cc, per_group=4):
    step = cc * per_group
    return [(g0, min(g0 + step, F)) for g0 in range(0, F, step)]


def _loss_head(x, gv, tgt):
    r = lax.rsqrt(jnp.mean(x * x, axis=-1, keepdims=True) + EPS)
    xn = x * r
    diff = xn * gv - tgt
    dy = diff * (1.0 / x.shape[-1])
    dxn = dy * gv
    return r * (dxn - xn * jnp.mean(dxn * xn, axis=-1, keepdims=True)), _rsum8(diff * diff), _rsum8(dy * xn)


def _ffn_act(gu, cw, cb, w_down, res, name, loss=None, tt=512):
    T = gu.shape[0]
    F = gu.shape[1] // 2
    w_down, wd_spec, (_, D) = _weight(w_down)
    cc = _chunk(F, 256)
    hb = tt // BF16_ROWS

    def body(gu_ref, halo_ref, cw_ref, cb_ref, wd_ref, res_ref, *rest):
        if loss is None:
            a_ref, gc_ref, o_ref = rest
        else:
            gf_ref, t_ref, a_ref, gc_ref, o_ref, sq_ref, dgf_ref = rest
        first = pl.program_id(0) == 0
        acc = res_ref[...]
        for g0, g1 in _col_groups(F, cc):
            for c0 in range(g0, g1, cc):
                cs = slice(c0, c0 + cc)
                x = gu_ref[:, cs].astype(F32)
                halo = jnp.where(first, 0.0, halo_ref[:, cs].astype(F32))
                e = jnp.concatenate([halo, x], axis=0)
                gc = (cb_ref[:, cs] + cw_ref[0:1, cs] * _shift_rows(e, 2)[BF16_ROWS:] + cw_ref[1:2, cs] * _shift_rows(e, 1)[BF16_ROWS:]
                      + cw_ref[2:3, cs] * x)
                up = gu_ref[:, F + c0:F + c0 + cc].astype(F32)
                gc_ref[:, cs] = gc.astype(BF16)
                a_ref[:, cs] = (gc * _sigmoid(gc) * up).astype(BF16)
            acc = acc + jnp.dot(a_ref[:, g0:g1], wd_ref[g0:g1, :], preferred_element_type=F32)
        if loss is None:
            o_ref[...] = acc
        else:
            @pl.when(first)
            def _():
                sq_ref[...] = jnp.zeros_like(sq_ref)
                dgf_ref[...] = jnp.zeros_like(dgf_ref)

            dx, sq, dgf = _loss_head(acc, gf_ref[...], t_ref[...])
            o_ref[...] = dx
            sq_ref[...] += sq
            dgf_ref[...] += dgf

    tok = lambda w: pl.BlockSpec((tt, w), lambda i: (i, 0))
    acc8 = pl.BlockSpec((SUBLANES, D), lambda i: (0, 0))
    in_specs = [tok(2 * F), pl.BlockSpec((BF16_ROWS, F), lambda i: (jnp.maximum(i * hb - 1, 0), 0)),
                _resident((SUBLANES, F)), _resident((1, F)), wd_spec, tok(D)]
    out_specs = [tok(F), tok(F), tok(D)]
    out_shape = [jax.ShapeDtypeStruct((T, F), BF16), jax.ShapeDtypeStruct((T, F), BF16), jax.ShapeDtypeStruct((T, D), F32)]
    args = [gu, gu, cw, cb, w_down, res]
    if loss is not None:
        in_specs += [_resident((1, D)), tok(D)]
        out_specs += [acc8, acc8]
        out_shape += [jax.ShapeDtypeStruct((SUBLANES, D), F32)] * 2
        args += list(loss)
    return pl.pallas_call(body, name=name, grid=(T // tt,), in_specs=in_specs, out_specs=out_specs, out_shape=out_shape,
                          compiler_params=_cp(1))(*args)


def _ffn_act_bwd(gu, gc, cw, w_up, w_down, x, g, dh, name, tt=256):
    T = gu.shape[0]
    F = gu.shape[1] // 2
    w_up, wu_spec, (D, _) = _weight(w_up)
    w_down, wd_spec, _ = _weight(w_down)
    cc = _chunk(F, 256)
    hb = tt // BF16_ROWS
    last_hb = T // BF16_ROWS - 1
    nt = T // tt

    def body(gu_ref, upnext_ref, gc_ref, gcnext_ref, cw_ref, wu_ref, wd_ref, x_ref, g_ref, dh_ref, dhnext_ref,
             dgu_ref, dc_ref, dx_ref, dg_ref):
        i = pl.program_id(0)

        @pl.when(i == 0)
        def _():
            dc_ref[...] = jnp.zeros_like(dc_ref)
            dg_ref[...] = jnp.zeros_like(dg_ref)

        n = tt + BF16_ROWS
        ext = lambda main, nxt: jnp.concatenate([main.astype(F32), nxt.astype(F32)], axis=0)
        dhe = ext(dh_ref[...], jnp.where(i == nt - 1, 0.0, dhnext_ref[...])).astype(BF16)
        dy = jnp.zeros((tt, D), F32)
        groups = _col_groups(F, cc)
        da_of = lambda grp: lax.dot_general(dhe, wd_ref[grp[0]:grp[1], :], NT, preferred_element_type=F32)
        da_next = da_of(groups[0])
        for gi, (lo, hi) in enumerate(groups):
            da, da_next = da_next, (da_of(groups[gi + 1]) if gi + 1 < len(groups) else None)
            for c0 in range(lo, hi, cc):
                cs = slice(c0, c0 + cc)
                us = slice(F + c0, F + c0 + cc)
                gc = ext(gc_ref[:, cs], gcnext_ref[:, cs])
                up = ext(gu_ref[:, us], upnext_ref[:, cs])
                dae = da[:, c0 - lo:c0 - lo + cc]
                s, ds = _silu_and_grad(gc)
                dgc = dae * up * ds
                dgu_ref[:, us] = (dae * s)[:tt].astype(BF16)
                dgc1 = _shift_rows(dgc, n - 1)[:tt]
                dgc2 = _shift_rows(dgc, n - 2)[:tt]
                dm = dgc[:tt]
                dgu_ref[:, cs] = (cw_ref[2:3, cs] * dm + cw_ref[1:2, cs] * dgc1 + cw_ref[0:1, cs] * dgc2).astype(BF16)
                gt = gu_ref[:, cs].astype(F32)
                dc_ref[0, :, cs] += _rsum8(dgc2 * gt)
                dc_ref[1, :, cs] += _rsum8(dgc1 * gt)
                dc_ref[2, :, cs] += _rsum8(dm * gt)
                dc_ref[3, :, cs] += _rsum8(dm)
            dy = (dy + lax.dot_general(dgu_ref[:, lo:hi], wu_ref[:, lo:hi], NT, preferred_element_type=F32)
                  + lax.dot_general(dgu_ref[:, F + lo:F + hi], wu_ref[:, F + lo:F + hi], NT, preferred_element_type=F32))
        xv = x_ref[...]
        r = lax.rsqrt(jnp.mean(xv * xv, axis=-1, keepdims=True) + EPS)
        xn = xv * r
        dg_ref[...] += _rsum8(dy * xn)
        dxn = dy * g_ref[...]
        dx_ref[...] = dh_ref[...] + r * (dxn - xn * jnp.mean(dxn * xn, axis=-1, keepdims=True))

    tok = lambda w: pl.BlockSpec((tt, w), lambda i: (i, 0))
    nxt = lambda w, col: pl.BlockSpec((BF16_ROWS, w), lambda i: (jnp.minimum((i + 1) * hb, last_hb), col))
    return pl.pallas_call(
        body, name=name, grid=(nt,),
        in_specs=[tok(2 * F), nxt(F, 1), tok(F), nxt(F, 0),
                  _resident((SUBLANES, F)), wu_spec, wd_spec, tok(D), _resident((1, D)), tok(D), nxt(D, 0)],
        out_specs=[tok(2 * F), pl.BlockSpec((4, SUBLANES, F), lambda i: (0, 0, 0)), tok(D),
                   pl.BlockSpec((SUBLANES, D), lambda i: (0, 0))],
        out_shape=[jax.ShapeDtypeStruct((T, 2 * F), BF16), jax.ShapeDtypeStruct((4, SUBLANES, F), F32),
                   jax.ShapeDtypeStruct((T, D), F32), jax.ShapeDtypeStruct((SUBLANES, D), F32)],
        compiler_params=_cp(1),
    )(gu, gu, gc, gc, cw, w_up, w_down, x, g, dh, dh)


def _softplus_neg(lam):
    x = -lam
    y = jnp.exp(-jnp.abs(x))
    l1p = jnp.where(y < 0.01, y * (1.0 - y * (0.5 - y * (1.0 / 3.0))), jnp.log(1.0 + y))
    return jnp.maximum(x, 0.0) + l1p


def _lru_gates(xc, wa_ref, ba_ref, wx_ref, bx_ref, sp):
    xcb = xc.astype(BF16)
    r = _sigmoid(jnp.dot(xcb, wa_ref[...], preferred_element_type=F32) + ba_ref[...])
    gi = _sigmoid(jnp.dot(xcb, wx_ref[...], preferred_element_type=F32) + bx_ref[...])
    log_a = -LRU_C * r * sp
    a = jnp.exp(log_a)
    x2 = 2.0 * log_a
    series = -x2 * (1.0 + x2 * 0.5 * (1.0 + x2 * (1.0 / 3.0)))
    om = jnp.where(x2 > -0.02, series, 1.0 - a * a)
    return r, gi, a, jnp.sqrt(om)


def _lru_conv(xr, halo, cw_ref, cb_ref):
    e = jnp.concatenate([halo, xr], axis=0)
    x1 = _shift_rows(e, 1)[BF16_ROWS:]
    x2 = _shift_rows(e, 2)[BF16_ROWS:]
    x3 = _shift_rows(e, 3)[BF16_ROWS:]
    xc = cb_ref[...] + cw_ref[0:1, :] * x3 + cw_ref[1:2, :] * x2 + cw_ref[2:3, :] * x1 + cw_ref[3:4, :] * xr
    return xc, x1, x2, x3


def _lru_fwd(z, cw, cb, wa, ba, wx, bx, lam, name="lru_fwd", tt=256):
    T = z.shape[0]
    W = LRU_W
    hb = tt // BF16_ROWS
    ng = tt // SUBLANES

    def body(z_ref, halo_ref, cw_ref, cb_ref, wa_ref, ba_ref, wx_ref, bx_ref, lam_ref, oa_ref, h_ref, a_s, sv_ref, u_s, hc):
        i = pl.program_id(0)

        @pl.when(i == 0)
        def _():
            hc[...] = jnp.zeros_like(hc)

        xr = z_ref[:, W:2 * W].astype(F32)
        halo = jnp.where(i == 0, 0.0, halo_ref[...].astype(F32))
        xc, _, _, _ = _lru_conv(xr, halo, cw_ref, cb_ref)
        sp = _softplus_neg(lam_ref[...])
        r, gi, a, mult = _lru_gates(xc, wa_ref, ba_ref, wx_ref, bx_ref, sp)
        a_s[...] = a
        u_s[...] = mult * gi * xc
        for k, saved in enumerate((mult, r, gi, xc)):
            sv_ref[:, k * W:(k + 1) * W] = saved.astype(BF16)
        row = lax.broadcasted_iota(jnp.int32, (SUBLANES, W), 0)

        def step(j, hprev):
            r0 = pl.multiple_of(j * SUBLANES, SUBLANES)
            A = a_s[pl.ds(r0, SUBLANES), :]
            U = u_s[pl.ds(r0, SUBLANES), :]
            for k in (1, 2, 4):
                m = row >= k
                U = jnp.where(m, A * pltpu.roll(U, k, 0) + U, U)
                A = jnp.where(m, A * pltpu.roll(A, k, 0), A)
            H = U + A * hprev
            h_ref[pl.ds(r0, SUBLANES), :] = H
            return jnp.broadcast_to(H[SUBLANES - 1:SUBLANES, :], (SUBLANES, W))

        hc[...] = lax.fori_loop(0, ng, step, hc[...])
        oa_ref[...] = (_gelu(z_ref[:, 0:W].astype(F32)) * h_ref[...]).astype(BF16)

    return pl.pallas_call(
        body, name=name, grid=(T // tt,),
        in_specs=[pl.BlockSpec((tt, 2 * W), lambda i: (i, 0)),
                  pl.BlockSpec((BF16_ROWS, W), lambda i: (jnp.maximum(i * hb - 1, 0), 1)),
                  _resident((SUBLANES, W)), _resident((1, W)), _resident((W, W)), _resident((1, W)),
                  _resident((W, W)), _resident((1, W)), _resident((1, W))],
        out_specs=[pl.BlockSpec((tt, W), lambda i: (i, 0)), pl.BlockSpec((tt, W), lambda i: (i, 0)),
                   pl.BlockSpec((tt, W), lambda i: (i, 0)), pl.BlockSpec((tt, 4 * W), lambda i: (i, 0))],
        out_shape=[jax.ShapeDtypeStruct((T, W), BF16), jax.ShapeDtypeStruct((T, W), F32),
                   jax.ShapeDtypeStruct((T, W), F32), jax.ShapeDtypeStruct((T, 4 * W), BF16)],
        scratch_shapes=[pltpu.VMEM((tt, W), F32), pltpu.VMEM((SUBLANES, W), F32)],
        compiler_params=_cp(1),
    )(z, z, cw, cb, wa, ba, wx, bx, lam)


def _lru_bwd(z, hseq, a_all, saved, dh, w_out, cw, wat, wxt, lam, name="lru_bwd", tt=256):
    T = z.shape[0]
    W = LRU_W
    nt = T // tt
    sb = tt // SUBLANES
    ng = tt // SUBLANES

    def body(z_ref, h_ref, hprev_ref, a_ref, sv_ref, dh_ref, wo_ref, cw_ref, wat_ref, wxt_ref, lam_ref,
             dz_ref, dc_ref, dwa_ref, dwx_ref, dv_ref, c_s, d_s, g_s, gc, an, dxn):
        i = pl.program_id(0)
        ti = nt - 1 - i

        @pl.when(i == 0)
        def _():
            dc_ref[...] = jnp.zeros_like(dc_ref)
            dwa_ref[...] = jnp.zeros_like(dwa_ref)
            dwx_ref[...] = jnp.zeros_like(dwx_ref)
            dv_ref[...] = jnp.zeros_like(dv_ref)
            gc[...] = jnp.zeros_like(gc)
            an[...] = jnp.zeros_like(an)
            dxn[...] = jnp.zeros_like(dxn)

        xr = z_ref[:, W:2 * W].astype(F32)
        yg = z_ref[:, 0:W].astype(F32)
        sp = _softplus_neg(lam_ref[...])
        a = a_ref[...]
        mult, r, gi, xc = (sv_ref[:, k * W:(k + 1) * W].astype(F32) for k in range(4))
        h = h_ref[...]
        hp = jnp.where(ti == 0, 0.0, hprev_ref[...])
        hm1 = _shift_rows(jnp.concatenate([hp, h], axis=0), 1)[SUBLANES:]
        dout = lax.dot_general(dh_ref[...].astype(BF16), wo_ref[...], NT, preferred_element_type=F32)
        d_s[...] = dout * _gelu(yg)
        dz_ref[:, 0:W] = (dout * h * _gelu_grad(yg)).astype(BF16)
        c_s[...] = _shift_rows(jnp.concatenate([a, an[...]], axis=0), tt + SUBLANES - 1)[:tt]
        an[...] = a[0:SUBLANES, :]
        row = lax.broadcasted_iota(jnp.int32, (SUBLANES, W), 0)

        def step(j, gnext):
            r0 = pl.multiple_of((ng - 1 - j) * SUBLANES, SUBLANES)
            C = c_s[pl.ds(r0, SUBLANES), :]
            G = d_s[pl.ds(r0, SUBLANES), :]
            for k in (1, 2, 4):
                m = row < SUBLANES - k
                G = jnp.where(m, G + C * pltpu.roll(G, SUBLANES - k, 0), G)
                C = jnp.where(m, C * pltpu.roll(C, SUBLANES - k, 0), C)
            G = G + C * gnext
            g_s[pl.ds(r0, SUBLANES), :] = G
            return jnp.broadcast_to(G[0:1, :], (SUBLANES, W))

        gc[...] = lax.fori_loop(0, ng, step, gc[...])
        du = g_s[...]
        da = du * hm1
        dgi = du * mult * xc
        dxc = du * mult * gi
        dmult = du * gi * xc
        dlog_a = da * a - dmult * (a * a) / mult
        dr = dlog_a * (-LRU_C * sp)
        dv_ref[2] += _rsum8(dlog_a * (-LRU_C * r))
        dpr = (dr * r * (1.0 - r)).astype(BF16)
        dpi = (dgi * gi * (1.0 - gi)).astype(BF16)
        dv_ref[0] += _rsum8(dpr.astype(F32))
        dv_ref[1] += _rsum8(dpi.astype(F32))
        xcb = sv_ref[:, 3 * W:4 * W]
        dwa_ref[...] += lax.dot_general(xcb, dpr, TN, preferred_element_type=F32)
        dwx_ref[...] += lax.dot_general(xcb, dpi, TN, preferred_element_type=F32)
        dxc = dxc + jnp.dot(dpr, wat_ref[...], preferred_element_type=F32) + jnp.dot(dpi, wxt_ref[...], preferred_element_type=F32)
        n = tt + BF16_ROWS
        de = jnp.concatenate([dxc, dxn[...]], axis=0)
        d1, d2, d3 = (_shift_rows(de, n - j)[:tt] for j in (1, 2, 3))
        dxn[...] = dxc[0:BF16_ROWS, :]
        dz_ref[:, W:2 * W] = (cw_ref[3:4, :] * dxc + cw_ref[2:3, :] * d1 + cw_ref[1:2, :] * d2 + cw_ref[0:1, :] * d3).astype(BF16)
        dc_ref[0] += _rsum8(d3 * xr)
        dc_ref[1] += _rsum8(d2 * xr)
        dc_ref[2] += _rsum8(d1 * xr)
        dc_ref[3] += _rsum8(dxc * xr)
        dc_ref[4] += _rsum8(dxc)

    rev = lambda i: nt - 1 - i
    tok = lambda w: pl.BlockSpec((tt, w), lambda i: (rev(i), 0))
    return pl.pallas_call(
        body, name=name, grid=(nt,),
        in_specs=[tok(2 * W), tok(W),
                  pl.BlockSpec((SUBLANES, W), lambda i: (jnp.maximum(rev(i) * sb - 1, 0), 0)),
                  tok(W), tok(4 * W), tok(dh.shape[1]), pl.BlockSpec((W, w_out.shape[1]), lambda i: (0, 0)),
                  _resident((SUBLANES, W)), _resident((W, W)), _resident((W, W)), _resident((1, W))],
        out_specs=[pl.BlockSpec((tt, 2 * W), lambda i: (rev(i), 0)),
                   pl.BlockSpec((5, SUBLANES, W), lambda i: (0, 0, 0)),
                   pl.BlockSpec((W, W), lambda i: (0, 0)), pl.BlockSpec((W, W), lambda i: (0, 0)),
                   pl.BlockSpec((3, SUBLANES, W), lambda i: (0, 0, 0))],
        out_shape=[jax.ShapeDtypeStruct((T, z.shape[1]), BF16), jax.ShapeDtypeStruct((5, SUBLANES, W), F32),
                   jax.ShapeDtypeStruct((W, W), F32), jax.ShapeDtypeStruct((W, W), F32),
                   jax.ShapeDtypeStruct((3, SUBLANES, W), F32)],
        scratch_shapes=[pltpu.VMEM((tt, W), F32), pltpu.VMEM((tt, W), F32), pltpu.VMEM((tt, W), F32),
                        pltpu.VMEM((SUBLANES, W), F32), pltpu.VMEM((SUBLANES, W), F32), pltpu.VMEM((BF16_ROWS, W), F32)],
        compiler_params=_cp(1),
    )(z, hseq, hseq, a_all, saved, dh, w_out, cw, wat, wxt, lam)


def _split3(x):
    hi = x.astype(BF16)
    r1 = x - hi.astype(F32)
    mid = r1.astype(BF16)
    lo = (r1 - mid.astype(F32)).astype(BF16)
    return hi, mid, lo


def _tri_matmul(tri, x):
    hi, mid, lo = _split3(x)
    return (jnp.dot(tri, hi, preferred_element_type=F32) + jnp.dot(tri, mid, preferred_element_type=F32)
            + jnp.dot(tri, lo, preferred_element_type=F32))


def _hg_chunk(q, fl, lb):
    C = q.shape[0]
    ri = lax.broadcasted_iota(jnp.int32, (C, C), 0)
    ci = lax.broadcasted_iota(jnp.int32, (C, C), 1)
    causal = ri >= ci
    sig = _sigmoid(fl)
    f = lb + (1.0 - lb) * sig
    k = 1.0 - f
    sq = _sigmoid(q)
    qf = q * sq
    b = _tri_matmul(causal.astype(BF16), jnp.log(f))
    bm = b[C // 2 - 1:C // 2, :]
    bl = b[C - 1:C, :]
    e_qt, e_kt, e_in, e_out = jnp.exp(b - bm), jnp.exp(bm - b), jnp.exp(b), jnp.exp(bl - b)
    qt = qf * e_qt
    kt = k * e_kt
    qin = qf * e_in
    kout = k * e_out
    qtb, ktb = qt.astype(BF16), kt.astype(BF16)
    att = [jnp.where(causal, _dot(qtb[:, _head(h)], ktb[:, _head(h)], NT), 0.0).astype(BF16) for h in range(HG_HEADS)]
    return dict(sig=sig, f=f, k=k, sq=sq, qf=qf, b=b, bm=bm, bl=bl, qt=qt, kt=kt, qin=qin, kout=kout, att=att,
                causal=causal, anti=ri <= ci, decay=jnp.exp(bl), e_qt=e_qt, e_kt=e_kt, e_in=e_in, e_out=e_out)


def _head(h):
    return slice(h * HG_D, (h + 1) * HG_D)


def _hgrn_fwd(z, lb, gn, name="hgrn_fwd", tt=512):
    T = z.shape[0]
    C = HG_CHUNK
    nc = tt // C
    Dh = HG_D
    Wd = HG_HEADS * Dh

    def body(q_ref, f_ref, v_ref, g_ref, lb_ref, gn_ref, o_ref, ss_ref, st):
        @pl.when(pl.program_id(0) == 0)
        def _():
            st[...] = jnp.zeros_like(st)

        S = [st[h] for h in range(HG_HEADS)]
        for c in range(nc):
            rows = slice(c * C, (c + 1) * C)
            ck = _hg_chunk(q_ref[rows, :].astype(F32), f_ref[rows, :].astype(F32), lb_ref[...])
            v = v_ref[rows, :]
            g = g_ref[rows, :].astype(F32)
            H = range(HG_HEADS)
            qinb, koutb = ck["qin"].astype(BF16), ck["kout"].astype(BF16)
            for h in H:
                ss_ref[h, c] = S[h]
            o = [_dot(ck["att"][h], v[:, _head(h)]) + _dot(qinb[:, _head(h)], S[h], NT) for h in H]
            S = [ck["decay"][:, _head(h)] * S[h] + _dot(v[:, _head(h)], koutb[:, _head(h)], TN) for h in H]
            outs = [o[h] * lax.rsqrt(jnp.mean(o[h] * o[h], axis=-1, keepdims=True) + EPS) * gn_ref[...] for h in H]
            o_ref[rows, :] = (jnp.concatenate(outs, axis=1) * (g * _sigmoid(g))).astype(BF16)
        for h in range(HG_HEADS):
            st[h] = S[h]

    col = lambda base: (lambda i: (i, base))
    return pl.pallas_call(
        body, name=name, grid=(T // tt,),
        in_specs=[pl.BlockSpec((tt, Wd), col(2)), pl.BlockSpec((tt, Wd), col(3)), pl.BlockSpec((tt, Wd), col(4)),
                  pl.BlockSpec((tt, Wd), col(5)), _resident((1, Wd)), _resident((1, Dh))],
        out_specs=[pl.BlockSpec((tt, Wd), lambda i: (i, 0)),
                   pl.BlockSpec((HG_HEADS, nc, Dh, Dh), lambda i: (0, i, 0, 0))],
        out_shape=[jax.ShapeDtypeStruct((T, Wd), BF16),
                   jax.ShapeDtypeStruct((HG_HEADS, T // C, Dh, Dh), F32)],
        scratch_shapes=[pltpu.VMEM((HG_HEADS, Dh, Dh), F32)],
        compiler_params=_cp(1),
    )(z, z, z, z, lb, gn)


def _hgrn_bwd(z, ss, dh, w_out, lb, gn, dz, name="hgrn_bwd", tt=512):
    T = z.shape[0]
    C = HG_CHUNK
    nc = tt // C
    nt = T // tt
    Dh = HG_D
    Wd = HG_HEADS * Dh

    def body(q_ref, f_ref, v_ref, g_ref, ss_ref, dh_ref, wo_ref, lb_ref, gn_ref, dz01_ref, dz_ref, dlb_ref, dgn_ref, dst):
        @pl.when(pl.program_id(0) == 0)
        def _():
            dst[...] = jnp.zeros_like(dst)
            dlb_ref[...] = jnp.zeros_like(dlb_ref)
            dgn_ref[...] = jnp.zeros_like(dgn_ref)

        dS = [dst[h] for h in range(HG_HEADS)]
        lbv = lb_ref[...]
        gnv = gn_ref[...]
        rowc = lax.broadcasted_iota(jnp.int32, (C, Wd), 0)
        cat = lambda xs: jnp.concatenate(xs, axis=1)
        for c in reversed(range(nc)):
            rows = slice(c * C, (c + 1) * C)
            q = q_ref[rows, :].astype(F32)
            ck = _hg_chunk(q, f_ref[rows, :].astype(F32), lbv)
            v = v_ref[rows, :]
            g = g_ref[rows, :].astype(F32)
            dout = lax.dot_general(dh_ref[rows, :].astype(BF16), wo_ref[...], NT, preferred_element_type=F32)
            sg, dsg = _silu_and_grad(g)
            d_ong = dout * sg
            H = range(HG_HEADS)
            qinb, koutb, qtb, ktb = (ck[n].astype(BF16) for n in ("qin", "kout", "qt", "kt"))
            S = [ss_ref[h, c] for h in H]
            Sb = [s.astype(BF16) for s in S]
            dSb = [d.astype(BF16) for d in dS]
            o = [_dot(ck["att"][h], v[:, _head(h)]) + _dot(qinb[:, _head(h)], Sb[h], NT) for h in H]
            rn = [lax.rsqrt(jnp.mean(o[h] * o[h], axis=-1, keepdims=True) + EPS) for h in H]
            on = [o[h] * rn[h] for h in H]
            don = [d_ong[:, _head(h)] * gnv for h in H]
            do = [(rn[h] * (don[h] - on[h] * jnp.mean(don[h] * on[h], axis=-1, keepdims=True))).astype(BF16) for h in H]
            datt = [jnp.where(ck["causal"], _dot(do[h], v[:, _head(h)], NT), 0.0).astype(BF16) for h in H]
            dvs = [_dot(ck["att"][h], do[h], TN) + _dot(koutb[:, _head(h)], dSb[h], NT) for h in H]
            dqins = [_dot(do[h], Sb[h]) for h in H]
            dkouts = [_dot(v[:, _head(h)], dSb[h]) for h in H]
            dqts = [_dot(datt[h], ktb[:, _head(h)]) for h in H]
            dkts = [_dot(datt[h], qtb[:, _head(h)], TN) for h in H]
            ddecays = [jnp.sum(dS[h] * S[h], axis=0, keepdims=True) for h in H]
            dS = [_dot(do[h], qinb[:, _head(h)], TN) + ck["decay"][:, _head(h)] * dS[h] for h in H]
            ons = [on[h] * gnv for h in H]
            dgn = _rsum8(d_ong[:, _head(0)] * on[0])
            for h in range(1, HG_HEADS):
                dgn = dgn + _rsum8(d_ong[:, _head(h)] * on[h])
            dgn_ref[...] += dgn
            dqt, dkt, dqin, dkout, ddecay = cat(dqts), cat(dkts), cat(dqins), cat(dkouts), cat(ddecays)
            dqf = dqt * ck["e_qt"] + dqin * ck["e_in"]
            dk = dkt * ck["e_kt"] + dkout * ck["e_out"]
            kk = dkout * ck["kout"]
            db = dqt * ck["qt"] - dkt * ck["kt"] + dqin * ck["qin"] - kk
            dbl = jnp.sum(kk, axis=0, keepdims=True) + ddecay * ck["decay"]
            db = db + jnp.where(rowc == C - 1, dbl, 0.0)
            dlogf = _tri_matmul(ck["anti"].astype(BF16), db)
            dfv = dlogf / ck["f"] - dk
            sig, sq = ck["sig"], ck["sq"]
            dlb_ref[...] += _rsum8(dfv * (1.0 - sig))
            dz_ref[rows, 2 * Wd:3 * Wd] = (dqf * (sq * (1.0 + q * (1.0 - sq)))).astype(BF16)
            dz_ref[rows, 3 * Wd:4 * Wd] = (dfv * (1.0 - lbv) * sig * (1.0 - sig)).astype(BF16)
            dz_ref[rows, 4 * Wd:5 * Wd] = cat(dvs).astype(BF16)
            dz_ref[rows, 5 * Wd:6 * Wd] = (dout * cat(ons) * dsg).astype(BF16)
        dz_ref[:, 0:2 * Wd] = dz01_ref[...]
        for h in range(HG_HEADS):
            dst[h] = dS[h]

    rev = lambda i: nt - 1 - i
    col = lambda base: (lambda i: (rev(i), base))
    return pl.pallas_call(
        body, name=name, grid=(nt,),
        in_specs=[pl.BlockSpec((tt, Wd), col(2)), pl.BlockSpec((tt, Wd), col(3)), pl.BlockSpec((tt, Wd), col(4)),
                  pl.BlockSpec((tt, Wd), col(5)),
                  pl.BlockSpec((HG_HEADS, nc, Dh, Dh), lambda i: (0, rev(i), 0, 0)),
                  pl.BlockSpec((tt, dh.shape[1]), col(0)), pl.BlockSpec((Wd, w_out.shape[1]), lambda i: (1, 0)),
                  _resident((1, Wd)), _resident((1, Dh)), pl.BlockSpec((tt, 2 * Wd), col(0))],
        out_specs=[pl.BlockSpec((tt, 6 * Wd), lambda i: (rev(i), 0)), pl.BlockSpec((SUBLANES, Wd), lambda i: (0, 0)),
                   pl.BlockSpec((SUBLANES, Dh), lambda i: (0, 0))],
        out_shape=[jax.ShapeDtypeStruct((T, 6 * Wd), BF16), jax.ShapeDtypeStruct((SUBLANES, Wd), F32),
                   jax.ShapeDtypeStruct((SUBLANES, Dh), F32)],
        input_output_aliases={9: 0},
        scratch_shapes=[pltpu.VMEM((HG_HEADS, Dh, Dh), F32)],
        compiler_params=_cp(1),
    )(z, z, z, z, ss, dh, w_out, lb, gn, dz)


def _sgu_core(p, lg_ref, lb_ref, wsc_ref, bsb_ref):
    Wd = D_MODEL
    G = SGU_CHUNK
    zz = _gelu(p)
    u = zz[:, :Wd]
    v = zz[:, Wd:]
    vc = v - jnp.mean(v, axis=-1, keepdims=True)
    rstd = lax.rsqrt(jnp.mean(vc * vc, axis=-1, keepdims=True) + EPS)
    vhat = vc * rstd
    vn = vhat * lg_ref[...] + lb_ref[...]
    svs = []
    for gi in range(SGU_G):
        svs.append(jnp.dot(wsc_ref[gi], vn[:, gi * G:(gi + 1) * G].astype(BF16), preferred_element_type=F32) + bsb_ref[gi])
    return u, vhat, rstd, vn, jnp.concatenate(svs, axis=1)


def _sgu_fwd(p1, lg, lbias, wsc, bsb, name="sgu_fwd", tt=512):
    T = p1.shape[0]
    Wd = D_MODEL
    C = SGU_CHUNK

    def body(p_ref, lg_ref, lb_ref, wsc_ref, bsb_ref, s_ref):
        for c in range(tt // C):
            rows = slice(c * C, (c + 1) * C)
            u, _, _, _, sv = _sgu_core(p_ref[rows, :].astype(F32), lg_ref, lb_ref, wsc_ref, bsb_ref)
            s_ref[rows, :] = (u * sv).astype(BF16)

    return pl.pallas_call(
        body, name=name, grid=(T // tt,),
        in_specs=[pl.BlockSpec((tt, 2 * Wd), lambda i: (i, 0)), _resident((1, Wd)), _resident((1, Wd)),
                  _resident((SGU_G, C, C)), _resident((SGU_G, C, C))],
        out_specs=pl.BlockSpec((tt, Wd), lambda i: (i, 0)),
        out_shape=jax.ShapeDtypeStruct((T, Wd), BF16),
        compiler_params=_cp(1),
    )(p1, lg, lbias, wsc, bsb)


def _sgu_bwd(p1, dh, w_out, lg, lbias, wsc, wsct, bsb, name="sgu_bwd", tt=512):
    T = p1.shape[0]
    Wd = D_MODEL
    C = SGU_CHUNK

    def body(p_ref, dh_ref, wo_ref, lg_ref, lb_ref, wsc_ref, wsct_ref, bsb_ref, dp_ref, dws_ref, dbs_ref, dlg_ref, dlb_ref, dbin_ref):
        @pl.when(pl.program_id(0) == 0)
        def _():
            dws_ref[...] = jnp.zeros_like(dws_ref)
            dbs_ref[...] = jnp.zeros_like(dbs_ref)
            dlg_ref[...] = jnp.zeros_like(dlg_ref)
            dlb_ref[...] = jnp.zeros_like(dlb_ref)
            dbin_ref[...] = jnp.zeros_like(dbin_ref)

        for c in range(tt // C):
            rows = slice(c * C, (c + 1) * C)
            p = p_ref[rows, :].astype(F32)
            u, vhat, rstd, vn, sv = _sgu_core(p, lg_ref, lb_ref, wsc_ref, bsb_ref)
            dsc = lax.dot_general(dh_ref[rows, :].astype(BF16), wo_ref[...], NT, preferred_element_type=F32)
            du = dsc * sv
            dsv = dsc * u
            dvns = []
            for gi in range(SGU_G):
                cs = slice(gi * C, (gi + 1) * C)
                dsv_g = dsv[:, cs]
                dvns.append(jnp.dot(wsct_ref[gi], dsv_g.astype(BF16), preferred_element_type=F32))
                dws_ref[gi] += _dot(dsv_g, vn[:, cs], NT)
                dbs_ref[gi] += dsv_g
            dvn = jnp.concatenate(dvns, axis=1)
            dlg_ref[...] += _rsum8(dvn * vhat)
            dlb_ref[...] += _rsum8(dvn)
            dvh = dvn * lg_ref[...]
            dv = rstd * (dvh - jnp.mean(dvh, axis=-1, keepdims=True) - vhat * jnp.mean(dvh * vhat, axis=-1, keepdims=True))
            dp = jnp.concatenate([du, dv], axis=1) * _gelu_grad(p)
            dbin_ref[...] += _rsum8(dp)
            dp_ref[rows, :] = dp.astype(BF16)

    full3 = pl.BlockSpec((SGU_G, C, C), lambda i: (0, 0, 0))
    return pl.pallas_call(
        body, name=name, grid=(T // tt,),
        in_specs=[pl.BlockSpec((tt, 2 * Wd), lambda i: (i, 0)), pl.BlockSpec((tt, dh.shape[1]), lambda i: (i, 0)),
                  _resident(w_out.shape), _resident((1, Wd)), _resident((1, Wd)), _resident((SGU_G, C, C)), _resident((SGU_G, C, C)),
                  _resident((SGU_G, C, C))],
        out_specs=[pl.BlockSpec((tt, 2 * Wd), lambda i: (i, 0)), full3, full3,
                   pl.BlockSpec((SUBLANES, Wd), lambda i: (0, 0)), pl.BlockSpec((SUBLANES, Wd), lambda i: (0, 0)),
                   pl.BlockSpec((SUBLANES, 2 * Wd), lambda i: (0, 0))],
        out_shape=[jax.ShapeDtypeStruct((T, 2 * Wd), BF16), jax.ShapeDtypeStruct((SGU_G, C, C), F32),
                   jax.ShapeDtypeStruct((SGU_G, C, C), F32), jax.ShapeDtypeStruct((SUBLANES, Wd), F32),
                   jax.ShapeDtypeStruct((SUBLANES, Wd), F32), jax.ShapeDtypeStruct((SUBLANES, 2 * Wd), F32)],
        compiler_params=_cp(1),
    )(p1, dh, w_out, lg, lbias, wsc, wsct, bsb)


def _pad_rows(w, rows=SUBLANES):
    return jnp.pad(w, ((0, rows - w.shape[0]), (0, 0)))


def _block_diag(w):
    n, b, _ = w.shape
    return (w[:, :, None, :] * jnp.eye(n, dtype=w.dtype)[:, None, :, None]).reshape(n * b, n * b)


def _diag_blocks(m, n):
    b = m.shape[0] // n
    m4 = m.reshape(n, b, n, b)
    return jnp.stack([m4[k, :, k, :] for k in range(n)], axis=0)


def _piece_major(dw):
    if dw.ndim == 2:
        K, N = dw.shape
        return dw.reshape(N_CHIPS, 2, K // (2 * N_CHIPS), N)
    _, K, ns = dw.shape
    return dw.reshape(N_CHIPS, 2, K // 2, ns)


def _ffn_fwd(h, g, w_up, cw, cb, w_down, tag, loss=None):
    hn, gu = _norm_mm(h, g, w_up, jnp.zeros((1, 2 * D_FF), F32), name=f"ffn_up_{tag}")
    a, gc, *out = _ffn_act(gu, cw, cb, w_down, h, name=f"ffn_act_down_{tag}", loss=loss)
    return (out[0] if loss is None else out), (hn, gu, gc, a)


def _ffn_bwd(dh, h, g, saved, w_up, cw, w_down, tag):
    hn, gu, gc, a = saved
    dwd = _mm_tn(a, dh, name=f"ffn_dwd_{tag}")
    dgu, dc, dhin, dg8 = _ffn_act_bwd(gu, gc, cw, w_up, w_down, h, g, dh, name=f"ffn_bwd_{tag}")
    dwu = _mm_tn(hn, dgu, name=f"ffn_dwu_{tag}", col_shards=N_CHIPS)
    dcs = dc.sum(axis=1)
    return dhin, dg8.sum(axis=0), dwu, dcs[0:3], dcs[3], dwd


REDUCE_GROUPS = {"g1": [("ffn_w_up", 1), ("ffn_w_down", 1), ("od_w_out", 0), ("od_w_in", 0)],
                 "g2": [("ffn_w_up", 0), ("ffn_w_down", 0)],
                 "g3": [("ev_w_out", 0), ("ev_w_in", 0)]}


def _local_step(x, tgt, p, start_reduce=None, continue_reduce=None):
    row = lambda v: v.reshape(1, -1)
    grads = {}

    lower = jax.nn.softmax(p["hg_lb_logits"], axis=0)
    lb0 = row(lower[0])
    ev_cw = _pad_rows(p["ev_conv_w"][0])
    ev_cb = row(p["ev_conv_b"][0])
    wa = _block_diag(p["ev_gate_a_w"][0]).astype(BF16)
    wx = _block_diag(p["ev_gate_x_w"][0]).astype(BF16)
    ba, bx, lam = row(p["ev_gate_a_b"][0]), row(p["ev_gate_x_b"][0]), row(p["ev_lru_lambda"][0])
    gn = row(p["ev_hg_norm"][0])
    tril = jnp.tril(jnp.ones((SGU_CHUNK, SGU_CHUNK), F32))
    wsc = (p["od_w_s"][0] * tril).astype(BF16)
    bsb = jnp.broadcast_to(p["od_b_s"][0][:, :, None], (SGU_G, SGU_CHUNK, SGU_CHUNK)).astype(F32)
    ffn_cw = [_pad_rows(p["ffn_conv_w"][l]) for l in range(2)]
    ffn_cb = [row(p["ffn_conv_b"][l]) for l in range(2)]
    ev_w_in, ev_w_out = p["ev_w_in"][0], p["ev_w_out"][0]
    nm = [row(p["norm_mix"][l]) for l in range(2)]
    nf = [row(p["norm_ffn"][l]) for l in range(2)]

    h0 = x
    hn0, z0 = _norm_mm(h0, nm[0], ev_w_in, jnp.zeros((1, ev_w_in.shape[1]), F32), name="ev_in")
    out_a, hseq, a_all, lru_saved = _lru_fwd(z0, ev_cw, ev_cb, wa, ba, wx, bx, lam)
    out_b, ss = _hgrn_fwd(z0, lb0, gn)
    mix0 = jnp.concatenate([out_a, out_b], axis=1)
    h1 = _mm_res(mix0, ev_w_out, h0, name="ev_out")
    late = p["late"](h1) if "late" in p else p
    od_w_in, od_w_out = late["od_w_in"][0], late["od_w_out"][0]
    w_up = [(late["ffn_w_up"], l) for l in range(2)]
    w_down = [(late["ffn_w_down"], l) for l in range(2)]
    h2, ffn0 = _ffn_fwd(h1, nf[0], w_up[0], ffn_cw[0], ffn_cb[0], w_down[0], "l0")
    hn1, p1 = _norm_mm(h2, nm[1], od_w_in, row(p["od_b_in"][0]), name="od_in")
    s1 = _sgu_fwd(p1, row(p["od_ln_g"][0]), row(p["od_ln_b"][0]), wsc, bsb)
    h3 = _mm_res(s1, od_w_out, h2, name="od_out")
    (dh4, sq8, dgf8), ffn1 = _ffn_fwd(h3, nf[1], w_up[1], ffn_cw[1], ffn_cb[1], w_down[1], "l1", loss=(row(p["norm_final"]), tgt))
    grads["norm_final"] = dgf8.sum(axis=0)

    big = {}
    dh3, dnf1, dwu1, dcw1, dcb1, dwd1 = _ffn_bwd(dh4, h3, nf[1], ffn1, w_up[1], ffn_cw[1], w_down[1], "l1")
    big["ffn_w_up", 1], big["ffn_w_down", 1] = _piece_major(dwu1), _piece_major(dwd1)
    big["od_w_out", 0] = _piece_major(_mm_tn(s1, dh3, name="od_dwo"))
    wsct = jnp.swapaxes(wsc, 1, 2)
    dp1, dws, dbs, dlg8, dlb8, dbin8 = _sgu_bwd(p1, dh3, od_w_out, row(p["od_ln_g"][0]), row(p["od_ln_b"][0]), wsc, wsct, bsb)
    grads["od_w_s"] = (dws * tril)[None]
    grads["od_b_s"] = dbs.sum(axis=-1)[None]
    grads["od_ln_g"] = dlg8.sum(axis=0)[None]
    grads["od_ln_b"] = dlb8.sum(axis=0)[None]
    grads["od_b_in"] = dbin8.sum(axis=0)[None]
    dh2, dnm1 = _mm_normbwd(dp1, od_w_in, h2, nm[1], dh3, name="od_dh")
    big["od_w_in", 0] = _piece_major(_mm_tn(hn1, dp1, name="od_dwi", col_shards=N_CHIPS))
    if start_reduce is not None:
        token = start_reduce("g1", [big[key] for key in REDUCE_GROUPS["g1"]], split=True)
        ffn_cw[0] = ffn_cw[0] + token[0:1, 0:1]

    dh1, dnf0, dwu0, dcw0, dcb0, dwd0 = _ffn_bwd(dh2, h1, nf[0], ffn0, w_up[0], ffn_cw[0], w_down[0], "l0")
    big["ffn_w_up", 0], big["ffn_w_down", 0] = _piece_major(dwu0), _piece_major(dwd0)
    if start_reduce is not None:
        token = continue_reduce("g1", dh1) + start_reduce("g2", [big[key] for key in REDUCE_GROUPS["g2"]], split=True)
        lam = lam + token[0:1, 0:1]
    big["ev_w_out", 0] = _piece_major(_mm_tn(mix0, dh1, name="ev_dwo"))
    dz01, dc5, dwa, dwx, dvec = _lru_bwd(z0, hseq, a_all, lru_saved, dh1, ev_w_out, ev_cw, wa.T, wx.T, lam)
    if start_reduce is not None:
        lb0 = lb0 + continue_reduce("g2", dc5)[0:1, 0:1]
    dz0, dlb8, dgn8 = _hgrn_bwd(z0, ss, dh1, ev_w_out, lb0, gn, dz01)
    big["ev_w_in", 0] = _piece_major(_mm_tn(hn0, dz0, name="ev_dwi", col_shards=N_CHIPS))
    if start_reduce is not None:
        token = start_reduce("g3", [big[key] for key in REDUCE_GROUPS["g3"]])
        nm[0] = nm[0] + token[0:1, 0:1]
    grad_x, dnm0 = _mm_normbwd(dz0, ev_w_in, h0, nm[0], dh1, name="ev_dh")

    dc5s = dc5.sum(axis=1)
    grads["ev_conv_w"] = dc5s[0:4][None]
    grads["ev_conv_b"] = dc5s[4][None]
    grads["ev_gate_a_w"] = _diag_blocks(dwa, LRU_BLOCKS)[None]
    grads["ev_gate_x_w"] = _diag_blocks(dwx, LRU_BLOCKS)[None]
    dvs = dvec.sum(axis=1)
    grads["ev_gate_a_b"] = dvs[0][None]
    grads["ev_gate_x_b"] = dvs[1][None]
    grads["ev_lru_lambda"] = (dvs[2] * (-jax.nn.sigmoid(-p["ev_lru_lambda"][0])))[None]
    dlb = dlb8.sum(axis=0)
    grads["hg_lb_logits"] = dlb[None, :] * lower[0][None, :] * (jnp.eye(3, dtype=F32)[0][:, None] - lower)
    grads["ev_hg_norm"] = dgn8.sum(axis=0)[None]
    grads["norm_mix"] = jnp.stack([dnm0.sum(axis=0), dnm1.sum(axis=0)])
    grads["norm_ffn"] = jnp.stack([dnf0, dnf1])
    grads["ffn_conv_w"] = jnp.stack([dcw0, dcw1])
    grads["ffn_conv_b"] = jnp.stack([dcb0, dcb1])
    return sq8, grad_x, grads, big


MESH = pl.DeviceIdType.MESH
ANY = pl.BlockSpec(memory_space=pl.ANY)
N_CHIPS = 4
N_DEV = 8

SH_BIG = {"ev_w_in": 2, "ev_w_out": 1, "od_w_in": 2, "od_w_out": 1, "ffn_w_up": 2, "ffn_w_down": 1}
SH_SMALL = {"ev_conv_w": 2, "od_b_in": 1, "od_ln_g": 1, "od_ln_b": 1, "ffn_conv_w": 2}
REP = ["norm_mix", "norm_ffn", "norm_final", "ev_conv_b", "ev_gate_a_w", "ev_gate_a_b", "ev_gate_x_w", "ev_gate_x_b",
       "ev_lru_lambda", "hg_lb_logits", "ev_hg_norm", "od_w_s", "od_b_s", "ffn_conv_b"]
WEIGHTS = ["norm_mix", "norm_ffn", "norm_final", "ev_w_in", "ev_conv_w", "ev_conv_b", "ev_gate_a_w", "ev_gate_a_b", "ev_gate_x_w",
           "ev_gate_x_b", "ev_lru_lambda", "hg_lb_logits", "ev_hg_norm", "ev_w_out", "od_w_in", "od_b_in", "od_ln_g", "od_ln_b",
           "od_w_s", "od_b_s", "od_w_out", "ffn_w_up", "ffn_conv_w", "ffn_conv_b", "ffn_w_down"]


def _rows(n_elems, mult=SUBLANES):
    r = -(-n_elems // LANES)
    return -(-r // mult) * mult


def _pack(arrs, rows, dtype):
    flat = jnp.concatenate([a.reshape(-1).astype(dtype) for a in arrs])
    return jnp.pad(flat, (0, rows * LANES - flat.shape[0])).reshape(rows, LANES)


def _unpack(flat2d, shapes):
    flat = flat2d.reshape(-1)
    out, off = [], 0
    for s in shapes:
        n = 1
        for d in s:
            n *= d
        out.append(flat[off:off + n].reshape(s))
        off += n
    return out


def _mesh_pos():
    return lax.axis_index("x"), lax.axis_index("y"), lax.axis_index("c")


def _other_chips(x, y):
    return [(1 - x, y), (x, 1 - y), (1 - x, 1 - y)]


def _half_rows(n):
    return lambda r, c: r.at[0, pl.ds(c * (n // 2), n // 2), :]


GATHER_BIG = {
    "ev_w_in": ((1024, 3072), _half_rows(1024), lambda o, k, c: o.at[pl.ds(c * 512, 512), pl.ds(k * 768, 768)]),
    "ev_w_out": ((1024, 1024), _half_rows(256), lambda o, k, c: o.at[pl.ds(k * 256 + c * 128, 128), :]),
    "od_w_in": ((1024, 2048), _half_rows(1024), lambda o, k, c: o.at[pl.ds(c * 512, 512), pl.ds(k * 512, 512)]),
    "od_w_out": ((1024, 1024), _half_rows(256), lambda o, k, c: o.at[pl.ds(k * 256 + c * 128, 128), :]),
    "ffn_w_up": ((2, 1024, 2 * D_FF), lambda r, c: r.at[c], lambda o, k, c: o.at[c, :, pl.ds(k * (2 * D_FF // 4), 2 * D_FF // 4)]),
    "ffn_w_down": ((2, D_FF, 1024), lambda r, c: r.at[c], lambda o, k, c: o.at[c, pl.ds(k * (D_FF // 4), D_FF // 4), :]),
}


def _gather_weights(names, big, small):
    nb = len(big)
    descs = [GATHER_BIG[n] for n in names]
    rs = small.shape[0]

    def body(*refs):
        ins, s_ref = refs[:nb], refs[nb]
        outs, os_ref = refs[nb + 1:2 * nb + 1], refs[2 * nb + 1]
        ici_send, ici_recv, d2d_send, d2d_recv, loc_sems = refs[2 * nb + 2:2 * nb + 7]
        vbufs = refs[2 * nb + 7:]
        x, y, c = _mesh_pos()
        k = 2 * x + y
        chips = _other_chips(x, y)
        sib = (x, y, 1 - c)

        def remote(src, dst, ssem, rsem, to):
            return pltpu.make_async_remote_copy(src_ref=src, dst_ref=dst, send_sem=ssem, recv_sem=rsem, device_id=to,
                                                device_id_type=MESH)

        stage = [pltpu.make_async_copy(ins[t], vbufs[t], loc_sems.at[2 * t]) for t in range(nb)]
        stage.append(pltpu.make_async_copy(s_ref, vbufs[nb], loc_sems.at[2 * nb]))
        for cp in stage:
            cp.start()
        sends = []
        for t, (_, src, dst) in enumerate(descs):
            for j, (px, py) in enumerate(chips):
                sends.append(remote(src(ins[t], c), dst(outs[t], k, c), ici_send.at[3 * t + j], ici_recv.at[3 * t + j], (px, py, c)))
        for j, (px, py) in enumerate(chips):
            sends.append(remote(s_ref, os_ref.at[k], ici_send.at[3 * nb + j], ici_recv.at[3 * nb + j], (px, py, c)))
        for cp in sends:
            cp.start()
        for cp in stage:
            cp.wait()
        local = []
        for t, (_, src, dst) in enumerate(descs):
            for cc in (0, 1):
                local.append(pltpu.make_async_copy(src(vbufs[t], cc), dst(outs[t], k, cc), loc_sems.at[2 * t + cc]))
        local.append(pltpu.make_async_copy(vbufs[nb], os_ref.at[k], loc_sems.at[2 * nb]))
        for cp in local:
            cp.start()
        for t, (_, src, dst) in enumerate(descs):
            for j, (px, py) in enumerate(chips):
                got = dst(outs[t], 2 * px + py, c)
                remote(got, got, ici_send.at[3 * t + j], ici_recv.at[3 * t + j], (px, py, c)).wait_recv()
                fwd = remote(got, got, d2d_send.at[3 * t + j], d2d_recv.at[3 * t + j], sib)
                fwd.start()
                sends.append(fwd)
        for j, (px, py) in enumerate(chips):
            remote(s_ref, os_ref.at[2 * px + py], ici_send.at[3 * nb + j], ici_recv.at[3 * nb + j], (px, py, c)).wait_recv()
        for t, (_, src, dst) in enumerate(descs):
            for j, (px, py) in enumerate(chips):
                theirs = dst(outs[t], 2 * px + py, 1 - c)
                remote(theirs, theirs, d2d_send.at[3 * t + j], d2d_recv.at[3 * t + j], sib).wait_recv()
        for cp in sends:
            cp.wait_send()
        for cp in local:
            cp.wait()

    out_shape = [jax.ShapeDtypeStruct(d[0], BF16) for d in descs] + [jax.ShapeDtypeStruct((N_CHIPS, rs, LANES), small.dtype)]
    return pl.pallas_call(
        body, name="gather_weights", in_specs=[ANY] * (nb + 1), out_specs=[ANY] * (nb + 1), out_shape=out_shape,
        scratch_shapes=[pltpu.SemaphoreType.DMA((3 * nb + 3,)), pltpu.SemaphoreType.DMA((3 * nb + 3,)),
                        pltpu.SemaphoreType.DMA((3 * nb,)), pltpu.SemaphoreType.DMA((3 * nb,)),
                        pltpu.SemaphoreType.DMA((2 * nb + 1,))]
        + [pltpu.VMEM(b.shape, b.dtype) for b in big] + [pltpu.VMEM(small.shape, small.dtype)],
        compiler_params=pltpu.CompilerParams(vmem_limit_bytes=VMEM_LIMIT),
    )(*big, small)


def _place_own(names, big):
    nb = len(big)
    descs = [GATHER_BIG[n] for n in names]

    def body(*refs):
        ins, outs = refs[:nb], refs[nb:2 * nb]
        sems, vbufs = refs[2 * nb], refs[2 * nb + 1:]
        x, y, c = _mesh_pos()
        k = 2 * x + y
        stage = [pltpu.make_async_copy(ins[t], vbufs[t], sems.at[2 * t]) for t in range(nb)]
        for cp in stage:
            cp.start()
        for cp in stage:
            cp.wait()
        local = [pltpu.make_async_copy(src(vbufs[t], cc), dst(outs[t], k, cc), sems.at[2 * t + cc])
                 for t, (_, src, dst) in enumerate(descs) for cc in (0, 1)]
        for cp in local:
            cp.start()
        for cp in local:
            cp.wait()

    return pl.pallas_call(
        body, name="place_own", in_specs=[ANY] * nb, out_specs=[ANY] * nb,
        out_shape=[jax.ShapeDtypeStruct(d[0], BF16) for d in descs],
        scratch_shapes=[pltpu.SemaphoreType.DMA((2 * nb,))] + [pltpu.VMEM(b.shape, b.dtype) for b in big],
        compiler_params=pltpu.CompilerParams(vmem_limit_bytes=VMEM_LIMIT),
    )(*big)


def _gather_start(names, big, bufs):
    nb = len(big)
    descs = [GATHER_BIG[n] for n in names]

    def body(*refs):
        ins, lnd = refs[:nb], refs[nb:2 * nb]
        send_sems, recv_sems, token = refs[2 * nb], refs[2 * nb + 1], refs[-1]
        x, y, c = _mesh_pos()
        k = 2 * x + y
        for t, (_, src, dst) in enumerate(descs):
            for j, (px, py) in enumerate(_other_chips(x, y)):
                _remote(src(ins[t], c), dst(lnd[t], k, c), send_sems.at[3 * t + j], recv_sems.at[3 * t + j], (px, py, c)).start()
        token[...] = jnp.zeros_like(token)

    out = pl.pallas_call(
        body, name="gather_start",
        out_shape=(pltpu.SemaphoreType.DMA((3 * nb,)), pltpu.SemaphoreType.DMA((3 * nb,)),
                   *[pltpu.HBM(b.shape, b.dtype) for b in big], *[pltpu.HBM(b.shape, b.dtype) for b in bufs],
                   jax.ShapeDtypeStruct((SUBLANES, LANES), F32)),
        in_specs=[HBM] * (2 * nb), out_specs=(SEM, SEM, *[HBM] * (2 * nb), pl.BlockSpec(memory_space=pltpu.VMEM)),
        input_output_aliases={i: 2 + i for i in range(2 * nb)},
        compiler_params=pltpu.CompilerParams(has_side_effects=DATAFLOW),
    )(*[pltpu.with_memory_space_constraint(b, pltpu.HBM) for b in big], *[pltpu.with_memory_space_constraint(b, pltpu.HBM) for b in bufs])
    return out[0], out[1], list(out[2:2 + nb]), list(out[2 + nb:2 + 2 * nb]), out[-1]


def _gather_wait(names, send_sems, recv_sems, big, bufs, after):
    nb = len(big)
    descs = [GATHER_BIG[n] for n in names]

    def body(*refs):
        ins, lnd = refs[:nb], refs[nb:2 * nb]
        ssem, rsem = refs[2 * nb], refs[2 * nb + 1]
        x, y, c = _mesh_pos()
        for t, (_, src, dst) in enumerate(descs):
            for j, (px, py) in enumerate(_other_chips(x, y)):
                cp = _remote(src(ins[t], c), dst(lnd[t], 2 * px + py, c), ssem.at[3 * t + j], rsem.at[3 * t + j], (px, py, c))
                cp.wait_send()
                cp.wait_recv()

    out = pl.pallas_call(
        body, name="gather_wait",
        out_shape=(*[pltpu.HBM(b.shape, b.dtype) for b in big], *[pltpu.HBM(b.shape, b.dtype) for b in bufs]),
        in_specs=[HBM] * (2 * nb) + [SEM, SEM, ANY], out_specs=tuple([HBM] * (2 * nb)),
        input_output_aliases={i: i for i in range(2 * nb)},
        compiler_params=pltpu.CompilerParams(has_side_effects=DATAFLOW),
    )(*big, *bufs, send_sems, recv_sems, after)
    return list(out[nb:])


def _gather_forward(names, bufs):
    nb = len(bufs)
    descs = [GATHER_BIG[n] for n in names]

    def body(*refs):
        outs = refs[nb:2 * nb]
        send_sems, recv_sems = refs[2 * nb:]
        x, y, c = _mesh_pos()
        sib = (x, y, 1 - c)
        sends = []
        for t, (_, src, dst) in enumerate(descs):
            for j, (px, py) in enumerate(_other_chips(x, y)):
                got = dst(outs[t], 2 * px + py, c)
                sends.append(_remote(got, got, send_sems.at[3 * t + j], recv_sems.at[3 * t + j], sib))
        for cp in sends:
            cp.start()
        for t, (_, src, dst) in enumerate(descs):
            for j, (px, py) in enumerate(_other_chips(x, y)):
                theirs = dst(outs[t], 2 * px + py, 1 - c)
                _remote(theirs, theirs, send_sems.at[3 * t + j], recv_sems.at[3 * t + j], sib).wait_recv()
        for cp in sends:
            cp.wait_send()

    return pl.pallas_call(
        body, name="gather_forward", in_specs=[ANY] * nb, out_specs=[ANY] * nb,
        out_shape=[jax.ShapeDtypeStruct(b.shape, b.dtype) for b in bufs], input_output_aliases={t: t for t in range(nb)},
        scratch_shapes=[pltpu.SemaphoreType.DMA((3 * nb,)), pltpu.SemaphoreType.DMA((3 * nb,))],
    )(*bufs)


def _remote(src, dst, ssem, rsem, to):
    return pltpu.make_async_remote_copy(src_ref=src, dst_ref=dst, send_sem=ssem, recv_sem=rsem, device_id=to, device_id_type=MESH)


def _rs_send_sibling(gs, tag):
    n = len(gs)
    counts = [N_CHIPS if g.ndim == 4 else 1 for g in gs]
    ns = sum(counts)

    def body(*refs):
        ins, outs = refs[:n], refs[n:2 * n]
        send_sems, recv_sems = refs[2 * n:]
        x, y, c = _mesh_pos()
        cps, s = [], 0
        for t in range(n):
            if counts[t] == 1:
                cps.append(_remote(ins[t].at[1 - c], outs[t], send_sems.at[s], recv_sems.at[s], (x, y, 1 - c)))
                s += 1
            else:
                for k in range(N_CHIPS):
                    cps.append(_remote(ins[t].at[k, 1 - c], outs[t].at[k], send_sems.at[s], recv_sems.at[s], (x, y, 1 - c)))
                    s += 1
        for cp in cps:
            cp.start()
        for cp in cps:
            cp.wait()

    out_shape = [jax.ShapeDtypeStruct(g.shape[:1] + g.shape[2:] if g.ndim == 4 else g.shape[1:], g.dtype) for g in gs]
    return pl.pallas_call(
        body, name=f"rs_send_sibling_{tag}", in_specs=[ANY] * n, out_specs=[ANY] * n, out_shape=out_shape,
        scratch_shapes=[pltpu.SemaphoreType.DMA((ns,)), pltpu.SemaphoreType.DMA((ns,))],
    )(*gs)


def _add_piece(g, recv, c, name):
    P, Q = g.shape[-2:]

    def body(c_ref, g_ref, r_ref, o_ref):
        o_ref[...] = g_ref[...].reshape(o_ref.shape) + r_ref[...]

    if g.ndim == 4:
        grid = (N_CHIPS,)
        in_specs = [pl.BlockSpec((1, 1, P, Q), lambda k, c_ref: (k, c_ref[0], 0, 0)), pl.BlockSpec((1, P, Q), lambda k, c_ref: (k, 0, 0))]
        out_spec = pl.BlockSpec((1, P, Q), lambda k, c_ref: (k, 0, 0))
    else:
        grid = (1,)
        in_specs = [pl.BlockSpec((1, P, Q), lambda k, c_ref: (c_ref[0], 0, 0)), pl.BlockSpec((P, Q), lambda k, c_ref: (0, 0))]
        out_spec = pl.BlockSpec((P, Q), lambda k, c_ref: (0, 0))
    return pl.pallas_call(
        body, name=name,
        grid_spec=pltpu.PrefetchScalarGridSpec(num_scalar_prefetch=1, grid=grid, in_specs=in_specs, out_specs=out_spec),
        out_shape=jax.ShapeDtypeStruct(recv.shape, g.dtype),
        compiler_params=_cp(1),
    )(c, g, recv)


HBM = pl.BlockSpec(memory_space=pltpu.HBM)
SEM = pl.BlockSpec(memory_space=pltpu.SEMAPHORE)
DATAFLOW = pltpu.SideEffectType.DATAFLOW_SIDE_EFFECTING


def _sibling_copies(gs, srcs, dsts, send_sems, recv_sems):
    x, y, c = _mesh_pos()
    cps, s = [], 0
    for t, g in enumerate(gs):
        if g.ndim == 4:
            for k in range(N_CHIPS):
                cps.append(_remote(srcs[t].at[k, 1 - c], dsts[t].at[k], send_sems.at[s], recv_sems.at[s], (x, y, 1 - c)))
                s += 1
        else:
            cps.append(_remote(srcs[t].at[1 - c], dsts[t], send_sems.at[s], recv_sems.at[s], (x, y, 1 - c)))
            s += 1
    return cps


def _sibling_start(gs, tag):
    n = len(gs)
    ns = sum(N_CHIPS if g.ndim == 4 else 1 for g in gs)
    lands = [pltpu.with_memory_space_constraint(lax.empty(g.shape[:1] + g.shape[2:] if g.ndim == 4 else g.shape[1:], g.dtype), pltpu.HBM)
             for g in gs]

    def body(*refs):
        for cp in _sibling_copies(gs, refs[:n], refs[n:2 * n], refs[2 * n], refs[2 * n + 1]):
            cp.start()
        refs[-1][...] = jnp.zeros_like(refs[-1])

    out = pl.pallas_call(
        body, name=f"sibling_start_{tag}",
        out_shape=(pltpu.SemaphoreType.DMA((ns,)), pltpu.SemaphoreType.DMA((ns,)),
                   *[pltpu.HBM(g.shape, g.dtype) for g in gs], *[pltpu.HBM(l.shape, l.dtype) for l in lands],
                   jax.ShapeDtypeStruct((SUBLANES, LANES), F32)),
        in_specs=[HBM] * (2 * n), out_specs=(SEM, SEM, *[HBM] * (2 * n), pl.BlockSpec(memory_space=pltpu.VMEM)),
        input_output_aliases={i: 2 + i for i in range(2 * n)},
        compiler_params=pltpu.CompilerParams(has_side_effects=DATAFLOW),
    )(*[pltpu.with_memory_space_constraint(g, pltpu.HBM) for g in gs], *lands)
    return (out[0], out[1], list(out[2:2 + n]), list(out[2 + n:2 + 2 * n])), out[-1]


def _sibling_wait(send_sems, recv_sems, gs, lands, after, tag):
    n = len(gs)

    def body(*refs):
        for cp in _sibling_copies(gs, refs[:n], refs[n:2 * n], refs[2 * n], refs[2 * n + 1]):
            cp.wait_send()
            cp.wait_recv()

    out = pl.pallas_call(
        body, name=f"sibling_wait_{tag}",
        out_shape=(*[pltpu.HBM(g.shape, g.dtype) for g in gs], *[pltpu.HBM(l.shape, l.dtype) for l in lands]),
        in_specs=[HBM] * (2 * n) + [SEM, SEM, ANY], out_specs=tuple([HBM] * (2 * n)),
        input_output_aliases={i: i for i in range(2 * n)},
        compiler_params=pltpu.CompilerParams(has_side_effects=DATAFLOW),
    )(*gs, *lands, send_sems, recv_sems, after)
    return list(out[:n]), list(out[n:])


def _chips_start(hs, tag):
    n = len(hs)
    lands = [pltpu.with_memory_space_constraint(lax.empty((N_CHIPS,) + h.shape[-2:], h.dtype), pltpu.HBM) for h in hs]

    def body(*refs):
        ins, lnd = refs[:n], refs[n:2 * n]
        send_sems, recv_sems, token = refs[2 * n], refs[2 * n + 1], refs[-1]
        x, y, c = _mesh_pos()
        k = 2 * x + y
        piece = lambda t, kk: ins[t].at[kk] if hs[t].ndim == 3 else ins[t]
        for t in range(n):
            for j, (px, py) in enumerate(_other_chips(x, y)):
                _remote(piece(t, 2 * px + py), lnd[t].at[k], send_sems.at[3 * t + j], recv_sems.at[3 * t + j], (px, py, c)).start()
        token[...] = jnp.zeros_like(token)

    out = pl.pallas_call(
        body, name=f"chips_start_{tag}",
        out_shape=(pltpu.SemaphoreType.DMA((3 * n,)), pltpu.SemaphoreType.DMA((3 * n,)),
                   *[pltpu.HBM(h.shape, h.dtype) for h in hs], *[pltpu.HBM(l.shape, l.dtype) for l in lands],
                   jax.ShapeDtypeStruct((SUBLANES, LANES), F32)),
        in_specs=[HBM] * (2 * n), out_specs=(SEM, SEM, *[HBM] * (2 * n), pl.BlockSpec(memory_space=pltpu.VMEM)),
        input_output_aliases={i: 2 + i for i in range(2 * n)},
        compiler_params=pltpu.CompilerParams(has_side_effects=DATAFLOW),
    )(*[pltpu.with_memory_space_constraint(h, pltpu.HBM) for h in hs], *lands)
    return out[0], out[1], list(out[2:2 + n]), list(out[2 + n:2 + 2 * n]), out[-1]


def _chips_wait(send_sems, recv_sems, hs, lands, after, tag):
    n = len(hs)

    def body(*refs):
        ins, lnd = refs[:n], refs[n:2 * n]
        ssem, rsem = refs[2 * n], refs[2 * n + 1]
        x, y, c = _mesh_pos()
        k = 2 * x + y
        piece = lambda t, kk: ins[t].at[kk] if hs[t].ndim == 3 else ins[t]
        for t in range(n):
            for j, (px, py) in enumerate(_other_chips(x, y)):
                cp = _remote(piece(t, k), lnd[t].at[2 * px + py], ssem.at[3 * t + j], rsem.at[3 * t + j], (px, py, c))
                cp.wait_send()
                cp.wait_recv()

    out = pl.pallas_call(
        body, name=f"chips_wait_{tag}",
        out_shape=(*[pltpu.HBM(h.shape, h.dtype) for h in hs], *[pltpu.HBM(l.shape, l.dtype) for l in lands]),
        in_specs=[HBM] * (2 * n) + [SEM, SEM, ANY], out_specs=tuple([HBM] * (2 * n)),
        input_output_aliases={i: i for i in range(2 * n)},
        compiler_params=pltpu.CompilerParams(has_side_effects=DATAFLOW),
    )(*hs, *lands, send_sems, recv_sems, after)
    return list(out[:n]), list(out[n:])


def _add_chips(p, own, kc, name):
    _, P, Q = p.shape
    tr = P
    while N_CHIPS * tr * Q * 4 > 6 * 1024 * 1024 and tr % 16 == 0:
        tr //= 2
    sharded = own.ndim == 3

    def body(kc_ref, p_ref, own_ref, o_ref):
        k = kc_ref[0]
        mine = own_ref[...].reshape(tr, Q)
        v = [jnp.where(k == j, mine, p_ref[j]) for j in range(N_CHIPS)]
        o_ref[0] = ((v[0] + v[1]) + v[2]) + v[3]

    own_spec = (pl.BlockSpec((1, tr, Q), lambda i, kc_ref: (kc_ref[0], i, 0)) if sharded
                else pl.BlockSpec((tr, Q), lambda i, kc_ref: (i, 0)))
    return pl.pallas_call(
        body, name=name,
        grid_spec=pltpu.PrefetchScalarGridSpec(
            num_scalar_prefetch=1, grid=(P // tr,),
            in_specs=[pl.BlockSpec((N_CHIPS, tr, Q), lambda i, kc_ref: (0, i, 0)), own_spec],
            out_specs=pl.BlockSpec((1, tr, Q), lambda i, kc_ref: (kc_ref[1], i, 0))),
        out_shape=jax.ShapeDtypeStruct((2, P, Q), p.dtype),
        compiler_params=_cp(1),
    )(kc, p, own)


def _rs_share(fs, tag):
    n = len(fs)

    def body(*refs):
        outs = refs[n:2 * n]
        send_sems, recv_sems = refs[2 * n:]
        x, y, c = _mesh_pos()
        sends = [_remote(outs[t].at[c], outs[t].at[c], send_sems.at[t], recv_sems.at[t], (x, y, 1 - c)) for t in range(n)]
        for cp in sends:
            cp.start()
        for t in range(n):
            _remote(outs[t].at[c], outs[t].at[1 - c], send_sems.at[t], recv_sems.at[t], (x, y, 1 - c)).wait_recv()
        for cp in sends:
            cp.wait_send()

    return pl.pallas_call(
        body, name=f"rs_share_{tag}", in_specs=[ANY] * n, out_specs=[ANY] * n,
        out_shape=[jax.ShapeDtypeStruct(f.shape, f.dtype) for f in fs], input_output_aliases={t: t for t in range(n)},
        scratch_shapes=[pltpu.SemaphoreType.DMA((n,)), pltpu.SemaphoreType.DMA((n,))],
    )(*fs)


def _reduce_start(gs, kc, tag):
    return _reduce_continue(gs, _rs_send_sibling(gs, tag), kc, tag)


def _reduce_continue(gs, from_sibling, kc, tag):
    chip_sums = [_add_piece(g, r, kc[1:], name=f"add_piece_{tag}_{t}") for t, (g, r) in enumerate(zip(gs, from_sibling))]
    send_sems, recv_sems, chip_sums, lands, token = _chips_start(chip_sums, tag)
    return (send_sems, recv_sems, chip_sums, lands, tag), token


def _reduce_finish(states, kc, after):
    mine = []
    for send_sems, recv_sems, chip_sums, lands, tag in states:
        chip_sums, from_chips = _chips_wait(send_sems, recv_sems, chip_sums, lands, after, tag)
        mine += [_add_chips(p, h, kc, name=f"add_chips_{tag}_{t}") for t, (p, h) in enumerate(zip(from_chips, chip_sums))]
    return _rs_share(mine, "all")


def _adamw(w, g, m, v, name):
    R, C = w.shape
    tr = R
    for cand in (512, 256, 128, 64, 32, 16, 8):
        if R % cand == 0 and cand * C * 4 <= 2 * 1024 * 1024:
            tr = cand
            break
    c1 = 1.0 / (1.0 - ADAM_B1 ** ADAM_STEP)
    c2 = 1.0 / (1.0 - ADAM_B2 ** ADAM_STEP)

    def body(w_ref, g_ref, m_ref, v_ref, d_ref, mo_ref, vo_ref):
        gv = g_ref[...]
        mn = ADAM_B1 * m_ref[...] + (1.0 - ADAM_B1) * gv
        vn = ADAM_B2 * v_ref[...] + (1.0 - ADAM_B2) * (gv * gv)
        mo_ref[...] = mn
        vo_ref[...] = vn
        d_ref[...] = -ADAM_LR * ((mn * c1) / (jnp.sqrt(vn * c2) + ADAM_EPS) + ADAM_WD * w_ref[...])

    spec = pl.BlockSpec((tr, C), lambda i: (i, 0))
    shp = jax.ShapeDtypeStruct((R, C), F32)
    return pl.pallas_call(body, name=name, grid=(R // tr,), in_specs=[spec] * 4, out_specs=[spec] * 3, out_shape=[shp] * 3,
                          compiler_params=_cp(1))(w, g, m, v)


def _adamw_many(ws, gs, ms, vs):
    n = len(ws)
    c1 = 1.0 / (1.0 - ADAM_B1 ** ADAM_STEP)
    c2 = 1.0 / (1.0 - ADAM_B2 ** ADAM_STEP)

    def body(*refs):
        w_refs, g_refs, m_refs, v_refs = (refs[k * n:(k + 1) * n] for k in range(4))
        outs = refs[4 * n:]
        for i in range(n):
            gv = g_refs[i][...]
            mn = ADAM_B1 * m_refs[i][...] + (1.0 - ADAM_B1) * gv
            vn = ADAM_B2 * v_refs[i][...] + (1.0 - ADAM_B2) * (gv * gv)
            outs[3 * i][...] = -ADAM_LR * ((mn * c1) / (jnp.sqrt(vn * c2) + ADAM_EPS) + ADAM_WD * w_refs[i][...])
            outs[3 * i + 1][...] = mn
            outs[3 * i + 2][...] = vn

    out_shape = [jax.ShapeDtypeStruct(w.shape, F32) for w in ws for _ in range(3)]
    return pl.pallas_call(body, name="adamw_small", out_shape=out_shape)(*ws, *gs, *ms, *vs)


def _step(a):
    x, y, c = _mesh_pos()
    kc = jnp.stack([2 * x + y, c]).astype(jnp.int32)

    rs = _rows(sum(a[n].size for n in SH_SMALL))
    first, later = ["ev_w_in", "ev_w_out"], ["od_w_in", "od_w_out", "ffn_w_up", "ffn_w_down"]
    lead = lambda w: w if w.ndim == 3 else w[None]
    *full, gs = _gather_weights(first, [a[n].astype(BF16) for n in first], _pack([a[n] for n in SH_SMALL], rs, F32))
    p = {n: a[n] for n in REP}
    p.update({n: lead(w) for n, w in zip(first, full)})
    parts = [_unpack(gs[k], [a[n].shape for n in SH_SMALL]) for k in range(N_CHIPS)]
    for i, n in enumerate(SH_SMALL):
        p[n] = jnp.concatenate([parts[k][i] for k in range(N_CHIPS)], axis=SH_SMALL[n])
    shards = lax.optimization_barrier(([a[n].astype(BF16) for n in later], full))[0]
    g_send, g_recv, shards, bufs, token = _gather_start(later, shards, _place_own(later, shards))
    p["norm_mix"] = p["norm_mix"] + token[0:1, 0:1]

    def late(after):
        got = _gather_forward(later, _gather_wait(later, g_send, g_recv, shards, bufs, after))
        return {n: lead(w) for n, w in zip(later, got)}

    p["late"] = late

    states, pending = {}, {}

    def start_reduce(tag, gs, split=False):
        if split:
            pending[tag], token = _sibling_start(gs, tag)
        else:
            states[tag], token = _reduce_start(gs, kc, tag)
        return token

    def continue_reduce(tag, after):
        send_sems, recv_sems, gs, lands = pending.pop(tag)
        gs, from_sibling = _sibling_wait(send_sems, recv_sems, gs, lands, after, tag)
        states[tag], token = _reduce_continue(gs, from_sibling, kc, tag)
        return token

    sq8, grad_x, grads, big = _local_step(a["x"][0], a["loss_target"][0], p, start_reduce, continue_reduce)
    loss = lax.psum(0.5 / D_MODEL * jnp.sum(sq8), ("x", "y", "c"))

    r_s = _rows(sum(a[n].size for n in SH_SMALL), 2 * SUBLANES) // 2
    small_pieces = []
    for k in range(N_CHIPS):
        pieces = [lax.slice_in_dim(grads[n], k * a[n].shape[ax], (k + 1) * a[n].shape[ax], axis=ax) for n, ax in SH_SMALL.items()]
        small_pieces.append(_pack(pieces, 2 * r_s, F32).reshape(2, r_s, LANES))
    g_small = jnp.stack(small_pieces)
    r_r = _rows(sum(a[n].size for n in REP), 2 * SUBLANES) // 2
    g_rep = _pack([grads[n] for n in REP], 2 * r_r, F32).reshape(2, r_r, LANES)
    start_reduce("g4", [g_small, g_rep])
    reduced = _reduce_finish([states[tag] for tag in ("g1", "g2", "g3", "g4")], kc, grad_x)
    red = dict(zip([key for tag in ("g1", "g2", "g3") for key in REDUCE_GROUPS[tag]], reduced))
    gfin = {}
    for n in ("ev_w_in", "ev_w_out", "od_w_in", "od_w_out"):
        gfin[n] = red[n, 0].reshape(a[n].shape)
    for n in ("ffn_w_up", "ffn_w_down"):
        gfin[n] = jnp.stack([red[n, l].reshape(a[n].shape[1:]) for l in range(2)])
    gfin.update(zip(SH_SMALL, _unpack(reduced[-2], [a[n].shape for n in SH_SMALL])))
    gfin.update(zip(REP, _unpack(reduced[-1], [a[n].shape for n in REP])))

    out = {"loss": loss, "grad_x": grad_x[None]}
    small_names = list(SH_SMALL) + REP
    for n in SH_BIG:
        shp = a[n].shape
        two_d = lambda t: t.reshape(-1, shp[-1])
        d, mo, vo = _adamw(two_d(a[n]), two_d(gfin[n]), two_d(a["m_" + n]), two_d(a["v_" + n]), name=f"adamw_{n}")
        out["delta_" + n], out["new_m_" + n], out["new_v_" + n] = d.reshape(shp), mo.reshape(shp), vo.reshape(shp)
    two_d = lambda t: t.reshape(-1, t.shape[-1])
    res = _adamw_many(*[[two_d(src(n)) for n in small_names]
                        for src in (lambda n: a[n], lambda n: gfin[n], lambda n: a["m_" + n], lambda n: a["v_" + n])])
    for i, n in enumerate(small_names):
        out["delta_" + n], out["new_m_" + n], out["new_v_" + n] = (r.reshape(a[n].shape) for r in res[3 * i:3 * i + 3])
    for n in WEIGHTS:
        out["grad_" + n] = gfin[n]
    return out


def kernel(x, norm_mix, norm_ffn, norm_final, ev_w_in, ev_conv_w, ev_conv_b, ev_gate_a_w, ev_gate_a_b, ev_gate_x_w, ev_gate_x_b, ev_lru_lambda, hg_lb_logits, ev_hg_norm, ev_w_out, od_w_in, od_b_in, od_ln_g, od_ln_b, od_w_s, od_b_s, od_w_out, ffn_w_up, ffn_conv_w, ffn_conv_b, ffn_w_down, loss_target, m_norm_mix, m_norm_ffn, m_norm_final, m_ev_w_in, m_ev_conv_w, m_ev_conv_b, m_ev_gate_a_w, m_ev_gate_a_b, m_ev_gate_x_w, m_ev_gate_x_b, m_ev_lru_lambda, m_hg_lb_logits, m_ev_hg_norm, m_ev_w_out, m_od_w_in, m_od_b_in, m_od_ln_g, m_od_ln_b, m_od_w_s, m_od_b_s, m_od_w_out, m_ffn_w_up, m_ffn_conv_w, m_ffn_conv_b, m_ffn_w_down, v_norm_mix, v_norm_ffn, v_norm_final, v_ev_w_in, v_ev_conv_w, v_ev_conv_b, v_ev_gate_a_w, v_ev_gate_a_b, v_ev_gate_x_w, v_ev_gate_x_b, v_ev_lru_lambda, v_hg_lb_logits, v_ev_hg_norm, v_ev_w_out, v_od_w_in, v_od_b_in, v_od_ln_g, v_od_ln_b, v_od_w_s, v_od_b_s, v_od_w_out, v_ffn_w_up, v_ffn_conv_w, v_ffn_conv_b, v_ffn_w_down):
    vals = (x, norm_mix, norm_ffn, norm_final, ev_w_in, ev_conv_w, ev_conv_b, ev_gate_a_w, ev_gate_a_b, ev_gate_x_w, ev_gate_x_b, ev_lru_lambda, hg_lb_logits, ev_hg_norm, ev_w_out, od_w_in, od_b_in, od_ln_g, od_ln_b, od_w_s, od_b_s, od_w_out, ffn_w_up, ffn_conv_w, ffn_conv_b, ffn_w_down, loss_target, m_norm_mix, m_norm_ffn, m_norm_final, m_ev_w_in, m_ev_conv_w, m_ev_conv_b, m_ev_gate_a_w, m_ev_gate_a_b, m_ev_gate_x_w, m_ev_gate_x_b, m_ev_lru_lambda, m_hg_lb_logits, m_ev_hg_norm, m_ev_w_out, m_od_w_in, m_od_b_in, m_od_ln_g, m_od_ln_b, m_od_w_s, m_od_b_s, m_od_w_out, m_ffn_w_up, m_ffn_conv_w, m_ffn_conv_b, m_ffn_w_down, v_norm_mix, v_norm_ffn, v_norm_final, v_ev_w_in, v_ev_conv_w, v_ev_conv_b, v_ev_gate_a_w, v_ev_gate_a_b, v_ev_gate_x_w, v_ev_gate_x_b, v_ev_lru_lambda, v_hg_lb_logits, v_ev_hg_norm, v_ev_w_out, v_od_w_in, v_od_b_in, v_od_ln_g, v_od_ln_b, v_od_w_s, v_od_b_s, v_od_w_out, v_ffn_w_up, v_ffn_conv_w, v_ffn_conv_b, v_ffn_w_down)
    names = ["x"] + WEIGHTS + ["loss_target"] + ["m_" + n for n in WEIGHTS] + ["v_" + n for n in WEIGHTS]
    out = _step(dict(zip(names, vals)))
    return (out["loss"], out["grad_x"], *[out["grad_" + n] for n in WEIGHTS], *[out["delta_" + n] for n in WEIGHTS],
            *[out["new_m_" + n] for n in WEIGHTS], *[out["new_v_" + n] for n in WEIGHTS])
```

```python
import functools

import jax
import jax.numpy as jnp
from jax import lax
from jax.experimental import pallas as pl
from jax.experimental.pallas import tpu as pltpu

F32 = jnp.float32
BF16 = jnp.bfloat16

EPS = 1e-6
D_MODEL = 1024
LRU_W = 512
LRU_BLOCKS = 8
LRU_C = 8.0
HG_HEADS = 4
HG_D = 128
HG_CHUNK = 64
SGU_G = 8
SGU_CHUNK = 128
D_FF = 2816
ADAM_LR, ADAM_B1, ADAM_B2, ADAM_EPS, ADAM_WD, ADAM_STEP = 0.001, 0.9, 0.999, 1e-08, 0.01, 10

V7X_VMEM_BYTES = 64 * 1024 * 1024
VMEM_LIMIT = V7X_VMEM_BYTES - 8 * 1024 * 1024
SUBLANES = 8
LANES = 128
BF16_ROWS = 16

GELU_C0 = 0.7978845608028654
GELU_C1 = 0.044715

NN = (((1,), (0,)), ((), ()))
NT = (((1,), (1,)), ((), ()))
TN = (((0,), (0,)), ((), ()))


def _dot(a, b, dims=NN):
    return lax.dot_general(a.astype(BF16), b.astype(BF16), dims, preferred_element_type=F32)


def _cp(n_grid):
    return pltpu.CompilerParams(dimension_semantics=("arbitrary",) * n_grid, vmem_limit_bytes=VMEM_LIMIT)


def _chunk(n, cap):
    best = LANES
    for c in range(LANES, cap + 1, LANES):
        if n % c == 0:
            best = c
    return best


def _resident(shape):
    nd = len(shape)
    return pl.BlockSpec(shape, lambda *_: (0,) * nd, pipeline_mode=pl.Buffered(1))


def _rsum8(x):
    r, c = x.shape
    return x.reshape(r // SUBLANES, SUBLANES, c).sum(axis=0)


def _sigmoid(x):
    return 0.5 * jnp.tanh(0.5 * x) + 0.5


def _gelu(x):
    return 0.5 * x * (1.0 + jnp.tanh(GELU_C0 * (x + GELU_C1 * x * x * x)))


def _gelu_grad(x):
    t = jnp.tanh(GELU_C0 * (x + GELU_C1 * x * x * x))
    return 0.5 * (1.0 + t) + 0.5 * x * (1.0 - t * t) * GELU_C0 * (1.0 + 3.0 * GELU_C1 * x * x)


def _silu_and_grad(x):
    s = _sigmoid(x)
    return x * s, s * (1.0 + x * (1.0 - s))


def _shift_rows(e, j):
    n = e.shape[0]
    return e if j % n == 0 else pltpu.roll(e, j % n, 0)


def _weight(w):
    if isinstance(w, tuple):
        stack, layer = w
        K, N = stack.shape[1:]
        return stack, pl.BlockSpec((None, K, N), lambda *_: (layer, 0, 0), pipeline_mode=pl.Buffered(1)), (K, N)
    return w, _resident(w.shape), w.shape


def _norm_mm(h, g, w, b, name, tt=1024):
    T, D = h.shape
    w, w_spec, (_, N) = _weight(w)
    cn = _chunk(N, 512)

    def body(h_ref, g_ref, w_ref, b_ref, hn_ref, z_ref):
        x = h_ref[...]
        r = lax.rsqrt(jnp.mean(x * x, axis=-1, keepdims=True) + EPS)
        hn = (x * r * g_ref[...]).astype(BF16)
        hn_ref[...] = hn
        for j in range(0, N, cn):
            acc = jnp.dot(hn, w_ref[:, j:j + cn], preferred_element_type=F32) + b_ref[:, j:j + cn]
            z_ref[:, j:j + cn] = acc.astype(BF16)

    return pl.pallas_call(
        body, name=name, grid=(T // tt,),
        in_specs=[pl.BlockSpec((tt, D), lambda i: (i, 0)), _resident((1, D)), w_spec, _resident((1, N))],
        out_specs=[pl.BlockSpec((tt, D), lambda i: (i, 0)), pl.BlockSpec((tt, N), lambda i: (i, 0))],
        out_shape=[jax.ShapeDtypeStruct((T, D), BF16), jax.ShapeDtypeStruct((T, N), BF16)],
        compiler_params=_cp(1),
    )(h, g, w, b)


def _mm(a, w, res, out_dtype, name, tt=1024, transpose_w=False):
    T, K = a.shape
    N = w.shape[0] if transpose_w else w.shape[1]
    cn = _chunk(N, 512)
    has_res = res is not None

    def body(*refs):
        a_ref, w_ref = refs[0], refs[1]
        res_ref = refs[2] if has_res else None
        o_ref = refs[-1]
        av = a_ref[...].astype(BF16)
        for j in range(0, N, cn):
            if transpose_w:
                acc = lax.dot_general(av, w_ref[j:j + cn, :], NT, preferred_element_type=F32)
            else:
                acc = jnp.dot(av, w_ref[:, j:j + cn], preferred_element_type=F32)
            if has_res:
                acc = acc + res_ref[:, j:j + cn]
            o_ref[:, j:j + cn] = acc.astype(out_dtype)

    in_specs = [pl.BlockSpec((tt, K), lambda i: (i, 0)), _resident(w.shape)]
    args = [a, w]
    if has_res:
        in_specs.append(pl.BlockSpec((tt, N), lambda i: (i, 0)))
        args.append(res)
    return pl.pallas_call(
        body, name=name, grid=(T // tt,), in_specs=in_specs,
        out_specs=pl.BlockSpec((tt, N), lambda i: (i, 0)),
        out_shape=jax.ShapeDtypeStruct((T, N), out_dtype),
        compiler_params=_cp(1),
    )(*args)


def _mm_tn(a, b, name, col_shards=1, tt=2048):
    T, K = a.shape
    N = b.shape[1]
    ns = N // col_shards
    tt = min(tt, T)
    while 2 * (tt * K * a.dtype.itemsize + tt * ns * b.dtype.itemsize + K * ns * 4) > VMEM_LIMIT - 12 * 1024 * 1024:
        tt //= 2

    def body(a_ref, b_ref, o_ref):
        acc = lax.dot_general(a_ref[...].astype(BF16), b_ref[...].astype(BF16), TN, preferred_element_type=F32)
        prev = jnp.where(pl.program_id(1) == 0, 0.0, o_ref[0])
        o_ref[0] = prev + acc

    out = pl.pallas_call(
        body, name=name, grid=(col_shards, T // tt),
        in_specs=[pl.BlockSpec((tt, K), lambda n, t: (t, 0)), pl.BlockSpec((tt, ns), lambda n, t: (t, n))],
        out_specs=pl.BlockSpec((1, K, ns), lambda n, t: (n, 0, 0)),
        out_shape=jax.ShapeDtypeStruct((col_shards, K, ns), F32),
        compiler_params=_cp(2),
    )(a, b)
    return out if col_shards > 1 else out[0]


def _mm_normbwd(dz, w, x, g, dres, name):
    T, N = dz.shape
    D = w.shape[0]
    tt = 1024 if N <= 3072 else 512

    def body(dz_ref, wt_ref, x_ref, g_ref, dres_ref, dx_ref, dg_ref):
        @pl.when(pl.program_id(0) == 0)
        def _():
            dg_ref[...] = jnp.zeros_like(dg_ref)

        dy = lax.dot_general(dz_ref[...], wt_ref[...], NT, preferred_element_type=F32)
        x = x_ref[...]
        r = lax.rsqrt(jnp.mean(x * x, axis=-1, keepdims=True) + EPS)
        xn = x * r
        dg_ref[...] += _rsum8(dy * xn)
        dxn = dy * g_ref[...]
        dx_ref[...] = dres_ref[...] + r * (dxn - xn * jnp.mean(dxn * xn, axis=-1, keepdims=True))

    return pl.pallas_call(
        body, name=name, grid=(T // tt,),
        in_specs=[pl.BlockSpec((tt, N), lambda i: (i, 0)), _resident((D, N)), pl.BlockSpec((tt, D), lambda i: (i, 0)),
                  _resident((1, D)), pl.BlockSpec((tt, D), lambda i: (i, 0))],
        out_specs=[pl.BlockSpec((tt, D), lambda i: (i, 0)), pl.BlockSpec((SUBLANES, D), lambda i: (0, 0))],
        out_shape=[jax.ShapeDtypeStruct((T, D), F32), jax.ShapeDtypeStruct((SUBLANES, D), F32)],
        compiler_params=_cp(1),
    )(dz, w, x, g, dres)


def _col_groups(F, cc, per_group=4):
    step = cc * per_group
    return [(g0, min(g0 + step, F)) for g0 in range(0, F, step)]


def _loss_head(x, gv, tgt):
    r = lax.rsqrt(jnp.mean(x * x, axis=-1, keepdims=True) + EPS)
    xn = x * r
    diff = xn * gv - tgt
    dy = diff * (1.0 / x.shape[-1])
    dxn = dy * gv
    return r * (dxn - xn * jnp.mean(dxn * xn, axis=-1, keepdims=True)), _rsum8(diff * diff), _rsum8(dy * xn)


def _ffn_act(gu, cw, cb, w_down, res, name, loss=None, tt=512):
    T = gu.shape[0]
    F = gu.shape[1] // 2
    w_down, wd_spec, (_, D) = _weight(w_down)
    cc = _chunk(F, 256)
    hb = tt // BF16_ROWS

    def body(gu_ref, halo_ref, cw_ref, cb_ref, wd_ref, res_ref, *rest):
        if loss is None:
            a_ref, gc_ref, o_ref = rest
        else:
            gf_ref, t_ref, a_ref, gc_ref, o_ref, sq_ref, dgf_ref = rest
        first = pl.program_id(0) == 0
        acc = res_ref[...]
        for g0, g1 in _col_groups(F, cc):
            for c0 in range(g0, g1, cc):
                cs = slice(c0, c0 + cc)
                x = gu_ref[:, cs].astype(F32)
                halo = jnp.where(first, 0.0, halo_ref[:, cs].astype(F32))
                e = jnp.concatenate([halo, x], axis=0)
                gc = (cb_ref[:, cs] + cw_ref[0:1, cs] * _shift_rows(e, 2)[BF16_ROWS:] + cw_ref[1:2, cs] * _shift_rows(e, 1)[BF16_ROWS:]
                      + cw_ref[2:3, cs] * x)
                up = gu_ref[:, F + c0:F + c0 + cc].astype(F32)
                gc_ref[:, cs] = gc.astype(BF16)
                a_ref[:, cs] = (gc * _sigmoid(gc) * up).astype(BF16)
            acc = acc + jnp.dot(a_ref[:, g0:g1], wd_ref[g0:g1, :], preferred_element_type=F32)
        if loss is None:
            o_ref[...] = acc
        else:
            @pl.when(first)
            def _():
                sq_ref[...] = jnp.zeros_like(sq_ref)
                dgf_ref[...] = jnp.zeros_like(dgf_ref)

            dx, sq, dgf = _loss_head(acc, gf_ref[...], t_ref[...])
            o_ref[...] = dx
            sq_ref[...] += sq
            dgf_ref[...] += dgf

    tok = lambda w: pl.BlockSpec((tt, w), lambda i: (i, 0))
    acc8 = pl.BlockSpec((SUBLANES, D), lambda i: (0, 0))
    in_specs = [tok(2 * F), pl.BlockSpec((BF16_ROWS, F), lambda i: (jnp.maximum(i * hb - 1, 0), 0)),
                _resident((SUBLANES, F)), _resident((1, F)), wd_spec, tok(D)]
    out_specs = [tok(F), tok(F), tok(D)]
    out_shape = [jax.ShapeDtypeStruct((T, F), BF16), jax.ShapeDtypeStruct((T, F), BF16), jax.ShapeDtypeStruct((T, D), F32)]
    args = [gu, gu, cw, cb, w_down, res]
    if loss is not None:
        in_specs += [_resident((1, D)), tok(D)]
        out_specs += [acc8, acc8]
        out_shape += [jax.ShapeDtypeStruct((SUBLANES, D), F32)] * 2
        args += list(loss)
    return pl.pallas_call(body, name=name, grid=(T // tt,), in_specs=in_specs, out_specs=out_specs, out_shape=out_shape,
                          compiler_params=_cp(1))(*args)


def _ffn_act_bwd(gu, gc, cw, w_up, w_down, x, g, dh, name, tt=256):
    T = gu.shape[0]
    F = gu.shape[1] // 2
    w_up, wu_spec, (D, _) = _weight(w_up)
    w_down, wd_spec, _ = _weight(w_down)
    cc = _chunk(F, 256)
    hb = tt // BF16_ROWS
    last_hb = T // BF16_ROWS - 1
    nt = T // tt

    def body(gu_ref, upnext_ref, gc_ref, gcnext_ref, cw_ref, wu_ref, wd_ref, x_ref, g_ref, dh_ref, dhnext_ref,
             dgu_ref, dc_ref, dx_ref, dg_ref):
        i = pl.program_id(0)

        @pl.when(i == 0)
        def _():
            dc_ref[...] = jnp.zeros_like(dc_ref)
            dg_ref[...] = jnp.zeros_like(dg_ref)

        n = tt + BF16_ROWS
        ext = lambda main, nxt: jnp.concatenate([main.astype(F32), nxt.astype(F32)], axis=0)
        dhe = ext(dh_ref[...], jnp.where(i == nt - 1, 0.0, dhnext_ref[...])).astype(BF16)
        dy = jnp.zeros((tt, D), F32)
        groups = _col_groups(F, cc)
        da_of = lambda grp: lax.dot_general(dhe, wd_ref[grp[0]:grp[1], :], NT, preferred_element_type=F32)
        da_next = da_of(groups[0])
        for gi, (lo, hi) in enumerate(groups):
            da, da_next = da_next, (da_of(groups[gi + 1]) if gi + 1 < len(groups) else None)
            for c0 in range(lo, hi, cc):
                cs = slice(c0, c0 + cc)
                us = slice(F + c0, F + c0 + cc)
                gc = ext(gc_ref[:, cs], gcnext_ref[:, cs])
                up = ext(gu_ref[:, us], upnext_ref[:, cs])
                dae = da[:, c0 - lo:c0 - lo + cc]
                s, ds = _silu_and_grad(gc)
                dgc = dae * up * ds
                dgu_ref[:, us] = (dae * s)[:tt].astype(BF16)
                dgc1 = _shift_rows(dgc, n - 1)[:tt]
                dgc2 = _shift_rows(dgc, n - 2)[:tt]
                dm = dgc[:tt]
                dgu_ref[:, cs] = (cw_ref[2:3, cs] * dm + cw_ref[1:2, cs] * dgc1 + cw_ref[0:1, cs] * dgc2).astype(BF16)
                gt = gu_ref[:, cs].astype(F32)
                dc_ref[0, :, cs] += _rsum8(dgc2 * gt)
                dc_ref[1, :, cs] += _rsum8(dgc1 * gt)
                dc_ref[2, :, cs] += _rsum8(dm * gt)
                dc_ref[3, :, cs] += _rsum8(dm)
            dy = (dy + lax.dot_general(dgu_ref[:, lo:hi], wu_ref[:, lo:hi], NT, preferred_element_type=F32)
                  + lax.dot_general(dgu_ref[:, F + lo:F + hi], wu_ref[:, F + lo:F + hi], NT, preferred_element_type=F32))
        xv = x_ref[...]
        r = lax.rsqrt(jnp.mean(xv * xv, axis=-1, keepdims=True) + EPS)
        xn = xv * r
        dg_ref[...] += _rsum8(dy * xn)
        dxn = dy * g_ref[...]
        dx_ref[...] = dh_ref[...] + r * (dxn - xn * jnp.mean(dxn * xn, axis=-1, keepdims=True))

    tok = lambda w: pl.BlockSpec((tt, w), lambda i: (i, 0))
    nxt = lambda w, col: pl.BlockSpec((BF16_ROWS, w), lambda i: (jnp.minimum((i + 1) * hb, last_hb), col))
    return pl.pallas_call(
        body, name=name, grid=(nt,),
        in_specs=[tok(2 * F), nxt(F, 1), tok(F), nxt(F, 0),
                  _resident((SUBLANES, F)), wu_spec, wd_spec, tok(D), _resident((1, D)), tok(D), nxt(D, 0)],
        out_specs=[tok(2 * F), pl.BlockSpec((4, SUBLANES, F), lambda i: (0, 0, 0)), tok(D),
                   pl.BlockSpec((SUBLANES, D), lambda i: (0, 0))],
        out_shape=[jax.ShapeDtypeStruct((T, 2 * F), BF16), jax.ShapeDtypeStruct((4, SUBLANES, F), F32),
                   jax.ShapeDtypeStruct((T, D), F32), jax.ShapeDtypeStruct((SUBLANES, D), F32)],
        compiler_params=_cp(1),
    )(gu, gu, gc, gc, cw, w_up, w_down, x, g, dh, dh)


def _softplus_neg(lam):
    x = -lam
    y = jnp.exp(-jnp.abs(x))
    l1p = jnp.where(y < 0.01, y * (1.0 - y * (0.5 - y * (1.0 / 3.0))), jnp.log(1.0 + y))
    return jnp.maximum(x, 0.0) + l1p


def _lru_gates(xc, wa_ref, ba_ref, wx_ref, bx_ref, sp):
    xcb = xc.astype(BF16)
    r = _sigmoid(jnp.dot(xcb, wa_ref[...], preferred_element_type=F32) + ba_ref[...])
    gi = _sigmoid(jnp.dot(xcb, wx_ref[...], preferred_element_type=F32) + bx_ref[...])
    log_a = -LRU_C * r * sp
    a = jnp.exp(log_a)
    x2 = 2.0 * log_a
    series = -x2 * (1.0 + x2 * 0.5 * (1.0 + x2 * (1.0 / 3.0)))
    om = jnp.where(x2 > -0.02, series, 1.0 - a * a)
    return r, gi, a, jnp.sqrt(om)


def _lru_conv(xr, halo, cw_ref, cb_ref):
    e = jnp.concatenate([halo, xr], axis=0)
    x1 = _shift_rows(e, 1)[BF16_ROWS:]
    x2 = _shift_rows(e, 2)[BF16_ROWS:]
    x3 = _shift_rows(e, 3)[BF16_ROWS:]
    xc = cb_ref[...] + cw_ref[0:1, :] * x3 + cw_ref[1:2, :] * x2 + cw_ref[2:3, :] * x1 + cw_ref[3:4, :] * xr
    return xc, x1, x2, x3


def _lru_fwd(z, cw, cb, wa, ba, wx, bx, lam, name="lru_fwd", tt=256):
    T = z.shape[0]
    W = LRU_W
    hb = tt // BF16_ROWS
    ng = tt // SUBLANES

    def body(z_ref, halo_ref, cw_ref, cb_ref, wa_ref, ba_ref, wx_ref, bx_ref, lam_ref, oa_ref, h_ref, a_s, sv_ref, u_s, hc):
        i = pl.program_id(0)

        @pl.when(i == 0)
        def _():
            hc[...] = jnp.zeros_like(hc)

        xr = z_ref[:, W:2 * W].astype(F32)
        halo = jnp.where(i == 0, 0.0, halo_ref[...].astype(F32))
        xc, _, _, _ = _lru_conv(xr, halo, cw_ref, cb_ref)
        sp = _softplus_neg(lam_ref[...])
        r, gi, a, mult = _lru_gates(xc, wa_ref, ba_ref, wx_ref, bx_ref, sp)
        a_s[...] = a
        u_s[...] = mult * gi * xc
        for k, saved in enumerate((mult, r, gi, xc)):
            sv_ref[:, k * W:(k + 1) * W] = saved.astype(BF16)
        row = lax.broadcasted_iota(jnp.int32, (SUBLANES, W), 0)

        def step(j, hprev):
            r0 = pl.multiple_of(j * SUBLANES, SUBLANES)
            A = a_s[pl.ds(r0, SUBLANES), :]
            U = u_s[pl.ds(r0, SUBLANES), :]
            for k in (1, 2, 4):
                m = row >= k
                U = jnp.where(m, A * pltpu.roll(U, k, 0) + U, U)
                A = jnp.where(m, A * pltpu.roll(A, k, 0), A)
            H = U + A * hprev
            h_ref[pl.ds(r0, SUBLANES), :] = H
            return jnp.broadcast_to(H[SUBLANES - 1:SUBLANES, :], (SUBLANES, W))

        hc[...] = lax.fori_loop(0, ng, step, hc[...])
        oa_ref[...] = (_gelu(z_ref[:, 0:W].astype(F32)) * h_ref[...]).astype(BF16)

    return pl.pallas_call(
        body, name=name, grid=(T // tt,),
        in_specs=[pl.BlockSpec((tt, 2 * W), lambda i: (i, 0)),
                  pl.BlockSpec((BF16_ROWS, W), lambda i: (jnp.maximum(i * hb - 1, 0), 1)),
                  _resident((SUBLANES, W)), _resident((1, W)), _resident((W, W)), _resident((1, W)),
                  _resident((W, W)), _resident((1, W)), _resident((1, W))],
        out_specs=[pl.BlockSpec((tt, W), lambda i: (i, 0)), pl.BlockSpec((tt, W), lambda i: (i, 0)),
                   pl.BlockSpec((tt, W), lambda i: (i, 0)), pl.BlockSpec((tt, 4 * W), lambda i: (i, 0))],
        out_shape=[jax.ShapeDtypeStruct((T, 2 * W), BF16), jax.ShapeDtypeStruct((T, W), F32),
                   jax.ShapeDtypeStruct((T, W), F32), jax.ShapeDtypeStruct((T, 4 * W), BF16)],
        scratch_shapes=[pltpu.VMEM((tt, W), F32), pltpu.VMEM((SUBLANES, W), F32)],
        compiler_params=_cp(1),
    )(z, z, cw, cb, wa, ba, wx, bx, lam)


def _lru_bwd(z, hseq, a_all, saved, dmix, cw, wat, wxt, lam, name="lru_bwd", tt=256):
    T = z.shape[0]
    W = LRU_W
    nt = T // tt
    sb = tt // SUBLANES
    ng = tt // SUBLANES

    def body(z_ref, h_ref, hprev_ref, a_ref, sv_ref, dm_ref, cw_ref, wat_ref, wxt_ref, lam_ref,
             dz_ref, dc_ref, dwa_ref, dwx_ref, dv_ref, c_s, d_s, g_s, gc, an, dxn):
        i = pl.program_id(0)
        ti = nt - 1 - i

        @pl.when(i == 0)
        def _():
            dc_ref[...] = jnp.zeros_like(dc_ref)
            dwa_ref[...] = jnp.zeros_like(dwa_ref)
            dwx_ref[...] = jnp.zeros_like(dwx_ref)
            dv_ref[...] = jnp.zeros_like(dv_ref)
            gc[...] = jnp.zeros_like(gc)
            an[...] = jnp.zeros_like(an)
            dxn[...] = jnp.zeros_like(dxn)

        xr = z_ref[:, W:2 * W].astype(F32)
        yg = z_ref[:, 0:W].astype(F32)
        sp = _softplus_neg(lam_ref[...])
        a = a_ref[...]
        mult, r, gi, xc = (sv_ref[:, k * W:(k + 1) * W].astype(F32) for k in range(4))
        h = h_ref[...]
        hp = jnp.where(ti == 0, 0.0, hprev_ref[...])
        hm1 = _shift_rows(jnp.concatenate([hp, h], axis=0), 1)[SUBLANES:]
        dout = dm_ref[...].astype(F32)
        d_s[...] = dout * _gelu(yg)
        dz_ref[:, 0:W] = (dout * h * _gelu_grad(yg)).astype(BF16)
        c_s[...] = _shift_rows(jnp.concatenate([a, an[...]], axis=0), tt + SUBLANES - 1)[:tt]
        an[...] = a[0:SUBLANES, :]
        row = lax.broadcasted_iota(jnp.int32, (SUBLANES, W), 0)

        def step(j, gnext):
            r0 = pl.multiple_of((ng - 1 - j) * SUBLANES, SUBLANES)
            C = c_s[pl.ds(r0, SUBLANES), :]
            G = d_s[pl.ds(r0, SUBLANES), :]
            for k in (1, 2, 4):
                m = row < SUBLANES - k
                G = jnp.where(m, G + C * pltpu.roll(G, SUBLANES - k, 0), G)
                C = jnp.where(m, C * pltpu.roll(C, SUBLANES - k, 0), C)
            G = G + C * gnext
            g_s[pl.ds(r0, SUBLANES), :] = G
            return jnp.broadcast_to(G[0:1, :], (SUBLANES, W))

        gc[...] = lax.fori_loop(0, ng, step, gc[...])
        du = g_s[...]
        da = du * hm1
        dgi = du * mult * xc
        dxc = du * mult * gi
        dmult = du * gi * xc
        dlog_a = da * a - dmult * (a * a) / mult
        dr = dlog_a * (-LRU_C * sp)
        dv_ref[2] += _rsum8(dlog_a * (-LRU_C * r))
        dpr = (dr * r * (1.0 - r)).astype(BF16)
        dpi = (dgi * gi * (1.0 - gi)).astype(BF16)
        dv_ref[0] += _rsum8(dpr.astype(F32))
        dv_ref[1] += _rsum8(dpi.astype(F32))
        xcb = sv_ref[:, 3 * W:4 * W]
        dwa_ref[...] += lax.dot_general(xcb, dpr, TN, preferred_element_type=F32)
        dwx_ref[...] += lax.dot_general(xcb, dpi, TN, preferred_element_type=F32)
        dxc = dxc + jnp.dot(dpr, wat_ref[...], preferred_element_type=F32) + jnp.dot(dpi, wxt_ref[...], preferred_element_type=F32)
        n = tt + BF16_ROWS
        de = jnp.concatenate([dxc, dxn[...]], axis=0)
        d1, d2, d3 = (_shift_rows(de, n - j)[:tt] for j in (1, 2, 3))
        dxn[...] = dxc[0:BF16_ROWS, :]
        dz_ref[:, W:2 * W] = (cw_ref[3:4, :] * dxc + cw_ref[2:3, :] * d1 + cw_ref[1:2, :] * d2 + cw_ref[0:1, :] * d3).astype(BF16)
        dc_ref[0] += _rsum8(d3 * xr)
        dc_ref[1] += _rsum8(d2 * xr)
        dc_ref[2] += _rsum8(d1 * xr)
        dc_ref[3] += _rsum8(dxc * xr)
        dc_ref[4] += _rsum8(dxc)

    rev = lambda i: nt - 1 - i
    tok = lambda w: pl.BlockSpec((tt, w), lambda i: (rev(i), 0))
    return pl.pallas_call(
        body, name=name, grid=(nt,),
        in_specs=[tok(2 * W), tok(W),
                  pl.BlockSpec((SUBLANES, W), lambda i: (jnp.maximum(rev(i) * sb - 1, 0), 0)),
                  tok(W), tok(4 * W), tok(W),
                  _resident((SUBLANES, W)), _resident((W, W)), _resident((W, W)), _resident((1, W))],
        out_specs=[pl.BlockSpec((tt, 2 * W), lambda i: (rev(i), 0)),
                   pl.BlockSpec((5, SUBLANES, W), lambda i: (0, 0, 0)),
                   pl.BlockSpec((W, W), lambda i: (0, 0)), pl.BlockSpec((W, W), lambda i: (0, 0)),
                   pl.BlockSpec((3, SUBLANES, W), lambda i: (0, 0, 0))],
        out_shape=[jax.ShapeDtypeStruct((T, z.shape[1]), BF16), jax.ShapeDtypeStruct((5, SUBLANES, W), F32),
                   jax.ShapeDtypeStruct((W, W), F32), jax.ShapeDtypeStruct((W, W), F32),
                   jax.ShapeDtypeStruct((3, SUBLANES, W), F32)],
        scratch_shapes=[pltpu.VMEM((tt, W), F32), pltpu.VMEM((tt, W), F32), pltpu.VMEM((tt, W), F32),
                        pltpu.VMEM((SUBLANES, W), F32), pltpu.VMEM((SUBLANES, W), F32), pltpu.VMEM((BF16_ROWS, W), F32)],
        compiler_params=_cp(1),
    )(z, hseq, hseq, a_all, saved, dmix, cw, wat, wxt, lam)


def _split3(x):
    hi = x.astype(BF16)
    r1 = x - hi.astype(F32)
    mid = r1.astype(BF16)
    lo = (r1 - mid.astype(F32)).astype(BF16)
    return hi, mid, lo


def _tri_matmul(tri, x):
    hi, mid, lo = _split3(x)
    return (jnp.dot(tri, hi, preferred_element_type=F32) + jnp.dot(tri, mid, preferred_element_type=F32)
            + jnp.dot(tri, lo, preferred_element_type=F32))


def _hg_chunk(q, fl, lb):
    C = q.shape[0]
    ri = lax.broadcasted_iota(jnp.int32, (C, C), 0)
    ci = lax.broadcasted_iota(jnp.int32, (C, C), 1)
    causal = ri >= ci
    sig = _sigmoid(fl)
    f = lb + (1.0 - lb) * sig
    k = 1.0 - f
    sq = _sigmoid(q)
    qf = q * sq
    b = _tri_matmul(causal.astype(BF16), jnp.log(f))
    bm = b[C // 2 - 1:C // 2, :]
    bl = b[C - 1:C, :]
    e_qt, e_kt, e_in, e_out = jnp.exp(b - bm), jnp.exp(bm - b), jnp.exp(b), jnp.exp(bl - b)
    qt = qf * e_qt
    kt = k * e_kt
    qin = qf * e_in
    kout = k * e_out
    qtb, ktb = qt.astype(BF16), kt.astype(BF16)
    att = [jnp.where(causal, _dot(qtb[:, _head(h)], ktb[:, _head(h)], NT), 0.0).astype(BF16) for h in range(HG_HEADS)]
    return dict(sig=sig, f=f, k=k, sq=sq, qf=qf, b=b, bm=bm, bl=bl, qt=qt, kt=kt, qin=qin, kout=kout, att=att,
                causal=causal, anti=ri <= ci, decay=jnp.exp(bl), e_qt=e_qt, e_kt=e_kt, e_in=e_in, e_out=e_out)


def _head(h):
    return slice(h * HG_D, (h + 1) * HG_D)


def _hgrn_fwd(z, lb, gn, mix, name="hgrn_fwd", tt=512):
    T = z.shape[0]
    C = HG_CHUNK
    nc = tt // C
    Dh = HG_D
    Wd = HG_HEADS * Dh

    def body(q_ref, f_ref, v_ref, g_ref, lb_ref, gn_ref, mix_ref, o_ref, ss_ref, st):
        del mix_ref

        @pl.when(pl.program_id(0) == 0)
        def _():
            st[...] = jnp.zeros_like(st)

        S = [st[h] for h in range(HG_HEADS)]
        for c in range(nc):
            rows = slice(c * C, (c + 1) * C)
            ck = _hg_chunk(q_ref[rows, :].astype(F32), f_ref[rows, :].astype(F32), lb_ref[...])
            v = v_ref[rows, :]
            g = g_ref[rows, :].astype(F32)
            H = range(HG_HEADS)
            qinb, koutb = ck["qin"].astype(BF16), ck["kout"].astype(BF16)
            for h in H:
                ss_ref[h, c] = S[h]
            o = [_dot(ck["att"][h], v[:, _head(h)]) + _dot(qinb[:, _head(h)], S[h], NT) for h in H]
            S = [ck["decay"][:, _head(h)] * S[h] + _dot(v[:, _head(h)], koutb[:, _head(h)], TN) for h in H]
            outs = [o[h] * lax.rsqrt(jnp.mean(o[h] * o[h], axis=-1, keepdims=True) + EPS) * gn_ref[...] for h in H]
            o_ref[rows, :] = (jnp.concatenate(outs, axis=1) * (g * _sigmoid(g))).astype(BF16)
        for h in range(HG_HEADS):
            st[h] = S[h]

    col = lambda base: (lambda i: (i, base))
    return pl.pallas_call(
        body, name=name, grid=(T // tt,),
        in_specs=[pl.BlockSpec((tt, Wd), col(2)), pl.BlockSpec((tt, Wd), col(3)), pl.BlockSpec((tt, Wd), col(4)),
                  pl.BlockSpec((tt, Wd), col(5)), _resident((1, Wd)), _resident((1, Dh)), ANY],
        out_specs=[pl.BlockSpec((tt, Wd), lambda i: (i, 1)),
                   pl.BlockSpec((HG_HEADS, nc, Dh, Dh), lambda i: (0, i, 0, 0))],
        out_shape=[jax.ShapeDtypeStruct((T, 2 * Wd), BF16),
                   jax.ShapeDtypeStruct((HG_HEADS, T // C, Dh, Dh), F32)],
        scratch_shapes=[pltpu.VMEM((HG_HEADS, Dh, Dh), F32)],
        compiler_params=_cp(1),
        input_output_aliases={6: 0},
    )(z, z, z, z, lb, gn, mix)


def _hgrn_bwd(z, ss, dmix, lb, gn, dz, name="hgrn_bwd", tt=512):
    T = z.shape[0]
    C = HG_CHUNK
    nc = tt // C
    nt = T // tt
    Dh = HG_D
    Wd = HG_HEADS * Dh

    def body(q_ref, f_ref, v_ref, g_ref, ss_ref, dm_ref, lb_ref, gn_ref, dz01_ref, dz_ref, dlb_ref, dgn_ref, dst):
        @pl.when(pl.program_id(0) == 0)
        def _():
            dst[...] = jnp.zeros_like(dst)
            dlb_ref[...] = jnp.zeros_like(dlb_ref)
            dgn_ref[...] = jnp.zeros_like(dgn_ref)

        dS = [dst[h] for h in range(HG_HEADS)]
        lbv = lb_ref[...]
        gnv = gn_ref[...]
        rowc = lax.broadcasted_iota(jnp.int32, (C, Wd), 0)
        cat = lambda xs: jnp.concatenate(xs, axis=1)
        for c in reversed(range(nc)):
            rows = slice(c * C, (c + 1) * C)
            q = q_ref[rows, :].astype(F32)
            ck = _hg_chunk(q, f_ref[rows, :].astype(F32), lbv)
            v = v_ref[rows, :]
            g = g_ref[rows, :].astype(F32)
            dout = dm_ref[rows, :].astype(F32)
            sg, dsg = _silu_and_grad(g)
            d_ong = dout * sg
            H = range(HG_HEADS)
            qinb, koutb, qtb, ktb = (ck[n].astype(BF16) for n in ("qin", "kout", "qt", "kt"))
            S = [ss_ref[h, c] for h in H]
            Sb = [s.astype(BF16) for s in S]
            dSb = [d.astype(BF16) for d in dS]
            o = [_dot(ck["att"][h], v[:, _head(h)]) + _dot(qinb[:, _head(h)], Sb[h], NT) for h in H]
            rn = [lax.rsqrt(jnp.mean(o[h] * o[h], axis=-1, keepdims=True) + EPS) for h in H]
            on = [o[h] * rn[h] for h in H]
            don = [d_ong[:, _head(h)] * gnv for h in H]
            do = [(rn[h] * (don[h] - on[h] * jnp.mean(don[h] * on[h], axis=-1, keepdims=True))).astype(BF16) for h in H]
            datt = [jnp.where(ck["causal"], _dot(do[h], v[:, _head(h)], NT), 0.0).astype(BF16) for h in H]
            dvs = [_dot(ck["att"][h], do[h], TN) + _dot(koutb[:, _head(h)], dSb[h], NT) for h in H]
            dqins = [_dot(do[h], Sb[h]) for h in H]
            dkouts = [_dot(v[:, _head(h)], dSb[h]) for h in H]
            dqts = [_dot(datt[h], ktb[:, _head(h)]) for h in H]
            dkts = [_dot(datt[h], qtb[:, _head(h)], TN) for h in H]
            ddecays = [jnp.sum(dS[h] * S[h], axis=0, keepdims=True) for h in H]
            dS = [_dot(do[h], qinb[:, _head(h)], TN) + ck["decay"][:, _head(h)] * dS[h] for h in H]
            ons = [on[h] * gnv for h in H]
            dgn = _rsum8(d_ong[:, _head(0)] * on[0])
            for h in range(1, HG_HEADS):
                dgn = dgn + _rsum8(d_ong[:, _head(h)] * on[h])
            dgn_ref[...] += dgn
            dqt, dkt, dqin, dkout, ddecay = cat(dqts), cat(dkts), cat(dqins), cat(dkouts), cat(ddecays)
            dqf = dqt * ck["e_qt"] + dqin * ck["e_in"]
            dk = dkt * ck["e_kt"] + dkout * ck["e_out"]
            kk = dkout * ck["kout"]
            db = dqt * ck["qt"] - dkt * ck["kt"] + dqin * ck["qin"] - kk
            dbl = jnp.sum(kk, axis=0, keepdims=True) + ddecay * ck["decay"]
            db = db + jnp.where(rowc == C - 1, dbl, 0.0)
            dlogf = _tri_matmul(ck["anti"].astype(BF16), db)
            dfv = dlogf / ck["f"] - dk
            sig, sq = ck["sig"], ck["sq"]
            dlb_ref[...] += _rsum8(dfv * (1.0 - sig))
            dz_ref[rows, 2 * Wd:3 * Wd] = (dqf * (sq * (1.0 + q * (1.0 - sq)))).astype(BF16)
            dz_ref[rows, 3 * Wd:4 * Wd] = (dfv * (1.0 - lbv) * sig * (1.0 - sig)).astype(BF16)
            dz_ref[rows, 4 * Wd:5 * Wd] = cat(dvs).astype(BF16)
            dz_ref[rows, 5 * Wd:6 * Wd] = (dout * cat(ons) * dsg).astype(BF16)
        dz_ref[:, 0:2 * Wd] = dz01_ref[...]
        for h in range(HG_HEADS):
            dst[h] = dS[h]

    rev = lambda i: nt - 1 - i
    col = lambda base: (lambda i: (rev(i), base))
    return pl.pallas_call(
        body, name=name, grid=(nt,),
        in_specs=[pl.BlockSpec((tt, Wd), col(2)), pl.BlockSpec((tt, Wd), col(3)), pl.BlockSpec((tt, Wd), col(4)),
                  pl.BlockSpec((tt, Wd), col(5)),
                  pl.BlockSpec((HG_HEADS, nc, Dh, Dh), lambda i: (0, rev(i), 0, 0)),
                  pl.BlockSpec((tt, Wd), col(1)), _resident((1, Wd)), _resident((1, Dh)),
                  pl.BlockSpec((tt, 2 * Wd), col(0))],
        out_specs=[pl.BlockSpec((tt, 6 * Wd), lambda i: (rev(i), 0)), pl.BlockSpec((SUBLANES, Wd), lambda i: (0, 0)),
                   pl.BlockSpec((SUBLANES, Dh), lambda i: (0, 0))],
        out_shape=[jax.ShapeDtypeStruct((T, 6 * Wd), BF16), jax.ShapeDtypeStruct((SUBLANES, Wd), F32),
                   jax.ShapeDtypeStruct((SUBLANES, Dh), F32)],
        input_output_aliases={8: 0},
        scratch_shapes=[pltpu.VMEM((HG_HEADS, Dh, Dh), F32)],
        compiler_params=_cp(1),
    )(z, z, z, z, ss, dmix, lb, gn, dz)


def _sgu_core(p, lg_ref, lb_ref, wsc_ref, bsb_ref):
    Wd = D_MODEL
    G = SGU_CHUNK
    zz = _gelu(p)
    u = zz[:, :Wd]
    v = zz[:, Wd:]
    vc = v - jnp.mean(v, axis=-1, keepdims=True)
    rstd = lax.rsqrt(jnp.mean(vc * vc, axis=-1, keepdims=True) + EPS)
    vhat = vc * rstd
    vn = vhat * lg_ref[...] + lb_ref[...]
    svs = []
    for gi in range(SGU_G):
        svs.append(jnp.dot(wsc_ref[gi], vn[:, gi * G:(gi + 1) * G].astype(BF16), preferred_element_type=F32) + bsb_ref[gi])
    return u, vhat, rstd, vn, jnp.concatenate(svs, axis=1)


def _sgu_fwd(p1, lg, lbias, wsc, bsb, name="sgu_fwd", tt=512):
    T = p1.shape[0]
    Wd = D_MODEL
    C = SGU_CHUNK

    def body(p_ref, lg_ref, lb_ref, wsc_ref, bsb_ref, s_ref):
        for c in range(tt // C):
            rows = slice(c * C, (c + 1) * C)
            u, _, _, _, sv = _sgu_core(p_ref[rows, :].astype(F32), lg_ref, lb_ref, wsc_ref, bsb_ref)
            s_ref[rows, :] = (u * sv).astype(BF16)

    return pl.pallas_call(
        body, name=name, grid=(T // tt,),
        in_specs=[pl.BlockSpec((tt, 2 * Wd), lambda i: (i, 0)), _resident((1, Wd)), _resident((1, Wd)),
                  _resident((SGU_G, C, C)), _resident((SGU_G, C, C))],
        out_specs=pl.BlockSpec((tt, Wd), lambda i: (i, 0)),
        out_shape=jax.ShapeDtypeStruct((T, Wd), BF16),
        compiler_params=_cp(1),
    )(p1, lg, lbias, wsc, bsb)


def _sgu_bwd(p1, ds, lg, lbias, wsc, wsct, bsb, name="sgu_bwd", tt=512):
    T = p1.shape[0]
    Wd = D_MODEL
    C = SGU_CHUNK

    def body(p_ref, ds_ref, lg_ref, lb_ref, wsc_ref, wsct_ref, bsb_ref, dp_ref, dws_ref, dbs_ref, dlg_ref, dlb_ref, dbin_ref):
        @pl.when(pl.program_id(0) == 0)
        def _():
            dws_ref[...] = jnp.zeros_like(dws_ref)
            dbs_ref[...] = jnp.zeros_like(dbs_ref)
            dlg_ref[...] = jnp.zeros_like(dlg_ref)
            dlb_ref[...] = jnp.zeros_like(dlb_ref)
            dbin_ref[...] = jnp.zeros_like(dbin_ref)

        for c in range(tt // C):
            rows = slice(c * C, (c + 1) * C)
            p = p_ref[rows, :].astype(F32)
            u, vhat, rstd, vn, sv = _sgu_core(p, lg_ref, lb_ref, wsc_ref, bsb_ref)
            dsc = ds_ref[rows, :].astype(F32)
            du = dsc * sv
            dsv = dsc * u
            dvns = []
            for gi in range(SGU_G):
                cs = slice(gi * C, (gi + 1) * C)
                dsv_g = dsv[:, cs]
                dvns.append(jnp.dot(wsct_ref[gi], dsv_g.astype(BF16), preferred_element_type=F32))
                dws_ref[gi] += _dot(dsv_g, vn[:, cs], NT)
                dbs_ref[gi] += dsv_g
            dvn = jnp.concatenate(dvns, axis=1)
            dlg_ref[...] += _rsum8(dvn * vhat)
            dlb_ref[...] += _rsum8(dvn)
            dvh = dvn * lg_ref[...]
            dv = rstd * (dvh - jnp.mean(dvh, axis=-1, keepdims=True) - vhat * jnp.mean(dvh * vhat, axis=-1, keepdims=True))
            dp = jnp.concatenate([du, dv], axis=1) * _gelu_grad(p)
            dbin_ref[...] += _rsum8(dp)
            dp_ref[rows, :] = dp.astype(BF16)

    full3 = pl.BlockSpec((SGU_G, C, C), lambda i: (0, 0, 0))
    return pl.pallas_call(
        body, name=name, grid=(T // tt,),
        in_specs=[pl.BlockSpec((tt, 2 * Wd), lambda i: (i, 0)), pl.BlockSpec((tt, Wd), lambda i: (i, 0)),
                  _resident((1, Wd)), _resident((1, Wd)), _resident((SGU_G, C, C)), _resident((SGU_G, C, C)),
                  _resident((SGU_G, C, C))],
        out_specs=[pl.BlockSpec((tt, 2 * Wd), lambda i: (i, 0)), full3, full3,
                   pl.BlockSpec((SUBLANES, Wd), lambda i: (0, 0)), pl.BlockSpec((SUBLANES, Wd), lambda i: (0, 0)),
                   pl.BlockSpec((SUBLANES, 2 * Wd), lambda i: (0, 0))],
        out_shape=[jax.ShapeDtypeStruct((T, 2 * Wd), BF16), jax.ShapeDtypeStruct((SGU_G, C, C), F32),
                   jax.ShapeDtypeStruct((SGU_G, C, C), F32), jax.ShapeDtypeStruct((SUBLANES, Wd), F32),
                   jax.ShapeDtypeStruct((SUBLANES, Wd), F32), jax.ShapeDtypeStruct((SUBLANES, 2 * Wd), F32)],
        compiler_params=_cp(1),
    )(p1, ds, lg, lbias, wsc, wsct, bsb)


def _pad_rows(w, rows=SUBLANES):
    return jnp.pad(w, ((0, rows - w.shape[0]), (0, 0)))


def _block_diag(w):
    n, b, _ = w.shape
    return (w[:, :, None, :] * jnp.eye(n, dtype=w.dtype)[:, None, :, None]).reshape(n * b, n * b)


def _diag_blocks(m, n):
    b = m.shape[0] // n
    m4 = m.reshape(n, b, n, b)
    return jnp.stack([m4[k, :, k, :] for k in range(n)], axis=0)


def _piece_major(dw):
    if dw.ndim == 2:
        K, N = dw.shape
        return dw.reshape(N_CHIPS, 2, K // (2 * N_CHIPS), N)
    _, K, ns = dw.shape
    return dw.reshape(N_CHIPS, 2, K // 2, ns)


def _ffn_fwd(h, g, w_up, cw, cb, w_down, tag, loss=None):
    hn, gu = _norm_mm(h, g, w_up, jnp.zeros((1, 2 * D_FF), F32), name=f"ffn_up_{tag}")
    a, gc, *out = _ffn_act(gu, cw, cb, w_down, h, name=f"ffn_act_down_{tag}", loss=loss)
    return (out[0] if loss is None else out), (hn, gu, gc, a)


def _ffn_bwd(dh, h, g, saved, w_up, cw, w_down, tag):
    hn, gu, gc, a = saved
    dwd = _mm_tn(a, dh, name=f"ffn_dwd_{tag}")
    dgu, dc, dhin, dg8 = _ffn_act_bwd(gu, gc, cw, w_up, w_down, h, g, dh, name=f"ffn_bwd_{tag}")
    dwu = _mm_tn(hn, dgu, name=f"ffn_dwu_{tag}", col_shards=N_CHIPS)
    dcs = dc.sum(axis=1)
    return dhin, dg8.sum(axis=0), dwu, dcs[0:3], dcs[3], dwd


REDUCE_GROUPS = {"g1": [("ffn_w_up", 1), ("ffn_w_down", 1), ("od_w_out", 0), ("od_w_in", 0)],
                 "g2": [("ffn_w_up", 0), ("ffn_w_down", 0)],
                 "g3": [("ev_w_out", 0), ("ev_w_in", 0)]}


def _local_step(x, tgt, p, start_reduce=None, continue_reduce=None):
    row = lambda v: v.reshape(1, -1)
    grads = {}

    lower = jax.nn.softmax(p["hg_lb_logits"], axis=0)
    lb0 = row(lower[0])
    ev_cw = _pad_rows(p["ev_conv_w"][0])
    ev_cb = row(p["ev_conv_b"][0])
    wa = _block_diag(p["ev_gate_a_w"][0]).astype(BF16)
    wx = _block_diag(p["ev_gate_x_w"][0]).astype(BF16)
    ba, bx, lam = row(p["ev_gate_a_b"][0]), row(p["ev_gate_x_b"][0]), row(p["ev_lru_lambda"][0])
    gn = row(p["ev_hg_norm"][0])
    tril = jnp.tril(jnp.ones((SGU_CHUNK, SGU_CHUNK), F32))
    wsc = (p["od_w_s"][0] * tril).astype(BF16)
    bsb = jnp.broadcast_to(p["od_b_s"][0][:, :, None], (SGU_G, SGU_CHUNK, SGU_CHUNK)).astype(F32)
    ffn_cw = [_pad_rows(p["ffn_conv_w"][l]) for l in range(2)]
    ffn_cb = [row(p["ffn_conv_b"][l]) for l in range(2)]
    ev_w_in, ev_w_out = p["ev_w_in"][0], p["ev_w_out"][0]
    nm = [row(p["norm_mix"][l]) for l in range(2)]
    nf = [row(p["norm_ffn"][l]) for l in range(2)]

    h0 = x
    hn0, z0 = _norm_mm(h0, nm[0], ev_w_in, jnp.zeros((1, ev_w_in.shape[1]), F32), name="ev_in")
    out_a, hseq, a_all, lru_saved = _lru_fwd(z0, ev_cw, ev_cb, wa, ba, wx, bx, lam)
    mix0, ss = _hgrn_fwd(z0, lb0, gn, out_a)
    h1 = _mm(mix0, ev_w_out, h0, F32, name="ev_out")
    late = p["late"](h1) if "late" in p else p
    od_w_in, od_w_out = late["od_w_in"][0], late["od_w_out"][0]
    w_up = [(late["ffn_w_up"], l) for l in range(2)]
    w_down = [(late["ffn_w_down"], l) for l in range(2)]
    h2, ffn0 = _ffn_fwd(h1, nf[0], w_up[0], ffn_cw[0], ffn_cb[0], w_down[0], "l0")
    hn1, p1 = _norm_mm(h2, nm[1], od_w_in, row(p["od_b_in"][0]), name="od_in")
    s1 = _sgu_fwd(p1, row(p["od_ln_g"][0]), row(p["od_ln_b"][0]), wsc, bsb)
    h3 = _mm(s1, od_w_out, h2, F32, name="od_out")
    (dh4, sq8, dgf8), ffn1 = _ffn_fwd(h3, nf[1], w_up[1], ffn_cw[1], ffn_cb[1], w_down[1], "l1", loss=(row(p["norm_final"]), tgt))
    grads["norm_final"] = dgf8.sum(axis=0)

    big = {}
    dh3, dnf1, dwu1, dcw1, dcb1, dwd1 = _ffn_bwd(dh4, h3, nf[1], ffn1, w_up[1], ffn_cw[1], w_down[1], "l1")
    big["ffn_w_up", 1], big["ffn_w_down", 1] = _piece_major(dwu1), _piece_major(dwd1)
    ds1 = _mm(dh3, od_w_out, None, BF16, name="od_ds", transpose_w=True)
    big["od_w_out", 0] = _piece_major(_mm_tn(s1, dh3, name="od_dwo"))
    wsct = jnp.swapaxes(wsc, 1, 2)
    dp1, dws, dbs, dlg8, dlb8, dbin8 = _sgu_bwd(p1, ds1, row(p["od_ln_g"][0]), row(p["od_ln_b"][0]), wsc, wsct, bsb)
    grads["od_w_s"] = (dws * tril)[None]
    grads["od_b_s"] = dbs.sum(axis=-1)[None]
    grads["od_ln_g"] = dlg8.sum(axis=0)[None]
    grads["od_ln_b"] = dlb8.sum(axis=0)[None]
    grads["od_b_in"] = dbin8.sum(axis=0)[None]
    dh2, dnm1 = _mm_normbwd(dp1, od_w_in, h2, nm[1], dh3, name="od_dh")
    big["od_w_in", 0] = _piece_major(_mm_tn(hn1, dp1, name="od_dwi", col_shards=N_CHIPS))
    if start_reduce is not None:
        token = start_reduce("g1", [big[key] for key in REDUCE_GROUPS["g1"]], split=True)
        ffn_cw[0] = ffn_cw[0] + token[0:1, 0:1]

    dh1, dnf0, dwu0, dcw0, dcb0, dwd0 = _ffn_bwd(dh2, h1, nf[0], ffn0, w_up[0], ffn_cw[0], w_down[0], "l0")
    big["ffn_w_up", 0], big["ffn_w_down", 0] = _piece_major(dwu0), _piece_major(dwd0)
    if start_reduce is not None:
        token = continue_reduce("g1", dh1) + start_reduce("g2", [big[key] for key in REDUCE_GROUPS["g2"]], split=True)
        lam = lam + token[0:1, 0:1]
    dmix = _mm(dh1, ev_w_out, None, BF16, name="ev_dmix", transpose_w=True)
    big["ev_w_out", 0] = _piece_major(_mm_tn(mix0, dh1, name="ev_dwo"))
    dz01, dc5, dwa, dwx, dvec = _lru_bwd(z0, hseq, a_all, lru_saved, dmix, ev_cw, wa.T, wx.T, lam)
    if start_reduce is not None:
        lb0 = lb0 + continue_reduce("g2", dc5)[0:1, 0:1]
    dz0, dlb8, dgn8 = _hgrn_bwd(z0, ss, dmix, lb0, gn, dz01)
    big["ev_w_in", 0] = _piece_major(_mm_tn(hn0, dz0, name="ev_dwi", col_shards=N_CHIPS))
    if start_reduce is not None:
        token = start_reduce("g3", [big[key] for key in REDUCE_GROUPS["g3"]])
        nm[0] = nm[0] + token[0:1, 0:1]
    grad_x, dnm0 = _mm_normbwd(dz0, ev_w_in, h0, nm[0], dh1, name="ev_dh")

    dc5s = dc5.sum(axis=1)
    grads["ev_conv_w"] = dc5s[0:4][None]
    grads["ev_conv_b"] = dc5s[4][None]
    grads["ev_gate_a_w"] = _diag_blocks(dwa, LRU_BLOCKS)[None]
    grads["ev_gate_x_w"] = _diag_blocks(dwx, LRU_BLOCKS)[None]
    dvs = dvec.sum(axis=1)
    grads["ev_gate_a_b"] = dvs[0][None]
    grads["ev_gate_x_b"] = dvs[1][None]
    grads["ev_lru_lambda"] = (dvs[2] * (-jax.nn.sigmoid(-p["ev_lru_lambda"][0])))[None]
    dlb = dlb8.sum(axis=0)
    grads["hg_lb_logits"] = dlb[None, :] * lower[0][None, :] * (jnp.eye(3, dtype=F32)[0][:, None] - lower)
    grads["ev_hg_norm"] = dgn8.sum(axis=0)[None]
    grads["norm_mix"] = jnp.stack([dnm0.sum(axis=0), dnm1.sum(axis=0)])
    grads["norm_ffn"] = jnp.stack([dnf0, dnf1])
    grads["ffn_conv_w"] = jnp.stack([dcw0, dcw1])
    grads["ffn_conv_b"] = jnp.stack([dcb0, dcb1])
    return sq8, grad_x, grads, big


MESH = pl.DeviceIdType.MESH
ANY = pl.BlockSpec(memory_space=pl.ANY)
N_CHIPS = 4
N_DEV = 8

SH_BIG = {"ev_w_in": 2, "ev_w_out": 1, "od_w_in": 2, "od_w_out": 1, "ffn_w_up": 2, "ffn_w_down": 1}
SH_SMALL = {"ev_conv_w": 2, "od_b_in": 1, "od_ln_g": 1, "od_ln_b": 1, "ffn_conv_w": 2}
REP = ["norm_mix", "norm_ffn", "norm_final", "ev_conv_b", "ev_gate_a_w", "ev_gate_a_b", "ev_gate_x_w", "ev_gate_x_b",
       "ev_lru_lambda", "hg_lb_logits", "ev_hg_norm", "od_w_s", "od_b_s", "ffn_conv_b"]
WEIGHTS = ["norm_mix", "norm_ffn", "norm_final", "ev_w_in", "ev_conv_w", "ev_conv_b", "ev_gate_a_w", "ev_gate_a_b", "ev_gate_x_w",
           "ev_gate_x_b", "ev_lru_lambda", "hg_lb_logits", "ev_hg_norm", "ev_w_out", "od_w_in", "od_b_in", "od_ln_g", "od_ln_b",
           "od_w_s", "od_b_s", "od_w_out", "ffn_w_up", "ffn_conv_w", "ffn_conv_b", "ffn_w_down"]


def _rows(n_elems, mult=SUBLANES):
    r = -(-n_elems // LANES)
    return -(-r // mult) * mult


def _pack(arrs, rows, dtype):
    flat = jnp.concatenate([a.reshape(-1).astype(dtype) for a in arrs])
    return jnp.pad(flat, (0, rows * LANES - flat.shape[0])).reshape(rows, LANES)


def _unpack(flat2d, shapes):
    flat = flat2d.reshape(-1)
    out, off = [], 0
    for s in shapes:
        n = 1
        for d in s:
            n *= d
        out.append(flat[off:off + n].reshape(s))
        off += n
    return out


def _mesh_pos():
    return lax.axis_index("x"), lax.axis_index("y"), lax.axis_index("c")


def _other_chips(x, y):
    return [(1 - x, y), (x, 1 - y), (1 - x, 1 - y)]


def _half_rows(n):
    return lambda r, c: r.at[0, pl.ds(c * (n // 2), n // 2), :]


GATHER_BIG = {
    "ev_w_in": ((1024, 3072), _half_rows(1024), lambda o, k, c: o.at[pl.ds(c * 512, 512), pl.ds(k * 768, 768)]),
    "ev_w_out": ((1024, 1024), _half_rows(256), lambda o, k, c: o.at[pl.ds(k * 256 + c * 128, 128), :]),
    "od_w_in": ((1024, 2048), _half_rows(1024), lambda o, k, c: o.at[pl.ds(c * 512, 512), pl.ds(k * 512, 512)]),
    "od_w_out": ((1024, 1024), _half_rows(256), lambda o, k, c: o.at[pl.ds(k * 256 + c * 128, 128), :]),
    "ffn_w_up": ((2, 1024, 2 * D_FF), lambda r, c: r.at[c], lambda o, k, c: o.at[c, :, pl.ds(k * (2 * D_FF // 4), 2 * D_FF // 4)]),
    "ffn_w_down": ((2, D_FF, 1024), lambda r, c: r.at[c], lambda o, k, c: o.at[c, pl.ds(k * (D_FF // 4), D_FF // 4), :]),
}


def _gather_weights(names, big, small):
    nb = len(big)
    descs = [GATHER_BIG[n] for n in names]
    rs = small.shape[0]

    def body(*refs):
        ins, s_ref = refs[:nb], refs[nb]
        outs, os_ref = refs[nb + 1:2 * nb + 1], refs[2 * nb + 1]
        ici_send, ici_recv, d2d_send, d2d_recv, loc_sems = refs[2 * nb + 2:2 * nb + 7]
        vbufs = refs[2 * nb + 7:]
        x, y, c = _mesh_pos()
        k = 2 * x + y
        chips = _other_chips(x, y)
        sib = (x, y, 1 - c)

        def remote(src, dst, ssem, rsem, to):
            return pltpu.make_async_remote_copy(src_ref=src, dst_ref=dst, send_sem=ssem, recv_sem=rsem, device_id=to,
                                                device_id_type=MESH)

        stage = [pltpu.make_async_copy(ins[t], vbufs[t], loc_sems.at[2 * t]) for t in range(nb)]
        stage.append(pltpu.make_async_copy(s_ref, vbufs[nb], loc_sems.at[2 * nb]))
        for cp in stage:
            cp.start()
        sends = []
        for t, (_, src, dst) in enumerate(descs):
            for j, (px, py) in enumerate(chips):
                sends.append(remote(src(ins[t], c), dst(outs[t], k, c), ici_send.at[3 * t + j], ici_recv.at[3 * t + j], (px, py, c)))
        for j, (px, py) in enumerate(chips):
            sends.append(remote(s_ref, os_ref.at[k], ici_send.at[3 * nb + j], ici_recv.at[3 * nb + j], (px, py, c)))
        for cp in sends:
            cp.start()
        for cp in stage:
            cp.wait()
        local = []
        for t, (_, src, dst) in enumerate(descs):
            for cc in (0, 1):
                local.append(pltpu.make_async_copy(src(vbufs[t], cc), dst(outs[t], k, cc), loc_sems.at[2 * t + cc]))
        local.append(pltpu.make_async_copy(vbufs[nb], os_ref.at[k], loc_sems.at[2 * nb]))
        for cp in local:
            cp.start()
        for t, (_, src, dst) in enumerate(descs):
            for j, (px, py) in enumerate(chips):
                got = dst(outs[t], 2 * px + py, c)
                remote(got, got, ici_send.at[3 * t + j], ici_recv.at[3 * t + j], (px, py, c)).wait_recv()
                fwd = remote(got, got, d2d_send.at[3 * t + j], d2d_recv.at[3 * t + j], sib)
                fwd.start()
                sends.append(fwd)
        for j, (px, py) in enumerate(chips):
            remote(s_ref, os_ref.at[2 * px + py], ici_send.at[3 * nb + j], ici_recv.at[3 * nb + j], (px, py, c)).wait_recv()
        for t, (_, src, dst) in enumerate(descs):
            for j, (px, py) in enumerate(chips):
                theirs = dst(outs[t], 2 * px + py, 1 - c)
                remote(theirs, theirs, d2d_send.at[3 * t + j], d2d_recv.at[3 * t + j], sib).wait_recv()
        for cp in sends:
            cp.wait_send()
        for cp in local:
            cp.wait()

    out_shape = [jax.ShapeDtypeStruct(d[0], BF16) for d in descs] + [jax.ShapeDtypeStruct((N_CHIPS, rs, LANES), small.dtype)]
    return pl.pallas_call(
        body, name="gather_weights", in_specs=[ANY] * (nb + 1), out_specs=[ANY] * (nb + 1), out_shape=out_shape,
        scratch_shapes=[pltpu.SemaphoreType.DMA((3 * nb + 3,)), pltpu.SemaphoreType.DMA((3 * nb + 3,)),
                        pltpu.SemaphoreType.DMA((3 * nb,)), pltpu.SemaphoreType.DMA((3 * nb,)),
                        pltpu.SemaphoreType.DMA((2 * nb + 1,))]
        + [pltpu.VMEM(b.shape, b.dtype) for b in big] + [pltpu.VMEM(small.shape, small.dtype)],
        compiler_params=pltpu.CompilerParams(vmem_limit_bytes=VMEM_LIMIT),
    )(*big, small)


def _place_own(names, big):
    nb = len(big)
    descs = [GATHER_BIG[n] for n in names]

    def body(*refs):
        ins, outs = refs[:nb], refs[nb:2 * nb]
        sems, vbufs = refs[2 * nb], refs[2 * nb + 1:]
        x, y, c = _mesh_pos()
        k = 2 * x + y
        stage = [pltpu.make_async_copy(ins[t], vbufs[t], sems.at[2 * t]) for t in range(nb)]
        for cp in stage:
            cp.start()
        for cp in stage:
            cp.wait()
        local = [pltpu.make_async_copy(src(vbufs[t], cc), dst(outs[t], k, cc), sems.at[2 * t + cc])
                 for t, (_, src, dst) in enumerate(descs) for cc in (0, 1)]
        for cp in local:
            cp.start()
        for cp in local:
            cp.wait()

    return pl.pallas_call(
        body, name="place_own", in_specs=[ANY] * nb, out_specs=[ANY] * nb,
        out_shape=[jax.ShapeDtypeStruct(d[0], BF16) for d in descs],
        scratch_shapes=[pltpu.SemaphoreType.DMA((2 * nb,))] + [pltpu.VMEM(b.shape, b.dtype) for b in big],
        compiler_params=pltpu.CompilerParams(vmem_limit_bytes=VMEM_LIMIT),
    )(*big)


def _gather_start(names, big, bufs):
    nb = len(big)
    descs = [GATHER_BIG[n] for n in names]

    def body(*refs):
        ins, lnd = refs[:nb], refs[nb:2 * nb]
        send_sems, recv_sems, token = refs[2 * nb], refs[2 * nb + 1], refs[-1]
        x, y, c = _mesh_pos()
        k = 2 * x + y
        for t, (_, src, dst) in enumerate(descs):
            for j, (px, py) in enumerate(_other_chips(x, y)):
                _remote(src(ins[t], c), dst(lnd[t], k, c), send_sems.at[3 * t + j], recv_sems.at[3 * t + j], (px, py, c)).start()
        token[...] = jnp.zeros_like(token)

    out = pl.pallas_call(
        body, name="gather_start",
        out_shape=(pltpu.SemaphoreType.DMA((3 * nb,)), pltpu.SemaphoreType.DMA((3 * nb,)),
                   *[pltpu.HBM(b.shape, b.dtype) for b in big], *[pltpu.HBM(b.shape, b.dtype) for b in bufs],
                   jax.ShapeDtypeStruct((SUBLANES, LANES), F32)),
        in_specs=[HBM] * (2 * nb), out_specs=(SEM, SEM, *[HBM] * (2 * nb), pl.BlockSpec(memory_space=pltpu.VMEM)),
        input_output_aliases={i: 2 + i for i in range(2 * nb)},
        compiler_params=pltpu.CompilerParams(has_side_effects=DATAFLOW),
    )(*[pltpu.with_memory_space_constraint(b, pltpu.HBM) for b in big], *[pltpu.with_memory_space_constraint(b, pltpu.HBM) for b in bufs])
    return out[0], out[1], list(out[2:2 + nb]), list(out[2 + nb:2 + 2 * nb]), out[-1]


def _gather_wait(names, send_sems, recv_sems, big, bufs, after):
    nb = len(big)
    descs = [GATHER_BIG[n] for n in names]

    def body(*refs):
        ins, lnd = refs[:nb], refs[nb:2 * nb]
        ssem, rsem = refs[2 * nb], refs[2 * nb + 1]
        x, y, c = _mesh_pos()
        for t, (_, src, dst) in enumerate(descs):
            for j, (px, py) in enumerate(_other_chips(x, y)):
                cp = _remote(src(ins[t], c), dst(lnd[t], 2 * px + py, c), ssem.at[3 * t + j], rsem.at[3 * t + j], (px, py, c))
                cp.wait_send()
                cp.wait_recv()

    out = pl.pallas_call(
        body, name="gather_wait",
        out_shape=(*[pltpu.HBM(b.shape, b.dtype) for b in big], *[pltpu.HBM(b.shape, b.dtype) for b in bufs]),
        in_specs=[HBM] * (2 * nb) + [SEM, SEM, ANY], out_specs=tuple([HBM] * (2 * nb)),
        input_output_aliases={i: i for i in range(2 * nb)},
        compiler_params=pltpu.CompilerParams(has_side_effects=DATAFLOW),
    )(*big, *bufs, send_sems, recv_sems, after)
    return list(out[nb:])


def _gather_forward(names, bufs):
    nb = len(bufs)
    descs = [GATHER_BIG[n] for n in names]

    def body(*refs):
        outs = refs[nb:2 * nb]
        send_sems, recv_sems = refs[2 * nb:]
        x, y, c = _mesh_pos()
        sib = (x, y, 1 - c)
        sends = []
        for t, (_, src, dst) in enumerate(descs):
            for j, (px, py) in enumerate(_other_chips(x, y)):
                got = dst(outs[t], 2 * px + py, c)
                sends.append(_remote(got, got, send_sems.at[3 * t + j], recv_sems.at[3 * t + j], sib))
        for cp in sends:
            cp.start()
        for t, (_, src, dst) in enumerate(descs):
            for j, (px, py) in enumerate(_other_chips(x, y)):
                theirs = dst(outs[t], 2 * px + py, 1 - c)
                _remote(theirs, theirs, send_sems.at[3 * t + j], recv_sems.at[3 * t + j], sib).wait_recv()
        for cp in sends:
            cp.wait_send()

    return pl.pallas_call(
        body, name="gather_forward", in_specs=[ANY] * nb, out_specs=[ANY] * nb,
        out_shape=[jax.ShapeDtypeStruct(b.shape, b.dtype) for b in bufs], input_output_aliases={t: t for t in range(nb)},
        scratch_shapes=[pltpu.SemaphoreType.DMA((3 * nb,)), pltpu.SemaphoreType.DMA((3 * nb,))],
    )(*bufs)


def _remote(src, dst, ssem, rsem, to):
    return pltpu.make_async_remote_copy(src_ref=src, dst_ref=dst, send_sem=ssem, recv_sem=rsem, device_id=to, device_id_type=MESH)


def _rs_send_sibling(gs, tag):
    n = len(gs)
    counts = [N_CHIPS if g.ndim == 4 else 1 for g in gs]
    ns = sum(counts)

    def body(*refs):
        ins, outs = refs[:n], refs[n:2 * n]
        send_sems, recv_sems = refs[2 * n:]
        x, y, c = _mesh_pos()
        cps, s = [], 0
        for t in range(n):
            if counts[t] == 1:
                cps.append(_remote(ins[t].at[1 - c], outs[t], send_sems.at[s], recv_sems.at[s], (x, y, 1 - c)))
                s += 1
            else:
                for k in range(N_CHIPS):
                    cps.append(_remote(ins[t].at[k, 1 - c], outs[t].at[k], send_sems.at[s], recv_sems.at[s], (x, y, 1 - c)))
                    s += 1
        for cp in cps:
            cp.start()
        for cp in cps:
            cp.wait()

    out_shape = [jax.ShapeDtypeStruct(g.shape[:1] + g.shape[2:] if g.ndim == 4 else g.shape[1:], g.dtype) for g in gs]
    return pl.pallas_call(
        body, name=f"rs_send_sibling_{tag}", in_specs=[ANY] * n, out_specs=[ANY] * n, out_shape=out_shape,
        scratch_shapes=[pltpu.SemaphoreType.DMA((ns,)), pltpu.SemaphoreType.DMA((ns,))],
    )(*gs)


def _add_piece(g, recv, c, name):
    P, Q = g.shape[-2:]

    def body(c_ref, g_ref, r_ref, o_ref):
        o_ref[...] = g_ref[...].reshape(o_ref.shape) + r_ref[...]

    if g.ndim == 4:
        grid = (N_CHIPS,)
        in_specs = [pl.BlockSpec((1, 1, P, Q), lambda k, c_ref: (k, c_ref[0], 0, 0)), pl.BlockSpec((1, P, Q), lambda k, c_ref: (k, 0, 0))]
        out_spec = pl.BlockSpec((1, P, Q), lambda k, c_ref: (k, 0, 0))
    else:
        grid = (1,)
        in_specs = [pl.BlockSpec((1, P, Q), lambda k, c_ref: (c_ref[0], 0, 0)), pl.BlockSpec((P, Q), lambda k, c_ref: (0, 0))]
        out_spec = pl.BlockSpec((P, Q), lambda k, c_ref: (0, 0))
    return pl.pallas_call(
        body, name=name,
        grid_spec=pltpu.PrefetchScalarGridSpec(num_scalar_prefetch=1, grid=grid, in_specs=in_specs, out_specs=out_spec),
        out_shape=jax.ShapeDtypeStruct(recv.shape, g.dtype),
        compiler_params=_cp(1),
    )(c, g, recv)


HBM = pl.BlockSpec(memory_space=pltpu.HBM)
SEM = pl.BlockSpec(memory_space=pltpu.SEMAPHORE)
DATAFLOW = pltpu.SideEffectType.DATAFLOW_SIDE_EFFECTING


def _sibling_copies(gs, srcs, dsts, send_sems, recv_sems):
    x, y, c = _mesh_pos()
    cps, s = [], 0
    for t, g in enumerate(gs):
        if g.ndim == 4:
            for k in range(N_CHIPS):
                cps.append(_remote(srcs[t].at[k, 1 - c], dsts[t].at[k], send_sems.at[s], recv_sems.at[s], (x, y, 1 - c)))
                s += 1
        else:
            cps.append(_remote(srcs[t].at[1 - c], dsts[t], send_sems.at[s], recv_sems.at[s], (x, y, 1 - c)))
            s += 1
    return cps


def _sibling_start(gs, tag):
    n = len(gs)
    ns = sum(N_CHIPS if g.ndim == 4 else 1 for g in gs)
    lands = [pltpu.with_memory_space_constraint(lax.empty(g.shape[:1] + g.shape[2:] if g.ndim == 4 else g.shape[1:], g.dtype), pltpu.HBM)
             for g in gs]

    def body(*refs):
        for cp in _sibling_copies(gs, refs[:n], refs[n:2 * n], refs[2 * n], refs[2 * n + 1]):
            cp.start()
        refs[-1][...] = jnp.zeros_like(refs[-1])

    out = pl.pallas_call(
        body, name=f"sibling_start_{tag}",
        out_shape=(pltpu.SemaphoreType.DMA((ns,)), pltpu.SemaphoreType.DMA((ns,)),
                   *[pltpu.HBM(g.shape, g.dtype) for g in gs], *[pltpu.HBM(l.shape, l.dtype) for l in lands],
                   jax.ShapeDtypeStruct((SUBLANES, LANES), F32)),
        in_specs=[HBM] * (2 * n), out_specs=(SEM, SEM, *[HBM] * (2 * n), pl.BlockSpec(memory_space=pltpu.VMEM)),
        input_output_aliases={i: 2 + i for i in range(2 * n)},
        compiler_params=pltpu.CompilerParams(has_side_effects=DATAFLOW),
    )(*[pltpu.with_memory_space_constraint(g, pltpu.HBM) for g in gs], *lands)
    return (out[0], out[1], list(out[2:2 + n]), list(out[2 + n:2 + 2 * n])), out[-1]


def _sibling_wait(send_sems, recv_sems, gs, lands, after, tag):
    n = len(gs)

    def body(*refs):
        for cp in _sibling_copies(gs, refs[:n], refs[n:2 * n], refs[2 * n], refs[2 * n + 1]):
            cp.wait_send()
            cp.wait_recv()

    out = pl.pallas_call(
        body, name=f"sibling_wait_{tag}",
        out_shape=(*[pltpu.HBM(g.shape, g.dtype) for g in gs], *[pltpu.HBM(l.shape, l.dtype) for l in lands]),
        in_specs=[HBM] * (2 * n) + [SEM, SEM, ANY], out_specs=tuple([HBM] * (2 * n)),
        input_output_aliases={i: i for i in range(2 * n)},
        compiler_params=pltpu.CompilerParams(has_side_effects=DATAFLOW),
    )(*gs, *lands, send_sems, recv_sems, after)
    return list(out[:n]), list(out[n:])


def _chips_start(hs, tag):
    n = len(hs)
    lands = [pltpu.with_memory_space_constraint(lax.empty((N_CHIPS,) + h.shape[-2:], h.dtype), pltpu.HBM) for h in hs]

    def body(*refs):
        ins, lnd = refs[:n], refs[n:2 * n]
        send_sems, recv_sems, token = refs[2 * n], refs[2 * n + 1], refs[-1]
        x, y, c = _mesh_pos()
        k = 2 * x + y
        piece = lambda t, kk: ins[t].at[kk] if hs[t].ndim == 3 else ins[t]
        for t in range(n):
            for j, (px, py) in enumerate(_other_chips(x, y)):
                _remote(piece(t, 2 * px + py), lnd[t].at[k], send_sems.at[3 * t + j], recv_sems.at[3 * t + j], (px, py, c)).start()
        token[...] = jnp.zeros_like(token)

    out = pl.pallas_call(
        body, name=f"chips_start_{tag}",
        out_shape=(pltpu.SemaphoreType.DMA((3 * n,)), pltpu.SemaphoreType.DMA((3 * n,)),
                   *[pltpu.HBM(h.shape, h.dtype) for h in hs], *[pltpu.HBM(l.shape, l.dtype) for l in lands],
                   jax.ShapeDtypeStruct((SUBLANES, LANES), F32)),
        in_specs=[HBM] * (2 * n), out_specs=(SEM, SEM, *[HBM] * (2 * n), pl.BlockSpec(memory_space=pltpu.VMEM)),
        input_output_aliases={i: 2 + i for i in range(2 * n)},
        compiler_params=pltpu.CompilerParams(has_side_effects=DATAFLOW),
    )(*[pltpu.with_memory_space_constraint(h, pltpu.HBM) for h in hs], *lands)
    return out[0], out[1], list(out[2:2 + n]), list(out[2 + n:2 + 2 * n]), out[-1]


def _chips_wait(send_sems, recv_sems, hs, lands, after, tag):
    n = len(hs)

    def body(*refs):
        ins, lnd = refs[:n], refs[n:2 * n]
        ssem, rsem = refs[2 * n], refs[2 * n + 1]
        x, y, c = _mesh_pos()
        k = 2 * x + y
        piece = lambda t, kk: ins[t].at[kk] if hs[t].ndim == 3 else ins[t]
        for t in range(n):
            for j, (px, py) in enumerate(_other_chips(x, y)):
                cp = _remote(piece(t, k), lnd[t].at[2 * px + py], ssem.at[3 * t + j], rsem.at[3 * t + j], (px, py, c))
                cp.wait_send()
                cp.wait_recv()

    out = pl.pallas_call(
        body, name=f"chips_wait_{tag}",
        out_shape=(*[pltpu.HBM(h.shape, h.dtype) for h in hs], *[pltpu.HBM(l.shape, l.dtype) for l in lands]),
        in_specs=[HBM] * (2 * n) + [SEM, SEM, ANY], out_specs=tuple([HBM] * (2 * n)),
        input_output_aliases={i: i for i in range(2 * n)},
        compiler_params=pltpu.CompilerParams(has_side_effects=DATAFLOW),
    )(*hs, *lands, send_sems, recv_sems, after)
    return list(out[:n]), list(out[n:])


def _add_chips(p, own, kc, name):
    _, P, Q = p.shape
    tr = P
    while N_CHIPS * tr * Q * 4 > 6 * 1024 * 1024 and tr % 16 == 0:
        tr //= 2
    sharded = own.ndim == 3

    def body(kc_ref, p_ref, own_ref, o_ref):
        k = kc_ref[0]
        mine = own_ref[...].reshape(tr, Q)
        v = [jnp.where(k == j, mine, p_ref[j]) for j in range(N_CHIPS)]
        o_ref[0] = ((v[0] + v[1]) + v[2]) + v[3]

    own_spec = (pl.BlockSpec((1, tr, Q), lambda i, kc_ref: (kc_ref[0], i, 0)) if sharded
                else pl.BlockSpec((tr, Q), lambda i, kc_ref: (i, 0)))
    return pl.pallas_call(
        body, name=name,
        grid_spec=pltpu.PrefetchScalarGridSpec(
            num_scalar_prefetch=1, grid=(P // tr,),
            in_specs=[pl.BlockSpec((N_CHIPS, tr, Q), lambda i, kc_ref: (0, i, 0)), own_spec],
            out_specs=pl.BlockSpec((1, tr, Q), lambda i, kc_ref: (kc_ref[1], i, 0))),
        out_shape=jax.ShapeDtypeStruct((2, P, Q), p.dtype),
        compiler_params=_cp(1),
    )(kc, p, own)


def _rs_share(fs, tag):
    n = len(fs)

    def body(*refs):
        outs = refs[n:2 * n]
        send_sems, recv_sems = refs[2 * n:]
        x, y, c = _mesh_pos()
        sends = [_remote(outs[t].at[c], outs[t].at[c], send_sems.at[t], recv_sems.at[t], (x, y, 1 - c)) for t in range(n)]
        for cp in sends:
            cp.start()
        for t in range(n):
            _remote(outs[t].at[c], outs[t].at[1 - c], send_sems.at[t], recv_sems.at[t], (x, y, 1 - c)).wait_recv()
        for cp in sends:
            cp.wait_send()

    return pl.pallas_call(
        body, name=f"rs_share_{tag}", in_specs=[ANY] * n, out_specs=[ANY] * n,
        out_shape=[jax.ShapeDtypeStruct(f.shape, f.dtype) for f in fs], input_output_aliases={t: t for t in range(n)},
        scratch_shapes=[pltpu.SemaphoreType.DMA((n,)), pltpu.SemaphoreType.DMA((n,))],
    )(*fs)


def _reduce_start(gs, kc, tag):
    return _reduce_continue(gs, _rs_send_sibling(gs, tag), kc, tag)


def _reduce_continue(gs, from_sibling, kc, tag):
    chip_sums = [_add_piece(g, r, kc[1:], name=f"add_piece_{tag}_{t}") for t, (g, r) in enumerate(zip(gs, from_sibling))]
    send_sems, recv_sems, chip_sums, lands, token = _chips_start(chip_sums, tag)
    return (send_sems, recv_sems, chip_sums, lands, tag), token


def _reduce_finish(states, kc, after):
    mine = []
    for send_sems, recv_sems, chip_sums, lands, tag in states:
        chip_sums, from_chips = _chips_wait(send_sems, recv_sems, chip_sums, lands, after, tag)
        mine += [_add_chips(p, h, kc, name=f"add_chips_{tag}_{t}") for t, (p, h) in enumerate(zip(from_chips, chip_sums))]
    return _rs_share(mine, "all")


def _adamw(w, g, m, v, name):
    R, C = w.shape
    tr = R
    for cand in (512, 256, 128, 64, 32, 16, 8):
        if R % cand == 0 and cand * C * 4 <= 2 * 1024 * 1024:
            tr = cand
            break
    c1 = 1.0 / (1.0 - ADAM_B1 ** ADAM_STEP)
    c2 = 1.0 / (1.0 - ADAM_B2 ** ADAM_STEP)

    def body(w_ref, g_ref, m_ref, v_ref, d_ref, mo_ref, vo_ref):
        gv = g_ref[...]
        mn = ADAM_B1 * m_ref[...] + (1.0 - ADAM_B1) * gv
        vn = ADAM_B2 * v_ref[...] + (1.0 - ADAM_B2) * (gv * gv)
        mo_ref[...] = mn
        vo_ref[...] = vn
        d_ref[...] = -ADAM_LR * ((mn * c1) / (jnp.sqrt(vn * c2) + ADAM_EPS) + ADAM_WD * w_ref[...])

    spec = pl.BlockSpec((tr, C), lambda i: (i, 0))
    shp = jax.ShapeDtypeStruct((R, C), F32)
    return pl.pallas_call(body, name=name, grid=(R // tr,), in_specs=[spec] * 4, out_specs=[spec] * 3, out_shape=[shp] * 3,
                          compiler_params=_cp(1))(w, g, m, v)


def _adamw_many(ws, gs, ms, vs):
    n = len(ws)
    c1 = 1.0 / (1.0 - ADAM_B1 ** ADAM_STEP)
    c2 = 1.0 / (1.0 - ADAM_B2 ** ADAM_STEP)

    def body(*refs):
        w_refs, g_refs, m_refs, v_refs = (refs[k * n:(k + 1) * n] for k in range(4))
        outs = refs[4 * n:]
        for i in range(n):
            gv = g_refs[i][...]
            mn = ADAM_B1 * m_refs[i][...] + (1.0 - ADAM_B1) * gv
            vn = ADAM_B2 * v_refs[i][...] + (1.0 - ADAM_B2) * (gv * gv)
            outs[3 * i][...] = -ADAM_LR * ((mn * c1) / (jnp.sqrt(vn * c2) + ADAM_EPS) + ADAM_WD * w_refs[i][...])
            outs[3 * i + 1][...] = mn
            outs[3 * i + 2][...] = vn

    out_shape = [jax.ShapeDtypeStruct(w.shape, F32) for w in ws for _ in range(3)]
    return pl.pallas_call(body, name="adamw_small", out_shape=out_shape)(*ws, *gs, *ms, *vs)


def _step(a):
    x, y, c = _mesh_pos()
    kc = jnp.stack([2 * x + y, c]).astype(jnp.int32)

    rs = _rows(sum(a[n].size for n in SH_SMALL))
    first, later = ["ev_w_in", "ev_w_out"], ["od_w_in", "od_w_out", "ffn_w_up", "ffn_w_down"]
    lead = lambda w: w if w.ndim == 3 else w[None]
    *full, gs = _gather_weights(first, [a[n].astype(BF16) for n in first], _pack([a[n] for n in SH_SMALL], rs, F32))
    p = {n: a[n] for n in REP}
    p.update({n: lead(w) for n, w in zip(first, full)})
    parts = [_unpack(gs[k], [a[n].shape for n in SH_SMALL]) for k in range(N_CHIPS)]
    for i, n in enumerate(SH_SMALL):
        p[n] = jnp.concatenate([parts[k][i] for k in range(N_CHIPS)], axis=SH_SMALL[n])
    shards = lax.optimization_barrier(([a[n].astype(BF16) for n in later], full))[0]
    g_send, g_recv, shards, bufs, token = _gather_start(later, shards, _place_own(later, shards))
    p["norm_mix"] = p["norm_mix"] + token[0:1, 0:1]

    def late(after):
        got = _gather_forward(later, _gather_wait(later, g_send, g_recv, shards, bufs, after))
        return {n: lead(w) for n, w in zip(later, got)}

    p["late"] = late

    states, pending = {}, {}

    def start_reduce(tag, gs, split=False):
        if split:
            pending[tag], token = _sibling_start(gs, tag)
        else:
            states[tag], token = _reduce_start(gs, kc, tag)
        return token

    def continue_reduce(tag, after):
        send_sems, recv_sems, gs, lands = pending.pop(tag)
        gs, from_sibling = _sibling_wait(send_sems, recv_sems, gs, lands, after, tag)
        states[tag], token = _reduce_continue(gs, from_sibling, kc, tag)
        return token

    sq8, grad_x, grads, big = _local_step(a["x"][0], a["loss_target"][0], p, start_reduce, continue_reduce)
    loss = lax.psum(0.5 / D_MODEL * jnp.sum(sq8), ("x", "y", "c"))

    r_s = _rows(sum(a[n].size for n in SH_SMALL), 2 * SUBLANES) // 2
    small_pieces = []
    for k in range(N_CHIPS):
        pieces = [lax.slice_in_dim(grads[n], k * a[n].shape[ax], (k + 1) * a[n].shape[ax], axis=ax) for n, ax in SH_SMALL.items()]
        small_pieces.append(_pack(pieces, 2 * r_s, F32).reshape(2, r_s, LANES))
    g_small = jnp.stack(small_pieces)
    r_r = _rows(sum(a[n].size for n in REP), 2 * SUBLANES) // 2
    g_rep = _pack([grads[n] for n in REP], 2 * r_r, F32).reshape(2, r_r, LANES)
    token = start_reduce("g4", [g_small, g_rep])
    reduced = _reduce_finish([states[tag] for tag in ("g1", "g2", "g3", "g4")], kc, token)
    red = dict(zip([key for tag in ("g1", "g2", "g3") for key in REDUCE_GROUPS[tag]], reduced))
    gfin = {}
    for n in ("ev_w_in", "ev_w_out", "od_w_in", "od_w_out"):
        gfin[n] = red[n, 0].reshape(a[n].shape)
    for n in ("ffn_w_up", "ffn_w_down"):
        gfin[n] = jnp.stack([red[n, l].reshape(a[n].shape[1:]) for l in range(2)])
    gfin.update(zip(SH_SMALL, _unpack(reduced[-2], [a[n].shape for n in SH_SMALL])))
    gfin.update(zip(REP, _unpack(reduced[-1], [a[n].shape for n in REP])))

    out = {"loss": loss, "grad_x": grad_x[None]}
    small_names = list(SH_SMALL) + REP
    for n in SH_BIG:
        shp = a[n].shape
        two_d = lambda t: t.reshape(-1, shp[-1])
        d, mo, vo = _adamw(two_d(a[n]), two_d(gfin[n]), two_d(a["m_" + n]), two_d(a["v_" + n]), name=f"adamw_{n}")
        out["delta_" + n], out["new_m_" + n], out["new_v_" + n] = d.reshape(shp), mo.reshape(shp), vo.reshape(shp)
    two_d = lambda t: t.reshape(-1, t.shape[-1])
    res = _adamw_many(*[[two_d(src(n)) for n in small_names]
                        for src in (lambda n: a[n], lambda n: gfin[n], lambda n: a["m_" + n], lambda n: a["v_" + n])])
    for i, n in enumerate(small_names):
        out["delta_" + n], out["new_m_" + n], out["new_v_" + n] = (r.reshape(a[n].shape) for r in res[3 * i:3 * i + 3])
    for n in WEIGHTS:
        out["grad_" + n] = gfin[n]
    return out


def kernel(x, norm_mix, norm_ffn, norm_final, ev_w_in, ev_conv_w, ev_conv_b, ev_gate_a_w, ev_gate_a_b, ev_gate_x_w, ev_gate_x_b, ev_lru_lambda, hg_lb_logits, ev_hg_norm, ev_w_out, od_w_in, od_b_in, od_ln_g, od_ln_b, od_w_s, od_b_s, od_w_out, ffn_w_up, ffn_conv_w, ffn_conv_b, ffn_w_down, loss_target, m_norm_mix, m_norm_ffn, m_norm_final, m_ev_w_in, m_ev_conv_w, m_ev_conv_b, m_ev_gate_a_w, m_ev_gate_a_b, m_ev_gate_x_w, m_ev_gate_x_b, m_ev_lru_lambda, m_hg_lb_logits, m_ev_hg_norm, m_ev_w_out, m_od_w_in, m_od_b_in, m_od_ln_g, m_od_ln_b, m_od_w_s, m_od_b_s, m_od_w_out, m_ffn_w_up, m_ffn_conv_w, m_ffn_conv_b, m_ffn_w_down, v_norm_mix, v_norm_ffn, v_norm_final, v_ev_w_in, v_ev_conv_w, v_ev_conv_b, v_ev_gate_a_w, v_ev_gate_a_b, v_ev_gate_x_w, v_ev_gate_x_b, v_ev_lru_lambda, v_hg_lb_logits, v_ev_hg_norm, v_ev_w_out, v_od_w_in, v_od_b_in, v_od_ln_g, v_od_ln_b, v_od_w_s, v_od_b_s, v_od_w_out, v_ffn_w_up, v_ffn_conv_w, v_ffn_conv_b, v_ffn_w_down):
    vals = (x, norm_mix, norm_ffn, norm_final, ev_w_in, ev_conv_w, ev_conv_b, ev_gate_a_w, ev_gate_a_b, ev_gate_x_w, ev_gate_x_b, ev_lru_lambda, hg_lb_logits, ev_hg_norm, ev_w_out, od_w_in, od_b_in, od_ln_g, od_ln_b, od_w_s, od_b_s, od_w_out, ffn_w_up, ffn_conv_w, ffn_conv_b, ffn_w_down, loss_target, m_norm_mix, m_norm_ffn, m_norm_final, m_ev_w_in, m_ev_conv_w, m_ev_conv_b, m_ev_gate_a_w, m_ev_gate_a_b, m_ev_gate_x_w, m_ev_gate_x_b, m_ev_lru_lambda, m_hg_lb_logits, m_ev_hg_norm, m_ev_w_out, m_od_w_in, m_od_b_in, m_od_ln_g, m_od_ln_b, m_od_w_s, m_od_b_s, m_od_w_out, m_ffn_w_up, m_ffn_conv_w, m_ffn_conv_b, m_ffn_w_down, v_norm_mix, v_norm_ffn, v_norm_final, v_ev_w_in, v_ev_conv_w, v_ev_conv_b, v_ev_gate_a_w, v_ev_gate_a_b, v_ev_gate_x_w, v_ev_gate_x_b, v_ev_lru_lambda, v_hg_lb_logits, v_ev_hg_norm, v_ev_w_out, v_od_w_in, v_od_b_in, v_od_ln_g, v_od_ln_b, v_od_w_s, v_od_b_s, v_od_w_out, v_ffn_w_up, v_ffn_conv_w, v_ffn_conv_b, v_ffn_w_down)
    names = ["x"] + WEIGHTS + ["loss_target"] + ["m_" + n for n in WEIGHTS] + ["v_" + n for n in WEIGHTS]
    out = _step(dict(zip(names, vals)))
    return (out["loss"], out["grad_x"], *[out["grad_" + n] for n in WEIGHTS], *[out["delta_" + n] for n in WEIGHTS],
            *[out["new_m_" + n] for n in WEIGHTS], *[out["new_v_" + n] for n in WEIGHTS])
```

```python
import jax
import jax.numpy as jnp
from jax import lax
from jax.experimental import pallas as pl
from jax.experimental.pallas import tpu as pltpu

F32 = jnp.float32
BF16 = jnp.bfloat16

EPS = 1e-6
D_MODEL = 1024
LRU_W = 512
LRU_BLOCKS = 8
LRU_C = 8.0
HG_HEADS = 4
HG_D = 128
HG_CHUNK = 64
SGU_G = 8
SGU_CHUNK = 128
D_FF = 2816
ADAM_LR, ADAM_B1, ADAM_B2, ADAM_EPS, ADAM_WD, ADAM_STEP = 0.001, 0.9, 0.999, 1e-08, 0.01, 10

V7X_VMEM_BYTES = 64 * 1024 * 1024
VMEM_LIMIT = V7X_VMEM_BYTES - 8 * 1024 * 1024
SUBLANES = 8
LANES = 128
BF16_ROWS = 16
ELEMENTWISE_TILE_BYTES = 2 * 1024 * 1024

N_CHIPS = 4
MESH = pl.DeviceIdType.MESH
ANY = pl.BlockSpec(memory_space=pl.ANY)
HBM = pl.BlockSpec(memory_space=pltpu.HBM)
SEM = pl.BlockSpec(memory_space=pltpu.SEMAPHORE)
DATAFLOW = pltpu.SideEffectType.DATAFLOW_SIDE_EFFECTING

GELU_C0 = 0.7978845608028654
GELU_C1 = 0.044715

NN = (((1,), (0,)), ((), ()))
NT = (((1,), (1,)), ((), ()))
TN = (((0,), (0,)), ((), ()))


def _dot(a, b, dims=NN):
    return lax.dot_general(a.astype(BF16), b.astype(BF16), dims, preferred_element_type=F32)


def _cp(n_grid):
    return pltpu.CompilerParams(dimension_semantics=("arbitrary",) * n_grid, vmem_limit_bytes=VMEM_LIMIT)


def _chunk(n, cap):
    best = LANES
    for c in range(LANES, cap + 1, LANES):
        if n % c == 0:
            best = c
    return best


def _resident(shape):
    nd = len(shape)
    return pl.BlockSpec(shape, lambda *_: (0,) * nd, pipeline_mode=pl.Buffered(1))


def _rsum8(x):
    r, c = x.shape
    return x.reshape(r // SUBLANES, SUBLANES, c).sum(axis=0)


def _sigmoid(x):
    return 0.5 * jnp.tanh(0.5 * x) + 0.5


def _gelu(x):
    return 0.5 * x * (1.0 + jnp.tanh(GELU_C0 * (x + GELU_C1 * x * x * x)))


def _gelu_grad(x):
    t = jnp.tanh(GELU_C0 * (x + GELU_C1 * x * x * x))
    return 0.5 * (1.0 + t) + 0.5 * x * (1.0 - t * t) * GELU_C0 * (1.0 + 3.0 * GELU_C1 * x * x)


def _silu_and_grad(x):
    s = _sigmoid(x)
    return x * s, s * (1.0 + x * (1.0 - s))


def _shift_rows(e, j):
    n = e.shape[0]
    return e if j % n == 0 else pltpu.roll(e, j % n, 0)


def _weight(w):
    if isinstance(w, tuple):
        stack, layer = w
        K, N = stack.shape[1:]
        return stack, pl.BlockSpec((None, K, N), lambda *_: (layer, 0, 0), pipeline_mode=pl.Buffered(1)), (K, N)
    return w, _resident(w.shape), w.shape


def _norm_mm(h, g, w, b, name, tt=1024):
    T, D = h.shape
    w, w_spec, (_, N) = _weight(w)
    cn = _chunk(N, 512)

    def body(h_ref, g_ref, w_ref, b_ref, hn_ref, z_ref):
        x = h_ref[...]
        r = lax.rsqrt(jnp.mean(x * x, axis=-1, keepdims=True) + EPS)
        hn = (x * r * g_ref[...]).astype(BF16)
        hn_ref[...] = hn
        for j in range(0, N, cn):
            acc = jnp.dot(hn, w_ref[:, j:j + cn], preferred_element_type=F32) + b_ref[:, j:j + cn]
            z_ref[:, j:j + cn] = acc.astype(BF16)

    return pl.pallas_call(
        body, name=name, grid=(T // tt,),
        in_specs=[pl.BlockSpec((tt, D), lambda i: (i, 0)), _resident((1, D)), w_spec, _resident((1, N))],
        out_specs=[pl.BlockSpec((tt, D), lambda i: (i, 0)), pl.BlockSpec((tt, N), lambda i: (i, 0))],
        out_shape=[jax.ShapeDtypeStruct((T, D), BF16), jax.ShapeDtypeStruct((T, N), BF16)],
        compiler_params=_cp(1),
    )(h, g, w, b)


def _mm(a, w, res, out_dtype, name, tt=1024, transpose_w=False):
    T, K = a.shape
    N = w.shape[0] if transpose_w else w.shape[1]
    cn = _chunk(N, 512)
    has_res = res is not None

    def body(*refs):
        a_ref, w_ref = refs[0], refs[1]
        res_ref = refs[2] if has_res else None
        o_ref = refs[-1]
        av = a_ref[...].astype(BF16)
        for j in range(0, N, cn):
            if transpose_w:
                acc = lax.dot_general(av, w_ref[j:j + cn, :], NT, preferred_element_type=F32)
            else:
                acc = jnp.dot(av, w_ref[:, j:j + cn], preferred_element_type=F32)
            if has_res:
                acc = acc + res_ref[:, j:j + cn]
            o_ref[:, j:j + cn] = acc.astype(out_dtype)

    in_specs = [pl.BlockSpec((tt, K), lambda i: (i, 0)), _resident(w.shape)]
    args = [a, w]
    if has_res:
        in_specs.append(pl.BlockSpec((tt, N), lambda i: (i, 0)))
        args.append(res)
    return pl.pallas_call(
        body, name=name, grid=(T // tt,), in_specs=in_specs,
        out_specs=pl.BlockSpec((tt, N), lambda i: (i, 0)),
        out_shape=jax.ShapeDtypeStruct((T, N), out_dtype),
        compiler_params=_cp(1),
    )(*args)


def _mm_tn(a, b, name, col_shards=1, tt=2048):
    T, K = a.shape
    N = b.shape[1]
    ns = N // col_shards
    tt = min(tt, T)
    while 2 * (tt * K * a.dtype.itemsize + tt * ns * b.dtype.itemsize + K * ns * 4) + K * ns * 4 > VMEM_LIMIT:
        tt //= 2

    def body(a_ref, b_ref, o_ref):
        acc = lax.dot_general(a_ref[...].astype(BF16), b_ref[...].astype(BF16), TN, preferred_element_type=F32)
        prev = jnp.where(pl.program_id(1) == 0, 0.0, o_ref[0])
        o_ref[0] = prev + acc

    out = pl.pallas_call(
        body, name=name, grid=(col_shards, T // tt),
        in_specs=[pl.BlockSpec((tt, K), lambda n, t: (t, 0)), pl.BlockSpec((tt, ns), lambda n, t: (t, n))],
        out_specs=pl.BlockSpec((1, K, ns), lambda n, t: (n, 0, 0)),
        out_shape=jax.ShapeDtypeStruct((col_shards, K, ns), F32),
        compiler_params=_cp(2),
    )(a, b)
    return out if col_shards > 1 else out[0]


def _mm_normbwd(dz, w, x, g, dres, name):
    T, N = dz.shape
    D = w.shape[0]
    tt = 1024 if N <= 3072 else 512

    def body(dz_ref, wt_ref, x_ref, g_ref, dres_ref, dx_ref, dg_ref):
        @pl.when(pl.program_id(0) == 0)
        def _():
            dg_ref[...] = jnp.zeros_like(dg_ref)

        dy = lax.dot_general(dz_ref[...], wt_ref[...], NT, preferred_element_type=F32)
        x = x_ref[...]
        r = lax.rsqrt(jnp.mean(x * x, axis=-1, keepdims=True) + EPS)
        xn = x * r
        dg_ref[...] += _rsum8(dy * xn)
        dxn = dy * g_ref[...]
        dx_ref[...] = dres_ref[...] + r * (dxn - xn * jnp.mean(dxn * xn, axis=-1, keepdims=True))

    return pl.pallas_call(
        body, name=name, grid=(T // tt,),
        in_specs=[pl.BlockSpec((tt, N), lambda i: (i, 0)), _resident((D, N)), pl.BlockSpec((tt, D), lambda i: (i, 0)),
                  _resident((1, D)), pl.BlockSpec((tt, D), lambda i: (i, 0))],
        out_specs=[pl.BlockSpec((tt, D), lambda i: (i, 0)), pl.BlockSpec((SUBLANES, D), lambda i: (0, 0))],
        out_shape=[jax.ShapeDtypeStruct((T, D), F32), jax.ShapeDtypeStruct((SUBLANES, D), F32)],
        compiler_params=_cp(1),
    )(dz, w, x, g, dres)


def _col_groups(F, cc, per_group=4):
    step = cc * per_group
    return [(g0, min(g0 + step, F)) for g0 in range(0, F, step)]


def _loss_head(x, gv, tgt):
    r = lax.rsqrt(jnp.mean(x * x, axis=-1, keepdims=True) + EPS)
    xn = x * r
    diff = xn * gv - tgt
    dy = diff * (1.0 / x.shape[-1])
    dxn = dy * gv
    return r * (dxn - xn * jnp.mean(dxn * xn, axis=-1, keepdims=True)), _rsum8(diff * diff), _rsum8(dy * xn)


def _ffn_act(gu, cw, cb, w_down, res, name, loss=None, tt=512):
    T = gu.shape[0]
    F = gu.shape[1] // 2
    w_down, wd_spec, (_, D) = _weight(w_down)
    cc = _chunk(F, 256)
    hb = tt // BF16_ROWS

    def body(gu_ref, halo_ref, cw_ref, cb_ref, wd_ref, res_ref, *rest):
        if loss is None:
            a_ref, gc_ref, o_ref = rest
        else:
            gf_ref, t_ref, a_ref, gc_ref, o_ref, sq_ref, dgf_ref = rest
        first = pl.program_id(0) == 0
        acc = res_ref[...]
        for g0, g1 in _col_groups(F, cc):
            for c0 in range(g0, g1, cc):
                cs = slice(c0, c0 + cc)
                x = gu_ref[:, cs].astype(F32)
                halo = jnp.where(first, 0.0, halo_ref[:, cs].astype(F32))
                e = jnp.concatenate([halo, x], axis=0)
                gc = (cb_ref[:, cs] + cw_ref[0:1, cs] * _shift_rows(e, 2)[BF16_ROWS:] + cw_ref[1:2, cs] * _shift_rows(e, 1)[BF16_ROWS:]
                      + cw_ref[2:3, cs] * x)
                up = gu_ref[:, F + c0:F + c0 + cc].astype(F32)
                gc_ref[:, cs] = gc.astype(BF16)
                a_ref[:, cs] = (gc * _sigmoid(gc) * up).astype(BF16)
            acc = acc + jnp.dot(a_ref[:, g0:g1], wd_ref[g0:g1, :], preferred_element_type=F32)
        if loss is None:
            o_ref[...] = acc
        else:
            @pl.when(first)
            def _():
                sq_ref[...] = jnp.zeros_like(sq_ref)
                dgf_ref[...] = jnp.zeros_like(dgf_ref)

            dx, sq, dgf = _loss_head(acc, gf_ref[...], t_ref[...])
            o_ref[...] = dx
            sq_ref[...] += sq
            dgf_ref[...] += dgf

    tok = lambda w: pl.BlockSpec((tt, w), lambda i: (i, 0))
    acc8 = pl.BlockSpec((SUBLANES, D), lambda i: (0, 0))
    in_specs = [tok(2 * F), pl.BlockSpec((BF16_ROWS, F), lambda i: (jnp.maximum(i * hb - 1, 0), 0)),
                _resident((SUBLANES, F)), _resident((1, F)), wd_spec, tok(D)]
    out_specs = [tok(F), tok(F), tok(D)]
    out_shape = [jax.ShapeDtypeStruct((T, F), BF16), jax.ShapeDtypeStruct((T, F), BF16), jax.ShapeDtypeStruct((T, D), F32)]
    args = [gu, gu, cw, cb, w_down, res]
    if loss is not None:
        in_specs += [_resident((1, D)), tok(D)]
        out_specs += [acc8, acc8]
        out_shape += [jax.ShapeDtypeStruct((SUBLANES, D), F32)] * 2
        args += list(loss)
    return pl.pallas_call(body, name=name, grid=(T // tt,), in_specs=in_specs, out_specs=out_specs, out_shape=out_shape,
                          compiler_params=_cp(1))(*args)


def _ffn_act_bwd(gu, gc, cw, w_up, w_down, x, g, dh, name, tt=256):
    T = gu.shape[0]
    F = gu.shape[1] // 2
    w_up, wu_spec, (D, _) = _weight(w_up)
    w_down, wd_spec, _ = _weight(w_down)
    cc = _chunk(F, 256)
    hb = tt // BF16_ROWS
    last_hb = T // BF16_ROWS - 1
    nt = T // tt

    def body(gu_ref, upnext_ref, gc_ref, gcnext_ref, cw_ref, wu_ref, wd_ref, x_ref, g_ref, dh_ref, dhnext_ref,
             dgu_ref, dc_ref, dx_ref, dg_ref):
        i = pl.program_id(0)

        @pl.when(i == 0)
        def _():
            dc_ref[...] = jnp.zeros_like(dc_ref)
            dg_ref[...] = jnp.zeros_like(dg_ref)

        n = tt + BF16_ROWS
        ext = lambda main, nxt: jnp.concatenate([main.astype(F32), nxt.astype(F32)], axis=0)
        dhe = ext(dh_ref[...], jnp.where(i == nt - 1, 0.0, dhnext_ref[...])).astype(BF16)
        dy = jnp.zeros((tt, D), F32)
        groups = _col_groups(F, cc)
        da_of = lambda grp: lax.dot_general(dhe, wd_ref[grp[0]:grp[1], :], NT, preferred_element_type=F32)
        da_next = da_of(groups[0])
        for gi, (lo, hi) in enumerate(groups):
            da, da_next = da_next, (da_of(groups[gi + 1]) if gi + 1 < len(groups) else None)
            for c0 in range(lo, hi, cc):
                cs = slice(c0, c0 + cc)
                us = slice(F + c0, F + c0 + cc)
                gc = ext(gc_ref[:, cs], gcnext_ref[:, cs])
                up = ext(gu_ref[:, us], upnext_ref[:, cs])
                dae = da[:, c0 - lo:c0 - lo + cc]
                s, ds = _silu_and_grad(gc)
                dgc = dae * up * ds
                dgu_ref[:, us] = (dae * s)[:tt].astype(BF16)
                dgc1 = _shift_rows(dgc, n - 1)[:tt]
                dgc2 = _shift_rows(dgc, n - 2)[:tt]
                dm = dgc[:tt]
                dgu_ref[:, cs] = (cw_ref[2:3, cs] * dm + cw_ref[1:2, cs] * dgc1 + cw_ref[0:1, cs] * dgc2).astype(BF16)
                gt = gu_ref[:, cs].astype(F32)
                dc_ref[0, :, cs] += _rsum8(dgc2 * gt)
                dc_ref[1, :, cs] += _rsum8(dgc1 * gt)
                dc_ref[2, :, cs] += _rsum8(dm * gt)
                dc_ref[3, :, cs] += _rsum8(dm)
            dy = (dy + lax.dot_general(dgu_ref[:, lo:hi], wu_ref[:, lo:hi], NT, preferred_element_type=F32)
                  + lax.dot_general(dgu_ref[:, F + lo:F + hi], wu_ref[:, F + lo:F + hi], NT, preferred_element_type=F32))
        xv = x_ref[...]
        r = lax.rsqrt(jnp.mean(xv * xv, axis=-1, keepdims=True) + EPS)
        xn = xv * r
        dg_ref[...] += _rsum8(dy * xn)
        dxn = dy * g_ref[...]
        dx_ref[...] = dh_ref[...] + r * (dxn - xn * jnp.mean(dxn * xn, axis=-1, keepdims=True))

    tok = lambda w: pl.BlockSpec((tt, w), lambda i: (i, 0))
    nxt = lambda w, col: pl.BlockSpec((BF16_ROWS, w), lambda i: (jnp.minimum((i + 1) * hb, last_hb), col))
    return pl.pallas_call(
        body, name=name, grid=(nt,),
        in_specs=[tok(2 * F), nxt(F, 1), tok(F), nxt(F, 0),
                  _resident((SUBLANES, F)), wu_spec, wd_spec, tok(D), _resident((1, D)), tok(D), nxt(D, 0)],
        out_specs=[tok(2 * F), pl.BlockSpec((4, SUBLANES, F), lambda i: (0, 0, 0)), tok(D),
                   pl.BlockSpec((SUBLANES, D), lambda i: (0, 0))],
        out_shape=[jax.ShapeDtypeStruct((T, 2 * F), BF16), jax.ShapeDtypeStruct((4, SUBLANES, F), F32),
                   jax.ShapeDtypeStruct((T, D), F32), jax.ShapeDtypeStruct((SUBLANES, D), F32)],
        compiler_params=_cp(1),
    )(gu, gu, gc, gc, cw, w_up, w_down, x, g, dh, dh)


def _softplus_neg(lam):
    x = -lam
    y = jnp.exp(-jnp.abs(x))
    l1p = jnp.where(y < 0.01, y * (1.0 - y * (0.5 - y * (1.0 / 3.0))), jnp.log(1.0 + y))
    return jnp.maximum(x, 0.0) + l1p


def _lru_gates(xc, wa_ref, ba_ref, wx_ref, bx_ref, sp):
    xcb = xc.astype(BF16)
    r = _sigmoid(jnp.dot(xcb, wa_ref[...], preferred_element_type=F32) + ba_ref[...])
    gi = _sigmoid(jnp.dot(xcb, wx_ref[...], preferred_element_type=F32) + bx_ref[...])
    log_a = -LRU_C * r * sp
    a = jnp.exp(log_a)
    x2 = 2.0 * log_a
    series = -x2 * (1.0 + x2 * 0.5 * (1.0 + x2 * (1.0 / 3.0)))
    om = jnp.where(x2 > -0.02, series, 1.0 - a * a)
    return r, gi, a, jnp.sqrt(om)


def _lru_conv(xr, halo, cw_ref, cb_ref):
    e = jnp.concatenate([halo, xr], axis=0)
    x1, x2, x3 = (_shift_rows(e, j)[BF16_ROWS:] for j in (1, 2, 3))
    return cb_ref[...] + cw_ref[0:1, :] * x3 + cw_ref[1:2, :] * x2 + cw_ref[2:3, :] * x1 + cw_ref[3:4, :] * xr


def _lru_fwd(z, cw, cb, wa, ba, wx, bx, lam, name="lru_fwd", tt=512):
    T = z.shape[0]
    W = LRU_W
    hb = tt // BF16_ROWS
    ng = tt // SUBLANES

    def body(z_ref, halo_ref, cw_ref, cb_ref, wa_ref, ba_ref, wx_ref, bx_ref, lam_ref, oa_ref, h_ref, a_s, sv_ref, u_s, hc):
        i = pl.program_id(0)

        @pl.when(i == 0)
        def _():
            hc[...] = jnp.zeros_like(hc)

        xr = z_ref[:, W:2 * W].astype(F32)
        halo = jnp.where(i == 0, 0.0, halo_ref[...].astype(F32))
        xc = _lru_conv(xr, halo, cw_ref, cb_ref)
        sp = _softplus_neg(lam_ref[...])
        r, gi, a, mult = _lru_gates(xc, wa_ref, ba_ref, wx_ref, bx_ref, sp)
        a_s[...] = a
        u_s[...] = mult * gi * xc
        for k, saved in enumerate((mult, r, gi, xc)):
            sv_ref[:, k * W:(k + 1) * W] = saved.astype(BF16)
        row = lax.broadcasted_iota(jnp.int32, (SUBLANES, W), 0)

        def step(j, hprev):
            r0 = pl.multiple_of(j * SUBLANES, SUBLANES)
            A = a_s[pl.ds(r0, SUBLANES), :]
            U = u_s[pl.ds(r0, SUBLANES), :]
            for k in (1, 2, 4):
                m = row >= k
                U = jnp.where(m, A * pltpu.roll(U, k, 0) + U, U)
                A = jnp.where(m, A * pltpu.roll(A, k, 0), A)
            H = U + A * hprev
            h_ref[pl.ds(r0, SUBLANES), :] = H
            return jnp.broadcast_to(H[SUBLANES - 1:SUBLANES, :], (SUBLANES, W))

        hc[...] = lax.fori_loop(0, ng, step, hc[...])
        oa_ref[...] = (_gelu(z_ref[:, 0:W].astype(F32)) * h_ref[...]).astype(BF16)

    return pl.pallas_call(
        body, name=name, grid=(T // tt,),
        in_specs=[pl.BlockSpec((tt, 2 * W), lambda i: (i, 0)),
                  pl.BlockSpec((BF16_ROWS, W), lambda i: (jnp.maximum(i * hb - 1, 0), 1)),
                  _resident((SUBLANES, W)), _resident((1, W)), _resident((W, W)), _resident((1, W)),
                  _resident((W, W)), _resident((1, W)), _resident((1, W))],
        out_specs=[pl.BlockSpec((tt, W), lambda i: (i, 0)), pl.BlockSpec((tt, W), lambda i: (i, 0)),
                   pl.BlockSpec((tt, W), lambda i: (i, 0)), pl.BlockSpec((tt, 4 * W), lambda i: (i, 0))],
        out_shape=[jax.ShapeDtypeStruct((T, 2 * W), BF16), jax.ShapeDtypeStruct((T, W), F32),
                   jax.ShapeDtypeStruct((T, W), F32), jax.ShapeDtypeStruct((T, 4 * W), BF16)],
        scratch_shapes=[pltpu.VMEM((tt, W), F32), pltpu.VMEM((SUBLANES, W), F32)],
        compiler_params=_cp(1),
    )(z, z, cw, cb, wa, ba, wx, bx, lam)


def _lru_bwd(z, hseq, a_all, saved, dmix, cw, wat, wxt, lam, name="lru_bwd", tt=512):
    T = z.shape[0]
    W = LRU_W
    nt = T // tt
    sb = tt // SUBLANES
    ng = tt // SUBLANES

    def body(z_ref, h_ref, hprev_ref, a_ref, sv_ref, dm_ref, cw_ref, wat_ref, wxt_ref, lam_ref,
             dz_ref, dc_ref, dwa_ref, dwx_ref, dv_ref, c_s, d_s, g_s, gc, an, dxn):
        i = pl.program_id(0)
        ti = nt - 1 - i

        @pl.when(i == 0)
        def _():
            dc_ref[...] = jnp.zeros_like(dc_ref)
            dwa_ref[...] = jnp.zeros_like(dwa_ref)
            dwx_ref[...] = jnp.zeros_like(dwx_ref)
            dv_ref[...] = jnp.zeros_like(dv_ref)
            gc[...] = jnp.zeros_like(gc)
            an[...] = jnp.zeros_like(an)
            dxn[...] = jnp.zeros_like(dxn)

        xr = z_ref[:, W:2 * W].astype(F32)
        yg = z_ref[:, 0:W].astype(F32)
        sp = _softplus_neg(lam_ref[...])
        a = a_ref[...]
        mult, r, gi, xc = (sv_ref[:, k * W:(k + 1) * W].astype(F32) for k in range(4))
        h = h_ref[...]
        hp = jnp.where(ti == 0, 0.0, hprev_ref[...])
        hm1 = _shift_rows(jnp.concatenate([hp, h], axis=0), 1)[SUBLANES:]
        dout = dm_ref[...].astype(F32)
        d_s[...] = dout * _gelu(yg)
        dz_ref[:, 0:W] = (dout * h * _gelu_grad(yg)).astype(BF16)
        c_s[...] = _shift_rows(jnp.concatenate([a, an[...]], axis=0), tt + SUBLANES - 1)[:tt]
        an[...] = a[0:SUBLANES, :]
        row = lax.broadcasted_iota(jnp.int32, (SUBLANES, W), 0)

        def step(j, gnext):
            r0 = pl.multiple_of((ng - 1 - j) * SUBLANES, SUBLANES)
            C = c_s[pl.ds(r0, SUBLANES), :]
            G = d_s[pl.ds(r0, SUBLANES), :]
            for k in (1, 2, 4):
                m = row < SUBLANES - k
                G = jnp.where(m, G + C * pltpu.roll(G, SUBLANES - k, 0), G)
                C = jnp.where(m, C * pltpu.roll(C, SUBLANES - k, 0), C)
            G = G + C * gnext
            g_s[pl.ds(r0, SUBLANES), :] = G
            return jnp.broadcast_to(G[0:1, :], (SUBLANES, W))

        gc[...] = lax.fori_loop(0, ng, step, gc[...])
        du = g_s[...]
        da = du * hm1
        dgi = du * mult * xc
        dxc = du * mult * gi
        dmult = du * gi * xc
        dlog_a = da * a - dmult * (a * a) / mult
        dr = dlog_a * (-LRU_C * sp)
        dv_ref[2] += _rsum8(dlog_a * (-LRU_C * r))
        dpr = (dr * r * (1.0 - r)).astype(BF16)
        dpi = (dgi * gi * (1.0 - gi)).astype(BF16)
        dv_ref[0] += _rsum8(dpr.astype(F32))
        dv_ref[1] += _rsum8(dpi.astype(F32))
        xcb = sv_ref[:, 3 * W:4 * W]
        dwa_ref[...] += lax.dot_general(xcb, dpr, TN, preferred_element_type=F32)
        dwx_ref[...] += lax.dot_general(xcb, dpi, TN, preferred_element_type=F32)
        dxc = dxc + jnp.dot(dpr, wat_ref[...], preferred_element_type=F32) + jnp.dot(dpi, wxt_ref[...], preferred_element_type=F32)
        n = tt + BF16_ROWS
        de = jnp.concatenate([dxc, dxn[...]], axis=0)
        d1, d2, d3 = (_shift_rows(de, n - j)[:tt] for j in (1, 2, 3))
        dxn[...] = dxc[0:BF16_ROWS, :]
        dz_ref[:, W:2 * W] = (cw_ref[3:4, :] * dxc + cw_ref[2:3, :] * d1 + cw_ref[1:2, :] * d2 + cw_ref[0:1, :] * d3).astype(BF16)
        dc_ref[0] += _rsum8(d3 * xr)
        dc_ref[1] += _rsum8(d2 * xr)
        dc_ref[2] += _rsum8(d1 * xr)
        dc_ref[3] += _rsum8(dxc * xr)
        dc_ref[4] += _rsum8(dxc)

    rev = lambda i: nt - 1 - i
    tok = lambda w: pl.BlockSpec((tt, w), lambda i: (rev(i), 0))
    return pl.pallas_call(
        body, name=name, grid=(nt,),
        in_specs=[tok(2 * W), tok(W),
                  pl.BlockSpec((SUBLANES, W), lambda i: (jnp.maximum(rev(i) * sb - 1, 0), 0)),
                  tok(W), tok(4 * W), tok(W),
                  _resident((SUBLANES, W)), _resident((W, W)), _resident((W, W)), _resident((1, W))],
        out_specs=[pl.BlockSpec((tt, 2 * W), lambda i: (rev(i), 0)),
                   pl.BlockSpec((5, SUBLANES, W), lambda i: (0, 0, 0)),
                   pl.BlockSpec((W, W), lambda i: (0, 0)), pl.BlockSpec((W, W), lambda i: (0, 0)),
                   pl.BlockSpec((3, SUBLANES, W), lambda i: (0, 0, 0))],
        out_shape=[jax.ShapeDtypeStruct((T, z.shape[1]), BF16), jax.ShapeDtypeStruct((5, SUBLANES, W), F32),
                   jax.ShapeDtypeStruct((W, W), F32), jax.ShapeDtypeStruct((W, W), F32),
                   jax.ShapeDtypeStruct((3, SUBLANES, W), F32)],
        scratch_shapes=[pltpu.VMEM((tt, W), F32), pltpu.VMEM((tt, W), F32), pltpu.VMEM((tt, W), F32),
                        pltpu.VMEM((SUBLANES, W), F32), pltpu.VMEM((SUBLANES, W), F32), pltpu.VMEM((BF16_ROWS, W), F32)],
        compiler_params=_cp(1),
    )(z, hseq, hseq, a_all, saved, dmix, cw, wat, wxt, lam)


def _split3(x):
    hi = x.astype(BF16)
    r1 = x - hi.astype(F32)
    mid = r1.astype(BF16)
    lo = (r1 - mid.astype(F32)).astype(BF16)
    return hi, mid, lo


def _tri_matmul(tri, x):
    hi, mid, lo = _split3(x)
    return (jnp.dot(tri, hi, preferred_element_type=F32) + jnp.dot(tri, mid, preferred_element_type=F32)
            + jnp.dot(tri, lo, preferred_element_type=F32))


def _hg_chunk(q, fl, lb):
    C = q.shape[0]
    ri = lax.broadcasted_iota(jnp.int32, (C, C), 0)
    ci = lax.broadcasted_iota(jnp.int32, (C, C), 1)
    causal = ri >= ci
    sig = _sigmoid(fl)
    f = lb + (1.0 - lb) * sig
    k = 1.0 - f
    sq = _sigmoid(q)
    qf = q * sq
    b = _tri_matmul(causal.astype(BF16), jnp.log(f))
    bm = b[C // 2 - 1:C // 2, :]
    bl = b[C - 1:C, :]
    e_qt, e_kt, e_in, e_out = jnp.exp(b - bm), jnp.exp(bm - b), jnp.exp(b), jnp.exp(bl - b)
    qt = qf * e_qt
    kt = k * e_kt
    qin = qf * e_in
    kout = k * e_out
    qtb, ktb = qt.astype(BF16), kt.astype(BF16)
    att = [jnp.where(causal, _dot(qtb[:, _head(h)], ktb[:, _head(h)], NT), 0.0).astype(BF16) for h in range(HG_HEADS)]
    return dict(sig=sig, f=f, k=k, sq=sq, qf=qf, b=b, bm=bm, bl=bl, qt=qt, kt=kt, qin=qin, kout=kout, att=att,
                causal=causal, anti=ri <= ci, decay=jnp.exp(bl), e_qt=e_qt, e_kt=e_kt, e_in=e_in, e_out=e_out)


def _head(h):
    return slice(h * HG_D, (h + 1) * HG_D)


def _hgrn_fwd(z, lb, gn, mix, name="hgrn_fwd", tt=512):
    T = z.shape[0]
    C = HG_CHUNK
    nc = tt // C
    Dh = HG_D
    Wd = HG_HEADS * Dh

    def body(q_ref, f_ref, v_ref, g_ref, lb_ref, gn_ref, mix_ref, o_ref, ss_ref, st):
        del mix_ref

        @pl.when(pl.program_id(0) == 0)
        def _():
            st[...] = jnp.zeros_like(st)

        S = [st[h] for h in range(HG_HEADS)]
        for c in range(nc):
            rows = slice(c * C, (c + 1) * C)
            ck = _hg_chunk(q_ref[rows, :].astype(F32), f_ref[rows, :].astype(F32), lb_ref[...])
            v = v_ref[rows, :]
            g = g_ref[rows, :].astype(F32)
            H = range(HG_HEADS)
            qinb, koutb = ck["qin"].astype(BF16), ck["kout"].astype(BF16)
            for h in H:
                ss_ref[h, c] = S[h]
            o = [_dot(ck["att"][h], v[:, _head(h)]) + _dot(qinb[:, _head(h)], S[h], NT) for h in H]
            S = [ck["decay"][:, _head(h)] * S[h] + _dot(v[:, _head(h)], koutb[:, _head(h)], TN) for h in H]
            outs = [o[h] * lax.rsqrt(jnp.mean(o[h] * o[h], axis=-1, keepdims=True) + EPS) * gn_ref[...] for h in H]
            o_ref[rows, :] = (jnp.concatenate(outs, axis=1) * (g * _sigmoid(g))).astype(BF16)
        for h in range(HG_HEADS):
            st[h] = S[h]

    col = lambda base: (lambda i: (i, base))
    return pl.pallas_call(
        body, name=name, grid=(T // tt,),
        in_specs=[pl.BlockSpec((tt, Wd), col(2)), pl.BlockSpec((tt, Wd), col(3)), pl.BlockSpec((tt, Wd), col(4)),
                  pl.BlockSpec((tt, Wd), col(5)), _resident((1, Wd)), _resident((1, Dh)), ANY],
        out_specs=[pl.BlockSpec((tt, Wd), lambda i: (i, 1)),
                   pl.BlockSpec((HG_HEADS, nc, Dh, Dh), lambda i: (0, i, 0, 0))],
        out_shape=[jax.ShapeDtypeStruct((T, 2 * Wd), BF16),
                   jax.ShapeDtypeStruct((HG_HEADS, T // C, Dh, Dh), F32)],
        scratch_shapes=[pltpu.VMEM((HG_HEADS, Dh, Dh), F32)],
        compiler_params=_cp(1),
        input_output_aliases={6: 0},
    )(z, z, z, z, lb, gn, mix)


def _hgrn_bwd(z, ss, dmix, lb, gn, dz, name="hgrn_bwd", tt=512):
    T = z.shape[0]
    C = HG_CHUNK
    nc = tt // C
    nt = T // tt
    Dh = HG_D
    Wd = HG_HEADS * Dh

    def body(q_ref, f_ref, v_ref, g_ref, ss_ref, dm_ref, lb_ref, gn_ref, dz01_ref, dz_ref, dlb_ref, dgn_ref, dst):
        @pl.when(pl.program_id(0) == 0)
        def _():
            dst[...] = jnp.zeros_like(dst)
            dlb_ref[...] = jnp.zeros_like(dlb_ref)
            dgn_ref[...] = jnp.zeros_like(dgn_ref)

        dS = [dst[h] for h in range(HG_HEADS)]
        lbv = lb_ref[...]
        gnv = gn_ref[...]
        rowc = lax.broadcasted_iota(jnp.int32, (C, Wd), 0)
        cat = lambda xs: jnp.concatenate(xs, axis=1)
        for c in reversed(range(nc)):
            rows = slice(c * C, (c + 1) * C)
            q = q_ref[rows, :].astype(F32)
            ck = _hg_chunk(q, f_ref[rows, :].astype(F32), lbv)
            v = v_ref[rows, :]
            g = g_ref[rows, :].astype(F32)
            dout = dm_ref[rows, :].astype(F32)
            sg, dsg = _silu_and_grad(g)
            d_ong = dout * sg
            H = range(HG_HEADS)
            qinb, koutb, qtb, ktb = (ck[n].astype(BF16) for n in ("qin", "kout", "qt", "kt"))
            S = [ss_ref[h, c] for h in H]
            Sb = [s.astype(BF16) for s in S]
            dSb = [d.astype(BF16) for d in dS]
            o = [_dot(ck["att"][h], v[:, _head(h)]) + _dot(qinb[:, _head(h)], Sb[h], NT) for h in H]
            rn = [lax.rsqrt(jnp.mean(o[h] * o[h], axis=-1, keepdims=True) + EPS) for h in H]
            on = [o[h] * rn[h] for h in H]
            don = [d_ong[:, _head(h)] * gnv for h in H]
            do = [(rn[h] * (don[h] - on[h] * jnp.mean(don[h] * on[h], axis=-1, keepdims=True))).astype(BF16) for h in H]
            datt = [jnp.where(ck["causal"], _dot(do[h], v[:, _head(h)], NT), 0.0).astype(BF16) for h in H]
            dvs = [_dot(ck["att"][h], do[h], TN) + _dot(koutb[:, _head(h)], dSb[h], NT) for h in H]
            dqins = [_dot(do[h], Sb[h]) for h in H]
            dkouts = [_dot(v[:, _head(h)], dSb[h]) for h in H]
            dqts = [_dot(datt[h], ktb[:, _head(h)]) for h in H]
            dkts = [_dot(datt[h], qtb[:, _head(h)], TN) for h in H]
            ddecays = [jnp.sum(dS[h] * S[h], axis=0, keepdims=True) for h in H]
            dS = [_dot(do[h], qinb[:, _head(h)], TN) + ck["decay"][:, _head(h)] * dS[h] for h in H]
            ons = [on[h] * gnv for h in H]
            dgn = _rsum8(d_ong[:, _head(0)] * on[0])
            for h in range(1, HG_HEADS):
                dgn = dgn + _rsum8(d_ong[:, _head(h)] * on[h])
            dgn_ref[...] += dgn
            dqt, dkt, dqin, dkout, ddecay = cat(dqts), cat(dkts), cat(dqins), cat(dkouts), cat(ddecays)
            dqf = dqt * ck["e_qt"] + dqin * ck["e_in"]
            dk = dkt * ck["e_kt"] + dkout * ck["e_out"]
            kk = dkout * ck["kout"]
            db = dqt * ck["qt"] - dkt * ck["kt"] + dqin * ck["qin"] - kk
            dbl = jnp.sum(kk, axis=0, keepdims=True) + ddecay * ck["decay"]
            db = db + jnp.where(rowc == C - 1, dbl, 0.0)
            dlogf = _tri_matmul(ck["anti"].astype(BF16), db)
            dfv = dlogf / ck["f"] - dk
            sig, sq = ck["sig"], ck["sq"]
            dlb_ref[...] += _rsum8(dfv * (1.0 - sig))
            dz_ref[rows, 2 * Wd:3 * Wd] = (dqf * (sq * (1.0 + q * (1.0 - sq)))).astype(BF16)
            dz_ref[rows, 3 * Wd:4 * Wd] = (dfv * (1.0 - lbv) * sig * (1.0 - sig)).astype(BF16)
            dz_ref[rows, 4 * Wd:5 * Wd] = cat(dvs).astype(BF16)
            dz_ref[rows, 5 * Wd:6 * Wd] = (dout * cat(ons) * dsg).astype(BF16)
        dz_ref[:, 0:2 * Wd] = dz01_ref[...]
        for h in range(HG_HEADS):
            dst[h] = dS[h]

    rev = lambda i: nt - 1 - i
    col = lambda base: (lambda i: (rev(i), base))
    return pl.pallas_call(
        body, name=name, grid=(nt,),
        in_specs=[pl.BlockSpec((tt, Wd), col(2)), pl.BlockSpec((tt, Wd), col(3)), pl.BlockSpec((tt, Wd), col(4)),
                  pl.BlockSpec((tt, Wd), col(5)),
                  pl.BlockSpec((HG_HEADS, nc, Dh, Dh), lambda i: (0, rev(i), 0, 0)),
                  pl.BlockSpec((tt, Wd), col(1)), _resident((1, Wd)), _resident((1, Dh)),
                  pl.BlockSpec((tt, 2 * Wd), col(0))],
        out_specs=[pl.BlockSpec((tt, 6 * Wd), lambda i: (rev(i), 0)), pl.BlockSpec((SUBLANES, Wd), lambda i: (0, 0)),
                   pl.BlockSpec((SUBLANES, Dh), lambda i: (0, 0))],
        out_shape=[jax.ShapeDtypeStruct((T, 6 * Wd), BF16), jax.ShapeDtypeStruct((SUBLANES, Wd), F32),
                   jax.ShapeDtypeStruct((SUBLANES, Dh), F32)],
        input_output_aliases={8: 0},
        scratch_shapes=[pltpu.VMEM((HG_HEADS, Dh, Dh), F32)],
        compiler_params=_cp(1),
    )(z, z, z, z, ss, dmix, lb, gn, dz)


def _sgu_core(p, lg_ref, lb_ref, wsc_ref, bsb_ref):
    Wd = D_MODEL
    G = SGU_CHUNK
    zz = _gelu(p)
    u = zz[:, :Wd]
    v = zz[:, Wd:]
    vc = v - jnp.mean(v, axis=-1, keepdims=True)
    rstd = lax.rsqrt(jnp.mean(vc * vc, axis=-1, keepdims=True) + EPS)
    vhat = vc * rstd
    vn = vhat * lg_ref[...] + lb_ref[...]
    svs = []
    for gi in range(SGU_G):
        svs.append(jnp.dot(wsc_ref[gi], vn[:, gi * G:(gi + 1) * G].astype(BF16), preferred_element_type=F32) + bsb_ref[gi])
    return u, vhat, rstd, vn, jnp.concatenate(svs, axis=1)


def _sgu_fwd(p1, lg, lbias, wsc, bsb, name="sgu_fwd", tt=512):
    T = p1.shape[0]
    Wd = D_MODEL
    C = SGU_CHUNK

    def body(p_ref, lg_ref, lb_ref, wsc_ref, bsb_ref, s_ref):
        for c in range(tt // C):
            rows = slice(c * C, (c + 1) * C)
            u, _, _, _, sv = _sgu_core(p_ref[rows, :].astype(F32), lg_ref, lb_ref, wsc_ref, bsb_ref)
            s_ref[rows, :] = (u * sv).astype(BF16)

    return pl.pallas_call(
        body, name=name, grid=(T // tt,),
        in_specs=[pl.BlockSpec((tt, 2 * Wd), lambda i: (i, 0)), _resident((1, Wd)), _resident((1, Wd)),
                  _resident((SGU_G, C, C)), _resident((SGU_G, C, C))],
        out_specs=pl.BlockSpec((tt, Wd), lambda i: (i, 0)),
        out_shape=jax.ShapeDtypeStruct((T, Wd), BF16),
        compiler_params=_cp(1),
    )(p1, lg, lbias, wsc, bsb)


def _sgu_bwd(p1, ds, lg, lbias, wsc, wsct, bsb, name="sgu_bwd", tt=512):
    T = p1.shape[0]
    Wd = D_MODEL
    C = SGU_CHUNK

    def body(p_ref, ds_ref, lg_ref, lb_ref, wsc_ref, wsct_ref, bsb_ref, dp_ref, dws_ref, dbs_ref, dlg_ref, dlb_ref, dbin_ref):
        @pl.when(pl.program_id(0) == 0)
        def _():
            dws_ref[...] = jnp.zeros_like(dws_ref)
            dbs_ref[...] = jnp.zeros_like(dbs_ref)
            dlg_ref[...] = jnp.zeros_like(dlg_ref)
            dlb_ref[...] = jnp.zeros_like(dlb_ref)
            dbin_ref[...] = jnp.zeros_like(dbin_ref)

        for c in range(tt // C):
            rows = slice(c * C, (c + 1) * C)
            p = p_ref[rows, :].astype(F32)
            u, vhat, rstd, vn, sv = _sgu_core(p, lg_ref, lb_ref, wsc_ref, bsb_ref)
            dsc = ds_ref[rows, :].astype(F32)
            du = dsc * sv
            dsv = dsc * u
            dvns = []
            for gi in range(SGU_G):
                cs = slice(gi * C, (gi + 1) * C)
                dsv_g = dsv[:, cs]
                dvns.append(jnp.dot(wsct_ref[gi], dsv_g.astype(BF16), preferred_element_type=F32))
                dws_ref[gi] += _dot(dsv_g, vn[:, cs], NT)
                dbs_ref[gi] += dsv_g
            dvn = jnp.concatenate(dvns, axis=1)
            dlg_ref[...] += _rsum8(dvn * vhat)
            dlb_ref[...] += _rsum8(dvn)
            dvh = dvn * lg_ref[...]
            dv = rstd * (dvh - jnp.mean(dvh, axis=-1, keepdims=True) - vhat * jnp.mean(dvh * vhat, axis=-1, keepdims=True))
            dp = jnp.concatenate([du, dv], axis=1) * _gelu_grad(p)
            dbin_ref[...] += _rsum8(dp)
            dp_ref[rows, :] = dp.astype(BF16)

    full3 = pl.BlockSpec((SGU_G, C, C), lambda i: (0, 0, 0))
    return pl.pallas_call(
        body, name=name, grid=(T // tt,),
        in_specs=[pl.BlockSpec((tt, 2 * Wd), lambda i: (i, 0)), pl.BlockSpec((tt, Wd), lambda i: (i, 0)),
                  _resident((1, Wd)), _resident((1, Wd)), _resident((SGU_G, C, C)), _resident((SGU_G, C, C)),
                  _resident((SGU_G, C, C))],
        out_specs=[pl.BlockSpec((tt, 2 * Wd), lambda i: (i, 0)), full3, full3,
                   pl.BlockSpec((SUBLANES, Wd), lambda i: (0, 0)), pl.BlockSpec((SUBLANES, Wd), lambda i: (0, 0)),
                   pl.BlockSpec((SUBLANES, 2 * Wd), lambda i: (0, 0))],
        out_shape=[jax.ShapeDtypeStruct((T, 2 * Wd), BF16), jax.ShapeDtypeStruct((SGU_G, C, C), F32),
                   jax.ShapeDtypeStruct((SGU_G, C, C), F32), jax.ShapeDtypeStruct((SUBLANES, Wd), F32),
                   jax.ShapeDtypeStruct((SUBLANES, Wd), F32), jax.ShapeDtypeStruct((SUBLANES, 2 * Wd), F32)],
        compiler_params=_cp(1),
    )(p1, ds, lg, lbias, wsc, wsct, bsb)


def _pad_rows(w, rows=SUBLANES):
    return jnp.pad(w, ((0, rows - w.shape[0]), (0, 0)))


def _block_diag(w):
    n, b, _ = w.shape
    return (w[:, :, None, :] * jnp.eye(n, dtype=w.dtype)[:, None, :, None]).reshape(n * b, n * b)


def _diag_blocks(m, n):
    b = m.shape[0] // n
    m4 = m.reshape(n, b, n, b)
    return jnp.stack([m4[k, :, k, :] for k in range(n)], axis=0)


def _piece_major(dw):
    if dw.ndim == 2:
        K, N = dw.shape
        return dw.reshape(N_CHIPS, 2, K // (2 * N_CHIPS), N)
    _, K, ns = dw.shape
    return dw.reshape(N_CHIPS, 2, K // 2, ns)


def _ffn_fwd(h, g, w_up, cw, cb, w_down, tag, loss=None):
    hn, gu = _norm_mm(h, g, w_up, jnp.zeros((1, 2 * D_FF), F32), name=f"ffn_up_{tag}")
    a, gc, *out = _ffn_act(gu, cw, cb, w_down, h, name=f"ffn_act_down_{tag}", loss=loss)
    return (out[0] if loss is None else out), (hn, gu, gc, a)


def _ffn_bwd(dh, h, g, saved, w_up, cw, w_down, tag):
    hn, gu, gc, a = saved
    dwd = _mm_tn(a, dh, name=f"ffn_dwd_{tag}")
    dgu, dc, dhin, dg8 = _ffn_act_bwd(gu, gc, cw, w_up, w_down, h, g, dh, name=f"ffn_bwd_{tag}")
    dwu = _mm_tn(hn, dgu, name=f"ffn_dwu_{tag}", col_shards=N_CHIPS)
    dcs = dc.sum(axis=1)
    return dhin, dg8.sum(axis=0), dwu, dcs[0:3], dcs[3], dwd


REDUCE_GROUPS = {"g1": [("ffn_w_up", 1), ("ffn_w_down", 1), ("od_w_out", 0), ("od_w_in", 0)],
                 "g2": [("ffn_w_up", 0), ("ffn_w_down", 0)],
                 "g3": [("ev_w_out", 0), ("ev_w_in", 0)]}


def _local_step(x, tgt, p, start_reduce=None, continue_reduce=None):
    row = lambda v: v.reshape(1, -1)
    grads = {}

    lower = jax.nn.softmax(p["hg_lb_logits"], axis=0)
    lb0 = row(lower[0])
    ev_cw = _pad_rows(p["ev_conv_w"][0])
    ev_cb = row(p["ev_conv_b"][0])
    wa = _block_diag(p["ev_gate_a_w"][0]).astype(BF16)
    wx = _block_diag(p["ev_gate_x_w"][0]).astype(BF16)
    ba, bx, lam = row(p["ev_gate_a_b"][0]), row(p["ev_gate_x_b"][0]), row(p["ev_lru_lambda"][0])
    gn = row(p["ev_hg_norm"][0])
    tril = jnp.tril(jnp.ones((SGU_CHUNK, SGU_CHUNK), F32))
    wsc = (p["od_w_s"][0] * tril).astype(BF16)
    bsb = jnp.broadcast_to(p["od_b_s"][0][:, :, None], (SGU_G, SGU_CHUNK, SGU_CHUNK)).astype(F32)
    ffn_cw = [_pad_rows(p["ffn_conv_w"][l]) for l in range(2)]
    ffn_cb = [row(p["ffn_conv_b"][l]) for l in range(2)]
    ev_w_in, ev_w_out = p["ev_w_in"][0], p["ev_w_out"][0]
    nm = [row(p["norm_mix"][l]) for l in range(2)]
    nf = [row(p["norm_ffn"][l]) for l in range(2)]

    h0 = x
    hn0, z0 = _norm_mm(h0, nm[0], ev_w_in, jnp.zeros((1, ev_w_in.shape[1]), F32), name="ev_in")
    out_a, hseq, a_all, lru_saved = _lru_fwd(z0, ev_cw, ev_cb, wa, ba, wx, bx, lam)
    mix0, ss = _hgrn_fwd(z0, lb0, gn, out_a)
    h1 = _mm(mix0, ev_w_out, h0, F32, name="ev_out")
    late = p["late"](h1) if "late" in p else p
    od_w_in, od_w_out = late["od_w_in"][0], late["od_w_out"][0]
    w_up = [(late["ffn_w_up"], l) for l in range(2)]
    w_down = [(late["ffn_w_down"], l) for l in range(2)]
    h2, ffn0 = _ffn_fwd(h1, nf[0], w_up[0], ffn_cw[0], ffn_cb[0], w_down[0], "l0")
    hn1, p1 = _norm_mm(h2, nm[1], od_w_in, row(p["od_b_in"][0]), name="od_in")
    s1 = _sgu_fwd(p1, row(p["od_ln_g"][0]), row(p["od_ln_b"][0]), wsc, bsb)
    h3 = _mm(s1, od_w_out, h2, F32, name="od_out")
    (dh4, sq8, dgf8), ffn1 = _ffn_fwd(h3, nf[1], w_up[1], ffn_cw[1], ffn_cb[1], w_down[1], "l1", loss=(row(p["norm_final"]), tgt))
    grads["norm_final"] = dgf8.sum(axis=0)

    big = {}
    dh3, dnf1, dwu1, dcw1, dcb1, dwd1 = _ffn_bwd(dh4, h3, nf[1], ffn1, w_up[1], ffn_cw[1], w_down[1], "l1")
    big["ffn_w_up", 1], big["ffn_w_down", 1] = _piece_major(dwu1), _piece_major(dwd1)
    ds1 = _mm(dh3, od_w_out, None, BF16, name="od_ds", transpose_w=True)
    big["od_w_out", 0] = _piece_major(_mm_tn(s1, dh3, name="od_dwo"))
    wsct = jnp.swapaxes(wsc, 1, 2)
    dp1, dws, dbs, dlg8, dlb8, dbin8 = _sgu_bwd(p1, ds1, row(p["od_ln_g"][0]), row(p["od_ln_b"][0]), wsc, wsct, bsb)
    grads["od_w_s"] = (dws * tril)[None]
    grads["od_b_s"] = dbs.sum(axis=-1)[None]
    grads["od_ln_g"] = dlg8.sum(axis=0)[None]
    grads["od_ln_b"] = dlb8.sum(axis=0)[None]
    grads["od_b_in"] = dbin8.sum(axis=0)[None]
    dh2, dnm1 = _mm_normbwd(dp1, od_w_in, h2, nm[1], dh3, name="od_dh")
    big["od_w_in", 0] = _piece_major(_mm_tn(hn1, dp1, name="od_dwi", col_shards=N_CHIPS))
    if start_reduce is not None:
        token = start_reduce("g1", [big[key] for key in REDUCE_GROUPS["g1"]], split=True)
        ffn_cw[0] = ffn_cw[0] + token[0:1, 0:1]

    dh1, dnf0, dwu0, dcw0, dcb0, dwd0 = _ffn_bwd(dh2, h1, nf[0], ffn0, w_up[0], ffn_cw[0], w_down[0], "l0")
    big["ffn_w_up", 0], big["ffn_w_down", 0] = _piece_major(dwu0), _piece_major(dwd0)
    if start_reduce is not None:
        token = continue_reduce("g1", dh1) + start_reduce("g2", [big[key] for key in REDUCE_GROUPS["g2"]], split=True)
        lam = lam + token[0:1, 0:1]
    dmix = _mm(dh1, ev_w_out, None, BF16, name="ev_dmix", transpose_w=True)
    big["ev_w_out", 0] = _piece_major(_mm_tn(mix0, dh1, name="ev_dwo"))
    dz01, dc5, dwa, dwx, dvec = _lru_bwd(z0, hseq, a_all, lru_saved, dmix, ev_cw, wa.T, wx.T, lam)
    if start_reduce is not None:
        lb0 = lb0 + continue_reduce("g2", dc5)[0:1, 0:1]
    dz0, dlb8, dgn8 = _hgrn_bwd(z0, ss, dmix, lb0, gn, dz01)
    big["ev_w_in", 0] = _piece_major(_mm_tn(hn0, dz0, name="ev_dwi", col_shards=N_CHIPS))
    if start_reduce is not None:
        token = start_reduce("g3", [big[key] for key in REDUCE_GROUPS["g3"]])
        nm[0] = nm[0] + token[0:1, 0:1]
    grad_x, dnm0 = _mm_normbwd(dz0, ev_w_in, h0, nm[0], dh1, name="ev_dh")

    dc5s = dc5.sum(axis=1)
    grads["ev_conv_w"] = dc5s[0:4][None]
    grads["ev_conv_b"] = dc5s[4][None]
    grads["ev_gate_a_w"] = _diag_blocks(dwa, LRU_BLOCKS)[None]
    grads["ev_gate_x_w"] = _diag_blocks(dwx, LRU_BLOCKS)[None]
    dvs = dvec.sum(axis=1)
    grads["ev_gate_a_b"] = dvs[0][None]
    grads["ev_gate_x_b"] = dvs[1][None]
    grads["ev_lru_lambda"] = (dvs[2] * (-jax.nn.sigmoid(-p["ev_lru_lambda"][0])))[None]
    dlb = dlb8.sum(axis=0)
    grads["hg_lb_logits"] = dlb[None, :] * lower[0][None, :] * (jnp.eye(3, dtype=F32)[0][:, None] - lower)
    grads["ev_hg_norm"] = dgn8.sum(axis=0)[None]
    grads["norm_mix"] = jnp.stack([dnm0.sum(axis=0), dnm1.sum(axis=0)])
    grads["norm_ffn"] = jnp.stack([dnf0, dnf1])
    grads["ffn_conv_w"] = jnp.stack([dcw0, dcw1])
    grads["ffn_conv_b"] = jnp.stack([dcb0, dcb1])
    return sq8, grad_x, grads, big


SH_BIG = {"ev_w_in": 2, "ev_w_out": 1, "od_w_in": 2, "od_w_out": 1, "ffn_w_up": 2, "ffn_w_down": 1}
SH_SMALL = {"ev_conv_w": 2, "od_b_in": 1, "od_ln_g": 1, "od_ln_b": 1, "ffn_conv_w": 2}
REP = ["norm_mix", "norm_ffn", "norm_final", "ev_conv_b", "ev_gate_a_w", "ev_gate_a_b", "ev_gate_x_w", "ev_gate_x_b",
       "ev_lru_lambda", "hg_lb_logits", "ev_hg_norm", "od_w_s", "od_b_s", "ffn_conv_b"]
WEIGHTS = ["norm_mix", "norm_ffn", "norm_final", "ev_w_in", "ev_conv_w", "ev_conv_b", "ev_gate_a_w", "ev_gate_a_b", "ev_gate_x_w",
           "ev_gate_x_b", "ev_lru_lambda", "hg_lb_logits", "ev_hg_norm", "ev_w_out", "od_w_in", "od_b_in", "od_ln_g", "od_ln_b",
           "od_w_s", "od_b_s", "od_w_out", "ffn_w_up", "ffn_conv_w", "ffn_conv_b", "ffn_w_down"]


def _rows(n_elems, mult=SUBLANES):
    r = -(-n_elems // LANES)
    return -(-r // mult) * mult


def _pack(arrs, rows, dtype):
    flat = jnp.concatenate([a.reshape(-1).astype(dtype) for a in arrs])
    return jnp.pad(flat, (0, rows * LANES - flat.shape[0])).reshape(rows, LANES)


def _unpack(flat2d, shapes):
    flat = flat2d.reshape(-1)
    out, off = [], 0
    for s in shapes:
        n = 1
        for d in s:
            n *= d
        out.append(flat[off:off + n].reshape(s))
        off += n
    return out


def _mesh_pos():
    return lax.axis_index("x"), lax.axis_index("y"), lax.axis_index("c")


def _other_chips(x, y):
    return [(1 - x, y), (x, 1 - y), (1 - x, 1 - y)]


def _half_rows(n):
    return lambda r, c: r.at[0, pl.ds(c * (n // 2), n // 2), :]


GATHER_BIG = {
    "ev_w_in": ((1024, 3072), _half_rows(1024), lambda o, k, c: o.at[pl.ds(c * 512, 512), pl.ds(k * 768, 768)]),
    "ev_w_out": ((1024, 1024), _half_rows(256), lambda o, k, c: o.at[pl.ds(k * 256 + c * 128, 128), :]),
    "od_w_in": ((1024, 2048), _half_rows(1024), lambda o, k, c: o.at[pl.ds(c * 512, 512), pl.ds(k * 512, 512)]),
    "od_w_out": ((1024, 1024), _half_rows(256), lambda o, k, c: o.at[pl.ds(k * 256 + c * 128, 128), :]),
    "ffn_w_up": ((2, 1024, 2 * D_FF), lambda r, c: r.at[c], lambda o, k, c: o.at[c, :, pl.ds(k * (2 * D_FF // 4), 2 * D_FF // 4)]),
    "ffn_w_down": ((2, D_FF, 1024), lambda r, c: r.at[c], lambda o, k, c: o.at[c, pl.ds(k * (D_FF // 4), D_FF // 4), :]),
}


def _gather_weights(names, big, small):
    nb = len(big)
    descs = [GATHER_BIG[n] for n in names]
    rs = small.shape[0]

    def body(*refs):
        ins, s_ref = refs[:nb], refs[nb]
        outs, os_ref = refs[nb + 1:2 * nb + 1], refs[2 * nb + 1]
        ici_send, ici_recv, d2d_send, d2d_recv, loc_sems = refs[2 * nb + 2:2 * nb + 7]
        vbufs = refs[2 * nb + 7:]
        x, y, c = _mesh_pos()
        k = 2 * x + y
        chips = _other_chips(x, y)
        sib = (x, y, 1 - c)

        def remote(src, dst, ssem, rsem, to):
            return pltpu.make_async_remote_copy(src_ref=src, dst_ref=dst, send_sem=ssem, recv_sem=rsem, device_id=to,
                                                device_id_type=MESH)

        stage = [pltpu.make_async_copy(ins[t], vbufs[t], loc_sems.at[2 * t]) for t in range(nb)]
        stage.append(pltpu.make_async_copy(s_ref, vbufs[nb], loc_sems.at[2 * nb]))
        for cp in stage:
            cp.start()
        sends = []
        for t, (_, src, dst) in enumerate(descs):
            for j, (px, py) in enumerate(chips):
                sends.append(remote(src(ins[t], c), dst(outs[t], k, c), ici_send.at[3 * t + j], ici_recv.at[3 * t + j], (px, py, c)))
        for j, (px, py) in enumerate(chips):
            sends.append(remote(s_ref, os_ref.at[k], ici_send.at[3 * nb + j], ici_recv.at[3 * nb + j], (px, py, c)))
        for cp in sends:
            cp.start()
        for cp in stage:
            cp.wait()
        local = []
        for t, (_, src, dst) in enumerate(descs):
            for cc in (0, 1):
                local.append(pltpu.make_async_copy(src(vbufs[t], cc), dst(outs[t], k, cc), loc_sems.at[2 * t + cc]))
        local.append(pltpu.make_async_copy(vbufs[nb], os_ref.at[k], loc_sems.at[2 * nb]))
        for cp in local:
            cp.start()
        for t, (_, src, dst) in enumerate(descs):
            for j, (px, py) in enumerate(chips):
                got = dst(outs[t], 2 * px + py, c)
                remote(got, got, ici_send.at[3 * t + j], ici_recv.at[3 * t + j], (px, py, c)).wait_recv()
                fwd = remote(got, got, d2d_send.at[3 * t + j], d2d_recv.at[3 * t + j], sib)
                fwd.start()
                sends.append(fwd)
        for j, (px, py) in enumerate(chips):
            remote(s_ref, os_ref.at[2 * px + py], ici_send.at[3 * nb + j], ici_recv.at[3 * nb + j], (px, py, c)).wait_recv()
        for t, (_, src, dst) in enumerate(descs):
            for j, (px, py) in enumerate(chips):
                theirs = dst(outs[t], 2 * px + py, 1 - c)
                remote(theirs, theirs, d2d_send.at[3 * t + j], d2d_recv.at[3 * t + j], sib).wait_recv()
        for cp in sends:
            cp.wait_send()
        for cp in local:
            cp.wait()

    out_shape = [jax.ShapeDtypeStruct(d[0], BF16) for d in descs] + [jax.ShapeDtypeStruct((N_CHIPS, rs, LANES), small.dtype)]
    return pl.pallas_call(
        body, name="gather_weights", in_specs=[ANY] * (nb + 1), out_specs=[ANY] * (nb + 1), out_shape=out_shape,
        scratch_shapes=[pltpu.SemaphoreType.DMA((3 * nb + 3,)), pltpu.SemaphoreType.DMA((3 * nb + 3,)),
                        pltpu.SemaphoreType.DMA((3 * nb,)), pltpu.SemaphoreType.DMA((3 * nb,)),
                        pltpu.SemaphoreType.DMA((2 * nb + 1,))]
        + [pltpu.VMEM(b.shape, b.dtype) for b in big] + [pltpu.VMEM(small.shape, small.dtype)],
        compiler_params=pltpu.CompilerParams(vmem_limit_bytes=VMEM_LIMIT),
    )(*big, small)


def _place_own(names, big):
    nb = len(big)
    descs = [GATHER_BIG[n] for n in names]

    def body(*refs):
        ins, outs = refs[:nb], refs[nb:2 * nb]
        sems, vbufs = refs[2 * nb], refs[2 * nb + 1:]
        x, y, c = _mesh_pos()
        k = 2 * x + y
        stage = [pltpu.make_async_copy(ins[t], vbufs[t], sems.at[2 * t]) for t in range(nb)]
        for cp in stage:
            cp.start()
        for cp in stage:
            cp.wait()
        local = [pltpu.make_async_copy(src(vbufs[t], cc), dst(outs[t], k, cc), sems.at[2 * t + cc])
                 for t, (_, src, dst) in enumerate(descs) for cc in (0, 1)]
        for cp in local:
            cp.start()
        for cp in local:
            cp.wait()

    return pl.pallas_call(
        body, name="place_own", in_specs=[ANY] * nb, out_specs=[ANY] * nb,
        out_shape=[jax.ShapeDtypeStruct(d[0], BF16) for d in descs],
        scratch_shapes=[pltpu.SemaphoreType.DMA((2 * nb,))] + [pltpu.VMEM(b.shape, b.dtype) for b in big],
        compiler_params=pltpu.CompilerParams(vmem_limit_bytes=VMEM_LIMIT),
    )(*big)


def _gather_start(names, big, bufs):
    nb = len(big)
    descs = [GATHER_BIG[n] for n in names]

    def body(*refs):
        ins, lnd = refs[:nb], refs[nb:2 * nb]
        send_sems, recv_sems, token = refs[2 * nb], refs[2 * nb + 1], refs[-1]
        x, y, c = _mesh_pos()
        k = 2 * x + y
        for t, (_, src, dst) in enumerate(descs):
            for j, (px, py) in enumerate(_other_chips(x, y)):
                _remote(src(ins[t], c), dst(lnd[t], k, c), send_sems.at[3 * t + j], recv_sems.at[3 * t + j], (px, py, c)).start()
        token[...] = jnp.zeros_like(token)

    out = pl.pallas_call(
        body, name="gather_start",
        out_shape=(pltpu.SemaphoreType.DMA((3 * nb,)), pltpu.SemaphoreType.DMA((3 * nb,)),
                   *[pltpu.HBM(b.shape, b.dtype) for b in big], *[pltpu.HBM(b.shape, b.dtype) for b in bufs],
                   jax.ShapeDtypeStruct((SUBLANES, LANES), F32)),
        in_specs=[HBM] * (2 * nb), out_specs=(SEM, SEM, *[HBM] * (2 * nb), pl.BlockSpec(memory_space=pltpu.VMEM)),
        input_output_aliases={i: 2 + i for i in range(2 * nb)},
        compiler_params=pltpu.CompilerParams(has_side_effects=DATAFLOW),
    )(*[pltpu.with_memory_space_constraint(b, pltpu.HBM) for b in big], *[pltpu.with_memory_space_constraint(b, pltpu.HBM) for b in bufs])
    return out[0], out[1], list(out[2:2 + nb]), list(out[2 + nb:2 + 2 * nb]), out[-1]


def _gather_wait(names, send_sems, recv_sems, big, bufs, after):
    nb = len(big)
    descs = [GATHER_BIG[n] for n in names]

    def body(*refs):
        ins, lnd = refs[:nb], refs[nb:2 * nb]
        ssem, rsem = refs[2 * nb], refs[2 * nb + 1]
        x, y, c = _mesh_pos()
        for t, (_, src, dst) in enumerate(descs):
            for j, (px, py) in enumerate(_other_chips(x, y)):
                cp = _remote(src(ins[t], c), dst(lnd[t], 2 * px + py, c), ssem.at[3 * t + j], rsem.at[3 * t + j], (px, py, c))
                cp.wait_send()
                cp.wait_recv()

    out = pl.pallas_call(
        body, name="gather_wait",
        out_shape=(*[pltpu.HBM(b.shape, b.dtype) for b in big], *[pltpu.HBM(b.shape, b.dtype) for b in bufs]),
        in_specs=[HBM] * (2 * nb) + [SEM, SEM, ANY], out_specs=tuple([HBM] * (2 * nb)),
        input_output_aliases={i: i for i in range(2 * nb)},
        compiler_params=pltpu.CompilerParams(has_side_effects=DATAFLOW),
    )(*big, *bufs, send_sems, recv_sems, after)
    return list(out[nb:])


def _gather_forward(names, bufs):
    nb = len(bufs)
    descs = [GATHER_BIG[n] for n in names]

    def body(*refs):
        outs = refs[nb:2 * nb]
        send_sems, recv_sems = refs[2 * nb:]
        x, y, c = _mesh_pos()
        sib = (x, y, 1 - c)
        sends = []
        for t, (_, src, dst) in enumerate(descs):
            for j, (px, py) in enumerate(_other_chips(x, y)):
                got = dst(outs[t], 2 * px + py, c)
                sends.append(_remote(got, got, send_sems.at[3 * t + j], recv_sems.at[3 * t + j], sib))
        for cp in sends:
            cp.start()
        for t, (_, src, dst) in enumerate(descs):
            for j, (px, py) in enumerate(_other_chips(x, y)):
                theirs = dst(outs[t], 2 * px + py, 1 - c)
                _remote(theirs, theirs, send_sems.at[3 * t + j], recv_sems.at[3 * t + j], sib).wait_recv()
        for cp in sends:
            cp.wait_send()

    return pl.pallas_call(
        body, name="gather_forward", in_specs=[ANY] * nb, out_specs=[ANY] * nb,
        out_shape=[jax.ShapeDtypeStruct(b.shape, b.dtype) for b in bufs], input_output_aliases={t: t for t in range(nb)},
        scratch_shapes=[pltpu.SemaphoreType.DMA((3 * nb,)), pltpu.SemaphoreType.DMA((3 * nb,))],
    )(*bufs)


def _remote(src, dst, ssem, rsem, to):
    return pltpu.make_async_remote_copy(src_ref=src, dst_ref=dst, send_sem=ssem, recv_sem=rsem, device_id=to, device_id_type=MESH)


def _rs_send_sibling(gs, tag):
    n = len(gs)
    ns = sum(N_CHIPS if g.ndim == 4 else 1 for g in gs)

    def body(*refs):
        cps = _sibling_copies(gs, refs[:n], refs[n:2 * n], refs[2 * n], refs[2 * n + 1])
        for cp in cps:
            cp.start()
        for cp in cps:
            cp.wait()

    out_shape = [jax.ShapeDtypeStruct(g.shape[:1] + g.shape[2:] if g.ndim == 4 else g.shape[1:], g.dtype) for g in gs]
    return pl.pallas_call(
        body, name=f"rs_send_sibling_{tag}", in_specs=[ANY] * n, out_specs=[ANY] * n, out_shape=out_shape,
        scratch_shapes=[pltpu.SemaphoreType.DMA((ns,)), pltpu.SemaphoreType.DMA((ns,))],
    )(*gs)


def _add_piece(g, recv, c, name):
    P, Q = g.shape[-2:]

    def body(c_ref, g_ref, r_ref, o_ref):
        o_ref[...] = g_ref[...].reshape(o_ref.shape) + r_ref[...]

    if g.ndim == 4:
        grid = (N_CHIPS,)
        in_specs = [pl.BlockSpec((1, 1, P, Q), lambda k, c_ref: (k, c_ref[0], 0, 0)), pl.BlockSpec((1, P, Q), lambda k, c_ref: (k, 0, 0))]
        out_spec = pl.BlockSpec((1, P, Q), lambda k, c_ref: (k, 0, 0))
    else:
        grid = (1,)
        in_specs = [pl.BlockSpec((1, P, Q), lambda k, c_ref: (c_ref[0], 0, 0)), pl.BlockSpec((P, Q), lambda k, c_ref: (0, 0))]
        out_spec = pl.BlockSpec((P, Q), lambda k, c_ref: (0, 0))
    return pl.pallas_call(
        body, name=name,
        grid_spec=pltpu.PrefetchScalarGridSpec(num_scalar_prefetch=1, grid=grid, in_specs=in_specs, out_specs=out_spec),
        out_shape=jax.ShapeDtypeStruct(recv.shape, g.dtype),
        compiler_params=_cp(1),
    )(c, g, recv)


def _sibling_copies(gs, srcs, dsts, send_sems, recv_sems):
    x, y, c = _mesh_pos()
    cps, s = [], 0
    for t, g in enumerate(gs):
        if g.ndim == 4:
            for k in range(N_CHIPS):
                cps.append(_remote(srcs[t].at[k, 1 - c], dsts[t].at[k], send_sems.at[s], recv_sems.at[s], (x, y, 1 - c)))
                s += 1
        else:
            cps.append(_remote(srcs[t].at[1 - c], dsts[t], send_sems.at[s], recv_sems.at[s], (x, y, 1 - c)))
            s += 1
    return cps


def _sibling_start(gs, tag):
    n = len(gs)
    ns = sum(N_CHIPS if g.ndim == 4 else 1 for g in gs)
    lands = [pltpu.with_memory_space_constraint(lax.empty(g.shape[:1] + g.shape[2:] if g.ndim == 4 else g.shape[1:], g.dtype), pltpu.HBM)
             for g in gs]

    def body(*refs):
        for cp in _sibling_copies(gs, refs[:n], refs[n:2 * n], refs[2 * n], refs[2 * n + 1]):
            cp.start()
        refs[-1][...] = jnp.zeros_like(refs[-1])

    out = pl.pallas_call(
        body, name=f"sibling_start_{tag}",
        out_shape=(pltpu.SemaphoreType.DMA((ns,)), pltpu.SemaphoreType.DMA((ns,)),
                   *[pltpu.HBM(g.shape, g.dtype) for g in gs], *[pltpu.HBM(l.shape, l.dtype) for l in lands],
                   jax.ShapeDtypeStruct((SUBLANES, LANES), F32)),
        in_specs=[HBM] * (2 * n), out_specs=(SEM, SEM, *[HBM] * (2 * n), pl.BlockSpec(memory_space=pltpu.VMEM)),
        input_output_aliases={i: 2 + i for i in range(2 * n)},
        compiler_params=pltpu.CompilerParams(has_side_effects=DATAFLOW),
    )(*[pltpu.with_memory_space_constraint(g, pltpu.HBM) for g in gs], *lands)
    return (out[0], out[1], list(out[2:2 + n]), list(out[2 + n:2 + 2 * n])), out[-1]


def _sibling_wait(send_sems, recv_sems, gs, lands, after, tag):
    n = len(gs)

    def body(*refs):
        for cp in _sibling_copies(gs, refs[:n], refs[n:2 * n], refs[2 * n], refs[2 * n + 1]):
            cp.wait_send()
            cp.wait_recv()

    out = pl.pallas_call(
        body, name=f"sibling_wait_{tag}",
        out_shape=(*[pltpu.HBM(g.shape, g.dtype) for g in gs], *[pltpu.HBM(l.shape, l.dtype) for l in lands]),
        in_specs=[HBM] * (2 * n) + [SEM, SEM, ANY], out_specs=tuple([HBM] * (2 * n)),
        input_output_aliases={i: i for i in range(2 * n)},
        compiler_params=pltpu.CompilerParams(has_side_effects=DATAFLOW),
    )(*gs, *lands, send_sems, recv_sems, after)
    return list(out[:n]), list(out[n:])


def _chips_start(hs, tag):
    n = len(hs)
    lands = [pltpu.with_memory_space_constraint(lax.empty((N_CHIPS,) + h.shape[-2:], h.dtype), pltpu.HBM) for h in hs]

    def body(*refs):
        ins, lnd = refs[:n], refs[n:2 * n]
        send_sems, recv_sems, token = refs[2 * n], refs[2 * n + 1], refs[-1]
        x, y, c = _mesh_pos()
        k = 2 * x + y
        piece = lambda t, kk: ins[t].at[kk] if hs[t].ndim == 3 else ins[t]
        for t in range(n):
            for j, (px, py) in enumerate(_other_chips(x, y)):
                _remote(piece(t, 2 * px + py), lnd[t].at[k], send_sems.at[3 * t + j], recv_sems.at[3 * t + j], (px, py, c)).start()
        token[...] = jnp.zeros_like(token)

    out = pl.pallas_call(
        body, name=f"chips_start_{tag}",
        out_shape=(pltpu.SemaphoreType.DMA((3 * n,)), pltpu.SemaphoreType.DMA((3 * n,)),
                   *[pltpu.HBM(h.shape, h.dtype) for h in hs], *[pltpu.HBM(l.shape, l.dtype) for l in lands],
                   jax.ShapeDtypeStruct((SUBLANES, LANES), F32)),
        in_specs=[HBM] * (2 * n), out_specs=(SEM, SEM, *[HBM] * (2 * n), pl.BlockSpec(memory_space=pltpu.VMEM)),
        input_output_aliases={i: 2 + i for i in range(2 * n)},
        compiler_params=pltpu.CompilerParams(has_side_effects=DATAFLOW),
    )(*[pltpu.with_memory_space_constraint(h, pltpu.HBM) for h in hs], *lands)
    return out[0], out[1], list(out[2:2 + n]), list(out[2 + n:2 + 2 * n]), out[-1]


def _chips_wait(send_sems, recv_sems, hs, lands, after, tag):
    n = len(hs)

    def body(*refs):
        ins, lnd = refs[:n], refs[n:2 * n]
        ssem, rsem = refs[2 * n], refs[2 * n + 1]
        x, y, c = _mesh_pos()
        k = 2 * x + y
        piece = lambda t, kk: ins[t].at[kk] if hs[t].ndim == 3 else ins[t]
        for t in range(n):
            for j, (px, py) in enumerate(_other_chips(x, y)):
                cp = _remote(piece(t, k), lnd[t].at[2 * px + py], ssem.at[3 * t + j], rsem.at[3 * t + j], (px, py, c))
                cp.wait_send()
                cp.wait_recv()

    out = pl.pallas_call(
        body, name=f"chips_wait_{tag}",
        out_shape=(*[pltpu.HBM(h.shape, h.dtype) for h in hs], *[pltpu.HBM(l.shape, l.dtype) for l in lands]),
        in_specs=[HBM] * (2 * n) + [SEM, SEM, ANY], out_specs=tuple([HBM] * (2 * n)),
        input_output_aliases={i: i for i in range(2 * n)},
        compiler_params=pltpu.CompilerParams(has_side_effects=DATAFLOW),
    )(*hs, *lands, send_sems, recv_sems, after)
    return list(out[:n]), list(out[n:])


def _add_chips(p, own, kc, name):
    _, P, Q = p.shape
    tr = P
    while tr * Q * 4 > ELEMENTWISE_TILE_BYTES and tr % 16 == 0:
        tr //= 2
    sharded = own.ndim == 3

    def body(kc_ref, p_ref, own_ref, o_ref):
        k = kc_ref[0]
        mine = own_ref[...].reshape(tr, Q)
        v = [jnp.where(k == j, mine, p_ref[j]) for j in range(N_CHIPS)]
        o_ref[0] = ((v[0] + v[1]) + v[2]) + v[3]

    own_spec = (pl.BlockSpec((1, tr, Q), lambda i, kc_ref: (kc_ref[0], i, 0)) if sharded
                else pl.BlockSpec((tr, Q), lambda i, kc_ref: (i, 0)))
    return pl.pallas_call(
        body, name=name,
        grid_spec=pltpu.PrefetchScalarGridSpec(
            num_scalar_prefetch=1, grid=(P // tr,),
            in_specs=[pl.BlockSpec((N_CHIPS, tr, Q), lambda i, kc_ref: (0, i, 0)), own_spec],
            out_specs=pl.BlockSpec((1, tr, Q), lambda i, kc_ref: (kc_ref[1], i, 0))),
        out_shape=jax.ShapeDtypeStruct((2, P, Q), p.dtype),
        compiler_params=_cp(1),
    )(kc, p, own)


def _rs_share(fs, tag):
    n = len(fs)

    def body(*refs):
        outs = refs[n:2 * n]
        send_sems, recv_sems = refs[2 * n:]
        x, y, c = _mesh_pos()
        sends = [_remote(outs[t].at[c], outs[t].at[c], send_sems.at[t], recv_sems.at[t], (x, y, 1 - c)) for t in range(n)]
        for cp in sends:
            cp.start()
        for t in range(n):
            _remote(outs[t].at[c], outs[t].at[1 - c], send_sems.at[t], recv_sems.at[t], (x, y, 1 - c)).wait_recv()
        for cp in sends:
            cp.wait_send()

    return pl.pallas_call(
        body, name=f"rs_share_{tag}", in_specs=[ANY] * n, out_specs=[ANY] * n,
        out_shape=[jax.ShapeDtypeStruct(f.shape, f.dtype) for f in fs], input_output_aliases={t: t for t in range(n)},
        scratch_shapes=[pltpu.SemaphoreType.DMA((n,)), pltpu.SemaphoreType.DMA((n,))],
    )(*fs)


def _reduce_start(gs, kc, tag):
    return _reduce_continue(gs, _rs_send_sibling(gs, tag), kc, tag)


def _reduce_continue(gs, from_sibling, kc, tag):
    chip_sums = [_add_piece(g, r, kc[1:], name=f"add_piece_{tag}_{t}") for t, (g, r) in enumerate(zip(gs, from_sibling))]
    send_sems, recv_sems, chip_sums, lands, token = _chips_start(chip_sums, tag)
    return (send_sems, recv_sems, chip_sums, lands, tag), token


def _reduce_finish(states, kc, after):
    mine = []
    for send_sems, recv_sems, chip_sums, lands, tag in states:
        chip_sums, from_chips = _chips_wait(send_sems, recv_sems, chip_sums, lands, after, tag)
        mine += [_add_chips(p, h, kc, name=f"add_chips_{tag}_{t}") for t, (p, h) in enumerate(zip(from_chips, chip_sums))]
    return _rs_share(mine, "all")


def _adamw(w, g, m, v, name):
    R, C = w.shape
    tr = R
    for cand in (512, 256, 128, 64, 32, 16, 8):
        if R % cand == 0 and cand * C * 4 <= ELEMENTWISE_TILE_BYTES:
            tr = cand
            break
    c1 = 1.0 / (1.0 - ADAM_B1 ** ADAM_STEP)
    c2 = 1.0 / (1.0 - ADAM_B2 ** ADAM_STEP)

    def body(w_ref, g_ref, m_ref, v_ref, d_ref, mo_ref, vo_ref):
        gv = g_ref[...]
        mn = ADAM_B1 * m_ref[...] + (1.0 - ADAM_B1) * gv
        vn = ADAM_B2 * v_ref[...] + (1.0 - ADAM_B2) * (gv * gv)
        mo_ref[...] = mn
        vo_ref[...] = vn
        d_ref[...] = -ADAM_LR * ((mn * c1) / (jnp.sqrt(vn * c2) + ADAM_EPS) + ADAM_WD * w_ref[...])

    spec = pl.BlockSpec((tr, C), lambda i: (i, 0))
    shp = jax.ShapeDtypeStruct((R, C), F32)
    return pl.pallas_call(body, name=name, grid=(R // tr,), in_specs=[spec] * 4, out_specs=[spec] * 3, out_shape=[shp] * 3,
                          compiler_params=_cp(1))(w, g, m, v)


def _adamw_many(ws, gs, ms, vs):
    n = len(ws)
    c1 = 1.0 / (1.0 - ADAM_B1 ** ADAM_STEP)
    c2 = 1.0 / (1.0 - ADAM_B2 ** ADAM_STEP)

    def body(*refs):
        w_refs, g_refs, m_refs, v_refs = (refs[k * n:(k + 1) * n] for k in range(4))
        outs = refs[4 * n:]
        for i in range(n):
            gv = g_refs[i][...]
            mn = ADAM_B1 * m_refs[i][...] + (1.0 - ADAM_B1) * gv
            vn = ADAM_B2 * v_refs[i][...] + (1.0 - ADAM_B2) * (gv * gv)
            outs[3 * i][...] = -ADAM_LR * ((mn * c1) / (jnp.sqrt(vn * c2) + ADAM_EPS) + ADAM_WD * w_refs[i][...])
            outs[3 * i + 1][...] = mn
            outs[3 * i + 2][...] = vn

    out_shape = [jax.ShapeDtypeStruct(w.shape, F32) for w in ws for _ in range(3)]
    return pl.pallas_call(body, name="adamw_small", out_shape=out_shape)(*ws, *gs, *ms, *vs)


def _step(a):
    x, y, c = _mesh_pos()
    kc = jnp.stack([2 * x + y, c]).astype(jnp.int32)

    rs = _rows(sum(a[n].size for n in SH_SMALL))
    first, later = ["ev_w_in", "ev_w_out"], ["od_w_in", "od_w_out", "ffn_w_up", "ffn_w_down"]
    lead = lambda w: w if w.ndim == 3 else w[None]
    *full, gs = _gather_weights(first, [a[n].astype(BF16) for n in first], _pack([a[n] for n in SH_SMALL], rs, F32))
    p = {n: a[n] for n in REP}
    p.update({n: lead(w) for n, w in zip(first, full)})
    parts = [_unpack(gs[k], [a[n].shape for n in SH_SMALL]) for k in range(N_CHIPS)]
    for i, n in enumerate(SH_SMALL):
        p[n] = jnp.concatenate([parts[k][i] for k in range(N_CHIPS)], axis=SH_SMALL[n])
    shards = lax.optimization_barrier(([a[n].astype(BF16) for n in later], full))[0]
    g_send, g_recv, shards, bufs, token = _gather_start(later, shards, _place_own(later, shards))
    p["norm_mix"] = p["norm_mix"] + token[0:1, 0:1]

    def late(after):
        got = _gather_forward(later, _gather_wait(later, g_send, g_recv, shards, bufs, after))
        return {n: lead(w) for n, w in zip(later, got)}

    p["late"] = late

    states, pending = {}, {}

    def start_reduce(tag, gs, split=False):
        if split:
            pending[tag], token = _sibling_start(gs, tag)
        else:
            states[tag], token = _reduce_start(gs, kc, tag)
        return token

    def continue_reduce(tag, after):
        send_sems, recv_sems, gs, lands = pending.pop(tag)
        gs, from_sibling = _sibling_wait(send_sems, recv_sems, gs, lands, after, tag)
        states[tag], token = _reduce_continue(gs, from_sibling, kc, tag)
        return token

    sq8, grad_x, grads, big = _local_step(a["x"][0], a["loss_target"][0], p, start_reduce, continue_reduce)
    loss = lax.psum(0.5 / D_MODEL * jnp.sum(sq8), ("x", "y", "c"))

    r_s = _rows(sum(a[n].size for n in SH_SMALL), 2 * SUBLANES) // 2
    small_pieces = []
    for k in range(N_CHIPS):
        pieces = [lax.slice_in_dim(grads[n], k * a[n].shape[ax], (k + 1) * a[n].shape[ax], axis=ax) for n, ax in SH_SMALL.items()]
        small_pieces.append(_pack(pieces, 2 * r_s, F32).reshape(2, r_s, LANES))
    g_small = jnp.stack(small_pieces)
    r_r = _rows(sum(a[n].size for n in REP), 2 * SUBLANES) // 2
    g_rep = _pack([grads[n] for n in REP], 2 * r_r, F32).reshape(2, r_r, LANES)
    token = start_reduce("g4", [g_small, g_rep])
    reduced = _reduce_finish([states[tag] for tag in ("g1", "g2", "g3", "g4")], kc, token)
    red = dict(zip([key for tag in ("g1", "g2", "g3") for key in REDUCE_GROUPS[tag]], reduced))
    gfin = {}
    for n in ("ev_w_in", "ev_w_out", "od_w_in", "od_w_out"):
        gfin[n] = red[n, 0].reshape(a[n].shape)
    for n in ("ffn_w_up", "ffn_w_down"):
        gfin[n] = jnp.stack([red[n, l].reshape(a[n].shape[1:]) for l in range(2)])
    gfin.update(zip(SH_SMALL, _unpack(reduced[-2], [a[n].shape for n in SH_SMALL])))
    gfin.update(zip(REP, _unpack(reduced[-1], [a[n].shape for n in REP])))

    out = {"loss": loss, "grad_x": grad_x[None]}
    small_names = list(SH_SMALL) + REP
    for n in SH_BIG:
        shp = a[n].shape
        two_d = lambda t: t.reshape(-1, shp[-1])
        d, mo, vo = _adamw(two_d(a[n]), two_d(gfin[n]), two_d(a["m_" + n]), two_d(a["v_" + n]), name=f"adamw_{n}")
        out["delta_" + n], out["new_m_" + n], out["new_v_" + n] = d.reshape(shp), mo.reshape(shp), vo.reshape(shp)
    two_d = lambda t: t.reshape(-1, t.shape[-1])
    res = _adamw_many(*[[two_d(src(n)) for n in small_names]
                        for src in (lambda n: a[n], lambda n: gfin[n], lambda n: a["m_" + n], lambda n: a["v_" + n])])
    for i, n in enumerate(small_names):
        out["delta_" + n], out["new_m_" + n], out["new_v_" + n] = (r.reshape(a[n].shape) for r in res[3 * i:3 * i + 3])
    for n in WEIGHTS:
        out["grad_" + n] = gfin[n]
    return out


def kernel(x, norm_mix, norm_ffn, norm_final, ev_w_in, ev_conv_w, ev_conv_b, ev_gate_a_w, ev_gate_a_b, ev_gate_x_w, ev_gate_x_b, ev_lru_lambda, hg_lb_logits, ev_hg_norm, ev_w_out, od_w_in, od_b_in, od_ln_g, od_ln_b, od_w_s, od_b_s, od_w_out, ffn_w_up, ffn_conv_w, ffn_conv_b, ffn_w_down, loss_target, m_norm_mix, m_norm_ffn, m_norm_final, m_ev_w_in, m_ev_conv_w, m_ev_conv_b, m_ev_gate_a_w, m_ev_gate_a_b, m_ev_gate_x_w, m_ev_gate_x_b, m_ev_lru_lambda, m_hg_lb_logits, m_ev_hg_norm, m_ev_w_out, m_od_w_in, m_od_b_in, m_od_ln_g, m_od_ln_b, m_od_w_s, m_od_b_s, m_od_w_out, m_ffn_w_up, m_ffn_conv_w, m_ffn_conv_b, m_ffn_w_down, v_norm_mix, v_norm_ffn, v_norm_final, v_ev_w_in, v_ev_conv_w, v_ev_conv_b, v_ev_gate_a_w, v_ev_gate_a_b, v_ev_gate_x_w, v_ev_gate_x_b, v_ev_lru_lambda, v_hg_lb_logits, v_ev_hg_norm, v_ev_w_out, v_od_w_in, v_od_b_in, v_od_ln_g, v_od_ln_b, v_od_w_s, v_od_b_s, v_od_w_out, v_ffn_w_up, v_ffn_conv_w, v_ffn_conv_b, v_ffn_w_down):
    vals = (x, norm_mix, norm_ffn, norm_final, ev_w_in, ev_conv_w, ev_conv_b, ev_gate_a_w, ev_gate_a_b, ev_gate_x_w, ev_gate_x_b, ev_lru_lambda, hg_lb_logits, ev_hg_norm, ev_w_out, od_w_in, od_b_in, od_ln_g, od_ln_b, od_w_s, od_b_s, od_w_out, ffn_w_up, ffn_conv_w, ffn_conv_b, ffn_w_down, loss_target, m_norm_mix, m_norm_ffn, m_norm_final, m_ev_w_in, m_ev_conv_w, m_ev_conv_b, m_ev_gate_a_w, m_ev_gate_a_b, m_ev_gate_x_w, m_ev_gate_x_b, m_ev_lru_lambda, m_hg_lb_logits, m_ev_hg_norm, m_ev_w_out, m_od_w_in, m_od_b_in, m_od_ln_g, m_od_ln_b, m_od_w_s, m_od_b_s, m_od_w_out, m_ffn_w_up, m_ffn_conv_w, m_ffn_conv_b, m_ffn_w_down, v_norm_mix, v_norm_ffn, v_norm_final, v_ev_w_in, v_ev_conv_w, v_ev_conv_b, v_ev_gate_a_w, v_ev_gate_a_b, v_ev_gate_x_w, v_ev_gate_x_b, v_ev_lru_lambda, v_hg_lb_logits, v_ev_hg_norm, v_ev_w_out, v_od_w_in, v_od_b_in, v_od_ln_g, v_od_ln_b, v_od_w_s, v_od_b_s, v_od_w_out, v_ffn_w_up, v_ffn_conv_w, v_ffn_conv_b, v_ffn_w_down)
    names = ["x"] + WEIGHTS + ["loss_target"] + ["m_" + n for n in WEIGHTS] + ["v_" + n for n in WEIGHTS]
    out = _step(dict(zip(names, vals)))
    return (out["loss"], out["grad_x"], *[out["grad_" + n] for n in WEIGHTS], *[out["delta_" + n] for n in WEIGHTS],
            *[out["new_m_" + n] for n in WEIGHTS], *[out["new_v_" + n] for n in WEIGHTS])
```

```python
import jax
import jax.numpy as jnp
from jax import lax
from jax.experimental import pallas as pl
from jax.experimental.pallas import tpu as pltpu

F32 = jnp.float32
BF16 = jnp.bfloat16

EPS = 1e-6
D_MODEL = 1024
LRU_W = 512
LRU_BLOCKS = 8
LRU_C = 8.0
HG_HEADS = 4
HG_D = 128
HG_CHUNK = 64
SGU_G = 8
SGU_CHUNK = 128
D_FF = 2816
ADAM_LR, ADAM_B1, ADAM_B2, ADAM_EPS, ADAM_WD, ADAM_STEP = 0.001, 0.9, 0.999, 1e-08, 0.01, 10

V7X_VMEM_BYTES = 64 * 1024 * 1024
VMEM_LIMIT = V7X_VMEM_BYTES - 8 * 1024 * 1024
SUBLANES = 8
LANES = 128
BF16_ROWS = 16
ELEMENTWISE_TILE_BYTES = 2 * 1024 * 1024
ADD_TILE_BYTES = 512 * 1024

N_CHIPS = 4
MESH = pl.DeviceIdType.MESH
ANY = pl.BlockSpec(memory_space=pl.ANY)
HBM = pl.BlockSpec(memory_space=pltpu.HBM)
SEM = pl.BlockSpec(memory_space=pltpu.SEMAPHORE)
DATAFLOW = pltpu.SideEffectType.DATAFLOW_SIDE_EFFECTING

GELU_C0 = 0.7978845608028654
GELU_C1 = 0.044715

NN = (((1,), (0,)), ((), ()))
NT = (((1,), (1,)), ((), ()))
TN = (((0,), (0,)), ((), ()))


def _dot(a, b, dims=NN):
    return lax.dot_general(a.astype(BF16), b.astype(BF16), dims, preferred_element_type=F32)


def _cp(n_grid):
    return pltpu.CompilerParams(dimension_semantics=("arbitrary",) * n_grid, vmem_limit_bytes=VMEM_LIMIT)


def _chunk(n, cap):
    best = LANES
    for c in range(LANES, cap + 1, LANES):
        if n % c == 0:
            best = c
    return best


def _resident(shape):
    nd = len(shape)
    return pl.BlockSpec(shape, lambda *_: (0,) * nd, pipeline_mode=pl.Buffered(1))


def _rsum8(x):
    r, c = x.shape
    return x.reshape(r // SUBLANES, SUBLANES, c).sum(axis=0)


def _sigmoid(x):
    return 0.5 * jnp.tanh(0.5 * x) + 0.5


def _gelu(x):
    return 0.5 * x * (1.0 + jnp.tanh(GELU_C0 * (x + GELU_C1 * x * x * x)))


def _gelu_grad(x):
    t = jnp.tanh(GELU_C0 * (x + GELU_C1 * x * x * x))
    return 0.5 * (1.0 + t) + 0.5 * x * (1.0 - t * t) * GELU_C0 * (1.0 + 3.0 * GELU_C1 * x * x)


def _silu_and_grad(x):
    s = _sigmoid(x)
    return x * s, s * (1.0 + x * (1.0 - s))


def _shift_rows(e, j):
    n = e.shape[0]
    return e if j % n == 0 else pltpu.roll(e, j % n, 0)


def _weight(w):
    if isinstance(w, tuple):
        stack, layer = w
        K, N = stack.shape[1:]
        return stack, pl.BlockSpec((None, K, N), lambda *_: (layer, 0, 0), pipeline_mode=pl.Buffered(1)), (K, N)
    return w, _resident(w.shape), w.shape


def _norm_mm(h, g, w, b, name, tt=1024):
    T, D = h.shape
    w, w_spec, (_, N) = _weight(w)
    cn = _chunk(N, 512)

    def body(h_ref, g_ref, w_ref, b_ref, hn_ref, z_ref):
        x = h_ref[...]
        r = lax.rsqrt(jnp.mean(x * x, axis=-1, keepdims=True) + EPS)
        hn = (x * r * g_ref[...]).astype(BF16)
        hn_ref[...] = hn
        for j in range(0, N, cn):
            acc = jnp.dot(hn, w_ref[:, j:j + cn], preferred_element_type=F32) + b_ref[:, j:j + cn]
            z_ref[:, j:j + cn] = acc.astype(BF16)

    return pl.pallas_call(
        body, name=name, grid=(T // tt,),
        in_specs=[pl.BlockSpec((tt, D), lambda i: (i, 0)), _resident((1, D)), w_spec, _resident((1, N))],
        out_specs=[pl.BlockSpec((tt, D), lambda i: (i, 0)), pl.BlockSpec((tt, N), lambda i: (i, 0))],
        out_shape=[jax.ShapeDtypeStruct((T, D), BF16), jax.ShapeDtypeStruct((T, N), BF16)],
        compiler_params=_cp(1),
    )(h, g, w, b)


def _mm(a, w, res, out_dtype, name, tt=1024, transpose_w=False):
    T, K = a.shape
    N = w.shape[0] if transpose_w else w.shape[1]
    cn = _chunk(N, 512)
    has_res = res is not None

    def body(*refs):
        a_ref, w_ref = refs[0], refs[1]
        res_ref = refs[2] if has_res else None
        o_ref = refs[-1]
        av = a_ref[...].astype(BF16)
        for j in range(0, N, cn):
            if transpose_w:
                acc = lax.dot_general(av, w_ref[j:j + cn, :], NT, preferred_element_type=F32)
            else:
                acc = jnp.dot(av, w_ref[:, j:j + cn], preferred_element_type=F32)
            if has_res:
                acc = acc + res_ref[:, j:j + cn]
            o_ref[:, j:j + cn] = acc.astype(out_dtype)

    in_specs = [pl.BlockSpec((tt, K), lambda i: (i, 0)), _resident(w.shape)]
    args = [a, w]
    if has_res:
        in_specs.append(pl.BlockSpec((tt, N), lambda i: (i, 0)))
        args.append(res)
    return pl.pallas_call(
        body, name=name, grid=(T // tt,), in_specs=in_specs,
        out_specs=pl.BlockSpec((tt, N), lambda i: (i, 0)),
        out_shape=jax.ShapeDtypeStruct((T, N), out_dtype),
        compiler_params=_cp(1),
    )(*args)


def _mm_tn(a, b, name, col_shards=1, tt=2048):
    T, K = a.shape
    N = b.shape[1]
    ns = N // col_shards
    tt = min(tt, T)
    while 2 * (tt * K * a.dtype.itemsize + tt * ns * b.dtype.itemsize + K * ns * 4) + K * ns * 4 > VMEM_LIMIT:
        tt //= 2

    def body(a_ref, b_ref, o_ref):
        acc = lax.dot_general(a_ref[...].astype(BF16), b_ref[...].astype(BF16), TN, preferred_element_type=F32)
        prev = jnp.where(pl.program_id(1) == 0, 0.0, o_ref[0])
        o_ref[0] = prev + acc

    out = pl.pallas_call(
        body, name=name, grid=(col_shards, T // tt),
        in_specs=[pl.BlockSpec((tt, K), lambda n, t: (t, 0)), pl.BlockSpec((tt, ns), lambda n, t: (t, n))],
        out_specs=pl.BlockSpec((1, K, ns), lambda n, t: (n, 0, 0)),
        out_shape=jax.ShapeDtypeStruct((col_shards, K, ns), F32),
        compiler_params=_cp(2),
    )(a, b)
    return out if col_shards > 1 else out[0]


def _mm_normbwd(dz, w, x, g, dres, name):
    T, N = dz.shape
    D = w.shape[0]
    tt = 1024 if N <= 3072 else 512

    def body(dz_ref, wt_ref, x_ref, g_ref, dres_ref, dx_ref, dg_ref):
        @pl.when(pl.program_id(0) == 0)
        def _():
            dg_ref[...] = jnp.zeros_like(dg_ref)

        dy = lax.dot_general(dz_ref[...], wt_ref[...], NT, preferred_element_type=F32)
        x = x_ref[...]
        r = lax.rsqrt(jnp.mean(x * x, axis=-1, keepdims=True) + EPS)
        xn = x * r
        dg_ref[...] += _rsum8(dy * xn)
        dxn = dy * g_ref[...]
        dx_ref[...] = dres_ref[...] + r * (dxn - xn * jnp.mean(dxn * xn, axis=-1, keepdims=True))

    return pl.pallas_call(
        body, name=name, grid=(T // tt,),
        in_specs=[pl.BlockSpec((tt, N), lambda i: (i, 0)), _resident((D, N)), pl.BlockSpec((tt, D), lambda i: (i, 0)),
                  _resident((1, D)), pl.BlockSpec((tt, D), lambda i: (i, 0))],
        out_specs=[pl.BlockSpec((tt, D), lambda i: (i, 0)), pl.BlockSpec((SUBLANES, D), lambda i: (0, 0))],
        out_shape=[jax.ShapeDtypeStruct((T, D), F32), jax.ShapeDtypeStruct((SUBLANES, D), F32)],
        compiler_params=_cp(1),
    )(dz, w, x, g, dres)


def _col_groups(F, cc, per_group=4):
    step = cc * per_group
    return [(g0, min(g0 + step, F)) for g0 in range(0, F, step)]


def _loss_head(x, gv, tgt):
    r = lax.rsqrt(jnp.mean(x * x, axis=-1, keepdims=True) + EPS)
    xn = x * r
    diff = xn * gv - tgt
    dy = diff * (1.0 / x.shape[-1])
    dxn = dy * gv
    return r * (dxn - xn * jnp.mean(dxn * xn, axis=-1, keepdims=True)), _rsum8(diff * diff), _rsum8(dy * xn)


def _ffn_act(gu, cw, cb, w_down, res, name, loss=None, tt=512):
    T = gu.shape[0]
    F = gu.shape[1] // 2
    w_down, wd_spec, (_, D) = _weight(w_down)
    cc = _chunk(F, 256)
    hb = tt // BF16_ROWS

    def body(gu_ref, halo_ref, cw_ref, cb_ref, wd_ref, res_ref, *rest):
        if loss is None:
            a_ref, gc_ref, o_ref = rest
        else:
            gf_ref, t_ref, a_ref, gc_ref, o_ref, sq_ref, dgf_ref = rest
        first = pl.program_id(0) == 0
        acc = res_ref[...]
        for g0, g1 in _col_groups(F, cc):
            for c0 in range(g0, g1, cc):
                cs = slice(c0, c0 + cc)
                x = gu_ref[:, cs].astype(F32)
                halo = jnp.where(first, 0.0, halo_ref[:, cs].astype(F32))
                e = jnp.concatenate([halo, x], axis=0)
                gc = (cb_ref[:, cs] + cw_ref[0:1, cs] * _shift_rows(e, 2)[BF16_ROWS:] + cw_ref[1:2, cs] * _shift_rows(e, 1)[BF16_ROWS:]
                      + cw_ref[2:3, cs] * x)
                up = gu_ref[:, F + c0:F + c0 + cc].astype(F32)
                gc_ref[:, cs] = gc.astype(BF16)
                a_ref[:, cs] = (gc * _sigmoid(gc) * up).astype(BF16)
            acc = acc + jnp.dot(a_ref[:, g0:g1], wd_ref[g0:g1, :], preferred_element_type=F32)
        if loss is None:
            o_ref[...] = acc
        else:
            @pl.when(first)
            def _():
                sq_ref[...] = jnp.zeros_like(sq_ref)
                dgf_ref[...] = jnp.zeros_like(dgf_ref)

            dx, sq, dgf = _loss_head(acc, gf_ref[...], t_ref[...])
            o_ref[...] = dx
            sq_ref[...] += sq
            dgf_ref[...] += dgf

    tok = lambda w: pl.BlockSpec((tt, w), lambda i: (i, 0))
    acc8 = pl.BlockSpec((SUBLANES, D), lambda i: (0, 0))
    in_specs = [tok(2 * F), pl.BlockSpec((BF16_ROWS, F), lambda i: (jnp.maximum(i * hb - 1, 0), 0)),
                _resident((SUBLANES, F)), _resident((1, F)), wd_spec, tok(D)]
    out_specs = [tok(F), tok(F), tok(D)]
    out_shape = [jax.ShapeDtypeStruct((T, F), BF16), jax.ShapeDtypeStruct((T, F), BF16), jax.ShapeDtypeStruct((T, D), F32)]
    args = [gu, gu, cw, cb, w_down, res]
    if loss is not None:
        in_specs += [_resident((1, D)), tok(D)]
        out_specs += [acc8, acc8]
        out_shape += [jax.ShapeDtypeStruct((SUBLANES, D), F32)] * 2
        args += list(loss)
    return pl.pallas_call(body, name=name, grid=(T // tt,), in_specs=in_specs, out_specs=out_specs, out_shape=out_shape,
                          compiler_params=_cp(1))(*args)


def _ffn_act_bwd(gu, gc, cw, w_up, w_down, x, g, dh, name, tt=256):
    T = gu.shape[0]
    F = gu.shape[1] // 2
    w_up, wu_spec, (D, _) = _weight(w_up)
    w_down, wd_spec, _ = _weight(w_down)
    cc = _chunk(F, 256)
    hb = tt // BF16_ROWS
    last_hb = T // BF16_ROWS - 1
    nt = T // tt

    def body(gu_ref, upnext_ref, gc_ref, gcnext_ref, cw_ref, wu_ref, wd_ref, x_ref, g_ref, dh_ref, dhnext_ref,
             dgu_ref, dc_ref, dx_ref, dg_ref):
        i = pl.program_id(0)

        @pl.when(i == 0)
        def _():
            dc_ref[...] = jnp.zeros_like(dc_ref)
            dg_ref[...] = jnp.zeros_like(dg_ref)

        n = tt + BF16_ROWS
        ext = lambda main, nxt: jnp.concatenate([main.astype(F32), nxt.astype(F32)], axis=0)
        dhe = ext(dh_ref[...], jnp.where(i == nt - 1, 0.0, dhnext_ref[...])).astype(BF16)
        dy = jnp.zeros((tt, D), F32)
        groups = _col_groups(F, cc)
        da_of = lambda grp: lax.dot_general(dhe, wd_ref[grp[0]:grp[1], :], NT, preferred_element_type=F32)
        da_next = da_of(groups[0])
        for gi, (lo, hi) in enumerate(groups):
            da, da_next = da_next, (da_of(groups[gi + 1]) if gi + 1 < len(groups) else None)
            for c0 in range(lo, hi, cc):
                cs = slice(c0, c0 + cc)
                us = slice(F + c0, F + c0 + cc)
                gc = ext(gc_ref[:, cs], gcnext_ref[:, cs])
                up = ext(gu_ref[:, us], upnext_ref[:, cs])
                dae = da[:, c0 - lo:c0 - lo + cc]
                s, ds = _silu_and_grad(gc)
                dgc = dae * up * ds
                dgu_ref[:, us] = (dae * s)[:tt].astype(BF16)
                dgc1 = _shift_rows(dgc, n - 1)[:tt]
                dgc2 = _shift_rows(dgc, n - 2)[:tt]
                dm = dgc[:tt]
                dgu_ref[:, cs] = (cw_ref[2:3, cs] * dm + cw_ref[1:2, cs] * dgc1 + cw_ref[0:1, cs] * dgc2).astype(BF16)
                gt = gu_ref[:, cs].astype(F32)
                dc_ref[0, :, cs] += _rsum8(dgc2 * gt)
                dc_ref[1, :, cs] += _rsum8(dgc1 * gt)
                dc_ref[2, :, cs] += _rsum8(dm * gt)
                dc_ref[3, :, cs] += _rsum8(dm)
            dy = (dy + lax.dot_general(dgu_ref[:, lo:hi], wu_ref[:, lo:hi], NT, preferred_element_type=F32)
                  + lax.dot_general(dgu_ref[:, F + lo:F + hi], wu_ref[:, F + lo:F + hi], NT, preferred_element_type=F32))
        xv = x_ref[...]
        r = lax.rsqrt(jnp.mean(xv * xv, axis=-1, keepdims=True) + EPS)
        xn = xv * r
        dg_ref[...] += _rsum8(dy * xn)
        dxn = dy * g_ref[...]
        dx_ref[...] = dh_ref[...] + r * (dxn - xn * jnp.mean(dxn * xn, axis=-1, keepdims=True))

    tok = lambda w: pl.BlockSpec((tt, w), lambda i: (i, 0))
    nxt = lambda w, col: pl.BlockSpec((BF16_ROWS, w), lambda i: (jnp.minimum((i + 1) * hb, last_hb), col))
    return pl.pallas_call(
        body, name=name, grid=(nt,),
        in_specs=[tok(2 * F), nxt(F, 1), tok(F), nxt(F, 0),
                  _resident((SUBLANES, F)), wu_spec, wd_spec, tok(D), _resident((1, D)), tok(D), nxt(D, 0)],
        out_specs=[tok(2 * F), pl.BlockSpec((4, SUBLANES, F), lambda i: (0, 0, 0)), tok(D),
                   pl.BlockSpec((SUBLANES, D), lambda i: (0, 0))],
        out_shape=[jax.ShapeDtypeStruct((T, 2 * F), BF16), jax.ShapeDtypeStruct((4, SUBLANES, F), F32),
                   jax.ShapeDtypeStruct((T, D), F32), jax.ShapeDtypeStruct((SUBLANES, D), F32)],
        compiler_params=_cp(1),
    )(gu, gu, gc, gc, cw, w_up, w_down, x, g, dh, dh)


def _softplus_neg(lam):
    x = -lam
    y = jnp.exp(-jnp.abs(x))
    l1p = jnp.where(y < 0.01, y * (1.0 - y * (0.5 - y * (1.0 / 3.0))), jnp.log(1.0 + y))
    return jnp.maximum(x, 0.0) + l1p


def _lru_gates(xc, wa_ref, ba_ref, wx_ref, bx_ref, sp):
    xcb = xc.astype(BF16)
    r = _sigmoid(jnp.dot(xcb, wa_ref[...], preferred_element_type=F32) + ba_ref[...])
    gi = _sigmoid(jnp.dot(xcb, wx_ref[...], preferred_element_type=F32) + bx_ref[...])
    log_a = -LRU_C * r * sp
    a = jnp.exp(log_a)
    x2 = 2.0 * log_a
    series = -x2 * (1.0 + x2 * 0.5 * (1.0 + x2 * (1.0 / 3.0)))
    om = jnp.where(x2 > -0.02, series, 1.0 - a * a)
    return r, gi, a, jnp.sqrt(om)


def _lru_conv(xr, halo, cw_ref, cb_ref):
    e = jnp.concatenate([halo, xr], axis=0)
    x1, x2, x3 = (_shift_rows(e, j)[BF16_ROWS:] for j in (1, 2, 3))
    return cb_ref[...] + cw_ref[0:1, :] * x3 + cw_ref[1:2, :] * x2 + cw_ref[2:3, :] * x1 + cw_ref[3:4, :] * xr


def _lru_fwd(z, cw, cb, wa, ba, wx, bx, lam, name="lru_fwd", tt=512):
    T = z.shape[0]
    W = LRU_W
    hb = tt // BF16_ROWS
    ng = tt // SUBLANES

    def body(z_ref, halo_ref, cw_ref, cb_ref, wa_ref, ba_ref, wx_ref, bx_ref, lam_ref, oa_ref, h_ref, a_s, sv_ref, u_s, hc):
        i = pl.program_id(0)

        @pl.when(i == 0)
        def _():
            hc[...] = jnp.zeros_like(hc)

        xr = z_ref[:, W:2 * W].astype(F32)
        halo = jnp.where(i == 0, 0.0, halo_ref[...].astype(F32))
        xc = _lru_conv(xr, halo, cw_ref, cb_ref)
        sp = _softplus_neg(lam_ref[...])
        r, gi, a, mult = _lru_gates(xc, wa_ref, ba_ref, wx_ref, bx_ref, sp)
        a_s[...] = a
        u_s[...] = mult * gi * xc
        for k, saved in enumerate((mult, r, gi, xc)):
            sv_ref[:, k * W:(k + 1) * W] = saved.astype(BF16)
        row = lax.broadcasted_iota(jnp.int32, (SUBLANES, W), 0)

        def step(j, hprev):
            r0 = pl.multiple_of(j * SUBLANES, SUBLANES)
            A = a_s[pl.ds(r0, SUBLANES), :]
            U = u_s[pl.ds(r0, SUBLANES), :]
            for k in (1, 2, 4):
                m = row >= k
                U = jnp.where(m, A * pltpu.roll(U, k, 0) + U, U)
                A = jnp.where(m, A * pltpu.roll(A, k, 0), A)
            H = U + A * hprev
            h_ref[pl.ds(r0, SUBLANES), :] = H
            return jnp.broadcast_to(H[SUBLANES - 1:SUBLANES, :], (SUBLANES, W))

        hc[...] = lax.fori_loop(0, ng, step, hc[...])
        oa_ref[...] = (_gelu(z_ref[:, 0:W].astype(F32)) * h_ref[...]).astype(BF16)

    return pl.pallas_call(
        body, name=name, grid=(T // tt,),
        in_specs=[pl.BlockSpec((tt, 2 * W), lambda i: (i, 0)),
                  pl.BlockSpec((BF16_ROWS, W), lambda i: (jnp.maximum(i * hb - 1, 0), 1)),
                  _resident((SUBLANES, W)), _resident((1, W)), _resident((W, W)), _resident((1, W)),
                  _resident((W, W)), _resident((1, W)), _resident((1, W))],
        out_specs=[pl.BlockSpec((tt, W), lambda i: (i, 0)), pl.BlockSpec((tt, W), lambda i: (i, 0)),
                   pl.BlockSpec((tt, W), lambda i: (i, 0)), pl.BlockSpec((tt, 4 * W), lambda i: (i, 0))],
        out_shape=[jax.ShapeDtypeStruct((T, 2 * W), BF16), jax.ShapeDtypeStruct((T, W), F32),
                   jax.ShapeDtypeStruct((T, W), F32), jax.ShapeDtypeStruct((T, 4 * W), BF16)],
        scratch_shapes=[pltpu.VMEM((tt, W), F32), pltpu.VMEM((SUBLANES, W), F32)],
        compiler_params=_cp(1),
    )(z, z, cw, cb, wa, ba, wx, bx, lam)


def _lru_bwd(z, hseq, a_all, saved, dmix, cw, wat, wxt, lam, name="lru_bwd", tt=512):
    T = z.shape[0]
    W = LRU_W
    nt = T // tt
    sb = tt // SUBLANES
    ng = tt // SUBLANES

    def body(z_ref, h_ref, hprev_ref, a_ref, sv_ref, dm_ref, cw_ref, wat_ref, wxt_ref, lam_ref,
             dz_ref, dc_ref, dwa_ref, dwx_ref, dv_ref, c_s, d_s, g_s, gc, an, dxn):
        i = pl.program_id(0)
        ti = nt - 1 - i

        @pl.when(i == 0)
        def _():
            dc_ref[...] = jnp.zeros_like(dc_ref)
            dwa_ref[...] = jnp.zeros_like(dwa_ref)
            dwx_ref[...] = jnp.zeros_like(dwx_ref)
            dv_ref[...] = jnp.zeros_like(dv_ref)
            gc[...] = jnp.zeros_like(gc)
            an[...] = jnp.zeros_like(an)
            dxn[...] = jnp.zeros_like(dxn)

        xr = z_ref[:, W:2 * W].astype(F32)
        yg = z_ref[:, 0:W].astype(F32)
        sp = _softplus_neg(lam_ref[...])
        a = a_ref[...]
        mult, r, gi, xc = (sv_ref[:, k * W:(k + 1) * W].astype(F32) for k in range(4))
        h = h_ref[...]
        hp = jnp.where(ti == 0, 0.0, hprev_ref[...])
        hm1 = _shift_rows(jnp.concatenate([hp, h], axis=0), 1)[SUBLANES:]
        dout = dm_ref[...].astype(F32)
        d_s[...] = dout * _gelu(yg)
        dz_ref[:, 0:W] = (dout * h * _gelu_grad(yg)).astype(BF16)
        c_s[...] = _shift_rows(jnp.concatenate([a, an[...]], axis=0), tt + SUBLANES - 1)[:tt]
        an[...] = a[0:SUBLANES, :]
        row = lax.broadcasted_iota(jnp.int32, (SUBLANES, W), 0)

        def step(j, gnext):
            r0 = pl.multiple_of((ng - 1 - j) * SUBLANES, SUBLANES)
            C = c_s[pl.ds(r0, SUBLANES), :]
            G = d_s[pl.ds(r0, SUBLANES), :]
            for k in (1, 2, 4):
                m = row < SUBLANES - k
                G = jnp.where(m, G + C * pltpu.roll(G, SUBLANES - k, 0), G)
                C = jnp.where(m, C * pltpu.roll(C, SUBLANES - k, 0), C)
            G = G + C * gnext
            g_s[pl.ds(r0, SUBLANES), :] = G
            return jnp.broadcast_to(G[0:1, :], (SUBLANES, W))

        gc[...] = lax.fori_loop(0, ng, step, gc[...])
        du = g_s[...]
        da = du * hm1
        dgi = du * mult * xc
        dxc = du * mult * gi
        dmult = du * gi * xc
        dlog_a = da * a - dmult * (a * a) / mult
        dr = dlog_a * (-LRU_C * sp)
        dv_ref[2] += _rsum8(dlog_a * (-LRU_C * r))
        dpr = (dr * r * (1.0 - r)).astype(BF16)
        dpi = (dgi * gi * (1.0 - gi)).astype(BF16)
        dv_ref[0] += _rsum8(dpr.astype(F32))
        dv_ref[1] += _rsum8(dpi.astype(F32))
        xcb = sv_ref[:, 3 * W:4 * W]
        dwa_ref[...] += lax.dot_general(xcb, dpr, TN, preferred_element_type=F32)
        dwx_ref[...] += lax.dot_general(xcb, dpi, TN, preferred_element_type=F32)
        dxc = dxc + jnp.dot(dpr, wat_ref[...], preferred_element_type=F32) + jnp.dot(dpi, wxt_ref[...], preferred_element_type=F32)
        n = tt + BF16_ROWS
        de = jnp.concatenate([dxc, dxn[...]], axis=0)
        d1, d2, d3 = (_shift_rows(de, n - j)[:tt] for j in (1, 2, 3))
        dxn[...] = dxc[0:BF16_ROWS, :]
        dz_ref[:, W:2 * W] = (cw_ref[3:4, :] * dxc + cw_ref[2:3, :] * d1 + cw_ref[1:2, :] * d2 + cw_ref[0:1, :] * d3).astype(BF16)
        dc_ref[0] += _rsum8(d3 * xr)
        dc_ref[1] += _rsum8(d2 * xr)
        dc_ref[2] += _rsum8(d1 * xr)
        dc_ref[3] += _rsum8(dxc * xr)
        dc_ref[4] += _rsum8(dxc)

    rev = lambda i: nt - 1 - i
    tok = lambda w: pl.BlockSpec((tt, w), lambda i: (rev(i), 0))
    return pl.pallas_call(
        body, name=name, grid=(nt,),
        in_specs=[tok(2 * W), tok(W),
                  pl.BlockSpec((SUBLANES, W), lambda i: (jnp.maximum(rev(i) * sb - 1, 0), 0)),
                  tok(W), tok(4 * W), tok(W),
                  _resident((SUBLANES, W)), _resident((W, W)), _resident((W, W)), _resident((1, W))],
        out_specs=[pl.BlockSpec((tt, 2 * W), lambda i: (rev(i), 0)),
                   pl.BlockSpec((5, SUBLANES, W), lambda i: (0, 0, 0)),
                   pl.BlockSpec((W, W), lambda i: (0, 0)), pl.BlockSpec((W, W), lambda i: (0, 0)),
                   pl.BlockSpec((3, SUBLANES, W), lambda i: (0, 0, 0))],
        out_shape=[jax.ShapeDtypeStruct((T, z.shape[1]), BF16), jax.ShapeDtypeStruct((5, SUBLANES, W), F32),
                   jax.ShapeDtypeStruct((W, W), F32), jax.ShapeDtypeStruct((W, W), F32),
                   jax.ShapeDtypeStruct((3, SUBLANES, W), F32)],
        scratch_shapes=[pltpu.VMEM((tt, W), F32), pltpu.VMEM((tt, W), F32), pltpu.VMEM((tt, W), F32),
                        pltpu.VMEM((SUBLANES, W), F32), pltpu.VMEM((SUBLANES, W), F32), pltpu.VMEM((BF16_ROWS, W), F32)],
        compiler_params=_cp(1),
    )(z, hseq, hseq, a_all, saved, dmix, cw, wat, wxt, lam)


def _split3(x):
    hi = x.astype(BF16)
    r1 = x - hi.astype(F32)
    mid = r1.astype(BF16)
    lo = (r1 - mid.astype(F32)).astype(BF16)
    return hi, mid, lo


def _tri_matmul(tri, x):
    hi, mid, lo = _split3(x)
    return (jnp.dot(tri, hi, preferred_element_type=F32) + jnp.dot(tri, mid, preferred_element_type=F32)
            + jnp.dot(tri, lo, preferred_element_type=F32))


def _hg_chunk(q, fl, lb):
    C = q.shape[0]
    ri = lax.broadcasted_iota(jnp.int32, (C, C), 0)
    ci = lax.broadcasted_iota(jnp.int32, (C, C), 1)
    causal = ri >= ci
    sig = _sigmoid(fl)
    f = lb + (1.0 - lb) * sig
    k = 1.0 - f
    sq = _sigmoid(q)
    qf = q * sq
    b = _tri_matmul(causal.astype(BF16), jnp.log(f))
    bm = b[C // 2 - 1:C // 2, :]
    bl = b[C - 1:C, :]
    e_qt, e_kt, e_in, e_out = jnp.exp(b - bm), jnp.exp(bm - b), jnp.exp(b), jnp.exp(bl - b)
    qt = qf * e_qt
    kt = k * e_kt
    qin = qf * e_in
    kout = k * e_out
    qtb, ktb = qt.astype(BF16), kt.astype(BF16)
    att = [jnp.where(causal, _dot(qtb[:, _head(h)], ktb[:, _head(h)], NT), 0.0).astype(BF16) for h in range(HG_HEADS)]
    return dict(sig=sig, f=f, k=k, sq=sq, qf=qf, b=b, bm=bm, bl=bl, qt=qt, kt=kt, qin=qin, kout=kout, att=att,
                causal=causal, anti=ri <= ci, decay=jnp.exp(bl), e_qt=e_qt, e_kt=e_kt, e_in=e_in, e_out=e_out)


def _head(h):
    return slice(h * HG_D, (h + 1) * HG_D)


def _hgrn_fwd(z, lb, gn, mix, name="hgrn_fwd", tt=512):
    T = z.shape[0]
    C = HG_CHUNK
    nc = tt // C
    Dh = HG_D
    Wd = HG_HEADS * Dh

    def body(q_ref, f_ref, v_ref, g_ref, lb_ref, gn_ref, mix_ref, o_ref, ss_ref, st):
        del mix_ref

        @pl.when(pl.program_id(0) == 0)
        def _():
            st[...] = jnp.zeros_like(st)

        S = [st[h] for h in range(HG_HEADS)]
        for c in range(nc):
            rows = slice(c * C, (c + 1) * C)
            ck = _hg_chunk(q_ref[rows, :].astype(F32), f_ref[rows, :].astype(F32), lb_ref[...])
            v = v_ref[rows, :]
            g = g_ref[rows, :].astype(F32)
            H = range(HG_HEADS)
            qinb, koutb = ck["qin"].astype(BF16), ck["kout"].astype(BF16)
            for h in H:
                ss_ref[h, c] = S[h]
            o = [_dot(ck["att"][h], v[:, _head(h)]) + _dot(qinb[:, _head(h)], S[h], NT) for h in H]
            S = [ck["decay"][:, _head(h)] * S[h] + _dot(v[:, _head(h)], koutb[:, _head(h)], TN) for h in H]
            outs = [o[h] * lax.rsqrt(jnp.mean(o[h] * o[h], axis=-1, keepdims=True) + EPS) * gn_ref[...] for h in H]
            o_ref[rows, :] = (jnp.concatenate(outs, axis=1) * (g * _sigmoid(g))).astype(BF16)
        for h in range(HG_HEADS):
            st[h] = S[h]

    col = lambda base: (lambda i: (i, base))
    return pl.pallas_call(
        body, name=name, grid=(T // tt,),
        in_specs=[pl.BlockSpec((tt, Wd), col(2)), pl.BlockSpec((tt, Wd), col(3)), pl.BlockSpec((tt, Wd), col(4)),
                  pl.BlockSpec((tt, Wd), col(5)), _resident((1, Wd)), _resident((1, Dh)), ANY],
        out_specs=[pl.BlockSpec((tt, Wd), lambda i: (i, 1)),
                   pl.BlockSpec((HG_HEADS, nc, Dh, Dh), lambda i: (0, i, 0, 0))],
        out_shape=[jax.ShapeDtypeStruct((T, 2 * Wd), BF16),
                   jax.ShapeDtypeStruct((HG_HEADS, T // C, Dh, Dh), F32)],
        scratch_shapes=[pltpu.VMEM((HG_HEADS, Dh, Dh), F32)],
        compiler_params=_cp(1),
        input_output_aliases={6: 0},
    )(z, z, z, z, lb, gn, mix)


def _hgrn_bwd(z, ss, dmix, lb, gn, dz, name="hgrn_bwd", tt=512):
    T = z.shape[0]
    C = HG_CHUNK
    nc = tt // C
    nt = T // tt
    Dh = HG_D
    Wd = HG_HEADS * Dh

    def body(q_ref, f_ref, v_ref, g_ref, ss_ref, dm_ref, lb_ref, gn_ref, dz01_ref, dz_ref, dlb_ref, dgn_ref, dst):
        @pl.when(pl.program_id(0) == 0)
        def _():
            dst[...] = jnp.zeros_like(dst)
            dlb_ref[...] = jnp.zeros_like(dlb_ref)
            dgn_ref[...] = jnp.zeros_like(dgn_ref)

        dS = [dst[h] for h in range(HG_HEADS)]
        lbv = lb_ref[...]
        gnv = gn_ref[...]
        rowc = lax.broadcasted_iota(jnp.int32, (C, Wd), 0)
        cat = lambda xs: jnp.concatenate(xs, axis=1)
        for c in reversed(range(nc)):
            rows = slice(c * C, (c + 1) * C)
            q = q_ref[rows, :].astype(F32)
            ck = _hg_chunk(q, f_ref[rows, :].astype(F32), lbv)
            v = v_ref[rows, :]
            g = g_ref[rows, :].astype(F32)
            dout = dm_ref[rows, :].astype(F32)
            sg, dsg = _silu_and_grad(g)
            d_ong = dout * sg
            H = range(HG_HEADS)
            qinb, koutb, qtb, ktb = (ck[n].astype(BF16) for n in ("qin", "kout", "qt", "kt"))
            S = [ss_ref[h, c] for h in H]
            Sb = [s.astype(BF16) for s in S]
            dSb = [d.astype(BF16) for d in dS]
            o = [_dot(ck["att"][h], v[:, _head(h)]) + _dot(qinb[:, _head(h)], Sb[h], NT) for h in H]
            rn = [lax.rsqrt(jnp.mean(o[h] * o[h], axis=-1, keepdims=True) + EPS) for h in H]
            on = [o[h] * rn[h] for h in H]
            don = [d_ong[:, _head(h)] * gnv for h in H]
            do = [(rn[h] * (don[h] - on[h] * jnp.mean(don[h] * on[h], axis=-1, keepdims=True))).astype(BF16) for h in H]
            datt = [jnp.where(ck["causal"], _dot(do[h], v[:, _head(h)], NT), 0.0).astype(BF16) for h in H]
            dvs = [_dot(ck["att"][h], do[h], TN) + _dot(koutb[:, _head(h)], dSb[h], NT) for h in H]
            dqins = [_dot(do[h], Sb[h]) for h in H]
            dkouts = [_dot(v[:, _head(h)], dSb[h]) for h in H]
            dqts = [_dot(datt[h], ktb[:, _head(h)]) for h in H]
            dkts = [_dot(datt[h], qtb[:, _head(h)], TN) for h in H]
            ddecays = [jnp.sum(dS[h] * S[h], axis=0, keepdims=True) for h in H]
            dS = [_dot(do[h], qinb[:, _head(h)], TN) + ck["decay"][:, _head(h)] * dS[h] for h in H]
            ons = [on[h] * gnv for h in H]
            dgn = _rsum8(d_ong[:, _head(0)] * on[0])
            for h in range(1, HG_HEADS):
                dgn = dgn + _rsum8(d_ong[:, _head(h)] * on[h])
            dgn_ref[...] += dgn
            dqt, dkt, dqin, dkout, ddecay = cat(dqts), cat(dkts), cat(dqins), cat(dkouts), cat(ddecays)
            dqf = dqt * ck["e_qt"] + dqin * ck["e_in"]
            dk = dkt * ck["e_kt"] + dkout * ck["e_out"]
            kk = dkout * ck["kout"]
            db = dqt * ck["qt"] - dkt * ck["kt"] + dqin * ck["qin"] - kk
            dbl = jnp.sum(kk, axis=0, keepdims=True) + ddecay * ck["decay"]
            db = db + jnp.where(rowc == C - 1, dbl, 0.0)
            dlogf = _tri_matmul(ck["anti"].astype(BF16), db)
            dfv = dlogf / ck["f"] - dk
            sig, sq = ck["sig"], ck["sq"]
            dlb_ref[...] += _rsum8(dfv * (1.0 - sig))
            dz_ref[rows, 2 * Wd:3 * Wd] = (dqf * (sq * (1.0 + q * (1.0 - sq)))).astype(BF16)
            dz_ref[rows, 3 * Wd:4 * Wd] = (dfv * (1.0 - lbv) * sig * (1.0 - sig)).astype(BF16)
            dz_ref[rows, 4 * Wd:5 * Wd] = cat(dvs).astype(BF16)
            dz_ref[rows, 5 * Wd:6 * Wd] = (dout * cat(ons) * dsg).astype(BF16)
        dz_ref[:, 0:2 * Wd] = dz01_ref[...]
        for h in range(HG_HEADS):
            dst[h] = dS[h]

    rev = lambda i: nt - 1 - i
    col = lambda base: (lambda i: (rev(i), base))
    return pl.pallas_call(
        body, name=name, grid=(nt,),
        in_specs=[pl.BlockSpec((tt, Wd), col(2)), pl.BlockSpec((tt, Wd), col(3)), pl.BlockSpec((tt, Wd), col(4)),
                  pl.BlockSpec((tt, Wd), col(5)),
                  pl.BlockSpec((HG_HEADS, nc, Dh, Dh), lambda i: (0, rev(i), 0, 0)),
                  pl.BlockSpec((tt, Wd), col(1)), _resident((1, Wd)), _resident((1, Dh)),
                  pl.BlockSpec((tt, 2 * Wd), col(0))],
        out_specs=[pl.BlockSpec((tt, 6 * Wd), lambda i: (rev(i), 0)), pl.BlockSpec((SUBLANES, Wd), lambda i: (0, 0)),
                   pl.BlockSpec((SUBLANES, Dh), lambda i: (0, 0))],
        out_shape=[jax.ShapeDtypeStruct((T, 6 * Wd), BF16), jax.ShapeDtypeStruct((SUBLANES, Wd), F32),
                   jax.ShapeDtypeStruct((SUBLANES, Dh), F32)],
        input_output_aliases={8: 0},
        scratch_shapes=[pltpu.VMEM((HG_HEADS, Dh, Dh), F32)],
        compiler_params=_cp(1),
    )(z, z, z, z, ss, dmix, lb, gn, dz)


def _sgu_core(p, lg_ref, lb_ref, wsc_ref, bsb_ref):
    Wd = D_MODEL
    G = SGU_CHUNK
    zz = _gelu(p)
    u = zz[:, :Wd]
    v = zz[:, Wd:]
    vc = v - jnp.mean(v, axis=-1, keepdims=True)
    rstd = lax.rsqrt(jnp.mean(vc * vc, axis=-1, keepdims=True) + EPS)
    vhat = vc * rstd
    vn = vhat * lg_ref[...] + lb_ref[...]
    svs = []
    for gi in range(SGU_G):
        svs.append(jnp.dot(wsc_ref[gi], vn[:, gi * G:(gi + 1) * G].astype(BF16), preferred_element_type=F32) + bsb_ref[gi])
    return u, vhat, rstd, vn, jnp.concatenate(svs, axis=1)


def _sgu_fwd(p1, lg, lbias, wsc, bsb, name="sgu_fwd", tt=512):
    T = p1.shape[0]
    Wd = D_MODEL
    C = SGU_CHUNK

    def body(p_ref, lg_ref, lb_ref, wsc_ref, bsb_ref, s_ref):
        for c in range(tt // C):
            rows = slice(c * C, (c + 1) * C)
            u, _, _, _, sv = _sgu_core(p_ref[rows, :].astype(F32), lg_ref, lb_ref, wsc_ref, bsb_ref)
            s_ref[rows, :] = (u * sv).astype(BF16)

    return pl.pallas_call(
        body, name=name, grid=(T // tt,),
        in_specs=[pl.BlockSpec((tt, 2 * Wd), lambda i: (i, 0)), _resident((1, Wd)), _resident((1, Wd)),
                  _resident((SGU_G, C, C)), _resident((SGU_G, C, C))],
        out_specs=pl.BlockSpec((tt, Wd), lambda i: (i, 0)),
        out_shape=jax.ShapeDtypeStruct((T, Wd), BF16),
        compiler_params=_cp(1),
    )(p1, lg, lbias, wsc, bsb)


def _sgu_bwd(p1, ds, lg, lbias, wsc, wsct, bsb, name="sgu_bwd", tt=512):
    T = p1.shape[0]
    Wd = D_MODEL
    C = SGU_CHUNK

    def body(p_ref, ds_ref, lg_ref, lb_ref, wsc_ref, wsct_ref, bsb_ref, dp_ref, dws_ref, dbs_ref, dlg_ref, dlb_ref, dbin_ref):
        @pl.when(pl.program_id(0) == 0)
        def _():
            dws_ref[...] = jnp.zeros_like(dws_ref)
            dbs_ref[...] = jnp.zeros_like(dbs_ref)
            dlg_ref[...] = jnp.zeros_like(dlg_ref)
            dlb_ref[...] = jnp.zeros_like(dlb_ref)
            dbin_ref[...] = jnp.zeros_like(dbin_ref)

        for c in range(tt // C):
            rows = slice(c * C, (c + 1) * C)
            p = p_ref[rows, :].astype(F32)
            u, vhat, rstd, vn, sv = _sgu_core(p, lg_ref, lb_ref, wsc_ref, bsb_ref)
            dsc = ds_ref[rows, :].astype(F32)
            du = dsc * sv
            dsv = dsc * u
            dvns = []
            for gi in range(SGU_G):
                cs = slice(gi * C, (gi + 1) * C)
                dsv_g = dsv[:, cs]
                dvns.append(jnp.dot(wsct_ref[gi], dsv_g.astype(BF16), preferred_element_type=F32))
                dws_ref[gi] += _dot(dsv_g, vn[:, cs], NT)
                dbs_ref[gi] += dsv_g
            dvn = jnp.concatenate(dvns, axis=1)
            dlg_ref[...] += _rsum8(dvn * vhat)
            dlb_ref[...] += _rsum8(dvn)
            dvh = dvn * lg_ref[...]
            dv = rstd * (dvh - jnp.mean(dvh, axis=-1, keepdims=True) - vhat * jnp.mean(dvh * vhat, axis=-1, keepdims=True))
            dp = jnp.concatenate([du, dv], axis=1) * _gelu_grad(p)
            dbin_ref[...] += _rsum8(dp)
            dp_ref[rows, :] = dp.astype(BF16)

    full3 = pl.BlockSpec((SGU_G, C, C), lambda i: (0, 0, 0))
    return pl.pallas_call(
        body, name=name, grid=(T // tt,),
        in_specs=[pl.BlockSpec((tt, 2 * Wd), lambda i: (i, 0)), pl.BlockSpec((tt, Wd), lambda i: (i, 0)),
                  _resident((1, Wd)), _resident((1, Wd)), _resident((SGU_G, C, C)), _resident((SGU_G, C, C)),
                  _resident((SGU_G, C, C))],
        out_specs=[pl.BlockSpec((tt, 2 * Wd), lambda i: (i, 0)), full3, full3,
                   pl.BlockSpec((SUBLANES, Wd), lambda i: (0, 0)), pl.BlockSpec((SUBLANES, Wd), lambda i: (0, 0)),
                   pl.BlockSpec((SUBLANES, 2 * Wd), lambda i: (0, 0))],
        out_shape=[jax.ShapeDtypeStruct((T, 2 * Wd), BF16), jax.ShapeDtypeStruct((SGU_G, C, C), F32),
                   jax.ShapeDtypeStruct((SGU_G, C, C), F32), jax.ShapeDtypeStruct((SUBLANES, Wd), F32),
                   jax.ShapeDtypeStruct((SUBLANES, Wd), F32), jax.ShapeDtypeStruct((SUBLANES, 2 * Wd), F32)],
        compiler_params=_cp(1),
    )(p1, ds, lg, lbias, wsc, wsct, bsb)


def _pad_rows(w, rows=SUBLANES):
    return jnp.pad(w, ((0, rows - w.shape[0]), (0, 0)))


def _block_diag(w):
    n, b, _ = w.shape
    return (w[:, :, None, :] * jnp.eye(n, dtype=w.dtype)[:, None, :, None]).reshape(n * b, n * b)


def _diag_blocks(m, n):
    b = m.shape[0] // n
    m4 = m.reshape(n, b, n, b)
    return jnp.stack([m4[k, :, k, :] for k in range(n)], axis=0)


def _piece_major(dw):
    if dw.ndim == 2:
        K, N = dw.shape
        return dw.reshape(N_CHIPS, 2, K // (2 * N_CHIPS), N)
    _, K, ns = dw.shape
    return dw.reshape(N_CHIPS, 2, K // 2, ns)


def _ffn_fwd(h, g, w_up, cw, cb, w_down, tag, loss=None):
    hn, gu = _norm_mm(h, g, w_up, jnp.zeros((1, 2 * D_FF), F32), name=f"ffn_up_{tag}")
    a, gc, *out = _ffn_act(gu, cw, cb, w_down, h, name=f"ffn_act_down_{tag}", loss=loss)
    return (out[0] if loss is None else out), (hn, gu, gc, a)


def _ffn_bwd(dh, h, g, saved, w_up, cw, w_down, tag):
    hn, gu, gc, a = saved
    dwd = _mm_tn(a, dh, name=f"ffn_dwd_{tag}")
    dgu, dc, dhin, dg8 = _ffn_act_bwd(gu, gc, cw, w_up, w_down, h, g, dh, name=f"ffn_bwd_{tag}")
    dwu = _mm_tn(hn, dgu, name=f"ffn_dwu_{tag}", col_shards=N_CHIPS)
    dcs = dc.sum(axis=1)
    return dhin, dg8.sum(axis=0), dwu, dcs[0:3], dcs[3], dwd


REDUCE_GROUPS = {"g1": [("ffn_w_up", 1), ("ffn_w_down", 1), ("od_w_out", 0), ("od_w_in", 0)],
                 "g2": [("ffn_w_up", 0), ("ffn_w_down", 0)],
                 "g3": [("ev_w_out", 0), ("ev_w_in", 0)]}


def _local_step(x, tgt, p, start_reduce=None, continue_reduce=None):
    row = lambda v: v.reshape(1, -1)
    grads = {}

    lower = jax.nn.softmax(p["hg_lb_logits"], axis=0)
    lb0 = row(lower[0])
    ev_cw = _pad_rows(p["ev_conv_w"][0])
    ev_cb = row(p["ev_conv_b"][0])
    wa = _block_diag(p["ev_gate_a_w"][0]).astype(BF16)
    wx = _block_diag(p["ev_gate_x_w"][0]).astype(BF16)
    ba, bx, lam = row(p["ev_gate_a_b"][0]), row(p["ev_gate_x_b"][0]), row(p["ev_lru_lambda"][0])
    gn = row(p["ev_hg_norm"][0])
    tril = jnp.tril(jnp.ones((SGU_CHUNK, SGU_CHUNK), F32))
    wsc = (p["od_w_s"][0] * tril).astype(BF16)
    bsb = jnp.broadcast_to(p["od_b_s"][0][:, :, None], (SGU_G, SGU_CHUNK, SGU_CHUNK)).astype(F32)
    ffn_cw = [_pad_rows(p["ffn_conv_w"][l]) for l in range(2)]
    ffn_cb = [row(p["ffn_conv_b"][l]) for l in range(2)]
    ev_w_in, ev_w_out = p["ev_w_in"][0], p["ev_w_out"][0]
    nm = [row(p["norm_mix"][l]) for l in range(2)]
    nf = [row(p["norm_ffn"][l]) for l in range(2)]

    h0 = x
    hn0, z0 = _norm_mm(h0, nm[0], ev_w_in, jnp.zeros((1, ev_w_in.shape[1]), F32), name="ev_in")
    out_a, hseq, a_all, lru_saved = _lru_fwd(z0, ev_cw, ev_cb, wa, ba, wx, bx, lam)
    mix0, ss = _hgrn_fwd(z0, lb0, gn, out_a)
    h1 = _mm(mix0, ev_w_out, h0, F32, name="ev_out")
    late = p["late"](h1) if "late" in p else p
    od_w_in, od_w_out = late["od_w_in"][0], late["od_w_out"][0]
    w_up = [(late["ffn_w_up"], l) for l in range(2)]
    w_down = [(late["ffn_w_down"], l) for l in range(2)]
    h2, ffn0 = _ffn_fwd(h1, nf[0], w_up[0], ffn_cw[0], ffn_cb[0], w_down[0], "l0")
    hn1, p1 = _norm_mm(h2, nm[1], od_w_in, row(p["od_b_in"][0]), name="od_in")
    s1 = _sgu_fwd(p1, row(p["od_ln_g"][0]), row(p["od_ln_b"][0]), wsc, bsb)
    h3 = _mm(s1, od_w_out, h2, F32, name="od_out")
    (dh4, sq8, dgf8), ffn1 = _ffn_fwd(h3, nf[1], w_up[1], ffn_cw[1], ffn_cb[1], w_down[1], "l1", loss=(row(p["norm_final"]), tgt))
    grads["norm_final"] = dgf8.sum(axis=0)

    big = {}
    dh3, dnf1, dwu1, dcw1, dcb1, dwd1 = _ffn_bwd(dh4, h3, nf[1], ffn1, w_up[1], ffn_cw[1], w_down[1], "l1")
    big["ffn_w_up", 1], big["ffn_w_down", 1] = _piece_major(dwu1), _piece_major(dwd1)
    ds1 = _mm(dh3, od_w_out, None, BF16, name="od_ds", transpose_w=True)
    big["od_w_out", 0] = _piece_major(_mm_tn(s1, dh3, name="od_dwo"))
    wsct = jnp.swapaxes(wsc, 1, 2)
    dp1, dws, dbs, dlg8, dlb8, dbin8 = _sgu_bwd(p1, ds1, row(p["od_ln_g"][0]), row(p["od_ln_b"][0]), wsc, wsct, bsb)
    grads["od_w_s"] = (dws * tril)[None]
    grads["od_b_s"] = dbs.sum(axis=-1)[None]
    grads["od_ln_g"] = dlg8.sum(axis=0)[None]
    grads["od_ln_b"] = dlb8.sum(axis=0)[None]
    grads["od_b_in"] = dbin8.sum(axis=0)[None]
    dh2, dnm1 = _mm_normbwd(dp1, od_w_in, h2, nm[1], dh3, name="od_dh")
    big["od_w_in", 0] = _piece_major(_mm_tn(hn1, dp1, name="od_dwi", col_shards=N_CHIPS))
    if start_reduce is not None:
        token = start_reduce("g1", [big[key] for key in REDUCE_GROUPS["g1"]], split=True)
        ffn_cw[0] = ffn_cw[0] + token[0:1, 0:1]

    dh1, dnf0, dwu0, dcw0, dcb0, dwd0 = _ffn_bwd(dh2, h1, nf[0], ffn0, w_up[0], ffn_cw[0], w_down[0], "l0")
    big["ffn_w_up", 0], big["ffn_w_down", 0] = _piece_major(dwu0), _piece_major(dwd0)
    if start_reduce is not None:
        token = continue_reduce("g1", dh1) + start_reduce("g2", [big[key] for key in REDUCE_GROUPS["g2"]], split=True)
        lam = lam + token[0:1, 0:1]
    dmix = _mm(dh1, ev_w_out, None, BF16, name="ev_dmix", transpose_w=True)
    big["ev_w_out", 0] = _piece_major(_mm_tn(mix0, dh1, name="ev_dwo"))
    dz01, dc5, dwa, dwx, dvec = _lru_bwd(z0, hseq, a_all, lru_saved, dmix, ev_cw, wa.T, wx.T, lam)
    if start_reduce is not None:
        lb0 = lb0 + continue_reduce("g2", dc5)[0:1, 0:1]
    dz0, dlb8, dgn8 = _hgrn_bwd(z0, ss, dmix, lb0, gn, dz01)
    big["ev_w_in", 0] = _piece_major(_mm_tn(hn0, dz0, name="ev_dwi", col_shards=N_CHIPS))
    if start_reduce is not None:
        token = start_reduce("g3", [big[key] for key in REDUCE_GROUPS["g3"]])
        nm[0] = nm[0] + token[0:1, 0:1]
    grad_x, dnm0 = _mm_normbwd(dz0, ev_w_in, h0, nm[0], dh1, name="ev_dh")

    dc5s = dc5.sum(axis=1)
    grads["ev_conv_w"] = dc5s[0:4][None]
    grads["ev_conv_b"] = dc5s[4][None]
    grads["ev_gate_a_w"] = _diag_blocks(dwa, LRU_BLOCKS)[None]
    grads["ev_gate_x_w"] = _diag_blocks(dwx, LRU_BLOCKS)[None]
    dvs = dvec.sum(axis=1)
    grads["ev_gate_a_b"] = dvs[0][None]
    grads["ev_gate_x_b"] = dvs[1][None]
    grads["ev_lru_lambda"] = (dvs[2] * (-jax.nn.sigmoid(-p["ev_lru_lambda"][0])))[None]
    dlb = dlb8.sum(axis=0)
    grads["hg_lb_logits"] = dlb[None, :] * lower[0][None, :] * (jnp.eye(3, dtype=F32)[0][:, None] - lower)
    grads["ev_hg_norm"] = dgn8.sum(axis=0)[None]
    grads["norm_mix"] = jnp.stack([dnm0.sum(axis=0), dnm1.sum(axis=0)])
    grads["norm_ffn"] = jnp.stack([dnf0, dnf1])
    grads["ffn_conv_w"] = jnp.stack([dcw0, dcw1])
    grads["ffn_conv_b"] = jnp.stack([dcb0, dcb1])
    return sq8, grad_x, grads, big


SH_BIG = {"ev_w_in": 2, "ev_w_out": 1, "od_w_in": 2, "od_w_out": 1, "ffn_w_up": 2, "ffn_w_down": 1}
SH_SMALL = {"ev_conv_w": 2, "od_b_in": 1, "od_ln_g": 1, "od_ln_b": 1, "ffn_conv_w": 2}
REP = ["norm_mix", "norm_ffn", "norm_final", "ev_conv_b", "ev_gate_a_w", "ev_gate_a_b", "ev_gate_x_w", "ev_gate_x_b",
       "ev_lru_lambda", "hg_lb_logits", "ev_hg_norm", "od_w_s", "od_b_s", "ffn_conv_b"]
WEIGHTS = ["norm_mix", "norm_ffn", "norm_final", "ev_w_in", "ev_conv_w", "ev_conv_b", "ev_gate_a_w", "ev_gate_a_b", "ev_gate_x_w",
           "ev_gate_x_b", "ev_lru_lambda", "hg_lb_logits", "ev_hg_norm", "ev_w_out", "od_w_in", "od_b_in", "od_ln_g", "od_ln_b",
           "od_w_s", "od_b_s", "od_w_out", "ffn_w_up", "ffn_conv_w", "ffn_conv_b", "ffn_w_down"]


def _rows(n_elems, mult=SUBLANES):
    r = -(-n_elems // LANES)
    return -(-r // mult) * mult


def _pack(arrs, rows, dtype):
    flat = jnp.concatenate([a.reshape(-1).astype(dtype) for a in arrs])
    return jnp.pad(flat, (0, rows * LANES - flat.shape[0])).reshape(rows, LANES)


def _unpack(flat2d, shapes):
    flat = flat2d.reshape(-1)
    out, off = [], 0
    for s in shapes:
        n = 1
        for d in s:
            n *= d
        out.append(flat[off:off + n].reshape(s))
        off += n
    return out


def _mesh_pos():
    return lax.axis_index("x"), lax.axis_index("y"), lax.axis_index("c")


def _other_chips(x, y):
    return [(1 - x, y), (x, 1 - y), (1 - x, 1 - y)]


def _half_rows(n):
    return lambda r, c: r.at[0, pl.ds(c * (n // 2), n // 2), :]


GATHER_BIG = {
    "ev_w_in": ((1024, 3072), _half_rows(1024), lambda o, k, c: o.at[pl.ds(c * 512, 512), pl.ds(k * 768, 768)]),
    "ev_w_out": ((1024, 1024), _half_rows(256), lambda o, k, c: o.at[pl.ds(k * 256 + c * 128, 128), :]),
    "od_w_in": ((1024, 2048), _half_rows(1024), lambda o, k, c: o.at[pl.ds(c * 512, 512), pl.ds(k * 512, 512)]),
    "od_w_out": ((1024, 1024), _half_rows(256), lambda o, k, c: o.at[pl.ds(k * 256 + c * 128, 128), :]),
    "ffn_w_up": ((2, 1024, 2 * D_FF), lambda r, c: r.at[c], lambda o, k, c: o.at[c, :, pl.ds(k * (2 * D_FF // 4), 2 * D_FF // 4)]),
    "ffn_w_down": ((2, D_FF, 1024), lambda r, c: r.at[c], lambda o, k, c: o.at[c, pl.ds(k * (D_FF // 4), D_FF // 4), :]),
}


def _gather_weights(names, big, small):
    nb = len(big)
    descs = [GATHER_BIG[n] for n in names]
    rs = small.shape[0]

    def body(*refs):
        ins, s_ref = refs[:nb], refs[nb]
        outs, os_ref = refs[nb + 1:2 * nb + 1], refs[2 * nb + 1]
        ici_send, ici_recv, d2d_send, d2d_recv, loc_sems = refs[2 * nb + 2:2 * nb + 7]
        vbufs = refs[2 * nb + 7:]
        x, y, c = _mesh_pos()
        k = 2 * x + y
        chips = _other_chips(x, y)
        sib = (x, y, 1 - c)

        def remote(src, dst, ssem, rsem, to):
            return pltpu.make_async_remote_copy(src_ref=src, dst_ref=dst, send_sem=ssem, recv_sem=rsem, device_id=to,
                                                device_id_type=MESH)

        stage = [pltpu.make_async_copy(ins[t], vbufs[t], loc_sems.at[2 * t]) for t in range(nb)]
        stage.append(pltpu.make_async_copy(s_ref, vbufs[nb], loc_sems.at[2 * nb]))
        for cp in stage:
            cp.start()
        sends = []
        for t, (_, src, dst) in enumerate(descs):
            for j, (px, py) in enumerate(chips):
                sends.append(remote(src(ins[t], c), dst(outs[t], k, c), ici_send.at[3 * t + j], ici_recv.at[3 * t + j], (px, py, c)))
        for j, (px, py) in enumerate(chips):
            sends.append(remote(s_ref, os_ref.at[k], ici_send.at[3 * nb + j], ici_recv.at[3 * nb + j], (px, py, c)))
        for cp in sends:
            cp.start()
        for cp in stage:
            cp.wait()
        local = []
        for t, (_, src, dst) in enumerate(descs):
            for cc in (0, 1):
                local.append(pltpu.make_async_copy(src(vbufs[t], cc), dst(outs[t], k, cc), loc_sems.at[2 * t + cc]))
        local.append(pltpu.make_async_copy(vbufs[nb], os_ref.at[k], loc_sems.at[2 * nb]))
        for cp in local:
            cp.start()
        for t, (_, src, dst) in enumerate(descs):
            for j, (px, py) in enumerate(chips):
                got = dst(outs[t], 2 * px + py, c)
                remote(got, got, ici_send.at[3 * t + j], ici_recv.at[3 * t + j], (px, py, c)).wait_recv()
                fwd = remote(got, got, d2d_send.at[3 * t + j], d2d_recv.at[3 * t + j], sib)
                fwd.start()
                sends.append(fwd)
        for j, (px, py) in enumerate(chips):
            remote(s_ref, os_ref.at[2 * px + py], ici_send.at[3 * nb + j], ici_recv.at[3 * nb + j], (px, py, c)).wait_recv()
        for t, (_, src, dst) in enumerate(descs):
            for j, (px, py) in enumerate(chips):
                theirs = dst(outs[t], 2 * px + py, 1 - c)
                remote(theirs, theirs, d2d_send.at[3 * t + j], d2d_recv.at[3 * t + j], sib).wait_recv()
        for cp in sends:
            cp.wait_send()
        for cp in local:
            cp.wait()

    out_shape = [jax.ShapeDtypeStruct(d[0], BF16) for d in descs] + [jax.ShapeDtypeStruct((N_CHIPS, rs, LANES), small.dtype)]
    return pl.pallas_call(
        body, name="gather_weights", in_specs=[ANY] * (nb + 1), out_specs=[ANY] * (nb + 1), out_shape=out_shape,
        scratch_shapes=[pltpu.SemaphoreType.DMA((3 * nb + 3,)), pltpu.SemaphoreType.DMA((3 * nb + 3,)),
                        pltpu.SemaphoreType.DMA((3 * nb,)), pltpu.SemaphoreType.DMA((3 * nb,)),
                        pltpu.SemaphoreType.DMA((2 * nb + 1,))]
        + [pltpu.VMEM(b.shape, b.dtype) for b in big] + [pltpu.VMEM(small.shape, small.dtype)],
        compiler_params=pltpu.CompilerParams(vmem_limit_bytes=VMEM_LIMIT),
    )(*big, small)


def _place_own(names, big):
    nb = len(big)
    descs = [GATHER_BIG[n] for n in names]

    def body(*refs):
        ins, outs = refs[:nb], refs[nb:2 * nb]
        sems, vbufs = refs[2 * nb], refs[2 * nb + 1:]
        x, y, c = _mesh_pos()
        k = 2 * x + y
        stage = [pltpu.make_async_copy(ins[t], vbufs[t], sems.at[2 * t]) for t in range(nb)]
        for cp in stage:
            cp.start()
        for cp in stage:
            cp.wait()
        local = [pltpu.make_async_copy(src(vbufs[t], cc), dst(outs[t], k, cc), sems.at[2 * t + cc])
                 for t, (_, src, dst) in enumerate(descs) for cc in (0, 1)]
        for cp in local:
            cp.start()
        for cp in local:
            cp.wait()

    return pl.pallas_call(
        body, name="place_own", in_specs=[ANY] * nb, out_specs=[ANY] * nb,
        out_shape=[jax.ShapeDtypeStruct(d[0], BF16) for d in descs],
        scratch_shapes=[pltpu.SemaphoreType.DMA((2 * nb,))] + [pltpu.VMEM(b.shape, b.dtype) for b in big],
        compiler_params=pltpu.CompilerParams(vmem_limit_bytes=VMEM_LIMIT),
    )(*big)


def _gather_start(names, big, bufs):
    nb = len(big)
    descs = [GATHER_BIG[n] for n in names]

    def body(*refs):
        ins, lnd = refs[:nb], refs[nb:2 * nb]
        send_sems, recv_sems, token = refs[2 * nb], refs[2 * nb + 1], refs[-1]
        x, y, c = _mesh_pos()
        k = 2 * x + y
        for t, (_, src, dst) in enumerate(descs):
            for j, (px, py) in enumerate(_other_chips(x, y)):
                _remote(src(ins[t], c), dst(lnd[t], k, c), send_sems.at[3 * t + j], recv_sems.at[3 * t + j], (px, py, c)).start()
        token[...] = jnp.zeros_like(token)

    out = pl.pallas_call(
        body, name="gather_start",
        out_shape=(pltpu.SemaphoreType.DMA((3 * nb,)), pltpu.SemaphoreType.DMA((3 * nb,)),
                   *[pltpu.HBM(b.shape, b.dtype) for b in big], *[pltpu.HBM(b.shape, b.dtype) for b in bufs],
                   jax.ShapeDtypeStruct((SUBLANES, LANES), F32)),
        in_specs=[HBM] * (2 * nb), out_specs=(SEM, SEM, *[HBM] * (2 * nb), pl.BlockSpec(memory_space=pltpu.VMEM)),
        input_output_aliases={i: 2 + i for i in range(2 * nb)},
        compiler_params=pltpu.CompilerParams(has_side_effects=DATAFLOW),
    )(*[pltpu.with_memory_space_constraint(b, pltpu.HBM) for b in big], *[pltpu.with_memory_space_constraint(b, pltpu.HBM) for b in bufs])
    return out[0], out[1], list(out[2:2 + nb]), list(out[2 + nb:2 + 2 * nb]), out[-1]


def _gather_wait(names, send_sems, recv_sems, big, bufs, after):
    nb = len(big)
    descs = [GATHER_BIG[n] for n in names]

    def body(*refs):
        ins, lnd = refs[:nb], refs[nb:2 * nb]
        ssem, rsem = refs[2 * nb], refs[2 * nb + 1]
        x, y, c = _mesh_pos()
        for t, (_, src, dst) in enumerate(descs):
            for j, (px, py) in enumerate(_other_chips(x, y)):
                cp = _remote(src(ins[t], c), dst(lnd[t], 2 * px + py, c), ssem.at[3 * t + j], rsem.at[3 * t + j], (px, py, c))
                cp.wait_send()
                cp.wait_recv()

    out = pl.pallas_call(
        body, name="gather_wait",
        out_shape=(*[pltpu.HBM(b.shape, b.dtype) for b in big], *[pltpu.HBM(b.shape, b.dtype) for b in bufs]),
        in_specs=[HBM] * (2 * nb) + [SEM, SEM, ANY], out_specs=tuple([HBM] * (2 * nb)),
        input_output_aliases={i: i for i in range(2 * nb)},
        compiler_params=pltpu.CompilerParams(has_side_effects=DATAFLOW),
    )(*big, *bufs, send_sems, recv_sems, after)
    return list(out[nb:])


def _gather_forward(names, bufs):
    nb = len(bufs)
    descs = [GATHER_BIG[n] for n in names]

    def body(*refs):
        outs = refs[nb:2 * nb]
        send_sems, recv_sems = refs[2 * nb:]
        x, y, c = _mesh_pos()
        sib = (x, y, 1 - c)
        sends = []
        for t, (_, src, dst) in enumerate(descs):
            for j, (px, py) in enumerate(_other_chips(x, y)):
                got = dst(outs[t], 2 * px + py, c)
                sends.append(_remote(got, got, send_sems.at[3 * t + j], recv_sems.at[3 * t + j], sib))
        for cp in sends:
            cp.start()
        for t, (_, src, dst) in enumerate(descs):
            for j, (px, py) in enumerate(_other_chips(x, y)):
                theirs = dst(outs[t], 2 * px + py, 1 - c)
                _remote(theirs, theirs, send_sems.at[3 * t + j], recv_sems.at[3 * t + j], sib).wait_recv()
        for cp in sends:
            cp.wait_send()

    return pl.pallas_call(
        body, name="gather_forward", in_specs=[ANY] * nb, out_specs=[ANY] * nb,
        out_shape=[jax.ShapeDtypeStruct(b.shape, b.dtype) for b in bufs], input_output_aliases={t: t for t in range(nb)},
        scratch_shapes=[pltpu.SemaphoreType.DMA((3 * nb,)), pltpu.SemaphoreType.DMA((3 * nb,))],
    )(*bufs)


def _remote(src, dst, ssem, rsem, to):
    return pltpu.make_async_remote_copy(src_ref=src, dst_ref=dst, send_sem=ssem, recv_sem=rsem, device_id=to, device_id_type=MESH)


def _rs_send_sibling(gs, tag):
    n = len(gs)
    ns = sum(N_CHIPS if g.ndim == 4 else 1 for g in gs)

    def body(*refs):
        cps = _sibling_copies(gs, refs[:n], refs[n:2 * n], refs[2 * n], refs[2 * n + 1])
        for cp in cps:
            cp.start()
        for cp in cps:
            cp.wait()

    out_shape = [jax.ShapeDtypeStruct(g.shape[:1] + g.shape[2:] if g.ndim == 4 else g.shape[1:], g.dtype) for g in gs]
    return pl.pallas_call(
        body, name=f"rs_send_sibling_{tag}", in_specs=[ANY] * n, out_specs=[ANY] * n, out_shape=out_shape,
        scratch_shapes=[pltpu.SemaphoreType.DMA((ns,)), pltpu.SemaphoreType.DMA((ns,))],
    )(*gs)


def _row_tile(P, Q):
    tr = P
    while tr * Q * 4 > ADD_TILE_BYTES and tr % (2 * SUBLANES) == 0:
        tr //= 2
    return tr


def _add_piece(g, recv, c, name):
    P, Q = g.shape[-2:]
    tr = _row_tile(P, Q)

    def body(c_ref, g_ref, r_ref, o_ref):
        o_ref[...] = g_ref[...].reshape(o_ref.shape) + r_ref[...]

    if g.ndim == 4:
        grid = (N_CHIPS, P // tr)
        in_specs = [pl.BlockSpec((1, 1, tr, Q), lambda k, i, c_ref: (k, c_ref[0], i, 0)),
                    pl.BlockSpec((1, tr, Q), lambda k, i, c_ref: (k, i, 0))]
        out_spec = pl.BlockSpec((1, tr, Q), lambda k, i, c_ref: (k, i, 0))
    else:
        grid = (1, P // tr)
        in_specs = [pl.BlockSpec((1, tr, Q), lambda k, i, c_ref: (c_ref[0], i, 0)), pl.BlockSpec((tr, Q), lambda k, i, c_ref: (i, 0))]
        out_spec = pl.BlockSpec((tr, Q), lambda k, i, c_ref: (i, 0))
    return pl.pallas_call(
        body, name=name,
        grid_spec=pltpu.PrefetchScalarGridSpec(num_scalar_prefetch=1, grid=grid, in_specs=in_specs, out_specs=out_spec),
        out_shape=jax.ShapeDtypeStruct(recv.shape, g.dtype),
        compiler_params=_cp(2),
    )(c, g, recv)


def _sibling_copies(gs, srcs, dsts, send_sems, recv_sems):
    x, y, c = _mesh_pos()
    cps, s = [], 0
    for t, g in enumerate(gs):
        if g.ndim == 4:
            for k in range(N_CHIPS):
                cps.append(_remote(srcs[t].at[k, 1 - c], dsts[t].at[k], send_sems.at[s], recv_sems.at[s], (x, y, 1 - c)))
                s += 1
        else:
            cps.append(_remote(srcs[t].at[1 - c], dsts[t], send_sems.at[s], recv_sems.at[s], (x, y, 1 - c)))
            s += 1
    return cps


def _sibling_start(gs, tag):
    n = len(gs)
    ns = sum(N_CHIPS if g.ndim == 4 else 1 for g in gs)
    lands = [pltpu.with_memory_space_constraint(lax.empty(g.shape[:1] + g.shape[2:] if g.ndim == 4 else g.shape[1:], g.dtype), pltpu.HBM)
             for g in gs]

    def body(*refs):
        for cp in _sibling_copies(gs, refs[:n], refs[n:2 * n], refs[2 * n], refs[2 * n + 1]):
            cp.start()
        refs[-1][...] = jnp.zeros_like(refs[-1])

    out = pl.pallas_call(
        body, name=f"sibling_start_{tag}",
        out_shape=(pltpu.SemaphoreType.DMA((ns,)), pltpu.SemaphoreType.DMA((ns,)),
                   *[pltpu.HBM(g.shape, g.dtype) for g in gs], *[pltpu.HBM(l.shape, l.dtype) for l in lands],
                   jax.ShapeDtypeStruct((SUBLANES, LANES), F32)),
        in_specs=[HBM] * (2 * n), out_specs=(SEM, SEM, *[HBM] * (2 * n), pl.BlockSpec(memory_space=pltpu.VMEM)),
        input_output_aliases={i: 2 + i for i in range(2 * n)},
        compiler_params=pltpu.CompilerParams(has_side_effects=DATAFLOW),
    )(*[pltpu.with_memory_space_constraint(g, pltpu.HBM) for g in gs], *lands)
    return (out[0], out[1], list(out[2:2 + n]), list(out[2 + n:2 + 2 * n])), out[-1]


def _sibling_wait(send_sems, recv_sems, gs, lands, after, tag):
    n = len(gs)

    def body(*refs):
        for cp in _sibling_copies(gs, refs[:n], refs[n:2 * n], refs[2 * n], refs[2 * n + 1]):
            cp.wait_send()
            cp.wait_recv()

    out = pl.pallas_call(
        body, name=f"sibling_wait_{tag}",
        out_shape=(*[pltpu.HBM(g.shape, g.dtype) for g in gs], *[pltpu.HBM(l.shape, l.dtype) for l in lands]),
        in_specs=[HBM] * (2 * n) + [SEM, SEM, ANY], out_specs=tuple([HBM] * (2 * n)),
        input_output_aliases={i: i for i in range(2 * n)},
        compiler_params=pltpu.CompilerParams(has_side_effects=DATAFLOW),
    )(*gs, *lands, send_sems, recv_sems, after)
    return list(out[:n]), list(out[n:])


def _chips_start(hs, tag):
    n = len(hs)
    lands = [pltpu.with_memory_space_constraint(lax.empty((N_CHIPS,) + h.shape[-2:], h.dtype), pltpu.HBM) for h in hs]

    def body(*refs):
        ins, lnd = refs[:n], refs[n:2 * n]
        send_sems, recv_sems, token = refs[2 * n], refs[2 * n + 1], refs[-1]
        x, y, c = _mesh_pos()
        k = 2 * x + y
        piece = lambda t, kk: ins[t].at[kk] if hs[t].ndim == 3 else ins[t]
        for t in range(n):
            for j, (px, py) in enumerate(_other_chips(x, y)):
                _remote(piece(t, 2 * px + py), lnd[t].at[k], send_sems.at[3 * t + j], recv_sems.at[3 * t + j], (px, py, c)).start()
        token[...] = jnp.zeros_like(token)

    out = pl.pallas_call(
        body, name=f"chips_start_{tag}",
        out_shape=(pltpu.SemaphoreType.DMA((3 * n,)), pltpu.SemaphoreType.DMA((3 * n,)),
                   *[pltpu.HBM(h.shape, h.dtype) for h in hs], *[pltpu.HBM(l.shape, l.dtype) for l in lands],
                   jax.ShapeDtypeStruct((SUBLANES, LANES), F32)),
        in_specs=[HBM] * (2 * n), out_specs=(SEM, SEM, *[HBM] * (2 * n), pl.BlockSpec(memory_space=pltpu.VMEM)),
        input_output_aliases={i: 2 + i for i in range(2 * n)},
        compiler_params=pltpu.CompilerParams(has_side_effects=DATAFLOW),
    )(*[pltpu.with_memory_space_constraint(h, pltpu.HBM) for h in hs], *lands)
    return out[0], out[1], list(out[2:2 + n]), list(out[2 + n:2 + 2 * n]), out[-1]


def _chips_wait(send_sems, recv_sems, hs, lands, after, tag):
    n = len(hs)

    def body(*refs):
        ins, lnd = refs[:n], refs[n:2 * n]
        ssem, rsem = refs[2 * n], refs[2 * n + 1]
        x, y, c = _mesh_pos()
        k = 2 * x + y
        piece = lambda t, kk: ins[t].at[kk] if hs[t].ndim == 3 else ins[t]
        for t in range(n):
            for j, (px, py) in enumerate(_other_chips(x, y)):
                cp = _remote(piece(t, k), lnd[t].at[2 * px + py], ssem.at[3 * t + j], rsem.at[3 * t + j], (px, py, c))
                cp.wait_send()
                cp.wait_recv()

    out = pl.pallas_call(
        body, name=f"chips_wait_{tag}",
        out_shape=(*[pltpu.HBM(h.shape, h.dtype) for h in hs], *[pltpu.HBM(l.shape, l.dtype) for l in lands]),
        in_specs=[HBM] * (2 * n) + [SEM, SEM, ANY], out_specs=tuple([HBM] * (2 * n)),
        input_output_aliases={i: i for i in range(2 * n)},
        compiler_params=pltpu.CompilerParams(has_side_effects=DATAFLOW),
    )(*hs, *lands, send_sems, recv_sems, after)
    return list(out[:n]), list(out[n:])


def _add_chips(p, own, kc, name):
    _, P, Q = p.shape
    tr = _row_tile(P, Q)
    sharded = own.ndim == 3

    def body(kc_ref, p_ref, own_ref, o_ref):
        k = kc_ref[0]
        mine = own_ref[...].reshape(tr, Q)
        v = [jnp.where(k == j, mine, p_ref[j]) for j in range(N_CHIPS)]
        o_ref[0] = ((v[0] + v[1]) + v[2]) + v[3]

    own_spec = (pl.BlockSpec((1, tr, Q), lambda i, kc_ref: (kc_ref[0], i, 0)) if sharded
                else pl.BlockSpec((tr, Q), lambda i, kc_ref: (i, 0)))
    return pl.pallas_call(
        body, name=name,
        grid_spec=pltpu.PrefetchScalarGridSpec(
            num_scalar_prefetch=1, grid=(P // tr,),
            in_specs=[pl.BlockSpec((N_CHIPS, tr, Q), lambda i, kc_ref: (0, i, 0)), own_spec],
            out_specs=pl.BlockSpec((1, tr, Q), lambda i, kc_ref: (kc_ref[1], i, 0))),
        out_shape=jax.ShapeDtypeStruct((2, P, Q), p.dtype),
        compiler_params=_cp(1),
    )(kc, p, own)


def _rs_share(fs, tag):
    n = len(fs)

    def body(*refs):
        outs = refs[n:2 * n]
        send_sems, recv_sems = refs[2 * n:]
        x, y, c = _mesh_pos()
        sends = [_remote(outs[t].at[c], outs[t].at[c], send_sems.at[t], recv_sems.at[t], (x, y, 1 - c)) for t in range(n)]
        for cp in sends:
            cp.start()
        for t in range(n):
            _remote(outs[t].at[c], outs[t].at[1 - c], send_sems.at[t], recv_sems.at[t], (x, y, 1 - c)).wait_recv()
        for cp in sends:
            cp.wait_send()

    return pl.pallas_call(
        body, name=f"rs_share_{tag}", in_specs=[ANY] * n, out_specs=[ANY] * n,
        out_shape=[jax.ShapeDtypeStruct(f.shape, f.dtype) for f in fs], input_output_aliases={t: t for t in range(n)},
        scratch_shapes=[pltpu.SemaphoreType.DMA((n,)), pltpu.SemaphoreType.DMA((n,))],
    )(*fs)


def _reduce_start(gs, kc, tag):
    return _reduce_continue(gs, _rs_send_sibling(gs, tag), kc, tag)


def _reduce_continue(gs, from_sibling, kc, tag):
    chip_sums = [_add_piece(g, r, kc[1:], name=f"add_piece_{tag}_{t}") for t, (g, r) in enumerate(zip(gs, from_sibling))]
    send_sems, recv_sems, chip_sums, lands, token = _chips_start(chip_sums, tag)
    return (send_sems, recv_sems, chip_sums, lands, tag), token


def _reduce_finish(states, kc, after):
    mine = []
    for send_sems, recv_sems, chip_sums, lands, tag in states:
        chip_sums, from_chips = _chips_wait(send_sems, recv_sems, chip_sums, lands, after, tag)
        mine += [_add_chips(p, h, kc, name=f"add_chips_{tag}_{t}") for t, (p, h) in enumerate(zip(from_chips, chip_sums))]
    return _rs_share(mine, "all")


def _adamw(w, g, m, v, name):
    R, C = w.shape
    tr = R
    for cand in (512, 256, 128, 64, 32, 16, 8):
        if R % cand == 0 and cand * C * 4 <= ELEMENTWISE_TILE_BYTES:
            tr = cand
            break
    c1 = 1.0 / (1.0 - ADAM_B1 ** ADAM_STEP)
    c2 = 1.0 / (1.0 - ADAM_B2 ** ADAM_STEP)

    def body(w_ref, g_ref, m_ref, v_ref, d_ref, mo_ref, vo_ref):
        gv = g_ref[...]
        mn = ADAM_B1 * m_ref[...] + (1.0 - ADAM_B1) * gv
        vn = ADAM_B2 * v_ref[...] + (1.0 - ADAM_B2) * (gv * gv)
        mo_ref[...] = mn
        vo_ref[...] = vn
        d_ref[...] = -ADAM_LR * ((mn * c1) / (jnp.sqrt(vn * c2) + ADAM_EPS) + ADAM_WD * w_ref[...])

    spec = pl.BlockSpec((tr, C), lambda i: (i, 0))
    shp = jax.ShapeDtypeStruct((R, C), F32)
    return pl.pallas_call(body, name=name, grid=(R // tr,), in_specs=[spec] * 4, out_specs=[spec] * 3, out_shape=[shp] * 3,
                          compiler_params=_cp(1))(w, g, m, v)


def _adamw_many(ws, gs, ms, vs):
    n = len(ws)
    c1 = 1.0 / (1.0 - ADAM_B1 ** ADAM_STEP)
    c2 = 1.0 / (1.0 - ADAM_B2 ** ADAM_STEP)

    def body(*refs):
        w_refs, g_refs, m_refs, v_refs = (refs[k * n:(k + 1) * n] for k in range(4))
        outs = refs[4 * n:]
        for i in range(n):
            gv = g_refs[i][...]
            mn = ADAM_B1 * m_refs[i][...] + (1.0 - ADAM_B1) * gv
            vn = ADAM_B2 * v_refs[i][...] + (1.0 - ADAM_B2) * (gv * gv)
            outs[3 * i][...] = -ADAM_LR * ((mn * c1) / (jnp.sqrt(vn * c2) + ADAM_EPS) + ADAM_WD * w_refs[i][...])
            outs[3 * i + 1][...] = mn
            outs[3 * i + 2][...] = vn

    out_shape = [jax.ShapeDtypeStruct(w.shape, F32) for w in ws for _ in range(3)]
    return pl.pallas_call(body, name="adamw_small", out_shape=out_shape)(*ws, *gs, *ms, *vs)


def _step(a):
    x, y, c = _mesh_pos()
    kc = jnp.stack([2 * x + y, c]).astype(jnp.int32)

    rs = _rows(sum(a[n].size for n in SH_SMALL))
    first, later = ["ev_w_in", "ev_w_out"], ["od_w_in", "od_w_out", "ffn_w_up", "ffn_w_down"]
    lead = lambda w: w if w.ndim == 3 else w[None]
    *full, gs = _gather_weights(first, [a[n].astype(BF16) for n in first], _pack([a[n] for n in SH_SMALL], rs, F32))
    p = {n: a[n] for n in REP}
    p.update({n: lead(w) for n, w in zip(first, full)})
    parts = [_unpack(gs[k], [a[n].shape for n in SH_SMALL]) for k in range(N_CHIPS)]
    for i, n in enumerate(SH_SMALL):
        p[n] = jnp.concatenate([parts[k][i] for k in range(N_CHIPS)], axis=SH_SMALL[n])
    shards = lax.optimization_barrier(([a[n].astype(BF16) for n in later], full))[0]
    g_send, g_recv, shards, bufs, token = _gather_start(later, shards, _place_own(later, shards))
    p["norm_mix"] = p["norm_mix"] + token[0:1, 0:1]

    def late(after):
        got = _gather_forward(later, _gather_wait(later, g_send, g_recv, shards, bufs, after))
        return {n: lead(w) for n, w in zip(later, got)}

    p["late"] = late

    states, pending = {}, {}

    def start_reduce(tag, gs, split=False):
        if split:
            pending[tag], token = _sibling_start(gs, tag)
        else:
            states[tag], token = _reduce_start(gs, kc, tag)
        return token

    def continue_reduce(tag, after):
        send_sems, recv_sems, gs, lands = pending.pop(tag)
        gs, from_sibling = _sibling_wait(send_sems, recv_sems, gs, lands, after, tag)
        states[tag], token = _reduce_continue(gs, from_sibling, kc, tag)
        return token

    sq8, grad_x, grads, big = _local_step(a["x"][0], a["loss_target"][0], p, start_reduce, continue_reduce)
    loss = lax.psum(0.5 / D_MODEL * jnp.sum(sq8), ("x", "y", "c"))

    r_s = _rows(sum(a[n].size for n in SH_SMALL), 2 * SUBLANES) // 2
    small_pieces = []
    for k in range(N_CHIPS):
        pieces = [lax.slice_in_dim(grads[n], k * a[n].shape[ax], (k + 1) * a[n].shape[ax], axis=ax) for n, ax in SH_SMALL.items()]
        small_pieces.append(_pack(pieces, 2 * r_s, F32).reshape(2, r_s, LANES))
    g_small = jnp.stack(small_pieces)
    r_r = _rows(sum(a[n].size for n in REP), 2 * SUBLANES) // 2
    g_rep = _pack([grads[n] for n in REP], 2 * r_r, F32).reshape(2, r_r, LANES)
    token = start_reduce("g4", [g_small, g_rep])
    reduced = _reduce_finish([states[tag] for tag in ("g1", "g2", "g3", "g4")], kc, token)
    red = dict(zip([key for tag in ("g1", "g2", "g3") for key in REDUCE_GROUPS[tag]], reduced))
    gfin = {}
    for n in ("ev_w_in", "ev_w_out", "od_w_in", "od_w_out"):
        gfin[n] = red[n, 0].reshape(a[n].shape)
    for n in ("ffn_w_up", "ffn_w_down"):
        gfin[n] = jnp.stack([red[n, l].reshape(a[n].shape[1:]) for l in range(2)])
    gfin.update(zip(SH_SMALL, _unpack(reduced[-2], [a[n].shape for n in SH_SMALL])))
    gfin.update(zip(REP, _unpack(reduced[-1], [a[n].shape for n in REP])))

    out = {"loss": loss, "grad_x": grad_x[None]}
    small_names = list(SH_SMALL) + REP
    for n in SH_BIG:
        shp = a[n].shape
        two_d = lambda t: t.reshape(-1, shp[-1])
        d, mo, vo = _adamw(two_d(a[n]), two_d(gfin[n]), two_d(a["m_" + n]), two_d(a["v_" + n]), name=f"adamw_{n}")
        out["delta_" + n], out["new_m_" + n], out["new_v_" + n] = d.reshape(shp), mo.reshape(shp), vo.reshape(shp)
    two_d = lambda t: t.reshape(-1, t.shape[-1])
    res = _adamw_many(*[[two_d(src(n)) for n in small_names]
                        for src in (lambda n: a[n], lambda n: gfin[n], lambda n: a["m_" + n], lambda n: a["v_" + n])])
    for i, n in enumerate(small_names):
        out["delta_" + n], out["new_m_" + n], out["new_v_" + n] = (r.reshape(a[n].shape) for r in res[3 * i:3 * i + 3])
    for n in WEIGHTS:
        out["grad_" + n] = gfin[n]
    return out


def kernel(x, norm_mix, norm_ffn, norm_final, ev_w_in, ev_conv_w, ev_conv_b, ev_gate_a_w, ev_gate_a_b, ev_gate_x_w, ev_gate_x_b, ev_lru_lambda, hg_lb_logits, ev_hg_norm, ev_w_out, od_w_in, od_b_in, od_ln_g, od_ln_b, od_w_s, od_b_s, od_w_out, ffn_w_up, ffn_conv_w, ffn_conv_b, ffn_w_down, loss_target, m_norm_mix, m_norm_ffn, m_norm_final, m_ev_w_in, m_ev_conv_w, m_ev_conv_b, m_ev_gate_a_w, m_ev_gate_a_b, m_ev_gate_x_w, m_ev_gate_x_b, m_ev_lru_lambda, m_hg_lb_logits, m_ev_hg_norm, m_ev_w_out, m_od_w_in, m_od_b_in, m_od_ln_g, m_od_ln_b, m_od_w_s, m_od_b_s, m_od_w_out, m_ffn_w_up, m_ffn_conv_w, m_ffn_conv_b, m_ffn_w_down, v_norm_mix, v_norm_ffn, v_norm_final, v_ev_w_in, v_ev_conv_w, v_ev_conv_b, v_ev_gate_a_w, v_ev_gate_a_b, v_ev_gate_x_w, v_ev_gate_x_b, v_ev_lru_lambda, v_hg_lb_logits, v_ev_hg_norm, v_ev_w_out, v_od_w_in, v_od_b_in, v_od_ln_g, v_od_ln_b, v_od_w_s, v_od_b_s, v_od_w_out, v_ffn_w_up, v_ffn_conv_w, v_ffn_conv_b, v_ffn_w_down):
    vals = (x, norm_mix, norm_ffn, norm_final, ev_w_in, ev_conv_w, ev_conv_b, ev_gate_a_w, ev_gate_a_b, ev_gate_x_w, ev_gate_x_b, ev_lru_lambda, hg_lb_logits, ev_hg_norm, ev_w_out, od_w_in, od_b_in, od_ln_g, od_ln_b, od_w_s, od_b_s, od_w_out, ffn_w_up, ffn_conv_w, ffn_conv_b, ffn_w_down, loss_target, m_norm_mix, m_norm_ffn, m_norm_final, m_ev_w_in, m_ev_conv_w, m_ev_conv_b, m_ev_gate_a_w, m_ev_gate_a_b, m_ev_gate_x_w, m_ev_gate_x_b, m_ev_lru_lambda, m_hg_lb_logits, m_ev_hg_norm, m_ev_w_out, m_od_w_in, m_od_b_in, m_od_ln_g, m_od_ln_b, m_od_w_s, m_od_b_s, m_od_w_out, m_ffn_w_up, m_ffn_conv_w, m_ffn_conv_b, m_ffn_w_down, v_norm_mix, v_norm_ffn, v_norm_final, v_ev_w_in, v_ev_conv_w, v_ev_conv_b, v_ev_gate_a_w, v_ev_gate_a_b, v_ev_gate_x_w, v_ev_gate_x_b, v_ev_lru_lambda, v_hg_lb_logits, v_ev_hg_norm, v_ev_w_out, v_od_w_in, v_od_b_in, v_od_ln_g, v_od_ln_b, v_od_w_s, v_od_b_s, v_od_w_out, v_ffn_w_up, v_ffn_conv_w, v_ffn_conv_b, v_ffn_w_down)
    names = ["x"] + WEIGHTS + ["loss_target"] + ["m_" + n for n in WEIGHTS] + ["v_" + n for n in WEIGHTS]
    out = _step(dict(zip(names, vals)))
    return (out["loss"], out["grad_x"], *[out["grad_" + n] for n in WEIGHTS], *[out["delta_" + n] for n in WEIGHTS],
            *[out["new_m_" + n] for n in WEIGHTS], *[out["new_v_" + n] for n in WEIGHTS])
```

```python
import jax
import jax.numpy as jnp
from jax import lax
from jax.experimental import pallas as pl
from jax.experimental.pallas import tpu as pltpu

F32 = jnp.float32
BF16 = jnp.bfloat16

EPS = 1e-6
D_MODEL = 1024
LRU_W = 512
LRU_BLOCKS = 8
LRU_C = 8.0
HG_HEADS = 4
HG_D = 128
HG_CHUNK = 64
SGU_G = 8
SGU_CHUNK = 128
D_FF = 2816
ADAM_LR, ADAM_B1, ADAM_B2, ADAM_EPS, ADAM_WD, ADAM_STEP = 0.001, 0.9, 0.999, 1e-08, 0.01, 10

V7X_VMEM_BYTES = 64 * 1024 * 1024
VMEM_LIMIT = V7X_VMEM_BYTES - 8 * 1024 * 1024
SUBLANES = 8
LANES = 128
BF16_ROWS = 16
ELEMENTWISE_TILE_BYTES = 2 * 1024 * 1024

N_CHIPS = 4
MESH = pl.DeviceIdType.MESH
ANY = pl.BlockSpec(memory_space=pl.ANY)
HBM = pl.BlockSpec(memory_space=pltpu.HBM)
SEM = pl.BlockSpec(memory_space=pltpu.SEMAPHORE)
DATAFLOW = pltpu.SideEffectType.DATAFLOW_SIDE_EFFECTING

GELU_C0 = 0.7978845608028654
GELU_C1 = 0.044715

NN = (((1,), (0,)), ((), ()))
NT = (((1,), (1,)), ((), ()))
TN = (((0,), (0,)), ((), ()))


def _dot(a, b, dims=NN):
    return lax.dot_general(a.astype(BF16), b.astype(BF16), dims, preferred_element_type=F32)


def _cp(n_grid):
    return pltpu.CompilerParams(dimension_semantics=("arbitrary",) * n_grid, vmem_limit_bytes=VMEM_LIMIT)


def _chunk(n, cap):
    best = LANES
    for c in range(LANES, cap + 1, LANES):
        if n % c == 0:
            best = c
    return best


def _resident(shape):
    nd = len(shape)
    return pl.BlockSpec(shape, lambda *_: (0,) * nd, pipeline_mode=pl.Buffered(1))


def _rsum8(x):
    r, c = x.shape
    return x.reshape(r // SUBLANES, SUBLANES, c).sum(axis=0)


def _sigmoid(x):
    return 0.5 * jnp.tanh(0.5 * x) + 0.5


def _gelu(x):
    return 0.5 * x * (1.0 + jnp.tanh(GELU_C0 * (x + GELU_C1 * x * x * x)))


def _gelu_grad(x):
    t = jnp.tanh(GELU_C0 * (x + GELU_C1 * x * x * x))
    return 0.5 * (1.0 + t) + 0.5 * x * (1.0 - t * t) * GELU_C0 * (1.0 + 3.0 * GELU_C1 * x * x)


def _silu_and_grad(x):
    s = _sigmoid(x)
    return x * s, s * (1.0 + x * (1.0 - s))


def _shift_rows(e, j):
    n = e.shape[0]
    return e if j % n == 0 else pltpu.roll(e, j % n, 0)


def _weight(w):
    if isinstance(w, tuple):
        stack, layer = w
        K, N = stack.shape[1:]
        return stack, pl.BlockSpec((None, K, N), lambda *_: (layer, 0, 0), pipeline_mode=pl.Buffered(1)), (K, N)
    return w, _resident(w.shape), w.shape


def _norm_mm(h, g, w, b, name, tt=1024):
    T, D = h.shape
    w, w_spec, (_, N) = _weight(w)
    cn = _chunk(N, 512)

    def body(h_ref, g_ref, w_ref, b_ref, hn_ref, z_ref):
        x = h_ref[...]
        r = lax.rsqrt(jnp.mean(x * x, axis=-1, keepdims=True) + EPS)
        hn = (x * r * g_ref[...]).astype(BF16)
        hn_ref[...] = hn
        for j in range(0, N, cn):
            acc = jnp.dot(hn, w_ref[:, j:j + cn], preferred_element_type=F32) + b_ref[:, j:j + cn]
            z_ref[:, j:j + cn] = acc.astype(BF16)

    return pl.pallas_call(
        body, name=name, grid=(T // tt,),
        in_specs=[pl.BlockSpec((tt, D), lambda i: (i, 0)), _resident((1, D)), w_spec, _resident((1, N))],
        out_specs=[pl.BlockSpec((tt, D), lambda i: (i, 0)), pl.BlockSpec((tt, N), lambda i: (i, 0))],
        out_shape=[jax.ShapeDtypeStruct((T, D), BF16), jax.ShapeDtypeStruct((T, N), BF16)],
        compiler_params=_cp(1),
    )(h, g, w, b)


def _mm(a, w, res, out_dtype, name, tt=1024, transpose_w=False):
    T, K = a.shape
    N = w.shape[0] if transpose_w else w.shape[1]
    cn = _chunk(N, 512)
    has_res = res is not None

    def body(*refs):
        a_ref, w_ref = refs[0], refs[1]
        res_ref = refs[2] if has_res else None
        o_ref = refs[-1]
        av = a_ref[...].astype(BF16)
        for j in range(0, N, cn):
            if transpose_w:
                acc = lax.dot_general(av, w_ref[j:j + cn, :], NT, preferred_element_type=F32)
            else:
                acc = jnp.dot(av, w_ref[:, j:j + cn], preferred_element_type=F32)
            if has_res:
                acc = acc + res_ref[:, j:j + cn]
            o_ref[:, j:j + cn] = acc.astype(out_dtype)

    in_specs = [pl.BlockSpec((tt, K), lambda i: (i, 0)), _resident(w.shape)]
    args = [a, w]
    if has_res:
        in_specs.append(pl.BlockSpec((tt, N), lambda i: (i, 0)))
        args.append(res)
    return pl.pallas_call(
        body, name=name, grid=(T // tt,), in_specs=in_specs,
        out_specs=pl.BlockSpec((tt, N), lambda i: (i, 0)),
        out_shape=jax.ShapeDtypeStruct((T, N), out_dtype),
        compiler_params=_cp(1),
    )(*args)


def _mm_tn(a, b, name, col_shards=1, tt=2048):
    T, K = a.shape
    N = b.shape[1]
    ns = N // col_shards
    tt = min(tt, T)
    while 2 * (tt * K * a.dtype.itemsize + tt * ns * b.dtype.itemsize + K * ns * 4) + K * ns * 4 > VMEM_LIMIT:
        tt //= 2

    def body(a_ref, b_ref, o_ref):
        acc = lax.dot_general(a_ref[...].astype(BF16), b_ref[...].astype(BF16), TN, preferred_element_type=F32)
        prev = jnp.where(pl.program_id(1) == 0, 0.0, o_ref[0])
        o_ref[0] = prev + acc

    out = pl.pallas_call(
        body, name=name, grid=(col_shards, T // tt),
        in_specs=[pl.BlockSpec((tt, K), lambda n, t: (t, 0)), pl.BlockSpec((tt, ns), lambda n, t: (t, n))],
        out_specs=pl.BlockSpec((1, K, ns), lambda n, t: (n, 0, 0)),
        out_shape=jax.ShapeDtypeStruct((col_shards, K, ns), F32),
        compiler_params=_cp(2),
    )(a, b)
    return out if col_shards > 1 else out[0]


def _mm_normbwd(dz, w, x, g, dres, name):
    T, N = dz.shape
    D = w.shape[0]
    tt = 1024 if N <= 3072 else 512

    def body(dz_ref, wt_ref, x_ref, g_ref, dres_ref, dx_ref, dg_ref):
        @pl.when(pl.program_id(0) == 0)
        def _():
            dg_ref[...] = jnp.zeros_like(dg_ref)

        dy = lax.dot_general(dz_ref[...], wt_ref[...], NT, preferred_element_type=F32)
        x = x_ref[...]
        r = lax.rsqrt(jnp.mean(x * x, axis=-1, keepdims=True) + EPS)
        xn = x * r
        dg_ref[...] += _rsum8(dy * xn)
        dxn = dy * g_ref[...]
        dx_ref[...] = dres_ref[...] + r * (dxn - xn * jnp.mean(dxn * xn, axis=-1, keepdims=True))

    return pl.pallas_call(
        body, name=name, grid=(T // tt,),
        in_specs=[pl.BlockSpec((tt, N), lambda i: (i, 0)), _resident((D, N)), pl.BlockSpec((tt, D), lambda i: (i, 0)),
                  _resident((1, D)), pl.BlockSpec((tt, D), lambda i: (i, 0))],
        out_specs=[pl.BlockSpec((tt, D), lambda i: (i, 0)), pl.BlockSpec((SUBLANES, D), lambda i: (0, 0))],
        out_shape=[jax.ShapeDtypeStruct((T, D), F32), jax.ShapeDtypeStruct((SUBLANES, D), F32)],
        compiler_params=_cp(1),
    )(dz, w, x, g, dres)


def _col_groups(F, cc, per_group=4):
    step = cc * per_group
    return [(g0, min(g0 + step, F)) for g0 in range(0, F, step)]


def _loss_head(x, gv, tgt):
    r = lax.rsqrt(jnp.mean(x * x, axis=-1, keepdims=True) + EPS)
    xn = x * r
    diff = xn * gv - tgt
    dy = diff * (1.0 / x.shape[-1])
    dxn = dy * gv
    return r * (dxn - xn * jnp.mean(dxn * xn, axis=-1, keepdims=True)), _rsum8(diff * diff), _rsum8(dy * xn)


def _ffn_act(gu, cw, cb, w_down, res, name, loss=None, tt=512):
    T = gu.shape[0]
    F = gu.shape[1] // 2
    w_down, wd_spec, (_, D) = _weight(w_down)
    cc = _chunk(F, 256)
    hb = tt // BF16_ROWS

    def body(gu_ref, halo_ref, cw_ref, cb_ref, wd_ref, res_ref, *rest):
        if loss is None:
            a_ref, gc_ref, o_ref = rest
        else:
            gf_ref, t_ref, a_ref, gc_ref, o_ref, sq_ref, dgf_ref = rest
        first = pl.program_id(0) == 0
        acc = res_ref[...]
        for g0, g1 in _col_groups(F, cc):
            for c0 in range(g0, g1, cc):
                cs = slice(c0, c0 + cc)
                x = gu_ref[:, cs].astype(F32)
                halo = jnp.where(first, 0.0, halo_ref[:, cs].astype(F32))
                e = jnp.concatenate([halo, x], axis=0)
                gc = (cb_ref[:, cs] + cw_ref[0:1, cs] * _shift_rows(e, 2)[BF16_ROWS:] + cw_ref[1:2, cs] * _shift_rows(e, 1)[BF16_ROWS:]
                      + cw_ref[2:3, cs] * x)
                up = gu_ref[:, F + c0:F + c0 + cc].astype(F32)
                gc_ref[:, cs] = gc.astype(BF16)
                a_ref[:, cs] = (gc * _sigmoid(gc) * up).astype(BF16)
            acc = acc + jnp.dot(a_ref[:, g0:g1], wd_ref[g0:g1, :], preferred_element_type=F32)
        if loss is None:
            o_ref[...] = acc
        else:
            @pl.when(first)
            def _():
                sq_ref[...] = jnp.zeros_like(sq_ref)
                dgf_ref[...] = jnp.zeros_like(dgf_ref)

            dx, sq, dgf = _loss_head(acc, gf_ref[...], t_ref[...])
            o_ref[...] = dx
            sq_ref[...] += sq
            dgf_ref[...] += dgf

    tok = lambda w: pl.BlockSpec((tt, w), lambda i: (i, 0))
    acc8 = pl.BlockSpec((SUBLANES, D), lambda i: (0, 0))
    in_specs = [tok(2 * F), pl.BlockSpec((BF16_ROWS, F), lambda i: (jnp.maximum(i * hb - 1, 0), 0)),
                _resident((SUBLANES, F)), _resident((1, F)), wd_spec, tok(D)]
    out_specs = [tok(F), tok(F), tok(D)]
    out_shape = [jax.ShapeDtypeStruct((T, F), BF16), jax.ShapeDtypeStruct((T, F), BF16), jax.ShapeDtypeStruct((T, D), F32)]
    args = [gu, gu, cw, cb, w_down, res]
    if loss is not None:
        in_specs += [_resident((1, D)), tok(D)]
        out_specs += [acc8, acc8]
        out_shape += [jax.ShapeDtypeStruct((SUBLANES, D), F32)] * 2
        args += list(loss)
    return pl.pallas_call(body, name=name, grid=(T // tt,), in_specs=in_specs, out_specs=out_specs, out_shape=out_shape,
                          compiler_params=_cp(1))(*args)


def _ffn_act_bwd(gu, gc, cw, w_up, w_down, x, g, dh, name, tt=256):
    T = gu.shape[0]
    F = gu.shape[1] // 2
    w_up, wu_spec, (D, _) = _weight(w_up)
    w_down, wd_spec, _ = _weight(w_down)
    cc = _chunk(F, 256)
    hb = tt // BF16_ROWS
    last_hb = T // BF16_ROWS - 1
    nt = T // tt

    def body(gu_ref, upnext_ref, gc_ref, gcnext_ref, cw_ref, wu_ref, wd_ref, x_ref, g_ref, dh_ref, dhnext_ref,
             dgu_ref, dc_ref, dx_ref, dg_ref):
        i = pl.program_id(0)

        @pl.when(i == 0)
        def _():
            dc_ref[...] = jnp.zeros_like(dc_ref)
            dg_ref[...] = jnp.zeros_like(dg_ref)

        n = tt + BF16_ROWS
        ext = lambda main, nxt: jnp.concatenate([main.astype(F32), nxt.astype(F32)], axis=0)
        dhe = ext(dh_ref[...], jnp.where(i == nt - 1, 0.0, dhnext_ref[...])).astype(BF16)
        dy = jnp.zeros((tt, D), F32)
        groups = _col_groups(F, cc)
        da_of = lambda grp: lax.dot_general(dhe, wd_ref[grp[0]:grp[1], :], NT, preferred_element_type=F32)
        da_next = da_of(groups[0])
        for gi, (lo, hi) in enumerate(groups):
            da, da_next = da_next, (da_of(groups[gi + 1]) if gi + 1 < len(groups) else None)
            for c0 in range(lo, hi, cc):
                cs = slice(c0, c0 + cc)
                us = slice(F + c0, F + c0 + cc)
                gc = ext(gc_ref[:, cs], gcnext_ref[:, cs])
                up = ext(gu_ref[:, us], upnext_ref[:, cs])
                dae = da[:, c0 - lo:c0 - lo + cc]
                s, ds = _silu_and_grad(gc)
                dgc = dae * up * ds
                dgu_ref[:, us] = (dae * s)[:tt].astype(BF16)
                dgc1 = _shift_rows(dgc, n - 1)[:tt]
                dgc2 = _shift_rows(dgc, n - 2)[:tt]
                dm = dgc[:tt]
                dgu_ref[:, cs] = (cw_ref[2:3, cs] * dm + cw_ref[1:2, cs] * dgc1 + cw_ref[0:1, cs] * dgc2).astype(BF16)
                gt = gu_ref[:, cs].astype(F32)
                dc_ref[0, :, cs] += _rsum8(dgc2 * gt)
                dc_ref[1, :, cs] += _rsum8(dgc1 * gt)
                dc_ref[2, :, cs] += _rsum8(dm * gt)
                dc_ref[3, :, cs] += _rsum8(dm)
            dy = (dy + lax.dot_general(dgu_ref[:, lo:hi], wu_ref[:, lo:hi], NT, preferred_element_type=F32)
                  + lax.dot_general(dgu_ref[:, F + lo:F + hi], wu_ref[:, F + lo:F + hi], NT, preferred_element_type=F32))
        xv = x_ref[...]
        r = lax.rsqrt(jnp.mean(xv * xv, axis=-1, keepdims=True) + EPS)
        xn = xv * r
        dg_ref[...] += _rsum8(dy * xn)
        dxn = dy * g_ref[...]
        dx_ref[...] = dh_ref[...] + r * (dxn - xn * jnp.mean(dxn * xn, axis=-1, keepdims=True))

    tok = lambda w: pl.BlockSpec((tt, w), lambda i: (i, 0))
    nxt = lambda w, col: pl.BlockSpec((BF16_ROWS, w), lambda i: (jnp.minimum((i + 1) * hb, last_hb), col))
    return pl.pallas_call(
        body, name=name, grid=(nt,),
        in_specs=[tok(2 * F), nxt(F, 1), tok(F), nxt(F, 0),
                  _resident((SUBLANES, F)), wu_spec, wd_spec, tok(D), _resident((1, D)), tok(D), nxt(D, 0)],
        out_specs=[tok(2 * F), pl.BlockSpec((4, SUBLANES, F), lambda i: (0, 0, 0)), tok(D),
                   pl.BlockSpec((SUBLANES, D), lambda i: (0, 0))],
        out_shape=[jax.ShapeDtypeStruct((T, 2 * F), BF16), jax.ShapeDtypeStruct((4, SUBLANES, F), F32),
                   jax.ShapeDtypeStruct((T, D), F32), jax.ShapeDtypeStruct((SUBLANES, D), F32)],
        compiler_params=_cp(1),
    )(gu, gu, gc, gc, cw, w_up, w_down, x, g, dh, dh)


def _softplus_neg(lam):
    x = -lam
    y = jnp.exp(-jnp.abs(x))
    l1p = jnp.where(y < 0.01, y * (1.0 - y * (0.5 - y * (1.0 / 3.0))), jnp.log(1.0 + y))
    return jnp.maximum(x, 0.0) + l1p


def _lru_gates(xc, wa_ref, ba_ref, wx_ref, bx_ref, sp):
    xcb = xc.astype(BF16)
    r = _sigmoid(jnp.dot(xcb, wa_ref[...], preferred_element_type=F32) + ba_ref[...])
    gi = _sigmoid(jnp.dot(xcb, wx_ref[...], preferred_element_type=F32) + bx_ref[...])
    log_a = -LRU_C * r * sp
    a = jnp.exp(log_a)
    x2 = 2.0 * log_a
    series = -x2 * (1.0 + x2 * 0.5 * (1.0 + x2 * (1.0 / 3.0)))
    om = jnp.where(x2 > -0.02, series, 1.0 - a * a)
    return r, gi, a, jnp.sqrt(om)


def _lru_conv(xr, halo, cw_ref, cb_ref):
    e = jnp.concatenate([halo, xr], axis=0)
    x1, x2, x3 = (_shift_rows(e, j)[BF16_ROWS:] for j in (1, 2, 3))
    return cb_ref[...] + cw_ref[0:1, :] * x3 + cw_ref[1:2, :] * x2 + cw_ref[2:3, :] * x1 + cw_ref[3:4, :] * xr


def _lru_fwd(z, cw, cb, wa, ba, wx, bx, lam, name="lru_fwd", tt=512):
    T = z.shape[0]
    W = LRU_W
    hb = tt // BF16_ROWS
    ng = tt // SUBLANES

    def body(z_ref, halo_ref, cw_ref, cb_ref, wa_ref, ba_ref, wx_ref, bx_ref, lam_ref, oa_ref, h_ref, a_s, sv_ref, u_s, hc):
        i = pl.program_id(0)

        @pl.when(i == 0)
        def _():
            hc[...] = jnp.zeros_like(hc)

        xr = z_ref[:, W:2 * W].astype(F32)
        halo = jnp.where(i == 0, 0.0, halo_ref[...].astype(F32))
        xc = _lru_conv(xr, halo, cw_ref, cb_ref)
        sp = _softplus_neg(lam_ref[...])
        r, gi, a, mult = _lru_gates(xc, wa_ref, ba_ref, wx_ref, bx_ref, sp)
        a_s[...] = a
        u_s[...] = mult * gi * xc
        for k, saved in enumerate((mult, r, gi, xc)):
            sv_ref[:, k * W:(k + 1) * W] = saved.astype(BF16)
        row = lax.broadcasted_iota(jnp.int32, (SUBLANES, W), 0)

        def step(j, hprev):
            r0 = pl.multiple_of(j * SUBLANES, SUBLANES)
            A = a_s[pl.ds(r0, SUBLANES), :]
            U = u_s[pl.ds(r0, SUBLANES), :]
            for k in (1, 2, 4):
                m = row >= k
                U = jnp.where(m, A * pltpu.roll(U, k, 0) + U, U)
                A = jnp.where(m, A * pltpu.roll(A, k, 0), A)
            H = U + A * hprev
            h_ref[pl.ds(r0, SUBLANES), :] = H
            return jnp.broadcast_to(H[SUBLANES - 1:SUBLANES, :], (SUBLANES, W))

        hc[...] = lax.fori_loop(0, ng, step, hc[...])
        oa_ref[...] = (_gelu(z_ref[:, 0:W].astype(F32)) * h_ref[...]).astype(BF16)

    return pl.pallas_call(
        body, name=name, grid=(T // tt,),
        in_specs=[pl.BlockSpec((tt, 2 * W), lambda i: (i, 0)),
                  pl.BlockSpec((BF16_ROWS, W), lambda i: (jnp.maximum(i * hb - 1, 0), 1)),
                  _resident((SUBLANES, W)), _resident((1, W)), _resident((W, W)), _resident((1, W)),
                  _resident((W, W)), _resident((1, W)), _resident((1, W))],
        out_specs=[pl.BlockSpec((tt, W), lambda i: (i, 0)), pl.BlockSpec((tt, W), lambda i: (i, 0)),
                   pl.BlockSpec((tt, W), lambda i: (i, 0)), pl.BlockSpec((tt, 4 * W), lambda i: (i, 0))],
        out_shape=[jax.ShapeDtypeStruct((T, 2 * W), BF16), jax.ShapeDtypeStruct((T, W), F32),
                   jax.ShapeDtypeStruct((T, W), F32), jax.ShapeDtypeStruct((T, 4 * W), BF16)],
        scratch_shapes=[pltpu.VMEM((tt, W), F32), pltpu.VMEM((SUBLANES, W), F32)],
        compiler_params=_cp(1),
    )(z, z, cw, cb, wa, ba, wx, bx, lam)


def _lru_bwd(z, hseq, a_all, saved, dmix, cw, wat, wxt, lam, name="lru_bwd", tt=512):
    T = z.shape[0]
    W = LRU_W
    nt = T // tt
    sb = tt // SUBLANES
    ng = tt // SUBLANES

    def body(z_ref, h_ref, hprev_ref, a_ref, sv_ref, dm_ref, cw_ref, wat_ref, wxt_ref, lam_ref,
             dz_ref, dc_ref, dwa_ref, dwx_ref, dv_ref, c_s, d_s, g_s, gc, an, dxn):
        i = pl.program_id(0)
        ti = nt - 1 - i

        @pl.when(i == 0)
        def _():
            dc_ref[...] = jnp.zeros_like(dc_ref)
            dwa_ref[...] = jnp.zeros_like(dwa_ref)
            dwx_ref[...] = jnp.zeros_like(dwx_ref)
            dv_ref[...] = jnp.zeros_like(dv_ref)
            gc[...] = jnp.zeros_like(gc)
            an[...] = jnp.zeros_like(an)
            dxn[...] = jnp.zeros_like(dxn)

        xr = z_ref[:, W:2 * W].astype(F32)
        yg = z_ref[:, 0:W].astype(F32)
        sp = _softplus_neg(lam_ref[...])
        a = a_ref[...]
        mult, r, gi, xc = (sv_ref[:, k * W:(k + 1) * W].astype(F32) for k in range(4))
        h = h_ref[...]
        hp = jnp.where(ti == 0, 0.0, hprev_ref[...])
        hm1 = _shift_rows(jnp.concatenate([hp, h], axis=0), 1)[SUBLANES:]
        dout = dm_ref[...].astype(F32)
        d_s[...] = dout * _gelu(yg)
        dz_ref[:, 0:W] = (dout * h * _gelu_grad(yg)).astype(BF16)
        c_s[...] = _shift_rows(jnp.concatenate([a, an[...]], axis=0), tt + SUBLANES - 1)[:tt]
        an[...] = a[0:SUBLANES, :]
        row = lax.broadcasted_iota(jnp.int32, (SUBLANES, W), 0)

        def step(j, gnext):
            r0 = pl.multiple_of((ng - 1 - j) * SUBLANES, SUBLANES)
            C = c_s[pl.ds(r0, SUBLANES), :]
            G = d_s[pl.ds(r0, SUBLANES), :]
            for k in (1, 2, 4):
                m = row < SUBLANES - k
                G = jnp.where(m, G + C * pltpu.roll(G, SUBLANES - k, 0), G)
                C = jnp.where(m, C * pltpu.roll(C, SUBLANES - k, 0), C)
            G = G + C * gnext
            g_s[pl.ds(r0, SUBLANES), :] = G
            return jnp.broadcast_to(G[0:1, :], (SUBLANES, W))

        gc[...] = lax.fori_loop(0, ng, step, gc[...])
        du = g_s[...]
        da = du * hm1
        dgi = du * mult * xc
        dxc = du * mult * gi
        dmult = du * gi * xc
        dlog_a = da * a - dmult * (a * a) / mult
        dr = dlog_a * (-LRU_C * sp)
        dv_ref[2] += _rsum8(dlog_a * (-LRU_C * r))
        dpr = (dr * r * (1.0 - r)).astype(BF16)
        dpi = (dgi * gi * (1.0 - gi)).astype(BF16)
        dv_ref[0] += _rsum8(dpr.astype(F32))
        dv_ref[1] += _rsum8(dpi.astype(F32))
        xcb = sv_ref[:, 3 * W:4 * W]
        dwa_ref[...] += lax.dot_general(xcb, dpr, TN, preferred_element_type=F32)
        dwx_ref[...] += lax.dot_general(xcb, dpi, TN, preferred_element_type=F32)
        dxc = dxc + jnp.dot(dpr, wat_ref[...], preferred_element_type=F32) + jnp.dot(dpi, wxt_ref[...], preferred_element_type=F32)
        n = tt + BF16_ROWS
        de = jnp.concatenate([dxc, dxn[...]], axis=0)
        d1, d2, d3 = (_shift_rows(de, n - j)[:tt] for j in (1, 2, 3))
        dxn[...] = dxc[0:BF16_ROWS, :]
        dz_ref[:, W:2 * W] = (cw_ref[3:4, :] * dxc + cw_ref[2:3, :] * d1 + cw_ref[1:2, :] * d2 + cw_ref[0:1, :] * d3).astype(BF16)
        dc_ref[0] += _rsum8(d3 * xr)
        dc_ref[1] += _rsum8(d2 * xr)
        dc_ref[2] += _rsum8(d1 * xr)
        dc_ref[3] += _rsum8(dxc * xr)
        dc_ref[4] += _rsum8(dxc)

    rev = lambda i: nt - 1 - i
    tok = lambda w: pl.BlockSpec((tt, w), lambda i: (rev(i), 0))
    return pl.pallas_call(
        body, name=name, grid=(nt,),
        in_specs=[tok(2 * W), tok(W),
                  pl.BlockSpec((SUBLANES, W), lambda i: (jnp.maximum(rev(i) * sb - 1, 0), 0)),
                  tok(W), tok(4 * W), tok(W),
                  _resident((SUBLANES, W)), _resident((W, W)), _resident((W, W)), _resident((1, W))],
        out_specs=[pl.BlockSpec((tt, 2 * W), lambda i: (rev(i), 0)),
                   pl.BlockSpec((5, SUBLANES, W), lambda i: (0, 0, 0)),
                   pl.BlockSpec((W, W), lambda i: (0, 0)), pl.BlockSpec((W, W), lambda i: (0, 0)),
                   pl.BlockSpec((3, SUBLANES, W), lambda i: (0, 0, 0))],
        out_shape=[jax.ShapeDtypeStruct((T, z.shape[1]), BF16), jax.ShapeDtypeStruct((5, SUBLANES, W), F32),
                   jax.ShapeDtypeStruct((W, W), F32), jax.ShapeDtypeStruct((W, W), F32),
                   jax.ShapeDtypeStruct((3, SUBLANES, W), F32)],
        scratch_shapes=[pltpu.VMEM((tt, W), F32), pltpu.VMEM((tt, W), F32), pltpu.VMEM((tt, W), F32),
                        pltpu.VMEM((SUBLANES, W), F32), pltpu.VMEM((SUBLANES, W), F32), pltpu.VMEM((BF16_ROWS, W), F32)],
        compiler_params=_cp(1),
    )(z, hseq, hseq, a_all, saved, dmix, cw, wat, wxt, lam)


def _split3(x):
    hi = x.astype(BF16)
    r1 = x - hi.astype(F32)
    mid = r1.astype(BF16)
    lo = (r1 - mid.astype(F32)).astype(BF16)
    return hi, mid, lo


def _tri_matmul(tri, x):
    hi, mid, lo = _split3(x)
    return (jnp.dot(tri, hi, preferred_element_type=F32) + jnp.dot(tri, mid, preferred_element_type=F32)
            + jnp.dot(tri, lo, preferred_element_type=F32))


def _hg_chunk(q, fl, lb):
    C = q.shape[0]
    ri = lax.broadcasted_iota(jnp.int32, (C, C), 0)
    ci = lax.broadcasted_iota(jnp.int32, (C, C), 1)
    causal = ri >= ci
    sig = _sigmoid(fl)
    f = lb + (1.0 - lb) * sig
    k = 1.0 - f
    sq = _sigmoid(q)
    qf = q * sq
    b = _tri_matmul(causal.astype(BF16), jnp.log(f))
    bm = b[C // 2 - 1:C // 2, :]
    bl = b[C - 1:C, :]
    e_qt, e_kt, e_in, e_out = jnp.exp(b - bm), jnp.exp(bm - b), jnp.exp(b), jnp.exp(bl - b)
    qt = qf * e_qt
    kt = k * e_kt
    qin = qf * e_in
    kout = k * e_out
    qtb, ktb = qt.astype(BF16), kt.astype(BF16)
    att = [jnp.where(causal, _dot(qtb[:, _head(h)], ktb[:, _head(h)], NT), 0.0).astype(BF16) for h in range(HG_HEADS)]
    return dict(sig=sig, f=f, k=k, sq=sq, qf=qf, b=b, bm=bm, bl=bl, qt=qt, kt=kt, qin=qin, kout=kout, att=att,
                causal=causal, anti=ri <= ci, decay=jnp.exp(bl), e_qt=e_qt, e_kt=e_kt, e_in=e_in, e_out=e_out)


def _head(h):
    return slice(h * HG_D, (h + 1) * HG_D)


def _hgrn_fwd(z, lb, gn, mix, name="hgrn_fwd", tt=512):
    T = z.shape[0]
    C = HG_CHUNK
    nc = tt // C
    Dh = HG_D
    Wd = HG_HEADS * Dh

    def body(q_ref, f_ref, v_ref, g_ref, lb_ref, gn_ref, mix_ref, o_ref, ss_ref, st):
        del mix_ref

        @pl.when(pl.program_id(0) == 0)
        def _():
            st[...] = jnp.zeros_like(st)

        S = [st[h] for h in range(HG_HEADS)]
        for c in range(nc):
            rows = slice(c * C, (c + 1) * C)
            ck = _hg_chunk(q_ref[rows, :].astype(F32), f_ref[rows, :].astype(F32), lb_ref[...])
            v = v_ref[rows, :]
            g = g_ref[rows, :].astype(F32)
            H = range(HG_HEADS)
            qinb, koutb = ck["qin"].astype(BF16), ck["kout"].astype(BF16)
            for h in H:
                ss_ref[h, c] = S[h]
            o = [_dot(ck["att"][h], v[:, _head(h)]) + _dot(qinb[:, _head(h)], S[h], NT) for h in H]
            S = [ck["decay"][:, _head(h)] * S[h] + _dot(v[:, _head(h)], koutb[:, _head(h)], TN) for h in H]
            outs = [o[h] * lax.rsqrt(jnp.mean(o[h] * o[h], axis=-1, keepdims=True) + EPS) * gn_ref[...] for h in H]
            o_ref[rows, :] = (jnp.concatenate(outs, axis=1) * (g * _sigmoid(g))).astype(BF16)
        for h in range(HG_HEADS):
            st[h] = S[h]

    col = lambda base: (lambda i: (i, base))
    return pl.pallas_call(
        body, name=name, grid=(T // tt,),
        in_specs=[pl.BlockSpec((tt, Wd), col(2)), pl.BlockSpec((tt, Wd), col(3)), pl.BlockSpec((tt, Wd), col(4)),
                  pl.BlockSpec((tt, Wd), col(5)), _resident((1, Wd)), _resident((1, Dh)), ANY],
        out_specs=[pl.BlockSpec((tt, Wd), lambda i: (i, 1)),
                   pl.BlockSpec((HG_HEADS, nc, Dh, Dh), lambda i: (0, i, 0, 0))],
        out_shape=[jax.ShapeDtypeStruct((T, 2 * Wd), BF16),
                   jax.ShapeDtypeStruct((HG_HEADS, T // C, Dh, Dh), F32)],
        scratch_shapes=[pltpu.VMEM((HG_HEADS, Dh, Dh), F32)],
        compiler_params=_cp(1),
        input_output_aliases={6: 0},
    )(z, z, z, z, lb, gn, mix)


def _hgrn_bwd(z, ss, dmix, lb, gn, dz, name="hgrn_bwd", tt=512):
    T = z.shape[0]
    C = HG_CHUNK
    nc = tt // C
    nt = T // tt
    Dh = HG_D
    Wd = HG_HEADS * Dh

    def body(q_ref, f_ref, v_ref, g_ref, ss_ref, dm_ref, lb_ref, gn_ref, dz01_ref, dz_ref, dlb_ref, dgn_ref, dst):
        @pl.when(pl.program_id(0) == 0)
        def _():
            dst[...] = jnp.zeros_like(dst)
            dlb_ref[...] = jnp.zeros_like(dlb_ref)
            dgn_ref[...] = jnp.zeros_like(dgn_ref)

        dS = [dst[h] for h in range(HG_HEADS)]
        lbv = lb_ref[...]
        gnv = gn_ref[...]
        rowc = lax.broadcasted_iota(jnp.int32, (C, Wd), 0)
        cat = lambda xs: jnp.concatenate(xs, axis=1)
        for c in reversed(range(nc)):
            rows = slice(c * C, (c + 1) * C)
            q = q_ref[rows, :].astype(F32)
            ck = _hg_chunk(q, f_ref[rows, :].astype(F32), lbv)
            v = v_ref[rows, :]
            g = g_ref[rows, :].astype(F32)
            dout = dm_ref[rows, :].astype(F32)
            sg, dsg = _silu_and_grad(g)
            d_ong = dout * sg
            H = range(HG_HEADS)
            qinb, koutb, qtb, ktb = (ck[n].astype(BF16) for n in ("qin", "kout", "qt", "kt"))
            S = [ss_ref[h, c] for h in H]
            Sb = [s.astype(BF16) for s in S]
            dSb = [d.astype(BF16) for d in dS]
            o = [_dot(ck["att"][h], v[:, _head(h)]) + _dot(qinb[:, _head(h)], Sb[h], NT) for h in H]
            rn = [lax.rsqrt(jnp.mean(o[h] * o[h], axis=-1, keepdims=True) + EPS) for h in H]
            on = [o[h] * rn[h] for h in H]
            don = [d_ong[:, _head(h)] * gnv for h in H]
            do = [(rn[h] * (don[h] - on[h] * jnp.mean(don[h] * on[h], axis=-1, keepdims=True))).astype(BF16) for h in H]
            datt = [jnp.where(ck["causal"], _dot(do[h], v[:, _head(h)], NT), 0.0).astype(BF16) for h in H]
            dvs = [_dot(ck["att"][h], do[h], TN) + _dot(koutb[:, _head(h)], dSb[h], NT) for h in H]
            dqins = [_dot(do[h], Sb[h]) for h in H]
            dkouts = [_dot(v[:, _head(h)], dSb[h]) for h in H]
            dqts = [_dot(datt[h], ktb[:, _head(h)]) for h in H]
            dkts = [_dot(datt[h], qtb[:, _head(h)], TN) for h in H]
            ddecays = [jnp.sum(dS[h] * S[h], axis=0, keepdims=True) for h in H]
            dS = [_dot(do[h], qinb[:, _head(h)], TN) + ck["decay"][:, _head(h)] * dS[h] for h in H]
            ons = [on[h] * gnv for h in H]
            dgn = _rsum8(d_ong[:, _head(0)] * on[0])
            for h in range(1, HG_HEADS):
                dgn = dgn + _rsum8(d_ong[:, _head(h)] * on[h])
            dgn_ref[...] += dgn
            dqt, dkt, dqin, dkout, ddecay = cat(dqts), cat(dkts), cat(dqins), cat(dkouts), cat(ddecays)
            dqf = dqt * ck["e_qt"] + dqin * ck["e_in"]
            dk = dkt * ck["e_kt"] + dkout * ck["e_out"]
            kk = dkout * ck["kout"]
            db = dqt * ck["qt"] - dkt * ck["kt"] + dqin * ck["qin"] - kk
            dbl = jnp.sum(kk, axis=0, keepdims=True) + ddecay * ck["decay"]
            db = db + jnp.where(rowc == C - 1, dbl, 0.0)
            dlogf = _tri_matmul(ck["anti"].astype(BF16), db)
            dfv = dlogf / ck["f"] - dk
            sig, sq = ck["sig"], ck["sq"]
            dlb_ref[...] += _rsum8(dfv * (1.0 - sig))
            dz_ref[rows, 2 * Wd:3 * Wd] = (dqf * (sq * (1.0 + q * (1.0 - sq)))).astype(BF16)
            dz_ref[rows, 3 * Wd:4 * Wd] = (dfv * (1.0 - lbv) * sig * (1.0 - sig)).astype(BF16)
            dz_ref[rows, 4 * Wd:5 * Wd] = cat(dvs).astype(BF16)
            dz_ref[rows, 5 * Wd:6 * Wd] = (dout * cat(ons) * dsg).astype(BF16)
        dz_ref[:, 0:2 * Wd] = dz01_ref[...]
        for h in range(HG_HEADS):
            dst[h] = dS[h]

    rev = lambda i: nt - 1 - i
    col = lambda base: (lambda i: (rev(i), base))
    return pl.pallas_call(
        body, name=name, grid=(nt,),
        in_specs=[pl.BlockSpec((tt, Wd), col(2)), pl.BlockSpec((tt, Wd), col(3)), pl.BlockSpec((tt, Wd), col(4)),
                  pl.BlockSpec((tt, Wd), col(5)),
                  pl.BlockSpec((HG_HEADS, nc, Dh, Dh), lambda i: (0, rev(i), 0, 0)),
                  pl.BlockSpec((tt, Wd), col(1)), _resident((1, Wd)), _resident((1, Dh)),
                  pl.BlockSpec((tt, 2 * Wd), col(0))],
        out_specs=[pl.BlockSpec((tt, 6 * Wd), lambda i: (rev(i), 0)), pl.BlockSpec((SUBLANES, Wd), lambda i: (0, 0)),
                   pl.BlockSpec((SUBLANES, Dh), lambda i: (0, 0))],
        out_shape=[jax.ShapeDtypeStruct((T, 6 * Wd), BF16), jax.ShapeDtypeStruct((SUBLANES, Wd), F32),
                   jax.ShapeDtypeStruct((SUBLANES, Dh), F32)],
        input_output_aliases={8: 0},
        scratch_shapes=[pltpu.VMEM((HG_HEADS, Dh, Dh), F32)],
        compiler_params=_cp(1),
    )(z, z, z, z, ss, dmix, lb, gn, dz)


def _sgu_core(p, lg_ref, lb_ref, wsc_ref, bsb_ref):
    Wd = D_MODEL
    G = SGU_CHUNK
    zz = _gelu(p)
    u = zz[:, :Wd]
    v = zz[:, Wd:]
    vc = v - jnp.mean(v, axis=-1, keepdims=True)
    rstd = lax.rsqrt(jnp.mean(vc * vc, axis=-1, keepdims=True) + EPS)
    vhat = vc * rstd
    vn = vhat * lg_ref[...] + lb_ref[...]
    svs = []
    for gi in range(SGU_G):
        svs.append(jnp.dot(wsc_ref[gi], vn[:, gi * G:(gi + 1) * G].astype(BF16), preferred_element_type=F32) + bsb_ref[gi])
    return u, vhat, rstd, vn, jnp.concatenate(svs, axis=1)


def _sgu_fwd(p1, lg, lbias, wsc, bsb, name="sgu_fwd", tt=512):
    T = p1.shape[0]
    Wd = D_MODEL
    C = SGU_CHUNK

    def body(p_ref, lg_ref, lb_ref, wsc_ref, bsb_ref, s_ref):
        for c in range(tt // C):
            rows = slice(c * C, (c + 1) * C)
            u, _, _, _, sv = _sgu_core(p_ref[rows, :].astype(F32), lg_ref, lb_ref, wsc_ref, bsb_ref)
            s_ref[rows, :] = (u * sv).astype(BF16)

    return pl.pallas_call(
        body, name=name, grid=(T // tt,),
        in_specs=[pl.BlockSpec((tt, 2 * Wd), lambda i: (i, 0)), _resident((1, Wd)), _resident((1, Wd)),
                  _resident((SGU_G, C, C)), _resident((SGU_G, C, C))],
        out_specs=pl.BlockSpec((tt, Wd), lambda i: (i, 0)),
        out_shape=jax.ShapeDtypeStruct((T, Wd), BF16),
        compiler_params=_cp(1),
    )(p1, lg, lbias, wsc, bsb)


def _sgu_bwd(p1, ds, lg, lbias, wsc, wsct, bsb, name="sgu_bwd", tt=512):
    T = p1.shape[0]
    Wd = D_MODEL
    C = SGU_CHUNK

    def body(p_ref, ds_ref, lg_ref, lb_ref, wsc_ref, wsct_ref, bsb_ref, dp_ref, dws_ref, dbs_ref, dlg_ref, dlb_ref, dbin_ref):
        @pl.when(pl.program_id(0) == 0)
        def _():
            dws_ref[...] = jnp.zeros_like(dws_ref)
            dbs_ref[...] = jnp.zeros_like(dbs_ref)
            dlg_ref[...] = jnp.zeros_like(dlg_ref)
            dlb_ref[...] = jnp.zeros_like(dlb_ref)
            dbin_ref[...] = jnp.zeros_like(dbin_ref)

        for c in range(tt // C):
            rows = slice(c * C, (c + 1) * C)
            p = p_ref[rows, :].astype(F32)
            u, vhat, rstd, vn, sv = _sgu_core(p, lg_ref, lb_ref, wsc_ref, bsb_ref)
            dsc = ds_ref[rows, :].astype(F32)
            du = dsc * sv
            dsv = dsc * u
            dvns = []
            for gi in range(SGU_G):
                cs = slice(gi * C, (gi + 1) * C)
                dsv_g = dsv[:, cs]
                dvns.append(jnp.dot(wsct_ref[gi], dsv_g.astype(BF16), preferred_element_type=F32))
                dws_ref[gi] += _dot(dsv_g, vn[:, cs], NT)
                dbs_ref[gi] += dsv_g
            dvn = jnp.concatenate(dvns, axis=1)
            dlg_ref[...] += _rsum8(dvn * vhat)
            dlb_ref[...] += _rsum8(dvn)
            dvh = dvn * lg_ref[...]
            dv = rstd * (dvh - jnp.mean(dvh, axis=-1, keepdims=True) - vhat * jnp.mean(dvh * vhat, axis=-1, keepdims=True))
            dp = jnp.concatenate([du, dv], axis=1) * _gelu_grad(p)
            dbin_ref[...] += _rsum8(dp)
            dp_ref[rows, :] = dp.astype(BF16)

    full3 = pl.BlockSpec((SGU_G, C, C), lambda i: (0, 0, 0))
    return pl.pallas_call(
        body, name=name, grid=(T // tt,),
        in_specs=[pl.BlockSpec((tt, 2 * Wd), lambda i: (i, 0)), pl.BlockSpec((tt, Wd), lambda i: (i, 0)),
                  _resident((1, Wd)), _resident((1, Wd)), _resident((SGU_G, C, C)), _resident((SGU_G, C, C)),
                  _resident((SGU_G, C, C))],
        out_specs=[pl.BlockSpec((tt, 2 * Wd), lambda i: (i, 0)), full3, full3,
                   pl.BlockSpec((SUBLANES, Wd), lambda i: (0, 0)), pl.BlockSpec((SUBLANES, Wd), lambda i: (0, 0)),
                   pl.BlockSpec((SUBLANES, 2 * Wd), lambda i: (0, 0))],
        out_shape=[jax.ShapeDtypeStruct((T, 2 * Wd), BF16), jax.ShapeDtypeStruct((SGU_G, C, C), F32),
                   jax.ShapeDtypeStruct((SGU_G, C, C), F32), jax.ShapeDtypeStruct((SUBLANES, Wd), F32),
                   jax.ShapeDtypeStruct((SUBLANES, Wd), F32), jax.ShapeDtypeStruct((SUBLANES, 2 * Wd), F32)],
        compiler_params=_cp(1),
    )(p1, ds, lg, lbias, wsc, wsct, bsb)


def _pad_rows(w, rows=SUBLANES):
    return jnp.pad(w, ((0, rows - w.shape[0]), (0, 0)))


def _block_diag(w):
    n, b, _ = w.shape
    return (w[:, :, None, :] * jnp.eye(n, dtype=w.dtype)[:, None, :, None]).reshape(n * b, n * b)


def _diag_blocks(m, n):
    b = m.shape[0] // n
    m4 = m.reshape(n, b, n, b)
    return jnp.stack([m4[k, :, k, :] for k in range(n)], axis=0)


def _piece_major(dw):
    if dw.ndim == 2:
        K, N = dw.shape
        return dw.reshape(N_CHIPS, 2, K // (2 * N_CHIPS), N)
    _, K, ns = dw.shape
    return dw.reshape(N_CHIPS, 2, K // 2, ns)


def _ffn_fwd(h, g, w_up, cw, cb, w_down, tag, loss=None):
    hn, gu = _norm_mm(h, g, w_up, jnp.zeros((1, 2 * D_FF), F32), name=f"ffn_up_{tag}")
    a, gc, *out = _ffn_act(gu, cw, cb, w_down, h, name=f"ffn_act_down_{tag}", loss=loss)
    return (out[0] if loss is None else out), (hn, gu, gc, a)


def _ffn_bwd(dh, h, g, saved, w_up, cw, w_down, tag):
    hn, gu, gc, a = saved
    dwd = _mm_tn(a, dh, name=f"ffn_dwd_{tag}")
    dgu, dc, dhin, dg8 = _ffn_act_bwd(gu, gc, cw, w_up, w_down, h, g, dh, name=f"ffn_bwd_{tag}")
    dwu = _mm_tn(hn, dgu, name=f"ffn_dwu_{tag}", col_shards=N_CHIPS)
    dcs = dc.sum(axis=1)
    return dhin, dg8.sum(axis=0), dwu, dcs[0:3], dcs[3], dwd


REDUCE_GROUPS = {"g1": [("ffn_w_up", 1), ("ffn_w_down", 1), ("od_w_out", 0), ("od_w_in", 0)],
                 "g2": [("ffn_w_up", 0), ("ffn_w_down", 0)],
                 "g3": [("ev_w_out", 0), ("ev_w_in", 0)]}


def _local_step(x, tgt, p, start_reduce=None, continue_reduce=None):
    row = lambda v: v.reshape(1, -1)
    grads = {}

    lower = jax.nn.softmax(p["hg_lb_logits"], axis=0)
    lb0 = row(lower[0])
    ev_cw = _pad_rows(p["ev_conv_w"][0])
    ev_cb = row(p["ev_conv_b"][0])
    wa = _block_diag(p["ev_gate_a_w"][0]).astype(BF16)
    wx = _block_diag(p["ev_gate_x_w"][0]).astype(BF16)
    ba, bx, lam = row(p["ev_gate_a_b"][0]), row(p["ev_gate_x_b"][0]), row(p["ev_lru_lambda"][0])
    gn = row(p["ev_hg_norm"][0])
    tril = jnp.tril(jnp.ones((SGU_CHUNK, SGU_CHUNK), F32))
    wsc = (p["od_w_s"][0] * tril).astype(BF16)
    bsb = jnp.broadcast_to(p["od_b_s"][0][:, :, None], (SGU_G, SGU_CHUNK, SGU_CHUNK)).astype(F32)
    ffn_cw = [_pad_rows(p["ffn_conv_w"][l]) for l in range(2)]
    ffn_cb = [row(p["ffn_conv_b"][l]) for l in range(2)]
    ev_w_in, ev_w_out = p["ev_w_in"][0], p["ev_w_out"][0]
    nm = [row(p["norm_mix"][l]) for l in range(2)]
    nf = [row(p["norm_ffn"][l]) for l in range(2)]

    h0 = x
    hn0, z0 = _norm_mm(h0, nm[0], ev_w_in, jnp.zeros((1, ev_w_in.shape[1]), F32), name="ev_in")
    out_a, hseq, a_all, lru_saved = _lru_fwd(z0, ev_cw, ev_cb, wa, ba, wx, bx, lam)
    mix0, ss = _hgrn_fwd(z0, lb0, gn, out_a)
    h1 = _mm(mix0, ev_w_out, h0, F32, name="ev_out")
    late = p["late"](h1) if "late" in p else p
    od_w_in, od_w_out = late["od_w_in"][0], late["od_w_out"][0]
    w_up = [(late["ffn_w_up"], l) for l in range(2)]
    w_down = [(late["ffn_w_down"], l) for l in range(2)]
    h2, ffn0 = _ffn_fwd(h1, nf[0], w_up[0], ffn_cw[0], ffn_cb[0], w_down[0], "l0")
    hn1, p1 = _norm_mm(h2, nm[1], od_w_in, row(p["od_b_in"][0]), name="od_in")
    s1 = _sgu_fwd(p1, row(p["od_ln_g"][0]), row(p["od_ln_b"][0]), wsc, bsb)
    h3 = _mm(s1, od_w_out, h2, F32, name="od_out")
    (dh4, sq8, dgf8), ffn1 = _ffn_fwd(h3, nf[1], w_up[1], ffn_cw[1], ffn_cb[1], w_down[1], "l1", loss=(row(p["norm_final"]), tgt))
    grads["norm_final"] = dgf8.sum(axis=0)

    big = {}
    dh3, dnf1, dwu1, dcw1, dcb1, dwd1 = _ffn_bwd(dh4, h3, nf[1], ffn1, w_up[1], ffn_cw[1], w_down[1], "l1")
    big["ffn_w_up", 1], big["ffn_w_down", 1] = _piece_major(dwu1), _piece_major(dwd1)
    ds1 = _mm(dh3, od_w_out, None, BF16, name="od_ds", transpose_w=True)
    big["od_w_out", 0] = _piece_major(_mm_tn(s1, dh3, name="od_dwo"))
    wsct = jnp.swapaxes(wsc, 1, 2)
    dp1, dws, dbs, dlg8, dlb8, dbin8 = _sgu_bwd(p1, ds1, row(p["od_ln_g"][0]), row(p["od_ln_b"][0]), wsc, wsct, bsb)
    grads["od_w_s"] = (dws * tril)[None]
    grads["od_b_s"] = dbs.sum(axis=-1)[None]
    grads["od_ln_g"] = dlg8.sum(axis=0)[None]
    grads["od_ln_b"] = dlb8.sum(axis=0)[None]
    grads["od_b_in"] = dbin8.sum(axis=0)[None]
    dh2, dnm1 = _mm_normbwd(dp1, od_w_in, h2, nm[1], dh3, name="od_dh")
    big["od_w_in", 0] = _piece_major(_mm_tn(hn1, dp1, name="od_dwi", col_shards=N_CHIPS))
    if start_reduce is not None:
        g1, dh2 = lax.optimization_barrier(([big[key] for key in REDUCE_GROUPS["g1"]], dh2))
        token = start_reduce("g1", g1, split=True)
        ffn_cw[0] = ffn_cw[0] + token[0:1, 0:1]

    dh1, dnf0, dwu0, dcw0, dcb0, dwd0 = _ffn_bwd(dh2, h1, nf[0], ffn0, w_up[0], ffn_cw[0], w_down[0], "l0")
    big["ffn_w_up", 0], big["ffn_w_down", 0] = _piece_major(dwu0), _piece_major(dwd0)
    if start_reduce is not None:
        token = continue_reduce("g1", dh1) + start_reduce("g2", [big[key] for key in REDUCE_GROUPS["g2"]], split=True)
        lam = lam + token[0:1, 0:1]
    dmix = _mm(dh1, ev_w_out, None, BF16, name="ev_dmix", transpose_w=True)
    big["ev_w_out", 0] = _piece_major(_mm_tn(mix0, dh1, name="ev_dwo"))
    dz01, dc5, dwa, dwx, dvec = _lru_bwd(z0, hseq, a_all, lru_saved, dmix, ev_cw, wa.T, wx.T, lam)
    if start_reduce is not None:
        lb0 = lb0 + continue_reduce("g2", dc5)[0:1, 0:1]
    dz0, dlb8, dgn8 = _hgrn_bwd(z0, ss, dmix, lb0, gn, dz01)
    big["ev_w_in", 0] = _piece_major(_mm_tn(hn0, dz0, name="ev_dwi", col_shards=N_CHIPS))
    if start_reduce is not None:
        token = start_reduce("g3", [big[key] for key in REDUCE_GROUPS["g3"]])
        nm[0] = nm[0] + token[0:1, 0:1]
    grad_x, dnm0 = _mm_normbwd(dz0, ev_w_in, h0, nm[0], dh1, name="ev_dh")

    dc5s = dc5.sum(axis=1)
    grads["ev_conv_w"] = dc5s[0:4][None]
    grads["ev_conv_b"] = dc5s[4][None]
    grads["ev_gate_a_w"] = _diag_blocks(dwa, LRU_BLOCKS)[None]
    grads["ev_gate_x_w"] = _diag_blocks(dwx, LRU_BLOCKS)[None]
    dvs = dvec.sum(axis=1)
    grads["ev_gate_a_b"] = dvs[0][None]
    grads["ev_gate_x_b"] = dvs[1][None]
    grads["ev_lru_lambda"] = (dvs[2] * (-jax.nn.sigmoid(-p["ev_lru_lambda"][0])))[None]
    dlb = dlb8.sum(axis=0)
    grads["hg_lb_logits"] = dlb[None, :] * lower[0][None, :] * (jnp.eye(3, dtype=F32)[0][:, None] - lower)
    grads["ev_hg_norm"] = dgn8.sum(axis=0)[None]
    grads["norm_mix"] = jnp.stack([dnm0.sum(axis=0), dnm1.sum(axis=0)])
    grads["norm_ffn"] = jnp.stack([dnf0, dnf1])
    grads["ffn_conv_w"] = jnp.stack([dcw0, dcw1])
    grads["ffn_conv_b"] = jnp.stack([dcb0, dcb1])
    return sq8, grad_x, grads, big


SH_BIG = {"ev_w_in": 2, "ev_w_out": 1, "od_w_in": 2, "od_w_out": 1, "ffn_w_up": 2, "ffn_w_down": 1}
SH_SMALL = {"ev_conv_w": 2, "od_b_in": 1, "od_ln_g": 1, "od_ln_b": 1, "ffn_conv_w": 2}
REP = ["norm_mix", "norm_ffn", "norm_final", "ev_conv_b", "ev_gate_a_w", "ev_gate_a_b", "ev_gate_x_w", "ev_gate_x_b",
       "ev_lru_lambda", "hg_lb_logits", "ev_hg_norm", "od_w_s", "od_b_s", "ffn_conv_b"]
WEIGHTS = ["norm_mix", "norm_ffn", "norm_final", "ev_w_in", "ev_conv_w", "ev_conv_b", "ev_gate_a_w", "ev_gate_a_b", "ev_gate_x_w",
           "ev_gate_x_b", "ev_lru_lambda", "hg_lb_logits", "ev_hg_norm", "ev_w_out", "od_w_in", "od_b_in", "od_ln_g", "od_ln_b",
           "od_w_s", "od_b_s", "od_w_out", "ffn_w_up", "ffn_conv_w", "ffn_conv_b", "ffn_w_down"]


def _rows(n_elems, mult=SUBLANES):
    r = -(-n_elems // LANES)
    return -(-r // mult) * mult


def _pack(arrs, rows, dtype):
    flat = jnp.concatenate([a.reshape(-1).astype(dtype) for a in arrs])
    return jnp.pad(flat, (0, rows * LANES - flat.shape[0])).reshape(rows, LANES)


def _unpack(flat2d, shapes):
    flat = flat2d.reshape(-1)
    out, off = [], 0
    for s in shapes:
        n = 1
        for d in s:
            n *= d
        out.append(flat[off:off + n].reshape(s))
        off += n
    return out


def _mesh_pos():
    return lax.axis_index("x"), lax.axis_index("y"), lax.axis_index("c")


def _other_chips(x, y):
    return [(1 - x, y), (x, 1 - y), (1 - x, 1 - y)]


def _half_rows(n):
    return lambda r, c: r.at[0, pl.ds(c * (n // 2), n // 2), :]


GATHER_BIG = {
    "ev_w_in": ((1024, 3072), _half_rows(1024), lambda o, k, c: o.at[pl.ds(c * 512, 512), pl.ds(k * 768, 768)]),
    "ev_w_out": ((1024, 1024), _half_rows(256), lambda o, k, c: o.at[pl.ds(k * 256 + c * 128, 128), :]),
    "od_w_in": ((1024, 2048), _half_rows(1024), lambda o, k, c: o.at[pl.ds(c * 512, 512), pl.ds(k * 512, 512)]),
    "od_w_out": ((1024, 1024), _half_rows(256), lambda o, k, c: o.at[pl.ds(k * 256 + c * 128, 128), :]),
    "ffn_w_up": ((2, 1024, 2 * D_FF), lambda r, c: r.at[c], lambda o, k, c: o.at[c, :, pl.ds(k * (2 * D_FF // 4), 2 * D_FF // 4)]),
    "ffn_w_down": ((2, D_FF, 1024), lambda r, c: r.at[c], lambda o, k, c: o.at[c, pl.ds(k * (D_FF // 4), D_FF // 4), :]),
}


def _gather_weights(names, big, small):
    nb = len(big)
    descs = [GATHER_BIG[n] for n in names]
    rs = small.shape[0]

    def body(*refs):
        ins, s_ref = refs[:nb], refs[nb]
        outs, os_ref = refs[nb + 1:2 * nb + 1], refs[2 * nb + 1]
        ici_send, ici_recv, d2d_send, d2d_recv, loc_sems = refs[2 * nb + 2:2 * nb + 7]
        vbufs = refs[2 * nb + 7:]
        x, y, c = _mesh_pos()
        k = 2 * x + y
        chips = _other_chips(x, y)
        sib = (x, y, 1 - c)

        def remote(src, dst, ssem, rsem, to):
            return pltpu.make_async_remote_copy(src_ref=src, dst_ref=dst, send_sem=ssem, recv_sem=rsem, device_id=to,
                                                device_id_type=MESH)

        stage = [pltpu.make_async_copy(ins[t], vbufs[t], loc_sems.at[2 * t]) for t in range(nb)]
        stage.append(pltpu.make_async_copy(s_ref, vbufs[nb], loc_sems.at[2 * nb]))
        for cp in stage:
            cp.start()
        sends = []
        for t, (_, src, dst) in enumerate(descs):
            for j, (px, py) in enumerate(chips):
                sends.append(remote(src(ins[t], c), dst(outs[t], k, c), ici_send.at[3 * t + j], ici_recv.at[3 * t + j], (px, py, c)))
        for j, (px, py) in enumerate(chips):
            sends.append(remote(s_ref, os_ref.at[k], ici_send.at[3 * nb + j], ici_recv.at[3 * nb + j], (px, py, c)))
        for cp in sends:
            cp.start()
        for cp in stage:
            cp.wait()
        local = []
        for t, (_, src, dst) in enumerate(descs):
            for cc in (0, 1):
                local.append(pltpu.make_async_copy(src(vbufs[t], cc), dst(outs[t], k, cc), loc_sems.at[2 * t + cc]))
        local.append(pltpu.make_async_copy(vbufs[nb], os_ref.at[k], loc_sems.at[2 * nb]))
        for cp in local:
            cp.start()
        for t, (_, src, dst) in enumerate(descs):
            for j, (px, py) in enumerate(chips):
                got = dst(outs[t], 2 * px + py, c)
                remote(got, got, ici_send.at[3 * t + j], ici_recv.at[3 * t + j], (px, py, c)).wait_recv()
                fwd = remote(got, got, d2d_send.at[3 * t + j], d2d_recv.at[3 * t + j], sib)
                fwd.start()
                sends.append(fwd)
        for j, (px, py) in enumerate(chips):
            remote(s_ref, os_ref.at[2 * px + py], ici_send.at[3 * nb + j], ici_recv.at[3 * nb + j], (px, py, c)).wait_recv()
        for t, (_, src, dst) in enumerate(descs):
            for j, (px, py) in enumerate(chips):
                theirs = dst(outs[t], 2 * px + py, 1 - c)
                remote(theirs, theirs, d2d_send.at[3 * t + j], d2d_recv.at[3 * t + j], sib).wait_recv()
        for cp in sends:
            cp.wait_send()
        for cp in local:
            cp.wait()

    out_shape = [jax.ShapeDtypeStruct(d[0], BF16) for d in descs] + [jax.ShapeDtypeStruct((N_CHIPS, rs, LANES), small.dtype)]
    return pl.pallas_call(
        body, name="gather_weights", in_specs=[ANY] * (nb + 1), out_specs=[ANY] * (nb + 1), out_shape=out_shape,
        scratch_shapes=[pltpu.SemaphoreType.DMA((3 * nb + 3,)), pltpu.SemaphoreType.DMA((3 * nb + 3,)),
                        pltpu.SemaphoreType.DMA((3 * nb,)), pltpu.SemaphoreType.DMA((3 * nb,)),
                        pltpu.SemaphoreType.DMA((2 * nb + 1,))]
        + [pltpu.VMEM(b.shape, b.dtype) for b in big] + [pltpu.VMEM(small.shape, small.dtype)],
        compiler_params=pltpu.CompilerParams(vmem_limit_bytes=VMEM_LIMIT),
    )(*big, small)


def _place_own(names, big):
    nb = len(big)
    descs = [GATHER_BIG[n] for n in names]

    def body(*refs):
        ins, outs = refs[:nb], refs[nb:2 * nb]
        sems, vbufs = refs[2 * nb], refs[2 * nb + 1:]
        x, y, c = _mesh_pos()
        k = 2 * x + y
        stage = [pltpu.make_async_copy(ins[t], vbufs[t], sems.at[2 * t]) for t in range(nb)]
        for cp in stage:
            cp.start()
        for cp in stage:
            cp.wait()
        local = [pltpu.make_async_copy(src(vbufs[t], cc), dst(outs[t], k, cc), sems.at[2 * t + cc])
                 for t, (_, src, dst) in enumerate(descs) for cc in (0, 1)]
        for cp in local:
            cp.start()
        for cp in local:
            cp.wait()

    return pl.pallas_call(
        body, name="place_own", in_specs=[ANY] * nb, out_specs=[ANY] * nb,
        out_shape=[jax.ShapeDtypeStruct(d[0], BF16) for d in descs],
        scratch_shapes=[pltpu.SemaphoreType.DMA((2 * nb,))] + [pltpu.VMEM(b.shape, b.dtype) for b in big],
        compiler_params=pltpu.CompilerParams(vmem_limit_bytes=VMEM_LIMIT),
    )(*big)


def _gather_start(names, big, bufs):
    nb = len(big)
    descs = [GATHER_BIG[n] for n in names]

    def body(*refs):
        ins, lnd = refs[:nb], refs[nb:2 * nb]
        send_sems, recv_sems, token = refs[2 * nb], refs[2 * nb + 1], refs[-1]
        x, y, c = _mesh_pos()
        k = 2 * x + y
        for t, (_, src, dst) in enumerate(descs):
            for j, (px, py) in enumerate(_other_chips(x, y)):
                _remote(src(ins[t], c), dst(lnd[t], k, c), send_sems.at[3 * t + j], recv_sems.at[3 * t + j], (px, py, c)).start()
        token[...] = jnp.zeros_like(token)

    out = pl.pallas_call(
        body, name="gather_start",
        out_shape=(pltpu.SemaphoreType.DMA((3 * nb,)), pltpu.SemaphoreType.DMA((3 * nb,)),
                   *[pltpu.HBM(b.shape, b.dtype) for b in big], *[pltpu.HBM(b.shape, b.dtype) for b in bufs],
                   jax.ShapeDtypeStruct((SUBLANES, LANES), F32)),
        in_specs=[HBM] * (2 * nb), out_specs=(SEM, SEM, *[HBM] * (2 * nb), pl.BlockSpec(memory_space=pltpu.VMEM)),
        input_output_aliases={i: 2 + i for i in range(2 * nb)},
        compiler_params=pltpu.CompilerParams(has_side_effects=DATAFLOW),
    )(*[pltpu.with_memory_space_constraint(b, pltpu.HBM) for b in big], *[pltpu.with_memory_space_constraint(b, pltpu.HBM) for b in bufs])
    return out[0], out[1], list(out[2:2 + nb]), list(out[2 + nb:2 + 2 * nb]), out[-1]


def _gather_wait(names, send_sems, recv_sems, big, bufs, after):
    nb = len(big)
    descs = [GATHER_BIG[n] for n in names]

    def body(*refs):
        ins, lnd = refs[:nb], refs[nb:2 * nb]
        ssem, rsem = refs[2 * nb], refs[2 * nb + 1]
        x, y, c = _mesh_pos()
        for t, (_, src, dst) in enumerate(descs):
            for j, (px, py) in enumerate(_other_chips(x, y)):
                cp = _remote(src(ins[t], c), dst(lnd[t], 2 * px + py, c), ssem.at[3 * t + j], rsem.at[3 * t + j], (px, py, c))
                cp.wait_send()
                cp.wait_recv()

    out = pl.pallas_call(
        body, name="gather_wait",
        out_shape=(*[pltpu.HBM(b.shape, b.dtype) for b in big], *[pltpu.HBM(b.shape, b.dtype) for b in bufs]),
        in_specs=[HBM] * (2 * nb) + [SEM, SEM, ANY], out_specs=tuple([HBM] * (2 * nb)),
        input_output_aliases={i: i for i in range(2 * nb)},
        compiler_params=pltpu.CompilerParams(has_side_effects=DATAFLOW),
    )(*big, *bufs, send_sems, recv_sems, after)
    return list(out[nb:])


def _gather_forward(names, bufs):
    nb = len(bufs)
    descs = [GATHER_BIG[n] for n in names]

    def body(*refs):
        outs = refs[nb:2 * nb]
        send_sems, recv_sems = refs[2 * nb:]
        x, y, c = _mesh_pos()
        sib = (x, y, 1 - c)
        sends = []
        for t, (_, src, dst) in enumerate(descs):
            for j, (px, py) in enumerate(_other_chips(x, y)):
                got = dst(outs[t], 2 * px + py, c)
                sends.append(_remote(got, got, send_sems.at[3 * t + j], recv_sems.at[3 * t + j], sib))
        for cp in sends:
            cp.start()
        for t, (_, src, dst) in enumerate(descs):
            for j, (px, py) in enumerate(_other_chips(x, y)):
                theirs = dst(outs[t], 2 * px + py, 1 - c)
                _remote(theirs, theirs, send_sems.at[3 * t + j], recv_sems.at[3 * t + j], sib).wait_recv()
        for cp in sends:
            cp.wait_send()

    return pl.pallas_call(
        body, name="gather_forward", in_specs=[ANY] * nb, out_specs=[ANY] * nb,
        out_shape=[jax.ShapeDtypeStruct(b.shape, b.dtype) for b in bufs], input_output_aliases={t: t for t in range(nb)},
        scratch_shapes=[pltpu.SemaphoreType.DMA((3 * nb,)), pltpu.SemaphoreType.DMA((3 * nb,))],
    )(*bufs)


def _remote(src, dst, ssem, rsem, to):
    return pltpu.make_async_remote_copy(src_ref=src, dst_ref=dst, send_sem=ssem, recv_sem=rsem, device_id=to, device_id_type=MESH)


def _rs_send_sibling(gs, tag):
    n = len(gs)
    ns = sum(N_CHIPS if g.ndim == 4 else 1 for g in gs)

    def body(*refs):
        cps = _sibling_copies(gs, refs[:n], refs[n:2 * n], refs[2 * n], refs[2 * n + 1])
        for cp in cps:
            cp.start()
        for cp in cps:
            cp.wait()

    out_shape = [jax.ShapeDtypeStruct(g.shape[:1] + g.shape[2:] if g.ndim == 4 else g.shape[1:], g.dtype) for g in gs]
    return pl.pallas_call(
        body, name=f"rs_send_sibling_{tag}", in_specs=[ANY] * n, out_specs=[ANY] * n, out_shape=out_shape,
        scratch_shapes=[pltpu.SemaphoreType.DMA((ns,)), pltpu.SemaphoreType.DMA((ns,))],
    )(*gs)


def _add_piece(g, recv, c, name):
    P, Q = g.shape[-2:]

    def body(c_ref, g_ref, r_ref, o_ref):
        o_ref[...] = g_ref[...].reshape(o_ref.shape) + r_ref[...]

    if g.ndim == 4:
        grid = (N_CHIPS,)
        in_specs = [pl.BlockSpec((1, 1, P, Q), lambda k, c_ref: (k, c_ref[0], 0, 0)), pl.BlockSpec((1, P, Q), lambda k, c_ref: (k, 0, 0))]
        out_spec = pl.BlockSpec((1, P, Q), lambda k, c_ref: (k, 0, 0))
    else:
        grid = (1,)
        in_specs = [pl.BlockSpec((1, P, Q), lambda k, c_ref: (c_ref[0], 0, 0)), pl.BlockSpec((P, Q), lambda k, c_ref: (0, 0))]
        out_spec = pl.BlockSpec((P, Q), lambda k, c_ref: (0, 0))
    return pl.pallas_call(
        body, name=name,
        grid_spec=pltpu.PrefetchScalarGridSpec(num_scalar_prefetch=1, grid=grid, in_specs=in_specs, out_specs=out_spec),
        out_shape=jax.ShapeDtypeStruct(recv.shape, g.dtype),
        compiler_params=_cp(1),
    )(c, g, recv)


def _sibling_copies(gs, srcs, dsts, send_sems, recv_sems):
    x, y, c = _mesh_pos()
    cps, s = [], 0
    for t, g in enumerate(gs):
        if g.ndim == 4:
            for k in range(N_CHIPS):
                cps.append(_remote(srcs[t].at[k, 1 - c], dsts[t].at[k], send_sems.at[s], recv_sems.at[s], (x, y, 1 - c)))
                s += 1
        else:
            cps.append(_remote(srcs[t].at[1 - c], dsts[t], send_sems.at[s], recv_sems.at[s], (x, y, 1 - c)))
            s += 1
    return cps


def _sibling_start(gs, tag):
    n = len(gs)
    ns = sum(N_CHIPS if g.ndim == 4 else 1 for g in gs)
    lands = [pltpu.with_memory_space_constraint(lax.empty(g.shape[:1] + g.shape[2:] if g.ndim == 4 else g.shape[1:], g.dtype), pltpu.HBM)
             for g in gs]

    def body(*refs):
        for cp in _sibling_copies(gs, refs[:n], refs[n:2 * n], refs[2 * n], refs[2 * n + 1]):
            cp.start()
        refs[-1][...] = jnp.zeros_like(refs[-1])

    out = pl.pallas_call(
        body, name=f"sibling_start_{tag}",
        out_shape=(pltpu.SemaphoreType.DMA((ns,)), pltpu.SemaphoreType.DMA((ns,)),
                   *[pltpu.HBM(g.shape, g.dtype) for g in gs], *[pltpu.HBM(l.shape, l.dtype) for l in lands],
                   jax.ShapeDtypeStruct((SUBLANES, LANES), F32)),
        in_specs=[HBM] * (2 * n), out_specs=(SEM, SEM, *[HBM] * (2 * n), pl.BlockSpec(memory_space=pltpu.VMEM)),
        input_output_aliases={i: 2 + i for i in range(2 * n)},
        compiler_params=pltpu.CompilerParams(has_side_effects=DATAFLOW),
    )(*[pltpu.with_memory_space_constraint(g, pltpu.HBM) for g in gs], *lands)
    return (out[0], out[1], list(out[2:2 + n]), list(out[2 + n:2 + 2 * n])), out[-1]


def _sibling_wait(send_sems, recv_sems, gs, lands, after, tag):
    n = len(gs)

    def body(*refs):
        for cp in _sibling_copies(gs, refs[:n], refs[n:2 * n], refs[2 * n], refs[2 * n + 1]):
            cp.wait_send()
            cp.wait_recv()

    out = pl.pallas_call(
        body, name=f"sibling_wait_{tag}",
        out_shape=(*[pltpu.HBM(g.shape, g.dtype) for g in gs], *[pltpu.HBM(l.shape, l.dtype) for l in lands]),
        in_specs=[HBM] * (2 * n) + [SEM, SEM, ANY], out_specs=tuple([HBM] * (2 * n)),
        input_output_aliases={i: i for i in range(2 * n)},
        compiler_params=pltpu.CompilerParams(has_side_effects=DATAFLOW),
    )(*gs, *lands, send_sems, recv_sems, after)
    return list(out[:n]), list(out[n:])


def _chips_start(hs, tag):
    n = len(hs)
    lands = [pltpu.with_memory_space_constraint(lax.empty((N_CHIPS,) + h.shape[-2:], h.dtype), pltpu.HBM) for h in hs]

    def body(*refs):
        ins, lnd = refs[:n], refs[n:2 * n]
        send_sems, recv_sems, token = refs[2 * n], refs[2 * n + 1], refs[-1]
        x, y, c = _mesh_pos()
        k = 2 * x + y
        piece = lambda t, kk: ins[t].at[kk] if hs[t].ndim == 3 else ins[t]
        for t in range(n):
            for j, (px, py) in enumerate(_other_chips(x, y)):
                _remote(piece(t, 2 * px + py), lnd[t].at[k], send_sems.at[3 * t + j], recv_sems.at[3 * t + j], (px, py, c)).start()
        token[...] = jnp.zeros_like(token)

    out = pl.pallas_call(
        body, name=f"chips_start_{tag}",
        out_shape=(pltpu.SemaphoreType.DMA((3 * n,)), pltpu.SemaphoreType.DMA((3 * n,)),
                   *[pltpu.HBM(h.shape, h.dtype) for h in hs], *[pltpu.HBM(l.shape, l.dtype) for l in lands],
                   jax.ShapeDtypeStruct((SUBLANES, LANES), F32)),
        in_specs=[HBM] * (2 * n), out_specs=(SEM, SEM, *[HBM] * (2 * n), pl.BlockSpec(memory_space=pltpu.VMEM)),
        input_output_aliases={i: 2 + i for i in range(2 * n)},
        compiler_params=pltpu.CompilerParams(has_side_effects=DATAFLOW),
    )(*[pltpu.with_memory_space_constraint(h, pltpu.HBM) for h in hs], *lands)
    return out[0], out[1], list(out[2:2 + n]), list(out[2 + n:2 + 2 * n]), out[-1]


def _chips_wait(send_sems, recv_sems, hs, lands, after, tag):
    n = len(hs)

    def body(*refs):
        ins, lnd = refs[:n], refs[n:2 * n]
        ssem, rsem = refs[2 * n], refs[2 * n + 1]
        x, y, c = _mesh_pos()
        k = 2 * x + y
        piece = lambda t, kk: ins[t].at[kk] if hs[t].ndim == 3 else ins[t]
        for t in range(n):
            for j, (px, py) in enumerate(_other_chips(x, y)):
                cp = _remote(piece(t, k), lnd[t].at[2 * px + py], ssem.at[3 * t + j], rsem.at[3 * t + j], (px, py, c))
                cp.wait_send()
                cp.wait_recv()

    out = pl.pallas_call(
        body, name=f"chips_wait_{tag}",
        out_shape=(*[pltpu.HBM(h.shape, h.dtype) for h in hs], *[pltpu.HBM(l.shape, l.dtype) for l in lands]),
        in_specs=[HBM] * (2 * n) + [SEM, SEM, ANY], out_specs=tuple([HBM] * (2 * n)),
        input_output_aliases={i: i for i in range(2 * n)},
        compiler_params=pltpu.CompilerParams(has_side_effects=DATAFLOW),
    )(*hs, *lands, send_sems, recv_sems, after)
    return list(out[:n]), list(out[n:])


def _add_chips(p, own, kc, name):
    _, P, Q = p.shape
    tr = P
    while tr * Q * 4 > ELEMENTWISE_TILE_BYTES and tr % 16 == 0:
        tr //= 2
    sharded = own.ndim == 3

    def body(kc_ref, p_ref, own_ref, o_ref):
        k = kc_ref[0]
        mine = own_ref[...].reshape(tr, Q)
        v = [jnp.where(k == j, mine, p_ref[j]) for j in range(N_CHIPS)]
        o_ref[0] = ((v[0] + v[1]) + v[2]) + v[3]

    own_spec = (pl.BlockSpec((1, tr, Q), lambda i, kc_ref: (kc_ref[0], i, 0)) if sharded
                else pl.BlockSpec((tr, Q), lambda i, kc_ref: (i, 0)))
    return pl.pallas_call(
        body, name=name,
        grid_spec=pltpu.PrefetchScalarGridSpec(
            num_scalar_prefetch=1, grid=(P // tr,),
            in_specs=[pl.BlockSpec((N_CHIPS, tr, Q), lambda i, kc_ref: (0, i, 0)), own_spec],
            out_specs=pl.BlockSpec((1, tr, Q), lambda i, kc_ref: (kc_ref[1], i, 0))),
        out_shape=jax.ShapeDtypeStruct((2, P, Q), p.dtype),
        compiler_params=_cp(1),
    )(kc, p, own)


def _rs_share(fs, tag):
    n = len(fs)

    def body(*refs):
        outs = refs[n:2 * n]
        send_sems, recv_sems = refs[2 * n:]
        x, y, c = _mesh_pos()
        sends = [_remote(outs[t].at[c], outs[t].at[c], send_sems.at[t], recv_sems.at[t], (x, y, 1 - c)) for t in range(n)]
        for cp in sends:
            cp.start()
        for t in range(n):
            _remote(outs[t].at[c], outs[t].at[1 - c], send_sems.at[t], recv_sems.at[t], (x, y, 1 - c)).wait_recv()
        for cp in sends:
            cp.wait_send()

    return pl.pallas_call(
        body, name=f"rs_share_{tag}", in_specs=[ANY] * n, out_specs=[ANY] * n,
        out_shape=[jax.ShapeDtypeStruct(f.shape, f.dtype) for f in fs], input_output_aliases={t: t for t in range(n)},
        scratch_shapes=[pltpu.SemaphoreType.DMA((n,)), pltpu.SemaphoreType.DMA((n,))],
    )(*fs)


def _reduce_start(gs, kc, tag):
    return _reduce_continue(gs, _rs_send_sibling(gs, tag), kc, tag)


def _reduce_continue(gs, from_sibling, kc, tag):
    chip_sums = [_add_piece(g, r, kc[1:], name=f"add_piece_{tag}_{t}") for t, (g, r) in enumerate(zip(gs, from_sibling))]
    send_sems, recv_sems, chip_sums, lands, token = _chips_start(chip_sums, tag)
    return (send_sems, recv_sems, chip_sums, lands, tag), token


def _reduce_finish(states, kc, after):
    mine = []
    for send_sems, recv_sems, chip_sums, lands, tag in states:
        chip_sums, from_chips = _chips_wait(send_sems, recv_sems, chip_sums, lands, after, tag)
        mine += [_add_chips(p, h, kc, name=f"add_chips_{tag}_{t}") for t, (p, h) in enumerate(zip(from_chips, chip_sums))]
    return _rs_share(mine, "all")


def _adamw(w, g, m, v, name):
    R, C = w.shape
    tr = R
    for cand in (512, 256, 128, 64, 32, 16, 8):
        if R % cand == 0 and cand * C * 4 <= ELEMENTWISE_TILE_BYTES:
            tr = cand
            break
    c1 = 1.0 / (1.0 - ADAM_B1 ** ADAM_STEP)
    c2 = 1.0 / (1.0 - ADAM_B2 ** ADAM_STEP)

    def body(w_ref, g_ref, m_ref, v_ref, d_ref, mo_ref, vo_ref):
        gv = g_ref[...]
        mn = ADAM_B1 * m_ref[...] + (1.0 - ADAM_B1) * gv
        vn = ADAM_B2 * v_ref[...] + (1.0 - ADAM_B2) * (gv * gv)
        mo_ref[...] = mn
        vo_ref[...] = vn
        d_ref[...] = -ADAM_LR * ((mn * c1) / (jnp.sqrt(vn * c2) + ADAM_EPS) + ADAM_WD * w_ref[...])

    spec = pl.BlockSpec((tr, C), lambda i: (i, 0))
    shp = jax.ShapeDtypeStruct((R, C), F32)
    return pl.pallas_call(body, name=name, grid=(R // tr,), in_specs=[spec] * 4, out_specs=[spec] * 3, out_shape=[shp] * 3,
                          compiler_params=_cp(1))(w, g, m, v)


def _adamw_many(ws, gs, ms, vs):
    n = len(ws)
    c1 = 1.0 / (1.0 - ADAM_B1 ** ADAM_STEP)
    c2 = 1.0 / (1.0 - ADAM_B2 ** ADAM_STEP)

    def body(*refs):
        w_refs, g_refs, m_refs, v_refs = (refs[k * n:(k + 1) * n] for k in range(4))
        outs = refs[4 * n:]
        for i in range(n):
            gv = g_refs[i][...]
            mn = ADAM_B1 * m_refs[i][...] + (1.0 - ADAM_B1) * gv
            vn = ADAM_B2 * v_refs[i][...] + (1.0 - ADAM_B2) * (gv * gv)
            outs[3 * i][...] = -ADAM_LR * ((mn * c1) / (jnp.sqrt(vn * c2) + ADAM_EPS) + ADAM_WD * w_refs[i][...])
            outs[3 * i + 1][...] = mn
            outs[3 * i + 2][...] = vn

    out_shape = [jax.ShapeDtypeStruct(w.shape, F32) for w in ws for _ in range(3)]
    return pl.pallas_call(body, name="adamw_small", out_shape=out_shape)(*ws, *gs, *ms, *vs)


def _step(a):
    x, y, c = _mesh_pos()
    kc = jnp.stack([2 * x + y, c]).astype(jnp.int32)

    rs = _rows(sum(a[n].size for n in SH_SMALL))
    first, later = ["ev_w_in", "ev_w_out"], ["od_w_in", "od_w_out", "ffn_w_up", "ffn_w_down"]
    lead = lambda w: w if w.ndim == 3 else w[None]
    *full, gs = _gather_weights(first, [a[n].astype(BF16) for n in first], _pack([a[n] for n in SH_SMALL], rs, F32))
    p = {n: a[n] for n in REP}
    p.update({n: lead(w) for n, w in zip(first, full)})
    parts = [_unpack(gs[k], [a[n].shape for n in SH_SMALL]) for k in range(N_CHIPS)]
    for i, n in enumerate(SH_SMALL):
        p[n] = jnp.concatenate([parts[k][i] for k in range(N_CHIPS)], axis=SH_SMALL[n])
    shards = lax.optimization_barrier(([a[n].astype(BF16) for n in later], full))[0]
    g_send, g_recv, shards, bufs, token = _gather_start(later, shards, _place_own(later, shards))
    p["norm_mix"] = p["norm_mix"] + token[0:1, 0:1]

    def late(after):
        got = _gather_forward(later, _gather_wait(later, g_send, g_recv, shards, bufs, after))
        return {n: lead(w) for n, w in zip(later, got)}

    p["late"] = late

    states, pending = {}, {}

    def start_reduce(tag, gs, split=False):
        if split:
            pending[tag], token = _sibling_start(gs, tag)
        else:
            states[tag], token = _reduce_start(gs, kc, tag)
        return token

    def continue_reduce(tag, after):
        send_sems, recv_sems, gs, lands = pending.pop(tag)
        gs, from_sibling = _sibling_wait(send_sems, recv_sems, gs, lands, after, tag)
        states[tag], token = _reduce_continue(gs, from_sibling, kc, tag)
        return token

    sq8, grad_x, grads, big = _local_step(a["x"][0], a["loss_target"][0], p, start_reduce, continue_reduce)
    loss = lax.psum(0.5 / D_MODEL * jnp.sum(sq8), ("x", "y", "c"))

    r_s = _rows(sum(a[n].size for n in SH_SMALL), 2 * SUBLANES) // 2
    small_pieces = []
    for k in range(N_CHIPS):
        pieces = [lax.slice_in_dim(grads[n], k * a[n].shape[ax], (k + 1) * a[n].shape[ax], axis=ax) for n, ax in SH_SMALL.items()]
        small_pieces.append(_pack(pieces, 2 * r_s, F32).reshape(2, r_s, LANES))
    g_small = jnp.stack(small_pieces)
    r_r = _rows(sum(a[n].size for n in REP), 2 * SUBLANES) // 2
    g_rep = _pack([grads[n] for n in REP], 2 * r_r, F32).reshape(2, r_r, LANES)
    token = start_reduce("g4", [g_small, g_rep])
    reduced = _reduce_finish([states[tag] for tag in ("g1", "g2", "g3", "g4")], kc, token)
    red = dict(zip([key for tag in ("g1", "g2", "g3") for key in REDUCE_GROUPS[tag]], reduced))
    gfin = {}
    for n in ("ev_w_in", "ev_w_out", "od_w_in", "od_w_out"):
        gfin[n] = red[n, 0].reshape(a[n].shape)
    for n in ("ffn_w_up", "ffn_w_down"):
        gfin[n] = jnp.stack([red[n, l].reshape(a[n].shape[1:]) for l in range(2)])
    gfin.update(zip(SH_SMALL, _unpack(reduced[-2], [a[n].shape for n in SH_SMALL])))
    gfin.update(zip(REP, _unpack(reduced[-1], [a[n].shape for n in REP])))

    out = {"loss": loss, "grad_x": grad_x[None]}
    small_names = list(SH_SMALL) + REP
    for n in SH_BIG:
        shp = a[n].shape
        two_d = lambda t: t.reshape(-1, shp[-1])
        d, mo, vo = _adamw(two_d(a[n]), two_d(gfin[n]), two_d(a["m_" + n]), two_d(a["v_" + n]), name=f"adamw_{n}")
        out["delta_" + n], out["new_m_" + n], out["new_v_" + n] = d.reshape(shp), mo.reshape(shp), vo.reshape(shp)
    two_d = lambda t: t.reshape(-1, t.shape[-1])
    res = _adamw_many(*[[two_d(src(n)) for n in small_names]
                        for src in (lambda n: a[n], lambda n: gfin[n], lambda n: a["m_" + n], lambda n: a["v_" + n])])
    for i, n in enumerate(small_names):
        out["delta_" + n], out["new_m_" + n], out["new_v_" + n] = (r.reshape(a[n].shape) for r in res[3 * i:3 * i + 3])
    for n in WEIGHTS:
        out["grad_" + n] = gfin[n]
    return out


def kernel(x, norm_mix, norm_ffn, norm_final, ev_w_in, ev_conv_w, ev_conv_b, ev_gate_a_w, ev_gate_a_b, ev_gate_x_w, ev_gate_x_b, ev_lru_lambda, hg_lb_logits, ev_hg_norm, ev_w_out, od_w_in, od_b_in, od_ln_g, od_ln_b, od_w_s, od_b_s, od_w_out, ffn_w_up, ffn_conv_w, ffn_conv_b, ffn_w_down, loss_target, m_norm_mix, m_norm_ffn, m_norm_final, m_ev_w_in, m_ev_conv_w, m_ev_conv_b, m_ev_gate_a_w, m_ev_gate_a_b, m_ev_gate_x_w, m_ev_gate_x_b, m_ev_lru_lambda, m_hg_lb_logits, m_ev_hg_norm, m_ev_w_out, m_od_w_in, m_od_b_in, m_od_ln_g, m_od_ln_b, m_od_w_s, m_od_b_s, m_od_w_out, m_ffn_w_up, m_ffn_conv_w, m_ffn_conv_b, m_ffn_w_down, v_norm_mix, v_norm_ffn, v_norm_final, v_ev_w_in, v_ev_conv_w, v_ev_conv_b, v_ev_gate_a_w, v_ev_gate_a_b, v_ev_gate_x_w, v_ev_gate_x_b, v_ev_lru_lambda, v_hg_lb_logits, v_ev_hg_norm, v_ev_w_out, v_od_w_in, v_od_b_in, v_od_ln_g, v_od_ln_b, v_od_w_s, v_od_b_s, v_od_w_out, v_ffn_w_up, v_ffn_conv_w, v_ffn_conv_b, v_ffn_w_down):
    vals = (x, norm_mix, norm_ffn, norm_final, ev_w_in, ev_conv_w, ev_conv_b, ev_gate_a_w, ev_gate_a_b, ev_gate_x_w, ev_gate_x_b, ev_lru_lambda, hg_lb_logits, ev_hg_norm, ev_w_out, od_w_in, od_b_in, od_ln_g, od_ln_b, od_w_s, od_b_s, od_w_out, ffn_w_up, ffn_conv_w, ffn_conv_b, ffn_w_down, loss_target, m_norm_mix, m_norm_ffn, m_norm_final, m_ev_w_in, m_ev_conv_w, m_ev_conv_b, m_ev_gate_a_w, m_ev_gate_a_b, m_ev_gate_x_w, m_ev_gate_x_b, m_ev_lru_lambda, m_hg_lb_logits, m_ev_hg_norm, m_ev_w_out, m_od_w_in, m_od_b_in, m_od_ln_g, m_od_ln_b, m_od_w_s, m_od_b_s, m_od_w_out, m_ffn_w_up, m_ffn_conv_w, m_ffn_conv_b, m_ffn_w_down, v_norm_mix, v_norm_ffn, v_norm_final, v_ev_w_in, v_ev_conv_w, v_ev_conv_b, v_ev_gate_a_w, v_ev_gate_a_b, v_ev_gate_x_w, v_ev_gate_x_b, v_ev_lru_lambda, v_hg_lb_logits, v_ev_hg_norm, v_ev_w_out, v_od_w_in, v_od_b_in, v_od_ln_g, v_od_ln_b, v_od_w_s, v_od_b_s, v_od_w_out, v_ffn_w_up, v_ffn_conv_w, v_ffn_conv_b, v_ffn_w_down)
    names = ["x"] + WEIGHTS + ["loss_target"] + ["m_" + n for n in WEIGHTS] + ["v_" + n for n in WEIGHTS]
    out = _step(dict(zip(names, vals)))
    return (out["loss"], out["grad_x"], *[out["grad_" + n] for n in WEIGHTS], *[out["delta_" + n] for n in WEIGHTS],
            *[out["new_m_" + n] for n in WEIGHTS], *[out["new_v_" + n] for n in WEIGHTS])
```

```python
import jax
import jax.numpy as jnp
from jax import lax
from jax.experimental import pallas as pl
from jax.experimental.pallas import tpu as pltpu

F32 = jnp.float32
BF16 = jnp.bfloat16

EPS = 1e-6
D_MODEL = 1024
LRU_W = 512
LRU_BLOCKS = 8
LRU_C = 8.0
HG_HEADS = 4
HG_D = 128
HG_CHUNK = 64
SGU_G = 8
SGU_CHUNK = 128
D_FF = 2816
ADAM_LR, ADAM_B1, ADAM_B2, ADAM_EPS, ADAM_WD, ADAM_STEP = 0.001, 0.9, 0.999, 1e-08, 0.01, 10

V7X_VMEM_BYTES = 64 * 1024 * 1024
VMEM_LIMIT = V7X_VMEM_BYTES - 8 * 1024 * 1024
SUBLANES = 8
LANES = 128
BF16_ROWS = 16
ELEMENTWISE_TILE_BYTES = 2 * 1024 * 1024

N_CHIPS = 4
MESH = pl.DeviceIdType.MESH
ANY = pl.BlockSpec(memory_space=pl.ANY)
HBM = pl.BlockSpec(memory_space=pltpu.HBM)
SEM = pl.BlockSpec(memory_space=pltpu.SEMAPHORE)
DATAFLOW = pltpu.SideEffectType.DATAFLOW_SIDE_EFFECTING

GELU_C0 = 0.7978845608028654
GELU_C1 = 0.044715

NN = (((1,), (0,)), ((), ()))
NT = (((1,), (1,)), ((), ()))
TN = (((0,), (0,)), ((), ()))


def _dot(a, b, dims=NN):
    return lax.dot_general(a.astype(BF16), b.astype(BF16), dims, preferred_element_type=F32)


def _cp(n_grid):
    return pltpu.CompilerParams(dimension_semantics=("arbitrary",) * n_grid, vmem_limit_bytes=VMEM_LIMIT)


def _chunk(n, cap):
    best = LANES
    for c in range(LANES, cap + 1, LANES):
        if n % c == 0:
            best = c
    return best


def _resident(shape):
    nd = len(shape)
    return pl.BlockSpec(shape, lambda *_: (0,) * nd, pipeline_mode=pl.Buffered(1))


def _rsum8(x):
    r, c = x.shape
    return x.reshape(r // SUBLANES, SUBLANES, c).sum(axis=0)


def _sigmoid(x):
    return 0.5 * jnp.tanh(0.5 * x) + 0.5


def _gelu(x):
    return 0.5 * x * (1.0 + jnp.tanh(GELU_C0 * (x + GELU_C1 * x * x * x)))


def _gelu_grad(x):
    t = jnp.tanh(GELU_C0 * (x + GELU_C1 * x * x * x))
    return 0.5 * (1.0 + t) + 0.5 * x * (1.0 - t * t) * GELU_C0 * (1.0 + 3.0 * GELU_C1 * x * x)


def _silu_and_grad(x):
    s = _sigmoid(x)
    return x * s, s * (1.0 + x * (1.0 - s))


def _shift_rows(e, j):
    n = e.shape[0]
    return e if j % n == 0 else pltpu.roll(e, j % n, 0)


def _weight(w):
    if isinstance(w, tuple):
        stack, layer = w
        K, N = stack.shape[1:]
        return stack, pl.BlockSpec((None, K, N), lambda *_: (layer, 0, 0), pipeline_mode=pl.Buffered(1)), (K, N)
    return w, _resident(w.shape), w.shape


def _norm_mm(h, g, w, b, name, tt=1024):
    T, D = h.shape
    w, w_spec, (_, N) = _weight(w)
    cn = _chunk(N, 512)

    def body(h_ref, g_ref, w_ref, b_ref, hn_ref, z_ref):
        x = h_ref[...]
        r = lax.rsqrt(jnp.mean(x * x, axis=-1, keepdims=True) + EPS)
        hn = (x * r * g_ref[...]).astype(BF16)
        hn_ref[...] = hn
        for j in range(0, N, cn):
            acc = jnp.dot(hn, w_ref[:, j:j + cn], preferred_element_type=F32) + b_ref[:, j:j + cn]
            z_ref[:, j:j + cn] = acc.astype(BF16)

    return pl.pallas_call(
        body, name=name, grid=(T // tt,),
        in_specs=[pl.BlockSpec((tt, D), lambda i: (i, 0)), _resident((1, D)), w_spec, _resident((1, N))],
        out_specs=[pl.BlockSpec((tt, D), lambda i: (i, 0)), pl.BlockSpec((tt, N), lambda i: (i, 0))],
        out_shape=[jax.ShapeDtypeStruct((T, D), BF16), jax.ShapeDtypeStruct((T, N), BF16)],
        compiler_params=_cp(1),
    )(h, g, w, b)


def _mm(a, w, res, out_dtype, name, tt=1024, transpose_w=False):
    T, K = a.shape
    N = w.shape[0] if transpose_w else w.shape[1]
    cn = _chunk(N, 512)
    has_res = res is not None

    def body(*refs):
        a_ref, w_ref = refs[0], refs[1]
        res_ref = refs[2] if has_res else None
        o_ref = refs[-1]
        av = a_ref[...].astype(BF16)
        for j in range(0, N, cn):
            if transpose_w:
                acc = lax.dot_general(av, w_ref[j:j + cn, :], NT, preferred_element_type=F32)
            else:
                acc = jnp.dot(av, w_ref[:, j:j + cn], preferred_element_type=F32)
            if has_res:
                acc = acc + res_ref[:, j:j + cn]
            o_ref[:, j:j + cn] = acc.astype(out_dtype)

    in_specs = [pl.BlockSpec((tt, K), lambda i: (i, 0)), _resident(w.shape)]
    args = [a, w]
    if has_res:
        in_specs.append(pl.BlockSpec((tt, N), lambda i: (i, 0)))
        args.append(res)
    return pl.pallas_call(
        body, name=name, grid=(T // tt,), in_specs=in_specs,
        out_specs=pl.BlockSpec((tt, N), lambda i: (i, 0)),
        out_shape=jax.ShapeDtypeStruct((T, N), out_dtype),
        compiler_params=_cp(1),
    )(*args)


def _mm_tn(a, b, name, col_shards=1, tt=2048):
    T, K = a.shape
    N = b.shape[1]
    ns = N // col_shards
    tt = min(tt, T)
    while 2 * (tt * K * a.dtype.itemsize + tt * ns * b.dtype.itemsize + K * ns * 4) + K * ns * 4 > VMEM_LIMIT:
        tt //= 2

    def body(a_ref, b_ref, o_ref):
        acc = lax.dot_general(a_ref[...].astype(BF16), b_ref[...].astype(BF16), TN, preferred_element_type=F32)
        prev = jnp.where(pl.program_id(1) == 0, 0.0, o_ref[0])
        o_ref[0] = prev + acc

    out = pl.pallas_call(
        body, name=name, grid=(col_shards, T // tt),
        in_specs=[pl.BlockSpec((tt, K), lambda n, t: (t, 0)), pl.BlockSpec((tt, ns), lambda n, t: (t, n))],
        out_specs=pl.BlockSpec((1, K, ns), lambda n, t: (n, 0, 0)),
        out_shape=jax.ShapeDtypeStruct((col_shards, K, ns), F32),
        compiler_params=_cp(2),
    )(a, b)
    return out if col_shards > 1 else out[0]


def _mm_normbwd(dz, w, x, g, dres, name):
    T, N = dz.shape
    D = w.shape[0]
    tt = 1024 if N <= 3072 else 512

    def body(dz_ref, wt_ref, x_ref, g_ref, dres_ref, dx_ref, dg_ref):
        @pl.when(pl.program_id(0) == 0)
        def _():
            dg_ref[...] = jnp.zeros_like(dg_ref)

        dy = lax.dot_general(dz_ref[...], wt_ref[...], NT, preferred_element_type=F32)
        x = x_ref[...]
        r = lax.rsqrt(jnp.mean(x * x, axis=-1, keepdims=True) + EPS)
        xn = x * r
        dg_ref[...] += _rsum8(dy * xn)
        dxn = dy * g_ref[...]
        dx_ref[...] = dres_ref[...] + r * (dxn - xn * jnp.mean(dxn * xn, axis=-1, keepdims=True))

    return pl.pallas_call(
        body, name=name, grid=(T // tt,),
        in_specs=[pl.BlockSpec((tt, N), lambda i: (i, 0)), _resident((D, N)), pl.BlockSpec((tt, D), lambda i: (i, 0)),
                  _resident((1, D)), pl.BlockSpec((tt, D), lambda i: (i, 0))],
        out_specs=[pl.BlockSpec((tt, D), lambda i: (i, 0)), pl.BlockSpec((SUBLANES, D), lambda i: (0, 0))],
        out_shape=[jax.ShapeDtypeStruct((T, D), F32), jax.ShapeDtypeStruct((SUBLANES, D), F32)],
        compiler_params=_cp(1),
    )(dz, w, x, g, dres)


def _col_groups(F, cc, per_group=4):
    step = cc * per_group
    return [(g0, min(g0 + step, F)) for g0 in range(0, F, step)]


def _loss_head(x, gv, tgt):
    r = lax.rsqrt(jnp.mean(x * x, axis=-1, keepdims=True) + EPS)
    xn = x * r
    diff = xn * gv - tgt
    dy = diff * (1.0 / x.shape[-1])
    dxn = dy * gv
    return r * (dxn - xn * jnp.mean(dxn * xn, axis=-1, keepdims=True)), _rsum8(diff * diff), _rsum8(dy * xn)


def _ffn_act(gu, cw, cb, w_down, res, name, loss=None, tt=512):
    T = gu.shape[0]
    F = gu.shape[1] // 2
    w_down, wd_spec, (_, D) = _weight(w_down)
    cc = _chunk(F, 256)
    hb = tt // BF16_ROWS

    def body(gu_ref, halo_ref, cw_ref, cb_ref, wd_ref, res_ref, *rest):
        if loss is None:
            a_ref, gc_ref, o_ref = rest
        else:
            gf_ref, t_ref, a_ref, gc_ref, o_ref, sq_ref, dgf_ref = rest
        first = pl.program_id(0) == 0
        acc = res_ref[...]
        for g0, g1 in _col_groups(F, cc):
            for c0 in range(g0, g1, cc):
                cs = slice(c0, c0 + cc)
                x = gu_ref[:, cs].astype(F32)
                halo = jnp.where(first, 0.0, halo_ref[:, cs].astype(F32))
                e = jnp.concatenate([halo, x], axis=0)
                gc = (cb_ref[:, cs] + cw_ref[0:1, cs] * _shift_rows(e, 2)[BF16_ROWS:] + cw_ref[1:2, cs] * _shift_rows(e, 1)[BF16_ROWS:]
                      + cw_ref[2:3, cs] * x)
                up = gu_ref[:, F + c0:F + c0 + cc].astype(F32)
                gc_ref[:, cs] = gc.astype(BF16)
                a_ref[:, cs] = (gc * _sigmoid(gc) * up).astype(BF16)
            acc = acc + jnp.dot(a_ref[:, g0:g1], wd_ref[g0:g1, :], preferred_element_type=F32)
        if loss is None:
            o_ref[...] = acc
        else:
            @pl.when(first)
            def _():
                sq_ref[...] = jnp.zeros_like(sq_ref)
                dgf_ref[...] = jnp.zeros_like(dgf_ref)

            dx, sq, dgf = _loss_head(acc, gf_ref[...], t_ref[...])
            o_ref[...] = dx
            sq_ref[...] += sq
            dgf_ref[...] += dgf

    tok = lambda w: pl.BlockSpec((tt, w), lambda i: (i, 0))
    acc8 = pl.BlockSpec((SUBLANES, D), lambda i: (0, 0))
    in_specs = [tok(2 * F), pl.BlockSpec((BF16_ROWS, F), lambda i: (jnp.maximum(i * hb - 1, 0), 0)),
                _resident((SUBLANES, F)), _resident((1, F)), wd_spec, tok(D)]
    out_specs = [tok(F), tok(F), tok(D)]
    out_shape = [jax.ShapeDtypeStruct((T, F), BF16), jax.ShapeDtypeStruct((T, F), BF16), jax.ShapeDtypeStruct((T, D), F32)]
    args = [gu, gu, cw, cb, w_down, res]
    if loss is not None:
        in_specs += [_resident((1, D)), tok(D)]
        out_specs += [acc8, acc8]
        out_shape += [jax.ShapeDtypeStruct((SUBLANES, D), F32)] * 2
        args += list(loss)
    return pl.pallas_call(body, name=name, grid=(T // tt,), in_specs=in_specs, out_specs=out_specs, out_shape=out_shape,
                          compiler_params=_cp(1))(*args)


def _ffn_act_bwd(gu, gc, cw, w_up, w_down, x, g, dh, name, tt=256):
    T = gu.shape[0]
    F = gu.shape[1] // 2
    w_up, wu_spec, (D, _) = _weight(w_up)
    w_down, wd_spec, _ = _weight(w_down)
    cc = _chunk(F, 256)
    hb = tt // BF16_ROWS
    last_hb = T // BF16_ROWS - 1
    nt = T // tt

    def body(gu_ref, upnext_ref, gc_ref, gcnext_ref, cw_ref, wu_ref, wd_ref, x_ref, g_ref, dh_ref, dhnext_ref,
             dgu_ref, dc_ref, dx_ref, dg_ref):
        i = pl.program_id(0)

        @pl.when(i == 0)
        def _():
            dc_ref[...] = jnp.zeros_like(dc_ref)
            dg_ref[...] = jnp.zeros_like(dg_ref)

        n = tt + BF16_ROWS
        ext = lambda main, nxt: jnp.concatenate([main.astype(F32), nxt.astype(F32)], axis=0)
        dhe = ext(dh_ref[...], jnp.where(i == nt - 1, 0.0, dhnext_ref[...])).astype(BF16)
        dy = jnp.zeros((tt, D), F32)
        groups = _col_groups(F, cc)
        da_of = lambda grp: lax.dot_general(dhe, wd_ref[grp[0]:grp[1], :], NT, preferred_element_type=F32)
        da_next = da_of(groups[0])
        for gi, (lo, hi) in enumerate(groups):
            da, da_next = da_next, (da_of(groups[gi + 1]) if gi + 1 < len(groups) else None)
            for c0 in range(lo, hi, cc):
                cs = slice(c0, c0 + cc)
                us = slice(F + c0, F + c0 + cc)
                gc = ext(gc_ref[:, cs], gcnext_ref[:, cs])
                up = ext(gu_ref[:, us], upnext_ref[:, cs])
                dae = da[:, c0 - lo:c0 - lo + cc]
                s, ds = _silu_and_grad(gc)
                dgc = dae * up * ds
                dgu_ref[:, us] = (dae * s)[:tt].astype(BF16)
                dgc1 = _shift_rows(dgc, n - 1)[:tt]
                dgc2 = _shift_rows(dgc, n - 2)[:tt]
                dm = dgc[:tt]
                dgu_ref[:, cs] = (cw_ref[2:3, cs] * dm + cw_ref[1:2, cs] * dgc1 + cw_ref[0:1, cs] * dgc2).astype(BF16)
                gt = gu_ref[:, cs].astype(F32)
                dc_ref[0, :, cs] += _rsum8(dgc2 * gt)
                dc_ref[1, :, cs] += _rsum8(dgc1 * gt)
                dc_ref[2, :, cs] += _rsum8(dm * gt)
                dc_ref[3, :, cs] += _rsum8(dm)
            dy = (dy + lax.dot_general(dgu_ref[:, lo:hi], wu_ref[:, lo:hi], NT, preferred_element_type=F32)
                  + lax.dot_general(dgu_ref[:, F + lo:F + hi], wu_ref[:, F + lo:F + hi], NT, preferred_element_type=F32))
        xv = x_ref[...]
        r = lax.rsqrt(jnp.mean(xv * xv, axis=-1, keepdims=True) + EPS)
        xn = xv * r
        dg_ref[...] += _rsum8(dy * xn)
        dxn = dy * g_ref[...]
        dx_ref[...] = dh_ref[...] + r * (dxn - xn * jnp.mean(dxn * xn, axis=-1, keepdims=True))

    tok = lambda w: pl.BlockSpec((tt, w), lambda i: (i, 0))
    nxt = lambda w, col: pl.BlockSpec((BF16_ROWS, w), lambda i: (jnp.minimum((i + 1) * hb, last_hb), col))
    return pl.pallas_call(
        body, name=name, grid=(nt,),
        in_specs=[tok(2 * F), nxt(F, 1), tok(F), nxt(F, 0),
                  _resident((SUBLANES, F)), wu_spec, wd_spec, tok(D), _resident((1, D)), tok(D), nxt(D, 0)],
        out_specs=[tok(2 * F), pl.BlockSpec((4, SUBLANES, F), lambda i: (0, 0, 0)), tok(D),
                   pl.BlockSpec((SUBLANES, D), lambda i: (0, 0))],
        out_shape=[jax.ShapeDtypeStruct((T, 2 * F), BF16), jax.ShapeDtypeStruct((4, SUBLANES, F), F32),
                   jax.ShapeDtypeStruct((T, D), F32), jax.ShapeDtypeStruct((SUBLANES, D), F32)],
        compiler_params=_cp(1),
    )(gu, gu, gc, gc, cw, w_up, w_down, x, g, dh, dh)


def _softplus_neg(lam):
    x = -lam
    y = jnp.exp(-jnp.abs(x))
    l1p = jnp.where(y < 0.01, y * (1.0 - y * (0.5 - y * (1.0 / 3.0))), jnp.log(1.0 + y))
    return jnp.maximum(x, 0.0) + l1p


def _lru_gates(xc, wa_ref, ba_ref, wx_ref, bx_ref, sp):
    xcb = xc.astype(BF16)
    r = _sigmoid(jnp.dot(xcb, wa_ref[...], preferred_element_type=F32) + ba_ref[...])
    gi = _sigmoid(jnp.dot(xcb, wx_ref[...], preferred_element_type=F32) + bx_ref[...])
    log_a = -LRU_C * r * sp
    a = jnp.exp(log_a)
    x2 = 2.0 * log_a
    series = -x2 * (1.0 + x2 * 0.5 * (1.0 + x2 * (1.0 / 3.0)))
    om = jnp.where(x2 > -0.02, series, 1.0 - a * a)
    return r, gi, a, jnp.sqrt(om)


def _lru_conv(xr, halo, cw_ref, cb_ref):
    e = jnp.concatenate([halo, xr], axis=0)
    x1, x2, x3 = (_shift_rows(e, j)[BF16_ROWS:] for j in (1, 2, 3))
    return cb_ref[...] + cw_ref[0:1, :] * x3 + cw_ref[1:2, :] * x2 + cw_ref[2:3, :] * x1 + cw_ref[3:4, :] * xr


def _lru_fwd(z, cw, cb, wa, ba, wx, bx, lam, name="lru_fwd", tt=512):
    T = z.shape[0]
    W = LRU_W
    hb = tt // BF16_ROWS
    ng = tt // SUBLANES

    def body(z_ref, halo_ref, cw_ref, cb_ref, wa_ref, ba_ref, wx_ref, bx_ref, lam_ref, oa_ref, h_ref, a_s, sv_ref, u_s, hc):
        i = pl.program_id(0)

        @pl.when(i == 0)
        def _():
            hc[...] = jnp.zeros_like(hc)

        xr = z_ref[:, W:2 * W].astype(F32)
        halo = jnp.where(i == 0, 0.0, halo_ref[...].astype(F32))
        xc = _lru_conv(xr, halo, cw_ref, cb_ref)
        sp = _softplus_neg(lam_ref[...])
        r, gi, a, mult = _lru_gates(xc, wa_ref, ba_ref, wx_ref, bx_ref, sp)
        a_s[...] = a
        u_s[...] = mult * gi * xc
        for k, saved in enumerate((mult, r, gi, xc)):
            sv_ref[:, k * W:(k + 1) * W] = saved.astype(BF16)
        row = lax.broadcasted_iota(jnp.int32, (SUBLANES, W), 0)

        def step(j, hprev):
            r0 = pl.multiple_of(j * SUBLANES, SUBLANES)
            A = a_s[pl.ds(r0, SUBLANES), :]
            U = u_s[pl.ds(r0, SUBLANES), :]
            for k in (1, 2, 4):
                m = row >= k
                U = jnp.where(m, A * pltpu.roll(U, k, 0) + U, U)
                A = jnp.where(m, A * pltpu.roll(A, k, 0), A)
            H = U + A * hprev
            h_ref[pl.ds(r0, SUBLANES), :] = H
            return jnp.broadcast_to(H[SUBLANES - 1:SUBLANES, :], (SUBLANES, W))

        hc[...] = lax.fori_loop(0, ng, step, hc[...])
        oa_ref[...] = (_gelu(z_ref[:, 0:W].astype(F32)) * h_ref[...]).astype(BF16)

    return pl.pallas_call(
        body, name=name, grid=(T // tt,),
        in_specs=[pl.BlockSpec((tt, 2 * W), lambda i: (i, 0)),
                  pl.BlockSpec((BF16_ROWS, W), lambda i: (jnp.maximum(i * hb - 1, 0), 1)),
                  _resident((SUBLANES, W)), _resident((1, W)), _resident((W, W)), _resident((1, W)),
                  _resident((W, W)), _resident((1, W)), _resident((1, W))],
        out_specs=[pl.BlockSpec((tt, W), lambda i: (i, 0)), pl.BlockSpec((tt, W), lambda i: (i, 0)),
                   pl.BlockSpec((tt, W), lambda i: (i, 0)), pl.BlockSpec((tt, 4 * W), lambda i: (i, 0))],
        out_shape=[jax.ShapeDtypeStruct((T, 2 * W), BF16), jax.ShapeDtypeStruct((T, W), F32),
                   jax.ShapeDtypeStruct((T, W), F32), jax.ShapeDtypeStruct((T, 4 * W), BF16)],
        scratch_shapes=[pltpu.VMEM((tt, W), F32), pltpu.VMEM((SUBLANES, W), F32)],
        compiler_params=_cp(1),
    )(z, z, cw, cb, wa, ba, wx, bx, lam)


def _lru_bwd(z, hseq, a_all, saved, dmix, cw, wat, wxt, lam, name="lru_bwd", tt=512):
    T = z.shape[0]
    W = LRU_W
    nt = T // tt
    sb = tt // SUBLANES
    ng = tt // SUBLANES

    def body(z_ref, h_ref, hprev_ref, a_ref, sv_ref, dm_ref, cw_ref, wat_ref, wxt_ref, lam_ref,
             dz_ref, dc_ref, dwa_ref, dwx_ref, dv_ref, c_s, d_s, g_s, gc, an, dxn):
        i = pl.program_id(0)
        ti = nt - 1 - i

        @pl.when(i == 0)
        def _():
            dc_ref[...] = jnp.zeros_like(dc_ref)
            dwa_ref[...] = jnp.zeros_like(dwa_ref)
            dwx_ref[...] = jnp.zeros_like(dwx_ref)
            dv_ref[...] = jnp.zeros_like(dv_ref)
            gc[...] = jnp.zeros_like(gc)
            an[...] = jnp.zeros_like(an)
            dxn[...] = jnp.zeros_like(dxn)

        xr = z_ref[:, W:2 * W].astype(F32)
        yg = z_ref[:, 0:W].astype(F32)
        sp = _softplus_neg(lam_ref[...])
        a = a_ref[...]
        mult, r, gi, xc = (sv_ref[:, k * W:(k + 1) * W].astype(F32) for k in range(4))
        h = h_ref[...]
        hp = jnp.where(ti == 0, 0.0, hprev_ref[...])
        hm1 = _shift_rows(jnp.concatenate([hp, h], axis=0), 1)[SUBLANES:]
        dout = dm_ref[...].astype(F32)
        d_s[...] = dout * _gelu(yg)
        dz_ref[:, 0:W] = (dout * h * _gelu_grad(yg)).astype(BF16)
        c_s[...] = _shift_rows(jnp.concatenate([a, an[...]], axis=0), tt + SUBLANES - 1)[:tt]
        an[...] = a[0:SUBLANES, :]
        row = lax.broadcasted_iota(jnp.int32, (SUBLANES, W), 0)

        def step(j, gnext):
            r0 = pl.multiple_of((ng - 1 - j) * SUBLANES, SUBLANES)
            C = c_s[pl.ds(r0, SUBLANES), :]
            G = d_s[pl.ds(r0, SUBLANES), :]
            for k in (1, 2, 4):
                m = row < SUBLANES - k
                G = jnp.where(m, G + C * pltpu.roll(G, SUBLANES - k, 0), G)
                C = jnp.where(m, C * pltpu.roll(C, SUBLANES - k, 0), C)
            G = G + C * gnext
            g_s[pl.ds(r0, SUBLANES), :] = G
            return jnp.broadcast_to(G[0:1, :], (SUBLANES, W))

        gc[...] = lax.fori_loop(0, ng, step, gc[...])
        du = g_s[...]
        da = du * hm1
        dgi = du * mult * xc
        dxc = du * mult * gi
        dmult = du * gi * xc
        dlog_a = da * a - dmult * (a * a) / mult
        dr = dlog_a * (-LRU_C * sp)
        dv_ref[2] += _rsum8(dlog_a * (-LRU_C * r))
        dpr = (dr * r * (1.0 - r)).astype(BF16)
        dpi = (dgi * gi * (1.0 - gi)).astype(BF16)
        dv_ref[0] += _rsum8(dpr.astype(F32))
        dv_ref[1] += _rsum8(dpi.astype(F32))
        xcb = sv_ref[:, 3 * W:4 * W]
        dwa_ref[...] += lax.dot_general(xcb, dpr, TN, preferred_element_type=F32)
        dwx_ref[...] += lax.dot_general(xcb, dpi, TN, preferred_element_type=F32)
        dxc = dxc + jnp.dot(dpr, wat_ref[...], preferred_element_type=F32) + jnp.dot(dpi, wxt_ref[...], preferred_element_type=F32)
        n = tt + BF16_ROWS
        de = jnp.concatenate([dxc, dxn[...]], axis=0)
        d1, d2, d3 = (_shift_rows(de, n - j)[:tt] for j in (1, 2, 3))
        dxn[...] = dxc[0:BF16_ROWS, :]
        dz_ref[:, W:2 * W] = (cw_ref[3:4, :] * dxc + cw_ref[2:3, :] * d1 + cw_ref[1:2, :] * d2 + cw_ref[0:1, :] * d3).astype(BF16)
        dc_ref[0] += _rsum8(d3 * xr)
        dc_ref[1] += _rsum8(d2 * xr)
        dc_ref[2] += _rsum8(d1 * xr)
        dc_ref[3] += _rsum8(dxc * xr)
        dc_ref[4] += _rsum8(dxc)

    rev = lambda i: nt - 1 - i
    tok = lambda w: pl.BlockSpec((tt, w), lambda i: (rev(i), 0))
    return pl.pallas_call(
        body, name=name, grid=(nt,),
        in_specs=[tok(2 * W), tok(W),
                  pl.BlockSpec((SUBLANES, W), lambda i: (jnp.maximum(rev(i) * sb - 1, 0), 0)),
                  tok(W), tok(4 * W), tok(W),
                  _resident((SUBLANES, W)), _resident((W, W)), _resident((W, W)), _resident((1, W))],
        out_specs=[pl.BlockSpec((tt, 2 * W), lambda i: (rev(i), 0)),
                   pl.BlockSpec((5, SUBLANES, W), lambda i: (0, 0, 0)),
                   pl.BlockSpec((W, W), lambda i: (0, 0)), pl.BlockSpec((W, W), lambda i: (0, 0)),
                   pl.BlockSpec((3, SUBLANES, W), lambda i: (0, 0, 0))],
        out_shape=[jax.ShapeDtypeStruct((T, z.shape[1]), BF16), jax.ShapeDtypeStruct((5, SUBLANES, W), F32),
                   jax.ShapeDtypeStruct((W, W), F32), jax.ShapeDtypeStruct((W, W), F32),
                   jax.ShapeDtypeStruct((3, SUBLANES, W), F32)],
        scratch_shapes=[pltpu.VMEM((tt, W), F32), pltpu.VMEM((tt, W), F32), pltpu.VMEM((tt, W), F32),
                        pltpu.VMEM((SUBLANES, W), F32), pltpu.VMEM((SUBLANES, W), F32), pltpu.VMEM((BF16_ROWS, W), F32)],
        compiler_params=_cp(1),
    )(z, hseq, hseq, a_all, saved, dmix, cw, wat, wxt, lam)


def _split3(x):
    hi = x.astype(BF16)
    r1 = x - hi.astype(F32)
    mid = r1.astype(BF16)
    lo = (r1 - mid.astype(F32)).astype(BF16)
    return hi, mid, lo


def _tri_matmul(tri, x):
    hi, mid, lo = _split3(x)
    return (jnp.dot(tri, hi, preferred_element_type=F32) + jnp.dot(tri, mid, preferred_element_type=F32)
            + jnp.dot(tri, lo, preferred_element_type=F32))


def _hg_chunk(q, fl, lb):
    C = q.shape[0]
    ri = lax.broadcasted_iota(jnp.int32, (C, C), 0)
    ci = lax.broadcasted_iota(jnp.int32, (C, C), 1)
    causal = ri >= ci
    sig = _sigmoid(fl)
    f = lb + (1.0 - lb) * sig
    k = 1.0 - f
    sq = _sigmoid(q)
    qf = q * sq
    b = _tri_matmul(causal.astype(BF16), jnp.log(f))
    bm = b[C // 2 - 1:C // 2, :]
    bl = b[C - 1:C, :]
    e_qt, e_kt, e_in, e_out = jnp.exp(b - bm), jnp.exp(bm - b), jnp.exp(b), jnp.exp(bl - b)
    qt = qf * e_qt
    kt = k * e_kt
    qin = qf * e_in
    kout = k * e_out
    qtb, ktb = qt.astype(BF16), kt.astype(BF16)
    att = [jnp.where(causal, _dot(qtb[:, _head(h)], ktb[:, _head(h)], NT), 0.0).astype(BF16) for h in range(HG_HEADS)]
    return dict(sig=sig, f=f, k=k, sq=sq, qf=qf, b=b, bm=bm, bl=bl, qt=qt, kt=kt, qin=qin, kout=kout, att=att,
                causal=causal, anti=ri <= ci, decay=jnp.exp(bl), e_qt=e_qt, e_kt=e_kt, e_in=e_in, e_out=e_out)


def _head(h):
    return slice(h * HG_D, (h + 1) * HG_D)


def _hgrn_fwd(z, lb, gn, mix, name="hgrn_fwd", tt=512):
    T = z.shape[0]
    C = HG_CHUNK
    nc = tt // C
    Dh = HG_D
    Wd = HG_HEADS * Dh

    def body(q_ref, f_ref, v_ref, g_ref, lb_ref, gn_ref, mix_ref, o_ref, ss_ref, st):
        del mix_ref

        @pl.when(pl.program_id(0) == 0)
        def _():
            st[...] = jnp.zeros_like(st)

        S = [st[h] for h in range(HG_HEADS)]
        for c in range(nc):
            rows = slice(c * C, (c + 1) * C)
            ck = _hg_chunk(q_ref[rows, :].astype(F32), f_ref[rows, :].astype(F32), lb_ref[...])
            v = v_ref[rows, :]
            g = g_ref[rows, :].astype(F32)
            H = range(HG_HEADS)
            qinb, koutb = ck["qin"].astype(BF16), ck["kout"].astype(BF16)
            for h in H:
                ss_ref[h, c] = S[h]
            o = [_dot(ck["att"][h], v[:, _head(h)]) + _dot(qinb[:, _head(h)], S[h], NT) for h in H]
            S = [ck["decay"][:, _head(h)] * S[h] + _dot(v[:, _head(h)], koutb[:, _head(h)], TN) for h in H]
            outs = [o[h] * lax.rsqrt(jnp.mean(o[h] * o[h], axis=-1, keepdims=True) + EPS) * gn_ref[...] for h in H]
            o_ref[rows, :] = (jnp.concatenate(outs, axis=1) * (g * _sigmoid(g))).astype(BF16)
        for h in range(HG_HEADS):
            st[h] = S[h]

    col = lambda base: (lambda i: (i, base))
    return pl.pallas_call(
        body, name=name, grid=(T // tt,),
        in_specs=[pl.BlockSpec((tt, Wd), col(2)), pl.BlockSpec((tt, Wd), col(3)), pl.BlockSpec((tt, Wd), col(4)),
                  pl.BlockSpec((tt, Wd), col(5)), _resident((1, Wd)), _resident((1, Dh)), ANY],
        out_specs=[pl.BlockSpec((tt, Wd), lambda i: (i, 1)),
                   pl.BlockSpec((HG_HEADS, nc, Dh, Dh), lambda i: (0, i, 0, 0))],
        out_shape=[jax.ShapeDtypeStruct((T, 2 * Wd), BF16),
                   jax.ShapeDtypeStruct((HG_HEADS, T // C, Dh, Dh), F32)],
        scratch_shapes=[pltpu.VMEM((HG_HEADS, Dh, Dh), F32)],
        compiler_params=_cp(1),
        input_output_aliases={6: 0},
    )(z, z, z, z, lb, gn, mix)


def _hgrn_bwd(z, ss, dmix, lb, gn, dz, name="hgrn_bwd", tt=512):
    T = z.shape[0]
    C = HG_CHUNK
    nc = tt // C
    nt = T // tt
    Dh = HG_D
    Wd = HG_HEADS * Dh

    def body(q_ref, f_ref, v_ref, g_ref, ss_ref, dm_ref, lb_ref, gn_ref, dz01_ref, dz_ref, dlb_ref, dgn_ref, dst):
        @pl.when(pl.program_id(0) == 0)
        def _():
            dst[...] = jnp.zeros_like(dst)
            dlb_ref[...] = jnp.zeros_like(dlb_ref)
            dgn_ref[...] = jnp.zeros_like(dgn_ref)

        dS = [dst[h] for h in range(HG_HEADS)]
        lbv = lb_ref[...]
        gnv = gn_ref[...]
        rowc = lax.broadcasted_iota(jnp.int32, (C, Wd), 0)
        cat = lambda xs: jnp.concatenate(xs, axis=1)
        for c in reversed(range(nc)):
            rows = slice(c * C, (c + 1) * C)
            q = q_ref[rows, :].astype(F32)
            ck = _hg_chunk(q, f_ref[rows, :].astype(F32), lbv)
            v = v_ref[rows, :]
            g = g_ref[rows, :].astype(F32)
            dout = dm_ref[rows, :].astype(F32)
            sg, dsg = _silu_and_grad(g)
            d_ong = dout * sg
            H = range(HG_HEADS)
            qinb, koutb, qtb, ktb = (ck[n].astype(BF16) for n in ("qin", "kout", "qt", "kt"))
            S = [ss_ref[h, c] for h in H]
            Sb = [s.astype(BF16) for s in S]
            dSb = [d.astype(BF16) for d in dS]
            o = [_dot(ck["att"][h], v[:, _head(h)]) + _dot(qinb[:, _head(h)], Sb[h], NT) for h in H]
            rn = [lax.rsqrt(jnp.mean(o[h] * o[h], axis=-1, keepdims=True) + EPS) for h in H]
            on = [o[h] * rn[h] for h in H]
            don = [d_ong[:, _head(h)] * gnv for h in H]
            do = [(rn[h] * (don[h] - on[h] * jnp.mean(don[h] * on[h], axis=-1, keepdims=True))).astype(BF16) for h in H]
            datt = [jnp.where(ck["causal"], _dot(do[h], v[:, _head(h)], NT), 0.0).astype(BF16) for h in H]
            dvs = [_dot(ck["att"][h], do[h], TN) + _dot(koutb[:, _head(h)], dSb[h], NT) for h in H]
            dqins = [_dot(do[h], Sb[h]) for h in H]
            dkouts = [_dot(v[:, _head(h)], dSb[h]) for h in H]
            dqts = [_dot(datt[h], ktb[:, _head(h)]) for h in H]
            dkts = [_dot(datt[h], qtb[:, _head(h)], TN) for h in H]
            ddecays = [jnp.sum(dS[h] * S[h], axis=0, keepdims=True) for h in H]
            dS = [_dot(do[h], qinb[:, _head(h)], TN) + ck["decay"][:, _head(h)] * dS[h] for h in H]
            ons = [on[h] * gnv for h in H]
            dgn = _rsum8(d_ong[:, _head(0)] * on[0])
            for h in range(1, HG_HEADS):
                dgn = dgn + _rsum8(d_ong[:, _head(h)] * on[h])
            dgn_ref[...] += dgn
            dqt, dkt, dqin, dkout, ddecay = cat(dqts), cat(dkts), cat(dqins), cat(dkouts), cat(ddecays)
            dqf = dqt * ck["e_qt"] + dqin * ck["e_in"]
            dk = dkt * ck["e_kt"] + dkout * ck["e_out"]
            kk = dkout * ck["kout"]
            db = dqt * ck["qt"] - dkt * ck["kt"] + dqin * ck["qin"] - kk
            dbl = jnp.sum(kk, axis=0, keepdims=True) + ddecay * ck["decay"]
            db = db + jnp.where(rowc == C - 1, dbl, 0.0)
            dlogf = _tri_matmul(ck["anti"].astype(BF16), db)
            dfv = dlogf / ck["f"] - dk
            sig, sq = ck["sig"], ck["sq"]
            dlb_ref[...] += _rsum8(dfv * (1.0 - sig))
            dz_ref[rows, 2 * Wd:3 * Wd] = (dqf * (sq * (1.0 + q * (1.0 - sq)))).astype(BF16)
            dz_ref[rows, 3 * Wd:4 * Wd] = (dfv * (1.0 - lbv) * sig * (1.0 - sig)).astype(BF16)
            dz_ref[rows, 4 * Wd:5 * Wd] = cat(dvs).astype(BF16)
            dz_ref[rows, 5 * Wd:6 * Wd] = (dout * cat(ons) * dsg).astype(BF16)
        dz_ref[:, 0:2 * Wd] = dz01_ref[...]
        for h in range(HG_HEADS):
            dst[h] = dS[h]

    rev = lambda i: nt - 1 - i
    col = lambda base: (lambda i: (rev(i), base))
    return pl.pallas_call(
        body, name=name, grid=(nt,),
        in_specs=[pl.BlockSpec((tt, Wd), col(2)), pl.BlockSpec((tt, Wd), col(3)), pl.BlockSpec((tt, Wd), col(4)),
                  pl.BlockSpec((tt, Wd), col(5)),
                  pl.BlockSpec((HG_HEADS, nc, Dh, Dh), lambda i: (0, rev(i), 0, 0)),
                  pl.BlockSpec((tt, Wd), col(1)), _resident((1, Wd)), _resident((1, Dh)),
                  pl.BlockSpec((tt, 2 * Wd), col(0))],
        out_specs=[pl.BlockSpec((tt, 6 * Wd), lambda i: (rev(i), 0)), pl.BlockSpec((SUBLANES, Wd), lambda i: (0, 0)),
                   pl.BlockSpec((SUBLANES, Dh), lambda i: (0, 0))],
        out_shape=[jax.ShapeDtypeStruct((T, 6 * Wd), BF16), jax.ShapeDtypeStruct((SUBLANES, Wd), F32),
                   jax.ShapeDtypeStruct((SUBLANES, Dh), F32)],
        input_output_aliases={8: 0},
        scratch_shapes=[pltpu.VMEM((HG_HEADS, Dh, Dh), F32)],
        compiler_params=_cp(1),
    )(z, z, z, z, ss, dmix, lb, gn, dz)


def _sgu_core(p, lg_ref, lb_ref, wsc_ref, bsb_ref):
    Wd = D_MODEL
    G = SGU_CHUNK
    zz = _gelu(p)
    u = zz[:, :Wd]
    v = zz[:, Wd:]
    vc = v - jnp.mean(v, axis=-1, keepdims=True)
    rstd = lax.rsqrt(jnp.mean(vc * vc, axis=-1, keepdims=True) + EPS)
    vhat = vc * rstd
    vn = vhat * lg_ref[...] + lb_ref[...]
    svs = []
    for gi in range(SGU_G):
        svs.append(jnp.dot(wsc_ref[gi], vn[:, gi * G:(gi + 1) * G].astype(BF16), preferred_element_type=F32) + bsb_ref[gi])
    return u, vhat, rstd, vn, jnp.concatenate(svs, axis=1)


def _sgu_fwd(p1, lg, lbias, wsc, bsb, name="sgu_fwd", tt=512):
    T = p1.shape[0]
    Wd = D_MODEL
    C = SGU_CHUNK

    def body(p_ref, lg_ref, lb_ref, wsc_ref, bsb_ref, s_ref):
        for c in range(tt // C):
            rows = slice(c * C, (c + 1) * C)
            u, _, _, _, sv = _sgu_core(p_ref[rows, :].astype(F32), lg_ref, lb_ref, wsc_ref, bsb_ref)
            s_ref[rows, :] = (u * sv).astype(BF16)

    return pl.pallas_call(
        body, name=name, grid=(T // tt,),
        in_specs=[pl.BlockSpec((tt, 2 * Wd), lambda i: (i, 0)), _resident((1, Wd)), _resident((1, Wd)),
                  _resident((SGU_G, C, C)), _resident((SGU_G, C, C))],
        out_specs=pl.BlockSpec((tt, Wd), lambda i: (i, 0)),
        out_shape=jax.ShapeDtypeStruct((T, Wd), BF16),
        compiler_params=_cp(1),
    )(p1, lg, lbias, wsc, bsb)


def _sgu_bwd(p1, ds, lg, lbias, wsc, wsct, bsb, name="sgu_bwd", tt=512):
    T = p1.shape[0]
    Wd = D_MODEL
    C = SGU_CHUNK

    def body(p_ref, ds_ref, lg_ref, lb_ref, wsc_ref, wsct_ref, bsb_ref, dp_ref, dws_ref, dbs_ref, dlg_ref, dlb_ref, dbin_ref):
        @pl.when(pl.program_id(0) == 0)
        def _():
            dws_ref[...] = jnp.zeros_like(dws_ref)
            dbs_ref[...] = jnp.zeros_like(dbs_ref)
            dlg_ref[...] = jnp.zeros_like(dlg_ref)
            dlb_ref[...] = jnp.zeros_like(dlb_ref)
            dbin_ref[...] = jnp.zeros_like(dbin_ref)

        for c in range(tt // C):
            rows = slice(c * C, (c + 1) * C)
            p = p_ref[rows, :].astype(F32)
            u, vhat, rstd, vn, sv = _sgu_core(p, lg_ref, lb_ref, wsc_ref, bsb_ref)
            dsc = ds_ref[rows, :].astype(F32)
            du = dsc * sv
            dsv = dsc * u
            dvns = []
            for gi in range(SGU_G):
                cs = slice(gi * C, (gi + 1) * C)
                dsv_g = dsv[:, cs]
                dvns.append(jnp.dot(wsct_ref[gi], dsv_g.astype(BF16), preferred_element_type=F32))
                dws_ref[gi] += _dot(dsv_g, vn[:, cs], NT)
                dbs_ref[gi] += dsv_g
            dvn = jnp.concatenate(dvns, axis=1)
            dlg_ref[...] += _rsum8(dvn * vhat)
            dlb_ref[...] += _rsum8(dvn)
            dvh = dvn * lg_ref[...]
            dv = rstd * (dvh - jnp.mean(dvh, axis=-1, keepdims=True) - vhat * jnp.mean(dvh * vhat, axis=-1, keepdims=True))
            dp = jnp.concatenate([du, dv], axis=1) * _gelu_grad(p)
            dbin_ref[...] += _rsum8(dp)
            dp_ref[rows, :] = dp.astype(BF16)

    full3 = pl.BlockSpec((SGU_G, C, C), lambda i: (0, 0, 0))
    return pl.pallas_call(
        body, name=name, grid=(T // tt,),
        in_specs=[pl.BlockSpec((tt, 2 * Wd), lambda i: (i, 0)), pl.BlockSpec((tt, Wd), lambda i: (i, 0)),
                  _resident((1, Wd)), _resident((1, Wd)), _resident((SGU_G, C, C)), _resident((SGU_G, C, C)),
                  _resident((SGU_G, C, C))],
        out_specs=[pl.BlockSpec((tt, 2 * Wd), lambda i: (i, 0)), full3, full3,
                   pl.BlockSpec((SUBLANES, Wd), lambda i: (0, 0)), pl.BlockSpec((SUBLANES, Wd), lambda i: (0, 0)),
                   pl.BlockSpec((SUBLANES, 2 * Wd), lambda i: (0, 0))],
        out_shape=[jax.ShapeDtypeStruct((T, 2 * Wd), BF16), jax.ShapeDtypeStruct((SGU_G, C, C), F32),
                   jax.ShapeDtypeStruct((SGU_G, C, C), F32), jax.ShapeDtypeStruct((SUBLANES, Wd), F32),
                   jax.ShapeDtypeStruct((SUBLANES, Wd), F32), jax.ShapeDtypeStruct((SUBLANES, 2 * Wd), F32)],
        compiler_params=_cp(1),
    )(p1, ds, lg, lbias, wsc, wsct, bsb)


def _pad_rows(w, rows=SUBLANES):
    return jnp.pad(w, ((0, rows - w.shape[0]), (0, 0)))


def _block_diag(w):
    n, b, _ = w.shape
    return (w[:, :, None, :] * jnp.eye(n, dtype=w.dtype)[:, None, :, None]).reshape(n * b, n * b)


def _diag_blocks(m, n):
    b = m.shape[0] // n
    m4 = m.reshape(n, b, n, b)
    return jnp.stack([m4[k, :, k, :] for k in range(n)], axis=0)


def _piece_major(dw):
    if dw.ndim == 2:
        K, N = dw.shape
        return dw.reshape(N_CHIPS, 2, K // (2 * N_CHIPS), N)
    _, K, ns = dw.shape
    return dw.reshape(N_CHIPS, 2, K // 2, ns)


def _ffn_fwd(h, g, w_up, cw, cb, w_down, tag, loss=None):
    hn, gu = _norm_mm(h, g, w_up, jnp.zeros((1, 2 * D_FF), F32), name=f"ffn_up_{tag}")
    a, gc, *out = _ffn_act(gu, cw, cb, w_down, h, name=f"ffn_act_down_{tag}", loss=loss)
    return (out[0] if loss is None else out), (hn, gu, gc, a)


def _ffn_bwd(dh, h, g, saved, w_up, cw, w_down, tag):
    hn, gu, gc, a = saved
    dwd = _mm_tn(a, dh, name=f"ffn_dwd_{tag}")
    dgu, dc, dhin, dg8 = _ffn_act_bwd(gu, gc, cw, w_up, w_down, h, g, dh, name=f"ffn_bwd_{tag}")
    dwu = _mm_tn(hn, dgu, name=f"ffn_dwu_{tag}", col_shards=N_CHIPS)
    dcs = dc.sum(axis=1)
    return dhin, dg8.sum(axis=0), dwu, dcs[0:3], dcs[3], dwd


REDUCE_GROUPS = {"g1": [("ffn_w_up", 1), ("ffn_w_down", 1), ("od_w_out", 0), ("od_w_in", 0)],
                 "g2": [("ffn_w_up", 0), ("ffn_w_down", 0)],
                 "g3": [("ev_w_out", 0), ("ev_w_in", 0)]}


def _local_step(x, tgt, p, start_reduce=None, continue_reduce=None):
    row = lambda v: v.reshape(1, -1)
    grads = {}

    lower = jax.nn.softmax(p["hg_lb_logits"], axis=0)
    lb0 = row(lower[0])
    ev_cw = _pad_rows(p["ev_conv_w"][0])
    ev_cb = row(p["ev_conv_b"][0])
    wa = _block_diag(p["ev_gate_a_w"][0]).astype(BF16)
    wx = _block_diag(p["ev_gate_x_w"][0]).astype(BF16)
    ba, bx, lam = row(p["ev_gate_a_b"][0]), row(p["ev_gate_x_b"][0]), row(p["ev_lru_lambda"][0])
    gn = row(p["ev_hg_norm"][0])
    tril = jnp.tril(jnp.ones((SGU_CHUNK, SGU_CHUNK), F32))
    wsc = (p["od_w_s"][0] * tril).astype(BF16)
    bsb = jnp.broadcast_to(p["od_b_s"][0][:, :, None], (SGU_G, SGU_CHUNK, SGU_CHUNK)).astype(F32)
    ffn_cw = [_pad_rows(p["ffn_conv_w"][l]) for l in range(2)]
    ffn_cb = [row(p["ffn_conv_b"][l]) for l in range(2)]
    ev_w_in, ev_w_out = p["ev_w_in"][0], p["ev_w_out"][0]
    nm = [row(p["norm_mix"][l]) for l in range(2)]
    nf = [row(p["norm_ffn"][l]) for l in range(2)]

    h0 = x
    hn0, z0 = _norm_mm(h0, nm[0], ev_w_in, jnp.zeros((1, ev_w_in.shape[1]), F32), name="ev_in")
    out_a, hseq, a_all, lru_saved = _lru_fwd(z0, ev_cw, ev_cb, wa, ba, wx, bx, lam)
    mix0, ss = _hgrn_fwd(z0, lb0, gn, out_a)
    h1 = _mm(mix0, ev_w_out, h0, F32, name="ev_out")
    late = p["late"](h1) if "late" in p else p
    od_w_in, od_w_out = late["od_w_in"][0], late["od_w_out"][0]
    w_up = [(late["ffn_w_up"], l) for l in range(2)]
    w_down = [(late["ffn_w_down"], l) for l in range(2)]
    h2, ffn0 = _ffn_fwd(h1, nf[0], w_up[0], ffn_cw[0], ffn_cb[0], w_down[0], "l0")
    hn1, p1 = _norm_mm(h2, nm[1], od_w_in, row(p["od_b_in"][0]), name="od_in")
    s1 = _sgu_fwd(p1, row(p["od_ln_g"][0]), row(p["od_ln_b"][0]), wsc, bsb)
    h3 = _mm(s1, od_w_out, h2, F32, name="od_out")
    (dh4, sq8, dgf8), ffn1 = _ffn_fwd(h3, nf[1], w_up[1], ffn_cw[1], ffn_cb[1], w_down[1], "l1", loss=(row(p["norm_final"]), tgt))
    grads["norm_final"] = dgf8.sum(axis=0)

    big = {}
    dh3, dnf1, dwu1, dcw1, dcb1, dwd1 = _ffn_bwd(dh4, h3, nf[1], ffn1, w_up[1], ffn_cw[1], w_down[1], "l1")
    big["ffn_w_up", 1], big["ffn_w_down", 1] = _piece_major(dwu1), _piece_major(dwd1)
    ds1 = _mm(dh3, od_w_out, None, BF16, name="od_ds", transpose_w=True)
    big["od_w_out", 0] = _piece_major(_mm_tn(s1, dh3, name="od_dwo"))
    wsct = jnp.swapaxes(wsc, 1, 2)
    dp1, dws, dbs, dlg8, dlb8, dbin8 = _sgu_bwd(p1, ds1, row(p["od_ln_g"][0]), row(p["od_ln_b"][0]), wsc, wsct, bsb)
    grads["od_w_s"] = (dws * tril)[None]
    grads["od_b_s"] = dbs.sum(axis=-1)[None]
    grads["od_ln_g"] = dlg8.sum(axis=0)[None]
    grads["od_ln_b"] = dlb8.sum(axis=0)[None]
    grads["od_b_in"] = dbin8.sum(axis=0)[None]
    dh2, dnm1 = _mm_normbwd(dp1, od_w_in, h2, nm[1], dh3, name="od_dh")
    big["od_w_in", 0] = _piece_major(_mm_tn(hn1, dp1, name="od_dwi", col_shards=N_CHIPS))
    if start_reduce is not None:
        g1, dh2 = lax.optimization_barrier(([big[key] for key in REDUCE_GROUPS["g1"]], dh2))
        token = start_reduce("g1", g1, split=True)
        ffn_cw[0] = ffn_cw[0] + token[0:1, 0:1]

    dh1, dnf0, dwu0, dcw0, dcb0, dwd0 = _ffn_bwd(dh2, h1, nf[0], ffn0, w_up[0], ffn_cw[0], w_down[0], "l0")
    big["ffn_w_up", 0], big["ffn_w_down", 0] = _piece_major(dwu0), _piece_major(dwd0)
    if start_reduce is not None:
        token = continue_reduce("g1", dh1) + start_reduce("g2", [big[key] for key in REDUCE_GROUPS["g2"]], split=True)
        lam = lam + token[0:1, 0:1]
    dmix = _mm(dh1, ev_w_out, None, BF16, name="ev_dmix", transpose_w=True)
    big["ev_w_out", 0] = _piece_major(_mm_tn(mix0, dh1, name="ev_dwo"))
    dz01, dc5, dwa, dwx, dvec = _lru_bwd(z0, hseq, a_all, lru_saved, dmix, ev_cw, wa.T, wx.T, lam)
    if start_reduce is not None:
        lb0 = lb0 + continue_reduce("g2", dc5)[0:1, 0:1]
    dz0, dlb8, dgn8 = _hgrn_bwd(z0, ss, dmix, lb0, gn, dz01)
    big["ev_w_in", 0] = _piece_major(_mm_tn(hn0, dz0, name="ev_dwi", col_shards=N_CHIPS))
    if start_reduce is not None:
        token = start_reduce("g3", [big[key] for key in REDUCE_GROUPS["g3"]])
        nm[0] = nm[0] + token[0:1, 0:1]
    grad_x, dnm0 = _mm_normbwd(dz0, ev_w_in, h0, nm[0], dh1, name="ev_dh")

    dc5s = dc5.sum(axis=1)
    grads["ev_conv_w"] = dc5s[0:4][None]
    grads["ev_conv_b"] = dc5s[4][None]
    grads["ev_gate_a_w"] = _diag_blocks(dwa, LRU_BLOCKS)[None]
    grads["ev_gate_x_w"] = _diag_blocks(dwx, LRU_BLOCKS)[None]
    dvs = dvec.sum(axis=1)
    grads["ev_gate_a_b"] = dvs[0][None]
    grads["ev_gate_x_b"] = dvs[1][None]
    grads["ev_lru_lambda"] = (dvs[2] * (-jax.nn.sigmoid(-p["ev_lru_lambda"][0])))[None]
    dlb = dlb8.sum(axis=0)
    grads["hg_lb_logits"] = dlb[None, :] * lower[0][None, :] * (jnp.eye(3, dtype=F32)[0][:, None] - lower)
    grads["ev_hg_norm"] = dgn8.sum(axis=0)[None]
    grads["norm_mix"] = jnp.stack([dnm0.sum(axis=0), dnm1.sum(axis=0)])
    grads["norm_ffn"] = jnp.stack([dnf0, dnf1])
    grads["ffn_conv_w"] = jnp.stack([dcw0, dcw1])
    grads["ffn_conv_b"] = jnp.stack([dcb0, dcb1])
    return sq8, grad_x, grads, big


SH_BIG = {"ev_w_in": 2, "ev_w_out": 1, "od_w_in": 2, "od_w_out": 1, "ffn_w_up": 2, "ffn_w_down": 1}
SH_SMALL = {"ev_conv_w": 2, "od_b_in": 1, "od_ln_g": 1, "od_ln_b": 1, "ffn_conv_w": 2}
REP = ["norm_mix", "norm_ffn", "norm_final", "ev_conv_b", "ev_gate_a_w", "ev_gate_a_b", "ev_gate_x_w", "ev_gate_x_b",
       "ev_lru_lambda", "hg_lb_logits", "ev_hg_norm", "od_w_s", "od_b_s", "ffn_conv_b"]
WEIGHTS = ["norm_mix", "norm_ffn", "norm_final", "ev_w_in", "ev_conv_w", "ev_conv_b", "ev_gate_a_w", "ev_gate_a_b", "ev_gate_x_w",
           "ev_gate_x_b", "ev_lru_lambda", "hg_lb_logits", "ev_hg_norm", "ev_w_out", "od_w_in", "od_b_in", "od_ln_g", "od_ln_b",
           "od_w_s", "od_b_s", "od_w_out", "ffn_w_up", "ffn_conv_w", "ffn_conv_b", "ffn_w_down"]


def _rows(n_elems, mult=SUBLANES):
    r = -(-n_elems // LANES)
    return -(-r // mult) * mult


def _pack(arrs, rows, dtype):
    flat = jnp.concatenate([a.reshape(-1).astype(dtype) for a in arrs])
    return jnp.pad(flat, (0, rows * LANES - flat.shape[0])).reshape(rows, LANES)


def _unpack(flat2d, shapes):
    flat = flat2d.reshape(-1)
    out, off = [], 0
    for s in shapes:
        n = 1
        for d in s:
            n *= d
        out.append(flat[off:off + n].reshape(s))
        off += n
    return out


def _mesh_pos():
    return lax.axis_index("x"), lax.axis_index("y"), lax.axis_index("c")


def _other_chips(x, y):
    return [(1 - x, y), (x, 1 - y), (1 - x, 1 - y)]


def _half_rows(n):
    return lambda r, c: r.at[0, pl.ds(c * (n // 2), n // 2), :]


GATHER_BIG = {
    "ev_w_in": ((1024, 3072), _half_rows(1024), lambda o, k, c: o.at[pl.ds(c * 512, 512), pl.ds(k * 768, 768)]),
    "ev_w_out": ((1024, 1024), _half_rows(256), lambda o, k, c: o.at[pl.ds(k * 256 + c * 128, 128), :]),
    "od_w_in": ((1024, 2048), _half_rows(1024), lambda o, k, c: o.at[pl.ds(c * 512, 512), pl.ds(k * 512, 512)]),
    "od_w_out": ((1024, 1024), _half_rows(256), lambda o, k, c: o.at[pl.ds(k * 256 + c * 128, 128), :]),
    "ffn_w_up": ((2, 1024, 2 * D_FF), lambda r, c: r.at[c], lambda o, k, c: o.at[c, :, pl.ds(k * (2 * D_FF // 4), 2 * D_FF // 4)]),
    "ffn_w_down": ((2, D_FF, 1024), lambda r, c: r.at[c], lambda o, k, c: o.at[c, pl.ds(k * (D_FF // 4), D_FF // 4), :]),
}


def _gather_weights(names, big, small):
    nb = len(big)
    descs = [GATHER_BIG[n] for n in names]
    rs = small.shape[0]

    def body(*refs):
        ins, s_ref = refs[:nb], refs[nb]
        outs, os_ref = refs[nb + 1:2 * nb + 1], refs[2 * nb + 1]
        ici_send, ici_recv, d2d_send, d2d_recv, loc_sems = refs[2 * nb + 2:2 * nb + 7]
        vbufs = refs[2 * nb + 7:]
        x, y, c = _mesh_pos()
        k = 2 * x + y
        chips = _other_chips(x, y)
        sib = (x, y, 1 - c)

        def remote(src, dst, ssem, rsem, to):
            return pltpu.make_async_remote_copy(src_ref=src, dst_ref=dst, send_sem=ssem, recv_sem=rsem, device_id=to,
                                                device_id_type=MESH)

        stage = [pltpu.make_async_copy(ins[t], vbufs[t], loc_sems.at[2 * t]) for t in range(nb)]
        stage.append(pltpu.make_async_copy(s_ref, vbufs[nb], loc_sems.at[2 * nb]))
        for cp in stage:
            cp.start()
        sends = []
        for t, (_, src, dst) in enumerate(descs):
            for j, (px, py) in enumerate(chips):
                sends.append(remote(src(ins[t], c), dst(outs[t], k, c), ici_send.at[3 * t + j], ici_recv.at[3 * t + j], (px, py, c)))
        for j, (px, py) in enumerate(chips):
            sends.append(remote(s_ref, os_ref.at[k], ici_send.at[3 * nb + j], ici_recv.at[3 * nb + j], (px, py, c)))
        for cp in sends:
            cp.start()
        for cp in stage:
            cp.wait()
        local = []
        for t, (_, src, dst) in enumerate(descs):
            for cc in (0, 1):
                local.append(pltpu.make_async_copy(src(vbufs[t], cc), dst(outs[t], k, cc), loc_sems.at[2 * t + cc]))
        local.append(pltpu.make_async_copy(vbufs[nb], os_ref.at[k], loc_sems.at[2 * nb]))
        for cp in local:
            cp.start()
        for t, (_, src, dst) in enumerate(descs):
            for j, (px, py) in enumerate(chips):
                got = dst(outs[t], 2 * px + py, c)
                remote(got, got, ici_send.at[3 * t + j], ici_recv.at[3 * t + j], (px, py, c)).wait_recv()
                fwd = remote(got, got, d2d_send.at[3 * t + j], d2d_recv.at[3 * t + j], sib)
                fwd.start()
                sends.append(fwd)
        for j, (px, py) in enumerate(chips):
            remote(s_ref, os_ref.at[2 * px + py], ici_send.at[3 * nb + j], ici_recv.at[3 * nb + j], (px, py, c)).wait_recv()
        for t, (_, src, dst) in enumerate(descs):
            for j, (px, py) in enumerate(chips):
                theirs = dst(outs[t], 2 * px + py, 1 - c)
                remote(theirs, theirs, d2d_send.at[3 * t + j], d2d_recv.at[3 * t + j], sib).wait_recv()
        for cp in sends:
            cp.wait_send()
        for cp in local:
            cp.wait()

    out_shape = [jax.ShapeDtypeStruct(d[0], BF16) for d in descs] + [jax.ShapeDtypeStruct((N_CHIPS, rs, LANES), small.dtype)]
    return pl.pallas_call(
        body, name="gather_weights", in_specs=[ANY] * (nb + 1), out_specs=[ANY] * (nb + 1), out_shape=out_shape,
        scratch_shapes=[pltpu.SemaphoreType.DMA((3 * nb + 3,)), pltpu.SemaphoreType.DMA((3 * nb + 3,)),
                        pltpu.SemaphoreType.DMA((3 * nb,)), pltpu.SemaphoreType.DMA((3 * nb,)),
                        pltpu.SemaphoreType.DMA((2 * nb + 1,))]
        + [pltpu.VMEM(b.shape, b.dtype) for b in big] + [pltpu.VMEM(small.shape, small.dtype)],
        compiler_params=pltpu.CompilerParams(vmem_limit_bytes=VMEM_LIMIT),
    )(*big, small)


def _place_own(names, big):
    nb = len(big)
    descs = [GATHER_BIG[n] for n in names]

    def body(*refs):
        ins, outs = refs[:nb], refs[nb:2 * nb]
        sems, vbufs = refs[2 * nb], refs[2 * nb + 1:]
        x, y, c = _mesh_pos()
        k = 2 * x + y
        stage = [pltpu.make_async_copy(ins[t], vbufs[t], sems.at[2 * t]) for t in range(nb)]
        for cp in stage:
            cp.start()
        for cp in stage:
            cp.wait()
        local = [pltpu.make_async_copy(src(vbufs[t], cc), dst(outs[t], k, cc), sems.at[2 * t + cc])
                 for t, (_, src, dst) in enumerate(descs) for cc in (0, 1)]
        for cp in local:
            cp.start()
        for cp in local:
            cp.wait()

    return pl.pallas_call(
        body, name="place_own", in_specs=[ANY] * nb, out_specs=[ANY] * nb,
        out_shape=[jax.ShapeDtypeStruct(d[0], BF16) for d in descs],
        scratch_shapes=[pltpu.SemaphoreType.DMA((2 * nb,))] + [pltpu.VMEM(b.shape, b.dtype) for b in big],
        compiler_params=pltpu.CompilerParams(vmem_limit_bytes=VMEM_LIMIT),
    )(*big)


def _gather_start(names, big, bufs):
    nb = len(big)
    descs = [GATHER_BIG[n] for n in names]

    def body(*refs):
        ins, lnd = refs[:nb], refs[nb:2 * nb]
        send_sems, recv_sems, token = refs[2 * nb], refs[2 * nb + 1], refs[-1]
        x, y, c = _mesh_pos()
        k = 2 * x + y
        for t, (_, src, dst) in enumerate(descs):
            for j, (px, py) in enumerate(_other_chips(x, y)):
                _remote(src(ins[t], c), dst(lnd[t], k, c), send_sems.at[3 * t + j], recv_sems.at[3 * t + j], (px, py, c)).start()
        token[...] = jnp.zeros_like(token)

    out = pl.pallas_call(
        body, name="gather_start",
        out_shape=(pltpu.SemaphoreType.DMA((3 * nb,)), pltpu.SemaphoreType.DMA((3 * nb,)),
                   *[pltpu.HBM(b.shape, b.dtype) for b in big], *[pltpu.HBM(b.shape, b.dtype) for b in bufs],
                   jax.ShapeDtypeStruct((SUBLANES, LANES), F32)),
        in_specs=[HBM] * (2 * nb), out_specs=(SEM, SEM, *[HBM] * (2 * nb), pl.BlockSpec(memory_space=pltpu.VMEM)),
        input_output_aliases={i: 2 + i for i in range(2 * nb)},
        compiler_params=pltpu.CompilerParams(has_side_effects=DATAFLOW),
    )(*[pltpu.with_memory_space_constraint(b, pltpu.HBM) for b in big], *[pltpu.with_memory_space_constraint(b, pltpu.HBM) for b in bufs])
    return out[0], out[1], list(out[2:2 + nb]), list(out[2 + nb:2 + 2 * nb]), out[-1]


def _gather_wait(names, send_sems, recv_sems, big, bufs, after):
    nb = len(big)
    descs = [GATHER_BIG[n] for n in names]

    def body(*refs):
        ins, lnd = refs[:nb], refs[nb:2 * nb]
        ssem, rsem = refs[2 * nb], refs[2 * nb + 1]
        x, y, c = _mesh_pos()
        for t, (_, src, dst) in enumerate(descs):
            for j, (px, py) in enumerate(_other_chips(x, y)):
                cp = _remote(src(ins[t], c), dst(lnd[t], 2 * px + py, c), ssem.at[3 * t + j], rsem.at[3 * t + j], (px, py, c))
                cp.wait_send()
                cp.wait_recv()

    out = pl.pallas_call(
        body, name="gather_wait",
        out_shape=(*[pltpu.HBM(b.shape, b.dtype) for b in big], *[pltpu.HBM(b.shape, b.dtype) for b in bufs]),
        in_specs=[HBM] * (2 * nb) + [SEM, SEM, ANY], out_specs=tuple([HBM] * (2 * nb)),
        input_output_aliases={i: i for i in range(2 * nb)},
        compiler_params=pltpu.CompilerParams(has_side_effects=DATAFLOW),
    )(*big, *bufs, send_sems, recv_sems, after)
    return list(out[nb:])


def _gather_forward(names, bufs):
    nb = len(bufs)
    descs = [GATHER_BIG[n] for n in names]

    def body(*refs):
        outs = refs[nb:2 * nb]
        send_sems, recv_sems = refs[2 * nb:]
        x, y, c = _mesh_pos()
        sib = (x, y, 1 - c)
        sends = []
        for t, (_, src, dst) in enumerate(descs):
            for j, (px, py) in enumerate(_other_chips(x, y)):
                got = dst(outs[t], 2 * px + py, c)
                sends.append(_remote(got, got, send_sems.at[3 * t + j], recv_sems.at[3 * t + j], sib))
        for cp in sends:
            cp.start()
        for t, (_, src, dst) in enumerate(descs):
            for j, (px, py) in enumerate(_other_chips(x, y)):
                theirs = dst(outs[t], 2 * px + py, 1 - c)
                _remote(theirs, theirs, send_sems.at[3 * t + j], recv_sems.at[3 * t + j], sib).wait_recv()
        for cp in sends:
            cp.wait_send()

    return pl.pallas_call(
        body, name="gather_forward", in_specs=[ANY] * nb, out_specs=[ANY] * nb,
        out_shape=[jax.ShapeDtypeStruct(b.shape, b.dtype) for b in bufs], input_output_aliases={t: t for t in range(nb)},
        scratch_shapes=[pltpu.SemaphoreType.DMA((3 * nb,)), pltpu.SemaphoreType.DMA((3 * nb,))],
    )(*bufs)


def _remote(src, dst, ssem, rsem, to):
    return pltpu.make_async_remote_copy(src_ref=src, dst_ref=dst, send_sem=ssem, recv_sem=rsem, device_id=to, device_id_type=MESH)


def _rs_send_sibling(gs, tag):
    n = len(gs)
    ns = sum(N_CHIPS if g.ndim == 4 else 1 for g in gs)

    def body(*refs):
        cps = _sibling_copies(gs, refs[:n], refs[n:2 * n], refs[2 * n], refs[2 * n + 1])
        for cp in cps:
            cp.start()
        for cp in cps:
            cp.wait()

    out_shape = [jax.ShapeDtypeStruct(g.shape[:1] + g.shape[2:] if g.ndim == 4 else g.shape[1:], g.dtype) for g in gs]
    return pl.pallas_call(
        body, name=f"rs_send_sibling_{tag}", in_specs=[ANY] * n, out_specs=[ANY] * n, out_shape=out_shape,
        scratch_shapes=[pltpu.SemaphoreType.DMA((ns,)), pltpu.SemaphoreType.DMA((ns,))],
    )(*gs)


def _add_piece(g, recv, c, name):
    P, Q = g.shape[-2:]

    def body(c_ref, g_ref, r_ref, o_ref):
        o_ref[...] = g_ref[...].reshape(o_ref.shape) + r_ref[...]

    if g.ndim == 4:
        grid = (N_CHIPS,)
        in_specs = [pl.BlockSpec((1, 1, P, Q), lambda k, c_ref: (k, c_ref[0], 0, 0)), pl.BlockSpec((1, P, Q), lambda k, c_ref: (k, 0, 0))]
        out_spec = pl.BlockSpec((1, P, Q), lambda k, c_ref: (k, 0, 0))
    else:
        grid = (1,)
        in_specs = [pl.BlockSpec((1, P, Q), lambda k, c_ref: (c_ref[0], 0, 0)), pl.BlockSpec((P, Q), lambda k, c_ref: (0, 0))]
        out_spec = pl.BlockSpec((P, Q), lambda k, c_ref: (0, 0))
    return pl.pallas_call(
        body, name=name,
        grid_spec=pltpu.PrefetchScalarGridSpec(num_scalar_prefetch=1, grid=grid, in_specs=in_specs, out_specs=out_spec),
        out_shape=jax.ShapeDtypeStruct(recv.shape, g.dtype),
        compiler_params=_cp(1),
    )(c, g, recv)


def _sibling_copies(gs, srcs, dsts, send_sems, recv_sems):
    x, y, c = _mesh_pos()
    cps, s = [], 0
    for t, g in enumerate(gs):
        if g.ndim == 4:
            for k in range(N_CHIPS):
                cps.append(_remote(srcs[t].at[k, 1 - c], dsts[t].at[k], send_sems.at[s], recv_sems.at[s], (x, y, 1 - c)))
                s += 1
        else:
            cps.append(_remote(srcs[t].at[1 - c], dsts[t], send_sems.at[s], recv_sems.at[s], (x, y, 1 - c)))
            s += 1
    return cps


def _sibling_start(gs, tag):
    n = len(gs)
    ns = sum(N_CHIPS if g.ndim == 4 else 1 for g in gs)
    lands = [pltpu.with_memory_space_constraint(lax.empty(g.shape[:1] + g.shape[2:] if g.ndim == 4 else g.shape[1:], g.dtype), pltpu.HBM)
             for g in gs]

    def body(*refs):
        for cp in _sibling_copies(gs, refs[:n], refs[n:2 * n], refs[2 * n], refs[2 * n + 1]):
            cp.start()
        refs[-1][...] = jnp.zeros_like(refs[-1])

    out = pl.pallas_call(
        body, name=f"sibling_start_{tag}",
        out_shape=(pltpu.SemaphoreType.DMA((ns,)), pltpu.SemaphoreType.DMA((ns,)),
                   *[pltpu.HBM(g.shape, g.dtype) for g in gs], *[pltpu.HBM(l.shape, l.dtype) for l in lands],
                   jax.ShapeDtypeStruct((SUBLANES, LANES), F32)),
        in_specs=[HBM] * (2 * n), out_specs=(SEM, SEM, *[HBM] * (2 * n), pl.BlockSpec(memory_space=pltpu.VMEM)),
        input_output_aliases={i: 2 + i for i in range(2 * n)},
        compiler_params=pltpu.CompilerParams(has_side_effects=DATAFLOW),
    )(*[pltpu.with_memory_space_constraint(g, pltpu.HBM) for g in gs], *lands)
    return (out[0], out[1], list(out[2:2 + n]), list(out[2 + n:2 + 2 * n])), out[-1]


def _sibling_wait(send_sems, recv_sems, gs, lands, after, tag):
    n = len(gs)

    def body(*refs):
        for cp in _sibling_copies(gs, refs[:n], refs[n:2 * n], refs[2 * n], refs[2 * n + 1]):
            cp.wait_send()
            cp.wait_recv()

    out = pl.pallas_call(
        body, name=f"sibling_wait_{tag}",
        out_shape=(*[pltpu.HBM(g.shape, g.dtype) for g in gs], *[pltpu.HBM(l.shape, l.dtype) for l in lands]),
        in_specs=[HBM] * (2 * n) + [SEM, SEM, ANY], out_specs=tuple([HBM] * (2 * n)),
        input_output_aliases={i: i for i in range(2 * n)},
        compiler_params=pltpu.CompilerParams(has_side_effects=DATAFLOW),
    )(*gs, *lands, send_sems, recv_sems, after)
    return list(out[:n]), list(out[n:])


def _chips_start(hs, tag):
    n = len(hs)
    lands = [pltpu.with_memory_space_constraint(lax.empty((N_CHIPS,) + h.shape[-2:], h.dtype), pltpu.HBM) for h in hs]

    def body(*refs):
        ins, lnd = refs[:n], refs[n:2 * n]
        send_sems, recv_sems, token = refs[2 * n], refs[2 * n + 1], refs[-1]
        x, y, c = _mesh_pos()
        k = 2 * x + y
        piece = lambda t, kk: ins[t].at[kk] if hs[t].ndim == 3 else ins[t]
        for t in range(n):
            for j, (px, py) in enumerate(_other_chips(x, y)):
                _remote(piece(t, 2 * px + py), lnd[t].at[k], send_sems.at[3 * t + j], recv_sems.at[3 * t + j], (px, py, c)).start()
        token[...] = jnp.zeros_like(token)

    out = pl.pallas_call(
        body, name=f"chips_start_{tag}",
        out_shape=(pltpu.SemaphoreType.DMA((3 * n,)), pltpu.SemaphoreType.DMA((3 * n,)),
                   *[pltpu.HBM(h.shape, h.dtype) for h in hs], *[pltpu.HBM(l.shape, l.dtype) for l in lands],
                   jax.ShapeDtypeStruct((SUBLANES, LANES), F32)),
        in_specs=[HBM] * (2 * n), out_specs=(SEM, SEM, *[HBM] * (2 * n), pl.BlockSpec(memory_space=pltpu.VMEM)),
        input_output_aliases={i: 2 + i for i in range(2 * n)},
        compiler_params=pltpu.CompilerParams(has_side_effects=DATAFLOW),
    )(*[pltpu.with_memory_space_constraint(h, pltpu.HBM) for h in hs], *lands)
    return out[0], out[1], list(out[2:2 + n]), list(out[2 + n:2 + 2 * n]), out[-1]


def _chips_wait(send_sems, recv_sems, hs, lands, after, tag):
    n = len(hs)

    def body(*refs):
        ins, lnd = refs[:n], refs[n:2 * n]
        ssem, rsem = refs[2 * n], refs[2 * n + 1]
        x, y, c = _mesh_pos()
        k = 2 * x + y
        piece = lambda t, kk: ins[t].at[kk] if hs[t].ndim == 3 else ins[t]
        for t in range(n):
            for j, (px, py) in enumerate(_other_chips(x, y)):
                cp = _remote(piece(t, k), lnd[t].at[2 * px + py], ssem.at[3 * t + j], rsem.at[3 * t + j], (px, py, c))
                cp.wait_send()
                cp.wait_recv()

    out = pl.pallas_call(
        body, name=f"chips_wait_{tag}",
        out_shape=(*[pltpu.HBM(h.shape, h.dtype) for h in hs], *[pltpu.HBM(l.shape, l.dtype) for l in lands]),
        in_specs=[HBM] * (2 * n) + [SEM, SEM, ANY], out_specs=tuple([HBM] * (2 * n)),
        input_output_aliases={i: i for i in range(2 * n)},
        compiler_params=pltpu.CompilerParams(has_side_effects=DATAFLOW),
    )(*hs, *lands, send_sems, recv_sems, after)
    return list(out[:n]), list(out[n:])


def _add_chips(p, own, kc, name):
    _, P, Q = p.shape
    tr = P
    while tr * Q * 4 > ELEMENTWISE_TILE_BYTES and tr % 16 == 0:
        tr //= 2
    sharded = own.ndim == 3

    def body(kc_ref, p_ref, own_ref, o_ref):
        k = kc_ref[0]
        mine = own_ref[...].reshape(tr, Q)
        v = [jnp.where(k == j, mine, p_ref[j]) for j in range(N_CHIPS)]
        o_ref[0] = ((v[0] + v[1]) + v[2]) + v[3]

    own_spec = (pl.BlockSpec((1, tr, Q), lambda i, kc_ref: (kc_ref[0], i, 0)) if sharded
                else pl.BlockSpec((tr, Q), lambda i, kc_ref: (i, 0)))
    return pl.pallas_call(
        body, name=name,
        grid_spec=pltpu.PrefetchScalarGridSpec(
            num_scalar_prefetch=1, grid=(P // tr,),
            in_specs=[pl.BlockSpec((N_CHIPS, tr, Q), lambda i, kc_ref: (0, i, 0)), own_spec],
            out_specs=pl.BlockSpec((1, tr, Q), lambda i, kc_ref: (kc_ref[1], i, 0))),
        out_shape=jax.ShapeDtypeStruct((2, P, Q), p.dtype),
        compiler_params=_cp(1),
    )(kc, p, own)


def _rs_share(fs, tag):
    n = len(fs)

    def body(*refs):
        outs = refs[n:2 * n]
        send_sems, recv_sems = refs[2 * n:]
        x, y, c = _mesh_pos()
        sends = [_remote(outs[t].at[c], outs[t].at[c], send_sems.at[t], recv_sems.at[t], (x, y, 1 - c)) for t in range(n)]
        for cp in sends:
            cp.start()
        for t in range(n):
            _remote(outs[t].at[c], outs[t].at[1 - c], send_sems.at[t], recv_sems.at[t], (x, y, 1 - c)).wait_recv()
        for cp in sends:
            cp.wait_send()

    return pl.pallas_call(
        body, name=f"rs_share_{tag}", in_specs=[ANY] * n, out_specs=[ANY] * n,
        out_shape=[jax.ShapeDtypeStruct(f.shape, f.dtype) for f in fs], input_output_aliases={t: t for t in range(n)},
        scratch_shapes=[pltpu.SemaphoreType.DMA((n,)), pltpu.SemaphoreType.DMA((n,))],
    )(*fs)


def _reduce_start(gs, kc, tag):
    return _reduce_continue(gs, _rs_send_sibling(gs, tag), kc, tag)


def _reduce_continue(gs, from_sibling, kc, tag):
    chip_sums = [_add_piece(g, r, kc[1:], name=f"add_piece_{tag}_{t}") for t, (g, r) in enumerate(zip(gs, from_sibling))]
    send_sems, recv_sems, chip_sums, lands, token = _chips_start(chip_sums, tag)
    return (send_sems, recv_sems, chip_sums, lands, tag), token


def _reduce_finish(states, kc, after):
    mine = []
    for send_sems, recv_sems, chip_sums, lands, tag in states:
        chip_sums, from_chips = _chips_wait(send_sems, recv_sems, chip_sums, lands, after, tag)
        mine += [_add_chips(p, h, kc, name=f"add_chips_{tag}_{t}") for t, (p, h) in enumerate(zip(from_chips, chip_sums))]
    return _rs_share(mine, "all")


def _adamw(w, g, m, v, name):
    R, C = w.shape
    tr = R
    for cand in (512, 256, 128, 64, 32, 16, 8):
        if R % cand == 0 and cand * C * 4 <= ELEMENTWISE_TILE_BYTES:
            tr = cand
            break
    c1 = 1.0 / (1.0 - ADAM_B1 ** ADAM_STEP)
    c2 = 1.0 / (1.0 - ADAM_B2 ** ADAM_STEP)

    def body(w_ref, g_ref, m_ref, v_ref, d_ref, mo_ref, vo_ref):
        gv = g_ref[...]
        mn = ADAM_B1 * m_ref[...] + (1.0 - ADAM_B1) * gv
        vn = ADAM_B2 * v_ref[...] + (1.0 - ADAM_B2) * (gv * gv)
        mo_ref[...] = mn
        vo_ref[...] = vn
        d_ref[...] = -ADAM_LR * ((mn * c1) / (jnp.sqrt(vn * c2) + ADAM_EPS) + ADAM_WD * w_ref[...])

    spec = pl.BlockSpec((tr, C), lambda i: (i, 0))
    shp = jax.ShapeDtypeStruct((R, C), F32)
    return pl.pallas_call(body, name=name, grid=(R // tr,), in_specs=[spec] * 4, out_specs=[spec] * 3, out_shape=[shp] * 3,
                          compiler_params=_cp(1))(w, g, m, v)


def _adamw_many(ws, gs, ms, vs):
    n = len(ws)
    c1 = 1.0 / (1.0 - ADAM_B1 ** ADAM_STEP)
    c2 = 1.0 / (1.0 - ADAM_B2 ** ADAM_STEP)

    def body(*refs):
        w_refs, g_refs, m_refs, v_refs = (refs[k * n:(k + 1) * n] for k in range(4))
        outs = refs[4 * n:]
        for i in range(n):
            gv = g_refs[i][...]
            mn = ADAM_B1 * m_refs[i][...] + (1.0 - ADAM_B1) * gv
            vn = ADAM_B2 * v_refs[i][...] + (1.0 - ADAM_B2) * (gv * gv)
            outs[3 * i][...] = -ADAM_LR * ((mn * c1) / (jnp.sqrt(vn * c2) + ADAM_EPS) + ADAM_WD * w_refs[i][...])
            outs[3 * i + 1][...] = mn
            outs[3 * i + 2][...] = vn

    out_shape = [jax.ShapeDtypeStruct(w.shape, F32) for w in ws for _ in range(3)]
    return pl.pallas_call(body, name="adamw_small", out_shape=out_shape)(*ws, *gs, *ms, *vs)


def _step(a):
    x, y, c = _mesh_pos()
    kc = jnp.stack([2 * x + y, c]).astype(jnp.int32)

    rs = _rows(sum(a[n].size for n in SH_SMALL))
    first, later = ["ev_w_in", "ev_w_out"], ["od_w_in", "od_w_out", "ffn_w_up", "ffn_w_down"]
    lead = lambda w: w if w.ndim == 3 else w[None]
    *full, gs = _gather_weights(first, [a[n].astype(BF16) for n in first], _pack([a[n] for n in SH_SMALL], rs, F32))
    p = {n: a[n] for n in REP}
    p.update({n: lead(w) for n, w in zip(first, full)})
    parts = [_unpack(gs[k], [a[n].shape for n in SH_SMALL]) for k in range(N_CHIPS)]
    for i, n in enumerate(SH_SMALL):
        p[n] = jnp.concatenate([parts[k][i] for k in range(N_CHIPS)], axis=SH_SMALL[n])
    shards = lax.optimization_barrier(([a[n].astype(BF16) for n in later], full))[0]
    g_send, g_recv, shards, bufs, token = _gather_start(later, shards, _place_own(later, shards))
    p["norm_mix"] = p["norm_mix"] + token[0:1, 0:1]

    def late(after):
        got = _gather_forward(later, _gather_wait(later, g_send, g_recv, shards, bufs, after))
        return {n: lead(w) for n, w in zip(later, got)}

    p["late"] = late

    states, pending = {}, {}

    def start_reduce(tag, gs, split=False):
        if split:
            pending[tag], token = _sibling_start(gs, tag)
        else:
            states[tag], token = _reduce_start(gs, kc, tag)
        return token

    def continue_reduce(tag, after):
        send_sems, recv_sems, gs, lands = pending.pop(tag)
        gs, from_sibling = _sibling_wait(send_sems, recv_sems, gs, lands, after, tag)
        states[tag], token = _reduce_continue(gs, from_sibling, kc, tag)
        return token

    sq8, grad_x, grads, big = _local_step(a["x"][0], a["loss_target"][0], p, start_reduce, continue_reduce)
    loss_part = (0.5 / D_MODEL * jnp.sum(sq8)).reshape(1)

    r_s = _rows(sum(a[n].size for n in SH_SMALL), 2 * SUBLANES) // 2
    small_pieces = []
    for k in range(N_CHIPS):
        pieces = [lax.slice_in_dim(grads[n], k * a[n].shape[ax], (k + 1) * a[n].shape[ax], axis=ax) for n, ax in SH_SMALL.items()]
        small_pieces.append(_pack(pieces, 2 * r_s, F32).reshape(2, r_s, LANES))
    g_small = jnp.stack(small_pieces)
    r_r = _rows(sum(a[n].size for n in REP) + 1, 2 * SUBLANES) // 2
    g_rep = _pack([grads[n] for n in REP] + [loss_part], 2 * r_r, F32).reshape(2, r_r, LANES)
    token = start_reduce("g4", [g_small, g_rep])
    reduced = _reduce_finish([states[tag] for tag in ("g1", "g2", "g3", "g4")], kc, token)
    red = dict(zip([key for tag in ("g1", "g2", "g3") for key in REDUCE_GROUPS[tag]], reduced))
    gfin = {}
    for n in ("ev_w_in", "ev_w_out", "od_w_in", "od_w_out"):
        gfin[n] = red[n, 0].reshape(a[n].shape)
    for n in ("ffn_w_up", "ffn_w_down"):
        gfin[n] = jnp.stack([red[n, l].reshape(a[n].shape[1:]) for l in range(2)])
    gfin.update(zip(SH_SMALL, _unpack(reduced[-2], [a[n].shape for n in SH_SMALL])))
    *rep_grads, loss = _unpack(reduced[-1], [a[n].shape for n in REP] + [(1,)])
    gfin.update(zip(REP, rep_grads))

    out = {"loss": loss.reshape(()), "grad_x": grad_x[None]}
    small_names = list(SH_SMALL) + REP
    for n in SH_BIG:
        shp = a[n].shape
        two_d = lambda t: t.reshape(-1, shp[-1])
        d, mo, vo = _adamw(two_d(a[n]), two_d(gfin[n]), two_d(a["m_" + n]), two_d(a["v_" + n]), name=f"adamw_{n}")
        out["delta_" + n], out["new_m_" + n], out["new_v_" + n] = d.reshape(shp), mo.reshape(shp), vo.reshape(shp)
    two_d = lambda t: t.reshape(-1, t.shape[-1])
    res = _adamw_many(*[[two_d(src(n)) for n in small_names]
                        for src in (lambda n: a[n], lambda n: gfin[n], lambda n: a["m_" + n], lambda n: a["v_" + n])])
    for i, n in enumerate(small_names):
        out["delta_" + n], out["new_m_" + n], out["new_v_" + n] = (r.reshape(a[n].shape) for r in res[3 * i:3 * i + 3])
    for n in WEIGHTS:
        out["grad_" + n] = gfin[n]
    return out


def kernel(x, norm_mix, norm_ffn, norm_final, ev_w_in, ev_conv_w, ev_conv_b, ev_gate_a_w, ev_gate_a_b, ev_gate_x_w, ev_gate_x_b, ev_lru_lambda, hg_lb_logits, ev_hg_norm, ev_w_out, od_w_in, od_b_in, od_ln_g, od_ln_b, od_w_s, od_b_s, od_w_out, ffn_w_up, ffn_conv_w, ffn_conv_b, ffn_w_down, loss_target, m_norm_mix, m_norm_ffn, m_norm_final, m_ev_w_in, m_ev_conv_w, m_ev_conv_b, m_ev_gate_a_w, m_ev_gate_a_b, m_ev_gate_x_w, m_ev_gate_x_b, m_ev_lru_lambda, m_hg_lb_logits, m_ev_hg_norm, m_ev_w_out, m_od_w_in, m_od_b_in, m_od_ln_g, m_od_ln_b, m_od_w_s, m_od_b_s, m_od_w_out, m_ffn_w_up, m_ffn_conv_w, m_ffn_conv_b, m_ffn_w_down, v_norm_mix, v_norm_ffn, v_norm_final, v_ev_w_in, v_ev_conv_w, v_ev_conv_b, v_ev_gate_a_w, v_ev_gate_a_b, v_ev_gate_x_w, v_ev_gate_x_b, v_ev_lru_lambda, v_hg_lb_logits, v_ev_hg_norm, v_ev_w_out, v_od_w_in, v_od_b_in, v_od_ln_g, v_od_ln_b, v_od_w_s, v_od_b_s, v_od_w_out, v_ffn_w_up, v_ffn_conv_w, v_ffn_conv_b, v_ffn_w_down):
    vals = (x, norm_mix, norm_ffn, norm_final, ev_w_in, ev_conv_w, ev_conv_b, ev_gate_a_w, ev_gate_a_b, ev_gate_x_w, ev_gate_x_b, ev_lru_lambda, hg_lb_logits, ev_hg_norm, ev_w_out, od_w_in, od_b_in, od_ln_g, od_ln_b, od_w_s, od_b_s, od_w_out, ffn_w_up, ffn_conv_w, ffn_conv_b, ffn_w_down, loss_target, m_norm_mix, m_norm_ffn, m_norm_final, m_ev_w_in, m_ev_conv_w, m_ev_conv_b, m_ev_gate_a_w, m_ev_gate_a_b, m_ev_gate_x_w, m_ev_gate_x_b, m_ev_lru_lambda, m_hg_lb_logits, m_ev_hg_norm, m_ev_w_out, m_od_w_in, m_od_b_in, m_od_ln_g, m_od_ln_b, m_od_w_s, m_od_b_s, m_od_w_out, m_ffn_w_up, m_ffn_conv_w, m_ffn_conv_b, m_ffn_w_down, v_norm_mix, v_norm_ffn, v_norm_final, v_ev_w_in, v_ev_conv_w, v_ev_conv_b, v_ev_gate_a_w, v_ev_gate_a_b, v_ev_gate_x_w, v_ev_gate_x_b, v_ev_lru_lambda, v_hg_lb_logits, v_ev_hg_norm, v_ev_w_out, v_od_w_in, v_od_b_in, v_od_ln_g, v_od_ln_b, v_od_w_s, v_od_b_s, v_od_w_out, v_ffn_w_up, v_ffn_conv_w, v_ffn_conv_b, v_ffn_w_down)
    names = ["x"] + WEIGHTS + ["loss_target"] + ["m_" + n for n in WEIGHTS] + ["v_" + n for n in WEIGHTS]
    out = _step(dict(zip(names, vals)))
    return (out["loss"], out["grad_x"], *[out["grad_" + n] for n in WEIGHTS], *[out["delta_" + n] for n in WEIGHTS],
            *[out["new_m_" + n] for n in WEIGHTS], *[out["new_v_" + n] for n in WEIGHTS])
```

```python
import jax
import jax.numpy as jnp
from jax import lax
from jax.experimental import pallas as pl
from jax.experimental.pallas import tpu as pltpu

F32 = jnp.float32
BF16 = jnp.bfloat16

EPS = 1e-6
D_MODEL = 1024
LRU_W = 512
LRU_BLOCKS = 8
LRU_C = 8.0
HG_HEADS = 4
HG_D = 128
HG_CHUNK = 64
SGU_G = 8
SGU_CHUNK = 128
D_FF = 2816
ADAM_LR, ADAM_B1, ADAM_B2, ADAM_EPS, ADAM_WD, ADAM_STEP = 0.001, 0.9, 0.999, 1e-08, 0.01, 10

V7X_VMEM_BYTES = 64 * 1024 * 1024
VMEM_LIMIT = V7X_VMEM_BYTES - 8 * 1024 * 1024
SUBLANES = 8
LANES = 128
BF16_ROWS = 16
ELEMENTWISE_TILE_BYTES = 2 * 1024 * 1024

N_CHIPS = 4
MESH = pl.DeviceIdType.MESH
ANY = pl.BlockSpec(memory_space=pl.ANY)
HBM = pl.BlockSpec(memory_space=pltpu.HBM)
SEM = pl.BlockSpec(memory_space=pltpu.SEMAPHORE)
DATAFLOW = pltpu.SideEffectType.DATAFLOW_SIDE_EFFECTING

GELU_C0 = 0.7978845608028654
GELU_C1 = 0.044715

NN = (((1,), (0,)), ((), ()))
NT = (((1,), (1,)), ((), ()))
TN = (((0,), (0,)), ((), ()))


def _dot(a, b, dims=NN):
    return lax.dot_general(a.astype(BF16), b.astype(BF16), dims, preferred_element_type=F32)


def _cp(n_grid):
    return pltpu.CompilerParams(dimension_semantics=("arbitrary",) * n_grid, vmem_limit_bytes=VMEM_LIMIT)


def _chunk(n, cap):
    best = LANES
    for c in range(LANES, cap + 1, LANES):
        if n % c == 0:
            best = c
    return best


def _resident(shape):
    nd = len(shape)
    return pl.BlockSpec(shape, lambda *_: (0,) * nd, pipeline_mode=pl.Buffered(1))


def _rsum8(x):
    r, c = x.shape
    return x.reshape(r // SUBLANES, SUBLANES, c).sum(axis=0)


def _sigmoid(x):
    return 0.5 * jnp.tanh(0.5 * x) + 0.5


def _gelu(x):
    return 0.5 * x * (1.0 + jnp.tanh(GELU_C0 * (x + GELU_C1 * x * x * x)))


def _gelu_grad(x):
    t = jnp.tanh(GELU_C0 * (x + GELU_C1 * x * x * x))
    return 0.5 * (1.0 + t) + 0.5 * x * (1.0 - t * t) * GELU_C0 * (1.0 + 3.0 * GELU_C1 * x * x)


def _silu_and_grad(x):
    s = _sigmoid(x)
    silu = x * s
    return silu, s + silu * (1.0 - s)


def _shift_rows(e, j):
    n = e.shape[0]
    return e if j % n == 0 else pltpu.roll(e, j % n, 0)


def _weight(w):
    if isinstance(w, tuple):
        stack, layer = w
        K, N = stack.shape[1:]
        return stack, pl.BlockSpec((None, K, N), lambda *_: (layer, 0, 0), pipeline_mode=pl.Buffered(1)), (K, N)
    return w, _resident(w.shape), w.shape


def _norm_mm(h, g, w, b, name, tt=1024):
    T, D = h.shape
    w, w_spec, (_, N) = _weight(w)
    cn = _chunk(N, 512)

    def body(h_ref, g_ref, w_ref, b_ref, hn_ref, z_ref):
        x = h_ref[...]
        r = lax.rsqrt(jnp.mean(x * x, axis=-1, keepdims=True) + EPS)
        hn = (x * r * g_ref[...]).astype(BF16)
        hn_ref[...] = hn
        for j in range(0, N, cn):
            acc = jnp.dot(hn, w_ref[:, j:j + cn], preferred_element_type=F32) + b_ref[:, j:j + cn]
            z_ref[:, j:j + cn] = acc.astype(BF16)

    return pl.pallas_call(
        body, name=name, grid=(T // tt,),
        in_specs=[pl.BlockSpec((tt, D), lambda i: (i, 0)), _resident((1, D)), w_spec, _resident((1, N))],
        out_specs=[pl.BlockSpec((tt, D), lambda i: (i, 0)), pl.BlockSpec((tt, N), lambda i: (i, 0))],
        out_shape=[jax.ShapeDtypeStruct((T, D), BF16), jax.ShapeDtypeStruct((T, N), BF16)],
        compiler_params=_cp(1),
    )(h, g, w, b)


def _mm(a, w, res, out_dtype, name, tt=1024, transpose_w=False):
    T, K = a.shape
    N = w.shape[0] if transpose_w else w.shape[1]
    cn = _chunk(N, 512)
    has_res = res is not None

    def body(*refs):
        a_ref, w_ref = refs[0], refs[1]
        res_ref = refs[2] if has_res else None
        o_ref = refs[-1]
        av = a_ref[...].astype(BF16)
        for j in range(0, N, cn):
            if transpose_w:
                acc = lax.dot_general(av, w_ref[j:j + cn, :], NT, preferred_element_type=F32)
            else:
                acc = jnp.dot(av, w_ref[:, j:j + cn], preferred_element_type=F32)
            if has_res:
                acc = acc + res_ref[:, j:j + cn]
            o_ref[:, j:j + cn] = acc.astype(out_dtype)

    in_specs = [pl.BlockSpec((tt, K), lambda i: (i, 0)), _resident(w.shape)]
    args = [a, w]
    if has_res:
        in_specs.append(pl.BlockSpec((tt, N), lambda i: (i, 0)))
        args.append(res)
    return pl.pallas_call(
        body, name=name, grid=(T // tt,), in_specs=in_specs,
        out_specs=pl.BlockSpec((tt, N), lambda i: (i, 0)),
        out_shape=jax.ShapeDtypeStruct((T, N), out_dtype),
        compiler_params=_cp(1),
    )(*args)


def _mm_tn(a, b, name, col_shards=1, tt=2048):
    T, K = a.shape
    N = b.shape[1]
    ns = N // col_shards
    tt = min(tt, T)
    while 2 * (tt * K * a.dtype.itemsize + tt * ns * b.dtype.itemsize + K * ns * 4) + K * ns * 4 > VMEM_LIMIT:
        tt //= 2

    def body(a_ref, b_ref, o_ref):
        acc = lax.dot_general(a_ref[...].astype(BF16), b_ref[...].astype(BF16), TN, preferred_element_type=F32)
        prev = jnp.where(pl.program_id(1) == 0, 0.0, o_ref[0])
        o_ref[0] = prev + acc

    out = pl.pallas_call(
        body, name=name, grid=(col_shards, T // tt),
        in_specs=[pl.BlockSpec((tt, K), lambda n, t: (t, 0)), pl.BlockSpec((tt, ns), lambda n, t: (t, n))],
        out_specs=pl.BlockSpec((1, K, ns), lambda n, t: (n, 0, 0)),
        out_shape=jax.ShapeDtypeStruct((col_shards, K, ns), F32),
        compiler_params=_cp(2),
    )(a, b)
    return out if col_shards > 1 else out[0]


def _mm_normbwd(dz, w, x, g, dres, name):
    T, N = dz.shape
    D = w.shape[0]
    tt = 1024 if N <= 3072 else 512

    def body(dz_ref, wt_ref, x_ref, g_ref, dres_ref, dx_ref, dg_ref):
        @pl.when(pl.program_id(0) == 0)
        def _():
            dg_ref[...] = jnp.zeros_like(dg_ref)

        dy = lax.dot_general(dz_ref[...], wt_ref[...], NT, preferred_element_type=F32)
        x = x_ref[...]
        r = lax.rsqrt(jnp.mean(x * x, axis=-1, keepdims=True) + EPS)
        xn = x * r
        dg_ref[...] += _rsum8(dy * xn)
        dxn = dy * g_ref[...]
        dx_ref[...] = dres_ref[...] + r * (dxn - xn * jnp.mean(dxn * xn, axis=-1, keepdims=True))

    return pl.pallas_call(
        body, name=name, grid=(T // tt,),
        in_specs=[pl.BlockSpec((tt, N), lambda i: (i, 0)), _resident((D, N)), pl.BlockSpec((tt, D), lambda i: (i, 0)),
                  _resident((1, D)), pl.BlockSpec((tt, D), lambda i: (i, 0))],
        out_specs=[pl.BlockSpec((tt, D), lambda i: (i, 0)), pl.BlockSpec((SUBLANES, D), lambda i: (0, 0))],
        out_shape=[jax.ShapeDtypeStruct((T, D), F32), jax.ShapeDtypeStruct((SUBLANES, D), F32)],
        compiler_params=_cp(1),
    )(dz, w, x, g, dres)


def _col_groups(F, cc, per_group=4):
    step = cc * per_group
    return [(g0, min(g0 + step, F)) for g0 in range(0, F, step)]


def _loss_head(x, gv, tgt):
    r = lax.rsqrt(jnp.mean(x * x, axis=-1, keepdims=True) + EPS)
    xn = x * r
    diff = xn * gv - tgt
    dy = diff * (1.0 / x.shape[-1])
    dxn = dy * gv
    return r * (dxn - xn * jnp.mean(dxn * xn, axis=-1, keepdims=True)), _rsum8(diff * diff), _rsum8(dy * xn)


def _ffn_act(gu, cw, cb, w_down, res, name, loss=None, tt=512):
    T = gu.shape[0]
    F = gu.shape[1] // 2
    w_down, wd_spec, (_, D) = _weight(w_down)
    cc = _chunk(F, 256)
    hb = tt // BF16_ROWS

    def body(gu_ref, halo_ref, cw_ref, cb_ref, wd_ref, res_ref, *rest):
        if loss is None:
            a_ref, gc_ref, o_ref = rest
        else:
            gf_ref, t_ref, a_ref, gc_ref, o_ref, sq_ref, dgf_ref = rest
        first = pl.program_id(0) == 0
        acc = res_ref[...]
        for g0, g1 in _col_groups(F, cc):
            for c0 in range(g0, g1, cc):
                cs = slice(c0, c0 + cc)
                x = gu_ref[:, cs].astype(F32)
                halo = jnp.where(first, 0.0, halo_ref[:, cs].astype(F32))
                e = jnp.concatenate([halo, x], axis=0)
                gc = (cb_ref[:, cs] + cw_ref[0:1, cs] * _shift_rows(e, 2)[BF16_ROWS:] + cw_ref[1:2, cs] * _shift_rows(e, 1)[BF16_ROWS:]
                      + cw_ref[2:3, cs] * x)
                up = gu_ref[:, F + c0:F + c0 + cc].astype(F32)
                gc_ref[:, cs] = gc.astype(BF16)
                a_ref[:, cs] = (gc * _sigmoid(gc) * up).astype(BF16)
            acc = acc + jnp.dot(a_ref[:, g0:g1], wd_ref[g0:g1, :], preferred_element_type=F32)
        if loss is None:
            o_ref[...] = acc
        else:
            @pl.when(first)
            def _():
                sq_ref[...] = jnp.zeros_like(sq_ref)
                dgf_ref[...] = jnp.zeros_like(dgf_ref)

            dx, sq, dgf = _loss_head(acc, gf_ref[...], t_ref[...])
            o_ref[...] = dx
            sq_ref[...] += sq
            dgf_ref[...] += dgf

    tok = lambda w: pl.BlockSpec((tt, w), lambda i: (i, 0))
    acc8 = pl.BlockSpec((SUBLANES, D), lambda i: (0, 0))
    in_specs = [tok(2 * F), pl.BlockSpec((BF16_ROWS, F), lambda i: (jnp.maximum(i * hb - 1, 0), 0)),
                _resident((SUBLANES, F)), _resident((1, F)), wd_spec, tok(D)]
    out_specs = [tok(F), tok(F), tok(D)]
    out_shape = [jax.ShapeDtypeStruct((T, F), BF16), jax.ShapeDtypeStruct((T, F), BF16), jax.ShapeDtypeStruct((T, D), F32)]
    args = [gu, gu, cw, cb, w_down, res]
    if loss is not None:
        in_specs += [_resident((1, D)), tok(D)]
        out_specs += [acc8, acc8]
        out_shape += [jax.ShapeDtypeStruct((SUBLANES, D), F32)] * 2
        args += list(loss)
    return pl.pallas_call(body, name=name, grid=(T // tt,), in_specs=in_specs, out_specs=out_specs, out_shape=out_shape,
                          compiler_params=_cp(1))(*args)


def _ffn_act_bwd(gu, gc, cw, w_up, w_down, x, g, dh, name, tt=256):
    T = gu.shape[0]
    F = gu.shape[1] // 2
    w_up, wu_spec, (D, _) = _weight(w_up)
    w_down, wd_spec, _ = _weight(w_down)
    cc = _chunk(F, 256)
    hb = tt // BF16_ROWS
    last_hb = T // BF16_ROWS - 1
    nt = T // tt

    def body(gu_ref, upnext_ref, gc_ref, gcnext_ref, cw_ref, wu_ref, wd_ref, x_ref, g_ref, dh_ref, dhnext_ref,
             dgu_ref, dc_ref, dx_ref, dg_ref):
        i = pl.program_id(0)

        @pl.when(i == 0)
        def _():
            dc_ref[...] = jnp.zeros_like(dc_ref)
            dg_ref[...] = jnp.zeros_like(dg_ref)

        n = tt + BF16_ROWS
        ext = lambda main, nxt: jnp.concatenate([main.astype(F32), nxt.astype(F32)], axis=0)
        dhe = ext(dh_ref[...], jnp.where(i == nt - 1, 0.0, dhnext_ref[...])).astype(BF16)
        dy = jnp.zeros((tt, D), F32)
        groups = _col_groups(F, cc)
        da_of = lambda grp: lax.dot_general(dhe, wd_ref[grp[0]:grp[1], :], NT, preferred_element_type=F32)
        da_next = da_of(groups[0])
        for gi, (lo, hi) in enumerate(groups):
            da, da_next = da_next, (da_of(groups[gi + 1]) if gi + 1 < len(groups) else None)
            for c0 in range(lo, hi, cc):
                cs = slice(c0, c0 + cc)
                us = slice(F + c0, F + c0 + cc)
                gc = ext(gc_ref[:, cs], gcnext_ref[:, cs])
                up = ext(gu_ref[:, us], upnext_ref[:, cs])
                dae = da[:, c0 - lo:c0 - lo + cc]
                s, ds = _silu_and_grad(gc)
                dgc = dae * up * ds
                dgu_ref[:, us] = (dae * s)[:tt].astype(BF16)
                dgc1 = _shift_rows(dgc, n - 1)[:tt]
                dgc2 = _shift_rows(dgc, n - 2)[:tt]
                dm = dgc[:tt]
                dgu_ref[:, cs] = (cw_ref[2:3, cs] * dm + cw_ref[1:2, cs] * dgc1 + cw_ref[0:1, cs] * dgc2).astype(BF16)
                gt = gu_ref[:, cs].astype(F32)
                dc_ref[0, :, cs] += _rsum8(dgc2 * gt)
                dc_ref[1, :, cs] += _rsum8(dgc1 * gt)
                dc_ref[2, :, cs] += _rsum8(dm * gt)
                dc_ref[3, :, cs] += _rsum8(dm)
            dy = (dy + lax.dot_general(dgu_ref[:, lo:hi], wu_ref[:, lo:hi], NT, preferred_element_type=F32)
                  + lax.dot_general(dgu_ref[:, F + lo:F + hi], wu_ref[:, F + lo:F + hi], NT, preferred_element_type=F32))
        xv = x_ref[...]
        r = lax.rsqrt(jnp.mean(xv * xv, axis=-1, keepdims=True) + EPS)
        xn = xv * r
        dg_ref[...] += _rsum8(dy * xn)
        dxn = dy * g_ref[...]
        dx_ref[...] = dh_ref[...] + r * (dxn - xn * jnp.mean(dxn * xn, axis=-1, keepdims=True))

    tok = lambda w: pl.BlockSpec((tt, w), lambda i: (i, 0))
    nxt = lambda w, col: pl.BlockSpec((BF16_ROWS, w), lambda i: (jnp.minimum((i + 1) * hb, last_hb), col))
    return pl.pallas_call(
        body, name=name, grid=(nt,),
        in_specs=[tok(2 * F), nxt(F, 1), tok(F), nxt(F, 0),
                  _resident((SUBLANES, F)), wu_spec, wd_spec, tok(D), _resident((1, D)), tok(D), nxt(D, 0)],
        out_specs=[tok(2 * F), pl.BlockSpec((4, SUBLANES, F), lambda i: (0, 0, 0)), tok(D),
                   pl.BlockSpec((SUBLANES, D), lambda i: (0, 0))],
        out_shape=[jax.ShapeDtypeStruct((T, 2 * F), BF16), jax.ShapeDtypeStruct((4, SUBLANES, F), F32),
                   jax.ShapeDtypeStruct((T, D), F32), jax.ShapeDtypeStruct((SUBLANES, D), F32)],
        compiler_params=_cp(1),
    )(gu, gu, gc, gc, cw, w_up, w_down, x, g, dh, dh)


def _softplus_neg(lam):
    x = -lam
    y = jnp.exp(-jnp.abs(x))
    l1p = jnp.where(y < 0.01, y * (1.0 - y * (0.5 - y * (1.0 / 3.0))), jnp.log(1.0 + y))
    return jnp.maximum(x, 0.0) + l1p


def _lru_gates(xc, wa_ref, ba_ref, wx_ref, bx_ref, sp):
    xcb = xc.astype(BF16)
    r = _sigmoid(jnp.dot(xcb, wa_ref[...], preferred_element_type=F32) + ba_ref[...])
    gi = _sigmoid(jnp.dot(xcb, wx_ref[...], preferred_element_type=F32) + bx_ref[...])
    log_a = -LRU_C * r * sp
    a = jnp.exp(log_a)
    x2 = 2.0 * log_a
    series = -x2 * (1.0 + x2 * 0.5 * (1.0 + x2 * (1.0 / 3.0)))
    om = jnp.where(x2 > -0.02, series, 1.0 - a * a)
    return r, gi, a, jnp.sqrt(om)


def _lru_conv(xr, halo, cw_ref, cb_ref):
    e = jnp.concatenate([halo, xr], axis=0)
    x1, x2, x3 = (_shift_rows(e, j)[BF16_ROWS:] for j in (1, 2, 3))
    return cb_ref[...] + cw_ref[0:1, :] * x3 + cw_ref[1:2, :] * x2 + cw_ref[2:3, :] * x1 + cw_ref[3:4, :] * xr


def _lru_fwd(z, cw, cb, wa, ba, wx, bx, lam, name="lru_fwd", tt=512):
    T = z.shape[0]
    W = LRU_W
    hb = tt // BF16_ROWS
    ng = tt // SUBLANES

    def body(z_ref, halo_ref, cw_ref, cb_ref, wa_ref, ba_ref, wx_ref, bx_ref, lam_ref, oa_ref, h_ref, a_s, sv_ref, u_s, hc):
        i = pl.program_id(0)

        @pl.when(i == 0)
        def _():
            hc[...] = jnp.zeros_like(hc)

        xr = z_ref[:, W:2 * W].astype(F32)
        halo = jnp.where(i == 0, 0.0, halo_ref[...].astype(F32))
        xc = _lru_conv(xr, halo, cw_ref, cb_ref)
        sp = _softplus_neg(lam_ref[...])
        r, gi, a, mult = _lru_gates(xc, wa_ref, ba_ref, wx_ref, bx_ref, sp)
        a_s[...] = a
        u_s[...] = mult * gi * xc
        for k, saved in enumerate((mult, r, gi, xc)):
            sv_ref[:, k * W:(k + 1) * W] = saved.astype(BF16)
        row = lax.broadcasted_iota(jnp.int32, (SUBLANES, W), 0)

        def step(j, hprev):
            r0 = pl.multiple_of(j * SUBLANES, SUBLANES)
            A = a_s[pl.ds(r0, SUBLANES), :]
            U = u_s[pl.ds(r0, SUBLANES), :]
            for k in (1, 2, 4):
                m = row >= k
                U = jnp.where(m, A * pltpu.roll(U, k, 0) + U, U)
                A = jnp.where(m, A * pltpu.roll(A, k, 0), A)
            H = U + A * hprev
            h_ref[pl.ds(r0, SUBLANES), :] = H
            return jnp.broadcast_to(H[SUBLANES - 1:SUBLANES, :], (SUBLANES, W))

        hc[...] = lax.fori_loop(0, ng, step, hc[...])
        oa_ref[...] = (_gelu(z_ref[:, 0:W].astype(F32)) * h_ref[...]).astype(BF16)

    return pl.pallas_call(
        body, name=name, grid=(T // tt,),
        in_specs=[pl.BlockSpec((tt, 2 * W), lambda i: (i, 0)),
                  pl.BlockSpec((BF16_ROWS, W), lambda i: (jnp.maximum(i * hb - 1, 0), 1)),
                  _resident((SUBLANES, W)), _resident((1, W)), _resident((W, W)), _resident((1, W)),
                  _resident((W, W)), _resident((1, W)), _resident((1, W))],
        out_specs=[pl.BlockSpec((tt, W), lambda i: (i, 0)), pl.BlockSpec((tt, W), lambda i: (i, 0)),
                   pl.BlockSpec((tt, W), lambda i: (i, 0)), pl.BlockSpec((tt, 4 * W), lambda i: (i, 0))],
        out_shape=[jax.ShapeDtypeStruct((T, 2 * W), BF16), jax.ShapeDtypeStruct((T, W), F32),
                   jax.ShapeDtypeStruct((T, W), F32), jax.ShapeDtypeStruct((T, 4 * W), BF16)],
        scratch_shapes=[pltpu.VMEM((tt, W), F32), pltpu.VMEM((SUBLANES, W), F32)],
        compiler_params=_cp(1),
    )(z, z, cw, cb, wa, ba, wx, bx, lam)


def _lru_bwd(z, hseq, a_all, saved, dmix, cw, wat, wxt, lam, name="lru_bwd", tt=512):
    T = z.shape[0]
    W = LRU_W
    nt = T // tt
    sb = tt // SUBLANES
    ng = tt // SUBLANES

    def body(z_ref, h_ref, hprev_ref, a_ref, sv_ref, dm_ref, cw_ref, wat_ref, wxt_ref, lam_ref,
             dz_ref, dc_ref, dwa_ref, dwx_ref, dv_ref, c_s, d_s, g_s, gc, an, dxn):
        i = pl.program_id(0)
        ti = nt - 1 - i

        @pl.when(i == 0)
        def _():
            dc_ref[...] = jnp.zeros_like(dc_ref)
            dwa_ref[...] = jnp.zeros_like(dwa_ref)
            dwx_ref[...] = jnp.zeros_like(dwx_ref)
            dv_ref[...] = jnp.zeros_like(dv_ref)
            gc[...] = jnp.zeros_like(gc)
            an[...] = jnp.zeros_like(an)
            dxn[...] = jnp.zeros_like(dxn)

        xr = z_ref[:, W:2 * W].astype(F32)
        yg = z_ref[:, 0:W].astype(F32)
        sp = _softplus_neg(lam_ref[...])
        a = a_ref[...]
        mult, r, gi, xc = (sv_ref[:, k * W:(k + 1) * W].astype(F32) for k in range(4))
        h = h_ref[...]
        hp = jnp.where(ti == 0, 0.0, hprev_ref[...])
        hm1 = _shift_rows(jnp.concatenate([hp, h], axis=0), 1)[SUBLANES:]
        dout = dm_ref[...].astype(F32)
        d_s[...] = dout * _gelu(yg)
        dz_ref[:, 0:W] = (dout * h * _gelu_grad(yg)).astype(BF16)
        c_s[...] = _shift_rows(jnp.concatenate([a, an[...]], axis=0), tt + SUBLANES - 1)[:tt]
        an[...] = a[0:SUBLANES, :]
        row = lax.broadcasted_iota(jnp.int32, (SUBLANES, W), 0)

        def step(j, gnext):
            r0 = pl.multiple_of((ng - 1 - j) * SUBLANES, SUBLANES)
            C = c_s[pl.ds(r0, SUBLANES), :]
            G = d_s[pl.ds(r0, SUBLANES), :]
            for k in (1, 2, 4):
                m = row < SUBLANES - k
                G = jnp.where(m, G + C * pltpu.roll(G, SUBLANES - k, 0), G)
                C = jnp.where(m, C * pltpu.roll(C, SUBLANES - k, 0), C)
            G = G + C * gnext
            g_s[pl.ds(r0, SUBLANES), :] = G
            return jnp.broadcast_to(G[0:1, :], (SUBLANES, W))

        gc[...] = lax.fori_loop(0, ng, step, gc[...])
        du = g_s[...]
        da = du * hm1
        dgi = du * mult * xc
        dxc = du * mult * gi
        dmult = du * gi * xc
        dlog_a = da * a - dmult * (a * a) / mult
        dr = dlog_a * (-LRU_C * sp)
        dv_ref[2] += _rsum8(dlog_a * (-LRU_C * r))
        dpr = (dr * r * (1.0 - r)).astype(BF16)
        dpi = (dgi * gi * (1.0 - gi)).astype(BF16)
        dv_ref[0] += _rsum8(dpr.astype(F32))
        dv_ref[1] += _rsum8(dpi.astype(F32))
        xcb = sv_ref[:, 3 * W:4 * W]
        dwa_ref[...] += lax.dot_general(xcb, dpr, TN, preferred_element_type=F32)
        dwx_ref[...] += lax.dot_general(xcb, dpi, TN, preferred_element_type=F32)
        dxc = dxc + jnp.dot(dpr, wat_ref[...], preferred_element_type=F32) + jnp.dot(dpi, wxt_ref[...], preferred_element_type=F32)
        n = tt + BF16_ROWS
        de = jnp.concatenate([dxc, dxn[...]], axis=0)
        d1, d2, d3 = (_shift_rows(de, n - j)[:tt] for j in (1, 2, 3))
        dxn[...] = dxc[0:BF16_ROWS, :]
        dz_ref[:, W:2 * W] = (cw_ref[3:4, :] * dxc + cw_ref[2:3, :] * d1 + cw_ref[1:2, :] * d2 + cw_ref[0:1, :] * d3).astype(BF16)
        dc_ref[0] += _rsum8(d3 * xr)
        dc_ref[1] += _rsum8(d2 * xr)
        dc_ref[2] += _rsum8(d1 * xr)
        dc_ref[3] += _rsum8(dxc * xr)
        dc_ref[4] += _rsum8(dxc)

    rev = lambda i: nt - 1 - i
    tok = lambda w: pl.BlockSpec((tt, w), lambda i: (rev(i), 0))
    return pl.pallas_call(
        body, name=name, grid=(nt,),
        in_specs=[tok(2 * W), tok(W),
                  pl.BlockSpec((SUBLANES, W), lambda i: (jnp.maximum(rev(i) * sb - 1, 0), 0)),
                  tok(W), tok(4 * W), tok(W),
                  _resident((SUBLANES, W)), _resident((W, W)), _resident((W, W)), _resident((1, W))],
        out_specs=[pl.BlockSpec((tt, 2 * W), lambda i: (rev(i), 0)),
                   pl.BlockSpec((5, SUBLANES, W), lambda i: (0, 0, 0)),
                   pl.BlockSpec((W, W), lambda i: (0, 0)), pl.BlockSpec((W, W), lambda i: (0, 0)),
                   pl.BlockSpec((3, SUBLANES, W), lambda i: (0, 0, 0))],
        out_shape=[jax.ShapeDtypeStruct((T, z.shape[1]), BF16), jax.ShapeDtypeStruct((5, SUBLANES, W), F32),
                   jax.ShapeDtypeStruct((W, W), F32), jax.ShapeDtypeStruct((W, W), F32),
                   jax.ShapeDtypeStruct((3, SUBLANES, W), F32)],
        scratch_shapes=[pltpu.VMEM((tt, W), F32), pltpu.VMEM((tt, W), F32), pltpu.VMEM((tt, W), F32),
                        pltpu.VMEM((SUBLANES, W), F32), pltpu.VMEM((SUBLANES, W), F32), pltpu.VMEM((BF16_ROWS, W), F32)],
        compiler_params=_cp(1),
    )(z, hseq, hseq, a_all, saved, dmix, cw, wat, wxt, lam)


def _split3(x):
    hi = x.astype(BF16)
    r1 = x - hi.astype(F32)
    mid = r1.astype(BF16)
    lo = (r1 - mid.astype(F32)).astype(BF16)
    return hi, mid, lo


def _tri_matmul(tri, x):
    hi, mid, lo = _split3(x)
    return (jnp.dot(tri, hi, preferred_element_type=F32) + jnp.dot(tri, mid, preferred_element_type=F32)
            + jnp.dot(tri, lo, preferred_element_type=F32))


def _hg_chunk(q, fl, lb):
    C = q.shape[0]
    ri = lax.broadcasted_iota(jnp.int32, (C, C), 0)
    ci = lax.broadcasted_iota(jnp.int32, (C, C), 1)
    causal = ri >= ci
    sig = _sigmoid(fl)
    f = lb + (1.0 - lb) * sig
    k = 1.0 - f
    sq = _sigmoid(q)
    qf = q * sq
    b = _tri_matmul(causal.astype(BF16), jnp.log(f))
    bm = b[C // 2 - 1:C // 2, :]
    bl = b[C - 1:C, :]
    e_qt, e_kt, e_in, e_out = jnp.exp(b - bm), jnp.exp(bm - b), jnp.exp(b), jnp.exp(bl - b)
    qt = qf * e_qt
    kt = k * e_kt
    qin = qf * e_in
    kout = k * e_out
    qtb, ktb = qt.astype(BF16), kt.astype(BF16)
    att = [jnp.where(causal, _dot(qtb[:, _head(h)], ktb[:, _head(h)], NT), 0.0).astype(BF16) for h in range(HG_HEADS)]
    return dict(sig=sig, f=f, k=k, sq=sq, qf=qf, b=b, bm=bm, bl=bl, qt=qt, kt=kt, qin=qin, kout=kout, att=att,
                causal=causal, anti=ri <= ci, decay=jnp.exp(bl), e_qt=e_qt, e_kt=e_kt, e_in=e_in, e_out=e_out)


def _head(h):
    return slice(h * HG_D, (h + 1) * HG_D)


def _hgrn_fwd(z, lb, gn, mix, name="hgrn_fwd", tt=512):
    T = z.shape[0]
    C = HG_CHUNK
    nc = tt // C
    Dh = HG_D
    Wd = HG_HEADS * Dh

    def body(q_ref, f_ref, v_ref, g_ref, lb_ref, gn_ref, mix_ref, o_ref, ss_ref, st):
        del mix_ref

        @pl.when(pl.program_id(0) == 0)
        def _():
            st[...] = jnp.zeros_like(st)

        S = [st[h] for h in range(HG_HEADS)]
        for c in range(nc):
            rows = slice(c * C, (c + 1) * C)
            ck = _hg_chunk(q_ref[rows, :].astype(F32), f_ref[rows, :].astype(F32), lb_ref[...])
            v = v_ref[rows, :]
            g = g_ref[rows, :].astype(F32)
            H = range(HG_HEADS)
            qinb, koutb = ck["qin"].astype(BF16), ck["kout"].astype(BF16)
            for h in H:
                ss_ref[h, c] = S[h]
            o = [_dot(ck["att"][h], v[:, _head(h)]) + _dot(qinb[:, _head(h)], S[h], NT) for h in H]
            S = [ck["decay"][:, _head(h)] * S[h] + _dot(v[:, _head(h)], koutb[:, _head(h)], TN) for h in H]
            outs = [o[h] * lax.rsqrt(jnp.mean(o[h] * o[h], axis=-1, keepdims=True) + EPS) * gn_ref[...] for h in H]
            o_ref[rows, :] = (jnp.concatenate(outs, axis=1) * (g * _sigmoid(g))).astype(BF16)
        for h in range(HG_HEADS):
            st[h] = S[h]

    col = lambda base: (lambda i: (i, base))
    return pl.pallas_call(
        body, name=name, grid=(T // tt,),
        in_specs=[pl.BlockSpec((tt, Wd), col(2)), pl.BlockSpec((tt, Wd), col(3)), pl.BlockSpec((tt, Wd), col(4)),
                  pl.BlockSpec((tt, Wd), col(5)), _resident((1, Wd)), _resident((1, Dh)), ANY],
        out_specs=[pl.BlockSpec((tt, Wd), lambda i: (i, 1)),
                   pl.BlockSpec((HG_HEADS, nc, Dh, Dh), lambda i: (0, i, 0, 0))],
        out_shape=[jax.ShapeDtypeStruct((T, 2 * Wd), BF16),
                   jax.ShapeDtypeStruct((HG_HEADS, T // C, Dh, Dh), F32)],
        scratch_shapes=[pltpu.VMEM((HG_HEADS, Dh, Dh), F32)],
        compiler_params=_cp(1),
        input_output_aliases={6: 0},
    )(z, z, z, z, lb, gn, mix)


def _hgrn_bwd(z, ss, dmix, lb, gn, dz, name="hgrn_bwd", tt=512):
    T = z.shape[0]
    C = HG_CHUNK
    nc = tt // C
    nt = T // tt
    Dh = HG_D
    Wd = HG_HEADS * Dh

    def body(q_ref, f_ref, v_ref, g_ref, ss_ref, dm_ref, lb_ref, gn_ref, dz01_ref, dz_ref, dlb_ref, dgn_ref, dst):
        @pl.when(pl.program_id(0) == 0)
        def _():
            dst[...] = jnp.zeros_like(dst)
            dlb_ref[...] = jnp.zeros_like(dlb_ref)
            dgn_ref[...] = jnp.zeros_like(dgn_ref)

        dS = [dst[h] for h in range(HG_HEADS)]
        lbv = lb_ref[...]
        gnv = gn_ref[...]
        rowc = lax.broadcasted_iota(jnp.int32, (C, Wd), 0)
        cat = lambda xs: jnp.concatenate(xs, axis=1)
        for c in reversed(range(nc)):
            rows = slice(c * C, (c + 1) * C)
            q = q_ref[rows, :].astype(F32)
            ck = _hg_chunk(q, f_ref[rows, :].astype(F32), lbv)
            v = v_ref[rows, :]
            g = g_ref[rows, :].astype(F32)
            dout = dm_ref[rows, :].astype(F32)
            sg, dsg = _silu_and_grad(g)
            d_ong = dout * sg
            H = range(HG_HEADS)
            qinb, koutb, qtb, ktb = (ck[n].astype(BF16) for n in ("qin", "kout", "qt", "kt"))
            S = [ss_ref[h, c] for h in H]
            Sb = [s.astype(BF16) for s in S]
            dSb = [d.astype(BF16) for d in dS]
            o = [_dot(ck["att"][h], v[:, _head(h)]) + _dot(qinb[:, _head(h)], Sb[h], NT) for h in H]
            rn = [lax.rsqrt(jnp.mean(o[h] * o[h], axis=-1, keepdims=True) + EPS) for h in H]
            on = [o[h] * rn[h] for h in H]
            don = [d_ong[:, _head(h)] * gnv for h in H]
            do = [(rn[h] * (don[h] - on[h] * jnp.mean(don[h] * on[h], axis=-1, keepdims=True))).astype(BF16) for h in H]
            datt = [jnp.where(ck["causal"], _dot(do[h], v[:, _head(h)], NT), 0.0).astype(BF16) for h in H]
            dvs = [_dot(ck["att"][h], do[h], TN) + _dot(koutb[:, _head(h)], dSb[h], NT) for h in H]
            dqins = [_dot(do[h], Sb[h]) for h in H]
            dkouts = [_dot(v[:, _head(h)], dSb[h]) for h in H]
            dqts = [_dot(datt[h], ktb[:, _head(h)]) for h in H]
            dkts = [_dot(datt[h], qtb[:, _head(h)], TN) for h in H]
            ddecays = [jnp.sum(dS[h] * S[h], axis=0, keepdims=True) for h in H]
            dS = [_dot(do[h], qinb[:, _head(h)], TN) + ck["decay"][:, _head(h)] * dS[h] for h in H]
            ons = [on[h] * gnv for h in H]
            dgn = _rsum8(d_ong[:, _head(0)] * on[0])
            for h in range(1, HG_HEADS):
                dgn = dgn + _rsum8(d_ong[:, _head(h)] * on[h])
            dgn_ref[...] += dgn
            dqt, dkt, dqin, dkout, ddecay = cat(dqts), cat(dkts), cat(dqins), cat(dkouts), cat(ddecays)
            dqf = dqt * ck["e_qt"] + dqin * ck["e_in"]
            dk = dkt * ck["e_kt"] + dkout * ck["e_out"]
            kk = dkout * ck["kout"]
            db = dqt * ck["qt"] - dkt * ck["kt"] + dqin * ck["qin"] - kk
            dbl = jnp.sum(kk, axis=0, keepdims=True) + ddecay * ck["decay"]
            db = db + jnp.where(rowc == C - 1, dbl, 0.0)
            dlogf = _tri_matmul(ck["anti"].astype(BF16), db)
            dfv = dlogf / ck["f"] - dk
            sig, sq = ck["sig"], ck["sq"]
            dlb_ref[...] += _rsum8(dfv * (1.0 - sig))
            dz_ref[rows, 2 * Wd:3 * Wd] = (dqf * (sq * (1.0 + q * (1.0 - sq)))).astype(BF16)
            dz_ref[rows, 3 * Wd:4 * Wd] = (dfv * (1.0 - lbv) * sig * (1.0 - sig)).astype(BF16)
            dz_ref[rows, 4 * Wd:5 * Wd] = cat(dvs).astype(BF16)
            dz_ref[rows, 5 * Wd:6 * Wd] = (dout * cat(ons) * dsg).astype(BF16)
        dz_ref[:, 0:2 * Wd] = dz01_ref[...]
        for h in range(HG_HEADS):
            dst[h] = dS[h]

    rev = lambda i: nt - 1 - i
    col = lambda base: (lambda i: (rev(i), base))
    return pl.pallas_call(
        body, name=name, grid=(nt,),
        in_specs=[pl.BlockSpec((tt, Wd), col(2)), pl.BlockSpec((tt, Wd), col(3)), pl.BlockSpec((tt, Wd), col(4)),
                  pl.BlockSpec((tt, Wd), col(5)),
                  pl.BlockSpec((HG_HEADS, nc, Dh, Dh), lambda i: (0, rev(i), 0, 0)),
                  pl.BlockSpec((tt, Wd), col(1)), _resident((1, Wd)), _resident((1, Dh)),
                  pl.BlockSpec((tt, 2 * Wd), col(0))],
        out_specs=[pl.BlockSpec((tt, 6 * Wd), lambda i: (rev(i), 0)), pl.BlockSpec((SUBLANES, Wd), lambda i: (0, 0)),
                   pl.BlockSpec((SUBLANES, Dh), lambda i: (0, 0))],
        out_shape=[jax.ShapeDtypeStruct((T, 6 * Wd), BF16), jax.ShapeDtypeStruct((SUBLANES, Wd), F32),
                   jax.ShapeDtypeStruct((SUBLANES, Dh), F32)],
        input_output_aliases={8: 0},
        scratch_shapes=[pltpu.VMEM((HG_HEADS, Dh, Dh), F32)],
        compiler_params=_cp(1),
    )(z, z, z, z, ss, dmix, lb, gn, dz)


def _sgu_core(p, lg_ref, lb_ref, wsc_ref, bsb_ref):
    Wd = D_MODEL
    G = SGU_CHUNK
    zz = _gelu(p)
    u = zz[:, :Wd]
    v = zz[:, Wd:]
    vc = v - jnp.mean(v, axis=-1, keepdims=True)
    rstd = lax.rsqrt(jnp.mean(vc * vc, axis=-1, keepdims=True) + EPS)
    vhat = vc * rstd
    vn = vhat * lg_ref[...] + lb_ref[...]
    svs = []
    for gi in range(SGU_G):
        svs.append(jnp.dot(wsc_ref[gi], vn[:, gi * G:(gi + 1) * G].astype(BF16), preferred_element_type=F32) + bsb_ref[gi])
    return u, vhat, rstd, vn, jnp.concatenate(svs, axis=1)


def _sgu_fwd(p1, lg, lbias, wsc, bsb, name="sgu_fwd", tt=512):
    T = p1.shape[0]
    Wd = D_MODEL
    C = SGU_CHUNK

    def body(p_ref, lg_ref, lb_ref, wsc_ref, bsb_ref, s_ref):
        for c in range(tt // C):
            rows = slice(c * C, (c + 1) * C)
            u, _, _, _, sv = _sgu_core(p_ref[rows, :].astype(F32), lg_ref, lb_ref, wsc_ref, bsb_ref)
            s_ref[rows, :] = (u * sv).astype(BF16)

    return pl.pallas_call(
        body, name=name, grid=(T // tt,),
        in_specs=[pl.BlockSpec((tt, 2 * Wd), lambda i: (i, 0)), _resident((1, Wd)), _resident((1, Wd)),
                  _resident((SGU_G, C, C)), _resident((SGU_G, C, C))],
        out_specs=pl.BlockSpec((tt, Wd), lambda i: (i, 0)),
        out_shape=jax.ShapeDtypeStruct((T, Wd), BF16),
        compiler_params=_cp(1),
    )(p1, lg, lbias, wsc, bsb)


def _sgu_bwd(p1, ds, lg, lbias, wsc, wsct, bsb, name="sgu_bwd", tt=512):
    T = p1.shape[0]
    Wd = D_MODEL
    C = SGU_CHUNK

    def body(p_ref, ds_ref, lg_ref, lb_ref, wsc_ref, wsct_ref, bsb_ref, dp_ref, dws_ref, dbs_ref, dlg_ref, dlb_ref, dbin_ref):
        @pl.when(pl.program_id(0) == 0)
        def _():
            dws_ref[...] = jnp.zeros_like(dws_ref)
            dbs_ref[...] = jnp.zeros_like(dbs_ref)
            dlg_ref[...] = jnp.zeros_like(dlg_ref)
            dlb_ref[...] = jnp.zeros_like(dlb_ref)
            dbin_ref[...] = jnp.zeros_like(dbin_ref)

        for c in range(tt // C):
            rows = slice(c * C, (c + 1) * C)
            p = p_ref[rows, :].astype(F32)
            u, vhat, rstd, vn, sv = _sgu_core(p, lg_ref, lb_ref, wsc_ref, bsb_ref)
            dsc = ds_ref[rows, :].astype(F32)
            du = dsc * sv
            dsv = dsc * u
            dvns = []
            for gi in range(SGU_G):
                cs = slice(gi * C, (gi + 1) * C)
                dsv_g = dsv[:, cs]
                dvns.append(jnp.dot(wsct_ref[gi], dsv_g.astype(BF16), preferred_element_type=F32))
                dws_ref[gi] += _dot(dsv_g, vn[:, cs], NT)
                dbs_ref[gi] += dsv_g
            dvn = jnp.concatenate(dvns, axis=1)
            dlg_ref[...] += _rsum8(dvn * vhat)
            dlb_ref[...] += _rsum8(dvn)
            dvh = dvn * lg_ref[...]
            dv = rstd * (dvh - jnp.mean(dvh, axis=-1, keepdims=True) - vhat * jnp.mean(dvh * vhat, axis=-1, keepdims=True))
            dp = jnp.concatenate([du, dv], axis=1) * _gelu_grad(p)
            dbin_ref[...] += _rsum8(dp)
            dp_ref[rows, :] = dp.astype(BF16)

    full3 = pl.BlockSpec((SGU_G, C, C), lambda i: (0, 0, 0))
    return pl.pallas_call(
        body, name=name, grid=(T // tt,),
        in_specs=[pl.BlockSpec((tt, 2 * Wd), lambda i: (i, 0)), pl.BlockSpec((tt, Wd), lambda i: (i, 0)),
                  _resident((1, Wd)), _resident((1, Wd)), _resident((SGU_G, C, C)), _resident((SGU_G, C, C)),
                  _resident((SGU_G, C, C))],
        out_specs=[pl.BlockSpec((tt, 2 * Wd), lambda i: (i, 0)), full3, full3,
                   pl.BlockSpec((SUBLANES, Wd), lambda i: (0, 0)), pl.BlockSpec((SUBLANES, Wd), lambda i: (0, 0)),
                   pl.BlockSpec((SUBLANES, 2 * Wd), lambda i: (0, 0))],
        out_shape=[jax.ShapeDtypeStruct((T, 2 * Wd), BF16), jax.ShapeDtypeStruct((SGU_G, C, C), F32),
                   jax.ShapeDtypeStruct((SGU_G, C, C), F32), jax.ShapeDtypeStruct((SUBLANES, Wd), F32),
                   jax.ShapeDtypeStruct((SUBLANES, Wd), F32), jax.ShapeDtypeStruct((SUBLANES, 2 * Wd), F32)],
        compiler_params=_cp(1),
    )(p1, ds, lg, lbias, wsc, wsct, bsb)


def _pad_rows(w, rows=SUBLANES):
    return jnp.pad(w, ((0, rows - w.shape[0]), (0, 0)))


def _block_diag(w):
    n, b, _ = w.shape
    return (w[:, :, None, :] * jnp.eye(n, dtype=w.dtype)[:, None, :, None]).reshape(n * b, n * b)


def _diag_blocks(m, n):
    b = m.shape[0] // n
    m4 = m.reshape(n, b, n, b)
    return jnp.stack([m4[k, :, k, :] for k in range(n)], axis=0)


def _piece_major(dw):
    if dw.ndim == 2:
        K, N = dw.shape
        return dw.reshape(N_CHIPS, 2, K // (2 * N_CHIPS), N)
    _, K, ns = dw.shape
    return dw.reshape(N_CHIPS, 2, K // 2, ns)


def _ffn_fwd(h, g, w_up, cw, cb, w_down, tag, loss=None):
    hn, gu = _norm_mm(h, g, w_up, jnp.zeros((1, 2 * D_FF), F32), name=f"ffn_up_{tag}")
    a, gc, *out = _ffn_act(gu, cw, cb, w_down, h, name=f"ffn_act_down_{tag}", loss=loss)
    return (out[0] if loss is None else out), (hn, gu, gc, a)


def _ffn_bwd(dh, h, g, saved, w_up, cw, w_down, tag):
    hn, gu, gc, a = saved
    dwd = _mm_tn(a, dh, name=f"ffn_dwd_{tag}")
    dgu, dc, dhin, dg8 = _ffn_act_bwd(gu, gc, cw, w_up, w_down, h, g, dh, name=f"ffn_bwd_{tag}")
    dwu = _mm_tn(hn, dgu, name=f"ffn_dwu_{tag}", col_shards=N_CHIPS)
    dcs = dc.sum(axis=1)
    return dhin, dg8.sum(axis=0), dwu, dcs[0:3], dcs[3], dwd


REDUCE_GROUPS = {"g1": [("ffn_w_up", 1), ("ffn_w_down", 1), ("od_w_out", 0), ("od_w_in", 0)],
                 "g2": [("ffn_w_up", 0), ("ffn_w_down", 0)],
                 "g3": [("ev_w_out", 0), ("ev_w_in", 0)]}


def _local_step(x, tgt, p, start_reduce=None, continue_reduce=None):
    row = lambda v: v.reshape(1, -1)
    grads = {}

    lower = jax.nn.softmax(p["hg_lb_logits"], axis=0)
    lb0 = row(lower[0])
    ev_cw = _pad_rows(p["ev_conv_w"][0])
    ev_cb = row(p["ev_conv_b"][0])
    wa = _block_diag(p["ev_gate_a_w"][0]).astype(BF16)
    wx = _block_diag(p["ev_gate_x_w"][0]).astype(BF16)
    ba, bx, lam = row(p["ev_gate_a_b"][0]), row(p["ev_gate_x_b"][0]), row(p["ev_lru_lambda"][0])
    gn = row(p["ev_hg_norm"][0])
    tril = jnp.tril(jnp.ones((SGU_CHUNK, SGU_CHUNK), F32))
    wsc = (p["od_w_s"][0] * tril).astype(BF16)
    bsb = jnp.broadcast_to(p["od_b_s"][0][:, :, None], (SGU_G, SGU_CHUNK, SGU_CHUNK)).astype(F32)
    ffn_cw = [_pad_rows(p["ffn_conv_w"][l]) for l in range(2)]
    ffn_cb = [row(p["ffn_conv_b"][l]) for l in range(2)]
    ev_w_in, ev_w_out = p["ev_w_in"][0], p["ev_w_out"][0]
    nm = [row(p["norm_mix"][l]) for l in range(2)]
    nf = [row(p["norm_ffn"][l]) for l in range(2)]

    h0 = x
    hn0, z0 = _norm_mm(h0, nm[0], ev_w_in, jnp.zeros((1, ev_w_in.shape[1]), F32), name="ev_in")
    out_a, hseq, a_all, lru_saved = _lru_fwd(z0, ev_cw, ev_cb, wa, ba, wx, bx, lam)
    mix0, ss = _hgrn_fwd(z0, lb0, gn, out_a)
    h1 = _mm(mix0, ev_w_out, h0, F32, name="ev_out")
    late = p["late"](h1) if "late" in p else p
    od_w_in, od_w_out = late["od_w_in"][0], late["od_w_out"][0]
    w_up = [(late["ffn_w_up"], l) for l in range(2)]
    w_down = [(late["ffn_w_down"], l) for l in range(2)]
    h2, ffn0 = _ffn_fwd(h1, nf[0], w_up[0], ffn_cw[0], ffn_cb[0], w_down[0], "l0")
    hn1, p1 = _norm_mm(h2, nm[1], od_w_in, row(p["od_b_in"][0]), name="od_in")
    s1 = _sgu_fwd(p1, row(p["od_ln_g"][0]), row(p["od_ln_b"][0]), wsc, bsb)
    h3 = _mm(s1, od_w_out, h2, F32, name="od_out")
    (dh4, sq8, dgf8), ffn1 = _ffn_fwd(h3, nf[1], w_up[1], ffn_cw[1], ffn_cb[1], w_down[1], "l1", loss=(row(p["norm_final"]), tgt))
    grads["norm_final"] = dgf8.sum(axis=0)

    big = {}
    dh3, dnf1, dwu1, dcw1, dcb1, dwd1 = _ffn_bwd(dh4, h3, nf[1], ffn1, w_up[1], ffn_cw[1], w_down[1], "l1")
    big["ffn_w_up", 1], big["ffn_w_down", 1] = _piece_major(dwu1), _piece_major(dwd1)
    ds1 = _mm(dh3, od_w_out, None, BF16, name="od_ds", transpose_w=True)
    big["od_w_out", 0] = _piece_major(_mm_tn(s1, dh3, name="od_dwo"))
    wsct = jnp.swapaxes(wsc, 1, 2)
    dp1, dws, dbs, dlg8, dlb8, dbin8 = _sgu_bwd(p1, ds1, row(p["od_ln_g"][0]), row(p["od_ln_b"][0]), wsc, wsct, bsb)
    grads["od_w_s"] = (dws * tril)[None]
    grads["od_b_s"] = dbs.sum(axis=-1)[None]
    grads["od_ln_g"] = dlg8.sum(axis=0)[None]
    grads["od_ln_b"] = dlb8.sum(axis=0)[None]
    grads["od_b_in"] = dbin8.sum(axis=0)[None]
    dh2, dnm1 = _mm_normbwd(dp1, od_w_in, h2, nm[1], dh3, name="od_dh")
    big["od_w_in", 0] = _piece_major(_mm_tn(hn1, dp1, name="od_dwi", col_shards=N_CHIPS))
    if start_reduce is not None:
        g1, dh2 = lax.optimization_barrier(([big[key] for key in REDUCE_GROUPS["g1"]], dh2))
        token = start_reduce("g1", g1, split=True)
        ffn_cw[0] = ffn_cw[0] + token[0:1, 0:1]

    dh1, dnf0, dwu0, dcw0, dcb0, dwd0 = _ffn_bwd(dh2, h1, nf[0], ffn0, w_up[0], ffn_cw[0], w_down[0], "l0")
    big["ffn_w_up", 0], big["ffn_w_down", 0] = _piece_major(dwu0), _piece_major(dwd0)
    if start_reduce is not None:
        token = continue_reduce("g1", dh1) + start_reduce("g2", [big[key] for key in REDUCE_GROUPS["g2"]], split=True)
        lam = lam + token[0:1, 0:1]
    dmix = _mm(dh1, ev_w_out, None, BF16, name="ev_dmix", transpose_w=True)
    big["ev_w_out", 0] = _piece_major(_mm_tn(mix0, dh1, name="ev_dwo"))
    dz01, dc5, dwa, dwx, dvec = _lru_bwd(z0, hseq, a_all, lru_saved, dmix, ev_cw, wa.T, wx.T, lam)
    if start_reduce is not None:
        lb0 = lb0 + continue_reduce("g2", dc5)[0:1, 0:1]
    dz0, dlb8, dgn8 = _hgrn_bwd(z0, ss, dmix, lb0, gn, dz01)
    big["ev_w_in", 0] = _piece_major(_mm_tn(hn0, dz0, name="ev_dwi", col_shards=N_CHIPS))
    if start_reduce is not None:
        token = start_reduce("g3", [big[key] for key in REDUCE_GROUPS["g3"]])
        nm[0] = nm[0] + token[0:1, 0:1]
    grad_x, dnm0 = _mm_normbwd(dz0, ev_w_in, h0, nm[0], dh1, name="ev_dh")

    dc5s = dc5.sum(axis=1)
    grads["ev_conv_w"] = dc5s[0:4][None]
    grads["ev_conv_b"] = dc5s[4][None]
    grads["ev_gate_a_w"] = _diag_blocks(dwa, LRU_BLOCKS)[None]
    grads["ev_gate_x_w"] = _diag_blocks(dwx, LRU_BLOCKS)[None]
    dvs = dvec.sum(axis=1)
    grads["ev_gate_a_b"] = dvs[0][None]
    grads["ev_gate_x_b"] = dvs[1][None]
    grads["ev_lru_lambda"] = (dvs[2] * (-jax.nn.sigmoid(-p["ev_lru_lambda"][0])))[None]
    dlb = dlb8.sum(axis=0)
    grads["hg_lb_logits"] = dlb[None, :] * lower[0][None, :] * (jnp.eye(3, dtype=F32)[0][:, None] - lower)
    grads["ev_hg_norm"] = dgn8.sum(axis=0)[None]
    grads["norm_mix"] = jnp.stack([dnm0.sum(axis=0), dnm1.sum(axis=0)])
    grads["norm_ffn"] = jnp.stack([dnf0, dnf1])
    grads["ffn_conv_w"] = jnp.stack([dcw0, dcw1])
    grads["ffn_conv_b"] = jnp.stack([dcb0, dcb1])
    return sq8, grad_x, grads, big


SH_BIG = {"ev_w_in": 2, "ev_w_out": 1, "od_w_in": 2, "od_w_out": 1, "ffn_w_up": 2, "ffn_w_down": 1}
SH_SMALL = {"ev_conv_w": 2, "od_b_in": 1, "od_ln_g": 1, "od_ln_b": 1, "ffn_conv_w": 2}
REP = ["norm_mix", "norm_ffn", "norm_final", "ev_conv_b", "ev_gate_a_w", "ev_gate_a_b", "ev_gate_x_w", "ev_gate_x_b",
       "ev_lru_lambda", "hg_lb_logits", "ev_hg_norm", "od_w_s", "od_b_s", "ffn_conv_b"]
WEIGHTS = ["norm_mix", "norm_ffn", "norm_final", "ev_w_in", "ev_conv_w", "ev_conv_b", "ev_gate_a_w", "ev_gate_a_b", "ev_gate_x_w",
           "ev_gate_x_b", "ev_lru_lambda", "hg_lb_logits", "ev_hg_norm", "ev_w_out", "od_w_in", "od_b_in", "od_ln_g", "od_ln_b",
           "od_w_s", "od_b_s", "od_w_out", "ffn_w_up", "ffn_conv_w", "ffn_conv_b", "ffn_w_down"]


def _rows(n_elems, mult=SUBLANES):
    r = -(-n_elems // LANES)
    return -(-r // mult) * mult


def _pack(arrs, rows, dtype):
    flat = jnp.concatenate([a.reshape(-1).astype(dtype) for a in arrs])
    return jnp.pad(flat, (0, rows * LANES - flat.shape[0])).reshape(rows, LANES)


def _unpack(flat2d, shapes):
    flat = flat2d.reshape(-1)
    out, off = [], 0
    for s in shapes:
        n = 1
        for d in s:
            n *= d
        out.append(flat[off:off + n].reshape(s))
        off += n
    return out


def _mesh_pos():
    return lax.axis_index("x"), lax.axis_index("y"), lax.axis_index("c")


def _other_chips(x, y):
    return [(1 - x, y), (x, 1 - y), (1 - x, 1 - y)]


def _half_rows(n):
    return lambda r, c: r.at[0, pl.ds(c * (n // 2), n // 2), :]


GATHER_BIG = {
    "ev_w_in": ((1024, 3072), _half_rows(1024), lambda o, k, c: o.at[pl.ds(c * 512, 512), pl.ds(k * 768, 768)]),
    "ev_w_out": ((1024, 1024), _half_rows(256), lambda o, k, c: o.at[pl.ds(k * 256 + c * 128, 128), :]),
    "od_w_in": ((1024, 2048), _half_rows(1024), lambda o, k, c: o.at[pl.ds(c * 512, 512), pl.ds(k * 512, 512)]),
    "od_w_out": ((1024, 1024), _half_rows(256), lambda o, k, c: o.at[pl.ds(k * 256 + c * 128, 128), :]),
    "ffn_w_up": ((2, 1024, 2 * D_FF), lambda r, c: r.at[c], lambda o, k, c: o.at[c, :, pl.ds(k * (2 * D_FF // 4), 2 * D_FF // 4)]),
    "ffn_w_down": ((2, D_FF, 1024), lambda r, c: r.at[c], lambda o, k, c: o.at[c, pl.ds(k * (D_FF // 4), D_FF // 4), :]),
}


def _gather_weights(names, big, small):
    nb = len(big)
    descs = [GATHER_BIG[n] for n in names]
    rs = small.shape[0]

    def body(*refs):
        ins, s_ref = refs[:nb], refs[nb]
        outs, os_ref = refs[nb + 1:2 * nb + 1], refs[2 * nb + 1]
        ici_send, ici_recv, d2d_send, d2d_recv, loc_sems = refs[2 * nb + 2:2 * nb + 7]
        vbufs = refs[2 * nb + 7:]
        x, y, c = _mesh_pos()
        k = 2 * x + y
        chips = _other_chips(x, y)
        sib = (x, y, 1 - c)

        def remote(src, dst, ssem, rsem, to):
            return pltpu.make_async_remote_copy(src_ref=src, dst_ref=dst, send_sem=ssem, recv_sem=rsem, device_id=to,
                                                device_id_type=MESH)

        stage = [pltpu.make_async_copy(ins[t], vbufs[t], loc_sems.at[2 * t]) for t in range(nb)]
        stage.append(pltpu.make_async_copy(s_ref, vbufs[nb], loc_sems.at[2 * nb]))
        for cp in stage:
            cp.start()
        sends = []
        for t, (_, src, dst) in enumerate(descs):
            for j, (px, py) in enumerate(chips):
                sends.append(remote(src(ins[t], c), dst(outs[t], k, c), ici_send.at[3 * t + j], ici_recv.at[3 * t + j], (px, py, c)))
        for j, (px, py) in enumerate(chips):
            sends.append(remote(s_ref, os_ref.at[k], ici_send.at[3 * nb + j], ici_recv.at[3 * nb + j], (px, py, c)))
        for cp in sends:
            cp.start()
        for cp in stage:
            cp.wait()
        local = []
        for t, (_, src, dst) in enumerate(descs):
            for cc in (0, 1):
                local.append(pltpu.make_async_copy(src(vbufs[t], cc), dst(outs[t], k, cc), loc_sems.at[2 * t + cc]))
        local.append(pltpu.make_async_copy(vbufs[nb], os_ref.at[k], loc_sems.at[2 * nb]))
        for cp in local:
            cp.start()
        for t, (_, src, dst) in enumerate(descs):
            for j, (px, py) in enumerate(chips):
                got = dst(outs[t], 2 * px + py, c)
                remote(got, got, ici_send.at[3 * t + j], ici_recv.at[3 * t + j], (px, py, c)).wait_recv()
                fwd = remote(got, got, d2d_send.at[3 * t + j], d2d_recv.at[3 * t + j], sib)
                fwd.start()
                sends.append(fwd)
        for j, (px, py) in enumerate(chips):
            remote(s_ref, os_ref.at[2 * px + py], ici_send.at[3 * nb + j], ici_recv.at[3 * nb + j], (px, py, c)).wait_recv()
        for t, (_, src, dst) in enumerate(descs):
            for j, (px, py) in enumerate(chips):
                theirs = dst(outs[t], 2 * px + py, 1 - c)
                remote(theirs, theirs, d2d_send.at[3 * t + j], d2d_recv.at[3 * t + j], sib).wait_recv()
        for cp in sends:
            cp.wait_send()
        for cp in local:
            cp.wait()

    out_shape = [jax.ShapeDtypeStruct(d[0], BF16) for d in descs] + [jax.ShapeDtypeStruct((N_CHIPS, rs, LANES), small.dtype)]
    return pl.pallas_call(
        body, name="gather_weights", in_specs=[ANY] * (nb + 1), out_specs=[ANY] * (nb + 1), out_shape=out_shape,
        scratch_shapes=[pltpu.SemaphoreType.DMA((3 * nb + 3,)), pltpu.SemaphoreType.DMA((3 * nb + 3,)),
                        pltpu.SemaphoreType.DMA((3 * nb,)), pltpu.SemaphoreType.DMA((3 * nb,)),
                        pltpu.SemaphoreType.DMA((2 * nb + 1,))]
        + [pltpu.VMEM(b.shape, b.dtype) for b in big] + [pltpu.VMEM(small.shape, small.dtype)],
        compiler_params=pltpu.CompilerParams(vmem_limit_bytes=VMEM_LIMIT),
    )(*big, small)


def _place_own(names, big):
    nb = len(big)
    descs = [GATHER_BIG[n] for n in names]

    def body(*refs):
        ins, outs = refs[:nb], refs[nb:2 * nb]
        sems, vbufs = refs[2 * nb], refs[2 * nb + 1:]
        x, y, c = _mesh_pos()
        k = 2 * x + y
        stage = [pltpu.make_async_copy(ins[t], vbufs[t], sems.at[2 * t]) for t in range(nb)]
        for cp in stage:
            cp.start()
        for cp in stage:
            cp.wait()
        local = [pltpu.make_async_copy(src(vbufs[t], cc), dst(outs[t], k, cc), sems.at[2 * t + cc])
                 for t, (_, src, dst) in enumerate(descs) for cc in (0, 1)]
        for cp in local:
            cp.start()
        for cp in local:
            cp.wait()

    return pl.pallas_call(
        body, name="place_own", in_specs=[ANY] * nb, out_specs=[ANY] * nb,
        out_shape=[jax.ShapeDtypeStruct(d[0], BF16) for d in descs],
        scratch_shapes=[pltpu.SemaphoreType.DMA((2 * nb,))] + [pltpu.VMEM(b.shape, b.dtype) for b in big],
        compiler_params=pltpu.CompilerParams(vmem_limit_bytes=VMEM_LIMIT),
    )(*big)


def _gather_start(names, big, bufs):
    nb = len(big)
    descs = [GATHER_BIG[n] for n in names]

    def body(*refs):
        ins, lnd = refs[:nb], refs[nb:2 * nb]
        send_sems, recv_sems, token = refs[2 * nb], refs[2 * nb + 1], refs[-1]
        x, y, c = _mesh_pos()
        k = 2 * x + y
        for t, (_, src, dst) in enumerate(descs):
            for j, (px, py) in enumerate(_other_chips(x, y)):
                _remote(src(ins[t], c), dst(lnd[t], k, c), send_sems.at[3 * t + j], recv_sems.at[3 * t + j], (px, py, c)).start()
        token[...] = jnp.zeros_like(token)

    out = pl.pallas_call(
        body, name="gather_start",
        out_shape=(pltpu.SemaphoreType.DMA((3 * nb,)), pltpu.SemaphoreType.DMA((3 * nb,)),
                   *[pltpu.HBM(b.shape, b.dtype) for b in big], *[pltpu.HBM(b.shape, b.dtype) for b in bufs],
                   jax.ShapeDtypeStruct((SUBLANES, LANES), F32)),
        in_specs=[HBM] * (2 * nb), out_specs=(SEM, SEM, *[HBM] * (2 * nb), pl.BlockSpec(memory_space=pltpu.VMEM)),
        input_output_aliases={i: 2 + i for i in range(2 * nb)},
        compiler_params=pltpu.CompilerParams(has_side_effects=DATAFLOW),
    )(*[pltpu.with_memory_space_constraint(b, pltpu.HBM) for b in big], *[pltpu.with_memory_space_constraint(b, pltpu.HBM) for b in bufs])
    return out[0], out[1], list(out[2:2 + nb]), list(out[2 + nb:2 + 2 * nb]), out[-1]


def _gather_wait(names, send_sems, recv_sems, big, bufs, after):
    nb = len(big)
    descs = [GATHER_BIG[n] for n in names]

    def body(*refs):
        ins, lnd = refs[:nb], refs[nb:2 * nb]
        ssem, rsem = refs[2 * nb], refs[2 * nb + 1]
        x, y, c = _mesh_pos()
        for t, (_, src, dst) in enumerate(descs):
            for j, (px, py) in enumerate(_other_chips(x, y)):
                cp = _remote(src(ins[t], c), dst(lnd[t], 2 * px + py, c), ssem.at[3 * t + j], rsem.at[3 * t + j], (px, py, c))
                cp.wait_send()
                cp.wait_recv()

    out = pl.pallas_call(
        body, name="gather_wait",
        out_shape=(*[pltpu.HBM(b.shape, b.dtype) for b in big], *[pltpu.HBM(b.shape, b.dtype) for b in bufs]),
        in_specs=[HBM] * (2 * nb) + [SEM, SEM, ANY], out_specs=tuple([HBM] * (2 * nb)),
        input_output_aliases={i: i for i in range(2 * nb)},
        compiler_params=pltpu.CompilerParams(has_side_effects=DATAFLOW),
    )(*big, *bufs, send_sems, recv_sems, after)
    return list(out[nb:])


def _gather_forward(names, bufs):
    nb = len(bufs)
    descs = [GATHER_BIG[n] for n in names]

    def body(*refs):
        outs = refs[nb:2 * nb]
        send_sems, recv_sems = refs[2 * nb:]
        x, y, c = _mesh_pos()
        sib = (x, y, 1 - c)
        sends = []
        for t, (_, src, dst) in enumerate(descs):
            for j, (px, py) in enumerate(_other_chips(x, y)):
                got = dst(outs[t], 2 * px + py, c)
                sends.append(_remote(got, got, send_sems.at[3 * t + j], recv_sems.at[3 * t + j], sib))
        for cp in sends:
            cp.start()
        for t, (_, src, dst) in enumerate(descs):
            for j, (px, py) in enumerate(_other_chips(x, y)):
                theirs = dst(outs[t], 2 * px + py, 1 - c)
                _remote(theirs, theirs, send_sems.at[3 * t + j], recv_sems.at[3 * t + j], sib).wait_recv()
        for cp in sends:
            cp.wait_send()

    return pl.pallas_call(
        body, name="gather_forward", in_specs=[ANY] * nb, out_specs=[ANY] * nb,
        out_shape=[jax.ShapeDtypeStruct(b.shape, b.dtype) for b in bufs], input_output_aliases={t: t for t in range(nb)},
        scratch_shapes=[pltpu.SemaphoreType.DMA((3 * nb,)), pltpu.SemaphoreType.DMA((3 * nb,))],
    )(*bufs)


def _remote(src, dst, ssem, rsem, to):
    return pltpu.make_async_remote_copy(src_ref=src, dst_ref=dst, send_sem=ssem, recv_sem=rsem, device_id=to, device_id_type=MESH)


def _rs_send_sibling(gs, tag):
    n = len(gs)
    ns = sum(N_CHIPS if g.ndim == 4 else 1 for g in gs)

    def body(*refs):
        cps = _sibling_copies(gs, refs[:n], refs[n:2 * n], refs[2 * n], refs[2 * n + 1])
        for cp in cps:
            cp.start()
        for cp in cps:
            cp.wait()

    out_shape = [jax.ShapeDtypeStruct(g.shape[:1] + g.shape[2:] if g.ndim == 4 else g.shape[1:], g.dtype) for g in gs]
    return pl.pallas_call(
        body, name=f"rs_send_sibling_{tag}", in_specs=[ANY] * n, out_specs=[ANY] * n, out_shape=out_shape,
        scratch_shapes=[pltpu.SemaphoreType.DMA((ns,)), pltpu.SemaphoreType.DMA((ns,))],
    )(*gs)


def _add_piece(g, recv, c, name):
    P, Q = g.shape[-2:]

    def body(c_ref, g_ref, r_ref, o_ref):
        o_ref[...] = g_ref[...].reshape(o_ref.shape) + r_ref[...]

    if g.ndim == 4:
        grid = (N_CHIPS,)
        in_specs = [pl.BlockSpec((1, 1, P, Q), lambda k, c_ref: (k, c_ref[0], 0, 0)), pl.BlockSpec((1, P, Q), lambda k, c_ref: (k, 0, 0))]
        out_spec = pl.BlockSpec((1, P, Q), lambda k, c_ref: (k, 0, 0))
    else:
        grid = (1,)
        in_specs = [pl.BlockSpec((1, P, Q), lambda k, c_ref: (c_ref[0], 0, 0)), pl.BlockSpec((P, Q), lambda k, c_ref: (0, 0))]
        out_spec = pl.BlockSpec((P, Q), lambda k, c_ref: (0, 0))
    return pl.pallas_call(
        body, name=name,
        grid_spec=pltpu.PrefetchScalarGridSpec(num_scalar_prefetch=1, grid=grid, in_specs=in_specs, out_specs=out_spec),
        out_shape=jax.ShapeDtypeStruct(recv.shape, g.dtype),
        compiler_params=_cp(1),
    )(c, g, recv)


def _sibling_copies(gs, srcs, dsts, send_sems, recv_sems):
    x, y, c = _mesh_pos()
    cps, s = [], 0
    for t, g in enumerate(gs):
        if g.ndim == 4:
            for k in range(N_CHIPS):
                cps.append(_remote(srcs[t].at[k, 1 - c], dsts[t].at[k], send_sems.at[s], recv_sems.at[s], (x, y, 1 - c)))
                s += 1
        else:
            cps.append(_remote(srcs[t].at[1 - c], dsts[t], send_sems.at[s], recv_sems.at[s], (x, y, 1 - c)))
            s += 1
    return cps


def _sibling_start(gs, tag):
    n = len(gs)
    ns = sum(N_CHIPS if g.ndim == 4 else 1 for g in gs)
    lands = [pltpu.with_memory_space_constraint(lax.empty(g.shape[:1] + g.shape[2:] if g.ndim == 4 else g.shape[1:], g.dtype), pltpu.HBM)
             for g in gs]

    def body(*refs):
        for cp in _sibling_copies(gs, refs[:n], refs[n:2 * n], refs[2 * n], refs[2 * n + 1]):
            cp.start()
        refs[-1][...] = jnp.zeros_like(refs[-1])

    out = pl.pallas_call(
        body, name=f"sibling_start_{tag}",
        out_shape=(pltpu.SemaphoreType.DMA((ns,)), pltpu.SemaphoreType.DMA((ns,)),
                   *[pltpu.HBM(g.shape, g.dtype) for g in gs], *[pltpu.HBM(l.shape, l.dtype) for l in lands],
                   jax.ShapeDtypeStruct((SUBLANES, LANES), F32)),
        in_specs=[HBM] * (2 * n), out_specs=(SEM, SEM, *[HBM] * (2 * n), pl.BlockSpec(memory_space=pltpu.VMEM)),
        input_output_aliases={i: 2 + i for i in range(2 * n)},
        compiler_params=pltpu.CompilerParams(has_side_effects=DATAFLOW),
    )(*[pltpu.with_memory_space_constraint(g, pltpu.HBM) for g in gs], *lands)
    return (out[0], out[1], list(out[2:2 + n]), list(out[2 + n:2 + 2 * n])), out[-1]


def _sibling_wait(send_sems, recv_sems, gs, lands, after, tag):
    n = len(gs)

    def body(*refs):
        for cp in _sibling_copies(gs, refs[:n], refs[n:2 * n], refs[2 * n], refs[2 * n + 1]):
            cp.wait_send()
            cp.wait_recv()

    out = pl.pallas_call(
        body, name=f"sibling_wait_{tag}",
        out_shape=(*[pltpu.HBM(g.shape, g.dtype) for g in gs], *[pltpu.HBM(l.shape, l.dtype) for l in lands]),
        in_specs=[HBM] * (2 * n) + [SEM, SEM, ANY], out_specs=tuple([HBM] * (2 * n)),
        input_output_aliases={i: i for i in range(2 * n)},
        compiler_params=pltpu.CompilerParams(has_side_effects=DATAFLOW),
    )(*gs, *lands, send_sems, recv_sems, after)
    return list(out[:n]), list(out[n:])


def _chips_start(hs, tag):
    n = len(hs)
    lands = [pltpu.with_memory_space_constraint(lax.empty((N_CHIPS,) + h.shape[-2:], h.dtype), pltpu.HBM) for h in hs]

    def body(*refs):
        ins, lnd = refs[:n], refs[n:2 * n]
        send_sems, recv_sems, token = refs[2 * n], refs[2 * n + 1], refs[-1]
        x, y, c = _mesh_pos()
        k = 2 * x + y
        piece = lambda t, kk: ins[t].at[kk] if hs[t].ndim == 3 else ins[t]
        for t in range(n):
            for j, (px, py) in enumerate(_other_chips(x, y)):
                _remote(piece(t, 2 * px + py), lnd[t].at[k], send_sems.at[3 * t + j], recv_sems.at[3 * t + j], (px, py, c)).start()
        token[...] = jnp.zeros_like(token)

    out = pl.pallas_call(
        body, name=f"chips_start_{tag}",
        out_shape=(pltpu.SemaphoreType.DMA((3 * n,)), pltpu.SemaphoreType.DMA((3 * n,)),
                   *[pltpu.HBM(h.shape, h.dtype) for h in hs], *[pltpu.HBM(l.shape, l.dtype) for l in lands],
                   jax.ShapeDtypeStruct((SUBLANES, LANES), F32)),
        in_specs=[HBM] * (2 * n), out_specs=(SEM, SEM, *[HBM] * (2 * n), pl.BlockSpec(memory_space=pltpu.VMEM)),
        input_output_aliases={i: 2 + i for i in range(2 * n)},
        compiler_params=pltpu.CompilerParams(has_side_effects=DATAFLOW),
    )(*[pltpu.with_memory_space_constraint(h, pltpu.HBM) for h in hs], *lands)
    return out[0], out[1], list(out[2:2 + n]), list(out[2 + n:2 + 2 * n]), out[-1]


def _chips_wait(send_sems, recv_sems, hs, lands, after, tag):
    n = len(hs)

    def body(*refs):
        ins, lnd = refs[:n], refs[n:2 * n]
        ssem, rsem = refs[2 * n], refs[2 * n + 1]
        x, y, c = _mesh_pos()
        k = 2 * x + y
        piece = lambda t, kk: ins[t].at[kk] if hs[t].ndim == 3 else ins[t]
        for t in range(n):
            for j, (px, py) in enumerate(_other_chips(x, y)):
                cp = _remote(piece(t, k), lnd[t].at[2 * px + py], ssem.at[3 * t + j], rsem.at[3 * t + j], (px, py, c))
                cp.wait_send()
                cp.wait_recv()

    out = pl.pallas_call(
        body, name=f"chips_wait_{tag}",
        out_shape=(*[pltpu.HBM(h.shape, h.dtype) for h in hs], *[pltpu.HBM(l.shape, l.dtype) for l in lands]),
        in_specs=[HBM] * (2 * n) + [SEM, SEM, ANY], out_specs=tuple([HBM] * (2 * n)),
        input_output_aliases={i: i for i in range(2 * n)},
        compiler_params=pltpu.CompilerParams(has_side_effects=DATAFLOW),
    )(*hs, *lands, send_sems, recv_sems, after)
    return list(out[:n]), list(out[n:])


def _add_chips(p, own, kc, name):
    _, P, Q = p.shape
    tr = P
    while tr * Q * 4 > ELEMENTWISE_TILE_BYTES and tr % 16 == 0:
        tr //= 2
    sharded = own.ndim == 3

    def body(kc_ref, p_ref, own_ref, o_ref):
        k = kc_ref[0]
        mine = own_ref[...].reshape(tr, Q)
        v = [jnp.where(k == j, mine, p_ref[j]) for j in range(N_CHIPS)]
        o_ref[0] = ((v[0] + v[1]) + v[2]) + v[3]

    own_spec = (pl.BlockSpec((1, tr, Q), lambda i, kc_ref: (kc_ref[0], i, 0)) if sharded
                else pl.BlockSpec((tr, Q), lambda i, kc_ref: (i, 0)))
    return pl.pallas_call(
        body, name=name,
        grid_spec=pltpu.PrefetchScalarGridSpec(
            num_scalar_prefetch=1, grid=(P // tr,),
            in_specs=[pl.BlockSpec((N_CHIPS, tr, Q), lambda i, kc_ref: (0, i, 0)), own_spec],
            out_specs=pl.BlockSpec((1, tr, Q), lambda i, kc_ref: (kc_ref[1], i, 0))),
        out_shape=jax.ShapeDtypeStruct((2, P, Q), p.dtype),
        compiler_params=_cp(1),
    )(kc, p, own)


def _rs_share(fs, tag):
    n = len(fs)

    def body(*refs):
        outs = refs[n:2 * n]
        send_sems, recv_sems = refs[2 * n:]
        x, y, c = _mesh_pos()
        sends = [_remote(outs[t].at[c], outs[t].at[c], send_sems.at[t], recv_sems.at[t], (x, y, 1 - c)) for t in range(n)]
        for cp in sends:
            cp.start()
        for t in range(n):
            _remote(outs[t].at[c], outs[t].at[1 - c], send_sems.at[t], recv_sems.at[t], (x, y, 1 - c)).wait_recv()
        for cp in sends:
            cp.wait_send()

    return pl.pallas_call(
        body, name=f"rs_share_{tag}", in_specs=[ANY] * n, out_specs=[ANY] * n,
        out_shape=[jax.ShapeDtypeStruct(f.shape, f.dtype) for f in fs], input_output_aliases={t: t for t in range(n)},
        scratch_shapes=[pltpu.SemaphoreType.DMA((n,)), pltpu.SemaphoreType.DMA((n,))],
    )(*fs)


def _reduce_start(gs, kc, tag):
    return _reduce_continue(gs, _rs_send_sibling(gs, tag), kc, tag)


def _reduce_continue(gs, from_sibling, kc, tag):
    chip_sums = [_add_piece(g, r, kc[1:], name=f"add_piece_{tag}_{t}") for t, (g, r) in enumerate(zip(gs, from_sibling))]
    send_sems, recv_sems, chip_sums, lands, token = _chips_start(chip_sums, tag)
    return (send_sems, recv_sems, chip_sums, lands, tag), token


def _reduce_finish(states, kc, after):
    mine = []
    for send_sems, recv_sems, chip_sums, lands, tag in states:
        chip_sums, from_chips = _chips_wait(send_sems, recv_sems, chip_sums, lands, after, tag)
        mine += [_add_chips(p, h, kc, name=f"add_chips_{tag}_{t}") for t, (p, h) in enumerate(zip(from_chips, chip_sums))]
    return _rs_share(mine, "all")


def _adamw(w, g, m, v, name):
    R, C = w.shape
    tr = R
    for cand in (512, 256, 128, 64, 32, 16, 8):
        if R % cand == 0 and cand * C * 4 <= ELEMENTWISE_TILE_BYTES:
            tr = cand
            break
    c1 = 1.0 / (1.0 - ADAM_B1 ** ADAM_STEP)
    c2 = 1.0 / (1.0 - ADAM_B2 ** ADAM_STEP)

    def body(w_ref, g_ref, m_ref, v_ref, d_ref, mo_ref, vo_ref):
        gv = g_ref[...]
        mn = ADAM_B1 * m_ref[...] + (1.0 - ADAM_B1) * gv
        vn = ADAM_B2 * v_ref[...] + (1.0 - ADAM_B2) * (gv * gv)
        mo_ref[...] = mn
        vo_ref[...] = vn
        d_ref[...] = -ADAM_LR * ((mn * c1) / (jnp.sqrt(vn * c2) + ADAM_EPS) + ADAM_WD * w_ref[...])

    spec = pl.BlockSpec((tr, C), lambda i: (i, 0))
    shp = jax.ShapeDtypeStruct((R, C), F32)
    return pl.pallas_call(body, name=name, grid=(R // tr,), in_specs=[spec] * 4, out_specs=[spec] * 3, out_shape=[shp] * 3,
                          compiler_params=_cp(1))(w, g, m, v)


def _adamw_many(ws, gs, ms, vs):
    n = len(ws)
    c1 = 1.0 / (1.0 - ADAM_B1 ** ADAM_STEP)
    c2 = 1.0 / (1.0 - ADAM_B2 ** ADAM_STEP)

    def body(*refs):
        w_refs, g_refs, m_refs, v_refs = (refs[k * n:(k + 1) * n] for k in range(4))
        outs = refs[4 * n:]
        for i in range(n):
            gv = g_refs[i][...]
            mn = ADAM_B1 * m_refs[i][...] + (1.0 - ADAM_B1) * gv
            vn = ADAM_B2 * v_refs[i][...] + (1.0 - ADAM_B2) * (gv * gv)
            outs[3 * i][...] = -ADAM_LR * ((mn * c1) / (jnp.sqrt(vn * c2) + ADAM_EPS) + ADAM_WD * w_refs[i][...])
            outs[3 * i + 1][...] = mn
            outs[3 * i + 2][...] = vn

    out_shape = [jax.ShapeDtypeStruct(w.shape, F32) for w in ws for _ in range(3)]
    return pl.pallas_call(body, name="adamw_small", out_shape=out_shape)(*ws, *gs, *ms, *vs)


def _step(a):
    x, y, c = _mesh_pos()
    kc = jnp.stack([2 * x + y, c]).astype(jnp.int32)

    rs = _rows(sum(a[n].size for n in SH_SMALL))
    first, later = ["ev_w_in", "ev_w_out"], ["od_w_in", "od_w_out", "ffn_w_up", "ffn_w_down"]
    lead = lambda w: w if w.ndim == 3 else w[None]
    *full, gs = _gather_weights(first, [a[n].astype(BF16) for n in first], _pack([a[n] for n in SH_SMALL], rs, F32))
    p = {n: a[n] for n in REP}
    p.update({n: lead(w) for n, w in zip(first, full)})
    parts = [_unpack(gs[k], [a[n].shape for n in SH_SMALL]) for k in range(N_CHIPS)]
    for i, n in enumerate(SH_SMALL):
        p[n] = jnp.concatenate([parts[k][i] for k in range(N_CHIPS)], axis=SH_SMALL[n])
    shards = lax.optimization_barrier(([a[n].astype(BF16) for n in later], full))[0]
    g_send, g_recv, shards, bufs, token = _gather_start(later, shards, _place_own(later, shards))
    p["norm_mix"] = p["norm_mix"] + token[0:1, 0:1]

    def late(after):
        got = _gather_forward(later, _gather_wait(later, g_send, g_recv, shards, bufs, after))
        return {n: lead(w) for n, w in zip(later, got)}

    p["late"] = late

    states, pending = {}, {}

    def start_reduce(tag, gs, split=False):
        if split:
            pending[tag], token = _sibling_start(gs, tag)
        else:
            states[tag], token = _reduce_start(gs, kc, tag)
        return token

    def continue_reduce(tag, after):
        send_sems, recv_sems, gs, lands = pending.pop(tag)
        gs, from_sibling = _sibling_wait(send_sems, recv_sems, gs, lands, after, tag)
        states[tag], token = _reduce_continue(gs, from_sibling, kc, tag)
        return token

    sq8, grad_x, grads, big = _local_step(a["x"][0], a["loss_target"][0], p, start_reduce, continue_reduce)
    loss = lax.psum(0.5 / D_MODEL * jnp.sum(sq8), ("x", "y", "c"))

    r_s = _rows(sum(a[n].size for n in SH_SMALL), 2 * SUBLANES) // 2
    small_pieces = []
    for k in range(N_CHIPS):
        pieces = [lax.slice_in_dim(grads[n], k * a[n].shape[ax], (k + 1) * a[n].shape[ax], axis=ax) for n, ax in SH_SMALL.items()]
        small_pieces.append(_pack(pieces, 2 * r_s, F32).reshape(2, r_s, LANES))
    g_small = jnp.stack(small_pieces)
    r_r = _rows(sum(a[n].size for n in REP), 2 * SUBLANES) // 2
    g_rep = _pack([grads[n] for n in REP], 2 * r_r, F32).reshape(2, r_r, LANES)
    token = start_reduce("g4", [g_small, g_rep])
    reduced = _reduce_finish([states[tag] for tag in ("g1", "g2", "g3", "g4")], kc, token)
    red = dict(zip([key for tag in ("g1", "g2", "g3") for key in REDUCE_GROUPS[tag]], reduced))
    gfin = {}
    for n in ("ev_w_in", "ev_w_out", "od_w_in", "od_w_out"):
        gfin[n] = red[n, 0].reshape(a[n].shape)
    for n in ("ffn_w_up", "ffn_w_down"):
        gfin[n] = jnp.stack([red[n, l].reshape(a[n].shape[1:]) for l in range(2)])
    gfin.update(zip(SH_SMALL, _unpack(reduced[-2], [a[n].shape for n in SH_SMALL])))
    gfin.update(zip(REP, _unpack(reduced[-1], [a[n].shape for n in REP])))

    out = {"loss": loss, "grad_x": grad_x[None]}
    small_names = list(SH_SMALL) + REP
    for n in SH_BIG:
        shp = a[n].shape
        two_d = lambda t: t.reshape(-1, shp[-1])
        d, mo, vo = _adamw(two_d(a[n]), two_d(gfin[n]), two_d(a["m_" + n]), two_d(a["v_" + n]), name=f"adamw_{n}")
        out["delta_" + n], out["new_m_" + n], out["new_v_" + n] = d.reshape(shp), mo.reshape(shp), vo.reshape(shp)
    two_d = lambda t: t.reshape(-1, t.shape[-1])
    res = _adamw_many(*[[two_d(src(n)) for n in small_names]
                        for src in (lambda n: a[n], lambda n: gfin[n], lambda n: a["m_" + n], lambda n: a["v_" + n])])
    for i, n in enumerate(small_names):
        out["delta_" + n], out["new_m_" + n], out["new_v_" + n] = (r.reshape(a[n].shape) for r in res[3 * i:3 * i + 3])
    for n in WEIGHTS:
        out["grad_" + n] = gfin[n]
    return out


def kernel(x, norm_mix, norm_ffn, norm_final, ev_w_in, ev_conv_w, ev_conv_b, ev_gate_a_w, ev_gate_a_b, ev_gate_x_w, ev_gate_x_b, ev_lru_lambda, hg_lb_logits, ev_hg_norm, ev_w_out, od_w_in, od_b_in, od_ln_g, od_ln_b, od_w_s, od_b_s, od_w_out, ffn_w_up, ffn_conv_w, ffn_conv_b, ffn_w_down, loss_target, m_norm_mix, m_norm_ffn, m_norm_final, m_ev_w_in, m_ev_conv_w, m_ev_conv_b, m_ev_gate_a_w, m_ev_gate_a_b, m_ev_gate_x_w, m_ev_gate_x_b, m_ev_lru_lambda, m_hg_lb_logits, m_ev_hg_norm, m_ev_w_out, m_od_w_in, m_od_b_in, m_od_ln_g, m_od_ln_b, m_od_w_s, m_od_b_s, m_od_w_out, m_ffn_w_up, m_ffn_conv_w, m_ffn_conv_b, m_ffn_w_down, v_norm_mix, v_norm_ffn, v_norm_final, v_ev_w_in, v_ev_conv_w, v_ev_conv_b, v_ev_gate_a_w, v_ev_gate_a_b, v_ev_gate_x_w, v_ev_gate_x_b, v_ev_lru_lambda, v_hg_lb_logits, v_ev_hg_norm, v_ev_w_out, v_od_w_in, v_od_b_in, v_od_ln_g, v_od_ln_b, v_od_w_s, v_od_b_s, v_od_w_out, v_ffn_w_up, v_ffn_conv_w, v_ffn_conv_b, v_ffn_w_down):
    vals = (x, norm_mix, norm_ffn, norm_final, ev_w_in, ev_conv_w, ev_conv_b, ev_gate_a_w, ev_gate_a_b, ev_gate_x_w, ev_gate_x_b, ev_lru_lambda, hg_lb_logits, ev_hg_norm, ev_w_out, od_w_in, od_b_in, od_ln_g, od_ln_b, od_w_s, od_b_s, od_w_out, ffn_w_up, ffn_conv_w, ffn_conv_b, ffn_w_down, loss_target, m_norm_mix, m_norm_ffn, m_norm_final, m_ev_w_in, m_ev_conv_w, m_ev_conv_b, m_ev_gate_a_w, m_ev_gate_a_b, m_ev_gate_x_w, m_ev_gate_x_b, m_ev_lru_lambda, m_hg_lb_logits, m_ev_hg_norm, m_ev_w_out, m_od_w_in, m_od_b_in, m_od_ln_g, m_od_ln_b, m_od_w_s, m_od_b_s, m_od_w_out, m_ffn_w_up, m_ffn_conv_w, m_ffn_conv_b, m_ffn_w_down, v_norm_mix, v_norm_ffn, v_norm_final, v_ev_w_in, v_ev_conv_w, v_ev_conv_b, v_ev_gate_a_w, v_ev_gate_a_b, v_ev_gate_x_w, v_ev_gate_x_b, v_ev_lru_lambda, v_hg_lb_logits, v_ev_hg_norm, v_ev_w_out, v_od_w_in, v_od_b_in, v_od_ln_g, v_od_ln_b, v_od_w_s, v_od_b_s, v_od_w_out, v_ffn_w_up, v_ffn_conv_w, v_ffn_conv_b, v_ffn_w_down)
    names = ["x"] + WEIGHTS + ["loss_target"] + ["m_" + n for n in WEIGHTS] + ["v_" + n for n in WEIGHTS]
    out = _step(dict(zip(names, vals)))
    return (out["loss"], out["grad_x"], *[out["grad_" + n] for n in WEIGHTS], *[out["delta_" + n] for n in WEIGHTS],
            *[out["new_m_" + n] for n in WEIGHTS], *[out["new_v_" + n] for n in WEIGHTS])
```
